```python
import math
import jax, jax.numpy as jnp
from jax import lax
import numpy as np

D_MODEL = 1024
BATCH = 16
SEQ = 4096
DEPTH = 1

D_MIX = D_MODEL
D_SSM = D_MIX // 2
SSM_GROUP = 16
N_SSM_GROUPS = D_SSM // SSM_GROUP
SSM_STATE = 64
D_ATTN = D_MIX - D_SSM
N_HEADS = 8
QK_NOPE = 64
QK_ROPE = 32
V_HEAD = D_ATTN // N_HEADS
Q_LORA = 384
KV_LORA = 256
IN_COLS = D_SSM + Q_LORA + KV_LORA + QK_ROPE
D_FF = 4 * D_MODEL
ROPE_BASE = 10000.0
Q_BLOCK = 128
EPS = 1e-6
DT_MIN = 1e-3
DT_MAX = 1e-1
N_MOD = 6

kernel_name = "hymba_s5_mla_adaln_block"


def rmsnorm(x, g):
    xf = x.astype(jnp.float32)
    y = xf * lax.rsqrt(jnp.mean(xf * xf, axis=-1, keepdims=True) + EPS)
    return (y * g.astype(jnp.float32)).astype(x.dtype)


def rope_tables(positions):
    inv_freq = ROPE_BASE ** (-jnp.arange(0, QK_ROPE, 2, dtype=jnp.float32) / QK_ROPE)
    ang = positions.astype(jnp.float32)[..., None] * inv_freq
    return jnp.cos(ang), jnp.sin(ang)


def apply_rope(x, cos, sin):
    xf = x.astype(jnp.float32)
    x1, x2 = jnp.split(xf, 2, axis=-1)
    out = jnp.concatenate([x1 * cos - x2 * sin, x1 * sin + x2 * cos], axis=-1)
    return out.astype(x.dtype)


def s5_mixer(u, lam_re, lam_im, b_re, b_im, c_re, c_im, d, log_dt, w_glu):
    f32 = jnp.float32
    bsz, seq, _ = u.shape
    uf = u.astype(f32).reshape(bsz, seq, N_SSM_GROUPS, SSM_GROUP)
    lam = lax.complex(lam_re.astype(f32), lam_im.astype(f32))
    dt = jnp.exp(log_dt.astype(f32))[:, None]
    lam_bar = jnp.exp(lam * dt)
    b = lax.complex(b_re.astype(f32), b_im.astype(f32))
    b_bar = ((lam_bar - 1.0) / lam)[..., None] * b
    bu = jnp.einsum("bsgh,gph->bsgp", uf, b_bar)
    a = jnp.broadcast_to(lam_bar, (1, seq) + lam_bar.shape)

    def combine(left, right):
        a_l, b_l = left
        a_r, b_r = right
        return a_r * a_l, a_r * b_l + b_r

    _, states = lax.associative_scan(combine, (a, bu), axis=1)
    y = (jnp.einsum("bsgp,ghp->bsgh", jnp.real(states), c_re.astype(f32))
         - jnp.einsum("bsgp,ghp->bsgh", jnp.imag(states), c_im.astype(f32))
         + d.astype(f32) * uf)
    y = jax.nn.gelu(y).reshape(bsz, seq, D_SSM).astype(u.dtype)
    z = y @ w_glu
    return z[..., :D_SSM] * jax.nn.sigmoid(z[..., D_SSM:])


def causal_block_attention(q_nope, q_rope, k_nope, k_rope, v):
    bsz, seq = q_nope.shape[:2]
    n_blocks = seq // Q_BLOCK
    scale = (QK_NOPE + QK_ROPE) ** -0.5
    key_pos = jnp.arange(seq)

    def one_block(i):
        start = i * Q_BLOCK
        qn = lax.dynamic_slice_in_dim(q_nope, start, Q_BLOCK, axis=1)
        qr = lax.dynamic_slice_in_dim(q_rope, start, Q_BLOCK, axis=1)
        s = (jnp.einsum("bqhd,bkhd->bhqk", qn, k_nope)
             + jnp.einsum("bqhr,bkr->bhqk", qr, k_rope)).astype(jnp.float32) * scale
        q_pos = start + jnp.arange(Q_BLOCK)
        mask = key_pos[None, :] <= q_pos[:, None]
        s = jnp.where(mask, s, -jnp.inf)
        p = jax.nn.softmax(s, axis=-1).astype(v.dtype)
        return jnp.einsum("bhqk,bkhd->bqhd", p, v)

    out = lax.map(one_block, jnp.arange(n_blocks))
    return out.transpose(1, 0, 2, 3, 4).reshape(bsz, seq, N_HEADS, V_HEAD)


def mla_mixer(q_lat, kv_lat, k_rope, cos, sin, q_norm_g, w_uq, kv_norm_g, w_ukv):
    bsz, seq, _ = q_lat.shape
    q = (rmsnorm(q_lat, q_norm_g) @ w_uq).reshape(bsz, seq, N_HEADS, QK_NOPE + QK_ROPE)
    kv = (rmsnorm(kv_lat, kv_norm_g) @ w_ukv).reshape(bsz, seq, N_HEADS, QK_NOPE + V_HEAD)
    q_nope, q_rope = q[..., :QK_NOPE], q[..., QK_NOPE:]
    k_nope, v = kv[..., :QK_NOPE], kv[..., QK_NOPE:]
    q_rope = apply_rope(q_rope, cos[:, :, None, :], sin[:, :, None, :])
    k_rope = apply_rope(k_rope, cos, sin)
    out = causal_block_attention(q_nope, q_rope, k_nope, k_rope, v)
    return out.reshape(bsz, seq, D_ATTN)


def _fwd_setup_inputs(seed: int = 0) -> dict:
    key = jax.random.key(seed)
    ks = jax.random.split(key, 32)
    f32 = jnp.float32
    L = DEPTH

    def nrm(k, shape, scale):
        return jax.random.normal(k, shape, f32) * scale

    def gain(k, shape):
        return 1.0 + 0.02 * jax.random.normal(k, shape, f32)

    x = jax.random.normal(ks[0], (BATCH, SEQ, D_MODEL), f32)
    c = jax.random.normal(ks[1], (BATCH, D_MODEL), f32)
    offset = jax.random.randint(ks[2], (BATCH, 1), 0, 2048, dtype=jnp.int32)
    positions = offset + jnp.arange(SEQ, dtype=jnp.int32)[None, :]

    n_idx = jnp.arange(SSM_STATE, dtype=f32)
    lam_re = -0.5 * jnp.exp(0.01 * jax.random.normal(ks[3], (L, N_SSM_GROUPS, SSM_STATE), f32))
    lam_im = math.pi * n_idx + 0.01 * jax.random.normal(ks[4], (L, N_SSM_GROUPS, SSM_STATE), f32)
    log_dt = jax.random.uniform(ks[5], (L, N_SSM_GROUPS), f32, math.log(DT_MIN), math.log(DT_MAX))

    return {
        "x": x,
        "c": c,
        "positions": positions,
        "ada_w": nrm(ks[6], (L, D_MODEL, N_MOD * D_MODEL), 0.5 * D_MODEL ** -0.5),
        "ada_b": nrm(ks[7], (L, N_MOD * D_MODEL), 0.02),
        "norm1_g": gain(ks[8], (L, D_MODEL)),
        "w_in": nrm(ks[9], (L, D_MODEL, IN_COLS), D_MODEL ** -0.5),
        "ssm_lambda_re": lam_re,
        "ssm_lambda_im": lam_im,
        "ssm_b_re": nrm(ks[10], (L, N_SSM_GROUPS, SSM_STATE, SSM_GROUP), (2 * SSM_GROUP) ** -0.5),
        "ssm_b_im": nrm(ks[11], (L, N_SSM_GROUPS, SSM_STATE, SSM_GROUP), (2 * SSM_GROUP) ** -0.5),
        "ssm_c_re": nrm(ks[12], (L, N_SSM_GROUPS, SSM_GROUP, SSM_STATE), (2 * SSM_STATE) ** -0.5),
        "ssm_c_im": nrm(ks[13], (L, N_SSM_GROUPS, SSM_GROUP, SSM_STATE), (2 * SSM_STATE) ** -0.5),
        "ssm_d": nrm(ks[14], (L, N_SSM_GROUPS, SSM_GROUP), 1.0),
        "ssm_log_dt": log_dt,
        "w_glu": nrm(ks[15], (L, D_SSM, 2 * D_SSM), D_SSM ** -0.5),
        "q_norm_g": gain(ks[16], (L, Q_LORA)),
        "w_uq": nrm(ks[17], (L, Q_LORA, N_HEADS * (QK_NOPE + QK_ROPE)), Q_LORA ** -0.5),
        "kv_norm_g": gain(ks[18], (L, KV_LORA)),
        "w_ukv": nrm(ks[19], (L, KV_LORA, N_HEADS * (QK_NOPE + V_HEAD)), KV_LORA ** -0.5),
        "ssm_out_g": gain(ks[20], (L, D_SSM)),
        "attn_out_g": gain(ks[21], (L, D_ATTN)),
        "w_out": nrm(ks[22], (L, D_MIX, D_MODEL), D_MIX ** -0.5),
        "norm2_g": gain(ks[23], (L, D_MODEL)),
        "w_ff1": nrm(ks[24], (L, D_MODEL, D_FF), D_MODEL ** -0.5),
        "w_ff2": nrm(ks[25], (L, D_FF, D_MODEL), D_FF ** -0.5),
        "final_ada_w": nrm(ks[26], (D_MODEL, 2 * D_MODEL), 0.5 * D_MODEL ** -0.5),
        "final_ada_b": nrm(ks[27], (2 * D_MODEL,), 0.02),
        "final_norm_g": gain(ks[28], (D_MODEL,)),
    }


def _fwd_reference(x, c, positions, ada_w, ada_b, norm1_g, w_in, ssm_lambda_re, ssm_lambda_im,
              ssm_b_re, ssm_b_im, ssm_c_re, ssm_c_im, ssm_d, ssm_log_dt, w_glu,
              q_norm_g, w_uq, kv_norm_g, w_ukv, ssm_out_g, attn_out_g, w_out,
              norm2_g, w_ff1, w_ff2, final_ada_w, final_ada_b, final_norm_g):
    cond = jax.nn.silu(c)
    cos, sin = rope_tables(positions)
    s1 = D_SSM
    s2 = s1 + Q_LORA
    s3 = s2 + KV_LORA
    for l in range(DEPTH):
        mod = (cond @ ada_w[l] + ada_b[l])[:, None, :]
        shift1, scale1, gate1, shift2, scale2, gate2 = jnp.split(mod, N_MOD, axis=-1)

        h = rmsnorm(x, norm1_g[l]) * (1.0 + scale1) + shift1
        proj = h @ w_in[l]
        u = proj[..., :s1]
        q_lat = proj[..., s1:s2]
        kv_lat = proj[..., s2:s3]
        k_rope = proj[..., s3:]
        y_ssm = s5_mixer(u, ssm_lambda_re[l], ssm_lambda_im[l], ssm_b_re[l], ssm_b_im[l],
                         ssm_c_re[l], ssm_c_im[l], ssm_d[l], ssm_log_dt[l], w_glu[l])
        y_attn = mla_mixer(q_lat, kv_lat, k_rope, cos, sin, q_norm_g[l], w_uq[l],
                           kv_norm_g[l], w_ukv[l])
        y = jnp.concatenate([rmsnorm(y_ssm, ssm_out_g[l]), rmsnorm(y_attn, attn_out_g[l])], axis=-1)
        x = x + gate1 * (y @ w_out[l])

        h = rmsnorm(x, norm2_g[l]) * (1.0 + scale2) + shift2
        ff = jnp.square(jax.nn.relu(h @ w_ff1[l])) @ w_ff2[l]
        x = x + gate2 * ff

    fmod = (cond @ final_ada_w + final_ada_b)[:, None, :]
    fshift, fscale = jnp.split(fmod, 2, axis=-1)
    return rmsnorm(x, final_norm_g) * (1.0 + fscale) + fshift


import jax as _jax
import jax.numpy as _jnp

TWIN_FORMAT = 'train_step'
FWD_PARAMS = ['x', 'c', 'positions', 'ada_w', 'ada_b', 'norm1_g', 'w_in', 'ssm_lambda_re', 'ssm_lambda_im', 'ssm_b_re', 'ssm_b_im', 'ssm_c_re', 'ssm_c_im', 'ssm_d', 'ssm_log_dt', 'w_glu', 'q_norm_g', 'w_uq', 'kv_norm_g', 'w_ukv', 'ssm_out_g', 'attn_out_g', 'w_out', 'norm2_g', 'w_ff1', 'w_ff2', 'final_ada_w', 'final_ada_b', 'final_norm_g']
TWIN_WEIGHTS = ['ada_w', 'ada_b', 'norm1_g', 'w_in', 'ssm_lambda_re', 'ssm_lambda_im', 'ssm_b_re', 'ssm_b_im', 'ssm_c_re', 'ssm_c_im', 'ssm_d', 'ssm_log_dt', 'w_glu', 'q_norm_g', 'w_uq', 'kv_norm_g', 'w_ukv', 'ssm_out_g', 'attn_out_g', 'w_out', 'norm2_g', 'w_ff1', 'w_ff2', 'final_ada_w', 'final_ada_b', 'final_norm_g']
TWIN_DIFF_INPUT = 'x'
TWIN_INPUTS = ['x', 'c', 'positions', 'ada_w', 'ada_b', 'norm1_g', 'w_in', 'ssm_lambda_re', 'ssm_lambda_im', 'ssm_b_re', 'ssm_b_im', 'ssm_c_re', 'ssm_c_im', 'ssm_d', 'ssm_log_dt', 'w_glu', 'q_norm_g', 'w_uq', 'kv_norm_g', 'w_ukv', 'ssm_out_g', 'attn_out_g', 'w_out', 'norm2_g', 'w_ff1', 'w_ff2', 'final_ada_w', 'final_ada_b', 'final_norm_g', 'loss_target', 'm_ada_w', 'm_ada_b', 'm_norm1_g', 'm_w_in', 'm_ssm_lambda_re', 'm_ssm_lambda_im', 'm_ssm_b_re', 'm_ssm_b_im', 'm_ssm_c_re', 'm_ssm_c_im', 'm_ssm_d', 'm_ssm_log_dt', 'm_w_glu', 'm_q_norm_g', 'm_w_uq', 'm_kv_norm_g', 'm_w_ukv', 'm_ssm_out_g', 'm_attn_out_g', 'm_w_out', 'm_norm2_g', 'm_w_ff1', 'm_w_ff2', 'm_final_ada_w', 'm_final_ada_b', 'm_final_norm_g', 'v_ada_w', 'v_ada_b', 'v_norm1_g', 'v_w_in', 'v_ssm_lambda_re', 'v_ssm_lambda_im', 'v_ssm_b_re', 'v_ssm_b_im', 'v_ssm_c_re', 'v_ssm_c_im', 'v_ssm_d', 'v_ssm_log_dt', 'v_w_glu', 'v_q_norm_g', 'v_w_uq', 'v_kv_norm_g', 'v_w_ukv', 'v_ssm_out_g', 'v_attn_out_g', 'v_w_out', 'v_norm2_g', 'v_w_ff1', 'v_w_ff2', 'v_final_ada_w', 'v_final_ada_b', 'v_final_norm_g']
TWIN_OUTPUTS = ['loss', 'grad_x', 'grad_ada_w', 'grad_ada_b', 'grad_norm1_g', 'grad_w_in', 'grad_ssm_lambda_re', 'grad_ssm_lambda_im', 'grad_ssm_b_re', 'grad_ssm_b_im', 'grad_ssm_c_re', 'grad_ssm_c_im', 'grad_ssm_d', 'grad_ssm_log_dt', 'grad_w_glu', 'grad_q_norm_g', 'grad_w_uq', 'grad_kv_norm_g', 'grad_w_ukv', 'grad_ssm_out_g', 'grad_attn_out_g', 'grad_w_out', 'grad_norm2_g', 'grad_w_ff1', 'grad_w_ff2', 'grad_final_ada_w', 'grad_final_ada_b', 'grad_final_norm_g', 'delta_ada_w', 'delta_ada_b', 'delta_norm1_g', 'delta_w_in', 'delta_ssm_lambda_re', 'delta_ssm_lambda_im', 'delta_ssm_b_re', 'delta_ssm_b_im', 'delta_ssm_c_re', 'delta_ssm_c_im', 'delta_ssm_d', 'delta_ssm_log_dt', 'delta_w_glu', 'delta_q_norm_g', 'delta_w_uq', 'delta_kv_norm_g', 'delta_w_ukv', 'delta_ssm_out_g', 'delta_attn_out_g', 'delta_w_out', 'delta_norm2_g', 'delta_w_ff1', 'delta_w_ff2', 'delta_final_ada_w', 'delta_final_ada_b', 'delta_final_norm_g', 'new_m_ada_w', 'new_m_ada_b', 'new_m_norm1_g', 'new_m_w_in', 'new_m_ssm_lambda_re', 'new_m_ssm_lambda_im', 'new_m_ssm_b_re', 'new_m_ssm_b_im', 'new_m_ssm_c_re', 'new_m_ssm_c_im', 'new_m_ssm_d', 'new_m_ssm_log_dt', 'new_m_w_glu', 'new_m_q_norm_g', 'new_m_w_uq', 'new_m_kv_norm_g', 'new_m_w_ukv', 'new_m_ssm_out_g', 'new_m_attn_out_g', 'new_m_w_out', 'new_m_norm2_g', 'new_m_w_ff1', 'new_m_w_ff2', 'new_m_final_ada_w', 'new_m_final_ada_b', 'new_m_final_norm_g', 'new_v_ada_w', 'new_v_ada_b', 'new_v_norm1_g', 'new_v_w_in', 'new_v_ssm_lambda_re', 'new_v_ssm_lambda_im', 'new_v_ssm_b_re', 'new_v_ssm_b_im', 'new_v_ssm_c_re', 'new_v_ssm_c_im', 'new_v_ssm_d', 'new_v_ssm_log_dt', 'new_v_w_glu', 'new_v_q_norm_g', 'new_v_w_uq', 'new_v_kv_norm_g', 'new_v_w_ukv', 'new_v_ssm_out_g', 'new_v_attn_out_g', 'new_v_w_out', 'new_v_norm2_g', 'new_v_w_ff1', 'new_v_w_ff2', 'new_v_final_ada_w', 'new_v_final_ada_b', 'new_v_final_norm_g']
TWIN_LEAF_KINDS = {'loss': 'loss', 'grad_x': 'grad_x', 'grad_ada_w': 'grad_w', 'grad_ada_b': 'grad_w', 'grad_norm1_g': 'grad_w', 'grad_w_in': 'grad_w', 'grad_ssm_lambda_re': 'grad_w', 'grad_ssm_lambda_im': 'grad_w', 'grad_ssm_b_re': 'grad_w', 'grad_ssm_b_im': 'grad_w', 'grad_ssm_c_re': 'grad_w', 'grad_ssm_c_im': 'grad_w', 'grad_ssm_d': 'grad_w', 'grad_ssm_log_dt': 'grad_w', 'grad_w_glu': 'grad_w', 'grad_q_norm_g': 'grad_w', 'grad_w_uq': 'grad_w', 'grad_kv_norm_g': 'grad_w', 'grad_w_ukv': 'grad_w', 'grad_ssm_out_g': 'grad_w', 'grad_attn_out_g': 'grad_w', 'grad_w_out': 'grad_w', 'grad_norm2_g': 'grad_w', 'grad_w_ff1': 'grad_w', 'grad_w_ff2': 'grad_w', 'grad_final_ada_w': 'grad_w', 'grad_final_ada_b': 'grad_w', 'grad_final_norm_g': 'grad_w', 'delta_ada_w': 'delta_w', 'delta_ada_b': 'delta_w', 'delta_norm1_g': 'delta_w', 'delta_w_in': 'delta_w', 'delta_ssm_lambda_re': 'delta_w', 'delta_ssm_lambda_im': 'delta_w', 'delta_ssm_b_re': 'delta_w', 'delta_ssm_b_im': 'delta_w', 'delta_ssm_c_re': 'delta_w', 'delta_ssm_c_im': 'delta_w', 'delta_ssm_d': 'delta_w', 'delta_ssm_log_dt': 'delta_w', 'delta_w_glu': 'delta_w', 'delta_q_norm_g': 'delta_w', 'delta_w_uq': 'delta_w', 'delta_kv_norm_g': 'delta_w', 'delta_w_ukv': 'delta_w', 'delta_ssm_out_g': 'delta_w', 'delta_attn_out_g': 'delta_w', 'delta_w_out': 'delta_w', 'delta_norm2_g': 'delta_w', 'delta_w_ff1': 'delta_w', 'delta_w_ff2': 'delta_w', 'delta_final_ada_w': 'delta_w', 'delta_final_ada_b': 'delta_w', 'delta_final_norm_g': 'delta_w', 'new_m_ada_w': 'new_m', 'new_m_ada_b': 'new_m', 'new_m_norm1_g': 'new_m', 'new_m_w_in': 'new_m', 'new_m_ssm_lambda_re': 'new_m', 'new_m_ssm_lambda_im': 'new_m', 'new_m_ssm_b_re': 'new_m', 'new_m_ssm_b_im': 'new_m', 'new_m_ssm_c_re': 'new_m', 'new_m_ssm_c_im': 'new_m', 'new_m_ssm_d': 'new_m', 'new_m_ssm_log_dt': 'new_m', 'new_m_w_glu': 'new_m', 'new_m_q_norm_g': 'new_m', 'new_m_w_uq': 'new_m', 'new_m_kv_norm_g': 'new_m', 'new_m_w_ukv': 'new_m', 'new_m_ssm_out_g': 'new_m', 'new_m_attn_out_g': 'new_m', 'new_m_w_out': 'new_m', 'new_m_norm2_g': 'new_m', 'new_m_w_ff1': 'new_m', 'new_m_w_ff2': 'new_m', 'new_m_final_ada_w': 'new_m', 'new_m_final_ada_b': 'new_m', 'new_m_final_norm_g': 'new_m', 'new_v_ada_w': 'new_v', 'new_v_ada_b': 'new_v', 'new_v_norm1_g': 'new_v', 'new_v_w_in': 'new_v', 'new_v_ssm_lambda_re': 'new_v', 'new_v_ssm_lambda_im': 'new_v', 'new_v_ssm_b_re': 'new_v', 'new_v_ssm_b_im': 'new_v', 'new_v_ssm_c_re': 'new_v', 'new_v_ssm_c_im': 'new_v', 'new_v_ssm_d': 'new_v', 'new_v_ssm_log_dt': 'new_v', 'new_v_w_glu': 'new_v', 'new_v_q_norm_g': 'new_v', 'new_v_w_uq': 'new_v', 'new_v_kv_norm_g': 'new_v', 'new_v_w_ukv': 'new_v', 'new_v_ssm_out_g': 'new_v', 'new_v_attn_out_g': 'new_v', 'new_v_w_out': 'new_v', 'new_v_norm2_g': 'new_v', 'new_v_w_ff1': 'new_v', 'new_v_w_ff2': 'new_v', 'new_v_final_ada_w': 'new_v', 'new_v_final_ada_b': 'new_v', 'new_v_final_norm_g': 'new_v'}


def _forward(args):
    return _fwd_reference(*[args[k] for k in FWD_PARAMS])


def _output_shape():
    out = _jax.eval_shape(lambda: _forward(_fwd_setup_inputs(0)))
    return out.shape, out.dtype

N_MICROBATCH = 1
ADAM_LR = 0.001
ADAM_B1 = 0.9
ADAM_B2 = 0.999
ADAM_EPS = 1e-08
ADAM_WD = 0.01
ADAM_STEP = 10
PER_EXAMPLE_BATCH_AXIS = {'x': 0, 'c': 0, 'positions': 0, 'loss_target': 0}
SHARED_INPUTS = []
_WEIGHT_DTYPES = {'ada_w': _jnp.float32, 'ada_b': _jnp.float32, 'norm1_g': _jnp.float32, 'w_in': _jnp.float32, 'ssm_lambda_re': _jnp.float32, 'ssm_lambda_im': _jnp.float32, 'ssm_b_re': _jnp.float32, 'ssm_b_im': _jnp.float32, 'ssm_c_re': _jnp.float32, 'ssm_c_im': _jnp.float32, 'ssm_d': _jnp.float32, 'ssm_log_dt': _jnp.float32, 'w_glu': _jnp.float32, 'q_norm_g': _jnp.float32, 'w_uq': _jnp.float32, 'kv_norm_g': _jnp.float32, 'w_ukv': _jnp.float32, 'ssm_out_g': _jnp.float32, 'attn_out_g': _jnp.float32, 'w_out': _jnp.float32, 'norm2_g': _jnp.float32, 'w_ff1': _jnp.float32, 'w_ff2': _jnp.float32, 'final_ada_w': _jnp.float32, 'final_ada_b': _jnp.float32, 'final_norm_g': _jnp.float32}
MOMENT_SCALE = {'ada_w': 4.899679e+00, 'ada_b': 8.453076e+00, 'norm1_g': 1.550110e-01, 'w_in': 2.366989e+00, 'ssm_lambda_re': 7.078116e-02, 'ssm_lambda_im': 7.392924e-02, 'ssm_b_re': 6.247529e-02, 'ssm_b_im': 5.123699e-02, 'ssm_c_re': 1.084620e-01, 'ssm_c_im': 1.095495e-01, 'ssm_d': 1.896911e+00, 'ssm_log_dt': 3.924075e+00, 'w_glu': 1.398538e+00, 'q_norm_g': 8.904907e-02, 'w_uq': 7.471435e-02, 'kv_norm_g': 4.871279e+00, 'w_ukv': 2.441080e+00, 'ssm_out_g': 1.898795e+00, 'attn_out_g': 3.462677e+00, 'w_out': 2.922653e+00, 'norm2_g': 5.399730e-01, 'w_ff1': 5.154925e-01, 'w_ff2': 2.376715e+00, 'final_ada_w': 1.875229e+01, 'final_ada_b': 4.824083e+01, 'final_norm_g': 7.606223e+01}


def _to_microbatches(a, axis):
    t = _jnp.moveaxis(a, axis, 0)
    t = t.reshape((N_MICROBATCH, t.shape[0] // N_MICROBATCH) + t.shape[1:])
    return _jnp.moveaxis(t, 1, axis + 1)


def setup_inputs(seed: int = 0) -> dict:
    inp = _fwd_setup_inputs(seed)
    key = _jax.random.fold_in(_jax.random.key(seed), 7919)
    shape, _ = _output_shape()
    out = dict(inp)
    out["loss_target"] = _jax.random.normal(_jax.random.fold_in(key, 0), shape, _jnp.float32)
    for i, name in enumerate(TWIN_WEIGHTS):
        w = inp[name].astype(_jnp.float32)
        if MOMENT_SCALE is None:
            s = _jnp.sqrt(_jnp.mean(_jnp.square(w)) + 1e-30)
        else:
            s = MOMENT_SCALE[name]
        km, kv = _jax.random.split(_jax.random.fold_in(key, i + 1))
        out[name] = w
        out["m_" + name] = s * _jax.random.normal(km, w.shape, _jnp.float32)
        out["v_" + name] = (s * s) * _jax.random.uniform(kv, w.shape, _jnp.float32, 0.5, 1.5)
    if N_MICROBATCH > 1:
        for name, axis in PER_EXAMPLE_BATCH_AXIS.items():
            out[name] = _to_microbatches(out[name], axis)
    return {'x': out['x'], 'c': out['c'], 'positions': out['positions'], 'ada_w': out['ada_w'], 'ada_b': out['ada_b'], 'norm1_g': out['norm1_g'], 'w_in': out['w_in'], 'ssm_lambda_re': out['ssm_lambda_re'], 'ssm_lambda_im': out['ssm_lambda_im'], 'ssm_b_re': out['ssm_b_re'], 'ssm_b_im': out['ssm_b_im'], 'ssm_c_re': out['ssm_c_re'], 'ssm_c_im': out['ssm_c_im'], 'ssm_d': out['ssm_d'], 'ssm_log_dt': out['ssm_log_dt'], 'w_glu': out['w_glu'], 'q_norm_g': out['q_norm_g'], 'w_uq': out['w_uq'], 'kv_norm_g': out['kv_norm_g'], 'w_ukv': out['w_ukv'], 'ssm_out_g': out['ssm_out_g'], 'attn_out_g': out['attn_out_g'], 'w_out': out['w_out'], 'norm2_g': out['norm2_g'], 'w_ff1': out['w_ff1'], 'w_ff2': out['w_ff2'], 'final_ada_w': out['final_ada_w'], 'final_ada_b': out['final_ada_b'], 'final_norm_g': out['final_norm_g'], 'loss_target': out['loss_target'], 'm_ada_w': out['m_ada_w'], 'm_ada_b': out['m_ada_b'], 'm_norm1_g': out['m_norm1_g'], 'm_w_in': out['m_w_in'], 'm_ssm_lambda_re': out['m_ssm_lambda_re'], 'm_ssm_lambda_im': out['m_ssm_lambda_im'], 'm_ssm_b_re': out['m_ssm_b_re'], 'm_ssm_b_im': out['m_ssm_b_im'], 'm_ssm_c_re': out['m_ssm_c_re'], 'm_ssm_c_im': out['m_ssm_c_im'], 'm_ssm_d': out['m_ssm_d'], 'm_ssm_log_dt': out['m_ssm_log_dt'], 'm_w_glu': out['m_w_glu'], 'm_q_norm_g': out['m_q_norm_g'], 'm_w_uq': out['m_w_uq'], 'm_kv_norm_g': out['m_kv_norm_g'], 'm_w_ukv': out['m_w_ukv'], 'm_ssm_out_g': out['m_ssm_out_g'], 'm_attn_out_g': out['m_attn_out_g'], 'm_w_out': out['m_w_out'], 'm_norm2_g': out['m_norm2_g'], 'm_w_ff1': out['m_w_ff1'], 'm_w_ff2': out['m_w_ff2'], 'm_final_ada_w': out['m_final_ada_w'], 'm_final_ada_b': out['m_final_ada_b'], 'm_final_norm_g': out['m_final_norm_g'], 'v_ada_w': out['v_ada_w'], 'v_ada_b': out['v_ada_b'], 'v_norm1_g': out['v_norm1_g'], 'v_w_in': out['v_w_in'], 'v_ssm_lambda_re': out['v_ssm_lambda_re'], 'v_ssm_lambda_im': out['v_ssm_lambda_im'], 'v_ssm_b_re': out['v_ssm_b_re'], 'v_ssm_b_im': out['v_ssm_b_im'], 'v_ssm_c_re': out['v_ssm_c_re'], 'v_ssm_c_im': out['v_ssm_c_im'], 'v_ssm_d': out['v_ssm_d'], 'v_ssm_log_dt': out['v_ssm_log_dt'], 'v_w_glu': out['v_w_glu'], 'v_q_norm_g': out['v_q_norm_g'], 'v_w_uq': out['v_w_uq'], 'v_kv_norm_g': out['v_kv_norm_g'], 'v_w_ukv': out['v_w_ukv'], 'v_ssm_out_g': out['v_ssm_out_g'], 'v_attn_out_g': out['v_attn_out_g'], 'v_w_out': out['v_w_out'], 'v_norm2_g': out['v_norm2_g'], 'v_w_ff1': out['v_w_ff1'], 'v_w_ff2': out['v_w_ff2'], 'v_final_ada_w': out['v_final_ada_w'], 'v_final_ada_b': out['v_final_ada_b'], 'v_final_norm_g': out['v_final_norm_g']}


def _loss(weights, diff, rest, loss_target):
    with _jax.named_scope("forward"):
        args = {**rest, TWIN_DIFF_INPUT: diff, **{k: w.astype(_WEIGHT_DTYPES[k]) for k, w in weights.items()}}
        y = _forward(args)
    with _jax.named_scope("loss_head"):
        err = _jnp.square(y.astype(_jnp.float32) - loss_target)
        return 0.5 * _jnp.sum(_jnp.mean(err, axis=-1)) if err.ndim else 0.5 * err


def _adamw(w, g, m, v):
    m = ADAM_B1 * m + (1.0 - ADAM_B1) * g
    v = ADAM_B2 * v + (1.0 - ADAM_B2) * _jnp.square(g)
    m_hat = m / (1.0 - ADAM_B1 ** ADAM_STEP)
    v_hat = v / (1.0 - ADAM_B2 ** ADAM_STEP)
    delta = -ADAM_LR * (m_hat / (_jnp.sqrt(v_hat) + ADAM_EPS) + ADAM_WD * w)
    return delta, m, v


def reference(x, c, positions, ada_w, ada_b, norm1_g, w_in, ssm_lambda_re, ssm_lambda_im, ssm_b_re, ssm_b_im, ssm_c_re, ssm_c_im, ssm_d, ssm_log_dt, w_glu, q_norm_g, w_uq, kv_norm_g, w_ukv, ssm_out_g, attn_out_g, w_out, norm2_g, w_ff1, w_ff2, final_ada_w, final_ada_b, final_norm_g, loss_target, m_ada_w, m_ada_b, m_norm1_g, m_w_in, m_ssm_lambda_re, m_ssm_lambda_im, m_ssm_b_re, m_ssm_b_im, m_ssm_c_re, m_ssm_c_im, m_ssm_d, m_ssm_log_dt, m_w_glu, m_q_norm_g, m_w_uq, m_kv_norm_g, m_w_ukv, m_ssm_out_g, m_attn_out_g, m_w_out, m_norm2_g, m_w_ff1, m_w_ff2, m_final_ada_w, m_final_ada_b, m_final_norm_g, v_ada_w, v_ada_b, v_norm1_g, v_w_in, v_ssm_lambda_re, v_ssm_lambda_im, v_ssm_b_re, v_ssm_b_im, v_ssm_c_re, v_ssm_c_im, v_ssm_d, v_ssm_log_dt, v_w_glu, v_q_norm_g, v_w_uq, v_kv_norm_g, v_w_ukv, v_ssm_out_g, v_attn_out_g, v_w_out, v_norm2_g, v_w_ff1, v_w_ff2, v_final_ada_w, v_final_ada_b, v_final_norm_g):
    given = dict(x=x, c=c, positions=positions, ada_w=ada_w, ada_b=ada_b, norm1_g=norm1_g, w_in=w_in, ssm_lambda_re=ssm_lambda_re, ssm_lambda_im=ssm_lambda_im, ssm_b_re=ssm_b_re, ssm_b_im=ssm_b_im, ssm_c_re=ssm_c_re, ssm_c_im=ssm_c_im, ssm_d=ssm_d, ssm_log_dt=ssm_log_dt, w_glu=w_glu, q_norm_g=q_norm_g, w_uq=w_uq, kv_norm_g=kv_norm_g, w_ukv=w_ukv, ssm_out_g=ssm_out_g, attn_out_g=attn_out_g, w_out=w_out, norm2_g=norm2_g, w_ff1=w_ff1, w_ff2=w_ff2, final_ada_w=final_ada_w, final_ada_b=final_ada_b, final_norm_g=final_norm_g, loss_target=loss_target, m_ada_w=m_ada_w, m_ada_b=m_ada_b, m_norm1_g=m_norm1_g, m_w_in=m_w_in, m_ssm_lambda_re=m_ssm_lambda_re, m_ssm_lambda_im=m_ssm_lambda_im, m_ssm_b_re=m_ssm_b_re, m_ssm_b_im=m_ssm_b_im, m_ssm_c_re=m_ssm_c_re, m_ssm_c_im=m_ssm_c_im, m_ssm_d=m_ssm_d, m_ssm_log_dt=m_ssm_log_dt, m_w_glu=m_w_glu, m_q_norm_g=m_q_norm_g, m_w_uq=m_w_uq, m_kv_norm_g=m_kv_norm_g, m_w_ukv=m_w_ukv, m_ssm_out_g=m_ssm_out_g, m_attn_out_g=m_attn_out_g, m_w_out=m_w_out, m_norm2_g=m_norm2_g, m_w_ff1=m_w_ff1, m_w_ff2=m_w_ff2, m_final_ada_w=m_final_ada_w, m_final_ada_b=m_final_ada_b, m_final_norm_g=m_final_norm_g, v_ada_w=v_ada_w, v_ada_b=v_ada_b, v_norm1_g=v_norm1_g, v_w_in=v_w_in, v_ssm_lambda_re=v_ssm_lambda_re, v_ssm_lambda_im=v_ssm_lambda_im, v_ssm_b_re=v_ssm_b_re, v_ssm_b_im=v_ssm_b_im, v_ssm_c_re=v_ssm_c_re, v_ssm_c_im=v_ssm_c_im, v_ssm_d=v_ssm_d, v_ssm_log_dt=v_ssm_log_dt, v_w_glu=v_w_glu, v_q_norm_g=v_q_norm_g, v_w_uq=v_w_uq, v_kv_norm_g=v_kv_norm_g, v_w_ukv=v_w_ukv, v_ssm_out_g=v_ssm_out_g, v_attn_out_g=v_attn_out_g, v_w_out=v_w_out, v_norm2_g=v_norm2_g, v_w_ff1=v_w_ff1, v_w_ff2=v_w_ff2, v_final_ada_w=v_final_ada_w, v_final_ada_b=v_final_ada_b, v_final_norm_g=v_final_norm_g)
    weights = {n: given[n] for n in TWIN_WEIGHTS}
    shared = {n: given[n] for n in SHARED_INPUTS}
    per_example = {n: given[n] for n in ['x', 'c', 'positions']}
    grad_fn = _jax.value_and_grad(_loss, argnums=(0, 1))

    def one_microbatch(ex, loss_target):
        ex = dict(ex)
        diff = ex.pop(TWIN_DIFF_INPUT)
        return grad_fn(weights, diff, {**shared, **ex}, loss_target)

    if N_MICROBATCH == 1:
        loss, (grad_w, grad_x) = one_microbatch(per_example, given["loss_target"])
    else:
        def body(carry, xs):
            loss_sum, grad_sum = carry
            l_k, (gw_k, gx_k) = one_microbatch(xs[0], xs[1])
            with _jax.named_scope("update"):
                return (loss_sum + l_k, _jax.tree.map(_jnp.add, grad_sum, gw_k)), gx_k

        init = (_jnp.zeros((), _jnp.float32), _jax.tree.map(_jnp.zeros_like, weights))
        (loss, grad_w), grad_x = _jax.lax.scan(body, init, (per_example, given["loss_target"]))
    with _jax.named_scope("update"):
        delta_w, new_m, new_v = {}, {}, {}
        for n in TWIN_WEIGHTS:
            delta_w[n], new_m[n], new_v[n] = _adamw(weights[n], grad_w[n], given["m_" + n], given["v_" + n])
    return (loss, grad_x, *[grad_w[n] for n in TWIN_WEIGHTS], *[delta_w[n] for n in TWIN_WEIGHTS],
            *[new_m[n] for n in TWIN_WEIGHTS], *[new_v[n] for n in TWIN_WEIGHTS])
```

```python
import functools
import inspect
import math

import jax
import jax.numpy as jnp
from jax import lax
from jax.experimental import pallas as pl
from jax.experimental.pallas import tpu as pltpu

F32 = jnp.float32
BF16 = jnp.bfloat16

D = 1024
D_SSM = 512
G = 32
H = 16
P = 64
NST = G * P
D_ATTN = 512
NH = 8
QK_NOPE = 64
QK_ROPE = 32
V_HEAD = 64
HP = 128
Q_LORA = 384
KV_LORA = 256
IN_COLS = D_SSM + Q_LORA + KV_LORA + QK_ROPE
IN_PAD = 1280
D_FF = 4096
ROPE_BASE = 10000.0
EPS = 1e-6
ADAM_LR = 0.001
ADAM_B1 = 0.9
ADAM_B2 = 0.999
ADAM_EPS = 1e-08
ADAM_WD = 0.01
ADAM_STEP = 10
NEG = -1e30
VMEM_LIMIT = 60 << 20

MESH = pl.DeviceIdType.MESH
_VM = pl.BlockSpec(memory_space=pltpu.VMEM)
_ANY = pl.BlockSpec(memory_space=pl.ANY)

BIG = ["ada_w", "w_in", "w_glu", "w_uq", "w_ukv", "w_out", "w_ff1", "w_ff2", "final_ada_w"]
ROW_SHARDED = ("w_out", "w_ff2")
PACK_ROW = 1024
PACK_ALIGN = 2 * 256 * PACK_ROW


def _cp(sem=None, vmem=VMEM_LIMIT):
    kw = dict(vmem_limit_bytes=vmem)
    if sem is not None:
        kw["dimension_semantics"] = sem
    return pltpu.CompilerParams(**kw)


def _dot(a, b):
    return jnp.dot(a, b, preferred_element_type=F32)


def _dot_nt(a, b):
    return lax.dot_general(a, b, (((1,), (1,)), ((), ())), preferred_element_type=F32)


def _dot_tn(a, b):
    return lax.dot_general(a, b, (((0,), (0,)), ((), ())), preferred_element_type=F32)


def _rms(x, n):
    r = lax.rsqrt(jnp.sum(x * x, axis=-1, keepdims=True) * (1.0 / n) + EPS)
    return x * r, r


def _rms_bwd(dyg, xhat, r, n):
    return r * (dyg - xhat * (jnp.sum(dyg * xhat, axis=-1, keepdims=True) * (1.0 / n)))


def _sigmoid(x):
    return 1.0 / (1.0 + jnp.exp(-x))


_GK = math.sqrt(2.0 / math.pi)
_GC = 0.044715


def _gelu(y):
    t = jnp.tanh(_GK * (y + _GC * y * y * y))
    return 0.5 * y * (1.0 + t)


def _gelu_grad(y):
    t = jnp.tanh(_GK * (y + _GC * y * y * y))
    return 0.5 * (1.0 + t) + 0.5 * y * (1.0 - t * t) * _GK * (1.0 + 3.0 * _GC * y * y)


def _colsum(x):
    return jnp.sum(x, axis=0, keepdims=True)


def _roll(x, s):
    return pltpu.roll(x, s % x.shape[-1], x.ndim - 1)


def _mod_fwd(c8, ada_w, ada_b, fada_w, fada_b):
    def body(c_ref, w_ref, b_ref, fw_ref, fb_ref, cond_ref, mod_ref, fmod_ref):
        cv = c_ref[...]
        cond = cv * _sigmoid(cv)
        cond_ref[...] = cond
        cb = cond.astype(BF16)
        mod_ref[...] = _dot(cb, w_ref[...]) + b_ref[...]
        fmod_ref[...] = _dot(cb, fw_ref[...]) + fb_ref[...]

    return pl.pallas_call(
        body, name="mod_fwd",
        out_shape=[jax.ShapeDtypeStruct((8, D), F32), jax.ShapeDtypeStruct((8, 6 * D), F32),
                   jax.ShapeDtypeStruct((8, 2 * D), F32)],
        in_specs=[_VM] * 5, out_specs=[_VM] * 3, compiler_params=_cp(),
    )(c8, ada_w, ada_b, fada_w, fada_b)


def _mod_bwd(cond_t, dmod, name):
    n = dmod.shape[1]
    bc = 512

    def body(ct_ref, dm_ref, gw_ref, gb_ref):
        ct = ct_ref[...]
        dm = dm_ref[...]
        acc = ct[:, 0:1] * dm[0:1, :]
        for b in range(1, 8):
            acc = acc + ct[:, b:b + 1] * dm[b:b + 1, :]
        gw_ref[...] = acc
        gb_ref[...] = _colsum(dm)

    return pl.pallas_call(
        body, name=name, grid=(n // bc,),
        out_shape=[jax.ShapeDtypeStruct((D, n), F32), jax.ShapeDtypeStruct((1, n), F32)],
        in_specs=[_VM, pl.BlockSpec((8, bc), lambda i: (0, i))],
        out_specs=[pl.BlockSpec((D, bc), lambda i: (0, i)), pl.BlockSpec((1, bc), lambda i: (0, i))],
        compiler_params=_cp(("parallel",)),
    )(cond_t, dmod)


def _f1_fwd(x, modp, g1, w_in, S, tm):
    n = x.shape[0]
    tps = S // tm

    def body(x_ref, mod_ref, g_ref, w_ref, h_ref, proj_ref):
        xhat, _ = _rms(x_ref[...], D)
        h = (xhat * g_ref[...]) * (1.0 + mod_ref[0, 1:2, :]) + mod_ref[0, 0:1, :]
        hb = h.astype(BF16)
        h_ref[...] = hb
        proj_ref[...] = _dot(hb, w_ref[...])

    return pl.pallas_call(
        body, name="f1_fwd", grid=(n // tm,),
        out_shape=[jax.ShapeDtypeStruct((n, D), BF16), jax.ShapeDtypeStruct((n, IN_PAD), F32)],
        in_specs=[pl.BlockSpec((tm, D), lambda i: (i, 0)),
                  pl.BlockSpec((1, 8, D), lambda i: (i // tps, 0, 0)), _VM, _VM],
        out_specs=[pl.BlockSpec((tm, D), lambda i: (i, 0)), pl.BlockSpec((tm, IN_PAD), lambda i: (i, 0))],
        compiler_params=_cp(("parallel",)),
    )(x, modp, g1, w_in)


def _f1_bwd(du, dmla, dx1, x, modp, g1, w_in, S, tm):
    n = x.shape[0]
    tps = S // tm
    nb = n // S

    def body(du_ref, dm_ref, dx1_ref, x_ref, mod_ref, g_ref, w_ref, dx_ref, dproj_ref, accs_ref, accg_ref):
        i = pl.program_id(0)
        dproj = jnp.concatenate([du_ref[...], dm_ref[...]], axis=1).astype(BF16)
        dproj_ref[...] = dproj
        dh = _dot_nt(dproj, w_ref[...])
        xhat, r = _rms(x_ref[...], D)
        g = g_ref[...]
        dn = dh * (1.0 + mod_ref[0, 1:2, :])
        dx_ref[...] = dx1_ref[...] + _rms_bwd(dn * g, xhat, r, D)

        @pl.when(i % tps == 0)
        def _():
            accs_ref[...] = jnp.zeros_like(accs_ref)

        @pl.when(i == 0)
        def _():
            accg_ref[...] = jnp.zeros_like(accg_ref)

        accs_ref[0, 0:1, :] += _colsum(dh)
        accs_ref[0, 1:2, :] += _colsum(dh * (xhat * g))
        accg_ref[0:1, :] += _colsum(dn * xhat)

    return pl.pallas_call(
        body, name="f1_bwd", grid=(n // tm,),
        out_shape=[jax.ShapeDtypeStruct((n, D), F32), jax.ShapeDtypeStruct((n, IN_PAD), BF16),
                   jax.ShapeDtypeStruct((nb, 8, D), F32), jax.ShapeDtypeStruct((8, D), F32)],
        in_specs=[pl.BlockSpec((tm, D_SSM), lambda i: (i, 0)), pl.BlockSpec((tm, IN_PAD - D_SSM), lambda i: (i, 0)),
                  pl.BlockSpec((tm, D), lambda i: (i, 0)), pl.BlockSpec((tm, D), lambda i: (i, 0)),
                  pl.BlockSpec((1, 8, D), lambda i: (i // tps, 0, 0)), _VM, _VM],
        out_specs=[pl.BlockSpec((tm, D), lambda i: (i, 0)), pl.BlockSpec((tm, IN_PAD), lambda i: (i, 0)),
                   pl.BlockSpec((1, 8, D), lambda i: (i // tps, 0, 0)), pl.BlockSpec((8, D), lambda i: (0, 0))],
        compiler_params=_cp(("arbitrary",)),
    )(du, dmla, dx1, x, modp, g1, w_in)


def _ssm_param_fwd(lam_re, lam_im, logdt, b_re, b_im):
    def body(lr_ref, li_ref, ld_ref, br_ref, bi_ref, lbr_ref, lbi_ref, bbr_ref, bbi_ref):
        lr, li = lr_ref[...], li_ref[...]
        dt = jnp.exp(ld_ref[...])
        er = jnp.exp(lr * dt)
        lbr = er * jnp.cos(li * dt)
        lbi = er * jnp.sin(li * dt)
        den = 1.0 / (lr * lr + li * li)
        cr = ((lbr - 1.0) * lr + lbi * li) * den
        ci = (lbi * lr - (lbr - 1.0) * li) * den
        lbr_ref[...] = lbr
        lbi_ref[...] = lbi
        bbr_ref[...] = cr * br_ref[...] - ci * bi_ref[...]
        bbi_ref[...] = cr * bi_ref[...] + ci * br_ref[...]

    return pl.pallas_call(
        body, name="ssm_param_fwd",
        out_shape=[jax.ShapeDtypeStruct((NST, 1), F32)] * 2 + [jax.ShapeDtypeStruct((NST, H), F32)] * 2,
        in_specs=[_VM] * 5, out_specs=[_VM] * 4, compiler_params=_cp(),
    )(lam_re, lam_im, logdt, b_re, b_im)


def _ssm_param_bwd(lam_re, lam_im, logdt, b_re, b_im, dlb_re, dlb_im, dbb_re, dbb_im):
    def body(lr_ref, li_ref, ld_ref, br_ref, bi_ref, dlr_ref, dli_ref, dbr_ref, dbi_ref,
             gbr_ref, gbi_ref, glr_ref, gli_ref, gdt_ref):
        lr, li = lr_ref[...], li_ref[...]
        dt = jnp.exp(ld_ref[...])
        er = jnp.exp(lr * dt)
        lbr = er * jnp.cos(li * dt)
        lbi = er * jnp.sin(li * dt)
        den = 1.0 / (lr * lr + li * li)
        nr, ni = lbr - 1.0, lbi
        cr = (nr * lr + ni * li) * den
        ci = (ni * lr - nr * li) * den
        br, bi = br_ref[...], bi_ref[...]
        dbr, dbi = dbr_ref[...], dbi_ref[...]
        gbr_ref[...] = cr * dbr + ci * dbi
        gbi_ref[...] = cr * dbi - ci * dbr
        gcr = jnp.sum(dbr * br + dbi * bi, axis=1, keepdims=True)
        gci = jnp.sum(dbi * br - dbr * bi, axis=1, keepdims=True)
        ilr, ili = lr * den, -li * den
        glbr = dlr_ref[...] + (gcr * ilr + gci * ili)
        glbi = dli_ref[...] + (gci * ilr - gcr * ili)
        qr = -(cr * ilr - ci * ili)
        qi = -(cr * ili + ci * ilr)
        glr = gcr * qr + gci * qi
        gli = gci * qr - gcr * qi
        glr = glr + dt * (glbr * lbr + glbi * lbi)
        gli = gli + dt * (glbi * lbr - glbr * lbi)
        wr = lr * lbr - li * lbi
        wi = lr * lbi + li * lbr
        glr_ref[...] = glr
        gli_ref[...] = gli
        gdt_ref[...] = (glbr * wr + glbi * wi) * dt

    return pl.pallas_call(
        body, name="ssm_param_bwd",
        out_shape=[jax.ShapeDtypeStruct((NST, H), F32)] * 2 + [jax.ShapeDtypeStruct((NST, 1), F32)] * 3,
        in_specs=[_VM] * 9, out_specs=[_VM] * 5, compiler_params=_cp(),
    )(lam_re, lam_im, logdt, b_re, b_im, dlb_re, dlb_im, dbb_re, dbb_im)


def _rowsum(a):
    def body(a_ref, o_ref):
        o_ref[...] = jnp.sum(a_ref[...], axis=1, keepdims=True)

    return pl.pallas_call(
        body, name="rowsum", out_shape=jax.ShapeDtypeStruct((a.shape[0], 1), F32),
        in_specs=[_VM], out_specs=_VM, compiler_params=_cp(),
    )(a)


def _pow2k(pr, pi, nsq):
    for _ in range(nsq):
        pr, pi = pr * pr - pi * pi, 2.0 * pr * pi
    return pr, pi


def _ssm_local(u_p, bm, lre8, lim8, S, tt):
    n = u_p.shape[0]
    nb, nt = n // S, S // tt
    nsq = int(round(math.log2(S // 8)))
    assert 2 ** nsq == S // 8

    def body(u_ref, bm_ref, lre_ref, lim_ref, cre_ref, cim_ref, sre, sim, bu):
        j = pl.program_id(1)

        @pl.when(j == 0)
        def _():
            sre[...] = jnp.zeros_like(sre)
            sim[...] = jnp.zeros_like(sim)

        bu[...] = _dot(u_ref[...].astype(BF16), bm_ref[...])
        lre, lim = lre_ref[...], lim_ref[...]

        def step(i, c):
            sr, si = c
            off = pl.multiple_of(i * 8, 8)
            br = bu[pl.ds(off, 8), 0:NST]
            bi = bu[pl.ds(off, 8), NST:2 * NST]
            return lre * sr - lim * si + br, lre * si + lim * sr + bi

        sr, si = lax.fori_loop(0, tt // 8, step, (sre[...], sim[...]))
        sre[...] = sr
        sim[...] = si

        @pl.when(j == nt - 1)
        def _():
            pr, pi = _pow2k(lre[0:1], lim[0:1], nsq)
            cr = jnp.zeros((1, NST), F32)
            ci = jnp.zeros((1, NST), F32)
            cre_ref[0:1, :] = cr
            cim_ref[0:1, :] = ci
            for k in range(1, 8):
                cr, ci = sr[k - 1:k] + pr * cr - pi * ci, si[k - 1:k] + pr * ci + pi * cr
                cre_ref[k:k + 1, :] = cr
                cim_ref[k:k + 1, :] = ci

    return pl.pallas_call(
        body, name="ssm_local", grid=(nb, nt),
        out_shape=[jax.ShapeDtypeStruct((nb * 8, NST), F32)] * 2,
        in_specs=[pl.BlockSpec((tt, D_SSM), lambda b, j: (b * nt + j, 0)), _VM, _VM, _VM],
        out_specs=[pl.BlockSpec((8, NST), lambda b, j: (b, 0))] * 2,
        scratch_shapes=[pltpu.VMEM((8, NST), F32), pltpu.VMEM((8, NST), F32), pltpu.VMEM((tt, 2 * NST), F32)],
        compiler_params=_cp(("arbitrary", "arbitrary")),
    )(u_p, bm, lre8, lim8)


def _ssm_fwd(u_p, cre, cim, bm, cm, dvec, w_glu, lre8, lim8, S, tt):
    n = u_p.shape[0]
    nb, nt = n // S, S // tt

    def body(u_ref, cre_ref, cim_ref, bm_ref, cm_ref, d_ref, wg_ref, lre_ref, lim_ref,
             st_ref, ypre_ref, z_ref, gact_ref, yssm_ref, sre, sim, bu):
        j = pl.program_id(1)

        @pl.when(j == 0)
        def _():
            sre[...] = cre_ref[...]
            sim[...] = cim_ref[...]

        u = u_ref[...]
        bu[...] = _dot(u.astype(BF16), bm_ref[...])
        lre, lim = lre_ref[...], lim_ref[...]

        def step(i, c):
            sr, si = c
            off = pl.multiple_of(i * 8, 8)
            nr = lre * sr - lim * si + bu[pl.ds(off, 8), 0:NST]
            ni = lre * si + lim * sr + bu[pl.ds(off, 8), NST:2 * NST]
            st_ref[pl.ds(off, 8), 0:NST] = nr
            st_ref[pl.ds(off, 8), NST:2 * NST] = ni
            return nr, ni

        sr, si = lax.fori_loop(0, tt // 8, step, (sre[...], sim[...]))
        sre[...] = sr
        sim[...] = si
        y = _dot(st_ref[...].astype(BF16), cm_ref[...]) + d_ref[...] * u
        ypre_ref[...] = y
        gb = _gelu(y).astype(BF16)
        gact_ref[...] = gb
        z = _dot(gb, wg_ref[...])
        z_ref[...] = z
        yssm_ref[...] = z[:, 0:D_SSM] * _sigmoid(z[:, D_SSM:2 * D_SSM])

    row = lambda w: pl.BlockSpec((tt, w), lambda b, j: (b * nt + j, 0))
    return pl.pallas_call(
        body, name="ssm_fwd", grid=(nb, nt),
        out_shape=[jax.ShapeDtypeStruct((n, 2 * NST), F32), jax.ShapeDtypeStruct((n, D_SSM), F32),
                   jax.ShapeDtypeStruct((n, 2 * D_SSM), F32), jax.ShapeDtypeStruct((n, D_SSM), BF16),
                   jax.ShapeDtypeStruct((n, D_SSM), F32)],
        in_specs=[row(D_SSM), pl.BlockSpec((8, NST), lambda b, j: (b, 0)), pl.BlockSpec((8, NST), lambda b, j: (b, 0)),
                  _VM, _VM, _VM, _VM, _VM, _VM],
        out_specs=[row(2 * NST), row(D_SSM), row(2 * D_SSM), row(D_SSM), row(D_SSM)],
        scratch_shapes=[pltpu.VMEM((8, NST), F32), pltpu.VMEM((8, NST), F32), pltpu.VMEM((tt, 2 * NST), F32)],
        compiler_params=_cp(("arbitrary", "arbitrary")),
    )(u_p, cre, cim, bm, cm, dvec, w_glu, lre8, lim8)


def _ssm_bwd_a(dys_p, z, ypre, w_glu, cm, lre8, lim8, S, tt):
    n = z.shape[0]
    nb, nt = n // S, S // tt
    nsq = int(round(math.log2(S // 8)))
    ng = tt // 8

    def body(dys_ref, z_ref, y_ref, wg_ref, cm_ref, lre_ref, lim_ref, dy_ref, dz_ref, are_ref, aim_ref, sre, sim, gb):
        j = pl.program_id(1)

        @pl.when(j == 0)
        def _():
            sre[...] = jnp.zeros_like(sre)
            sim[...] = jnp.zeros_like(sim)

        z = z_ref[...]
        z1, z2 = z[:, 0:D_SSM], z[:, D_SSM:2 * D_SSM]
        sg = _sigmoid(z2)
        dys = dys_ref[...]
        dz = jnp.concatenate([dys * sg, dys * z1 * sg * (1.0 - sg)], axis=1).astype(BF16)
        dz_ref[...] = dz
        dy = _dot_nt(dz, wg_ref[...]) * _gelu_grad(y_ref[...])
        dy_ref[...] = dy
        gb[...] = _dot_nt(dy.astype(BF16), cm_ref[...])
        lre, lim = lre_ref[...], lim_ref[...]

        def step(i, c):
            ar, ai = c
            off = pl.multiple_of((ng - 1 - i) * 8, 8)
            gr = gb[pl.ds(off, 8), 0:NST]
            gi = gb[pl.ds(off, 8), NST:2 * NST]
            return lre * ar + lim * ai + gr, lre * ai - lim * ar + gi

        ar, ai = lax.fori_loop(0, ng, step, (sre[...], sim[...]))
        sre[...] = ar
        sim[...] = ai

        @pl.when(j == nt - 1)
        def _():
            pr, pi = _pow2k(lre[0:1], -lim[0:1], nsq)
            cr = jnp.zeros((1, NST), F32)
            ci = jnp.zeros((1, NST), F32)
            are_ref[7:8, :] = cr
            aim_ref[7:8, :] = ci
            for k in range(6, -1, -1):
                cr, ci = ar[k + 1:k + 2] + pr * cr - pi * ci, ai[k + 1:k + 2] + pr * ci + pi * cr
                are_ref[k:k + 1, :] = cr
                aim_ref[k:k + 1, :] = ci

    row = lambda w: pl.BlockSpec((tt, w), lambda b, j: (b * nt + nt - 1 - j, 0))
    return pl.pallas_call(
        body, name="ssm_bwd_a", grid=(nb, nt),
        out_shape=[jax.ShapeDtypeStruct((n, D_SSM), F32), jax.ShapeDtypeStruct((n, 2 * D_SSM), BF16),
                   jax.ShapeDtypeStruct((nb * 8, NST), F32), jax.ShapeDtypeStruct((nb * 8, NST), F32)],
        in_specs=[row(D_SSM), row(2 * D_SSM), row(D_SSM), _VM, _VM, _VM, _VM],
        out_specs=[row(D_SSM), row(2 * D_SSM), pl.BlockSpec((8, NST), lambda b, j: (b, 0)),
                   pl.BlockSpec((8, NST), lambda b, j: (b, 0))],
        scratch_shapes=[pltpu.VMEM((8, NST), F32), pltpu.VMEM((8, NST), F32), pltpu.VMEM((tt, 2 * NST), F32)],
        compiler_params=_cp(("arbitrary", "arbitrary")),
    )(dys_p, z, ypre, w_glu, cm, lre8, lim8)


def _ssm_bwd_b(dy, u_p, st, fcr, fci, air, aii, bm, cm, dvec, lre8, lim8, S, tt):
    n = u_p.shape[0]
    nb, nt = n // S, S // tt
    ng = tt // 8
    QB = D_SSM // 4

    def body(dy_ref, u_ref, st_ref, stp_ref, fcr_ref, fci_ref, air_ref, aii_ref, bm_ref, cm_ref, d_ref, lre_ref, lim_ref,
             du_ref, dcm_ref, dbm_ref, dd_ref, dlr_ref, dli_ref, are, aim, accr, acci, sp, ab):
        b = pl.program_id(0)
        j = pl.program_id(1)
        jt = nt - 1 - j

        @pl.when((b == 0) & (j == 0))
        def _():
            dcm_ref[...] = jnp.zeros_like(dcm_ref)
            dbm_ref[...] = jnp.zeros_like(dbm_ref)
            dd_ref[...] = jnp.zeros_like(dd_ref)
            accr[...] = jnp.zeros_like(accr)
            acci[...] = jnp.zeros_like(acci)

        @pl.when(j == 0)
        def _():
            are[...] = air_ref[...]
            aim[...] = aii_ref[...]

        sp[8:tt + 8, :] = st_ref[...]

        @pl.when(jt == 0)
        def _():
            sp[0:8, 0:NST] = fcr_ref[...]
            sp[0:8, NST:2 * NST] = fci_ref[...]

        @pl.when(jt != 0)
        def _():
            sp[0:8, :] = stp_ref[...]

        dy = dy_ref[...]
        u = u_ref[...]
        dyb = dy.astype(BF16)
        ab[...] = _dot_nt(dyb, cm_ref[...])
        lre, lim = lre_ref[...], lim_ref[...]

        def step(i, c):
            ar, ai = c
            off = pl.multiple_of((ng - 1 - i) * 8, 8)
            nr = lre * ar + lim * ai + ab[pl.ds(off, 8), 0:NST]
            ni = lre * ai - lim * ar + ab[pl.ds(off, 8), NST:2 * NST]
            ab[pl.ds(off, 8), 0:NST] = nr
            ab[pl.ds(off, 8), NST:2 * NST] = ni
            pr = sp[pl.ds(off, 8), 0:NST]
            pi = sp[pl.ds(off, 8), NST:2 * NST]
            accr[...] += nr * pr + ni * pi
            acci[...] += ni * pr - nr * pi
            return nr, ni

        ar, ai = lax.fori_loop(0, ng, step, (are[...], aim[...]))
        are[...] = ar
        aim[...] = ai
        a_b = ab[...].astype(BF16)
        du_ref[...] = _dot_nt(a_b, bm_ref[...]) + d_ref[...] * dy
        ub = u.astype(BF16)
        for q in range(4):
            for part in range(2):
                lo = part * NST + q * 4 * QB
                s_q = sp[8:tt + 8, lo:lo + 4 * QB].astype(BF16)
                dcm_ref[lo:lo + 4 * QB, :] += _dot_tn(s_q, dyb[:, q * QB:(q + 1) * QB])
                dbm_ref[:, lo:lo + 4 * QB] += _dot_tn(ub[:, q * QB:(q + 1) * QB], a_b[:, lo:lo + 4 * QB])
        dd_ref[...] += _colsum(dy * u)

        @pl.when((b == nb - 1) & (j == nt - 1))
        def _():
            dlr_ref[...] = _colsum(accr[...])
            dli_ref[...] = _colsum(acci[...])

    row = lambda w: pl.BlockSpec((tt, w), lambda b, j: (b * nt + nt - 1 - j, 0))
    seq8 = pl.BlockSpec((8, NST), lambda b, j: (b, 0))
    prev = pl.BlockSpec((8, 2 * NST), lambda b, j: (jnp.maximum((b * nt + nt - 1 - j) * ng - 1, 0), 0))
    const = lambda shape: pl.BlockSpec(shape, lambda b, j: (0, 0))
    return pl.pallas_call(
        body, name="ssm_bwd_b", grid=(nb, nt),
        out_shape=[jax.ShapeDtypeStruct((n, D_SSM), F32), jax.ShapeDtypeStruct((2 * NST, QB), F32),
                   jax.ShapeDtypeStruct((QB, 2 * NST), F32), jax.ShapeDtypeStruct((1, D_SSM), F32),
                   jax.ShapeDtypeStruct((1, NST), F32), jax.ShapeDtypeStruct((1, NST), F32)],
        in_specs=[row(D_SSM), row(D_SSM), row(2 * NST), prev, seq8, seq8, seq8, seq8, _VM, _VM, _VM, _VM, _VM],
        out_specs=[row(D_SSM), const((2 * NST, QB)), const((QB, 2 * NST)), const((1, D_SSM)),
                   const((1, NST)), const((1, NST))],
        scratch_shapes=[pltpu.VMEM((8, NST), F32)] * 4 + [pltpu.VMEM((tt + 8, 2 * NST), F32),
                                                          pltpu.VMEM((tt, 2 * NST), F32)],
        compiler_params=_cp(("arbitrary", "arbitrary")),
    )(dy, u_p, st, st, fcr, fci, air, aii, bm, cm, dvec, lre8, lim8)


def _rope(v, c, s1, s2):
    return v * c + _roll(v, -16) * s1 + _roll(v, 16) * s2


def _rope_t(dv, c, s1, s2):
    return dv * c + _roll(dv * s1, 16) + _roll(dv * s2, -16)


def _mla_fwd(proj, rc, rs1, rs2, gq, gkv, w_uq, w_ukv, tm):
    n = proj.shape[0]

    def body(ql_ref, kvl_ref, kr_ref, c_ref, s1_ref, s2_ref, gq_ref, gkv_ref, wq_ref, wkv_ref,
             q_ref, k_ref, v_ref, qn_ref, kvn_ref):
        c, s1, s2 = c_ref[...], s1_ref[...], s2_ref[...]
        qhat, _ = _rms(ql_ref[...], Q_LORA)
        qn = (qhat * gq_ref[...]).astype(BF16)
        qn_ref[...] = qn
        q = _dot(qn, wq_ref[...])
        q_ref[...] = _rope(q, jnp.tile(c, (1, NH)), jnp.tile(s1, (1, NH)), jnp.tile(s2, (1, NH))).astype(BF16)
        khat, _ = _rms(kvl_ref[...], KV_LORA)
        kvn = (khat * gkv_ref[...]).astype(BF16)
        kvn_ref[...] = kvn
        kv = _dot(kvn, wkv_ref[...])
        kr = _rope(_roll(kr_ref[...], 64), c, s1, s2)
        k_ref[...] = (kv[:, 0:NH * HP] + jnp.tile(kr, (1, NH))).astype(BF16)
        v_ref[...] = kv[:, NH * HP:2 * NH * HP].astype(BF16)

    def wrapped(proj_ref, *rest):
        ql = proj_ref.at[:, D_SSM:D_SSM + Q_LORA]
        kvl = proj_ref.at[:, D_SSM + Q_LORA:D_SSM + Q_LORA + KV_LORA]
        kr = proj_ref.at[:, IN_PAD - HP:IN_PAD]
        body(ql, kvl, kr, *rest)

    row = lambda w: pl.BlockSpec((tm, w), lambda i: (i, 0))
    return pl.pallas_call(
        wrapped, name="mla_fwd", grid=(n // tm,),
        out_shape=[jax.ShapeDtypeStruct((n, NH * HP), BF16)] * 3 +
                  [jax.ShapeDtypeStruct((n, Q_LORA), BF16), jax.ShapeDtypeStruct((n, KV_LORA), BF16)],
        in_specs=[row(IN_PAD), row(HP), row(HP), row(HP), _VM, _VM, _VM, _VM],
        out_specs=[row(NH * HP)] * 3 + [row(Q_LORA), row(KV_LORA)],
        compiler_params=_cp(("parallel",)),
    )(proj, rc, rs1, rs2, gq, gkv, w_uq, w_ukv)


def _mla_bwd(dq, dk, dv, proj, rc, rs1, rs2, gq, gkv, w_uq, w_ukv, tm):
    n = proj.shape[0]

    def body(dq_ref, dk_ref, dv_ref, proj_ref, c_ref, s1_ref, s2_ref, gq_ref, gkv_ref, wq_ref, wkv_ref,
             dmla_ref, dqb_ref, dkvb_ref, acc_ref):
        i = pl.program_id(0)
        c, s1, s2 = c_ref[...], s1_ref[...], s2_ref[...]
        dqu = _rope_t(dq_ref[...], jnp.tile(c, (1, NH)), jnp.tile(s1, (1, NH)), jnp.tile(s2, (1, NH))).astype(BF16)
        dqb_ref[...] = dqu
        dqn = _dot_nt(dqu, wq_ref[...])
        qhat, rq = _rms(proj_ref[:, D_SSM:D_SSM + Q_LORA], Q_LORA)
        dql = _rms_bwd(dqn * gq_ref[...], qhat, rq, Q_LORA)
        dkf = dk_ref[...]
        dkv = jnp.concatenate([dkf, dv_ref[...]], axis=1).astype(BF16)
        dkvb_ref[...] = dkv
        dkvn = _dot_nt(dkv, wkv_ref[...])
        khat, rk = _rms(proj_ref[:, D_SSM + Q_LORA:D_SSM + Q_LORA + KV_LORA], KV_LORA)
        dkvl = _rms_bwd(dkvn * gkv_ref[...], khat, rk, KV_LORA)
        dkr = dkf[:, 0:HP]
        for h in range(1, NH):
            dkr = dkr + dkf[:, h * HP:(h + 1) * HP]
        lane = lax.broadcasted_iota(jnp.int32, dkr.shape, 1)
        dkr = jnp.where((lane >= QK_NOPE) & (lane < QK_NOPE + QK_ROPE), dkr, 0.0)
        dkr = _roll(_rope_t(dkr, c, s1, s2), -64)
        dmla_ref[...] = jnp.concatenate([dql, dkvl, dkr], axis=1)

        @pl.when(i == 0)
        def _():
            acc_ref[...] = jnp.zeros_like(acc_ref)

        acc_ref[0:1, 0:Q_LORA] += _colsum(dqn * qhat)
        acc_ref[1:2, 0:KV_LORA] += _colsum(dkvn * khat)

    row = lambda w: pl.BlockSpec((tm, w), lambda i: (i, 0))
    return pl.pallas_call(
        body, name="mla_bwd", grid=(n // tm,),
        out_shape=[jax.ShapeDtypeStruct((n, IN_PAD - D_SSM), F32), jax.ShapeDtypeStruct((n, NH * HP), BF16),
                   jax.ShapeDtypeStruct((n, 2 * NH * HP), BF16), jax.ShapeDtypeStruct((8, Q_LORA), F32)],
        in_specs=[row(NH * HP)] * 3 + [row(IN_PAD), row(HP), row(HP), row(HP), _VM, _VM, _VM, _VM],
        out_specs=[row(IN_PAD - D_SSM), row(NH * HP), row(2 * NH * HP), pl.BlockSpec((8, Q_LORA), lambda i: (0, 0))],
        compiler_params=_cp(("arbitrary",)),
    )(dq, dk, dv, proj, rc, rs1, rs2, gq, gkv, w_uq, w_ukv)


_SCALE = (QK_NOPE + QK_ROPE) ** -0.5


def _causal(s, row0, col0):
    rows = row0 + lax.broadcasted_iota(jnp.int32, s.shape, 0)
    cols = col0 + lax.broadcasted_iota(jnp.int32, s.shape, 1)
    return cols <= rows


def _attn_fwd(q, k, v, S, tq):
    n = q.shape[0]
    nb, nq = n // S, S // tq

    def body(q_ref, k_ref, v_ref, o_ref, lse_ref):
        qi = pl.program_id(2)
        qv = q_ref[...]

        def step(j, c):
            m, l, acc = c
            off = pl.multiple_of(j * tq, tq)
            s = _dot_nt(qv, k_ref[pl.ds(off, tq), :]) * _SCALE
            s = jnp.where(_causal(s, qi * tq, off), s, NEG)
            mn = jnp.maximum(m, jnp.max(s, axis=1, keepdims=True))
            p = jnp.exp(s - mn)
            al = jnp.exp(m - mn)
            return mn, al * l + jnp.sum(p, axis=1, keepdims=True), al * acc + _dot(p.astype(BF16), v_ref[pl.ds(off, tq), :])

        m, l, acc = lax.fori_loop(0, qi + 1, step, (jnp.full((tq, 1), NEG, F32), jnp.zeros((tq, 1), F32),
                                                    jnp.zeros((tq, HP), F32)))
        o_ref[...] = acc / l
        lse_ref[...] = jnp.broadcast_to(m + jnp.log(l), (tq, HP))

    qs = pl.BlockSpec((tq, HP), lambda b, h, i: (b * nq + i, h))
    ks = pl.BlockSpec((S, HP), lambda b, h, i: (b, h))
    return pl.pallas_call(
        body, name="attn_fwd", grid=(nb, NH, nq),
        out_shape=[jax.ShapeDtypeStruct((n, NH * HP), F32)] * 2,
        in_specs=[qs, ks, ks], out_specs=[qs, qs],
        compiler_params=_cp(("parallel", "parallel", "arbitrary")),
    )(q, k, v)


def _attn_bwd_dq(q, k, v, o, do, lse, S, tq):
    n = q.shape[0]
    nb, nq = n // S, S // tq

    def body(q_ref, k_ref, v_ref, o_ref, do_ref, lse_ref, dq_ref):
        qi = pl.program_id(2)
        qv = q_ref[...]
        dov = do_ref[...]
        dob = dov.astype(BF16)
        delta = jnp.sum(dov * o_ref[...], axis=1, keepdims=True)
        lse = lse_ref[:, 0:1]

        def step(j, acc):
            off = pl.multiple_of(j * tq, tq)
            kt = k_ref[pl.ds(off, tq), :]
            s = _dot_nt(qv, kt) * _SCALE
            p = jnp.where(_causal(s, qi * tq, off), jnp.exp(s - lse), 0.0)
            dp = _dot_nt(dob, v_ref[pl.ds(off, tq), :])
            ds = (p * (dp - delta) * _SCALE).astype(BF16)
            return acc + _dot(ds, kt)

        dq_ref[...] = lax.fori_loop(0, qi + 1, step, jnp.zeros((tq, HP), F32))

    qs = pl.BlockSpec((tq, HP), lambda b, h, i: (b * nq + i, h))
    ks = pl.BlockSpec((S, HP), lambda b, h, i: (b, h))
    return pl.pallas_call(
        body, name="attn_bwd_dq", grid=(nb, NH, nq),
        out_shape=jax.ShapeDtypeStruct((n, NH * HP), F32),
        in_specs=[qs, ks, ks, qs, qs, qs], out_specs=qs,
        compiler_params=_cp(("parallel", "parallel", "arbitrary")),
    )(q, k, v, o, do, lse)


def _attn_bwd_dkv(q, k, v, o, do, lse, S, tq):
    n = q.shape[0]
    nb, nq = n // S, S // tq

    def body(q_ref, k_ref, v_ref, o_ref, do_ref, lse_ref, dk_ref, dv_ref):
        kj = pl.program_id(2)
        kt = k_ref[...]
        vt = v_ref[...]

        def step(i, c):
            dk, dv = c
            off = pl.multiple_of(i * tq, tq)
            qv = q_ref[pl.ds(off, tq), :]
            dov = do_ref[pl.ds(off, tq), :]
            dob = dov.astype(BF16)
            delta = jnp.sum(dov * o_ref[pl.ds(off, tq), :], axis=1, keepdims=True)
            s = _dot_nt(qv, kt) * _SCALE
            p = jnp.where(_causal(s, off, kj * tq), jnp.exp(s - lse_ref[pl.ds(off, tq), 0:1]), 0.0)
            dv = dv + _dot_tn(p.astype(BF16), dob)
            dp = _dot_nt(dob, vt)
            ds = (p * (dp - delta) * _SCALE).astype(BF16)
            return dk + _dot_tn(ds, qv), dv

        dk, dv = lax.fori_loop(kj, nq, step, (jnp.zeros((tq, HP), F32), jnp.zeros((tq, HP), F32)))
        dk_ref[...] = dk
        dv_ref[...] = dv

    ts = pl.BlockSpec((tq, HP), lambda b, h, i: (b * nq + i, h))
    fs = pl.BlockSpec((S, HP), lambda b, h, i: (b, h))
    return pl.pallas_call(
        body, name="attn_bwd_dkv", grid=(nb, NH, nq),
        out_shape=[jax.ShapeDtypeStruct((n, NH * HP), F32)] * 2,
        in_specs=[fs, ts, ts, fs, fs, fs], out_specs=[ts, ts],
        compiler_params=_cp(("parallel", "parallel", "arbitrary")),
    )(q, k, v, o, do, lse)


def _p1_fwd(yssm, oattn, x, modp, gs, ga, w_out, g2, S, tm):
    n = x.shape[0]
    tps = S // tm

    def body(ys_ref, oa_ref, x_ref, mod_ref, gs_ref, ga_ref, w_ref, g2_ref, yn_ref, o_ref, x1_ref, h2_ref):
        yh, _ = _rms(ys_ref[...], D_SSM)
        ah, _ = _rms(oa_ref[...], D_ATTN)
        yn = jnp.concatenate([yh * gs_ref[...], ah * ga_ref[...]], axis=1).astype(BF16)
        yn_ref[...] = yn
        o = _dot(yn, w_ref[...])
        o_ref[...] = o
        x1 = x_ref[...] + mod_ref[0, 2:3, :] * o
        x1_ref[...] = x1
        xh, _ = _rms(x1, D)
        h2_ref[...] = ((xh * g2_ref[...]) * (1.0 + mod_ref[0, 4:5, :]) + mod_ref[0, 3:4, :]).astype(BF16)

    row = lambda w: pl.BlockSpec((tm, w), lambda i: (i, 0))
    return pl.pallas_call(
        body, name="p1_fwd", grid=(n // tm,),
        out_shape=[jax.ShapeDtypeStruct((n, D_SSM + NH * HP), BF16), jax.ShapeDtypeStruct((n, D), F32),
                   jax.ShapeDtypeStruct((n, D), F32), jax.ShapeDtypeStruct((n, D), BF16)],
        in_specs=[row(D_SSM), row(NH * HP), row(D), pl.BlockSpec((1, 8, D), lambda i: (i // tps, 0, 0)),
                  _VM, _VM, _VM, _VM],
        out_specs=[row(D_SSM + NH * HP), row(D), row(D), row(D)],
        compiler_params=_cp(("parallel",)),
    )(yssm, oattn, x, modp, gs, ga, w_out, g2)


def _p2(x1, h2, target, modp, g2, gf, w_ff1, w_ff2, S, tm):
    n = x1.shape[0]
    tps = S // tm
    nb = n // S

    def body(x1_ref, h2_ref, t_ref, mod_ref, g2_ref, gf_ref, w1_ref, w2_ref,
             dx1_ref, r_ref, da_ref, dff_ref, accs_ref, accg_ref):
        i = pl.program_id(0)
        sh2, sc2, gt2 = mod_ref[0, 3:4, :], mod_ref[0, 4:5, :], mod_ref[0, 5:6, :]
        fsh, fsc = mod_ref[0, 6:7, :], mod_ref[0, 7:8, :]
        x1 = x1_ref[...]
        a = _dot(h2_ref[...], w1_ref[...])
        ra = jnp.maximum(a, 0.0)
        rb = (ra * ra).astype(BF16)
        r_ref[...] = rb
        ff = _dot(rb, w2_ref[...])
        x2 = x1 + gt2 * ff
        x2h, rf = _rms(x2, D)
        gf_v = gf_ref[...]
        outn = x2h * gf_v
        err = outn * (1.0 + fsc) + fsh - t_ref[...]
        dout = err * (1.0 / D)
        doutn = dout * (1.0 + fsc)
        dx2 = _rms_bwd(doutn * gf_v, x2h, rf, D)
        dff = (gt2 * dx2).astype(BF16)
        dff_ref[...] = dff
        dr = _dot_nt(dff, w2_ref[...])
        da = (dr * (2.0 * ra)).astype(BF16)
        da_ref[...] = da
        dh2 = _dot_nt(da, w1_ref[...])
        x1h, r2 = _rms(x1, D)
        g2_v = g2_ref[...]
        dn2 = dh2 * (1.0 + sc2)
        dx1_ref[...] = dx2 + _rms_bwd(dn2 * g2_v, x1h, r2, D)

        @pl.when(i % tps == 0)
        def _():
            accs_ref[...] = jnp.zeros_like(accs_ref)

        @pl.when(i == 0)
        def _():
            accg_ref[...] = jnp.zeros_like(accg_ref)

        accs_ref[0, 3:4, :] += _colsum(dh2)
        accs_ref[0, 4:5, :] += _colsum(dh2 * (x1h * g2_v))
        accs_ref[0, 5:6, :] += _colsum(dx2 * ff)
        accs_ref[0, 6:7, :] += _colsum(dout)
        accs_ref[0, 7:8, :] += _colsum(dout * outn)
        accg_ref[0:1, :] += _colsum(dn2 * x1h)
        accg_ref[1:2, :] += _colsum(doutn * x2h)
        accg_ref[2:3, :] += _colsum(err * err) * (0.5 / D)

    row = lambda w: pl.BlockSpec((tm, w), lambda i: (i, 0))
    return pl.pallas_call(
        body, name="p2_mlp_loss", grid=(n // tm,),
        out_shape=[jax.ShapeDtypeStruct((n, D), F32), jax.ShapeDtypeStruct((n, D_FF), BF16),
                   jax.ShapeDtypeStruct((n, D_FF), BF16), jax.ShapeDtypeStruct((n, D), BF16),
                   jax.ShapeDtypeStruct((nb, 8, D), F32), jax.ShapeDtypeStruct((8, D), F32)],
        in_specs=[row(D), row(D), row(D), pl.BlockSpec((1, 8, D), lambda i: (i // tps, 0, 0)), _VM, _VM, _VM, _VM],
        out_specs=[row(D), row(D_FF), row(D_FF), row(D), pl.BlockSpec((1, 8, D), lambda i: (i // tps, 0, 0)),
                   pl.BlockSpec((8, D), lambda i: (0, 0))],
        compiler_params=_cp(("arbitrary",)),
    )(x1, h2, target, modp, g2, gf, w_ff1, w_ff2)


def _p3_bwd(dx1, o, yssm, oattn, modp, gs, ga, w_out, S, tm):
    n = dx1.shape[0]
    tps = S // tm
    nb = n // S

    def body(dx1_ref, o_ref, ys_ref, oa_ref, mod_ref, gs_ref, ga_ref, w_ref,
             do_ref, dys_ref, doa_ref, accs_ref, accg_ref):
        i = pl.program_id(0)
        dx1 = dx1_ref[...]
        dob = (mod_ref[0, 2:3, :] * dx1).astype(BF16)
        do_ref[...] = dob
        dyn = _dot_nt(dob, w_ref[...])
        yh, rs = _rms(ys_ref[...], D_SSM)
        ah, ra = _rms(oa_ref[...], D_ATTN)
        d1 = dyn[:, 0:D_SSM]
        d2 = dyn[:, D_SSM:D_SSM + NH * HP]
        dys_ref[...] = _rms_bwd(d1 * gs_ref[...], yh, rs, D_SSM)
        doa_ref[...] = _rms_bwd(d2 * ga_ref[...], ah, ra, D_ATTN)

        @pl.when(i % tps == 0)
        def _():
            accs_ref[...] = jnp.zeros_like(accs_ref)

        @pl.when(i == 0)
        def _():
            accg_ref[...] = jnp.zeros_like(accg_ref)

        accs_ref[0, 2:3, :] += _colsum(dx1 * o_ref[...])
        accg_ref[0:1, 0:D_SSM] += _colsum(d1 * yh)
        accg_ref[1:2, :] += _colsum(d2 * ah)

    row = lambda w: pl.BlockSpec((tm, w), lambda i: (i, 0))
    return pl.pallas_call(
        body, name="p3_bwd", grid=(n // tm,),
        out_shape=[jax.ShapeDtypeStruct((n, D), BF16), jax.ShapeDtypeStruct((n, D_SSM), F32),
                   jax.ShapeDtypeStruct((n, NH * HP), F32), jax.ShapeDtypeStruct((nb, 8, D), F32),
                   jax.ShapeDtypeStruct((8, NH * HP), F32)],
        in_specs=[row(D), row(D), row(D_SSM), row(NH * HP), pl.BlockSpec((1, 8, D), lambda i: (i // tps, 0, 0)),
                  _VM, _VM, _VM],
        out_specs=[row(D), row(D_SSM), row(NH * HP), pl.BlockSpec((1, 8, D), lambda i: (i // tps, 0, 0)),
                   pl.BlockSpec((8, NH * HP), lambda i: (0, 0))],
        compiler_params=_cp(("arbitrary",)),
    )(dx1, o, yssm, oattn, modp, gs, ga, w_out)


def _wgrad(a, b, name):
    n, k1 = a.shape
    k2 = b.shape[1]
    bn = 512 if n % 512 == 0 else n
    bk1 = 512 if k1 % 512 == 0 else k1
    bk2 = 1024 if (k2 % 1024 == 0) else k2

    def body(a_ref, b_ref, o_ref):
        @pl.when(pl.program_id(2) == 0)
        def _():
            o_ref[...] = jnp.zeros_like(o_ref)

        o_ref[...] += _dot_tn(a_ref[...], b_ref[...])

    return pl.pallas_call(
        body, name=name, grid=(k1 // bk1, k2 // bk2, n // bn),
        out_shape=jax.ShapeDtypeStruct((k1, k2), F32),
        in_specs=[pl.BlockSpec((bn, bk1), lambda i, j, t: (t, i)), pl.BlockSpec((bn, bk2), lambda i, j, t: (t, j))],
        out_specs=pl.BlockSpec((bk1, bk2), lambda i, j, t: (i, j)),
        compiler_params=_cp(("parallel", "parallel", "arbitrary")),
    )(a, b)


def _add_half(g, recv, cidx):
    _, _, rows, w = g.shape
    br = 256

    def body(c_ref, g_ref, r_ref, o_ref):
        o_ref[...] = g_ref[0] + r_ref[...]

    return pl.pallas_call(
        body, name="grad_add_sibling",
        grid_spec=pltpu.PrefetchScalarGridSpec(
            num_scalar_prefetch=1, grid=(4, rows // br),
            in_specs=[pl.BlockSpec((1, 1, br, w), lambda s, i, c: (s, c[0], i, 0)),
                      pl.BlockSpec((1, br, w), lambda s, i, c: (s, i, 0))],
            out_specs=pl.BlockSpec((1, br, w), lambda s, i, c: (s, i, 0))),
        out_shape=jax.ShapeDtypeStruct((4, rows, w), F32),
        compiler_params=_cp(("parallel", "parallel")),
    )(cidx, g, recv)


def _add_chips(r):
    _, rows, w = r.shape
    br = 256

    def body(r_ref, o_ref):
        o_ref[...] = ((r_ref[0] + r_ref[1]) + r_ref[2]) + r_ref[3]

    return pl.pallas_call(
        body, name="grad_add_chips", grid=(rows // br,),
        out_shape=jax.ShapeDtypeStruct((rows, w), F32),
        in_specs=[pl.BlockSpec((4, br, w), lambda i: (0, i, 0))],
        out_specs=pl.BlockSpec((br, w), lambda i: (i, 0)),
        compiler_params=_cp(("parallel",)),
    )(r)


def _adamw(w, g, m, v):
    rows, wd = w.shape
    br = 256

    def body(w_ref, g_ref, m_ref, v_ref, d_ref, nm_ref, nv_ref):
        gv = g_ref[...]
        m_new = ADAM_B1 * m_ref[...] + (1.0 - ADAM_B1) * gv
        v_new = ADAM_B2 * v_ref[...] + (1.0 - ADAM_B2) * (gv * gv)
        m_hat = m_new / (1.0 - ADAM_B1 ** ADAM_STEP)
        v_hat = v_new / (1.0 - ADAM_B2 ** ADAM_STEP)
        d_ref[...] = -ADAM_LR * (m_hat / (jnp.sqrt(v_hat) + ADAM_EPS) + ADAM_WD * w_ref[...])
        nm_ref[...] = m_new
        nv_ref[...] = v_new

    spec = pl.BlockSpec((br, wd), lambda i: (i, 0))
    return pl.pallas_call(
        body, name="adamw", grid=(rows // br,),
        out_shape=[jax.ShapeDtypeStruct((rows, wd), F32)] * 3,
        in_specs=[spec] * 4, out_specs=[spec] * 3,
        compiler_params=_cp(("parallel",)),
    )(w, g, m, v)


def _place():
    x, y, c = lax.axis_index("x"), lax.axis_index("y"), lax.axis_index("c")
    chips = [(1 - x, y), (x, 1 - y), (1 - x, 1 - y)]
    return x, y, c, chips


def _gather_chips(pw):
    rows, w = pw.shape

    def body(pw_ref, out_ref, send_sems, recv_sems, local_sem):
        x, y, c, chips = _place()
        mine = 2 * x + y
        own = pltpu.make_async_copy(pw_ref, out_ref.at[mine], local_sem)
        own.start()
        sends = [pltpu.make_async_remote_copy(src_ref=pw_ref, dst_ref=out_ref.at[mine], send_sem=send_sems.at[j],
                                              recv_sem=recv_sems.at[j], device_id=(cx, cy, c), device_id_type=MESH)
                 for j, (cx, cy) in enumerate(chips)]
        for cp in sends:
            cp.start()
        for j, (cx, cy) in enumerate(chips):
            pltpu.make_async_remote_copy(src_ref=pw_ref, dst_ref=out_ref.at[2 * cx + cy], send_sem=send_sems.at[j],
                                         recv_sem=recv_sems.at[j], device_id=(cx, cy, c),
                                         device_id_type=MESH).wait_recv()
        for cp in sends:
            cp.wait_send()
        own.wait()

    return pl.pallas_call(
        body, name="gather_weights", out_shape=jax.ShapeDtypeStruct((4, rows, w), pw.dtype),
        in_specs=[_ANY], out_specs=_ANY,
        scratch_shapes=[pltpu.SemaphoreType.DMA((3,)), pltpu.SemaphoreType.DMA((3,)), pltpu.SemaphoreType.DMA],
        compiler_params=pltpu.CompilerParams(has_side_effects=True),
    )(pw)


def _swap_halves(g):
    _, _, rows, w = g.shape

    def body(g_ref, out_ref, send_sem, recv_sem):
        x, y, c, _ = _place()
        cp = pltpu.make_async_remote_copy(src_ref=g_ref.at[:, 1 - c], dst_ref=out_ref, send_sem=send_sem,
                                          recv_sem=recv_sem, device_id=(x, y, 1 - c), device_id_type=MESH)
        cp.start()
        cp.wait()

    return pl.pallas_call(
        body, name="grad_swap_sibling", out_shape=jax.ShapeDtypeStruct((4, rows, w), g.dtype),
        in_specs=[_ANY], out_specs=_ANY,
        scratch_shapes=[pltpu.SemaphoreType.DMA, pltpu.SemaphoreType.DMA],
        compiler_params=pltpu.CompilerParams(has_side_effects=True),
    )(g)


def _scatter_chips(a):
    _, rows, w = a.shape

    def body(a_ref, out_ref, send_sems, recv_sems, local_sem):
        x, y, c, chips = _place()
        mine = 2 * x + y
        own = pltpu.make_async_copy(a_ref.at[mine], out_ref.at[mine], local_sem)
        own.start()
        sends = [pltpu.make_async_remote_copy(src_ref=a_ref.at[2 * cx + cy], dst_ref=out_ref.at[mine],
                                              send_sem=send_sems.at[j], recv_sem=recv_sems.at[j],
                                              device_id=(cx, cy, c), device_id_type=MESH)
                 for j, (cx, cy) in enumerate(chips)]
        for cp in sends:
            cp.start()
        for j, (cx, cy) in enumerate(chips):
            pltpu.make_async_remote_copy(src_ref=a_ref.at[mine], dst_ref=out_ref.at[2 * cx + cy],
                                         send_sem=send_sems.at[j], recv_sem=recv_sems.at[j],
                                         device_id=(cx, cy, c), device_id_type=MESH).wait_recv()
        for cp in sends:
            cp.wait_send()
        own.wait()

    return pl.pallas_call(
        body, name="grad_scatter_chips", out_shape=jax.ShapeDtypeStruct((4, rows, w), a.dtype),
        in_specs=[_ANY], out_specs=_ANY,
        scratch_shapes=[pltpu.SemaphoreType.DMA((3,)), pltpu.SemaphoreType.DMA((3,)), pltpu.SemaphoreType.DMA],
        compiler_params=pltpu.CompilerParams(has_side_effects=True),
    )(a)


def _join_halves(r):
    rows, w = r.shape

    def body(r_ref, out_ref, send_sem, recv_sem, local_sem):
        x, y, c, _ = _place()
        own = pltpu.make_async_copy(r_ref, out_ref.at[c], local_sem)
        own.start()
        cp = pltpu.make_async_remote_copy(src_ref=r_ref, dst_ref=out_ref.at[c], send_sem=send_sem,
                                          recv_sem=recv_sem, device_id=(x, y, 1 - c), device_id_type=MESH)
        cp.start()
        pltpu.make_async_remote_copy(src_ref=r_ref, dst_ref=out_ref.at[1 - c], send_sem=send_sem,
                                     recv_sem=recv_sem, device_id=(x, y, 1 - c), device_id_type=MESH).wait_recv()
        cp.wait_send()
        own.wait()

    return pl.pallas_call(
        body, name="grad_join_sibling", out_shape=jax.ShapeDtypeStruct((2, rows, w), r.dtype),
        in_specs=[_ANY], out_specs=_ANY,
        scratch_shapes=[pltpu.SemaphoreType.DMA, pltpu.SemaphoreType.DMA, pltpu.SemaphoreType.DMA],
        compiler_params=pltpu.CompilerParams(has_side_effects=True),
    )(r)


def _pad_heads_cols(w, per, used):
    k = w.shape[0]
    w = w.reshape(k, NH, per)[:, :, :used]
    return jnp.pad(w, ((0, 0), (0, 0), (0, HP - used))).reshape(k, NH * HP)


def _unpad_heads_cols(w, used):
    k = w.shape[0]
    return w.reshape(k, NH, HP)[:, :, :used]


def _prep_weights(wf):
    bf = lambda a: a.astype(BF16)
    out = {}
    out["ada_w"] = bf(wf["ada_w"])
    out["final_ada_w"] = bf(wf["final_ada_w"])
    out["w_in"] = jnp.pad(bf(wf["w_in"]), ((0, 0), (0, IN_PAD - IN_COLS)))
    out["w_glu"] = bf(wf["w_glu"])
    out["w_uq"] = _pad_heads_cols(bf(wf["w_uq"]), QK_NOPE + QK_ROPE, QK_NOPE + QK_ROPE)
    wkv = bf(wf["w_ukv"]).reshape(KV_LORA, NH, QK_NOPE + V_HEAD)
    wk = jnp.pad(wkv[:, :, :QK_NOPE], ((0, 0), (0, 0), (0, HP - QK_NOPE))).reshape(KV_LORA, NH * HP)
    wv = jnp.pad(wkv[:, :, QK_NOPE:], ((0, 0), (0, 0), (0, HP - V_HEAD))).reshape(KV_LORA, NH * HP)
    out["w_ukv"] = jnp.concatenate([wk, wv], axis=1)
    wo = bf(wf["w_out"])
    wo_a = jnp.pad(wo[D_SSM:].reshape(NH, V_HEAD, D), ((0, 0), (0, HP - V_HEAD), (0, 0))).reshape(NH * HP, D)
    out["w_out"] = jnp.concatenate([wo[:D_SSM], wo_a], axis=0)
    out["w_ff1"] = bf(wf["w_ff1"])
    out["w_ff2"] = bf(wf["w_ff2"])
    return out


def _rope_tables(positions):
    inv_freq = ROPE_BASE ** (-jnp.arange(0, QK_ROPE, 2, dtype=F32) / QK_ROPE)
    ang = positions.astype(F32)[:, None] * inv_freq
    cos, sin = jnp.cos(ang), jnp.sin(ang)
    n = positions.shape[0]
    one = jnp.ones((n, QK_NOPE), F32)
    z16 = jnp.zeros((n, 16), F32)
    z32 = jnp.zeros((n, 32), F32)
    z64 = jnp.zeros((n, QK_NOPE), F32)
    rc = jnp.concatenate([one, cos, cos, z32], axis=1)
    rs1 = jnp.concatenate([z64, -sin, z16, z32], axis=1)
    rs2 = jnp.concatenate([z64, z16, sin, z32], axis=1)
    return rc, rs1, rs2


def _permute_rows(a, S):
    n, w = a.shape
    return a.reshape(n // S, 8, S // 8, w).transpose(0, 2, 1, 3).reshape(n, w)


def _unpermute_rows(a, S):
    n, w = a.shape
    return a.reshape(n // S, S // 8, 8, w).transpose(0, 2, 1, 3).reshape(n, w)


def _block_diag_in(bb):
    eye = jnp.eye(G, dtype=bb.dtype)
    return jnp.einsum("gph,gk->ghkp", bb, eye).reshape(G * H, G * P)


def _block_diag_out(cc):
    eye = jnp.eye(G, dtype=cc.dtype)
    return jnp.einsum("ghp,gk->gpkh", cc, eye).reshape(G * P, G * H)


def _local_step(x, c, positions, target, wf):
    nb, S, _ = x.shape
    n = nb * S
    tm = min(256, S)
    tt = min(256, S)
    tq = min(512, S // 2)
    kw = _prep_weights(wf)
    row = lambda a: a.reshape(1, -1).astype(F32)

    c8 = jnp.pad(c, ((0, 8 - nb), (0, 0)))
    cond, mod, fmod = _mod_fwd(c8, kw["ada_w"], row(wf["ada_b"]), kw["final_ada_w"], row(wf["final_ada_b"]))
    modp = jnp.concatenate([mod[:nb].reshape(nb, 6, D), fmod[:nb].reshape(nb, 2, D)], axis=1)

    xf = x.reshape(n, D)
    tf = target.reshape(n, D)
    g1, g2, gf = row(wf["norm1_g"]), row(wf["norm2_g"]), row(wf["final_norm_g"])
    h1, proj = _f1_fwd(xf, modp, g1, kw["w_in"], S, tm)

    col = lambda a: a.reshape(NST, 1)
    lam_re, lam_im = col(wf["ssm_lambda_re"]), col(wf["ssm_lambda_im"])
    logdt = jnp.repeat(wf["ssm_log_dt"].reshape(G, 1), P, axis=1).reshape(NST, 1)
    b_re, b_im = wf["ssm_b_re"].reshape(NST, H), wf["ssm_b_im"].reshape(NST, H)
    lbr, lbi, bbr, bbi = _ssm_param_fwd(lam_re, lam_im, logdt, b_re, b_im)
    lre8 = jnp.broadcast_to(lbr.reshape(1, NST), (8, NST))
    lim8 = jnp.broadcast_to(lbi.reshape(1, NST), (8, NST))
    bm = jnp.concatenate([_block_diag_in(bbr.reshape(G, P, H)), _block_diag_in(bbi.reshape(G, P, H))],
                         axis=1).astype(BF16)
    cm = jnp.concatenate([_block_diag_out(wf["ssm_c_re"]), -_block_diag_out(wf["ssm_c_im"])], axis=0).astype(BF16)
    dvec = row(wf["ssm_d"])
    u_p = _permute_rows(proj[:, :D_SSM], S)
    fcr, fci = _ssm_local(u_p, bm, lre8, lim8, S, tt)
    st, ypre, z, gact, yssm_p = _ssm_fwd(u_p, fcr, fci, bm, cm, dvec, kw["w_glu"], lre8, lim8, S, tt)
    yssm = _unpermute_rows(yssm_p, S)

    rc, rs1, rs2 = _rope_tables(positions.reshape(n))
    gq, gkv = row(wf["q_norm_g"]), row(wf["kv_norm_g"])
    q, k, v, qn, kvn = _mla_fwd(proj, rc, rs1, rs2, gq, gkv, kw["w_uq"], kw["w_ukv"], tm)
    oattn, lse = _attn_fwd(q, k, v, S, tq)

    gs = row(wf["ssm_out_g"])
    ga = jnp.pad(wf["attn_out_g"].reshape(NH, V_HEAD), ((0, 0), (0, HP - V_HEAD))).reshape(1, NH * HP)
    yn, o, x1, h2 = _p1_fwd(yssm, oattn, xf, modp, gs, ga, kw["w_out"], g2, S, tm)
    dx1, r, da, dff, accs2, accg2 = _p2(x1, h2, tf, modp, g2, gf, kw["w_ff1"], kw["w_ff2"], S, tm)
    loss = jnp.sum(accg2[2])
    do, dyssm, doattn, accs3, accg3 = _p3_bwd(dx1, o, yssm, oattn, modp, gs, ga, kw["w_out"], S, tm)

    dq = _attn_bwd_dq(q, k, v, oattn, doattn, lse, S, tq)
    dk, dv = _attn_bwd_dkv(q, k, v, oattn, doattn, lse, S, tq)
    dmla, dqb, dkvb, accm = _mla_bwd(dq, dk, dv, proj, rc, rs1, rs2, gq, gkv, kw["w_uq"], kw["w_ukv"], tm)

    dys_p = _permute_rows(dyssm, S)
    dy, dz, air, aii = _ssm_bwd_a(dys_p, z, ypre, kw["w_glu"], cm, lre8, lim8, S, tt)
    du_p, dcm, dbm, dd, dlr, dli = _ssm_bwd_b(dy, u_p, st, fcr, fci, air, aii, bm, cm, dvec, lre8, lim8, S, tt)
    du = _unpermute_rows(du_p, S)
    dcm = dcm.reshape(2, 4, 8, P, 8, H)
    dc_re = jnp.einsum("qgpgh->qghp", dcm[0]).reshape(G, H, P)
    dc_im = -jnp.einsum("qgpgh->qghp", dcm[1]).reshape(G, H, P)
    dbm = dbm.reshape(8, H, 2, 4, 8, P)
    dbb_re = jnp.einsum("ghqgp->qgph", dbm[:, :, 0]).reshape(NST, H)
    dbb_im = jnp.einsum("ghqgp->qgph", dbm[:, :, 1]).reshape(NST, H)
    gb_re, gb_im, glr, gli, gdt = _ssm_param_bwd(lam_re, lam_im, logdt, b_re, b_im, dlr.reshape(NST, 1),
                                                 dli.reshape(NST, 1), dbb_re, dbb_im)
    glogdt = _rowsum(gdt.reshape(G, P))

    dx, dproj, accs1, accg1 = _f1_bwd(du, dmla, dx1, xf, modp, g1, kw["w_in"], S, tm)

    grads = {}
    grads["w_in"] = _wgrad(h1, dproj, "wgrad_in")[:, :IN_COLS]
    grads["w_glu"] = _wgrad(gact, dz, "wgrad_glu")
    grads["w_uq"] = _unpad_heads_cols(_wgrad(qn, dqb, "wgrad_uq"), QK_NOPE + QK_ROPE).reshape(Q_LORA, -1)
    gkvw = _wgrad(kvn, dkvb, "wgrad_ukv")
    grads["w_ukv"] = jnp.concatenate([_unpad_heads_cols(gkvw[:, :NH * HP], QK_NOPE),
                                      _unpad_heads_cols(gkvw[:, NH * HP:], V_HEAD)], axis=2).reshape(KV_LORA, -1)
    gwo = _wgrad(yn, do, "wgrad_out")
    grads["w_out"] = jnp.concatenate([gwo[:D_SSM], gwo[D_SSM:].reshape(NH, HP, D)[:, :V_HEAD].reshape(D_ATTN, D)], axis=0)
    grads["w_ff1"] = _wgrad(h2, da, "wgrad_ff1")
    grads["w_ff2"] = _wgrad(r, dff, "wgrad_ff2")

    dmod = (accs1 + accs2 + accs3)
    dmod8 = jnp.pad(dmod.reshape(nb, 8 * D), ((0, 8 - nb), (0, 0)))
    cond_t = cond.T
    grads["ada_w"], grads["ada_b"] = _mod_bwd(cond_t, dmod8[:, :6 * D], "mod_bwd")
    grads["final_ada_w"], grads["final_ada_b"] = _mod_bwd(cond_t, dmod8[:, 6 * D:], "fmod_bwd")

    grads["norm1_g"] = accg1[0]
    grads["norm2_g"] = accg2[0]
    grads["final_norm_g"] = accg2[1]
    grads["ssm_out_g"] = accg3[0, :D_SSM]
    grads["attn_out_g"] = accg3[1].reshape(NH, HP)[:, :V_HEAD].reshape(D_ATTN)
    grads["q_norm_g"] = accm[0, :Q_LORA]
    grads["kv_norm_g"] = accm[1, :KV_LORA]
    grads["ssm_lambda_re"] = glr
    grads["ssm_lambda_im"] = gli
    grads["ssm_b_re"] = gb_re
    grads["ssm_b_im"] = gb_im
    grads["ssm_c_re"] = dc_re
    grads["ssm_c_im"] = dc_im
    grads["ssm_d"] = dd
    grads["ssm_log_dt"] = glogdt
    return loss, dx.reshape(nb, S, D), grads


def _to_shards(name, full):
    r, cdim = full.shape
    if name in ROW_SHARDED:
        return full.reshape(4, (r // 4) * cdim)
    return full.reshape(r, 4, cdim // 4).transpose(1, 0, 2).reshape(4, r * (cdim // 4))


def _from_shards(name, sh, shape2d):
    r, cdim = shape2d
    if name in ROW_SHARDED:
        return sh.reshape(r, cdim)
    return sh.reshape(4, r, cdim // 4).transpose(1, 0, 2).reshape(r, cdim)


def kernel(x, c, positions, ada_w, ada_b, norm1_g, w_in, ssm_lambda_re, ssm_lambda_im, ssm_b_re, ssm_b_im, ssm_c_re, ssm_c_im, ssm_d, ssm_log_dt, w_glu, q_norm_g, w_uq, kv_norm_g, w_ukv, ssm_out_g, attn_out_g, w_out, norm2_g, w_ff1, w_ff2, final_ada_w, final_ada_b, final_norm_g, loss_target, m_ada_w, m_ada_b, m_norm1_g, m_w_in, m_ssm_lambda_re, m_ssm_lambda_im, m_ssm_b_re, m_ssm_b_im, m_ssm_c_re, m_ssm_c_im, m_ssm_d, m_ssm_log_dt, m_w_glu, m_q_norm_g, m_w_uq, m_kv_norm_g, m_w_ukv, m_ssm_out_g, m_attn_out_g, m_w_out, m_norm2_g, m_w_ff1, m_w_ff2, m_final_ada_w, m_final_ada_b, m_final_norm_g, v_ada_w, v_ada_b, v_norm1_g, v_w_in, v_ssm_lambda_re, v_ssm_lambda_im, v_ssm_b_re, v_ssm_b_im, v_ssm_c_re, v_ssm_c_im, v_ssm_d, v_ssm_log_dt, v_w_glu, v_q_norm_g, v_w_uq, v_kv_norm_g, v_w_ukv, v_ssm_out_g, v_attn_out_g, v_w_out, v_norm2_g, v_w_ff1, v_w_ff2, v_final_ada_w, v_final_ada_b, v_final_norm_g):
    args = dict(locals())
    names = list(inspect.signature(kernel).parameters)
    wnames = names[3:names.index("loss_target")]
    small = [nm for nm in wnames if nm not in BIG]
    w = {nm: args[nm] for nm in wnames}
    m = {nm: args["m_" + nm] for nm in wnames}
    v = {nm: args["v_" + nm] for nm in wnames}

    def shard2d(a):
        return a.reshape(-1, a.shape[-1])

    big_sizes = [shard2d(w[nm]).size for nm in BIG]
    pw = jnp.concatenate([shard2d(w[nm]).astype(BF16).reshape(-1) for nm in BIG])
    gathered = _gather_chips(pw.reshape(-1, PACK_ROW)).reshape(4, -1)
    wf = {}
    off = 0
    for nm, sz in zip(BIG, big_sizes):
        r, cs = shard2d(w[nm]).shape
        full_shape = (4 * r, cs) if nm in ROW_SHARDED else (r, 4 * cs)
        wf[nm] = _from_shards(nm, gathered[:, off:off + sz], full_shape)
        off += sz
    for nm in small:
        wf[nm] = w[nm][0] if nm not in ("final_ada_b", "final_norm_g") else w[nm]

    loss, grad_x, grads = _local_step(x, c, positions, loss_target, wf)
    loss = lax.psum(loss, ("x", "y", "c"))

    pieces = [_to_shards(nm, grads[nm]) for nm in BIG]
    pieces += [jnp.broadcast_to(grads[nm].reshape(1, -1), (4, grads[nm].size)) for nm in small]
    used = sum(p.shape[1] for p in pieces)
    total = -(-used // PACK_ALIGN) * PACK_ALIGN
    pieces.append(jnp.zeros((4, total - used), F32))
    gpk = jnp.concatenate(pieces, axis=1)
    rows_h = total // PACK_ROW // 2
    g4 = gpk.reshape(4, 2, rows_h, PACK_ROW)
    cidx = lax.axis_index("c").astype(jnp.int32).reshape(1)
    chip_sum = _add_half(g4, _swap_halves(g4), cidx)
    reduced = _add_chips(_scatter_chips(chip_sum))
    gfull = _join_halves(reduced).reshape(2 * rows_h, PACK_ROW)

    def pack_local(d):
        parts = [shard2d(d[nm]).reshape(-1) for nm in BIG] + [d[nm].reshape(-1) for nm in small]
        parts.append(jnp.zeros((total - used,), F32))
        return jnp.concatenate(parts).reshape(2 * rows_h, PACK_ROW)

    delta, new_m, new_v = _adamw(pack_local(w), gfull, pack_local(m), pack_local(v))

    def unpack(p):
        flat = p.reshape(-1)
        out, o2 = {}, 0
        for nm in BIG + small:
            out[nm] = flat[o2:o2 + w[nm].size].reshape(w[nm].shape)
            o2 += w[nm].size
        return out

    outs = [unpack(gfull), unpack(delta), unpack(new_m), unpack(new_v)]
    return (loss, grad_x, *[d[nm] for d in outs for nm in wnames])
```

```python
import functools
import inspect
import math

import jax
import jax.numpy as jnp
from jax import lax
from jax.experimental import pallas as pl
from jax.experimental.pallas import tpu as pltpu

F32 = jnp.float32
BF16 = jnp.bfloat16

D = 1024
D_SSM = 512
G = 32
H = 16
P = 64
NST = G * P
D_ATTN = 512
NH = 8
QK_NOPE = 64
QK_ROPE = 32
V_HEAD = 64
HP = 128
Q_LORA = 384
KV_LORA = 256
IN_COLS = D_SSM + Q_LORA + KV_LORA + QK_ROPE
IN_PAD = 1280
D_FF = 4096
ROPE_BASE = 10000.0
EPS = 1e-6
ADAM_LR = 0.001
ADAM_B1 = 0.9
ADAM_B2 = 0.999
ADAM_EPS = 1e-08
ADAM_WD = 0.01
ADAM_STEP = 10
NEG = -1e30
VMEM_LIMIT = 60 << 20

MESH = pl.DeviceIdType.MESH
_VM = pl.BlockSpec(memory_space=pltpu.VMEM)
_ANY = pl.BlockSpec(memory_space=pl.ANY)

GATHERED = ["w_in", "w_glu", "w_uq", "w_ukv", "w_out", "w_ff1", "w_ff2"]
TP = ["ada_w", "final_ada_w"]
ROW_SHARDED = ("w_out", "w_ff2")


def _cp(sem=None, vmem=VMEM_LIMIT):
    kw = dict(vmem_limit_bytes=vmem)
    if sem is not None:
        kw["dimension_semantics"] = sem
    return pltpu.CompilerParams(**kw)


def _dot(a, b):
    return jnp.dot(a, b, preferred_element_type=F32)


def _dot_nt(a, b):
    return lax.dot_general(a, b, (((1,), (1,)), ((), ())), preferred_element_type=F32)


def _dot_tn(a, b):
    return lax.dot_general(a, b, (((0,), (0,)), ((), ())), preferred_element_type=F32)


def _rms(x, n):
    r = lax.rsqrt(jnp.sum(x * x, axis=-1, keepdims=True) * (1.0 / n) + EPS)
    return x * r, r


def _rms_bwd(dyg, xhat, r, n):
    return r * (dyg - xhat * (jnp.sum(dyg * xhat, axis=-1, keepdims=True) * (1.0 / n)))


def _sigmoid(x):
    return 1.0 / (1.0 + jnp.exp(-x))


_GK = math.sqrt(2.0 / math.pi)
_GC = 0.044715


def _gelu(y):
    t = jnp.tanh(_GK * (y + _GC * y * y * y))
    return 0.5 * y * (1.0 + t)


def _gelu_grad(y):
    t = jnp.tanh(_GK * (y + _GC * y * y * y))
    return 0.5 * (1.0 + t) + 0.5 * y * (1.0 - t * t) * _GK * (1.0 + 3.0 * _GC * y * y)


def _colsum(x):
    return jnp.sum(x, axis=0, keepdims=True)


def _roll(x, s):
    return pltpu.roll(x, s % x.shape[-1], x.ndim - 1)


def _mod_fwd(c_all, ada_w_s, ada_b_s, fada_w_s, fada_b_s):
    nseq = c_all.shape[0]
    na, nf = ada_w_s.shape[1], fada_w_s.shape[1]

    def body(c_ref, w_ref, b_ref, fw_ref, fb_ref, cond_ref, mod_ref):
        cv = c_ref[...]
        cond = cv * _sigmoid(cv)
        cond_ref[...] = cond
        cb = cond.astype(BF16)
        mod_ref[:, 0:na] = _dot(cb, w_ref[...].astype(BF16)) + b_ref[...]
        mod_ref[:, na:na + nf] = _dot(cb, fw_ref[...].astype(BF16)) + fb_ref[...]

    return pl.pallas_call(
        body, name="mod_fwd",
        out_shape=[jax.ShapeDtypeStruct((nseq, D), F32), jax.ShapeDtypeStruct((nseq, na + nf), F32)],
        in_specs=[_VM] * 5, out_specs=[_VM] * 2, compiler_params=_cp(),
    )(c_all, ada_w_s, ada_b_s, fada_w_s, fada_b_s)


def _mod_bwd(cond_t, dsl, dall):
    nseq, n = dsl.shape
    bc = 512

    def body(ct_ref, dm_ref, da_ref, gw_ref, gb_ref):
        ct = ct_ref[...]
        dm = dm_ref[...]
        acc = ct[:, 0:1] * dm[0:1, :]
        for b in range(1, nseq):
            acc = acc + ct[:, b:b + 1] * dm[b:b + 1, :]
        gw_ref[...] = acc

        @pl.when(pl.program_id(0) == 0)
        def _():
            gb_ref[...] = _colsum(da_ref[...])

    return pl.pallas_call(
        body, name="mod_bwd", grid=(n // bc,),
        out_shape=[jax.ShapeDtypeStruct((D, n), F32), jax.ShapeDtypeStruct((1, dall.shape[1]), F32)],
        in_specs=[_VM, pl.BlockSpec((nseq, bc), lambda i: (0, i)), _VM],
        out_specs=[pl.BlockSpec((D, bc), lambda i: (0, i)), pl.BlockSpec((1, dall.shape[1]), lambda i: (0, 0))],
        compiler_params=_cp(("arbitrary",)),
    )(cond_t, dsl, dall)


def _f1_fwd(x, modp, g1, w_in, S, tm):
    n = x.shape[0]
    tps = S // tm

    def body(x_ref, mod_ref, g_ref, w_ref, h_ref, proj_ref):
        xhat, _ = _rms(x_ref[...], D)
        h = (xhat * g_ref[...]) * (1.0 + mod_ref[0, 1:2, :]) + mod_ref[0, 0:1, :]
        hb = h.astype(BF16)
        h_ref[...] = hb
        proj_ref[...] = _dot(hb, w_ref[...])

    return pl.pallas_call(
        body, name="f1_fwd", grid=(n // tm,),
        out_shape=[jax.ShapeDtypeStruct((n, D), BF16), jax.ShapeDtypeStruct((n, IN_PAD), F32)],
        in_specs=[pl.BlockSpec((tm, D), lambda i: (i, 0)),
                  pl.BlockSpec((1, 8, D), lambda i: (i // tps, 0, 0)), _VM, _VM],
        out_specs=[pl.BlockSpec((tm, D), lambda i: (i, 0)), pl.BlockSpec((tm, IN_PAD), lambda i: (i, 0))],
        compiler_params=_cp(("parallel",)),
    )(x, modp, g1, w_in)


def _f1_bwd(du, dmla, dx1, x, modp, g1, w_in, S, tm):
    n = x.shape[0]
    tps = S // tm
    nb = n // S

    def body(du_ref, dm_ref, dx1_ref, x_ref, mod_ref, g_ref, w_ref, dx_ref, dproj_ref, accs_ref, accg_ref):
        i = pl.program_id(0)
        dproj = jnp.concatenate([du_ref[...], dm_ref[...]], axis=1).astype(BF16)
        dproj_ref[...] = dproj
        dh = _dot_nt(dproj, w_ref[...])
        xhat, r = _rms(x_ref[...], D)
        g = g_ref[...]
        dn = dh * (1.0 + mod_ref[0, 1:2, :])
        dx_ref[...] = dx1_ref[...] + _rms_bwd(dn * g, xhat, r, D)

        @pl.when(i % tps == 0)
        def _():
            accs_ref[...] = jnp.zeros_like(accs_ref)

        @pl.when(i == 0)
        def _():
            accg_ref[...] = jnp.zeros_like(accg_ref)

        accs_ref[0, 0:1, :] += _colsum(dh)
        accs_ref[0, 1:2, :] += _colsum(dh * (xhat * g))
        accg_ref[0:1, :] += _colsum(dn * xhat)

    return pl.pallas_call(
        body, name="f1_bwd", grid=(n // tm,),
        out_shape=[jax.ShapeDtypeStruct((n, D), F32), jax.ShapeDtypeStruct((n, IN_PAD), BF16),
                   jax.ShapeDtypeStruct((nb, 8, D), F32), jax.ShapeDtypeStruct((8, D), F32)],
        in_specs=[pl.BlockSpec((tm, D_SSM), lambda i: (i, 0)), pl.BlockSpec((tm, IN_PAD - D_SSM), lambda i: (i, 0)),
                  pl.BlockSpec((tm, D), lambda i: (i, 0)), pl.BlockSpec((tm, D), lambda i: (i, 0)),
                  pl.BlockSpec((1, 8, D), lambda i: (i // tps, 0, 0)), _VM, _VM],
        out_specs=[pl.BlockSpec((tm, D), lambda i: (i, 0)), pl.BlockSpec((tm, IN_PAD), lambda i: (i, 0)),
                   pl.BlockSpec((1, 8, D), lambda i: (i // tps, 0, 0)), pl.BlockSpec((8, D), lambda i: (0, 0))],
        compiler_params=_cp(("arbitrary",)),
    )(du, dmla, dx1, x, modp, g1, w_in)


def _ssm_param_fwd(lam_re, lam_im, logdt, b_re, b_im):
    def body(lr_ref, li_ref, ld_ref, br_ref, bi_ref, lbr_ref, lbi_ref, bbr_ref, bbi_ref):
        lr, li = lr_ref[...], li_ref[...]
        dt = jnp.exp(ld_ref[...])
        er = jnp.exp(lr * dt)
        lbr = er * jnp.cos(li * dt)
        lbi = er * jnp.sin(li * dt)
        den = 1.0 / (lr * lr + li * li)
        cr = ((lbr - 1.0) * lr + lbi * li) * den
        ci = (lbi * lr - (lbr - 1.0) * li) * den
        lbr_ref[...] = lbr
        lbi_ref[...] = lbi
        bbr_ref[...] = cr * br_ref[...] - ci * bi_ref[...]
        bbi_ref[...] = cr * bi_ref[...] + ci * br_ref[...]

    return pl.pallas_call(
        body, name="ssm_param_fwd",
        out_shape=[jax.ShapeDtypeStruct((NST, 1), F32)] * 2 + [jax.ShapeDtypeStruct((NST, H), F32)] * 2,
        in_specs=[_VM] * 5, out_specs=[_VM] * 4, compiler_params=_cp(),
    )(lam_re, lam_im, logdt, b_re, b_im)


def _ssm_param_bwd(lam_re, lam_im, logdt, b_re, b_im, dlb_re, dlb_im, dbb_re, dbb_im):
    def body(lr_ref, li_ref, ld_ref, br_ref, bi_ref, dlr_ref, dli_ref, dbr_ref, dbi_ref,
             gbr_ref, gbi_ref, glr_ref, gli_ref, gdt_ref):
        lr, li = lr_ref[...], li_ref[...]
        dt = jnp.exp(ld_ref[...])
        er = jnp.exp(lr * dt)
        lbr = er * jnp.cos(li * dt)
        lbi = er * jnp.sin(li * dt)
        den = 1.0 / (lr * lr + li * li)
        nr, ni = lbr - 1.0, lbi
        cr = (nr * lr + ni * li) * den
        ci = (ni * lr - nr * li) * den
        br, bi = br_ref[...], bi_ref[...]
        dbr, dbi = dbr_ref[...], dbi_ref[...]
        gbr_ref[...] = cr * dbr + ci * dbi
        gbi_ref[...] = cr * dbi - ci * dbr
        gcr = jnp.sum(dbr * br + dbi * bi, axis=1, keepdims=True)
        gci = jnp.sum(dbi * br - dbr * bi, axis=1, keepdims=True)
        ilr, ili = lr * den, -li * den
        glbr = dlr_ref[...] + (gcr * ilr + gci * ili)
        glbi = dli_ref[...] + (gci * ilr - gcr * ili)
        qr = -(cr * ilr - ci * ili)
        qi = -(cr * ili + ci * ilr)
        glr = gcr * qr + gci * qi
        gli = gci * qr - gcr * qi
        glr = glr + dt * (glbr * lbr + glbi * lbi)
        gli = gli + dt * (glbi * lbr - glbr * lbi)
        wr = lr * lbr - li * lbi
        wi = lr * lbi + li * lbr
        glr_ref[...] = glr
        gli_ref[...] = gli
        gdt_ref[...] = (glbr * wr + glbi * wi) * dt

    return pl.pallas_call(
        body, name="ssm_param_bwd",
        out_shape=[jax.ShapeDtypeStruct((NST, H), F32)] * 2 + [jax.ShapeDtypeStruct((NST, 1), F32)] * 3,
        in_specs=[_VM] * 9, out_specs=[_VM] * 5, compiler_params=_cp(),
    )(lam_re, lam_im, logdt, b_re, b_im, dlb_re, dlb_im, dbb_re, dbb_im)


def _rowsum(a):
    def body(a_ref, o_ref):
        o_ref[...] = jnp.sum(a_ref[...], axis=1, keepdims=True)

    return pl.pallas_call(
        body, name="rowsum", out_shape=jax.ShapeDtypeStruct((a.shape[0], 1), F32),
        in_specs=[_VM], out_specs=_VM, compiler_params=_cp(),
    )(a)


def _pow2k(pr, pi, nsq):
    for _ in range(nsq):
        pr, pi = pr * pr - pi * pi, 2.0 * pr * pi
    return pr, pi


def _ssm_local(u_p, bm, lre8, lim8, S, tt):
    n = u_p.shape[0]
    nb, nt = n // S, S // tt
    nsq = int(round(math.log2(S // 8)))
    assert 2 ** nsq == S // 8

    def body(u_ref, bm_ref, lre_ref, lim_ref, cre_ref, cim_ref, sre, sim, bu):
        j = pl.program_id(1)

        @pl.when(j == 0)
        def _():
            sre[...] = jnp.zeros_like(sre)
            sim[...] = jnp.zeros_like(sim)

        bu[...] = _dot(u_ref[...].astype(BF16), bm_ref[...])
        lre, lim = lre_ref[...], lim_ref[...]

        def step(i, c):
            sr, si = c
            off = pl.multiple_of(i * 8, 8)
            br = bu[pl.ds(off, 8), 0:NST]
            bi = bu[pl.ds(off, 8), NST:2 * NST]
            return lre * sr - lim * si + br, lre * si + lim * sr + bi

        sr, si = lax.fori_loop(0, tt // 8, step, (sre[...], sim[...]))
        sre[...] = sr
        sim[...] = si

        @pl.when(j == nt - 1)
        def _():
            pr, pi = _pow2k(lre[0:1], lim[0:1], nsq)
            cr = jnp.zeros((1, NST), F32)
            ci = jnp.zeros((1, NST), F32)
            cre_ref[0:1, :] = cr
            cim_ref[0:1, :] = ci
            for k in range(1, 8):
                cr, ci = sr[k - 1:k] + pr * cr - pi * ci, si[k - 1:k] + pr * ci + pi * cr
                cre_ref[k:k + 1, :] = cr
                cim_ref[k:k + 1, :] = ci

    return pl.pallas_call(
        body, name="ssm_local", grid=(nb, nt),
        out_shape=[jax.ShapeDtypeStruct((nb * 8, NST), F32)] * 2,
        in_specs=[pl.BlockSpec((tt, D_SSM), lambda b, j: (b * nt + j, 0)), _VM, _VM, _VM],
        out_specs=[pl.BlockSpec((8, NST), lambda b, j: (b, 0))] * 2,
        scratch_shapes=[pltpu.VMEM((8, NST), F32), pltpu.VMEM((8, NST), F32), pltpu.VMEM((tt, 2 * NST), F32)],
        compiler_params=_cp(("arbitrary", "arbitrary")),
    )(u_p, bm, lre8, lim8)


def _ssm_fwd(u_p, cre, cim, bm, cm, dvec, w_glu, lre8, lim8, S, tt):
    n = u_p.shape[0]
    nb, nt = n // S, S // tt

    def body(u_ref, cre_ref, cim_ref, bm_ref, cm_ref, d_ref, wg_ref, lre_ref, lim_ref,
             st_ref, ypre_ref, z_ref, gact_ref, yssm_ref, sre, sim, bu):
        j = pl.program_id(1)

        @pl.when(j == 0)
        def _():
            sre[...] = cre_ref[...]
            sim[...] = cim_ref[...]

        u = u_ref[...]
        bu[...] = _dot(u.astype(BF16), bm_ref[...])
        lre, lim = lre_ref[...], lim_ref[...]

        def step(i, c):
            sr, si = c
            off = pl.multiple_of(i * 8, 8)
            nr = lre * sr - lim * si + bu[pl.ds(off, 8), 0:NST]
            ni = lre * si + lim * sr + bu[pl.ds(off, 8), NST:2 * NST]
            st_ref[pl.ds(off, 8), 0:NST] = nr
            st_ref[pl.ds(off, 8), NST:2 * NST] = ni
            return nr, ni

        sr, si = lax.fori_loop(0, tt // 8, step, (sre[...], sim[...]))
        sre[...] = sr
        sim[...] = si
        y = _dot(st_ref[...].astype(BF16), cm_ref[...]) + d_ref[...] * u
        ypre_ref[...] = y
        gb = _gelu(y).astype(BF16)
        gact_ref[...] = gb
        z = _dot(gb, wg_ref[...])
        z_ref[...] = z
        yssm_ref[...] = z[:, 0:D_SSM] * _sigmoid(z[:, D_SSM:2 * D_SSM])

    row = lambda w: pl.BlockSpec((tt, w), lambda b, j: (b * nt + j, 0))
    return pl.pallas_call(
        body, name="ssm_fwd", grid=(nb, nt),
        out_shape=[jax.ShapeDtypeStruct((n, 2 * NST), F32), jax.ShapeDtypeStruct((n, D_SSM), F32),
                   jax.ShapeDtypeStruct((n, 2 * D_SSM), F32), jax.ShapeDtypeStruct((n, D_SSM), BF16),
                   jax.ShapeDtypeStruct((n, D_SSM), F32)],
        in_specs=[row(D_SSM), pl.BlockSpec((8, NST), lambda b, j: (b, 0)), pl.BlockSpec((8, NST), lambda b, j: (b, 0)),
                  _VM, _VM, _VM, _VM, _VM, _VM],
        out_specs=[row(2 * NST), row(D_SSM), row(2 * D_SSM), row(D_SSM), row(D_SSM)],
        scratch_shapes=[pltpu.VMEM((8, NST), F32), pltpu.VMEM((8, NST), F32), pltpu.VMEM((tt, 2 * NST), F32)],
        compiler_params=_cp(("arbitrary", "arbitrary")),
    )(u_p, cre, cim, bm, cm, dvec, w_glu, lre8, lim8)


def _ssm_bwd_a(dys_p, z, ypre, w_glu, cm, lre8, lim8, S, tt):
    n = z.shape[0]
    nb, nt = n // S, S // tt
    nsq = int(round(math.log2(S // 8)))
    ng = tt // 8

    def body(dys_ref, z_ref, y_ref, wg_ref, cm_ref, lre_ref, lim_ref, dy_ref, dz_ref, are_ref, aim_ref, sre, sim, gb):
        j = pl.program_id(1)

        @pl.when(j == 0)
        def _():
            sre[...] = jnp.zeros_like(sre)
            sim[...] = jnp.zeros_like(sim)

        z = z_ref[...]
        z1, z2 = z[:, 0:D_SSM], z[:, D_SSM:2 * D_SSM]
        sg = _sigmoid(z2)
        dys = dys_ref[...]
        dz = jnp.concatenate([dys * sg, dys * z1 * sg * (1.0 - sg)], axis=1).astype(BF16)
        dz_ref[...] = dz
        dy = _dot_nt(dz, wg_ref[...]) * _gelu_grad(y_ref[...])
        dy_ref[...] = dy
        gb[...] = _dot_nt(dy.astype(BF16), cm_ref[...])
        lre, lim = lre_ref[...], lim_ref[...]

        def step(i, c):
            ar, ai = c
            off = pl.multiple_of((ng - 1 - i) * 8, 8)
            gr = gb[pl.ds(off, 8), 0:NST]
            gi = gb[pl.ds(off, 8), NST:2 * NST]
            return lre * ar + lim * ai + gr, lre * ai - lim * ar + gi

        ar, ai = lax.fori_loop(0, ng, step, (sre[...], sim[...]))
        sre[...] = ar
        sim[...] = ai

        @pl.when(j == nt - 1)
        def _():
            pr, pi = _pow2k(lre[0:1], -lim[0:1], nsq)
            cr = jnp.zeros((1, NST), F32)
            ci = jnp.zeros((1, NST), F32)
            are_ref[7:8, :] = cr
            aim_ref[7:8, :] = ci
            for k in range(6, -1, -1):
                cr, ci = ar[k + 1:k + 2] + pr * cr - pi * ci, ai[k + 1:k + 2] + pr * ci + pi * cr
                are_ref[k:k + 1, :] = cr
                aim_ref[k:k + 1, :] = ci

    row = lambda w: pl.BlockSpec((tt, w), lambda b, j: (b * nt + nt - 1 - j, 0))
    return pl.pallas_call(
        body, name="ssm_bwd_a", grid=(nb, nt),
        out_shape=[jax.ShapeDtypeStruct((n, D_SSM), F32), jax.ShapeDtypeStruct((n, 2 * D_SSM), BF16),
                   jax.ShapeDtypeStruct((nb * 8, NST), F32), jax.ShapeDtypeStruct((nb * 8, NST), F32)],
        in_specs=[row(D_SSM), row(2 * D_SSM), row(D_SSM), _VM, _VM, _VM, _VM],
        out_specs=[row(D_SSM), row(2 * D_SSM), pl.BlockSpec((8, NST), lambda b, j: (b, 0)),
                   pl.BlockSpec((8, NST), lambda b, j: (b, 0))],
        scratch_shapes=[pltpu.VMEM((8, NST), F32), pltpu.VMEM((8, NST), F32), pltpu.VMEM((tt, 2 * NST), F32)],
        compiler_params=_cp(("arbitrary", "arbitrary")),
    )(dys_p, z, ypre, w_glu, cm, lre8, lim8)


def _ssm_bwd_b(dy, u_p, st, fcr, fci, air, aii, bm, cm, dvec, lre8, lim8, S, tt):
    n = u_p.shape[0]
    nb, nt = n // S, S // tt
    ng = tt // 8
    QB = D_SSM // 4

    def body(dy_ref, u_ref, st_ref, stp_ref, fcr_ref, fci_ref, air_ref, aii_ref, bm_ref, cm_ref, d_ref, lre_ref, lim_ref,
             du_ref, dcm_ref, dbm_ref, dd_ref, dlr_ref, dli_ref, are, aim, accr, acci, sp, ab):
        b = pl.program_id(0)
        j = pl.program_id(1)
        jt = nt - 1 - j

        @pl.when((b == 0) & (j == 0))
        def _():
            dcm_ref[...] = jnp.zeros_like(dcm_ref)
            dbm_ref[...] = jnp.zeros_like(dbm_ref)
            dd_ref[...] = jnp.zeros_like(dd_ref)
            accr[...] = jnp.zeros_like(accr)
            acci[...] = jnp.zeros_like(acci)

        @pl.when(j == 0)
        def _():
            are[...] = air_ref[...]
            aim[...] = aii_ref[...]

        sp[8:tt + 8, :] = st_ref[...]

        @pl.when(jt == 0)
        def _():
            sp[0:8, 0:NST] = fcr_ref[...]
            sp[0:8, NST:2 * NST] = fci_ref[...]

        @pl.when(jt != 0)
        def _():
            sp[0:8, :] = stp_ref[...]

        dy = dy_ref[...]
        u = u_ref[...]
        dyb = dy.astype(BF16)
        ab[...] = _dot_nt(dyb, cm_ref[...])
        lre, lim = lre_ref[...], lim_ref[...]

        def step(i, c):
            ar, ai = c
            off = pl.multiple_of((ng - 1 - i) * 8, 8)
            nr = lre * ar + lim * ai + ab[pl.ds(off, 8), 0:NST]
            ni = lre * ai - lim * ar + ab[pl.ds(off, 8), NST:2 * NST]
            ab[pl.ds(off, 8), 0:NST] = nr
            ab[pl.ds(off, 8), NST:2 * NST] = ni
            pr = sp[pl.ds(off, 8), 0:NST]
            pi = sp[pl.ds(off, 8), NST:2 * NST]
            accr[...] += nr * pr + ni * pi
            acci[...] += ni * pr - nr * pi
            return nr, ni

        ar, ai = lax.fori_loop(0, ng, step, (are[...], aim[...]))
        are[...] = ar
        aim[...] = ai
        a_b = ab[...].astype(BF16)
        du_ref[...] = _dot_nt(a_b, bm_ref[...]) + d_ref[...] * dy
        ub = u.astype(BF16)
        for q in range(4):
            for part in range(2):
                lo = part * NST + q * 4 * QB
                s_q = sp[8:tt + 8, lo:lo + 4 * QB].astype(BF16)
                dcm_ref[lo:lo + 4 * QB, :] += _dot_tn(s_q, dyb[:, q * QB:(q + 1) * QB])
                dbm_ref[:, lo:lo + 4 * QB] += _dot_tn(ub[:, q * QB:(q + 1) * QB], a_b[:, lo:lo + 4 * QB])
        dd_ref[...] += _colsum(dy * u)

        @pl.when((b == nb - 1) & (j == nt - 1))
        def _():
            dlr_ref[...] = _colsum(accr[...])
            dli_ref[...] = _colsum(acci[...])

    row = lambda w: pl.BlockSpec((tt, w), lambda b, j: (b * nt + nt - 1 - j, 0))
    seq8 = pl.BlockSpec((8, NST), lambda b, j: (b, 0))
    prev = pl.BlockSpec((8, 2 * NST), lambda b, j: (jnp.maximum((b * nt + nt - 1 - j) * ng - 1, 0), 0))
    const = lambda shape: pl.BlockSpec(shape, lambda b, j: (0, 0))
    return pl.pallas_call(
        body, name="ssm_bwd_b", grid=(nb, nt),
        out_shape=[jax.ShapeDtypeStruct((n, D_SSM), F32), jax.ShapeDtypeStruct((2 * NST, QB), F32),
                   jax.ShapeDtypeStruct((QB, 2 * NST), F32), jax.ShapeDtypeStruct((1, D_SSM), F32),
                   jax.ShapeDtypeStruct((1, NST), F32), jax.ShapeDtypeStruct((1, NST), F32)],
        in_specs=[row(D_SSM), row(D_SSM), row(2 * NST), prev, seq8, seq8, seq8, seq8, _VM, _VM, _VM, _VM, _VM],
        out_specs=[row(D_SSM), const((2 * NST, QB)), const((QB, 2 * NST)), const((1, D_SSM)),
                   const((1, NST)), const((1, NST))],
        scratch_shapes=[pltpu.VMEM((8, NST), F32)] * 4 + [pltpu.VMEM((tt + 8, 2 * NST), F32),
                                                          pltpu.VMEM((tt, 2 * NST), F32)],
        compiler_params=_cp(("arbitrary", "arbitrary")),
    )(dy, u_p, st, st, fcr, fci, air, aii, bm, cm, dvec, lre8, lim8)


def _rope(v, c, s1, s2):
    return v * c + _roll(v, -16) * s1 + _roll(v, 16) * s2


def _rope_t(dv, c, s1, s2):
    return dv * c + _roll(dv * s1, 16) + _roll(dv * s2, -16)


def _mla_fwd(proj, rc, rs1, rs2, gq, gkv, w_uq, w_ukv, tm):
    n = proj.shape[0]

    def body(ql_ref, kvl_ref, kr_ref, c_ref, s1_ref, s2_ref, gq_ref, gkv_ref, wq_ref, wkv_ref,
             q_ref, k_ref, v_ref, qn_ref, kvn_ref):
        c, s1, s2 = c_ref[...], s1_ref[...], s2_ref[...]
        qhat, _ = _rms(ql_ref[...], Q_LORA)
        qn = (qhat * gq_ref[...]).astype(BF16)
        qn_ref[...] = qn
        q = _dot(qn, wq_ref[...])
        q_ref[...] = _rope(q, jnp.tile(c, (1, NH)), jnp.tile(s1, (1, NH)), jnp.tile(s2, (1, NH))).astype(BF16)
        khat, _ = _rms(kvl_ref[...], KV_LORA)
        kvn = (khat * gkv_ref[...]).astype(BF16)
        kvn_ref[...] = kvn
        kv = _dot(kvn, wkv_ref[...])
        kr = _rope(_roll(kr_ref[...], 64), c, s1, s2)
        k_ref[...] = (kv[:, 0:NH * HP] + jnp.tile(kr, (1, NH))).astype(BF16)
        v_ref[...] = kv[:, NH * HP:2 * NH * HP].astype(BF16)

    def wrapped(proj_ref, *rest):
        ql = proj_ref.at[:, D_SSM:D_SSM + Q_LORA]
        kvl = proj_ref.at[:, D_SSM + Q_LORA:D_SSM + Q_LORA + KV_LORA]
        kr = proj_ref.at[:, IN_PAD - HP:IN_PAD]
        body(ql, kvl, kr, *rest)

    row = lambda w: pl.BlockSpec((tm, w), lambda i: (i, 0))
    return pl.pallas_call(
        wrapped, name="mla_fwd", grid=(n // tm,),
        out_shape=[jax.ShapeDtypeStruct((n, NH * HP), BF16)] * 3 +
                  [jax.ShapeDtypeStruct((n, Q_LORA), BF16), jax.ShapeDtypeStruct((n, KV_LORA), BF16)],
        in_specs=[row(IN_PAD), row(HP), row(HP), row(HP), _VM, _VM, _VM, _VM],
        out_specs=[row(NH * HP)] * 3 + [row(Q_LORA), row(KV_LORA)],
        compiler_params=_cp(("parallel",)),
    )(proj, rc, rs1, rs2, gq, gkv, w_uq, w_ukv)


def _mla_bwd(dq, dk, dv, proj, rc, rs1, rs2, gq, gkv, w_uq, w_ukv, tm):
    n = proj.shape[0]

    def body(dq_ref, dk_ref, dv_ref, proj_ref, c_ref, s1_ref, s2_ref, gq_ref, gkv_ref, wq_ref, wkv_ref,
             dmla_ref, dqb_ref, dkvb_ref, acc_ref):
        i = pl.program_id(0)
        c, s1, s2 = c_ref[...], s1_ref[...], s2_ref[...]
        dqu = _rope_t(dq_ref[...], jnp.tile(c, (1, NH)), jnp.tile(s1, (1, NH)), jnp.tile(s2, (1, NH))).astype(BF16)
        dqb_ref[...] = dqu
        dqn = _dot_nt(dqu, wq_ref[...])
        qhat, rq = _rms(proj_ref[:, D_SSM:D_SSM + Q_LORA], Q_LORA)
        dql = _rms_bwd(dqn * gq_ref[...], qhat, rq, Q_LORA)
        dkf = dk_ref[...]
        dkv = jnp.concatenate([dkf, dv_ref[...]], axis=1).astype(BF16)
        dkvb_ref[...] = dkv
        dkvn = _dot_nt(dkv, wkv_ref[...])
        khat, rk = _rms(proj_ref[:, D_SSM + Q_LORA:D_SSM + Q_LORA + KV_LORA], KV_LORA)
        dkvl = _rms_bwd(dkvn * gkv_ref[...], khat, rk, KV_LORA)
        dkr = dkf[:, 0:HP]
        for h in range(1, NH):
            dkr = dkr + dkf[:, h * HP:(h + 1) * HP]
        lane = lax.broadcasted_iota(jnp.int32, dkr.shape, 1)
        dkr = jnp.where((lane >= QK_NOPE) & (lane < QK_NOPE + QK_ROPE), dkr, 0.0)
        dkr = _roll(_rope_t(dkr, c, s1, s2), -64)
        dmla_ref[...] = jnp.concatenate([dql, dkvl, dkr], axis=1)

        @pl.when(i == 0)
        def _():
            acc_ref[...] = jnp.zeros_like(acc_ref)

        acc_ref[0:1, 0:Q_LORA] += _colsum(dqn * qhat)
        acc_ref[1:2, 0:KV_LORA] += _colsum(dkvn * khat)

    row = lambda w: pl.BlockSpec((tm, w), lambda i: (i, 0))
    return pl.pallas_call(
        body, name="mla_bwd", grid=(n // tm,),
        out_shape=[jax.ShapeDtypeStruct((n, IN_PAD - D_SSM), F32), jax.ShapeDtypeStruct((n, NH * HP), BF16),
                   jax.ShapeDtypeStruct((n, 2 * NH * HP), BF16), jax.ShapeDtypeStruct((8, Q_LORA), F32)],
        in_specs=[row(NH * HP)] * 3 + [row(IN_PAD), row(HP), row(HP), row(HP), _VM, _VM, _VM, _VM],
        out_specs=[row(IN_PAD - D_SSM), row(NH * HP), row(2 * NH * HP), pl.BlockSpec((8, Q_LORA), lambda i: (0, 0))],
        compiler_params=_cp(("arbitrary",)),
    )(dq, dk, dv, proj, rc, rs1, rs2, gq, gkv, w_uq, w_ukv)


_SCALE = (QK_NOPE + QK_ROPE) ** -0.5


def _causal(s, row0, col0):
    rows = row0 + lax.broadcasted_iota(jnp.int32, s.shape, 0)
    cols = col0 + lax.broadcasted_iota(jnp.int32, s.shape, 1)
    return cols <= rows


def _attn_fwd(q, k, v, S, tq):
    n = q.shape[0]
    nb, nq = n // S, S // tq

    def body(q_ref, k_ref, v_ref, o_ref, lse_ref):
        qi = pl.program_id(2)
        qv = q_ref[...]

        def step(j, c):
            m, l, acc = c
            off = pl.multiple_of(j * tq, tq)
            s = _dot_nt(qv, k_ref[pl.ds(off, tq), :]) * _SCALE
            s = jnp.where(_causal(s, qi * tq, off), s, NEG)
            mn = jnp.maximum(m, jnp.max(s, axis=1, keepdims=True))
            p = jnp.exp(s - mn)
            al = jnp.exp(m - mn)
            return mn, al * l + jnp.sum(p, axis=1, keepdims=True), al * acc + _dot(p.astype(BF16), v_ref[pl.ds(off, tq), :])

        m, l, acc = lax.fori_loop(0, qi + 1, step, (jnp.full((tq, 1), NEG, F32), jnp.zeros((tq, 1), F32),
                                                    jnp.zeros((tq, HP), F32)))
        o_ref[...] = acc / l
        lse_ref[...] = jnp.broadcast_to(m + jnp.log(l), (tq, HP))

    qs = pl.BlockSpec((tq, HP), lambda b, h, i: (b * nq + i, h))
    ks = pl.BlockSpec((S, HP), lambda b, h, i: (b, h))
    return pl.pallas_call(
        body, name="attn_fwd", grid=(nb, NH, nq),
        out_shape=[jax.ShapeDtypeStruct((n, NH * HP), F32)] * 2,
        in_specs=[qs, ks, ks], out_specs=[qs, qs],
        compiler_params=_cp(("parallel", "parallel", "arbitrary")),
    )(q, k, v)


def _attn_bwd_dq(q, k, v, o, do, lse, S, tq):
    n = q.shape[0]
    nb, nq = n // S, S // tq

    def body(q_ref, k_ref, v_ref, o_ref, do_ref, lse_ref, dq_ref):
        qi = pl.program_id(2)
        qv = q_ref[...]
        dov = do_ref[...]
        dob = dov.astype(BF16)
        delta = jnp.sum(dov * o_ref[...], axis=1, keepdims=True)
        lse = lse_ref[:, 0:1]

        def step(j, acc):
            off = pl.multiple_of(j * tq, tq)
            kt = k_ref[pl.ds(off, tq), :]
            s = _dot_nt(qv, kt) * _SCALE
            p = jnp.where(_causal(s, qi * tq, off), jnp.exp(s - lse), 0.0)
            dp = _dot_nt(dob, v_ref[pl.ds(off, tq), :])
            ds = (p * (dp - delta) * _SCALE).astype(BF16)
            return acc + _dot(ds, kt)

        dq_ref[...] = lax.fori_loop(0, qi + 1, step, jnp.zeros((tq, HP), F32))

    qs = pl.BlockSpec((tq, HP), lambda b, h, i: (b * nq + i, h))
    ks = pl.BlockSpec((S, HP), lambda b, h, i: (b, h))
    return pl.pallas_call(
        body, name="attn_bwd_dq", grid=(nb, NH, nq),
        out_shape=jax.ShapeDtypeStruct((n, NH * HP), F32),
        in_specs=[qs, ks, ks, qs, qs, qs], out_specs=qs,
        compiler_params=_cp(("parallel", "parallel", "arbitrary")),
    )(q, k, v, o, do, lse)


def _attn_bwd_dkv(q, k, v, o, do, lse, S, tq):
    n = q.shape[0]
    nb, nq = n // S, S // tq

    def body(q_ref, k_ref, v_ref, o_ref, do_ref, lse_ref, dk_ref, dv_ref):
        kj = pl.program_id(2)
        kt = k_ref[...]
        vt = v_ref[...]

        def step(i, c):
            dk, dv = c
            off = pl.multiple_of(i * tq, tq)
            qv = q_ref[pl.ds(off, tq), :]
            dov = do_ref[pl.ds(off, tq), :]
            dob = dov.astype(BF16)
            delta = jnp.sum(dov * o_ref[pl.ds(off, tq), :], axis=1, keepdims=True)
            s = _dot_nt(qv, kt) * _SCALE
            p = jnp.where(_causal(s, off, kj * tq), jnp.exp(s - lse_ref[pl.ds(off, tq), 0:1]), 0.0)
            dv = dv + _dot_tn(p.astype(BF16), dob)
            dp = _dot_nt(dob, vt)
            ds = (p * (dp - delta) * _SCALE).astype(BF16)
            return dk + _dot_tn(ds, qv), dv

        dk, dv = lax.fori_loop(kj, nq, step, (jnp.zeros((tq, HP), F32), jnp.zeros((tq, HP), F32)))
        dk_ref[...] = dk
        dv_ref[...] = dv

    ts = pl.BlockSpec((tq, HP), lambda b, h, i: (b * nq + i, h))
    fs = pl.BlockSpec((S, HP), lambda b, h, i: (b, h))
    return pl.pallas_call(
        body, name="attn_bwd_dkv", grid=(nb, NH, nq),
        out_shape=[jax.ShapeDtypeStruct((n, NH * HP), F32)] * 2,
        in_specs=[fs, ts, ts, fs, fs, fs], out_specs=[ts, ts],
        compiler_params=_cp(("parallel", "parallel", "arbitrary")),
    )(q, k, v, o, do, lse)


def _p1_fwd(yssm, oattn, x, modp, gs, ga, w_out, g2, S, tm):
    n = x.shape[0]
    tps = S // tm

    def body(ys_ref, oa_ref, x_ref, mod_ref, gs_ref, ga_ref, w_ref, g2_ref, yn_ref, o_ref, x1_ref, h2_ref):
        yh, _ = _rms(ys_ref[...], D_SSM)
        ah, _ = _rms(oa_ref[...], D_ATTN)
        yn = jnp.concatenate([yh * gs_ref[...], ah * ga_ref[...]], axis=1).astype(BF16)
        yn_ref[...] = yn
        o = _dot(yn, w_ref[...])
        o_ref[...] = o
        x1 = x_ref[...] + mod_ref[0, 2:3, :] * o
        x1_ref[...] = x1
        xh, _ = _rms(x1, D)
        h2_ref[...] = ((xh * g2_ref[...]) * (1.0 + mod_ref[0, 4:5, :]) + mod_ref[0, 3:4, :]).astype(BF16)

    row = lambda w: pl.BlockSpec((tm, w), lambda i: (i, 0))
    return pl.pallas_call(
        body, name="p1_fwd", grid=(n // tm,),
        out_shape=[jax.ShapeDtypeStruct((n, D_SSM + NH * HP), BF16), jax.ShapeDtypeStruct((n, D), F32),
                   jax.ShapeDtypeStruct((n, D), F32), jax.ShapeDtypeStruct((n, D), BF16)],
        in_specs=[row(D_SSM), row(NH * HP), row(D), pl.BlockSpec((1, 8, D), lambda i: (i // tps, 0, 0)),
                  _VM, _VM, _VM, _VM],
        out_specs=[row(D_SSM + NH * HP), row(D), row(D), row(D)],
        compiler_params=_cp(("parallel",)),
    )(yssm, oattn, x, modp, gs, ga, w_out, g2)


def _p2(x1, h2, target, modp, g2, gf, w_ff1, w_ff2, S, tm):
    n = x1.shape[0]
    tps = S // tm
    nb = n // S

    def body(x1_ref, h2_ref, t_ref, mod_ref, g2_ref, gf_ref, w1_ref, w2_ref,
             dx1_ref, r_ref, da_ref, dff_ref, accs_ref, accg_ref):
        i = pl.program_id(0)
        sh2, sc2, gt2 = mod_ref[0, 3:4, :], mod_ref[0, 4:5, :], mod_ref[0, 5:6, :]
        fsh, fsc = mod_ref[0, 6:7, :], mod_ref[0, 7:8, :]
        x1 = x1_ref[...]
        a = _dot(h2_ref[...], w1_ref[...])
        ra = jnp.maximum(a, 0.0)
        rb = (ra * ra).astype(BF16)
        r_ref[...] = rb
        ff = _dot(rb, w2_ref[...])
        x2 = x1 + gt2 * ff
        x2h, rf = _rms(x2, D)
        gf_v = gf_ref[...]
        outn = x2h * gf_v
        err = outn * (1.0 + fsc) + fsh - t_ref[...]
        dout = err * (1.0 / D)
        doutn = dout * (1.0 + fsc)
        dx2 = _rms_bwd(doutn * gf_v, x2h, rf, D)
        dff = (gt2 * dx2).astype(BF16)
        dff_ref[...] = dff
        dr = _dot_nt(dff, w2_ref[...])
        da = (dr * (2.0 * ra)).astype(BF16)
        da_ref[...] = da
        dh2 = _dot_nt(da, w1_ref[...])
        x1h, r2 = _rms(x1, D)
        g2_v = g2_ref[...]
        dn2 = dh2 * (1.0 + sc2)
        dx1_ref[...] = dx2 + _rms_bwd(dn2 * g2_v, x1h, r2, D)

        @pl.when(i % tps == 0)
        def _():
            accs_ref[...] = jnp.zeros_like(accs_ref)

        @pl.when(i == 0)
        def _():
            accg_ref[...] = jnp.zeros_like(accg_ref)

        accs_ref[0, 3:4, :] += _colsum(dh2)
        accs_ref[0, 4:5, :] += _colsum(dh2 * (x1h * g2_v))
        accs_ref[0, 5:6, :] += _colsum(dx2 * ff)
        accs_ref[0, 6:7, :] += _colsum(dout)
        accs_ref[0, 7:8, :] += _colsum(dout * outn)
        accg_ref[0:1, :] += _colsum(dn2 * x1h)
        accg_ref[1:2, :] += _colsum(doutn * x2h)
        accg_ref[2:3, :] += _colsum(err * err) * (0.5 / D)

    row = lambda w: pl.BlockSpec((tm, w), lambda i: (i, 0))
    return pl.pallas_call(
        body, name="p2_mlp_loss", grid=(n // tm,),
        out_shape=[jax.ShapeDtypeStruct((n, D), F32), jax.ShapeDtypeStruct((n, D_FF), BF16),
                   jax.ShapeDtypeStruct((n, D_FF), BF16), jax.ShapeDtypeStruct((n, D), BF16),
                   jax.ShapeDtypeStruct((nb, 8, D), F32), jax.ShapeDtypeStruct((8, D), F32)],
        in_specs=[row(D), row(D), row(D), pl.BlockSpec((1, 8, D), lambda i: (i // tps, 0, 0)), _VM, _VM, _VM, _VM],
        out_specs=[row(D), row(D_FF), row(D_FF), row(D), pl.BlockSpec((1, 8, D), lambda i: (i // tps, 0, 0)),
                   pl.BlockSpec((8, D), lambda i: (0, 0))],
        compiler_params=_cp(("arbitrary",)),
    )(x1, h2, target, modp, g2, gf, w_ff1, w_ff2)


def _p3_bwd(dx1, o, yssm, oattn, modp, gs, ga, w_out, S, tm):
    n = dx1.shape[0]
    tps = S // tm
    nb = n // S

    def body(dx1_ref, o_ref, ys_ref, oa_ref, mod_ref, gs_ref, ga_ref, w_ref,
             do_ref, dys_ref, doa_ref, accs_ref, accg_ref):
        i = pl.program_id(0)
        dx1 = dx1_ref[...]
        dob = (mod_ref[0, 2:3, :] * dx1).astype(BF16)
        do_ref[...] = dob
        dyn = _dot_nt(dob, w_ref[...])
        yh, rs = _rms(ys_ref[...], D_SSM)
        ah, ra = _rms(oa_ref[...], D_ATTN)
        d1 = dyn[:, 0:D_SSM]
        d2 = dyn[:, D_SSM:D_SSM + NH * HP]
        dys_ref[...] = _rms_bwd(d1 * gs_ref[...], yh, rs, D_SSM)
        doa_ref[...] = _rms_bwd(d2 * ga_ref[...], ah, ra, D_ATTN)

        @pl.when(i % tps == 0)
        def _():
            accs_ref[...] = jnp.zeros_like(accs_ref)

        @pl.when(i == 0)
        def _():
            accg_ref[...] = jnp.zeros_like(accg_ref)

        accs_ref[0, 2:3, :] += _colsum(dx1 * o_ref[...])
        accg_ref[0:1, 0:D_SSM] += _colsum(d1 * yh)
        accg_ref[1:2, :] += _colsum(d2 * ah)

    row = lambda w: pl.BlockSpec((tm, w), lambda i: (i, 0))
    return pl.pallas_call(
        body, name="p3_bwd", grid=(n // tm,),
        out_shape=[jax.ShapeDtypeStruct((n, D), BF16), jax.ShapeDtypeStruct((n, D_SSM), F32),
                   jax.ShapeDtypeStruct((n, NH * HP), F32), jax.ShapeDtypeStruct((nb, 8, D), F32),
                   jax.ShapeDtypeStruct((8, NH * HP), F32)],
        in_specs=[row(D), row(D), row(D_SSM), row(NH * HP), pl.BlockSpec((1, 8, D), lambda i: (i // tps, 0, 0)),
                  _VM, _VM, _VM],
        out_specs=[row(D), row(D_SSM), row(NH * HP), pl.BlockSpec((1, 8, D), lambda i: (i // tps, 0, 0)),
                   pl.BlockSpec((8, NH * HP), lambda i: (0, 0))],
        compiler_params=_cp(("arbitrary",)),
    )(dx1, o, yssm, oattn, modp, gs, ga, w_out)


def _wgrad(a, b, name, col_slots=0):
    n, k1 = a.shape
    k2 = b.shape[1]
    bn = 512 if n % 512 == 0 else n
    bk1 = 512 if k1 % 512 == 0 else k1
    bk2 = k2 // col_slots if col_slots else (1024 if (k2 % 1024 == 0) else k2)

    def body(a_ref, b_ref, o_ref):
        @pl.when(pl.program_id(2) == 0)
        def _():
            o_ref[...] = jnp.zeros_like(o_ref)

        o_ref[...] += _dot_tn(a_ref[...], b_ref[...]).reshape(o_ref.shape)

    if col_slots:
        out_shape = jax.ShapeDtypeStruct((col_slots, k1, bk2), F32)
        out_spec = pl.BlockSpec((1, bk1, bk2), lambda i, j, t: (j, i, 0))
    else:
        out_shape = jax.ShapeDtypeStruct((k1, k2), F32)
        out_spec = pl.BlockSpec((bk1, bk2), lambda i, j, t: (i, j))
    return pl.pallas_call(
        body, name=name, grid=(k1 // bk1, k2 // bk2, n // bn),
        out_shape=out_shape,
        in_specs=[pl.BlockSpec((bn, bk1), lambda i, j, t: (t, i)), pl.BlockSpec((bn, bk2), lambda i, j, t: (t, j))],
        out_specs=out_spec,
        compiler_params=_cp(("parallel", "parallel", "arbitrary")),
    )(a, b)


def _row_block(rows):
    if rows <= 256:
        return rows
    return next(b for b in (256, 192, 128, 64, 32, 16, 8) if rows % b == 0)


def _add_half(g, recv, cidx, name):
    _, rows2, w = g.shape
    rows = rows2 // 2
    br = _row_block(rows)
    nblk = rows // br

    def body(c_ref, g_ref, r_ref, o_ref):
        o_ref[...] = g_ref[...] + r_ref[...]

    return pl.pallas_call(
        body, name=name,
        grid_spec=pltpu.PrefetchScalarGridSpec(
            num_scalar_prefetch=1, grid=(4, nblk),
            in_specs=[pl.BlockSpec((1, br, w), lambda s, i, c: (s, c[0] * nblk + i, 0)),
                      pl.BlockSpec((1, br, w), lambda s, i, c: (s, i, 0))],
            out_specs=pl.BlockSpec((1, br, w), lambda s, i, c: (s, i, 0))),
        out_shape=jax.ShapeDtypeStruct((4, rows, w), F32),
        compiler_params=_cp(("parallel", "parallel")),
    )(cidx, g, recv)


def _add_chips(r, name):
    _, rows, w = r.shape
    br = _row_block(rows)

    def body(r_ref, o_ref):
        o_ref[...] = ((r_ref[0] + r_ref[1]) + r_ref[2]) + r_ref[3]

    return pl.pallas_call(
        body, name=name, grid=(rows // br,),
        out_shape=jax.ShapeDtypeStruct((rows, w), F32),
        in_specs=[pl.BlockSpec((4, br, w), lambda i: (0, i, 0))],
        out_specs=pl.BlockSpec((br, w), lambda i: (i, 0)),
        compiler_params=_cp(("parallel",)),
    )(r)


def _sum_devices(a):
    def body(a_ref, o_ref):
        acc = a_ref[0:1, :]
        for k in range(1, 8):
            acc = acc + a_ref[k:k + 1, :]
        o_ref[...] = acc

    return pl.pallas_call(
        body, name="small_grad_sum", out_shape=jax.ShapeDtypeStruct((1, a.shape[1]), F32),
        in_specs=[_VM], out_specs=_VM, compiler_params=_cp(),
    )(a)


def _adamw_math(wv, gv, mv, vv):
    m_new = ADAM_B1 * mv + (1.0 - ADAM_B1) * gv
    v_new = ADAM_B2 * vv + (1.0 - ADAM_B2) * (gv * gv)
    m_hat = m_new / (1.0 - ADAM_B1 ** ADAM_STEP)
    v_hat = v_new / (1.0 - ADAM_B2 ** ADAM_STEP)
    return -ADAM_LR * (m_hat / (jnp.sqrt(v_hat) + ADAM_EPS) + ADAM_WD * wv), m_new, v_new


def _adamw_small(ws, gs, ms, vs):
    k = len(ws)

    def body(*refs):
        ins, outs = refs[:4 * k], refs[4 * k:]
        for t in range(k):
            d, m_new, v_new = _adamw_math(ins[t][...], ins[k + t][...], ins[2 * k + t][...], ins[3 * k + t][...])
            outs[t][...] = d
            outs[k + t][...] = m_new
            outs[2 * k + t][...] = v_new

    shapes = [jax.ShapeDtypeStruct(w.shape, F32) for w in ws]
    return pl.pallas_call(
        body, name="adamw_small", out_shape=shapes * 3,
        in_specs=[_VM] * (4 * k), out_specs=[_VM] * (3 * k), compiler_params=_cp(),
    )(*ws, *gs, *ms, *vs)


def _adamw(w, g, m, v, name):
    rows, wd = w.shape
    br = _row_block(rows)

    def body(w_ref, g_ref, m_ref, v_ref, d_ref, nm_ref, nv_ref):
        d, m_new, v_new = _adamw_math(w_ref[...], g_ref[...], m_ref[...], v_ref[...])
        d_ref[...] = d
        nm_ref[...] = m_new
        nv_ref[...] = v_new

    spec = pl.BlockSpec((br, wd), lambda i: (i, 0))
    return pl.pallas_call(
        body, name=name, grid=(rows // br,),
        out_shape=[jax.ShapeDtypeStruct((rows, wd), F32)] * 3,
        in_specs=[spec] * 4, out_specs=[spec] * 3,
        compiler_params=_cp(("parallel",)),
    )(w, g, m, v)


def _other_chips(x, y):
    return [(1 - x, y), (x, 1 - y), (1 - x, 1 - y)]


def _other_devices(x, y, c):
    flip = lambda v, d: (1 - v) if d else v
    return [(flip(x, dx), flip(y, dy), flip(c, dc))
            for dx in (0, 1) for dy in (0, 1) for dc in (0, 1) if (dx, dy, dc) != (0, 0, 0)]


def _exchange(name, ins, out_shapes, n_local, n_remote, plan):
    ni, no = len(ins), len(out_shapes)

    def body(*refs):
        in_refs, out_refs = refs[:ni], refs[ni:ni + no]
        send_sems, recv_sems, local_sems = refs[ni + no:]
        x, y, c = lax.axis_index("x"), lax.axis_index("y"), lax.axis_index("c")
        local, remote = plan(in_refs, out_refs, x, y, c)
        assert len(local) == n_local and len(remote) == n_remote

        def push(k, src, dst, dev):
            return pltpu.make_async_remote_copy(src_ref=src, dst_ref=dst, send_sem=send_sems.at[k],
                                                recv_sem=recv_sems.at[k], device_id=dev, device_id_type=MESH)

        own = [pltpu.make_async_copy(s, d, local_sems.at[i]) for i, (s, d) in enumerate(local)]
        for cp in own:
            cp.start()
        sends = [push(k, s, d, dev) for k, (s, d, dev, _) in enumerate(remote)]
        for cp in sends:
            cp.start()
        for k, (s, _, dev, landing) in enumerate(remote):
            push(k, s, landing, dev).wait_recv()
        for cp in sends:
            cp.wait_send()
        for cp in own:
            cp.wait()

    return pl.pallas_call(
        body, name=name, out_shape=out_shapes,
        in_specs=[_ANY] * ni, out_specs=[_ANY] * no,
        scratch_shapes=[pltpu.SemaphoreType.DMA((n_remote,)), pltpu.SemaphoreType.DMA((n_remote,)),
                        pltpu.SemaphoreType.DMA((max(n_local, 1),))],
        compiler_params=pltpu.CompilerParams(has_side_effects=True),
    )(*ins)


def _gather_chips(name, shards, everyone=()):
    ns, ne = len(shards), len(everyone)
    outs = [jax.ShapeDtypeStruct((4,) + a.shape, a.dtype) for a in shards]
    outs += [jax.ShapeDtypeStruct((8,) + a.shape, a.dtype) for a in everyone]

    def plan(i, o, x, y, c):
        mine, me = 2 * x + y, 4 * x + 2 * y + c
        local, remote = [], []
        for t in range(ns):
            local.append((i[t], o[t].at[mine]))
            for px, py in _other_chips(x, y):
                remote.append((i[t], o[t].at[mine], (px, py, c), o[t].at[2 * px + py]))
        for t in range(ns, ns + ne):
            local.append((i[t], o[t].at[me]))
            for px, py, pc in _other_devices(x, y, c):
                remote.append((i[t], o[t].at[me], (px, py, pc), o[t].at[4 * px + 2 * py + pc]))
        return local, remote

    return _exchange(name, list(shards) + list(everyone), outs, ns + ne, 3 * ns + 7 * ne, plan)


def _swap_halves(gs, everyone):
    ns, ne = len(gs), len(everyone)
    outs = [jax.ShapeDtypeStruct((4, g.shape[1] // 2, g.shape[2]), g.dtype) for g in gs]
    outs += [jax.ShapeDtypeStruct((8,) + a.shape, a.dtype) for a in everyone]

    def plan(i, o, x, y, c):
        me = 4 * x + 2 * y + c
        local, remote = [], []
        for t in range(ns):
            h = gs[t].shape[1] // 2
            theirs = i[t].at[:, pl.ds(pl.multiple_of((1 - c) * h, 8), h), :]
            remote.append((theirs, o[t], (x, y, 1 - c), o[t]))
        for t in range(ns, ns + ne):
            local.append((i[t], o[t].at[me]))
            for px, py, pc in _other_devices(x, y, c):
                remote.append((i[t], o[t].at[me], (px, py, pc), o[t].at[4 * px + 2 * py + pc]))
        return local, remote

    return _exchange("grad_swap_sibling", list(gs) + list(everyone), outs, ne, ns + 7 * ne, plan)


def _scatter_chips(parts):
    ns = len(parts)
    outs = [jax.ShapeDtypeStruct(a.shape, a.dtype) for a in parts]

    def plan(i, o, x, y, c):
        mine = 2 * x + y
        local, remote = [], []
        for t in range(ns):
            local.append((i[t].at[mine], o[t].at[mine]))
            for px, py in _other_chips(x, y):
                remote.append((i[t].at[2 * px + py], o[t].at[mine], (px, py, c), o[t].at[2 * px + py]))
        return local, remote

    return _exchange("grad_scatter_chips", list(parts), outs, ns, 3 * ns, plan)


def _join_halves(halves):
    ns = len(halves)
    outs = [jax.ShapeDtypeStruct((2 * a.shape[0], a.shape[1]), a.dtype) for a in halves]

    def plan(i, o, x, y, c):
        local, remote = [], []
        for t in range(ns):
            h = halves[t].shape[0]
            mine = o[t].at[pl.ds(pl.multiple_of(c * h, 8), h), :]
            theirs = o[t].at[pl.ds(pl.multiple_of((1 - c) * h, 8), h), :]
            local.append((i[t], mine))
            remote.append((i[t], mine, (x, y, 1 - c), theirs))
        return local, remote

    return _exchange("grad_join_sibling", list(halves), outs, ns, ns, plan)


def _pad_heads_cols(w, per, used):
    k = w.shape[0]
    w = w.reshape(k, NH, per)[:, :, :used]
    return jnp.pad(w, ((0, 0), (0, 0), (0, HP - used))).reshape(k, NH * HP)


def _unpad_heads_cols(w, used):
    k = w.shape[0]
    return w.reshape(k, NH, HP)[:, :, :used]


def _prep_weights(wf):
    bf = lambda a: a.astype(BF16)
    out = {}
    out["w_in"] = jnp.pad(bf(wf["w_in"]), ((0, 0), (0, IN_PAD - IN_COLS)))
    out["w_glu"] = bf(wf["w_glu"])
    out["w_uq"] = _pad_heads_cols(bf(wf["w_uq"]), QK_NOPE + QK_ROPE, QK_NOPE + QK_ROPE)
    wkv = bf(wf["w_ukv"]).reshape(KV_LORA, NH, QK_NOPE + V_HEAD)
    wk = jnp.pad(wkv[:, :, :QK_NOPE], ((0, 0), (0, 0), (0, HP - QK_NOPE))).reshape(KV_LORA, NH * HP)
    wv = jnp.pad(wkv[:, :, QK_NOPE:], ((0, 0), (0, 0), (0, HP - V_HEAD))).reshape(KV_LORA, NH * HP)
    out["w_ukv"] = jnp.concatenate([wk, wv], axis=1)
    wo = bf(wf["w_out"])
    wo_a = jnp.pad(wo[D_SSM:].reshape(NH, V_HEAD, D), ((0, 0), (0, HP - V_HEAD), (0, 0))).reshape(NH * HP, D)
    out["w_out"] = jnp.concatenate([wo[:D_SSM], wo_a], axis=0)
    out["w_ff1"] = bf(wf["w_ff1"])
    out["w_ff2"] = bf(wf["w_ff2"])
    return out


def _rope_tables(positions):
    inv_freq = ROPE_BASE ** (-jnp.arange(0, QK_ROPE, 2, dtype=F32) / QK_ROPE)
    ang = positions.astype(F32)[:, None] * inv_freq
    cos, sin = jnp.cos(ang), jnp.sin(ang)
    n = positions.shape[0]
    one = jnp.ones((n, QK_NOPE), F32)
    z16 = jnp.zeros((n, 16), F32)
    z32 = jnp.zeros((n, 32), F32)
    z64 = jnp.zeros((n, QK_NOPE), F32)
    rc = jnp.concatenate([one, cos, cos, z32], axis=1)
    rs1 = jnp.concatenate([z64, -sin, z16, z32], axis=1)
    rs2 = jnp.concatenate([z64, z16, sin, z32], axis=1)
    return rc, rs1, rs2


def _permute_rows(a, S):
    n, w = a.shape
    return a.reshape(n // S, 8, S // 8, w).transpose(0, 2, 1, 3).reshape(n, w)


def _unpermute_rows(a, S):
    n, w = a.shape
    return a.reshape(n // S, S // 8, 8, w).transpose(0, 2, 1, 3).reshape(n, w)


def _block_diag_in(bb):
    eye = jnp.eye(G, dtype=bb.dtype)
    return jnp.einsum("gph,gk->ghkp", bb, eye).reshape(G * H, G * P)


def _block_diag_out(cc):
    eye = jnp.eye(G, dtype=cc.dtype)
    return jnp.einsum("ghp,gk->gpkh", cc, eye).reshape(G * P, G * H)


def _slots(full):
    r, cdim = full.shape
    return full.reshape(r, 4, cdim // 4).transpose(1, 0, 2)


def _unslots(g):
    s, r, cs = g.shape
    return g.transpose(1, 0, 2).reshape(r, s * cs)


def _local_step(x, positions, target, modp, wf):
    nb, S, _ = x.shape
    n = nb * S
    tm = min(256, S)
    tt = min(256, S)
    tq = min(512, S // 2)
    kw = _prep_weights(wf)
    row = lambda a: a.reshape(1, -1).astype(F32)

    xf = x.reshape(n, D)
    tf = target.reshape(n, D)
    g1, g2, gf = row(wf["norm1_g"]), row(wf["norm2_g"]), row(wf["final_norm_g"])
    h1, proj = _f1_fwd(xf, modp, g1, kw["w_in"], S, tm)

    col = lambda a: a.reshape(NST, 1)
    lam_re, lam_im = col(wf["ssm_lambda_re"]), col(wf["ssm_lambda_im"])
    logdt = jnp.repeat(wf["ssm_log_dt"].reshape(G, 1), P, axis=1).reshape(NST, 1)
    b_re, b_im = wf["ssm_b_re"].reshape(NST, H), wf["ssm_b_im"].reshape(NST, H)
    lbr, lbi, bbr, bbi = _ssm_param_fwd(lam_re, lam_im, logdt, b_re, b_im)
    lre8 = jnp.broadcast_to(lbr.reshape(1, NST), (8, NST))
    lim8 = jnp.broadcast_to(lbi.reshape(1, NST), (8, NST))
    bm = jnp.concatenate([_block_diag_in(bbr.reshape(G, P, H)), _block_diag_in(bbi.reshape(G, P, H))],
                         axis=1).astype(BF16)
    cm = jnp.concatenate([_block_diag_out(wf["ssm_c_re"]), -_block_diag_out(wf["ssm_c_im"])], axis=0).astype(BF16)
    dvec = row(wf["ssm_d"])
    u_p = _permute_rows(proj[:, :D_SSM], S)
    fcr, fci = _ssm_local(u_p, bm, lre8, lim8, S, tt)
    st, ypre, z, gact, yssm_p = _ssm_fwd(u_p, fcr, fci, bm, cm, dvec, kw["w_glu"], lre8, lim8, S, tt)
    yssm = _unpermute_rows(yssm_p, S)

    rc, rs1, rs2 = _rope_tables(positions.reshape(n))
    gq, gkv = row(wf["q_norm_g"]), row(wf["kv_norm_g"])
    q, k, v, qn, kvn = _mla_fwd(proj, rc, rs1, rs2, gq, gkv, kw["w_uq"], kw["w_ukv"], tm)
    oattn, lse = _attn_fwd(q, k, v, S, tq)

    gs = row(wf["ssm_out_g"])
    ga = jnp.pad(wf["attn_out_g"].reshape(NH, V_HEAD), ((0, 0), (0, HP - V_HEAD))).reshape(1, NH * HP)
    yn, o, x1, h2 = _p1_fwd(yssm, oattn, xf, modp, gs, ga, kw["w_out"], g2, S, tm)
    dx1, r, da, dff, accs2, accg2 = _p2(x1, h2, tf, modp, g2, gf, kw["w_ff1"], kw["w_ff2"], S, tm)
    loss = jnp.sum(accg2[2])
    do, dyssm, doattn, accs3, accg3 = _p3_bwd(dx1, o, yssm, oattn, modp, gs, ga, kw["w_out"], S, tm)

    dq = _attn_bwd_dq(q, k, v, oattn, doattn, lse, S, tq)
    dk, dv = _attn_bwd_dkv(q, k, v, oattn, doattn, lse, S, tq)
    dmla, dqb, dkvb, accm = _mla_bwd(dq, dk, dv, proj, rc, rs1, rs2, gq, gkv, kw["w_uq"], kw["w_ukv"], tm)

    dys_p = _permute_rows(dyssm, S)
    dy, dz, air, aii = _ssm_bwd_a(dys_p, z, ypre, kw["w_glu"], cm, lre8, lim8, S, tt)
    du_p, dcm, dbm, dd, dlr, dli = _ssm_bwd_b(dy, u_p, st, fcr, fci, air, aii, bm, cm, dvec, lre8, lim8, S, tt)
    du = _unpermute_rows(du_p, S)
    dcm = dcm.reshape(2, 4, 8, P, 8, H)
    dc_re = jnp.einsum("qgpgh->qghp", dcm[0]).reshape(G, H, P)
    dc_im = -jnp.einsum("qgpgh->qghp", dcm[1]).reshape(G, H, P)
    dbm = dbm.reshape(8, H, 2, 4, 8, P)
    dbb_re = jnp.einsum("ghqgp->qgph", dbm[:, :, 0]).reshape(NST, H)
    dbb_im = jnp.einsum("ghqgp->qgph", dbm[:, :, 1]).reshape(NST, H)
    gb_re, gb_im, glr, gli, gdt = _ssm_param_bwd(lam_re, lam_im, logdt, b_re, b_im, dlr.reshape(NST, 1),
                                                 dli.reshape(NST, 1), dbb_re, dbb_im)
    glogdt = _rowsum(gdt.reshape(G, P))

    dx, dproj, accs1, accg1 = _f1_bwd(du, dmla, dx1, xf, modp, g1, kw["w_in"], S, tm)

    big = {}
    big["w_in"] = _slots(_wgrad(h1, dproj, "wgrad_in")[:, :IN_COLS])
    big["w_glu"] = _wgrad(gact, dz, "wgrad_glu", col_slots=4)
    big["w_uq"] = _slots(_unpad_heads_cols(_wgrad(qn, dqb, "wgrad_uq"), QK_NOPE + QK_ROPE).reshape(Q_LORA, -1))
    gkvw = _wgrad(kvn, dkvb, "wgrad_ukv")
    big["w_ukv"] = _slots(jnp.concatenate([_unpad_heads_cols(gkvw[:, :NH * HP], QK_NOPE),
                                           _unpad_heads_cols(gkvw[:, NH * HP:], V_HEAD)], axis=2).reshape(KV_LORA, -1))
    gwo = _wgrad(yn, do, "wgrad_out")
    big["w_out"] = jnp.concatenate([gwo[:D_SSM].reshape(2, D_SSM // 2, D),
                                    gwo[D_SSM:].reshape(2, NH // 2 * HP, D).reshape(2, NH // 2, HP, D)[:, :, :V_HEAD]
                                    .reshape(2, D_ATTN // 2, D)], axis=0)
    big["w_ff1"] = _wgrad(h2, da, "wgrad_ff1", col_slots=4)
    big["w_ff2"] = _wgrad(r, dff, "wgrad_ff2").reshape(4, D_FF // 4, D)

    small = {}
    small["norm1_g"] = accg1[0:1]
    small["norm2_g"] = accg2[0:1]
    small["final_norm_g"] = accg2[1:2]
    small["ssm_out_g"] = accg3[0:1, :D_SSM]
    small["attn_out_g"] = accg3[1].reshape(NH, HP)[:, :V_HEAD].reshape(1, D_ATTN)
    small["q_norm_g"] = accm[0:1, :Q_LORA]
    small["kv_norm_g"] = accm[1:2, :KV_LORA]
    small["ssm_lambda_re"] = glr.reshape(G, P)
    small["ssm_lambda_im"] = gli.reshape(G, P)
    small["ssm_b_re"] = gb_re
    small["ssm_b_im"] = gb_im
    small["ssm_c_re"] = dc_re.reshape(G * H, P)
    small["ssm_c_im"] = dc_im.reshape(G * H, P)
    small["ssm_d"] = dd.reshape(G, H)
    small["ssm_log_dt"] = glogdt.reshape(1, G)
    return loss, dx.reshape(nb, S, D), big, small, accs1 + accs2 + accs3


def _view2d(a):
    return a.reshape(-1, a.shape[-1]) if a.ndim > 1 else a.reshape(1, -1)


def kernel(x, c, positions, ada_w, ada_b, norm1_g, w_in, ssm_lambda_re, ssm_lambda_im, ssm_b_re, ssm_b_im, ssm_c_re, ssm_c_im, ssm_d, ssm_log_dt, w_glu, q_norm_g, w_uq, kv_norm_g, w_ukv, ssm_out_g, attn_out_g, w_out, norm2_g, w_ff1, w_ff2, final_ada_w, final_ada_b, final_norm_g, loss_target, m_ada_w, m_ada_b, m_norm1_g, m_w_in, m_ssm_lambda_re, m_ssm_lambda_im, m_ssm_b_re, m_ssm_b_im, m_ssm_c_re, m_ssm_c_im, m_ssm_d, m_ssm_log_dt, m_w_glu, m_q_norm_g, m_w_uq, m_kv_norm_g, m_w_ukv, m_ssm_out_g, m_attn_out_g, m_w_out, m_norm2_g, m_w_ff1, m_w_ff2, m_final_ada_w, m_final_ada_b, m_final_norm_g, v_ada_w, v_ada_b, v_norm1_g, v_w_in, v_ssm_lambda_re, v_ssm_lambda_im, v_ssm_b_re, v_ssm_b_im, v_ssm_c_re, v_ssm_c_im, v_ssm_d, v_ssm_log_dt, v_w_glu, v_q_norm_g, v_w_uq, v_kv_norm_g, v_w_ukv, v_ssm_out_g, v_attn_out_g, v_w_out, v_norm2_g, v_w_ff1, v_w_ff2, v_final_ada_w, v_final_ada_b, v_final_norm_g):
    args = dict(locals())
    names = list(inspect.signature(kernel).parameters)
    wnames = names[3:names.index("loss_target")]
    small_names = [nm for nm in wnames if nm not in GATHERED and nm not in TP]
    reduced_names = [nm for nm in small_names if nm not in ("ada_b", "final_ada_b")]
    w = {nm: args[nm] for nm in wnames}
    m = {nm: args["m_" + nm] for nm in wnames}
    v = {nm: args["v_" + nm] for nm in wnames}
    nb = x.shape[0]
    xi, yi, ci = lax.axis_index("x"), lax.axis_index("y"), lax.axis_index("c")
    chip, me = 2 * xi + yi, 4 * xi + 2 * yi + ci

    got = _gather_chips("gather_weights", [_view2d(w[nm]).astype(BF16) for nm in GATHERED], [c])
    wf = {nm: (g.reshape(-1, g.shape[-1]) if nm in ROW_SHARDED else _unslots(g)) for nm, g in zip(GATHERED, got)}
    for nm in small_names:
        wf[nm] = w[nm][0] if w[nm].ndim > 1 else w[nm]
    c_all = got[len(GATHERED)].reshape(8 * nb, D)

    na, nf = ada_w.shape[-1], final_ada_w.shape[-1]
    ada_b_s = lax.dynamic_slice(ada_b, (0, chip * na), (1, na))
    fada_b_s = lax.dynamic_slice(final_ada_b.reshape(1, -1), (0, chip * nf), (1, nf))
    cond_all, modcols = _mod_fwd(c_all, ada_w[0], ada_b_s, final_ada_w, fada_b_s)
    (mod_g,) = _gather_chips("gather_mod", [modcols])
    mine = lax.dynamic_slice(mod_g, (0, me * nb, 0), (4, nb, na + nf))
    modp = jnp.concatenate([mine[:, :, :na].transpose(1, 0, 2).reshape(nb, 6, D),
                            mine[:, :, na:].transpose(1, 0, 2).reshape(nb, 2, D)], axis=1)

    loss, grad_x, big, small, dmodp = _local_step(x, positions, loss_target, modp, wf)
    loss = lax.psum(loss, ("x", "y", "c"))

    sizes = [small[nm].size for nm in reduced_names]
    pad = -sum(sizes) % 128
    packed = jnp.concatenate([small[nm].reshape(1, -1) for nm in reduced_names] + [jnp.zeros((1, pad), F32)], axis=1)
    swapped = _swap_halves([big[nm] for nm in GATHERED], [dmodp.reshape(nb, 8 * D), packed])
    cidx = ci.astype(jnp.int32).reshape(1)
    chip_sums = [_add_half(big[nm], r, cidx, "grad_add_sibling_" + nm) for nm, r in zip(GATHERED, swapped)]
    halves = [_add_chips(r, "grad_add_chips_" + nm) for nm, r in zip(GATHERED, _scatter_chips(chip_sums))]
    grads = dict(zip(GATHERED, _join_halves(halves)))
    dmod_all = swapped[len(GATHERED)].reshape(8 * nb, 8 * D)
    small_sum = _sum_devices(swapped[len(GATHERED) + 1].reshape(8, -1))
    off = 0
    for nm, sz in zip(reduced_names, sizes):
        grads[nm] = small_sum[:, off:off + sz].reshape(small[nm].shape)
        off += sz

    dsl = jnp.concatenate([lax.dynamic_slice(dmod_all, (0, chip * na), (8 * nb, na)),
                           lax.dynamic_slice(dmod_all, (0, 6 * D + chip * nf), (8 * nb, nf))], axis=1)
    gw, gb = _mod_bwd(cond_all.T, dsl, dmod_all)
    grads["ada_w"], grads["final_ada_w"] = gw[:, :na], gw[:, na:]
    grads["ada_b"], grads["final_ada_b"] = gb[:, :6 * D], gb[:, 6 * D:]

    delta, new_m, new_v = {}, {}, {}
    for nm in GATHERED + TP:
        delta[nm], new_m[nm], new_v[nm] = _adamw(_view2d(w[nm]), grads[nm], _view2d(m[nm]), _view2d(v[nm]),
                                                  "adamw_" + nm)
    upd = _adamw_small([_view2d(w[nm]) for nm in small_names], [grads[nm] for nm in small_names],
                       [_view2d(m[nm]) for nm in small_names], [_view2d(v[nm]) for nm in small_names])
    k = len(small_names)
    for t, nm in enumerate(small_names):
        delta[nm], new_m[nm], new_v[nm] = upd[t], upd[k + t], upd[2 * k + t]

    outs = [grads, delta, new_m, new_v]
    return (loss, grad_x, *[d[nm].reshape(w[nm].shape) for d in outs for nm in wnames])
```

```python
import functools
import inspect
import math

import jax
import jax.numpy as jnp
from jax import lax
from jax.experimental import pallas as pl
from jax.experimental.pallas import tpu as pltpu

F32 = jnp.float32
BF16 = jnp.bfloat16

D = 1024
D_SSM = 512
G = 32
H = 16
P = 64
NST = G * P
D_ATTN = 512
NH = 8
QK_NOPE = 64
QK_ROPE = 32
V_HEAD = 64
HP = 128
Q_LORA = 384
KV_LORA = 256
IN_COLS = D_SSM + Q_LORA + KV_LORA + QK_ROPE
IN_PAD = 1280
D_FF = 4096
ROPE_BASE = 10000.0
EPS = 1e-6
ADAM_LR = 0.001
ADAM_B1 = 0.9
ADAM_B2 = 0.999
ADAM_EPS = 1e-08
ADAM_WD = 0.01
ADAM_STEP = 10
NEG = -1e30
VMEM_LIMIT = 60 << 20

MESH = pl.DeviceIdType.MESH
_VM = pl.BlockSpec(memory_space=pltpu.VMEM)
_ANY = pl.BlockSpec(memory_space=pl.ANY)

GATHERED = ["w_in", "w_glu", "w_uq", "w_ukv", "w_out", "w_ff1", "w_ff2"]
TP = ["ada_w", "final_ada_w"]
ROW_SHARDED = ("w_out", "w_ff2")


def _cp(sem=None, vmem=VMEM_LIMIT):
    kw = dict(vmem_limit_bytes=vmem)
    if sem is not None:
        kw["dimension_semantics"] = sem
    return pltpu.CompilerParams(**kw)


def _dot(a, b):
    return jnp.dot(a, b, preferred_element_type=F32)


def _dot_nt(a, b):
    return lax.dot_general(a, b, (((1,), (1,)), ((), ())), preferred_element_type=F32)


def _dot_tn(a, b):
    return lax.dot_general(a, b, (((0,), (0,)), ((), ())), preferred_element_type=F32)


def _rms(x, n):
    r = lax.rsqrt(jnp.sum(x * x, axis=-1, keepdims=True) * (1.0 / n) + EPS)
    return x * r, r


def _rms_bwd(dyg, xhat, r, n):
    return r * (dyg - xhat * (jnp.sum(dyg * xhat, axis=-1, keepdims=True) * (1.0 / n)))


def _sigmoid(x):
    return 1.0 / (1.0 + jnp.exp(-x))


_GK = math.sqrt(2.0 / math.pi)
_GC = 0.044715


def _gelu(y):
    t = jnp.tanh(_GK * (y + _GC * y * y * y))
    return 0.5 * y * (1.0 + t)


def _gelu_grad(y):
    t = jnp.tanh(_GK * (y + _GC * y * y * y))
    return 0.5 * (1.0 + t) + 0.5 * y * (1.0 - t * t) * _GK * (1.0 + 3.0 * _GC * y * y)


def _colsum(x):
    return jnp.sum(x, axis=0, keepdims=True)


def _roll(x, s):
    return pltpu.roll(x, s % x.shape[-1], x.ndim - 1)


def _mod_fwd(c_all, ada_w_s, ada_b_s, fada_w_s, fada_b_s):
    nseq = c_all.shape[0]
    na, nf = ada_w_s.shape[1], fada_w_s.shape[1]

    def body(c_ref, w_ref, b_ref, fw_ref, fb_ref, cond_ref, mod_ref):
        cv = c_ref[...]
        cond = cv * _sigmoid(cv)
        cond_ref[...] = cond
        cb = cond.astype(BF16)
        mod_ref[:, 0:na] = _dot(cb, w_ref[...].astype(BF16)) + b_ref[...]
        mod_ref[:, na:na + nf] = _dot(cb, fw_ref[...].astype(BF16)) + fb_ref[...]

    return pl.pallas_call(
        body, name="mod_fwd",
        out_shape=[jax.ShapeDtypeStruct((nseq, D), F32), jax.ShapeDtypeStruct((nseq, na + nf), F32)],
        in_specs=[_VM] * 5, out_specs=[_VM] * 2, compiler_params=_cp(),
    )(c_all, ada_w_s, ada_b_s, fada_w_s, fada_b_s)


def _mod_bwd(cond_t, dsl, dall):
    nseq, n = dsl.shape
    bc = 512

    def body(ct_ref, dm_ref, da_ref, gw_ref, gb_ref):
        ct = ct_ref[...]
        dm = dm_ref[...]
        acc = ct[:, 0:1] * dm[0:1, :]
        for b in range(1, nseq):
            acc = acc + ct[:, b:b + 1] * dm[b:b + 1, :]
        gw_ref[...] = acc

        @pl.when(pl.program_id(0) == 0)
        def _():
            gb_ref[...] = _colsum(da_ref[...])

    return pl.pallas_call(
        body, name="mod_bwd", grid=(n // bc,),
        out_shape=[jax.ShapeDtypeStruct((D, n), F32), jax.ShapeDtypeStruct((1, dall.shape[1]), F32)],
        in_specs=[_VM, pl.BlockSpec((nseq, bc), lambda i: (0, i)), _VM],
        out_specs=[pl.BlockSpec((D, bc), lambda i: (0, i)), pl.BlockSpec((1, dall.shape[1]), lambda i: (0, 0))],
        compiler_params=_cp(("arbitrary",)),
    )(cond_t, dsl, dall)


def _f1_fwd(x, modp, g1, w_in, S, tm):
    n = x.shape[0]
    tps = S // tm

    def body(x_ref, mod_ref, g_ref, w_ref, h_ref, proj_ref):
        xhat, _ = _rms(x_ref[...], D)
        h = (xhat * g_ref[...]) * (1.0 + mod_ref[0, 1:2, :]) + mod_ref[0, 0:1, :]
        hb = h.astype(BF16)
        h_ref[...] = hb
        proj_ref[...] = _dot(hb, w_ref[...])

    return pl.pallas_call(
        body, name="f1_fwd", grid=(n // tm,),
        out_shape=[jax.ShapeDtypeStruct((n, D), BF16), jax.ShapeDtypeStruct((n, IN_PAD), F32)],
        in_specs=[pl.BlockSpec((tm, D), lambda i: (i, 0)),
                  pl.BlockSpec((1, 8, D), lambda i: (i // tps, 0, 0)), _VM, _VM],
        out_specs=[pl.BlockSpec((tm, D), lambda i: (i, 0)), pl.BlockSpec((tm, IN_PAD), lambda i: (i, 0))],
        compiler_params=_cp(("parallel",)),
    )(x, modp, g1, w_in)


def _f1_bwd(du, dmla, dx1, x, modp, g1, w_in, S, tm):
    n = x.shape[0]
    tps = S // tm
    nb = n // S

    def body(du_ref, dm_ref, dx1_ref, x_ref, mod_ref, g_ref, w_ref, dx_ref, dproj_ref, accs_ref, accg_ref):
        i = pl.program_id(0)
        dproj = jnp.concatenate([du_ref[...], dm_ref[...]], axis=1).astype(BF16)
        dproj_ref[...] = dproj
        dh = _dot_nt(dproj, w_ref[...])
        xhat, r = _rms(x_ref[...], D)
        g = g_ref[...]
        dn = dh * (1.0 + mod_ref[0, 1:2, :])
        dx_ref[...] = dx1_ref[...] + _rms_bwd(dn * g, xhat, r, D)

        @pl.when(i % tps == 0)
        def _():
            accs_ref[...] = jnp.zeros_like(accs_ref)

        @pl.when(i == 0)
        def _():
            accg_ref[...] = jnp.zeros_like(accg_ref)

        accs_ref[0, 0:1, :] += _colsum(dh)
        accs_ref[0, 1:2, :] += _colsum(dh * (xhat * g))
        accg_ref[0:1, :] += _colsum(dn * xhat)

    return pl.pallas_call(
        body, name="f1_bwd", grid=(n // tm,),
        out_shape=[jax.ShapeDtypeStruct((n, D), F32), jax.ShapeDtypeStruct((n, IN_PAD), BF16),
                   jax.ShapeDtypeStruct((nb, 8, D), F32), jax.ShapeDtypeStruct((8, D), F32)],
        in_specs=[pl.BlockSpec((tm, D_SSM), lambda i: (i, 0)), pl.BlockSpec((tm, IN_PAD - D_SSM), lambda i: (i, 0)),
                  pl.BlockSpec((tm, D), lambda i: (i, 0)), pl.BlockSpec((tm, D), lambda i: (i, 0)),
                  pl.BlockSpec((1, 8, D), lambda i: (i // tps, 0, 0)), _VM, _VM],
        out_specs=[pl.BlockSpec((tm, D), lambda i: (i, 0)), pl.BlockSpec((tm, IN_PAD), lambda i: (i, 0)),
                   pl.BlockSpec((1, 8, D), lambda i: (i // tps, 0, 0)), pl.BlockSpec((8, D), lambda i: (0, 0))],
        compiler_params=_cp(("arbitrary",)),
    )(du, dmla, dx1, x, modp, g1, w_in)


def _ssm_param_fwd(lam_re, lam_im, logdt, b_re, b_im):
    def body(lr_ref, li_ref, ld_ref, br_ref, bi_ref, lbr_ref, lbi_ref, bbr_ref, bbi_ref):
        lr, li = lr_ref[...], li_ref[...]
        dt = jnp.exp(ld_ref[...])
        er = jnp.exp(lr * dt)
        lbr = er * jnp.cos(li * dt)
        lbi = er * jnp.sin(li * dt)
        den = 1.0 / (lr * lr + li * li)
        cr = ((lbr - 1.0) * lr + lbi * li) * den
        ci = (lbi * lr - (lbr - 1.0) * li) * den
        lbr_ref[...] = lbr
        lbi_ref[...] = lbi
        bbr_ref[...] = cr * br_ref[...] - ci * bi_ref[...]
        bbi_ref[...] = cr * bi_ref[...] + ci * br_ref[...]

    return pl.pallas_call(
        body, name="ssm_param_fwd",
        out_shape=[jax.ShapeDtypeStruct((NST, 1), F32)] * 2 + [jax.ShapeDtypeStruct((NST, H), F32)] * 2,
        in_specs=[_VM] * 5, out_specs=[_VM] * 4, compiler_params=_cp(),
    )(lam_re, lam_im, logdt, b_re, b_im)


def _ssm_param_bwd(lam_re, lam_im, logdt, b_re, b_im, dlb_re, dlb_im, dbb_re, dbb_im):
    def body(lr_ref, li_ref, ld_ref, br_ref, bi_ref, dlr_ref, dli_ref, dbr_ref, dbi_ref,
             gbr_ref, gbi_ref, glr_ref, gli_ref, gdt_ref):
        lr, li = lr_ref[...], li_ref[...]
        dt = jnp.exp(ld_ref[...])
        er = jnp.exp(lr * dt)
        lbr = er * jnp.cos(li * dt)
        lbi = er * jnp.sin(li * dt)
        den = 1.0 / (lr * lr + li * li)
        nr, ni = lbr - 1.0, lbi
        cr = (nr * lr + ni * li) * den
        ci = (ni * lr - nr * li) * den
        br, bi = br_ref[...], bi_ref[...]
        dbr, dbi = dbr_ref[...], dbi_ref[...]
        gbr_ref[...] = cr * dbr + ci * dbi
        gbi_ref[...] = cr * dbi - ci * dbr
        gcr = jnp.sum(dbr * br + dbi * bi, axis=1, keepdims=True)
        gci = jnp.sum(dbi * br - dbr * bi, axis=1, keepdims=True)
        ilr, ili = lr * den, -li * den
        glbr = dlr_ref[...] + (gcr * ilr + gci * ili)
        glbi = dli_ref[...] + (gci * ilr - gcr * ili)
        qr = -(cr * ilr - ci * ili)
        qi = -(cr * ili + ci * ilr)
        glr = gcr * qr + gci * qi
        gli = gci * qr - gcr * qi
        glr = glr + dt * (glbr * lbr + glbi * lbi)
        gli = gli + dt * (glbi * lbr - glbr * lbi)
        wr = lr * lbr - li * lbi
        wi = lr * lbi + li * lbr
        glr_ref[...] = glr
        gli_ref[...] = gli
        gdt_ref[...] = (glbr * wr + glbi * wi) * dt

    return pl.pallas_call(
        body, name="ssm_param_bwd",
        out_shape=[jax.ShapeDtypeStruct((NST, H), F32)] * 2 + [jax.ShapeDtypeStruct((NST, 1), F32)] * 3,
        in_specs=[_VM] * 9, out_specs=[_VM] * 5, compiler_params=_cp(),
    )(lam_re, lam_im, logdt, b_re, b_im, dlb_re, dlb_im, dbb_re, dbb_im)


def _rowsum(a):
    def body(a_ref, o_ref):
        o_ref[...] = jnp.sum(a_ref[...], axis=1, keepdims=True)

    return pl.pallas_call(
        body, name="rowsum", out_shape=jax.ShapeDtypeStruct((a.shape[0], 1), F32),
        in_specs=[_VM], out_specs=_VM, compiler_params=_cp(),
    )(a)


def _pow2k(pr, pi, nsq):
    for _ in range(nsq):
        pr, pi = pr * pr - pi * pi, 2.0 * pr * pi
    return pr, pi


def _ssm_local(u_p, bm, lre8, lim8, S, tt):
    n = u_p.shape[0]
    nb, nt = n // S, S // tt
    nsq = int(round(math.log2(S // 8)))
    assert 2 ** nsq == S // 8

    def body(u_ref, bm_ref, lre_ref, lim_ref, cre_ref, cim_ref, sre, sim, bu):
        j = pl.program_id(1)

        @pl.when(j == 0)
        def _():
            sre[...] = jnp.zeros_like(sre)
            sim[...] = jnp.zeros_like(sim)

        bu[...] = _dot(u_ref[...].astype(BF16), bm_ref[...])
        lre, lim = lre_ref[...], lim_ref[...]

        def step(i, c):
            sr, si = c
            off = pl.multiple_of(i * 8, 8)
            br = bu[pl.ds(off, 8), 0:NST]
            bi = bu[pl.ds(off, 8), NST:2 * NST]
            return lre * sr - lim * si + br, lre * si + lim * sr + bi

        sr, si = lax.fori_loop(0, tt // 8, step, (sre[...], sim[...]))
        sre[...] = sr
        sim[...] = si

        @pl.when(j == nt - 1)
        def _():
            pr, pi = _pow2k(lre[0:1], lim[0:1], nsq)
            cr = jnp.zeros((1, NST), F32)
            ci = jnp.zeros((1, NST), F32)
            cre_ref[0:1, :] = cr
            cim_ref[0:1, :] = ci
            for k in range(1, 8):
                cr, ci = sr[k - 1:k] + pr * cr - pi * ci, si[k - 1:k] + pr * ci + pi * cr
                cre_ref[k:k + 1, :] = cr
                cim_ref[k:k + 1, :] = ci

    return pl.pallas_call(
        body, name="ssm_local", grid=(nb, nt),
        out_shape=[jax.ShapeDtypeStruct((nb * 8, NST), F32)] * 2,
        in_specs=[pl.BlockSpec((tt, D_SSM), lambda b, j: (b * nt + j, 0)), _VM, _VM, _VM],
        out_specs=[pl.BlockSpec((8, NST), lambda b, j: (b, 0))] * 2,
        scratch_shapes=[pltpu.VMEM((8, NST), F32), pltpu.VMEM((8, NST), F32), pltpu.VMEM((tt, 2 * NST), F32)],
        compiler_params=_cp(("arbitrary", "arbitrary")),
    )(u_p, bm, lre8, lim8)


def _ssm_fwd(u_p, cre, cim, bm, cm, dvec, w_glu, lre8, lim8, S, tt):
    n = u_p.shape[0]
    nb, nt = n // S, S // tt

    def body(u_ref, cre_ref, cim_ref, bm_ref, cm_ref, d_ref, wg_ref, lre_ref, lim_ref,
             st_ref, ypre_ref, z_ref, gact_ref, yssm_ref, sre, sim, bu):
        j = pl.program_id(1)

        @pl.when(j == 0)
        def _():
            sre[...] = cre_ref[...]
            sim[...] = cim_ref[...]

        u = u_ref[...]
        bu[...] = _dot(u.astype(BF16), bm_ref[...])
        lre, lim = lre_ref[...], lim_ref[...]

        def step(i, c):
            sr, si = c
            off = pl.multiple_of(i * 8, 8)
            nr = lre * sr - lim * si + bu[pl.ds(off, 8), 0:NST]
            ni = lre * si + lim * sr + bu[pl.ds(off, 8), NST:2 * NST]
            st_ref[pl.ds(off, 8), 0:NST] = nr
            st_ref[pl.ds(off, 8), NST:2 * NST] = ni
            return nr, ni

        sr, si = lax.fori_loop(0, tt // 8, step, (sre[...], sim[...]))
        sre[...] = sr
        sim[...] = si
        y = _dot(st_ref[...].astype(BF16), cm_ref[...]) + d_ref[...] * u
        ypre_ref[...] = y
        gb = _gelu(y).astype(BF16)
        gact_ref[...] = gb
        z = _dot(gb, wg_ref[...])
        z_ref[...] = z
        yssm_ref[...] = z[:, 0:D_SSM] * _sigmoid(z[:, D_SSM:2 * D_SSM])

    row = lambda w: pl.BlockSpec((tt, w), lambda b, j: (b * nt + j, 0))
    return pl.pallas_call(
        body, name="ssm_fwd", grid=(nb, nt),
        out_shape=[jax.ShapeDtypeStruct((n, 2 * NST), F32), jax.ShapeDtypeStruct((n, D_SSM), F32),
                   jax.ShapeDtypeStruct((n, 2 * D_SSM), F32), jax.ShapeDtypeStruct((n, D_SSM), BF16),
                   jax.ShapeDtypeStruct((n, D_SSM), F32)],
        in_specs=[row(D_SSM), pl.BlockSpec((8, NST), lambda b, j: (b, 0)), pl.BlockSpec((8, NST), lambda b, j: (b, 0)),
                  _VM, _VM, _VM, _VM, _VM, _VM],
        out_specs=[row(2 * NST), row(D_SSM), row(2 * D_SSM), row(D_SSM), row(D_SSM)],
        scratch_shapes=[pltpu.VMEM((8, NST), F32), pltpu.VMEM((8, NST), F32), pltpu.VMEM((tt, 2 * NST), F32)],
        compiler_params=_cp(("arbitrary", "arbitrary")),
    )(u_p, cre, cim, bm, cm, dvec, w_glu, lre8, lim8)


def _ssm_bwd_a(dys_p, z, ypre, w_glu, cm, lre8, lim8, S, tt):
    n = z.shape[0]
    nb, nt = n // S, S // tt
    nsq = int(round(math.log2(S // 8)))
    ng = tt // 8

    def body(dys_ref, z_ref, y_ref, wg_ref, cm_ref, lre_ref, lim_ref, dy_ref, dz_ref, are_ref, aim_ref, sre, sim, gb):
        j = pl.program_id(1)

        @pl.when(j == 0)
        def _():
            sre[...] = jnp.zeros_like(sre)
            sim[...] = jnp.zeros_like(sim)

        z = z_ref[...]
        z1, z2 = z[:, 0:D_SSM], z[:, D_SSM:2 * D_SSM]
        sg = _sigmoid(z2)
        dys = dys_ref[...]
        dz = jnp.concatenate([dys * sg, dys * z1 * sg * (1.0 - sg)], axis=1).astype(BF16)
        dz_ref[...] = dz
        dy = _dot_nt(dz, wg_ref[...]) * _gelu_grad(y_ref[...])
        dy_ref[...] = dy
        gb[...] = _dot_nt(dy.astype(BF16), cm_ref[...])
        lre, lim = lre_ref[...], lim_ref[...]

        def step(i, c):
            ar, ai = c
            off = pl.multiple_of((ng - 1 - i) * 8, 8)
            gr = gb[pl.ds(off, 8), 0:NST]
            gi = gb[pl.ds(off, 8), NST:2 * NST]
            return lre * ar + lim * ai + gr, lre * ai - lim * ar + gi

        ar, ai = lax.fori_loop(0, ng, step, (sre[...], sim[...]))
        sre[...] = ar
        sim[...] = ai

        @pl.when(j == nt - 1)
        def _():
            pr, pi = _pow2k(lre[0:1], -lim[0:1], nsq)
            cr = jnp.zeros((1, NST), F32)
            ci = jnp.zeros((1, NST), F32)
            are_ref[7:8, :] = cr
            aim_ref[7:8, :] = ci
            for k in range(6, -1, -1):
                cr, ci = ar[k + 1:k + 2] + pr * cr - pi * ci, ai[k + 1:k + 2] + pr * ci + pi * cr
                are_ref[k:k + 1, :] = cr
                aim_ref[k:k + 1, :] = ci

    row = lambda w: pl.BlockSpec((tt, w), lambda b, j: (b * nt + nt - 1 - j, 0))
    return pl.pallas_call(
        body, name="ssm_bwd_a", grid=(nb, nt),
        out_shape=[jax.ShapeDtypeStruct((n, D_SSM), F32), jax.ShapeDtypeStruct((n, 2 * D_SSM), BF16),
                   jax.ShapeDtypeStruct((nb * 8, NST), F32), jax.ShapeDtypeStruct((nb * 8, NST), F32)],
        in_specs=[row(D_SSM), row(2 * D_SSM), row(D_SSM), _VM, _VM, _VM, _VM],
        out_specs=[row(D_SSM), row(2 * D_SSM), pl.BlockSpec((8, NST), lambda b, j: (b, 0)),
                   pl.BlockSpec((8, NST), lambda b, j: (b, 0))],
        scratch_shapes=[pltpu.VMEM((8, NST), F32), pltpu.VMEM((8, NST), F32), pltpu.VMEM((tt, 2 * NST), F32)],
        compiler_params=_cp(("arbitrary", "arbitrary")),
    )(dys_p, z, ypre, w_glu, cm, lre8, lim8)


def _ssm_bwd_b(dy, u_p, st, fcr, fci, air, aii, bm, cm, dvec, lre8, lim8, S, tt):
    n = u_p.shape[0]
    nb, nt = n // S, S // tt
    ng = tt // 8
    QB = D_SSM // 4

    def body(dy_ref, u_ref, st_ref, stp_ref, fcr_ref, fci_ref, air_ref, aii_ref, bm_ref, cm_ref, d_ref, lre_ref, lim_ref,
             du_ref, dcm_ref, dbm_ref, dd_ref, dlr_ref, dli_ref, are, aim, accr, acci, sp, ab):
        b = pl.program_id(0)
        j = pl.program_id(1)
        jt = nt - 1 - j

        @pl.when((b == 0) & (j == 0))
        def _():
            dcm_ref[...] = jnp.zeros_like(dcm_ref)
            dbm_ref[...] = jnp.zeros_like(dbm_ref)
            dd_ref[...] = jnp.zeros_like(dd_ref)
            accr[...] = jnp.zeros_like(accr)
            acci[...] = jnp.zeros_like(acci)

        @pl.when(j == 0)
        def _():
            are[...] = air_ref[...]
            aim[...] = aii_ref[...]

        sp[8:tt + 8, :] = st_ref[...]

        @pl.when(jt == 0)
        def _():
            sp[0:8, 0:NST] = fcr_ref[...]
            sp[0:8, NST:2 * NST] = fci_ref[...]

        @pl.when(jt != 0)
        def _():
            sp[0:8, :] = stp_ref[...]

        dy = dy_ref[...]
        u = u_ref[...]
        dyb = dy.astype(BF16)
        ab[...] = _dot_nt(dyb, cm_ref[...])
        lre, lim = lre_ref[...], lim_ref[...]

        def step(i, c):
            ar, ai = c
            off = pl.multiple_of((ng - 1 - i) * 8, 8)
            nr = lre * ar + lim * ai + ab[pl.ds(off, 8), 0:NST]
            ni = lre * ai - lim * ar + ab[pl.ds(off, 8), NST:2 * NST]
            ab[pl.ds(off, 8), 0:NST] = nr
            ab[pl.ds(off, 8), NST:2 * NST] = ni
            pr = sp[pl.ds(off, 8), 0:NST]
            pi = sp[pl.ds(off, 8), NST:2 * NST]
            accr[...] += nr * pr + ni * pi
            acci[...] += ni * pr - nr * pi
            return nr, ni

        ar, ai = lax.fori_loop(0, ng, step, (are[...], aim[...]))
        are[...] = ar
        aim[...] = ai
        a_b = ab[...].astype(BF16)
        du_ref[...] = _dot_nt(a_b, bm_ref[...]) + d_ref[...] * dy
        ub = u.astype(BF16)
        for q in range(4):
            for part in range(2):
                lo = part * NST + q * 4 * QB
                s_q = sp[8:tt + 8, lo:lo + 4 * QB].astype(BF16)
                dcm_ref[lo:lo + 4 * QB, :] += _dot_tn(s_q, dyb[:, q * QB:(q + 1) * QB])
                dbm_ref[:, lo:lo + 4 * QB] += _dot_tn(ub[:, q * QB:(q + 1) * QB], a_b[:, lo:lo + 4 * QB])
        dd_ref[...] += _colsum(dy * u)

        @pl.when((b == nb - 1) & (j == nt - 1))
        def _():
            dlr_ref[...] = _colsum(accr[...])
            dli_ref[...] = _colsum(acci[...])

    row = lambda w: pl.BlockSpec((tt, w), lambda b, j: (b * nt + nt - 1 - j, 0))
    seq8 = pl.BlockSpec((8, NST), lambda b, j: (b, 0))
    prev = pl.BlockSpec((8, 2 * NST), lambda b, j: (jnp.maximum((b * nt + nt - 1 - j) * ng - 1, 0), 0))
    const = lambda shape: pl.BlockSpec(shape, lambda b, j: (0, 0))
    return pl.pallas_call(
        body, name="ssm_bwd_b", grid=(nb, nt),
        out_shape=[jax.ShapeDtypeStruct((n, D_SSM), F32), jax.ShapeDtypeStruct((2 * NST, QB), F32),
                   jax.ShapeDtypeStruct((QB, 2 * NST), F32), jax.ShapeDtypeStruct((1, D_SSM), F32),
                   jax.ShapeDtypeStruct((1, NST), F32), jax.ShapeDtypeStruct((1, NST), F32)],
        in_specs=[row(D_SSM), row(D_SSM), row(2 * NST), prev, seq8, seq8, seq8, seq8, _VM, _VM, _VM, _VM, _VM],
        out_specs=[row(D_SSM), const((2 * NST, QB)), const((QB, 2 * NST)), const((1, D_SSM)),
                   const((1, NST)), const((1, NST))],
        scratch_shapes=[pltpu.VMEM((8, NST), F32)] * 4 + [pltpu.VMEM((tt + 8, 2 * NST), F32),
                                                          pltpu.VMEM((tt, 2 * NST), F32)],
        compiler_params=_cp(("arbitrary", "arbitrary")),
    )(dy, u_p, st, st, fcr, fci, air, aii, bm, cm, dvec, lre8, lim8)


def _rope(v, c, s1, s2):
    return v * c + _roll(v, -16) * s1 + _roll(v, 16) * s2


def _rope_t(dv, c, s1, s2):
    return dv * c + _roll(dv * s1, 16) + _roll(dv * s2, -16)


def _mla_fwd(proj, rc, rs1, rs2, gq, gkv, w_uq, w_ukv, tm):
    n = proj.shape[0]

    def body(ql_ref, kvl_ref, kr_ref, c_ref, s1_ref, s2_ref, gq_ref, gkv_ref, wq_ref, wkv_ref,
             q_ref, k_ref, v_ref, qn_ref, kvn_ref):
        c, s1, s2 = c_ref[...], s1_ref[...], s2_ref[...]
        qhat, _ = _rms(ql_ref[...], Q_LORA)
        qn = (qhat * gq_ref[...]).astype(BF16)
        qn_ref[...] = qn
        q = _dot(qn, wq_ref[...])
        q_ref[...] = _rope(q, jnp.tile(c, (1, NH)), jnp.tile(s1, (1, NH)), jnp.tile(s2, (1, NH))).astype(BF16)
        khat, _ = _rms(kvl_ref[...], KV_LORA)
        kvn = (khat * gkv_ref[...]).astype(BF16)
        kvn_ref[...] = kvn
        kv = _dot(kvn, wkv_ref[...])
        kr = _rope(_roll(kr_ref[...], 64), c, s1, s2)
        k_ref[...] = (kv[:, 0:NH * HP] + jnp.tile(kr, (1, NH))).astype(BF16)
        v_ref[...] = kv[:, NH * HP:2 * NH * HP].astype(BF16)

    def wrapped(proj_ref, *rest):
        ql = proj_ref.at[:, D_SSM:D_SSM + Q_LORA]
        kvl = proj_ref.at[:, D_SSM + Q_LORA:D_SSM + Q_LORA + KV_LORA]
        kr = proj_ref.at[:, IN_PAD - HP:IN_PAD]
        body(ql, kvl, kr, *rest)

    row = lambda w: pl.BlockSpec((tm, w), lambda i: (i, 0))
    return pl.pallas_call(
        wrapped, name="mla_fwd", grid=(n // tm,),
        out_shape=[jax.ShapeDtypeStruct((n, NH * HP), BF16)] * 3 +
                  [jax.ShapeDtypeStruct((n, Q_LORA), BF16), jax.ShapeDtypeStruct((n, KV_LORA), BF16)],
        in_specs=[row(IN_PAD), row(HP), row(HP), row(HP), _VM, _VM, _VM, _VM],
        out_specs=[row(NH * HP)] * 3 + [row(Q_LORA), row(KV_LORA)],
        compiler_params=_cp(("parallel",)),
    )(proj, rc, rs1, rs2, gq, gkv, w_uq, w_ukv)


def _mla_bwd(dq, dk, dv, proj, rc, rs1, rs2, gq, gkv, w_uq, w_ukv, tm):
    n = proj.shape[0]

    def body(dq_ref, dk_ref, dv_ref, proj_ref, c_ref, s1_ref, s2_ref, gq_ref, gkv_ref, wq_ref, wkv_ref,
             dmla_ref, dqb_ref, dkvb_ref, acc_ref):
        i = pl.program_id(0)
        c, s1, s2 = c_ref[...], s1_ref[...], s2_ref[...]
        dqu = _rope_t(dq_ref[...], jnp.tile(c, (1, NH)), jnp.tile(s1, (1, NH)), jnp.tile(s2, (1, NH))).astype(BF16)
        dqb_ref[...] = dqu
        dqn = _dot_nt(dqu, wq_ref[...])
        qhat, rq = _rms(proj_ref[:, D_SSM:D_SSM + Q_LORA], Q_LORA)
        dql = _rms_bwd(dqn * gq_ref[...], qhat, rq, Q_LORA)
        dkf = dk_ref[...]
        dkv = jnp.concatenate([dkf, dv_ref[...]], axis=1).astype(BF16)
        dkvb_ref[...] = dkv
        dkvn = _dot_nt(dkv, wkv_ref[...])
        khat, rk = _rms(proj_ref[:, D_SSM + Q_LORA:D_SSM + Q_LORA + KV_LORA], KV_LORA)
        dkvl = _rms_bwd(dkvn * gkv_ref[...], khat, rk, KV_LORA)
        dkr = dkf[:, 0:HP]
        for h in range(1, NH):
            dkr = dkr + dkf[:, h * HP:(h + 1) * HP]
        lane = lax.broadcasted_iota(jnp.int32, dkr.shape, 1)
        dkr = jnp.where((lane >= QK_NOPE) & (lane < QK_NOPE + QK_ROPE), dkr, 0.0)
        dkr = _roll(_rope_t(dkr, c, s1, s2), -64)
        dmla_ref[...] = jnp.concatenate([dql, dkvl, dkr], axis=1)

        @pl.when(i == 0)
        def _():
            acc_ref[...] = jnp.zeros_like(acc_ref)

        acc_ref[0:1, 0:Q_LORA] += _colsum(dqn * qhat)
        acc_ref[1:2, 0:KV_LORA] += _colsum(dkvn * khat)

    row = lambda w: pl.BlockSpec((tm, w), lambda i: (i, 0))
    return pl.pallas_call(
        body, name="mla_bwd", grid=(n // tm,),
        out_shape=[jax.ShapeDtypeStruct((n, IN_PAD - D_SSM), F32), jax.ShapeDtypeStruct((n, NH * HP), BF16),
                   jax.ShapeDtypeStruct((n, 2 * NH * HP), BF16), jax.ShapeDtypeStruct((8, Q_LORA), F32)],
        in_specs=[row(NH * HP)] * 3 + [row(IN_PAD), row(HP), row(HP), row(HP), _VM, _VM, _VM, _VM],
        out_specs=[row(IN_PAD - D_SSM), row(NH * HP), row(2 * NH * HP), pl.BlockSpec((8, Q_LORA), lambda i: (0, 0))],
        compiler_params=_cp(("arbitrary",)),
    )(dq, dk, dv, proj, rc, rs1, rs2, gq, gkv, w_uq, w_ukv)


_SCALE = (QK_NOPE + QK_ROPE) ** -0.5


def _causal(s, row0, col0):
    rows = row0 + lax.broadcasted_iota(jnp.int32, s.shape, 0)
    cols = col0 + lax.broadcasted_iota(jnp.int32, s.shape, 1)
    return cols <= rows


def _attn_fwd(q, k, v, S, tq):
    n = q.shape[0]
    nb, nq = n // S, S // tq

    def body(q_ref, k_ref, v_ref, o_ref, lse_ref):
        qi = pl.program_id(2)
        qv = q_ref[...]

        def step(j, c):
            m, l, acc = c
            off = pl.multiple_of(j * tq, tq)
            s = _dot_nt(qv, k_ref[pl.ds(off, tq), :]) * _SCALE
            s = jnp.where(_causal(s, qi * tq, off), s, NEG)
            mn = jnp.maximum(m, jnp.max(s, axis=1, keepdims=True))
            p = jnp.exp(s - mn)
            al = jnp.exp(m - mn)
            return mn, al * l + jnp.sum(p, axis=1, keepdims=True), al * acc + _dot(p.astype(BF16), v_ref[pl.ds(off, tq), :])

        m, l, acc = lax.fori_loop(0, qi + 1, step, (jnp.full((tq, 1), NEG, F32), jnp.zeros((tq, 1), F32),
                                                    jnp.zeros((tq, HP), F32)))
        o_ref[...] = acc / l
        lse_ref[...] = jnp.broadcast_to(m + jnp.log(l), (tq, HP))

    qs = pl.BlockSpec((tq, HP), lambda b, h, i: (b * nq + i, h))
    ks = pl.BlockSpec((S, HP), lambda b, h, i: (b, h))
    return pl.pallas_call(
        body, name="attn_fwd", grid=(nb, NH, nq),
        out_shape=[jax.ShapeDtypeStruct((n, NH * HP), F32)] * 2,
        in_specs=[qs, ks, ks], out_specs=[qs, qs],
        compiler_params=_cp(("parallel", "parallel", "arbitrary")),
    )(q, k, v)


def _attn_bwd_dq(q, k, v, o, do, lse, S, tq):
    n = q.shape[0]
    nb, nq = n // S, S // tq

    def body(q_ref, k_ref, v_ref, o_ref, do_ref, lse_ref, dq_ref):
        qi = pl.program_id(2)
        qv = q_ref[...]
        dov = do_ref[...]
        dob = dov.astype(BF16)
        delta = jnp.sum(dov * o_ref[...], axis=1, keepdims=True)
        lse = lse_ref[:, 0:1]

        def step(j, acc):
            off = pl.multiple_of(j * tq, tq)
            kt = k_ref[pl.ds(off, tq), :]
            s = _dot_nt(qv, kt) * _SCALE
            p = jnp.where(_causal(s, qi * tq, off), jnp.exp(s - lse), 0.0)
            dp = _dot_nt(dob, v_ref[pl.ds(off, tq), :])
            ds = (p * (dp - delta) * _SCALE).astype(BF16)
            return acc + _dot(ds, kt)

        dq_ref[...] = lax.fori_loop(0, qi + 1, step, jnp.zeros((tq, HP), F32))

    qs = pl.BlockSpec((tq, HP), lambda b, h, i: (b * nq + i, h))
    ks = pl.BlockSpec((S, HP), lambda b, h, i: (b, h))
    return pl.pallas_call(
        body, name="attn_bwd_dq", grid=(nb, NH, nq),
        out_shape=jax.ShapeDtypeStruct((n, NH * HP), F32),
        in_specs=[qs, ks, ks, qs, qs, qs], out_specs=qs,
        compiler_params=_cp(("parallel", "parallel", "arbitrary")),
    )(q, k, v, o, do, lse)


def _attn_bwd_dkv(q, k, v, o, do, lse, S, tq):
    n = q.shape[0]
    nb, nq = n // S, S // tq

    def body(q_ref, k_ref, v_ref, o_ref, do_ref, lse_ref, dk_ref, dv_ref):
        kj = pl.program_id(2)
        kt = k_ref[...]
        vt = v_ref[...]

        def step(i, c):
            dk, dv = c
            off = pl.multiple_of(i * tq, tq)
            qv = q_ref[pl.ds(off, tq), :]
            dov = do_ref[pl.ds(off, tq), :]
            dob = dov.astype(BF16)
            delta = jnp.sum(dov * o_ref[pl.ds(off, tq), :], axis=1, keepdims=True)
            s = _dot_nt(qv, kt) * _SCALE
            p = jnp.where(_causal(s, off, kj * tq), jnp.exp(s - lse_ref[pl.ds(off, tq), 0:1]), 0.0)
            dv = dv + _dot_tn(p.astype(BF16), dob)
            dp = _dot_nt(dob, vt)
            ds = (p * (dp - delta) * _SCALE).astype(BF16)
            return dk + _dot_tn(ds, qv), dv

        dk, dv = lax.fori_loop(kj, nq, step, (jnp.zeros((tq, HP), F32), jnp.zeros((tq, HP), F32)))
        dk_ref[...] = dk
        dv_ref[...] = dv

    ts = pl.BlockSpec((tq, HP), lambda b, h, i: (b * nq + i, h))
    fs = pl.BlockSpec((S, HP), lambda b, h, i: (b, h))
    return pl.pallas_call(
        body, name="attn_bwd_dkv", grid=(nb, NH, nq),
        out_shape=[jax.ShapeDtypeStruct((n, NH * HP), F32)] * 2,
        in_specs=[fs, ts, ts, fs, fs, fs], out_specs=[ts, ts],
        compiler_params=_cp(("parallel", "parallel", "arbitrary")),
    )(q, k, v, o, do, lse)


def _p1_fwd(yssm, oattn, x, modp, gs, ga, w_out, g2, S, tm):
    n = x.shape[0]
    tps = S // tm

    def body(ys_ref, oa_ref, x_ref, mod_ref, gs_ref, ga_ref, w_ref, g2_ref, yn_ref, o_ref, x1_ref, h2_ref):
        yh, _ = _rms(ys_ref[...], D_SSM)
        ah, _ = _rms(oa_ref[...], D_ATTN)
        yn = jnp.concatenate([yh * gs_ref[...], ah * ga_ref[...]], axis=1).astype(BF16)
        yn_ref[...] = yn
        o = _dot(yn, w_ref[...])
        o_ref[...] = o
        x1 = x_ref[...] + mod_ref[0, 2:3, :] * o
        x1_ref[...] = x1
        xh, _ = _rms(x1, D)
        h2_ref[...] = ((xh * g2_ref[...]) * (1.0 + mod_ref[0, 4:5, :]) + mod_ref[0, 3:4, :]).astype(BF16)

    row = lambda w: pl.BlockSpec((tm, w), lambda i: (i, 0))
    return pl.pallas_call(
        body, name="p1_fwd", grid=(n // tm,),
        out_shape=[jax.ShapeDtypeStruct((n, D_SSM + NH * HP), BF16), jax.ShapeDtypeStruct((n, D), F32),
                   jax.ShapeDtypeStruct((n, D), F32), jax.ShapeDtypeStruct((n, D), BF16)],
        in_specs=[row(D_SSM), row(NH * HP), row(D), pl.BlockSpec((1, 8, D), lambda i: (i // tps, 0, 0)),
                  _VM, _VM, _VM, _VM],
        out_specs=[row(D_SSM + NH * HP), row(D), row(D), row(D)],
        compiler_params=_cp(("parallel",)),
    )(yssm, oattn, x, modp, gs, ga, w_out, g2)


def _p2(x1, h2, target, modp, g2, gf, w_ff1, w_ff2, S, tm):
    n = x1.shape[0]
    tps = S // tm
    nb = n // S

    def body(x1_ref, h2_ref, t_ref, mod_ref, g2_ref, gf_ref, w1_ref, w2_ref,
             dx1_ref, r_ref, da_ref, dff_ref, accs_ref, accg_ref):
        i = pl.program_id(0)
        sh2, sc2, gt2 = mod_ref[0, 3:4, :], mod_ref[0, 4:5, :], mod_ref[0, 5:6, :]
        fsh, fsc = mod_ref[0, 6:7, :], mod_ref[0, 7:8, :]
        x1 = x1_ref[...]
        a = _dot(h2_ref[...], w1_ref[...])
        ra = jnp.maximum(a, 0.0)
        rb = (ra * ra).astype(BF16)
        r_ref[...] = rb
        ff = _dot(rb, w2_ref[...])
        x2 = x1 + gt2 * ff
        x2h, rf = _rms(x2, D)
        gf_v = gf_ref[...]
        outn = x2h * gf_v
        err = outn * (1.0 + fsc) + fsh - t_ref[...]
        dout = err * (1.0 / D)
        doutn = dout * (1.0 + fsc)
        dx2 = _rms_bwd(doutn * gf_v, x2h, rf, D)
        dff = (gt2 * dx2).astype(BF16)
        dff_ref[...] = dff
        dr = _dot_nt(dff, w2_ref[...])
        da = (dr * (2.0 * ra)).astype(BF16)
        da_ref[...] = da
        dh2 = _dot_nt(da, w1_ref[...])
        x1h, r2 = _rms(x1, D)
        g2_v = g2_ref[...]
        dn2 = dh2 * (1.0 + sc2)
        dx1_ref[...] = dx2 + _rms_bwd(dn2 * g2_v, x1h, r2, D)

        @pl.when(i % tps == 0)
        def _():
            accs_ref[...] = jnp.zeros_like(accs_ref)

        @pl.when(i == 0)
        def _():
            accg_ref[...] = jnp.zeros_like(accg_ref)

        accs_ref[0, 3:4, :] += _colsum(dh2)
        accs_ref[0, 4:5, :] += _colsum(dh2 * (x1h * g2_v))
        accs_ref[0, 5:6, :] += _colsum(dx2 * ff)
        accs_ref[0, 6:7, :] += _colsum(dout)
        accs_ref[0, 7:8, :] += _colsum(dout * outn)
        accg_ref[0:1, :] += _colsum(dn2 * x1h)
        accg_ref[1:2, :] += _colsum(doutn * x2h)
        accg_ref[2:3, :] += _colsum(err * err) * (0.5 / D)

    row = lambda w: pl.BlockSpec((tm, w), lambda i: (i, 0))
    return pl.pallas_call(
        body, name="p2_mlp_loss", grid=(n // tm,),
        out_shape=[jax.ShapeDtypeStruct((n, D), F32), jax.ShapeDtypeStruct((n, D_FF), BF16),
                   jax.ShapeDtypeStruct((n, D_FF), BF16), jax.ShapeDtypeStruct((n, D), BF16),
                   jax.ShapeDtypeStruct((nb, 8, D), F32), jax.ShapeDtypeStruct((8, D), F32)],
        in_specs=[row(D), row(D), row(D), pl.BlockSpec((1, 8, D), lambda i: (i // tps, 0, 0)), _VM, _VM, _VM, _VM],
        out_specs=[row(D), row(D_FF), row(D_FF), row(D), pl.BlockSpec((1, 8, D), lambda i: (i // tps, 0, 0)),
                   pl.BlockSpec((8, D), lambda i: (0, 0))],
        compiler_params=_cp(("arbitrary",)),
    )(x1, h2, target, modp, g2, gf, w_ff1, w_ff2)


def _p3_bwd(dx1, o, yssm, oattn, modp, gs, ga, w_out, S, tm):
    n = dx1.shape[0]
    tps = S // tm
    nb = n // S

    def body(dx1_ref, o_ref, ys_ref, oa_ref, mod_ref, gs_ref, ga_ref, w_ref,
             do_ref, dys_ref, doa_ref, accs_ref, accg_ref):
        i = pl.program_id(0)
        dx1 = dx1_ref[...]
        dob = (mod_ref[0, 2:3, :] * dx1).astype(BF16)
        do_ref[...] = dob
        dyn = _dot_nt(dob, w_ref[...])
        yh, rs = _rms(ys_ref[...], D_SSM)
        ah, ra = _rms(oa_ref[...], D_ATTN)
        d1 = dyn[:, 0:D_SSM]
        d2 = dyn[:, D_SSM:D_SSM + NH * HP]
        dys_ref[...] = _rms_bwd(d1 * gs_ref[...], yh, rs, D_SSM)
        doa_ref[...] = _rms_bwd(d2 * ga_ref[...], ah, ra, D_ATTN)

        @pl.when(i % tps == 0)
        def _():
            accs_ref[...] = jnp.zeros_like(accs_ref)

        @pl.when(i == 0)
        def _():
            accg_ref[...] = jnp.zeros_like(accg_ref)

        accs_ref[0, 2:3, :] += _colsum(dx1 * o_ref[...])
        accg_ref[0:1, 0:D_SSM] += _colsum(d1 * yh)
        accg_ref[1:2, :] += _colsum(d2 * ah)

    row = lambda w: pl.BlockSpec((tm, w), lambda i: (i, 0))
    return pl.pallas_call(
        body, name="p3_bwd", grid=(n // tm,),
        out_shape=[jax.ShapeDtypeStruct((n, D), BF16), jax.ShapeDtypeStruct((n, D_SSM), F32),
                   jax.ShapeDtypeStruct((n, NH * HP), F32), jax.ShapeDtypeStruct((nb, 8, D), F32),
                   jax.ShapeDtypeStruct((8, NH * HP), F32)],
        in_specs=[row(D), row(D), row(D_SSM), row(NH * HP), pl.BlockSpec((1, 8, D), lambda i: (i // tps, 0, 0)),
                  _VM, _VM, _VM],
        out_specs=[row(D), row(D_SSM), row(NH * HP), pl.BlockSpec((1, 8, D), lambda i: (i // tps, 0, 0)),
                   pl.BlockSpec((8, NH * HP), lambda i: (0, 0))],
        compiler_params=_cp(("arbitrary",)),
    )(dx1, o, yssm, oattn, modp, gs, ga, w_out)


def _wgrad(a, b, name, col_slots=0):
    n, k1 = a.shape
    k2 = b.shape[1]
    bn = 512 if n % 512 == 0 else n
    bk1 = 512 if k1 % 512 == 0 else k1
    bk2 = k2 // col_slots if col_slots else (1024 if (k2 % 1024 == 0) else k2)

    def body(a_ref, b_ref, o_ref):
        @pl.when(pl.program_id(2) == 0)
        def _():
            o_ref[...] = jnp.zeros_like(o_ref)

        o_ref[...] += _dot_tn(a_ref[...], b_ref[...]).reshape(o_ref.shape)

    if col_slots:
        out_shape = jax.ShapeDtypeStruct((col_slots, k1, bk2), F32)
        out_spec = pl.BlockSpec((1, bk1, bk2), lambda i, j, t: (j, i, 0))
    else:
        out_shape = jax.ShapeDtypeStruct((k1, k2), F32)
        out_spec = pl.BlockSpec((bk1, bk2), lambda i, j, t: (i, j))
    return pl.pallas_call(
        body, name=name, grid=(k1 // bk1, k2 // bk2, n // bn),
        out_shape=out_shape,
        in_specs=[pl.BlockSpec((bn, bk1), lambda i, j, t: (t, i)), pl.BlockSpec((bn, bk2), lambda i, j, t: (t, j))],
        out_specs=out_spec,
        compiler_params=_cp(("parallel", "parallel", "arbitrary")),
    )(a, b)


def _row_block(rows):
    if rows <= 256:
        return rows
    return next(b for b in (256, 192, 128, 64, 32, 16, 8) if rows % b == 0)


def _add_half(g, recv, cidx, name):
    _, rows2, w = g.shape
    rows = rows2 // 2
    br = _row_block(rows)
    nblk = rows // br

    def body(c_ref, g_ref, r_ref, o_ref):
        o_ref[...] = (g_ref[...] + r_ref[...]).astype(BF16)

    return pl.pallas_call(
        body, name=name,
        grid_spec=pltpu.PrefetchScalarGridSpec(
            num_scalar_prefetch=1, grid=(4, nblk),
            in_specs=[pl.BlockSpec((1, br, w), lambda s, i, c: (s, c[0] * nblk + i, 0)),
                      pl.BlockSpec((1, br, w), lambda s, i, c: (s, i, 0))],
            out_specs=pl.BlockSpec((1, br, w), lambda s, i, c: (s, i, 0))),
        out_shape=jax.ShapeDtypeStruct((4, rows, w), BF16),
        compiler_params=_cp(("parallel", "parallel")),
    )(cidx, g, recv)


def _add_chips(r, name):
    _, rows, w = r.shape
    br = _row_block(rows)

    def body(r_ref, o_ref):
        f = lambda k: r_ref[k].astype(F32)
        o_ref[...] = ((f(0) + f(1)) + f(2)) + f(3)

    return pl.pallas_call(
        body, name=name, grid=(rows // br,),
        out_shape=jax.ShapeDtypeStruct((rows, w), F32),
        in_specs=[pl.BlockSpec((4, br, w), lambda i: (0, i, 0))],
        out_specs=pl.BlockSpec((br, w), lambda i: (i, 0)),
        compiler_params=_cp(("parallel",)),
    )(r)


def _sum_devices(a):
    def body(a_ref, o_ref):
        acc = a_ref[0:1, :]
        for k in range(1, 8):
            acc = acc + a_ref[k:k + 1, :]
        o_ref[...] = acc

    return pl.pallas_call(
        body, name="small_grad_sum", out_shape=jax.ShapeDtypeStruct((1, a.shape[1]), F32),
        in_specs=[_VM], out_specs=_VM, compiler_params=_cp(),
    )(a)


def _adamw_math(wv, gv, mv, vv):
    m_new = ADAM_B1 * mv + (1.0 - ADAM_B1) * gv
    v_new = ADAM_B2 * vv + (1.0 - ADAM_B2) * (gv * gv)
    m_hat = m_new / (1.0 - ADAM_B1 ** ADAM_STEP)
    v_hat = v_new / (1.0 - ADAM_B2 ** ADAM_STEP)
    return -ADAM_LR * (m_hat / (jnp.sqrt(v_hat) + ADAM_EPS) + ADAM_WD * wv), m_new, v_new


def _adamw_small(ws, gs, ms, vs):
    k = len(ws)

    def body(*refs):
        ins, outs = refs[:4 * k], refs[4 * k:]
        for t in range(k):
            d, m_new, v_new = _adamw_math(ins[t][...], ins[k + t][...], ins[2 * k + t][...], ins[3 * k + t][...])
            outs[t][...] = d
            outs[k + t][...] = m_new
            outs[2 * k + t][...] = v_new

    shapes = [jax.ShapeDtypeStruct(w.shape, F32) for w in ws]
    return pl.pallas_call(
        body, name="adamw_small", out_shape=shapes * 3,
        in_specs=[_VM] * (4 * k), out_specs=[_VM] * (3 * k), compiler_params=_cp(),
    )(*ws, *gs, *ms, *vs)


def _adamw(w, g, m, v, name):
    rows, wd = w.shape
    br = _row_block(rows)

    def body(w_ref, g_ref, m_ref, v_ref, d_ref, nm_ref, nv_ref):
        d, m_new, v_new = _adamw_math(w_ref[...], g_ref[...], m_ref[...], v_ref[...])
        d_ref[...] = d
        nm_ref[...] = m_new
        nv_ref[...] = v_new

    spec = pl.BlockSpec((br, wd), lambda i: (i, 0))
    return pl.pallas_call(
        body, name=name, grid=(rows // br,),
        out_shape=[jax.ShapeDtypeStruct((rows, wd), F32)] * 3,
        in_specs=[spec] * 4, out_specs=[spec] * 3,
        compiler_params=_cp(("parallel",)),
    )(w, g, m, v)


def _adamw_halves(w, mine, other, m, v, cidx, name):
    rows, wd = w.shape
    h = rows // 2
    br = _row_block(h)
    nblk = h // br

    def body(c_ref, w_ref, a_ref, b_ref, m_ref, v_ref, g_ref, d_ref, nm_ref, nv_ref):
        gv = jnp.where(pl.program_id(0) == c_ref[0], a_ref[...], b_ref[...])
        d, m_new, v_new = _adamw_math(w_ref[...], gv, m_ref[...], v_ref[...])
        g_ref[...] = gv
        d_ref[...] = d
        nm_ref[...] = m_new
        nv_ref[...] = v_new

    full = pl.BlockSpec((br, wd), lambda hf, i, c: (hf * nblk + i, 0))
    half = pl.BlockSpec((br, wd), lambda hf, i, c: (i, 0))
    return pl.pallas_call(
        body, name=name,
        grid_spec=pltpu.PrefetchScalarGridSpec(
            num_scalar_prefetch=1, grid=(2, nblk),
            in_specs=[full, half, half, full, full], out_specs=[full] * 4),
        out_shape=[jax.ShapeDtypeStruct((rows, wd), F32)] * 4,
        compiler_params=_cp(("parallel", "parallel")),
    )(cidx, w, mine, other, m, v)


def _other_chips(x, y):
    return [(1 - x, y), (x, 1 - y), (1 - x, 1 - y)]


def _other_devices(x, y, c):
    flip = lambda v, d: (1 - v) if d else v
    return [(flip(x, dx), flip(y, dy), flip(c, dc))
            for dx in (0, 1) for dy in (0, 1) for dc in (0, 1) if (dx, dy, dc) != (0, 0, 0)]


def _exchange(name, ins, out_shapes, n_local, n_remote, plan):
    ni, no = len(ins), len(out_shapes)

    def body(*refs):
        in_refs, out_refs = refs[:ni], refs[ni:ni + no]
        send_sems, recv_sems, local_sems = refs[ni + no:]
        x, y, c = lax.axis_index("x"), lax.axis_index("y"), lax.axis_index("c")
        local, remote = plan(in_refs, out_refs, x, y, c)
        assert len(local) == n_local and len(remote) == n_remote

        def push(k, src, dst, dev):
            return pltpu.make_async_remote_copy(src_ref=src, dst_ref=dst, send_sem=send_sems.at[k],
                                                recv_sem=recv_sems.at[k], device_id=dev, device_id_type=MESH)

        own = [pltpu.make_async_copy(s, d, local_sems.at[i]) for i, (s, d) in enumerate(local)]
        for cp in own:
            cp.start()
        sends = [push(k, s, d, dev) for k, (s, d, dev, _) in enumerate(remote)]
        for cp in sends:
            cp.start()
        for k, (s, _, dev, landing) in enumerate(remote):
            push(k, s, landing, dev).wait_recv()
        for cp in sends:
            cp.wait_send()
        for cp in own:
            cp.wait()

    return pl.pallas_call(
        body, name=name, out_shape=out_shapes,
        in_specs=[_ANY] * ni, out_specs=[_ANY] * no,
        scratch_shapes=[pltpu.SemaphoreType.DMA((n_remote,)), pltpu.SemaphoreType.DMA((n_remote,)),
                        pltpu.SemaphoreType.DMA((max(n_local, 1),))],
        compiler_params=pltpu.CompilerParams(has_side_effects=True),
    )(*ins)


def _gather_chips(name, shards, everyone=()):
    ns, ne = len(shards), len(everyone)
    outs = [jax.ShapeDtypeStruct((4,) + a.shape, a.dtype) for a in shards]
    outs += [jax.ShapeDtypeStruct((8,) + a.shape, a.dtype) for a in everyone]

    def plan(i, o, x, y, c):
        mine, me = 2 * x + y, 4 * x + 2 * y + c
        local, remote = [], []
        for t in range(ns):
            local.append((i[t], o[t].at[mine]))
            for px, py in _other_chips(x, y):
                remote.append((i[t], o[t].at[mine], (px, py, c), o[t].at[2 * px + py]))
        for t in range(ns, ns + ne):
            local.append((i[t], o[t].at[me]))
            for px, py, pc in _other_devices(x, y, c):
                remote.append((i[t], o[t].at[me], (px, py, pc), o[t].at[4 * px + 2 * py + pc]))
        return local, remote

    return _exchange(name, list(shards) + list(everyone), outs, ns + ne, 3 * ns + 7 * ne, plan)


def _swap_halves(gs, everyone):
    ns, ne = len(gs), len(everyone)
    outs = [jax.ShapeDtypeStruct((4, g.shape[1] // 2, g.shape[2]), g.dtype) for g in gs]
    outs += [jax.ShapeDtypeStruct((8,) + a.shape, a.dtype) for a in everyone]

    def plan(i, o, x, y, c):
        me = 4 * x + 2 * y + c
        local, remote = [], []
        for t in range(ns):
            h = gs[t].shape[1] // 2
            theirs = i[t].at[:, pl.ds(pl.multiple_of((1 - c) * h, 8), h), :]
            remote.append((theirs, o[t], (x, y, 1 - c), o[t]))
        for t in range(ns, ns + ne):
            local.append((i[t], o[t].at[me]))
            for px, py, pc in _other_devices(x, y, c):
                remote.append((i[t], o[t].at[me], (px, py, pc), o[t].at[4 * px + 2 * py + pc]))
        return local, remote

    return _exchange("grad_swap_sibling", list(gs) + list(everyone), outs, ne, ns + 7 * ne, plan)


def _scatter_chips(parts):
    ns = len(parts)
    outs = [jax.ShapeDtypeStruct(a.shape, a.dtype) for a in parts]

    def plan(i, o, x, y, c):
        mine = 2 * x + y
        local, remote = [], []
        for t in range(ns):
            local.append((i[t].at[mine], o[t].at[mine]))
            for px, py in _other_chips(x, y):
                remote.append((i[t].at[2 * px + py], o[t].at[mine], (px, py, c), o[t].at[2 * px + py]))
        return local, remote

    return _exchange("grad_scatter_chips", list(parts), outs, ns, 3 * ns, plan)


def _join_halves(halves):
    ns = len(halves)
    outs = [jax.ShapeDtypeStruct(a.shape, a.dtype) for a in halves]

    def plan(i, o, x, y, c):
        return [], [(i[t], o[t], (x, y, 1 - c), o[t]) for t in range(ns)]

    return _exchange("grad_join_sibling", list(halves), outs, 0, ns, plan)


def _pad_heads_cols(w, per, used):
    k = w.shape[0]
    w = w.reshape(k, NH, per)[:, :, :used]
    return jnp.pad(w, ((0, 0), (0, 0), (0, HP - used))).reshape(k, NH * HP)


def _unpad_heads_cols(w, used):
    k = w.shape[0]
    return w.reshape(k, NH, HP)[:, :, :used]


def _prep_weights(wf):
    bf = lambda a: a.astype(BF16)
    out = {}
    out["w_in"] = jnp.pad(bf(wf["w_in"]), ((0, 0), (0, IN_PAD - IN_COLS)))
    out["w_glu"] = bf(wf["w_glu"])
    out["w_uq"] = _pad_heads_cols(bf(wf["w_uq"]), QK_NOPE + QK_ROPE, QK_NOPE + QK_ROPE)
    wkv = bf(wf["w_ukv"]).reshape(KV_LORA, NH, QK_NOPE + V_HEAD)
    wk = jnp.pad(wkv[:, :, :QK_NOPE], ((0, 0), (0, 0), (0, HP - QK_NOPE))).reshape(KV_LORA, NH * HP)
    wv = jnp.pad(wkv[:, :, QK_NOPE:], ((0, 0), (0, 0), (0, HP - V_HEAD))).reshape(KV_LORA, NH * HP)
    out["w_ukv"] = jnp.concatenate([wk, wv], axis=1)
    wo = bf(wf["w_out"])
    wo_a = jnp.pad(wo[D_SSM:].reshape(NH, V_HEAD, D), ((0, 0), (0, HP - V_HEAD), (0, 0))).reshape(NH * HP, D)
    out["w_out"] = jnp.concatenate([wo[:D_SSM], wo_a], axis=0)
    out["w_ff1"] = bf(wf["w_ff1"])
    out["w_ff2"] = bf(wf["w_ff2"])
    return out


def _rope_tables(positions):
    inv_freq = ROPE_BASE ** (-jnp.arange(0, QK_ROPE, 2, dtype=F32) / QK_ROPE)
    ang = positions.astype(F32)[:, None] * inv_freq
    cos, sin = jnp.cos(ang), jnp.sin(ang)
    n = positions.shape[0]
    one = jnp.ones((n, QK_NOPE), F32)
    z16 = jnp.zeros((n, 16), F32)
    z32 = jnp.zeros((n, 32), F32)
    z64 = jnp.zeros((n, QK_NOPE), F32)
    rc = jnp.concatenate([one, cos, cos, z32], axis=1)
    rs1 = jnp.concatenate([z64, -sin, z16, z32], axis=1)
    rs2 = jnp.concatenate([z64, z16, sin, z32], axis=1)
    return rc, rs1, rs2


def _permute_rows(a, S):
    n, w = a.shape
    return a.reshape(n // S, 8, S // 8, w).transpose(0, 2, 1, 3).reshape(n, w)


def _unpermute_rows(a, S):
    n, w = a.shape
    return a.reshape(n // S, S // 8, 8, w).transpose(0, 2, 1, 3).reshape(n, w)


def _block_diag_in(bb):
    eye = jnp.eye(G, dtype=bb.dtype)
    return jnp.einsum("gph,gk->ghkp", bb, eye).reshape(G * H, G * P)


def _block_diag_out(cc):
    eye = jnp.eye(G, dtype=cc.dtype)
    return jnp.einsum("ghp,gk->gpkh", cc, eye).reshape(G * P, G * H)


def _slots(full):
    r, cdim = full.shape
    return full.reshape(r, 4, cdim // 4).transpose(1, 0, 2)


def _unslots(g):
    s, r, cs = g.shape
    return g.transpose(1, 0, 2).reshape(r, s * cs)


def _local_step(x, positions, target, modp, wf):
    nb, S, _ = x.shape
    n = nb * S
    tm = min(256, S)
    tt = min(256, S)
    tq = min(512, S // 2)
    kw = _prep_weights(wf)
    row = lambda a: a.reshape(1, -1).astype(F32)

    xf = x.reshape(n, D)
    tf = target.reshape(n, D)
    g1, g2, gf = row(wf["norm1_g"]), row(wf["norm2_g"]), row(wf["final_norm_g"])
    h1, proj = _f1_fwd(xf, modp, g1, kw["w_in"], S, tm)

    col = lambda a: a.reshape(NST, 1)
    lam_re, lam_im = col(wf["ssm_lambda_re"]), col(wf["ssm_lambda_im"])
    logdt = jnp.repeat(wf["ssm_log_dt"].reshape(G, 1), P, axis=1).reshape(NST, 1)
    b_re, b_im = wf["ssm_b_re"].reshape(NST, H), wf["ssm_b_im"].reshape(NST, H)
    lbr, lbi, bbr, bbi = _ssm_param_fwd(lam_re, lam_im, logdt, b_re, b_im)
    lre8 = jnp.broadcast_to(lbr.reshape(1, NST), (8, NST))
    lim8 = jnp.broadcast_to(lbi.reshape(1, NST), (8, NST))
    bm = jnp.concatenate([_block_diag_in(bbr.reshape(G, P, H)), _block_diag_in(bbi.reshape(G, P, H))],
                         axis=1).astype(BF16)
    cm = jnp.concatenate([_block_diag_out(wf["ssm_c_re"]), -_block_diag_out(wf["ssm_c_im"])], axis=0).astype(BF16)
    dvec = row(wf["ssm_d"])
    u_p = _permute_rows(proj[:, :D_SSM], S)
    fcr, fci = _ssm_local(u_p, bm, lre8, lim8, S, tt)
    st, ypre, z, gact, yssm_p = _ssm_fwd(u_p, fcr, fci, bm, cm, dvec, kw["w_glu"], lre8, lim8, S, tt)
    yssm = _unpermute_rows(yssm_p, S)

    rc, rs1, rs2 = _rope_tables(positions.reshape(n))
    gq, gkv = row(wf["q_norm_g"]), row(wf["kv_norm_g"])
    q, k, v, qn, kvn = _mla_fwd(proj, rc, rs1, rs2, gq, gkv, kw["w_uq"], kw["w_ukv"], tm)
    oattn, lse = _attn_fwd(q, k, v, S, tq)

    gs = row(wf["ssm_out_g"])
    ga = jnp.pad(wf["attn_out_g"].reshape(NH, V_HEAD), ((0, 0), (0, HP - V_HEAD))).reshape(1, NH * HP)
    yn, o, x1, h2 = _p1_fwd(yssm, oattn, xf, modp, gs, ga, kw["w_out"], g2, S, tm)
    dx1, r, da, dff, accs2, accg2 = _p2(x1, h2, tf, modp, g2, gf, kw["w_ff1"], kw["w_ff2"], S, tm)
    loss = jnp.sum(accg2[2])
    do, dyssm, doattn, accs3, accg3 = _p3_bwd(dx1, o, yssm, oattn, modp, gs, ga, kw["w_out"], S, tm)

    dq = _attn_bwd_dq(q, k, v, oattn, doattn, lse, S, tq)
    dk, dv = _attn_bwd_dkv(q, k, v, oattn, doattn, lse, S, tq)
    dmla, dqb, dkvb, accm = _mla_bwd(dq, dk, dv, proj, rc, rs1, rs2, gq, gkv, kw["w_uq"], kw["w_ukv"], tm)

    dys_p = _permute_rows(dyssm, S)
    dy, dz, air, aii = _ssm_bwd_a(dys_p, z, ypre, kw["w_glu"], cm, lre8, lim8, S, tt)
    du_p, dcm, dbm, dd, dlr, dli = _ssm_bwd_b(dy, u_p, st, fcr, fci, air, aii, bm, cm, dvec, lre8, lim8, S, tt)
    du = _unpermute_rows(du_p, S)
    dcm = dcm.reshape(2, 4, 8, P, 8, H)
    dc_re = jnp.einsum("qgpgh->qghp", dcm[0]).reshape(G, H, P)
    dc_im = -jnp.einsum("qgpgh->qghp", dcm[1]).reshape(G, H, P)
    dbm = dbm.reshape(8, H, 2, 4, 8, P)
    dbb_re = jnp.einsum("ghqgp->qgph", dbm[:, :, 0]).reshape(NST, H)
    dbb_im = jnp.einsum("ghqgp->qgph", dbm[:, :, 1]).reshape(NST, H)
    gb_re, gb_im, glr, gli, gdt = _ssm_param_bwd(lam_re, lam_im, logdt, b_re, b_im, dlr.reshape(NST, 1),
                                                 dli.reshape(NST, 1), dbb_re, dbb_im)
    glogdt = _rowsum(gdt.reshape(G, P))

    dx, dproj, accs1, accg1 = _f1_bwd(du, dmla, dx1, xf, modp, g1, kw["w_in"], S, tm)

    big = {}
    big["w_in"] = _slots(_wgrad(h1, dproj, "wgrad_in")[:, :IN_COLS])
    big["w_glu"] = _wgrad(gact, dz, "wgrad_glu", col_slots=4)
    big["w_uq"] = _slots(_unpad_heads_cols(_wgrad(qn, dqb, "wgrad_uq"), QK_NOPE + QK_ROPE).reshape(Q_LORA, -1))
    gkvw = _wgrad(kvn, dkvb, "wgrad_ukv")
    big["w_ukv"] = _slots(jnp.concatenate([_unpad_heads_cols(gkvw[:, :NH * HP], QK_NOPE),
                                           _unpad_heads_cols(gkvw[:, NH * HP:], V_HEAD)], axis=2).reshape(KV_LORA, -1))
    gwo = _wgrad(yn, do, "wgrad_out")
    big["w_out"] = jnp.concatenate([gwo[:D_SSM].reshape(2, D_SSM // 2, D),
                                    gwo[D_SSM:].reshape(2, NH // 2 * HP, D).reshape(2, NH // 2, HP, D)[:, :, :V_HEAD]
                                    .reshape(2, D_ATTN // 2, D)], axis=0)
    big["w_ff1"] = _wgrad(h2, da, "wgrad_ff1", col_slots=4)
    big["w_ff2"] = _wgrad(r, dff, "wgrad_ff2").reshape(4, D_FF // 4, D)

    small = {}
    small["norm1_g"] = accg1[0:1]
    small["norm2_g"] = accg2[0:1]
    small["final_norm_g"] = accg2[1:2]
    small["ssm_out_g"] = accg3[0:1, :D_SSM]
    small["attn_out_g"] = accg3[1].reshape(NH, HP)[:, :V_HEAD].reshape(1, D_ATTN)
    small["q_norm_g"] = accm[0:1, :Q_LORA]
    small["kv_norm_g"] = accm[1:2, :KV_LORA]
    small["ssm_lambda_re"] = glr.reshape(G, P)
    small["ssm_lambda_im"] = gli.reshape(G, P)
    small["ssm_b_re"] = gb_re
    small["ssm_b_im"] = gb_im
    small["ssm_c_re"] = dc_re.reshape(G * H, P)
    small["ssm_c_im"] = dc_im.reshape(G * H, P)
    small["ssm_d"] = dd.reshape(G, H)
    small["ssm_log_dt"] = glogdt.reshape(1, G)
    return loss, dx.reshape(nb, S, D), big, small, accs1 + accs2 + accs3


def _view2d(a):
    return a.reshape(-1, a.shape[-1]) if a.ndim > 1 else a.reshape(1, -1)


def kernel(x, c, positions, ada_w, ada_b, norm1_g, w_in, ssm_lambda_re, ssm_lambda_im, ssm_b_re, ssm_b_im, ssm_c_re, ssm_c_im, ssm_d, ssm_log_dt, w_glu, q_norm_g, w_uq, kv_norm_g, w_ukv, ssm_out_g, attn_out_g, w_out, norm2_g, w_ff1, w_ff2, final_ada_w, final_ada_b, final_norm_g, loss_target, m_ada_w, m_ada_b, m_norm1_g, m_w_in, m_ssm_lambda_re, m_ssm_lambda_im, m_ssm_b_re, m_ssm_b_im, m_ssm_c_re, m_ssm_c_im, m_ssm_d, m_ssm_log_dt, m_w_glu, m_q_norm_g, m_w_uq, m_kv_norm_g, m_w_ukv, m_ssm_out_g, m_attn_out_g, m_w_out, m_norm2_g, m_w_ff1, m_w_ff2, m_final_ada_w, m_final_ada_b, m_final_norm_g, v_ada_w, v_ada_b, v_norm1_g, v_w_in, v_ssm_lambda_re, v_ssm_lambda_im, v_ssm_b_re, v_ssm_b_im, v_ssm_c_re, v_ssm_c_im, v_ssm_d, v_ssm_log_dt, v_w_glu, v_q_norm_g, v_w_uq, v_kv_norm_g, v_w_ukv, v_ssm_out_g, v_attn_out_g, v_w_out, v_norm2_g, v_w_ff1, v_w_ff2, v_final_ada_w, v_final_ada_b, v_final_norm_g):
    args = dict(locals())
    names = list(inspect.signature(kernel).parameters)
    wnames = names[3:names.index("loss_target")]
    small_names = [nm for nm in wnames if nm not in GATHERED and nm not in TP]
    reduced_names = [nm for nm in small_names if nm not in ("ada_b", "final_ada_b")]
    w = {nm: args[nm] for nm in wnames}
    m = {nm: args["m_" + nm] for nm in wnames}
    v = {nm: args["v_" + nm] for nm in wnames}
    nb = x.shape[0]
    xi, yi, ci = lax.axis_index("x"), lax.axis_index("y"), lax.axis_index("c")
    chip, me = 2 * xi + yi, 4 * xi + 2 * yi + ci

    got = _gather_chips("gather_weights", [_view2d(w[nm]).astype(BF16) for nm in GATHERED], [c])
    wf = {nm: (g.reshape(-1, g.shape[-1]) if nm in ROW_SHARDED else _unslots(g)) for nm, g in zip(GATHERED, got)}
    for nm in small_names:
        wf[nm] = w[nm][0] if w[nm].ndim > 1 else w[nm]
    c_all = got[len(GATHERED)].reshape(8 * nb, D)

    na, nf = ada_w.shape[-1], final_ada_w.shape[-1]
    ada_b_s = lax.dynamic_slice(ada_b, (0, chip * na), (1, na))
    fada_b_s = lax.dynamic_slice(final_ada_b.reshape(1, -1), (0, chip * nf), (1, nf))
    cond_all, modcols = _mod_fwd(c_all, ada_w[0], ada_b_s, final_ada_w, fada_b_s)
    (mod_g,) = _gather_chips("gather_mod", [modcols])
    mine = lax.dynamic_slice(mod_g, (0, me * nb, 0), (4, nb, na + nf))
    modp = jnp.concatenate([mine[:, :, :na].transpose(1, 0, 2).reshape(nb, 6, D),
                            mine[:, :, na:].transpose(1, 0, 2).reshape(nb, 2, D)], axis=1)

    loss, grad_x, big, small, dmodp = _local_step(x, positions, loss_target, modp, wf)
    loss = lax.psum(loss, ("x", "y", "c"))

    sizes = [small[nm].size for nm in reduced_names]
    pad = -sum(sizes) % 128
    packed = jnp.concatenate([small[nm].reshape(1, -1) for nm in reduced_names] + [jnp.zeros((1, pad), F32)], axis=1)
    swapped = _swap_halves([big[nm] for nm in GATHERED], [dmodp.reshape(nb, 8 * D), packed])
    cidx = ci.astype(jnp.int32).reshape(1)
    chip_sums = [_add_half(big[nm], r, cidx, "grad_add_sibling_" + nm) for nm, r in zip(GATHERED, swapped)]
    halves = [_add_chips(r, "grad_add_chips_" + nm) for nm, r in zip(GATHERED, _scatter_chips(chip_sums))]
    others = _join_halves(halves)
    grads = {}
    dmod_all = swapped[len(GATHERED)].reshape(8 * nb, 8 * D)
    small_sum = _sum_devices(swapped[len(GATHERED) + 1].reshape(8, -1))
    off = 0
    for nm, sz in zip(reduced_names, sizes):
        grads[nm] = small_sum[:, off:off + sz].reshape(small[nm].shape)
        off += sz

    dsl = jnp.concatenate([lax.dynamic_slice(dmod_all, (0, chip * na), (8 * nb, na)),
                           lax.dynamic_slice(dmod_all, (0, 6 * D + chip * nf), (8 * nb, nf))], axis=1)
    gw, gb = _mod_bwd(cond_all.T, dsl, dmod_all)
    grads["ada_w"], grads["final_ada_w"] = gw[:, :na], gw[:, na:]
    grads["ada_b"], grads["final_ada_b"] = gb[:, :6 * D], gb[:, 6 * D:]

    delta, new_m, new_v = {}, {}, {}
    for nm, mine_h, other_h in zip(GATHERED, halves, others):
        grads[nm], delta[nm], new_m[nm], new_v[nm] = _adamw_halves(
            _view2d(w[nm]), mine_h, other_h, _view2d(m[nm]), _view2d(v[nm]), cidx, "adamw_" + nm)
    for nm in TP:
        delta[nm], new_m[nm], new_v[nm] = _adamw(_view2d(w[nm]), grads[nm], _view2d(m[nm]), _view2d(v[nm]),
                                                  "adamw_" + nm)
    upd = _adamw_small([_view2d(w[nm]) for nm in small_names], [grads[nm] for nm in small_names],
                       [_view2d(m[nm]) for nm in small_names], [_view2d(v[nm]) for nm in small_names])
    k = len(small_names)
    for t, nm in enumerate(small_names):
        delta[nm], new_m[nm], new_v[nm] = upd[t], upd[k + t], upd[2 * k + t]

    outs = [grads, delta, new_m, new_v]
    return (loss, grad_x, *[d[nm].reshape(w[nm].shape) for d in outs for nm in wnames])
```

```python
import functools
import inspect
import math

import jax
import jax.numpy as jnp
from jax import lax
from jax.experimental import pallas as pl
from jax.experimental.pallas import tpu as pltpu

F32 = jnp.float32
BF16 = jnp.bfloat16

D = 1024
D_SSM = 512
G = 32
H = 16
P = 64
NST = G * P
D_ATTN = 512
NH = 8
QK_NOPE = 64
QK_ROPE = 32
V_HEAD = 64
HP = 128
Q_LORA = 384
KV_LORA = 256
IN_COLS = D_SSM + Q_LORA + KV_LORA + QK_ROPE
IN_PAD = 1280
D_FF = 4096
ROPE_BASE = 10000.0
EPS = 1e-6
ADAM_LR = 0.001
ADAM_B1 = 0.9
ADAM_B2 = 0.999
ADAM_EPS = 1e-08
ADAM_WD = 0.01
ADAM_STEP = 10
NEG = -1e30
VMEM_LIMIT = 60 << 20

MESH = pl.DeviceIdType.MESH
_VM = pl.BlockSpec(memory_space=pltpu.VMEM)
_ANY = pl.BlockSpec(memory_space=pl.ANY)

GATHERED = ["w_in", "w_glu", "w_uq", "w_ukv", "w_out", "w_ff1", "w_ff2"]
TP = ["ada_w", "final_ada_w"]
ROW_SHARDED = ("w_out", "w_ff2")


def _cp(sem=None, vmem=VMEM_LIMIT):
    kw = dict(vmem_limit_bytes=vmem)
    if sem is not None:
        kw["dimension_semantics"] = sem
    return pltpu.CompilerParams(**kw)


def _dot(a, b):
    return jnp.dot(a, b, preferred_element_type=F32)


def _dot_nt(a, b):
    return lax.dot_general(a, b, (((1,), (1,)), ((), ())), preferred_element_type=F32)


def _dot_tn(a, b):
    return lax.dot_general(a, b, (((0,), (0,)), ((), ())), preferred_element_type=F32)


def _rms(x, n):
    r = lax.rsqrt(jnp.sum(x * x, axis=-1, keepdims=True) * (1.0 / n) + EPS)
    return x * r, r


def _rms_bwd(dyg, xhat, r, n):
    return r * (dyg - xhat * (jnp.sum(dyg * xhat, axis=-1, keepdims=True) * (1.0 / n)))


def _sigmoid(x):
    return 1.0 / (1.0 + jnp.exp(-x))


_GK = math.sqrt(2.0 / math.pi)
_GC = 0.044715


def _gelu(y):
    t = jnp.tanh(_GK * (y + _GC * y * y * y))
    return 0.5 * y * (1.0 + t)


def _gelu_grad(y):
    t = jnp.tanh(_GK * (y + _GC * y * y * y))
    return 0.5 * (1.0 + t) + 0.5 * y * (1.0 - t * t) * _GK * (1.0 + 3.0 * _GC * y * y)


def _colsum(x):
    return jnp.sum(x, axis=0, keepdims=True)


def _roll(x, s):
    return pltpu.roll(x, s % x.shape[-1], x.ndim - 1)


def _mod_fwd(c_all, ada_w_s, ada_b_s, fada_w_s, fada_b_s):
    nseq = c_all.shape[0]
    na, nf = ada_w_s.shape[1], fada_w_s.shape[1]

    def body(c_ref, w_ref, b_ref, fw_ref, fb_ref, cond_ref, mod_ref):
        cv = c_ref[...]
        cond = cv * _sigmoid(cv)
        cond_ref[...] = cond
        cb = cond.astype(BF16)
        mod_ref[:, 0:na] = _dot(cb, w_ref[...].astype(BF16)) + b_ref[...]
        mod_ref[:, na:na + nf] = _dot(cb, fw_ref[...].astype(BF16)) + fb_ref[...]

    return pl.pallas_call(
        body, name="mod_fwd",
        out_shape=[jax.ShapeDtypeStruct((nseq, D), F32), jax.ShapeDtypeStruct((nseq, na + nf), F32)],
        in_specs=[_VM] * 5, out_specs=[_VM] * 2, compiler_params=_cp(),
    )(c_all, ada_w_s, ada_b_s, fada_w_s, fada_b_s)


def _mod_bwd(cond_t, dsl, dall):
    nseq, n = dsl.shape
    bc = 512

    def body(ct_ref, dm_ref, da_ref, gw_ref, gb_ref):
        ct = ct_ref[...]
        dm = dm_ref[...]
        acc = ct[:, 0:1] * dm[0:1, :]
        for b in range(1, nseq):
            acc = acc + ct[:, b:b + 1] * dm[b:b + 1, :]
        gw_ref[...] = acc

        @pl.when(pl.program_id(0) == 0)
        def _():
            gb_ref[...] = _colsum(da_ref[...])

    return pl.pallas_call(
        body, name="mod_bwd", grid=(n // bc,),
        out_shape=[jax.ShapeDtypeStruct((D, n), F32), jax.ShapeDtypeStruct((1, dall.shape[1]), F32)],
        in_specs=[_VM, pl.BlockSpec((nseq, bc), lambda i: (0, i)), _VM],
        out_specs=[pl.BlockSpec((D, bc), lambda i: (0, i)), pl.BlockSpec((1, dall.shape[1]), lambda i: (0, 0))],
        compiler_params=_cp(("arbitrary",)),
    )(cond_t, dsl, dall)


def _f1_fwd(x, modp, g1, w_in, S, tm):
    n = x.shape[0]
    tps = S // tm

    def body(x_ref, mod_ref, g_ref, w_ref, h_ref, proj_ref):
        xhat, _ = _rms(x_ref[...], D)
        h = (xhat * g_ref[...]) * (1.0 + mod_ref[0, 1:2, :]) + mod_ref[0, 0:1, :]
        hb = h.astype(BF16)
        h_ref[...] = hb
        proj_ref[...] = _dot(hb, w_ref[...])

    return pl.pallas_call(
        body, name="f1_fwd", grid=(n // tm,),
        out_shape=[jax.ShapeDtypeStruct((n, D), BF16), jax.ShapeDtypeStruct((n, IN_PAD), F32)],
        in_specs=[pl.BlockSpec((tm, D), lambda i: (i, 0)),
                  pl.BlockSpec((1, 8, D), lambda i: (i // tps, 0, 0)), _VM, _VM],
        out_specs=[pl.BlockSpec((tm, D), lambda i: (i, 0)), pl.BlockSpec((tm, IN_PAD), lambda i: (i, 0))],
        compiler_params=_cp(("parallel",)),
    )(x, modp, g1, w_in)


def _f1_bwd(du, dmla, dx1, x, modp, g1, w_in, S, tm):
    n = x.shape[0]
    tps = S // tm
    nb = n // S

    def body(du_ref, dm_ref, dx1_ref, x_ref, mod_ref, g_ref, w_ref, dx_ref, dproj_ref, accs_ref, accg_ref):
        i = pl.program_id(0)
        dproj = jnp.concatenate([du_ref[...], dm_ref[...]], axis=1).astype(BF16)
        dproj_ref[...] = dproj
        dh = _dot_nt(dproj, w_ref[...])
        xhat, r = _rms(x_ref[...], D)
        g = g_ref[...]
        dn = dh * (1.0 + mod_ref[0, 1:2, :])
        dx_ref[...] = dx1_ref[...] + _rms_bwd(dn * g, xhat, r, D)

        @pl.when(i % tps == 0)
        def _():
            accs_ref[...] = jnp.zeros_like(accs_ref)

        @pl.when(i == 0)
        def _():
            accg_ref[...] = jnp.zeros_like(accg_ref)

        accs_ref[0, 0:1, :] += _colsum(dh)
        accs_ref[0, 1:2, :] += _colsum(dh * (xhat * g))
        accg_ref[0:1, :] += _colsum(dn * xhat)

    return pl.pallas_call(
        body, name="f1_bwd", grid=(n // tm,),
        out_shape=[jax.ShapeDtypeStruct((n, D), F32), jax.ShapeDtypeStruct((n, IN_PAD), BF16),
                   jax.ShapeDtypeStruct((nb, 8, D), F32), jax.ShapeDtypeStruct((8, D), F32)],
        in_specs=[pl.BlockSpec((tm, D_SSM), lambda i: (i, 0)), pl.BlockSpec((tm, IN_PAD - D_SSM), lambda i: (i, 0)),
                  pl.BlockSpec((tm, D), lambda i: (i, 0)), pl.BlockSpec((tm, D), lambda i: (i, 0)),
                  pl.BlockSpec((1, 8, D), lambda i: (i // tps, 0, 0)), _VM, _VM],
        out_specs=[pl.BlockSpec((tm, D), lambda i: (i, 0)), pl.BlockSpec((tm, IN_PAD), lambda i: (i, 0)),
                   pl.BlockSpec((1, 8, D), lambda i: (i // tps, 0, 0)), pl.BlockSpec((8, D), lambda i: (0, 0))],
        compiler_params=_cp(("arbitrary",)),
    )(du, dmla, dx1, x, modp, g1, w_in)


def _ssm_param_fwd(lam_re, lam_im, logdt, b_re, b_im):
    def body(lr_ref, li_ref, ld_ref, br_ref, bi_ref, lbr_ref, lbi_ref, bbr_ref, bbi_ref):
        lr, li = lr_ref[...], li_ref[...]
        dt = jnp.exp(ld_ref[...])
        er = jnp.exp(lr * dt)
        lbr = er * jnp.cos(li * dt)
        lbi = er * jnp.sin(li * dt)
        den = 1.0 / (lr * lr + li * li)
        cr = ((lbr - 1.0) * lr + lbi * li) * den
        ci = (lbi * lr - (lbr - 1.0) * li) * den
        lbr_ref[...] = lbr
        lbi_ref[...] = lbi
        bbr_ref[...] = cr * br_ref[...] - ci * bi_ref[...]
        bbi_ref[...] = cr * bi_ref[...] + ci * br_ref[...]

    return pl.pallas_call(
        body, name="ssm_param_fwd",
        out_shape=[jax.ShapeDtypeStruct((NST, 1), F32)] * 2 + [jax.ShapeDtypeStruct((NST, H), F32)] * 2,
        in_specs=[_VM] * 5, out_specs=[_VM] * 4, compiler_params=_cp(),
    )(lam_re, lam_im, logdt, b_re, b_im)


def _ssm_param_bwd(lam_re, lam_im, logdt, b_re, b_im, dlb_re, dlb_im, dbb_re, dbb_im):
    def body(lr_ref, li_ref, ld_ref, br_ref, bi_ref, dlr_ref, dli_ref, dbr_ref, dbi_ref,
             gbr_ref, gbi_ref, glr_ref, gli_ref, gdt_ref):
        lr, li = lr_ref[...], li_ref[...]
        dt = jnp.exp(ld_ref[...])
        er = jnp.exp(lr * dt)
        lbr = er * jnp.cos(li * dt)
        lbi = er * jnp.sin(li * dt)
        den = 1.0 / (lr * lr + li * li)
        nr, ni = lbr - 1.0, lbi
        cr = (nr * lr + ni * li) * den
        ci = (ni * lr - nr * li) * den
        br, bi = br_ref[...], bi_ref[...]
        dbr, dbi = dbr_ref[...], dbi_ref[...]
        gbr_ref[...] = cr * dbr + ci * dbi
        gbi_ref[...] = cr * dbi - ci * dbr
        gcr = jnp.sum(dbr * br + dbi * bi, axis=1, keepdims=True)
        gci = jnp.sum(dbi * br - dbr * bi, axis=1, keepdims=True)
        ilr, ili = lr * den, -li * den
        glbr = dlr_ref[...] + (gcr * ilr + gci * ili)
        glbi = dli_ref[...] + (gci * ilr - gcr * ili)
        qr = -(cr * ilr - ci * ili)
        qi = -(cr * ili + ci * ilr)
        glr = gcr * qr + gci * qi
        gli = gci * qr - gcr * qi
        glr = glr + dt * (glbr * lbr + glbi * lbi)
        gli = gli + dt * (glbi * lbr - glbr * lbi)
        wr = lr * lbr - li * lbi
        wi = lr * lbi + li * lbr
        glr_ref[...] = glr
        gli_ref[...] = gli
        gdt_ref[...] = (glbr * wr + glbi * wi) * dt

    return pl.pallas_call(
        body, name="ssm_param_bwd",
        out_shape=[jax.ShapeDtypeStruct((NST, H), F32)] * 2 + [jax.ShapeDtypeStruct((NST, 1), F32)] * 3,
        in_specs=[_VM] * 9, out_specs=[_VM] * 5, compiler_params=_cp(),
    )(lam_re, lam_im, logdt, b_re, b_im, dlb_re, dlb_im, dbb_re, dbb_im)


def _rowsum(a):
    def body(a_ref, o_ref):
        o_ref[...] = jnp.sum(a_ref[...], axis=1, keepdims=True)

    return pl.pallas_call(
        body, name="rowsum", out_shape=jax.ShapeDtypeStruct((a.shape[0], 1), F32),
        in_specs=[_VM], out_specs=_VM, compiler_params=_cp(),
    )(a)


def _pow2k(pr, pi, nsq):
    for _ in range(nsq):
        pr, pi = pr * pr - pi * pi, 2.0 * pr * pi
    return pr, pi


def _ssm_local(u_p, bm, lre8, lim8, S, tt):
    n = u_p.shape[0]
    nb, nt = n // S, S // tt
    nsq = int(round(math.log2(S // 8)))
    assert 2 ** nsq == S // 8

    def body(u_ref, bm_ref, lre_ref, lim_ref, cre_ref, cim_ref, sre, sim, bu):
        j = pl.program_id(1)

        @pl.when(j == 0)
        def _():
            sre[...] = jnp.zeros_like(sre)
            sim[...] = jnp.zeros_like(sim)

        bu[...] = _dot(u_ref[...].astype(BF16), bm_ref[...])
        lre, lim = lre_ref[...], lim_ref[...]

        def step(i, c):
            sr, si = c
            off = pl.multiple_of(i * 8, 8)
            br = bu[pl.ds(off, 8), 0:NST]
            bi = bu[pl.ds(off, 8), NST:2 * NST]
            return lre * sr - lim * si + br, lre * si + lim * sr + bi

        sr, si = lax.fori_loop(0, tt // 8, step, (sre[...], sim[...]))
        sre[...] = sr
        sim[...] = si

        @pl.when(j == nt - 1)
        def _():
            pr, pi = _pow2k(lre[0:1], lim[0:1], nsq)
            cr = jnp.zeros((1, NST), F32)
            ci = jnp.zeros((1, NST), F32)
            cre_ref[0:1, :] = cr
            cim_ref[0:1, :] = ci
            for k in range(1, 8):
                cr, ci = sr[k - 1:k] + pr * cr - pi * ci, si[k - 1:k] + pr * ci + pi * cr
                cre_ref[k:k + 1, :] = cr
                cim_ref[k:k + 1, :] = ci

    return pl.pallas_call(
        body, name="ssm_local", grid=(nb, nt),
        out_shape=[jax.ShapeDtypeStruct((nb * 8, NST), F32)] * 2,
        in_specs=[pl.BlockSpec((tt, D_SSM), lambda b, j: (b * nt + j, 0)), _VM, _VM, _VM],
        out_specs=[pl.BlockSpec((8, NST), lambda b, j: (b, 0))] * 2,
        scratch_shapes=[pltpu.VMEM((8, NST), F32), pltpu.VMEM((8, NST), F32), pltpu.VMEM((tt, 2 * NST), F32)],
        compiler_params=_cp(("arbitrary", "arbitrary")),
    )(u_p, bm, lre8, lim8)


def _ssm_fwd(u_p, cre, cim, bm, cm, dvec, w_glu, lre8, lim8, S, tt):
    n = u_p.shape[0]
    nb, nt = n // S, S // tt

    def body(u_ref, cre_ref, cim_ref, bm_ref, cm_ref, d_ref, wg_ref, lre_ref, lim_ref,
             st_ref, ypre_ref, z_ref, gact_ref, yssm_ref, sre, sim, bu):
        j = pl.program_id(1)

        @pl.when(j == 0)
        def _():
            sre[...] = cre_ref[...]
            sim[...] = cim_ref[...]

        u = u_ref[...]
        bu[...] = _dot(u.astype(BF16), bm_ref[...])
        lre, lim = lre_ref[...], lim_ref[...]

        def step(i, c):
            sr, si = c
            off = pl.multiple_of(i * 8, 8)
            nr = lre * sr - lim * si + bu[pl.ds(off, 8), 0:NST]
            ni = lre * si + lim * sr + bu[pl.ds(off, 8), NST:2 * NST]
            st_ref[pl.ds(off, 8), 0:NST] = nr
            st_ref[pl.ds(off, 8), NST:2 * NST] = ni
            return nr, ni

        sr, si = lax.fori_loop(0, tt // 8, step, (sre[...], sim[...]))
        sre[...] = sr
        sim[...] = si
        y = _dot(st_ref[...].astype(BF16), cm_ref[...]) + d_ref[...] * u
        ypre_ref[...] = y
        gb = _gelu(y).astype(BF16)
        gact_ref[...] = gb
        z = _dot(gb, wg_ref[...])
        z_ref[...] = z
        yssm_ref[...] = z[:, 0:D_SSM] * _sigmoid(z[:, D_SSM:2 * D_SSM])

    row = lambda w: pl.BlockSpec((tt, w), lambda b, j: (b * nt + j, 0))
    return pl.pallas_call(
        body, name="ssm_fwd", grid=(nb, nt),
        out_shape=[jax.ShapeDtypeStruct((n, 2 * NST), F32), jax.ShapeDtypeStruct((n, D_SSM), F32),
                   jax.ShapeDtypeStruct((n, 2 * D_SSM), F32), jax.ShapeDtypeStruct((n, D_SSM), BF16),
                   jax.ShapeDtypeStruct((n, D_SSM), F32)],
        in_specs=[row(D_SSM), pl.BlockSpec((8, NST), lambda b, j: (b, 0)), pl.BlockSpec((8, NST), lambda b, j: (b, 0)),
                  _VM, _VM, _VM, _VM, _VM, _VM],
        out_specs=[row(2 * NST), row(D_SSM), row(2 * D_SSM), row(D_SSM), row(D_SSM)],
        scratch_shapes=[pltpu.VMEM((8, NST), F32), pltpu.VMEM((8, NST), F32), pltpu.VMEM((tt, 2 * NST), F32)],
        compiler_params=_cp(("arbitrary", "arbitrary")),
    )(u_p, cre, cim, bm, cm, dvec, w_glu, lre8, lim8)


def _ssm_bwd_a(dys_p, z, ypre, w_glu, cm, lre8, lim8, S, tt):
    n = z.shape[0]
    nb, nt = n // S, S // tt
    nsq = int(round(math.log2(S // 8)))
    ng = tt // 8

    def body(dys_ref, z_ref, y_ref, wg_ref, cm_ref, lre_ref, lim_ref, dy_ref, dz_ref, are_ref, aim_ref, sre, sim, gb):
        j = pl.program_id(1)

        @pl.when(j == 0)
        def _():
            sre[...] = jnp.zeros_like(sre)
            sim[...] = jnp.zeros_like(sim)

        z = z_ref[...]
        z1, z2 = z[:, 0:D_SSM], z[:, D_SSM:2 * D_SSM]
        sg = _sigmoid(z2)
        dys = dys_ref[...]
        dz = jnp.concatenate([dys * sg, dys * z1 * sg * (1.0 - sg)], axis=1).astype(BF16)
        dz_ref[...] = dz
        dy = _dot_nt(dz, wg_ref[...]) * _gelu_grad(y_ref[...])
        dy_ref[...] = dy
        gb[...] = _dot_nt(dy.astype(BF16), cm_ref[...])
        lre, lim = lre_ref[...], lim_ref[...]

        def step(i, c):
            ar, ai = c
            off = pl.multiple_of((ng - 1 - i) * 8, 8)
            gr = gb[pl.ds(off, 8), 0:NST]
            gi = gb[pl.ds(off, 8), NST:2 * NST]
            return lre * ar + lim * ai + gr, lre * ai - lim * ar + gi

        ar, ai = lax.fori_loop(0, ng, step, (sre[...], sim[...]))
        sre[...] = ar
        sim[...] = ai

        @pl.when(j == nt - 1)
        def _():
            pr, pi = _pow2k(lre[0:1], -lim[0:1], nsq)
            cr = jnp.zeros((1, NST), F32)
            ci = jnp.zeros((1, NST), F32)
            are_ref[7:8, :] = cr
            aim_ref[7:8, :] = ci
            for k in range(6, -1, -1):
                cr, ci = ar[k + 1:k + 2] + pr * cr - pi * ci, ai[k + 1:k + 2] + pr * ci + pi * cr
                are_ref[k:k + 1, :] = cr
                aim_ref[k:k + 1, :] = ci

    row = lambda w: pl.BlockSpec((tt, w), lambda b, j: (b * nt + nt - 1 - j, 0))
    return pl.pallas_call(
        body, name="ssm_bwd_a", grid=(nb, nt),
        out_shape=[jax.ShapeDtypeStruct((n, D_SSM), F32), jax.ShapeDtypeStruct((n, 2 * D_SSM), BF16),
                   jax.ShapeDtypeStruct((nb * 8, NST), F32), jax.ShapeDtypeStruct((nb * 8, NST), F32)],
        in_specs=[row(D_SSM), row(2 * D_SSM), row(D_SSM), _VM, _VM, _VM, _VM],
        out_specs=[row(D_SSM), row(2 * D_SSM), pl.BlockSpec((8, NST), lambda b, j: (b, 0)),
                   pl.BlockSpec((8, NST), lambda b, j: (b, 0))],
        scratch_shapes=[pltpu.VMEM((8, NST), F32), pltpu.VMEM((8, NST), F32), pltpu.VMEM((tt, 2 * NST), F32)],
        compiler_params=_cp(("arbitrary", "arbitrary")),
    )(dys_p, z, ypre, w_glu, cm, lre8, lim8)


def _ssm_bwd_b(dy, u_p, st, fcr, fci, air, aii, bm, cm, dvec, lre8, lim8, S, tt):
    n = u_p.shape[0]
    nb, nt = n // S, S // tt
    ng = tt // 8
    QB = D_SSM // 4

    def body(dy_ref, u_ref, st_ref, stp_ref, fcr_ref, fci_ref, air_ref, aii_ref, bm_ref, cm_ref, d_ref, lre_ref, lim_ref,
             du_ref, dcm_ref, dbm_ref, dd_ref, dlr_ref, dli_ref, are, aim, accr, acci, sp, ab):
        b = pl.program_id(0)
        j = pl.program_id(1)
        jt = nt - 1 - j

        @pl.when((b == 0) & (j == 0))
        def _():
            dcm_ref[...] = jnp.zeros_like(dcm_ref)
            dbm_ref[...] = jnp.zeros_like(dbm_ref)
            dd_ref[...] = jnp.zeros_like(dd_ref)
            accr[...] = jnp.zeros_like(accr)
            acci[...] = jnp.zeros_like(acci)

        @pl.when(j == 0)
        def _():
            are[...] = air_ref[...]
            aim[...] = aii_ref[...]

        sp[8:tt + 8, :] = st_ref[...]

        @pl.when(jt == 0)
        def _():
            sp[0:8, 0:NST] = fcr_ref[...]
            sp[0:8, NST:2 * NST] = fci_ref[...]

        @pl.when(jt != 0)
        def _():
            sp[0:8, :] = stp_ref[...]

        dy = dy_ref[...]
        u = u_ref[...]
        dyb = dy.astype(BF16)
        ab[...] = _dot_nt(dyb, cm_ref[...])
        lre, lim = lre_ref[...], lim_ref[...]

        def step(i, c):
            ar, ai = c
            off = pl.multiple_of((ng - 1 - i) * 8, 8)
            nr = lre * ar + lim * ai + ab[pl.ds(off, 8), 0:NST]
            ni = lre * ai - lim * ar + ab[pl.ds(off, 8), NST:2 * NST]
            ab[pl.ds(off, 8), 0:NST] = nr
            ab[pl.ds(off, 8), NST:2 * NST] = ni
            pr = sp[pl.ds(off, 8), 0:NST]
            pi = sp[pl.ds(off, 8), NST:2 * NST]
            accr[...] += nr * pr + ni * pi
            acci[...] += ni * pr - nr * pi
            return nr, ni

        ar, ai = lax.fori_loop(0, ng, step, (are[...], aim[...]))
        are[...] = ar
        aim[...] = ai
        a_b = ab[...].astype(BF16)
        du_ref[...] = _dot_nt(a_b, bm_ref[...]) + d_ref[...] * dy
        ub = u.astype(BF16)
        for q in range(4):
            for part in range(2):
                lo = part * NST + q * 4 * QB
                s_q = sp[8:tt + 8, lo:lo + 4 * QB].astype(BF16)
                dcm_ref[lo:lo + 4 * QB, :] += _dot_tn(s_q, dyb[:, q * QB:(q + 1) * QB])
                dbm_ref[:, lo:lo + 4 * QB] += _dot_tn(ub[:, q * QB:(q + 1) * QB], a_b[:, lo:lo + 4 * QB])
        dd_ref[...] += _colsum(dy * u)

        @pl.when((b == nb - 1) & (j == nt - 1))
        def _():
            dlr_ref[...] = _colsum(accr[...])
            dli_ref[...] = _colsum(acci[...])

    row = lambda w: pl.BlockSpec((tt, w), lambda b, j: (b * nt + nt - 1 - j, 0))
    seq8 = pl.BlockSpec((8, NST), lambda b, j: (b, 0))
    prev = pl.BlockSpec((8, 2 * NST), lambda b, j: (jnp.maximum((b * nt + nt - 1 - j) * ng - 1, 0), 0))
    const = lambda shape: pl.BlockSpec(shape, lambda b, j: (0, 0))
    return pl.pallas_call(
        body, name="ssm_bwd_b", grid=(nb, nt),
        out_shape=[jax.ShapeDtypeStruct((n, D_SSM), F32), jax.ShapeDtypeStruct((2 * NST, QB), F32),
                   jax.ShapeDtypeStruct((QB, 2 * NST), F32), jax.ShapeDtypeStruct((1, D_SSM), F32),
                   jax.ShapeDtypeStruct((1, NST), F32), jax.ShapeDtypeStruct((1, NST), F32)],
        in_specs=[row(D_SSM), row(D_SSM), row(2 * NST), prev, seq8, seq8, seq8, seq8, _VM, _VM, _VM, _VM, _VM],
        out_specs=[row(D_SSM), const((2 * NST, QB)), const((QB, 2 * NST)), const((1, D_SSM)),
                   const((1, NST)), const((1, NST))],
        scratch_shapes=[pltpu.VMEM((8, NST), F32)] * 4 + [pltpu.VMEM((tt + 8, 2 * NST), F32),
                                                          pltpu.VMEM((tt, 2 * NST), F32)],
        compiler_params=_cp(("arbitrary", "arbitrary")),
    )(dy, u_p, st, st, fcr, fci, air, aii, bm, cm, dvec, lre8, lim8)


def _rope(v, c, s1, s2):
    return v * c + _roll(v, -16) * s1 + _roll(v, 16) * s2


def _rope_t(dv, c, s1, s2):
    return dv * c + _roll(dv * s1, 16) + _roll(dv * s2, -16)


def _mla_fwd(proj, rc, rs1, rs2, gq, gkv, w_uq, w_ukv, tm):
    n = proj.shape[0]

    def body(ql_ref, kvl_ref, kr_ref, c_ref, s1_ref, s2_ref, gq_ref, gkv_ref, wq_ref, wkv_ref,
             q_ref, k_ref, v_ref, qn_ref, kvn_ref):
        c, s1, s2 = c_ref[...], s1_ref[...], s2_ref[...]
        qhat, _ = _rms(ql_ref[...], Q_LORA)
        qn = (qhat * gq_ref[...]).astype(BF16)
        qn_ref[...] = qn
        q = _dot(qn, wq_ref[...])
        q_ref[...] = _rope(q, jnp.tile(c, (1, NH)), jnp.tile(s1, (1, NH)), jnp.tile(s2, (1, NH))).astype(BF16)
        khat, _ = _rms(kvl_ref[...], KV_LORA)
        kvn = (khat * gkv_ref[...]).astype(BF16)
        kvn_ref[...] = kvn
        kv = _dot(kvn, wkv_ref[...])
        kr = _rope(_roll(kr_ref[...], 64), c, s1, s2)
        k_ref[...] = (kv[:, 0:NH * HP] + jnp.tile(kr, (1, NH))).astype(BF16)
        v_ref[...] = kv[:, NH * HP:2 * NH * HP].astype(BF16)

    def wrapped(proj_ref, *rest):
        ql = proj_ref.at[:, D_SSM:D_SSM + Q_LORA]
        kvl = proj_ref.at[:, D_SSM + Q_LORA:D_SSM + Q_LORA + KV_LORA]
        kr = proj_ref.at[:, IN_PAD - HP:IN_PAD]
        body(ql, kvl, kr, *rest)

    row = lambda w: pl.BlockSpec((tm, w), lambda i: (i, 0))
    return pl.pallas_call(
        wrapped, name="mla_fwd", grid=(n // tm,),
        out_shape=[jax.ShapeDtypeStruct((n, NH * HP), BF16)] * 3 +
                  [jax.ShapeDtypeStruct((n, Q_LORA), BF16), jax.ShapeDtypeStruct((n, KV_LORA), BF16)],
        in_specs=[row(IN_PAD), row(HP), row(HP), row(HP), _VM, _VM, _VM, _VM],
        out_specs=[row(NH * HP)] * 3 + [row(Q_LORA), row(KV_LORA)],
        compiler_params=_cp(("parallel",)),
    )(proj, rc, rs1, rs2, gq, gkv, w_uq, w_ukv)


def _mla_bwd(dq, dk, dv, proj, rc, rs1, rs2, gq, gkv, w_uq, w_ukv, tm):
    n = proj.shape[0]

    def body(dq_ref, dk_ref, dv_ref, proj_ref, c_ref, s1_ref, s2_ref, gq_ref, gkv_ref, wq_ref, wkv_ref,
             dmla_ref, dqb_ref, dkvb_ref, acc_ref):
        i = pl.program_id(0)
        c, s1, s2 = c_ref[...], s1_ref[...], s2_ref[...]
        dqu = _rope_t(dq_ref[...], jnp.tile(c, (1, NH)), jnp.tile(s1, (1, NH)), jnp.tile(s2, (1, NH))).astype(BF16)
        dqb_ref[...] = dqu
        dqn = _dot_nt(dqu, wq_ref[...])
        qhat, rq = _rms(proj_ref[:, D_SSM:D_SSM + Q_LORA], Q_LORA)
        dql = _rms_bwd(dqn * gq_ref[...], qhat, rq, Q_LORA)
        dkf = dk_ref[...]
        dkv = jnp.concatenate([dkf, dv_ref[...]], axis=1).astype(BF16)
        dkvb_ref[...] = dkv
        dkvn = _dot_nt(dkv, wkv_ref[...])
        khat, rk = _rms(proj_ref[:, D_SSM + Q_LORA:D_SSM + Q_LORA + KV_LORA], KV_LORA)
        dkvl = _rms_bwd(dkvn * gkv_ref[...], khat, rk, KV_LORA)
        dkr = dkf[:, 0:HP]
        for h in range(1, NH):
            dkr = dkr + dkf[:, h * HP:(h + 1) * HP]
        lane = lax.broadcasted_iota(jnp.int32, dkr.shape, 1)
        dkr = jnp.where((lane >= QK_NOPE) & (lane < QK_NOPE + QK_ROPE), dkr, 0.0)
        dkr = _roll(_rope_t(dkr, c, s1, s2), -64)
        dmla_ref[...] = jnp.concatenate([dql, dkvl, dkr], axis=1)

        @pl.when(i == 0)
        def _():
            acc_ref[...] = jnp.zeros_like(acc_ref)

        acc_ref[0:1, 0:Q_LORA] += _colsum(dqn * qhat)
        acc_ref[1:2, 0:KV_LORA] += _colsum(dkvn * khat)

    row = lambda w: pl.BlockSpec((tm, w), lambda i: (i, 0))
    return pl.pallas_call(
        body, name="mla_bwd", grid=(n // tm,),
        out_shape=[jax.ShapeDtypeStruct((n, IN_PAD - D_SSM), F32), jax.ShapeDtypeStruct((n, NH * HP), BF16),
                   jax.ShapeDtypeStruct((n, 2 * NH * HP), BF16), jax.ShapeDtypeStruct((8, Q_LORA), F32)],
        in_specs=[row(NH * HP)] * 3 + [row(IN_PAD), row(HP), row(HP), row(HP), _VM, _VM, _VM, _VM],
        out_specs=[row(IN_PAD - D_SSM), row(NH * HP), row(2 * NH * HP), pl.BlockSpec((8, Q_LORA), lambda i: (0, 0))],
        compiler_params=_cp(("arbitrary",)),
    )(dq, dk, dv, proj, rc, rs1, rs2, gq, gkv, w_uq, w_ukv)


_SCALE = (QK_NOPE + QK_ROPE) ** -0.5
_LOG2E = 1.4426950408889634
_C2 = _SCALE * _LOG2E


def _row_blocks(tq):
    rb = 64 if tq % 64 == 0 else tq
    return [(r, rb) for r in range(0, tq, rb)]


def _attn_fwd(q, k, v, S, tq):
    n = q.shape[0]
    nb, nq = n // S, S // tq

    def body(q_ref, k_ref, v_ref, o_ref, lse_ref):
        qi = pl.program_id(2)
        qv = q_ref[...]

        def tile(j, c, diagonal):
            m, l, acc = c
            off = pl.multiple_of(j * tq, tq)
            s = _dot_nt(qv, k_ref[pl.ds(off, tq), :]) * _C2
            if diagonal:
                rows = lax.broadcasted_iota(jnp.int32, s.shape, 0)
                cols = lax.broadcasted_iota(jnp.int32, s.shape, 1)
                s = jnp.where(cols <= rows, s, NEG)
            mn = jnp.maximum(m, jnp.max(s, axis=1, keepdims=True))
            p = jnp.exp2(s - mn)
            al = jnp.exp2(m - mn)
            l = al * l + jnp.sum(p, axis=1, keepdims=True)
            acc = al * acc + _dot(p.astype(BF16), v_ref[pl.ds(off, tq), :])
            return mn, l, acc

        init = (jnp.full((tq, 1), NEG, F32), jnp.zeros((tq, 1), F32), jnp.zeros((tq, HP), F32))
        c = lax.fori_loop(0, qi, lambda j, c: tile(j, c, False), init)
        m, l, acc = tile(qi, c, True)
        o_ref[...] = acc / l
        lse_ref[...] = jnp.broadcast_to(m + jnp.log(l) * _LOG2E, (tq, HP))

    qs = pl.BlockSpec((tq, HP), lambda b, h, i: (b * nq + i, h))
    ks = pl.BlockSpec((S, HP), lambda b, h, i: (b, h))
    return pl.pallas_call(
        body, name="attn_fwd", grid=(nb, NH, nq),
        out_shape=[jax.ShapeDtypeStruct((n, NH * HP), F32)] * 2,
        in_specs=[qs, ks, ks], out_specs=[qs, qs],
        compiler_params=_cp(("parallel", "parallel", "arbitrary")),
    )(q, k, v)


def _rows_of(x, pick):
    x1 = x.astype(BF16)
    r1 = x - x1.astype(F32)
    x2 = r1.astype(BF16)
    x3 = (r1 - x2.astype(F32)).astype(BF16)
    return _dot_nt(pick, x1) + _dot_nt(pick, x2) + _dot_nt(pick, x3)


def _attn_rows(o, do, lse, S, tq):
    n = o.shape[0]
    nb, nq = n // S, S // tq

    def body(o_ref, do_ref, lse_ref, lr_ref, dr_ref, dob_ref):
        dov = do_ref[...]
        dob_ref[...] = dov.astype(BF16)
        lane = lax.broadcasted_iota(jnp.int32, (8, HP), 1)
        lr_ref[...] = _rows_of(lse_ref[...], jnp.where(lane == 0, 1.0, 0.0).astype(BF16))
        dr_ref[...] = _rows_of(dov * o_ref[...] * _SCALE, jnp.ones((8, HP), BF16))

    ts = pl.BlockSpec((tq, HP), lambda b, h, i: (b * nq + i, h))
    rs = pl.BlockSpec((8, tq), lambda b, h, i: (b * NH + h, i))
    return pl.pallas_call(
        body, name="attn_rows", grid=(nb, NH, nq),
        out_shape=[jax.ShapeDtypeStruct((nb * NH * 8, S), F32)] * 2 + [jax.ShapeDtypeStruct((n, NH * HP), BF16)],
        in_specs=[ts, ts, ts], out_specs=[rs, rs, ts],
        compiler_params=_cp(("parallel", "parallel", "parallel")),
    )(o, do, lse)


def _attn_bwd(q, k, v, dob, lrow, drow, S, tq):
    n = q.shape[0]
    nb, nq = n // S, S // tq

    def body(q_ref, k_ref, v_ref, do_ref, lr_ref, dr_ref, dq_ref, dk_ref, dv_ref):
        kj = pl.program_id(2)

        @pl.when(kj == 0)
        def _():
            dq_ref[...] = jnp.zeros_like(dq_ref)

        kt = k_ref[...]
        vt = v_ref[...]

        def tile(i, c, diagonal):
            dk, dv = c
            off = pl.multiple_of(i * tq, tq)
            qv = q_ref[pl.ds(off, tq), :]
            dob = do_ref[pl.ds(off, tq), :]
            lr = lr_ref[0:1, pl.ds(off, tq)]
            dr = dr_ref[0:1, pl.ds(off, tq)]
            st = _dot_nt(kt, qv)
            dpt = _dot_nt(vt, dob)
            pt = jnp.exp2(st * _C2 - lr)
            if diagonal:
                keys = lax.broadcasted_iota(jnp.int32, pt.shape, 0)
                qrys = lax.broadcasted_iota(jnp.int32, pt.shape, 1)
                pt = jnp.where(keys <= qrys, pt, 0.0)
            dst = (pt * (dpt * _SCALE - dr)).astype(BF16)
            dq_ref[pl.ds(off, tq), :] += _dot_tn(dst, kt)
            return dk + _dot(dst, qv), dv + _dot(pt.astype(BF16), dob)

        zero = jnp.zeros((tq, HP), F32)
        c = tile(kj, (zero, zero), True)
        dk, dv = lax.fori_loop(kj + 1, nq, lambda i, c: tile(i, c, False), c)
        dk_ref[...] = dk
        dv_ref[...] = dv

    ts = pl.BlockSpec((tq, HP), lambda b, h, i: (b * nq + i, h))
    fs = pl.BlockSpec((S, HP), lambda b, h, i: (b, h))
    rs = pl.BlockSpec((8, S), lambda b, h, i: (b * NH + h, 0))
    return pl.pallas_call(
        body, name="attn_bwd", grid=(nb, NH, nq),
        out_shape=[jax.ShapeDtypeStruct((n, NH * HP), F32)] * 3,
        in_specs=[fs, ts, ts, fs, rs, rs], out_specs=[fs, ts, ts],
        compiler_params=_cp(("parallel", "parallel", "arbitrary")),
    )(q, k, v, dob, lrow, drow)


def _p1_fwd(yssm, oattn, x, modp, gs, ga, w_out, g2, S, tm):
    n = x.shape[0]
    tps = S // tm

    def body(ys_ref, oa_ref, x_ref, mod_ref, gs_ref, ga_ref, w_ref, g2_ref, yn_ref, o_ref, x1_ref, h2_ref):
        yh, _ = _rms(ys_ref[...], D_SSM)
        ah, _ = _rms(oa_ref[...], D_ATTN)
        yn = jnp.concatenate([yh * gs_ref[...], ah * ga_ref[...]], axis=1).astype(BF16)
        yn_ref[...] = yn
        o = _dot(yn, w_ref[...])
        o_ref[...] = o
        x1 = x_ref[...] + mod_ref[0, 2:3, :] * o
        x1_ref[...] = x1
        xh, _ = _rms(x1, D)
        h2_ref[...] = ((xh * g2_ref[...]) * (1.0 + mod_ref[0, 4:5, :]) + mod_ref[0, 3:4, :]).astype(BF16)

    row = lambda w: pl.BlockSpec((tm, w), lambda i: (i, 0))
    return pl.pallas_call(
        body, name="p1_fwd", grid=(n // tm,),
        out_shape=[jax.ShapeDtypeStruct((n, D_SSM + NH * HP), BF16), jax.ShapeDtypeStruct((n, D), F32),
                   jax.ShapeDtypeStruct((n, D), F32), jax.ShapeDtypeStruct((n, D), BF16)],
        in_specs=[row(D_SSM), row(NH * HP), row(D), pl.BlockSpec((1, 8, D), lambda i: (i // tps, 0, 0)),
                  _VM, _VM, _VM, _VM],
        out_specs=[row(D_SSM + NH * HP), row(D), row(D), row(D)],
        compiler_params=_cp(("parallel",)),
    )(yssm, oattn, x, modp, gs, ga, w_out, g2)


def _p2(x1, h2, target, modp, g2, gf, w_ff1, w_ff2, S, tm):
    n = x1.shape[0]
    tps = S // tm
    nb = n // S

    def body(x1_ref, h2_ref, t_ref, mod_ref, g2_ref, gf_ref, w1_ref, w2_ref,
             dx1_ref, r_ref, da_ref, dff_ref, accs_ref, accg_ref):
        i = pl.program_id(0)
        sh2, sc2, gt2 = mod_ref[0, 3:4, :], mod_ref[0, 4:5, :], mod_ref[0, 5:6, :]
        fsh, fsc = mod_ref[0, 6:7, :], mod_ref[0, 7:8, :]
        x1 = x1_ref[...]
        a = _dot(h2_ref[...], w1_ref[...])
        ra = jnp.maximum(a, 0.0)
        rb = (ra * ra).astype(BF16)
        r_ref[...] = rb
        ff = _dot(rb, w2_ref[...])
        x2 = x1 + gt2 * ff
        x2h, rf = _rms(x2, D)
        gf_v = gf_ref[...]
        outn = x2h * gf_v
        err = outn * (1.0 + fsc) + fsh - t_ref[...]
        dout = err * (1.0 / D)
        doutn = dout * (1.0 + fsc)
        dx2 = _rms_bwd(doutn * gf_v, x2h, rf, D)
        dff = (gt2 * dx2).astype(BF16)
        dff_ref[...] = dff
        dr = _dot_nt(dff, w2_ref[...])
        da = (dr * (2.0 * ra)).astype(BF16)
        da_ref[...] = da
        dh2 = _dot_nt(da, w1_ref[...])
        x1h, r2 = _rms(x1, D)
        g2_v = g2_ref[...]
        dn2 = dh2 * (1.0 + sc2)
        dx1_ref[...] = dx2 + _rms_bwd(dn2 * g2_v, x1h, r2, D)

        @pl.when(i % tps == 0)
        def _():
            accs_ref[...] = jnp.zeros_like(accs_ref)

        @pl.when(i == 0)
        def _():
            accg_ref[...] = jnp.zeros_like(accg_ref)

        accs_ref[0, 3:4, :] += _colsum(dh2)
        accs_ref[0, 4:5, :] += _colsum(dh2 * (x1h * g2_v))
        accs_ref[0, 5:6, :] += _colsum(dx2 * ff)
        accs_ref[0, 6:7, :] += _colsum(dout)
        accs_ref[0, 7:8, :] += _colsum(dout * outn)
        accg_ref[0:1, :] += _colsum(dn2 * x1h)
        accg_ref[1:2, :] += _colsum(doutn * x2h)
        accg_ref[2:3, :] += _colsum(err * err) * (0.5 / D)

    row = lambda w: pl.BlockSpec((tm, w), lambda i: (i, 0))
    return pl.pallas_call(
        body, name="p2_mlp_loss", grid=(n // tm,),
        out_shape=[jax.ShapeDtypeStruct((n, D), F32), jax.ShapeDtypeStruct((n, D_FF), BF16),
                   jax.ShapeDtypeStruct((n, D_FF), BF16), jax.ShapeDtypeStruct((n, D), BF16),
                   jax.ShapeDtypeStruct((nb, 8, D), F32), jax.ShapeDtypeStruct((8, D), F32)],
        in_specs=[row(D), row(D), row(D), pl.BlockSpec((1, 8, D), lambda i: (i // tps, 0, 0)), _VM, _VM, _VM, _VM],
        out_specs=[row(D), row(D_FF), row(D_FF), row(D), pl.BlockSpec((1, 8, D), lambda i: (i // tps, 0, 0)),
                   pl.BlockSpec((8, D), lambda i: (0, 0))],
        compiler_params=_cp(("arbitrary",)),
    )(x1, h2, target, modp, g2, gf, w_ff1, w_ff2)


def _p3_bwd(dx1, o, yssm, oattn, modp, gs, ga, w_out, S, tm):
    n = dx1.shape[0]
    tps = S // tm
    nb = n // S

    def body(dx1_ref, o_ref, ys_ref, oa_ref, mod_ref, gs_ref, ga_ref, w_ref,
             do_ref, dys_ref, doa_ref, accs_ref, accg_ref):
        i = pl.program_id(0)
        dx1 = dx1_ref[...]
        dob = (mod_ref[0, 2:3, :] * dx1).astype(BF16)
        do_ref[...] = dob
        dyn = _dot_nt(dob, w_ref[...])
        yh, rs = _rms(ys_ref[...], D_SSM)
        ah, ra = _rms(oa_ref[...], D_ATTN)
        d1 = dyn[:, 0:D_SSM]
        d2 = dyn[:, D_SSM:D_SSM + NH * HP]
        dys_ref[...] = _rms_bwd(d1 * gs_ref[...], yh, rs, D_SSM)
        doa_ref[...] = _rms_bwd(d2 * ga_ref[...], ah, ra, D_ATTN)

        @pl.when(i % tps == 0)
        def _():
            accs_ref[...] = jnp.zeros_like(accs_ref)

        @pl.when(i == 0)
        def _():
            accg_ref[...] = jnp.zeros_like(accg_ref)

        accs_ref[0, 2:3, :] += _colsum(dx1 * o_ref[...])
        accg_ref[0:1, 0:D_SSM] += _colsum(d1 * yh)
        accg_ref[1:2, :] += _colsum(d2 * ah)

    row = lambda w: pl.BlockSpec((tm, w), lambda i: (i, 0))
    return pl.pallas_call(
        body, name="p3_bwd", grid=(n // tm,),
        out_shape=[jax.ShapeDtypeStruct((n, D), BF16), jax.ShapeDtypeStruct((n, D_SSM), F32),
                   jax.ShapeDtypeStruct((n, NH * HP), F32), jax.ShapeDtypeStruct((nb, 8, D), F32),
                   jax.ShapeDtypeStruct((8, NH * HP), F32)],
        in_specs=[row(D), row(D), row(D_SSM), row(NH * HP), pl.BlockSpec((1, 8, D), lambda i: (i // tps, 0, 0)),
                  _VM, _VM, _VM],
        out_specs=[row(D), row(D_SSM), row(NH * HP), pl.BlockSpec((1, 8, D), lambda i: (i // tps, 0, 0)),
                   pl.BlockSpec((8, NH * HP), lambda i: (0, 0))],
        compiler_params=_cp(("arbitrary",)),
    )(dx1, o, yssm, oattn, modp, gs, ga, w_out)


def _wgrad(a, b, name, col_slots=0):
    n, k1 = a.shape
    k2 = b.shape[1]
    bn = 512 if n % 512 == 0 else n
    bk1 = 512 if k1 % 512 == 0 else k1
    bk2 = k2 // col_slots if col_slots else (1024 if (k2 % 1024 == 0) else k2)

    def body(a_ref, b_ref, o_ref):
        @pl.when(pl.program_id(2) == 0)
        def _():
            o_ref[...] = jnp.zeros_like(o_ref)

        o_ref[...] += _dot_tn(a_ref[...], b_ref[...]).reshape(o_ref.shape)

    if col_slots:
        out_shape = jax.ShapeDtypeStruct((col_slots, k1, bk2), F32)
        out_spec = pl.BlockSpec((1, bk1, bk2), lambda i, j, t: (j, i, 0))
    else:
        out_shape = jax.ShapeDtypeStruct((k1, k2), F32)
        out_spec = pl.BlockSpec((bk1, bk2), lambda i, j, t: (i, j))
    return pl.pallas_call(
        body, name=name, grid=(k1 // bk1, k2 // bk2, n // bn),
        out_shape=out_shape,
        in_specs=[pl.BlockSpec((bn, bk1), lambda i, j, t: (t, i)), pl.BlockSpec((bn, bk2), lambda i, j, t: (t, j))],
        out_specs=out_spec,
        compiler_params=_cp(("parallel", "parallel", "arbitrary")),
    )(a, b)


def _row_block(rows):
    if rows <= 256:
        return rows
    return next(b for b in (256, 192, 128, 64, 32, 16, 8) if rows % b == 0)


def _add_half(g, recv, cidx, name):
    _, rows2, w = g.shape
    rows = rows2 // 2
    br = _row_block(rows)
    nblk = rows // br

    def body(c_ref, g_ref, r_ref, o_ref):
        o_ref[...] = (g_ref[...] + r_ref[...]).astype(BF16)

    return pl.pallas_call(
        body, name=name,
        grid_spec=pltpu.PrefetchScalarGridSpec(
            num_scalar_prefetch=1, grid=(4, nblk),
            in_specs=[pl.BlockSpec((1, br, w), lambda s, i, c: (s, c[0] * nblk + i, 0)),
                      pl.BlockSpec((1, br, w), lambda s, i, c: (s, i, 0))],
            out_specs=pl.BlockSpec((1, br, w), lambda s, i, c: (s, i, 0))),
        out_shape=jax.ShapeDtypeStruct((4, rows, w), BF16),
        compiler_params=_cp(("parallel", "parallel")),
    )(cidx, g, recv)


def _add_chips(r, name):
    _, rows, w = r.shape
    br = _row_block(rows)

    def body(r_ref, o_ref):
        f = lambda k: r_ref[k].astype(F32)
        o_ref[...] = ((f(0) + f(1)) + f(2)) + f(3)

    return pl.pallas_call(
        body, name=name, grid=(rows // br,),
        out_shape=jax.ShapeDtypeStruct((rows, w), F32),
        in_specs=[pl.BlockSpec((4, br, w), lambda i: (0, i, 0))],
        out_specs=pl.BlockSpec((br, w), lambda i: (i, 0)),
        compiler_params=_cp(("parallel",)),
    )(r)


def _sum_devices(a):
    def body(a_ref, o_ref):
        acc = a_ref[0:1, :]
        for k in range(1, 8):
            acc = acc + a_ref[k:k + 1, :]
        o_ref[...] = acc

    return pl.pallas_call(
        body, name="small_grad_sum", out_shape=jax.ShapeDtypeStruct((1, a.shape[1]), F32),
        in_specs=[_VM], out_specs=_VM, compiler_params=_cp(),
    )(a)


def _adamw_math(wv, gv, mv, vv):
    m_new = ADAM_B1 * mv + (1.0 - ADAM_B1) * gv
    v_new = ADAM_B2 * vv + (1.0 - ADAM_B2) * (gv * gv)
    m_hat = m_new / (1.0 - ADAM_B1 ** ADAM_STEP)
    v_hat = v_new / (1.0 - ADAM_B2 ** ADAM_STEP)
    return -ADAM_LR * (m_hat / (jnp.sqrt(v_hat) + ADAM_EPS) + ADAM_WD * wv), m_new, v_new


def _adamw_small(ws, gs, ms, vs):
    k = len(ws)

    def body(*refs):
        ins, outs = refs[:4 * k], refs[4 * k:]
        for t in range(k):
            d, m_new, v_new = _adamw_math(ins[t][...], ins[k + t][...], ins[2 * k + t][...], ins[3 * k + t][...])
            outs[t][...] = d
            outs[k + t][...] = m_new
            outs[2 * k + t][...] = v_new

    shapes = [jax.ShapeDtypeStruct(w.shape, F32) for w in ws]
    return pl.pallas_call(
        body, name="adamw_small", out_shape=shapes * 3,
        in_specs=[_VM] * (4 * k), out_specs=[_VM] * (3 * k), compiler_params=_cp(),
    )(*ws, *gs, *ms, *vs)


def _adamw(w, g, m, v, name):
    rows, wd = w.shape
    br = _row_block(rows)

    def body(w_ref, g_ref, m_ref, v_ref, d_ref, nm_ref, nv_ref):
        d, m_new, v_new = _adamw_math(w_ref[...], g_ref[...], m_ref[...], v_ref[...])
        d_ref[...] = d
        nm_ref[...] = m_new
        nv_ref[...] = v_new

    spec = pl.BlockSpec((br, wd), lambda i: (i, 0))
    return pl.pallas_call(
        body, name=name, grid=(rows // br,),
        out_shape=[jax.ShapeDtypeStruct((rows, wd), F32)] * 3,
        in_specs=[spec] * 4, out_specs=[spec] * 3,
        compiler_params=_cp(("parallel",)),
    )(w, g, m, v)


def _adamw_halves(w, mine, other, m, v, cidx, name):
    rows, wd = w.shape
    h = rows // 2
    br = _row_block(h)
    nblk = h // br

    def body(c_ref, w_ref, a_ref, b_ref, m_ref, v_ref, g_ref, d_ref, nm_ref, nv_ref):
        gv = jnp.where(pl.program_id(0) == c_ref[0], a_ref[...], b_ref[...])
        d, m_new, v_new = _adamw_math(w_ref[...], gv, m_ref[...], v_ref[...])
        g_ref[...] = gv
        d_ref[...] = d
        nm_ref[...] = m_new
        nv_ref[...] = v_new

    full = pl.BlockSpec((br, wd), lambda hf, i, c: (hf * nblk + i, 0))
    half = pl.BlockSpec((br, wd), lambda hf, i, c: (i, 0))
    return pl.pallas_call(
        body, name=name,
        grid_spec=pltpu.PrefetchScalarGridSpec(
            num_scalar_prefetch=1, grid=(2, nblk),
            in_specs=[full, half, half, full, full], out_specs=[full] * 4),
        out_shape=[jax.ShapeDtypeStruct((rows, wd), F32)] * 4,
        compiler_params=_cp(("parallel", "parallel")),
    )(cidx, w, mine, other, m, v)


def _other_chips(x, y):
    return [(1 - x, y), (x, 1 - y), (1 - x, 1 - y)]


def _other_devices(x, y, c):
    flip = lambda v, d: (1 - v) if d else v
    return [(flip(x, dx), flip(y, dy), flip(c, dc))
            for dx in (0, 1) for dy in (0, 1) for dc in (0, 1) if (dx, dy, dc) != (0, 0, 0)]


def _exchange(name, ins, out_shapes, n_local, n_remote, plan):
    ni, no = len(ins), len(out_shapes)

    def body(*refs):
        in_refs, out_refs = refs[:ni], refs[ni:ni + no]
        send_sems, recv_sems, local_sems = refs[ni + no:]
        x, y, c = lax.axis_index("x"), lax.axis_index("y"), lax.axis_index("c")
        local, remote = plan(in_refs, out_refs, x, y, c)
        assert len(local) == n_local and len(remote) == n_remote

        def push(k, src, dst, dev):
            return pltpu.make_async_remote_copy(src_ref=src, dst_ref=dst, send_sem=send_sems.at[k],
                                                recv_sem=recv_sems.at[k], device_id=dev, device_id_type=MESH)

        own = [pltpu.make_async_copy(s, d, local_sems.at[i]) for i, (s, d) in enumerate(local)]
        for cp in own:
            cp.start()
        sends = [push(k, s, d, dev) for k, (s, d, dev, _) in enumerate(remote)]
        for cp in sends:
            cp.start()
        for k, (s, _, dev, landing) in enumerate(remote):
            push(k, s, landing, dev).wait_recv()
        for cp in sends:
            cp.wait_send()
        for cp in own:
            cp.wait()

    return pl.pallas_call(
        body, name=name, out_shape=out_shapes,
        in_specs=[_ANY] * ni, out_specs=[_ANY] * no,
        scratch_shapes=[pltpu.SemaphoreType.DMA((n_remote,)), pltpu.SemaphoreType.DMA((n_remote,)),
                        pltpu.SemaphoreType.DMA((max(n_local, 1),))],
        compiler_params=pltpu.CompilerParams(has_side_effects=True),
    )(*ins)


def _gather_chips(name, shards, everyone=()):
    ns, ne = len(shards), len(everyone)
    outs = [jax.ShapeDtypeStruct((4,) + a.shape, a.dtype) for a in shards]
    outs += [jax.ShapeDtypeStruct((8,) + a.shape, a.dtype) for a in everyone]

    def plan(i, o, x, y, c):
        mine, me = 2 * x + y, 4 * x + 2 * y + c
        local, remote = [], []
        for t in range(ns):
            local.append((i[t], o[t].at[mine]))
            for px, py in _other_chips(x, y):
                remote.append((i[t], o[t].at[mine], (px, py, c), o[t].at[2 * px + py]))
        for t in range(ns, ns + ne):
            local.append((i[t], o[t].at[me]))
            for px, py, pc in _other_devices(x, y, c):
                remote.append((i[t], o[t].at[me], (px, py, pc), o[t].at[4 * px + 2 * py + pc]))
        return local, remote

    return _exchange(name, list(shards) + list(everyone), outs, ns + ne, 3 * ns + 7 * ne, plan)


def _swap_halves(gs, everyone):
    ns, ne = len(gs), len(everyone)
    outs = [jax.ShapeDtypeStruct((4, g.shape[1] // 2, g.shape[2]), g.dtype) for g in gs]
    outs += [jax.ShapeDtypeStruct((8,) + a.shape, a.dtype) for a in everyone]

    def plan(i, o, x, y, c):
        me = 4 * x + 2 * y + c
        local, remote = [], []
        for t in range(ns):
            h = gs[t].shape[1] // 2
            theirs = i[t].at[:, pl.ds(pl.multiple_of((1 - c) * h, 8), h), :]
            remote.append((theirs, o[t], (x, y, 1 - c), o[t]))
        for t in range(ns, ns + ne):
            local.append((i[t], o[t].at[me]))
            for px, py, pc in _other_devices(x, y, c):
                remote.append((i[t], o[t].at[me], (px, py, pc), o[t].at[4 * px + 2 * py + pc]))
        return local, remote

    return _exchange("grad_swap_sibling", list(gs) + list(everyone), outs, ne, ns + 7 * ne, plan)


def _scatter_chips(parts):
    ns = len(parts)
    outs = [jax.ShapeDtypeStruct(a.shape, a.dtype) for a in parts]

    def plan(i, o, x, y, c):
        mine = 2 * x + y
        local, remote = [], []
        for t in range(ns):
            local.append((i[t].at[mine], o[t].at[mine]))
            for px, py in _other_chips(x, y):
                remote.append((i[t].at[2 * px + py], o[t].at[mine], (px, py, c), o[t].at[2 * px + py]))
        return local, remote

    return _exchange("grad_scatter_chips", list(parts), outs, ns, 3 * ns, plan)


def _join_halves(halves):
    ns = len(halves)
    outs = [jax.ShapeDtypeStruct(a.shape, a.dtype) for a in halves]

    def plan(i, o, x, y, c):
        return [], [(i[t], o[t], (x, y, 1 - c), o[t]) for t in range(ns)]

    return _exchange("grad_join_sibling", list(halves), outs, 0, ns, plan)


def _pad_heads_cols(w, per, used):
    k = w.shape[0]
    w = w.reshape(k, NH, per)[:, :, :used]
    return jnp.pad(w, ((0, 0), (0, 0), (0, HP - used))).reshape(k, NH * HP)


def _unpad_heads_cols(w, used):
    k = w.shape[0]
    return w.reshape(k, NH, HP)[:, :, :used]


def _prep_weights(wf):
    bf = lambda a: a.astype(BF16)
    out = {}
    out["w_in"] = jnp.pad(bf(wf["w_in"]), ((0, 0), (0, IN_PAD - IN_COLS)))
    out["w_glu"] = bf(wf["w_glu"])
    out["w_uq"] = _pad_heads_cols(bf(wf["w_uq"]), QK_NOPE + QK_ROPE, QK_NOPE + QK_ROPE)
    wkv = bf(wf["w_ukv"]).reshape(KV_LORA, NH, QK_NOPE + V_HEAD)
    wk = jnp.pad(wkv[:, :, :QK_NOPE], ((0, 0), (0, 0), (0, HP - QK_NOPE))).reshape(KV_LORA, NH * HP)
    wv = jnp.pad(wkv[:, :, QK_NOPE:], ((0, 0), (0, 0), (0, HP - V_HEAD))).reshape(KV_LORA, NH * HP)
    out["w_ukv"] = jnp.concatenate([wk, wv], axis=1)
    wo = bf(wf["w_out"])
    wo_a = jnp.pad(wo[D_SSM:].reshape(NH, V_HEAD, D), ((0, 0), (0, HP - V_HEAD), (0, 0))).reshape(NH * HP, D)
    out["w_out"] = jnp.concatenate([wo[:D_SSM], wo_a], axis=0)
    out["w_ff1"] = bf(wf["w_ff1"])
    out["w_ff2"] = bf(wf["w_ff2"])
    return out


def _rope_tables(positions):
    inv_freq = ROPE_BASE ** (-jnp.arange(0, QK_ROPE, 2, dtype=F32) / QK_ROPE)
    ang = positions.astype(F32)[:, None] * inv_freq
    cos, sin = jnp.cos(ang), jnp.sin(ang)
    n = positions.shape[0]
    one = jnp.ones((n, QK_NOPE), F32)
    z16 = jnp.zeros((n, 16), F32)
    z32 = jnp.zeros((n, 32), F32)
    z64 = jnp.zeros((n, QK_NOPE), F32)
    rc = jnp.concatenate([one, cos, cos, z32], axis=1)
    rs1 = jnp.concatenate([z64, -sin, z16, z32], axis=1)
    rs2 = jnp.concatenate([z64, z16, sin, z32], axis=1)
    return rc, rs1, rs2


def _permute_rows(a, S):
    n, w = a.shape
    return a.reshape(n // S, 8, S // 8, w).transpose(0, 2, 1, 3).reshape(n, w)


def _unpermute_rows(a, S):
    n, w = a.shape
    return a.reshape(n // S, S // 8, 8, w).transpose(0, 2, 1, 3).reshape(n, w)


def _block_diag_in(bb):
    eye = jnp.eye(G, dtype=bb.dtype)
    return jnp.einsum("gph,gk->ghkp", bb, eye).reshape(G * H, G * P)


def _block_diag_out(cc):
    eye = jnp.eye(G, dtype=cc.dtype)
    return jnp.einsum("ghp,gk->gpkh", cc, eye).reshape(G * P, G * H)


def _slots(full):
    r, cdim = full.shape
    return full.reshape(r, 4, cdim // 4).transpose(1, 0, 2)


def _unslots(g):
    s, r, cs = g.shape
    return g.transpose(1, 0, 2).reshape(r, s * cs)


def _local_step(x, positions, target, modp, wf):
    nb, S, _ = x.shape
    n = nb * S
    tm = min(256, S)
    tt = min(256, S)
    tq = min(512, S // 2)
    kw = _prep_weights(wf)
    row = lambda a: a.reshape(1, -1).astype(F32)

    xf = x.reshape(n, D)
    tf = target.reshape(n, D)
    g1, g2, gf = row(wf["norm1_g"]), row(wf["norm2_g"]), row(wf["final_norm_g"])
    h1, proj = _f1_fwd(xf, modp, g1, kw["w_in"], S, tm)

    col = lambda a: a.reshape(NST, 1)
    lam_re, lam_im = col(wf["ssm_lambda_re"]), col(wf["ssm_lambda_im"])
    logdt = jnp.repeat(wf["ssm_log_dt"].reshape(G, 1), P, axis=1).reshape(NST, 1)
    b_re, b_im = wf["ssm_b_re"].reshape(NST, H), wf["ssm_b_im"].reshape(NST, H)
    lbr, lbi, bbr, bbi = _ssm_param_fwd(lam_re, lam_im, logdt, b_re, b_im)
    lre8 = jnp.broadcast_to(lbr.reshape(1, NST), (8, NST))
    lim8 = jnp.broadcast_to(lbi.reshape(1, NST), (8, NST))
    bm = jnp.concatenate([_block_diag_in(bbr.reshape(G, P, H)), _block_diag_in(bbi.reshape(G, P, H))],
                         axis=1).astype(BF16)
    cm = jnp.concatenate([_block_diag_out(wf["ssm_c_re"]), -_block_diag_out(wf["ssm_c_im"])], axis=0).astype(BF16)
    dvec = row(wf["ssm_d"])
    u_p = _permute_rows(proj[:, :D_SSM], S)
    fcr, fci = _ssm_local(u_p, bm, lre8, lim8, S, tt)
    st, ypre, z, gact, yssm_p = _ssm_fwd(u_p, fcr, fci, bm, cm, dvec, kw["w_glu"], lre8, lim8, S, tt)
    yssm = _unpermute_rows(yssm_p, S)

    rc, rs1, rs2 = _rope_tables(positions.reshape(n))
    gq, gkv = row(wf["q_norm_g"]), row(wf["kv_norm_g"])
    q, k, v, qn, kvn = _mla_fwd(proj, rc, rs1, rs2, gq, gkv, kw["w_uq"], kw["w_ukv"], tm)
    oattn, lse = _attn_fwd(q, k, v, S, tq)

    gs = row(wf["ssm_out_g"])
    ga = jnp.pad(wf["attn_out_g"].reshape(NH, V_HEAD), ((0, 0), (0, HP - V_HEAD))).reshape(1, NH * HP)
    yn, o, x1, h2 = _p1_fwd(yssm, oattn, xf, modp, gs, ga, kw["w_out"], g2, S, tm)
    dx1, r, da, dff, accs2, accg2 = _p2(x1, h2, tf, modp, g2, gf, kw["w_ff1"], kw["w_ff2"], S, tm)
    loss = jnp.sum(accg2[2])
    do, dyssm, doattn, accs3, accg3 = _p3_bwd(dx1, o, yssm, oattn, modp, gs, ga, kw["w_out"], S, tm)

    lrow, drow, dob = _attn_rows(oattn, doattn, lse, S, tq)
    dq, dk, dv = _attn_bwd(q, k, v, dob, lrow, drow, S, tq)
    dmla, dqb, dkvb, accm = _mla_bwd(dq, dk, dv, proj, rc, rs1, rs2, gq, gkv, kw["w_uq"], kw["w_ukv"], tm)

    dys_p = _permute_rows(dyssm, S)
    dy, dz, air, aii = _ssm_bwd_a(dys_p, z, ypre, kw["w_glu"], cm, lre8, lim8, S, tt)
    du_p, dcm, dbm, dd, dlr, dli = _ssm_bwd_b(dy, u_p, st, fcr, fci, air, aii, bm, cm, dvec, lre8, lim8, S, tt)
    du = _unpermute_rows(du_p, S)
    dcm = dcm.reshape(2, 4, 8, P, 8, H)
    dc_re = jnp.einsum("qgpgh->qghp", dcm[0]).reshape(G, H, P)
    dc_im = -jnp.einsum("qgpgh->qghp", dcm[1]).reshape(G, H, P)
    dbm = dbm.reshape(8, H, 2, 4, 8, P)
    dbb_re = jnp.einsum("ghqgp->qgph", dbm[:, :, 0]).reshape(NST, H)
    dbb_im = jnp.einsum("ghqgp->qgph", dbm[:, :, 1]).reshape(NST, H)
    gb_re, gb_im, glr, gli, gdt = _ssm_param_bwd(lam_re, lam_im, logdt, b_re, b_im, dlr.reshape(NST, 1),
                                                 dli.reshape(NST, 1), dbb_re, dbb_im)
    glogdt = _rowsum(gdt.reshape(G, P))

    dx, dproj, accs1, accg1 = _f1_bwd(du, dmla, dx1, xf, modp, g1, kw["w_in"], S, tm)

    big = {}
    big["w_in"] = _slots(_wgrad(h1, dproj, "wgrad_in")[:, :IN_COLS])
    big["w_glu"] = _wgrad(gact, dz, "wgrad_glu", col_slots=4)
    big["w_uq"] = _slots(_unpad_heads_cols(_wgrad(qn, dqb, "wgrad_uq"), QK_NOPE + QK_ROPE).reshape(Q_LORA, -1))
    gkvw = _wgrad(kvn, dkvb, "wgrad_ukv")
    big["w_ukv"] = _slots(jnp.concatenate([_unpad_heads_cols(gkvw[:, :NH * HP], QK_NOPE),
                                           _unpad_heads_cols(gkvw[:, NH * HP:], V_HEAD)], axis=2).reshape(KV_LORA, -1))
    gwo = _wgrad(yn, do, "wgrad_out")
    big["w_out"] = jnp.concatenate([gwo[:D_SSM].reshape(2, D_SSM // 2, D),
                                    gwo[D_SSM:].reshape(2, NH // 2 * HP, D).reshape(2, NH // 2, HP, D)[:, :, :V_HEAD]
                                    .reshape(2, D_ATTN // 2, D)], axis=0)
    big["w_ff1"] = _wgrad(h2, da, "wgrad_ff1", col_slots=4)
    big["w_ff2"] = _wgrad(r, dff, "wgrad_ff2").reshape(4, D_FF // 4, D)

    small = {}
    small["norm1_g"] = accg1[0:1]
    small["norm2_g"] = accg2[0:1]
    small["final_norm_g"] = accg2[1:2]
    small["ssm_out_g"] = accg3[0:1, :D_SSM]
    small["attn_out_g"] = accg3[1].reshape(NH, HP)[:, :V_HEAD].reshape(1, D_ATTN)
    small["q_norm_g"] = accm[0:1, :Q_LORA]
    small["kv_norm_g"] = accm[1:2, :KV_LORA]
    small["ssm_lambda_re"] = glr.reshape(G, P)
    small["ssm_lambda_im"] = gli.reshape(G, P)
    small["ssm_b_re"] = gb_re
    small["ssm_b_im"] = gb_im
    small["ssm_c_re"] = dc_re.reshape(G * H, P)
    small["ssm_c_im"] = dc_im.reshape(G * H, P)
    small["ssm_d"] = dd.reshape(G, H)
    small["ssm_log_dt"] = glogdt.reshape(1, G)
    return loss, dx.reshape(nb, S, D), big, small, accs1 + accs2 + accs3


def _view2d(a):
    return a.reshape(-1, a.shape[-1]) if a.ndim > 1 else a.reshape(1, -1)


def kernel(x, c, positions, ada_w, ada_b, norm1_g, w_in, ssm_lambda_re, ssm_lambda_im, ssm_b_re, ssm_b_im, ssm_c_re, ssm_c_im, ssm_d, ssm_log_dt, w_glu, q_norm_g, w_uq, kv_norm_g, w_ukv, ssm_out_g, attn_out_g, w_out, norm2_g, w_ff1, w_ff2, final_ada_w, final_ada_b, final_norm_g, loss_target, m_ada_w, m_ada_b, m_norm1_g, m_w_in, m_ssm_lambda_re, m_ssm_lambda_im, m_ssm_b_re, m_ssm_b_im, m_ssm_c_re, m_ssm_c_im, m_ssm_d, m_ssm_log_dt, m_w_glu, m_q_norm_g, m_w_uq, m_kv_norm_g, m_w_ukv, m_ssm_out_g, m_attn_out_g, m_w_out, m_norm2_g, m_w_ff1, m_w_ff2, m_final_ada_w, m_final_ada_b, m_final_norm_g, v_ada_w, v_ada_b, v_norm1_g, v_w_in, v_ssm_lambda_re, v_ssm_lambda_im, v_ssm_b_re, v_ssm_b_im, v_ssm_c_re, v_ssm_c_im, v_ssm_d, v_ssm_log_dt, v_w_glu, v_q_norm_g, v_w_uq, v_kv_norm_g, v_w_ukv, v_ssm_out_g, v_attn_out_g, v_w_out, v_norm2_g, v_w_ff1, v_w_ff2, v_final_ada_w, v_final_ada_b, v_final_norm_g):
    args = dict(locals())
    names = list(inspect.signature(kernel).parameters)
    wnames = names[3:names.index("loss_target")]
    small_names = [nm for nm in wnames if nm not in GATHERED and nm not in TP]
    reduced_names = [nm for nm in small_names if nm not in ("ada_b", "final_ada_b")]
    w = {nm: args[nm] for nm in wnames}
    m = {nm: args["m_" + nm] for nm in wnames}
    v = {nm: args["v_" + nm] for nm in wnames}
    nb = x.shape[0]
    xi, yi, ci = lax.axis_index("x"), lax.axis_index("y"), lax.axis_index("c")
    chip, me = 2 * xi + yi, 4 * xi + 2 * yi + ci

    got = _gather_chips("gather_weights", [_view2d(w[nm]).astype(BF16) for nm in GATHERED], [c])
    wf = {nm: (g.reshape(-1, g.shape[-1]) if nm in ROW_SHARDED else _unslots(g)) for nm, g in zip(GATHERED, got)}
    for nm in small_names:
        wf[nm] = w[nm][0] if w[nm].ndim > 1 else w[nm]
    c_all = got[len(GATHERED)].reshape(8 * nb, D)

    na, nf = ada_w.shape[-1], final_ada_w.shape[-1]
    ada_b_s = lax.dynamic_slice(ada_b, (0, chip * na), (1, na))
    fada_b_s = lax.dynamic_slice(final_ada_b.reshape(1, -1), (0, chip * nf), (1, nf))
    cond_all, modcols = _mod_fwd(c_all, ada_w[0], ada_b_s, final_ada_w, fada_b_s)
    (mod_g,) = _gather_chips("gather_mod", [modcols])
    mine = lax.dynamic_slice(mod_g, (0, me * nb, 0), (4, nb, na + nf))
    modp = jnp.concatenate([mine[:, :, :na].transpose(1, 0, 2).reshape(nb, 6, D),
                            mine[:, :, na:].transpose(1, 0, 2).reshape(nb, 2, D)], axis=1)

    loss, grad_x, big, small, dmodp = _local_step(x, positions, loss_target, modp, wf)
    loss = lax.psum(loss, ("x", "y", "c"))

    sizes = [small[nm].size for nm in reduced_names]
    pad = -sum(sizes) % 128
    packed = jnp.concatenate([small[nm].reshape(1, -1) for nm in reduced_names] + [jnp.zeros((1, pad), F32)], axis=1)
    swapped = _swap_halves([big[nm] for nm in GATHERED], [dmodp.reshape(nb, 8 * D), packed])
    cidx = ci.astype(jnp.int32).reshape(1)
    chip_sums = [_add_half(big[nm], r, cidx, "grad_add_sibling_" + nm) for nm, r in zip(GATHERED, swapped)]
    halves = [_add_chips(r, "grad_add_chips_" + nm) for nm, r in zip(GATHERED, _scatter_chips(chip_sums))]
    others = _join_halves(halves)
    grads = {}
    dmod_all = swapped[len(GATHERED)].reshape(8 * nb, 8 * D)
    small_sum = _sum_devices(swapped[len(GATHERED) + 1].reshape(8, -1))
    off = 0
    for nm, sz in zip(reduced_names, sizes):
        grads[nm] = small_sum[:, off:off + sz].reshape(small[nm].shape)
        off += sz

    dsl = jnp.concatenate([lax.dynamic_slice(dmod_all, (0, chip * na), (8 * nb, na)),
                           lax.dynamic_slice(dmod_all, (0, 6 * D + chip * nf), (8 * nb, nf))], axis=1)
    gw, gb = _mod_bwd(cond_all.T, dsl, dmod_all)
    grads["ada_w"], grads["final_ada_w"] = gw[:, :na], gw[:, na:]
    grads["ada_b"], grads["final_ada_b"] = gb[:, :6 * D], gb[:, 6 * D:]

    delta, new_m, new_v = {}, {}, {}
    for nm, mine_h, other_h in zip(GATHERED, halves, others):
        grads[nm], delta[nm], new_m[nm], new_v[nm] = _adamw_halves(
            _view2d(w[nm]), mine_h, other_h, _view2d(m[nm]), _view2d(v[nm]), cidx, "adamw_" + nm)
    for nm in TP:
        delta[nm], new_m[nm], new_v[nm] = _adamw(_view2d(w[nm]), grads[nm], _view2d(m[nm]), _view2d(v[nm]),
                                                  "adamw_" + nm)
    upd = _adamw_small([_view2d(w[nm]) for nm in small_names], [grads[nm] for nm in small_names],
                       [_view2d(m[nm]) for nm in small_names], [_view2d(v[nm]) for nm in small_names])
    k = len(small_names)
    for t, nm in enumerate(small_names):
        delta[nm], new_m[nm], new_v[nm] = upd[t], upd[k + t], upd[2 * k + t]

    outs = [grads, delta, new_m, new_v]
    return (loss, grad_x, *[d[nm].reshape(w[nm].shape) for d in outs for nm in wnames])
```

```python
import functools
import inspect
import math

import jax
import jax.numpy as jnp
from jax import lax
from jax.experimental import pallas as pl
from jax.experimental.pallas import tpu as pltpu

F32 = jnp.float32
BF16 = jnp.bfloat16

D = 1024
D_SSM = 512
G = 32
H = 16
P = 64
NST = G * P
D_ATTN = 512
NH = 8
QK_NOPE = 64
QK_ROPE = 32
V_HEAD = 64
HP = 128
Q_LORA = 384
KV_LORA = 256
IN_COLS = D_SSM + Q_LORA + KV_LORA + QK_ROPE
IN_PAD = 1280
D_FF = 4096
ROPE_BASE = 10000.0
EPS = 1e-6
ADAM_LR = 0.001
ADAM_B1 = 0.9
ADAM_B2 = 0.999
ADAM_EPS = 1e-08
ADAM_WD = 0.01
ADAM_STEP = 10
NEG = -1e30
VMEM_LIMIT = 60 << 20

MESH = pl.DeviceIdType.MESH
_VM = pl.BlockSpec(memory_space=pltpu.VMEM)
_ANY = pl.BlockSpec(memory_space=pl.ANY)

GATHERED = ["w_in", "w_glu", "w_uq", "w_ukv", "w_out", "w_ff1", "w_ff2"]
TP = ["ada_w", "final_ada_w"]
ROW_SHARDED = ("w_out", "w_ff2")


def _cp(sem=None, vmem=VMEM_LIMIT):
    kw = dict(vmem_limit_bytes=vmem)
    if sem is not None:
        kw["dimension_semantics"] = sem
    return pltpu.CompilerParams(**kw)


def _dot(a, b):
    return jnp.dot(a, b, preferred_element_type=F32)


def _dot_nt(a, b):
    return lax.dot_general(a, b, (((1,), (1,)), ((), ())), preferred_element_type=F32)


def _dot_tn(a, b):
    return lax.dot_general(a, b, (((0,), (0,)), ((), ())), preferred_element_type=F32)


def _rms(x, n):
    r = lax.rsqrt(jnp.sum(x * x, axis=-1, keepdims=True) * (1.0 / n) + EPS)
    return x * r, r


def _rms_bwd(dyg, xhat, r, n):
    return r * (dyg - xhat * (jnp.sum(dyg * xhat, axis=-1, keepdims=True) * (1.0 / n)))


def _sigmoid(x):
    return 1.0 / (1.0 + jnp.exp(-x))


_GK = math.sqrt(2.0 / math.pi)
_GC = 0.044715


def _gelu(y):
    t = jnp.tanh(_GK * (y + _GC * y * y * y))
    return 0.5 * y * (1.0 + t)


def _gelu_grad(y):
    t = jnp.tanh(_GK * (y + _GC * y * y * y))
    return 0.5 * (1.0 + t) + 0.5 * y * (1.0 - t * t) * _GK * (1.0 + 3.0 * _GC * y * y)


def _colsum(x):
    return jnp.sum(x, axis=0, keepdims=True)


def _roll(x, s):
    return pltpu.roll(x, s % x.shape[-1], x.ndim - 1)


def _mod_fwd(c_all, ada_w_s, ada_b_s, fada_w_s, fada_b_s):
    nseq = c_all.shape[0]
    na, nf = ada_w_s.shape[1], fada_w_s.shape[1]

    def body(c_ref, w_ref, b_ref, fw_ref, fb_ref, cond_ref, mod_ref):
        cv = c_ref[...]
        cond = cv * _sigmoid(cv)
        cond_ref[...] = cond
        cb = cond.astype(BF16)
        mod_ref[:, 0:na] = _dot(cb, w_ref[...].astype(BF16)) + b_ref[...]
        mod_ref[:, na:na + nf] = _dot(cb, fw_ref[...].astype(BF16)) + fb_ref[...]

    return pl.pallas_call(
        body, name="mod_fwd",
        out_shape=[jax.ShapeDtypeStruct((nseq, D), F32), jax.ShapeDtypeStruct((nseq, na + nf), F32)],
        in_specs=[_VM] * 5, out_specs=[_VM] * 2, compiler_params=_cp(),
    )(c_all, ada_w_s, ada_b_s, fada_w_s, fada_b_s)


def _mod_bwd(cond_t, dsl, dall):
    nseq, n = dsl.shape
    bc = 512

    def body(ct_ref, dm_ref, da_ref, gw_ref, gb_ref):
        ct = ct_ref[...]
        dm = dm_ref[...]
        acc = ct[:, 0:1] * dm[0:1, :]
        for b in range(1, nseq):
            acc = acc + ct[:, b:b + 1] * dm[b:b + 1, :]
        gw_ref[...] = acc

        @pl.when(pl.program_id(0) == 0)
        def _():
            gb_ref[...] = _colsum(da_ref[...])

    return pl.pallas_call(
        body, name="mod_bwd", grid=(n // bc,),
        out_shape=[jax.ShapeDtypeStruct((D, n), F32), jax.ShapeDtypeStruct((1, dall.shape[1]), F32)],
        in_specs=[_VM, pl.BlockSpec((nseq, bc), lambda i: (0, i)), _VM],
        out_specs=[pl.BlockSpec((D, bc), lambda i: (0, i)), pl.BlockSpec((1, dall.shape[1]), lambda i: (0, 0))],
        compiler_params=_cp(("arbitrary",)),
    )(cond_t, dsl, dall)


def _f1_fwd(x, modp, g1, w_in, S, tm):
    n = x.shape[0]
    tps = S // tm

    def body(x_ref, mod_ref, g_ref, w_ref, h_ref, proj_ref):
        xhat, _ = _rms(x_ref[...], D)
        h = (xhat * g_ref[...]) * (1.0 + mod_ref[0, 1:2, :]) + mod_ref[0, 0:1, :]
        hb = h.astype(BF16)
        h_ref[...] = hb
        proj_ref[...] = _dot(hb, w_ref[...])

    return pl.pallas_call(
        body, name="f1_fwd", grid=(n // tm,),
        out_shape=[jax.ShapeDtypeStruct((n, D), BF16), jax.ShapeDtypeStruct((n, IN_PAD), F32)],
        in_specs=[pl.BlockSpec((tm, D), lambda i: (i, 0)),
                  pl.BlockSpec((1, 8, D), lambda i: (i // tps, 0, 0)), _VM, _VM],
        out_specs=[pl.BlockSpec((tm, D), lambda i: (i, 0)), pl.BlockSpec((tm, IN_PAD), lambda i: (i, 0))],
        compiler_params=_cp(("parallel",)),
    )(x, modp, g1, w_in)


def _f1_bwd(du, dmla, dx1, x, modp, g1, w_in, S, tm):
    n = x.shape[0]
    tps = S // tm
    nb = n // S

    def body(du_ref, dm_ref, dx1_ref, x_ref, mod_ref, g_ref, w_ref, dx_ref, dproj_ref, accs_ref, accg_ref):
        i = pl.program_id(0)
        dproj = jnp.concatenate([du_ref[...], dm_ref[...]], axis=1).astype(BF16)
        dproj_ref[...] = dproj
        dh = _dot_nt(dproj, w_ref[...])
        xhat, r = _rms(x_ref[...], D)
        g = g_ref[...]
        dn = dh * (1.0 + mod_ref[0, 1:2, :])
        dx_ref[...] = dx1_ref[...] + _rms_bwd(dn * g, xhat, r, D)

        @pl.when(i % tps == 0)
        def _():
            accs_ref[...] = jnp.zeros_like(accs_ref)

        @pl.when(i == 0)
        def _():
            accg_ref[...] = jnp.zeros_like(accg_ref)

        accs_ref[0, 0:1, :] += _colsum(dh)
        accs_ref[0, 1:2, :] += _colsum(dh * (xhat * g))
        accg_ref[0:1, :] += _colsum(dn * xhat)

    return pl.pallas_call(
        body, name="f1_bwd", grid=(n // tm,),
        out_shape=[jax.ShapeDtypeStruct((n, D), F32), jax.ShapeDtypeStruct((n, IN_PAD), BF16),
                   jax.ShapeDtypeStruct((nb, 8, D), F32), jax.ShapeDtypeStruct((8, D), F32)],
        in_specs=[pl.BlockSpec((tm, D_SSM), lambda i: (i, 0)), pl.BlockSpec((tm, IN_PAD - D_SSM), lambda i: (i, 0)),
                  pl.BlockSpec((tm, D), lambda i: (i, 0)), pl.BlockSpec((tm, D), lambda i: (i, 0)),
                  pl.BlockSpec((1, 8, D), lambda i: (i // tps, 0, 0)), _VM, _VM],
        out_specs=[pl.BlockSpec((tm, D), lambda i: (i, 0)), pl.BlockSpec((tm, IN_PAD), lambda i: (i, 0)),
                   pl.BlockSpec((1, 8, D), lambda i: (i // tps, 0, 0)), pl.BlockSpec((8, D), lambda i: (0, 0))],
        compiler_params=_cp(("arbitrary",)),
    )(du, dmla, dx1, x, modp, g1, w_in)


def _ssm_param_fwd(lam_re, lam_im, logdt, b_re, b_im):
    def body(lr_ref, li_ref, ld_ref, br_ref, bi_ref, lbr_ref, lbi_ref, bbr_ref, bbi_ref):
        lr, li = lr_ref[...], li_ref[...]
        dt = jnp.exp(ld_ref[...])
        er = jnp.exp(lr * dt)
        lbr = er * jnp.cos(li * dt)
        lbi = er * jnp.sin(li * dt)
        den = 1.0 / (lr * lr + li * li)
        cr = ((lbr - 1.0) * lr + lbi * li) * den
        ci = (lbi * lr - (lbr - 1.0) * li) * den
        lbr_ref[...] = lbr
        lbi_ref[...] = lbi
        bbr_ref[...] = cr * br_ref[...] - ci * bi_ref[...]
        bbi_ref[...] = cr * bi_ref[...] + ci * br_ref[...]

    return pl.pallas_call(
        body, name="ssm_param_fwd",
        out_shape=[jax.ShapeDtypeStruct((NST, 1), F32)] * 2 + [jax.ShapeDtypeStruct((NST, H), F32)] * 2,
        in_specs=[_VM] * 5, out_specs=[_VM] * 4, compiler_params=_cp(),
    )(lam_re, lam_im, logdt, b_re, b_im)


def _ssm_param_bwd(lam_re, lam_im, logdt, b_re, b_im, dlb_re, dlb_im, dbb_re, dbb_im):
    def body(lr_ref, li_ref, ld_ref, br_ref, bi_ref, dlr_ref, dli_ref, dbr_ref, dbi_ref,
             gbr_ref, gbi_ref, glr_ref, gli_ref, gdt_ref):
        lr, li = lr_ref[...], li_ref[...]
        dt = jnp.exp(ld_ref[...])
        er = jnp.exp(lr * dt)
        lbr = er * jnp.cos(li * dt)
        lbi = er * jnp.sin(li * dt)
        den = 1.0 / (lr * lr + li * li)
        nr, ni = lbr - 1.0, lbi
        cr = (nr * lr + ni * li) * den
        ci = (ni * lr - nr * li) * den
        br, bi = br_ref[...], bi_ref[...]
        dbr, dbi = dbr_ref[...], dbi_ref[...]
        gbr_ref[...] = cr * dbr + ci * dbi
        gbi_ref[...] = cr * dbi - ci * dbr
        gcr = jnp.sum(dbr * br + dbi * bi, axis=1, keepdims=True)
        gci = jnp.sum(dbi * br - dbr * bi, axis=1, keepdims=True)
        ilr, ili = lr * den, -li * den
        glbr = dlr_ref[...] + (gcr * ilr + gci * ili)
        glbi = dli_ref[...] + (gci * ilr - gcr * ili)
        qr = -(cr * ilr - ci * ili)
        qi = -(cr * ili + ci * ilr)
        glr = gcr * qr + gci * qi
        gli = gci * qr - gcr * qi
        glr = glr + dt * (glbr * lbr + glbi * lbi)
        gli = gli + dt * (glbi * lbr - glbr * lbi)
        wr = lr * lbr - li * lbi
        wi = lr * lbi + li * lbr
        glr_ref[...] = glr
        gli_ref[...] = gli
        gdt_ref[...] = (glbr * wr + glbi * wi) * dt

    return pl.pallas_call(
        body, name="ssm_param_bwd",
        out_shape=[jax.ShapeDtypeStruct((NST, H), F32)] * 2 + [jax.ShapeDtypeStruct((NST, 1), F32)] * 3,
        in_specs=[_VM] * 9, out_specs=[_VM] * 5, compiler_params=_cp(),
    )(lam_re, lam_im, logdt, b_re, b_im, dlb_re, dlb_im, dbb_re, dbb_im)


def _rowsum(a):
    def body(a_ref, o_ref):
        o_ref[...] = jnp.sum(a_ref[...], axis=1, keepdims=True)

    return pl.pallas_call(
        body, name="rowsum", out_shape=jax.ShapeDtypeStruct((a.shape[0], 1), F32),
        in_specs=[_VM], out_specs=_VM, compiler_params=_cp(),
    )(a)


def _pow2k(pr, pi, nsq):
    for _ in range(nsq):
        pr, pi = pr * pr - pi * pi, 2.0 * pr * pi
    return pr, pi


def _ssm_local(u_p, bm, lre8, lim8, S, tt):
    n = u_p.shape[0]
    nb, nt = n // S, S // tt
    nsq = int(round(math.log2(S // 8)))
    assert 2 ** nsq == S // 8

    def body(u_ref, bm_ref, lre_ref, lim_ref, cre_ref, cim_ref, sre, sim, bu):
        j = pl.program_id(1)

        @pl.when(j == 0)
        def _():
            sre[...] = jnp.zeros_like(sre)
            sim[...] = jnp.zeros_like(sim)

        bu[...] = _dot(u_ref[...].astype(BF16), bm_ref[...])
        lre, lim = lre_ref[...], lim_ref[...]

        def step(i, c):
            sr, si = c
            off = pl.multiple_of(i * 8, 8)
            br = bu[pl.ds(off, 8), 0:NST]
            bi = bu[pl.ds(off, 8), NST:2 * NST]
            return lre * sr - lim * si + br, lre * si + lim * sr + bi

        sr, si = lax.fori_loop(0, tt // 8, step, (sre[...], sim[...]))
        sre[...] = sr
        sim[...] = si

        @pl.when(j == nt - 1)
        def _():
            pr, pi = _pow2k(lre[0:1], lim[0:1], nsq)
            cr = jnp.zeros((1, NST), F32)
            ci = jnp.zeros((1, NST), F32)
            cre_ref[0:1, :] = cr
            cim_ref[0:1, :] = ci
            for k in range(1, 8):
                cr, ci = sr[k - 1:k] + pr * cr - pi * ci, si[k - 1:k] + pr * ci + pi * cr
                cre_ref[k:k + 1, :] = cr
                cim_ref[k:k + 1, :] = ci

    return pl.pallas_call(
        body, name="ssm_local", grid=(nb, nt),
        out_shape=[jax.ShapeDtypeStruct((nb * 8, NST), F32)] * 2,
        in_specs=[pl.BlockSpec((tt, D_SSM), lambda b, j: (b * nt + j, 0)), _VM, _VM, _VM],
        out_specs=[pl.BlockSpec((8, NST), lambda b, j: (b, 0))] * 2,
        scratch_shapes=[pltpu.VMEM((8, NST), F32), pltpu.VMEM((8, NST), F32), pltpu.VMEM((tt, 2 * NST), F32)],
        compiler_params=_cp(("arbitrary", "arbitrary")),
    )(u_p, bm, lre8, lim8)


def _ssm_fwd(u_p, cre, cim, bm, cm, dvec, w_glu, lre8, lim8, S, tt):
    n = u_p.shape[0]
    nb, nt = n // S, S // tt

    def body(u_ref, cre_ref, cim_ref, bm_ref, cm_ref, d_ref, wg_ref, lre_ref, lim_ref,
             st_ref, ypre_ref, z_ref, gact_ref, yssm_ref, sre, sim, bu):
        j = pl.program_id(1)

        @pl.when(j == 0)
        def _():
            sre[...] = cre_ref[...]
            sim[...] = cim_ref[...]

        u = u_ref[...]
        bu[...] = _dot(u.astype(BF16), bm_ref[...])
        lre, lim = lre_ref[...], lim_ref[...]

        def step(i, c):
            sr, si = c
            off = pl.multiple_of(i * 8, 8)
            nr = lre * sr - lim * si + bu[pl.ds(off, 8), 0:NST]
            ni = lre * si + lim * sr + bu[pl.ds(off, 8), NST:2 * NST]
            st_ref[pl.ds(off, 8), 0:NST] = nr
            st_ref[pl.ds(off, 8), NST:2 * NST] = ni
            return nr, ni

        sr, si = lax.fori_loop(0, tt // 8, step, (sre[...], sim[...]))
        sre[...] = sr
        sim[...] = si
        y = _dot(st_ref[...].astype(BF16), cm_ref[...]) + d_ref[...] * u
        ypre_ref[...] = y
        gb = _gelu(y).astype(BF16)
        gact_ref[...] = gb
        z = _dot(gb, wg_ref[...])
        z_ref[...] = z
        yssm_ref[...] = z[:, 0:D_SSM] * _sigmoid(z[:, D_SSM:2 * D_SSM])

    row = lambda w: pl.BlockSpec((tt, w), lambda b, j: (b * nt + j, 0))
    return pl.pallas_call(
        body, name="ssm_fwd", grid=(nb, nt),
        out_shape=[jax.ShapeDtypeStruct((n, 2 * NST), F32), jax.ShapeDtypeStruct((n, D_SSM), F32),
                   jax.ShapeDtypeStruct((n, 2 * D_SSM), F32), jax.ShapeDtypeStruct((n, D_SSM), BF16),
                   jax.ShapeDtypeStruct((n, D_SSM), F32)],
        in_specs=[row(D_SSM), pl.BlockSpec((8, NST), lambda b, j: (b, 0)), pl.BlockSpec((8, NST), lambda b, j: (b, 0)),
                  _VM, _VM, _VM, _VM, _VM, _VM],
        out_specs=[row(2 * NST), row(D_SSM), row(2 * D_SSM), row(D_SSM), row(D_SSM)],
        scratch_shapes=[pltpu.VMEM((8, NST), F32), pltpu.VMEM((8, NST), F32), pltpu.VMEM((tt, 2 * NST), F32)],
        compiler_params=_cp(("arbitrary", "arbitrary")),
    )(u_p, cre, cim, bm, cm, dvec, w_glu, lre8, lim8)


def _ssm_bwd_a(dys_p, z, ypre, w_glu, cm, lre8, lim8, S, tt):
    n = z.shape[0]
    nb, nt = n // S, S // tt
    nsq = int(round(math.log2(S // 8)))
    ng = tt // 8

    def body(dys_ref, z_ref, y_ref, wg_ref, cm_ref, lre_ref, lim_ref, dy_ref, dz_ref, are_ref, aim_ref, sre, sim, gb):
        j = pl.program_id(1)

        @pl.when(j == 0)
        def _():
            sre[...] = jnp.zeros_like(sre)
            sim[...] = jnp.zeros_like(sim)

        z = z_ref[...]
        z1, z2 = z[:, 0:D_SSM], z[:, D_SSM:2 * D_SSM]
        sg = _sigmoid(z2)
        dys = dys_ref[...]
        dz = jnp.concatenate([dys * sg, dys * z1 * sg * (1.0 - sg)], axis=1).astype(BF16)
        dz_ref[...] = dz
        dy = _dot_nt(dz, wg_ref[...]) * _gelu_grad(y_ref[...])
        dy_ref[...] = dy
        gb[...] = _dot_nt(dy.astype(BF16), cm_ref[...])
        lre, lim = lre_ref[...], lim_ref[...]

        def step(i, c):
            ar, ai = c
            off = pl.multiple_of((ng - 1 - i) * 8, 8)
            gr = gb[pl.ds(off, 8), 0:NST]
            gi = gb[pl.ds(off, 8), NST:2 * NST]
            return lre * ar + lim * ai + gr, lre * ai - lim * ar + gi

        ar, ai = lax.fori_loop(0, ng, step, (sre[...], sim[...]))
        sre[...] = ar
        sim[...] = ai

        @pl.when(j == nt - 1)
        def _():
            pr, pi = _pow2k(lre[0:1], -lim[0:1], nsq)
            cr = jnp.zeros((1, NST), F32)
            ci = jnp.zeros((1, NST), F32)
            are_ref[7:8, :] = cr
            aim_ref[7:8, :] = ci
            for k in range(6, -1, -1):
                cr, ci = ar[k + 1:k + 2] + pr * cr - pi * ci, ai[k + 1:k + 2] + pr * ci + pi * cr
                are_ref[k:k + 1, :] = cr
                aim_ref[k:k + 1, :] = ci

    row = lambda w: pl.BlockSpec((tt, w), lambda b, j: (b * nt + nt - 1 - j, 0))
    return pl.pallas_call(
        body, name="ssm_bwd_a", grid=(nb, nt),
        out_shape=[jax.ShapeDtypeStruct((n, D_SSM), F32), jax.ShapeDtypeStruct((n, 2 * D_SSM), BF16),
                   jax.ShapeDtypeStruct((nb * 8, NST), F32), jax.ShapeDtypeStruct((nb * 8, NST), F32)],
        in_specs=[row(D_SSM), row(2 * D_SSM), row(D_SSM), _VM, _VM, _VM, _VM],
        out_specs=[row(D_SSM), row(2 * D_SSM), pl.BlockSpec((8, NST), lambda b, j: (b, 0)),
                   pl.BlockSpec((8, NST), lambda b, j: (b, 0))],
        scratch_shapes=[pltpu.VMEM((8, NST), F32), pltpu.VMEM((8, NST), F32), pltpu.VMEM((tt, 2 * NST), F32)],
        compiler_params=_cp(("arbitrary", "arbitrary")),
    )(dys_p, z, ypre, w_glu, cm, lre8, lim8)


def _ssm_bwd_b(dy, u_p, st, fcr, fci, air, aii, bm, cm, dvec, lre8, lim8, S, tt):
    n = u_p.shape[0]
    nb, nt = n // S, S // tt
    ng = tt // 8
    QB = D_SSM // 4

    def body(dy_ref, u_ref, st_ref, stp_ref, fcr_ref, fci_ref, air_ref, aii_ref, bm_ref, cm_ref, d_ref, lre_ref, lim_ref,
             du_ref, dcm_ref, dbm_ref, dd_ref, dlr_ref, dli_ref, are, aim, accr, acci, sp, ab):
        b = pl.program_id(0)
        j = pl.program_id(1)
        jt = nt - 1 - j

        @pl.when((b == 0) & (j == 0))
        def _():
            dcm_ref[...] = jnp.zeros_like(dcm_ref)
            dbm_ref[...] = jnp.zeros_like(dbm_ref)
            dd_ref[...] = jnp.zeros_like(dd_ref)
            accr[...] = jnp.zeros_like(accr)
            acci[...] = jnp.zeros_like(acci)

        @pl.when(j == 0)
        def _():
            are[...] = air_ref[...]
            aim[...] = aii_ref[...]

        sp[8:tt + 8, :] = st_ref[...]

        @pl.when(jt == 0)
        def _():
            sp[0:8, 0:NST] = fcr_ref[...]
            sp[0:8, NST:2 * NST] = fci_ref[...]

        @pl.when(jt != 0)
        def _():
            sp[0:8, :] = stp_ref[...]

        dy = dy_ref[...]
        u = u_ref[...]
        dyb = dy.astype(BF16)
        ab[...] = _dot_nt(dyb, cm_ref[...])
        lre, lim = lre_ref[...], lim_ref[...]

        def step(i, c):
            ar, ai = c
            off = pl.multiple_of((ng - 1 - i) * 8, 8)
            nr = lre * ar + lim * ai + ab[pl.ds(off, 8), 0:NST]
            ni = lre * ai - lim * ar + ab[pl.ds(off, 8), NST:2 * NST]
            ab[pl.ds(off, 8), 0:NST] = nr
            ab[pl.ds(off, 8), NST:2 * NST] = ni
            pr = sp[pl.ds(off, 8), 0:NST]
            pi = sp[pl.ds(off, 8), NST:2 * NST]
            accr[...] += nr * pr + ni * pi
            acci[...] += ni * pr - nr * pi
            return nr, ni

        ar, ai = lax.fori_loop(0, ng, step, (are[...], aim[...]))
        are[...] = ar
        aim[...] = ai
        a_b = ab[...].astype(BF16)
        du_ref[...] = _dot_nt(a_b, bm_ref[...]) + d_ref[...] * dy
        ub = u.astype(BF16)
        for q in range(4):
            for part in range(2):
                lo = part * NST + q * 4 * QB
                s_q = sp[8:tt + 8, lo:lo + 4 * QB].astype(BF16)
                dcm_ref[lo:lo + 4 * QB, :] += _dot_tn(s_q, dyb[:, q * QB:(q + 1) * QB])
                dbm_ref[:, lo:lo + 4 * QB] += _dot_tn(ub[:, q * QB:(q + 1) * QB], a_b[:, lo:lo + 4 * QB])
        dd_ref[...] += _colsum(dy * u)

        @pl.when((b == nb - 1) & (j == nt - 1))
        def _():
            dlr_ref[...] = _colsum(accr[...])
            dli_ref[...] = _colsum(acci[...])

    row = lambda w: pl.BlockSpec((tt, w), lambda b, j: (b * nt + nt - 1 - j, 0))
    seq8 = pl.BlockSpec((8, NST), lambda b, j: (b, 0))
    prev = pl.BlockSpec((8, 2 * NST), lambda b, j: (jnp.maximum((b * nt + nt - 1 - j) * ng - 1, 0), 0))
    const = lambda shape: pl.BlockSpec(shape, lambda b, j: (0, 0))
    return pl.pallas_call(
        body, name="ssm_bwd_b", grid=(nb, nt),
        out_shape=[jax.ShapeDtypeStruct((n, D_SSM), F32), jax.ShapeDtypeStruct((2 * NST, QB), F32),
                   jax.ShapeDtypeStruct((QB, 2 * NST), F32), jax.ShapeDtypeStruct((1, D_SSM), F32),
                   jax.ShapeDtypeStruct((1, NST), F32), jax.ShapeDtypeStruct((1, NST), F32)],
        in_specs=[row(D_SSM), row(D_SSM), row(2 * NST), prev, seq8, seq8, seq8, seq8, _VM, _VM, _VM, _VM, _VM],
        out_specs=[row(D_SSM), const((2 * NST, QB)), const((QB, 2 * NST)), const((1, D_SSM)),
                   const((1, NST)), const((1, NST))],
        scratch_shapes=[pltpu.VMEM((8, NST), F32)] * 4 + [pltpu.VMEM((tt + 8, 2 * NST), F32),
                                                          pltpu.VMEM((tt, 2 * NST), F32)],
        compiler_params=_cp(("arbitrary", "arbitrary")),
    )(dy, u_p, st, st, fcr, fci, air, aii, bm, cm, dvec, lre8, lim8)


def _rope(v, c, s1, s2):
    return v * c + _roll(v, -16) * s1 + _roll(v, 16) * s2


def _rope_t(dv, c, s1, s2):
    return dv * c + _roll(dv * s1, 16) + _roll(dv * s2, -16)


def _mla_fwd(proj, rc, rs1, rs2, gq, gkv, w_uq, w_ukv, tm):
    n = proj.shape[0]

    def body(ql_ref, kvl_ref, kr_ref, c_ref, s1_ref, s2_ref, gq_ref, gkv_ref, wq_ref, wkv_ref,
             q_ref, k_ref, v_ref, qn_ref, kvn_ref):
        c, s1, s2 = c_ref[...], s1_ref[...], s2_ref[...]
        qhat, _ = _rms(ql_ref[...], Q_LORA)
        qn = (qhat * gq_ref[...]).astype(BF16)
        qn_ref[...] = qn
        q = _dot(qn, wq_ref[...])
        q_ref[...] = _rope(q, jnp.tile(c, (1, NH)), jnp.tile(s1, (1, NH)), jnp.tile(s2, (1, NH))).astype(BF16)
        khat, _ = _rms(kvl_ref[...], KV_LORA)
        kvn = (khat * gkv_ref[...]).astype(BF16)
        kvn_ref[...] = kvn
        kv = _dot(kvn, wkv_ref[...])
        kr = _rope(_roll(kr_ref[...], 64), c, s1, s2)
        k_ref[...] = (kv[:, 0:NH * HP] + jnp.tile(kr, (1, NH))).astype(BF16)
        v_ref[...] = kv[:, NH * HP:2 * NH * HP].astype(BF16)

    def wrapped(proj_ref, *rest):
        ql = proj_ref.at[:, D_SSM:D_SSM + Q_LORA]
        kvl = proj_ref.at[:, D_SSM + Q_LORA:D_SSM + Q_LORA + KV_LORA]
        kr = proj_ref.at[:, IN_PAD - HP:IN_PAD]
        body(ql, kvl, kr, *rest)

    row = lambda w: pl.BlockSpec((tm, w), lambda i: (i, 0))
    return pl.pallas_call(
        wrapped, name="mla_fwd", grid=(n // tm,),
        out_shape=[jax.ShapeDtypeStruct((n, NH * HP), BF16)] * 3 +
                  [jax.ShapeDtypeStruct((n, Q_LORA), BF16), jax.ShapeDtypeStruct((n, KV_LORA), BF16)],
        in_specs=[row(IN_PAD), row(HP), row(HP), row(HP), _VM, _VM, _VM, _VM],
        out_specs=[row(NH * HP)] * 3 + [row(Q_LORA), row(KV_LORA)],
        compiler_params=_cp(("parallel",)),
    )(proj, rc, rs1, rs2, gq, gkv, w_uq, w_ukv)


def _mla_bwd(dq, dk, dv, proj, rc, rs1, rs2, gq, gkv, w_uq, w_ukv, tm):
    n = proj.shape[0]

    def body(dq_ref, dk_ref, dv_ref, proj_ref, c_ref, s1_ref, s2_ref, gq_ref, gkv_ref, wq_ref, wkv_ref,
             dmla_ref, dqb_ref, dkvb_ref, acc_ref):
        i = pl.program_id(0)
        c, s1, s2 = c_ref[...], s1_ref[...], s2_ref[...]
        dqu = _rope_t(dq_ref[...], jnp.tile(c, (1, NH)), jnp.tile(s1, (1, NH)), jnp.tile(s2, (1, NH))).astype(BF16)
        dqb_ref[...] = dqu
        dqn = _dot_nt(dqu, wq_ref[...])
        qhat, rq = _rms(proj_ref[:, D_SSM:D_SSM + Q_LORA], Q_LORA)
        dql = _rms_bwd(dqn * gq_ref[...], qhat, rq, Q_LORA)
        dkf = dk_ref[...]
        dkv = jnp.concatenate([dkf, dv_ref[...]], axis=1).astype(BF16)
        dkvb_ref[...] = dkv
        dkvn = _dot_nt(dkv, wkv_ref[...])
        khat, rk = _rms(proj_ref[:, D_SSM + Q_LORA:D_SSM + Q_LORA + KV_LORA], KV_LORA)
        dkvl = _rms_bwd(dkvn * gkv_ref[...], khat, rk, KV_LORA)
        dkr = dkf[:, 0:HP]
        for h in range(1, NH):
            dkr = dkr + dkf[:, h * HP:(h + 1) * HP]
        lane = lax.broadcasted_iota(jnp.int32, dkr.shape, 1)
        dkr = jnp.where((lane >= QK_NOPE) & (lane < QK_NOPE + QK_ROPE), dkr, 0.0)
        dkr = _roll(_rope_t(dkr, c, s1, s2), -64)
        dmla_ref[...] = jnp.concatenate([dql, dkvl, dkr], axis=1)

        @pl.when(i == 0)
        def _():
            acc_ref[...] = jnp.zeros_like(acc_ref)

        acc_ref[0:1, 0:Q_LORA] += _colsum(dqn * qhat)
        acc_ref[1:2, 0:KV_LORA] += _colsum(dkvn * khat)

    row = lambda w: pl.BlockSpec((tm, w), lambda i: (i, 0))
    return pl.pallas_call(
        body, name="mla_bwd", grid=(n // tm,),
        out_shape=[jax.ShapeDtypeStruct((n, IN_PAD - D_SSM), F32), jax.ShapeDtypeStruct((n, NH * HP), BF16),
                   jax.ShapeDtypeStruct((n, 2 * NH * HP), BF16), jax.ShapeDtypeStruct((8, Q_LORA), F32)],
        in_specs=[row(NH * HP)] * 3 + [row(IN_PAD), row(HP), row(HP), row(HP), _VM, _VM, _VM, _VM],
        out_specs=[row(IN_PAD - D_SSM), row(NH * HP), row(2 * NH * HP), pl.BlockSpec((8, Q_LORA), lambda i: (0, 0))],
        compiler_params=_cp(("arbitrary",)),
    )(dq, dk, dv, proj, rc, rs1, rs2, gq, gkv, w_uq, w_ukv)


_SCALE = (QK_NOPE + QK_ROPE) ** -0.5
_LOG2E = 1.4426950408889634
_C2 = _SCALE * _LOG2E


def _attn_fwd(q, k, v, S, tq):
    n = q.shape[0]
    nb, nq = n // S, S // tq

    def body(q_ref, k_ref, v_ref, o_ref, lr_ref):
        qi = pl.program_id(2)
        qv = q_ref[...]

        def tile(j, c, diagonal):
            m, l, acc = c
            off = pl.multiple_of(j * tq, tq)
            s = _dot_nt(qv, k_ref[pl.ds(off, tq), :]) * _C2
            if diagonal:
                rows = lax.broadcasted_iota(jnp.int32, s.shape, 0)
                cols = lax.broadcasted_iota(jnp.int32, s.shape, 1)
                s = jnp.where(cols <= rows, s, NEG)
            mn = jnp.maximum(m, jnp.max(s, axis=1, keepdims=True))
            p = jnp.exp2(s - mn)
            al = jnp.exp2(m - mn)
            l = al * l + jnp.sum(p, axis=1, keepdims=True)
            acc = al * acc + _dot(p.astype(BF16), v_ref[pl.ds(off, tq), :])
            return mn, l, acc

        init = (jnp.full((tq, 1), NEG, F32), jnp.zeros((tq, 1), F32), jnp.zeros((tq, HP), F32))
        c = lax.fori_loop(0, qi, lambda j, c: tile(j, c, False), init)
        m, l, acc = tile(qi, c, True)
        o_ref[...] = acc / l
        lane = lax.broadcasted_iota(jnp.int32, (8, HP), 1)
        lse = jnp.broadcast_to(m + jnp.log(l) * _LOG2E, (tq, HP))
        lr_ref[...] = _rows_of(lse, jnp.where(lane == 0, 1.0, 0.0).astype(BF16))

    qs = pl.BlockSpec((tq, HP), lambda b, h, i: (b * nq + i, h))
    ks = pl.BlockSpec((S, HP), lambda b, h, i: (b, h))
    return pl.pallas_call(
        body, name="attn_fwd", grid=(nb, NH, nq),
        out_shape=[jax.ShapeDtypeStruct((n, NH * HP), F32), jax.ShapeDtypeStruct((nb * NH * 8, S), F32)],
        in_specs=[qs, ks, ks], out_specs=[qs, pl.BlockSpec((8, tq), lambda b, h, i: (b * NH + h, i))],
        compiler_params=_cp(("parallel", "parallel", "arbitrary")),
    )(q, k, v)


def _rows_of(x, pick):
    x1 = x.astype(BF16)
    r1 = x - x1.astype(F32)
    x2 = r1.astype(BF16)
    x3 = (r1 - x2.astype(F32)).astype(BF16)
    return _dot_nt(pick, x1) + _dot_nt(pick, x2) + _dot_nt(pick, x3)


def _attn_bwd(q, k, v, dob, lrow, drow, S, tq):
    n = q.shape[0]
    nb, nq = n // S, S // tq

    def body(q_ref, k_ref, v_ref, do_ref, lr_ref, dr_ref, dq_ref, dk_ref, dv_ref):
        kj = pl.program_id(2)

        @pl.when(kj == 0)
        def _():
            dq_ref[...] = jnp.zeros_like(dq_ref)

        kt = k_ref[...]
        vt = v_ref[...]

        def tile(i, c, diagonal):
            dk, dv = c
            off = pl.multiple_of(i * tq, tq)
            qv = q_ref[pl.ds(off, tq), :]
            dob = do_ref[pl.ds(off, tq), :]
            lr = lr_ref[0:1, pl.ds(off, tq)]
            dr = dr_ref[0:1, pl.ds(off, tq)]
            st = _dot_nt(kt, qv)
            dpt = _dot_nt(vt, dob)
            pt = jnp.exp2(st * _C2 - lr)
            if diagonal:
                keys = lax.broadcasted_iota(jnp.int32, pt.shape, 0)
                qrys = lax.broadcasted_iota(jnp.int32, pt.shape, 1)
                pt = jnp.where(keys <= qrys, pt, 0.0)
            dst = (pt * (dpt * _SCALE - dr)).astype(BF16)
            dq_ref[pl.ds(off, tq), :] += _dot_tn(dst, kt)
            return dk + _dot(dst, qv), dv + _dot(pt.astype(BF16), dob)

        zero = jnp.zeros((tq, HP), F32)
        c = tile(kj, (zero, zero), True)
        dk, dv = lax.fori_loop(kj + 1, nq, lambda i, c: tile(i, c, False), c)
        dk_ref[...] = dk
        dv_ref[...] = dv

    ts = pl.BlockSpec((tq, HP), lambda b, h, i: (b * nq + i, h))
    fs = pl.BlockSpec((S, HP), lambda b, h, i: (b, h))
    rs = pl.BlockSpec((8, S), lambda b, h, i: (b * NH + h, 0))
    return pl.pallas_call(
        body, name="attn_bwd", grid=(nb, NH, nq),
        out_shape=[jax.ShapeDtypeStruct((n, NH * HP), F32)] * 3,
        in_specs=[fs, ts, ts, fs, rs, rs], out_specs=[fs, ts, ts],
        compiler_params=_cp(("parallel", "parallel", "arbitrary")),
    )(q, k, v, dob, lrow, drow)


def _p1_fwd(yssm, oattn, x, modp, gs, ga, w_out, g2, S, tm):
    n = x.shape[0]
    tps = S // tm

    def body(ys_ref, oa_ref, x_ref, mod_ref, gs_ref, ga_ref, w_ref, g2_ref, yn_ref, o_ref, x1_ref, h2_ref):
        yh, _ = _rms(ys_ref[...], D_SSM)
        ah, _ = _rms(oa_ref[...], D_ATTN)
        yn = jnp.concatenate([yh * gs_ref[...], ah * ga_ref[...]], axis=1).astype(BF16)
        yn_ref[...] = yn
        o = _dot(yn, w_ref[...])
        o_ref[...] = o
        x1 = x_ref[...] + mod_ref[0, 2:3, :] * o
        x1_ref[...] = x1
        xh, _ = _rms(x1, D)
        h2_ref[...] = ((xh * g2_ref[...]) * (1.0 + mod_ref[0, 4:5, :]) + mod_ref[0, 3:4, :]).astype(BF16)

    row = lambda w: pl.BlockSpec((tm, w), lambda i: (i, 0))
    return pl.pallas_call(
        body, name="p1_fwd", grid=(n // tm,),
        out_shape=[jax.ShapeDtypeStruct((n, D_SSM + NH * HP), BF16), jax.ShapeDtypeStruct((n, D), F32),
                   jax.ShapeDtypeStruct((n, D), F32), jax.ShapeDtypeStruct((n, D), BF16)],
        in_specs=[row(D_SSM), row(NH * HP), row(D), pl.BlockSpec((1, 8, D), lambda i: (i // tps, 0, 0)),
                  _VM, _VM, _VM, _VM],
        out_specs=[row(D_SSM + NH * HP), row(D), row(D), row(D)],
        compiler_params=_cp(("parallel",)),
    )(yssm, oattn, x, modp, gs, ga, w_out, g2)


def _p2(x1, h2, target, modp, g2, gf, w_ff1, w_ff2, S, tm):
    n = x1.shape[0]
    tps = S // tm
    nb = n // S

    def body(x1_ref, h2_ref, t_ref, mod_ref, g2_ref, gf_ref, w1_ref, w2_ref,
             dx1_ref, r_ref, da_ref, dff_ref, accs_ref, accg_ref):
        i = pl.program_id(0)
        sh2, sc2, gt2 = mod_ref[0, 3:4, :], mod_ref[0, 4:5, :], mod_ref[0, 5:6, :]
        fsh, fsc = mod_ref[0, 6:7, :], mod_ref[0, 7:8, :]
        x1 = x1_ref[...]
        a = _dot(h2_ref[...], w1_ref[...])
        ra = jnp.maximum(a, 0.0)
        rb = (ra * ra).astype(BF16)
        r_ref[...] = rb
        ff = _dot(rb, w2_ref[...])
        x2 = x1 + gt2 * ff
        x2h, rf = _rms(x2, D)
        gf_v = gf_ref[...]
        outn = x2h * gf_v
        err = outn * (1.0 + fsc) + fsh - t_ref[...]
        dout = err * (1.0 / D)
        doutn = dout * (1.0 + fsc)
        dx2 = _rms_bwd(doutn * gf_v, x2h, rf, D)
        dff = (gt2 * dx2).astype(BF16)
        dff_ref[...] = dff
        dr = _dot_nt(dff, w2_ref[...])
        da = (dr * (2.0 * ra)).astype(BF16)
        da_ref[...] = da
        dh2 = _dot_nt(da, w1_ref[...])
        x1h, r2 = _rms(x1, D)
        g2_v = g2_ref[...]
        dn2 = dh2 * (1.0 + sc2)
        dx1_ref[...] = dx2 + _rms_bwd(dn2 * g2_v, x1h, r2, D)

        @pl.when(i % tps == 0)
        def _():
            accs_ref[...] = jnp.zeros_like(accs_ref)

        @pl.when(i == 0)
        def _():
            accg_ref[...] = jnp.zeros_like(accg_ref)

        accs_ref[0, 3:4, :] += _colsum(dh2)
        accs_ref[0, 4:5, :] += _colsum(dh2 * (x1h * g2_v))
        accs_ref[0, 5:6, :] += _colsum(dx2 * ff)
        accs_ref[0, 6:7, :] += _colsum(dout)
        accs_ref[0, 7:8, :] += _colsum(dout * outn)
        accg_ref[0:1, :] += _colsum(dn2 * x1h)
        accg_ref[1:2, :] += _colsum(doutn * x2h)
        accg_ref[2:3, :] += _colsum(err * err) * (0.5 / D)

    row = lambda w: pl.BlockSpec((tm, w), lambda i: (i, 0))
    return pl.pallas_call(
        body, name="p2_mlp_loss", grid=(n // tm,),
        out_shape=[jax.ShapeDtypeStruct((n, D), F32), jax.ShapeDtypeStruct((n, D_FF), BF16),
                   jax.ShapeDtypeStruct((n, D_FF), BF16), jax.ShapeDtypeStruct((n, D), BF16),
                   jax.ShapeDtypeStruct((nb, 8, D), F32), jax.ShapeDtypeStruct((8, D), F32)],
        in_specs=[row(D), row(D), row(D), pl.BlockSpec((1, 8, D), lambda i: (i // tps, 0, 0)), _VM, _VM, _VM, _VM],
        out_specs=[row(D), row(D_FF), row(D_FF), row(D), pl.BlockSpec((1, 8, D), lambda i: (i // tps, 0, 0)),
                   pl.BlockSpec((8, D), lambda i: (0, 0))],
        compiler_params=_cp(("arbitrary",)),
    )(x1, h2, target, modp, g2, gf, w_ff1, w_ff2)


def _p3_bwd(dx1, o, yssm, oattn, modp, gs, ga, w_out, S, tm):
    n = dx1.shape[0]
    tps = S // tm
    nb = n // S

    def body(dx1_ref, o_ref, ys_ref, oa_ref, mod_ref, gs_ref, ga_ref, w_ref,
             do_ref, dys_ref, doa_ref, dr_ref, accs_ref, accg_ref):
        i = pl.program_id(0)
        dx1 = dx1_ref[...]
        dob = (mod_ref[0, 2:3, :] * dx1).astype(BF16)
        do_ref[...] = dob
        dyn = _dot_nt(dob, w_ref[...])
        yh, rs = _rms(ys_ref[...], D_SSM)
        oa = oa_ref[...]
        ah, ra = _rms(oa, D_ATTN)
        d1 = dyn[:, 0:D_SSM]
        d2 = dyn[:, D_SSM:D_SSM + NH * HP]
        dys_ref[...] = _rms_bwd(d1 * gs_ref[...], yh, rs, D_SSM)
        doa = _rms_bwd(d2 * ga_ref[...], ah, ra, D_ATTN)
        doa_ref[...] = doa.astype(BF16)
        prod = doa * oa * _SCALE
        ones = jnp.ones((8, HP), BF16)
        for h in range(NH):
            dr_ref[h * 8:(h + 1) * 8, :] = _rows_of(prod[:, h * HP:(h + 1) * HP], ones)

        @pl.when(i % tps == 0)
        def _():
            accs_ref[...] = jnp.zeros_like(accs_ref)

        @pl.when(i == 0)
        def _():
            accg_ref[...] = jnp.zeros_like(accg_ref)

        accs_ref[0, 2:3, :] += _colsum(dx1 * o_ref[...])
        accg_ref[0:1, 0:D_SSM] += _colsum(d1 * yh)
        accg_ref[1:2, :] += _colsum(d2 * ah)

    row = lambda w: pl.BlockSpec((tm, w), lambda i: (i, 0))
    return pl.pallas_call(
        body, name="p3_bwd", grid=(n // tm,),
        out_shape=[jax.ShapeDtypeStruct((n, D), BF16), jax.ShapeDtypeStruct((n, D_SSM), F32),
                   jax.ShapeDtypeStruct((n, NH * HP), BF16), jax.ShapeDtypeStruct((nb * NH * 8, S), F32),
                   jax.ShapeDtypeStruct((nb, 8, D), F32), jax.ShapeDtypeStruct((8, NH * HP), F32)],
        in_specs=[row(D), row(D), row(D_SSM), row(NH * HP), pl.BlockSpec((1, 8, D), lambda i: (i // tps, 0, 0)),
                  _VM, _VM, _VM],
        out_specs=[row(D), row(D_SSM), row(NH * HP), pl.BlockSpec((NH * 8, tm), lambda i: (i // tps, i % tps)),
                   pl.BlockSpec((1, 8, D), lambda i: (i // tps, 0, 0)), pl.BlockSpec((8, NH * HP), lambda i: (0, 0))],
        compiler_params=_cp(("arbitrary",)),
    )(dx1, o, yssm, oattn, modp, gs, ga, w_out)


def _wgrad(a, b, name, col_slots=0):
    n, k1 = a.shape
    k2 = b.shape[1]
    bn = next((b for b in (1024, 512) if n % b == 0), n)
    bk1 = next((b for b in (1024, 512) if k1 % b == 0), k1)
    bk2 = k2 // col_slots if col_slots else (1024 if (k2 % 1024 == 0) else k2)

    def body(a_ref, b_ref, o_ref):
        @pl.when(pl.program_id(2) == 0)
        def _():
            o_ref[...] = jnp.zeros_like(o_ref)

        o_ref[...] += _dot_tn(a_ref[...], b_ref[...]).reshape(o_ref.shape)

    if col_slots:
        out_shape = jax.ShapeDtypeStruct((col_slots, k1, bk2), F32)
        out_spec = pl.BlockSpec((1, bk1, bk2), lambda i, j, t: (j, i, 0))
    else:
        out_shape = jax.ShapeDtypeStruct((k1, k2), F32)
        out_spec = pl.BlockSpec((bk1, bk2), lambda i, j, t: (i, j))
    return pl.pallas_call(
        body, name=name, grid=(k1 // bk1, k2 // bk2, n // bn),
        out_shape=out_shape,
        in_specs=[pl.BlockSpec((bn, bk1), lambda i, j, t: (t, i)), pl.BlockSpec((bn, bk2), lambda i, j, t: (t, j))],
        out_specs=out_spec,
        compiler_params=_cp(("parallel", "parallel", "arbitrary")),
    )(a, b)


def _row_block(rows):
    if rows <= 256:
        return rows
    return next(b for b in (256, 192, 128, 64, 32, 16, 8) if rows % b == 0)


def _add_half(g, recv, cidx, name):
    _, rows2, w = g.shape
    rows = rows2 // 2
    br = _row_block(rows)
    nblk = rows // br

    def body(c_ref, g_ref, r_ref, o_ref):
        o_ref[...] = (g_ref[...] + r_ref[...]).astype(BF16)

    return pl.pallas_call(
        body, name=name,
        grid_spec=pltpu.PrefetchScalarGridSpec(
            num_scalar_prefetch=1, grid=(4, nblk),
            in_specs=[pl.BlockSpec((1, br, w), lambda s, i, c: (s, c[0] * nblk + i, 0)),
                      pl.BlockSpec((1, br, w), lambda s, i, c: (s, i, 0))],
            out_specs=pl.BlockSpec((1, br, w), lambda s, i, c: (s, i, 0))),
        out_shape=jax.ShapeDtypeStruct((4, rows, w), BF16),
        compiler_params=_cp(("parallel", "parallel")),
    )(cidx, g, recv)


def _add_chips(r, name):
    _, rows, w = r.shape
    br = _row_block(rows)

    def body(r_ref, o_ref):
        f = lambda k: r_ref[k].astype(F32)
        o_ref[...] = ((f(0) + f(1)) + f(2)) + f(3)

    return pl.pallas_call(
        body, name=name, grid=(rows // br,),
        out_shape=jax.ShapeDtypeStruct((rows, w), F32),
        in_specs=[pl.BlockSpec((4, br, w), lambda i: (0, i, 0))],
        out_specs=pl.BlockSpec((br, w), lambda i: (i, 0)),
        compiler_params=_cp(("parallel",)),
    )(r)


def _sum_devices(a):
    def body(a_ref, o_ref):
        acc = a_ref[0:1, :]
        for k in range(1, 8):
            acc = acc + a_ref[k:k + 1, :]
        o_ref[...] = acc

    return pl.pallas_call(
        body, name="small_grad_sum", out_shape=jax.ShapeDtypeStruct((1, a.shape[1]), F32),
        in_specs=[_VM], out_specs=_VM, compiler_params=_cp(),
    )(a)


def _adamw_math(wv, gv, mv, vv):
    m_new = ADAM_B1 * mv + (1.0 - ADAM_B1) * gv
    v_new = ADAM_B2 * vv + (1.0 - ADAM_B2) * (gv * gv)
    m_hat = m_new / (1.0 - ADAM_B1 ** ADAM_STEP)
    v_hat = v_new / (1.0 - ADAM_B2 ** ADAM_STEP)
    return -ADAM_LR * (m_hat / (jnp.sqrt(v_hat) + ADAM_EPS) + ADAM_WD * wv), m_new, v_new


def _adamw_small(ws, gs, ms, vs):
    k = len(ws)

    def body(*refs):
        ins, outs = refs[:4 * k], refs[4 * k:]
        for t in range(k):
            d, m_new, v_new = _adamw_math(ins[t][...], ins[k + t][...], ins[2 * k + t][...], ins[3 * k + t][...])
            outs[t][...] = d
            outs[k + t][...] = m_new
            outs[2 * k + t][...] = v_new

    shapes = [jax.ShapeDtypeStruct(w.shape, F32) for w in ws]
    return pl.pallas_call(
        body, name="adamw_small", out_shape=shapes * 3,
        in_specs=[_VM] * (4 * k), out_specs=[_VM] * (3 * k), compiler_params=_cp(),
    )(*ws, *gs, *ms, *vs)


def _adamw(w, g, m, v, name):
    rows, wd = w.shape
    br = _row_block(rows)

    def body(w_ref, g_ref, m_ref, v_ref, d_ref, nm_ref, nv_ref):
        d, m_new, v_new = _adamw_math(w_ref[...], g_ref[...], m_ref[...], v_ref[...])
        d_ref[...] = d
        nm_ref[...] = m_new
        nv_ref[...] = v_new

    spec = pl.BlockSpec((br, wd), lambda i: (i, 0))
    return pl.pallas_call(
        body, name=name, grid=(rows // br,),
        out_shape=[jax.ShapeDtypeStruct((rows, wd), F32)] * 3,
        in_specs=[spec] * 4, out_specs=[spec] * 3,
        compiler_params=_cp(("parallel",)),
    )(w, g, m, v)


def _adamw_halves(w, mine, other, m, v, cidx, name):
    rows, wd = w.shape
    h = rows // 2
    br = _row_block(h)
    nblk = h // br

    def body(c_ref, w_ref, a_ref, b_ref, m_ref, v_ref, g_ref, d_ref, nm_ref, nv_ref):
        gv = jnp.where(pl.program_id(0) == c_ref[0], a_ref[...], b_ref[...])
        d, m_new, v_new = _adamw_math(w_ref[...], gv, m_ref[...], v_ref[...])
        g_ref[...] = gv
        d_ref[...] = d
        nm_ref[...] = m_new
        nv_ref[...] = v_new

    full = pl.BlockSpec((br, wd), lambda hf, i, c: (hf * nblk + i, 0))
    half = pl.BlockSpec((br, wd), lambda hf, i, c: (i, 0))
    return pl.pallas_call(
        body, name=name,
        grid_spec=pltpu.PrefetchScalarGridSpec(
            num_scalar_prefetch=1, grid=(2, nblk),
            in_specs=[full, half, half, full, full], out_specs=[full] * 4),
        out_shape=[jax.ShapeDtypeStruct((rows, wd), F32)] * 4,
        compiler_params=_cp(("parallel", "parallel")),
    )(cidx, w, mine, other, m, v)


def _other_chips(x, y):
    return [(1 - x, y), (x, 1 - y), (1 - x, 1 - y)]


def _other_devices(x, y, c):
    flip = lambda v, d: (1 - v) if d else v
    return [(flip(x, dx), flip(y, dy), flip(c, dc))
            for dx in (0, 1) for dy in (0, 1) for dc in (0, 1) if (dx, dy, dc) != (0, 0, 0)]


def _exchange(name, ins, out_shapes, n_local, n_remote, plan):
    ni, no = len(ins), len(out_shapes)

    def body(*refs):
        in_refs, out_refs = refs[:ni], refs[ni:ni + no]
        send_sems, recv_sems, local_sems = refs[ni + no:]
        x, y, c = lax.axis_index("x"), lax.axis_index("y"), lax.axis_index("c")
        local, remote = plan(in_refs, out_refs, x, y, c)
        assert len(local) == n_local and len(remote) == n_remote

        def push(k, src, dst, dev):
            return pltpu.make_async_remote_copy(src_ref=src, dst_ref=dst, send_sem=send_sems.at[k],
                                                recv_sem=recv_sems.at[k], device_id=dev, device_id_type=MESH)

        own = [pltpu.make_async_copy(s, d, local_sems.at[i]) for i, (s, d) in enumerate(local)]
        for cp in own:
            cp.start()
        sends = [push(k, s, d, dev) for k, (s, d, dev, _) in enumerate(remote)]
        for cp in sends:
            cp.start()
        for k, (s, _, dev, landing) in enumerate(remote):
            push(k, s, landing, dev).wait_recv()
        for cp in sends:
            cp.wait_send()
        for cp in own:
            cp.wait()

    return pl.pallas_call(
        body, name=name, out_shape=out_shapes,
        in_specs=[_ANY] * ni, out_specs=[_ANY] * no,
        scratch_shapes=[pltpu.SemaphoreType.DMA((n_remote,)), pltpu.SemaphoreType.DMA((n_remote,)),
                        pltpu.SemaphoreType.DMA((max(n_local, 1),))],
        compiler_params=pltpu.CompilerParams(has_side_effects=True),
    )(*ins)


def _gather_chips(name, shards, everyone=()):
    ns, ne = len(shards), len(everyone)
    outs = [jax.ShapeDtypeStruct((4,) + a.shape, a.dtype) for a in shards]
    outs += [jax.ShapeDtypeStruct((8,) + a.shape, a.dtype) for a in everyone]

    def plan(i, o, x, y, c):
        mine, me = 2 * x + y, 4 * x + 2 * y + c
        local, remote = [], []
        for t in range(ns):
            local.append((i[t], o[t].at[mine]))
            for px, py in _other_chips(x, y):
                remote.append((i[t], o[t].at[mine], (px, py, c), o[t].at[2 * px + py]))
        for t in range(ns, ns + ne):
            local.append((i[t], o[t].at[me]))
            for px, py, pc in _other_devices(x, y, c):
                remote.append((i[t], o[t].at[me], (px, py, pc), o[t].at[4 * px + 2 * py + pc]))
        return local, remote

    return _exchange(name, list(shards) + list(everyone), outs, ns + ne, 3 * ns + 7 * ne, plan)


def _swap_halves(gs, everyone):
    ns, ne = len(gs), len(everyone)
    outs = [jax.ShapeDtypeStruct((4, g.shape[1] // 2, g.shape[2]), g.dtype) for g in gs]
    outs += [jax.ShapeDtypeStruct((8,) + a.shape, a.dtype) for a in everyone]

    def plan(i, o, x, y, c):
        me = 4 * x + 2 * y + c
        local, remote = [], []
        for t in range(ns):
            h = gs[t].shape[1] // 2
            theirs = i[t].at[:, pl.ds(pl.multiple_of((1 - c) * h, 8), h), :]
            remote.append((theirs, o[t], (x, y, 1 - c), o[t]))
        for t in range(ns, ns + ne):
            local.append((i[t], o[t].at[me]))
            for px, py, pc in _other_devices(x, y, c):
                remote.append((i[t], o[t].at[me], (px, py, pc), o[t].at[4 * px + 2 * py + pc]))
        return local, remote

    return _exchange("grad_swap_sibling", list(gs) + list(everyone), outs, ne, ns + 7 * ne, plan)


def _scatter_chips(parts):
    ns = len(parts)
    outs = [jax.ShapeDtypeStruct(a.shape, a.dtype) for a in parts]

    def plan(i, o, x, y, c):
        mine = 2 * x + y
        local, remote = [], []
        for t in range(ns):
            local.append((i[t].at[mine], o[t].at[mine]))
            for px, py in _other_chips(x, y):
                remote.append((i[t].at[2 * px + py], o[t].at[mine], (px, py, c), o[t].at[2 * px + py]))
        return local, remote

    return _exchange("grad_scatter_chips", list(parts), outs, ns, 3 * ns, plan)


def _join_halves(halves):
    ns = len(halves)
    outs = [jax.ShapeDtypeStruct(a.shape, a.dtype) for a in halves]

    def plan(i, o, x, y, c):
        return [], [(i[t], o[t], (x, y, 1 - c), o[t]) for t in range(ns)]

    return _exchange("grad_join_sibling", list(halves), outs, 0, ns, plan)


def _pad_heads_cols(w, per, used):
    k = w.shape[0]
    w = w.reshape(k, NH, per)[:, :, :used]
    return jnp.pad(w, ((0, 0), (0, 0), (0, HP - used))).reshape(k, NH * HP)


def _unpad_heads_cols(w, used):
    k = w.shape[0]
    return w.reshape(k, NH, HP)[:, :, :used]


def _prep_weights(wf):
    bf = lambda a: a.astype(BF16)
    out = {}
    out["w_in"] = jnp.pad(bf(wf["w_in"]), ((0, 0), (0, IN_PAD - IN_COLS)))
    out["w_glu"] = bf(wf["w_glu"])
    out["w_uq"] = _pad_heads_cols(bf(wf["w_uq"]), QK_NOPE + QK_ROPE, QK_NOPE + QK_ROPE)
    wkv = bf(wf["w_ukv"]).reshape(KV_LORA, NH, QK_NOPE + V_HEAD)
    wk = jnp.pad(wkv[:, :, :QK_NOPE], ((0, 0), (0, 0), (0, HP - QK_NOPE))).reshape(KV_LORA, NH * HP)
    wv = jnp.pad(wkv[:, :, QK_NOPE:], ((0, 0), (0, 0), (0, HP - V_HEAD))).reshape(KV_LORA, NH * HP)
    out["w_ukv"] = jnp.concatenate([wk, wv], axis=1)
    wo = bf(wf["w_out"])
    wo_a = jnp.pad(wo[D_SSM:].reshape(NH, V_HEAD, D), ((0, 0), (0, HP - V_HEAD), (0, 0))).reshape(NH * HP, D)
    out["w_out"] = jnp.concatenate([wo[:D_SSM], wo_a], axis=0)
    out["w_ff1"] = bf(wf["w_ff1"])
    out["w_ff2"] = bf(wf["w_ff2"])
    return out


def _rope_tables(positions):
    inv_freq = ROPE_BASE ** (-jnp.arange(0, QK_ROPE, 2, dtype=F32) / QK_ROPE)
    ang = positions.astype(F32)[:, None] * inv_freq
    cos, sin = jnp.cos(ang), jnp.sin(ang)
    n = positions.shape[0]
    one = jnp.ones((n, QK_NOPE), F32)
    z16 = jnp.zeros((n, 16), F32)
    z32 = jnp.zeros((n, 32), F32)
    z64 = jnp.zeros((n, QK_NOPE), F32)
    rc = jnp.concatenate([one, cos, cos, z32], axis=1)
    rs1 = jnp.concatenate([z64, -sin, z16, z32], axis=1)
    rs2 = jnp.concatenate([z64, z16, sin, z32], axis=1)
    return rc, rs1, rs2


def _permute_rows(a, S):
    n, w = a.shape
    return a.reshape(n // S, 8, S // 8, w).transpose(0, 2, 1, 3).reshape(n, w)


def _unpermute_rows(a, S):
    n, w = a.shape
    return a.reshape(n // S, S // 8, 8, w).transpose(0, 2, 1, 3).reshape(n, w)


def _block_diag_in(bb):
    eye = jnp.eye(G, dtype=bb.dtype)
    return jnp.einsum("gph,gk->ghkp", bb, eye).reshape(G * H, G * P)


def _block_diag_out(cc):
    eye = jnp.eye(G, dtype=cc.dtype)
    return jnp.einsum("ghp,gk->gpkh", cc, eye).reshape(G * P, G * H)


def _slots(full):
    r, cdim = full.shape
    return full.reshape(r, 4, cdim // 4).transpose(1, 0, 2)


def _unslots(g):
    s, r, cs = g.shape
    return g.transpose(1, 0, 2).reshape(r, s * cs)


def _local_step(x, positions, target, modp, wf):
    nb, S, _ = x.shape
    n = nb * S
    tm = min(256, S)
    tt = min(256, S)
    tq = min(512, S // 2)
    kw = _prep_weights(wf)
    row = lambda a: a.reshape(1, -1).astype(F32)

    xf = x.reshape(n, D)
    tf = target.reshape(n, D)
    g1, g2, gf = row(wf["norm1_g"]), row(wf["norm2_g"]), row(wf["final_norm_g"])
    h1, proj = _f1_fwd(xf, modp, g1, kw["w_in"], S, tm)

    col = lambda a: a.reshape(NST, 1)
    lam_re, lam_im = col(wf["ssm_lambda_re"]), col(wf["ssm_lambda_im"])
    logdt = jnp.repeat(wf["ssm_log_dt"].reshape(G, 1), P, axis=1).reshape(NST, 1)
    b_re, b_im = wf["ssm_b_re"].reshape(NST, H), wf["ssm_b_im"].reshape(NST, H)
    lbr, lbi, bbr, bbi = _ssm_param_fwd(lam_re, lam_im, logdt, b_re, b_im)
    lre8 = jnp.broadcast_to(lbr.reshape(1, NST), (8, NST))
    lim8 = jnp.broadcast_to(lbi.reshape(1, NST), (8, NST))
    bm = jnp.concatenate([_block_diag_in(bbr.reshape(G, P, H)), _block_diag_in(bbi.reshape(G, P, H))],
                         axis=1).astype(BF16)
    cm = jnp.concatenate([_block_diag_out(wf["ssm_c_re"]), -_block_diag_out(wf["ssm_c_im"])], axis=0).astype(BF16)
    dvec = row(wf["ssm_d"])
    u_p = _permute_rows(proj[:, :D_SSM], S)
    fcr, fci = _ssm_local(u_p, bm, lre8, lim8, S, tt)
    st, ypre, z, gact, yssm_p = _ssm_fwd(u_p, fcr, fci, bm, cm, dvec, kw["w_glu"], lre8, lim8, S, tt)
    yssm = _unpermute_rows(yssm_p, S)

    rc, rs1, rs2 = _rope_tables(positions.reshape(n))
    gq, gkv = row(wf["q_norm_g"]), row(wf["kv_norm_g"])
    q, k, v, qn, kvn = _mla_fwd(proj, rc, rs1, rs2, gq, gkv, kw["w_uq"], kw["w_ukv"], tm)
    oattn, lrow = _attn_fwd(q, k, v, S, tq)

    gs = row(wf["ssm_out_g"])
    ga = jnp.pad(wf["attn_out_g"].reshape(NH, V_HEAD), ((0, 0), (0, HP - V_HEAD))).reshape(1, NH * HP)
    yn, o, x1, h2 = _p1_fwd(yssm, oattn, xf, modp, gs, ga, kw["w_out"], g2, S, tm)
    dx1, r, da, dff, accs2, accg2 = _p2(x1, h2, tf, modp, g2, gf, kw["w_ff1"], kw["w_ff2"], S, tm)
    loss = jnp.sum(accg2[2])
    do, dyssm, dob, drow, accs3, accg3 = _p3_bwd(dx1, o, yssm, oattn, modp, gs, ga, kw["w_out"], S, tm)

    dq, dk, dv = _attn_bwd(q, k, v, dob, lrow, drow, S, tq)
    dmla, dqb, dkvb, accm = _mla_bwd(dq, dk, dv, proj, rc, rs1, rs2, gq, gkv, kw["w_uq"], kw["w_ukv"], tm)

    dys_p = _permute_rows(dyssm, S)
    dy, dz, air, aii = _ssm_bwd_a(dys_p, z, ypre, kw["w_glu"], cm, lre8, lim8, S, tt)
    du_p, dcm, dbm, dd, dlr, dli = _ssm_bwd_b(dy, u_p, st, fcr, fci, air, aii, bm, cm, dvec, lre8, lim8, S, tt)
    du = _unpermute_rows(du_p, S)
    dcm = dcm.reshape(2, 4, 8, P, 8, H)
    dc_re = jnp.einsum("qgpgh->qghp", dcm[0]).reshape(G, H, P)
    dc_im = -jnp.einsum("qgpgh->qghp", dcm[1]).reshape(G, H, P)
    dbm = dbm.reshape(8, H, 2, 4, 8, P)
    dbb_re = jnp.einsum("ghqgp->qgph", dbm[:, :, 0]).reshape(NST, H)
    dbb_im = jnp.einsum("ghqgp->qgph", dbm[:, :, 1]).reshape(NST, H)
    gb_re, gb_im, glr, gli, gdt = _ssm_param_bwd(lam_re, lam_im, logdt, b_re, b_im, dlr.reshape(NST, 1),
                                                 dli.reshape(NST, 1), dbb_re, dbb_im)
    glogdt = _rowsum(gdt.reshape(G, P))

    dx, dproj, accs1, accg1 = _f1_bwd(du, dmla, dx1, xf, modp, g1, kw["w_in"], S, tm)

    big = {}
    big["w_in"] = _slots(_wgrad(h1, dproj, "wgrad_in")[:, :IN_COLS])
    big["w_glu"] = _wgrad(gact, dz, "wgrad_glu", col_slots=4)
    big["w_uq"] = _slots(_unpad_heads_cols(_wgrad(qn, dqb, "wgrad_uq"), QK_NOPE + QK_ROPE).reshape(Q_LORA, -1))
    gkvw = _wgrad(kvn, dkvb, "wgrad_ukv")
    big["w_ukv"] = _slots(jnp.concatenate([_unpad_heads_cols(gkvw[:, :NH * HP], QK_NOPE),
                                           _unpad_heads_cols(gkvw[:, NH * HP:], V_HEAD)], axis=2).reshape(KV_LORA, -1))
    gwo = _wgrad(yn, do, "wgrad_out")
    big["w_out"] = jnp.concatenate([gwo[:D_SSM].reshape(2, D_SSM // 2, D),
                                    gwo[D_SSM:].reshape(2, NH // 2 * HP, D).reshape(2, NH // 2, HP, D)[:, :, :V_HEAD]
                                    .reshape(2, D_ATTN // 2, D)], axis=0)
    big["w_ff1"] = _wgrad(h2, da, "wgrad_ff1", col_slots=4)
    big["w_ff2"] = _wgrad(r, dff, "wgrad_ff2").reshape(4, D_FF // 4, D)

    small = {}
    small["norm1_g"] = accg1[0:1]
    small["norm2_g"] = accg2[0:1]
    small["final_norm_g"] = accg2[1:2]
    small["ssm_out_g"] = accg3[0:1, :D_SSM]
    small["attn_out_g"] = accg3[1].reshape(NH, HP)[:, :V_HEAD].reshape(1, D_ATTN)
    small["q_norm_g"] = accm[0:1, :Q_LORA]
    small["kv_norm_g"] = accm[1:2, :KV_LORA]
    small["ssm_lambda_re"] = glr.reshape(G, P)
    small["ssm_lambda_im"] = gli.reshape(G, P)
    small["ssm_b_re"] = gb_re
    small["ssm_b_im"] = gb_im
    small["ssm_c_re"] = dc_re.reshape(G * H, P)
    small["ssm_c_im"] = dc_im.reshape(G * H, P)
    small["ssm_d"] = dd.reshape(G, H)
    small["ssm_log_dt"] = glogdt.reshape(1, G)
    return loss, dx.reshape(nb, S, D), big, small, accs1 + accs2 + accs3


def _view2d(a):
    return a.reshape(-1, a.shape[-1]) if a.ndim > 1 else a.reshape(1, -1)


def kernel(x, c, positions, ada_w, ada_b, norm1_g, w_in, ssm_lambda_re, ssm_lambda_im, ssm_b_re, ssm_b_im, ssm_c_re, ssm_c_im, ssm_d, ssm_log_dt, w_glu, q_norm_g, w_uq, kv_norm_g, w_ukv, ssm_out_g, attn_out_g, w_out, norm2_g, w_ff1, w_ff2, final_ada_w, final_ada_b, final_norm_g, loss_target, m_ada_w, m_ada_b, m_norm1_g, m_w_in, m_ssm_lambda_re, m_ssm_lambda_im, m_ssm_b_re, m_ssm_b_im, m_ssm_c_re, m_ssm_c_im, m_ssm_d, m_ssm_log_dt, m_w_glu, m_q_norm_g, m_w_uq, m_kv_norm_g, m_w_ukv, m_ssm_out_g, m_attn_out_g, m_w_out, m_norm2_g, m_w_ff1, m_w_ff2, m_final_ada_w, m_final_ada_b, m_final_norm_g, v_ada_w, v_ada_b, v_norm1_g, v_w_in, v_ssm_lambda_re, v_ssm_lambda_im, v_ssm_b_re, v_ssm_b_im, v_ssm_c_re, v_ssm_c_im, v_ssm_d, v_ssm_log_dt, v_w_glu, v_q_norm_g, v_w_uq, v_kv_norm_g, v_w_ukv, v_ssm_out_g, v_attn_out_g, v_w_out, v_norm2_g, v_w_ff1, v_w_ff2, v_final_ada_w, v_final_ada_b, v_final_norm_g):
    args = dict(locals())
    names = list(inspect.signature(kernel).parameters)
    wnames = names[3:names.index("loss_target")]
    small_names = [nm for nm in wnames if nm not in GATHERED and nm not in TP]
    reduced_names = [nm for nm in small_names if nm not in ("ada_b", "final_ada_b")]
    w = {nm: args[nm] for nm in wnames}
    m = {nm: args["m_" + nm] for nm in wnames}
    v = {nm: args["v_" + nm] for nm in wnames}
    nb = x.shape[0]
    xi, yi, ci = lax.axis_index("x"), lax.axis_index("y"), lax.axis_index("c")
    chip, me = 2 * xi + yi, 4 * xi + 2 * yi + ci

    got = _gather_chips("gather_weights", [_view2d(w[nm]).astype(BF16) for nm in GATHERED], [c])
    wf = {nm: (g.reshape(-1, g.shape[-1]) if nm in ROW_SHARDED else _unslots(g)) for nm, g in zip(GATHERED, got)}
    for nm in small_names:
        wf[nm] = w[nm][0] if w[nm].ndim > 1 else w[nm]
    c_all = got[len(GATHERED)].reshape(8 * nb, D)

    na, nf = ada_w.shape[-1], final_ada_w.shape[-1]
    ada_b_s = lax.dynamic_slice(ada_b, (0, chip * na), (1, na))
    fada_b_s = lax.dynamic_slice(final_ada_b.reshape(1, -1), (0, chip * nf), (1, nf))
    cond_all, modcols = _mod_fwd(c_all, ada_w[0], ada_b_s, final_ada_w, fada_b_s)
    (mod_g,) = _gather_chips("gather_mod", [modcols])
    mine = lax.dynamic_slice(mod_g, (0, me * nb, 0), (4, nb, na + nf))
    modp = jnp.concatenate([mine[:, :, :na].transpose(1, 0, 2).reshape(nb, 6, D),
                            mine[:, :, na:].transpose(1, 0, 2).reshape(nb, 2, D)], axis=1)

    loss, grad_x, big, small, dmodp = _local_step(x, positions, loss_target, modp, wf)
    loss = lax.psum(loss, ("x", "y", "c"))

    sizes = [small[nm].size for nm in reduced_names]
    pad = -sum(sizes) % 128
    packed = jnp.concatenate([small[nm].reshape(1, -1) for nm in reduced_names] + [jnp.zeros((1, pad), F32)], axis=1)
    swapped = _swap_halves([big[nm] for nm in GATHERED], [dmodp.reshape(nb, 8 * D), packed])
    cidx = ci.astype(jnp.int32).reshape(1)
    chip_sums = [_add_half(big[nm], r, cidx, "grad_add_sibling_" + nm) for nm, r in zip(GATHERED, swapped)]
    halves = [_add_chips(r, "grad_add_chips_" + nm) for nm, r in zip(GATHERED, _scatter_chips(chip_sums))]
    others = _join_halves(halves)
    grads = {}
    dmod_all = swapped[len(GATHERED)].reshape(8 * nb, 8 * D)
    small_sum = _sum_devices(swapped[len(GATHERED) + 1].reshape(8, -1))
    off = 0
    for nm, sz in zip(reduced_names, sizes):
        grads[nm] = small_sum[:, off:off + sz].reshape(small[nm].shape)
        off += sz

    dsl = jnp.concatenate([lax.dynamic_slice(dmod_all, (0, chip * na), (8 * nb, na)),
                           lax.dynamic_slice(dmod_all, (0, 6 * D + chip * nf), (8 * nb, nf))], axis=1)
    gw, gb = _mod_bwd(cond_all.T, dsl, dmod_all)
    grads["ada_w"], grads["final_ada_w"] = gw[:, :na], gw[:, na:]
    grads["ada_b"], grads["final_ada_b"] = gb[:, :6 * D], gb[:, 6 * D:]

    delta, new_m, new_v = {}, {}, {}
    for nm, mine_h, other_h in zip(GATHERED, halves, others):
        grads[nm], delta[nm], new_m[nm], new_v[nm] = _adamw_halves(
            _view2d(w[nm]), mine_h, other_h, _view2d(m[nm]), _view2d(v[nm]), cidx, "adamw_" + nm)
    for nm in TP:
        delta[nm], new_m[nm], new_v[nm] = _adamw(_view2d(w[nm]), grads[nm], _view2d(m[nm]), _view2d(v[nm]),
                                                  "adamw_" + nm)
    upd = _adamw_small([_view2d(w[nm]) for nm in small_names], [grads[nm] for nm in small_names],
                       [_view2d(m[nm]) for nm in small_names], [_view2d(v[nm]) for nm in small_names])
    k = len(small_names)
    for t, nm in enumerate(small_names):
        delta[nm], new_m[nm], new_v[nm] = upd[t], upd[k + t], upd[2 * k + t]

    outs = [grads, delta, new_m, new_v]
    return (loss, grad_x, *[d[nm].reshape(w[nm].shape) for d in outs for nm in wnames])
```

```python
import functools
import inspect
import math

import jax
import jax.numpy as jnp
from jax import lax
from jax.experimental import pallas as pl
from jax.experimental.pallas import tpu as pltpu

F32 = jnp.float32
BF16 = jnp.bfloat16

D = 1024
D_SSM = 512
G = 32
H = 16
P = 64
NST = G * P
D_ATTN = 512
NH = 8
QK_NOPE = 64
QK_ROPE = 32
V_HEAD = 64
HP = 128
Q_LORA = 384
KV_LORA = 256
IN_COLS = D_SSM + Q_LORA + KV_LORA + QK_ROPE
IN_PAD = 1280
D_FF = 4096
ROPE_BASE = 10000.0
EPS = 1e-6
ADAM_LR = 0.001
ADAM_B1 = 0.9
ADAM_B2 = 0.999
ADAM_EPS = 1e-08
ADAM_WD = 0.01
ADAM_STEP = 10
NEG = -1e30
VMEM_LIMIT = 60 << 20

MESH = pl.DeviceIdType.MESH
_VM = pl.BlockSpec(memory_space=pltpu.VMEM)
_ANY = pl.BlockSpec(memory_space=pl.ANY)

GATHERED = ["w_in", "w_glu", "w_uq", "w_ukv", "w_out", "w_ff1", "w_ff2"]
TP = ["ada_w", "final_ada_w"]
ROW_SHARDED = ("w_out", "w_ff2")
LATE = ["w_out", "w_ff1", "w_ff2"]


def _cp(sem=None, vmem=VMEM_LIMIT):
    kw = dict(vmem_limit_bytes=vmem)
    if sem is not None:
        kw["dimension_semantics"] = sem
    return pltpu.CompilerParams(**kw)


def _dot(a, b):
    return jnp.dot(a, b, preferred_element_type=F32)


def _dot_nt(a, b):
    return lax.dot_general(a, b, (((1,), (1,)), ((), ())), preferred_element_type=F32)


def _dot_tn(a, b):
    return lax.dot_general(a, b, (((0,), (0,)), ((), ())), preferred_element_type=F32)


def _rms(x, n):
    r = lax.rsqrt(jnp.sum(x * x, axis=-1, keepdims=True) * (1.0 / n) + EPS)
    return x * r, r


def _rms_bwd(dyg, xhat, r, n):
    return r * (dyg - xhat * (jnp.sum(dyg * xhat, axis=-1, keepdims=True) * (1.0 / n)))


def _sigmoid(x):
    return 1.0 / (1.0 + jnp.exp(-x))


_GK = math.sqrt(2.0 / math.pi)
_GC = 0.044715


def _gelu(y):
    t = jnp.tanh(_GK * (y + _GC * y * y * y))
    return 0.5 * y * (1.0 + t)


def _gelu_grad(y):
    t = jnp.tanh(_GK * (y + _GC * y * y * y))
    return 0.5 * (1.0 + t) + 0.5 * y * (1.0 - t * t) * _GK * (1.0 + 3.0 * _GC * y * y)


def _colsum(x):
    return jnp.sum(x, axis=0, keepdims=True)


def _roll(x, s):
    return pltpu.roll(x, s % x.shape[-1], x.ndim - 1)


def _mod_fwd(c_all, ada_w_s, ada_b_s, fada_w_s, fada_b_s):
    nseq = c_all.shape[0]
    na, nf = ada_w_s.shape[1], fada_w_s.shape[1]

    def body(c_ref, w_ref, b_ref, fw_ref, fb_ref, cond_ref, mod_ref):
        cv = c_ref[...]
        cond = cv * _sigmoid(cv)
        cond_ref[...] = cond
        cb = cond.astype(BF16)
        mod_ref[:, 0:na] = _dot(cb, w_ref[...].astype(BF16)) + b_ref[...]
        mod_ref[:, na:na + nf] = _dot(cb, fw_ref[...].astype(BF16)) + fb_ref[...]

    return pl.pallas_call(
        body, name="mod_fwd",
        out_shape=[jax.ShapeDtypeStruct((nseq, D), F32), jax.ShapeDtypeStruct((nseq, na + nf), F32)],
        in_specs=[_VM] * 5, out_specs=[_VM] * 2, compiler_params=_cp(),
    )(c_all, ada_w_s, ada_b_s, fada_w_s, fada_b_s)


def _mod_bwd(cond_t, dsl, dall):
    nseq, n = dsl.shape
    bc = 512

    def body(ct_ref, dm_ref, da_ref, gw_ref, gb_ref):
        ct = ct_ref[...]
        dm = dm_ref[...]
        acc = ct[:, 0:1] * dm[0:1, :]
        for b in range(1, nseq):
            acc = acc + ct[:, b:b + 1] * dm[b:b + 1, :]
        gw_ref[...] = acc

        @pl.when(pl.program_id(0) == 0)
        def _():
            gb_ref[...] = _colsum(da_ref[...])

    return pl.pallas_call(
        body, name="mod_bwd", grid=(n // bc,),
        out_shape=[jax.ShapeDtypeStruct((D, n), F32), jax.ShapeDtypeStruct((1, dall.shape[1]), F32)],
        in_specs=[_VM, pl.BlockSpec((nseq, bc), lambda i: (0, i)), _VM],
        out_specs=[pl.BlockSpec((D, bc), lambda i: (0, i)), pl.BlockSpec((1, dall.shape[1]), lambda i: (0, 0))],
        compiler_params=_cp(("arbitrary",)),
    )(cond_t, dsl, dall)


def _f1_fwd(x, modp, g1, w_in, S, tm):
    n = x.shape[0]
    tps = S // tm

    def body(x_ref, mod_ref, g_ref, w_ref, h_ref, proj_ref):
        xhat, _ = _rms(x_ref[...], D)
        h = (xhat * g_ref[...]) * (1.0 + mod_ref[0, 1:2, :]) + mod_ref[0, 0:1, :]
        hb = h.astype(BF16)
        h_ref[...] = hb
        proj_ref[...] = _dot(hb, w_ref[...])

    return pl.pallas_call(
        body, name="f1_fwd", grid=(n // tm,),
        out_shape=[jax.ShapeDtypeStruct((n, D), BF16), jax.ShapeDtypeStruct((n, IN_PAD), F32)],
        in_specs=[pl.BlockSpec((tm, D), lambda i: (i, 0)),
                  pl.BlockSpec((1, 8, D), lambda i: (i // tps, 0, 0)), _VM, _VM],
        out_specs=[pl.BlockSpec((tm, D), lambda i: (i, 0)), pl.BlockSpec((tm, IN_PAD), lambda i: (i, 0))],
        compiler_params=_cp(("parallel",)),
    )(x, modp, g1, w_in)


def _f1_bwd(du, dmla, dx1, x, modp, g1, w_in, S, tm):
    n = x.shape[0]
    tps = S // tm
    nb = n // S

    def body(du_ref, dm_ref, dx1_ref, x_ref, mod_ref, g_ref, w_ref, dx_ref, dproj_ref, accs_ref, accg_ref):
        i = pl.program_id(0)
        dproj = jnp.concatenate([du_ref[...], dm_ref[...]], axis=1).astype(BF16)
        dproj_ref[...] = dproj
        dh = _dot_nt(dproj, w_ref[...])
        xhat, r = _rms(x_ref[...], D)
        g = g_ref[...]
        dn = dh * (1.0 + mod_ref[0, 1:2, :])
        dx_ref[...] = dx1_ref[...] + _rms_bwd(dn * g, xhat, r, D)

        @pl.when(i % tps == 0)
        def _():
            accs_ref[...] = jnp.zeros_like(accs_ref)

        @pl.when(i == 0)
        def _():
            accg_ref[...] = jnp.zeros_like(accg_ref)

        accs_ref[0, 0:1, :] += _colsum(dh)
        accs_ref[0, 1:2, :] += _colsum(dh * (xhat * g))
        accg_ref[0:1, :] += _colsum(dn * xhat)

    return pl.pallas_call(
        body, name="f1_bwd", grid=(n // tm,),
        out_shape=[jax.ShapeDtypeStruct((n, D), F32), jax.ShapeDtypeStruct((n, IN_PAD), BF16),
                   jax.ShapeDtypeStruct((nb, 8, D), F32), jax.ShapeDtypeStruct((8, D), F32)],
        in_specs=[pl.BlockSpec((tm, D_SSM), lambda i: (i, 0)), pl.BlockSpec((tm, IN_PAD - D_SSM), lambda i: (i, 0)),
                  pl.BlockSpec((tm, D), lambda i: (i, 0)), pl.BlockSpec((tm, D), lambda i: (i, 0)),
                  pl.BlockSpec((1, 8, D), lambda i: (i // tps, 0, 0)), _VM, _VM],
        out_specs=[pl.BlockSpec((tm, D), lambda i: (i, 0)), pl.BlockSpec((tm, IN_PAD), lambda i: (i, 0)),
                   pl.BlockSpec((1, 8, D), lambda i: (i // tps, 0, 0)), pl.BlockSpec((8, D), lambda i: (0, 0))],
        compiler_params=_cp(("arbitrary",)),
    )(du, dmla, dx1, x, modp, g1, w_in)


def _ssm_param_fwd(lam_re, lam_im, logdt, b_re, b_im):
    def body(lr_ref, li_ref, ld_ref, br_ref, bi_ref, lbr_ref, lbi_ref, bbr_ref, bbi_ref):
        lr, li = lr_ref[...], li_ref[...]
        dt = jnp.exp(ld_ref[...])
        er = jnp.exp(lr * dt)
        lbr = er * jnp.cos(li * dt)
        lbi = er * jnp.sin(li * dt)
        den = 1.0 / (lr * lr + li * li)
        cr = ((lbr - 1.0) * lr + lbi * li) * den
        ci = (lbi * lr - (lbr - 1.0) * li) * den
        lbr_ref[...] = lbr
        lbi_ref[...] = lbi
        bbr_ref[...] = cr * br_ref[...] - ci * bi_ref[...]
        bbi_ref[...] = cr * bi_ref[...] + ci * br_ref[...]

    return pl.pallas_call(
        body, name="ssm_param_fwd",
        out_shape=[jax.ShapeDtypeStruct((NST, 1), F32)] * 2 + [jax.ShapeDtypeStruct((NST, H), F32)] * 2,
        in_specs=[_VM] * 5, out_specs=[_VM] * 4, compiler_params=_cp(),
    )(lam_re, lam_im, logdt, b_re, b_im)


def _ssm_param_bwd(lam_re, lam_im, logdt, b_re, b_im, dlb_re, dlb_im, dbb_re, dbb_im):
    def body(lr_ref, li_ref, ld_ref, br_ref, bi_ref, dlr_ref, dli_ref, dbr_ref, dbi_ref,
             gbr_ref, gbi_ref, glr_ref, gli_ref, gdt_ref):
        lr, li = lr_ref[...], li_ref[...]
        dt = jnp.exp(ld_ref[...])
        er = jnp.exp(lr * dt)
        lbr = er * jnp.cos(li * dt)
        lbi = er * jnp.sin(li * dt)
        den = 1.0 / (lr * lr + li * li)
        nr, ni = lbr - 1.0, lbi
        cr = (nr * lr + ni * li) * den
        ci = (ni * lr - nr * li) * den
        br, bi = br_ref[...], bi_ref[...]
        dbr, dbi = dbr_ref[...], dbi_ref[...]
        gbr_ref[...] = cr * dbr + ci * dbi
        gbi_ref[...] = cr * dbi - ci * dbr
        gcr = jnp.sum(dbr * br + dbi * bi, axis=1, keepdims=True)
        gci = jnp.sum(dbi * br - dbr * bi, axis=1, keepdims=True)
        ilr, ili = lr * den, -li * den
        glbr = dlr_ref[...] + (gcr * ilr + gci * ili)
        glbi = dli_ref[...] + (gci * ilr - gcr * ili)
        qr = -(cr * ilr - ci * ili)
        qi = -(cr * ili + ci * ilr)
        glr = gcr * qr + gci * qi
        gli = gci * qr - gcr * qi
        glr = glr + dt * (glbr * lbr + glbi * lbi)
        gli = gli + dt * (glbi * lbr - glbr * lbi)
        wr = lr * lbr - li * lbi
        wi = lr * lbi + li * lbr
        glr_ref[...] = glr
        gli_ref[...] = gli
        gdt_ref[...] = (glbr * wr + glbi * wi) * dt

    return pl.pallas_call(
        body, name="ssm_param_bwd",
        out_shape=[jax.ShapeDtypeStruct((NST, H), F32)] * 2 + [jax.ShapeDtypeStruct((NST, 1), F32)] * 3,
        in_specs=[_VM] * 9, out_specs=[_VM] * 5, compiler_params=_cp(),
    )(lam_re, lam_im, logdt, b_re, b_im, dlb_re, dlb_im, dbb_re, dbb_im)


def _rowsum(a):
    def body(a_ref, o_ref):
        o_ref[...] = jnp.sum(a_ref[...], axis=1, keepdims=True)

    return pl.pallas_call(
        body, name="rowsum", out_shape=jax.ShapeDtypeStruct((a.shape[0], 1), F32),
        in_specs=[_VM], out_specs=_VM, compiler_params=_cp(),
    )(a)


def _pow2k(pr, pi, nsq):
    for _ in range(nsq):
        pr, pi = pr * pr - pi * pi, 2.0 * pr * pi
    return pr, pi


def _ssm_local(u_p, bm, lre8, lim8, S, tt):
    n = u_p.shape[0]
    nb, nt = n // S, S // tt
    nsq = int(round(math.log2(S // 8)))
    assert 2 ** nsq == S // 8

    def body(u_ref, bm_ref, lre_ref, lim_ref, cre_ref, cim_ref, sre, sim, bu):
        j = pl.program_id(1)

        @pl.when(j == 0)
        def _():
            sre[...] = jnp.zeros_like(sre)
            sim[...] = jnp.zeros_like(sim)

        bu[...] = _dot(u_ref[...].astype(BF16), bm_ref[...])
        lre, lim = lre_ref[...], lim_ref[...]

        def step(i, c):
            sr, si = c
            off = pl.multiple_of(i * 8, 8)
            br = bu[pl.ds(off, 8), 0:NST]
            bi = bu[pl.ds(off, 8), NST:2 * NST]
            return lre * sr - lim * si + br, lre * si + lim * sr + bi

        sr, si = lax.fori_loop(0, tt // 8, step, (sre[...], sim[...]))
        sre[...] = sr
        sim[...] = si

        @pl.when(j == nt - 1)
        def _():
            pr, pi = _pow2k(lre[0:1], lim[0:1], nsq)
            cr = jnp.zeros((1, NST), F32)
            ci = jnp.zeros((1, NST), F32)
            cre_ref[0:1, :] = cr
            cim_ref[0:1, :] = ci
            for k in range(1, 8):
                cr, ci = sr[k - 1:k] + pr * cr - pi * ci, si[k - 1:k] + pr * ci + pi * cr
                cre_ref[k:k + 1, :] = cr
                cim_ref[k:k + 1, :] = ci

    return pl.pallas_call(
        body, name="ssm_local", grid=(nb, nt),
        out_shape=[jax.ShapeDtypeStruct((nb * 8, NST), F32)] * 2,
        in_specs=[pl.BlockSpec((tt, D_SSM), lambda b, j: (b * nt + j, 0)), _VM, _VM, _VM],
        out_specs=[pl.BlockSpec((8, NST), lambda b, j: (b, 0))] * 2,
        scratch_shapes=[pltpu.VMEM((8, NST), F32), pltpu.VMEM((8, NST), F32), pltpu.VMEM((tt, 2 * NST), F32)],
        compiler_params=_cp(("arbitrary", "arbitrary")),
    )(u_p, bm, lre8, lim8)


def _ssm_fwd(u_p, cre, cim, bm, cm, dvec, w_glu, lre8, lim8, S, tt):
    n = u_p.shape[0]
    nb, nt = n // S, S // tt

    def body(u_ref, cre_ref, cim_ref, bm_ref, cm_ref, d_ref, wg_ref, lre_ref, lim_ref,
             st_ref, ypre_ref, z_ref, gact_ref, yssm_ref, sre, sim, bu):
        j = pl.program_id(1)

        @pl.when(j == 0)
        def _():
            sre[...] = cre_ref[...]
            sim[...] = cim_ref[...]

        u = u_ref[...]
        bu[...] = _dot(u.astype(BF16), bm_ref[...])
        lre, lim = lre_ref[...], lim_ref[...]

        def step(i, c):
            sr, si = c
            off = pl.multiple_of(i * 8, 8)
            nr = lre * sr - lim * si + bu[pl.ds(off, 8), 0:NST]
            ni = lre * si + lim * sr + bu[pl.ds(off, 8), NST:2 * NST]
            st_ref[pl.ds(off, 8), 0:NST] = nr
            st_ref[pl.ds(off, 8), NST:2 * NST] = ni
            return nr, ni

        sr, si = lax.fori_loop(0, tt // 8, step, (sre[...], sim[...]))
        sre[...] = sr
        sim[...] = si
        y = _dot(st_ref[...].astype(BF16), cm_ref[...]) + d_ref[...] * u
        ypre_ref[...] = y
        gb = _gelu(y).astype(BF16)
        gact_ref[...] = gb
        z = _dot(gb, wg_ref[...])
        z_ref[...] = z
        yssm_ref[...] = z[:, 0:D_SSM] * _sigmoid(z[:, D_SSM:2 * D_SSM])

    row = lambda w: pl.BlockSpec((tt, w), lambda b, j: (b * nt + j, 0))
    return pl.pallas_call(
        body, name="ssm_fwd", grid=(nb, nt),
        out_shape=[jax.ShapeDtypeStruct((n, 2 * NST), F32), jax.ShapeDtypeStruct((n, D_SSM), F32),
                   jax.ShapeDtypeStruct((n, 2 * D_SSM), F32), jax.ShapeDtypeStruct((n, D_SSM), BF16),
                   jax.ShapeDtypeStruct((n, D_SSM), F32)],
        in_specs=[row(D_SSM), pl.BlockSpec((8, NST), lambda b, j: (b, 0)), pl.BlockSpec((8, NST), lambda b, j: (b, 0)),
                  _VM, _VM, _VM, _VM, _VM, _VM],
        out_specs=[row(2 * NST), row(D_SSM), row(2 * D_SSM), row(D_SSM), row(D_SSM)],
        scratch_shapes=[pltpu.VMEM((8, NST), F32), pltpu.VMEM((8, NST), F32), pltpu.VMEM((tt, 2 * NST), F32)],
        compiler_params=_cp(("arbitrary", "arbitrary")),
    )(u_p, cre, cim, bm, cm, dvec, w_glu, lre8, lim8)


def _ssm_bwd_a(dys_p, z, ypre, w_glu, cm, lre8, lim8, S, tt):
    n = z.shape[0]
    nb, nt = n // S, S // tt
    nsq = int(round(math.log2(S // 8)))
    ng = tt // 8

    def body(dys_ref, z_ref, y_ref, wg_ref, cm_ref, lre_ref, lim_ref, dy_ref, dz_ref, are_ref, aim_ref, sre, sim, gb):
        j = pl.program_id(1)

        @pl.when(j == 0)
        def _():
            sre[...] = jnp.zeros_like(sre)
            sim[...] = jnp.zeros_like(sim)

        z = z_ref[...]
        z1, z2 = z[:, 0:D_SSM], z[:, D_SSM:2 * D_SSM]
        sg = _sigmoid(z2)
        dys = dys_ref[...]
        dz = jnp.concatenate([dys * sg, dys * z1 * sg * (1.0 - sg)], axis=1).astype(BF16)
        dz_ref[...] = dz
        dy = _dot_nt(dz, wg_ref[...]) * _gelu_grad(y_ref[...])
        dy_ref[...] = dy
        gb[...] = _dot_nt(dy.astype(BF16), cm_ref[...])
        lre, lim = lre_ref[...], lim_ref[...]

        def step(i, c):
            ar, ai = c
            off = pl.multiple_of((ng - 1 - i) * 8, 8)
            gr = gb[pl.ds(off, 8), 0:NST]
            gi = gb[pl.ds(off, 8), NST:2 * NST]
            return lre * ar + lim * ai + gr, lre * ai - lim * ar + gi

        ar, ai = lax.fori_loop(0, ng, step, (sre[...], sim[...]))
        sre[...] = ar
        sim[...] = ai

        @pl.when(j == nt - 1)
        def _():
            pr, pi = _pow2k(lre[0:1], -lim[0:1], nsq)
            cr = jnp.zeros((1, NST), F32)
            ci = jnp.zeros((1, NST), F32)
            are_ref[7:8, :] = cr
            aim_ref[7:8, :] = ci
            for k in range(6, -1, -1):
                cr, ci = ar[k + 1:k + 2] + pr * cr - pi * ci, ai[k + 1:k + 2] + pr * ci + pi * cr
                are_ref[k:k + 1, :] = cr
                aim_ref[k:k + 1, :] = ci

    row = lambda w: pl.BlockSpec((tt, w), lambda b, j: (b * nt + nt - 1 - j, 0))
    return pl.pallas_call(
        body, name="ssm_bwd_a", grid=(nb, nt),
        out_shape=[jax.ShapeDtypeStruct((n, D_SSM), F32), jax.ShapeDtypeStruct((n, 2 * D_SSM), BF16),
                   jax.ShapeDtypeStruct((nb * 8, NST), F32), jax.ShapeDtypeStruct((nb * 8, NST), F32)],
        in_specs=[row(D_SSM), row(2 * D_SSM), row(D_SSM), _VM, _VM, _VM, _VM],
        out_specs=[row(D_SSM), row(2 * D_SSM), pl.BlockSpec((8, NST), lambda b, j: (b, 0)),
                   pl.BlockSpec((8, NST), lambda b, j: (b, 0))],
        scratch_shapes=[pltpu.VMEM((8, NST), F32), pltpu.VMEM((8, NST), F32), pltpu.VMEM((tt, 2 * NST), F32)],
        compiler_params=_cp(("arbitrary", "arbitrary")),
    )(dys_p, z, ypre, w_glu, cm, lre8, lim8)


def _ssm_bwd_b(dy, u_p, st, fcr, fci, air, aii, bm, cm, dvec, lre8, lim8, S, tt):
    n = u_p.shape[0]
    nb, nt = n // S, S // tt
    ng = tt // 8
    QB = D_SSM // 4

    def body(dy_ref, u_ref, st_ref, stp_ref, fcr_ref, fci_ref, air_ref, aii_ref, bm_ref, cm_ref, d_ref, lre_ref, lim_ref,
             du_ref, dcm_ref, dbm_ref, dd_ref, dlr_ref, dli_ref, are, aim, accr, acci, sp, ab):
        b = pl.program_id(0)
        j = pl.program_id(1)
        jt = nt - 1 - j

        @pl.when((b == 0) & (j == 0))
        def _():
            dcm_ref[...] = jnp.zeros_like(dcm_ref)
            dbm_ref[...] = jnp.zeros_like(dbm_ref)
            dd_ref[...] = jnp.zeros_like(dd_ref)
            accr[...] = jnp.zeros_like(accr)
            acci[...] = jnp.zeros_like(acci)

        @pl.when(j == 0)
        def _():
            are[...] = air_ref[...]
            aim[...] = aii_ref[...]

        sp[8:tt + 8, :] = st_ref[...]

        @pl.when(jt == 0)
        def _():
            sp[0:8, 0:NST] = fcr_ref[...]
            sp[0:8, NST:2 * NST] = fci_ref[...]

        @pl.when(jt != 0)
        def _():
            sp[0:8, :] = stp_ref[...]

        dy = dy_ref[...]
        u = u_ref[...]
        dyb = dy.astype(BF16)
        ab[...] = _dot_nt(dyb, cm_ref[...])
        lre, lim = lre_ref[...], lim_ref[...]

        def step(i, c):
            ar, ai = c
            off = pl.multiple_of((ng - 1 - i) * 8, 8)
            nr = lre * ar + lim * ai + ab[pl.ds(off, 8), 0:NST]
            ni = lre * ai - lim * ar + ab[pl.ds(off, 8), NST:2 * NST]
            ab[pl.ds(off, 8), 0:NST] = nr
            ab[pl.ds(off, 8), NST:2 * NST] = ni
            pr = sp[pl.ds(off, 8), 0:NST]
            pi = sp[pl.ds(off, 8), NST:2 * NST]
            accr[...] += nr * pr + ni * pi
            acci[...] += ni * pr - nr * pi
            return nr, ni

        ar, ai = lax.fori_loop(0, ng, step, (are[...], aim[...]))
        are[...] = ar
        aim[...] = ai
        a_b = ab[...].astype(BF16)
        du_ref[...] = _dot_nt(a_b, bm_ref[...]) + d_ref[...] * dy
        ub = u.astype(BF16)
        for q in range(4):
            for part in range(2):
                lo = part * NST + q * 4 * QB
                s_q = sp[8:tt + 8, lo:lo + 4 * QB].astype(BF16)
                dcm_ref[lo:lo + 4 * QB, :] += _dot_tn(s_q, dyb[:, q * QB:(q + 1) * QB])
                dbm_ref[:, lo:lo + 4 * QB] += _dot_tn(ub[:, q * QB:(q + 1) * QB], a_b[:, lo:lo + 4 * QB])
        dd_ref[...] += _colsum(dy * u)

        @pl.when((b == nb - 1) & (j == nt - 1))
        def _():
            dlr_ref[...] = _colsum(accr[...])
            dli_ref[...] = _colsum(acci[...])

    row = lambda w: pl.BlockSpec((tt, w), lambda b, j: (b * nt + nt - 1 - j, 0))
    seq8 = pl.BlockSpec((8, NST), lambda b, j: (b, 0))
    prev = pl.BlockSpec((8, 2 * NST), lambda b, j: (jnp.maximum((b * nt + nt - 1 - j) * ng - 1, 0), 0))
    const = lambda shape: pl.BlockSpec(shape, lambda b, j: (0, 0))
    return pl.pallas_call(
        body, name="ssm_bwd_b", grid=(nb, nt),
        out_shape=[jax.ShapeDtypeStruct((n, D_SSM), F32), jax.ShapeDtypeStruct((2 * NST, QB), F32),
                   jax.ShapeDtypeStruct((QB, 2 * NST), F32), jax.ShapeDtypeStruct((1, D_SSM), F32),
                   jax.ShapeDtypeStruct((1, NST), F32), jax.ShapeDtypeStruct((1, NST), F32)],
        in_specs=[row(D_SSM), row(D_SSM), row(2 * NST), prev, seq8, seq8, seq8, seq8, _VM, _VM, _VM, _VM, _VM],
        out_specs=[row(D_SSM), const((2 * NST, QB)), const((QB, 2 * NST)), const((1, D_SSM)),
                   const((1, NST)), const((1, NST))],
        scratch_shapes=[pltpu.VMEM((8, NST), F32)] * 4 + [pltpu.VMEM((tt + 8, 2 * NST), F32),
                                                          pltpu.VMEM((tt, 2 * NST), F32)],
        compiler_params=_cp(("arbitrary", "arbitrary")),
    )(dy, u_p, st, st, fcr, fci, air, aii, bm, cm, dvec, lre8, lim8)


def _rope(v, c, s1, s2):
    return v * c + _roll(v, -16) * s1 + _roll(v, 16) * s2


def _rope_t(dv, c, s1, s2):
    return dv * c + _roll(dv * s1, 16) + _roll(dv * s2, -16)


def _mla_fwd(proj, rc, rs1, rs2, gq, gkv, w_uq, w_ukv, tm):
    n = proj.shape[0]

    def body(ql_ref, kvl_ref, kr_ref, c_ref, s1_ref, s2_ref, gq_ref, gkv_ref, wq_ref, wkv_ref,
             q_ref, k_ref, v_ref, qn_ref, kvn_ref):
        c, s1, s2 = c_ref[...], s1_ref[...], s2_ref[...]
        qhat, _ = _rms(ql_ref[...], Q_LORA)
        qn = (qhat * gq_ref[...]).astype(BF16)
        qn_ref[...] = qn
        q = _dot(qn, wq_ref[...])
        q_ref[...] = _rope(q, jnp.tile(c, (1, NH)), jnp.tile(s1, (1, NH)), jnp.tile(s2, (1, NH))).astype(BF16)
        khat, _ = _rms(kvl_ref[...], KV_LORA)
        kvn = (khat * gkv_ref[...]).astype(BF16)
        kvn_ref[...] = kvn
        kv = _dot(kvn, wkv_ref[...])
        kr = _rope(_roll(kr_ref[...], 64), c, s1, s2)
        k_ref[...] = (kv[:, 0:NH * HP] + jnp.tile(kr, (1, NH))).astype(BF16)
        v_ref[...] = kv[:, NH * HP:2 * NH * HP].astype(BF16)

    def wrapped(proj_ref, *rest):
        ql = proj_ref.at[:, D_SSM:D_SSM + Q_LORA]
        kvl = proj_ref.at[:, D_SSM + Q_LORA:D_SSM + Q_LORA + KV_LORA]
        kr = proj_ref.at[:, IN_PAD - HP:IN_PAD]
        body(ql, kvl, kr, *rest)

    row = lambda w: pl.BlockSpec((tm, w), lambda i: (i, 0))
    return pl.pallas_call(
        wrapped, name="mla_fwd", grid=(n // tm,),
        out_shape=[jax.ShapeDtypeStruct((n, NH * HP), BF16)] * 3 +
                  [jax.ShapeDtypeStruct((n, Q_LORA), BF16), jax.ShapeDtypeStruct((n, KV_LORA), BF16)],
        in_specs=[row(IN_PAD), row(HP), row(HP), row(HP), _VM, _VM, _VM, _VM],
        out_specs=[row(NH * HP)] * 3 + [row(Q_LORA), row(KV_LORA)],
        compiler_params=_cp(("parallel",)),
    )(proj, rc, rs1, rs2, gq, gkv, w_uq, w_ukv)


def _mla_bwd(dq, dk, dv, proj, rc, rs1, rs2, gq, gkv, w_uq, w_ukv, tm):
    n = proj.shape[0]

    def body(dq_ref, dk_ref, dv_ref, proj_ref, c_ref, s1_ref, s2_ref, gq_ref, gkv_ref, wq_ref, wkv_ref,
             dmla_ref, dqb_ref, dkvb_ref, acc_ref):
        i = pl.program_id(0)
        c, s1, s2 = c_ref[...], s1_ref[...], s2_ref[...]
        dqu = _rope_t(dq_ref[...], jnp.tile(c, (1, NH)), jnp.tile(s1, (1, NH)), jnp.tile(s2, (1, NH))).astype(BF16)
        dqb_ref[...] = dqu
        dqn = _dot_nt(dqu, wq_ref[...])
        qhat, rq = _rms(proj_ref[:, D_SSM:D_SSM + Q_LORA], Q_LORA)
        dql = _rms_bwd(dqn * gq_ref[...], qhat, rq, Q_LORA)
        dkf = dk_ref[...]
        dkv = jnp.concatenate([dkf, dv_ref[...]], axis=1).astype(BF16)
        dkvb_ref[...] = dkv
        dkvn = _dot_nt(dkv, wkv_ref[...])
        khat, rk = _rms(proj_ref[:, D_SSM + Q_LORA:D_SSM + Q_LORA + KV_LORA], KV_LORA)
        dkvl = _rms_bwd(dkvn * gkv_ref[...], khat, rk, KV_LORA)
        dkr = dkf[:, 0:HP]
        for h in range(1, NH):
            dkr = dkr + dkf[:, h * HP:(h + 1) * HP]
        lane = lax.broadcasted_iota(jnp.int32, dkr.shape, 1)
        dkr = jnp.where((lane >= QK_NOPE) & (lane < QK_NOPE + QK_ROPE), dkr, 0.0)
        dkr = _roll(_rope_t(dkr, c, s1, s2), -64)
        dmla_ref[...] = jnp.concatenate([dql, dkvl, dkr], axis=1)

        @pl.when(i == 0)
        def _():
            acc_ref[...] = jnp.zeros_like(acc_ref)

        acc_ref[0:1, 0:Q_LORA] += _colsum(dqn * qhat)
        acc_ref[1:2, 0:KV_LORA] += _colsum(dkvn * khat)

    row = lambda w: pl.BlockSpec((tm, w), lambda i: (i, 0))
    return pl.pallas_call(
        body, name="mla_bwd", grid=(n // tm,),
        out_shape=[jax.ShapeDtypeStruct((n, IN_PAD - D_SSM), F32), jax.ShapeDtypeStruct((n, NH * HP), BF16),
                   jax.ShapeDtypeStruct((n, 2 * NH * HP), BF16), jax.ShapeDtypeStruct((8, Q_LORA), F32)],
        in_specs=[row(NH * HP)] * 3 + [row(IN_PAD), row(HP), row(HP), row(HP), _VM, _VM, _VM, _VM],
        out_specs=[row(IN_PAD - D_SSM), row(NH * HP), row(2 * NH * HP), pl.BlockSpec((8, Q_LORA), lambda i: (0, 0))],
        compiler_params=_cp(("arbitrary",)),
    )(dq, dk, dv, proj, rc, rs1, rs2, gq, gkv, w_uq, w_ukv)


_SCALE = (QK_NOPE + QK_ROPE) ** -0.5
_LOG2E = 1.4426950408889634
_C2 = _SCALE * _LOG2E


def _attn_fwd(q, k, v, S, tq):
    n = q.shape[0]
    nb, nq = n // S, S // tq

    def body(q_ref, k_ref, v_ref, o_ref, lr_ref):
        qi = pl.program_id(2)
        qv = q_ref[...]

        def tile(j, c, diagonal):
            m, l, acc = c
            off = pl.multiple_of(j * tq, tq)
            s = _dot_nt(qv, k_ref[pl.ds(off, tq), :]) * _C2
            if diagonal:
                rows = lax.broadcasted_iota(jnp.int32, s.shape, 0)
                cols = lax.broadcasted_iota(jnp.int32, s.shape, 1)
                s = jnp.where(cols <= rows, s, NEG)
            mn = jnp.maximum(m, jnp.max(s, axis=1, keepdims=True))
            p = jnp.exp2(s - mn)
            al = jnp.exp2(m - mn)
            l = al * l + jnp.sum(p, axis=1, keepdims=True)
            acc = al * acc + _dot(p.astype(BF16), v_ref[pl.ds(off, tq), :])
            return mn, l, acc

        init = (jnp.full((tq, 1), NEG, F32), jnp.zeros((tq, 1), F32), jnp.zeros((tq, HP), F32))
        c = lax.fori_loop(0, qi, lambda j, c: tile(j, c, False), init)
        m, l, acc = tile(qi, c, True)
        o_ref[...] = acc / l
        lane = lax.broadcasted_iota(jnp.int32, (8, HP), 1)
        lse = jnp.broadcast_to(m + jnp.log(l) * _LOG2E, (tq, HP))
        lr_ref[...] = _rows_of(lse, jnp.where(lane == 0, 1.0, 0.0).astype(BF16))

    qs = pl.BlockSpec((tq, HP), lambda b, h, i: (b * nq + i, h))
    ks = pl.BlockSpec((S, HP), lambda b, h, i: (b, h))
    return pl.pallas_call(
        body, name="attn_fwd", grid=(nb, NH, nq),
        out_shape=[jax.ShapeDtypeStruct((n, NH * HP), F32), jax.ShapeDtypeStruct((nb * NH * 8, S), F32)],
        in_specs=[qs, ks, ks], out_specs=[qs, pl.BlockSpec((8, tq), lambda b, h, i: (b * NH + h, i))],
        compiler_params=_cp(("parallel", "parallel", "arbitrary")),
    )(q, k, v)


def _rows_of(x, pick):
    x1 = x.astype(BF16)
    r1 = x - x1.astype(F32)
    x2 = r1.astype(BF16)
    x3 = (r1 - x2.astype(F32)).astype(BF16)
    return _dot_nt(pick, x1) + _dot_nt(pick, x2) + _dot_nt(pick, x3)


def _attn_bwd(q, k, v, dob, lrow, drow, S, tq):
    n = q.shape[0]
    nb, nq = n // S, S // tq

    def body(q_ref, k_ref, v_ref, do_ref, lr_ref, dr_ref, dq_ref, dk_ref, dv_ref):
        kj = pl.program_id(2)

        @pl.when(kj == 0)
        def _():
            dq_ref[...] = jnp.zeros_like(dq_ref)

        kt = k_ref[...]
        vt = v_ref[...]

        def tile(i, c, diagonal):
            dk, dv = c
            off = pl.multiple_of(i * tq, tq)
            qv = q_ref[pl.ds(off, tq), :]
            dob = do_ref[pl.ds(off, tq), :]
            lr = lr_ref[0:1, pl.ds(off, tq)]
            dr = dr_ref[0:1, pl.ds(off, tq)]
            st = _dot_nt(kt, qv)
            dpt = _dot_nt(vt, dob)
            pt = jnp.exp2(st * _C2 - lr)
            if diagonal:
                keys = lax.broadcasted_iota(jnp.int32, pt.shape, 0)
                qrys = lax.broadcasted_iota(jnp.int32, pt.shape, 1)
                pt = jnp.where(keys <= qrys, pt, 0.0)
            dst = (pt * (dpt * _SCALE - dr)).astype(BF16)
            dq_ref[pl.ds(off, tq), :] += _dot_tn(dst, kt)
            return dk + _dot(dst, qv), dv + _dot(pt.astype(BF16), dob)

        zero = jnp.zeros((tq, HP), F32)
        c = tile(kj, (zero, zero), True)
        dk, dv = lax.fori_loop(kj + 1, nq, lambda i, c: tile(i, c, False), c)
        dk_ref[...] = dk
        dv_ref[...] = dv

    ts = pl.BlockSpec((tq, HP), lambda b, h, i: (b * nq + i, h))
    fs = pl.BlockSpec((S, HP), lambda b, h, i: (b, h))
    rs = pl.BlockSpec((8, S), lambda b, h, i: (b * NH + h, 0))
    return pl.pallas_call(
        body, name="attn_bwd", grid=(nb, NH, nq),
        out_shape=[jax.ShapeDtypeStruct((n, NH * HP), F32)] * 3,
        in_specs=[fs, ts, ts, fs, rs, rs], out_specs=[fs, ts, ts],
        compiler_params=_cp(("parallel", "parallel", "arbitrary")),
    )(q, k, v, dob, lrow, drow)


def _p1_fwd(yssm, oattn, x, modp, gs, ga, w_out, g2, S, tm):
    n = x.shape[0]
    tps = S // tm

    def body(ys_ref, oa_ref, x_ref, mod_ref, gs_ref, ga_ref, w_ref, g2_ref, yn_ref, o_ref, x1_ref, h2_ref):
        yh, _ = _rms(ys_ref[...], D_SSM)
        ah, _ = _rms(oa_ref[...], D_ATTN)
        yn = jnp.concatenate([yh * gs_ref[...], ah * ga_ref[...]], axis=1).astype(BF16)
        yn_ref[...] = yn
        o = _dot(yn, w_ref[...])
        o_ref[...] = o
        x1 = x_ref[...] + mod_ref[0, 2:3, :] * o
        x1_ref[...] = x1
        xh, _ = _rms(x1, D)
        h2_ref[...] = ((xh * g2_ref[...]) * (1.0 + mod_ref[0, 4:5, :]) + mod_ref[0, 3:4, :]).astype(BF16)

    row = lambda w: pl.BlockSpec((tm, w), lambda i: (i, 0))
    return pl.pallas_call(
        body, name="p1_fwd", grid=(n // tm,),
        out_shape=[jax.ShapeDtypeStruct((n, D_SSM + NH * HP), BF16), jax.ShapeDtypeStruct((n, D), F32),
                   jax.ShapeDtypeStruct((n, D), F32), jax.ShapeDtypeStruct((n, D), BF16)],
        in_specs=[row(D_SSM), row(NH * HP), row(D), pl.BlockSpec((1, 8, D), lambda i: (i // tps, 0, 0)),
                  _VM, _VM, _VM, _VM],
        out_specs=[row(D_SSM + NH * HP), row(D), row(D), row(D)],
        compiler_params=_cp(("parallel",)),
    )(yssm, oattn, x, modp, gs, ga, w_out, g2)


def _p2(x1, h2, target, modp, g2, gf, w_ff1, w_ff2, S, tm):
    n = x1.shape[0]
    tps = S // tm
    nb = n // S

    def body(x1_ref, h2_ref, t_ref, mod_ref, g2_ref, gf_ref, w1_ref, w2_ref,
             dx1_ref, r_ref, da_ref, dff_ref, accs_ref, accg_ref):
        i = pl.program_id(0)
        sh2, sc2, gt2 = mod_ref[0, 3:4, :], mod_ref[0, 4:5, :], mod_ref[0, 5:6, :]
        fsh, fsc = mod_ref[0, 6:7, :], mod_ref[0, 7:8, :]
        x1 = x1_ref[...]
        a = _dot(h2_ref[...], w1_ref[...])
        ra = jnp.maximum(a, 0.0)
        rb = (ra * ra).astype(BF16)
        r_ref[...] = rb
        ff = _dot(rb, w2_ref[...])
        x2 = x1 + gt2 * ff
        x2h, rf = _rms(x2, D)
        gf_v = gf_ref[...]
        outn = x2h * gf_v
        err = outn * (1.0 + fsc) + fsh - t_ref[...]
        dout = err * (1.0 / D)
        doutn = dout * (1.0 + fsc)
        dx2 = _rms_bwd(doutn * gf_v, x2h, rf, D)
        dff = (gt2 * dx2).astype(BF16)
        dff_ref[...] = dff
        dr = _dot_nt(dff, w2_ref[...])
        da = (dr * (2.0 * ra)).astype(BF16)
        da_ref[...] = da
        dh2 = _dot_nt(da, w1_ref[...])
        x1h, r2 = _rms(x1, D)
        g2_v = g2_ref[...]
        dn2 = dh2 * (1.0 + sc2)
        dx1_ref[...] = dx2 + _rms_bwd(dn2 * g2_v, x1h, r2, D)

        @pl.when(i % tps == 0)
        def _():
            accs_ref[...] = jnp.zeros_like(accs_ref)

        @pl.when(i == 0)
        def _():
            accg_ref[...] = jnp.zeros_like(accg_ref)

        accs_ref[0, 3:4, :] += _colsum(dh2)
        accs_ref[0, 4:5, :] += _colsum(dh2 * (x1h * g2_v))
        accs_ref[0, 5:6, :] += _colsum(dx2 * ff)
        accs_ref[0, 6:7, :] += _colsum(dout)
        accs_ref[0, 7:8, :] += _colsum(dout * outn)
        accg_ref[0:1, :] += _colsum(dn2 * x1h)
        accg_ref[1:2, :] += _colsum(doutn * x2h)
        accg_ref[2:3, :] += _colsum(err * err) * (0.5 / D)

    row = lambda w: pl.BlockSpec((tm, w), lambda i: (i, 0))
    return pl.pallas_call(
        body, name="p2_mlp_loss", grid=(n // tm,),
        out_shape=[jax.ShapeDtypeStruct((n, D), F32), jax.ShapeDtypeStruct((n, D_FF), BF16),
                   jax.ShapeDtypeStruct((n, D_FF), BF16), jax.ShapeDtypeStruct((n, D), BF16),
                   jax.ShapeDtypeStruct((nb, 8, D), F32), jax.ShapeDtypeStruct((8, D), F32)],
        in_specs=[row(D), row(D), row(D), pl.BlockSpec((1, 8, D), lambda i: (i // tps, 0, 0)), _VM, _VM, _VM, _VM],
        out_specs=[row(D), row(D_FF), row(D_FF), row(D), pl.BlockSpec((1, 8, D), lambda i: (i // tps, 0, 0)),
                   pl.BlockSpec((8, D), lambda i: (0, 0))],
        compiler_params=_cp(("arbitrary",)),
    )(x1, h2, target, modp, g2, gf, w_ff1, w_ff2)


def _p3_bwd(dx1, o, yssm, oattn, modp, gs, ga, w_out, S, tm):
    n = dx1.shape[0]
    tps = S // tm
    nb = n // S

    def body(dx1_ref, o_ref, ys_ref, oa_ref, mod_ref, gs_ref, ga_ref, w_ref,
             do_ref, dys_ref, doa_ref, dr_ref, accs_ref, accg_ref):
        i = pl.program_id(0)
        dx1 = dx1_ref[...]
        dob = (mod_ref[0, 2:3, :] * dx1).astype(BF16)
        do_ref[...] = dob
        dyn = _dot_nt(dob, w_ref[...])
        yh, rs = _rms(ys_ref[...], D_SSM)
        oa = oa_ref[...]
        ah, ra = _rms(oa, D_ATTN)
        d1 = dyn[:, 0:D_SSM]
        d2 = dyn[:, D_SSM:D_SSM + NH * HP]
        dys_ref[...] = _rms_bwd(d1 * gs_ref[...], yh, rs, D_SSM)
        doa = _rms_bwd(d2 * ga_ref[...], ah, ra, D_ATTN)
        doa_ref[...] = doa.astype(BF16)
        prod = doa * oa * _SCALE
        ones = jnp.ones((8, HP), BF16)
        for h in range(NH):
            dr_ref[h * 8:(h + 1) * 8, :] = _rows_of(prod[:, h * HP:(h + 1) * HP], ones)

        @pl.when(i % tps == 0)
        def _():
            accs_ref[...] = jnp.zeros_like(accs_ref)

        @pl.when(i == 0)
        def _():
            accg_ref[...] = jnp.zeros_like(accg_ref)

        accs_ref[0, 2:3, :] += _colsum(dx1 * o_ref[...])
        accg_ref[0:1, 0:D_SSM] += _colsum(d1 * yh)
        accg_ref[1:2, :] += _colsum(d2 * ah)

    row = lambda w: pl.BlockSpec((tm, w), lambda i: (i, 0))
    return pl.pallas_call(
        body, name="p3_bwd", grid=(n // tm,),
        out_shape=[jax.ShapeDtypeStruct((n, D), BF16), jax.ShapeDtypeStruct((n, D_SSM), F32),
                   jax.ShapeDtypeStruct((n, NH * HP), BF16), jax.ShapeDtypeStruct((nb * NH * 8, S), F32),
                   jax.ShapeDtypeStruct((nb, 8, D), F32), jax.ShapeDtypeStruct((8, NH * HP), F32)],
        in_specs=[row(D), row(D), row(D_SSM), row(NH * HP), pl.BlockSpec((1, 8, D), lambda i: (i // tps, 0, 0)),
                  _VM, _VM, _VM],
        out_specs=[row(D), row(D_SSM), row(NH * HP), pl.BlockSpec((NH * 8, tm), lambda i: (i // tps, i % tps)),
                   pl.BlockSpec((1, 8, D), lambda i: (i // tps, 0, 0)), pl.BlockSpec((8, NH * HP), lambda i: (0, 0))],
        compiler_params=_cp(("arbitrary",)),
    )(dx1, o, yssm, oattn, modp, gs, ga, w_out)


def _wgrad(a, b, name, col_slots=0):
    n, k1 = a.shape
    k2 = b.shape[1]
    bn = next((b for b in (1024, 512) if n % b == 0), n)
    bk1 = next((b for b in (1024, 512) if k1 % b == 0), k1)
    bk2 = k2 // col_slots if col_slots else (1024 if (k2 % 1024 == 0) else k2)

    def body(a_ref, b_ref, o_ref):
        @pl.when(pl.program_id(2) == 0)
        def _():
            o_ref[...] = jnp.zeros_like(o_ref)

        o_ref[...] += _dot_tn(a_ref[...], b_ref[...]).reshape(o_ref.shape)

    if col_slots:
        out_shape = jax.ShapeDtypeStruct((col_slots, k1, bk2), F32)
        out_spec = pl.BlockSpec((1, bk1, bk2), lambda i, j, t: (j, i, 0))
    else:
        out_shape = jax.ShapeDtypeStruct((k1, k2), F32)
        out_spec = pl.BlockSpec((bk1, bk2), lambda i, j, t: (i, j))
    return pl.pallas_call(
        body, name=name, grid=(k1 // bk1, k2 // bk2, n // bn),
        out_shape=out_shape,
        in_specs=[pl.BlockSpec((bn, bk1), lambda i, j, t: (t, i)), pl.BlockSpec((bn, bk2), lambda i, j, t: (t, j))],
        out_specs=out_spec,
        compiler_params=_cp(("parallel", "parallel", "arbitrary")),
    )(a, b)


def _row_block(rows):
    if rows <= 256:
        return rows
    return next(b for b in (256, 192, 128, 64, 32, 16, 8) if rows % b == 0)


def _add_half(g, recv, cidx, name):
    _, rows2, w = g.shape
    rows = rows2 // 2
    br = _row_block(rows)
    nblk = rows // br

    def body(c_ref, g_ref, r_ref, o_ref):
        o_ref[...] = (g_ref[...] + r_ref[...]).astype(BF16)

    return pl.pallas_call(
        body, name=name,
        grid_spec=pltpu.PrefetchScalarGridSpec(
            num_scalar_prefetch=1, grid=(4, nblk),
            in_specs=[pl.BlockSpec((1, br, w), lambda s, i, c: (s, c[0] * nblk + i, 0)),
                      pl.BlockSpec((1, br, w), lambda s, i, c: (s, i, 0))],
            out_specs=pl.BlockSpec((1, br, w), lambda s, i, c: (s, i, 0))),
        out_shape=jax.ShapeDtypeStruct((4, rows, w), BF16),
        compiler_params=_cp(("parallel", "parallel")),
    )(cidx, g, recv)


def _add_chips(r, name):
    _, rows, w = r.shape
    br = _row_block(rows)

    def body(r_ref, o_ref):
        f = lambda k: r_ref[k].astype(F32)
        o_ref[...] = ((f(0) + f(1)) + f(2)) + f(3)

    return pl.pallas_call(
        body, name=name, grid=(rows // br,),
        out_shape=jax.ShapeDtypeStruct((rows, w), F32),
        in_specs=[pl.BlockSpec((4, br, w), lambda i: (0, i, 0))],
        out_specs=pl.BlockSpec((br, w), lambda i: (i, 0)),
        compiler_params=_cp(("parallel",)),
    )(r)


def _sum_devices(a):
    def body(a_ref, o_ref):
        acc = a_ref[0:1, :]
        for k in range(1, 8):
            acc = acc + a_ref[k:k + 1, :]
        o_ref[...] = acc

    return pl.pallas_call(
        body, name="small_grad_sum", out_shape=jax.ShapeDtypeStruct((1, a.shape[1]), F32),
        in_specs=[_VM], out_specs=_VM, compiler_params=_cp(),
    )(a)


def _adamw_math(wv, gv, mv, vv):
    m_new = ADAM_B1 * mv + (1.0 - ADAM_B1) * gv
    v_new = ADAM_B2 * vv + (1.0 - ADAM_B2) * (gv * gv)
    m_hat = m_new / (1.0 - ADAM_B1 ** ADAM_STEP)
    v_hat = v_new / (1.0 - ADAM_B2 ** ADAM_STEP)
    return -ADAM_LR * (m_hat / (jnp.sqrt(v_hat) + ADAM_EPS) + ADAM_WD * wv), m_new, v_new


def _adamw_small(ws, gs, ms, vs):
    k = len(ws)

    def body(*refs):
        ins, outs = refs[:4 * k], refs[4 * k:]
        for t in range(k):
            d, m_new, v_new = _adamw_math(ins[t][...], ins[k + t][...], ins[2 * k + t][...], ins[3 * k + t][...])
            outs[t][...] = d
            outs[k + t][...] = m_new
            outs[2 * k + t][...] = v_new

    shapes = [jax.ShapeDtypeStruct(w.shape, F32) for w in ws]
    return pl.pallas_call(
        body, name="adamw_small", out_shape=shapes * 3,
        in_specs=[_VM] * (4 * k), out_specs=[_VM] * (3 * k), compiler_params=_cp(),
    )(*ws, *gs, *ms, *vs)


def _adamw(w, g, m, v, name):
    rows, wd = w.shape
    br = _row_block(rows)

    def body(w_ref, g_ref, m_ref, v_ref, d_ref, nm_ref, nv_ref):
        d, m_new, v_new = _adamw_math(w_ref[...], g_ref[...], m_ref[...], v_ref[...])
        d_ref[...] = d
        nm_ref[...] = m_new
        nv_ref[...] = v_new

    spec = pl.BlockSpec((br, wd), lambda i: (i, 0))
    return pl.pallas_call(
        body, name=name, grid=(rows // br,),
        out_shape=[jax.ShapeDtypeStruct((rows, wd), F32)] * 3,
        in_specs=[spec] * 4, out_specs=[spec] * 3,
        compiler_params=_cp(("parallel",)),
    )(w, g, m, v)


def _adamw_halves(w, mine, other, m, v, cidx, name):
    rows, wd = w.shape
    h = rows // 2
    br = _row_block(h)
    nblk = h // br

    def body(c_ref, w_ref, a_ref, b_ref, m_ref, v_ref, g_ref, d_ref, nm_ref, nv_ref):
        gv = jnp.where(pl.program_id(0) == c_ref[0], a_ref[...], b_ref[...])
        d, m_new, v_new = _adamw_math(w_ref[...], gv, m_ref[...], v_ref[...])
        g_ref[...] = gv
        d_ref[...] = d
        nm_ref[...] = m_new
        nv_ref[...] = v_new

    full = pl.BlockSpec((br, wd), lambda hf, i, c: (hf * nblk + i, 0))
    half = pl.BlockSpec((br, wd), lambda hf, i, c: (i, 0))
    return pl.pallas_call(
        body, name=name,
        grid_spec=pltpu.PrefetchScalarGridSpec(
            num_scalar_prefetch=1, grid=(2, nblk),
            in_specs=[full, half, half, full, full], out_specs=[full] * 4),
        out_shape=[jax.ShapeDtypeStruct((rows, wd), F32)] * 4,
        compiler_params=_cp(("parallel", "parallel")),
    )(cidx, w, mine, other, m, v)


def _other_chips(x, y):
    return [(1 - x, y), (x, 1 - y), (1 - x, 1 - y)]


def _other_devices(x, y, c):
    flip = lambda v, d: (1 - v) if d else v
    return [(flip(x, dx), flip(y, dy), flip(c, dc))
            for dx in (0, 1) for dy in (0, 1) for dc in (0, 1) if (dx, dy, dc) != (0, 0, 0)]


def _exchange(name, ins, out_shapes, n_local, n_remote, plan):
    ni, no = len(ins), len(out_shapes)

    def body(*refs):
        in_refs, out_refs = refs[:ni], refs[ni:ni + no]
        send_sems, recv_sems, local_sems = refs[ni + no:]
        x, y, c = lax.axis_index("x"), lax.axis_index("y"), lax.axis_index("c")
        local, remote = plan(in_refs, out_refs, x, y, c)
        assert len(local) == n_local and len(remote) == n_remote

        def push(k, src, dst, dev):
            return pltpu.make_async_remote_copy(src_ref=src, dst_ref=dst, send_sem=send_sems.at[k],
                                                recv_sem=recv_sems.at[k], device_id=dev, device_id_type=MESH)

        own = [pltpu.make_async_copy(s, d, local_sems.at[i]) for i, (s, d) in enumerate(local)]
        for cp in own:
            cp.start()
        sends = [push(k, s, d, dev) for k, (s, d, dev, _) in enumerate(remote)]
        for cp in sends:
            cp.start()
        for k, (s, _, dev, landing) in enumerate(remote):
            push(k, s, landing, dev).wait_recv()
        for cp in sends:
            cp.wait_send()
        for cp in own:
            cp.wait()

    return pl.pallas_call(
        body, name=name, out_shape=out_shapes,
        in_specs=[_ANY] * ni, out_specs=[_ANY] * no,
        scratch_shapes=[pltpu.SemaphoreType.DMA((n_remote,)), pltpu.SemaphoreType.DMA((n_remote,)),
                        pltpu.SemaphoreType.DMA((max(n_local, 1),))],
        compiler_params=pltpu.CompilerParams(has_side_effects=True),
    )(*ins)


def _gather_chips(name, shards, everyone=()):
    ns, ne = len(shards), len(everyone)
    outs = [jax.ShapeDtypeStruct((4,) + a.shape, a.dtype) for a in shards]
    outs += [jax.ShapeDtypeStruct((8,) + a.shape, a.dtype) for a in everyone]

    def plan(i, o, x, y, c):
        mine, me = 2 * x + y, 4 * x + 2 * y + c
        local, remote = [], []
        for t in range(ns):
            local.append((i[t], o[t].at[mine]))
            for px, py in _other_chips(x, y):
                remote.append((i[t], o[t].at[mine], (px, py, c), o[t].at[2 * px + py]))
        for t in range(ns, ns + ne):
            local.append((i[t], o[t].at[me]))
            for px, py, pc in _other_devices(x, y, c):
                remote.append((i[t], o[t].at[me], (px, py, pc), o[t].at[4 * px + 2 * py + pc]))
        return local, remote

    return _exchange(name, list(shards) + list(everyone), outs, ns + ne, 3 * ns + 7 * ne, plan)


_HBM = pl.BlockSpec(memory_space=pltpu.HBM)
_SEM = pl.BlockSpec(memory_space=pltpu.SEMAPHORE)
_EFFECT = pltpu.SideEffectType.DATAFLOW_SIDE_EFFECTING


def _gather_start(shards):
    ns = len(shards)
    lands = [pltpu.with_memory_space_constraint(lax.empty((4,) + a.shape, a.dtype), pltpu.HBM) for a in shards]
    srcs = [pltpu.with_memory_space_constraint(a, pltpu.HBM) for a in shards]

    def body(*refs):
        src, land = refs[:ns], refs[ns:2 * ns]
        send, recv = refs[2 * ns:5 * ns], refs[5 * ns:8 * ns]
        token = refs[10 * ns]
        x, y, c = lax.axis_index("x"), lax.axis_index("y"), lax.axis_index("c")
        mine = 2 * x + y
        for t in range(ns):
            for j, (px, py) in enumerate(_other_chips(x, y)):
                pltpu.make_async_remote_copy(src_ref=src[t], dst_ref=land[t].at[mine], send_sem=send[3 * t + j],
                                             recv_sem=recv[3 * t + j], device_id=(px, py, c),
                                             device_id_type=MESH).start()
        token[...] = jnp.zeros_like(token)

    out = pl.pallas_call(
        body, name="gather_late_start",
        out_shape=[pltpu.SemaphoreType.DMA(())] * (6 * ns)
                  + [pltpu.HBM(a.shape, a.dtype) for a in shards] + [pltpu.HBM((4,) + a.shape, a.dtype) for a in shards]
                  + [jax.ShapeDtypeStruct((8, 128), F32)],
        in_specs=[_HBM] * (2 * ns), out_specs=[_SEM] * (6 * ns) + [_HBM] * (2 * ns) + [_VM],
        input_output_aliases={t: 6 * ns + t for t in range(2 * ns)},
        compiler_params=pltpu.CompilerParams(has_side_effects=_EFFECT),
    )(*srcs, *lands)
    return out[:6 * ns], out[6 * ns:7 * ns], out[7 * ns:8 * ns], out[8 * ns]


def _gather_wait(sems, srcs, lands, after):
    ns = len(srcs)

    def body(*refs):
        src, land = refs[:ns], refs[ns:2 * ns]
        send, recv = refs[2 * ns:5 * ns], refs[5 * ns:8 * ns]
        x, y, c = lax.axis_index("x"), lax.axis_index("y"), lax.axis_index("c")
        for t in range(ns):
            for j, (px, py) in enumerate(_other_chips(x, y)):
                cp = pltpu.make_async_remote_copy(src_ref=src[t], dst_ref=land[t].at[2 * px + py],
                                                  send_sem=send[3 * t + j], recv_sem=recv[3 * t + j],
                                                  device_id=(px, py, c), device_id_type=MESH)
                cp.wait_send()
                cp.wait_recv()

    out = pl.pallas_call(
        body, name="gather_late_wait",
        out_shape=[pltpu.HBM(a.shape, a.dtype) for a in srcs] + [pltpu.HBM(a.shape, a.dtype) for a in lands],
        in_specs=[_HBM] * (2 * ns) + [_SEM] * (6 * ns) + [_ANY], out_specs=[_HBM] * (2 * ns),
        input_output_aliases={t: t for t in range(2 * ns)},
        compiler_params=pltpu.CompilerParams(has_side_effects=_EFFECT),
    )(*srcs, *lands, *sems, after)
    return out[ns:]


def _swap_halves(gs, everyone):
    ns, ne = len(gs), len(everyone)
    outs = [jax.ShapeDtypeStruct((4, g.shape[1] // 2, g.shape[2]), g.dtype) for g in gs]
    outs += [jax.ShapeDtypeStruct((8,) + a.shape, a.dtype) for a in everyone]

    def plan(i, o, x, y, c):
        me = 4 * x + 2 * y + c
        local, remote = [], []
        for t in range(ns):
            h = gs[t].shape[1] // 2
            theirs = i[t].at[:, pl.ds(pl.multiple_of((1 - c) * h, 8), h), :]
            remote.append((theirs, o[t], (x, y, 1 - c), o[t]))
        for t in range(ns, ns + ne):
            local.append((i[t], o[t].at[me]))
            for px, py, pc in _other_devices(x, y, c):
                remote.append((i[t], o[t].at[me], (px, py, pc), o[t].at[4 * px + 2 * py + pc]))
        return local, remote

    return _exchange("grad_swap_sibling", list(gs) + list(everyone), outs, ne, ns + 7 * ne, plan)


def _scatter_chips(parts):
    ns = len(parts)
    outs = [jax.ShapeDtypeStruct(a.shape, a.dtype) for a in parts]

    def plan(i, o, x, y, c):
        mine = 2 * x + y
        local, remote = [], []
        for t in range(ns):
            local.append((i[t].at[mine], o[t].at[mine]))
            for px, py in _other_chips(x, y):
                remote.append((i[t].at[2 * px + py], o[t].at[mine], (px, py, c), o[t].at[2 * px + py]))
        return local, remote

    return _exchange("grad_scatter_chips", list(parts), outs, ns, 3 * ns, plan)


def _join_halves(halves):
    ns = len(halves)
    outs = [jax.ShapeDtypeStruct(a.shape, a.dtype) for a in halves]

    def plan(i, o, x, y, c):
        return [], [(i[t], o[t], (x, y, 1 - c), o[t]) for t in range(ns)]

    return _exchange("grad_join_sibling", list(halves), outs, 0, ns, plan)


def _pad_heads_cols(w, per, used):
    k = w.shape[0]
    w = w.reshape(k, NH, per)[:, :, :used]
    return jnp.pad(w, ((0, 0), (0, 0), (0, HP - used))).reshape(k, NH * HP)


def _unpad_heads_cols(w, used):
    k = w.shape[0]
    return w.reshape(k, NH, HP)[:, :, :used]


def _prep_weights(wf):
    bf = lambda a: a.astype(BF16)
    out = {}
    out["w_in"] = jnp.pad(bf(wf["w_in"]), ((0, 0), (0, IN_PAD - IN_COLS)))
    out["w_glu"] = bf(wf["w_glu"])
    out["w_uq"] = _pad_heads_cols(bf(wf["w_uq"]), QK_NOPE + QK_ROPE, QK_NOPE + QK_ROPE)
    wkv = bf(wf["w_ukv"]).reshape(KV_LORA, NH, QK_NOPE + V_HEAD)
    wk = jnp.pad(wkv[:, :, :QK_NOPE], ((0, 0), (0, 0), (0, HP - QK_NOPE))).reshape(KV_LORA, NH * HP)
    wv = jnp.pad(wkv[:, :, QK_NOPE:], ((0, 0), (0, 0), (0, HP - V_HEAD))).reshape(KV_LORA, NH * HP)
    out["w_ukv"] = jnp.concatenate([wk, wv], axis=1)
    return out


def _prep_late_weights(wf):
    bf = lambda a: a.astype(BF16)
    out = {}
    wo = bf(wf["w_out"])
    wo_a = jnp.pad(wo[D_SSM:].reshape(NH, V_HEAD, D), ((0, 0), (0, HP - V_HEAD), (0, 0))).reshape(NH * HP, D)
    out["w_out"] = jnp.concatenate([wo[:D_SSM], wo_a], axis=0)
    out["w_ff1"] = bf(wf["w_ff1"])
    out["w_ff2"] = bf(wf["w_ff2"])
    return out


def _rope_tables(positions):
    inv_freq = ROPE_BASE ** (-jnp.arange(0, QK_ROPE, 2, dtype=F32) / QK_ROPE)
    ang = positions.astype(F32)[:, None] * inv_freq
    cos, sin = jnp.cos(ang), jnp.sin(ang)
    n = positions.shape[0]
    one = jnp.ones((n, QK_NOPE), F32)
    z16 = jnp.zeros((n, 16), F32)
    z32 = jnp.zeros((n, 32), F32)
    z64 = jnp.zeros((n, QK_NOPE), F32)
    rc = jnp.concatenate([one, cos, cos, z32], axis=1)
    rs1 = jnp.concatenate([z64, -sin, z16, z32], axis=1)
    rs2 = jnp.concatenate([z64, z16, sin, z32], axis=1)
    return rc, rs1, rs2


def _permute_rows(a, S):
    n, w = a.shape
    return a.reshape(n // S, 8, S // 8, w).transpose(0, 2, 1, 3).reshape(n, w)


def _unpermute_rows(a, S):
    n, w = a.shape
    return a.reshape(n // S, S // 8, 8, w).transpose(0, 2, 1, 3).reshape(n, w)


def _block_diag_in(bb):
    eye = jnp.eye(G, dtype=bb.dtype)
    return jnp.einsum("gph,gk->ghkp", bb, eye).reshape(G * H, G * P)


def _block_diag_out(cc):
    eye = jnp.eye(G, dtype=cc.dtype)
    return jnp.einsum("ghp,gk->gpkh", cc, eye).reshape(G * P, G * H)


def _slots(full):
    r, cdim = full.shape
    return full.reshape(r, 4, cdim // 4).transpose(1, 0, 2)


def _unslots(g):
    s, r, cs = g.shape
    return g.transpose(1, 0, 2).reshape(r, s * cs)


def _local_step(x, positions, target, modp, wf, late_weights=None):
    nb, S, _ = x.shape
    n = nb * S
    tm = min(256, S)
    tt = min(256, S)
    tq = min(512, S // 2)
    kw = _prep_weights(wf)
    row = lambda a: a.reshape(1, -1).astype(F32)

    xf = x.reshape(n, D)
    tf = target.reshape(n, D)
    g1, g2, gf = row(wf["norm1_g"]), row(wf["norm2_g"]), row(wf["final_norm_g"])
    h1, proj = _f1_fwd(xf, modp, g1, kw["w_in"], S, tm)

    col = lambda a: a.reshape(NST, 1)
    lam_re, lam_im = col(wf["ssm_lambda_re"]), col(wf["ssm_lambda_im"])
    logdt = jnp.repeat(wf["ssm_log_dt"].reshape(G, 1), P, axis=1).reshape(NST, 1)
    b_re, b_im = wf["ssm_b_re"].reshape(NST, H), wf["ssm_b_im"].reshape(NST, H)
    lbr, lbi, bbr, bbi = _ssm_param_fwd(lam_re, lam_im, logdt, b_re, b_im)
    lre8 = jnp.broadcast_to(lbr.reshape(1, NST), (8, NST))
    lim8 = jnp.broadcast_to(lbi.reshape(1, NST), (8, NST))
    bm = jnp.concatenate([_block_diag_in(bbr.reshape(G, P, H)), _block_diag_in(bbi.reshape(G, P, H))],
                         axis=1).astype(BF16)
    cm = jnp.concatenate([_block_diag_out(wf["ssm_c_re"]), -_block_diag_out(wf["ssm_c_im"])], axis=0).astype(BF16)
    dvec = row(wf["ssm_d"])
    u_p = _permute_rows(proj[:, :D_SSM], S)
    fcr, fci = _ssm_local(u_p, bm, lre8, lim8, S, tt)
    st, ypre, z, gact, yssm_p = _ssm_fwd(u_p, fcr, fci, bm, cm, dvec, kw["w_glu"], lre8, lim8, S, tt)
    yssm = _unpermute_rows(yssm_p, S)

    rc, rs1, rs2 = _rope_tables(positions.reshape(n))
    gq, gkv = row(wf["q_norm_g"]), row(wf["kv_norm_g"])
    q, k, v, qn, kvn = _mla_fwd(proj, rc, rs1, rs2, gq, gkv, kw["w_uq"], kw["w_ukv"], tm)
    oattn, lrow = _attn_fwd(q, k, v, S, tq)

    gs = row(wf["ssm_out_g"])
    ga = jnp.pad(wf["attn_out_g"].reshape(NH, V_HEAD), ((0, 0), (0, HP - V_HEAD))).reshape(1, NH * HP)
    kw.update(_prep_late_weights(late_weights(oattn) if late_weights is not None else wf))
    yn, o, x1, h2 = _p1_fwd(yssm, oattn, xf, modp, gs, ga, kw["w_out"], g2, S, tm)
    dx1, r, da, dff, accs2, accg2 = _p2(x1, h2, tf, modp, g2, gf, kw["w_ff1"], kw["w_ff2"], S, tm)
    loss = jnp.sum(accg2[2])
    do, dyssm, dob, drow, accs3, accg3 = _p3_bwd(dx1, o, yssm, oattn, modp, gs, ga, kw["w_out"], S, tm)

    dq, dk, dv = _attn_bwd(q, k, v, dob, lrow, drow, S, tq)
    dmla, dqb, dkvb, accm = _mla_bwd(dq, dk, dv, proj, rc, rs1, rs2, gq, gkv, kw["w_uq"], kw["w_ukv"], tm)

    dys_p = _permute_rows(dyssm, S)
    dy, dz, air, aii = _ssm_bwd_a(dys_p, z, ypre, kw["w_glu"], cm, lre8, lim8, S, tt)
    du_p, dcm, dbm, dd, dlr, dli = _ssm_bwd_b(dy, u_p, st, fcr, fci, air, aii, bm, cm, dvec, lre8, lim8, S, tt)
    du = _unpermute_rows(du_p, S)
    dcm = dcm.reshape(2, 4, 8, P, 8, H)
    dc_re = jnp.einsum("qgpgh->qghp", dcm[0]).reshape(G, H, P)
    dc_im = -jnp.einsum("qgpgh->qghp", dcm[1]).reshape(G, H, P)
    dbm = dbm.reshape(8, H, 2, 4, 8, P)
    dbb_re = jnp.einsum("ghqgp->qgph", dbm[:, :, 0]).reshape(NST, H)
    dbb_im = jnp.einsum("ghqgp->qgph", dbm[:, :, 1]).reshape(NST, H)
    gb_re, gb_im, glr, gli, gdt = _ssm_param_bwd(lam_re, lam_im, logdt, b_re, b_im, dlr.reshape(NST, 1),
                                                 dli.reshape(NST, 1), dbb_re, dbb_im)
    glogdt = _rowsum(gdt.reshape(G, P))

    dx, dproj, accs1, accg1 = _f1_bwd(du, dmla, dx1, xf, modp, g1, kw["w_in"], S, tm)

    big = {}
    big["w_in"] = _slots(_wgrad(h1, dproj, "wgrad_in")[:, :IN_COLS])
    big["w_glu"] = _wgrad(gact, dz, "wgrad_glu", col_slots=4)
    big["w_uq"] = _slots(_unpad_heads_cols(_wgrad(qn, dqb, "wgrad_uq"), QK_NOPE + QK_ROPE).reshape(Q_LORA, -1))
    gkvw = _wgrad(kvn, dkvb, "wgrad_ukv")
    big["w_ukv"] = _slots(jnp.concatenate([_unpad_heads_cols(gkvw[:, :NH * HP], QK_NOPE),
                                           _unpad_heads_cols(gkvw[:, NH * HP:], V_HEAD)], axis=2).reshape(KV_LORA, -1))
    gwo = _wgrad(yn, do, "wgrad_out")
    big["w_out"] = jnp.concatenate([gwo[:D_SSM].reshape(2, D_SSM // 2, D),
                                    gwo[D_SSM:].reshape(2, NH // 2 * HP, D).reshape(2, NH // 2, HP, D)[:, :, :V_HEAD]
                                    .reshape(2, D_ATTN // 2, D)], axis=0)
    big["w_ff1"] = _wgrad(h2, da, "wgrad_ff1", col_slots=4)
    big["w_ff2"] = _wgrad(r, dff, "wgrad_ff2").reshape(4, D_FF // 4, D)

    small = {}
    small["norm1_g"] = accg1[0:1]
    small["norm2_g"] = accg2[0:1]
    small["final_norm_g"] = accg2[1:2]
    small["ssm_out_g"] = accg3[0:1, :D_SSM]
    small["attn_out_g"] = accg3[1].reshape(NH, HP)[:, :V_HEAD].reshape(1, D_ATTN)
    small["q_norm_g"] = accm[0:1, :Q_LORA]
    small["kv_norm_g"] = accm[1:2, :KV_LORA]
    small["ssm_lambda_re"] = glr.reshape(G, P)
    small["ssm_lambda_im"] = gli.reshape(G, P)
    small["ssm_b_re"] = gb_re
    small["ssm_b_im"] = gb_im
    small["ssm_c_re"] = dc_re.reshape(G * H, P)
    small["ssm_c_im"] = dc_im.reshape(G * H, P)
    small["ssm_d"] = dd.reshape(G, H)
    small["ssm_log_dt"] = glogdt.reshape(1, G)
    return loss, dx.reshape(nb, S, D), big, small, accs1 + accs2 + accs3


def _view2d(a):
    return a.reshape(-1, a.shape[-1]) if a.ndim > 1 else a.reshape(1, -1)


def kernel(x, c, positions, ada_w, ada_b, norm1_g, w_in, ssm_lambda_re, ssm_lambda_im, ssm_b_re, ssm_b_im, ssm_c_re, ssm_c_im, ssm_d, ssm_log_dt, w_glu, q_norm_g, w_uq, kv_norm_g, w_ukv, ssm_out_g, attn_out_g, w_out, norm2_g, w_ff1, w_ff2, final_ada_w, final_ada_b, final_norm_g, loss_target, m_ada_w, m_ada_b, m_norm1_g, m_w_in, m_ssm_lambda_re, m_ssm_lambda_im, m_ssm_b_re, m_ssm_b_im, m_ssm_c_re, m_ssm_c_im, m_ssm_d, m_ssm_log_dt, m_w_glu, m_q_norm_g, m_w_uq, m_kv_norm_g, m_w_ukv, m_ssm_out_g, m_attn_out_g, m_w_out, m_norm2_g, m_w_ff1, m_w_ff2, m_final_ada_w, m_final_ada_b, m_final_norm_g, v_ada_w, v_ada_b, v_norm1_g, v_w_in, v_ssm_lambda_re, v_ssm_lambda_im, v_ssm_b_re, v_ssm_b_im, v_ssm_c_re, v_ssm_c_im, v_ssm_d, v_ssm_log_dt, v_w_glu, v_q_norm_g, v_w_uq, v_kv_norm_g, v_w_ukv, v_ssm_out_g, v_attn_out_g, v_w_out, v_norm2_g, v_w_ff1, v_w_ff2, v_final_ada_w, v_final_ada_b, v_final_norm_g):
    args = dict(locals())
    names = list(inspect.signature(kernel).parameters)
    wnames = names[3:names.index("loss_target")]
    small_names = [nm for nm in wnames if nm not in GATHERED and nm not in TP]
    reduced_names = [nm for nm in small_names if nm not in ("ada_b", "final_ada_b")]
    w = {nm: args[nm] for nm in wnames}
    m = {nm: args["m_" + nm] for nm in wnames}
    v = {nm: args["v_" + nm] for nm in wnames}
    nb = x.shape[0]
    xi, yi, ci = lax.axis_index("x"), lax.axis_index("y"), lax.axis_index("c")
    chip, me = 2 * xi + yi, 4 * xi + 2 * yi + ci

    unslot = lambda nm, g: g.reshape(-1, g.shape[-1]) if nm in ROW_SHARDED else _unslots(g)
    own_late = [_view2d(w[nm]).astype(BF16) for nm in LATE]
    sems, srcs, lands, token = _gather_start(own_late)

    def late_weights(after):
        landed = _gather_wait(sems, srcs, lands, after)
        return {nm: unslot(nm, lax.dynamic_update_slice(g, own[None], (chip, 0, 0)))
                for nm, g, own in zip(LATE, landed, own_late)}

    early = [nm for nm in GATHERED if nm not in LATE]
    got = _gather_chips("gather_weights", [_view2d(w[nm]).astype(BF16) for nm in early], [c + token[0, 0]])
    wf = {nm: unslot(nm, g) for nm, g in zip(early, got)}
    for nm in small_names:
        wf[nm] = w[nm][0] if w[nm].ndim > 1 else w[nm]
    c_all = got[len(early)].reshape(8 * nb, D)

    na, nf = ada_w.shape[-1], final_ada_w.shape[-1]
    ada_b_s = lax.dynamic_slice(ada_b, (0, chip * na), (1, na))
    fada_b_s = lax.dynamic_slice(final_ada_b.reshape(1, -1), (0, chip * nf), (1, nf))
    cond_all, modcols = _mod_fwd(c_all, ada_w[0], ada_b_s, final_ada_w, fada_b_s)
    (mod_g,) = _gather_chips("gather_mod", [modcols])
    mine = lax.dynamic_slice(mod_g, (0, me * nb, 0), (4, nb, na + nf))
    modp = jnp.concatenate([mine[:, :, :na].transpose(1, 0, 2).reshape(nb, 6, D),
                            mine[:, :, na:].transpose(1, 0, 2).reshape(nb, 2, D)], axis=1)

    loss, grad_x, big, small, dmodp = _local_step(x, positions, loss_target, modp, wf, late_weights)
    loss = lax.psum(loss, ("x", "y", "c"))

    sizes = [small[nm].size for nm in reduced_names]
    pad = -sum(sizes) % 128
    packed = jnp.concatenate([small[nm].reshape(1, -1) for nm in reduced_names] + [jnp.zeros((1, pad), F32)], axis=1)
    swapped = _swap_halves([big[nm] for nm in GATHERED], [dmodp.reshape(nb, 8 * D), packed])
    cidx = ci.astype(jnp.int32).reshape(1)
    chip_sums = [_add_half(big[nm], r, cidx, "grad_add_sibling_" + nm) for nm, r in zip(GATHERED, swapped)]
    halves = [_add_chips(r, "grad_add_chips_" + nm) for nm, r in zip(GATHERED, _scatter_chips(chip_sums))]
    others = _join_halves(halves)
    grads = {}
    dmod_all = swapped[len(GATHERED)].reshape(8 * nb, 8 * D)
    small_sum = _sum_devices(swapped[len(GATHERED) + 1].reshape(8, -1))
    off = 0
    for nm, sz in zip(reduced_names, sizes):
        grads[nm] = small_sum[:, off:off + sz].reshape(small[nm].shape)
        off += sz

    dsl = jnp.concatenate([lax.dynamic_slice(dmod_all, (0, chip * na), (8 * nb, na)),
                           lax.dynamic_slice(dmod_all, (0, 6 * D + chip * nf), (8 * nb, nf))], axis=1)
    gw, gb = _mod_bwd(cond_all.T, dsl, dmod_all)
    grads["ada_w"], grads["final_ada_w"] = gw[:, :na], gw[:, na:]
    grads["ada_b"], grads["final_ada_b"] = gb[:, :6 * D], gb[:, 6 * D:]

    delta, new_m, new_v = {}, {}, {}
    for nm, mine_h, other_h in zip(GATHERED, halves, others):
        grads[nm], delta[nm], new_m[nm], new_v[nm] = _adamw_halves(
            _view2d(w[nm]), mine_h, other_h, _view2d(m[nm]), _view2d(v[nm]), cidx, "adamw_" + nm)
    for nm in TP:
        delta[nm], new_m[nm], new_v[nm] = _adamw(_view2d(w[nm]), grads[nm], _view2d(m[nm]), _view2d(v[nm]),
                                                  "adamw_" + nm)
    upd = _adamw_small([_view2d(w[nm]) for nm in small_names], [grads[nm] for nm in small_names],
                       [_view2d(m[nm]) for nm in small_names], [_view2d(v[nm]) for nm in small_names])
    k = len(small_names)
    for t, nm in enumerate(small_names):
        delta[nm], new_m[nm], new_v[nm] = upd[t], upd[k + t], upd[2 * k + t]

    outs = [grads, delta, new_m, new_v]
    return (loss, grad_x, *[d[nm].reshape(w[nm].shape) for d in outs for nm in wnames])
```

```python
import functools
import inspect
import math

import jax
import jax.numpy as jnp
from jax import lax
from jax.experimental import pallas as pl
from jax.experimental.pallas import tpu as pltpu

F32 = jnp.float32
BF16 = jnp.bfloat16

D = 1024
D_SSM = 512
G = 32
H = 16
P = 64
NST = G * P
D_ATTN = 512
NH = 8
QK_NOPE = 64
QK_ROPE = 32
V_HEAD = 64
HP = 128
Q_LORA = 384
KV_LORA = 256
IN_COLS = D_SSM + Q_LORA + KV_LORA + QK_ROPE
IN_PAD = 1280
D_FF = 4096
ROPE_BASE = 10000.0
EPS = 1e-6
ADAM_LR = 0.001
ADAM_B1 = 0.9
ADAM_B2 = 0.999
ADAM_EPS = 1e-08
ADAM_WD = 0.01
ADAM_STEP = 10
NEG = -1e30
VMEM_LIMIT = 60 << 20

MESH = pl.DeviceIdType.MESH
_VM = pl.BlockSpec(memory_space=pltpu.VMEM)
_ANY = pl.BlockSpec(memory_space=pl.ANY)

GATHERED = ["w_in", "w_glu", "w_uq", "w_ukv", "w_out", "w_ff1", "w_ff2"]
TP = ["ada_w", "final_ada_w"]
ROW_SHARDED = ("w_out", "w_ff2")
LATE = ["w_out", "w_ff1", "w_ff2"]


def _cp(sem=None, vmem=VMEM_LIMIT):
    kw = dict(vmem_limit_bytes=vmem)
    if sem is not None:
        kw["dimension_semantics"] = sem
    return pltpu.CompilerParams(**kw)


def _dot(a, b):
    return jnp.dot(a, b, preferred_element_type=F32)


def _dot_nt(a, b):
    return lax.dot_general(a, b, (((1,), (1,)), ((), ())), preferred_element_type=F32)


def _dot_tn(a, b):
    return lax.dot_general(a, b, (((0,), (0,)), ((), ())), preferred_element_type=F32)


def _rms(x, n):
    r = lax.rsqrt(jnp.sum(x * x, axis=-1, keepdims=True) * (1.0 / n) + EPS)
    return x * r, r


def _rms_bwd(dyg, xhat, r, n):
    return r * (dyg - xhat * (jnp.sum(dyg * xhat, axis=-1, keepdims=True) * (1.0 / n)))


def _sigmoid(x):
    return 1.0 / (1.0 + jnp.exp(-x))


_GK = math.sqrt(2.0 / math.pi)
_GC = 0.044715


def _gelu(y):
    t = jnp.tanh(_GK * (y + _GC * y * y * y))
    return 0.5 * y * (1.0 + t)


def _gelu_grad(y):
    t = jnp.tanh(_GK * (y + _GC * y * y * y))
    return 0.5 * (1.0 + t) + 0.5 * y * (1.0 - t * t) * _GK * (1.0 + 3.0 * _GC * y * y)


def _colsum(x):
    return jnp.sum(x, axis=0, keepdims=True)


def _roll(x, s):
    return pltpu.roll(x, s % x.shape[-1], x.ndim - 1)


def _mod_fwd(c_all, ada_w_s, ada_b_s, fada_w_s, fada_b_s):
    nseq = c_all.shape[0]
    na, nf = ada_w_s.shape[1], fada_w_s.shape[1]

    def body(c_ref, w_ref, b_ref, fw_ref, fb_ref, cond_ref, mod_ref):
        cv = c_ref[...]
        cond = cv * _sigmoid(cv)
        cond_ref[...] = cond
        cb = cond.astype(BF16)
        mod_ref[:, 0:na] = _dot(cb, w_ref[...].astype(BF16)) + b_ref[...]
        mod_ref[:, na:na + nf] = _dot(cb, fw_ref[...].astype(BF16)) + fb_ref[...]

    return pl.pallas_call(
        body, name="mod_fwd",
        out_shape=[jax.ShapeDtypeStruct((nseq, D), F32), jax.ShapeDtypeStruct((nseq, na + nf), F32)],
        in_specs=[_VM] * 5, out_specs=[_VM] * 2, compiler_params=_cp(),
    )(c_all, ada_w_s, ada_b_s, fada_w_s, fada_b_s)


def _mod_bwd(cond_t, dsl, dall):
    nseq, n = dsl.shape
    bc = 512

    def body(ct_ref, dm_ref, da_ref, gw_ref, gb_ref):
        ct = ct_ref[...]
        dm = dm_ref[...]
        acc = ct[:, 0:1] * dm[0:1, :]
        for b in range(1, nseq):
            acc = acc + ct[:, b:b + 1] * dm[b:b + 1, :]
        gw_ref[...] = acc

        @pl.when(pl.program_id(0) == 0)
        def _():
            gb_ref[...] = _colsum(da_ref[...])

    return pl.pallas_call(
        body, name="mod_bwd", grid=(n // bc,),
        out_shape=[jax.ShapeDtypeStruct((D, n), F32), jax.ShapeDtypeStruct((1, dall.shape[1]), F32)],
        in_specs=[_VM, pl.BlockSpec((nseq, bc), lambda i: (0, i)), _VM],
        out_specs=[pl.BlockSpec((D, bc), lambda i: (0, i)), pl.BlockSpec((1, dall.shape[1]), lambda i: (0, 0))],
        compiler_params=_cp(("arbitrary",)),
    )(cond_t, dsl, dall)


def _f1_fwd(x, modp, g1, w_in, S, tm):
    n = x.shape[0]
    tps = S // tm

    def body(x_ref, mod_ref, g_ref, w_ref, h_ref, proj_ref):
        xhat, _ = _rms(x_ref[...], D)
        h = (xhat * g_ref[...]) * (1.0 + mod_ref[0, 1:2, :]) + mod_ref[0, 0:1, :]
        hb = h.astype(BF16)
        h_ref[...] = hb
        proj_ref[...] = _dot(hb, w_ref[...])

    return pl.pallas_call(
        body, name="f1_fwd", grid=(n // tm,),
        out_shape=[jax.ShapeDtypeStruct((n, D), BF16), jax.ShapeDtypeStruct((n, IN_PAD), F32)],
        in_specs=[pl.BlockSpec((tm, D), lambda i: (i, 0)),
                  pl.BlockSpec((1, 8, D), lambda i: (i // tps, 0, 0)), _VM, _VM],
        out_specs=[pl.BlockSpec((tm, D), lambda i: (i, 0)), pl.BlockSpec((tm, IN_PAD), lambda i: (i, 0))],
        compiler_params=_cp(("parallel",)),
    )(x, modp, g1, w_in)


def _f1_bwd(du, dmla, dx1, x, modp, g1, w_in, S, tm):
    n = x.shape[0]
    tps = S // tm
    nb = n // S

    def body(du_ref, dm_ref, dx1_ref, x_ref, mod_ref, g_ref, w_ref, dx_ref, dproj_ref, accs_ref, accg_ref):
        i = pl.program_id(0)
        dproj = jnp.concatenate([du_ref[...], dm_ref[...]], axis=1).astype(BF16)
        dproj_ref[...] = dproj
        dh = _dot_nt(dproj, w_ref[...])
        xhat, r = _rms(x_ref[...], D)
        g = g_ref[...]
        dn = dh * (1.0 + mod_ref[0, 1:2, :])
        dx_ref[...] = dx1_ref[...] + _rms_bwd(dn * g, xhat, r, D)

        @pl.when(i % tps == 0)
        def _():
            accs_ref[...] = jnp.zeros_like(accs_ref)

        @pl.when(i == 0)
        def _():
            accg_ref[...] = jnp.zeros_like(accg_ref)

        accs_ref[0, 0:1, :] += _colsum(dh)
        accs_ref[0, 1:2, :] += _colsum(dh * (xhat * g))
        accg_ref[0:1, :] += _colsum(dn * xhat)

    return pl.pallas_call(
        body, name="f1_bwd", grid=(n // tm,),
        out_shape=[jax.ShapeDtypeStruct((n, D), F32), jax.ShapeDtypeStruct((n, IN_PAD), BF16),
                   jax.ShapeDtypeStruct((nb, 8, D), F32), jax.ShapeDtypeStruct((8, D), F32)],
        in_specs=[pl.BlockSpec((tm, D_SSM), lambda i: (i, 0)), pl.BlockSpec((tm, IN_PAD - D_SSM), lambda i: (i, 0)),
                  pl.BlockSpec((tm, D), lambda i: (i, 0)), pl.BlockSpec((tm, D), lambda i: (i, 0)),
                  pl.BlockSpec((1, 8, D), lambda i: (i // tps, 0, 0)), _VM, _VM],
        out_specs=[pl.BlockSpec((tm, D), lambda i: (i, 0)), pl.BlockSpec((tm, IN_PAD), lambda i: (i, 0)),
                   pl.BlockSpec((1, 8, D), lambda i: (i // tps, 0, 0)), pl.BlockSpec((8, D), lambda i: (0, 0))],
        compiler_params=_cp(("arbitrary",)),
    )(du, dmla, dx1, x, modp, g1, w_in)


def _ssm_param_fwd(lam_re, lam_im, logdt, b_re, b_im):
    def body(lr_ref, li_ref, ld_ref, br_ref, bi_ref, lbr_ref, lbi_ref, bbr_ref, bbi_ref):
        lr, li = lr_ref[...], li_ref[...]
        dt = jnp.exp(ld_ref[...])
        er = jnp.exp(lr * dt)
        lbr = er * jnp.cos(li * dt)
        lbi = er * jnp.sin(li * dt)
        den = 1.0 / (lr * lr + li * li)
        cr = ((lbr - 1.0) * lr + lbi * li) * den
        ci = (lbi * lr - (lbr - 1.0) * li) * den
        lbr_ref[...] = lbr
        lbi_ref[...] = lbi
        bbr_ref[...] = cr * br_ref[...] - ci * bi_ref[...]
        bbi_ref[...] = cr * bi_ref[...] + ci * br_ref[...]

    return pl.pallas_call(
        body, name="ssm_param_fwd",
        out_shape=[jax.ShapeDtypeStruct((NST, 1), F32)] * 2 + [jax.ShapeDtypeStruct((NST, H), F32)] * 2,
        in_specs=[_VM] * 5, out_specs=[_VM] * 4, compiler_params=_cp(),
    )(lam_re, lam_im, logdt, b_re, b_im)


def _ssm_param_bwd(lam_re, lam_im, logdt, b_re, b_im, dlb_re, dlb_im, dbb_re, dbb_im):
    def body(lr_ref, li_ref, ld_ref, br_ref, bi_ref, dlr_ref, dli_ref, dbr_ref, dbi_ref,
             gbr_ref, gbi_ref, glr_ref, gli_ref, gdt_ref):
        lr, li = lr_ref[...], li_ref[...]
        dt = jnp.exp(ld_ref[...])
        er = jnp.exp(lr * dt)
        lbr = er * jnp.cos(li * dt)
        lbi = er * jnp.sin(li * dt)
        den = 1.0 / (lr * lr + li * li)
        nr, ni = lbr - 1.0, lbi
        cr = (nr * lr + ni * li) * den
        ci = (ni * lr - nr * li) * den
        br, bi = br_ref[...], bi_ref[...]
        dbr, dbi = dbr_ref[...], dbi_ref[...]
        gbr_ref[...] = cr * dbr + ci * dbi
        gbi_ref[...] = cr * dbi - ci * dbr
        gcr = jnp.sum(dbr * br + dbi * bi, axis=1, keepdims=True)
        gci = jnp.sum(dbi * br - dbr * bi, axis=1, keepdims=True)
        ilr, ili = lr * den, -li * den
        glbr = dlr_ref[...] + (gcr * ilr + gci * ili)
        glbi = dli_ref[...] + (gci * ilr - gcr * ili)
        qr = -(cr * ilr - ci * ili)
        qi = -(cr * ili + ci * ilr)
        glr = gcr * qr + gci * qi
        gli = gci * qr - gcr * qi
        glr = glr + dt * (glbr * lbr + glbi * lbi)
        gli = gli + dt * (glbi * lbr - glbr * lbi)
        wr = lr * lbr - li * lbi
        wi = lr * lbi + li * lbr
        glr_ref[...] = glr
        gli_ref[...] = gli
        gdt_ref[...] = (glbr * wr + glbi * wi) * dt

    return pl.pallas_call(
        body, name="ssm_param_bwd",
        out_shape=[jax.ShapeDtypeStruct((NST, H), F32)] * 2 + [jax.ShapeDtypeStruct((NST, 1), F32)] * 3,
        in_specs=[_VM] * 9, out_specs=[_VM] * 5, compiler_params=_cp(),
    )(lam_re, lam_im, logdt, b_re, b_im, dlb_re, dlb_im, dbb_re, dbb_im)


def _rowsum(a):
    def body(a_ref, o_ref):
        o_ref[...] = jnp.sum(a_ref[...], axis=1, keepdims=True)

    return pl.pallas_call(
        body, name="rowsum", out_shape=jax.ShapeDtypeStruct((a.shape[0], 1), F32),
        in_specs=[_VM], out_specs=_VM, compiler_params=_cp(),
    )(a)


def _pow2k(pr, pi, nsq):
    for _ in range(nsq):
        pr, pi = pr * pr - pi * pi, 2.0 * pr * pi
    return pr, pi


def _ssm_local(u_p, bm, lre8, lim8, S, tt):
    n = u_p.shape[0]
    nb, nt = n // S, S // tt
    nsq = int(round(math.log2(S // 8)))
    assert 2 ** nsq == S // 8

    def body(u_ref, bm_ref, lre_ref, lim_ref, cre_ref, cim_ref, sre, sim, bu):
        j = pl.program_id(1)

        @pl.when(j == 0)
        def _():
            sre[...] = jnp.zeros_like(sre)
            sim[...] = jnp.zeros_like(sim)

        bu[...] = _dot(u_ref[...].astype(BF16), bm_ref[...])
        lre, lim = lre_ref[...], lim_ref[...]

        def step(i, c):
            sr, si = c
            off = pl.multiple_of(i * 8, 8)
            br = bu[pl.ds(off, 8), 0:NST]
            bi = bu[pl.ds(off, 8), NST:2 * NST]
            return lre * sr - lim * si + br, lre * si + lim * sr + bi

        sr, si = lax.fori_loop(0, tt // 8, step, (sre[...], sim[...]))
        sre[...] = sr
        sim[...] = si

        @pl.when(j == nt - 1)
        def _():
            pr, pi = _pow2k(lre[0:1], lim[0:1], nsq)
            cr = jnp.zeros((1, NST), F32)
            ci = jnp.zeros((1, NST), F32)
            cre_ref[0:1, :] = cr
            cim_ref[0:1, :] = ci
            for k in range(1, 8):
                cr, ci = sr[k - 1:k] + pr * cr - pi * ci, si[k - 1:k] + pr * ci + pi * cr
                cre_ref[k:k + 1, :] = cr
                cim_ref[k:k + 1, :] = ci

    return pl.pallas_call(
        body, name="ssm_local", grid=(nb, nt),
        out_shape=[jax.ShapeDtypeStruct((nb * 8, NST), F32)] * 2,
        in_specs=[pl.BlockSpec((tt, D_SSM), lambda b, j: (b * nt + j, 0)), _VM, _VM, _VM],
        out_specs=[pl.BlockSpec((8, NST), lambda b, j: (b, 0))] * 2,
        scratch_shapes=[pltpu.VMEM((8, NST), F32), pltpu.VMEM((8, NST), F32), pltpu.VMEM((tt, 2 * NST), F32)],
        compiler_params=_cp(("arbitrary", "arbitrary")),
    )(u_p, bm, lre8, lim8)


def _ssm_fwd(u_p, cre, cim, bm, cm, dvec, w_glu, lre8, lim8, S, tt):
    n = u_p.shape[0]
    nb, nt = n // S, S // tt

    def body(u_ref, cre_ref, cim_ref, bm_ref, cm_ref, d_ref, wg_ref, lre_ref, lim_ref,
             st_ref, ypre_ref, z_ref, gact_ref, yssm_ref, sre, sim, bu):
        j = pl.program_id(1)

        @pl.when(j == 0)
        def _():
            sre[...] = cre_ref[...]
            sim[...] = cim_ref[...]

        u = u_ref[...]
        bu[...] = _dot(u.astype(BF16), bm_ref[...])
        lre, lim = lre_ref[...], lim_ref[...]

        def step(i, c):
            sr, si = c
            off = pl.multiple_of(i * 8, 8)
            nr = lre * sr - lim * si + bu[pl.ds(off, 8), 0:NST]
            ni = lre * si + lim * sr + bu[pl.ds(off, 8), NST:2 * NST]
            st_ref[pl.ds(off, 8), 0:NST] = nr
            st_ref[pl.ds(off, 8), NST:2 * NST] = ni
            return nr, ni

        sr, si = lax.fori_loop(0, tt // 8, step, (sre[...], sim[...]))
        sre[...] = sr
        sim[...] = si
        y = _dot(st_ref[...].astype(BF16), cm_ref[...]) + d_ref[...] * u
        ypre_ref[...] = y
        gb = _gelu(y).astype(BF16)
        gact_ref[...] = gb
        z = _dot(gb, wg_ref[...])
        z_ref[...] = z
        yssm_ref[...] = z[:, 0:D_SSM] * _sigmoid(z[:, D_SSM:2 * D_SSM])

    row = lambda w: pl.BlockSpec((tt, w), lambda b, j: (b * nt + j, 0))
    return pl.pallas_call(
        body, name="ssm_fwd", grid=(nb, nt),
        out_shape=[jax.ShapeDtypeStruct((n, 2 * NST), F32), jax.ShapeDtypeStruct((n, D_SSM), F32),
                   jax.ShapeDtypeStruct((n, 2 * D_SSM), F32), jax.ShapeDtypeStruct((n, D_SSM), BF16),
                   jax.ShapeDtypeStruct((n, D_SSM), F32)],
        in_specs=[row(D_SSM), pl.BlockSpec((8, NST), lambda b, j: (b, 0)), pl.BlockSpec((8, NST), lambda b, j: (b, 0)),
                  _VM, _VM, _VM, _VM, _VM, _VM],
        out_specs=[row(2 * NST), row(D_SSM), row(2 * D_SSM), row(D_SSM), row(D_SSM)],
        scratch_shapes=[pltpu.VMEM((8, NST), F32), pltpu.VMEM((8, NST), F32), pltpu.VMEM((tt, 2 * NST), F32)],
        compiler_params=_cp(("arbitrary", "arbitrary")),
    )(u_p, cre, cim, bm, cm, dvec, w_glu, lre8, lim8)


def _ssm_bwd_a(dys_p, z, ypre, w_glu, cm, lre8, lim8, S, tt):
    n = z.shape[0]
    nb, nt = n // S, S // tt
    nsq = int(round(math.log2(S // 8)))
    ng = tt // 8

    def body(dys_ref, z_ref, y_ref, wg_ref, cm_ref, lre_ref, lim_ref, dy_ref, dz_ref, are_ref, aim_ref, sre, sim, gb):
        j = pl.program_id(1)

        @pl.when(j == 0)
        def _():
            sre[...] = jnp.zeros_like(sre)
            sim[...] = jnp.zeros_like(sim)

        z = z_ref[...]
        z1, z2 = z[:, 0:D_SSM], z[:, D_SSM:2 * D_SSM]
        sg = _sigmoid(z2)
        dys = dys_ref[...]
        dz = jnp.concatenate([dys * sg, dys * z1 * sg * (1.0 - sg)], axis=1).astype(BF16)
        dz_ref[...] = dz
        dy = _dot_nt(dz, wg_ref[...]) * _gelu_grad(y_ref[...])
        dy_ref[...] = dy
        gb[...] = _dot_nt(dy.astype(BF16), cm_ref[...])
        lre, lim = lre_ref[...], lim_ref[...]

        def step(i, c):
            ar, ai = c
            off = pl.multiple_of((ng - 1 - i) * 8, 8)
            gr = gb[pl.ds(off, 8), 0:NST]
            gi = gb[pl.ds(off, 8), NST:2 * NST]
            return lre * ar + lim * ai + gr, lre * ai - lim * ar + gi

        ar, ai = lax.fori_loop(0, ng, step, (sre[...], sim[...]))
        sre[...] = ar
        sim[...] = ai

        @pl.when(j == nt - 1)
        def _():
            pr, pi = _pow2k(lre[0:1], -lim[0:1], nsq)
            cr = jnp.zeros((1, NST), F32)
            ci = jnp.zeros((1, NST), F32)
            are_ref[7:8, :] = cr
            aim_ref[7:8, :] = ci
            for k in range(6, -1, -1):
                cr, ci = ar[k + 1:k + 2] + pr * cr - pi * ci, ai[k + 1:k + 2] + pr * ci + pi * cr
                are_ref[k:k + 1, :] = cr
                aim_ref[k:k + 1, :] = ci

    row = lambda w: pl.BlockSpec((tt, w), lambda b, j: (b * nt + nt - 1 - j, 0))
    return pl.pallas_call(
        body, name="ssm_bwd_a", grid=(nb, nt),
        out_shape=[jax.ShapeDtypeStruct((n, D_SSM), F32), jax.ShapeDtypeStruct((n, 2 * D_SSM), BF16),
                   jax.ShapeDtypeStruct((nb * 8, NST), F32), jax.ShapeDtypeStruct((nb * 8, NST), F32)],
        in_specs=[row(D_SSM), row(2 * D_SSM), row(D_SSM), _VM, _VM, _VM, _VM],
        out_specs=[row(D_SSM), row(2 * D_SSM), pl.BlockSpec((8, NST), lambda b, j: (b, 0)),
                   pl.BlockSpec((8, NST), lambda b, j: (b, 0))],
        scratch_shapes=[pltpu.VMEM((8, NST), F32), pltpu.VMEM((8, NST), F32), pltpu.VMEM((tt, 2 * NST), F32)],
        compiler_params=_cp(("arbitrary", "arbitrary")),
    )(dys_p, z, ypre, w_glu, cm, lre8, lim8)


def _ssm_bwd_b(dy, u_p, st, fcr, fci, air, aii, bm, cm, dvec, lre8, lim8, S, tt):
    n = u_p.shape[0]
    nb, nt = n // S, S // tt
    ng = tt // 8
    QB = D_SSM // 4

    def body(dy_ref, u_ref, st_ref, stp_ref, fcr_ref, fci_ref, air_ref, aii_ref, bm_ref, cm_ref, d_ref, lre_ref, lim_ref,
             du_ref, dcm_ref, dbm_ref, dd_ref, dlr_ref, dli_ref, are, aim, accr, acci, sp, ab):
        b = pl.program_id(0)
        j = pl.program_id(1)
        jt = nt - 1 - j

        @pl.when((b == 0) & (j == 0))
        def _():
            dcm_ref[...] = jnp.zeros_like(dcm_ref)
            dbm_ref[...] = jnp.zeros_like(dbm_ref)
            dd_ref[...] = jnp.zeros_like(dd_ref)
            accr[...] = jnp.zeros_like(accr)
            acci[...] = jnp.zeros_like(acci)

        @pl.when(j == 0)
        def _():
            are[...] = air_ref[...]
            aim[...] = aii_ref[...]

        sp[8:tt + 8, :] = st_ref[...]

        @pl.when(jt == 0)
        def _():
            sp[0:8, 0:NST] = fcr_ref[...]
            sp[0:8, NST:2 * NST] = fci_ref[...]

        @pl.when(jt != 0)
        def _():
            sp[0:8, :] = stp_ref[...]

        dy = dy_ref[...]
        u = u_ref[...]
        dyb = dy.astype(BF16)
        ab[...] = _dot_nt(dyb, cm_ref[...])
        lre, lim = lre_ref[...], lim_ref[...]

        def step(i, c):
            ar, ai = c
            off = pl.multiple_of((ng - 1 - i) * 8, 8)
            nr = lre * ar + lim * ai + ab[pl.ds(off, 8), 0:NST]
            ni = lre * ai - lim * ar + ab[pl.ds(off, 8), NST:2 * NST]
            ab[pl.ds(off, 8), 0:NST] = nr
            ab[pl.ds(off, 8), NST:2 * NST] = ni
            pr = sp[pl.ds(off, 8), 0:NST]
            pi = sp[pl.ds(off, 8), NST:2 * NST]
            accr[...] += nr * pr + ni * pi
            acci[...] += ni * pr - nr * pi
            return nr, ni

        ar, ai = lax.fori_loop(0, ng, step, (are[...], aim[...]))
        are[...] = ar
        aim[...] = ai
        a_b = ab[...].astype(BF16)
        du_ref[...] = _dot_nt(a_b, bm_ref[...]) + d_ref[...] * dy
        ub = u.astype(BF16)
        for q in range(4):
            for part in range(2):
                lo = part * NST + q * 4 * QB
                s_q = sp[8:tt + 8, lo:lo + 4 * QB].astype(BF16)
                dcm_ref[lo:lo + 4 * QB, :] += _dot_tn(s_q, dyb[:, q * QB:(q + 1) * QB])
                dbm_ref[:, lo:lo + 4 * QB] += _dot_tn(ub[:, q * QB:(q + 1) * QB], a_b[:, lo:lo + 4 * QB])
        dd_ref[...] += _colsum(dy * u)

        @pl.when((b == nb - 1) & (j == nt - 1))
        def _():
            dlr_ref[...] = _colsum(accr[...])
            dli_ref[...] = _colsum(acci[...])

    row = lambda w: pl.BlockSpec((tt, w), lambda b, j: (b * nt + nt - 1 - j, 0))
    seq8 = pl.BlockSpec((8, NST), lambda b, j: (b, 0))
    prev = pl.BlockSpec((8, 2 * NST), lambda b, j: (jnp.maximum((b * nt + nt - 1 - j) * ng - 1, 0), 0))
    const = lambda shape: pl.BlockSpec(shape, lambda b, j: (0, 0))
    return pl.pallas_call(
        body, name="ssm_bwd_b", grid=(nb, nt),
        out_shape=[jax.ShapeDtypeStruct((n, D_SSM), F32), jax.ShapeDtypeStruct((2 * NST, QB), F32),
                   jax.ShapeDtypeStruct((QB, 2 * NST), F32), jax.ShapeDtypeStruct((1, D_SSM), F32),
                   jax.ShapeDtypeStruct((1, NST), F32), jax.ShapeDtypeStruct((1, NST), F32)],
        in_specs=[row(D_SSM), row(D_SSM), row(2 * NST), prev, seq8, seq8, seq8, seq8, _VM, _VM, _VM, _VM, _VM],
        out_specs=[row(D_SSM), const((2 * NST, QB)), const((QB, 2 * NST)), const((1, D_SSM)),
                   const((1, NST)), const((1, NST))],
        scratch_shapes=[pltpu.VMEM((8, NST), F32)] * 4 + [pltpu.VMEM((tt + 8, 2 * NST), F32),
                                                          pltpu.VMEM((tt, 2 * NST), F32)],
        compiler_params=_cp(("arbitrary", "arbitrary")),
    )(dy, u_p, st, st, fcr, fci, air, aii, bm, cm, dvec, lre8, lim8)


def _rope(v, c, s1, s2):
    return v * c + _roll(v, -16) * s1 + _roll(v, 16) * s2


def _rope_t(dv, c, s1, s2):
    return dv * c + _roll(dv * s1, 16) + _roll(dv * s2, -16)


def _mla_fwd(proj, rc, rs1, rs2, gq, gkv, w_uq, w_ukv, tm):
    n = proj.shape[0]

    def body(ql_ref, kvl_ref, kr_ref, c_ref, s1_ref, s2_ref, gq_ref, gkv_ref, wq_ref, wkv_ref,
             q_ref, k_ref, v_ref, qn_ref, kvn_ref):
        c, s1, s2 = c_ref[...], s1_ref[...], s2_ref[...]
        qhat, _ = _rms(ql_ref[...], Q_LORA)
        qn = (qhat * gq_ref[...]).astype(BF16)
        qn_ref[...] = qn
        q = _dot(qn, wq_ref[...])
        q_ref[...] = _rope(q, jnp.tile(c, (1, NH)), jnp.tile(s1, (1, NH)), jnp.tile(s2, (1, NH))).astype(BF16)
        khat, _ = _rms(kvl_ref[...], KV_LORA)
        kvn = (khat * gkv_ref[...]).astype(BF16)
        kvn_ref[...] = kvn
        kv = _dot(kvn, wkv_ref[...])
        kr = _rope(_roll(kr_ref[...], 64), c, s1, s2)
        k_ref[...] = (kv[:, 0:NH * HP] + jnp.tile(kr, (1, NH))).astype(BF16)
        v_ref[...] = kv[:, NH * HP:2 * NH * HP].astype(BF16)

    def wrapped(proj_ref, *rest):
        ql = proj_ref.at[:, D_SSM:D_SSM + Q_LORA]
        kvl = proj_ref.at[:, D_SSM + Q_LORA:D_SSM + Q_LORA + KV_LORA]
        kr = proj_ref.at[:, IN_PAD - HP:IN_PAD]
        body(ql, kvl, kr, *rest)

    row = lambda w: pl.BlockSpec((tm, w), lambda i: (i, 0))
    return pl.pallas_call(
        wrapped, name="mla_fwd", grid=(n // tm,),
        out_shape=[jax.ShapeDtypeStruct((n, NH * HP), BF16)] * 3 +
                  [jax.ShapeDtypeStruct((n, Q_LORA), BF16), jax.ShapeDtypeStruct((n, KV_LORA), BF16)],
        in_specs=[row(IN_PAD), row(HP), row(HP), row(HP), _VM, _VM, _VM, _VM],
        out_specs=[row(NH * HP)] * 3 + [row(Q_LORA), row(KV_LORA)],
        compiler_params=_cp(("parallel",)),
    )(proj, rc, rs1, rs2, gq, gkv, w_uq, w_ukv)


def _mla_bwd(dq, dk, dv, proj, rc, rs1, rs2, gq, gkv, w_uq, w_ukv, tm):
    n = proj.shape[0]

    def body(dq_ref, dk_ref, dv_ref, proj_ref, c_ref, s1_ref, s2_ref, gq_ref, gkv_ref, wq_ref, wkv_ref,
             dmla_ref, dqb_ref, dkvb_ref, acc_ref):
        i = pl.program_id(0)
        c, s1, s2 = c_ref[...], s1_ref[...], s2_ref[...]
        dqu = _rope_t(dq_ref[...], jnp.tile(c, (1, NH)), jnp.tile(s1, (1, NH)), jnp.tile(s2, (1, NH))).astype(BF16)
        dqb_ref[...] = dqu
        dqn = _dot_nt(dqu, wq_ref[...])
        qhat, rq = _rms(proj_ref[:, D_SSM:D_SSM + Q_LORA], Q_LORA)
        dql = _rms_bwd(dqn * gq_ref[...], qhat, rq, Q_LORA)
        dkf = dk_ref[...]
        dkv = jnp.concatenate([dkf, dv_ref[...]], axis=1).astype(BF16)
        dkvb_ref[...] = dkv
        dkvn = _dot_nt(dkv, wkv_ref[...])
        khat, rk = _rms(proj_ref[:, D_SSM + Q_LORA:D_SSM + Q_LORA + KV_LORA], KV_LORA)
        dkvl = _rms_bwd(dkvn * gkv_ref[...], khat, rk, KV_LORA)
        dkr = dkf[:, 0:HP]
        for h in range(1, NH):
            dkr = dkr + dkf[:, h * HP:(h + 1) * HP]
        lane = lax.broadcasted_iota(jnp.int32, dkr.shape, 1)
        dkr = jnp.where((lane >= QK_NOPE) & (lane < QK_NOPE + QK_ROPE), dkr, 0.0)
        dkr = _roll(_rope_t(dkr, c, s1, s2), -64)
        dmla_ref[...] = jnp.concatenate([dql, dkvl, dkr], axis=1)

        @pl.when(i == 0)
        def _():
            acc_ref[...] = jnp.zeros_like(acc_ref)

        acc_ref[0:1, 0:Q_LORA] += _colsum(dqn * qhat)
        acc_ref[1:2, 0:KV_LORA] += _colsum(dkvn * khat)

    row = lambda w: pl.BlockSpec((tm, w), lambda i: (i, 0))
    return pl.pallas_call(
        body, name="mla_bwd", grid=(n // tm,),
        out_shape=[jax.ShapeDtypeStruct((n, IN_PAD - D_SSM), F32), jax.ShapeDtypeStruct((n, NH * HP), BF16),
                   jax.ShapeDtypeStruct((n, 2 * NH * HP), BF16), jax.ShapeDtypeStruct((8, Q_LORA), F32)],
        in_specs=[row(NH * HP)] * 3 + [row(IN_PAD), row(HP), row(HP), row(HP), _VM, _VM, _VM, _VM],
        out_specs=[row(IN_PAD - D_SSM), row(NH * HP), row(2 * NH * HP), pl.BlockSpec((8, Q_LORA), lambda i: (0, 0))],
        compiler_params=_cp(("arbitrary",)),
    )(dq, dk, dv, proj, rc, rs1, rs2, gq, gkv, w_uq, w_ukv)


_SCALE = (QK_NOPE + QK_ROPE) ** -0.5
_LOG2E = 1.4426950408889634
_C2 = _SCALE * _LOG2E


def _attn_fwd(q, k, v, S, tq):
    n = q.shape[0]
    nb, nq = n // S, S // tq

    def body(q_ref, k_ref, v_ref, o_ref, lr_ref):
        qi = pl.program_id(2)
        qv = q_ref[...]

        def tile(j, c, diagonal):
            m, l, acc = c
            off = pl.multiple_of(j * tq, tq)
            s = _dot_nt(qv, k_ref[pl.ds(off, tq), :]) * _C2
            if diagonal:
                rows = lax.broadcasted_iota(jnp.int32, s.shape, 0)
                cols = lax.broadcasted_iota(jnp.int32, s.shape, 1)
                s = jnp.where(cols <= rows, s, NEG)
            mn = jnp.maximum(m, jnp.max(s, axis=1, keepdims=True))
            p = jnp.exp2(s - mn)
            al = jnp.exp2(m - mn)
            l = al * l + jnp.sum(p, axis=1, keepdims=True)
            acc = al * acc + _dot(p.astype(BF16), v_ref[pl.ds(off, tq), :])
            return mn, l, acc

        init = (jnp.full((tq, 1), NEG, F32), jnp.zeros((tq, 1), F32), jnp.zeros((tq, HP), F32))
        c = lax.fori_loop(0, qi, lambda j, c: tile(j, c, False), init)
        m, l, acc = tile(qi, c, True)
        o_ref[...] = acc / l
        lane = lax.broadcasted_iota(jnp.int32, (8, HP), 1)
        lse = jnp.broadcast_to(m + jnp.log(l) * _LOG2E, (tq, HP))
        lr_ref[...] = _rows_of(lse, jnp.where(lane == 0, 1.0, 0.0).astype(BF16))

    qs = pl.BlockSpec((tq, HP), lambda b, h, i: (b * nq + i, h))
    ks = pl.BlockSpec((S, HP), lambda b, h, i: (b, h))
    return pl.pallas_call(
        body, name="attn_fwd", grid=(nb, NH, nq),
        out_shape=[jax.ShapeDtypeStruct((n, NH * HP), F32), jax.ShapeDtypeStruct((nb * NH * 8, S), F32)],
        in_specs=[qs, ks, ks], out_specs=[qs, pl.BlockSpec((8, tq), lambda b, h, i: (b * NH + h, i))],
        compiler_params=_cp(("parallel", "parallel", "arbitrary")),
    )(q, k, v)


def _rows_of(x, pick):
    x1 = x.astype(BF16)
    r1 = x - x1.astype(F32)
    x2 = r1.astype(BF16)
    x3 = (r1 - x2.astype(F32)).astype(BF16)
    return _dot_nt(pick, x1) + _dot_nt(pick, x2) + _dot_nt(pick, x3)


def _attn_bwd(q, k, v, dob, lrow, drow, S, tq):
    n = q.shape[0]
    nb, nq = n // S, S // tq

    def body(q_ref, k_ref, v_ref, do_ref, lr_ref, dr_ref, dq_ref, dk_ref, dv_ref):
        kj = pl.program_id(2)

        @pl.when(kj == 0)
        def _():
            dq_ref[...] = jnp.zeros_like(dq_ref)

        kt = k_ref[...]
        vt = v_ref[...]

        def tile(i, c, diagonal):
            dk, dv = c
            off = pl.multiple_of(i * tq, tq)
            qv = q_ref[pl.ds(off, tq), :]
            dob = do_ref[pl.ds(off, tq), :]
            lr = lr_ref[0:1, pl.ds(off, tq)]
            dr = dr_ref[0:1, pl.ds(off, tq)]
            st = _dot_nt(kt, qv)
            dpt = _dot_nt(vt, dob)
            pt = jnp.exp2(st * _C2 - lr)
            if diagonal:
                keys = lax.broadcasted_iota(jnp.int32, pt.shape, 0)
                qrys = lax.broadcasted_iota(jnp.int32, pt.shape, 1)
                pt = jnp.where(keys <= qrys, pt, 0.0)
            dst = (pt * (dpt * _SCALE - dr)).astype(BF16)
            dq_ref[pl.ds(off, tq), :] += _dot_tn(dst, kt)
            return dk + _dot(dst, qv), dv + _dot(pt.astype(BF16), dob)

        zero = jnp.zeros((tq, HP), F32)
        c = tile(kj, (zero, zero), True)
        dk, dv = lax.fori_loop(kj + 1, nq, lambda i, c: tile(i, c, False), c)
        dk_ref[...] = dk
        dv_ref[...] = dv

    ts = pl.BlockSpec((tq, HP), lambda b, h, i: (b * nq + i, h))
    fs = pl.BlockSpec((S, HP), lambda b, h, i: (b, h))
    rs = pl.BlockSpec((8, S), lambda b, h, i: (b * NH + h, 0))
    return pl.pallas_call(
        body, name="attn_bwd", grid=(nb, NH, nq),
        out_shape=[jax.ShapeDtypeStruct((n, NH * HP), F32)] * 3,
        in_specs=[fs, ts, ts, fs, rs, rs], out_specs=[fs, ts, ts],
        compiler_params=_cp(("parallel", "parallel", "arbitrary")),
    )(q, k, v, dob, lrow, drow)


def _p1_fwd(yssm, oattn, x, modp, gs, ga, w_out, g2, S, tm):
    n = x.shape[0]
    tps = S // tm

    def body(ys_ref, oa_ref, x_ref, mod_ref, gs_ref, ga_ref, w_ref, g2_ref, yn_ref, o_ref, x1_ref, h2_ref):
        yh, _ = _rms(ys_ref[...], D_SSM)
        ah, _ = _rms(oa_ref[...], D_ATTN)
        yn = jnp.concatenate([yh * gs_ref[...], ah * ga_ref[...]], axis=1).astype(BF16)
        yn_ref[...] = yn
        o = _dot(yn, w_ref[...])
        o_ref[...] = o
        x1 = x_ref[...] + mod_ref[0, 2:3, :] * o
        x1_ref[...] = x1
        xh, _ = _rms(x1, D)
        h2_ref[...] = ((xh * g2_ref[...]) * (1.0 + mod_ref[0, 4:5, :]) + mod_ref[0, 3:4, :]).astype(BF16)

    row = lambda w: pl.BlockSpec((tm, w), lambda i: (i, 0))
    return pl.pallas_call(
        body, name="p1_fwd", grid=(n // tm,),
        out_shape=[jax.ShapeDtypeStruct((n, D_SSM + NH * HP), BF16), jax.ShapeDtypeStruct((n, D), F32),
                   jax.ShapeDtypeStruct((n, D), F32), jax.ShapeDtypeStruct((n, D), BF16)],
        in_specs=[row(D_SSM), row(NH * HP), row(D), pl.BlockSpec((1, 8, D), lambda i: (i // tps, 0, 0)),
                  _VM, _VM, _VM, _VM],
        out_specs=[row(D_SSM + NH * HP), row(D), row(D), row(D)],
        compiler_params=_cp(("parallel",)),
    )(yssm, oattn, x, modp, gs, ga, w_out, g2)


def _p2(x1, h2, target, modp, g2, gf, w_ff1, w_ff2, S, tm):
    n = x1.shape[0]
    tps = S // tm
    nb = n // S

    def body(x1_ref, h2_ref, t_ref, mod_ref, g2_ref, gf_ref, w1_ref, w2_ref,
             dx1_ref, r_ref, da_ref, dff_ref, accs_ref, accg_ref):
        i = pl.program_id(0)
        sh2, sc2, gt2 = mod_ref[0, 3:4, :], mod_ref[0, 4:5, :], mod_ref[0, 5:6, :]
        fsh, fsc = mod_ref[0, 6:7, :], mod_ref[0, 7:8, :]
        x1 = x1_ref[...]
        a = _dot(h2_ref[...], w1_ref[...])
        ra = jnp.maximum(a, 0.0)
        rb = (ra * ra).astype(BF16)
        r_ref[...] = rb
        ff = _dot(rb, w2_ref[...])
        x2 = x1 + gt2 * ff
        x2h, rf = _rms(x2, D)
        gf_v = gf_ref[...]
        outn = x2h * gf_v
        err = outn * (1.0 + fsc) + fsh - t_ref[...]
        dout = err * (1.0 / D)
        doutn = dout * (1.0 + fsc)
        dx2 = _rms_bwd(doutn * gf_v, x2h, rf, D)
        dff = (gt2 * dx2).astype(BF16)
        dff_ref[...] = dff
        dr = _dot_nt(dff, w2_ref[...])
        da = (dr * (2.0 * ra)).astype(BF16)
        da_ref[...] = da
        dh2 = _dot_nt(da, w1_ref[...])
        x1h, r2 = _rms(x1, D)
        g2_v = g2_ref[...]
        dn2 = dh2 * (1.0 + sc2)
        dx1_ref[...] = dx2 + _rms_bwd(dn2 * g2_v, x1h, r2, D)

        @pl.when(i % tps == 0)
        def _():
            accs_ref[...] = jnp.zeros_like(accs_ref)

        @pl.when(i == 0)
        def _():
            accg_ref[...] = jnp.zeros_like(accg_ref)

        accs_ref[0, 3:4, :] += _colsum(dh2)
        accs_ref[0, 4:5, :] += _colsum(dh2 * (x1h * g2_v))
        accs_ref[0, 5:6, :] += _colsum(dx2 * ff)
        accs_ref[0, 6:7, :] += _colsum(dout)
        accs_ref[0, 7:8, :] += _colsum(dout * outn)
        accg_ref[0:1, :] += _colsum(dn2 * x1h)
        accg_ref[1:2, :] += _colsum(doutn * x2h)
        accg_ref[2:3, :] += _colsum(err * err) * (0.5 / D)

    row = lambda w: pl.BlockSpec((tm, w), lambda i: (i, 0))
    return pl.pallas_call(
        body, name="p2_mlp_loss", grid=(n // tm,),
        out_shape=[jax.ShapeDtypeStruct((n, D), F32), jax.ShapeDtypeStruct((n, D_FF), BF16),
                   jax.ShapeDtypeStruct((n, D_FF), BF16), jax.ShapeDtypeStruct((n, D), BF16),
                   jax.ShapeDtypeStruct((nb, 8, D), F32), jax.ShapeDtypeStruct((8, D), F32)],
        in_specs=[row(D), row(D), row(D), pl.BlockSpec((1, 8, D), lambda i: (i // tps, 0, 0)), _VM, _VM, _VM, _VM],
        out_specs=[row(D), row(D_FF), row(D_FF), row(D), pl.BlockSpec((1, 8, D), lambda i: (i // tps, 0, 0)),
                   pl.BlockSpec((8, D), lambda i: (0, 0))],
        compiler_params=_cp(("arbitrary",)),
    )(x1, h2, target, modp, g2, gf, w_ff1, w_ff2)


def _p3_bwd(dx1, o, yssm, oattn, modp, gs, ga, w_out, S, tm):
    n = dx1.shape[0]
    tps = S // tm
    nb = n // S

    def body(dx1_ref, o_ref, ys_ref, oa_ref, mod_ref, gs_ref, ga_ref, w_ref,
             do_ref, dys_ref, doa_ref, dr_ref, accs_ref, accg_ref):
        i = pl.program_id(0)
        dx1 = dx1_ref[...]
        dob = (mod_ref[0, 2:3, :] * dx1).astype(BF16)
        do_ref[...] = dob
        dyn = _dot_nt(dob, w_ref[...])
        yh, rs = _rms(ys_ref[...], D_SSM)
        oa = oa_ref[...]
        ah, ra = _rms(oa, D_ATTN)
        d1 = dyn[:, 0:D_SSM]
        d2 = dyn[:, D_SSM:D_SSM + NH * HP]
        dys_ref[...] = _rms_bwd(d1 * gs_ref[...], yh, rs, D_SSM)
        doa = _rms_bwd(d2 * ga_ref[...], ah, ra, D_ATTN)
        doa_ref[...] = doa.astype(BF16)
        prod = doa * oa * _SCALE
        ones = jnp.ones((8, HP), BF16)
        for h in range(NH):
            dr_ref[h * 8:(h + 1) * 8, :] = _rows_of(prod[:, h * HP:(h + 1) * HP], ones)

        @pl.when(i % tps == 0)
        def _():
            accs_ref[...] = jnp.zeros_like(accs_ref)

        @pl.when(i == 0)
        def _():
            accg_ref[...] = jnp.zeros_like(accg_ref)

        accs_ref[0, 2:3, :] += _colsum(dx1 * o_ref[...])
        accg_ref[0:1, 0:D_SSM] += _colsum(d1 * yh)
        accg_ref[1:2, :] += _colsum(d2 * ah)

    row = lambda w: pl.BlockSpec((tm, w), lambda i: (i, 0))
    return pl.pallas_call(
        body, name="p3_bwd", grid=(n // tm,),
        out_shape=[jax.ShapeDtypeStruct((n, D), BF16), jax.ShapeDtypeStruct((n, D_SSM), F32),
                   jax.ShapeDtypeStruct((n, NH * HP), BF16), jax.ShapeDtypeStruct((nb * NH * 8, S), F32),
                   jax.ShapeDtypeStruct((nb, 8, D), F32), jax.ShapeDtypeStruct((8, NH * HP), F32)],
        in_specs=[row(D), row(D), row(D_SSM), row(NH * HP), pl.BlockSpec((1, 8, D), lambda i: (i // tps, 0, 0)),
                  _VM, _VM, _VM],
        out_specs=[row(D), row(D_SSM), row(NH * HP), pl.BlockSpec((NH * 8, tm), lambda i: (i // tps, i % tps)),
                   pl.BlockSpec((1, 8, D), lambda i: (i // tps, 0, 0)), pl.BlockSpec((8, NH * HP), lambda i: (0, 0))],
        compiler_params=_cp(("arbitrary",)),
    )(dx1, o, yssm, oattn, modp, gs, ga, w_out)


def _wgrad(a, b, name, col_slots=0):
    n, k1 = a.shape
    k2 = b.shape[1]
    bn = next((b for b in (1024, 512) if n % b == 0), n)
    bk1 = next((b for b in (1024, 512) if k1 % b == 0), k1)
    bk2 = k2 // col_slots if col_slots else (1024 if (k2 % 1024 == 0) else k2)

    def body(a_ref, b_ref, o_ref):
        @pl.when(pl.program_id(2) == 0)
        def _():
            o_ref[...] = jnp.zeros_like(o_ref)

        o_ref[...] += _dot_tn(a_ref[...], b_ref[...]).reshape(o_ref.shape)

    if col_slots:
        out_shape = jax.ShapeDtypeStruct((col_slots, k1, bk2), F32)
        out_spec = pl.BlockSpec((1, bk1, bk2), lambda i, j, t: (j, i, 0))
    else:
        out_shape = jax.ShapeDtypeStruct((k1, k2), F32)
        out_spec = pl.BlockSpec((bk1, bk2), lambda i, j, t: (i, j))
    return pl.pallas_call(
        body, name=name, grid=(k1 // bk1, k2 // bk2, n // bn),
        out_shape=out_shape,
        in_specs=[pl.BlockSpec((bn, bk1), lambda i, j, t: (t, i)), pl.BlockSpec((bn, bk2), lambda i, j, t: (t, j))],
        out_specs=out_spec,
        compiler_params=_cp(("parallel", "parallel", "arbitrary")),
    )(a, b)


def _row_block(rows):
    if rows <= 256:
        return rows
    return next(b for b in (256, 192, 128, 64, 32, 16, 8) if rows % b == 0)


def _add_half(g, recv, cidx, name):
    _, rows2, w = g.shape
    rows = rows2 // 2
    br = _row_block(rows)
    nblk = rows // br

    def body(c_ref, g_ref, r_ref, o_ref):
        o_ref[...] = (g_ref[...] + r_ref[...]).astype(BF16)

    return pl.pallas_call(
        body, name=name,
        grid_spec=pltpu.PrefetchScalarGridSpec(
            num_scalar_prefetch=1, grid=(4, nblk),
            in_specs=[pl.BlockSpec((1, br, w), lambda s, i, c: (s, c[0] * nblk + i, 0)),
                      pl.BlockSpec((1, br, w), lambda s, i, c: (s, i, 0))],
            out_specs=pl.BlockSpec((1, br, w), lambda s, i, c: (s, i, 0))),
        out_shape=jax.ShapeDtypeStruct((4, rows, w), BF16),
        compiler_params=_cp(("parallel", "parallel")),
    )(cidx, g, recv)


def _add_chips(r, name):
    _, rows, w = r.shape
    br = _row_block(rows)

    def body(r_ref, o_ref):
        f = lambda k: r_ref[k].astype(F32)
        o_ref[...] = ((f(0) + f(1)) + f(2)) + f(3)

    return pl.pallas_call(
        body, name=name, grid=(rows // br,),
        out_shape=jax.ShapeDtypeStruct((rows, w), F32),
        in_specs=[pl.BlockSpec((4, br, w), lambda i: (0, i, 0))],
        out_specs=pl.BlockSpec((br, w), lambda i: (i, 0)),
        compiler_params=_cp(("parallel",)),
    )(r)


def _sum_devices(a):
    def body(a_ref, o_ref):
        acc = a_ref[0:1, :]
        for k in range(1, 8):
            acc = acc + a_ref[k:k + 1, :]
        o_ref[...] = acc

    return pl.pallas_call(
        body, name="small_grad_sum", out_shape=jax.ShapeDtypeStruct((1, a.shape[1]), F32),
        in_specs=[_VM], out_specs=_VM, compiler_params=_cp(),
    )(a)


def _adamw_math(wv, gv, mv, vv):
    m_new = ADAM_B1 * mv + (1.0 - ADAM_B1) * gv
    v_new = ADAM_B2 * vv + (1.0 - ADAM_B2) * (gv * gv)
    m_hat = m_new / (1.0 - ADAM_B1 ** ADAM_STEP)
    v_hat = v_new / (1.0 - ADAM_B2 ** ADAM_STEP)
    return -ADAM_LR * (m_hat / (jnp.sqrt(v_hat) + ADAM_EPS) + ADAM_WD * wv), m_new, v_new


def _adamw_small(ws, gs, ms, vs):
    k = len(ws)

    def body(*refs):
        ins, outs = refs[:4 * k], refs[4 * k:]
        for t in range(k):
            d, m_new, v_new = _adamw_math(ins[t][...], ins[k + t][...], ins[2 * k + t][...], ins[3 * k + t][...])
            outs[t][...] = d
            outs[k + t][...] = m_new
            outs[2 * k + t][...] = v_new

    shapes = [jax.ShapeDtypeStruct(w.shape, F32) for w in ws]
    return pl.pallas_call(
        body, name="adamw_small", out_shape=shapes * 3,
        in_specs=[_VM] * (4 * k), out_specs=[_VM] * (3 * k), compiler_params=_cp(),
    )(*ws, *gs, *ms, *vs)


def _adamw(w, g, m, v, name):
    rows, wd = w.shape
    br = _row_block(rows)

    def body(w_ref, g_ref, m_ref, v_ref, d_ref, nm_ref, nv_ref):
        d, m_new, v_new = _adamw_math(w_ref[...], g_ref[...], m_ref[...], v_ref[...])
        d_ref[...] = d
        nm_ref[...] = m_new
        nv_ref[...] = v_new

    spec = pl.BlockSpec((br, wd), lambda i: (i, 0))
    return pl.pallas_call(
        body, name=name, grid=(rows // br,),
        out_shape=[jax.ShapeDtypeStruct((rows, wd), F32)] * 3,
        in_specs=[spec] * 4, out_specs=[spec] * 3,
        compiler_params=_cp(("parallel",)),
    )(w, g, m, v)


def _adamw_halves(w, mine, other, m, v, cidx, name):
    rows, wd = w.shape
    h = rows // 2
    br = _row_block(h)
    nblk = h // br

    def body(c_ref, w_ref, a_ref, b_ref, m_ref, v_ref, g_ref, d_ref, nm_ref, nv_ref):
        gv = jnp.where(pl.program_id(0) == c_ref[0], a_ref[...], b_ref[...])
        d, m_new, v_new = _adamw_math(w_ref[...], gv, m_ref[...], v_ref[...])
        g_ref[...] = gv
        d_ref[...] = d
        nm_ref[...] = m_new
        nv_ref[...] = v_new

    full = pl.BlockSpec((br, wd), lambda hf, i, c: (hf * nblk + i, 0))
    half = pl.BlockSpec((br, wd), lambda hf, i, c: (i, 0))
    return pl.pallas_call(
        body, name=name,
        grid_spec=pltpu.PrefetchScalarGridSpec(
            num_scalar_prefetch=1, grid=(2, nblk),
            in_specs=[full, half, half, full, full], out_specs=[full] * 4),
        out_shape=[jax.ShapeDtypeStruct((rows, wd), F32)] * 4,
        compiler_params=_cp(("parallel", "parallel")),
    )(cidx, w, mine, other, m, v)


def _other_chips(x, y):
    return [(1 - x, y), (x, 1 - y), (1 - x, 1 - y)]


def _other_devices(x, y, c):
    flip = lambda v, d: (1 - v) if d else v
    return [(flip(x, dx), flip(y, dy), flip(c, dc))
            for dx in (0, 1) for dy in (0, 1) for dc in (0, 1) if (dx, dy, dc) != (0, 0, 0)]


def _exchange(name, ins, out_shapes, n_local, n_remote, plan):
    ni, no = len(ins), len(out_shapes)

    def body(*refs):
        in_refs, out_refs = refs[:ni], refs[ni:ni + no]
        send_sems, recv_sems, local_sems = refs[ni + no:]
        x, y, c = lax.axis_index("x"), lax.axis_index("y"), lax.axis_index("c")
        local, remote = plan(in_refs, out_refs, x, y, c)
        assert len(local) == n_local and len(remote) == n_remote

        def push(k, src, dst, dev):
            return pltpu.make_async_remote_copy(src_ref=src, dst_ref=dst, send_sem=send_sems.at[k],
                                                recv_sem=recv_sems.at[k], device_id=dev, device_id_type=MESH)

        own = [pltpu.make_async_copy(s, d, local_sems.at[i]) for i, (s, d) in enumerate(local)]
        for cp in own:
            cp.start()
        sends = [push(k, s, d, dev) for k, (s, d, dev, _) in enumerate(remote)]
        for cp in sends:
            cp.start()
        for k, (s, _, dev, landing) in enumerate(remote):
            push(k, s, landing, dev).wait_recv()
        for cp in sends:
            cp.wait_send()
        for cp in own:
            cp.wait()

    return pl.pallas_call(
        body, name=name, out_shape=out_shapes,
        in_specs=[_ANY] * ni, out_specs=[_ANY] * no,
        scratch_shapes=[pltpu.SemaphoreType.DMA((n_remote,)), pltpu.SemaphoreType.DMA((n_remote,)),
                        pltpu.SemaphoreType.DMA((max(n_local, 1),))],
        compiler_params=pltpu.CompilerParams(has_side_effects=True),
    )(*ins)


def _gather_chips(name, shards, everyone=()):
    ns, ne = len(shards), len(everyone)
    outs = [jax.ShapeDtypeStruct((4,) + a.shape, a.dtype) for a in shards]
    outs += [jax.ShapeDtypeStruct((8,) + a.shape, a.dtype) for a in everyone]

    def plan(i, o, x, y, c):
        mine, me = 2 * x + y, 4 * x + 2 * y + c
        local, remote = [], []
        for t in range(ns):
            local.append((i[t], o[t].at[mine]))
            for px, py in _other_chips(x, y):
                remote.append((i[t], o[t].at[mine], (px, py, c), o[t].at[2 * px + py]))
        for t in range(ns, ns + ne):
            local.append((i[t], o[t].at[me]))
            for px, py, pc in _other_devices(x, y, c):
                remote.append((i[t], o[t].at[me], (px, py, pc), o[t].at[4 * px + 2 * py + pc]))
        return local, remote

    return _exchange(name, list(shards) + list(everyone), outs, ns + ne, 3 * ns + 7 * ne, plan)


_HBM = pl.BlockSpec(memory_space=pltpu.HBM)
_SEM = pl.BlockSpec(memory_space=pltpu.SEMAPHORE)
_EFFECT = pltpu.SideEffectType.DATAFLOW_SIDE_EFFECTING


def _gather_start(shards, after):
    ns = len(shards)
    lands = [pltpu.with_memory_space_constraint(lax.empty((4,) + a.shape, a.dtype), pltpu.HBM) for a in shards]
    srcs = [pltpu.with_memory_space_constraint(a, pltpu.HBM) for a in shards]

    def body(*refs):
        src, land = refs[:ns], refs[ns:2 * ns]
        first = 2 * ns + 1
        send, recv = refs[first:first + 3 * ns], refs[first + 3 * ns:first + 6 * ns]
        token = refs[first + 8 * ns]
        x, y, c = lax.axis_index("x"), lax.axis_index("y"), lax.axis_index("c")
        mine = 2 * x + y
        for t in range(ns):
            for j, (px, py) in enumerate(_other_chips(x, y)):
                pltpu.make_async_remote_copy(src_ref=src[t], dst_ref=land[t].at[mine], send_sem=send[3 * t + j],
                                             recv_sem=recv[3 * t + j], device_id=(px, py, c),
                                             device_id_type=MESH).start()
        token[...] = jnp.zeros_like(token)

    out = pl.pallas_call(
        body, name="gather_late_start",
        out_shape=[pltpu.SemaphoreType.DMA(())] * (6 * ns)
                  + [pltpu.HBM(a.shape, a.dtype) for a in shards] + [pltpu.HBM((4,) + a.shape, a.dtype) for a in shards]
                  + [jax.ShapeDtypeStruct((8, 128), F32)],
        in_specs=[_HBM] * (2 * ns) + [_ANY], out_specs=[_SEM] * (6 * ns) + [_HBM] * (2 * ns) + [_VM],
        input_output_aliases={t: 6 * ns + t for t in range(2 * ns)},
        compiler_params=pltpu.CompilerParams(has_side_effects=_EFFECT),
    )(*srcs, *lands, after)
    return out[:6 * ns], out[6 * ns:7 * ns], out[7 * ns:8 * ns], out[8 * ns]


def _gather_wait(sems, srcs, lands, after):
    ns = len(srcs)

    def body(*refs):
        src, land = refs[:ns], refs[ns:2 * ns]
        send, recv = refs[2 * ns:5 * ns], refs[5 * ns:8 * ns]
        x, y, c = lax.axis_index("x"), lax.axis_index("y"), lax.axis_index("c")
        for t in range(ns):
            for j, (px, py) in enumerate(_other_chips(x, y)):
                cp = pltpu.make_async_remote_copy(src_ref=src[t], dst_ref=land[t].at[2 * px + py],
                                                  send_sem=send[3 * t + j], recv_sem=recv[3 * t + j],
                                                  device_id=(px, py, c), device_id_type=MESH)
                cp.wait_send()
                cp.wait_recv()

    out = pl.pallas_call(
        body, name="gather_late_wait",
        out_shape=[pltpu.HBM(a.shape, a.dtype) for a in srcs] + [pltpu.HBM(a.shape, a.dtype) for a in lands],
        in_specs=[_HBM] * (2 * ns) + [_SEM] * (6 * ns) + [_ANY], out_specs=[_HBM] * (2 * ns),
        input_output_aliases={t: t for t in range(2 * ns)},
        compiler_params=pltpu.CompilerParams(has_side_effects=_EFFECT),
    )(*srcs, *lands, *sems, after)
    return out[ns:]


def _swap_halves(gs, everyone):
    ns, ne = len(gs), len(everyone)
    outs = [jax.ShapeDtypeStruct((4, g.shape[1] // 2, g.shape[2]), g.dtype) for g in gs]
    outs += [jax.ShapeDtypeStruct((8,) + a.shape, a.dtype) for a in everyone]

    def plan(i, o, x, y, c):
        me = 4 * x + 2 * y + c
        local, remote = [], []
        for t in range(ns):
            h = gs[t].shape[1] // 2
            theirs = i[t].at[:, pl.ds(pl.multiple_of((1 - c) * h, 8), h), :]
            remote.append((theirs, o[t], (x, y, 1 - c), o[t]))
        for t in range(ns, ns + ne):
            local.append((i[t], o[t].at[me]))
            for px, py, pc in _other_devices(x, y, c):
                remote.append((i[t], o[t].at[me], (px, py, pc), o[t].at[4 * px + 2 * py + pc]))
        return local, remote

    return _exchange("grad_swap_sibling", list(gs) + list(everyone), outs, ne, ns + 7 * ne, plan)


def _scatter_chips(parts):
    ns = len(parts)
    outs = [jax.ShapeDtypeStruct(a.shape, a.dtype) for a in parts]

    def plan(i, o, x, y, c):
        mine = 2 * x + y
        local, remote = [], []
        for t in range(ns):
            local.append((i[t].at[mine], o[t].at[mine]))
            for px, py in _other_chips(x, y):
                remote.append((i[t].at[2 * px + py], o[t].at[mine], (px, py, c), o[t].at[2 * px + py]))
        return local, remote

    return _exchange("grad_scatter_chips", list(parts), outs, ns, 3 * ns, plan)


def _join_halves(halves):
    ns = len(halves)
    outs = [jax.ShapeDtypeStruct(a.shape, a.dtype) for a in halves]

    def plan(i, o, x, y, c):
        return [], [(i[t], o[t], (x, y, 1 - c), o[t]) for t in range(ns)]

    return _exchange("grad_join_sibling", list(halves), outs, 0, ns, plan)


def _pad_heads_cols(w, per, used):
    k = w.shape[0]
    w = w.reshape(k, NH, per)[:, :, :used]
    return jnp.pad(w, ((0, 0), (0, 0), (0, HP - used))).reshape(k, NH * HP)


def _unpad_heads_cols(w, used):
    k = w.shape[0]
    return w.reshape(k, NH, HP)[:, :, :used]


def _prep_weights(wf):
    bf = lambda a: a.astype(BF16)
    out = {}
    out["w_in"] = jnp.pad(bf(wf["w_in"]), ((0, 0), (0, IN_PAD - IN_COLS)))
    out["w_glu"] = bf(wf["w_glu"])
    out["w_uq"] = _pad_heads_cols(bf(wf["w_uq"]), QK_NOPE + QK_ROPE, QK_NOPE + QK_ROPE)
    wkv = bf(wf["w_ukv"]).reshape(KV_LORA, NH, QK_NOPE + V_HEAD)
    wk = jnp.pad(wkv[:, :, :QK_NOPE], ((0, 0), (0, 0), (0, HP - QK_NOPE))).reshape(KV_LORA, NH * HP)
    wv = jnp.pad(wkv[:, :, QK_NOPE:], ((0, 0), (0, 0), (0, HP - V_HEAD))).reshape(KV_LORA, NH * HP)
    out["w_ukv"] = jnp.concatenate([wk, wv], axis=1)
    return out


def _prep_late_weights(wf):
    bf = lambda a: a.astype(BF16)
    out = {}
    wo = bf(wf["w_out"])
    wo_a = jnp.pad(wo[D_SSM:].reshape(NH, V_HEAD, D), ((0, 0), (0, HP - V_HEAD), (0, 0))).reshape(NH * HP, D)
    out["w_out"] = jnp.concatenate([wo[:D_SSM], wo_a], axis=0)
    out["w_ff1"] = bf(wf["w_ff1"])
    out["w_ff2"] = bf(wf["w_ff2"])
    return out


def _rope_tables(positions):
    inv_freq = ROPE_BASE ** (-jnp.arange(0, QK_ROPE, 2, dtype=F32) / QK_ROPE)
    ang = positions.astype(F32)[:, None] * inv_freq
    cos, sin = jnp.cos(ang), jnp.sin(ang)
    n = positions.shape[0]
    one = jnp.ones((n, QK_NOPE), F32)
    z16 = jnp.zeros((n, 16), F32)
    z32 = jnp.zeros((n, 32), F32)
    z64 = jnp.zeros((n, QK_NOPE), F32)
    rc = jnp.concatenate([one, cos, cos, z32], axis=1)
    rs1 = jnp.concatenate([z64, -sin, z16, z32], axis=1)
    rs2 = jnp.concatenate([z64, z16, sin, z32], axis=1)
    return rc, rs1, rs2


def _permute_rows(a, S):
    n, w = a.shape
    return a.reshape(n // S, 8, S // 8, w).transpose(0, 2, 1, 3).reshape(n, w)


def _unpermute_rows(a, S):
    n, w = a.shape
    return a.reshape(n // S, S // 8, 8, w).transpose(0, 2, 1, 3).reshape(n, w)


def _block_diag_in(bb):
    eye = jnp.eye(G, dtype=bb.dtype)
    return jnp.einsum("gph,gk->ghkp", bb, eye).reshape(G * H, G * P)


def _block_diag_out(cc):
    eye = jnp.eye(G, dtype=cc.dtype)
    return jnp.einsum("ghp,gk->gpkh", cc, eye).reshape(G * P, G * H)


def _slots(full):
    r, cdim = full.shape
    return full.reshape(r, 4, cdim // 4).transpose(1, 0, 2)


def _unslots(g):
    s, r, cs = g.shape
    return g.transpose(1, 0, 2).reshape(r, s * cs)


def _local_step(x, positions, target, modp, wf, late_weights=None):
    nb, S, _ = x.shape
    n = nb * S
    tm = min(256, S)
    tt = min(256, S)
    tq = min(512, S // 2)
    kw = _prep_weights(wf)
    row = lambda a: a.reshape(1, -1).astype(F32)

    xf = x.reshape(n, D)
    tf = target.reshape(n, D)
    g1, g2, gf = row(wf["norm1_g"]), row(wf["norm2_g"]), row(wf["final_norm_g"])
    h1, proj = _f1_fwd(xf, modp, g1, kw["w_in"], S, tm)

    col = lambda a: a.reshape(NST, 1)
    lam_re, lam_im = col(wf["ssm_lambda_re"]), col(wf["ssm_lambda_im"])
    logdt = jnp.repeat(wf["ssm_log_dt"].reshape(G, 1), P, axis=1).reshape(NST, 1)
    b_re, b_im = wf["ssm_b_re"].reshape(NST, H), wf["ssm_b_im"].reshape(NST, H)
    lbr, lbi, bbr, bbi = _ssm_param_fwd(lam_re, lam_im, logdt, b_re, b_im)
    lre8 = jnp.broadcast_to(lbr.reshape(1, NST), (8, NST))
    lim8 = jnp.broadcast_to(lbi.reshape(1, NST), (8, NST))
    bm = jnp.concatenate([_block_diag_in(bbr.reshape(G, P, H)), _block_diag_in(bbi.reshape(G, P, H))],
                         axis=1).astype(BF16)
    cm = jnp.concatenate([_block_diag_out(wf["ssm_c_re"]), -_block_diag_out(wf["ssm_c_im"])], axis=0).astype(BF16)
    dvec = row(wf["ssm_d"])
    u_p = _permute_rows(proj[:, :D_SSM], S)
    fcr, fci = _ssm_local(u_p, bm, lre8, lim8, S, tt)
    st, ypre, z, gact, yssm_p = _ssm_fwd(u_p, fcr, fci, bm, cm, dvec, kw["w_glu"], lre8, lim8, S, tt)
    yssm = _unpermute_rows(yssm_p, S)

    rc, rs1, rs2 = _rope_tables(positions.reshape(n))
    gq, gkv = row(wf["q_norm_g"]), row(wf["kv_norm_g"])
    q, k, v, qn, kvn = _mla_fwd(proj, rc, rs1, rs2, gq, gkv, kw["w_uq"], kw["w_ukv"], tm)
    oattn, lrow = _attn_fwd(q, k, v, S, tq)

    gs = row(wf["ssm_out_g"])
    ga = jnp.pad(wf["attn_out_g"].reshape(NH, V_HEAD), ((0, 0), (0, HP - V_HEAD))).reshape(1, NH * HP)
    kw.update(_prep_late_weights(late_weights(oattn) if late_weights is not None else wf))
    yn, o, x1, h2 = _p1_fwd(yssm, oattn, xf, modp, gs, ga, kw["w_out"], g2, S, tm)
    dx1, r, da, dff, accs2, accg2 = _p2(x1, h2, tf, modp, g2, gf, kw["w_ff1"], kw["w_ff2"], S, tm)
    loss = jnp.sum(accg2[2])
    do, dyssm, dob, drow, accs3, accg3 = _p3_bwd(dx1, o, yssm, oattn, modp, gs, ga, kw["w_out"], S, tm)

    dq, dk, dv = _attn_bwd(q, k, v, dob, lrow, drow, S, tq)
    dmla, dqb, dkvb, accm = _mla_bwd(dq, dk, dv, proj, rc, rs1, rs2, gq, gkv, kw["w_uq"], kw["w_ukv"], tm)

    dys_p = _permute_rows(dyssm, S)
    dy, dz, air, aii = _ssm_bwd_a(dys_p, z, ypre, kw["w_glu"], cm, lre8, lim8, S, tt)
    du_p, dcm, dbm, dd, dlr, dli = _ssm_bwd_b(dy, u_p, st, fcr, fci, air, aii, bm, cm, dvec, lre8, lim8, S, tt)
    du = _unpermute_rows(du_p, S)
    dcm = dcm.reshape(2, 4, 8, P, 8, H)
    dc_re = jnp.einsum("qgpgh->qghp", dcm[0]).reshape(G, H, P)
    dc_im = -jnp.einsum("qgpgh->qghp", dcm[1]).reshape(G, H, P)
    dbm = dbm.reshape(8, H, 2, 4, 8, P)
    dbb_re = jnp.einsum("ghqgp->qgph", dbm[:, :, 0]).reshape(NST, H)
    dbb_im = jnp.einsum("ghqgp->qgph", dbm[:, :, 1]).reshape(NST, H)
    gb_re, gb_im, glr, gli, gdt = _ssm_param_bwd(lam_re, lam_im, logdt, b_re, b_im, dlr.reshape(NST, 1),
                                                 dli.reshape(NST, 1), dbb_re, dbb_im)
    glogdt = _rowsum(gdt.reshape(G, P))

    dx, dproj, accs1, accg1 = _f1_bwd(du, dmla, dx1, xf, modp, g1, kw["w_in"], S, tm)

    big = {}
    big["w_in"] = _slots(_wgrad(h1, dproj, "wgrad_in")[:, :IN_COLS])
    big["w_glu"] = _wgrad(gact, dz, "wgrad_glu", col_slots=4)
    big["w_uq"] = _slots(_unpad_heads_cols(_wgrad(qn, dqb, "wgrad_uq"), QK_NOPE + QK_ROPE).reshape(Q_LORA, -1))
    gkvw = _wgrad(kvn, dkvb, "wgrad_ukv")
    big["w_ukv"] = _slots(jnp.concatenate([_unpad_heads_cols(gkvw[:, :NH * HP], QK_NOPE),
                                           _unpad_heads_cols(gkvw[:, NH * HP:], V_HEAD)], axis=2).reshape(KV_LORA, -1))
    gwo = _wgrad(yn, do, "wgrad_out")
    big["w_out"] = jnp.concatenate([gwo[:D_SSM].reshape(2, D_SSM // 2, D),
                                    gwo[D_SSM:].reshape(2, NH // 2 * HP, D).reshape(2, NH // 2, HP, D)[:, :, :V_HEAD]
                                    .reshape(2, D_ATTN // 2, D)], axis=0)
    big["w_ff1"] = _wgrad(h2, da, "wgrad_ff1", col_slots=4)
    big["w_ff2"] = _wgrad(r, dff, "wgrad_ff2").reshape(4, D_FF // 4, D)

    small = {}
    small["norm1_g"] = accg1[0:1]
    small["norm2_g"] = accg2[0:1]
    small["final_norm_g"] = accg2[1:2]
    small["ssm_out_g"] = accg3[0:1, :D_SSM]
    small["attn_out_g"] = accg3[1].reshape(NH, HP)[:, :V_HEAD].reshape(1, D_ATTN)
    small["q_norm_g"] = accm[0:1, :Q_LORA]
    small["kv_norm_g"] = accm[1:2, :KV_LORA]
    small["ssm_lambda_re"] = glr.reshape(G, P)
    small["ssm_lambda_im"] = gli.reshape(G, P)
    small["ssm_b_re"] = gb_re
    small["ssm_b_im"] = gb_im
    small["ssm_c_re"] = dc_re.reshape(G * H, P)
    small["ssm_c_im"] = dc_im.reshape(G * H, P)
    small["ssm_d"] = dd.reshape(G, H)
    small["ssm_log_dt"] = glogdt.reshape(1, G)
    return loss, dx.reshape(nb, S, D), big, small, accs1 + accs2 + accs3


def _view2d(a):
    return a.reshape(-1, a.shape[-1]) if a.ndim > 1 else a.reshape(1, -1)


def kernel(x, c, positions, ada_w, ada_b, norm1_g, w_in, ssm_lambda_re, ssm_lambda_im, ssm_b_re, ssm_b_im, ssm_c_re, ssm_c_im, ssm_d, ssm_log_dt, w_glu, q_norm_g, w_uq, kv_norm_g, w_ukv, ssm_out_g, attn_out_g, w_out, norm2_g, w_ff1, w_ff2, final_ada_w, final_ada_b, final_norm_g, loss_target, m_ada_w, m_ada_b, m_norm1_g, m_w_in, m_ssm_lambda_re, m_ssm_lambda_im, m_ssm_b_re, m_ssm_b_im, m_ssm_c_re, m_ssm_c_im, m_ssm_d, m_ssm_log_dt, m_w_glu, m_q_norm_g, m_w_uq, m_kv_norm_g, m_w_ukv, m_ssm_out_g, m_attn_out_g, m_w_out, m_norm2_g, m_w_ff1, m_w_ff2, m_final_ada_w, m_final_ada_b, m_final_norm_g, v_ada_w, v_ada_b, v_norm1_g, v_w_in, v_ssm_lambda_re, v_ssm_lambda_im, v_ssm_b_re, v_ssm_b_im, v_ssm_c_re, v_ssm_c_im, v_ssm_d, v_ssm_log_dt, v_w_glu, v_q_norm_g, v_w_uq, v_kv_norm_g, v_w_ukv, v_ssm_out_g, v_attn_out_g, v_w_out, v_norm2_g, v_w_ff1, v_w_ff2, v_final_ada_w, v_final_ada_b, v_final_norm_g):
    args = dict(locals())
    names = list(inspect.signature(kernel).parameters)
    wnames = names[3:names.index("loss_target")]
    small_names = [nm for nm in wnames if nm not in GATHERED and nm not in TP]
    reduced_names = [nm for nm in small_names if nm not in ("ada_b", "final_ada_b")]
    w = {nm: args[nm] for nm in wnames}
    m = {nm: args["m_" + nm] for nm in wnames}
    v = {nm: args["v_" + nm] for nm in wnames}
    nb = x.shape[0]
    xi, yi, ci = lax.axis_index("x"), lax.axis_index("y"), lax.axis_index("c")
    chip, me = 2 * xi + yi, 4 * xi + 2 * yi + ci

    unslot = lambda nm, g: g.reshape(-1, g.shape[-1]) if nm in ROW_SHARDED else _unslots(g)
    early = [nm for nm in GATHERED if nm not in LATE]
    got = _gather_chips("gather_weights", [_view2d(w[nm]).astype(BF16) for nm in early], [c])
    wf = {nm: unslot(nm, g) for nm, g in zip(early, got)}
    for nm in small_names:
        wf[nm] = w[nm][0] if w[nm].ndim > 1 else w[nm]
    c_all = got[len(early)].reshape(8 * nb, D)

    na, nf = ada_w.shape[-1], final_ada_w.shape[-1]
    ada_b_s = lax.dynamic_slice(ada_b, (0, chip * na), (1, na))
    fada_b_s = lax.dynamic_slice(final_ada_b.reshape(1, -1), (0, chip * nf), (1, nf))
    cond_all, modcols = _mod_fwd(c_all, ada_w[0], ada_b_s, final_ada_w, fada_b_s)
    (mod_g,) = _gather_chips("gather_mod", [modcols])
    mine = lax.dynamic_slice(mod_g, (0, me * nb, 0), (4, nb, na + nf))
    modp = jnp.concatenate([mine[:, :, :na].transpose(1, 0, 2).reshape(nb, 6, D),
                            mine[:, :, na:].transpose(1, 0, 2).reshape(nb, 2, D)], axis=1)

    own_late = [_view2d(w[nm]).astype(BF16) for nm in LATE]
    sems, srcs, lands, token = _gather_start(own_late, modp)
    modp = modp + token[0, 0]

    def late_weights(after):
        landed = _gather_wait(sems, srcs, lands, after)
        return {nm: unslot(nm, lax.dynamic_update_slice(g, own[None], (chip, 0, 0)))
                for nm, g, own in zip(LATE, landed, own_late)}

    loss, grad_x, big, small, dmodp = _local_step(x, positions, loss_target, modp, wf, late_weights)
    loss = lax.psum(loss, ("x", "y", "c"))

    sizes = [small[nm].size for nm in reduced_names]
    pad = -sum(sizes) % 128
    packed = jnp.concatenate([small[nm].reshape(1, -1) for nm in reduced_names] + [jnp.zeros((1, pad), F32)], axis=1)
    swapped = _swap_halves([big[nm] for nm in GATHERED], [dmodp.reshape(nb, 8 * D), packed])
    cidx = ci.astype(jnp.int32).reshape(1)
    chip_sums = [_add_half(big[nm], r, cidx, "grad_add_sibling_" + nm) for nm, r in zip(GATHERED, swapped)]
    halves = [_add_chips(r, "grad_add_chips_" + nm) for nm, r in zip(GATHERED, _scatter_chips(chip_sums))]
    others = _join_halves(halves)
    grads = {}
    dmod_all = swapped[len(GATHERED)].reshape(8 * nb, 8 * D)
    small_sum = _sum_devices(swapped[len(GATHERED) + 1].reshape(8, -1))
    off = 0
    for nm, sz in zip(reduced_names, sizes):
        grads[nm] = small_sum[:, off:off + sz].reshape(small[nm].shape)
        off += sz

    dsl = jnp.concatenate([lax.dynamic_slice(dmod_all, (0, chip * na), (8 * nb, na)),
                           lax.dynamic_slice(dmod_all, (0, 6 * D + chip * nf), (8 * nb, nf))], axis=1)
    gw, gb = _mod_bwd(cond_all.T, dsl, dmod_all)
    grads["ada_w"], grads["final_ada_w"] = gw[:, :na], gw[:, na:]
    grads["ada_b"], grads["final_ada_b"] = gb[:, :6 * D], gb[:, 6 * D:]

    delta, new_m, new_v = {}, {}, {}
    for nm, mine_h, other_h in zip(GATHERED, halves, others):
        grads[nm], delta[nm], new_m[nm], new_v[nm] = _adamw_halves(
            _view2d(w[nm]), mine_h, other_h, _view2d(m[nm]), _view2d(v[nm]), cidx, "adamw_" + nm)
    for nm in TP:
        delta[nm], new_m[nm], new_v[nm] = _adamw(_view2d(w[nm]), grads[nm], _view2d(m[nm]), _view2d(v[nm]),
                                                  "adamw_" + nm)
    upd = _adamw_small([_view2d(w[nm]) for nm in small_names], [grads[nm] for nm in small_names],
                       [_view2d(m[nm]) for nm in small_names], [_view2d(v[nm]) for nm in small_names])
    k = len(small_names)
    for t, nm in enumerate(small_names):
        delta[nm], new_m[nm], new_v[nm] = upd[t], upd[k + t], upd[2 * k + t]

    outs = [grads, delta, new_m, new_v]
    return (loss, grad_x, *[d[nm].reshape(w[nm].shape) for d in outs for nm in wnames])
```

```python
import functools
import inspect
import math

import jax
import jax.numpy as jnp
from jax import lax
from jax.experimental import pallas as pl
from jax.experimental.pallas import tpu as pltpu

F32 = jnp.float32
BF16 = jnp.bfloat16

D = 1024
D_SSM = 512
G = 32
H = 16
P = 64
NST = G * P
D_ATTN = 512
NH = 8
QK_NOPE = 64
QK_ROPE = 32
V_HEAD = 64
HP = 128
Q_LORA = 384
KV_LORA = 256
IN_COLS = D_SSM + Q_LORA + KV_LORA + QK_ROPE
IN_PAD = 1280
D_FF = 4096
ROPE_BASE = 10000.0
EPS = 1e-6
ADAM_LR = 0.001
ADAM_B1 = 0.9
ADAM_B2 = 0.999
ADAM_EPS = 1e-08
ADAM_WD = 0.01
ADAM_STEP = 10
NEG = -1e30
VMEM_LIMIT = 60 << 20

MESH = pl.DeviceIdType.MESH
_VM = pl.BlockSpec(memory_space=pltpu.VMEM)
_ANY = pl.BlockSpec(memory_space=pl.ANY)

GATHERED = ["w_in", "w_glu", "w_uq", "w_ukv", "w_out", "w_ff1", "w_ff2"]
TP = ["ada_w", "final_ada_w"]
ROW_SHARDED = ("w_out", "w_ff2")
LATE = ["w_out", "w_ff1", "w_ff2"]


def _cp(sem=None, vmem=VMEM_LIMIT):
    kw = dict(vmem_limit_bytes=vmem)
    if sem is not None:
        kw["dimension_semantics"] = sem
    return pltpu.CompilerParams(**kw)


def _dot(a, b):
    return jnp.dot(a, b, preferred_element_type=F32)


def _dot_nt(a, b):
    return lax.dot_general(a, b, (((1,), (1,)), ((), ())), preferred_element_type=F32)


def _dot_tn(a, b):
    return lax.dot_general(a, b, (((0,), (0,)), ((), ())), preferred_element_type=F32)


def _rms(x, n):
    r = lax.rsqrt(jnp.sum(x * x, axis=-1, keepdims=True) * (1.0 / n) + EPS)
    return x * r, r


def _rms_bwd(dyg, xhat, r, n):
    return r * (dyg - xhat * (jnp.sum(dyg * xhat, axis=-1, keepdims=True) * (1.0 / n)))


def _sigmoid(x):
    return 1.0 / (1.0 + jnp.exp(-x))


_GK = math.sqrt(2.0 / math.pi)
_GC = 0.044715


def _gelu(y):
    t = jnp.tanh(_GK * (y + _GC * y * y * y))
    return 0.5 * y * (1.0 + t)


def _gelu_grad(y):
    t = jnp.tanh(_GK * (y + _GC * y * y * y))
    return 0.5 * (1.0 + t) + 0.5 * y * (1.0 - t * t) * _GK * (1.0 + 3.0 * _GC * y * y)


def _colsum(x):
    return jnp.sum(x, axis=0, keepdims=True)


def _roll(x, s):
    return pltpu.roll(x, s % x.shape[-1], x.ndim - 1)


def _mod_fwd(c_all, ada_w_s, ada_b_s, fada_w_s, fada_b_s):
    nseq = c_all.shape[0]
    na, nf = ada_w_s.shape[1], fada_w_s.shape[1]

    def body(c_ref, w_ref, b_ref, fw_ref, fb_ref, cond_ref, mod_ref):
        cv = c_ref[...]
        cond = cv * _sigmoid(cv)
        cond_ref[...] = cond
        cb = cond.astype(BF16)
        mod_ref[:, 0:na] = _dot(cb, w_ref[...].astype(BF16)) + b_ref[...]
        mod_ref[:, na:na + nf] = _dot(cb, fw_ref[...].astype(BF16)) + fb_ref[...]

    return pl.pallas_call(
        body, name="mod_fwd",
        out_shape=[jax.ShapeDtypeStruct((nseq, D), F32), jax.ShapeDtypeStruct((nseq, na + nf), F32)],
        in_specs=[_VM] * 5, out_specs=[_VM] * 2, compiler_params=_cp(),
    )(c_all, ada_w_s, ada_b_s, fada_w_s, fada_b_s)


def _mod_bwd(cond_t, dsl, dall):
    nseq, n = dsl.shape
    bc = 512

    def body(ct_ref, dm_ref, da_ref, gw_ref, gb_ref):
        ct = ct_ref[...]
        dm = dm_ref[...]
        acc = ct[:, 0:1] * dm[0:1, :]
        for b in range(1, nseq):
            acc = acc + ct[:, b:b + 1] * dm[b:b + 1, :]
        gw_ref[...] = acc

        @pl.when(pl.program_id(0) == 0)
        def _():
            gb_ref[...] = _colsum(da_ref[...])

    return pl.pallas_call(
        body, name="mod_bwd", grid=(n // bc,),
        out_shape=[jax.ShapeDtypeStruct((D, n), F32), jax.ShapeDtypeStruct((1, dall.shape[1]), F32)],
        in_specs=[_VM, pl.BlockSpec((nseq, bc), lambda i: (0, i)), _VM],
        out_specs=[pl.BlockSpec((D, bc), lambda i: (0, i)), pl.BlockSpec((1, dall.shape[1]), lambda i: (0, 0))],
        compiler_params=_cp(("arbitrary",)),
    )(cond_t, dsl, dall)


def _f1_fwd(x, modp, g1, w_in, S, tm):
    n = x.shape[0]
    tps = S // tm

    def body(x_ref, mod_ref, g_ref, w_ref, h_ref, proj_ref):
        xhat, _ = _rms(x_ref[...], D)
        h = (xhat * g_ref[...]) * (1.0 + mod_ref[0, 1:2, :]) + mod_ref[0, 0:1, :]
        hb = h.astype(BF16)
        h_ref[...] = hb
        proj_ref[...] = _dot(hb, w_ref[...])

    return pl.pallas_call(
        body, name="f1_fwd", grid=(n // tm,),
        out_shape=[jax.ShapeDtypeStruct((n, D), BF16), jax.ShapeDtypeStruct((n, IN_PAD), F32)],
        in_specs=[pl.BlockSpec((tm, D), lambda i: (i, 0)),
                  pl.BlockSpec((1, 8, D), lambda i: (i // tps, 0, 0)), _VM, _VM],
        out_specs=[pl.BlockSpec((tm, D), lambda i: (i, 0)), pl.BlockSpec((tm, IN_PAD), lambda i: (i, 0))],
        compiler_params=_cp(("parallel",)),
    )(x, modp, g1, w_in)


def _f1_bwd(du, dmla, dx1, x, modp, g1, w_in, S, tm):
    n = x.shape[0]
    tps = S // tm
    nb = n // S

    def body(du_ref, dm_ref, dx1_ref, x_ref, mod_ref, g_ref, w_ref, dx_ref, dproj_ref, accs_ref, accg_ref):
        i = pl.program_id(0)
        dproj = jnp.concatenate([du_ref[...], dm_ref[...]], axis=1).astype(BF16)
        dproj_ref[...] = dproj
        dh = _dot_nt(dproj, w_ref[...])
        xhat, r = _rms(x_ref[...], D)
        g = g_ref[...]
        dn = dh * (1.0 + mod_ref[0, 1:2, :])
        dx_ref[...] = dx1_ref[...] + _rms_bwd(dn * g, xhat, r, D)

        @pl.when(i % tps == 0)
        def _():
            accs_ref[...] = jnp.zeros_like(accs_ref)

        @pl.when(i == 0)
        def _():
            accg_ref[...] = jnp.zeros_like(accg_ref)

        accs_ref[0, 0:1, :] += _colsum(dh)
        accs_ref[0, 1:2, :] += _colsum(dh * (xhat * g))
        accg_ref[0:1, :] += _colsum(dn * xhat)

    return pl.pallas_call(
        body, name="f1_bwd", grid=(n // tm,),
        out_shape=[jax.ShapeDtypeStruct((n, D), F32), jax.ShapeDtypeStruct((n, IN_PAD), BF16),
                   jax.ShapeDtypeStruct((nb, 8, D), F32), jax.ShapeDtypeStruct((8, D), F32)],
        in_specs=[pl.BlockSpec((tm, D_SSM), lambda i: (i, 0)), pl.BlockSpec((tm, IN_PAD - D_SSM), lambda i: (i, 0)),
                  pl.BlockSpec((tm, D), lambda i: (i, 0)), pl.BlockSpec((tm, D), lambda i: (i, 0)),
                  pl.BlockSpec((1, 8, D), lambda i: (i // tps, 0, 0)), _VM, _VM],
        out_specs=[pl.BlockSpec((tm, D), lambda i: (i, 0)), pl.BlockSpec((tm, IN_PAD), lambda i: (i, 0)),
                   pl.BlockSpec((1, 8, D), lambda i: (i // tps, 0, 0)), pl.BlockSpec((8, D), lambda i: (0, 0))],
        compiler_params=_cp(("arbitrary",)),
    )(du, dmla, dx1, x, modp, g1, w_in)


def _ssm_param_fwd(lam_re, lam_im, logdt, b_re, b_im):
    def body(lr_ref, li_ref, ld_ref, br_ref, bi_ref, lbr_ref, lbi_ref, bbr_ref, bbi_ref):
        lr, li = lr_ref[...], li_ref[...]
        dt = jnp.exp(ld_ref[...])
        er = jnp.exp(lr * dt)
        lbr = er * jnp.cos(li * dt)
        lbi = er * jnp.sin(li * dt)
        den = 1.0 / (lr * lr + li * li)
        cr = ((lbr - 1.0) * lr + lbi * li) * den
        ci = (lbi * lr - (lbr - 1.0) * li) * den
        lbr_ref[...] = lbr
        lbi_ref[...] = lbi
        bbr_ref[...] = cr * br_ref[...] - ci * bi_ref[...]
        bbi_ref[...] = cr * bi_ref[...] + ci * br_ref[...]

    return pl.pallas_call(
        body, name="ssm_param_fwd",
        out_shape=[jax.ShapeDtypeStruct((NST, 1), F32)] * 2 + [jax.ShapeDtypeStruct((NST, H), F32)] * 2,
        in_specs=[_VM] * 5, out_specs=[_VM] * 4, compiler_params=_cp(),
    )(lam_re, lam_im, logdt, b_re, b_im)


def _ssm_param_bwd(lam_re, lam_im, logdt, b_re, b_im, dlb_re, dlb_im, dbb_re, dbb_im):
    def body(lr_ref, li_ref, ld_ref, br_ref, bi_ref, dlr_ref, dli_ref, dbr_ref, dbi_ref,
             gbr_ref, gbi_ref, glr_ref, gli_ref, gdt_ref):
        lr, li = lr_ref[...], li_ref[...]
        dt = jnp.exp(ld_ref[...])
        er = jnp.exp(lr * dt)
        lbr = er * jnp.cos(li * dt)
        lbi = er * jnp.sin(li * dt)
        den = 1.0 / (lr * lr + li * li)
        nr, ni = lbr - 1.0, lbi
        cr = (nr * lr + ni * li) * den
        ci = (ni * lr - nr * li) * den
        br, bi = br_ref[...], bi_ref[...]
        dbr, dbi = dbr_ref[...], dbi_ref[...]
        gbr_ref[...] = cr * dbr + ci * dbi
        gbi_ref[...] = cr * dbi - ci * dbr
        gcr = jnp.sum(dbr * br + dbi * bi, axis=1, keepdims=True)
        gci = jnp.sum(dbi * br - dbr * bi, axis=1, keepdims=True)
        ilr, ili = lr * den, -li * den
        glbr = dlr_ref[...] + (gcr * ilr + gci * ili)
        glbi = dli_ref[...] + (gci * ilr - gcr * ili)
        qr = -(cr * ilr - ci * ili)
        qi = -(cr * ili + ci * ilr)
        glr = gcr * qr + gci * qi
        gli = gci * qr - gcr * qi
        glr = glr + dt * (glbr * lbr + glbi * lbi)
        gli = gli + dt * (glbi * lbr - glbr * lbi)
        wr = lr * lbr - li * lbi
        wi = lr * lbi + li * lbr
        glr_ref[...] = glr
        gli_ref[...] = gli
        gdt_ref[...] = (glbr * wr + glbi * wi) * dt

    return pl.pallas_call(
        body, name="ssm_param_bwd",
        out_shape=[jax.ShapeDtypeStruct((NST, H), F32)] * 2 + [jax.ShapeDtypeStruct((NST, 1), F32)] * 3,
        in_specs=[_VM] * 9, out_specs=[_VM] * 5, compiler_params=_cp(),
    )(lam_re, lam_im, logdt, b_re, b_im, dlb_re, dlb_im, dbb_re, dbb_im)


def _rowsum(a):
    def body(a_ref, o_ref):
        o_ref[...] = jnp.sum(a_ref[...], axis=1, keepdims=True)

    return pl.pallas_call(
        body, name="rowsum", out_shape=jax.ShapeDtypeStruct((a.shape[0], 1), F32),
        in_specs=[_VM], out_specs=_VM, compiler_params=_cp(),
    )(a)


QB = D_SSM // 4
QS = 4 * QB


def _bd_lo(part, q):
    return part * NST + q * QS


def _bd_expand(ub, bm_ref, out_ref):
    for part in range(2):
        for q in range(4):
            lo = _bd_lo(part, q)
            out_ref[:, lo:lo + QS] = _dot(ub[:, q * QB:(q + 1) * QB], bm_ref[:, lo:lo + QS])


def _bd_expand_t(db, cm_ref, out_ref):
    for part in range(2):
        for q in range(4):
            lo = _bd_lo(part, q)
            out_ref[:, lo:lo + QS] = _dot_nt(db[:, q * QB:(q + 1) * QB], cm_ref[lo:lo + QS, :])


def _bd_project(sb, cm_ref):
    return jnp.concatenate(
        [_dot(sb[:, _bd_lo(0, q):_bd_lo(0, q) + QS], cm_ref[_bd_lo(0, q):_bd_lo(0, q) + QS, :])
         + _dot(sb[:, _bd_lo(1, q):_bd_lo(1, q) + QS], cm_ref[_bd_lo(1, q):_bd_lo(1, q) + QS, :])
         for q in range(4)], axis=1)


def _bd_project_t(ab, bm_ref):
    return jnp.concatenate(
        [_dot_nt(ab[:, _bd_lo(0, q):_bd_lo(0, q) + QS], bm_ref[:, _bd_lo(0, q):_bd_lo(0, q) + QS])
         + _dot_nt(ab[:, _bd_lo(1, q):_bd_lo(1, q) + QS], bm_ref[:, _bd_lo(1, q):_bd_lo(1, q) + QS])
         for q in range(4)], axis=1)


def _pow2k(pr, pi, nsq):
    for _ in range(nsq):
        pr, pi = pr * pr - pi * pi, 2.0 * pr * pi
    return pr, pi


def _ssm_local(u_p, bm, lre8, lim8, S, tt):
    n = u_p.shape[0]
    nb, nt = n // S, S // tt
    nsq = int(round(math.log2(S // 8)))
    assert 2 ** nsq == S // 8

    def body(u_ref, bm_ref, lre_ref, lim_ref, cre_ref, cim_ref, sre, sim, bu):
        j = pl.program_id(1)

        @pl.when(j == 0)
        def _():
            sre[...] = jnp.zeros_like(sre)
            sim[...] = jnp.zeros_like(sim)

        _bd_expand(u_ref[...].astype(BF16), bm_ref, bu)
        lre, lim = lre_ref[...], lim_ref[...]

        def step(i, c):
            sr, si = c
            off = pl.multiple_of(i * 8, 8)
            br = bu[pl.ds(off, 8), 0:NST]
            bi = bu[pl.ds(off, 8), NST:2 * NST]
            return lre * sr - lim * si + br, lre * si + lim * sr + bi

        sr, si = lax.fori_loop(0, tt // 8, step, (sre[...], sim[...]))
        sre[...] = sr
        sim[...] = si

        @pl.when(j == nt - 1)
        def _():
            pr, pi = _pow2k(lre[0:1], lim[0:1], nsq)
            cr = jnp.zeros((1, NST), F32)
            ci = jnp.zeros((1, NST), F32)
            cre_ref[0:1, :] = cr
            cim_ref[0:1, :] = ci
            for k in range(1, 8):
                cr, ci = sr[k - 1:k] + pr * cr - pi * ci, si[k - 1:k] + pr * ci + pi * cr
                cre_ref[k:k + 1, :] = cr
                cim_ref[k:k + 1, :] = ci

    return pl.pallas_call(
        body, name="ssm_local", grid=(nb, nt),
        out_shape=[jax.ShapeDtypeStruct((nb * 8, NST), F32)] * 2,
        in_specs=[pl.BlockSpec((tt, D_SSM), lambda b, j: (b * nt + j, 0)), _VM, _VM, _VM],
        out_specs=[pl.BlockSpec((8, NST), lambda b, j: (b, 0))] * 2,
        scratch_shapes=[pltpu.VMEM((8, NST), F32), pltpu.VMEM((8, NST), F32), pltpu.VMEM((tt, 2 * NST), F32)],
        compiler_params=_cp(("arbitrary", "arbitrary")),
    )(u_p, bm, lre8, lim8)


def _ssm_fwd(u_p, cre, cim, bm, cm, dvec, w_glu, lre8, lim8, S, tt):
    n = u_p.shape[0]
    nb, nt = n // S, S // tt

    def body(u_ref, cre_ref, cim_ref, bm_ref, cm_ref, d_ref, wg_ref, lre_ref, lim_ref,
             st_ref, ypre_ref, z_ref, gact_ref, yssm_ref, sre, sim, bu):
        j = pl.program_id(1)

        @pl.when(j == 0)
        def _():
            sre[...] = cre_ref[...]
            sim[...] = cim_ref[...]

        u = u_ref[...]
        _bd_expand(u.astype(BF16), bm_ref, bu)
        lre, lim = lre_ref[...], lim_ref[...]

        def step(i, c):
            sr, si = c
            off = pl.multiple_of(i * 8, 8)
            nr = lre * sr - lim * si + bu[pl.ds(off, 8), 0:NST]
            ni = lre * si + lim * sr + bu[pl.ds(off, 8), NST:2 * NST]
            st_ref[pl.ds(off, 8), 0:NST] = nr
            st_ref[pl.ds(off, 8), NST:2 * NST] = ni
            return nr, ni

        sr, si = lax.fori_loop(0, tt // 8, step, (sre[...], sim[...]))
        sre[...] = sr
        sim[...] = si
        y = _bd_project(st_ref[...].astype(BF16), cm_ref) + d_ref[...] * u
        ypre_ref[...] = y
        gb = _gelu(y).astype(BF16)
        gact_ref[...] = gb
        z = _dot(gb, wg_ref[...])
        z_ref[...] = z
        yssm_ref[...] = z[:, 0:D_SSM] * _sigmoid(z[:, D_SSM:2 * D_SSM])

    row = lambda w: pl.BlockSpec((tt, w), lambda b, j: (b * nt + j, 0))
    return pl.pallas_call(
        body, name="ssm_fwd", grid=(nb, nt),
        out_shape=[jax.ShapeDtypeStruct((n, 2 * NST), F32), jax.ShapeDtypeStruct((n, D_SSM), F32),
                   jax.ShapeDtypeStruct((n, 2 * D_SSM), F32), jax.ShapeDtypeStruct((n, D_SSM), BF16),
                   jax.ShapeDtypeStruct((n, D_SSM), F32)],
        in_specs=[row(D_SSM), pl.BlockSpec((8, NST), lambda b, j: (b, 0)), pl.BlockSpec((8, NST), lambda b, j: (b, 0)),
                  _VM, _VM, _VM, _VM, _VM, _VM],
        out_specs=[row(2 * NST), row(D_SSM), row(2 * D_SSM), row(D_SSM), row(D_SSM)],
        scratch_shapes=[pltpu.VMEM((8, NST), F32), pltpu.VMEM((8, NST), F32), pltpu.VMEM((tt, 2 * NST), F32)],
        compiler_params=_cp(("arbitrary", "arbitrary")),
    )(u_p, cre, cim, bm, cm, dvec, w_glu, lre8, lim8)


def _ssm_bwd_a(dys_p, z, ypre, w_glu, cm, lre8, lim8, S, tt):
    n = z.shape[0]
    nb, nt = n // S, S // tt
    nsq = int(round(math.log2(S // 8)))
    ng = tt // 8

    def body(dys_ref, z_ref, y_ref, wg_ref, cm_ref, lre_ref, lim_ref, dy_ref, dz_ref, are_ref, aim_ref, sre, sim, gb):
        j = pl.program_id(1)

        @pl.when(j == 0)
        def _():
            sre[...] = jnp.zeros_like(sre)
            sim[...] = jnp.zeros_like(sim)

        z = z_ref[...]
        z1, z2 = z[:, 0:D_SSM], z[:, D_SSM:2 * D_SSM]
        sg = _sigmoid(z2)
        dys = dys_ref[...]
        dz = jnp.concatenate([dys * sg, dys * z1 * sg * (1.0 - sg)], axis=1).astype(BF16)
        dz_ref[...] = dz
        dy = _dot_nt(dz, wg_ref[...]) * _gelu_grad(y_ref[...])
        dy_ref[...] = dy
        _bd_expand_t(dy.astype(BF16), cm_ref, gb)
        lre, lim = lre_ref[...], lim_ref[...]

        def step(i, c):
            ar, ai = c
            off = pl.multiple_of((ng - 1 - i) * 8, 8)
            gr = gb[pl.ds(off, 8), 0:NST]
            gi = gb[pl.ds(off, 8), NST:2 * NST]
            return lre * ar + lim * ai + gr, lre * ai - lim * ar + gi

        ar, ai = lax.fori_loop(0, ng, step, (sre[...], sim[...]))
        sre[...] = ar
        sim[...] = ai

        @pl.when(j == nt - 1)
        def _():
            pr, pi = _pow2k(lre[0:1], -lim[0:1], nsq)
            cr = jnp.zeros((1, NST), F32)
            ci = jnp.zeros((1, NST), F32)
            are_ref[7:8, :] = cr
            aim_ref[7:8, :] = ci
            for k in range(6, -1, -1):
                cr, ci = ar[k + 1:k + 2] + pr * cr - pi * ci, ai[k + 1:k + 2] + pr * ci + pi * cr
                are_ref[k:k + 1, :] = cr
                aim_ref[k:k + 1, :] = ci

    row = lambda w: pl.BlockSpec((tt, w), lambda b, j: (b * nt + nt - 1 - j, 0))
    return pl.pallas_call(
        body, name="ssm_bwd_a", grid=(nb, nt),
        out_shape=[jax.ShapeDtypeStruct((n, D_SSM), F32), jax.ShapeDtypeStruct((n, 2 * D_SSM), BF16),
                   jax.ShapeDtypeStruct((nb * 8, NST), F32), jax.ShapeDtypeStruct((nb * 8, NST), F32)],
        in_specs=[row(D_SSM), row(2 * D_SSM), row(D_SSM), _VM, _VM, _VM, _VM],
        out_specs=[row(D_SSM), row(2 * D_SSM), pl.BlockSpec((8, NST), lambda b, j: (b, 0)),
                   pl.BlockSpec((8, NST), lambda b, j: (b, 0))],
        scratch_shapes=[pltpu.VMEM((8, NST), F32), pltpu.VMEM((8, NST), F32), pltpu.VMEM((tt, 2 * NST), F32)],
        compiler_params=_cp(("arbitrary", "arbitrary")),
    )(dys_p, z, ypre, w_glu, cm, lre8, lim8)


def _ssm_bwd_b(dy, u_p, st, fcr, fci, air, aii, bm, cm, dvec, lre8, lim8, S, tt):
    n = u_p.shape[0]
    nb, nt = n // S, S // tt
    ng = tt // 8

    def body(dy_ref, u_ref, st_ref, stp_ref, fcr_ref, fci_ref, air_ref, aii_ref, bm_ref, cm_ref, d_ref, lre_ref, lim_ref,
             du_ref, dcm_ref, dbm_ref, dd_ref, dlr_ref, dli_ref, are, aim, accr, acci, sp, ab):
        b = pl.program_id(0)
        j = pl.program_id(1)
        jt = nt - 1 - j

        @pl.when((b == 0) & (j == 0))
        def _():
            dcm_ref[...] = jnp.zeros_like(dcm_ref)
            dbm_ref[...] = jnp.zeros_like(dbm_ref)
            dd_ref[...] = jnp.zeros_like(dd_ref)
            accr[...] = jnp.zeros_like(accr)
            acci[...] = jnp.zeros_like(acci)

        @pl.when(j == 0)
        def _():
            are[...] = air_ref[...]
            aim[...] = aii_ref[...]

        sp[8:tt + 8, :] = st_ref[...]

        @pl.when(jt == 0)
        def _():
            sp[0:8, 0:NST] = fcr_ref[...]
            sp[0:8, NST:2 * NST] = fci_ref[...]

        @pl.when(jt != 0)
        def _():
            sp[0:8, :] = stp_ref[...]

        dy = dy_ref[...]
        u = u_ref[...]
        dyb = dy.astype(BF16)
        _bd_expand_t(dyb, cm_ref, ab)
        lre, lim = lre_ref[...], lim_ref[...]

        def step(i, c):
            ar, ai = c
            off = pl.multiple_of((ng - 1 - i) * 8, 8)
            nr = lre * ar + lim * ai + ab[pl.ds(off, 8), 0:NST]
            ni = lre * ai - lim * ar + ab[pl.ds(off, 8), NST:2 * NST]
            ab[pl.ds(off, 8), 0:NST] = nr
            ab[pl.ds(off, 8), NST:2 * NST] = ni
            pr = sp[pl.ds(off, 8), 0:NST]
            pi = sp[pl.ds(off, 8), NST:2 * NST]
            accr[...] += nr * pr + ni * pi
            acci[...] += ni * pr - nr * pi
            return nr, ni

        ar, ai = lax.fori_loop(0, ng, step, (are[...], aim[...]))
        are[...] = ar
        aim[...] = ai
        a_b = ab[...].astype(BF16)
        du_ref[...] = _bd_project_t(a_b, bm_ref) + d_ref[...] * dy
        ub = u.astype(BF16)
        for q in range(4):
            for part in range(2):
                lo = part * NST + q * 4 * QB
                s_q = sp[8:tt + 8, lo:lo + 4 * QB].astype(BF16)
                dcm_ref[lo:lo + 4 * QB, :] += _dot_tn(s_q, dyb[:, q * QB:(q + 1) * QB])
                dbm_ref[:, lo:lo + 4 * QB] += _dot_tn(ub[:, q * QB:(q + 1) * QB], a_b[:, lo:lo + 4 * QB])
        dd_ref[...] += _colsum(dy * u)

        @pl.when((b == nb - 1) & (j == nt - 1))
        def _():
            dlr_ref[...] = _colsum(accr[...])
            dli_ref[...] = _colsum(acci[...])

    row = lambda w: pl.BlockSpec((tt, w), lambda b, j: (b * nt + nt - 1 - j, 0))
    seq8 = pl.BlockSpec((8, NST), lambda b, j: (b, 0))
    prev = pl.BlockSpec((8, 2 * NST), lambda b, j: (jnp.maximum((b * nt + nt - 1 - j) * ng - 1, 0), 0))
    const = lambda shape: pl.BlockSpec(shape, lambda b, j: (0, 0))
    return pl.pallas_call(
        body, name="ssm_bwd_b", grid=(nb, nt),
        out_shape=[jax.ShapeDtypeStruct((n, D_SSM), F32), jax.ShapeDtypeStruct((2 * NST, QB), F32),
                   jax.ShapeDtypeStruct((QB, 2 * NST), F32), jax.ShapeDtypeStruct((1, D_SSM), F32),
                   jax.ShapeDtypeStruct((1, NST), F32), jax.ShapeDtypeStruct((1, NST), F32)],
        in_specs=[row(D_SSM), row(D_SSM), row(2 * NST), prev, seq8, seq8, seq8, seq8, _VM, _VM, _VM, _VM, _VM],
        out_specs=[row(D_SSM), const((2 * NST, QB)), const((QB, 2 * NST)), const((1, D_SSM)),
                   const((1, NST)), const((1, NST))],
        scratch_shapes=[pltpu.VMEM((8, NST), F32)] * 4 + [pltpu.VMEM((tt + 8, 2 * NST), F32),
                                                          pltpu.VMEM((tt, 2 * NST), F32)],
        compiler_params=_cp(("arbitrary", "arbitrary")),
    )(dy, u_p, st, st, fcr, fci, air, aii, bm, cm, dvec, lre8, lim8)


def _rope(v, c, s1, s2):
    return v * c + _roll(v, -16) * s1 + _roll(v, 16) * s2


def _rope_t(dv, c, s1, s2):
    return dv * c + _roll(dv * s1, 16) + _roll(dv * s2, -16)


def _mla_fwd(proj, rc, rs1, rs2, gq, gkv, w_uq, w_ukv, tm):
    n = proj.shape[0]

    def body(ql_ref, kvl_ref, kr_ref, c_ref, s1_ref, s2_ref, gq_ref, gkv_ref, wq_ref, wkv_ref,
             q_ref, k_ref, v_ref, qn_ref, kvn_ref):
        c, s1, s2 = c_ref[...], s1_ref[...], s2_ref[...]
        qhat, _ = _rms(ql_ref[...], Q_LORA)
        qn = (qhat * gq_ref[...]).astype(BF16)
        qn_ref[...] = qn
        q = _dot(qn, wq_ref[...])
        q_ref[...] = _rope(q, jnp.tile(c, (1, NH)), jnp.tile(s1, (1, NH)), jnp.tile(s2, (1, NH))).astype(BF16)
        khat, _ = _rms(kvl_ref[...], KV_LORA)
        kvn = (khat * gkv_ref[...]).astype(BF16)
        kvn_ref[...] = kvn
        kv = _dot(kvn, wkv_ref[...])
        kr = _rope(_roll(kr_ref[...], 64), c, s1, s2)
        k_ref[...] = (kv[:, 0:NH * HP] + jnp.tile(kr, (1, NH))).astype(BF16)
        v_ref[...] = kv[:, NH * HP:2 * NH * HP].astype(BF16)

    def wrapped(proj_ref, *rest):
        ql = proj_ref.at[:, D_SSM:D_SSM + Q_LORA]
        kvl = proj_ref.at[:, D_SSM + Q_LORA:D_SSM + Q_LORA + KV_LORA]
        kr = proj_ref.at[:, IN_PAD - HP:IN_PAD]
        body(ql, kvl, kr, *rest)

    row = lambda w: pl.BlockSpec((tm, w), lambda i: (i, 0))
    return pl.pallas_call(
        wrapped, name="mla_fwd", grid=(n // tm,),
        out_shape=[jax.ShapeDtypeStruct((n, NH * HP), BF16)] * 3 +
                  [jax.ShapeDtypeStruct((n, Q_LORA), BF16), jax.ShapeDtypeStruct((n, KV_LORA), BF16)],
        in_specs=[row(IN_PAD), row(HP), row(HP), row(HP), _VM, _VM, _VM, _VM],
        out_specs=[row(NH * HP)] * 3 + [row(Q_LORA), row(KV_LORA)],
        compiler_params=_cp(("parallel",)),
    )(proj, rc, rs1, rs2, gq, gkv, w_uq, w_ukv)


def _mla_bwd(dq, dk, dv, proj, rc, rs1, rs2, gq, gkv, w_uq, w_ukv, tm):
    n = proj.shape[0]

    def body(dq_ref, dk_ref, dv_ref, proj_ref, c_ref, s1_ref, s2_ref, gq_ref, gkv_ref, wq_ref, wkv_ref,
             dmla_ref, dqb_ref, dkvb_ref, acc_ref):
        i = pl.program_id(0)
        c, s1, s2 = c_ref[...], s1_ref[...], s2_ref[...]
        dqu = _rope_t(dq_ref[...], jnp.tile(c, (1, NH)), jnp.tile(s1, (1, NH)), jnp.tile(s2, (1, NH))).astype(BF16)
        dqb_ref[...] = dqu
        dqn = _dot_nt(dqu, wq_ref[...])
        qhat, rq = _rms(proj_ref[:, D_SSM:D_SSM + Q_LORA], Q_LORA)
        dql = _rms_bwd(dqn * gq_ref[...], qhat, rq, Q_LORA)
        dkf = dk_ref[...]
        dkv = jnp.concatenate([dkf, dv_ref[...]], axis=1).astype(BF16)
        dkvb_ref[...] = dkv
        dkvn = _dot_nt(dkv, wkv_ref[...])
        khat, rk = _rms(proj_ref[:, D_SSM + Q_LORA:D_SSM + Q_LORA + KV_LORA], KV_LORA)
        dkvl = _rms_bwd(dkvn * gkv_ref[...], khat, rk, KV_LORA)
        dkr = dkf[:, 0:HP]
        for h in range(1, NH):
            dkr = dkr + dkf[:, h * HP:(h + 1) * HP]
        lane = lax.broadcasted_iota(jnp.int32, dkr.shape, 1)
        dkr = jnp.where((lane >= QK_NOPE) & (lane < QK_NOPE + QK_ROPE), dkr, 0.0)
        dkr = _roll(_rope_t(dkr, c, s1, s2), -64)
        dmla_ref[...] = jnp.concatenate([dql, dkvl, dkr], axis=1)

        @pl.when(i == 0)
        def _():
            acc_ref[...] = jnp.zeros_like(acc_ref)

        acc_ref[0:1, 0:Q_LORA] += _colsum(dqn * qhat)
        acc_ref[1:2, 0:KV_LORA] += _colsum(dkvn * khat)

    row = lambda w: pl.BlockSpec((tm, w), lambda i: (i, 0))
    return pl.pallas_call(
        body, name="mla_bwd", grid=(n // tm,),
        out_shape=[jax.ShapeDtypeStruct((n, IN_PAD - D_SSM), F32), jax.ShapeDtypeStruct((n, NH * HP), BF16),
                   jax.ShapeDtypeStruct((n, 2 * NH * HP), BF16), jax.ShapeDtypeStruct((8, Q_LORA), F32)],
        in_specs=[row(NH * HP)] * 3 + [row(IN_PAD), row(HP), row(HP), row(HP), _VM, _VM, _VM, _VM],
        out_specs=[row(IN_PAD - D_SSM), row(NH * HP), row(2 * NH * HP), pl.BlockSpec((8, Q_LORA), lambda i: (0, 0))],
        compiler_params=_cp(("arbitrary",)),
    )(dq, dk, dv, proj, rc, rs1, rs2, gq, gkv, w_uq, w_ukv)


_SCALE = (QK_NOPE + QK_ROPE) ** -0.5
_LOG2E = 1.4426950408889634
_C2 = _SCALE * _LOG2E


def _attn_fwd(q, k, v, S, tq):
    n = q.shape[0]
    nb, nq = n // S, S // tq

    def body(q_ref, k_ref, v_ref, o_ref, lr_ref):
        qi = pl.program_id(2)
        qv = q_ref[...]

        def tile(j, c, diagonal):
            m, l, acc = c
            off = pl.multiple_of(j * tq, tq)
            s = _dot_nt(qv, k_ref[pl.ds(off, tq), :]) * _C2
            if diagonal:
                rows = lax.broadcasted_iota(jnp.int32, s.shape, 0)
                cols = lax.broadcasted_iota(jnp.int32, s.shape, 1)
                s = jnp.where(cols <= rows, s, NEG)
            mn = jnp.maximum(m, jnp.max(s, axis=1, keepdims=True))
            p = jnp.exp2(s - mn)
            al = jnp.exp2(m - mn)
            l = al * l + jnp.sum(p, axis=1, keepdims=True)
            acc = al * acc + _dot(p.astype(BF16), v_ref[pl.ds(off, tq), :])
            return mn, l, acc

        init = (jnp.full((tq, 1), NEG, F32), jnp.zeros((tq, 1), F32), jnp.zeros((tq, HP), F32))
        c = lax.fori_loop(0, qi, lambda j, c: tile(j, c, False), init)
        m, l, acc = tile(qi, c, True)
        o_ref[...] = acc / l
        lane = lax.broadcasted_iota(jnp.int32, (8, HP), 1)
        lse = jnp.broadcast_to(m + jnp.log(l) * _LOG2E, (tq, HP))
        lr_ref[...] = _rows_of(lse, jnp.where(lane == 0, 1.0, 0.0).astype(BF16))

    qs = pl.BlockSpec((tq, HP), lambda b, h, i: (b * nq + i, h))
    ks = pl.BlockSpec((S, HP), lambda b, h, i: (b, h))
    return pl.pallas_call(
        body, name="attn_fwd", grid=(nb, NH, nq),
        out_shape=[jax.ShapeDtypeStruct((n, NH * HP), F32), jax.ShapeDtypeStruct((nb * NH * 8, S), F32)],
        in_specs=[qs, ks, ks], out_specs=[qs, pl.BlockSpec((8, tq), lambda b, h, i: (b * NH + h, i))],
        compiler_params=_cp(("parallel", "parallel", "arbitrary")),
    )(q, k, v)


def _rows_of(x, pick):
    x1 = x.astype(BF16)
    r1 = x - x1.astype(F32)
    x2 = r1.astype(BF16)
    x3 = (r1 - x2.astype(F32)).astype(BF16)
    return _dot_nt(pick, x1) + _dot_nt(pick, x2) + _dot_nt(pick, x3)


def _attn_bwd(q, k, v, dob, lrow, drow, S, tq):
    n = q.shape[0]
    nb, nq = n // S, S // tq

    def body(q_ref, k_ref, v_ref, do_ref, lr_ref, dr_ref, dq_ref, dk_ref, dv_ref):
        kj = pl.program_id(2)

        @pl.when(kj == 0)
        def _():
            dq_ref[...] = jnp.zeros_like(dq_ref)

        kt = k_ref[...]
        vt = v_ref[...]

        def tile(i, c, diagonal):
            dk, dv = c
            off = pl.multiple_of(i * tq, tq)
            qv = q_ref[pl.ds(off, tq), :]
            dob = do_ref[pl.ds(off, tq), :]
            lr = lr_ref[0:1, pl.ds(off, tq)]
            dr = dr_ref[0:1, pl.ds(off, tq)]
            st = _dot_nt(kt, qv)
            dpt = _dot_nt(vt, dob)
            pt = jnp.exp2(st * _C2 - lr)
            if diagonal:
                keys = lax.broadcasted_iota(jnp.int32, pt.shape, 0)
                qrys = lax.broadcasted_iota(jnp.int32, pt.shape, 1)
                pt = jnp.where(keys <= qrys, pt, 0.0)
            dst = (pt * (dpt * _SCALE - dr)).astype(BF16)
            dq_ref[pl.ds(off, tq), :] += _dot_tn(dst, kt)
            return dk + _dot(dst, qv), dv + _dot(pt.astype(BF16), dob)

        zero = jnp.zeros((tq, HP), F32)
        c = tile(kj, (zero, zero), True)
        dk, dv = lax.fori_loop(kj + 1, nq, lambda i, c: tile(i, c, False), c)
        dk_ref[...] = dk
        dv_ref[...] = dv

    ts = pl.BlockSpec((tq, HP), lambda b, h, i: (b * nq + i, h))
    fs = pl.BlockSpec((S, HP), lambda b, h, i: (b, h))
    rs = pl.BlockSpec((8, S), lambda b, h, i: (b * NH + h, 0))
    return pl.pallas_call(
        body, name="attn_bwd", grid=(nb, NH, nq),
        out_shape=[jax.ShapeDtypeStruct((n, NH * HP), F32)] * 3,
        in_specs=[fs, ts, ts, fs, rs, rs], out_specs=[fs, ts, ts],
        compiler_params=_cp(("parallel", "parallel", "arbitrary")),
    )(q, k, v, dob, lrow, drow)


def _p1_fwd(yssm, oattn, x, modp, gs, ga, w_out, g2, S, tm):
    n = x.shape[0]
    tps = S // tm

    def body(ys_ref, oa_ref, x_ref, mod_ref, gs_ref, ga_ref, w_ref, g2_ref, yn_ref, o_ref, x1_ref, h2_ref):
        yh, _ = _rms(ys_ref[...], D_SSM)
        ah, _ = _rms(oa_ref[...], D_ATTN)
        yn = jnp.concatenate([yh * gs_ref[...], ah * ga_ref[...]], axis=1).astype(BF16)
        yn_ref[...] = yn
        o = _dot(yn, w_ref[...])
        o_ref[...] = o
        x1 = x_ref[...] + mod_ref[0, 2:3, :] * o
        x1_ref[...] = x1
        xh, _ = _rms(x1, D)
        h2_ref[...] = ((xh * g2_ref[...]) * (1.0 + mod_ref[0, 4:5, :]) + mod_ref[0, 3:4, :]).astype(BF16)

    row = lambda w: pl.BlockSpec((tm, w), lambda i: (i, 0))
    return pl.pallas_call(
        body, name="p1_fwd", grid=(n // tm,),
        out_shape=[jax.ShapeDtypeStruct((n, D_SSM + NH * HP), BF16), jax.ShapeDtypeStruct((n, D), F32),
                   jax.ShapeDtypeStruct((n, D), F32), jax.ShapeDtypeStruct((n, D), BF16)],
        in_specs=[row(D_SSM), row(NH * HP), row(D), pl.BlockSpec((1, 8, D), lambda i: (i // tps, 0, 0)),
                  _VM, _VM, _VM, _VM],
        out_specs=[row(D_SSM + NH * HP), row(D), row(D), row(D)],
        compiler_params=_cp(("parallel",)),
    )(yssm, oattn, x, modp, gs, ga, w_out, g2)


def _p2(x1, h2, target, modp, g2, gf, w_ff1, w_ff2, S, tm):
    n = x1.shape[0]
    tps = S // tm
    nb = n // S

    def body(x1_ref, h2_ref, t_ref, mod_ref, g2_ref, gf_ref, w1_ref, w2_ref,
             dx1_ref, r_ref, da_ref, dff_ref, accs_ref, accg_ref):
        i = pl.program_id(0)
        sh2, sc2, gt2 = mod_ref[0, 3:4, :], mod_ref[0, 4:5, :], mod_ref[0, 5:6, :]
        fsh, fsc = mod_ref[0, 6:7, :], mod_ref[0, 7:8, :]
        x1 = x1_ref[...]
        a = _dot(h2_ref[...], w1_ref[...])
        ra = jnp.maximum(a, 0.0)
        rb = (ra * ra).astype(BF16)
        r_ref[...] = rb
        ff = _dot(rb, w2_ref[...])
        x2 = x1 + gt2 * ff
        x2h, rf = _rms(x2, D)
        gf_v = gf_ref[...]
        outn = x2h * gf_v
        err = outn * (1.0 + fsc) + fsh - t_ref[...]
        dout = err * (1.0 / D)
        doutn = dout * (1.0 + fsc)
        dx2 = _rms_bwd(doutn * gf_v, x2h, rf, D)
        dff = (gt2 * dx2).astype(BF16)
        dff_ref[...] = dff
        dr = _dot_nt(dff, w2_ref[...])
        da = (dr * (2.0 * ra)).astype(BF16)
        da_ref[...] = da
        dh2 = _dot_nt(da, w1_ref[...])
        x1h, r2 = _rms(x1, D)
        g2_v = g2_ref[...]
        dn2 = dh2 * (1.0 + sc2)
        dx1_ref[...] = dx2 + _rms_bwd(dn2 * g2_v, x1h, r2, D)

        @pl.when(i % tps == 0)
        def _():
            accs_ref[...] = jnp.zeros_like(accs_ref)

        @pl.when(i == 0)
        def _():
            accg_ref[...] = jnp.zeros_like(accg_ref)

        accs_ref[0, 3:4, :] += _colsum(dh2)
        accs_ref[0, 4:5, :] += _colsum(dh2 * (x1h * g2_v))
        accs_ref[0, 5:6, :] += _colsum(dx2 * ff)
        accs_ref[0, 6:7, :] += _colsum(dout)
        accs_ref[0, 7:8, :] += _colsum(dout * outn)
        accg_ref[0:1, :] += _colsum(dn2 * x1h)
        accg_ref[1:2, :] += _colsum(doutn * x2h)
        accg_ref[2:3, :] += _colsum(err * err) * (0.5 / D)

    row = lambda w: pl.BlockSpec((tm, w), lambda i: (i, 0))
    return pl.pallas_call(
        body, name="p2_mlp_loss", grid=(n // tm,),
        out_shape=[jax.ShapeDtypeStruct((n, D), F32), jax.ShapeDtypeStruct((n, D_FF), BF16),
                   jax.ShapeDtypeStruct((n, D_FF), BF16), jax.ShapeDtypeStruct((n, D), BF16),
                   jax.ShapeDtypeStruct((nb, 8, D), F32), jax.ShapeDtypeStruct((8, D), F32)],
        in_specs=[row(D), row(D), row(D), pl.BlockSpec((1, 8, D), lambda i: (i // tps, 0, 0)), _VM, _VM, _VM, _VM],
        out_specs=[row(D), row(D_FF), row(D_FF), row(D), pl.BlockSpec((1, 8, D), lambda i: (i // tps, 0, 0)),
                   pl.BlockSpec((8, D), lambda i: (0, 0))],
        compiler_params=_cp(("arbitrary",)),
    )(x1, h2, target, modp, g2, gf, w_ff1, w_ff2)


def _p3_bwd(dx1, o, yssm, oattn, modp, gs, ga, w_out, S, tm):
    n = dx1.shape[0]
    tps = S // tm
    nb = n // S

    def body(dx1_ref, o_ref, ys_ref, oa_ref, mod_ref, gs_ref, ga_ref, w_ref,
             do_ref, dys_ref, doa_ref, dr_ref, accs_ref, accg_ref):
        i = pl.program_id(0)
        dx1 = dx1_ref[...]
        dob = (mod_ref[0, 2:3, :] * dx1).astype(BF16)
        do_ref[...] = dob
        dyn = _dot_nt(dob, w_ref[...])
        yh, rs = _rms(ys_ref[...], D_SSM)
        oa = oa_ref[...]
        ah, ra = _rms(oa, D_ATTN)
        d1 = dyn[:, 0:D_SSM]
        d2 = dyn[:, D_SSM:D_SSM + NH * HP]
        dys_ref[...] = _rms_bwd(d1 * gs_ref[...], yh, rs, D_SSM)
        doa = _rms_bwd(d2 * ga_ref[...], ah, ra, D_ATTN)
        doa_ref[...] = doa.astype(BF16)
        prod = doa * oa * _SCALE
        ones = jnp.ones((8, HP), BF16)
        for h in range(NH):
            dr_ref[h * 8:(h + 1) * 8, :] = _rows_of(prod[:, h * HP:(h + 1) * HP], ones)

        @pl.when(i % tps == 0)
        def _():
            accs_ref[...] = jnp.zeros_like(accs_ref)

        @pl.when(i == 0)
        def _():
            accg_ref[...] = jnp.zeros_like(accg_ref)

        accs_ref[0, 2:3, :] += _colsum(dx1 * o_ref[...])
        accg_ref[0:1, 0:D_SSM] += _colsum(d1 * yh)
        accg_ref[1:2, :] += _colsum(d2 * ah)

    row = lambda w: pl.BlockSpec((tm, w), lambda i: (i, 0))
    return pl.pallas_call(
        body, name="p3_bwd", grid=(n // tm,),
        out_shape=[jax.ShapeDtypeStruct((n, D), BF16), jax.ShapeDtypeStruct((n, D_SSM), F32),
                   jax.ShapeDtypeStruct((n, NH * HP), BF16), jax.ShapeDtypeStruct((nb * NH * 8, S), F32),
                   jax.ShapeDtypeStruct((nb, 8, D), F32), jax.ShapeDtypeStruct((8, NH * HP), F32)],
        in_specs=[row(D), row(D), row(D_SSM), row(NH * HP), pl.BlockSpec((1, 8, D), lambda i: (i // tps, 0, 0)),
                  _VM, _VM, _VM],
        out_specs=[row(D), row(D_SSM), row(NH * HP), pl.BlockSpec((NH * 8, tm), lambda i: (i // tps, i % tps)),
                   pl.BlockSpec((1, 8, D), lambda i: (i // tps, 0, 0)), pl.BlockSpec((8, NH * HP), lambda i: (0, 0))],
        compiler_params=_cp(("arbitrary",)),
    )(dx1, o, yssm, oattn, modp, gs, ga, w_out)


def _wgrad(a, b, name, col_slots=0):
    n, k1 = a.shape
    k2 = b.shape[1]
    bn = next((b for b in (1024, 512) if n % b == 0), n)
    bk1 = next((b for b in (1024, 512) if k1 % b == 0), k1)
    bk2 = k2 // col_slots if col_slots else (1024 if (k2 % 1024 == 0) else k2)

    def body(a_ref, b_ref, o_ref):
        @pl.when(pl.program_id(2) == 0)
        def _():
            o_ref[...] = jnp.zeros_like(o_ref)

        o_ref[...] += _dot_tn(a_ref[...], b_ref[...]).reshape(o_ref.shape)

    if col_slots:
        out_shape = jax.ShapeDtypeStruct((col_slots, k1, bk2), F32)
        out_spec = pl.BlockSpec((1, bk1, bk2), lambda i, j, t: (j, i, 0))
    else:
        out_shape = jax.ShapeDtypeStruct((k1, k2), F32)
        out_spec = pl.BlockSpec((bk1, bk2), lambda i, j, t: (i, j))
    return pl.pallas_call(
        body, name=name, grid=(k1 // bk1, k2 // bk2, n // bn),
        out_shape=out_shape,
        in_specs=[pl.BlockSpec((bn, bk1), lambda i, j, t: (t, i)), pl.BlockSpec((bn, bk2), lambda i, j, t: (t, j))],
        out_specs=out_spec,
        compiler_params=_cp(("parallel", "parallel", "arbitrary")),
    )(a, b)


def _row_block(rows):
    if rows <= 256:
        return rows
    return next(b for b in (256, 192, 128, 64, 32, 16, 8) if rows % b == 0)


def _add_half(g, recv, cidx, name):
    _, rows2, w = g.shape
    rows = rows2 // 2
    br = _row_block(rows)
    nblk = rows // br

    def body(c_ref, g_ref, r_ref, o_ref):
        o_ref[...] = (g_ref[...] + r_ref[...]).astype(BF16)

    return pl.pallas_call(
        body, name=name,
        grid_spec=pltpu.PrefetchScalarGridSpec(
            num_scalar_prefetch=1, grid=(4, nblk),
            in_specs=[pl.BlockSpec((1, br, w), lambda s, i, c: (s, c[0] * nblk + i, 0)),
                      pl.BlockSpec((1, br, w), lambda s, i, c: (s, i, 0))],
            out_specs=pl.BlockSpec((1, br, w), lambda s, i, c: (s, i, 0))),
        out_shape=jax.ShapeDtypeStruct((4, rows, w), BF16),
        compiler_params=_cp(("parallel", "parallel")),
    )(cidx, g, recv)


def _add_chips(r, name):
    _, rows, w = r.shape
    br = _row_block(rows)

    def body(r_ref, o_ref):
        f = lambda k: r_ref[k].astype(F32)
        o_ref[...] = ((f(0) + f(1)) + f(2)) + f(3)

    return pl.pallas_call(
        body, name=name, grid=(rows // br,),
        out_shape=jax.ShapeDtypeStruct((rows, w), F32),
        in_specs=[pl.BlockSpec((4, br, w), lambda i: (0, i, 0))],
        out_specs=pl.BlockSpec((br, w), lambda i: (i, 0)),
        compiler_params=_cp(("parallel",)),
    )(r)


def _sum_devices(a):
    def body(a_ref, o_ref):
        acc = a_ref[0:1, :]
        for k in range(1, 8):
            acc = acc + a_ref[k:k + 1, :]
        o_ref[...] = acc

    return pl.pallas_call(
        body, name="small_grad_sum", out_shape=jax.ShapeDtypeStruct((1, a.shape[1]), F32),
        in_specs=[_VM], out_specs=_VM, compiler_params=_cp(),
    )(a)


def _adamw_math(wv, gv, mv, vv):
    m_new = ADAM_B1 * mv + (1.0 - ADAM_B1) * gv
    v_new = ADAM_B2 * vv + (1.0 - ADAM_B2) * (gv * gv)
    m_hat = m_new / (1.0 - ADAM_B1 ** ADAM_STEP)
    v_hat = v_new / (1.0 - ADAM_B2 ** ADAM_STEP)
    return -ADAM_LR * (m_hat / (jnp.sqrt(v_hat) + ADAM_EPS) + ADAM_WD * wv), m_new, v_new


def _adamw_small(ws, gs, ms, vs):
    k = len(ws)

    def body(*refs):
        ins, outs = refs[:4 * k], refs[4 * k:]
        for t in range(k):
            d, m_new, v_new = _adamw_math(ins[t][...], ins[k + t][...], ins[2 * k + t][...], ins[3 * k + t][...])
            outs[t][...] = d
            outs[k + t][...] = m_new
            outs[2 * k + t][...] = v_new

    shapes = [jax.ShapeDtypeStruct(w.shape, F32) for w in ws]
    return pl.pallas_call(
        body, name="adamw_small", out_shape=shapes * 3,
        in_specs=[_VM] * (4 * k), out_specs=[_VM] * (3 * k), compiler_params=_cp(),
    )(*ws, *gs, *ms, *vs)


def _adamw(w, g, m, v, name):
    rows, wd = w.shape
    br = _row_block(rows)

    def body(w_ref, g_ref, m_ref, v_ref, d_ref, nm_ref, nv_ref):
        d, m_new, v_new = _adamw_math(w_ref[...], g_ref[...], m_ref[...], v_ref[...])
        d_ref[...] = d
        nm_ref[...] = m_new
        nv_ref[...] = v_new

    spec = pl.BlockSpec((br, wd), lambda i: (i, 0))
    return pl.pallas_call(
        body, name=name, grid=(rows // br,),
        out_shape=[jax.ShapeDtypeStruct((rows, wd), F32)] * 3,
        in_specs=[spec] * 4, out_specs=[spec] * 3,
        compiler_params=_cp(("parallel",)),
    )(w, g, m, v)


def _adamw_halves(w, mine, other, m, v, cidx, name):
    rows, wd = w.shape
    h = rows // 2
    br = _row_block(h)
    nblk = h // br

    def body(c_ref, w_ref, a_ref, b_ref, m_ref, v_ref, g_ref, d_ref, nm_ref, nv_ref):
        gv = jnp.where(pl.program_id(0) == c_ref[0], a_ref[...], b_ref[...])
        d, m_new, v_new = _adamw_math(w_ref[...], gv, m_ref[...], v_ref[...])
        g_ref[...] = gv
        d_ref[...] = d
        nm_ref[...] = m_new
        nv_ref[...] = v_new

    full = pl.BlockSpec((br, wd), lambda hf, i, c: (hf * nblk + i, 0))
    half = pl.BlockSpec((br, wd), lambda hf, i, c: (i, 0))
    return pl.pallas_call(
        body, name=name,
        grid_spec=pltpu.PrefetchScalarGridSpec(
            num_scalar_prefetch=1, grid=(2, nblk),
            in_specs=[full, half, half, full, full], out_specs=[full] * 4),
        out_shape=[jax.ShapeDtypeStruct((rows, wd), F32)] * 4,
        compiler_params=_cp(("parallel", "parallel")),
    )(cidx, w, mine, other, m, v)


def _other_chips(x, y):
    return [(1 - x, y), (x, 1 - y), (1 - x, 1 - y)]


def _other_devices(x, y, c):
    flip = lambda v, d: (1 - v) if d else v
    return [(flip(x, dx), flip(y, dy), flip(c, dc))
            for dx in (0, 1) for dy in (0, 1) for dc in (0, 1) if (dx, dy, dc) != (0, 0, 0)]


def _exchange(name, ins, out_shapes, n_local, n_remote, plan):
    ni, no = len(ins), len(out_shapes)

    def body(*refs):
        in_refs, out_refs = refs[:ni], refs[ni:ni + no]
        send_sems, recv_sems, local_sems = refs[ni + no:]
        x, y, c = lax.axis_index("x"), lax.axis_index("y"), lax.axis_index("c")
        local, remote = plan(in_refs, out_refs, x, y, c)
        assert len(local) == n_local and len(remote) == n_remote

        def push(k, src, dst, dev):
            return pltpu.make_async_remote_copy(src_ref=src, dst_ref=dst, send_sem=send_sems.at[k],
                                                recv_sem=recv_sems.at[k], device_id=dev, device_id_type=MESH)

        own = [pltpu.make_async_copy(s, d, local_sems.at[i]) for i, (s, d) in enumerate(local)]
        for cp in own:
            cp.start()
        sends = [push(k, s, d, dev) for k, (s, d, dev, _) in enumerate(remote)]
        for cp in sends:
            cp.start()
        for k, (s, _, dev, landing) in enumerate(remote):
            push(k, s, landing, dev).wait_recv()
        for cp in sends:
            cp.wait_send()
        for cp in own:
            cp.wait()

    return pl.pallas_call(
        body, name=name, out_shape=out_shapes,
        in_specs=[_ANY] * ni, out_specs=[_ANY] * no,
        scratch_shapes=[pltpu.SemaphoreType.DMA((n_remote,)), pltpu.SemaphoreType.DMA((n_remote,)),
                        pltpu.SemaphoreType.DMA((max(n_local, 1),))],
        compiler_params=pltpu.CompilerParams(has_side_effects=True),
    )(*ins)


def _gather_chips(name, shards, everyone=()):
    ns, ne = len(shards), len(everyone)
    outs = [jax.ShapeDtypeStruct((4,) + a.shape, a.dtype) for a in shards]
    outs += [jax.ShapeDtypeStruct((8,) + a.shape, a.dtype) for a in everyone]

    def plan(i, o, x, y, c):
        mine, me = 2 * x + y, 4 * x + 2 * y + c
        local, remote = [], []
        for t in range(ns):
            local.append((i[t], o[t].at[mine]))
            for px, py in _other_chips(x, y):
                remote.append((i[t], o[t].at[mine], (px, py, c), o[t].at[2 * px + py]))
        for t in range(ns, ns + ne):
            local.append((i[t], o[t].at[me]))
            for px, py, pc in _other_devices(x, y, c):
                remote.append((i[t], o[t].at[me], (px, py, pc), o[t].at[4 * px + 2 * py + pc]))
        return local, remote

    return _exchange(name, list(shards) + list(everyone), outs, ns + ne, 3 * ns + 7 * ne, plan)


_HBM = pl.BlockSpec(memory_space=pltpu.HBM)
_SEM = pl.BlockSpec(memory_space=pltpu.SEMAPHORE)
_EFFECT = pltpu.SideEffectType.DATAFLOW_SIDE_EFFECTING


def _gather_start(shards, after):
    ns = len(shards)
    lands = [pltpu.with_memory_space_constraint(lax.empty((4,) + a.shape, a.dtype), pltpu.HBM) for a in shards]
    srcs = [pltpu.with_memory_space_constraint(a, pltpu.HBM) for a in shards]

    def body(*refs):
        src, land = refs[:ns], refs[ns:2 * ns]
        first = 2 * ns + 1
        send, recv = refs[first:first + 3 * ns], refs[first + 3 * ns:first + 6 * ns]
        token = refs[first + 8 * ns]
        x, y, c = lax.axis_index("x"), lax.axis_index("y"), lax.axis_index("c")
        mine = 2 * x + y
        for t in range(ns):
            for j, (px, py) in enumerate(_other_chips(x, y)):
                pltpu.make_async_remote_copy(src_ref=src[t], dst_ref=land[t].at[mine], send_sem=send[3 * t + j],
                                             recv_sem=recv[3 * t + j], device_id=(px, py, c),
                                             device_id_type=MESH).start()
        token[...] = jnp.zeros_like(token)

    out = pl.pallas_call(
        body, name="gather_late_start",
        out_shape=[pltpu.SemaphoreType.DMA(())] * (6 * ns)
                  + [pltpu.HBM(a.shape, a.dtype) for a in shards] + [pltpu.HBM((4,) + a.shape, a.dtype) for a in shards]
                  + [jax.ShapeDtypeStruct((8, 128), F32)],
        in_specs=[_HBM] * (2 * ns) + [_ANY], out_specs=[_SEM] * (6 * ns) + [_HBM] * (2 * ns) + [_VM],
        input_output_aliases={t: 6 * ns + t for t in range(2 * ns)},
        compiler_params=pltpu.CompilerParams(has_side_effects=_EFFECT),
    )(*srcs, *lands, after)
    return out[:6 * ns], out[6 * ns:7 * ns], out[7 * ns:8 * ns], out[8 * ns]


def _gather_wait(sems, srcs, lands, after):
    ns = len(srcs)

    def body(*refs):
        src, land = refs[:ns], refs[ns:2 * ns]
        send, recv = refs[2 * ns:5 * ns], refs[5 * ns:8 * ns]
        x, y, c = lax.axis_index("x"), lax.axis_index("y"), lax.axis_index("c")
        for t in range(ns):
            for j, (px, py) in enumerate(_other_chips(x, y)):
                cp = pltpu.make_async_remote_copy(src_ref=src[t], dst_ref=land[t].at[2 * px + py],
                                                  send_sem=send[3 * t + j], recv_sem=recv[3 * t + j],
                                                  device_id=(px, py, c), device_id_type=MESH)
                cp.wait_send()
                cp.wait_recv()

    out = pl.pallas_call(
        body, name="gather_late_wait",
        out_shape=[pltpu.HBM(a.shape, a.dtype) for a in srcs] + [pltpu.HBM(a.shape, a.dtype) for a in lands],
        in_specs=[_HBM] * (2 * ns) + [_SEM] * (6 * ns) + [_ANY], out_specs=[_HBM] * (2 * ns),
        input_output_aliases={t: t for t in range(2 * ns)},
        compiler_params=pltpu.CompilerParams(has_side_effects=_EFFECT),
    )(*srcs, *lands, *sems, after)
    return out[ns:]


def _swap_halves(gs, everyone):
    ns, ne = len(gs), len(everyone)
    outs = [jax.ShapeDtypeStruct((4, g.shape[1] // 2, g.shape[2]), g.dtype) for g in gs]
    outs += [jax.ShapeDtypeStruct((8,) + a.shape, a.dtype) for a in everyone]

    def plan(i, o, x, y, c):
        me = 4 * x + 2 * y + c
        local, remote = [], []
        for t in range(ns):
            h = gs[t].shape[1] // 2
            theirs = i[t].at[:, pl.ds(pl.multiple_of((1 - c) * h, 8), h), :]
            remote.append((theirs, o[t], (x, y, 1 - c), o[t]))
        for t in range(ns, ns + ne):
            local.append((i[t], o[t].at[me]))
            for px, py, pc in _other_devices(x, y, c):
                remote.append((i[t], o[t].at[me], (px, py, pc), o[t].at[4 * px + 2 * py + pc]))
        return local, remote

    return _exchange("grad_swap_sibling", list(gs) + list(everyone), outs, ne, ns + 7 * ne, plan)


def _scatter_chips(parts):
    ns = len(parts)
    outs = [jax.ShapeDtypeStruct(a.shape, a.dtype) for a in parts]

    def plan(i, o, x, y, c):
        mine = 2 * x + y
        local, remote = [], []
        for t in range(ns):
            local.append((i[t].at[mine], o[t].at[mine]))
            for px, py in _other_chips(x, y):
                remote.append((i[t].at[2 * px + py], o[t].at[mine], (px, py, c), o[t].at[2 * px + py]))
        return local, remote

    return _exchange("grad_scatter_chips", list(parts), outs, ns, 3 * ns, plan)


def _join_halves(halves):
    ns = len(halves)
    outs = [jax.ShapeDtypeStruct(a.shape, a.dtype) for a in halves]

    def plan(i, o, x, y, c):
        return [], [(i[t], o[t], (x, y, 1 - c), o[t]) for t in range(ns)]

    return _exchange("grad_join_sibling", list(halves), outs, 0, ns, plan)


def _pad_heads_cols(w, per, used):
    k = w.shape[0]
    w = w.reshape(k, NH, per)[:, :, :used]
    return jnp.pad(w, ((0, 0), (0, 0), (0, HP - used))).reshape(k, NH * HP)


def _unpad_heads_cols(w, used):
    k = w.shape[0]
    return w.reshape(k, NH, HP)[:, :, :used]


def _prep_weights(wf):
    bf = lambda a: a.astype(BF16)
    out = {}
    out["w_in"] = jnp.pad(bf(wf["w_in"]), ((0, 0), (0, IN_PAD - IN_COLS)))
    out["w_glu"] = bf(wf["w_glu"])
    out["w_uq"] = _pad_heads_cols(bf(wf["w_uq"]), QK_NOPE + QK_ROPE, QK_NOPE + QK_ROPE)
    wkv = bf(wf["w_ukv"]).reshape(KV_LORA, NH, QK_NOPE + V_HEAD)
    wk = jnp.pad(wkv[:, :, :QK_NOPE], ((0, 0), (0, 0), (0, HP - QK_NOPE))).reshape(KV_LORA, NH * HP)
    wv = jnp.pad(wkv[:, :, QK_NOPE:], ((0, 0), (0, 0), (0, HP - V_HEAD))).reshape(KV_LORA, NH * HP)
    out["w_ukv"] = jnp.concatenate([wk, wv], axis=1)
    return out


def _prep_late_weights(wf):
    bf = lambda a: a.astype(BF16)
    out = {}
    wo = bf(wf["w_out"])
    wo_a = jnp.pad(wo[D_SSM:].reshape(NH, V_HEAD, D), ((0, 0), (0, HP - V_HEAD), (0, 0))).reshape(NH * HP, D)
    out["w_out"] = jnp.concatenate([wo[:D_SSM], wo_a], axis=0)
    out["w_ff1"] = bf(wf["w_ff1"])
    out["w_ff2"] = bf(wf["w_ff2"])
    return out


def _rope_tables(positions):
    inv_freq = ROPE_BASE ** (-jnp.arange(0, QK_ROPE, 2, dtype=F32) / QK_ROPE)
    ang = positions.astype(F32)[:, None] * inv_freq
    cos, sin = jnp.cos(ang), jnp.sin(ang)
    n = positions.shape[0]
    one = jnp.ones((n, QK_NOPE), F32)
    z16 = jnp.zeros((n, 16), F32)
    z32 = jnp.zeros((n, 32), F32)
    z64 = jnp.zeros((n, QK_NOPE), F32)
    rc = jnp.concatenate([one, cos, cos, z32], axis=1)
    rs1 = jnp.concatenate([z64, -sin, z16, z32], axis=1)
    rs2 = jnp.concatenate([z64, z16, sin, z32], axis=1)
    return rc, rs1, rs2


def _permute_rows(a, S):
    n, w = a.shape
    return a.reshape(n // S, 8, S // 8, w).transpose(0, 2, 1, 3).reshape(n, w)


def _unpermute_rows(a, S):
    n, w = a.shape
    return a.reshape(n // S, S // 8, 8, w).transpose(0, 2, 1, 3).reshape(n, w)


def _block_diag_in(bb):
    eye = jnp.eye(8, dtype=bb.dtype)
    blocks = jnp.einsum("qgph,gk->qghkp", bb.reshape(4, 8, P, H), eye).reshape(4, QB, QS)
    return blocks.transpose(1, 0, 2).reshape(QB, NST)


def _block_diag_out(cc):
    eye = jnp.eye(8, dtype=cc.dtype)
    return jnp.einsum("qghp,gk->qgpkh", cc.reshape(4, 8, H, P), eye).reshape(NST, QB)


def _slots(full):
    r, cdim = full.shape
    return full.reshape(r, 4, cdim // 4).transpose(1, 0, 2)


def _unslots(g):
    s, r, cs = g.shape
    return g.transpose(1, 0, 2).reshape(r, s * cs)


def _local_step(x, positions, target, modp, wf, late_weights=None):
    nb, S, _ = x.shape
    n = nb * S
    tm = min(256, S)
    tt = min(256, S)
    tq = min(512, S // 2)
    kw = _prep_weights(wf)
    row = lambda a: a.reshape(1, -1).astype(F32)

    xf = x.reshape(n, D)
    tf = target.reshape(n, D)
    g1, g2, gf = row(wf["norm1_g"]), row(wf["norm2_g"]), row(wf["final_norm_g"])
    h1, proj = _f1_fwd(xf, modp, g1, kw["w_in"], S, tm)

    col = lambda a: a.reshape(NST, 1)
    lam_re, lam_im = col(wf["ssm_lambda_re"]), col(wf["ssm_lambda_im"])
    logdt = jnp.repeat(wf["ssm_log_dt"].reshape(G, 1), P, axis=1).reshape(NST, 1)
    b_re, b_im = wf["ssm_b_re"].reshape(NST, H), wf["ssm_b_im"].reshape(NST, H)
    lbr, lbi, bbr, bbi = _ssm_param_fwd(lam_re, lam_im, logdt, b_re, b_im)
    lre8 = jnp.broadcast_to(lbr.reshape(1, NST), (8, NST))
    lim8 = jnp.broadcast_to(lbi.reshape(1, NST), (8, NST))
    bm = jnp.concatenate([_block_diag_in(bbr.reshape(G, P, H)), _block_diag_in(bbi.reshape(G, P, H))],
                         axis=1).astype(BF16)
    cm = jnp.concatenate([_block_diag_out(wf["ssm_c_re"]), -_block_diag_out(wf["ssm_c_im"])], axis=0).astype(BF16)
    dvec = row(wf["ssm_d"])
    u_p = _permute_rows(proj[:, :D_SSM], S)
    fcr, fci = _ssm_local(u_p, bm, lre8, lim8, S, tt)
    st, ypre, z, gact, yssm_p = _ssm_fwd(u_p, fcr, fci, bm, cm, dvec, kw["w_glu"], lre8, lim8, S, tt)
    yssm = _unpermute_rows(yssm_p, S)

    rc, rs1, rs2 = _rope_tables(positions.reshape(n))
    gq, gkv = row(wf["q_norm_g"]), row(wf["kv_norm_g"])
    q, k, v, qn, kvn = _mla_fwd(proj, rc, rs1, rs2, gq, gkv, kw["w_uq"], kw["w_ukv"], tm)
    oattn, lrow = _attn_fwd(q, k, v, S, tq)

    gs = row(wf["ssm_out_g"])
    ga = jnp.pad(wf["attn_out_g"].reshape(NH, V_HEAD), ((0, 0), (0, HP - V_HEAD))).reshape(1, NH * HP)
    kw.update(_prep_late_weights(late_weights(oattn) if late_weights is not None else wf))
    yn, o, x1, h2 = _p1_fwd(yssm, oattn, xf, modp, gs, ga, kw["w_out"], g2, S, tm)
    dx1, r, da, dff, accs2, accg2 = _p2(x1, h2, tf, modp, g2, gf, kw["w_ff1"], kw["w_ff2"], S, tm)
    loss = jnp.sum(accg2[2])
    do, dyssm, dob, drow, accs3, accg3 = _p3_bwd(dx1, o, yssm, oattn, modp, gs, ga, kw["w_out"], S, tm)

    dq, dk, dv = _attn_bwd(q, k, v, dob, lrow, drow, S, tq)
    dmla, dqb, dkvb, accm = _mla_bwd(dq, dk, dv, proj, rc, rs1, rs2, gq, gkv, kw["w_uq"], kw["w_ukv"], tm)

    dys_p = _permute_rows(dyssm, S)
    dy, dz, air, aii = _ssm_bwd_a(dys_p, z, ypre, kw["w_glu"], cm, lre8, lim8, S, tt)
    du_p, dcm, dbm, dd, dlr, dli = _ssm_bwd_b(dy, u_p, st, fcr, fci, air, aii, bm, cm, dvec, lre8, lim8, S, tt)
    du = _unpermute_rows(du_p, S)
    dcm = dcm.reshape(2, 4, 8, P, 8, H)
    dc_re = jnp.einsum("qgpgh->qghp", dcm[0]).reshape(G, H, P)
    dc_im = -jnp.einsum("qgpgh->qghp", dcm[1]).reshape(G, H, P)
    dbm = dbm.reshape(8, H, 2, 4, 8, P)
    dbb_re = jnp.einsum("ghqgp->qgph", dbm[:, :, 0]).reshape(NST, H)
    dbb_im = jnp.einsum("ghqgp->qgph", dbm[:, :, 1]).reshape(NST, H)
    gb_re, gb_im, glr, gli, gdt = _ssm_param_bwd(lam_re, lam_im, logdt, b_re, b_im, dlr.reshape(NST, 1),
                                                 dli.reshape(NST, 1), dbb_re, dbb_im)
    glogdt = _rowsum(gdt.reshape(G, P))

    dx, dproj, accs1, accg1 = _f1_bwd(du, dmla, dx1, xf, modp, g1, kw["w_in"], S, tm)

    big = {}
    big["w_in"] = _slots(_wgrad(h1, dproj, "wgrad_in")[:, :IN_COLS])
    big["w_glu"] = _wgrad(gact, dz, "wgrad_glu", col_slots=4)
    big["w_uq"] = _slots(_unpad_heads_cols(_wgrad(qn, dqb, "wgrad_uq"), QK_NOPE + QK_ROPE).reshape(Q_LORA, -1))
    gkvw = _wgrad(kvn, dkvb, "wgrad_ukv")
    big["w_ukv"] = _slots(jnp.concatenate([_unpad_heads_cols(gkvw[:, :NH * HP], QK_NOPE),
                                           _unpad_heads_cols(gkvw[:, NH * HP:], V_HEAD)], axis=2).reshape(KV_LORA, -1))
    gwo = _wgrad(yn, do, "wgrad_out")
    big["w_out"] = jnp.concatenate([gwo[:D_SSM].reshape(2, D_SSM // 2, D),
                                    gwo[D_SSM:].reshape(2, NH // 2 * HP, D).reshape(2, NH // 2, HP, D)[:, :, :V_HEAD]
                                    .reshape(2, D_ATTN // 2, D)], axis=0)
    big["w_ff1"] = _wgrad(h2, da, "wgrad_ff1", col_slots=4)
    big["w_ff2"] = _wgrad(r, dff, "wgrad_ff2").reshape(4, D_FF // 4, D)

    small = {}
    small["norm1_g"] = accg1[0:1]
    small["norm2_g"] = accg2[0:1]
    small["final_norm_g"] = accg2[1:2]
    small["ssm_out_g"] = accg3[0:1, :D_SSM]
    small["attn_out_g"] = accg3[1].reshape(NH, HP)[:, :V_HEAD].reshape(1, D_ATTN)
    small["q_norm_g"] = accm[0:1, :Q_LORA]
    small["kv_norm_g"] = accm[1:2, :KV_LORA]
    small["ssm_lambda_re"] = glr.reshape(G, P)
    small["ssm_lambda_im"] = gli.reshape(G, P)
    small["ssm_b_re"] = gb_re
    small["ssm_b_im"] = gb_im
    small["ssm_c_re"] = dc_re.reshape(G * H, P)
    small["ssm_c_im"] = dc_im.reshape(G * H, P)
    small["ssm_d"] = dd.reshape(G, H)
    small["ssm_log_dt"] = glogdt.reshape(1, G)
    return loss, dx.reshape(nb, S, D), big, small, accs1 + accs2 + accs3


def _view2d(a):
    return a.reshape(-1, a.shape[-1]) if a.ndim > 1 else a.reshape(1, -1)


def kernel(x, c, positions, ada_w, ada_b, norm1_g, w_in, ssm_lambda_re, ssm_lambda_im, ssm_b_re, ssm_b_im, ssm_c_re, ssm_c_im, ssm_d, ssm_log_dt, w_glu, q_norm_g, w_uq, kv_norm_g, w_ukv, ssm_out_g, attn_out_g, w_out, norm2_g, w_ff1, w_ff2, final_ada_w, final_ada_b, final_norm_g, loss_target, m_ada_w, m_ada_b, m_norm1_g, m_w_in, m_ssm_lambda_re, m_ssm_lambda_im, m_ssm_b_re, m_ssm_b_im, m_ssm_c_re, m_ssm_c_im, m_ssm_d, m_ssm_log_dt, m_w_glu, m_q_norm_g, m_w_uq, m_kv_norm_g, m_w_ukv, m_ssm_out_g, m_attn_out_g, m_w_out, m_norm2_g, m_w_ff1, m_w_ff2, m_final_ada_w, m_final_ada_b, m_final_norm_g, v_ada_w, v_ada_b, v_norm1_g, v_w_in, v_ssm_lambda_re, v_ssm_lambda_im, v_ssm_b_re, v_ssm_b_im, v_ssm_c_re, v_ssm_c_im, v_ssm_d, v_ssm_log_dt, v_w_glu, v_q_norm_g, v_w_uq, v_kv_norm_g, v_w_ukv, v_ssm_out_g, v_attn_out_g, v_w_out, v_norm2_g, v_w_ff1, v_w_ff2, v_final_ada_w, v_final_ada_b, v_final_norm_g):
    args = dict(locals())
    names = list(inspect.signature(kernel).parameters)
    wnames = names[3:names.index("loss_target")]
    small_names = [nm for nm in wnames if nm not in GATHERED and nm not in TP]
    reduced_names = [nm for nm in small_names if nm not in ("ada_b", "final_ada_b")]
    w = {nm: args[nm] for nm in wnames}
    m = {nm: args["m_" + nm] for nm in wnames}
    v = {nm: args["v_" + nm] for nm in wnames}
    nb = x.shape[0]
    xi, yi, ci = lax.axis_index("x"), lax.axis_index("y"), lax.axis_index("c")
    chip, me = 2 * xi + yi, 4 * xi + 2 * yi + ci

    unslot = lambda nm, g: g.reshape(-1, g.shape[-1]) if nm in ROW_SHARDED else _unslots(g)
    early = [nm for nm in GATHERED if nm not in LATE]
    got = _gather_chips("gather_weights", [_view2d(w[nm]).astype(BF16) for nm in early], [c])
    wf = {nm: unslot(nm, g) for nm, g in zip(early, got)}
    for nm in small_names:
        wf[nm] = w[nm][0] if w[nm].ndim > 1 else w[nm]
    c_all = got[len(early)].reshape(8 * nb, D)

    na, nf = ada_w.shape[-1], final_ada_w.shape[-1]
    ada_b_s = lax.dynamic_slice(ada_b, (0, chip * na), (1, na))
    fada_b_s = lax.dynamic_slice(final_ada_b.reshape(1, -1), (0, chip * nf), (1, nf))
    cond_all, modcols = _mod_fwd(c_all, ada_w[0], ada_b_s, final_ada_w, fada_b_s)
    (mod_g,) = _gather_chips("gather_mod", [modcols])
    mine = lax.dynamic_slice(mod_g, (0, me * nb, 0), (4, nb, na + nf))
    modp = jnp.concatenate([mine[:, :, :na].transpose(1, 0, 2).reshape(nb, 6, D),
                            mine[:, :, na:].transpose(1, 0, 2).reshape(nb, 2, D)], axis=1)

    own_late = [_view2d(w[nm]).astype(BF16) for nm in LATE]
    sems, srcs, lands, token = _gather_start(own_late, modp)
    modp = modp + token[0, 0]

    def late_weights(after):
        landed = _gather_wait(sems, srcs, lands, after)
        return {nm: unslot(nm, lax.dynamic_update_slice(g, own[None], (chip, 0, 0)))
                for nm, g, own in zip(LATE, landed, own_late)}

    loss, grad_x, big, small, dmodp = _local_step(x, positions, loss_target, modp, wf, late_weights)
    loss = lax.psum(loss, ("x", "y", "c"))

    sizes = [small[nm].size for nm in reduced_names]
    pad = -sum(sizes) % 128
    packed = jnp.concatenate([small[nm].reshape(1, -1) for nm in reduced_names] + [jnp.zeros((1, pad), F32)], axis=1)
    swapped = _swap_halves([big[nm] for nm in GATHERED], [dmodp.reshape(nb, 8 * D), packed])
    cidx = ci.astype(jnp.int32).reshape(1)
    chip_sums = [_add_half(big[nm], r, cidx, "grad_add_sibling_" + nm) for nm, r in zip(GATHERED, swapped)]
    halves = [_add_chips(r, "grad_add_chips_" + nm) for nm, r in zip(GATHERED, _scatter_chips(chip_sums))]
    others = _join_halves(halves)
    grads = {}
    dmod_all = swapped[len(GATHERED)].reshape(8 * nb, 8 * D)
    small_sum = _sum_devices(swapped[len(GATHERED) + 1].reshape(8, -1))
    off = 0
    for nm, sz in zip(reduced_names, sizes):
        grads[nm] = small_sum[:, off:off + sz].reshape(small[nm].shape)
        off += sz

    dsl = jnp.concatenate([lax.dynamic_slice(dmod_all, (0, chip * na), (8 * nb, na)),
                           lax.dynamic_slice(dmod_all, (0, 6 * D + chip * nf), (8 * nb, nf))], axis=1)
    gw, gb = _mod_bwd(cond_all.T, dsl, dmod_all)
    grads["ada_w"], grads["final_ada_w"] = gw[:, :na], gw[:, na:]
    grads["ada_b"], grads["final_ada_b"] = gb[:, :6 * D], gb[:, 6 * D:]

    delta, new_m, new_v = {}, {}, {}
    for nm, mine_h, other_h in zip(GATHERED, halves, others):
        grads[nm], delta[nm], new_m[nm], new_v[nm] = _adamw_halves(
            _view2d(w[nm]), mine_h, other_h, _view2d(m[nm]), _view2d(v[nm]), cidx, "adamw_" + nm)
    for nm in TP:
        delta[nm], new_m[nm], new_v[nm] = _adamw(_view2d(w[nm]), grads[nm], _view2d(m[nm]), _view2d(v[nm]),
                                                  "adamw_" + nm)
    upd = _adamw_small([_view2d(w[nm]) for nm in small_names], [grads[nm] for nm in small_names],
                       [_view2d(m[nm]) for nm in small_names], [_view2d(v[nm]) for nm in small_names])
    k = len(small_names)
    for t, nm in enumerate(small_names):
        delta[nm], new_m[nm], new_v[nm] = upd[t], upd[k + t], upd[2 * k + t]

    outs = [grads, delta, new_m, new_v]
    return (loss, grad_x, *[d[nm].reshape(w[nm].shape) for d in outs for nm in wnames])
```

```python
import functools
import inspect
import math

import jax
import jax.numpy as jnp
from jax import lax
from jax.experimental import pallas as pl
from jax.experimental.pallas import tpu as pltpu

F32 = jnp.float32
BF16 = jnp.bfloat16

D = 1024
D_SSM = 512
G = 32
H = 16
P = 64
NST = G * P
D_ATTN = 512
NH = 8
QK_NOPE = 64
QK_ROPE = 32
V_HEAD = 64
HP = 128
Q_LORA = 384
KV_LORA = 256
IN_COLS = D_SSM + Q_LORA + KV_LORA + QK_ROPE
IN_PAD = 1280
D_FF = 4096
ROPE_BASE = 10000.0
EPS = 1e-6
ADAM_LR = 0.001
ADAM_B1 = 0.9
ADAM_B2 = 0.999
ADAM_EPS = 1e-08
ADAM_WD = 0.01
ADAM_STEP = 10
NEG = -1e30
VMEM_LIMIT = 60 << 20

MESH = pl.DeviceIdType.MESH
_VM = pl.BlockSpec(memory_space=pltpu.VMEM)
_ANY = pl.BlockSpec(memory_space=pl.ANY)

GATHERED = ["w_in", "w_glu", "w_uq", "w_ukv", "w_out", "w_ff1", "w_ff2"]
TP = ["ada_w", "final_ada_w"]
ROW_SHARDED = ("w_out", "w_ff2")
LATE = ["w_out", "w_ff1", "w_ff2"]


def _cp(sem=None, vmem=VMEM_LIMIT):
    kw = dict(vmem_limit_bytes=vmem)
    if sem is not None:
        kw["dimension_semantics"] = sem
    return pltpu.CompilerParams(**kw)


def _dot(a, b):
    return jnp.dot(a, b, preferred_element_type=F32)


def _dot_nt(a, b):
    return lax.dot_general(a, b, (((1,), (1,)), ((), ())), preferred_element_type=F32)


def _dot_tn(a, b):
    return lax.dot_general(a, b, (((0,), (0,)), ((), ())), preferred_element_type=F32)


def _rms(x, n):
    r = lax.rsqrt(jnp.sum(x * x, axis=-1, keepdims=True) * (1.0 / n) + EPS)
    return x * r, r


def _rms_bwd(dyg, xhat, r, n):
    return r * (dyg - xhat * (jnp.sum(dyg * xhat, axis=-1, keepdims=True) * (1.0 / n)))


def _sigmoid(x):
    return 1.0 / (1.0 + jnp.exp(-x))


_GK = math.sqrt(2.0 / math.pi)
_GC = 0.044715


def _gelu(y):
    t = jnp.tanh(_GK * (y + _GC * y * y * y))
    return 0.5 * y * (1.0 + t)


def _gelu_grad(y):
    t = jnp.tanh(_GK * (y + _GC * y * y * y))
    return 0.5 * (1.0 + t) + 0.5 * y * (1.0 - t * t) * _GK * (1.0 + 3.0 * _GC * y * y)


def _colsum(x):
    return jnp.sum(x, axis=0, keepdims=True)


def _roll(x, s):
    return pltpu.roll(x, s % x.shape[-1], x.ndim - 1)


def _mod_fwd(c_all, ada_w_s, ada_b_s, fada_w_s, fada_b_s):
    nseq = c_all.shape[0]
    na, nf = ada_w_s.shape[1], fada_w_s.shape[1]

    def body(c_ref, w_ref, b_ref, fw_ref, fb_ref, cond_ref, mod_ref):
        cv = c_ref[...]
        cond = cv * _sigmoid(cv)
        cond_ref[...] = cond
        cb = cond.astype(BF16)
        mod_ref[:, 0:na] = _dot(cb, w_ref[...].astype(BF16)) + b_ref[...]
        mod_ref[:, na:na + nf] = _dot(cb, fw_ref[...].astype(BF16)) + fb_ref[...]

    return pl.pallas_call(
        body, name="mod_fwd",
        out_shape=[jax.ShapeDtypeStruct((nseq, D), F32), jax.ShapeDtypeStruct((nseq, na + nf), F32)],
        in_specs=[_VM] * 5, out_specs=[_VM] * 2, compiler_params=_cp(),
    )(c_all, ada_w_s, ada_b_s, fada_w_s, fada_b_s)


def _mod_bwd(cond_t, dsl, dall):
    nseq, n = dsl.shape
    bc = 512

    def body(ct_ref, dm_ref, da_ref, gw_ref, gb_ref):
        ct = ct_ref[...]
        dm = dm_ref[...]
        acc = ct[:, 0:1] * dm[0:1, :]
        for b in range(1, nseq):
            acc = acc + ct[:, b:b + 1] * dm[b:b + 1, :]
        gw_ref[...] = acc

        @pl.when(pl.program_id(0) == 0)
        def _():
            gb_ref[...] = _colsum(da_ref[...])

    return pl.pallas_call(
        body, name="mod_bwd", grid=(n // bc,),
        out_shape=[jax.ShapeDtypeStruct((D, n), F32), jax.ShapeDtypeStruct((1, dall.shape[1]), F32)],
        in_specs=[_VM, pl.BlockSpec((nseq, bc), lambda i: (0, i)), _VM],
        out_specs=[pl.BlockSpec((D, bc), lambda i: (0, i)), pl.BlockSpec((1, dall.shape[1]), lambda i: (0, 0))],
        compiler_params=_cp(("arbitrary",)),
    )(cond_t, dsl, dall)


def _f1_fwd(x, modp, g1, w_in, S, tm):
    n = x.shape[0]
    tps = S // tm

    def body(x_ref, mod_ref, g_ref, w_ref, h_ref, proj_ref):
        xhat, _ = _rms(x_ref[...], D)
        h = (xhat * g_ref[...]) * (1.0 + mod_ref[0, 1:2, :]) + mod_ref[0, 0:1, :]
        hb = h.astype(BF16)
        h_ref[...] = hb
        proj_ref[...] = _dot(hb, w_ref[...])

    return pl.pallas_call(
        body, name="f1_fwd", grid=(n // tm,),
        out_shape=[jax.ShapeDtypeStruct((n, D), BF16), jax.ShapeDtypeStruct((n, IN_PAD), F32)],
        in_specs=[pl.BlockSpec((tm, D), lambda i: (i, 0)),
                  pl.BlockSpec((1, 8, D), lambda i: (i // tps, 0, 0)), _VM, _VM],
        out_specs=[pl.BlockSpec((tm, D), lambda i: (i, 0)), pl.BlockSpec((tm, IN_PAD), lambda i: (i, 0))],
        compiler_params=_cp(("parallel",)),
    )(x, modp, g1, w_in)


def _f1_bwd(du, dmla, dx1, x, modp, g1, w_in, S, tm):
    n = x.shape[0]
    tps = S // tm
    nb = n // S

    def body(du_ref, dm_ref, dx1_ref, x_ref, mod_ref, g_ref, w_ref, dx_ref, dproj_ref, accs_ref, accg_ref):
        i = pl.program_id(0)
        dproj = jnp.concatenate([du_ref[...], dm_ref[...]], axis=1).astype(BF16)
        dproj_ref[...] = dproj
        dh = _dot_nt(dproj, w_ref[...])
        xhat, r = _rms(x_ref[...], D)
        g = g_ref[...]
        dn = dh * (1.0 + mod_ref[0, 1:2, :])
        dx_ref[...] = dx1_ref[...] + _rms_bwd(dn * g, xhat, r, D)

        @pl.when(i % tps == 0)
        def _():
            accs_ref[...] = jnp.zeros_like(accs_ref)

        @pl.when(i == 0)
        def _():
            accg_ref[...] = jnp.zeros_like(accg_ref)

        accs_ref[0, 0:1, :] += _colsum(dh)
        accs_ref[0, 1:2, :] += _colsum(dh * (xhat * g))
        accg_ref[0:1, :] += _colsum(dn * xhat)

    return pl.pallas_call(
        body, name="f1_bwd", grid=(n // tm,),
        out_shape=[jax.ShapeDtypeStruct((n, D), F32), jax.ShapeDtypeStruct((n, IN_PAD), BF16),
                   jax.ShapeDtypeStruct((nb, 8, D), F32), jax.ShapeDtypeStruct((8, D), F32)],
        in_specs=[pl.BlockSpec((tm, D_SSM), lambda i: (i, 0)), pl.BlockSpec((tm, IN_PAD - D_SSM), lambda i: (i, 0)),
                  pl.BlockSpec((tm, D), lambda i: (i, 0)), pl.BlockSpec((tm, D), lambda i: (i, 0)),
                  pl.BlockSpec((1, 8, D), lambda i: (i // tps, 0, 0)), _VM, _VM],
        out_specs=[pl.BlockSpec((tm, D), lambda i: (i, 0)), pl.BlockSpec((tm, IN_PAD), lambda i: (i, 0)),
                   pl.BlockSpec((1, 8, D), lambda i: (i // tps, 0, 0)), pl.BlockSpec((8, D), lambda i: (0, 0))],
        compiler_params=_cp(("arbitrary",)),
    )(du, dmla, dx1, x, modp, g1, w_in)


def _ssm_param_fwd(lam_re, lam_im, logdt, b_re, b_im):
    def body(lr_ref, li_ref, ld_ref, br_ref, bi_ref, lbr_ref, lbi_ref, bbr_ref, bbi_ref):
        lr, li = lr_ref[...], li_ref[...]
        dt = jnp.exp(ld_ref[...])
        er = jnp.exp(lr * dt)
        lbr = er * jnp.cos(li * dt)
        lbi = er * jnp.sin(li * dt)
        den = 1.0 / (lr * lr + li * li)
        cr = ((lbr - 1.0) * lr + lbi * li) * den
        ci = (lbi * lr - (lbr - 1.0) * li) * den
        lbr_ref[...] = lbr
        lbi_ref[...] = lbi
        bbr_ref[...] = cr * br_ref[...] - ci * bi_ref[...]
        bbi_ref[...] = cr * bi_ref[...] + ci * br_ref[...]

    return pl.pallas_call(
        body, name="ssm_param_fwd",
        out_shape=[jax.ShapeDtypeStruct((NST, 1), F32)] * 2 + [jax.ShapeDtypeStruct((NST, H), F32)] * 2,
        in_specs=[_VM] * 5, out_specs=[_VM] * 4, compiler_params=_cp(),
    )(lam_re, lam_im, logdt, b_re, b_im)


def _ssm_param_bwd(lam_re, lam_im, logdt, b_re, b_im, dlb_re, dlb_im, dbb_re, dbb_im):
    def body(lr_ref, li_ref, ld_ref, br_ref, bi_ref, dlr_ref, dli_ref, dbr_ref, dbi_ref,
             gbr_ref, gbi_ref, glr_ref, gli_ref, gdt_ref):
        lr, li = lr_ref[...], li_ref[...]
        dt = jnp.exp(ld_ref[...])
        er = jnp.exp(lr * dt)
        lbr = er * jnp.cos(li * dt)
        lbi = er * jnp.sin(li * dt)
        den = 1.0 / (lr * lr + li * li)
        nr, ni = lbr - 1.0, lbi
        cr = (nr * lr + ni * li) * den
        ci = (ni * lr - nr * li) * den
        br, bi = br_ref[...], bi_ref[...]
        dbr, dbi = dbr_ref[...], dbi_ref[...]
        gbr_ref[...] = cr * dbr + ci * dbi
        gbi_ref[...] = cr * dbi - ci * dbr
        gcr = jnp.sum(dbr * br + dbi * bi, axis=1, keepdims=True)
        gci = jnp.sum(dbi * br - dbr * bi, axis=1, keepdims=True)
        ilr, ili = lr * den, -li * den
        glbr = dlr_ref[...] + (gcr * ilr + gci * ili)
        glbi = dli_ref[...] + (gci * ilr - gcr * ili)
        qr = -(cr * ilr - ci * ili)
        qi = -(cr * ili + ci * ilr)
        glr = gcr * qr + gci * qi
        gli = gci * qr - gcr * qi
        glr = glr + dt * (glbr * lbr + glbi * lbi)
        gli = gli + dt * (glbi * lbr - glbr * lbi)
        wr = lr * lbr - li * lbi
        wi = lr * lbi + li * lbr
        glr_ref[...] = glr
        gli_ref[...] = gli
        gdt_ref[...] = (glbr * wr + glbi * wi) * dt

    return pl.pallas_call(
        body, name="ssm_param_bwd",
        out_shape=[jax.ShapeDtypeStruct((NST, H), F32)] * 2 + [jax.ShapeDtypeStruct((NST, 1), F32)] * 3,
        in_specs=[_VM] * 9, out_specs=[_VM] * 5, compiler_params=_cp(),
    )(lam_re, lam_im, logdt, b_re, b_im, dlb_re, dlb_im, dbb_re, dbb_im)


def _rowsum(a):
    def body(a_ref, o_ref):
        o_ref[...] = jnp.sum(a_ref[...], axis=1, keepdims=True)

    return pl.pallas_call(
        body, name="rowsum", out_shape=jax.ShapeDtypeStruct((a.shape[0], 1), F32),
        in_specs=[_VM], out_specs=_VM, compiler_params=_cp(),
    )(a)


QB = D_SSM // 4
QS = 4 * QB


def _bd_lo(part, q):
    return part * NST + q * QS


def _bd_expand(ub, bm_ref, out_ref):
    for part in range(2):
        for q in range(4):
            lo = _bd_lo(part, q)
            out_ref[:, lo:lo + QS] = _dot(ub[:, q * QB:(q + 1) * QB], bm_ref[:, lo:lo + QS])


def _bd_expand_t(db, cm_ref, out_ref):
    for part in range(2):
        for q in range(4):
            lo = _bd_lo(part, q)
            out_ref[:, lo:lo + QS] = _dot_nt(db[:, q * QB:(q + 1) * QB], cm_ref[lo:lo + QS, :])


def _bd_project(sb, cm_ref):
    return jnp.concatenate(
        [_dot(sb[:, _bd_lo(0, q):_bd_lo(0, q) + QS], cm_ref[_bd_lo(0, q):_bd_lo(0, q) + QS, :])
         + _dot(sb[:, _bd_lo(1, q):_bd_lo(1, q) + QS], cm_ref[_bd_lo(1, q):_bd_lo(1, q) + QS, :])
         for q in range(4)], axis=1)


def _bd_project_t(ab, bm_ref):
    return jnp.concatenate(
        [_dot_nt(ab[:, _bd_lo(0, q):_bd_lo(0, q) + QS], bm_ref[:, _bd_lo(0, q):_bd_lo(0, q) + QS])
         + _dot_nt(ab[:, _bd_lo(1, q):_bd_lo(1, q) + QS], bm_ref[:, _bd_lo(1, q):_bd_lo(1, q) + QS])
         for q in range(4)], axis=1)


def _pow2k(pr, pi, nsq):
    for _ in range(nsq):
        pr, pi = pr * pr - pi * pi, 2.0 * pr * pi
    return pr, pi


def _ssm_local(u_p, bm, lre8, lim8, S, tt):
    n = u_p.shape[0]
    nb, nt = n // S, S // tt
    nsq = int(round(math.log2(S // 8)))
    assert 2 ** nsq == S // 8

    def body(u_ref, bm_ref, lre_ref, lim_ref, cre_ref, cim_ref, sre, sim, bu):
        j = pl.program_id(1)

        @pl.when(j == 0)
        def _():
            sre[...] = jnp.zeros_like(sre)
            sim[...] = jnp.zeros_like(sim)

        _bd_expand(u_ref[...].astype(BF16), bm_ref, bu)
        lre, lim = lre_ref[...], lim_ref[...]

        def step(i, c):
            sr, si = c
            off = pl.multiple_of(i * 8, 8)
            br = bu[pl.ds(off, 8), 0:NST]
            bi = bu[pl.ds(off, 8), NST:2 * NST]
            return lre * sr - lim * si + br, lre * si + lim * sr + bi

        sr, si = lax.fori_loop(0, tt // 8, step, (sre[...], sim[...]))
        sre[...] = sr
        sim[...] = si

        @pl.when(j == nt - 1)
        def _():
            pr, pi = _pow2k(lre[0:1], lim[0:1], nsq)
            cr = jnp.zeros((1, NST), F32)
            ci = jnp.zeros((1, NST), F32)
            cre_ref[0:1, :] = cr
            cim_ref[0:1, :] = ci
            for k in range(1, 8):
                cr, ci = sr[k - 1:k] + pr * cr - pi * ci, si[k - 1:k] + pr * ci + pi * cr
                cre_ref[k:k + 1, :] = cr
                cim_ref[k:k + 1, :] = ci

    return pl.pallas_call(
        body, name="ssm_local", grid=(nb, nt),
        out_shape=[jax.ShapeDtypeStruct((nb * 8, NST), F32)] * 2,
        in_specs=[pl.BlockSpec((tt, D_SSM), lambda b, j: (b * nt + j, 0)), _VM, _VM, _VM],
        out_specs=[pl.BlockSpec((8, NST), lambda b, j: (b, 0))] * 2,
        scratch_shapes=[pltpu.VMEM((8, NST), F32), pltpu.VMEM((8, NST), F32), pltpu.VMEM((tt, 2 * NST), F32)],
        compiler_params=_cp(("arbitrary", "arbitrary")),
    )(u_p, bm, lre8, lim8)


def _ssm_fwd(u_p, cre, cim, bm, cm, dvec, w_glu, lre8, lim8, S, tt):
    n = u_p.shape[0]
    nb, nt = n // S, S // tt

    def body(u_ref, cre_ref, cim_ref, bm_ref, cm_ref, d_ref, wg_ref, lre_ref, lim_ref,
             st_ref, ypre_ref, z_ref, gact_ref, yssm_ref, sre, sim, bu):
        j = pl.program_id(1)

        @pl.when(j == 0)
        def _():
            sre[...] = cre_ref[...]
            sim[...] = cim_ref[...]

        u = u_ref[...]
        _bd_expand(u.astype(BF16), bm_ref, bu)
        lre, lim = lre_ref[...], lim_ref[...]

        def step(i, c):
            sr, si = c
            off = pl.multiple_of(i * 8, 8)
            nr = lre * sr - lim * si + bu[pl.ds(off, 8), 0:NST]
            ni = lre * si + lim * sr + bu[pl.ds(off, 8), NST:2 * NST]
            st_ref[pl.ds(off, 8), 0:NST] = nr
            st_ref[pl.ds(off, 8), NST:2 * NST] = ni
            return nr, ni

        sr, si = lax.fori_loop(0, tt // 8, step, (sre[...], sim[...]))
        sre[...] = sr
        sim[...] = si
        y = _bd_project(st_ref[...].astype(BF16), cm_ref) + d_ref[...] * u
        ypre_ref[...] = y
        gb = _gelu(y).astype(BF16)
        gact_ref[...] = gb
        z = _dot(gb, wg_ref[...])
        z_ref[...] = z
        yssm_ref[...] = z[:, 0:D_SSM] * _sigmoid(z[:, D_SSM:2 * D_SSM])

    row = lambda w: pl.BlockSpec((tt, w), lambda b, j: (b * nt + j, 0))
    return pl.pallas_call(
        body, name="ssm_fwd", grid=(nb, nt),
        out_shape=[jax.ShapeDtypeStruct((n, 2 * NST), F32), jax.ShapeDtypeStruct((n, D_SSM), F32),
                   jax.ShapeDtypeStruct((n, 2 * D_SSM), F32), jax.ShapeDtypeStruct((n, D_SSM), BF16),
                   jax.ShapeDtypeStruct((n, D_SSM), F32)],
        in_specs=[row(D_SSM), pl.BlockSpec((8, NST), lambda b, j: (b, 0)), pl.BlockSpec((8, NST), lambda b, j: (b, 0)),
                  _VM, _VM, _VM, _VM, _VM, _VM],
        out_specs=[row(2 * NST), row(D_SSM), row(2 * D_SSM), row(D_SSM), row(D_SSM)],
        scratch_shapes=[pltpu.VMEM((8, NST), F32), pltpu.VMEM((8, NST), F32), pltpu.VMEM((tt, 2 * NST), F32)],
        compiler_params=_cp(("arbitrary", "arbitrary")),
    )(u_p, cre, cim, bm, cm, dvec, w_glu, lre8, lim8)


def _ssm_bwd_a(dys_p, z, ypre, w_glu, cm, lre8, lim8, S, tt):
    n = z.shape[0]
    nb, nt = n // S, S // tt
    nsq = int(round(math.log2(S // 8)))
    ng = tt // 8

    def body(dys_ref, z_ref, y_ref, wg_ref, cm_ref, lre_ref, lim_ref, dy_ref, dz_ref, are_ref, aim_ref, sre, sim, gb):
        j = pl.program_id(1)

        @pl.when(j == 0)
        def _():
            sre[...] = jnp.zeros_like(sre)
            sim[...] = jnp.zeros_like(sim)

        z = z_ref[...]
        z1, z2 = z[:, 0:D_SSM], z[:, D_SSM:2 * D_SSM]
        sg = _sigmoid(z2)
        dys = dys_ref[...]
        dz = jnp.concatenate([dys * sg, dys * z1 * sg * (1.0 - sg)], axis=1).astype(BF16)
        dz_ref[...] = dz
        dy = _dot_nt(dz, wg_ref[...]) * _gelu_grad(y_ref[...])
        dy_ref[...] = dy
        _bd_expand_t(dy.astype(BF16), cm_ref, gb)
        lre, lim = lre_ref[...], lim_ref[...]

        def step(i, c):
            ar, ai = c
            off = pl.multiple_of((ng - 1 - i) * 8, 8)
            gr = gb[pl.ds(off, 8), 0:NST]
            gi = gb[pl.ds(off, 8), NST:2 * NST]
            return lre * ar + lim * ai + gr, lre * ai - lim * ar + gi

        ar, ai = lax.fori_loop(0, ng, step, (sre[...], sim[...]))
        sre[...] = ar
        sim[...] = ai

        @pl.when(j == nt - 1)
        def _():
            pr, pi = _pow2k(lre[0:1], -lim[0:1], nsq)
            cr = jnp.zeros((1, NST), F32)
            ci = jnp.zeros((1, NST), F32)
            are_ref[7:8, :] = cr
            aim_ref[7:8, :] = ci
            for k in range(6, -1, -1):
                cr, ci = ar[k + 1:k + 2] + pr * cr - pi * ci, ai[k + 1:k + 2] + pr * ci + pi * cr
                are_ref[k:k + 1, :] = cr
                aim_ref[k:k + 1, :] = ci

    row = lambda w: pl.BlockSpec((tt, w), lambda b, j: (b * nt + nt - 1 - j, 0))
    return pl.pallas_call(
        body, name="ssm_bwd_a", grid=(nb, nt),
        out_shape=[jax.ShapeDtypeStruct((n, D_SSM), F32), jax.ShapeDtypeStruct((n, 2 * D_SSM), BF16),
                   jax.ShapeDtypeStruct((nb * 8, NST), F32), jax.ShapeDtypeStruct((nb * 8, NST), F32)],
        in_specs=[row(D_SSM), row(2 * D_SSM), row(D_SSM), _VM, _VM, _VM, _VM],
        out_specs=[row(D_SSM), row(2 * D_SSM), pl.BlockSpec((8, NST), lambda b, j: (b, 0)),
                   pl.BlockSpec((8, NST), lambda b, j: (b, 0))],
        scratch_shapes=[pltpu.VMEM((8, NST), F32), pltpu.VMEM((8, NST), F32), pltpu.VMEM((tt, 2 * NST), F32)],
        compiler_params=_cp(("arbitrary", "arbitrary")),
    )(dys_p, z, ypre, w_glu, cm, lre8, lim8)


def _ssm_bwd_b(dy, u_p, st, fcr, fci, air, aii, bm, cm, dvec, lre8, lim8, S, tt):
    n = u_p.shape[0]
    nb, nt = n // S, S // tt
    ng = tt // 8

    def body(dy_ref, u_ref, st_ref, stp_ref, fcr_ref, fci_ref, air_ref, aii_ref, bm_ref, cm_ref, d_ref, lre_ref, lim_ref,
             du_ref, dcm_ref, dbm_ref, dd_ref, dlr_ref, dli_ref, are, aim, accr, acci, sp, ab):
        b = pl.program_id(0)
        j = pl.program_id(1)
        jt = nt - 1 - j

        @pl.when((b == 0) & (j == 0))
        def _():
            dcm_ref[...] = jnp.zeros_like(dcm_ref)
            dbm_ref[...] = jnp.zeros_like(dbm_ref)
            dd_ref[...] = jnp.zeros_like(dd_ref)
            accr[...] = jnp.zeros_like(accr)
            acci[...] = jnp.zeros_like(acci)

        @pl.when(j == 0)
        def _():
            are[...] = air_ref[...]
            aim[...] = aii_ref[...]

        sp[8:tt + 8, :] = st_ref[...]

        @pl.when(jt == 0)
        def _():
            sp[0:8, 0:NST] = fcr_ref[...]
            sp[0:8, NST:2 * NST] = fci_ref[...]

        @pl.when(jt != 0)
        def _():
            sp[0:8, :] = stp_ref[...]

        dy = dy_ref[...]
        u = u_ref[...]
        dyb = dy.astype(BF16)
        _bd_expand_t(dyb, cm_ref, ab)
        lre, lim = lre_ref[...], lim_ref[...]

        def step(i, c):
            ar, ai = c
            off = pl.multiple_of((ng - 1 - i) * 8, 8)
            nr = lre * ar + lim * ai + ab[pl.ds(off, 8), 0:NST]
            ni = lre * ai - lim * ar + ab[pl.ds(off, 8), NST:2 * NST]
            ab[pl.ds(off, 8), 0:NST] = nr
            ab[pl.ds(off, 8), NST:2 * NST] = ni
            pr = sp[pl.ds(off, 8), 0:NST]
            pi = sp[pl.ds(off, 8), NST:2 * NST]
            accr[...] += nr * pr + ni * pi
            acci[...] += ni * pr - nr * pi
            return nr, ni

        ar, ai = lax.fori_loop(0, ng, step, (are[...], aim[...]))
        are[...] = ar
        aim[...] = ai
        a_b = ab[...].astype(BF16)
        du_ref[...] = _bd_project_t(a_b, bm_ref) + d_ref[...] * dy
        ub = u.astype(BF16)
        for q in range(4):
            for part in range(2):
                lo = part * NST + q * 4 * QB
                s_q = sp[8:tt + 8, lo:lo + 4 * QB].astype(BF16)
                dcm_ref[lo:lo + 4 * QB, :] += _dot_tn(s_q, dyb[:, q * QB:(q + 1) * QB])
                dbm_ref[:, lo:lo + 4 * QB] += _dot_tn(ub[:, q * QB:(q + 1) * QB], a_b[:, lo:lo + 4 * QB])
        dd_ref[...] += _colsum(dy * u)

        @pl.when((b == nb - 1) & (j == nt - 1))
        def _():
            dlr_ref[...] = _colsum(accr[...])
            dli_ref[...] = _colsum(acci[...])

    row = lambda w: pl.BlockSpec((tt, w), lambda b, j: (b * nt + nt - 1 - j, 0))
    seq8 = pl.BlockSpec((8, NST), lambda b, j: (b, 0))
    prev = pl.BlockSpec((8, 2 * NST), lambda b, j: (jnp.maximum((b * nt + nt - 1 - j) * ng - 1, 0), 0))
    const = lambda shape: pl.BlockSpec(shape, lambda b, j: (0, 0))
    return pl.pallas_call(
        body, name="ssm_bwd_b", grid=(nb, nt),
        out_shape=[jax.ShapeDtypeStruct((n, D_SSM), F32), jax.ShapeDtypeStruct((2 * NST, QB), F32),
                   jax.ShapeDtypeStruct((QB, 2 * NST), F32), jax.ShapeDtypeStruct((1, D_SSM), F32),
                   jax.ShapeDtypeStruct((1, NST), F32), jax.ShapeDtypeStruct((1, NST), F32)],
        in_specs=[row(D_SSM), row(D_SSM), row(2 * NST), prev, seq8, seq8, seq8, seq8, _VM, _VM, _VM, _VM, _VM],
        out_specs=[row(D_SSM), const((2 * NST, QB)), const((QB, 2 * NST)), const((1, D_SSM)),
                   const((1, NST)), const((1, NST))],
        scratch_shapes=[pltpu.VMEM((8, NST), F32)] * 4 + [pltpu.VMEM((tt + 8, 2 * NST), F32),
                                                          pltpu.VMEM((tt, 2 * NST), F32)],
        compiler_params=_cp(("arbitrary", "arbitrary")),
    )(dy, u_p, st, st, fcr, fci, air, aii, bm, cm, dvec, lre8, lim8)


def _rope(v, c, s1, s2):
    return v * c + _roll(v, -16) * s1 + _roll(v, 16) * s2


def _rope_t(dv, c, s1, s2):
    return dv * c + _roll(dv * s1, 16) + _roll(dv * s2, -16)


def _mla_fwd(proj, rc, rs1, rs2, gq, gkv, w_uq, w_ukv, tm):
    n = proj.shape[0]

    def body(ql_ref, kvl_ref, kr_ref, c_ref, s1_ref, s2_ref, gq_ref, gkv_ref, wq_ref, wkv_ref,
             q_ref, k_ref, v_ref, qn_ref, kvn_ref):
        c, s1, s2 = c_ref[...], s1_ref[...], s2_ref[...]
        qhat, _ = _rms(ql_ref[...], Q_LORA)
        qn = (qhat * gq_ref[...]).astype(BF16)
        qn_ref[...] = qn
        q = _dot(qn, wq_ref[...])
        q_ref[...] = _rope(q, jnp.tile(c, (1, NH)), jnp.tile(s1, (1, NH)), jnp.tile(s2, (1, NH))).astype(BF16)
        khat, _ = _rms(kvl_ref[...], KV_LORA)
        kvn = (khat * gkv_ref[...]).astype(BF16)
        kvn_ref[...] = kvn
        kv = _dot(kvn, wkv_ref[...])
        kr = _rope(_roll(kr_ref[...], 64), c, s1, s2)
        k_ref[...] = (kv[:, 0:NH * HP] + jnp.tile(kr, (1, NH))).astype(BF16)
        v_ref[...] = kv[:, NH * HP:2 * NH * HP].astype(BF16)

    def wrapped(proj_ref, *rest):
        ql = proj_ref.at[:, D_SSM:D_SSM + Q_LORA]
        kvl = proj_ref.at[:, D_SSM + Q_LORA:D_SSM + Q_LORA + KV_LORA]
        kr = proj_ref.at[:, IN_PAD - HP:IN_PAD]
        body(ql, kvl, kr, *rest)

    row = lambda w: pl.BlockSpec((tm, w), lambda i: (i, 0))
    return pl.pallas_call(
        wrapped, name="mla_fwd", grid=(n // tm,),
        out_shape=[jax.ShapeDtypeStruct((n, NH * HP), BF16)] * 3 +
                  [jax.ShapeDtypeStruct((n, Q_LORA), BF16), jax.ShapeDtypeStruct((n, KV_LORA), BF16)],
        in_specs=[row(IN_PAD), row(HP), row(HP), row(HP), _VM, _VM, _VM, _VM],
        out_specs=[row(NH * HP)] * 3 + [row(Q_LORA), row(KV_LORA)],
        compiler_params=_cp(("parallel",)),
    )(proj, rc, rs1, rs2, gq, gkv, w_uq, w_ukv)


def _mla_bwd(dq, dk, dv, proj, rc, rs1, rs2, gq, gkv, w_uq, w_ukv, tm):
    n = proj.shape[0]

    def body(dq_ref, dk_ref, dv_ref, proj_ref, c_ref, s1_ref, s2_ref, gq_ref, gkv_ref, wq_ref, wkv_ref,
             dmla_ref, dqb_ref, dkvb_ref, acc_ref):
        i = pl.program_id(0)
        c, s1, s2 = c_ref[...], s1_ref[...], s2_ref[...]
        dqu = _rope_t(dq_ref[...], jnp.tile(c, (1, NH)), jnp.tile(s1, (1, NH)), jnp.tile(s2, (1, NH))).astype(BF16)
        dqb_ref[...] = dqu
        dqn = _dot_nt(dqu, wq_ref[...])
        qhat, rq = _rms(proj_ref[:, D_SSM:D_SSM + Q_LORA], Q_LORA)
        dql = _rms_bwd(dqn * gq_ref[...], qhat, rq, Q_LORA)
        dkf = dk_ref[...]
        dkv = jnp.concatenate([dkf, dv_ref[...]], axis=1).astype(BF16)
        dkvb_ref[...] = dkv
        dkvn = _dot_nt(dkv, wkv_ref[...])
        khat, rk = _rms(proj_ref[:, D_SSM + Q_LORA:D_SSM + Q_LORA + KV_LORA], KV_LORA)
        dkvl = _rms_bwd(dkvn * gkv_ref[...], khat, rk, KV_LORA)
        dkr = dkf[:, 0:HP]
        for h in range(1, NH):
            dkr = dkr + dkf[:, h * HP:(h + 1) * HP]
        lane = lax.broadcasted_iota(jnp.int32, dkr.shape, 1)
        dkr = jnp.where((lane >= QK_NOPE) & (lane < QK_NOPE + QK_ROPE), dkr, 0.0)
        dkr = _roll(_rope_t(dkr, c, s1, s2), -64)
        dmla_ref[...] = jnp.concatenate([dql, dkvl, dkr], axis=1)

        @pl.when(i == 0)
        def _():
            acc_ref[...] = jnp.zeros_like(acc_ref)

        acc_ref[0:1, 0:Q_LORA] += _colsum(dqn * qhat)
        acc_ref[1:2, 0:KV_LORA] += _colsum(dkvn * khat)

    row = lambda w: pl.BlockSpec((tm, w), lambda i: (i, 0))
    return pl.pallas_call(
        body, name="mla_bwd", grid=(n // tm,),
        out_shape=[jax.ShapeDtypeStruct((n, IN_PAD - D_SSM), F32), jax.ShapeDtypeStruct((n, NH * HP), BF16),
                   jax.ShapeDtypeStruct((n, 2 * NH * HP), BF16), jax.ShapeDtypeStruct((8, Q_LORA), F32)],
        in_specs=[row(NH * HP)] * 3 + [row(IN_PAD), row(HP), row(HP), row(HP), _VM, _VM, _VM, _VM],
        out_specs=[row(IN_PAD - D_SSM), row(NH * HP), row(2 * NH * HP), pl.BlockSpec((8, Q_LORA), lambda i: (0, 0))],
        compiler_params=_cp(("arbitrary",)),
    )(dq, dk, dv, proj, rc, rs1, rs2, gq, gkv, w_uq, w_ukv)


_SCALE = (QK_NOPE + QK_ROPE) ** -0.5
_LOG2E = 1.4426950408889634
_C2 = _SCALE * _LOG2E


def _attn_fwd(q, k, v, S, tq):
    n = q.shape[0]
    nb, nq = n // S, S // tq

    def body(q_ref, k_ref, v_ref, o_ref, lr_ref):
        qi = pl.program_id(2)
        qv = q_ref[...]

        def tile(j, c, diagonal):
            m, l, acc = c
            off = pl.multiple_of(j * tq, tq)
            s = _dot_nt(qv, k_ref[pl.ds(off, tq), :]) * _C2
            if diagonal:
                rows = lax.broadcasted_iota(jnp.int32, s.shape, 0)
                cols = lax.broadcasted_iota(jnp.int32, s.shape, 1)
                s = jnp.where(cols <= rows, s, NEG)
            mn = jnp.maximum(m, jnp.max(s, axis=1, keepdims=True))
            p = jnp.exp2(s - mn)
            al = jnp.exp2(m - mn)
            l = al * l + jnp.sum(p, axis=1, keepdims=True)
            acc = al * acc + _dot(p.astype(BF16), v_ref[pl.ds(off, tq), :])
            return mn, l, acc

        init = (jnp.full((tq, 1), NEG, F32), jnp.zeros((tq, 1), F32), jnp.zeros((tq, HP), F32))
        c = lax.fori_loop(0, qi, lambda j, c: tile(j, c, False), init)
        m, l, acc = tile(qi, c, True)
        o_ref[...] = acc / l
        lane = lax.broadcasted_iota(jnp.int32, (8, HP), 1)
        lse = jnp.broadcast_to(m + jnp.log(l) * _LOG2E, (tq, HP))
        lr_ref[...] = _rows_of(lse, jnp.where(lane == 0, 1.0, 0.0).astype(BF16))

    qs = pl.BlockSpec((tq, HP), lambda b, h, i: (b * nq + i, h))
    ks = pl.BlockSpec((S, HP), lambda b, h, i: (b, h))
    return pl.pallas_call(
        body, name="attn_fwd", grid=(nb, NH, nq),
        out_shape=[jax.ShapeDtypeStruct((n, NH * HP), F32), jax.ShapeDtypeStruct((nb * NH * 8, S), F32)],
        in_specs=[qs, ks, ks], out_specs=[qs, pl.BlockSpec((8, tq), lambda b, h, i: (b * NH + h, i))],
        compiler_params=_cp(("parallel", "parallel", "arbitrary")),
    )(q, k, v)


def _rows_of(x, pick):
    x1 = x.astype(BF16)
    r1 = x - x1.astype(F32)
    x2 = r1.astype(BF16)
    x3 = (r1 - x2.astype(F32)).astype(BF16)
    return _dot_nt(pick, x1) + _dot_nt(pick, x2) + _dot_nt(pick, x3)


def _attn_bwd(q, k, v, dob, lrow, drow, S, tq):
    n = q.shape[0]
    nb, nq = n // S, S // tq

    def body(q_ref, k_ref, v_ref, do_ref, lr_ref, dr_ref, dq_ref, dk_ref, dv_ref):
        kj = pl.program_id(2)

        @pl.when(kj == 0)
        def _():
            dq_ref[...] = jnp.zeros_like(dq_ref)

        kt = k_ref[...]
        vt = v_ref[...]

        def tile(i, c, diagonal):
            dk, dv = c
            off = pl.multiple_of(i * tq, tq)
            qv = q_ref[pl.ds(off, tq), :]
            dob = do_ref[pl.ds(off, tq), :]
            lr = lr_ref[0:1, pl.ds(off, tq)]
            dr = dr_ref[0:1, pl.ds(off, tq)]
            st = _dot_nt(kt, qv)
            dpt = _dot_nt(vt, dob)
            pt = jnp.exp2(st * _C2 - lr)
            if diagonal:
                keys = lax.broadcasted_iota(jnp.int32, pt.shape, 0)
                qrys = lax.broadcasted_iota(jnp.int32, pt.shape, 1)
                pt = jnp.where(keys <= qrys, pt, 0.0)
            dst = (pt * (dpt * _SCALE - dr)).astype(BF16)
            dq_ref[pl.ds(off, tq), :] += _dot_tn(dst, kt)
            return dk + _dot(dst, qv), dv + _dot(pt.astype(BF16), dob)

        zero = jnp.zeros((tq, HP), F32)
        c = tile(kj, (zero, zero), True)
        dk, dv = lax.fori_loop(kj + 1, nq, lambda i, c: tile(i, c, False), c)
        dk_ref[...] = dk
        dv_ref[...] = dv

    ts = pl.BlockSpec((tq, HP), lambda b, h, i: (b * nq + i, h))
    fs = pl.BlockSpec((S, HP), lambda b, h, i: (b, h))
    rs = pl.BlockSpec((8, S), lambda b, h, i: (b * NH + h, 0))
    return pl.pallas_call(
        body, name="attn_bwd", grid=(nb, NH, nq),
        out_shape=[jax.ShapeDtypeStruct((n, NH * HP), F32)] * 3,
        in_specs=[fs, ts, ts, fs, rs, rs], out_specs=[fs, ts, ts],
        compiler_params=_cp(("parallel", "parallel", "arbitrary")),
    )(q, k, v, dob, lrow, drow)


def _p1_fwd(yssm, oattn, x, modp, gs, ga, w_out, g2, S, tm):
    n = x.shape[0]
    tps = S // tm

    def body(ys_ref, oa_ref, x_ref, mod_ref, gs_ref, ga_ref, w_ref, g2_ref, yn_ref, o_ref, x1_ref, h2_ref):
        yh, _ = _rms(ys_ref[...], D_SSM)
        ah, _ = _rms(oa_ref[...], D_ATTN)
        yn = jnp.concatenate([yh * gs_ref[...], ah * ga_ref[...]], axis=1).astype(BF16)
        yn_ref[...] = yn
        o = _dot(yn, w_ref[...])
        o_ref[...] = o
        x1 = x_ref[...] + mod_ref[0, 2:3, :] * o
        x1_ref[...] = x1
        xh, _ = _rms(x1, D)
        h2_ref[...] = ((xh * g2_ref[...]) * (1.0 + mod_ref[0, 4:5, :]) + mod_ref[0, 3:4, :]).astype(BF16)

    row = lambda w: pl.BlockSpec((tm, w), lambda i: (i, 0))
    return pl.pallas_call(
        body, name="p1_fwd", grid=(n // tm,),
        out_shape=[jax.ShapeDtypeStruct((n, D_SSM + NH * HP), BF16), jax.ShapeDtypeStruct((n, D), F32),
                   jax.ShapeDtypeStruct((n, D), F32), jax.ShapeDtypeStruct((n, D), BF16)],
        in_specs=[row(D_SSM), row(NH * HP), row(D), pl.BlockSpec((1, 8, D), lambda i: (i // tps, 0, 0)),
                  _VM, _VM, _VM, _VM],
        out_specs=[row(D_SSM + NH * HP), row(D), row(D), row(D)],
        compiler_params=_cp(("parallel",)),
    )(yssm, oattn, x, modp, gs, ga, w_out, g2)


def _p2(x1, h2, target, modp, g2, gf, w_ff1, w_ff2, S, tm):
    n = x1.shape[0]
    tps = S // tm
    nb = n // S

    def body(x1_ref, h2_ref, t_ref, mod_ref, g2_ref, gf_ref, w1_ref, w2_ref,
             dx1_ref, r_ref, da_ref, dff_ref, accs_ref, accg_ref):
        i = pl.program_id(0)
        sh2, sc2, gt2 = mod_ref[0, 3:4, :], mod_ref[0, 4:5, :], mod_ref[0, 5:6, :]
        fsh, fsc = mod_ref[0, 6:7, :], mod_ref[0, 7:8, :]
        x1 = x1_ref[...]
        a = _dot(h2_ref[...], w1_ref[...])
        ra = jnp.maximum(a, 0.0)
        rb = (ra * ra).astype(BF16)
        r_ref[...] = rb
        ff = _dot(rb, w2_ref[...])
        x2 = x1 + gt2 * ff
        x2h, rf = _rms(x2, D)
        gf_v = gf_ref[...]
        outn = x2h * gf_v
        err = outn * (1.0 + fsc) + fsh - t_ref[...]
        dout = err * (1.0 / D)
        doutn = dout * (1.0 + fsc)
        dx2 = _rms_bwd(doutn * gf_v, x2h, rf, D)
        dff = (gt2 * dx2).astype(BF16)
        dff_ref[...] = dff
        dr = _dot_nt(dff, w2_ref[...])
        da = (dr * (2.0 * ra)).astype(BF16)
        da_ref[...] = da
        dh2 = _dot_nt(da, w1_ref[...])
        x1h, r2 = _rms(x1, D)
        g2_v = g2_ref[...]
        dn2 = dh2 * (1.0 + sc2)
        dx1_ref[...] = dx2 + _rms_bwd(dn2 * g2_v, x1h, r2, D)

        @pl.when(i % tps == 0)
        def _():
            accs_ref[...] = jnp.zeros_like(accs_ref)

        @pl.when(i == 0)
        def _():
            accg_ref[...] = jnp.zeros_like(accg_ref)

        accs_ref[0, 3:4, :] += _colsum(dh2)
        accs_ref[0, 4:5, :] += _colsum(dh2 * (x1h * g2_v))
        accs_ref[0, 5:6, :] += _colsum(dx2 * ff)
        accs_ref[0, 6:7, :] += _colsum(dout)
        accs_ref[0, 7:8, :] += _colsum(dout * outn)
        accg_ref[0:1, :] += _colsum(dn2 * x1h)
        accg_ref[1:2, :] += _colsum(doutn * x2h)
        accg_ref[2:3, :] += _colsum(err * err) * (0.5 / D)

    row = lambda w: pl.BlockSpec((tm, w), lambda i: (i, 0))
    return pl.pallas_call(
        body, name="p2_mlp_loss", grid=(n // tm,),
        out_shape=[jax.ShapeDtypeStruct((n, D), F32), jax.ShapeDtypeStruct((n, D_FF), BF16),
                   jax.ShapeDtypeStruct((n, D_FF), BF16), jax.ShapeDtypeStruct((n, D), BF16),
                   jax.ShapeDtypeStruct((nb, 8, D), F32), jax.ShapeDtypeStruct((8, D), F32)],
        in_specs=[row(D), row(D), row(D), pl.BlockSpec((1, 8, D), lambda i: (i // tps, 0, 0)), _VM, _VM, _VM, _VM],
        out_specs=[row(D), row(D_FF), row(D_FF), row(D), pl.BlockSpec((1, 8, D), lambda i: (i // tps, 0, 0)),
                   pl.BlockSpec((8, D), lambda i: (0, 0))],
        compiler_params=_cp(("arbitrary",)),
    )(x1, h2, target, modp, g2, gf, w_ff1, w_ff2)


def _p3_bwd(dx1, o, yssm, oattn, modp, gs, ga, w_out, S, tm):
    n = dx1.shape[0]
    tps = S // tm
    nb = n // S

    def body(dx1_ref, o_ref, ys_ref, oa_ref, mod_ref, gs_ref, ga_ref, w_ref,
             do_ref, dys_ref, doa_ref, dr_ref, accs_ref, accg_ref):
        i = pl.program_id(0)
        dx1 = dx1_ref[...]
        dob = (mod_ref[0, 2:3, :] * dx1).astype(BF16)
        do_ref[...] = dob
        dyn = _dot_nt(dob, w_ref[...])
        yh, rs = _rms(ys_ref[...], D_SSM)
        oa = oa_ref[...]
        ah, ra = _rms(oa, D_ATTN)
        d1 = dyn[:, 0:D_SSM]
        d2 = dyn[:, D_SSM:D_SSM + NH * HP]
        dys_ref[...] = _rms_bwd(d1 * gs_ref[...], yh, rs, D_SSM)
        doa = _rms_bwd(d2 * ga_ref[...], ah, ra, D_ATTN)
        doa_ref[...] = doa.astype(BF16)
        prod = doa * oa * _SCALE
        ones = jnp.ones((8, HP), BF16)
        for h in range(NH):
            dr_ref[h * 8:(h + 1) * 8, :] = _rows_of(prod[:, h * HP:(h + 1) * HP], ones)

        @pl.when(i % tps == 0)
        def _():
            accs_ref[...] = jnp.zeros_like(accs_ref)

        @pl.when(i == 0)
        def _():
            accg_ref[...] = jnp.zeros_like(accg_ref)

        accs_ref[0, 2:3, :] += _colsum(dx1 * o_ref[...])
        accg_ref[0:1, 0:D_SSM] += _colsum(d1 * yh)
        accg_ref[1:2, :] += _colsum(d2 * ah)

    row = lambda w: pl.BlockSpec((tm, w), lambda i: (i, 0))
    return pl.pallas_call(
        body, name="p3_bwd", grid=(n // tm,),
        out_shape=[jax.ShapeDtypeStruct((n, D), BF16), jax.ShapeDtypeStruct((n, D_SSM), F32),
                   jax.ShapeDtypeStruct((n, NH * HP), BF16), jax.ShapeDtypeStruct((nb * NH * 8, S), F32),
                   jax.ShapeDtypeStruct((nb, 8, D), F32), jax.ShapeDtypeStruct((8, NH * HP), F32)],
        in_specs=[row(D), row(D), row(D_SSM), row(NH * HP), pl.BlockSpec((1, 8, D), lambda i: (i // tps, 0, 0)),
                  _VM, _VM, _VM],
        out_specs=[row(D), row(D_SSM), row(NH * HP), pl.BlockSpec((NH * 8, tm), lambda i: (i // tps, i % tps)),
                   pl.BlockSpec((1, 8, D), lambda i: (i // tps, 0, 0)), pl.BlockSpec((8, NH * HP), lambda i: (0, 0))],
        compiler_params=_cp(("arbitrary",)),
    )(dx1, o, yssm, oattn, modp, gs, ga, w_out)


def _wgrad(a, b, name, col_slots=0):
    n, k1 = a.shape
    k2 = b.shape[1]
    bn = next((b for b in (1024, 512) if n % b == 0), n)
    bk1 = next((b for b in (1024, 512) if k1 % b == 0), k1)
    bk2 = k2 // col_slots if col_slots else (1024 if (k2 % 1024 == 0) else k2)

    def body(a_ref, b_ref, o_ref):
        @pl.when(pl.program_id(2) == 0)
        def _():
            o_ref[...] = jnp.zeros_like(o_ref)

        o_ref[...] += _dot_tn(a_ref[...], b_ref[...]).reshape(o_ref.shape)

    if col_slots:
        out_shape = jax.ShapeDtypeStruct((col_slots, k1, bk2), F32)
        out_spec = pl.BlockSpec((1, bk1, bk2), lambda i, j, t: (j, i, 0))
    else:
        out_shape = jax.ShapeDtypeStruct((k1, k2), F32)
        out_spec = pl.BlockSpec((bk1, bk2), lambda i, j, t: (i, j))
    return pl.pallas_call(
        body, name=name, grid=(k1 // bk1, k2 // bk2, n // bn),
        out_shape=out_shape,
        in_specs=[pl.BlockSpec((bn, bk1), lambda i, j, t: (t, i)), pl.BlockSpec((bn, bk2), lambda i, j, t: (t, j))],
        out_specs=out_spec,
        compiler_params=_cp(("parallel", "parallel", "arbitrary")),
    )(a, b)


def _row_block(rows):
    if rows <= 256:
        return rows
    return next(b for b in (256, 192, 128, 64, 32, 16, 8) if rows % b == 0)


def _add_half(g, recv, cidx, name):
    _, rows2, w = g.shape
    rows = rows2 // 2
    br = _row_block(rows)
    nblk = rows // br

    def body(c_ref, g_ref, r_ref, o_ref):
        o_ref[...] = (g_ref[...] + r_ref[...]).astype(BF16)

    return pl.pallas_call(
        body, name=name,
        grid_spec=pltpu.PrefetchScalarGridSpec(
            num_scalar_prefetch=1, grid=(4, nblk),
            in_specs=[pl.BlockSpec((1, br, w), lambda s, i, c: (s, c[0] * nblk + i, 0)),
                      pl.BlockSpec((1, br, w), lambda s, i, c: (s, i, 0))],
            out_specs=pl.BlockSpec((1, br, w), lambda s, i, c: (s, i, 0))),
        out_shape=jax.ShapeDtypeStruct((4, rows, w), BF16),
        compiler_params=_cp(("parallel", "parallel")),
    )(cidx, g, recv)


def _add_chips(r, name):
    _, rows, w = r.shape
    br = _row_block(rows)

    def body(r_ref, o_ref):
        f = lambda k: r_ref[k].astype(F32)
        o_ref[...] = ((f(0) + f(1)) + f(2)) + f(3)

    return pl.pallas_call(
        body, name=name, grid=(rows // br,),
        out_shape=jax.ShapeDtypeStruct((rows, w), F32),
        in_specs=[pl.BlockSpec((4, br, w), lambda i: (0, i, 0))],
        out_specs=pl.BlockSpec((br, w), lambda i: (i, 0)),
        compiler_params=_cp(("parallel",)),
    )(r)


def _sum_devices(a):
    def body(a_ref, o_ref):
        acc = a_ref[0:1, :]
        for k in range(1, 8):
            acc = acc + a_ref[k:k + 1, :]
        o_ref[...] = acc

    return pl.pallas_call(
        body, name="small_grad_sum", out_shape=jax.ShapeDtypeStruct((1, a.shape[1]), F32),
        in_specs=[_VM], out_specs=_VM, compiler_params=_cp(),
    )(a)


def _adamw_math(wv, gv, mv, vv):
    m_new = ADAM_B1 * mv + (1.0 - ADAM_B1) * gv
    v_new = ADAM_B2 * vv + (1.0 - ADAM_B2) * (gv * gv)
    m_hat = m_new / (1.0 - ADAM_B1 ** ADAM_STEP)
    v_hat = v_new / (1.0 - ADAM_B2 ** ADAM_STEP)
    return -ADAM_LR * (m_hat / (jnp.sqrt(v_hat) + ADAM_EPS) + ADAM_WD * wv), m_new, v_new


def _adamw_small(ws, gs, ms, vs):
    k = len(ws)

    def body(*refs):
        ins, outs = refs[:4 * k], refs[4 * k:]
        for t in range(k):
            d, m_new, v_new = _adamw_math(ins[t][...], ins[k + t][...], ins[2 * k + t][...], ins[3 * k + t][...])
            outs[t][...] = d
            outs[k + t][...] = m_new
            outs[2 * k + t][...] = v_new

    shapes = [jax.ShapeDtypeStruct(w.shape, F32) for w in ws]
    return pl.pallas_call(
        body, name="adamw_small", out_shape=shapes * 3,
        in_specs=[_VM] * (4 * k), out_specs=[_VM] * (3 * k), compiler_params=_cp(),
    )(*ws, *gs, *ms, *vs)


def _adamw(w, g, m, v, name):
    rows, wd = w.shape
    br = _row_block(rows)

    def body(w_ref, g_ref, m_ref, v_ref, d_ref, nm_ref, nv_ref):
        d, m_new, v_new = _adamw_math(w_ref[...], g_ref[...], m_ref[...], v_ref[...])
        d_ref[...] = d
        nm_ref[...] = m_new
        nv_ref[...] = v_new

    spec = pl.BlockSpec((br, wd), lambda i: (i, 0))
    return pl.pallas_call(
        body, name=name, grid=(rows // br,),
        out_shape=[jax.ShapeDtypeStruct((rows, wd), F32)] * 3,
        in_specs=[spec] * 4, out_specs=[spec] * 3,
        compiler_params=_cp(("parallel",)),
    )(w, g, m, v)


def _adamw_halves(w, mine, other, m, v, cidx, name):
    rows, wd = w.shape
    h = rows // 2
    br = _row_block(h)
    nblk = h // br

    def body(c_ref, w_ref, a_ref, b_ref, m_ref, v_ref, g_ref, d_ref, nm_ref, nv_ref):
        gv = jnp.where(pl.program_id(0) == c_ref[0], a_ref[...], b_ref[...])
        d, m_new, v_new = _adamw_math(w_ref[...], gv, m_ref[...], v_ref[...])
        g_ref[...] = gv
        d_ref[...] = d
        nm_ref[...] = m_new
        nv_ref[...] = v_new

    full = pl.BlockSpec((br, wd), lambda hf, i, c: (hf * nblk + i, 0))
    half = pl.BlockSpec((br, wd), lambda hf, i, c: (i, 0))
    return pl.pallas_call(
        body, name=name,
        grid_spec=pltpu.PrefetchScalarGridSpec(
            num_scalar_prefetch=1, grid=(2, nblk),
            in_specs=[full, half, half, full, full], out_specs=[full] * 4),
        out_shape=[jax.ShapeDtypeStruct((rows, wd), F32)] * 4,
        compiler_params=_cp(("parallel", "parallel")),
    )(cidx, w, mine, other, m, v)


def _other_chips(x, y):
    return [(1 - x, y), (x, 1 - y), (1 - x, 1 - y)]


def _other_devices(x, y, c):
    flip = lambda v, d: (1 - v) if d else v
    return [(flip(x, dx), flip(y, dy), flip(c, dc))
            for dx in (0, 1) for dy in (0, 1) for dc in (0, 1) if (dx, dy, dc) != (0, 0, 0)]


def _exchange(name, ins, out_shapes, n_local, n_remote, plan):
    ni, no = len(ins), len(out_shapes)

    def body(*refs):
        in_refs, out_refs = refs[:ni], refs[ni:ni + no]
        send_sems, recv_sems, local_sems = refs[ni + no:]
        x, y, c = lax.axis_index("x"), lax.axis_index("y"), lax.axis_index("c")
        local, remote = plan(in_refs, out_refs, x, y, c)
        assert len(local) == n_local and len(remote) == n_remote

        def push(k, src, dst, dev):
            return pltpu.make_async_remote_copy(src_ref=src, dst_ref=dst, send_sem=send_sems.at[k],
                                                recv_sem=recv_sems.at[k], device_id=dev, device_id_type=MESH)

        own = [pltpu.make_async_copy(s, d, local_sems.at[i]) for i, (s, d) in enumerate(local)]
        for cp in own:
            cp.start()
        sends = [push(k, s, d, dev) for k, (s, d, dev, _) in enumerate(remote)]
        for cp in sends:
            cp.start()
        for k, (s, _, dev, landing) in enumerate(remote):
            push(k, s, landing, dev).wait_recv()
        for cp in sends:
            cp.wait_send()
        for cp in own:
            cp.wait()

    return pl.pallas_call(
        body, name=name, out_shape=out_shapes,
        in_specs=[_ANY] * ni, out_specs=[_ANY] * no,
        scratch_shapes=[pltpu.SemaphoreType.DMA((n_remote,)), pltpu.SemaphoreType.DMA((n_remote,)),
                        pltpu.SemaphoreType.DMA((max(n_local, 1),))],
        compiler_params=pltpu.CompilerParams(has_side_effects=True),
    )(*ins)


def _gather_chips(name, shards, everyone=()):
    ns, ne = len(shards), len(everyone)
    outs = [jax.ShapeDtypeStruct((4,) + a.shape, a.dtype) for a in shards]
    outs += [jax.ShapeDtypeStruct((8,) + a.shape, a.dtype) for a in everyone]

    def plan(i, o, x, y, c):
        mine, me = 2 * x + y, 4 * x + 2 * y + c
        local, remote = [], []
        for t in range(ns):
            local.append((i[t], o[t].at[mine]))
            for px, py in _other_chips(x, y):
                remote.append((i[t], o[t].at[mine], (px, py, c), o[t].at[2 * px + py]))
        for t in range(ns, ns + ne):
            local.append((i[t], o[t].at[me]))
            for px, py, pc in _other_devices(x, y, c):
                remote.append((i[t], o[t].at[me], (px, py, pc), o[t].at[4 * px + 2 * py + pc]))
        return local, remote

    return _exchange(name, list(shards) + list(everyone), outs, ns + ne, 3 * ns + 7 * ne, plan)


_HBM = pl.BlockSpec(memory_space=pltpu.HBM)
_SEM = pl.BlockSpec(memory_space=pltpu.SEMAPHORE)
_EFFECT = pltpu.SideEffectType.DATAFLOW_SIDE_EFFECTING


def _split_start(name, ins, land_shapes, n_remote, plan, after):
    ni, nl = len(ins), len(land_shapes)
    srcs = [pltpu.with_memory_space_constraint(a, pltpu.HBM) for a in ins]
    lands = [pltpu.with_memory_space_constraint(lax.empty(s.shape, s.dtype), pltpu.HBM) for s in land_shapes]

    def body(*refs):
        src, land = refs[:ni], refs[ni:ni + nl]
        first = ni + nl + 1
        send, recv = refs[first:first + n_remote], refs[first + n_remote:first + 2 * n_remote]
        token = refs[first + 2 * n_remote + ni + nl]
        x, y, c = lax.axis_index("x"), lax.axis_index("y"), lax.axis_index("c")
        remote = plan(src, land, x, y, c)
        assert len(remote) == n_remote
        for k, (s, d, dev, _) in enumerate(remote):
            pltpu.make_async_remote_copy(src_ref=s, dst_ref=d, send_sem=send[k], recv_sem=recv[k],
                                         device_id=dev, device_id_type=MESH).start()
        token[...] = jnp.zeros_like(token)

    out = pl.pallas_call(
        body, name=name + "_start",
        out_shape=[pltpu.SemaphoreType.DMA(())] * (2 * n_remote)
                  + [pltpu.HBM(a.shape, a.dtype) for a in ins] + [pltpu.HBM(s.shape, s.dtype) for s in land_shapes]
                  + [jax.ShapeDtypeStruct((8, 128), F32)],
        in_specs=[_HBM] * (ni + nl) + [_ANY], out_specs=[_SEM] * (2 * n_remote) + [_HBM] * (ni + nl) + [_VM],
        input_output_aliases={t: 2 * n_remote + t for t in range(ni + nl)},
        compiler_params=pltpu.CompilerParams(has_side_effects=_EFFECT),
    )(*srcs, *lands, after)
    sems, thru = out[:2 * n_remote], out[2 * n_remote:2 * n_remote + ni + nl]
    return (name, sems, thru[:ni], thru[ni:], n_remote, plan), out[-1]


def _split_wait(handle, after):
    name, sems, srcs, lands, n_remote, plan = handle
    ni, nl = len(srcs), len(lands)

    def body(*refs):
        src, land = refs[:ni], refs[ni:ni + nl]
        send, recv = refs[ni + nl:ni + nl + n_remote], refs[ni + nl + n_remote:ni + nl + 2 * n_remote]
        x, y, c = lax.axis_index("x"), lax.axis_index("y"), lax.axis_index("c")
        for k, (s, _, dev, landing) in enumerate(plan(src, land, x, y, c)):
            cp = pltpu.make_async_remote_copy(src_ref=s, dst_ref=landing, send_sem=send[k], recv_sem=recv[k],
                                              device_id=dev, device_id_type=MESH)
            cp.wait_send()
            cp.wait_recv()

    out = pl.pallas_call(
        body, name=name + "_wait",
        out_shape=[pltpu.HBM(a.shape, a.dtype) for a in srcs] + [pltpu.HBM(a.shape, a.dtype) for a in lands],
        in_specs=[_HBM] * (ni + nl) + [_SEM] * (2 * n_remote) + [_ANY], out_specs=[_HBM] * (ni + nl),
        input_output_aliases={t: t for t in range(ni + nl)},
        compiler_params=pltpu.CompilerParams(has_side_effects=_EFFECT),
    )(*srcs, *lands, *sems, after)
    return out[ni:]


def _plan_to_chips(src, land, x, y, c):
    mine = 2 * x + y
    return [(src[t], land[t].at[mine], (px, py, c), land[t].at[2 * px + py])
            for t in range(len(src)) for px, py in _other_chips(x, y)]


def _plan_swap_halves(src, land, x, y, c):
    out = []
    for t in range(len(src)):
        h = src[t].shape[1] // 2
        out.append((src[t].at[:, pl.ds(pl.multiple_of((1 - c) * h, 8), h), :], land[t], (x, y, 1 - c), land[t]))
    return out


def _plan_scatter_chips(src, land, x, y, c):
    mine = 2 * x + y
    return [(src[t].at[2 * px + py], land[t].at[mine], (px, py, c), land[t].at[2 * px + py])
            for t in range(len(src)) for px, py in _other_chips(x, y)]


def _swap_halves(gs, everyone):
    ns, ne = len(gs), len(everyone)
    outs = [jax.ShapeDtypeStruct((4, g.shape[1] // 2, g.shape[2]), g.dtype) for g in gs]
    outs += [jax.ShapeDtypeStruct((8,) + a.shape, a.dtype) for a in everyone]

    def plan(i, o, x, y, c):
        me = 4 * x + 2 * y + c
        local, remote = [], []
        for t in range(ns):
            h = gs[t].shape[1] // 2
            theirs = i[t].at[:, pl.ds(pl.multiple_of((1 - c) * h, 8), h), :]
            remote.append((theirs, o[t], (x, y, 1 - c), o[t]))
        for t in range(ns, ns + ne):
            local.append((i[t], o[t].at[me]))
            for px, py, pc in _other_devices(x, y, c):
                remote.append((i[t], o[t].at[me], (px, py, pc), o[t].at[4 * px + 2 * py + pc]))
        return local, remote

    return _exchange("grad_swap_sibling", list(gs) + list(everyone), outs, ne, ns + 7 * ne, plan)


def _scatter_chips(parts):
    ns = len(parts)
    outs = [jax.ShapeDtypeStruct(a.shape, a.dtype) for a in parts]

    def plan(i, o, x, y, c):
        mine = 2 * x + y
        local, remote = [], []
        for t in range(ns):
            local.append((i[t].at[mine], o[t].at[mine]))
            for px, py in _other_chips(x, y):
                remote.append((i[t].at[2 * px + py], o[t].at[mine], (px, py, c), o[t].at[2 * px + py]))
        return local, remote

    return _exchange("grad_scatter_chips", list(parts), outs, ns, 3 * ns, plan)


def _join_halves(halves):
    ns = len(halves)
    outs = [jax.ShapeDtypeStruct(a.shape, a.dtype) for a in halves]

    def plan(i, o, x, y, c):
        return [], [(i[t], o[t], (x, y, 1 - c), o[t]) for t in range(ns)]

    return _exchange("grad_join_sibling", list(halves), outs, 0, ns, plan)


def _pad_heads_cols(w, per, used):
    k = w.shape[0]
    w = w.reshape(k, NH, per)[:, :, :used]
    return jnp.pad(w, ((0, 0), (0, 0), (0, HP - used))).reshape(k, NH * HP)


def _unpad_heads_cols(w, used):
    k = w.shape[0]
    return w.reshape(k, NH, HP)[:, :, :used]


def _prep_weights(wf):
    bf = lambda a: a.astype(BF16)
    out = {}
    out["w_in"] = jnp.pad(bf(wf["w_in"]), ((0, 0), (0, IN_PAD - IN_COLS)))
    out["w_glu"] = bf(wf["w_glu"])
    out["w_uq"] = _pad_heads_cols(bf(wf["w_uq"]), QK_NOPE + QK_ROPE, QK_NOPE + QK_ROPE)
    wkv = bf(wf["w_ukv"]).reshape(KV_LORA, NH, QK_NOPE + V_HEAD)
    wk = jnp.pad(wkv[:, :, :QK_NOPE], ((0, 0), (0, 0), (0, HP - QK_NOPE))).reshape(KV_LORA, NH * HP)
    wv = jnp.pad(wkv[:, :, QK_NOPE:], ((0, 0), (0, 0), (0, HP - V_HEAD))).reshape(KV_LORA, NH * HP)
    out["w_ukv"] = jnp.concatenate([wk, wv], axis=1)
    return out


def _prep_late_weights(wf):
    bf = lambda a: a.astype(BF16)
    out = {}
    wo = bf(wf["w_out"])
    wo_a = jnp.pad(wo[D_SSM:].reshape(NH, V_HEAD, D), ((0, 0), (0, HP - V_HEAD), (0, 0))).reshape(NH * HP, D)
    out["w_out"] = jnp.concatenate([wo[:D_SSM], wo_a], axis=0)
    out["w_ff1"] = bf(wf["w_ff1"])
    out["w_ff2"] = bf(wf["w_ff2"])
    return out


def _rope_tables(positions):
    inv_freq = ROPE_BASE ** (-jnp.arange(0, QK_ROPE, 2, dtype=F32) / QK_ROPE)
    ang = positions.astype(F32)[:, None] * inv_freq
    cos, sin = jnp.cos(ang), jnp.sin(ang)
    n = positions.shape[0]
    one = jnp.ones((n, QK_NOPE), F32)
    z16 = jnp.zeros((n, 16), F32)
    z32 = jnp.zeros((n, 32), F32)
    z64 = jnp.zeros((n, QK_NOPE), F32)
    rc = jnp.concatenate([one, cos, cos, z32], axis=1)
    rs1 = jnp.concatenate([z64, -sin, z16, z32], axis=1)
    rs2 = jnp.concatenate([z64, z16, sin, z32], axis=1)
    return rc, rs1, rs2


def _permute_rows(a, S):
    n, w = a.shape
    return a.reshape(n // S, 8, S // 8, w).transpose(0, 2, 1, 3).reshape(n, w)


def _unpermute_rows(a, S):
    n, w = a.shape
    return a.reshape(n // S, S // 8, 8, w).transpose(0, 2, 1, 3).reshape(n, w)


def _block_diag_in(bb):
    eye = jnp.eye(8, dtype=bb.dtype)
    blocks = jnp.einsum("qgph,gk->qghkp", bb.reshape(4, 8, P, H), eye).reshape(4, QB, QS)
    return blocks.transpose(1, 0, 2).reshape(QB, NST)


def _block_diag_out(cc):
    eye = jnp.eye(8, dtype=cc.dtype)
    return jnp.einsum("qghp,gk->qgpkh", cc.reshape(4, 8, H, P), eye).reshape(NST, QB)


def _slots(full):
    r, cdim = full.shape
    return full.reshape(r, 4, cdim // 4).transpose(1, 0, 2)


def _unslots(g):
    s, r, cs = g.shape
    return g.transpose(1, 0, 2).reshape(r, s * cs)


def _local_step(x, positions, target, modp, wf, late_weights=None, reducer=None):
    nb, S, _ = x.shape
    n = nb * S
    tm = min(256, S)
    tt = min(256, S)
    tq = min(512, S // 2)
    kw = _prep_weights(wf)
    row = lambda a: a.reshape(1, -1).astype(F32)

    xf = x.reshape(n, D)
    tf = target.reshape(n, D)
    g1, g2, gf = row(wf["norm1_g"]), row(wf["norm2_g"]), row(wf["final_norm_g"])
    h1, proj = _f1_fwd(xf, modp, g1, kw["w_in"], S, tm)

    col = lambda a: a.reshape(NST, 1)
    lam_re, lam_im = col(wf["ssm_lambda_re"]), col(wf["ssm_lambda_im"])
    logdt = jnp.repeat(wf["ssm_log_dt"].reshape(G, 1), P, axis=1).reshape(NST, 1)
    b_re, b_im = wf["ssm_b_re"].reshape(NST, H), wf["ssm_b_im"].reshape(NST, H)
    lbr, lbi, bbr, bbi = _ssm_param_fwd(lam_re, lam_im, logdt, b_re, b_im)
    lre8 = jnp.broadcast_to(lbr.reshape(1, NST), (8, NST))
    lim8 = jnp.broadcast_to(lbi.reshape(1, NST), (8, NST))
    bm = jnp.concatenate([_block_diag_in(bbr.reshape(G, P, H)), _block_diag_in(bbi.reshape(G, P, H))],
                         axis=1).astype(BF16)
    cm = jnp.concatenate([_block_diag_out(wf["ssm_c_re"]), -_block_diag_out(wf["ssm_c_im"])], axis=0).astype(BF16)
    dvec = row(wf["ssm_d"])
    u_p = _permute_rows(proj[:, :D_SSM], S)
    fcr, fci = _ssm_local(u_p, bm, lre8, lim8, S, tt)
    st, ypre, z, gact, yssm_p = _ssm_fwd(u_p, fcr, fci, bm, cm, dvec, kw["w_glu"], lre8, lim8, S, tt)
    yssm = _unpermute_rows(yssm_p, S)

    rc, rs1, rs2 = _rope_tables(positions.reshape(n))
    gq, gkv = row(wf["q_norm_g"]), row(wf["kv_norm_g"])
    q, k, v, qn, kvn = _mla_fwd(proj, rc, rs1, rs2, gq, gkv, kw["w_uq"], kw["w_ukv"], tm)
    oattn, lrow = _attn_fwd(q, k, v, S, tq)

    gs = row(wf["ssm_out_g"])
    ga = jnp.pad(wf["attn_out_g"].reshape(NH, V_HEAD), ((0, 0), (0, HP - V_HEAD))).reshape(1, NH * HP)
    kw.update(_prep_late_weights(late_weights(oattn) if late_weights is not None else wf))
    yn, o, x1, h2 = _p1_fwd(yssm, oattn, xf, modp, gs, ga, kw["w_out"], g2, S, tm)
    dx1, r, da, dff, accs2, accg2 = _p2(x1, h2, tf, modp, g2, gf, kw["w_ff1"], kw["w_ff2"], S, tm)
    loss = jnp.sum(accg2[2])
    g_ff1 = _wgrad(h2, da, "wgrad_ff1", col_slots=4)
    g_ff2 = _wgrad(r, dff, "wgrad_ff2").reshape(4, D_FF // 4, D)
    gs_b, gq_b = gs, gq
    if reducer is not None:
        gs_b = gs + reducer.start([g_ff1, g_ff2])[0, 0]
    do, dyssm, dob, drow, accs3, accg3 = _p3_bwd(dx1, o, yssm, oattn, modp, gs_b, ga, kw["w_out"], S, tm)

    dq, dk, dv = _attn_bwd(q, k, v, dob, lrow, drow, S, tq)
    if reducer is not None:
        gq_b = gq + reducer.middle(dq)[0, 0]
    dmla, dqb, dkvb, accm = _mla_bwd(dq, dk, dv, proj, rc, rs1, rs2, gq_b, gkv, kw["w_uq"], kw["w_ukv"], tm)

    dys_p = _permute_rows(dyssm, S)
    dy, dz, air, aii = _ssm_bwd_a(dys_p, z, ypre, kw["w_glu"], cm, lre8, lim8, S, tt)
    du_p, dcm, dbm, dd, dlr, dli = _ssm_bwd_b(dy, u_p, st, fcr, fci, air, aii, bm, cm, dvec, lre8, lim8, S, tt)
    du = _unpermute_rows(du_p, S)
    dcm = dcm.reshape(2, 4, 8, P, 8, H)
    dc_re = jnp.einsum("qgpgh->qghp", dcm[0]).reshape(G, H, P)
    dc_im = -jnp.einsum("qgpgh->qghp", dcm[1]).reshape(G, H, P)
    dbm = dbm.reshape(8, H, 2, 4, 8, P)
    dbb_re = jnp.einsum("ghqgp->qgph", dbm[:, :, 0]).reshape(NST, H)
    dbb_im = jnp.einsum("ghqgp->qgph", dbm[:, :, 1]).reshape(NST, H)
    gb_re, gb_im, glr, gli, gdt = _ssm_param_bwd(lam_re, lam_im, logdt, b_re, b_im, dlr.reshape(NST, 1),
                                                 dli.reshape(NST, 1), dbb_re, dbb_im)
    glogdt = _rowsum(gdt.reshape(G, P))

    dx, dproj, accs1, accg1 = _f1_bwd(du, dmla, dx1, xf, modp, g1, kw["w_in"], S, tm)

    big = {}
    big["w_in"] = _slots(_wgrad(h1, dproj, "wgrad_in")[:, :IN_COLS])
    big["w_glu"] = _wgrad(gact, dz, "wgrad_glu", col_slots=4)
    big["w_uq"] = _slots(_unpad_heads_cols(_wgrad(qn, dqb, "wgrad_uq"), QK_NOPE + QK_ROPE).reshape(Q_LORA, -1))
    gkvw = _wgrad(kvn, dkvb, "wgrad_ukv")
    big["w_ukv"] = _slots(jnp.concatenate([_unpad_heads_cols(gkvw[:, :NH * HP], QK_NOPE),
                                           _unpad_heads_cols(gkvw[:, NH * HP:], V_HEAD)], axis=2).reshape(KV_LORA, -1))
    gwo = _wgrad(yn, do, "wgrad_out")
    big["w_out"] = jnp.concatenate([gwo[:D_SSM].reshape(2, D_SSM // 2, D),
                                    gwo[D_SSM:].reshape(2, NH // 2 * HP, D).reshape(2, NH // 2, HP, D)[:, :, :V_HEAD]
                                    .reshape(2, D_ATTN // 2, D)], axis=0)
    big["w_ff1"] = g_ff1
    big["w_ff2"] = g_ff2

    small = {}
    small["norm1_g"] = accg1[0:1]
    small["norm2_g"] = accg2[0:1]
    small["final_norm_g"] = accg2[1:2]
    small["ssm_out_g"] = accg3[0:1, :D_SSM]
    small["attn_out_g"] = accg3[1].reshape(NH, HP)[:, :V_HEAD].reshape(1, D_ATTN)
    small["q_norm_g"] = accm[0:1, :Q_LORA]
    small["kv_norm_g"] = accm[1:2, :KV_LORA]
    small["ssm_lambda_re"] = glr.reshape(G, P)
    small["ssm_lambda_im"] = gli.reshape(G, P)
    small["ssm_b_re"] = gb_re
    small["ssm_b_im"] = gb_im
    small["ssm_c_re"] = dc_re.reshape(G * H, P)
    small["ssm_c_im"] = dc_im.reshape(G * H, P)
    small["ssm_d"] = dd.reshape(G, H)
    small["ssm_log_dt"] = glogdt.reshape(1, G)
    return loss, dx.reshape(nb, S, D), big, small, accs1 + accs2 + accs3


def _view2d(a):
    return a.reshape(-1, a.shape[-1]) if a.ndim > 1 else a.reshape(1, -1)


def kernel(x, c, positions, ada_w, ada_b, norm1_g, w_in, ssm_lambda_re, ssm_lambda_im, ssm_b_re, ssm_b_im, ssm_c_re, ssm_c_im, ssm_d, ssm_log_dt, w_glu, q_norm_g, w_uq, kv_norm_g, w_ukv, ssm_out_g, attn_out_g, w_out, norm2_g, w_ff1, w_ff2, final_ada_w, final_ada_b, final_norm_g, loss_target, m_ada_w, m_ada_b, m_norm1_g, m_w_in, m_ssm_lambda_re, m_ssm_lambda_im, m_ssm_b_re, m_ssm_b_im, m_ssm_c_re, m_ssm_c_im, m_ssm_d, m_ssm_log_dt, m_w_glu, m_q_norm_g, m_w_uq, m_kv_norm_g, m_w_ukv, m_ssm_out_g, m_attn_out_g, m_w_out, m_norm2_g, m_w_ff1, m_w_ff2, m_final_ada_w, m_final_ada_b, m_final_norm_g, v_ada_w, v_ada_b, v_norm1_g, v_w_in, v_ssm_lambda_re, v_ssm_lambda_im, v_ssm_b_re, v_ssm_b_im, v_ssm_c_re, v_ssm_c_im, v_ssm_d, v_ssm_log_dt, v_w_glu, v_q_norm_g, v_w_uq, v_kv_norm_g, v_w_ukv, v_ssm_out_g, v_attn_out_g, v_w_out, v_norm2_g, v_w_ff1, v_w_ff2, v_final_ada_w, v_final_ada_b, v_final_norm_g):
    args = dict(locals())
    names = list(inspect.signature(kernel).parameters)
    wnames = names[3:names.index("loss_target")]
    small_names = [nm for nm in wnames if nm not in GATHERED and nm not in TP]
    reduced_names = [nm for nm in small_names if nm not in ("ada_b", "final_ada_b")]
    w = {nm: args[nm] for nm in wnames}
    m = {nm: args["m_" + nm] for nm in wnames}
    v = {nm: args["v_" + nm] for nm in wnames}
    nb = x.shape[0]
    xi, yi, ci = lax.axis_index("x"), lax.axis_index("y"), lax.axis_index("c")
    chip, me = 2 * xi + yi, 4 * xi + 2 * yi + ci

    unslot = lambda nm, g: g.reshape(-1, g.shape[-1]) if nm in ROW_SHARDED else _unslots(g)
    early = [nm for nm in GATHERED if nm not in LATE]
    got = _gather_chips("gather_weights", [_view2d(w[nm]).astype(BF16) for nm in early], [c])
    wf = {nm: unslot(nm, g) for nm, g in zip(early, got)}
    for nm in small_names:
        wf[nm] = w[nm][0] if w[nm].ndim > 1 else w[nm]
    c_all = got[len(early)].reshape(8 * nb, D)

    na, nf = ada_w.shape[-1], final_ada_w.shape[-1]
    ada_b_s = lax.dynamic_slice(ada_b, (0, chip * na), (1, na))
    fada_b_s = lax.dynamic_slice(final_ada_b.reshape(1, -1), (0, chip * nf), (1, nf))
    cond_all, modcols = _mod_fwd(c_all, ada_w[0], ada_b_s, final_ada_w, fada_b_s)
    (mod_g,) = _gather_chips("gather_mod", [modcols])
    mine = lax.dynamic_slice(mod_g, (0, me * nb, 0), (4, nb, na + nf))
    modp = jnp.concatenate([mine[:, :, :na].transpose(1, 0, 2).reshape(nb, 6, D),
                            mine[:, :, na:].transpose(1, 0, 2).reshape(nb, 2, D)], axis=1)

    own_late = [_view2d(w[nm]).astype(BF16) for nm in LATE]
    late_gather, token = _split_start("gather_late", own_late,
                                      [jax.ShapeDtypeStruct((4,) + a.shape, a.dtype) for a in own_late],
                                      3 * len(LATE), _plan_to_chips, modp)
    modp = modp + token[0, 0]

    def late_weights(after):
        landed = _split_wait(late_gather, after)
        return {nm: unslot(nm, lax.dynamic_update_slice(g, own[None], (chip, 0, 0)))
                for nm, g, own in zip(LATE, landed, own_late)}

    cidx = ci.astype(jnp.int32).reshape(1)
    ahead = ["w_ff1", "w_ff2"]

    class Reducer:
        def start(self, gs):
            self.gs = gs
            lands = [jax.ShapeDtypeStruct((4, g.shape[1] // 2, g.shape[2]), g.dtype) for g in gs]
            self.swap, tok = _split_start("grad_swap_ff", gs, lands, len(gs), _plan_swap_halves, modp)
            return tok

        def middle(self, after):
            got = _split_wait(self.swap, after)
            self.sums = [_add_half(g, r, cidx, "grad_add_sibling_" + nm) for nm, g, r in zip(ahead, self.gs, got)]
            lands = [jax.ShapeDtypeStruct(s.shape, s.dtype) for s in self.sums]
            self.scatter, tok = _split_start("grad_scatter_ff", self.sums, lands, 3 * len(self.sums),
                                             _plan_scatter_chips, modp)
            return tok

        def finish(self, after):
            out = []
            for nm, s, l in zip(ahead, self.sums, _split_wait(self.scatter, after)):
                own = lax.dynamic_slice(s, (chip, 0, 0), (1,) + s.shape[1:])
                out.append(_add_chips(lax.dynamic_update_slice(l, own, (chip, 0, 0)), "grad_add_chips_" + nm))
            return out

    reducer = Reducer()
    loss, grad_x, big, small, dmodp = _local_step(x, positions, loss_target, modp, wf, late_weights, reducer)
    loss = lax.psum(loss, ("x", "y", "c"))

    rest = [nm for nm in GATHERED if nm not in ahead]
    sizes = [small[nm].size for nm in reduced_names]
    pad = -sum(sizes) % 128
    packed = jnp.concatenate([small[nm].reshape(1, -1) for nm in reduced_names] + [jnp.zeros((1, pad), F32)], axis=1)
    swapped = _swap_halves([big[nm] for nm in rest], [dmodp.reshape(nb, 8 * D), packed])
    chip_sums = [_add_half(big[nm], r, cidx, "grad_add_sibling_" + nm) for nm, r in zip(rest, swapped)]
    half_of = {nm: _add_chips(r, "grad_add_chips_" + nm) for nm, r in zip(rest, _scatter_chips(chip_sums))}
    half_of.update(zip(ahead, reducer.finish(grad_x)))
    halves = [half_of[nm] for nm in GATHERED]
    others = _join_halves(halves)
    grads = {}
    dmod_all = swapped[len(rest)].reshape(8 * nb, 8 * D)
    small_sum = _sum_devices(swapped[len(rest) + 1].reshape(8, -1))
    off = 0
    for nm, sz in zip(reduced_names, sizes):
        grads[nm] = small_sum[:, off:off + sz].reshape(small[nm].shape)
        off += sz

    dsl = jnp.concatenate([lax.dynamic_slice(dmod_all, (0, chip * na), (8 * nb, na)),
                           lax.dynamic_slice(dmod_all, (0, 6 * D + chip * nf), (8 * nb, nf))], axis=1)
    gw, gb = _mod_bwd(cond_all.T, dsl, dmod_all)
    grads["ada_w"], grads["final_ada_w"] = gw[:, :na], gw[:, na:]
    grads["ada_b"], grads["final_ada_b"] = gb[:, :6 * D], gb[:, 6 * D:]

    delta, new_m, new_v = {}, {}, {}
    for nm, mine_h, other_h in zip(GATHERED, halves, others):
        grads[nm], delta[nm], new_m[nm], new_v[nm] = _adamw_halves(
            _view2d(w[nm]), mine_h, other_h, _view2d(m[nm]), _view2d(v[nm]), cidx, "adamw_" + nm)
    for nm in TP:
        delta[nm], new_m[nm], new_v[nm] = _adamw(_view2d(w[nm]), grads[nm], _view2d(m[nm]), _view2d(v[nm]),
                                                  "adamw_" + nm)
    upd = _adamw_small([_view2d(w[nm]) for nm in small_names], [grads[nm] for nm in small_names],
                       [_view2d(m[nm]) for nm in small_names], [_view2d(v[nm]) for nm in small_names])
    k = len(small_names)
    for t, nm in enumerate(small_names):
        delta[nm], new_m[nm], new_v[nm] = upd[t], upd[k + t], upd[2 * k + t]

    outs = [grads, delta, new_m, new_v]
    return (loss, grad_x, *[d[nm].reshape(w[nm].shape) for d in outs for nm in wnames])
```

```python
import functools
import inspect
import math

import jax
import jax.numpy as jnp
from jax import lax
from jax.experimental import pallas as pl
from jax.experimental.pallas import tpu as pltpu

F32 = jnp.float32
BF16 = jnp.bfloat16

D = 1024
D_SSM = 512
G = 32
H = 16
P = 64
NST = G * P
D_ATTN = 512
NH = 8
QK_NOPE = 64
QK_ROPE = 32
V_HEAD = 64
HP = 128
Q_LORA = 384
KV_LORA = 256
IN_COLS = D_SSM + Q_LORA + KV_LORA + QK_ROPE
IN_PAD = 1280
D_FF = 4096
ROPE_BASE = 10000.0
EPS = 1e-6
ADAM_LR = 0.001
ADAM_B1 = 0.9
ADAM_B2 = 0.999
ADAM_EPS = 1e-08
ADAM_WD = 0.01
ADAM_STEP = 10
NEG = -1e30
VMEM_LIMIT = 60 << 20

MESH = pl.DeviceIdType.MESH
_VM = pl.BlockSpec(memory_space=pltpu.VMEM)
_ANY = pl.BlockSpec(memory_space=pl.ANY)

GATHERED = ["w_in", "w_glu", "w_uq", "w_ukv", "w_out", "w_ff1", "w_ff2"]
TP = ["ada_w", "final_ada_w"]
ROW_SHARDED = ("w_out", "w_ff2")
LATE = ["w_out", "w_ff1", "w_ff2"]


def _cp(sem=None, vmem=VMEM_LIMIT):
    kw = dict(vmem_limit_bytes=vmem)
    if sem is not None:
        kw["dimension_semantics"] = sem
    return pltpu.CompilerParams(**kw)


def _dot(a, b):
    return jnp.dot(a, b, preferred_element_type=F32)


def _dot_nt(a, b):
    return lax.dot_general(a, b, (((1,), (1,)), ((), ())), preferred_element_type=F32)


def _dot_tn(a, b):
    return lax.dot_general(a, b, (((0,), (0,)), ((), ())), preferred_element_type=F32)


def _rms(x, n):
    r = lax.rsqrt(jnp.sum(x * x, axis=-1, keepdims=True) * (1.0 / n) + EPS)
    return x * r, r


def _rms_bwd(dyg, xhat, r, n):
    return r * (dyg - xhat * (jnp.sum(dyg * xhat, axis=-1, keepdims=True) * (1.0 / n)))


def _sigmoid(x):
    return 1.0 / (1.0 + jnp.exp(-x))


_GK = math.sqrt(2.0 / math.pi)
_GC = 0.044715


def _gelu(y):
    t = jnp.tanh(_GK * (y + _GC * y * y * y))
    return 0.5 * y * (1.0 + t)


def _gelu_grad(y):
    t = jnp.tanh(_GK * (y + _GC * y * y * y))
    return 0.5 * (1.0 + t) + 0.5 * y * (1.0 - t * t) * _GK * (1.0 + 3.0 * _GC * y * y)


def _colsum(x):
    return jnp.sum(x, axis=0, keepdims=True)


def _roll(x, s):
    return pltpu.roll(x, s % x.shape[-1], x.ndim - 1)


def _mod_fwd(c_all, ada_w_s, ada_b_s, fada_w_s, fada_b_s):
    nseq = c_all.shape[0]
    na, nf = ada_w_s.shape[1], fada_w_s.shape[1]

    def body(c_ref, w_ref, b_ref, fw_ref, fb_ref, cond_ref, mod_ref):
        cv = c_ref[...]
        cond = cv * _sigmoid(cv)
        cond_ref[...] = cond
        cb = cond.astype(BF16)
        mod_ref[:, 0:na] = _dot(cb, w_ref[...].astype(BF16)) + b_ref[...]
        mod_ref[:, na:na + nf] = _dot(cb, fw_ref[...].astype(BF16)) + fb_ref[...]

    return pl.pallas_call(
        body, name="mod_fwd",
        out_shape=[jax.ShapeDtypeStruct((nseq, D), F32), jax.ShapeDtypeStruct((nseq, na + nf), F32)],
        in_specs=[_VM] * 5, out_specs=[_VM] * 2, compiler_params=_cp(),
    )(c_all, ada_w_s, ada_b_s, fada_w_s, fada_b_s)


def _mod_bwd(cond_t, dsl, dall):
    nseq, n = dsl.shape
    bc = 512

    def body(ct_ref, dm_ref, da_ref, gw_ref, gb_ref):
        ct = ct_ref[...]
        dm = dm_ref[...]
        acc = ct[:, 0:1] * dm[0:1, :]
        for b in range(1, nseq):
            acc = acc + ct[:, b:b + 1] * dm[b:b + 1, :]
        gw_ref[...] = acc

        @pl.when(pl.program_id(0) == 0)
        def _():
            gb_ref[...] = _colsum(da_ref[...])

    return pl.pallas_call(
        body, name="mod_bwd", grid=(n // bc,),
        out_shape=[jax.ShapeDtypeStruct((D, n), F32), jax.ShapeDtypeStruct((1, dall.shape[1]), F32)],
        in_specs=[_VM, pl.BlockSpec((nseq, bc), lambda i: (0, i)), _VM],
        out_specs=[pl.BlockSpec((D, bc), lambda i: (0, i)), pl.BlockSpec((1, dall.shape[1]), lambda i: (0, 0))],
        compiler_params=_cp(("arbitrary",)),
    )(cond_t, dsl, dall)


def _f1_fwd(x, modp, g1, w_in, S, tm):
    n = x.shape[0]
    tps = S // tm

    def body(x_ref, mod_ref, g_ref, w_ref, h_ref, proj_ref):
        xhat, _ = _rms(x_ref[...], D)
        h = (xhat * g_ref[...]) * (1.0 + mod_ref[0, 1:2, :]) + mod_ref[0, 0:1, :]
        hb = h.astype(BF16)
        h_ref[...] = hb
        proj_ref[...] = _dot(hb, w_ref[...])

    return pl.pallas_call(
        body, name="f1_fwd", grid=(n // tm,),
        out_shape=[jax.ShapeDtypeStruct((n, D), BF16), jax.ShapeDtypeStruct((n, IN_PAD), F32)],
        in_specs=[pl.BlockSpec((tm, D), lambda i: (i, 0)),
                  pl.BlockSpec((1, 8, D), lambda i: (i // tps, 0, 0)), _VM, _VM],
        out_specs=[pl.BlockSpec((tm, D), lambda i: (i, 0)), pl.BlockSpec((tm, IN_PAD), lambda i: (i, 0))],
        compiler_params=_cp(("parallel",)),
    )(x, modp, g1, w_in)


def _f1_bwd(du, dmla, dx1, x, modp, g1, w_in, S, tm):
    n = x.shape[0]
    tps = S // tm
    nb = n // S

    def body(du_ref, dm_ref, dx1_ref, x_ref, mod_ref, g_ref, w_ref, dx_ref, dproj_ref, accs_ref, accg_ref):
        i = pl.program_id(0)
        dproj = jnp.concatenate([du_ref[...], dm_ref[...]], axis=1).astype(BF16)
        dproj_ref[...] = dproj
        dh = _dot_nt(dproj, w_ref[...])
        xhat, r = _rms(x_ref[...], D)
        g = g_ref[...]
        dn = dh * (1.0 + mod_ref[0, 1:2, :])
        dx_ref[...] = dx1_ref[...] + _rms_bwd(dn * g, xhat, r, D)

        @pl.when(i % tps == 0)
        def _():
            accs_ref[...] = jnp.zeros_like(accs_ref)

        @pl.when(i == 0)
        def _():
            accg_ref[...] = jnp.zeros_like(accg_ref)

        accs_ref[0, 0:1, :] += _colsum(dh)
        accs_ref[0, 1:2, :] += _colsum(dh * (xhat * g))
        accg_ref[0:1, :] += _colsum(dn * xhat)

    return pl.pallas_call(
        body, name="f1_bwd", grid=(n // tm,),
        out_shape=[jax.ShapeDtypeStruct((n, D), F32), jax.ShapeDtypeStruct((n, IN_PAD), BF16),
                   jax.ShapeDtypeStruct((nb, 8, D), F32), jax.ShapeDtypeStruct((8, D), F32)],
        in_specs=[pl.BlockSpec((tm, D_SSM), lambda i: (i, 0)), pl.BlockSpec((tm, IN_PAD - D_SSM), lambda i: (i, 0)),
                  pl.BlockSpec((tm, D), lambda i: (i, 0)), pl.BlockSpec((tm, D), lambda i: (i, 0)),
                  pl.BlockSpec((1, 8, D), lambda i: (i // tps, 0, 0)), _VM, _VM],
        out_specs=[pl.BlockSpec((tm, D), lambda i: (i, 0)), pl.BlockSpec((tm, IN_PAD), lambda i: (i, 0)),
                   pl.BlockSpec((1, 8, D), lambda i: (i // tps, 0, 0)), pl.BlockSpec((8, D), lambda i: (0, 0))],
        compiler_params=_cp(("arbitrary",)),
    )(du, dmla, dx1, x, modp, g1, w_in)


def _ssm_param_fwd(lam_re, lam_im, logdt, b_re, b_im):
    def body(lr_ref, li_ref, ld_ref, br_ref, bi_ref, lbr_ref, lbi_ref, bbr_ref, bbi_ref):
        lr, li = lr_ref[...], li_ref[...]
        dt = jnp.exp(ld_ref[...])
        er = jnp.exp(lr * dt)
        lbr = er * jnp.cos(li * dt)
        lbi = er * jnp.sin(li * dt)
        den = 1.0 / (lr * lr + li * li)
        cr = ((lbr - 1.0) * lr + lbi * li) * den
        ci = (lbi * lr - (lbr - 1.0) * li) * den
        lbr_ref[...] = lbr
        lbi_ref[...] = lbi
        bbr_ref[...] = cr * br_ref[...] - ci * bi_ref[...]
        bbi_ref[...] = cr * bi_ref[...] + ci * br_ref[...]

    return pl.pallas_call(
        body, name="ssm_param_fwd",
        out_shape=[jax.ShapeDtypeStruct((NST, 1), F32)] * 2 + [jax.ShapeDtypeStruct((NST, H), F32)] * 2,
        in_specs=[_VM] * 5, out_specs=[_VM] * 4, compiler_params=_cp(),
    )(lam_re, lam_im, logdt, b_re, b_im)


def _ssm_param_bwd(lam_re, lam_im, logdt, b_re, b_im, dlb_re, dlb_im, dbb_re, dbb_im):
    def body(lr_ref, li_ref, ld_ref, br_ref, bi_ref, dlr_ref, dli_ref, dbr_ref, dbi_ref,
             gbr_ref, gbi_ref, glr_ref, gli_ref, gdt_ref):
        lr, li = lr_ref[...], li_ref[...]
        dt = jnp.exp(ld_ref[...])
        er = jnp.exp(lr * dt)
        lbr = er * jnp.cos(li * dt)
        lbi = er * jnp.sin(li * dt)
        den = 1.0 / (lr * lr + li * li)
        nr, ni = lbr - 1.0, lbi
        cr = (nr * lr + ni * li) * den
        ci = (ni * lr - nr * li) * den
        br, bi = br_ref[...], bi_ref[...]
        dbr, dbi = dbr_ref[...], dbi_ref[...]
        gbr_ref[...] = cr * dbr + ci * dbi
        gbi_ref[...] = cr * dbi - ci * dbr
        gcr = jnp.sum(dbr * br + dbi * bi, axis=1, keepdims=True)
        gci = jnp.sum(dbi * br - dbr * bi, axis=1, keepdims=True)
        ilr, ili = lr * den, -li * den
        glbr = dlr_ref[...] + (gcr * ilr + gci * ili)
        glbi = dli_ref[...] + (gci * ilr - gcr * ili)
        qr = -(cr * ilr - ci * ili)
        qi = -(cr * ili + ci * ilr)
        glr = gcr * qr + gci * qi
        gli = gci * qr - gcr * qi
        glr = glr + dt * (glbr * lbr + glbi * lbi)
        gli = gli + dt * (glbi * lbr - glbr * lbi)
        wr = lr * lbr - li * lbi
        wi = lr * lbi + li * lbr
        glr_ref[...] = glr
        gli_ref[...] = gli
        gdt_ref[...] = (glbr * wr + glbi * wi) * dt

    return pl.pallas_call(
        body, name="ssm_param_bwd",
        out_shape=[jax.ShapeDtypeStruct((NST, H), F32)] * 2 + [jax.ShapeDtypeStruct((NST, 1), F32)] * 3,
        in_specs=[_VM] * 9, out_specs=[_VM] * 5, compiler_params=_cp(),
    )(lam_re, lam_im, logdt, b_re, b_im, dlb_re, dlb_im, dbb_re, dbb_im)


def _rowsum(a):
    def body(a_ref, o_ref):
        o_ref[...] = jnp.sum(a_ref[...], axis=1, keepdims=True)

    return pl.pallas_call(
        body, name="rowsum", out_shape=jax.ShapeDtypeStruct((a.shape[0], 1), F32),
        in_specs=[_VM], out_specs=_VM, compiler_params=_cp(),
    )(a)


QB = D_SSM // 4
QS = 4 * QB


def _bd_lo(part, q):
    return part * NST + q * QS


def _bd_expand(ub, bm_ref, out_ref):
    for part in range(2):
        for q in range(4):
            lo = _bd_lo(part, q)
            out_ref[:, lo:lo + QS] = _dot(ub[:, q * QB:(q + 1) * QB], bm_ref[:, lo:lo + QS])


def _bd_expand_t(db, cm_ref, out_ref):
    for part in range(2):
        for q in range(4):
            lo = _bd_lo(part, q)
            out_ref[:, lo:lo + QS] = _dot_nt(db[:, q * QB:(q + 1) * QB], cm_ref[lo:lo + QS, :])


def _bd_project(sb, cm_ref):
    return jnp.concatenate(
        [_dot(sb[:, _bd_lo(0, q):_bd_lo(0, q) + QS], cm_ref[_bd_lo(0, q):_bd_lo(0, q) + QS, :])
         + _dot(sb[:, _bd_lo(1, q):_bd_lo(1, q) + QS], cm_ref[_bd_lo(1, q):_bd_lo(1, q) + QS, :])
         for q in range(4)], axis=1)


def _bd_project_t(ab, bm_ref):
    return jnp.concatenate(
        [_dot_nt(ab[:, _bd_lo(0, q):_bd_lo(0, q) + QS], bm_ref[:, _bd_lo(0, q):_bd_lo(0, q) + QS])
         + _dot_nt(ab[:, _bd_lo(1, q):_bd_lo(1, q) + QS], bm_ref[:, _bd_lo(1, q):_bd_lo(1, q) + QS])
         for q in range(4)], axis=1)


def _pow2k(pr, pi, nsq):
    for _ in range(nsq):
        pr, pi = pr * pr - pi * pi, 2.0 * pr * pi
    return pr, pi


def _ssm_local(u_p, bm, lre8, lim8, S, tt):
    n = u_p.shape[0]
    nb, nt = n // S, S // tt
    nsq = int(round(math.log2(S // 8)))
    assert 2 ** nsq == S // 8

    def body(u_ref, bm_ref, lre_ref, lim_ref, cre_ref, cim_ref, sre, sim, bu):
        j = pl.program_id(1)

        @pl.when(j == 0)
        def _():
            sre[...] = jnp.zeros_like(sre)
            sim[...] = jnp.zeros_like(sim)

        _bd_expand(u_ref[...].astype(BF16), bm_ref, bu)
        lre, lim = lre_ref[...], lim_ref[...]

        def step(i, c):
            sr, si = c
            off = pl.multiple_of(i * 8, 8)
            br = bu[pl.ds(off, 8), 0:NST]
            bi = bu[pl.ds(off, 8), NST:2 * NST]
            return lre * sr - lim * si + br, lre * si + lim * sr + bi

        sr, si = lax.fori_loop(0, tt // 8, step, (sre[...], sim[...]))
        sre[...] = sr
        sim[...] = si

        @pl.when(j == nt - 1)
        def _():
            pr, pi = _pow2k(lre[0:1], lim[0:1], nsq)
            cr = jnp.zeros((1, NST), F32)
            ci = jnp.zeros((1, NST), F32)
            cre_ref[0:1, :] = cr
            cim_ref[0:1, :] = ci
            for k in range(1, 8):
                cr, ci = sr[k - 1:k] + pr * cr - pi * ci, si[k - 1:k] + pr * ci + pi * cr
                cre_ref[k:k + 1, :] = cr
                cim_ref[k:k + 1, :] = ci

    return pl.pallas_call(
        body, name="ssm_local", grid=(nb, nt),
        out_shape=[jax.ShapeDtypeStruct((nb * 8, NST), F32)] * 2,
        in_specs=[pl.BlockSpec((tt, D_SSM), lambda b, j: (b * nt + j, 0)), _VM, _VM, _VM],
        out_specs=[pl.BlockSpec((8, NST), lambda b, j: (b, 0))] * 2,
        scratch_shapes=[pltpu.VMEM((8, NST), F32), pltpu.VMEM((8, NST), F32), pltpu.VMEM((tt, 2 * NST), F32)],
        compiler_params=_cp(("arbitrary", "arbitrary")),
    )(u_p, bm, lre8, lim8)


def _ssm_fwd(u_p, cre, cim, bm, cm, dvec, w_glu, lre8, lim8, S, tt):
    n = u_p.shape[0]
    nb, nt = n // S, S // tt

    def body(u_ref, cre_ref, cim_ref, bm_ref, cm_ref, d_ref, wg_ref, lre_ref, lim_ref,
             st_ref, ypre_ref, z_ref, gact_ref, yssm_ref, sre, sim, bu):
        j = pl.program_id(1)

        @pl.when(j == 0)
        def _():
            sre[...] = cre_ref[...]
            sim[...] = cim_ref[...]

        u = u_ref[...]
        _bd_expand(u.astype(BF16), bm_ref, bu)
        lre, lim = lre_ref[...], lim_ref[...]

        def step(i, c):
            sr, si = c
            off = pl.multiple_of(i * 8, 8)
            nr = lre * sr - lim * si + bu[pl.ds(off, 8), 0:NST]
            ni = lre * si + lim * sr + bu[pl.ds(off, 8), NST:2 * NST]
            st_ref[pl.ds(off, 8), 0:NST] = nr
            st_ref[pl.ds(off, 8), NST:2 * NST] = ni
            return nr, ni

        sr, si = lax.fori_loop(0, tt // 8, step, (sre[...], sim[...]))
        sre[...] = sr
        sim[...] = si
        y = _bd_project(st_ref[...].astype(BF16), cm_ref) + d_ref[...] * u
        ypre_ref[...] = y
        gb = _gelu(y).astype(BF16)
        gact_ref[...] = gb
        z = _dot(gb, wg_ref[...])
        z_ref[...] = z
        yssm_ref[...] = z[:, 0:D_SSM] * _sigmoid(z[:, D_SSM:2 * D_SSM])

    row = lambda w: pl.BlockSpec((tt, w), lambda b, j: (b * nt + j, 0))
    return pl.pallas_call(
        body, name="ssm_fwd", grid=(nb, nt),
        out_shape=[jax.ShapeDtypeStruct((n, 2 * NST), F32), jax.ShapeDtypeStruct((n, D_SSM), F32),
                   jax.ShapeDtypeStruct((n, 2 * D_SSM), F32), jax.ShapeDtypeStruct((n, D_SSM), BF16),
                   jax.ShapeDtypeStruct((n, D_SSM), F32)],
        in_specs=[row(D_SSM), pl.BlockSpec((8, NST), lambda b, j: (b, 0)), pl.BlockSpec((8, NST), lambda b, j: (b, 0)),
                  _VM, _VM, _VM, _VM, _VM, _VM],
        out_specs=[row(2 * NST), row(D_SSM), row(2 * D_SSM), row(D_SSM), row(D_SSM)],
        scratch_shapes=[pltpu.VMEM((8, NST), F32), pltpu.VMEM((8, NST), F32), pltpu.VMEM((tt, 2 * NST), F32)],
        compiler_params=_cp(("arbitrary", "arbitrary")),
    )(u_p, cre, cim, bm, cm, dvec, w_glu, lre8, lim8)


def _ssm_bwd_a(dys_p, z, ypre, w_glu, cm, lre8, lim8, S, tt):
    n = z.shape[0]
    nb, nt = n // S, S // tt
    nsq = int(round(math.log2(S // 8)))
    ng = tt // 8

    def body(dys_ref, z_ref, y_ref, wg_ref, cm_ref, lre_ref, lim_ref, dy_ref, dz_ref, are_ref, aim_ref, sre, sim, gb):
        j = pl.program_id(1)

        @pl.when(j == 0)
        def _():
            sre[...] = jnp.zeros_like(sre)
            sim[...] = jnp.zeros_like(sim)

        z = z_ref[...]
        z1, z2 = z[:, 0:D_SSM], z[:, D_SSM:2 * D_SSM]
        sg = _sigmoid(z2)
        dys = dys_ref[...]
        dz = jnp.concatenate([dys * sg, dys * z1 * sg * (1.0 - sg)], axis=1).astype(BF16)
        dz_ref[...] = dz
        dy = _dot_nt(dz, wg_ref[...]) * _gelu_grad(y_ref[...])
        dy_ref[...] = dy
        _bd_expand_t(dy.astype(BF16), cm_ref, gb)
        lre, lim = lre_ref[...], lim_ref[...]

        def step(i, c):
            ar, ai = c
            off = pl.multiple_of((ng - 1 - i) * 8, 8)
            gr = gb[pl.ds(off, 8), 0:NST]
            gi = gb[pl.ds(off, 8), NST:2 * NST]
            return lre * ar + lim * ai + gr, lre * ai - lim * ar + gi

        ar, ai = lax.fori_loop(0, ng, step, (sre[...], sim[...]))
        sre[...] = ar
        sim[...] = ai

        @pl.when(j == nt - 1)
        def _():
            pr, pi = _pow2k(lre[0:1], -lim[0:1], nsq)
            cr = jnp.zeros((1, NST), F32)
            ci = jnp.zeros((1, NST), F32)
            are_ref[7:8, :] = cr
            aim_ref[7:8, :] = ci
            for k in range(6, -1, -1):
                cr, ci = ar[k + 1:k + 2] + pr * cr - pi * ci, ai[k + 1:k + 2] + pr * ci + pi * cr
                are_ref[k:k + 1, :] = cr
                aim_ref[k:k + 1, :] = ci

    row = lambda w: pl.BlockSpec((tt, w), lambda b, j: (b * nt + nt - 1 - j, 0))
    return pl.pallas_call(
        body, name="ssm_bwd_a", grid=(nb, nt),
        out_shape=[jax.ShapeDtypeStruct((n, D_SSM), F32), jax.ShapeDtypeStruct((n, 2 * D_SSM), BF16),
                   jax.ShapeDtypeStruct((nb * 8, NST), F32), jax.ShapeDtypeStruct((nb * 8, NST), F32)],
        in_specs=[row(D_SSM), row(2 * D_SSM), row(D_SSM), _VM, _VM, _VM, _VM],
        out_specs=[row(D_SSM), row(2 * D_SSM), pl.BlockSpec((8, NST), lambda b, j: (b, 0)),
                   pl.BlockSpec((8, NST), lambda b, j: (b, 0))],
        scratch_shapes=[pltpu.VMEM((8, NST), F32), pltpu.VMEM((8, NST), F32), pltpu.VMEM((tt, 2 * NST), F32)],
        compiler_params=_cp(("arbitrary", "arbitrary")),
    )(dys_p, z, ypre, w_glu, cm, lre8, lim8)


def _ssm_bwd_b(dy, u_p, st, fcr, fci, air, aii, bm, cm, dvec, lre8, lim8, S, tt):
    n = u_p.shape[0]
    nb, nt = n // S, S // tt
    ng = tt // 8

    def body(dy_ref, u_ref, st_ref, stp_ref, fcr_ref, fci_ref, air_ref, aii_ref, bm_ref, cm_ref, d_ref, lre_ref, lim_ref,
             du_ref, dcm_ref, dbm_ref, dd_ref, dlr_ref, dli_ref, are, aim, accr, acci, sp, ab):
        b = pl.program_id(0)
        j = pl.program_id(1)
        jt = nt - 1 - j

        @pl.when((b == 0) & (j == 0))
        def _():
            dcm_ref[...] = jnp.zeros_like(dcm_ref)
            dbm_ref[...] = jnp.zeros_like(dbm_ref)
            dd_ref[...] = jnp.zeros_like(dd_ref)
            accr[...] = jnp.zeros_like(accr)
            acci[...] = jnp.zeros_like(acci)

        @pl.when(j == 0)
        def _():
            are[...] = air_ref[...]
            aim[...] = aii_ref[...]

        sp[8:tt + 8, :] = st_ref[...]

        @pl.when(jt == 0)
        def _():
            sp[0:8, 0:NST] = fcr_ref[...]
            sp[0:8, NST:2 * NST] = fci_ref[...]

        @pl.when(jt != 0)
        def _():
            sp[0:8, :] = stp_ref[...]

        dy = dy_ref[...]
        u = u_ref[...]
        dyb = dy.astype(BF16)
        _bd_expand_t(dyb, cm_ref, ab)
        lre, lim = lre_ref[...], lim_ref[...]

        def step(i, c):
            ar, ai = c
            off = pl.multiple_of((ng - 1 - i) * 8, 8)
            nr = lre * ar + lim * ai + ab[pl.ds(off, 8), 0:NST]
            ni = lre * ai - lim * ar + ab[pl.ds(off, 8), NST:2 * NST]
            ab[pl.ds(off, 8), 0:NST] = nr
            ab[pl.ds(off, 8), NST:2 * NST] = ni
            pr = sp[pl.ds(off, 8), 0:NST]
            pi = sp[pl.ds(off, 8), NST:2 * NST]
            accr[...] += nr * pr + ni * pi
            acci[...] += ni * pr - nr * pi
            return nr, ni

        ar, ai = lax.fori_loop(0, ng, step, (are[...], aim[...]))
        are[...] = ar
        aim[...] = ai
        a_b = ab[...].astype(BF16)
        du_ref[...] = _bd_project_t(a_b, bm_ref) + d_ref[...] * dy
        ub = u.astype(BF16)
        for q in range(4):
            for part in range(2):
                lo = part * NST + q * 4 * QB
                s_q = sp[8:tt + 8, lo:lo + 4 * QB].astype(BF16)
                dcm_ref[lo:lo + 4 * QB, :] += _dot_tn(s_q, dyb[:, q * QB:(q + 1) * QB])
                dbm_ref[:, lo:lo + 4 * QB] += _dot_tn(ub[:, q * QB:(q + 1) * QB], a_b[:, lo:lo + 4 * QB])
        dd_ref[...] += _colsum(dy * u)

        @pl.when((b == nb - 1) & (j == nt - 1))
        def _():
            dlr_ref[...] = _colsum(accr[...])
            dli_ref[...] = _colsum(acci[...])

    row = lambda w: pl.BlockSpec((tt, w), lambda b, j: (b * nt + nt - 1 - j, 0))
    seq8 = pl.BlockSpec((8, NST), lambda b, j: (b, 0))
    prev = pl.BlockSpec((8, 2 * NST), lambda b, j: (jnp.maximum((b * nt + nt - 1 - j) * ng - 1, 0), 0))
    const = lambda shape: pl.BlockSpec(shape, lambda b, j: (0, 0))
    return pl.pallas_call(
        body, name="ssm_bwd_b", grid=(nb, nt),
        out_shape=[jax.ShapeDtypeStruct((n, D_SSM), F32), jax.ShapeDtypeStruct((2 * NST, QB), F32),
                   jax.ShapeDtypeStruct((QB, 2 * NST), F32), jax.ShapeDtypeStruct((1, D_SSM), F32),
                   jax.ShapeDtypeStruct((1, NST), F32), jax.ShapeDtypeStruct((1, NST), F32)],
        in_specs=[row(D_SSM), row(D_SSM), row(2 * NST), prev, seq8, seq8, seq8, seq8, _VM, _VM, _VM, _VM, _VM],
        out_specs=[row(D_SSM), const((2 * NST, QB)), const((QB, 2 * NST)), const((1, D_SSM)),
                   const((1, NST)), const((1, NST))],
        scratch_shapes=[pltpu.VMEM((8, NST), F32)] * 4 + [pltpu.VMEM((tt + 8, 2 * NST), F32),
                                                          pltpu.VMEM((tt, 2 * NST), F32)],
        compiler_params=_cp(("arbitrary", "arbitrary")),
    )(dy, u_p, st, st, fcr, fci, air, aii, bm, cm, dvec, lre8, lim8)


def _rope(v, c, s1, s2):
    return v * c + _roll(v, -16) * s1 + _roll(v, 16) * s2


def _rope_t(dv, c, s1, s2):
    return dv * c + _roll(dv * s1, 16) + _roll(dv * s2, -16)


def _mla_fwd(proj, rc, rs1, rs2, gq, gkv, w_uq, w_ukv, tm):
    n = proj.shape[0]

    def body(ql_ref, kvl_ref, kr_ref, c_ref, s1_ref, s2_ref, gq_ref, gkv_ref, wq_ref, wkv_ref,
             q_ref, k_ref, v_ref, qn_ref, kvn_ref):
        c, s1, s2 = c_ref[...], s1_ref[...], s2_ref[...]
        qhat, _ = _rms(ql_ref[...], Q_LORA)
        qn = (qhat * gq_ref[...]).astype(BF16)
        qn_ref[...] = qn
        q = _dot(qn, wq_ref[...])
        q_ref[...] = _rope(q, jnp.tile(c, (1, NH)), jnp.tile(s1, (1, NH)), jnp.tile(s2, (1, NH))).astype(BF16)
        khat, _ = _rms(kvl_ref[...], KV_LORA)
        kvn = (khat * gkv_ref[...]).astype(BF16)
        kvn_ref[...] = kvn
        kv = _dot(kvn, wkv_ref[...])
        kr = _rope(_roll(kr_ref[...], 64), c, s1, s2)
        k_ref[...] = (kv[:, 0:NH * HP] + jnp.tile(kr, (1, NH))).astype(BF16)
        v_ref[...] = kv[:, NH * HP:2 * NH * HP].astype(BF16)

    def wrapped(proj_ref, *rest):
        ql = proj_ref.at[:, D_SSM:D_SSM + Q_LORA]
        kvl = proj_ref.at[:, D_SSM + Q_LORA:D_SSM + Q_LORA + KV_LORA]
        kr = proj_ref.at[:, IN_PAD - HP:IN_PAD]
        body(ql, kvl, kr, *rest)

    row = lambda w: pl.BlockSpec((tm, w), lambda i: (i, 0))
    return pl.pallas_call(
        wrapped, name="mla_fwd", grid=(n // tm,),
        out_shape=[jax.ShapeDtypeStruct((n, NH * HP), BF16)] * 3 +
                  [jax.ShapeDtypeStruct((n, Q_LORA), BF16), jax.ShapeDtypeStruct((n, KV_LORA), BF16)],
        in_specs=[row(IN_PAD), row(HP), row(HP), row(HP), _VM, _VM, _VM, _VM],
        out_specs=[row(NH * HP)] * 3 + [row(Q_LORA), row(KV_LORA)],
        compiler_params=_cp(("parallel",)),
    )(proj, rc, rs1, rs2, gq, gkv, w_uq, w_ukv)


def _mla_bwd(dq, dk, dv, proj, rc, rs1, rs2, gq, gkv, w_uq, w_ukv, tm):
    n = proj.shape[0]

    def body(dq_ref, dk_ref, dv_ref, proj_ref, c_ref, s1_ref, s2_ref, gq_ref, gkv_ref, wq_ref, wkv_ref,
             dmla_ref, dqb_ref, dkvb_ref, acc_ref):
        i = pl.program_id(0)
        c, s1, s2 = c_ref[...], s1_ref[...], s2_ref[...]
        dqu = _rope_t(dq_ref[...], jnp.tile(c, (1, NH)), jnp.tile(s1, (1, NH)), jnp.tile(s2, (1, NH))).astype(BF16)
        dqb_ref[...] = dqu
        dqn = _dot_nt(dqu, wq_ref[...])
        qhat, rq = _rms(proj_ref[:, D_SSM:D_SSM + Q_LORA], Q_LORA)
        dql = _rms_bwd(dqn * gq_ref[...], qhat, rq, Q_LORA)
        dkf = dk_ref[...]
        dkv = jnp.concatenate([dkf, dv_ref[...]], axis=1).astype(BF16)
        dkvb_ref[...] = dkv
        dkvn = _dot_nt(dkv, wkv_ref[...])
        khat, rk = _rms(proj_ref[:, D_SSM + Q_LORA:D_SSM + Q_LORA + KV_LORA], KV_LORA)
        dkvl = _rms_bwd(dkvn * gkv_ref[...], khat, rk, KV_LORA)
        dkr = dkf[:, 0:HP]
        for h in range(1, NH):
            dkr = dkr + dkf[:, h * HP:(h + 1) * HP]
        lane = lax.broadcasted_iota(jnp.int32, dkr.shape, 1)
        dkr = jnp.where((lane >= QK_NOPE) & (lane < QK_NOPE + QK_ROPE), dkr, 0.0)
        dkr = _roll(_rope_t(dkr, c, s1, s2), -64)
        dmla_ref[...] = jnp.concatenate([dql, dkvl, dkr], axis=1)

        @pl.when(i == 0)
        def _():
            acc_ref[...] = jnp.zeros_like(acc_ref)

        acc_ref[0:1, 0:Q_LORA] += _colsum(dqn * qhat)
        acc_ref[1:2, 0:KV_LORA] += _colsum(dkvn * khat)

    row = lambda w: pl.BlockSpec((tm, w), lambda i: (i, 0))
    return pl.pallas_call(
        body, name="mla_bwd", grid=(n // tm,),
        out_shape=[jax.ShapeDtypeStruct((n, IN_PAD - D_SSM), F32), jax.ShapeDtypeStruct((n, NH * HP), BF16),
                   jax.ShapeDtypeStruct((n, 2 * NH * HP), BF16), jax.ShapeDtypeStruct((8, Q_LORA), F32)],
        in_specs=[row(NH * HP)] * 3 + [row(IN_PAD), row(HP), row(HP), row(HP), _VM, _VM, _VM, _VM],
        out_specs=[row(IN_PAD - D_SSM), row(NH * HP), row(2 * NH * HP), pl.BlockSpec((8, Q_LORA), lambda i: (0, 0))],
        compiler_params=_cp(("arbitrary",)),
    )(dq, dk, dv, proj, rc, rs1, rs2, gq, gkv, w_uq, w_ukv)


_SCALE = (QK_NOPE + QK_ROPE) ** -0.5
_LOG2E = 1.4426950408889634
_C2 = _SCALE * _LOG2E


def _attn_fwd(q, k, v, S, tq):
    n = q.shape[0]
    nb, nq = n // S, S // tq

    def body(q_ref, k_ref, v_ref, o_ref, lr_ref):
        qi = pl.program_id(2)
        qv = q_ref[...]

        def tile(j, c, diagonal):
            m, l, acc = c
            off = pl.multiple_of(j * tq, tq)
            s = _dot_nt(qv, k_ref[pl.ds(off, tq), :]) * _C2
            if diagonal:
                rows = lax.broadcasted_iota(jnp.int32, s.shape, 0)
                cols = lax.broadcasted_iota(jnp.int32, s.shape, 1)
                s = jnp.where(cols <= rows, s, NEG)
            mn = jnp.maximum(m, jnp.max(s, axis=1, keepdims=True))
            p = jnp.exp2(s - mn)
            al = jnp.exp2(m - mn)
            l = al * l + jnp.sum(p, axis=1, keepdims=True)
            acc = al * acc + _dot(p.astype(BF16), v_ref[pl.ds(off, tq), :])
            return mn, l, acc

        init = (jnp.full((tq, 1), NEG, F32), jnp.zeros((tq, 1), F32), jnp.zeros((tq, HP), F32))
        c = lax.fori_loop(0, qi, lambda j, c: tile(j, c, False), init)
        m, l, acc = tile(qi, c, True)
        o_ref[...] = acc / l
        lane = lax.broadcasted_iota(jnp.int32, (8, HP), 1)
        lse = jnp.broadcast_to(m + jnp.log(l) * _LOG2E, (tq, HP))
        lr_ref[...] = _rows_of(lse, jnp.where(lane == 0, 1.0, 0.0).astype(BF16))

    qs = pl.BlockSpec((tq, HP), lambda b, h, i: (b * nq + i, h))
    ks = pl.BlockSpec((S, HP), lambda b, h, i: (b, h))
    return pl.pallas_call(
        body, name="attn_fwd", grid=(nb, NH, nq),
        out_shape=[jax.ShapeDtypeStruct((n, NH * HP), F32), jax.ShapeDtypeStruct((nb * NH * 8, S), F32)],
        in_specs=[qs, ks, ks], out_specs=[qs, pl.BlockSpec((8, tq), lambda b, h, i: (b * NH + h, i))],
        compiler_params=_cp(("parallel", "parallel", "arbitrary")),
    )(q, k, v)


def _rows_of(x, pick):
    x1 = x.astype(BF16)
    r1 = x - x1.astype(F32)
    x2 = r1.astype(BF16)
    x3 = (r1 - x2.astype(F32)).astype(BF16)
    return _dot_nt(pick, x1) + _dot_nt(pick, x2) + _dot_nt(pick, x3)


def _attn_bwd(q, k, v, dob, lrow, drow, S, tq):
    n = q.shape[0]
    nb, nq = n // S, S // tq

    def body(q_ref, k_ref, v_ref, do_ref, lr_ref, dr_ref, dq_ref, dk_ref, dv_ref):
        kj = pl.program_id(2)

        @pl.when(kj == 0)
        def _():
            dq_ref[...] = jnp.zeros_like(dq_ref)

        kt = k_ref[...]
        vt = v_ref[...]

        def tile(i, c, diagonal):
            dk, dv = c
            off = pl.multiple_of(i * tq, tq)
            qv = q_ref[pl.ds(off, tq), :]
            dob = do_ref[pl.ds(off, tq), :]
            lr = lr_ref[0:1, pl.ds(off, tq)]
            dr = dr_ref[0:1, pl.ds(off, tq)]
            st = _dot_nt(kt, qv)
            dpt = _dot_nt(vt, dob)
            pt = jnp.exp2(st * _C2 - lr)
            if diagonal:
                keys = lax.broadcasted_iota(jnp.int32, pt.shape, 0)
                qrys = lax.broadcasted_iota(jnp.int32, pt.shape, 1)
                pt = jnp.where(keys <= qrys, pt, 0.0)
            dst = (pt * (dpt * _SCALE - dr)).astype(BF16)
            dq_ref[pl.ds(off, tq), :] += _dot_tn(dst, kt)
            return dk + _dot(dst, qv), dv + _dot(pt.astype(BF16), dob)

        zero = jnp.zeros((tq, HP), F32)
        c = tile(kj, (zero, zero), True)
        dk, dv = lax.fori_loop(kj + 1, nq, lambda i, c: tile(i, c, False), c)
        dk_ref[...] = dk
        dv_ref[...] = dv

    ts = pl.BlockSpec((tq, HP), lambda b, h, i: (b * nq + i, h))
    fs = pl.BlockSpec((S, HP), lambda b, h, i: (b, h))
    rs = pl.BlockSpec((8, S), lambda b, h, i: (b * NH + h, 0))
    return pl.pallas_call(
        body, name="attn_bwd", grid=(nb, NH, nq),
        out_shape=[jax.ShapeDtypeStruct((n, NH * HP), F32)] * 3,
        in_specs=[fs, ts, ts, fs, rs, rs], out_specs=[fs, ts, ts],
        compiler_params=_cp(("parallel", "parallel", "arbitrary")),
    )(q, k, v, dob, lrow, drow)


def _p1_fwd(yssm, oattn, x, modp, gs, ga, w_out, g2, S, tm):
    n = x.shape[0]
    tps = S // tm

    def body(ys_ref, oa_ref, x_ref, mod_ref, gs_ref, ga_ref, w_ref, g2_ref, yn_ref, o_ref, x1_ref, h2_ref):
        yh, _ = _rms(ys_ref[...], D_SSM)
        ah, _ = _rms(oa_ref[...], D_ATTN)
        yn = jnp.concatenate([yh * gs_ref[...], ah * ga_ref[...]], axis=1).astype(BF16)
        yn_ref[...] = yn
        o = _dot(yn, w_ref[...])
        o_ref[...] = o
        x1 = x_ref[...] + mod_ref[0, 2:3, :] * o
        x1_ref[...] = x1
        xh, _ = _rms(x1, D)
        h2_ref[...] = ((xh * g2_ref[...]) * (1.0 + mod_ref[0, 4:5, :]) + mod_ref[0, 3:4, :]).astype(BF16)

    row = lambda w: pl.BlockSpec((tm, w), lambda i: (i, 0))
    return pl.pallas_call(
        body, name="p1_fwd", grid=(n // tm,),
        out_shape=[jax.ShapeDtypeStruct((n, D_SSM + NH * HP), BF16), jax.ShapeDtypeStruct((n, D), F32),
                   jax.ShapeDtypeStruct((n, D), F32), jax.ShapeDtypeStruct((n, D), BF16)],
        in_specs=[row(D_SSM), row(NH * HP), row(D), pl.BlockSpec((1, 8, D), lambda i: (i // tps, 0, 0)),
                  _VM, _VM, _VM, _VM],
        out_specs=[row(D_SSM + NH * HP), row(D), row(D), row(D)],
        compiler_params=_cp(("parallel",)),
    )(yssm, oattn, x, modp, gs, ga, w_out, g2)


def _p2(x1, h2, target, modp, g2, gf, w_ff1, w_ff2, S, tm):
    n = x1.shape[0]
    tps = S // tm
    nb = n // S

    def body(x1_ref, h2_ref, t_ref, mod_ref, g2_ref, gf_ref, w1_ref, w2_ref,
             dx1_ref, r_ref, da_ref, dff_ref, accs_ref, accg_ref):
        i = pl.program_id(0)
        sh2, sc2, gt2 = mod_ref[0, 3:4, :], mod_ref[0, 4:5, :], mod_ref[0, 5:6, :]
        fsh, fsc = mod_ref[0, 6:7, :], mod_ref[0, 7:8, :]
        x1 = x1_ref[...]
        a = _dot(h2_ref[...], w1_ref[...])
        ra = jnp.maximum(a, 0.0)
        rb = (ra * ra).astype(BF16)
        r_ref[...] = rb
        ff = _dot(rb, w2_ref[...])
        x2 = x1 + gt2 * ff
        x2h, rf = _rms(x2, D)
        gf_v = gf_ref[...]
        outn = x2h * gf_v
        err = outn * (1.0 + fsc) + fsh - t_ref[...]
        dout = err * (1.0 / D)
        doutn = dout * (1.0 + fsc)
        dx2 = _rms_bwd(doutn * gf_v, x2h, rf, D)
        dff = (gt2 * dx2).astype(BF16)
        dff_ref[...] = dff
        dr = _dot_nt(dff, w2_ref[...])
        da = (dr * (2.0 * ra)).astype(BF16)
        da_ref[...] = da
        dh2 = _dot_nt(da, w1_ref[...])
        x1h, r2 = _rms(x1, D)
        g2_v = g2_ref[...]
        dn2 = dh2 * (1.0 + sc2)
        dx1_ref[...] = dx2 + _rms_bwd(dn2 * g2_v, x1h, r2, D)

        @pl.when(i % tps == 0)
        def _():
            accs_ref[...] = jnp.zeros_like(accs_ref)

        @pl.when(i == 0)
        def _():
            accg_ref[...] = jnp.zeros_like(accg_ref)

        accs_ref[0, 3:4, :] += _colsum(dh2)
        accs_ref[0, 4:5, :] += _colsum(dh2 * (x1h * g2_v))
        accs_ref[0, 5:6, :] += _colsum(dx2 * ff)
        accs_ref[0, 6:7, :] += _colsum(dout)
        accs_ref[0, 7:8, :] += _colsum(dout * outn)
        accg_ref[0:1, :] += _colsum(dn2 * x1h)
        accg_ref[1:2, :] += _colsum(doutn * x2h)
        accg_ref[2:3, :] += _colsum(err * err) * (0.5 / D)

    row = lambda w: pl.BlockSpec((tm, w), lambda i: (i, 0))
    return pl.pallas_call(
        body, name="p2_mlp_loss", grid=(n // tm,),
        out_shape=[jax.ShapeDtypeStruct((n, D), F32), jax.ShapeDtypeStruct((n, D_FF), BF16),
                   jax.ShapeDtypeStruct((n, D_FF), BF16), jax.ShapeDtypeStruct((n, D), BF16),
                   jax.ShapeDtypeStruct((nb, 8, D), F32), jax.ShapeDtypeStruct((8, D), F32)],
        in_specs=[row(D), row(D), row(D), pl.BlockSpec((1, 8, D), lambda i: (i // tps, 0, 0)), _VM, _VM, _VM, _VM],
        out_specs=[row(D), row(D_FF), row(D_FF), row(D), pl.BlockSpec((1, 8, D), lambda i: (i // tps, 0, 0)),
                   pl.BlockSpec((8, D), lambda i: (0, 0))],
        compiler_params=_cp(("arbitrary",)),
    )(x1, h2, target, modp, g2, gf, w_ff1, w_ff2)


def _p3_bwd(dx1, o, yssm, oattn, modp, gs, ga, w_out, S, tm):
    n = dx1.shape[0]
    tps = S // tm
    nb = n // S

    def body(dx1_ref, o_ref, ys_ref, oa_ref, mod_ref, gs_ref, ga_ref, w_ref,
             do_ref, dys_ref, doa_ref, dr_ref, accs_ref, accg_ref):
        i = pl.program_id(0)
        dx1 = dx1_ref[...]
        dob = (mod_ref[0, 2:3, :] * dx1).astype(BF16)
        do_ref[...] = dob
        dyn = _dot_nt(dob, w_ref[...])
        yh, rs = _rms(ys_ref[...], D_SSM)
        oa = oa_ref[...]
        ah, ra = _rms(oa, D_ATTN)
        d1 = dyn[:, 0:D_SSM]
        d2 = dyn[:, D_SSM:D_SSM + NH * HP]
        dys_ref[...] = _rms_bwd(d1 * gs_ref[...], yh, rs, D_SSM)
        doa = _rms_bwd(d2 * ga_ref[...], ah, ra, D_ATTN)
        doa_ref[...] = doa.astype(BF16)
        prod = doa * oa * _SCALE
        ones = jnp.ones((8, HP), BF16)
        for h in range(NH):
            dr_ref[h * 8:(h + 1) * 8, :] = _rows_of(prod[:, h * HP:(h + 1) * HP], ones)

        @pl.when(i % tps == 0)
        def _():
            accs_ref[...] = jnp.zeros_like(accs_ref)

        @pl.when(i == 0)
        def _():
            accg_ref[...] = jnp.zeros_like(accg_ref)

        accs_ref[0, 2:3, :] += _colsum(dx1 * o_ref[...])
        accg_ref[0:1, 0:D_SSM] += _colsum(d1 * yh)
        accg_ref[1:2, :] += _colsum(d2 * ah)

    row = lambda w: pl.BlockSpec((tm, w), lambda i: (i, 0))
    return pl.pallas_call(
        body, name="p3_bwd", grid=(n // tm,),
        out_shape=[jax.ShapeDtypeStruct((n, D), BF16), jax.ShapeDtypeStruct((n, D_SSM), F32),
                   jax.ShapeDtypeStruct((n, NH * HP), BF16), jax.ShapeDtypeStruct((nb * NH * 8, S), F32),
                   jax.ShapeDtypeStruct((nb, 8, D), F32), jax.ShapeDtypeStruct((8, NH * HP), F32)],
        in_specs=[row(D), row(D), row(D_SSM), row(NH * HP), pl.BlockSpec((1, 8, D), lambda i: (i // tps, 0, 0)),
                  _VM, _VM, _VM],
        out_specs=[row(D), row(D_SSM), row(NH * HP), pl.BlockSpec((NH * 8, tm), lambda i: (i // tps, i % tps)),
                   pl.BlockSpec((1, 8, D), lambda i: (i // tps, 0, 0)), pl.BlockSpec((8, NH * HP), lambda i: (0, 0))],
        compiler_params=_cp(("arbitrary",)),
    )(dx1, o, yssm, oattn, modp, gs, ga, w_out)


def _wgrad(a, b, name, col_slots=0):
    n, k1 = a.shape
    k2 = b.shape[1]
    bn = next((b for b in (1024, 512) if n % b == 0), n)
    bk1 = next((b for b in (1024, 512) if k1 % b == 0), k1)
    bk2 = k2 // col_slots if col_slots else (1024 if (k2 % 1024 == 0) else k2)

    def body(a_ref, b_ref, o_ref):
        @pl.when(pl.program_id(2) == 0)
        def _():
            o_ref[...] = jnp.zeros_like(o_ref)

        o_ref[...] += _dot_tn(a_ref[...], b_ref[...]).reshape(o_ref.shape)

    if col_slots:
        out_shape = jax.ShapeDtypeStruct((col_slots, k1, bk2), F32)
        out_spec = pl.BlockSpec((1, bk1, bk2), lambda i, j, t: (j, i, 0))
    else:
        out_shape = jax.ShapeDtypeStruct((k1, k2), F32)
        out_spec = pl.BlockSpec((bk1, bk2), lambda i, j, t: (i, j))
    return pl.pallas_call(
        body, name=name, grid=(k1 // bk1, k2 // bk2, n // bn),
        out_shape=out_shape,
        in_specs=[pl.BlockSpec((bn, bk1), lambda i, j, t: (t, i)), pl.BlockSpec((bn, bk2), lambda i, j, t: (t, j))],
        out_specs=out_spec,
        compiler_params=_cp(("parallel", "parallel", "arbitrary")),
    )(a, b)


def _row_block(rows):
    if rows <= 256:
        return rows
    return next(b for b in (256, 192, 128, 64, 32, 16, 8) if rows % b == 0)


def _add_half(g, recv, cidx, name):
    _, rows2, w = g.shape
    rows = rows2 // 2
    br = _row_block(rows)
    nblk = rows // br

    def body(c_ref, g_ref, r_ref, o_ref):
        o_ref[...] = (g_ref[...] + r_ref[...]).astype(BF16)

    return pl.pallas_call(
        body, name=name,
        grid_spec=pltpu.PrefetchScalarGridSpec(
            num_scalar_prefetch=1, grid=(4, nblk),
            in_specs=[pl.BlockSpec((1, br, w), lambda s, i, c: (s, c[0] * nblk + i, 0)),
                      pl.BlockSpec((1, br, w), lambda s, i, c: (s, i, 0))],
            out_specs=pl.BlockSpec((1, br, w), lambda s, i, c: (s, i, 0))),
        out_shape=jax.ShapeDtypeStruct((4, rows, w), BF16),
        compiler_params=_cp(("parallel", "parallel")),
    )(cidx, g, recv)


def _add_chips(r, name):
    _, rows, w = r.shape
    br = _row_block(rows)

    def body(r_ref, o_ref):
        f = lambda k: r_ref[k].astype(F32)
        o_ref[...] = ((f(0) + f(1)) + f(2)) + f(3)

    return pl.pallas_call(
        body, name=name, grid=(rows // br,),
        out_shape=jax.ShapeDtypeStruct((rows, w), F32),
        in_specs=[pl.BlockSpec((4, br, w), lambda i: (0, i, 0))],
        out_specs=pl.BlockSpec((br, w), lambda i: (i, 0)),
        compiler_params=_cp(("parallel",)),
    )(r)


def _sum_devices(a, b):
    def body(a_ref, b_ref, oa_ref, ob_ref):
        acc = a_ref[0:1, :].astype(F32)
        accb = b_ref[0:1, :]
        for k in range(1, 8):
            acc = acc + a_ref[k:k + 1, :].astype(F32)
            accb = accb + b_ref[k:k + 1, :]
        oa_ref[...] = acc
        ob_ref[...] = accb

    return pl.pallas_call(
        body, name="small_grad_sum",
        out_shape=[jax.ShapeDtypeStruct((1, a.shape[1]), F32), jax.ShapeDtypeStruct((1, b.shape[1]), F32)],
        in_specs=[_VM, _VM], out_specs=[_VM, _VM], compiler_params=_cp(),
    )(a, b)


def _adamw_math(wv, gv, mv, vv):
    m_new = ADAM_B1 * mv + (1.0 - ADAM_B1) * gv
    v_new = ADAM_B2 * vv + (1.0 - ADAM_B2) * (gv * gv)
    m_hat = m_new / (1.0 - ADAM_B1 ** ADAM_STEP)
    v_hat = v_new / (1.0 - ADAM_B2 ** ADAM_STEP)
    return -ADAM_LR * (m_hat / (jnp.sqrt(v_hat) + ADAM_EPS) + ADAM_WD * wv), m_new, v_new


def _adamw_small(ws, gs, ms, vs):
    k = len(ws)

    def body(*refs):
        ins, outs = refs[:4 * k], refs[4 * k:]
        for t in range(k):
            d, m_new, v_new = _adamw_math(ins[t][...], ins[k + t][...], ins[2 * k + t][...], ins[3 * k + t][...])
            outs[t][...] = d
            outs[k + t][...] = m_new
            outs[2 * k + t][...] = v_new

    shapes = [jax.ShapeDtypeStruct(w.shape, F32) for w in ws]
    return pl.pallas_call(
        body, name="adamw_small", out_shape=shapes * 3,
        in_specs=[_VM] * (4 * k), out_specs=[_VM] * (3 * k), compiler_params=_cp(),
    )(*ws, *gs, *ms, *vs)


def _adamw(w, g, m, v, name):
    rows, wd = w.shape
    br = _row_block(rows)

    def body(w_ref, g_ref, m_ref, v_ref, d_ref, nm_ref, nv_ref):
        d, m_new, v_new = _adamw_math(w_ref[...], g_ref[...], m_ref[...], v_ref[...])
        d_ref[...] = d
        nm_ref[...] = m_new
        nv_ref[...] = v_new

    spec = pl.BlockSpec((br, wd), lambda i: (i, 0))
    return pl.pallas_call(
        body, name=name, grid=(rows // br,),
        out_shape=[jax.ShapeDtypeStruct((rows, wd), F32)] * 3,
        in_specs=[spec] * 4, out_specs=[spec] * 3,
        compiler_params=_cp(("parallel",)),
    )(w, g, m, v)


def _adamw_halves(w, mine, other, m, v, cidx, name):
    rows, wd = w.shape
    h = rows // 2
    br = _row_block(h)
    nblk = h // br

    def body(c_ref, w_ref, a_ref, b_ref, m_ref, v_ref, g_ref, d_ref, nm_ref, nv_ref):
        gv = jnp.where(pl.program_id(0) == c_ref[0], a_ref[...], b_ref[...])
        d, m_new, v_new = _adamw_math(w_ref[...], gv, m_ref[...], v_ref[...])
        g_ref[...] = gv
        d_ref[...] = d
        nm_ref[...] = m_new
        nv_ref[...] = v_new

    full = pl.BlockSpec((br, wd), lambda hf, i, c: (hf * nblk + i, 0))
    half = pl.BlockSpec((br, wd), lambda hf, i, c: (i, 0))
    return pl.pallas_call(
        body, name=name,
        grid_spec=pltpu.PrefetchScalarGridSpec(
            num_scalar_prefetch=1, grid=(2, nblk),
            in_specs=[full, half, half, full, full], out_specs=[full] * 4),
        out_shape=[jax.ShapeDtypeStruct((rows, wd), F32)] * 4,
        compiler_params=_cp(("parallel", "parallel")),
    )(cidx, w, mine, other, m, v)


def _other_chips(x, y):
    return [(1 - x, y), (x, 1 - y), (1 - x, 1 - y)]


def _other_devices(x, y, c):
    flip = lambda v, d: (1 - v) if d else v
    return [(flip(x, dx), flip(y, dy), flip(c, dc))
            for dx in (0, 1) for dy in (0, 1) for dc in (0, 1) if (dx, dy, dc) != (0, 0, 0)]


def _exchange(name, ins, out_shapes, n_local, n_remote, plan):
    ni, no = len(ins), len(out_shapes)

    def body(*refs):
        in_refs, out_refs = refs[:ni], refs[ni:ni + no]
        send_sems, recv_sems, local_sems = refs[ni + no:]
        x, y, c = lax.axis_index("x"), lax.axis_index("y"), lax.axis_index("c")
        local, remote = plan(in_refs, out_refs, x, y, c)
        assert len(local) == n_local and len(remote) == n_remote

        def push(k, src, dst, dev):
            return pltpu.make_async_remote_copy(src_ref=src, dst_ref=dst, send_sem=send_sems.at[k],
                                                recv_sem=recv_sems.at[k], device_id=dev, device_id_type=MESH)

        own = [pltpu.make_async_copy(s, d, local_sems.at[i]) for i, (s, d) in enumerate(local)]
        for cp in own:
            cp.start()
        sends = [push(k, s, d, dev) for k, (s, d, dev, _) in enumerate(remote)]
        for cp in sends:
            cp.start()
        for k, (s, _, dev, landing) in enumerate(remote):
            push(k, s, landing, dev).wait_recv()
        for cp in sends:
            cp.wait_send()
        for cp in own:
            cp.wait()

    return pl.pallas_call(
        body, name=name, out_shape=out_shapes,
        in_specs=[_ANY] * ni, out_specs=[_ANY] * no,
        scratch_shapes=[pltpu.SemaphoreType.DMA((n_remote,)), pltpu.SemaphoreType.DMA((n_remote,)),
                        pltpu.SemaphoreType.DMA((max(n_local, 1),))],
        compiler_params=pltpu.CompilerParams(has_side_effects=True),
    )(*ins)


def _gather_chips(name, shards, everyone=()):
    ns, ne = len(shards), len(everyone)
    outs = [jax.ShapeDtypeStruct((4,) + a.shape, a.dtype) for a in shards]
    outs += [jax.ShapeDtypeStruct((8,) + a.shape, a.dtype) for a in everyone]

    def plan(i, o, x, y, c):
        mine, me = 2 * x + y, 4 * x + 2 * y + c
        local, remote = [], []
        for t in range(ns):
            local.append((i[t], o[t].at[mine]))
            for px, py in _other_chips(x, y):
                remote.append((i[t], o[t].at[mine], (px, py, c), o[t].at[2 * px + py]))
        for t in range(ns, ns + ne):
            local.append((i[t], o[t].at[me]))
            for px, py, pc in _other_devices(x, y, c):
                remote.append((i[t], o[t].at[me], (px, py, pc), o[t].at[4 * px + 2 * py + pc]))
        return local, remote

    return _exchange(name, list(shards) + list(everyone), outs, ns + ne, 3 * ns + 7 * ne, plan)


_HBM = pl.BlockSpec(memory_space=pltpu.HBM)
_SEM = pl.BlockSpec(memory_space=pltpu.SEMAPHORE)
_EFFECT = pltpu.SideEffectType.DATAFLOW_SIDE_EFFECTING


def _split_start(name, ins, land_shapes, n_remote, plan, after):
    ni, nl = len(ins), len(land_shapes)
    srcs = [pltpu.with_memory_space_constraint(a, pltpu.HBM) for a in ins]
    lands = [pltpu.with_memory_space_constraint(lax.empty(s.shape, s.dtype), pltpu.HBM) for s in land_shapes]

    def body(*refs):
        src, land = refs[:ni], refs[ni:ni + nl]
        first = ni + nl + 1
        send, recv = refs[first:first + n_remote], refs[first + n_remote:first + 2 * n_remote]
        token = refs[first + 2 * n_remote + ni + nl]
        x, y, c = lax.axis_index("x"), lax.axis_index("y"), lax.axis_index("c")
        remote = plan(src, land, x, y, c)
        assert len(remote) == n_remote
        for k, (s, d, dev, _) in enumerate(remote):
            pltpu.make_async_remote_copy(src_ref=s, dst_ref=d, send_sem=send[k], recv_sem=recv[k],
                                         device_id=dev, device_id_type=MESH).start()
        token[...] = jnp.zeros_like(token)

    out = pl.pallas_call(
        body, name=name + "_start",
        out_shape=[pltpu.SemaphoreType.DMA(())] * (2 * n_remote)
                  + [pltpu.HBM(a.shape, a.dtype) for a in ins] + [pltpu.HBM(s.shape, s.dtype) for s in land_shapes]
                  + [jax.ShapeDtypeStruct((8, 128), F32)],
        in_specs=[_HBM] * (ni + nl) + [_ANY], out_specs=[_SEM] * (2 * n_remote) + [_HBM] * (ni + nl) + [_VM],
        input_output_aliases={t: 2 * n_remote + t for t in range(ni + nl)},
        compiler_params=pltpu.CompilerParams(has_side_effects=_EFFECT),
    )(*srcs, *lands, after)
    sems, thru = out[:2 * n_remote], out[2 * n_remote:2 * n_remote + ni + nl]
    return (name, sems, thru[:ni], thru[ni:], n_remote, plan), out[-1]


def _split_wait(handle, after):
    name, sems, srcs, lands, n_remote, plan = handle
    ni, nl = len(srcs), len(lands)

    def body(*refs):
        src, land = refs[:ni], refs[ni:ni + nl]
        send, recv = refs[ni + nl:ni + nl + n_remote], refs[ni + nl + n_remote:ni + nl + 2 * n_remote]
        x, y, c = lax.axis_index("x"), lax.axis_index("y"), lax.axis_index("c")
        for k, (s, _, dev, landing) in enumerate(plan(src, land, x, y, c)):
            cp = pltpu.make_async_remote_copy(src_ref=s, dst_ref=landing, send_sem=send[k], recv_sem=recv[k],
                                              device_id=dev, device_id_type=MESH)
            cp.wait_send()
            cp.wait_recv()

    out = pl.pallas_call(
        body, name=name + "_wait",
        out_shape=[pltpu.HBM(a.shape, a.dtype) for a in srcs] + [pltpu.HBM(a.shape, a.dtype) for a in lands],
        in_specs=[_HBM] * (ni + nl) + [_SEM] * (2 * n_remote) + [_ANY], out_specs=[_HBM] * (ni + nl),
        input_output_aliases={t: t for t in range(ni + nl)},
        compiler_params=pltpu.CompilerParams(has_side_effects=_EFFECT),
    )(*srcs, *lands, *sems, after)
    return out[:ni], out[ni:]


def _plan_to_chips(src, land, x, y, c):
    mine = 2 * x + y
    return [(src[t], land[t].at[mine], (px, py, c), land[t].at[2 * px + py])
            for t in range(len(src)) for px, py in _other_chips(x, y)]


def _plan_swap_halves(src, land, x, y, c):
    out = []
    for t in range(len(src)):
        h = src[t].shape[1] // 2
        out.append((src[t].at[:, pl.ds(pl.multiple_of((1 - c) * h, 8), h), :], land[t], (x, y, 1 - c), land[t]))
    return out


def _plan_scatter_chips(src, land, x, y, c):
    mine = 2 * x + y
    return [(src[t].at[2 * px + py], land[t].at[mine], (px, py, c), land[t].at[2 * px + py])
            for t in range(len(src)) for px, py in _other_chips(x, y)]


def _swap_halves(gs, everyone):
    ns, ne = len(gs), len(everyone)
    outs = [jax.ShapeDtypeStruct((4, g.shape[1] // 2, g.shape[2]), g.dtype) for g in gs]
    outs += [jax.ShapeDtypeStruct((8,) + a.shape, a.dtype) for a in everyone]

    def plan(i, o, x, y, c):
        me = 4 * x + 2 * y + c
        local, remote = [], []
        for t in range(ns):
            h = gs[t].shape[1] // 2
            theirs = i[t].at[:, pl.ds(pl.multiple_of((1 - c) * h, 8), h), :]
            remote.append((theirs, o[t], (x, y, 1 - c), o[t]))
        for t in range(ns, ns + ne):
            local.append((i[t], o[t].at[me]))
            for px, py, pc in _other_devices(x, y, c):
                remote.append((i[t], o[t].at[me], (px, py, pc), o[t].at[4 * px + 2 * py + pc]))
        return local, remote

    return _exchange("grad_swap_sibling", list(gs) + list(everyone), outs, ne, ns + 7 * ne, plan)


def _scatter_chips(parts):
    ns = len(parts)
    outs = [jax.ShapeDtypeStruct(a.shape, a.dtype) for a in parts]

    def plan(i, o, x, y, c):
        mine = 2 * x + y
        local, remote = [], []
        for t in range(ns):
            local.append((i[t].at[mine], o[t].at[mine]))
            for px, py in _other_chips(x, y):
                remote.append((i[t].at[2 * px + py], o[t].at[mine], (px, py, c), o[t].at[2 * px + py]))
        return local, remote

    return _exchange("grad_scatter_chips", list(parts), outs, ns, 3 * ns, plan)


def _join_halves(halves):
    ns = len(halves)
    outs = [jax.ShapeDtypeStruct(a.shape, a.dtype) for a in halves]

    def plan(i, o, x, y, c):
        return [], [(i[t], o[t], (x, y, 1 - c), o[t]) for t in range(ns)]

    return _exchange("grad_join_sibling", list(halves), outs, 0, ns, plan)


def _pad_heads_cols(w, per, used):
    k = w.shape[0]
    w = w.reshape(k, NH, per)[:, :, :used]
    return jnp.pad(w, ((0, 0), (0, 0), (0, HP - used))).reshape(k, NH * HP)


def _unpad_heads_cols(w, used):
    k = w.shape[0]
    return w.reshape(k, NH, HP)[:, :, :used]


def _prep_weights(wf):
    bf = lambda a: a.astype(BF16)
    out = {}
    out["w_in"] = jnp.pad(bf(wf["w_in"]), ((0, 0), (0, IN_PAD - IN_COLS)))
    out["w_glu"] = bf(wf["w_glu"])
    out["w_uq"] = _pad_heads_cols(bf(wf["w_uq"]), QK_NOPE + QK_ROPE, QK_NOPE + QK_ROPE)
    wkv = bf(wf["w_ukv"]).reshape(KV_LORA, NH, QK_NOPE + V_HEAD)
    wk = jnp.pad(wkv[:, :, :QK_NOPE], ((0, 0), (0, 0), (0, HP - QK_NOPE))).reshape(KV_LORA, NH * HP)
    wv = jnp.pad(wkv[:, :, QK_NOPE:], ((0, 0), (0, 0), (0, HP - V_HEAD))).reshape(KV_LORA, NH * HP)
    out["w_ukv"] = jnp.concatenate([wk, wv], axis=1)
    return out


def _prep_late_weights(wf):
    bf = lambda a: a.astype(BF16)
    out = {}
    wo = bf(wf["w_out"])
    wo_a = jnp.pad(wo[D_SSM:].reshape(NH, V_HEAD, D), ((0, 0), (0, HP - V_HEAD), (0, 0))).reshape(NH * HP, D)
    out["w_out"] = jnp.concatenate([wo[:D_SSM], wo_a], axis=0)
    out["w_ff1"] = bf(wf["w_ff1"])
    out["w_ff2"] = bf(wf["w_ff2"])
    return out


def _rope_tables(positions):
    inv_freq = ROPE_BASE ** (-jnp.arange(0, QK_ROPE, 2, dtype=F32) / QK_ROPE)
    ang = positions.astype(F32)[:, None] * inv_freq
    cos, sin = jnp.cos(ang), jnp.sin(ang)
    n = positions.shape[0]
    one = jnp.ones((n, QK_NOPE), F32)
    z16 = jnp.zeros((n, 16), F32)
    z32 = jnp.zeros((n, 32), F32)
    z64 = jnp.zeros((n, QK_NOPE), F32)
    rc = jnp.concatenate([one, cos, cos, z32], axis=1)
    rs1 = jnp.concatenate([z64, -sin, z16, z32], axis=1)
    rs2 = jnp.concatenate([z64, z16, sin, z32], axis=1)
    return rc, rs1, rs2


def _permute_rows(a, S):
    n, w = a.shape
    return a.reshape(n // S, 8, S // 8, w).transpose(0, 2, 1, 3).reshape(n, w)


def _unpermute_rows(a, S):
    n, w = a.shape
    return a.reshape(n // S, S // 8, 8, w).transpose(0, 2, 1, 3).reshape(n, w)


def _block_diag_in(bb):
    eye = jnp.eye(8, dtype=bb.dtype)
    blocks = jnp.einsum("qgph,gk->qghkp", bb.reshape(4, 8, P, H), eye).reshape(4, QB, QS)
    return blocks.transpose(1, 0, 2).reshape(QB, NST)


def _block_diag_out(cc):
    eye = jnp.eye(8, dtype=cc.dtype)
    return jnp.einsum("qghp,gk->qgpkh", cc.reshape(4, 8, H, P), eye).reshape(NST, QB)


def _slots(full):
    r, cdim = full.shape
    return full.reshape(r, 4, cdim // 4).transpose(1, 0, 2)


def _unslots(g):
    s, r, cs = g.shape
    return g.transpose(1, 0, 2).reshape(r, s * cs)


def _local_step(x, positions, target, modp, wf, late_weights=None, reducer=None):
    nb, S, _ = x.shape
    n = nb * S
    tm = min(256, S)
    tt = min(256, S)
    tq = min(512, S // 2)
    kw = _prep_weights(wf)
    row = lambda a: a.reshape(1, -1).astype(F32)

    xf = x.reshape(n, D)
    tf = target.reshape(n, D)
    g1, g2, gf = row(wf["norm1_g"]), row(wf["norm2_g"]), row(wf["final_norm_g"])
    h1, proj = _f1_fwd(xf, modp, g1, kw["w_in"], S, tm)

    col = lambda a: a.reshape(NST, 1)
    lam_re, lam_im = col(wf["ssm_lambda_re"]), col(wf["ssm_lambda_im"])
    logdt = jnp.repeat(wf["ssm_log_dt"].reshape(G, 1), P, axis=1).reshape(NST, 1)
    b_re, b_im = wf["ssm_b_re"].reshape(NST, H), wf["ssm_b_im"].reshape(NST, H)
    lbr, lbi, bbr, bbi = _ssm_param_fwd(lam_re, lam_im, logdt, b_re, b_im)
    lre8 = jnp.broadcast_to(lbr.reshape(1, NST), (8, NST))
    lim8 = jnp.broadcast_to(lbi.reshape(1, NST), (8, NST))
    bm = jnp.concatenate([_block_diag_in(bbr.reshape(G, P, H)), _block_diag_in(bbi.reshape(G, P, H))],
                         axis=1).astype(BF16)
    cm = jnp.concatenate([_block_diag_out(wf["ssm_c_re"]), -_block_diag_out(wf["ssm_c_im"])], axis=0).astype(BF16)
    dvec = row(wf["ssm_d"])
    u_p = _permute_rows(proj[:, :D_SSM], S)
    fcr, fci = _ssm_local(u_p, bm, lre8, lim8, S, tt)
    st, ypre, z, gact, yssm_p = _ssm_fwd(u_p, fcr, fci, bm, cm, dvec, kw["w_glu"], lre8, lim8, S, tt)
    yssm = _unpermute_rows(yssm_p, S)

    rc, rs1, rs2 = _rope_tables(positions.reshape(n))
    gq, gkv = row(wf["q_norm_g"]), row(wf["kv_norm_g"])
    q, k, v, qn, kvn = _mla_fwd(proj, rc, rs1, rs2, gq, gkv, kw["w_uq"], kw["w_ukv"], tm)
    oattn, lrow = _attn_fwd(q, k, v, S, tq)

    gs = row(wf["ssm_out_g"])
    ga = jnp.pad(wf["attn_out_g"].reshape(NH, V_HEAD), ((0, 0), (0, HP - V_HEAD))).reshape(1, NH * HP)
    kw.update(_prep_late_weights(late_weights(oattn) if late_weights is not None else wf))
    yn, o, x1, h2 = _p1_fwd(yssm, oattn, xf, modp, gs, ga, kw["w_out"], g2, S, tm)
    dx1, r, da, dff, accs2, accg2 = _p2(x1, h2, tf, modp, g2, gf, kw["w_ff1"], kw["w_ff2"], S, tm)
    loss = accg2[2:3]
    g_ff1 = _wgrad(h2, da, "wgrad_ff1", col_slots=4)
    g_ff2 = _wgrad(r, dff, "wgrad_ff2").reshape(4, D_FF // 4, D)
    gs_b, gq_b = gs, gq
    if reducer is not None:
        gs_b = gs + reducer.start([g_ff1, g_ff2])[0, 0]
    do, dyssm, dob, drow, accs3, accg3 = _p3_bwd(dx1, o, yssm, oattn, modp, gs_b, ga, kw["w_out"], S, tm)

    dq, dk, dv = _attn_bwd(q, k, v, dob, lrow, drow, S, tq)
    if reducer is not None:
        gq_b = gq + reducer.middle(dq)[0, 0]
    dmla, dqb, dkvb, accm = _mla_bwd(dq, dk, dv, proj, rc, rs1, rs2, gq_b, gkv, kw["w_uq"], kw["w_ukv"], tm)

    dys_p = _permute_rows(dyssm, S)
    dy, dz, air, aii = _ssm_bwd_a(dys_p, z, ypre, kw["w_glu"], cm, lre8, lim8, S, tt)
    du_p, dcm, dbm, dd, dlr, dli = _ssm_bwd_b(dy, u_p, st, fcr, fci, air, aii, bm, cm, dvec, lre8, lim8, S, tt)
    du = _unpermute_rows(du_p, S)
    dcm = dcm.reshape(2, 4, 8, P, 8, H)
    dc_re = jnp.einsum("qgpgh->qghp", dcm[0]).reshape(G, H, P)
    dc_im = -jnp.einsum("qgpgh->qghp", dcm[1]).reshape(G, H, P)
    dbm = dbm.reshape(8, H, 2, 4, 8, P)
    dbb_re = jnp.einsum("ghqgp->qgph", dbm[:, :, 0]).reshape(NST, H)
    dbb_im = jnp.einsum("ghqgp->qgph", dbm[:, :, 1]).reshape(NST, H)
    gb_re, gb_im, glr, gli, gdt = _ssm_param_bwd(lam_re, lam_im, logdt, b_re, b_im, dlr.reshape(NST, 1),
                                                 dli.reshape(NST, 1), dbb_re, dbb_im)
    glogdt = _rowsum(gdt.reshape(G, P))

    dx, dproj, accs1, accg1 = _f1_bwd(du, dmla, dx1, xf, modp, g1, kw["w_in"], S, tm)

    big = {}
    big["w_in"] = _slots(_wgrad(h1, dproj, "wgrad_in")[:, :IN_COLS])
    big["w_glu"] = _wgrad(gact, dz, "wgrad_glu", col_slots=4)
    big["w_uq"] = _slots(_unpad_heads_cols(_wgrad(qn, dqb, "wgrad_uq"), QK_NOPE + QK_ROPE).reshape(Q_LORA, -1))
    gkvw = _wgrad(kvn, dkvb, "wgrad_ukv")
    big["w_ukv"] = _slots(jnp.concatenate([_unpad_heads_cols(gkvw[:, :NH * HP], QK_NOPE),
                                           _unpad_heads_cols(gkvw[:, NH * HP:], V_HEAD)], axis=2).reshape(KV_LORA, -1))
    gwo = _wgrad(yn, do, "wgrad_out")
    big["w_out"] = jnp.concatenate([gwo[:D_SSM].reshape(2, D_SSM // 2, D),
                                    gwo[D_SSM:].reshape(2, NH // 2 * HP, D).reshape(2, NH // 2, HP, D)[:, :, :V_HEAD]
                                    .reshape(2, D_ATTN // 2, D)], axis=0)
    big["w_ff1"] = g_ff1
    big["w_ff2"] = g_ff2

    small = {}
    small["norm1_g"] = accg1[0:1]
    small["norm2_g"] = accg2[0:1]
    small["final_norm_g"] = accg2[1:2]
    small["ssm_out_g"] = accg3[0:1, :D_SSM]
    small["attn_out_g"] = accg3[1].reshape(NH, HP)[:, :V_HEAD].reshape(1, D_ATTN)
    small["q_norm_g"] = accm[0:1, :Q_LORA]
    small["kv_norm_g"] = accm[1:2, :KV_LORA]
    small["ssm_lambda_re"] = glr.reshape(G, P)
    small["ssm_lambda_im"] = gli.reshape(G, P)
    small["ssm_b_re"] = gb_re
    small["ssm_b_im"] = gb_im
    small["ssm_c_re"] = dc_re.reshape(G * H, P)
    small["ssm_c_im"] = dc_im.reshape(G * H, P)
    small["ssm_d"] = dd.reshape(G, H)
    small["ssm_log_dt"] = glogdt.reshape(1, G)
    return loss, dx.reshape(nb, S, D), big, small, accs1 + accs2 + accs3


def _view2d(a):
    return a.reshape(-1, a.shape[-1]) if a.ndim > 1 else a.reshape(1, -1)


def kernel(x, c, positions, ada_w, ada_b, norm1_g, w_in, ssm_lambda_re, ssm_lambda_im, ssm_b_re, ssm_b_im, ssm_c_re, ssm_c_im, ssm_d, ssm_log_dt, w_glu, q_norm_g, w_uq, kv_norm_g, w_ukv, ssm_out_g, attn_out_g, w_out, norm2_g, w_ff1, w_ff2, final_ada_w, final_ada_b, final_norm_g, loss_target, m_ada_w, m_ada_b, m_norm1_g, m_w_in, m_ssm_lambda_re, m_ssm_lambda_im, m_ssm_b_re, m_ssm_b_im, m_ssm_c_re, m_ssm_c_im, m_ssm_d, m_ssm_log_dt, m_w_glu, m_q_norm_g, m_w_uq, m_kv_norm_g, m_w_ukv, m_ssm_out_g, m_attn_out_g, m_w_out, m_norm2_g, m_w_ff1, m_w_ff2, m_final_ada_w, m_final_ada_b, m_final_norm_g, v_ada_w, v_ada_b, v_norm1_g, v_w_in, v_ssm_lambda_re, v_ssm_lambda_im, v_ssm_b_re, v_ssm_b_im, v_ssm_c_re, v_ssm_c_im, v_ssm_d, v_ssm_log_dt, v_w_glu, v_q_norm_g, v_w_uq, v_kv_norm_g, v_w_ukv, v_ssm_out_g, v_attn_out_g, v_w_out, v_norm2_g, v_w_ff1, v_w_ff2, v_final_ada_w, v_final_ada_b, v_final_norm_g):
    args = dict(locals())
    names = list(inspect.signature(kernel).parameters)
    wnames = names[3:names.index("loss_target")]
    small_names = [nm for nm in wnames if nm not in GATHERED and nm not in TP]
    reduced_names = [nm for nm in small_names if nm not in ("ada_b", "final_ada_b")]
    w = {nm: args[nm] for nm in wnames}
    m = {nm: args["m_" + nm] for nm in wnames}
    v = {nm: args["v_" + nm] for nm in wnames}
    nb = x.shape[0]
    xi, yi, ci = lax.axis_index("x"), lax.axis_index("y"), lax.axis_index("c")
    chip, me = 2 * xi + yi, 4 * xi + 2 * yi + ci

    unslot = lambda nm, g: g.reshape(-1, g.shape[-1]) if nm in ROW_SHARDED else _unslots(g)
    early = [nm for nm in GATHERED if nm not in LATE]
    got = _gather_chips("gather_weights", [_view2d(w[nm]).astype(BF16) for nm in early], [c])
    wf = {nm: unslot(nm, g) for nm, g in zip(early, got)}
    for nm in small_names:
        wf[nm] = w[nm][0] if w[nm].ndim > 1 else w[nm]
    c_all = got[len(early)].reshape(8 * nb, D)

    na, nf = ada_w.shape[-1], final_ada_w.shape[-1]
    ada_b_s = lax.dynamic_slice(ada_b, (0, chip * na), (1, na))
    fada_b_s = lax.dynamic_slice(final_ada_b.reshape(1, -1), (0, chip * nf), (1, nf))
    cond_all, modcols = _mod_fwd(c_all, ada_w[0], ada_b_s, final_ada_w, fada_b_s)
    (mod_g,) = _gather_chips("gather_mod", [modcols])
    mine = lax.dynamic_slice(mod_g, (0, me * nb, 0), (4, nb, na + nf))
    modp = jnp.concatenate([mine[:, :, :na].transpose(1, 0, 2).reshape(nb, 6, D),
                            mine[:, :, na:].transpose(1, 0, 2).reshape(nb, 2, D)], axis=1)

    own_late = [_view2d(w[nm]).astype(BF16) for nm in LATE]
    late_gather, token = _split_start("gather_late", own_late,
                                      [jax.ShapeDtypeStruct((4,) + a.shape, a.dtype) for a in own_late],
                                      3 * len(LATE), _plan_to_chips, modp)
    modp = modp + token[0, 0]

    def late_weights(after):
        sent, landed = _split_wait(late_gather, after)
        return {nm: unslot(nm, lax.dynamic_update_slice(g, own[None], (chip, 0, 0)))
                for nm, g, own in zip(LATE, landed, sent)}

    cidx = ci.astype(jnp.int32).reshape(1)
    ahead = ["w_ff1", "w_ff2"]

    class Reducer:
        def start(self, gs):
            lands = [jax.ShapeDtypeStruct((4, g.shape[1] // 2, g.shape[2]), g.dtype) for g in gs]
            self.swap, tok = _split_start("grad_swap_ff", gs, lands, len(gs), _plan_swap_halves, modp)
            return tok

        def middle(self, after):
            gs, got = _split_wait(self.swap, after)
            sums = [_add_half(g, r, cidx, "grad_add_sibling_" + nm) for nm, g, r in zip(ahead, gs, got)]
            lands = [jax.ShapeDtypeStruct(s.shape, s.dtype) for s in sums]
            self.scatter, tok = _split_start("grad_scatter_ff", sums, lands, 3 * len(sums), _plan_scatter_chips, modp)
            return tok

        def finish(self, after):
            out = []
            for nm, s, l in zip(ahead, *_split_wait(self.scatter, after)):
                own = lax.dynamic_slice(s, (chip, 0, 0), (1,) + s.shape[1:])
                out.append(_add_chips(lax.dynamic_update_slice(l, own, (chip, 0, 0)), "grad_add_chips_" + nm))
            return out

    reducer = Reducer()
    loss_row, grad_x, big, small, dmodp = _local_step(x, positions, loss_target, modp, wf, late_weights, reducer)

    rest = [nm for nm in GATHERED if nm not in ahead]
    sizes = [small[nm].size for nm in reduced_names]
    pad = -sum(sizes) % 128
    packed = jnp.concatenate([small[nm].reshape(1, -1) for nm in reduced_names] + [jnp.zeros((1, pad), F32)],
                             axis=1).astype(BF16)
    swapped = _swap_halves([big[nm] for nm in rest], [dmodp.reshape(nb, 8 * D), packed, loss_row])
    chip_sums = [_add_half(big[nm], r, cidx, "grad_add_sibling_" + nm) for nm, r in zip(rest, swapped)]
    half_of = {nm: _add_chips(r, "grad_add_chips_" + nm) for nm, r in zip(rest, _scatter_chips(chip_sums))}
    half_of.update(zip(ahead, reducer.finish(grad_x)))
    halves = [half_of[nm] for nm in GATHERED]
    others = _join_halves(halves)
    grads = {}
    dmod_all = swapped[len(rest)].reshape(8 * nb, 8 * D)
    small_sum, loss_sum = _sum_devices(swapped[len(rest) + 1].reshape(8, -1), swapped[len(rest) + 2].reshape(8, -1))
    loss = jnp.sum(loss_sum)
    off = 0
    for nm, sz in zip(reduced_names, sizes):
        grads[nm] = small_sum[:, off:off + sz].reshape(small[nm].shape)
        off += sz

    dsl = jnp.concatenate([lax.dynamic_slice(dmod_all, (0, chip * na), (8 * nb, na)),
                           lax.dynamic_slice(dmod_all, (0, 6 * D + chip * nf), (8 * nb, nf))], axis=1)
    gw, gb = _mod_bwd(cond_all.T, dsl, dmod_all)
    grads["ada_w"], grads["final_ada_w"] = gw[:, :na], gw[:, na:]
    grads["ada_b"], grads["final_ada_b"] = gb[:, :6 * D], gb[:, 6 * D:]

    delta, new_m, new_v = {}, {}, {}
    for nm, mine_h, other_h in zip(GATHERED, halves, others):
        grads[nm], delta[nm], new_m[nm], new_v[nm] = _adamw_halves(
            _view2d(w[nm]), mine_h, other_h, _view2d(m[nm]), _view2d(v[nm]), cidx, "adamw_" + nm)
    for nm in TP:
        delta[nm], new_m[nm], new_v[nm] = _adamw(_view2d(w[nm]), grads[nm], _view2d(m[nm]), _view2d(v[nm]),
                                                  "adamw_" + nm)
    upd = _adamw_small([_view2d(w[nm]) for nm in small_names], [grads[nm] for nm in small_names],
                       [_view2d(m[nm]) for nm in small_names], [_view2d(v[nm]) for nm in small_names])
    k = len(small_names)
    for t, nm in enumerate(small_names):
        delta[nm], new_m[nm], new_v[nm] = upd[t], upd[k + t], upd[2 * k + t]

    outs = [grads, delta, new_m, new_v]
    return (loss, grad_x, *[d[nm].reshape(w[nm].shape) for d in outs for nm in wnames])
```

```python
import functools
import inspect
import math

import jax
import jax.numpy as jnp
from jax import lax
from jax.experimental import pallas as pl
from jax.experimental.pallas import tpu as pltpu

F32 = jnp.float32
BF16 = jnp.bfloat16

D = 1024
D_SSM = 512
G = 32
H = 16
P = 64
NST = G * P
D_ATTN = 512
NH = 8
QK_NOPE = 64
QK_ROPE = 32
V_HEAD = 64
HP = 128
Q_LORA = 384
KV_LORA = 256
IN_COLS = D_SSM + Q_LORA + KV_LORA + QK_ROPE
IN_PAD = 1280
D_FF = 4096
ROPE_BASE = 10000.0
EPS = 1e-6
ADAM_LR = 0.001
ADAM_B1 = 0.9
ADAM_B2 = 0.999
ADAM_EPS = 1e-08
ADAM_WD = 0.01
ADAM_STEP = 10
NEG = -1e30
VMEM_LIMIT = 60 << 20

MESH = pl.DeviceIdType.MESH
_VM = pl.BlockSpec(memory_space=pltpu.VMEM)
_ANY = pl.BlockSpec(memory_space=pl.ANY)

GATHERED = ["w_in", "w_glu", "w_uq", "w_ukv", "w_out", "w_ff1", "w_ff2"]
TP = ["ada_w", "final_ada_w"]
ROW_SHARDED = ("w_out", "w_ff2")
LATE = ["w_out", "w_ff1", "w_ff2"]


def _cp(sem=None, vmem=VMEM_LIMIT):
    kw = dict(vmem_limit_bytes=vmem)
    if sem is not None:
        kw["dimension_semantics"] = sem
    return pltpu.CompilerParams(**kw)


def _dot(a, b):
    return jnp.dot(a, b, preferred_element_type=F32)


def _dot_nt(a, b):
    return lax.dot_general(a, b, (((1,), (1,)), ((), ())), preferred_element_type=F32)


def _dot_tn(a, b):
    return lax.dot_general(a, b, (((0,), (0,)), ((), ())), preferred_element_type=F32)


def _rms(x, n):
    r = lax.rsqrt(jnp.sum(x * x, axis=-1, keepdims=True) * (1.0 / n) + EPS)
    return x * r, r


def _rms_bwd(dyg, xhat, r, n):
    return r * (dyg - xhat * (jnp.sum(dyg * xhat, axis=-1, keepdims=True) * (1.0 / n)))


def _sigmoid(x):
    return 1.0 / (1.0 + jnp.exp(-x))


_GK = math.sqrt(2.0 / math.pi)
_GC = 0.044715


def _gelu(y):
    t = jnp.tanh(_GK * (y + _GC * y * y * y))
    return 0.5 * y * (1.0 + t)


def _gelu_grad(y):
    t = jnp.tanh(_GK * (y + _GC * y * y * y))
    return 0.5 * (1.0 + t) + 0.5 * y * (1.0 - t * t) * _GK * (1.0 + 3.0 * _GC * y * y)


def _colsum(x):
    return jnp.sum(x, axis=0, keepdims=True)


def _roll(x, s):
    return pltpu.roll(x, s % x.shape[-1], x.ndim - 1)


def _mod_fwd(c_all, ada_w_s, ada_b_s, fada_w_s, fada_b_s):
    nseq = c_all.shape[0]
    na, nf = ada_w_s.shape[1], fada_w_s.shape[1]

    def body(c_ref, w_ref, b_ref, fw_ref, fb_ref, cond_ref, mod_ref):
        cv = c_ref[...]
        cond = cv * _sigmoid(cv)
        cond_ref[...] = cond
        cb = cond.astype(BF16)
        mod_ref[:, 0:na] = _dot(cb, w_ref[...].astype(BF16)) + b_ref[...]
        mod_ref[:, na:na + nf] = _dot(cb, fw_ref[...].astype(BF16)) + fb_ref[...]

    return pl.pallas_call(
        body, name="mod_fwd",
        out_shape=[jax.ShapeDtypeStruct((nseq, D), F32), jax.ShapeDtypeStruct((nseq, na + nf), F32)],
        in_specs=[_VM] * 5, out_specs=[_VM] * 2, compiler_params=_cp(),
    )(c_all, ada_w_s, ada_b_s, fada_w_s, fada_b_s)


def _mod_bwd(cond_t, dsl, dall):
    nseq, n = dsl.shape
    bc = 512

    def body(ct_ref, dm_ref, da_ref, gw_ref, gb_ref):
        ct = ct_ref[...]
        dm = dm_ref[...]
        acc = ct[:, 0:1] * dm[0:1, :]
        for b in range(1, nseq):
            acc = acc + ct[:, b:b + 1] * dm[b:b + 1, :]
        gw_ref[...] = acc

        @pl.when(pl.program_id(0) == 0)
        def _():
            gb_ref[...] = _colsum(da_ref[...])

    return pl.pallas_call(
        body, name="mod_bwd", grid=(n // bc,),
        out_shape=[jax.ShapeDtypeStruct((D, n), F32), jax.ShapeDtypeStruct((1, dall.shape[1]), F32)],
        in_specs=[_VM, pl.BlockSpec((nseq, bc), lambda i: (0, i)), _VM],
        out_specs=[pl.BlockSpec((D, bc), lambda i: (0, i)), pl.BlockSpec((1, dall.shape[1]), lambda i: (0, 0))],
        compiler_params=_cp(("arbitrary",)),
    )(cond_t, dsl, dall)


def _f1_fwd(x, modp, g1, w_in, S, tm):
    n = x.shape[0]
    tps = S // tm

    def body(x_ref, mod_ref, g_ref, w_ref, h_ref, proj_ref):
        xhat, _ = _rms(x_ref[...], D)
        h = (xhat * g_ref[...]) * (1.0 + mod_ref[0, 1:2, :]) + mod_ref[0, 0:1, :]
        hb = h.astype(BF16)
        h_ref[...] = hb
        proj_ref[...] = _dot(hb, w_ref[...])

    return pl.pallas_call(
        body, name="f1_fwd", grid=(n // tm,),
        out_shape=[jax.ShapeDtypeStruct((n, D), BF16), jax.ShapeDtypeStruct((n, IN_PAD), F32)],
        in_specs=[pl.BlockSpec((tm, D), lambda i: (i, 0)),
                  pl.BlockSpec((1, 8, D), lambda i: (i // tps, 0, 0)), _VM, _VM],
        out_specs=[pl.BlockSpec((tm, D), lambda i: (i, 0)), pl.BlockSpec((tm, IN_PAD), lambda i: (i, 0))],
        compiler_params=_cp(("parallel",)),
    )(x, modp, g1, w_in)


def _f1_bwd(du, dmla, dx1, x, modp, g1, w_in, S, tm):
    n = x.shape[0]
    tps = S // tm
    nb = n // S

    def body(du_ref, dm_ref, dx1_ref, x_ref, mod_ref, g_ref, w_ref, dx_ref, dproj_ref, accs_ref, accg_ref):
        i = pl.program_id(0)
        dproj = jnp.concatenate([du_ref[...], dm_ref[...]], axis=1).astype(BF16)
        dproj_ref[...] = dproj
        dh = _dot_nt(dproj, w_ref[...])
        xhat, r = _rms(x_ref[...], D)
        g = g_ref[...]
        dn = dh * (1.0 + mod_ref[0, 1:2, :])
        dx_ref[...] = dx1_ref[...] + _rms_bwd(dn * g, xhat, r, D)

        @pl.when(i % tps == 0)
        def _():
            accs_ref[...] = jnp.zeros_like(accs_ref)

        @pl.when(i == 0)
        def _():
            accg_ref[...] = jnp.zeros_like(accg_ref)

        accs_ref[0, 0:1, :] += _colsum(dh)
        accs_ref[0, 1:2, :] += _colsum(dh * (xhat * g))
        accg_ref[0:1, :] += _colsum(dn * xhat)

    return pl.pallas_call(
        body, name="f1_bwd", grid=(n // tm,),
        out_shape=[jax.ShapeDtypeStruct((n, D), F32), jax.ShapeDtypeStruct((n, IN_PAD), BF16),
                   jax.ShapeDtypeStruct((nb, 8, D), F32), jax.ShapeDtypeStruct((8, D), F32)],
        in_specs=[pl.BlockSpec((tm, D_SSM), lambda i: (i, 0)), pl.BlockSpec((tm, IN_PAD - D_SSM), lambda i: (i, 0)),
                  pl.BlockSpec((tm, D), lambda i: (i, 0)), pl.BlockSpec((tm, D), lambda i: (i, 0)),
                  pl.BlockSpec((1, 8, D), lambda i: (i // tps, 0, 0)), _VM, _VM],
        out_specs=[pl.BlockSpec((tm, D), lambda i: (i, 0)), pl.BlockSpec((tm, IN_PAD), lambda i: (i, 0)),
                   pl.BlockSpec((1, 8, D), lambda i: (i // tps, 0, 0)), pl.BlockSpec((8, D), lambda i: (0, 0))],
        compiler_params=_cp(("arbitrary",)),
    )(du, dmla, dx1, x, modp, g1, w_in)


def _ssm_param_fwd(lam_re, lam_im, logdt, b_re, b_im):
    def body(lr_ref, li_ref, ld_ref, br_ref, bi_ref, lbr_ref, lbi_ref, bbr_ref, bbi_ref):
        lr, li = lr_ref[...], li_ref[...]
        dt = jnp.exp(ld_ref[...])
        er = jnp.exp(lr * dt)
        lbr = er * jnp.cos(li * dt)
        lbi = er * jnp.sin(li * dt)
        den = 1.0 / (lr * lr + li * li)
        cr = ((lbr - 1.0) * lr + lbi * li) * den
        ci = (lbi * lr - (lbr - 1.0) * li) * den
        lbr_ref[...] = lbr
        lbi_ref[...] = lbi
        bbr_ref[...] = cr * br_ref[...] - ci * bi_ref[...]
        bbi_ref[...] = cr * bi_ref[...] + ci * br_ref[...]

    return pl.pallas_call(
        body, name="ssm_param_fwd",
        out_shape=[jax.ShapeDtypeStruct((NST, 1), F32)] * 2 + [jax.ShapeDtypeStruct((NST, H), F32)] * 2,
        in_specs=[_VM] * 5, out_specs=[_VM] * 4, compiler_params=_cp(),
    )(lam_re, lam_im, logdt, b_re, b_im)


def _ssm_param_bwd(lam_re, lam_im, logdt, b_re, b_im, dlb_re, dlb_im, dbb_re, dbb_im):
    def body(lr_ref, li_ref, ld_ref, br_ref, bi_ref, dlr_ref, dli_ref, dbr_ref, dbi_ref,
             gbr_ref, gbi_ref, glr_ref, gli_ref, gdt_ref):
        lr, li = lr_ref[...], li_ref[...]
        dt = jnp.exp(ld_ref[...])
        er = jnp.exp(lr * dt)
        lbr = er * jnp.cos(li * dt)
        lbi = er * jnp.sin(li * dt)
        den = 1.0 / (lr * lr + li * li)
        nr, ni = lbr - 1.0, lbi
        cr = (nr * lr + ni * li) * den
        ci = (ni * lr - nr * li) * den
        br, bi = br_ref[...], bi_ref[...]
        dbr, dbi = dbr_ref[...], dbi_ref[...]
        gbr_ref[...] = cr * dbr + ci * dbi
        gbi_ref[...] = cr * dbi - ci * dbr
        gcr = jnp.sum(dbr * br + dbi * bi, axis=1, keepdims=True)
        gci = jnp.sum(dbi * br - dbr * bi, axis=1, keepdims=True)
        ilr, ili = lr * den, -li * den
        glbr = dlr_ref[...] + (gcr * ilr + gci * ili)
        glbi = dli_ref[...] + (gci * ilr - gcr * ili)
        qr = -(cr * ilr - ci * ili)
        qi = -(cr * ili + ci * ilr)
        glr = gcr * qr + gci * qi
        gli = gci * qr - gcr * qi
        glr = glr + dt * (glbr * lbr + glbi * lbi)
        gli = gli + dt * (glbi * lbr - glbr * lbi)
        wr = lr * lbr - li * lbi
        wi = lr * lbi + li * lbr
        glr_ref[...] = glr
        gli_ref[...] = gli
        gdt_ref[...] = (glbr * wr + glbi * wi) * dt

    return pl.pallas_call(
        body, name="ssm_param_bwd",
        out_shape=[jax.ShapeDtypeStruct((NST, H), F32)] * 2 + [jax.ShapeDtypeStruct((NST, 1), F32)] * 3,
        in_specs=[_VM] * 9, out_specs=[_VM] * 5, compiler_params=_cp(),
    )(lam_re, lam_im, logdt, b_re, b_im, dlb_re, dlb_im, dbb_re, dbb_im)


def _rowsum(a):
    def body(a_ref, o_ref):
        o_ref[...] = jnp.sum(a_ref[...], axis=1, keepdims=True)

    return pl.pallas_call(
        body, name="rowsum", out_shape=jax.ShapeDtypeStruct((a.shape[0], 1), F32),
        in_specs=[_VM], out_specs=_VM, compiler_params=_cp(),
    )(a)


QB = D_SSM // 4
QS = 4 * QB


def _bd_lo(part, q):
    return part * NST + q * QS


def _bd_expand(ub, bm_ref, out_ref):
    for part in range(2):
        for q in range(4):
            lo = _bd_lo(part, q)
            out_ref[:, lo:lo + QS] = _dot(ub[:, q * QB:(q + 1) * QB], bm_ref[:, lo:lo + QS])


def _bd_expand_t(db, cm_ref, out_ref):
    for part in range(2):
        for q in range(4):
            lo = _bd_lo(part, q)
            out_ref[:, lo:lo + QS] = _dot_nt(db[:, q * QB:(q + 1) * QB], cm_ref[lo:lo + QS, :])


def _bd_project(sb, cm_ref):
    return jnp.concatenate(
        [_dot(sb[:, _bd_lo(0, q):_bd_lo(0, q) + QS], cm_ref[_bd_lo(0, q):_bd_lo(0, q) + QS, :])
         + _dot(sb[:, _bd_lo(1, q):_bd_lo(1, q) + QS], cm_ref[_bd_lo(1, q):_bd_lo(1, q) + QS, :])
         for q in range(4)], axis=1)


def _bd_project_t(ab, bm_ref):
    return jnp.concatenate(
        [_dot_nt(ab[:, _bd_lo(0, q):_bd_lo(0, q) + QS], bm_ref[:, _bd_lo(0, q):_bd_lo(0, q) + QS])
         + _dot_nt(ab[:, _bd_lo(1, q):_bd_lo(1, q) + QS], bm_ref[:, _bd_lo(1, q):_bd_lo(1, q) + QS])
         for q in range(4)], axis=1)


def _pow2k(pr, pi, nsq):
    for _ in range(nsq):
        pr, pi = pr * pr - pi * pi, 2.0 * pr * pi
    return pr, pi


def _ssm_local(u_p, bm, lre8, lim8, S, tt):
    n = u_p.shape[0]
    nb, nt = n // S, S // tt
    nsq = int(round(math.log2(S // 8)))
    assert 2 ** nsq == S // 8

    def body(u_ref, bm_ref, lre_ref, lim_ref, cre_ref, cim_ref, sre, sim, bu):
        j = pl.program_id(1)

        @pl.when(j == 0)
        def _():
            sre[...] = jnp.zeros_like(sre)
            sim[...] = jnp.zeros_like(sim)

        _bd_expand(u_ref[...].astype(BF16), bm_ref, bu)
        lre, lim = lre_ref[...], lim_ref[...]

        def step(i, c):
            sr, si = c
            off = pl.multiple_of(i * 8, 8)
            br = bu[pl.ds(off, 8), 0:NST]
            bi = bu[pl.ds(off, 8), NST:2 * NST]
            return lre * sr - lim * si + br, lre * si + lim * sr + bi

        sr, si = lax.fori_loop(0, tt // 8, step, (sre[...], sim[...]))
        sre[...] = sr
        sim[...] = si

        @pl.when(j == nt - 1)
        def _():
            pr, pi = _pow2k(lre[0:1], lim[0:1], nsq)
            cr = jnp.zeros((1, NST), F32)
            ci = jnp.zeros((1, NST), F32)
            cre_ref[0:1, :] = cr
            cim_ref[0:1, :] = ci
            for k in range(1, 8):
                cr, ci = sr[k - 1:k] + pr * cr - pi * ci, si[k - 1:k] + pr * ci + pi * cr
                cre_ref[k:k + 1, :] = cr
                cim_ref[k:k + 1, :] = ci

    return pl.pallas_call(
        body, name="ssm_local", grid=(nb, nt),
        out_shape=[jax.ShapeDtypeStruct((nb * 8, NST), F32)] * 2,
        in_specs=[pl.BlockSpec((tt, D_SSM), lambda b, j: (b * nt + j, 0)), _VM, _VM, _VM],
        out_specs=[pl.BlockSpec((8, NST), lambda b, j: (b, 0))] * 2,
        scratch_shapes=[pltpu.VMEM((8, NST), F32), pltpu.VMEM((8, NST), F32), pltpu.VMEM((tt, 2 * NST), F32)],
        compiler_params=_cp(("arbitrary", "arbitrary")),
    )(u_p, bm, lre8, lim8)


def _ssm_fwd(u_p, cre, cim, bm, cm, dvec, w_glu, lre8, lim8, S, tt):
    n = u_p.shape[0]
    nb, nt = n // S, S // tt

    def body(u_ref, cre_ref, cim_ref, bm_ref, cm_ref, d_ref, wg_ref, lre_ref, lim_ref,
             st_ref, ypre_ref, z_ref, gact_ref, yssm_ref, sre, sim, bu):
        j = pl.program_id(1)

        @pl.when(j == 0)
        def _():
            sre[...] = cre_ref[...]
            sim[...] = cim_ref[...]

        u = u_ref[...]
        _bd_expand(u.astype(BF16), bm_ref, bu)
        lre, lim = lre_ref[...], lim_ref[...]

        def step(i, c):
            sr, si = c
            off = pl.multiple_of(i * 8, 8)
            nr = lre * sr - lim * si + bu[pl.ds(off, 8), 0:NST]
            ni = lre * si + lim * sr + bu[pl.ds(off, 8), NST:2 * NST]
            st_ref[pl.ds(off, 8), 0:NST] = nr
            st_ref[pl.ds(off, 8), NST:2 * NST] = ni
            return nr, ni

        sr, si = lax.fori_loop(0, tt // 8, step, (sre[...], sim[...]))
        sre[...] = sr
        sim[...] = si
        y = _bd_project(st_ref[...].astype(BF16), cm_ref) + d_ref[...] * u
        ypre_ref[...] = y
        gb = _gelu(y).astype(BF16)
        gact_ref[...] = gb
        z = _dot(gb, wg_ref[...])
        z_ref[...] = z
        yssm_ref[...] = z[:, 0:D_SSM] * _sigmoid(z[:, D_SSM:2 * D_SSM])

    row = lambda w: pl.BlockSpec((tt, w), lambda b, j: (b * nt + j, 0))
    return pl.pallas_call(
        body, name="ssm_fwd", grid=(nb, nt),
        out_shape=[jax.ShapeDtypeStruct((n, 2 * NST), F32), jax.ShapeDtypeStruct((n, D_SSM), F32),
                   jax.ShapeDtypeStruct((n, 2 * D_SSM), F32), jax.ShapeDtypeStruct((n, D_SSM), BF16),
                   jax.ShapeDtypeStruct((n, D_SSM), F32)],
        in_specs=[row(D_SSM), pl.BlockSpec((8, NST), lambda b, j: (b, 0)), pl.BlockSpec((8, NST), lambda b, j: (b, 0)),
                  _VM, _VM, _VM, _VM, _VM, _VM],
        out_specs=[row(2 * NST), row(D_SSM), row(2 * D_SSM), row(D_SSM), row(D_SSM)],
        scratch_shapes=[pltpu.VMEM((8, NST), F32), pltpu.VMEM((8, NST), F32), pltpu.VMEM((tt, 2 * NST), F32)],
        compiler_params=_cp(("arbitrary", "arbitrary")),
    )(u_p, cre, cim, bm, cm, dvec, w_glu, lre8, lim8)


def _ssm_bwd_a(dys_p, z, ypre, w_glu, cm, lre8, lim8, S, tt):
    n = z.shape[0]
    nb, nt = n // S, S // tt
    nsq = int(round(math.log2(S // 8)))
    ng = tt // 8

    def body(dys_ref, z_ref, y_ref, wg_ref, cm_ref, lre_ref, lim_ref, dy_ref, dz_ref, are_ref, aim_ref, sre, sim, gb):
        j = pl.program_id(1)

        @pl.when(j == 0)
        def _():
            sre[...] = jnp.zeros_like(sre)
            sim[...] = jnp.zeros_like(sim)

        z = z_ref[...]
        z1, z2 = z[:, 0:D_SSM], z[:, D_SSM:2 * D_SSM]
        sg = _sigmoid(z2)
        dys = dys_ref[...]
        dz = jnp.concatenate([dys * sg, dys * z1 * sg * (1.0 - sg)], axis=1).astype(BF16)
        dz_ref[...] = dz
        dy = _dot_nt(dz, wg_ref[...]) * _gelu_grad(y_ref[...])
        dy_ref[...] = dy
        _bd_expand_t(dy.astype(BF16), cm_ref, gb)
        lre, lim = lre_ref[...], lim_ref[...]

        def step(i, c):
            ar, ai = c
            off = pl.multiple_of((ng - 1 - i) * 8, 8)
            gr = gb[pl.ds(off, 8), 0:NST]
            gi = gb[pl.ds(off, 8), NST:2 * NST]
            return lre * ar + lim * ai + gr, lre * ai - lim * ar + gi

        ar, ai = lax.fori_loop(0, ng, step, (sre[...], sim[...]))
        sre[...] = ar
        sim[...] = ai

        @pl.when(j == nt - 1)
        def _():
            pr, pi = _pow2k(lre[0:1], -lim[0:1], nsq)
            cr = jnp.zeros((1, NST), F32)
            ci = jnp.zeros((1, NST), F32)
            are_ref[7:8, :] = cr
            aim_ref[7:8, :] = ci
            for k in range(6, -1, -1):
                cr, ci = ar[k + 1:k + 2] + pr * cr - pi * ci, ai[k + 1:k + 2] + pr * ci + pi * cr
                are_ref[k:k + 1, :] = cr
                aim_ref[k:k + 1, :] = ci

    row = lambda w: pl.BlockSpec((tt, w), lambda b, j: (b * nt + nt - 1 - j, 0))
    return pl.pallas_call(
        body, name="ssm_bwd_a", grid=(nb, nt),
        out_shape=[jax.ShapeDtypeStruct((n, D_SSM), F32), jax.ShapeDtypeStruct((n, 2 * D_SSM), BF16),
                   jax.ShapeDtypeStruct((nb * 8, NST), F32), jax.ShapeDtypeStruct((nb * 8, NST), F32)],
        in_specs=[row(D_SSM), row(2 * D_SSM), row(D_SSM), _VM, _VM, _VM, _VM],
        out_specs=[row(D_SSM), row(2 * D_SSM), pl.BlockSpec((8, NST), lambda b, j: (b, 0)),
                   pl.BlockSpec((8, NST), lambda b, j: (b, 0))],
        scratch_shapes=[pltpu.VMEM((8, NST), F32), pltpu.VMEM((8, NST), F32), pltpu.VMEM((tt, 2 * NST), F32)],
        compiler_params=_cp(("arbitrary", "arbitrary")),
    )(dys_p, z, ypre, w_glu, cm, lre8, lim8)


def _ssm_bwd_b(dy, u_p, st, fcr, fci, air, aii, bm, cm, dvec, lre8, lim8, S, tt):
    n = u_p.shape[0]
    nb, nt = n // S, S // tt
    ng = tt // 8

    def body(dy_ref, u_ref, st_ref, stp_ref, fcr_ref, fci_ref, air_ref, aii_ref, bm_ref, cm_ref, d_ref, lre_ref, lim_ref,
             du_ref, dcm_ref, dbm_ref, dd_ref, dlr_ref, dli_ref, are, aim, accr, acci, sp, ab):
        b = pl.program_id(0)
        j = pl.program_id(1)
        jt = nt - 1 - j

        @pl.when((b == 0) & (j == 0))
        def _():
            dcm_ref[...] = jnp.zeros_like(dcm_ref)
            dbm_ref[...] = jnp.zeros_like(dbm_ref)
            dd_ref[...] = jnp.zeros_like(dd_ref)
            accr[...] = jnp.zeros_like(accr)
            acci[...] = jnp.zeros_like(acci)

        @pl.when(j == 0)
        def _():
            are[...] = air_ref[...]
            aim[...] = aii_ref[...]

        sp[8:tt + 8, :] = st_ref[...]

        @pl.when(jt == 0)
        def _():
            sp[0:8, 0:NST] = fcr_ref[...]
            sp[0:8, NST:2 * NST] = fci_ref[...]

        @pl.when(jt != 0)
        def _():
            sp[0:8, :] = stp_ref[...]

        dy = dy_ref[...]
        u = u_ref[...]
        dyb = dy.astype(BF16)
        _bd_expand_t(dyb, cm_ref, ab)
        lre, lim = lre_ref[...], lim_ref[...]

        def step(i, c):
            ar, ai = c
            off = pl.multiple_of((ng - 1 - i) * 8, 8)
            nr = lre * ar + lim * ai + ab[pl.ds(off, 8), 0:NST]
            ni = lre * ai - lim * ar + ab[pl.ds(off, 8), NST:2 * NST]
            ab[pl.ds(off, 8), 0:NST] = nr
            ab[pl.ds(off, 8), NST:2 * NST] = ni
            pr = sp[pl.ds(off, 8), 0:NST]
            pi = sp[pl.ds(off, 8), NST:2 * NST]
            accr[...] += nr * pr + ni * pi
            acci[...] += ni * pr - nr * pi
            return nr, ni

        ar, ai = lax.fori_loop(0, ng, step, (are[...], aim[...]))
        are[...] = ar
        aim[...] = ai
        a_b = ab[...].astype(BF16)
        du_ref[...] = _bd_project_t(a_b, bm_ref) + d_ref[...] * dy
        ub = u.astype(BF16)
        for q in range(4):
            for part in range(2):
                lo = part * NST + q * 4 * QB
                s_q = sp[8:tt + 8, lo:lo + 4 * QB].astype(BF16)
                dcm_ref[lo:lo + 4 * QB, :] += _dot_tn(s_q, dyb[:, q * QB:(q + 1) * QB])
                dbm_ref[:, lo:lo + 4 * QB] += _dot_tn(ub[:, q * QB:(q + 1) * QB], a_b[:, lo:lo + 4 * QB])
        dd_ref[...] += _colsum(dy * u)

        @pl.when((b == nb - 1) & (j == nt - 1))
        def _():
            dlr_ref[...] = _colsum(accr[...])
            dli_ref[...] = _colsum(acci[...])

    row = lambda w: pl.BlockSpec((tt, w), lambda b, j: (b * nt + nt - 1 - j, 0))
    seq8 = pl.BlockSpec((8, NST), lambda b, j: (b, 0))
    prev = pl.BlockSpec((8, 2 * NST), lambda b, j: (jnp.maximum((b * nt + nt - 1 - j) * ng - 1, 0), 0))
    const = lambda shape: pl.BlockSpec(shape, lambda b, j: (0, 0))
    return pl.pallas_call(
        body, name="ssm_bwd_b", grid=(nb, nt),
        out_shape=[jax.ShapeDtypeStruct((n, D_SSM), F32), jax.ShapeDtypeStruct((2 * NST, QB), F32),
                   jax.ShapeDtypeStruct((QB, 2 * NST), F32), jax.ShapeDtypeStruct((1, D_SSM), F32),
                   jax.ShapeDtypeStruct((1, NST), F32), jax.ShapeDtypeStruct((1, NST), F32)],
        in_specs=[row(D_SSM), row(D_SSM), row(2 * NST), prev, seq8, seq8, seq8, seq8, _VM, _VM, _VM, _VM, _VM],
        out_specs=[row(D_SSM), const((2 * NST, QB)), const((QB, 2 * NST)), const((1, D_SSM)),
                   const((1, NST)), const((1, NST))],
        scratch_shapes=[pltpu.VMEM((8, NST), F32)] * 4 + [pltpu.VMEM((tt + 8, 2 * NST), F32),
                                                          pltpu.VMEM((tt, 2 * NST), F32)],
        compiler_params=_cp(("arbitrary", "arbitrary")),
    )(dy, u_p, st, st, fcr, fci, air, aii, bm, cm, dvec, lre8, lim8)


def _rope(v, c, s1, s2):
    return v * c + _roll(v, -16) * s1 + _roll(v, 16) * s2


def _rope_t(dv, c, s1, s2):
    return dv * c + _roll(dv * s1, 16) + _roll(dv * s2, -16)


def _mla_fwd(proj, rc, rs1, rs2, gq, gkv, w_uq, w_ukv, tm):
    n = proj.shape[0]

    def body(ql_ref, kvl_ref, kr_ref, c_ref, s1_ref, s2_ref, gq_ref, gkv_ref, wq_ref, wkv_ref,
             q_ref, k_ref, v_ref, qn_ref, kvn_ref):
        c, s1, s2 = c_ref[...], s1_ref[...], s2_ref[...]
        qhat, _ = _rms(ql_ref[...], Q_LORA)
        qn = (qhat * gq_ref[...]).astype(BF16)
        qn_ref[...] = qn
        q = _dot(qn, wq_ref[...])
        qr = _rope(q, jnp.tile(c, (1, NH)), jnp.tile(s1, (1, NH)), jnp.tile(s2, (1, NH)))
        q_ref[...] = (qr * _C2).astype(BF16)
        khat, _ = _rms(kvl_ref[...], KV_LORA)
        kvn = (khat * gkv_ref[...]).astype(BF16)
        kvn_ref[...] = kvn
        kv = _dot(kvn, wkv_ref[...])
        kr = _rope(_roll(kr_ref[...], 64), c, s1, s2)
        k_ref[...] = (kv[:, 0:NH * HP] + jnp.tile(kr, (1, NH))).astype(BF16)
        v_ref[...] = kv[:, NH * HP:2 * NH * HP].astype(BF16)

    def wrapped(proj_ref, *rest):
        ql = proj_ref.at[:, D_SSM:D_SSM + Q_LORA]
        kvl = proj_ref.at[:, D_SSM + Q_LORA:D_SSM + Q_LORA + KV_LORA]
        kr = proj_ref.at[:, IN_PAD - HP:IN_PAD]
        body(ql, kvl, kr, *rest)

    row = lambda w: pl.BlockSpec((tm, w), lambda i: (i, 0))
    return pl.pallas_call(
        wrapped, name="mla_fwd", grid=(n // tm,),
        out_shape=[jax.ShapeDtypeStruct((n, NH * HP), BF16)] * 3 +
                  [jax.ShapeDtypeStruct((n, Q_LORA), BF16), jax.ShapeDtypeStruct((n, KV_LORA), BF16)],
        in_specs=[row(IN_PAD), row(HP), row(HP), row(HP), _VM, _VM, _VM, _VM],
        out_specs=[row(NH * HP)] * 3 + [row(Q_LORA), row(KV_LORA)],
        compiler_params=_cp(("parallel",)),
    )(proj, rc, rs1, rs2, gq, gkv, w_uq, w_ukv)


def _mla_bwd(dq, dk, dv, proj, rc, rs1, rs2, gq, gkv, w_uq, w_ukv, tm):
    n = proj.shape[0]

    def body(dq_ref, dk_ref, dv_ref, proj_ref, c_ref, s1_ref, s2_ref, gq_ref, gkv_ref, wq_ref, wkv_ref,
             dmla_ref, dqb_ref, dkvb_ref, acc_ref):
        i = pl.program_id(0)
        c, s1, s2 = c_ref[...], s1_ref[...], s2_ref[...]
        dqu = _rope_t(dq_ref[...] * _SCALE, jnp.tile(c, (1, NH)), jnp.tile(s1, (1, NH)),
                      jnp.tile(s2, (1, NH))).astype(BF16)
        dqb_ref[...] = dqu
        dqn = _dot_nt(dqu, wq_ref[...])
        qhat, rq = _rms(proj_ref[:, D_SSM:D_SSM + Q_LORA], Q_LORA)
        dql = _rms_bwd(dqn * gq_ref[...], qhat, rq, Q_LORA)
        dkf = dk_ref[...] * (1.0 / _LOG2E)
        dkv = jnp.concatenate([dkf, dv_ref[...]], axis=1).astype(BF16)
        dkvb_ref[...] = dkv
        dkvn = _dot_nt(dkv, wkv_ref[...])
        khat, rk = _rms(proj_ref[:, D_SSM + Q_LORA:D_SSM + Q_LORA + KV_LORA], KV_LORA)
        dkvl = _rms_bwd(dkvn * gkv_ref[...], khat, rk, KV_LORA)
        dkr = dkf[:, 0:HP]
        for h in range(1, NH):
            dkr = dkr + dkf[:, h * HP:(h + 1) * HP]
        lane = lax.broadcasted_iota(jnp.int32, dkr.shape, 1)
        dkr = jnp.where((lane >= QK_NOPE) & (lane < QK_NOPE + QK_ROPE), dkr, 0.0)
        dkr = _roll(_rope_t(dkr, c, s1, s2), -64)
        dmla_ref[...] = jnp.concatenate([dql, dkvl, dkr], axis=1)

        @pl.when(i == 0)
        def _():
            acc_ref[...] = jnp.zeros_like(acc_ref)

        acc_ref[0:1, 0:Q_LORA] += _colsum(dqn * qhat)
        acc_ref[1:2, 0:KV_LORA] += _colsum(dkvn * khat)

    row = lambda w: pl.BlockSpec((tm, w), lambda i: (i, 0))
    return pl.pallas_call(
        body, name="mla_bwd", grid=(n // tm,),
        out_shape=[jax.ShapeDtypeStruct((n, IN_PAD - D_SSM), F32), jax.ShapeDtypeStruct((n, NH * HP), BF16),
                   jax.ShapeDtypeStruct((n, 2 * NH * HP), BF16), jax.ShapeDtypeStruct((8, Q_LORA), F32)],
        in_specs=[row(NH * HP)] * 3 + [row(IN_PAD), row(HP), row(HP), row(HP), _VM, _VM, _VM, _VM],
        out_specs=[row(IN_PAD - D_SSM), row(NH * HP), row(2 * NH * HP), pl.BlockSpec((8, Q_LORA), lambda i: (0, 0))],
        compiler_params=_cp(("arbitrary",)),
    )(dq, dk, dv, proj, rc, rs1, rs2, gq, gkv, w_uq, w_ukv)


_SCALE = (QK_NOPE + QK_ROPE) ** -0.5
_LOG2E = 1.4426950408889634
_C2 = _SCALE * _LOG2E


def _attn_fwd(q, k, v, S, tq):
    n = q.shape[0]
    nb, nq = n // S, S // tq

    def body(q_ref, k_ref, v_ref, o_ref, lr_ref):
        qi = pl.program_id(2)
        qv = q_ref[...]

        def tile(j, c, diagonal):
            m, l, acc = c
            off = pl.multiple_of(j * tq, tq)
            s = _dot_nt(qv, k_ref[pl.ds(off, tq), :])
            if diagonal:
                rows = lax.broadcasted_iota(jnp.int32, s.shape, 0)
                cols = lax.broadcasted_iota(jnp.int32, s.shape, 1)
                s = jnp.where(cols <= rows, s, NEG)
            mn = jnp.maximum(m, jnp.max(s, axis=1, keepdims=True))
            p = jnp.exp2(s - mn)
            al = jnp.exp2(m - mn)
            l = al * l + jnp.sum(p, axis=1, keepdims=True)
            acc = al * acc + _dot(p.astype(BF16), v_ref[pl.ds(off, tq), :])
            return mn, l, acc

        init = (jnp.full((tq, 1), NEG, F32), jnp.zeros((tq, 1), F32), jnp.zeros((tq, HP), F32))
        c = lax.fori_loop(0, qi, lambda j, c: tile(j, c, False), init)
        m, l, acc = tile(qi, c, True)
        o_ref[...] = acc / l
        lane = lax.broadcasted_iota(jnp.int32, (8, HP), 1)
        lse = jnp.broadcast_to(m + jnp.log(l) * _LOG2E, (tq, HP))
        lr_ref[...] = _rows_of(lse, jnp.where(lane == 0, 1.0, 0.0).astype(BF16))

    qs = pl.BlockSpec((tq, HP), lambda b, h, i: (b * nq + i, h))
    ks = pl.BlockSpec((S, HP), lambda b, h, i: (b, h))
    return pl.pallas_call(
        body, name="attn_fwd", grid=(nb, NH, nq),
        out_shape=[jax.ShapeDtypeStruct((n, NH * HP), F32), jax.ShapeDtypeStruct((nb * NH * 8, S), F32)],
        in_specs=[qs, ks, ks], out_specs=[qs, pl.BlockSpec((8, tq), lambda b, h, i: (b * NH + h, i))],
        compiler_params=_cp(("parallel", "parallel", "arbitrary")),
    )(q, k, v)


def _rows_of(x, pick):
    x1 = x.astype(BF16)
    r1 = x - x1.astype(F32)
    x2 = r1.astype(BF16)
    x3 = (r1 - x2.astype(F32)).astype(BF16)
    return _dot_nt(pick, x1) + _dot_nt(pick, x2) + _dot_nt(pick, x3)


def _attn_bwd(q, k, v, dob, lrow, drow, S, tq):
    n = q.shape[0]
    nb, nq = n // S, S // tq

    def body(q_ref, k_ref, v_ref, do_ref, lr_ref, dr_ref, dq_ref, dk_ref, dv_ref):
        kj = pl.program_id(2)

        @pl.when(kj == 0)
        def _():
            dq_ref[...] = jnp.zeros_like(dq_ref)

        kt = k_ref[...]
        vt = v_ref[...]

        def tile(i, c, diagonal):
            dk, dv = c
            off = pl.multiple_of(i * tq, tq)
            qv = q_ref[pl.ds(off, tq), :]
            dob = do_ref[pl.ds(off, tq), :]
            lr = lr_ref[0:1, pl.ds(off, tq)]
            dr = dr_ref[0:1, pl.ds(off, tq)]
            st = _dot_nt(kt, qv)
            dpt = _dot_nt(vt, dob)
            pt = jnp.exp2(st - lr)
            if diagonal:
                keys = lax.broadcasted_iota(jnp.int32, pt.shape, 0)
                qrys = lax.broadcasted_iota(jnp.int32, pt.shape, 1)
                pt = jnp.where(keys <= qrys, pt, 0.0)
            dst = (pt * (dpt - dr)).astype(BF16)
            dq_ref[pl.ds(off, tq), :] += _dot_tn(dst, kt)
            return dk + _dot(dst, qv), dv + _dot(pt.astype(BF16), dob)

        zero = jnp.zeros((tq, HP), F32)
        c = tile(kj, (zero, zero), True)
        dk, dv = lax.fori_loop(kj + 1, nq, lambda i, c: tile(i, c, False), c)
        dk_ref[...] = dk
        dv_ref[...] = dv

    ts = pl.BlockSpec((tq, HP), lambda b, h, i: (b * nq + i, h))
    fs = pl.BlockSpec((S, HP), lambda b, h, i: (b, h))
    rs = pl.BlockSpec((8, S), lambda b, h, i: (b * NH + h, 0))
    return pl.pallas_call(
        body, name="attn_bwd", grid=(nb, NH, nq),
        out_shape=[jax.ShapeDtypeStruct((n, NH * HP), F32)] * 3,
        in_specs=[fs, ts, ts, fs, rs, rs], out_specs=[fs, ts, ts],
        compiler_params=_cp(("parallel", "parallel", "arbitrary")),
    )(q, k, v, dob, lrow, drow)


def _p1_fwd(yssm, oattn, x, modp, gs, ga, w_out, g2, S, tm):
    n = x.shape[0]
    tps = S // tm

    def body(ys_ref, oa_ref, x_ref, mod_ref, gs_ref, ga_ref, w_ref, g2_ref, yn_ref, o_ref, x1_ref, h2_ref):
        yh, _ = _rms(ys_ref[...], D_SSM)
        ah, _ = _rms(oa_ref[...], D_ATTN)
        yn = jnp.concatenate([yh * gs_ref[...], ah * ga_ref[...]], axis=1).astype(BF16)
        yn_ref[...] = yn
        o = _dot(yn, w_ref[...])
        o_ref[...] = o
        x1 = x_ref[...] + mod_ref[0, 2:3, :] * o
        x1_ref[...] = x1
        xh, _ = _rms(x1, D)
        h2_ref[...] = ((xh * g2_ref[...]) * (1.0 + mod_ref[0, 4:5, :]) + mod_ref[0, 3:4, :]).astype(BF16)

    row = lambda w: pl.BlockSpec((tm, w), lambda i: (i, 0))
    return pl.pallas_call(
        body, name="p1_fwd", grid=(n // tm,),
        out_shape=[jax.ShapeDtypeStruct((n, D_SSM + NH * HP), BF16), jax.ShapeDtypeStruct((n, D), F32),
                   jax.ShapeDtypeStruct((n, D), F32), jax.ShapeDtypeStruct((n, D), BF16)],
        in_specs=[row(D_SSM), row(NH * HP), row(D), pl.BlockSpec((1, 8, D), lambda i: (i // tps, 0, 0)),
                  _VM, _VM, _VM, _VM],
        out_specs=[row(D_SSM + NH * HP), row(D), row(D), row(D)],
        compiler_params=_cp(("parallel",)),
    )(yssm, oattn, x, modp, gs, ga, w_out, g2)


def _p2(x1, h2, target, modp, g2, gf, w_ff1, w_ff2, S, tm):
    n = x1.shape[0]
    tps = S // tm
    nb = n // S

    def body(x1_ref, h2_ref, t_ref, mod_ref, g2_ref, gf_ref, w1_ref, w2_ref,
             dx1_ref, r_ref, da_ref, dff_ref, accs_ref, accg_ref):
        i = pl.program_id(0)
        sh2, sc2, gt2 = mod_ref[0, 3:4, :], mod_ref[0, 4:5, :], mod_ref[0, 5:6, :]
        fsh, fsc = mod_ref[0, 6:7, :], mod_ref[0, 7:8, :]
        x1 = x1_ref[...]
        a = _dot(h2_ref[...], w1_ref[...])
        ra = jnp.maximum(a, 0.0)
        rb = (ra * ra).astype(BF16)
        r_ref[...] = rb
        ff = _dot(rb, w2_ref[...])
        x2 = x1 + gt2 * ff
        x2h, rf = _rms(x2, D)
        gf_v = gf_ref[...]
        outn = x2h * gf_v
        err = outn * (1.0 + fsc) + fsh - t_ref[...]
        dout = err * (1.0 / D)
        doutn = dout * (1.0 + fsc)
        dx2 = _rms_bwd(doutn * gf_v, x2h, rf, D)
        dff = (gt2 * dx2).astype(BF16)
        dff_ref[...] = dff
        dr = _dot_nt(dff, w2_ref[...])
        da = (dr * (2.0 * ra)).astype(BF16)
        da_ref[...] = da
        dh2 = _dot_nt(da, w1_ref[...])
        x1h, r2 = _rms(x1, D)
        g2_v = g2_ref[...]
        dn2 = dh2 * (1.0 + sc2)
        dx1_ref[...] = dx2 + _rms_bwd(dn2 * g2_v, x1h, r2, D)

        @pl.when(i % tps == 0)
        def _():
            accs_ref[...] = jnp.zeros_like(accs_ref)

        @pl.when(i == 0)
        def _():
            accg_ref[...] = jnp.zeros_like(accg_ref)

        accs_ref[0, 3:4, :] += _colsum(dh2)
        accs_ref[0, 4:5, :] += _colsum(dh2 * (x1h * g2_v))
        accs_ref[0, 5:6, :] += _colsum(dx2 * ff)
        accs_ref[0, 6:7, :] += _colsum(dout)
        accs_ref[0, 7:8, :] += _colsum(dout * outn)
        accg_ref[0:1, :] += _colsum(dn2 * x1h)
        accg_ref[1:2, :] += _colsum(doutn * x2h)
        accg_ref[2:3, :] += _colsum(err * err) * (0.5 / D)

    row = lambda w: pl.BlockSpec((tm, w), lambda i: (i, 0))
    return pl.pallas_call(
        body, name="p2_mlp_loss", grid=(n // tm,),
        out_shape=[jax.ShapeDtypeStruct((n, D), F32), jax.ShapeDtypeStruct((n, D_FF), BF16),
                   jax.ShapeDtypeStruct((n, D_FF), BF16), jax.ShapeDtypeStruct((n, D), BF16),
                   jax.ShapeDtypeStruct((nb, 8, D), F32), jax.ShapeDtypeStruct((8, D), F32)],
        in_specs=[row(D), row(D), row(D), pl.BlockSpec((1, 8, D), lambda i: (i // tps, 0, 0)), _VM, _VM, _VM, _VM],
        out_specs=[row(D), row(D_FF), row(D_FF), row(D), pl.BlockSpec((1, 8, D), lambda i: (i // tps, 0, 0)),
                   pl.BlockSpec((8, D), lambda i: (0, 0))],
        compiler_params=_cp(("arbitrary",)),
    )(x1, h2, target, modp, g2, gf, w_ff1, w_ff2)


def _p3_bwd(dx1, o, yssm, oattn, modp, gs, ga, w_out, S, tm):
    n = dx1.shape[0]
    tps = S // tm
    nb = n // S

    def body(dx1_ref, o_ref, ys_ref, oa_ref, mod_ref, gs_ref, ga_ref, w_ref,
             do_ref, dys_ref, doa_ref, dr_ref, accs_ref, accg_ref):
        i = pl.program_id(0)
        dx1 = dx1_ref[...]
        dob = (mod_ref[0, 2:3, :] * dx1).astype(BF16)
        do_ref[...] = dob
        dyn = _dot_nt(dob, w_ref[...])
        yh, rs = _rms(ys_ref[...], D_SSM)
        oa = oa_ref[...]
        ah, ra = _rms(oa, D_ATTN)
        d1 = dyn[:, 0:D_SSM]
        d2 = dyn[:, D_SSM:D_SSM + NH * HP]
        dys_ref[...] = _rms_bwd(d1 * gs_ref[...], yh, rs, D_SSM)
        doa = _rms_bwd(d2 * ga_ref[...], ah, ra, D_ATTN)
        doa_ref[...] = doa.astype(BF16)
        prod = doa * oa
        ones = jnp.ones((8, HP), BF16)
        for h in range(NH):
            dr_ref[h * 8:(h + 1) * 8, :] = _rows_of(prod[:, h * HP:(h + 1) * HP], ones)

        @pl.when(i % tps == 0)
        def _():
            accs_ref[...] = jnp.zeros_like(accs_ref)

        @pl.when(i == 0)
        def _():
            accg_ref[...] = jnp.zeros_like(accg_ref)

        accs_ref[0, 2:3, :] += _colsum(dx1 * o_ref[...])
        accg_ref[0:1, 0:D_SSM] += _colsum(d1 * yh)
        accg_ref[1:2, :] += _colsum(d2 * ah)

    row = lambda w: pl.BlockSpec((tm, w), lambda i: (i, 0))
    return pl.pallas_call(
        body, name="p3_bwd", grid=(n // tm,),
        out_shape=[jax.ShapeDtypeStruct((n, D), BF16), jax.ShapeDtypeStruct((n, D_SSM), F32),
                   jax.ShapeDtypeStruct((n, NH * HP), BF16), jax.ShapeDtypeStruct((nb * NH * 8, S), F32),
                   jax.ShapeDtypeStruct((nb, 8, D), F32), jax.ShapeDtypeStruct((8, NH * HP), F32)],
        in_specs=[row(D), row(D), row(D_SSM), row(NH * HP), pl.BlockSpec((1, 8, D), lambda i: (i // tps, 0, 0)),
                  _VM, _VM, _VM],
        out_specs=[row(D), row(D_SSM), row(NH * HP), pl.BlockSpec((NH * 8, tm), lambda i: (i // tps, i % tps)),
                   pl.BlockSpec((1, 8, D), lambda i: (i // tps, 0, 0)), pl.BlockSpec((8, NH * HP), lambda i: (0, 0))],
        compiler_params=_cp(("arbitrary",)),
    )(dx1, o, yssm, oattn, modp, gs, ga, w_out)


def _wgrad(a, b, name, col_slots=0):
    n, k1 = a.shape
    k2 = b.shape[1]
    bn = next((b for b in (1024, 512) if n % b == 0), n)
    bk1 = next((b for b in (1024, 512) if k1 % b == 0), k1)
    bk2 = k2 // col_slots if col_slots else (1024 if (k2 % 1024 == 0) else k2)

    def body(a_ref, b_ref, o_ref):
        @pl.when(pl.program_id(2) == 0)
        def _():
            o_ref[...] = jnp.zeros_like(o_ref)

        o_ref[...] += _dot_tn(a_ref[...], b_ref[...]).reshape(o_ref.shape)

    if col_slots:
        out_shape = jax.ShapeDtypeStruct((col_slots, k1, bk2), F32)
        out_spec = pl.BlockSpec((1, bk1, bk2), lambda i, j, t: (j, i, 0))
    else:
        out_shape = jax.ShapeDtypeStruct((k1, k2), F32)
        out_spec = pl.BlockSpec((bk1, bk2), lambda i, j, t: (i, j))
    return pl.pallas_call(
        body, name=name, grid=(k1 // bk1, k2 // bk2, n // bn),
        out_shape=out_shape,
        in_specs=[pl.BlockSpec((bn, bk1), lambda i, j, t: (t, i)), pl.BlockSpec((bn, bk2), lambda i, j, t: (t, j))],
        out_specs=out_spec,
        compiler_params=_cp(("parallel", "parallel", "arbitrary")),
    )(a, b)


def _row_block(rows):
    if rows <= 256:
        return rows
    return next(b for b in (256, 192, 128, 64, 32, 16, 8) if rows % b == 0)


def _add_half(g, recv, cidx, name):
    _, rows2, w = g.shape
    rows = rows2 // 2
    br = _row_block(rows)
    nblk = rows // br

    def body(c_ref, g_ref, r_ref, o_ref):
        o_ref[...] = (g_ref[...] + r_ref[...]).astype(BF16)

    return pl.pallas_call(
        body, name=name,
        grid_spec=pltpu.PrefetchScalarGridSpec(
            num_scalar_prefetch=1, grid=(4, nblk),
            in_specs=[pl.BlockSpec((1, br, w), lambda s, i, c: (s, c[0] * nblk + i, 0)),
                      pl.BlockSpec((1, br, w), lambda s, i, c: (s, i, 0))],
            out_specs=pl.BlockSpec((1, br, w), lambda s, i, c: (s, i, 0))),
        out_shape=jax.ShapeDtypeStruct((4, rows, w), BF16),
        compiler_params=_cp(("parallel", "parallel")),
    )(cidx, g, recv)


def _add_chips(r, name):
    _, rows, w = r.shape
    br = _row_block(rows)

    def body(r_ref, o_ref):
        f = lambda k: r_ref[k].astype(F32)
        o_ref[...] = ((f(0) + f(1)) + f(2)) + f(3)

    return pl.pallas_call(
        body, name=name, grid=(rows // br,),
        out_shape=jax.ShapeDtypeStruct((rows, w), F32),
        in_specs=[pl.BlockSpec((4, br, w), lambda i: (0, i, 0))],
        out_specs=pl.BlockSpec((br, w), lambda i: (i, 0)),
        compiler_params=_cp(("parallel",)),
    )(r)


def _sum_devices(a, b):
    def body(a_ref, b_ref, oa_ref, ob_ref):
        acc = a_ref[0:1, :].astype(F32)
        accb = b_ref[0:1, :]
        for k in range(1, 8):
            acc = acc + a_ref[k:k + 1, :].astype(F32)
            accb = accb + b_ref[k:k + 1, :]
        oa_ref[...] = acc
        ob_ref[...] = accb

    return pl.pallas_call(
        body, name="small_grad_sum",
        out_shape=[jax.ShapeDtypeStruct((1, a.shape[1]), F32), jax.ShapeDtypeStruct((1, b.shape[1]), F32)],
        in_specs=[_VM, _VM], out_specs=[_VM, _VM], compiler_params=_cp(),
    )(a, b)


def _adamw_math(wv, gv, mv, vv):
    m_new = ADAM_B1 * mv + (1.0 - ADAM_B1) * gv
    v_new = ADAM_B2 * vv + (1.0 - ADAM_B2) * (gv * gv)
    m_hat = m_new / (1.0 - ADAM_B1 ** ADAM_STEP)
    v_hat = v_new / (1.0 - ADAM_B2 ** ADAM_STEP)
    return -ADAM_LR * (m_hat / (jnp.sqrt(v_hat) + ADAM_EPS) + ADAM_WD * wv), m_new, v_new


def _adamw_small(ws, gs, ms, vs):
    k = len(ws)

    def body(*refs):
        ins, outs = refs[:4 * k], refs[4 * k:]
        for t in range(k):
            d, m_new, v_new = _adamw_math(ins[t][...], ins[k + t][...], ins[2 * k + t][...], ins[3 * k + t][...])
            outs[t][...] = d
            outs[k + t][...] = m_new
            outs[2 * k + t][...] = v_new

    shapes = [jax.ShapeDtypeStruct(w.shape, F32) for w in ws]
    return pl.pallas_call(
        body, name="adamw_small", out_shape=shapes * 3,
        in_specs=[_VM] * (4 * k), out_specs=[_VM] * (3 * k), compiler_params=_cp(),
    )(*ws, *gs, *ms, *vs)


def _adamw(w, g, m, v, name):
    rows, wd = w.shape
    br = _row_block(rows)

    def body(w_ref, g_ref, m_ref, v_ref, d_ref, nm_ref, nv_ref):
        d, m_new, v_new = _adamw_math(w_ref[...], g_ref[...], m_ref[...], v_ref[...])
        d_ref[...] = d
        nm_ref[...] = m_new
        nv_ref[...] = v_new

    spec = pl.BlockSpec((br, wd), lambda i: (i, 0))
    return pl.pallas_call(
        body, name=name, grid=(rows // br,),
        out_shape=[jax.ShapeDtypeStruct((rows, wd), F32)] * 3,
        in_specs=[spec] * 4, out_specs=[spec] * 3,
        compiler_params=_cp(("parallel",)),
    )(w, g, m, v)


def _adamw_halves(w, mine, other, m, v, cidx, name):
    rows, wd = w.shape
    h = rows // 2
    br = _row_block(h)
    nblk = h // br

    def body(c_ref, w_ref, a_ref, b_ref, m_ref, v_ref, g_ref, d_ref, nm_ref, nv_ref):
        gv = jnp.where(pl.program_id(0) == c_ref[0], a_ref[...], b_ref[...])
        d, m_new, v_new = _adamw_math(w_ref[...], gv, m_ref[...], v_ref[...])
        g_ref[...] = gv
        d_ref[...] = d
        nm_ref[...] = m_new
        nv_ref[...] = v_new

    full = pl.BlockSpec((br, wd), lambda hf, i, c: (hf * nblk + i, 0))
    half = pl.BlockSpec((br, wd), lambda hf, i, c: (i, 0))
    return pl.pallas_call(
        body, name=name,
        grid_spec=pltpu.PrefetchScalarGridSpec(
            num_scalar_prefetch=1, grid=(2, nblk),
            in_specs=[full, half, half, full, full], out_specs=[full] * 4),
        out_shape=[jax.ShapeDtypeStruct((rows, wd), F32)] * 4,
        compiler_params=_cp(("parallel", "parallel")),
    )(cidx, w, mine, other, m, v)


def _other_chips(x, y):
    return [(1 - x, y), (x, 1 - y), (1 - x, 1 - y)]


def _other_devices(x, y, c):
    flip = lambda v, d: (1 - v) if d else v
    return [(flip(x, dx), flip(y, dy), flip(c, dc))
            for dx in (0, 1) for dy in (0, 1) for dc in (0, 1) if (dx, dy, dc) != (0, 0, 0)]


def _exchange(name, ins, out_shapes, n_local, n_remote, plan):
    ni, no = len(ins), len(out_shapes)

    def body(*refs):
        in_refs, out_refs = refs[:ni], refs[ni:ni + no]
        send_sems, recv_sems, local_sems = refs[ni + no:]
        x, y, c = lax.axis_index("x"), lax.axis_index("y"), lax.axis_index("c")
        local, remote = plan(in_refs, out_refs, x, y, c)
        assert len(local) == n_local and len(remote) == n_remote

        def push(k, src, dst, dev):
            return pltpu.make_async_remote_copy(src_ref=src, dst_ref=dst, send_sem=send_sems.at[k],
                                                recv_sem=recv_sems.at[k], device_id=dev, device_id_type=MESH)

        own = [pltpu.make_async_copy(s, d, local_sems.at[i]) for i, (s, d) in enumerate(local)]
        for cp in own:
            cp.start()
        sends = [push(k, s, d, dev) for k, (s, d, dev, _) in enumerate(remote)]
        for cp in sends:
            cp.start()
        for k, (s, _, dev, landing) in enumerate(remote):
            push(k, s, landing, dev).wait_recv()
        for cp in sends:
            cp.wait_send()
        for cp in own:
            cp.wait()

    return pl.pallas_call(
        body, name=name, out_shape=out_shapes,
        in_specs=[_ANY] * ni, out_specs=[_ANY] * no,
        scratch_shapes=[pltpu.SemaphoreType.DMA((n_remote,)), pltpu.SemaphoreType.DMA((n_remote,)),
                        pltpu.SemaphoreType.DMA((max(n_local, 1),))],
        compiler_params=pltpu.CompilerParams(has_side_effects=True),
    )(*ins)


def _gather_chips(name, shards, everyone=()):
    ns, ne = len(shards), len(everyone)
    outs = [jax.ShapeDtypeStruct((4,) + a.shape, a.dtype) for a in shards]
    outs += [jax.ShapeDtypeStruct((8,) + a.shape, a.dtype) for a in everyone]

    def plan(i, o, x, y, c):
        mine, me = 2 * x + y, 4 * x + 2 * y + c
        local, remote = [], []
        for t in range(ns):
            local.append((i[t], o[t].at[mine]))
            for px, py in _other_chips(x, y):
                remote.append((i[t], o[t].at[mine], (px, py, c), o[t].at[2 * px + py]))
        for t in range(ns, ns + ne):
            local.append((i[t], o[t].at[me]))
            for px, py, pc in _other_devices(x, y, c):
                remote.append((i[t], o[t].at[me], (px, py, pc), o[t].at[4 * px + 2 * py + pc]))
        return local, remote

    return _exchange(name, list(shards) + list(everyone), outs, ns + ne, 3 * ns + 7 * ne, plan)


_HBM = pl.BlockSpec(memory_space=pltpu.HBM)
_SEM = pl.BlockSpec(memory_space=pltpu.SEMAPHORE)
_EFFECT = pltpu.SideEffectType.DATAFLOW_SIDE_EFFECTING


def _split_start(name, ins, land_shapes, n_remote, plan, after):
    ni, nl = len(ins), len(land_shapes)
    srcs = [pltpu.with_memory_space_constraint(a, pltpu.HBM) for a in ins]
    lands = [pltpu.with_memory_space_constraint(lax.empty(s.shape, s.dtype), pltpu.HBM) for s in land_shapes]

    def body(*refs):
        src, land = refs[:ni], refs[ni:ni + nl]
        first = ni + nl + 1
        send, recv = refs[first:first + n_remote], refs[first + n_remote:first + 2 * n_remote]
        token = refs[first + 2 * n_remote + ni + nl]
        x, y, c = lax.axis_index("x"), lax.axis_index("y"), lax.axis_index("c")
        remote = plan(src, land, x, y, c)
        assert len(remote) == n_remote
        for k, (s, d, dev, _) in enumerate(remote):
            pltpu.make_async_remote_copy(src_ref=s, dst_ref=d, send_sem=send[k], recv_sem=recv[k],
                                         device_id=dev, device_id_type=MESH).start()
        token[...] = jnp.zeros_like(token)

    out = pl.pallas_call(
        body, name=name + "_start",
        out_shape=[pltpu.SemaphoreType.DMA(())] * (2 * n_remote)
                  + [pltpu.HBM(a.shape, a.dtype) for a in ins] + [pltpu.HBM(s.shape, s.dtype) for s in land_shapes]
                  + [jax.ShapeDtypeStruct((8, 128), F32)],
        in_specs=[_HBM] * (ni + nl) + [_ANY], out_specs=[_SEM] * (2 * n_remote) + [_HBM] * (ni + nl) + [_VM],
        input_output_aliases={t: 2 * n_remote + t for t in range(ni + nl)},
        compiler_params=pltpu.CompilerParams(has_side_effects=_EFFECT),
    )(*srcs, *lands, after)
    sems, thru = out[:2 * n_remote], out[2 * n_remote:2 * n_remote + ni + nl]
    return (name, sems, thru[:ni], thru[ni:], n_remote, plan), out[-1]


def _split_wait(handle, after):
    name, sems, srcs, lands, n_remote, plan = handle
    ni, nl = len(srcs), len(lands)

    def body(*refs):
        src, land = refs[:ni], refs[ni:ni + nl]
        send, recv = refs[ni + nl:ni + nl + n_remote], refs[ni + nl + n_remote:ni + nl + 2 * n_remote]
        x, y, c = lax.axis_index("x"), lax.axis_index("y"), lax.axis_index("c")
        for k, (s, _, dev, landing) in enumerate(plan(src, land, x, y, c)):
            cp = pltpu.make_async_remote_copy(src_ref=s, dst_ref=landing, send_sem=send[k], recv_sem=recv[k],
                                              device_id=dev, device_id_type=MESH)
            cp.wait_send()
            cp.wait_recv()

    out = pl.pallas_call(
        body, name=name + "_wait",
        out_shape=[pltpu.HBM(a.shape, a.dtype) for a in srcs] + [pltpu.HBM(a.shape, a.dtype) for a in lands],
        in_specs=[_HBM] * (ni + nl) + [_SEM] * (2 * n_remote) + [_ANY], out_specs=[_HBM] * (ni + nl),
        input_output_aliases={t: t for t in range(ni + nl)},
        compiler_params=pltpu.CompilerParams(has_side_effects=_EFFECT),
    )(*srcs, *lands, *sems, after)
    return out[:ni], out[ni:]


def _plan_to_chips(src, land, x, y, c):
    mine = 2 * x + y
    return [(src[t], land[t].at[mine], (px, py, c), land[t].at[2 * px + py])
            for t in range(len(src)) for px, py in _other_chips(x, y)]


def _plan_swap_halves(src, land, x, y, c):
    out = []
    for t in range(len(src)):
        h = src[t].shape[1] // 2
        out.append((src[t].at[:, pl.ds(pl.multiple_of((1 - c) * h, 8), h), :], land[t], (x, y, 1 - c), land[t]))
    return out


def _plan_scatter_chips(src, land, x, y, c):
    mine = 2 * x + y
    return [(src[t].at[2 * px + py], land[t].at[mine], (px, py, c), land[t].at[2 * px + py])
            for t in range(len(src)) for px, py in _other_chips(x, y)]


def _swap_halves(gs, everyone):
    ns, ne = len(gs), len(everyone)
    outs = [jax.ShapeDtypeStruct((4, g.shape[1] // 2, g.shape[2]), g.dtype) for g in gs]
    outs += [jax.ShapeDtypeStruct((8,) + a.shape, a.dtype) for a in everyone]

    def plan(i, o, x, y, c):
        me = 4 * x + 2 * y + c
        local, remote = [], []
        for t in range(ns):
            h = gs[t].shape[1] // 2
            theirs = i[t].at[:, pl.ds(pl.multiple_of((1 - c) * h, 8), h), :]
            remote.append((theirs, o[t], (x, y, 1 - c), o[t]))
        for t in range(ns, ns + ne):
            local.append((i[t], o[t].at[me]))
            for px, py, pc in _other_devices(x, y, c):
                remote.append((i[t], o[t].at[me], (px, py, pc), o[t].at[4 * px + 2 * py + pc]))
        return local, remote

    return _exchange("grad_swap_sibling", list(gs) + list(everyone), outs, ne, ns + 7 * ne, plan)


def _scatter_chips(parts):
    ns = len(parts)
    outs = [jax.ShapeDtypeStruct(a.shape, a.dtype) for a in parts]

    def plan(i, o, x, y, c):
        mine = 2 * x + y
        local, remote = [], []
        for t in range(ns):
            local.append((i[t].at[mine], o[t].at[mine]))
            for px, py in _other_chips(x, y):
                remote.append((i[t].at[2 * px + py], o[t].at[mine], (px, py, c), o[t].at[2 * px + py]))
        return local, remote

    return _exchange("grad_scatter_chips", list(parts), outs, ns, 3 * ns, plan)


def _join_halves(halves):
    ns = len(halves)
    outs = [jax.ShapeDtypeStruct(a.shape, a.dtype) for a in halves]

    def plan(i, o, x, y, c):
        return [], [(i[t], o[t], (x, y, 1 - c), o[t]) for t in range(ns)]

    return _exchange("grad_join_sibling", list(halves), outs, 0, ns, plan)


def _pad_heads_cols(w, per, used):
    k = w.shape[0]
    w = w.reshape(k, NH, per)[:, :, :used]
    return jnp.pad(w, ((0, 0), (0, 0), (0, HP - used))).reshape(k, NH * HP)


def _unpad_heads_cols(w, used):
    k = w.shape[0]
    return w.reshape(k, NH, HP)[:, :, :used]


def _prep_weights(wf):
    bf = lambda a: a.astype(BF16)
    out = {}
    out["w_in"] = jnp.pad(bf(wf["w_in"]), ((0, 0), (0, IN_PAD - IN_COLS)))
    out["w_glu"] = bf(wf["w_glu"])
    out["w_uq"] = _pad_heads_cols(bf(wf["w_uq"]), QK_NOPE + QK_ROPE, QK_NOPE + QK_ROPE)
    wkv = bf(wf["w_ukv"]).reshape(KV_LORA, NH, QK_NOPE + V_HEAD)
    wk = jnp.pad(wkv[:, :, :QK_NOPE], ((0, 0), (0, 0), (0, HP - QK_NOPE))).reshape(KV_LORA, NH * HP)
    wv = jnp.pad(wkv[:, :, QK_NOPE:], ((0, 0), (0, 0), (0, HP - V_HEAD))).reshape(KV_LORA, NH * HP)
    out["w_ukv"] = jnp.concatenate([wk, wv], axis=1)
    return out


def _prep_late_weights(wf):
    bf = lambda a: a.astype(BF16)
    out = {}
    wo = bf(wf["w_out"])
    wo_a = jnp.pad(wo[D_SSM:].reshape(NH, V_HEAD, D), ((0, 0), (0, HP - V_HEAD), (0, 0))).reshape(NH * HP, D)
    out["w_out"] = jnp.concatenate([wo[:D_SSM], wo_a], axis=0)
    out["w_ff1"] = bf(wf["w_ff1"])
    out["w_ff2"] = bf(wf["w_ff2"])
    return out


def _rope_tables(positions):
    inv_freq = ROPE_BASE ** (-jnp.arange(0, QK_ROPE, 2, dtype=F32) / QK_ROPE)
    ang = positions.astype(F32)[:, None] * inv_freq
    cos, sin = jnp.cos(ang), jnp.sin(ang)
    n = positions.shape[0]
    one = jnp.ones((n, QK_NOPE), F32)
    z16 = jnp.zeros((n, 16), F32)
    z32 = jnp.zeros((n, 32), F32)
    z64 = jnp.zeros((n, QK_NOPE), F32)
    rc = jnp.concatenate([one, cos, cos, z32], axis=1)
    rs1 = jnp.concatenate([z64, -sin, z16, z32], axis=1)
    rs2 = jnp.concatenate([z64, z16, sin, z32], axis=1)
    return rc, rs1, rs2


def _permute_rows(a, S):
    n, w = a.shape
    return a.reshape(n // S, 8, S // 8, w).transpose(0, 2, 1, 3).reshape(n, w)


def _unpermute_rows(a, S):
    n, w = a.shape
    return a.reshape(n // S, S // 8, 8, w).transpose(0, 2, 1, 3).reshape(n, w)


def _block_diag_in(bb):
    eye = jnp.eye(8, dtype=bb.dtype)
    blocks = jnp.einsum("qgph,gk->qghkp", bb.reshape(4, 8, P, H), eye).reshape(4, QB, QS)
    return blocks.transpose(1, 0, 2).reshape(QB, NST)


def _block_diag_out(cc):
    eye = jnp.eye(8, dtype=cc.dtype)
    return jnp.einsum("qghp,gk->qgpkh", cc.reshape(4, 8, H, P), eye).reshape(NST, QB)


def _slots(full):
    r, cdim = full.shape
    return full.reshape(r, 4, cdim // 4).transpose(1, 0, 2)


def _unslots(g):
    s, r, cs = g.shape
    return g.transpose(1, 0, 2).reshape(r, s * cs)


def _local_step(x, positions, target, modp, wf, late_weights=None, reducer=None):
    nb, S, _ = x.shape
    n = nb * S
    tm = min(256, S)
    tt = min(256, S)
    tq = min(512, S // 2)
    kw = _prep_weights(wf)
    row = lambda a: a.reshape(1, -1).astype(F32)

    xf = x.reshape(n, D)
    tf = target.reshape(n, D)
    g1, g2, gf = row(wf["norm1_g"]), row(wf["norm2_g"]), row(wf["final_norm_g"])
    h1, proj = _f1_fwd(xf, modp, g1, kw["w_in"], S, tm)

    col = lambda a: a.reshape(NST, 1)
    lam_re, lam_im = col(wf["ssm_lambda_re"]), col(wf["ssm_lambda_im"])
    logdt = jnp.repeat(wf["ssm_log_dt"].reshape(G, 1), P, axis=1).reshape(NST, 1)
    b_re, b_im = wf["ssm_b_re"].reshape(NST, H), wf["ssm_b_im"].reshape(NST, H)
    lbr, lbi, bbr, bbi = _ssm_param_fwd(lam_re, lam_im, logdt, b_re, b_im)
    lre8 = jnp.broadcast_to(lbr.reshape(1, NST), (8, NST))
    lim8 = jnp.broadcast_to(lbi.reshape(1, NST), (8, NST))
    bm = jnp.concatenate([_block_diag_in(bbr.reshape(G, P, H)), _block_diag_in(bbi.reshape(G, P, H))],
                         axis=1).astype(BF16)
    cm = jnp.concatenate([_block_diag_out(wf["ssm_c_re"]), -_block_diag_out(wf["ssm_c_im"])], axis=0).astype(BF16)
    dvec = row(wf["ssm_d"])
    u_p = _permute_rows(proj[:, :D_SSM], S)
    fcr, fci = _ssm_local(u_p, bm, lre8, lim8, S, tt)
    st, ypre, z, gact, yssm_p = _ssm_fwd(u_p, fcr, fci, bm, cm, dvec, kw["w_glu"], lre8, lim8, S, tt)
    yssm = _unpermute_rows(yssm_p, S)

    rc, rs1, rs2 = _rope_tables(positions.reshape(n))
    gq, gkv = row(wf["q_norm_g"]), row(wf["kv_norm_g"])
    q, k, v, qn, kvn = _mla_fwd(proj, rc, rs1, rs2, gq, gkv, kw["w_uq"], kw["w_ukv"], tm)
    oattn, lrow = _attn_fwd(q, k, v, S, tq)

    gs = row(wf["ssm_out_g"])
    ga = jnp.pad(wf["attn_out_g"].reshape(NH, V_HEAD), ((0, 0), (0, HP - V_HEAD))).reshape(1, NH * HP)
    kw.update(_prep_late_weights(late_weights(oattn) if late_weights is not None else wf))
    yn, o, x1, h2 = _p1_fwd(yssm, oattn, xf, modp, gs, ga, kw["w_out"], g2, S, tm)
    dx1, r, da, dff, accs2, accg2 = _p2(x1, h2, tf, modp, g2, gf, kw["w_ff1"], kw["w_ff2"], S, tm)
    loss = accg2[2:3]
    g_ff1 = _wgrad(h2, da, "wgrad_ff1", col_slots=4)
    g_ff2 = _wgrad(r, dff, "wgrad_ff2").reshape(4, D_FF // 4, D)
    gs_b, gq_b = gs, gq
    if reducer is not None:
        gs_b = gs + reducer.start([g_ff1, g_ff2])[0, 0]
    do, dyssm, dob, drow, accs3, accg3 = _p3_bwd(dx1, o, yssm, oattn, modp, gs_b, ga, kw["w_out"], S, tm)

    dq, dk, dv = _attn_bwd(q, k, v, dob, lrow, drow, S, tq)
    if reducer is not None:
        gq_b = gq + reducer.middle(dq)[0, 0]
    dmla, dqb, dkvb, accm = _mla_bwd(dq, dk, dv, proj, rc, rs1, rs2, gq_b, gkv, kw["w_uq"], kw["w_ukv"], tm)

    dys_p = _permute_rows(dyssm, S)
    dy, dz, air, aii = _ssm_bwd_a(dys_p, z, ypre, kw["w_glu"], cm, lre8, lim8, S, tt)
    du_p, dcm, dbm, dd, dlr, dli = _ssm_bwd_b(dy, u_p, st, fcr, fci, air, aii, bm, cm, dvec, lre8, lim8, S, tt)
    du = _unpermute_rows(du_p, S)
    dcm = dcm.reshape(2, 4, 8, P, 8, H)
    dc_re = jnp.einsum("qgpgh->qghp", dcm[0]).reshape(G, H, P)
    dc_im = -jnp.einsum("qgpgh->qghp", dcm[1]).reshape(G, H, P)
    dbm = dbm.reshape(8, H, 2, 4, 8, P)
    dbb_re = jnp.einsum("ghqgp->qgph", dbm[:, :, 0]).reshape(NST, H)
    dbb_im = jnp.einsum("ghqgp->qgph", dbm[:, :, 1]).reshape(NST, H)
    gb_re, gb_im, glr, gli, gdt = _ssm_param_bwd(lam_re, lam_im, logdt, b_re, b_im, dlr.reshape(NST, 1),
                                                 dli.reshape(NST, 1), dbb_re, dbb_im)
    glogdt = _rowsum(gdt.reshape(G, P))

    dx, dproj, accs1, accg1 = _f1_bwd(du, dmla, dx1, xf, modp, g1, kw["w_in"], S, tm)

    big = {}
    big["w_in"] = _slots(_wgrad(h1, dproj, "wgrad_in")[:, :IN_COLS])
    big["w_glu"] = _wgrad(gact, dz, "wgrad_glu", col_slots=4)
    big["w_uq"] = _slots(_unpad_heads_cols(_wgrad(qn, dqb, "wgrad_uq"), QK_NOPE + QK_ROPE).reshape(Q_LORA, -1))
    gkvw = _wgrad(kvn, dkvb, "wgrad_ukv")
    big["w_ukv"] = _slots(jnp.concatenate([_unpad_heads_cols(gkvw[:, :NH * HP], QK_NOPE),
                                           _unpad_heads_cols(gkvw[:, NH * HP:], V_HEAD)], axis=2).reshape(KV_LORA, -1))
    gwo = _wgrad(yn, do, "wgrad_out")
    big["w_out"] = jnp.concatenate([gwo[:D_SSM].reshape(2, D_SSM // 2, D),
                                    gwo[D_SSM:].reshape(2, NH // 2 * HP, D).reshape(2, NH // 2, HP, D)[:, :, :V_HEAD]
                                    .reshape(2, D_ATTN // 2, D)], axis=0)
    big["w_ff1"] = g_ff1
    big["w_ff2"] = g_ff2

    small = {}
    small["norm1_g"] = accg1[0:1]
    small["norm2_g"] = accg2[0:1]
    small["final_norm_g"] = accg2[1:2]
    small["ssm_out_g"] = accg3[0:1, :D_SSM]
    small["attn_out_g"] = accg3[1].reshape(NH, HP)[:, :V_HEAD].reshape(1, D_ATTN)
    small["q_norm_g"] = accm[0:1, :Q_LORA]
    small["kv_norm_g"] = accm[1:2, :KV_LORA]
    small["ssm_lambda_re"] = glr.reshape(G, P)
    small["ssm_lambda_im"] = gli.reshape(G, P)
    small["ssm_b_re"] = gb_re
    small["ssm_b_im"] = gb_im
    small["ssm_c_re"] = dc_re.reshape(G * H, P)
    small["ssm_c_im"] = dc_im.reshape(G * H, P)
    small["ssm_d"] = dd.reshape(G, H)
    small["ssm_log_dt"] = glogdt.reshape(1, G)
    return loss, dx.reshape(nb, S, D), big, small, accs1 + accs2 + accs3


def _view2d(a):
    return a.reshape(-1, a.shape[-1]) if a.ndim > 1 else a.reshape(1, -1)


def kernel(x, c, positions, ada_w, ada_b, norm1_g, w_in, ssm_lambda_re, ssm_lambda_im, ssm_b_re, ssm_b_im, ssm_c_re, ssm_c_im, ssm_d, ssm_log_dt, w_glu, q_norm_g, w_uq, kv_norm_g, w_ukv, ssm_out_g, attn_out_g, w_out, norm2_g, w_ff1, w_ff2, final_ada_w, final_ada_b, final_norm_g, loss_target, m_ada_w, m_ada_b, m_norm1_g, m_w_in, m_ssm_lambda_re, m_ssm_lambda_im, m_ssm_b_re, m_ssm_b_im, m_ssm_c_re, m_ssm_c_im, m_ssm_d, m_ssm_log_dt, m_w_glu, m_q_norm_g, m_w_uq, m_kv_norm_g, m_w_ukv, m_ssm_out_g, m_attn_out_g, m_w_out, m_norm2_g, m_w_ff1, m_w_ff2, m_final_ada_w, m_final_ada_b, m_final_norm_g, v_ada_w, v_ada_b, v_norm1_g, v_w_in, v_ssm_lambda_re, v_ssm_lambda_im, v_ssm_b_re, v_ssm_b_im, v_ssm_c_re, v_ssm_c_im, v_ssm_d, v_ssm_log_dt, v_w_glu, v_q_norm_g, v_w_uq, v_kv_norm_g, v_w_ukv, v_ssm_out_g, v_attn_out_g, v_w_out, v_norm2_g, v_w_ff1, v_w_ff2, v_final_ada_w, v_final_ada_b, v_final_norm_g):
    args = dict(locals())
    names = list(inspect.signature(kernel).parameters)
    wnames = names[3:names.index("loss_target")]
    small_names = [nm for nm in wnames if nm not in GATHERED and nm not in TP]
    reduced_names = [nm for nm in small_names if nm not in ("ada_b", "final_ada_b")]
    w = {nm: args[nm] for nm in wnames}
    m = {nm: args["m_" + nm] for nm in wnames}
    v = {nm: args["v_" + nm] for nm in wnames}
    nb = x.shape[0]
    xi, yi, ci = lax.axis_index("x"), lax.axis_index("y"), lax.axis_index("c")
    chip, me = 2 * xi + yi, 4 * xi + 2 * yi + ci

    unslot = lambda nm, g: g.reshape(-1, g.shape[-1]) if nm in ROW_SHARDED else _unslots(g)
    early = [nm for nm in GATHERED if nm not in LATE]
    got = _gather_chips("gather_weights", [_view2d(w[nm]).astype(BF16) for nm in early], [c])
    wf = {nm: unslot(nm, g) for nm, g in zip(early, got)}
    for nm in small_names:
        wf[nm] = w[nm][0] if w[nm].ndim > 1 else w[nm]
    c_all = got[len(early)].reshape(8 * nb, D)

    na, nf = ada_w.shape[-1], final_ada_w.shape[-1]
    ada_b_s = lax.dynamic_slice(ada_b, (0, chip * na), (1, na))
    fada_b_s = lax.dynamic_slice(final_ada_b.reshape(1, -1), (0, chip * nf), (1, nf))
    cond_all, modcols = _mod_fwd(c_all, ada_w[0], ada_b_s, final_ada_w, fada_b_s)
    (mod_g,) = _gather_chips("gather_mod", [modcols])
    mine = lax.dynamic_slice(mod_g, (0, me * nb, 0), (4, nb, na + nf))
    modp = jnp.concatenate([mine[:, :, :na].transpose(1, 0, 2).reshape(nb, 6, D),
                            mine[:, :, na:].transpose(1, 0, 2).reshape(nb, 2, D)], axis=1)

    own_late = [_view2d(w[nm]).astype(BF16) for nm in LATE]
    late_gather, token = _split_start("gather_late", own_late,
                                      [jax.ShapeDtypeStruct((4,) + a.shape, a.dtype) for a in own_late],
                                      3 * len(LATE), _plan_to_chips, modp)
    modp = modp + token[0, 0]

    def late_weights(after):
        sent, landed = _split_wait(late_gather, after)
        return {nm: unslot(nm, lax.dynamic_update_slice(g, own[None], (chip, 0, 0)))
                for nm, g, own in zip(LATE, landed, sent)}

    cidx = ci.astype(jnp.int32).reshape(1)
    ahead = ["w_ff1", "w_ff2"]

    class Reducer:
        def start(self, gs):
            lands = [jax.ShapeDtypeStruct((4, g.shape[1] // 2, g.shape[2]), g.dtype) for g in gs]
            self.swap, tok = _split_start("grad_swap_ff", gs, lands, len(gs), _plan_swap_halves, modp)
            return tok

        def middle(self, after):
            gs, got = _split_wait(self.swap, after)
            sums = [_add_half(g, r, cidx, "grad_add_sibling_" + nm) for nm, g, r in zip(ahead, gs, got)]
            lands = [jax.ShapeDtypeStruct(s.shape, s.dtype) for s in sums]
            self.scatter, tok = _split_start("grad_scatter_ff", sums, lands, 3 * len(sums), _plan_scatter_chips, modp)
            return tok

        def finish(self, after):
            out = []
            for nm, s, l in zip(ahead, *_split_wait(self.scatter, after)):
                own = lax.dynamic_slice(s, (chip, 0, 0), (1,) + s.shape[1:])
                out.append(_add_chips(lax.dynamic_update_slice(l, own, (chip, 0, 0)), "grad_add_chips_" + nm))
            return out

    reducer = Reducer()
    loss_row, grad_x, big, small, dmodp = _local_step(x, positions, loss_target, modp, wf, late_weights, reducer)

    rest = [nm for nm in GATHERED if nm not in ahead]
    sizes = [small[nm].size for nm in reduced_names]
    pad = -sum(sizes) % 128
    packed = jnp.concatenate([small[nm].reshape(1, -1) for nm in reduced_names] + [jnp.zeros((1, pad), F32)],
                             axis=1).astype(BF16)
    swapped = _swap_halves([big[nm] for nm in rest], [dmodp.reshape(nb, 8 * D), packed, loss_row])
    chip_sums = [_add_half(big[nm], r, cidx, "grad_add_sibling_" + nm) for nm, r in zip(rest, swapped)]
    half_of = {nm: _add_chips(r, "grad_add_chips_" + nm) for nm, r in zip(rest, _scatter_chips(chip_sums))}
    half_of.update(zip(ahead, reducer.finish(grad_x)))
    halves = [half_of[nm] for nm in GATHERED]
    others = _join_halves(halves)
    grads = {}
    dmod_all = swapped[len(rest)].reshape(8 * nb, 8 * D)
    small_sum, loss_sum = _sum_devices(swapped[len(rest) + 1].reshape(8, -1), swapped[len(rest) + 2].reshape(8, -1))
    loss = jnp.sum(loss_sum)
    off = 0
    for nm, sz in zip(reduced_names, sizes):
        grads[nm] = small_sum[:, off:off + sz].reshape(small[nm].shape)
        off += sz

    dsl = jnp.concatenate([lax.dynamic_slice(dmod_all, (0, chip * na), (8 * nb, na)),
                           lax.dynamic_slice(dmod_all, (0, 6 * D + chip * nf), (8 * nb, nf))], axis=1)
    gw, gb = _mod_bwd(cond_all.T, dsl, dmod_all)
    grads["ada_w"], grads["final_ada_w"] = gw[:, :na], gw[:, na:]
    grads["ada_b"], grads["final_ada_b"] = gb[:, :6 * D], gb[:, 6 * D:]

    delta, new_m, new_v = {}, {}, {}
    for nm, mine_h, other_h in zip(GATHERED, halves, others):
        grads[nm], delta[nm], new_m[nm], new_v[nm] = _adamw_halves(
            _view2d(w[nm]), mine_h, other_h, _view2d(m[nm]), _view2d(v[nm]), cidx, "adamw_" + nm)
    for nm in TP:
        delta[nm], new_m[nm], new_v[nm] = _adamw(_view2d(w[nm]), grads[nm], _view2d(m[nm]), _view2d(v[nm]),
                                                  "adamw_" + nm)
    upd = _adamw_small([_view2d(w[nm]) for nm in small_names], [grads[nm] for nm in small_names],
                       [_view2d(m[nm]) for nm in small_names], [_view2d(v[nm]) for nm in small_names])
    k = len(small_names)
    for t, nm in enumerate(small_names):
        delta[nm], new_m[nm], new_v[nm] = upd[t], upd[k + t], upd[2 * k + t]

    outs = [grads, delta, new_m, new_v]
    return (loss, grad_x, *[d[nm].reshape(w[nm].shape) for d in outs for nm in wnames])
```

```python
import inspect
import math

import jax
import jax.numpy as jnp
from jax import lax
from jax.experimental import pallas as pl
from jax.experimental.pallas import tpu as pltpu

F32 = jnp.float32
BF16 = jnp.bfloat16

D = 1024
D_SSM = 512
G = 32
H = 16
P = 64
NST = G * P
D_ATTN = 512
NH = 8
QK_NOPE = 64
QK_ROPE = 32
V_HEAD = 64
HP = 128
Q_LORA = 384
KV_LORA = 256
IN_COLS = D_SSM + Q_LORA + KV_LORA + QK_ROPE
IN_PAD = 1280
D_FF = 4096
ROPE_BASE = 10000.0
EPS = 1e-6
ADAM_LR = 0.001
ADAM_B1 = 0.9
ADAM_B2 = 0.999
ADAM_EPS = 1e-08
ADAM_WD = 0.01
ADAM_STEP = 10
NEG = -1e30
VMEM_LIMIT = 60 << 20

MESH = pl.DeviceIdType.MESH
_VM = pl.BlockSpec(memory_space=pltpu.VMEM)
_ANY = pl.BlockSpec(memory_space=pl.ANY)

GATHERED = ["w_in", "w_glu", "w_uq", "w_ukv", "w_out", "w_ff1", "w_ff2"]
TP = ["ada_w", "final_ada_w"]
ROW_SHARDED = ("w_out", "w_ff2")
LATE = ["w_out", "w_ff1", "w_ff2"]


def _cp(sem=None, vmem=VMEM_LIMIT):
    kw = dict(vmem_limit_bytes=vmem)
    if sem is not None:
        kw["dimension_semantics"] = sem
    return pltpu.CompilerParams(**kw)


def _dot(a, b):
    return jnp.dot(a, b, preferred_element_type=F32)


def _dot_nt(a, b):
    return lax.dot_general(a, b, (((1,), (1,)), ((), ())), preferred_element_type=F32)


def _dot_tn(a, b):
    return lax.dot_general(a, b, (((0,), (0,)), ((), ())), preferred_element_type=F32)


def _rms(x, n):
    r = lax.rsqrt(jnp.sum(x * x, axis=-1, keepdims=True) * (1.0 / n) + EPS)
    return x * r, r


def _rms_bwd(dyg, xhat, r, n):
    return r * (dyg - xhat * (jnp.sum(dyg * xhat, axis=-1, keepdims=True) * (1.0 / n)))


def _sigmoid(x):
    return 1.0 / (1.0 + jnp.exp(-x))


_GK = math.sqrt(2.0 / math.pi)
_GC = 0.044715


def _gelu(y):
    t = jnp.tanh(_GK * (y + _GC * y * y * y))
    return 0.5 * y * (1.0 + t)


def _gelu_grad(y):
    t = jnp.tanh(_GK * (y + _GC * y * y * y))
    return 0.5 * (1.0 + t) + 0.5 * y * (1.0 - t * t) * _GK * (1.0 + 3.0 * _GC * y * y)


def _colsum(x):
    return jnp.sum(x, axis=0, keepdims=True)


def _roll(x, s):
    return pltpu.roll(x, s % x.shape[-1], x.ndim - 1)


def _mod_fwd(c_all, ada_w_s, ada_b_s, fada_w_s, fada_b_s):
    nseq = c_all.shape[0]
    na, nf = ada_w_s.shape[1], fada_w_s.shape[1]

    def body(c_ref, w_ref, b_ref, fw_ref, fb_ref, cond_ref, mod_ref):
        cv = c_ref[...]
        cond = cv * _sigmoid(cv)
        cond_ref[...] = cond
        cb = cond.astype(BF16)
        mod_ref[:, 0:na] = _dot(cb, w_ref[...].astype(BF16)) + b_ref[...]
        mod_ref[:, na:na + nf] = _dot(cb, fw_ref[...].astype(BF16)) + fb_ref[...]

    return pl.pallas_call(
        body, name="mod_fwd",
        out_shape=[jax.ShapeDtypeStruct((nseq, D), F32), jax.ShapeDtypeStruct((nseq, na + nf), F32)],
        in_specs=[_VM] * 5, out_specs=[_VM] * 2, compiler_params=_cp(),
    )(c_all, ada_w_s, ada_b_s, fada_w_s, fada_b_s)


def _mod_bwd(cond_t, dsl, dall):
    nseq, n = dsl.shape
    bc = 512

    def body(ct_ref, dm_ref, da_ref, gw_ref, gb_ref):
        ct = ct_ref[...]
        dm = dm_ref[...]
        acc = ct[:, 0:1] * dm[0:1, :]
        for b in range(1, nseq):
            acc = acc + ct[:, b:b + 1] * dm[b:b + 1, :]
        gw_ref[...] = acc

        @pl.when(pl.program_id(0) == 0)
        def _():
            gb_ref[...] = _colsum(da_ref[...])

    return pl.pallas_call(
        body, name="mod_bwd", grid=(n // bc,),
        out_shape=[jax.ShapeDtypeStruct((D, n), F32), jax.ShapeDtypeStruct((1, dall.shape[1]), F32)],
        in_specs=[_VM, pl.BlockSpec((nseq, bc), lambda i: (0, i)), _VM],
        out_specs=[pl.BlockSpec((D, bc), lambda i: (0, i)), pl.BlockSpec((1, dall.shape[1]), lambda i: (0, 0))],
        compiler_params=_cp(("arbitrary",)),
    )(cond_t, dsl, dall)


def _f1_fwd(x, modp, g1, w_in, S, tm):
    n = x.shape[0]
    tps = S // tm

    def body(x_ref, mod_ref, g_ref, w_ref, h_ref, proj_ref):
        xhat, _ = _rms(x_ref[...], D)
        h = (xhat * g_ref[...]) * (1.0 + mod_ref[0, 1:2, :]) + mod_ref[0, 0:1, :]
        hb = h.astype(BF16)
        h_ref[...] = hb
        proj_ref[...] = _dot(hb, w_ref[...])

    return pl.pallas_call(
        body, name="f1_fwd", grid=(n // tm,),
        out_shape=[jax.ShapeDtypeStruct((n, D), BF16), jax.ShapeDtypeStruct((n, IN_PAD), F32)],
        in_specs=[pl.BlockSpec((tm, D), lambda i: (i, 0)),
                  pl.BlockSpec((1, 8, D), lambda i: (i // tps, 0, 0)), _VM, _VM],
        out_specs=[pl.BlockSpec((tm, D), lambda i: (i, 0)), pl.BlockSpec((tm, IN_PAD), lambda i: (i, 0))],
        compiler_params=_cp(("parallel",)),
    )(x, modp, g1, w_in)


def _f1_bwd(du, dmla, dx1, x, modp, g1, w_in, S, tm):
    n = x.shape[0]
    tps = S // tm
    nb = n // S

    def body(du_ref, dm_ref, dx1_ref, x_ref, mod_ref, g_ref, w_ref, dx_ref, dproj_ref, accs_ref, accg_ref):
        i = pl.program_id(0)
        dproj = jnp.concatenate([du_ref[...], dm_ref[...]], axis=1).astype(BF16)
        dproj_ref[...] = dproj
        dh = _dot_nt(dproj, w_ref[...])
        xhat, r = _rms(x_ref[...], D)
        g = g_ref[...]
        dn = dh * (1.0 + mod_ref[0, 1:2, :])
        dx_ref[...] = dx1_ref[...] + _rms_bwd(dn * g, xhat, r, D)

        @pl.when(i % tps == 0)
        def _():
            accs_ref[...] = jnp.zeros_like(accs_ref)

        @pl.when(i == 0)
        def _():
            accg_ref[...] = jnp.zeros_like(accg_ref)

        accs_ref[0, 0:1, :] += _colsum(dh)
        accs_ref[0, 1:2, :] += _colsum(dh * (xhat * g))
        accg_ref[0:1, :] += _colsum(dn * xhat)

    return pl.pallas_call(
        body, name="f1_bwd", grid=(n // tm,),
        out_shape=[jax.ShapeDtypeStruct((n, D), F32), jax.ShapeDtypeStruct((n, IN_PAD), BF16),
                   jax.ShapeDtypeStruct((nb, 8, D), F32), jax.ShapeDtypeStruct((8, D), F32)],
        in_specs=[pl.BlockSpec((tm, D_SSM), lambda i: (i, 0)), pl.BlockSpec((tm, IN_PAD - D_SSM), lambda i: (i, 0)),
                  pl.BlockSpec((tm, D), lambda i: (i, 0)), pl.BlockSpec((tm, D), lambda i: (i, 0)),
                  pl.BlockSpec((1, 8, D), lambda i: (i // tps, 0, 0)), _VM, _VM],
        out_specs=[pl.BlockSpec((tm, D), lambda i: (i, 0)), pl.BlockSpec((tm, IN_PAD), lambda i: (i, 0)),
                   pl.BlockSpec((1, 8, D), lambda i: (i // tps, 0, 0)), pl.BlockSpec((8, D), lambda i: (0, 0))],
        compiler_params=_cp(("arbitrary",)),
    )(du, dmla, dx1, x, modp, g1, w_in)


def _ssm_param_fwd(lam_re, lam_im, logdt, b_re, b_im):
    def body(lr_ref, li_ref, ld_ref, br_ref, bi_ref, lbr_ref, lbi_ref, bbr_ref, bbi_ref):
        lr, li = lr_ref[...], li_ref[...]
        dt = jnp.exp(ld_ref[...])
        er = jnp.exp(lr * dt)
        lbr = er * jnp.cos(li * dt)
        lbi = er * jnp.sin(li * dt)
        den = 1.0 / (lr * lr + li * li)
        cr = ((lbr - 1.0) * lr + lbi * li) * den
        ci = (lbi * lr - (lbr - 1.0) * li) * den
        lbr_ref[...] = lbr
        lbi_ref[...] = lbi
        bbr_ref[...] = cr * br_ref[...] - ci * bi_ref[...]
        bbi_ref[...] = cr * bi_ref[...] + ci * br_ref[...]

    return pl.pallas_call(
        body, name="ssm_param_fwd",
        out_shape=[jax.ShapeDtypeStruct((NST, 1), F32)] * 2 + [jax.ShapeDtypeStruct((NST, H), F32)] * 2,
        in_specs=[_VM] * 5, out_specs=[_VM] * 4, compiler_params=_cp(),
    )(lam_re, lam_im, logdt, b_re, b_im)


def _ssm_param_bwd(lam_re, lam_im, logdt, b_re, b_im, dlb_re, dlb_im, dbb_re, dbb_im):
    def body(lr_ref, li_ref, ld_ref, br_ref, bi_ref, dlr_ref, dli_ref, dbr_ref, dbi_ref,
             gbr_ref, gbi_ref, glr_ref, gli_ref, gdt_ref):
        lr, li = lr_ref[...], li_ref[...]
        dt = jnp.exp(ld_ref[...])
        er = jnp.exp(lr * dt)
        lbr = er * jnp.cos(li * dt)
        lbi = er * jnp.sin(li * dt)
        den = 1.0 / (lr * lr + li * li)
        nr, ni = lbr - 1.0, lbi
        cr = (nr * lr + ni * li) * den
        ci = (ni * lr - nr * li) * den
        br, bi = br_ref[...], bi_ref[...]
        dbr, dbi = dbr_ref[...], dbi_ref[...]
        gbr_ref[...] = cr * dbr + ci * dbi
        gbi_ref[...] = cr * dbi - ci * dbr
        gcr = jnp.sum(dbr * br + dbi * bi, axis=1, keepdims=True)
        gci = jnp.sum(dbi * br - dbr * bi, axis=1, keepdims=True)
        ilr, ili = lr * den, -li * den
        glbr = dlr_ref[...] + (gcr * ilr + gci * ili)
        glbi = dli_ref[...] + (gci * ilr - gcr * ili)
        qr = -(cr * ilr - ci * ili)
        qi = -(cr * ili + ci * ilr)
        glr = gcr * qr + gci * qi
        gli = gci * qr - gcr * qi
        glr = glr + dt * (glbr * lbr + glbi * lbi)
        gli = gli + dt * (glbi * lbr - glbr * lbi)
        wr = lr * lbr - li * lbi
        wi = lr * lbi + li * lbr
        glr_ref[...] = glr
        gli_ref[...] = gli
        gdt_ref[...] = (glbr * wr + glbi * wi) * dt

    return pl.pallas_call(
        body, name="ssm_param_bwd",
        out_shape=[jax.ShapeDtypeStruct((NST, H), F32)] * 2 + [jax.ShapeDtypeStruct((NST, 1), F32)] * 3,
        in_specs=[_VM] * 9, out_specs=[_VM] * 5, compiler_params=_cp(),
    )(lam_re, lam_im, logdt, b_re, b_im, dlb_re, dlb_im, dbb_re, dbb_im)


def _rowsum(a):
    def body(a_ref, o_ref):
        o_ref[...] = jnp.sum(a_ref[...], axis=1, keepdims=True)

    return pl.pallas_call(
        body, name="rowsum", out_shape=jax.ShapeDtypeStruct((a.shape[0], 1), F32),
        in_specs=[_VM], out_specs=_VM, compiler_params=_cp(),
    )(a)


QB = D_SSM // 4
QS = 4 * QB


def _bd_lo(part, q):
    return part * NST + q * QS


def _bd_expand(ub, bm_ref, out_ref):
    for part in range(2):
        for q in range(4):
            lo = _bd_lo(part, q)
            out_ref[:, lo:lo + QS] = _dot(ub[:, q * QB:(q + 1) * QB], bm_ref[:, lo:lo + QS])


def _bd_expand_t(db, cm_ref, out_ref):
    for part in range(2):
        for q in range(4):
            lo = _bd_lo(part, q)
            out_ref[:, lo:lo + QS] = _dot_nt(db[:, q * QB:(q + 1) * QB], cm_ref[lo:lo + QS, :])


def _bd_project(sb, cm_ref):
    return jnp.concatenate(
        [_dot(sb[:, _bd_lo(0, q):_bd_lo(0, q) + QS], cm_ref[_bd_lo(0, q):_bd_lo(0, q) + QS, :])
         + _dot(sb[:, _bd_lo(1, q):_bd_lo(1, q) + QS], cm_ref[_bd_lo(1, q):_bd_lo(1, q) + QS, :])
         for q in range(4)], axis=1)


def _bd_project_t(ab, bm_ref):
    return jnp.concatenate(
        [_dot_nt(ab[:, _bd_lo(0, q):_bd_lo(0, q) + QS], bm_ref[:, _bd_lo(0, q):_bd_lo(0, q) + QS])
         + _dot_nt(ab[:, _bd_lo(1, q):_bd_lo(1, q) + QS], bm_ref[:, _bd_lo(1, q):_bd_lo(1, q) + QS])
         for q in range(4)], axis=1)


def _pow2k(pr, pi, nsq):
    for _ in range(nsq):
        pr, pi = pr * pr - pi * pi, 2.0 * pr * pi
    return pr, pi


def _ssm_local(u_p, bm, lre8, lim8, S, tt):
    n = u_p.shape[0]
    nb, nt = n // S, S // tt
    nsq = int(round(math.log2(S // 8)))
    assert 2 ** nsq == S // 8

    def body(u_ref, bm_ref, lre_ref, lim_ref, cre_ref, cim_ref, sre, sim, bu):
        j = pl.program_id(1)

        @pl.when(j == 0)
        def _():
            sre[...] = jnp.zeros_like(sre)
            sim[...] = jnp.zeros_like(sim)

        _bd_expand(u_ref[...].astype(BF16), bm_ref, bu)
        lre, lim = lre_ref[...], lim_ref[...]

        def step(i, c):
            sr, si = c
            off = pl.multiple_of(i * 8, 8)
            br = bu[pl.ds(off, 8), 0:NST]
            bi = bu[pl.ds(off, 8), NST:2 * NST]
            return lre * sr - lim * si + br, lre * si + lim * sr + bi

        sr, si = lax.fori_loop(0, tt // 8, step, (sre[...], sim[...]))
        sre[...] = sr
        sim[...] = si

        @pl.when(j == nt - 1)
        def _():
            pr, pi = _pow2k(lre[0:1], lim[0:1], nsq)
            cr = jnp.zeros((1, NST), F32)
            ci = jnp.zeros((1, NST), F32)
            cre_ref[0:1, :] = cr
            cim_ref[0:1, :] = ci
            for k in range(1, 8):
                cr, ci = sr[k - 1:k] + pr * cr - pi * ci, si[k - 1:k] + pr * ci + pi * cr
                cre_ref[k:k + 1, :] = cr
                cim_ref[k:k + 1, :] = ci

    return pl.pallas_call(
        body, name="ssm_local", grid=(nb, nt),
        out_shape=[jax.ShapeDtypeStruct((nb * 8, NST), F32)] * 2,
        in_specs=[pl.BlockSpec((tt, D_SSM), lambda b, j: (b * nt + j, 0)), _VM, _VM, _VM],
        out_specs=[pl.BlockSpec((8, NST), lambda b, j: (b, 0))] * 2,
        scratch_shapes=[pltpu.VMEM((8, NST), F32), pltpu.VMEM((8, NST), F32), pltpu.VMEM((tt, 2 * NST), F32)],
        compiler_params=_cp(("arbitrary", "arbitrary")),
    )(u_p, bm, lre8, lim8)


def _ssm_fwd(u_p, cre, cim, bm, cm, dvec, w_glu, lre8, lim8, S, tt):
    n = u_p.shape[0]
    nb, nt = n // S, S // tt

    def body(u_ref, cre_ref, cim_ref, bm_ref, cm_ref, d_ref, wg_ref, lre_ref, lim_ref,
             st_ref, ypre_ref, z_ref, gact_ref, yssm_ref, sre, sim, bu):
        j = pl.program_id(1)

        @pl.when(j == 0)
        def _():
            sre[...] = cre_ref[...]
            sim[...] = cim_ref[...]

        u = u_ref[...]
        _bd_expand(u.astype(BF16), bm_ref, bu)
        lre, lim = lre_ref[...], lim_ref[...]

        def step(i, c):
            sr, si = c
            off = pl.multiple_of(i * 8, 8)
            nr = lre * sr - lim * si + bu[pl.ds(off, 8), 0:NST]
            ni = lre * si + lim * sr + bu[pl.ds(off, 8), NST:2 * NST]
            st_ref[pl.ds(off, 8), 0:NST] = nr
            st_ref[pl.ds(off, 8), NST:2 * NST] = ni
            return nr, ni

        sr, si = lax.fori_loop(0, tt // 8, step, (sre[...], sim[...]))
        sre[...] = sr
        sim[...] = si
        y = _bd_project(st_ref[...].astype(BF16), cm_ref) + d_ref[...] * u
        ypre_ref[...] = y
        gb = _gelu(y).astype(BF16)
        gact_ref[...] = gb
        z = _dot(gb, wg_ref[...])
        z_ref[...] = z
        yssm_ref[...] = z[:, 0:D_SSM] * _sigmoid(z[:, D_SSM:2 * D_SSM])

    row = lambda w: pl.BlockSpec((tt, w), lambda b, j: (b * nt + j, 0))
    return pl.pallas_call(
        body, name="ssm_fwd", grid=(nb, nt),
        out_shape=[jax.ShapeDtypeStruct((n, 2 * NST), F32), jax.ShapeDtypeStruct((n, D_SSM), F32),
                   jax.ShapeDtypeStruct((n, 2 * D_SSM), F32), jax.ShapeDtypeStruct((n, D_SSM), BF16),
                   jax.ShapeDtypeStruct((n, D_SSM), F32)],
        in_specs=[row(D_SSM), pl.BlockSpec((8, NST), lambda b, j: (b, 0)), pl.BlockSpec((8, NST), lambda b, j: (b, 0)),
                  _VM, _VM, _VM, _VM, _VM, _VM],
        out_specs=[row(2 * NST), row(D_SSM), row(2 * D_SSM), row(D_SSM), row(D_SSM)],
        scratch_shapes=[pltpu.VMEM((8, NST), F32), pltpu.VMEM((8, NST), F32), pltpu.VMEM((tt, 2 * NST), F32)],
        compiler_params=_cp(("arbitrary", "arbitrary")),
    )(u_p, cre, cim, bm, cm, dvec, w_glu, lre8, lim8)


def _ssm_bwd_a(dys_p, z, ypre, w_glu, cm, lre8, lim8, S, tt):
    n = z.shape[0]
    nb, nt = n // S, S // tt
    nsq = int(round(math.log2(S // 8)))
    ng = tt // 8

    def body(dys_ref, z_ref, y_ref, wg_ref, cm_ref, lre_ref, lim_ref, dy_ref, dz_ref, are_ref, aim_ref, sre, sim, gb):
        j = pl.program_id(1)

        @pl.when(j == 0)
        def _():
            sre[...] = jnp.zeros_like(sre)
            sim[...] = jnp.zeros_like(sim)

        z = z_ref[...]
        z1, z2 = z[:, 0:D_SSM], z[:, D_SSM:2 * D_SSM]
        sg = _sigmoid(z2)
        dys = dys_ref[...]
        dz = jnp.concatenate([dys * sg, dys * z1 * sg * (1.0 - sg)], axis=1).astype(BF16)
        dz_ref[...] = dz
        dy = _dot_nt(dz, wg_ref[...]) * _gelu_grad(y_ref[...])
        dy_ref[...] = dy
        _bd_expand_t(dy.astype(BF16), cm_ref, gb)
        lre, lim = lre_ref[...], lim_ref[...]

        def step(i, c):
            ar, ai = c
            off = pl.multiple_of((ng - 1 - i) * 8, 8)
            gr = gb[pl.ds(off, 8), 0:NST]
            gi = gb[pl.ds(off, 8), NST:2 * NST]
            return lre * ar + lim * ai + gr, lre * ai - lim * ar + gi

        ar, ai = lax.fori_loop(0, ng, step, (sre[...], sim[...]))
        sre[...] = ar
        sim[...] = ai

        @pl.when(j == nt - 1)
        def _():
            pr, pi = _pow2k(lre[0:1], -lim[0:1], nsq)
            cr = jnp.zeros((1, NST), F32)
            ci = jnp.zeros((1, NST), F32)
            are_ref[7:8, :] = cr
            aim_ref[7:8, :] = ci
            for k in range(6, -1, -1):
                cr, ci = ar[k + 1:k + 2] + pr * cr - pi * ci, ai[k + 1:k + 2] + pr * ci + pi * cr
                are_ref[k:k + 1, :] = cr
                aim_ref[k:k + 1, :] = ci

    row = lambda w: pl.BlockSpec((tt, w), lambda b, j: (b * nt + nt - 1 - j, 0))
    return pl.pallas_call(
        body, name="ssm_bwd_a", grid=(nb, nt),
        out_shape=[jax.ShapeDtypeStruct((n, D_SSM), F32), jax.ShapeDtypeStruct((n, 2 * D_SSM), BF16),
                   jax.ShapeDtypeStruct((nb * 8, NST), F32), jax.ShapeDtypeStruct((nb * 8, NST), F32)],
        in_specs=[row(D_SSM), row(2 * D_SSM), row(D_SSM), _VM, _VM, _VM, _VM],
        out_specs=[row(D_SSM), row(2 * D_SSM), pl.BlockSpec((8, NST), lambda b, j: (b, 0)),
                   pl.BlockSpec((8, NST), lambda b, j: (b, 0))],
        scratch_shapes=[pltpu.VMEM((8, NST), F32), pltpu.VMEM((8, NST), F32), pltpu.VMEM((tt, 2 * NST), F32)],
        compiler_params=_cp(("arbitrary", "arbitrary")),
    )(dys_p, z, ypre, w_glu, cm, lre8, lim8)


def _ssm_bwd_b(dy, u_p, st, fcr, fci, air, aii, bm, cm, dvec, lre8, lim8, S, tt):
    n = u_p.shape[0]
    nb, nt = n // S, S // tt
    ng = tt // 8

    def body(dy_ref, u_ref, st_ref, stp_ref, fcr_ref, fci_ref, air_ref, aii_ref, bm_ref, cm_ref, d_ref, lre_ref, lim_ref,
             du_ref, dcm_ref, dbm_ref, dd_ref, dlr_ref, dli_ref, are, aim, accr, acci, sp, ab):
        b = pl.program_id(0)
        j = pl.program_id(1)
        jt = nt - 1 - j

        @pl.when((b == 0) & (j == 0))
        def _():
            dcm_ref[...] = jnp.zeros_like(dcm_ref)
            dbm_ref[...] = jnp.zeros_like(dbm_ref)
            dd_ref[...] = jnp.zeros_like(dd_ref)
            accr[...] = jnp.zeros_like(accr)
            acci[...] = jnp.zeros_like(acci)

        @pl.when(j == 0)
        def _():
            are[...] = air_ref[...]
            aim[...] = aii_ref[...]

        sp[8:tt + 8, :] = st_ref[...]

        @pl.when(jt == 0)
        def _():
            sp[0:8, 0:NST] = fcr_ref[...]
            sp[0:8, NST:2 * NST] = fci_ref[...]

        @pl.when(jt != 0)
        def _():
            sp[0:8, :] = stp_ref[...]

        dy = dy_ref[...]
        u = u_ref[...]
        dyb = dy.astype(BF16)
        _bd_expand_t(dyb, cm_ref, ab)
        lre, lim = lre_ref[...], lim_ref[...]

        def step(i, c):
            ar, ai = c
            off = pl.multiple_of((ng - 1 - i) * 8, 8)
            nr = lre * ar + lim * ai + ab[pl.ds(off, 8), 0:NST]
            ni = lre * ai - lim * ar + ab[pl.ds(off, 8), NST:2 * NST]
            ab[pl.ds(off, 8), 0:NST] = nr
            ab[pl.ds(off, 8), NST:2 * NST] = ni
            pr = sp[pl.ds(off, 8), 0:NST]
            pi = sp[pl.ds(off, 8), NST:2 * NST]
            accr[...] += nr * pr + ni * pi
            acci[...] += ni * pr - nr * pi
            return nr, ni

        ar, ai = lax.fori_loop(0, ng, step, (are[...], aim[...]))
        are[...] = ar
        aim[...] = ai
        a_b = ab[...].astype(BF16)
        du_ref[...] = _bd_project_t(a_b, bm_ref) + d_ref[...] * dy
        ub = u.astype(BF16)
        for q in range(4):
            for part in range(2):
                lo = part * NST + q * 4 * QB
                s_q = sp[8:tt + 8, lo:lo + 4 * QB].astype(BF16)
                dcm_ref[lo:lo + 4 * QB, :] += _dot_tn(s_q, dyb[:, q * QB:(q + 1) * QB])
                dbm_ref[:, lo:lo + 4 * QB] += _dot_tn(ub[:, q * QB:(q + 1) * QB], a_b[:, lo:lo + 4 * QB])
        dd_ref[...] += _colsum(dy * u)

        @pl.when((b == nb - 1) & (j == nt - 1))
        def _():
            dlr_ref[...] = _colsum(accr[...])
            dli_ref[...] = _colsum(acci[...])

    row = lambda w: pl.BlockSpec((tt, w), lambda b, j: (b * nt + nt - 1 - j, 0))
    seq8 = pl.BlockSpec((8, NST), lambda b, j: (b, 0))
    prev = pl.BlockSpec((8, 2 * NST), lambda b, j: (jnp.maximum((b * nt + nt - 1 - j) * ng - 1, 0), 0))
    const = lambda shape: pl.BlockSpec(shape, lambda b, j: (0, 0))
    return pl.pallas_call(
        body, name="ssm_bwd_b", grid=(nb, nt),
        out_shape=[jax.ShapeDtypeStruct((n, D_SSM), F32), jax.ShapeDtypeStruct((2 * NST, QB), F32),
                   jax.ShapeDtypeStruct((QB, 2 * NST), F32), jax.ShapeDtypeStruct((1, D_SSM), F32),
                   jax.ShapeDtypeStruct((1, NST), F32), jax.ShapeDtypeStruct((1, NST), F32)],
        in_specs=[row(D_SSM), row(D_SSM), row(2 * NST), prev, seq8, seq8, seq8, seq8, _VM, _VM, _VM, _VM, _VM],
        out_specs=[row(D_SSM), const((2 * NST, QB)), const((QB, 2 * NST)), const((1, D_SSM)),
                   const((1, NST)), const((1, NST))],
        scratch_shapes=[pltpu.VMEM((8, NST), F32)] * 4 + [pltpu.VMEM((tt + 8, 2 * NST), F32),
                                                          pltpu.VMEM((tt, 2 * NST), F32)],
        compiler_params=_cp(("arbitrary", "arbitrary")),
    )(dy, u_p, st, st, fcr, fci, air, aii, bm, cm, dvec, lre8, lim8)


def _rope(v, c, s1, s2):
    return v * c + _roll(v, -16) * s1 + _roll(v, 16) * s2


def _rope_t(dv, c, s1, s2):
    return dv * c + _roll(dv * s1, 16) + _roll(dv * s2, -16)


def _mla_fwd(proj, rc, rs1, rs2, gq, gkv, w_uq, w_ukv, tm):
    n = proj.shape[0]

    def body(ql_ref, kvl_ref, kr_ref, c_ref, s1_ref, s2_ref, gq_ref, gkv_ref, wq_ref, wkv_ref,
             q_ref, k_ref, v_ref, qn_ref, kvn_ref):
        c, s1, s2 = c_ref[...], s1_ref[...], s2_ref[...]
        qhat, _ = _rms(ql_ref[...], Q_LORA)
        qn = (qhat * gq_ref[...]).astype(BF16)
        qn_ref[...] = qn
        q = _dot(qn, wq_ref[...])
        qr = _rope(q, jnp.tile(c, (1, NH)), jnp.tile(s1, (1, NH)), jnp.tile(s2, (1, NH)))
        q_ref[...] = (qr * _C2).astype(BF16)
        khat, _ = _rms(kvl_ref[...], KV_LORA)
        kvn = (khat * gkv_ref[...]).astype(BF16)
        kvn_ref[...] = kvn
        kv = _dot(kvn, wkv_ref[...])
        kr = _rope(_roll(kr_ref[...], 64), c, s1, s2)
        k_ref[...] = (kv[:, 0:NH * HP] + jnp.tile(kr, (1, NH))).astype(BF16)
        v_ref[...] = kv[:, NH * HP:2 * NH * HP].astype(BF16)

    def wrapped(proj_ref, *rest):
        ql = proj_ref.at[:, D_SSM:D_SSM + Q_LORA]
        kvl = proj_ref.at[:, D_SSM + Q_LORA:D_SSM + Q_LORA + KV_LORA]
        kr = proj_ref.at[:, IN_PAD - HP:IN_PAD]
        body(ql, kvl, kr, *rest)

    row = lambda w: pl.BlockSpec((tm, w), lambda i: (i, 0))
    return pl.pallas_call(
        wrapped, name="mla_fwd", grid=(n // tm,),
        out_shape=[jax.ShapeDtypeStruct((n, NH * HP), BF16)] * 3 +
                  [jax.ShapeDtypeStruct((n, Q_LORA), BF16), jax.ShapeDtypeStruct((n, KV_LORA), BF16)],
        in_specs=[row(IN_PAD), row(HP), row(HP), row(HP), _VM, _VM, _VM, _VM],
        out_specs=[row(NH * HP)] * 3 + [row(Q_LORA), row(KV_LORA)],
        compiler_params=_cp(("parallel",)),
    )(proj, rc, rs1, rs2, gq, gkv, w_uq, w_ukv)


def _mla_bwd(dq, dk, dv, proj, rc, rs1, rs2, gq, gkv, w_uq, w_ukv, tm):
    n = proj.shape[0]

    def body(dq_ref, dk_ref, dv_ref, proj_ref, c_ref, s1_ref, s2_ref, gq_ref, gkv_ref, wq_ref, wkv_ref,
             dmla_ref, dqb_ref, dkvb_ref, acc_ref):
        i = pl.program_id(0)
        c, s1, s2 = c_ref[...], s1_ref[...], s2_ref[...]
        dqu = _rope_t(dq_ref[...] * _SCALE, jnp.tile(c, (1, NH)), jnp.tile(s1, (1, NH)),
                      jnp.tile(s2, (1, NH))).astype(BF16)
        dqb_ref[...] = dqu
        dqn = _dot_nt(dqu, wq_ref[...])
        qhat, rq = _rms(proj_ref[:, D_SSM:D_SSM + Q_LORA], Q_LORA)
        dql = _rms_bwd(dqn * gq_ref[...], qhat, rq, Q_LORA)
        dkf = dk_ref[...] * (1.0 / _LOG2E)
        dkv = jnp.concatenate([dkf.astype(BF16), dv_ref[...].astype(BF16)], axis=1)
        dkvb_ref[...] = dkv
        dkvn = _dot_nt(dkv, wkv_ref[...])
        khat, rk = _rms(proj_ref[:, D_SSM + Q_LORA:D_SSM + Q_LORA + KV_LORA], KV_LORA)
        dkvl = _rms_bwd(dkvn * gkv_ref[...], khat, rk, KV_LORA)
        dkr = dkf[:, 0:HP]
        for h in range(1, NH):
            dkr = dkr + dkf[:, h * HP:(h + 1) * HP]
        lane = lax.broadcasted_iota(jnp.int32, dkr.shape, 1)
        dkr = jnp.where((lane >= QK_NOPE) & (lane < QK_NOPE + QK_ROPE), dkr, 0.0)
        dkr = _roll(_rope_t(dkr, c, s1, s2), -64)
        dmla_ref[...] = jnp.concatenate([dql, dkvl, dkr], axis=1)

        @pl.when(i == 0)
        def _():
            acc_ref[...] = jnp.zeros_like(acc_ref)

        acc_ref[0:1, 0:Q_LORA] += _colsum(dqn * qhat)
        acc_ref[1:2, 0:KV_LORA] += _colsum(dkvn * khat)

    row = lambda w: pl.BlockSpec((tm, w), lambda i: (i, 0))
    return pl.pallas_call(
        body, name="mla_bwd", grid=(n // tm,),
        out_shape=[jax.ShapeDtypeStruct((n, IN_PAD - D_SSM), F32), jax.ShapeDtypeStruct((n, NH * HP), BF16),
                   jax.ShapeDtypeStruct((n, 2 * NH * HP), BF16), jax.ShapeDtypeStruct((8, Q_LORA), F32)],
        in_specs=[row(NH * HP)] * 3 + [row(IN_PAD), row(HP), row(HP), row(HP), _VM, _VM, _VM, _VM],
        out_specs=[row(IN_PAD - D_SSM), row(NH * HP), row(2 * NH * HP), pl.BlockSpec((8, Q_LORA), lambda i: (0, 0))],
        compiler_params=_cp(("arbitrary",)),
    )(dq, dk, dv, proj, rc, rs1, rs2, gq, gkv, w_uq, w_ukv)


_SCALE = (QK_NOPE + QK_ROPE) ** -0.5
_LOG2E = 1.4426950408889634
_C2 = _SCALE * _LOG2E


def _attn_fwd(q, k, v, S, tq):
    n = q.shape[0]
    nb, nq = n // S, S // tq

    def body(q_ref, k_ref, v_ref, o_ref, lr_ref):
        qi = pl.program_id(2)
        qv = q_ref[...]

        def tile(j, c, diagonal):
            m, l, acc = c
            off = pl.multiple_of(j * tq, tq)
            s = _dot_nt(qv, k_ref[pl.ds(off, tq), :])
            if diagonal:
                rows = lax.broadcasted_iota(jnp.int32, s.shape, 0)
                cols = lax.broadcasted_iota(jnp.int32, s.shape, 1)
                s = jnp.where(cols <= rows, s, NEG)
            mn = jnp.maximum(m, jnp.max(s, axis=1, keepdims=True))
            p = jnp.exp2(s - mn)
            al = jnp.exp2(m - mn)
            l = al * l + jnp.sum(p, axis=1, keepdims=True)
            acc = al * acc + _dot(p.astype(BF16), v_ref[pl.ds(off, tq), :])
            return mn, l, acc

        init = (jnp.full((tq, 1), NEG, F32), jnp.zeros((tq, 1), F32), jnp.zeros((tq, HP), F32))
        c = lax.fori_loop(0, qi, lambda j, c: tile(j, c, False), init)
        m, l, acc = tile(qi, c, True)
        o_ref[...] = acc / l
        lane = lax.broadcasted_iota(jnp.int32, (8, HP), 1)
        lse = jnp.broadcast_to(m + jnp.log(l) * _LOG2E, (tq, HP))
        lr_ref[...] = _rows_of(lse, jnp.where(lane == 0, 1.0, 0.0).astype(BF16))

    qs = pl.BlockSpec((tq, HP), lambda b, h, i: (b * nq + i, h))
    ks = pl.BlockSpec((S, HP), lambda b, h, i: (b, h))
    return pl.pallas_call(
        body, name="attn_fwd", grid=(nb, NH, nq),
        out_shape=[jax.ShapeDtypeStruct((n, NH * HP), F32), jax.ShapeDtypeStruct((nb * NH * 8, S), F32)],
        in_specs=[qs, ks, ks], out_specs=[qs, pl.BlockSpec((8, tq), lambda b, h, i: (b * NH + h, i))],
        compiler_params=_cp(("parallel", "parallel", "arbitrary")),
    )(q, k, v)


def _rows_of(x, pick):
    x1 = x.astype(BF16)
    r1 = x - x1.astype(F32)
    x2 = r1.astype(BF16)
    x3 = (r1 - x2.astype(F32)).astype(BF16)
    return _dot_nt(pick, x1) + _dot_nt(pick, x2) + _dot_nt(pick, x3)


def _attn_bwd(q, k, v, dob, lrow, drow, S, tq):
    n = q.shape[0]
    nb, nq = n // S, S // tq

    def body(q_ref, k_ref, v_ref, do_ref, lr_ref, dr_ref, dqo_ref, dk_ref, dv_ref, dq_ref):
        kj = pl.program_id(2)

        @pl.when(kj == 0)
        def _():
            dq_ref[...] = jnp.zeros_like(dq_ref)

        kt = k_ref[...]
        vt = v_ref[...]

        def tile(i, c, diagonal):
            dk, dv = c
            off = pl.multiple_of(i * tq, tq)
            qv = q_ref[pl.ds(off, tq), :]
            dob = do_ref[pl.ds(off, tq), :]
            lr = lr_ref[0:1, pl.ds(off, tq)]
            dr = dr_ref[0:1, pl.ds(off, tq)]
            st = _dot_nt(kt, qv)
            dpt = _dot_nt(vt, dob)
            pt = jnp.exp2(st - lr)
            if diagonal:
                keys = lax.broadcasted_iota(jnp.int32, pt.shape, 0)
                qrys = lax.broadcasted_iota(jnp.int32, pt.shape, 1)
                pt = jnp.where(keys <= qrys, pt, 0.0)
            dst = (pt * (dpt - dr)).astype(BF16)
            dq_ref[pl.ds(off, tq), :] += _dot_tn(dst, kt)
            return dk + _dot(dst, qv), dv + _dot(pt.astype(BF16), dob)

        zero = jnp.zeros((tq, HP), F32)
        c = tile(kj, (zero, zero), True)
        dk, dv = lax.fori_loop(kj + 1, nq, lambda i, c: tile(i, c, False), c)
        dk_ref[...] = dk.astype(BF16)
        dv_ref[...] = dv.astype(BF16)

        @pl.when(kj == nq - 1)
        def _():
            dqo_ref[...] = dq_ref[...].astype(BF16)

    ts = pl.BlockSpec((tq, HP), lambda b, h, i: (b * nq + i, h))
    fs = pl.BlockSpec((S, HP), lambda b, h, i: (b, h))
    rs = pl.BlockSpec((8, S), lambda b, h, i: (b * NH + h, 0))
    return pl.pallas_call(
        body, name="attn_bwd", grid=(nb, NH, nq),
        out_shape=[jax.ShapeDtypeStruct((n, NH * HP), BF16)] * 3,
        in_specs=[fs, ts, ts, fs, rs, rs], out_specs=[fs, ts, ts],
        scratch_shapes=[pltpu.VMEM((S, HP), F32)],
        compiler_params=_cp(("parallel", "parallel", "arbitrary")),
    )(q, k, v, dob, lrow, drow)


def _p1_fwd(yssm, oattn, x, modp, gs, ga, w_out, g2, S, tm):
    n = x.shape[0]
    tps = S // tm

    def body(ys_ref, oa_ref, x_ref, mod_ref, gs_ref, ga_ref, w_ref, g2_ref, yn_ref, o_ref, x1_ref, h2_ref):
        yh, _ = _rms(ys_ref[...], D_SSM)
        ah, _ = _rms(oa_ref[...], D_ATTN)
        yn = jnp.concatenate([yh * gs_ref[...], ah * ga_ref[...]], axis=1).astype(BF16)
        yn_ref[...] = yn
        o = _dot(yn, w_ref[...])
        o_ref[...] = o.astype(BF16)
        x1 = x_ref[...] + mod_ref[0, 2:3, :] * o
        x1_ref[...] = x1
        xh, _ = _rms(x1, D)
        h2_ref[...] = ((xh * g2_ref[...]) * (1.0 + mod_ref[0, 4:5, :]) + mod_ref[0, 3:4, :]).astype(BF16)

    row = lambda w: pl.BlockSpec((tm, w), lambda i: (i, 0))
    return pl.pallas_call(
        body, name="p1_fwd", grid=(n // tm,),
        out_shape=[jax.ShapeDtypeStruct((n, D_SSM + NH * HP), BF16), jax.ShapeDtypeStruct((n, D), BF16),
                   jax.ShapeDtypeStruct((n, D), F32), jax.ShapeDtypeStruct((n, D), BF16)],
        in_specs=[row(D_SSM), row(NH * HP), row(D), pl.BlockSpec((1, 8, D), lambda i: (i // tps, 0, 0)),
                  _VM, _VM, _VM, _VM],
        out_specs=[row(D_SSM + NH * HP), row(D), row(D), row(D)],
        compiler_params=_cp(("parallel",)),
    )(yssm, oattn, x, modp, gs, ga, w_out, g2)


def _p2(x1, h2, target, modp, g2, gf, w_ff1, w_ff2, S, tm):
    n = x1.shape[0]
    tps = S // tm
    nb = n // S

    def body(x1_ref, h2_ref, t_ref, mod_ref, g2_ref, gf_ref, w1_ref, w2_ref,
             dx1_ref, r_ref, da_ref, dff_ref, accs_ref, accg_ref):
        i = pl.program_id(0)
        sh2, sc2, gt2 = mod_ref[0, 3:4, :], mod_ref[0, 4:5, :], mod_ref[0, 5:6, :]
        fsh, fsc = mod_ref[0, 6:7, :], mod_ref[0, 7:8, :]
        x1 = x1_ref[...]
        a = _dot(h2_ref[...], w1_ref[...])
        ra = jnp.maximum(a, 0.0)
        rb = (ra * ra).astype(BF16)
        r_ref[...] = rb
        ff = _dot(rb, w2_ref[...])
        x2 = x1 + gt2 * ff
        x2h, rf = _rms(x2, D)
        gf_v = gf_ref[...]
        outn = x2h * gf_v
        err = outn * (1.0 + fsc) + fsh - t_ref[...]
        dout = err * (1.0 / D)
        doutn = dout * (1.0 + fsc)
        dx2 = _rms_bwd(doutn * gf_v, x2h, rf, D)
        dff = (gt2 * dx2).astype(BF16)
        dff_ref[...] = dff
        dr = _dot_nt(dff, w2_ref[...])
        da = (dr * (2.0 * ra)).astype(BF16)
        da_ref[...] = da
        dh2 = _dot_nt(da, w1_ref[...])
        x1h, r2 = _rms(x1, D)
        g2_v = g2_ref[...]
        dn2 = dh2 * (1.0 + sc2)
        dx1_ref[...] = dx2 + _rms_bwd(dn2 * g2_v, x1h, r2, D)

        @pl.when(i % tps == 0)
        def _():
            accs_ref[...] = jnp.zeros_like(accs_ref)

        @pl.when(i == 0)
        def _():
            accg_ref[...] = jnp.zeros_like(accg_ref)

        accs_ref[0, 3:4, :] += _colsum(dh2)
        accs_ref[0, 4:5, :] += _colsum(dh2 * (x1h * g2_v))
        accs_ref[0, 5:6, :] += _colsum(dx2 * ff)
        accs_ref[0, 6:7, :] += _colsum(dout)
        accs_ref[0, 7:8, :] += _colsum(dout * outn)
        accg_ref[0:1, :] += _colsum(dn2 * x1h)
        accg_ref[1:2, :] += _colsum(doutn * x2h)
        accg_ref[2:3, :] += _colsum(err * err) * (0.5 / D)

    row = lambda w: pl.BlockSpec((tm, w), lambda i: (i, 0))
    return pl.pallas_call(
        body, name="p2_mlp_loss", grid=(n // tm,),
        out_shape=[jax.ShapeDtypeStruct((n, D), F32), jax.ShapeDtypeStruct((n, D_FF), BF16),
                   jax.ShapeDtypeStruct((n, D_FF), BF16), jax.ShapeDtypeStruct((n, D), BF16),
                   jax.ShapeDtypeStruct((nb, 8, D), F32), jax.ShapeDtypeStruct((8, D), F32)],
        in_specs=[row(D), row(D), row(D), pl.BlockSpec((1, 8, D), lambda i: (i // tps, 0, 0)), _VM, _VM, _VM, _VM],
        out_specs=[row(D), row(D_FF), row(D_FF), row(D), pl.BlockSpec((1, 8, D), lambda i: (i // tps, 0, 0)),
                   pl.BlockSpec((8, D), lambda i: (0, 0))],
        compiler_params=_cp(("arbitrary",)),
    )(x1, h2, target, modp, g2, gf, w_ff1, w_ff2)


def _p3_bwd(dx1, o, yssm, oattn, modp, gs, ga, w_out, S, tm):
    n = dx1.shape[0]
    tps = S // tm
    nb = n // S

    def body(dx1_ref, o_ref, ys_ref, oa_ref, mod_ref, gs_ref, ga_ref, w_ref,
             do_ref, dys_ref, doa_ref, dr_ref, accs_ref, accg_ref):
        i = pl.program_id(0)
        dx1 = dx1_ref[...]
        dob = (mod_ref[0, 2:3, :] * dx1).astype(BF16)
        do_ref[...] = dob
        dyn = _dot_nt(dob, w_ref[...])
        yh, rs = _rms(ys_ref[...], D_SSM)
        oa = oa_ref[...]
        ah, ra = _rms(oa, D_ATTN)
        d1 = dyn[:, 0:D_SSM]
        d2 = dyn[:, D_SSM:D_SSM + NH * HP]
        dys_ref[...] = _rms_bwd(d1 * gs_ref[...], yh, rs, D_SSM)
        doa = _rms_bwd(d2 * ga_ref[...], ah, ra, D_ATTN)
        doa_ref[...] = doa.astype(BF16)
        prod = doa * oa
        ones = jnp.ones((8, HP), BF16)
        for h in range(NH):
            dr_ref[h * 8:(h + 1) * 8, :] = _rows_of(prod[:, h * HP:(h + 1) * HP], ones)

        @pl.when(i % tps == 0)
        def _():
            accs_ref[...] = jnp.zeros_like(accs_ref)

        @pl.when(i == 0)
        def _():
            accg_ref[...] = jnp.zeros_like(accg_ref)

        accs_ref[0, 2:3, :] += _colsum(dx1 * o_ref[...])
        accg_ref[0:1, 0:D_SSM] += _colsum(d1 * yh)
        accg_ref[1:2, :] += _colsum(d2 * ah)

    row = lambda w: pl.BlockSpec((tm, w), lambda i: (i, 0))
    return pl.pallas_call(
        body, name="p3_bwd", grid=(n // tm,),
        out_shape=[jax.ShapeDtypeStruct((n, D), BF16), jax.ShapeDtypeStruct((n, D_SSM), F32),
                   jax.ShapeDtypeStruct((n, NH * HP), BF16), jax.ShapeDtypeStruct((nb * NH * 8, S), F32),
                   jax.ShapeDtypeStruct((nb, 8, D), F32), jax.ShapeDtypeStruct((8, NH * HP), F32)],
        in_specs=[row(D), row(D), row(D_SSM), row(NH * HP), pl.BlockSpec((1, 8, D), lambda i: (i // tps, 0, 0)),
                  _VM, _VM, _VM],
        out_specs=[row(D), row(D_SSM), row(NH * HP), pl.BlockSpec((NH * 8, tm), lambda i: (i // tps, i % tps)),
                   pl.BlockSpec((1, 8, D), lambda i: (i // tps, 0, 0)), pl.BlockSpec((8, NH * HP), lambda i: (0, 0))],
        compiler_params=_cp(("arbitrary",)),
    )(dx1, o, yssm, oattn, modp, gs, ga, w_out)


def _wgrad(a, b, name, col_slots=0):
    n, k1 = a.shape
    k2 = b.shape[1]
    bn = next((b for b in (1024, 512) if n % b == 0), n)
    bk1 = next((b for b in (1024, 512) if k1 % b == 0), k1)
    bk2 = k2 // col_slots if col_slots else (1024 if (k2 % 1024 == 0) else k2)

    def body(a_ref, b_ref, o_ref):
        @pl.when(pl.program_id(2) == 0)
        def _():
            o_ref[...] = jnp.zeros_like(o_ref)

        o_ref[...] += _dot_tn(a_ref[...], b_ref[...]).reshape(o_ref.shape)

    if col_slots:
        out_shape = jax.ShapeDtypeStruct((col_slots, k1, bk2), F32)
        out_spec = pl.BlockSpec((1, bk1, bk2), lambda i, j, t: (j, i, 0))
    else:
        out_shape = jax.ShapeDtypeStruct((k1, k2), F32)
        out_spec = pl.BlockSpec((bk1, bk2), lambda i, j, t: (i, j))
    return pl.pallas_call(
        body, name=name, grid=(k1 // bk1, k2 // bk2, n // bn),
        out_shape=out_shape,
        in_specs=[pl.BlockSpec((bn, bk1), lambda i, j, t: (t, i)), pl.BlockSpec((bn, bk2), lambda i, j, t: (t, j))],
        out_specs=out_spec,
        compiler_params=_cp(("parallel", "parallel", "arbitrary")),
    )(a, b)


def _row_block(rows):
    if rows <= 256:
        return rows
    return next(b for b in (256, 192, 128, 64, 32, 16, 8) if rows % b == 0)


def _add_half(g, recv, cidx, name):
    _, rows2, w = g.shape
    rows = rows2 // 2
    br = _row_block(rows)
    nblk = rows // br

    def body(c_ref, g_ref, r_ref, o_ref):
        o_ref[...] = (g_ref[...] + r_ref[...]).astype(BF16)

    return pl.pallas_call(
        body, name=name,
        grid_spec=pltpu.PrefetchScalarGridSpec(
            num_scalar_prefetch=1, grid=(4, nblk),
            in_specs=[pl.BlockSpec((1, br, w), lambda s, i, c: (s, c[0] * nblk + i, 0)),
                      pl.BlockSpec((1, br, w), lambda s, i, c: (s, i, 0))],
            out_specs=pl.BlockSpec((1, br, w), lambda s, i, c: (s, i, 0))),
        out_shape=jax.ShapeDtypeStruct((4, rows, w), BF16),
        compiler_params=_cp(("parallel", "parallel")),
    )(cidx, g, recv)


def _add_chips(r, name):
    _, rows, w = r.shape
    br = _row_block(rows)

    def body(r_ref, o_ref):
        f = lambda k: r_ref[k].astype(F32)
        o_ref[...] = ((f(0) + f(1)) + f(2)) + f(3)

    return pl.pallas_call(
        body, name=name, grid=(rows // br,),
        out_shape=jax.ShapeDtypeStruct((rows, w), F32),
        in_specs=[pl.BlockSpec((4, br, w), lambda i: (0, i, 0))],
        out_specs=pl.BlockSpec((br, w), lambda i: (i, 0)),
        compiler_params=_cp(("parallel",)),
    )(r)


def _sum_devices(a, b):
    def body(a_ref, b_ref, oa_ref, ob_ref):
        acc = a_ref[0:1, :].astype(F32)
        accb = b_ref[0:1, :]
        for k in range(1, 8):
            acc = acc + a_ref[k:k + 1, :].astype(F32)
            accb = accb + b_ref[k:k + 1, :]
        oa_ref[...] = acc
        ob_ref[...] = accb

    return pl.pallas_call(
        body, name="small_grad_sum",
        out_shape=[jax.ShapeDtypeStruct((1, a.shape[1]), F32), jax.ShapeDtypeStruct((1, b.shape[1]), F32)],
        in_specs=[_VM, _VM], out_specs=[_VM, _VM], compiler_params=_cp(),
    )(a, b)


def _adamw_math(wv, gv, mv, vv):
    m_new = ADAM_B1 * mv + (1.0 - ADAM_B1) * gv
    v_new = ADAM_B2 * vv + (1.0 - ADAM_B2) * (gv * gv)
    m_hat = m_new / (1.0 - ADAM_B1 ** ADAM_STEP)
    v_hat = v_new / (1.0 - ADAM_B2 ** ADAM_STEP)
    return -ADAM_LR * (m_hat / (jnp.sqrt(v_hat) + ADAM_EPS) + ADAM_WD * wv), m_new, v_new


def _adamw_small(ws, gs, ms, vs):
    k = len(ws)

    def body(*refs):
        ins, outs = refs[:4 * k], refs[4 * k:]
        for t in range(k):
            d, m_new, v_new = _adamw_math(ins[t][...], ins[k + t][...], ins[2 * k + t][...], ins[3 * k + t][...])
            outs[t][...] = d
            outs[k + t][...] = m_new
            outs[2 * k + t][...] = v_new

    shapes = [jax.ShapeDtypeStruct(w.shape, F32) for w in ws]
    return pl.pallas_call(
        body, name="adamw_small", out_shape=shapes * 3,
        in_specs=[_VM] * (4 * k), out_specs=[_VM] * (3 * k), compiler_params=_cp(),
    )(*ws, *gs, *ms, *vs)


def _adamw(w, g, m, v, name):
    rows, wd = w.shape
    br = _row_block(rows)

    def body(w_ref, g_ref, m_ref, v_ref, d_ref, nm_ref, nv_ref):
        d, m_new, v_new = _adamw_math(w_ref[...], g_ref[...], m_ref[...], v_ref[...])
        d_ref[...] = d
        nm_ref[...] = m_new
        nv_ref[...] = v_new

    spec = pl.BlockSpec((br, wd), lambda i: (i, 0))
    return pl.pallas_call(
        body, name=name, grid=(rows // br,),
        out_shape=[jax.ShapeDtypeStruct((rows, wd), F32)] * 3,
        in_specs=[spec] * 4, out_specs=[spec] * 3,
        compiler_params=_cp(("parallel",)),
    )(w, g, m, v)


def _adamw_halves(w, mine, other, m, v, cidx, name):
    rows, wd = w.shape
    h = rows // 2
    br = _row_block(h)
    nblk = h // br

    def body(c_ref, w_ref, a_ref, b_ref, m_ref, v_ref, g_ref, d_ref, nm_ref, nv_ref):
        gv = jnp.where(pl.program_id(0) == c_ref[0], a_ref[...], b_ref[...])
        d, m_new, v_new = _adamw_math(w_ref[...], gv, m_ref[...], v_ref[...])
        g_ref[...] = gv
        d_ref[...] = d
        nm_ref[...] = m_new
        nv_ref[...] = v_new

    full = pl.BlockSpec((br, wd), lambda hf, i, c: (hf * nblk + i, 0))
    half = pl.BlockSpec((br, wd), lambda hf, i, c: (i, 0))
    return pl.pallas_call(
        body, name=name,
        grid_spec=pltpu.PrefetchScalarGridSpec(
            num_scalar_prefetch=1, grid=(2, nblk),
            in_specs=[full, half, half, full, full], out_specs=[full] * 4),
        out_shape=[jax.ShapeDtypeStruct((rows, wd), F32)] * 4,
        compiler_params=_cp(("parallel", "parallel")),
    )(cidx, w, mine, other, m, v)


def _other_chips(x, y):
    return [(1 - x, y), (x, 1 - y), (1 - x, 1 - y)]


def _other_devices(x, y, c):
    flip = lambda v, d: (1 - v) if d else v
    return [(flip(x, dx), flip(y, dy), flip(c, dc))
            for dx in (0, 1) for dy in (0, 1) for dc in (0, 1) if (dx, dy, dc) != (0, 0, 0)]


def _exchange(name, ins, out_shapes, n_local, n_remote, plan):
    ni, no = len(ins), len(out_shapes)

    def body(*refs):
        in_refs, out_refs = refs[:ni], refs[ni:ni + no]
        send_sems, recv_sems, local_sems = refs[ni + no:]
        x, y, c = lax.axis_index("x"), lax.axis_index("y"), lax.axis_index("c")
        local, remote = plan(in_refs, out_refs, x, y, c)
        assert len(local) == n_local and len(remote) == n_remote

        def push(k, src, dst, dev):
            return pltpu.make_async_remote_copy(src_ref=src, dst_ref=dst, send_sem=send_sems.at[k],
                                                recv_sem=recv_sems.at[k], device_id=dev, device_id_type=MESH)

        own = [pltpu.make_async_copy(s, d, local_sems.at[i]) for i, (s, d) in enumerate(local)]
        for cp in own:
            cp.start()
        sends = [push(k, s, d, dev) for k, (s, d, dev, _) in enumerate(remote)]
        for cp in sends:
            cp.start()
        for k, (s, _, dev, landing) in enumerate(remote):
            push(k, s, landing, dev).wait_recv()
        for cp in sends:
            cp.wait_send()
        for cp in own:
            cp.wait()

    return pl.pallas_call(
        body, name=name, out_shape=out_shapes,
        in_specs=[_ANY] * ni, out_specs=[_ANY] * no,
        scratch_shapes=[pltpu.SemaphoreType.DMA((n_remote,)), pltpu.SemaphoreType.DMA((n_remote,)),
                        pltpu.SemaphoreType.DMA((max(n_local, 1),))],
        compiler_params=pltpu.CompilerParams(has_side_effects=True),
    )(*ins)


def _gather_chips(name, shards, everyone=()):
    ns, ne = len(shards), len(everyone)
    outs = [jax.ShapeDtypeStruct((4,) + a.shape, a.dtype) for a in shards]
    outs += [jax.ShapeDtypeStruct((8,) + a.shape, a.dtype) for a in everyone]

    def plan(i, o, x, y, c):
        mine, me = 2 * x + y, 4 * x + 2 * y + c
        local, remote = [], []
        for t in range(ns):
            local.append((i[t], o[t].at[mine]))
            for px, py in _other_chips(x, y):
                remote.append((i[t], o[t].at[mine], (px, py, c), o[t].at[2 * px + py]))
        for t in range(ns, ns + ne):
            local.append((i[t], o[t].at[me]))
            for px, py, pc in _other_devices(x, y, c):
                remote.append((i[t], o[t].at[me], (px, py, pc), o[t].at[4 * px + 2 * py + pc]))
        return local, remote

    return _exchange(name, list(shards) + list(everyone), outs, ns + ne, 3 * ns + 7 * ne, plan)


_HBM = pl.BlockSpec(memory_space=pltpu.HBM)
_SEM = pl.BlockSpec(memory_space=pltpu.SEMAPHORE)
_EFFECT = pltpu.SideEffectType.DATAFLOW_SIDE_EFFECTING


def _split_start(name, ins, land_shapes, n_remote, plan, after):
    ni, nl = len(ins), len(land_shapes)
    srcs = [pltpu.with_memory_space_constraint(a, pltpu.HBM) for a in ins]
    lands = [pltpu.with_memory_space_constraint(lax.empty(s.shape, s.dtype), pltpu.HBM) for s in land_shapes]

    def body(*refs):
        src, land = refs[:ni], refs[ni:ni + nl]
        first = ni + nl + 1
        send, recv = refs[first:first + n_remote], refs[first + n_remote:first + 2 * n_remote]
        token = refs[first + 2 * n_remote + ni + nl]
        x, y, c = lax.axis_index("x"), lax.axis_index("y"), lax.axis_index("c")
        remote = plan(src, land, x, y, c)
        assert len(remote) == n_remote
        for k, (s, d, dev, _) in enumerate(remote):
            pltpu.make_async_remote_copy(src_ref=s, dst_ref=d, send_sem=send[k], recv_sem=recv[k],
                                         device_id=dev, device_id_type=MESH).start()
        token[...] = jnp.zeros_like(token)

    out = pl.pallas_call(
        body, name=name + "_start",
        out_shape=[pltpu.SemaphoreType.DMA(())] * (2 * n_remote)
                  + [pltpu.HBM(a.shape, a.dtype) for a in ins] + [pltpu.HBM(s.shape, s.dtype) for s in land_shapes]
                  + [jax.ShapeDtypeStruct((8, 128), F32)],
        in_specs=[_HBM] * (ni + nl) + [_ANY], out_specs=[_SEM] * (2 * n_remote) + [_HBM] * (ni + nl) + [_VM],
        input_output_aliases={t: 2 * n_remote + t for t in range(ni + nl)},
        compiler_params=pltpu.CompilerParams(has_side_effects=_EFFECT),
    )(*srcs, *lands, after)
    sems, thru = out[:2 * n_remote], out[2 * n_remote:2 * n_remote + ni + nl]
    return (name, sems, thru[:ni], thru[ni:], n_remote, plan), out[-1]


def _split_wait(handle, after):
    name, sems, srcs, lands, n_remote, plan = handle
    ni, nl = len(srcs), len(lands)

    def body(*refs):
        src, land = refs[:ni], refs[ni:ni + nl]
        send, recv = refs[ni + nl:ni + nl + n_remote], refs[ni + nl + n_remote:ni + nl + 2 * n_remote]
        x, y, c = lax.axis_index("x"), lax.axis_index("y"), lax.axis_index("c")
        for k, (s, _, dev, landing) in enumerate(plan(src, land, x, y, c)):
            cp = pltpu.make_async_remote_copy(src_ref=s, dst_ref=landing, send_sem=send[k], recv_sem=recv[k],
                                              device_id=dev, device_id_type=MESH)
            cp.wait_send()
            cp.wait_recv()

    out = pl.pallas_call(
        body, name=name + "_wait",
        out_shape=[pltpu.HBM(a.shape, a.dtype) for a in srcs] + [pltpu.HBM(a.shape, a.dtype) for a in lands],
        in_specs=[_HBM] * (ni + nl) + [_SEM] * (2 * n_remote) + [_ANY], out_specs=[_HBM] * (ni + nl),
        input_output_aliases={t: t for t in range(ni + nl)},
        compiler_params=pltpu.CompilerParams(has_side_effects=_EFFECT),
    )(*srcs, *lands, *sems, after)
    return out[:ni], out[ni:]


def _plan_to_chips(src, land, x, y, c):
    mine = 2 * x + y
    return [(src[t], land[t].at[mine], (px, py, c), land[t].at[2 * px + py])
            for t in range(len(src)) for px, py in _other_chips(x, y)]


def _plan_swap_halves(src, land, x, y, c):
    out = []
    for t in range(len(src)):
        h = src[t].shape[1] // 2
        out.append((src[t].at[:, pl.ds(pl.multiple_of((1 - c) * h, 8), h), :], land[t], (x, y, 1 - c), land[t]))
    return out


def _plan_scatter_chips(src, land, x, y, c):
    mine = 2 * x + y
    return [(src[t].at[2 * px + py], land[t].at[mine], (px, py, c), land[t].at[2 * px + py])
            for t in range(len(src)) for px, py in _other_chips(x, y)]


def _swap_halves(gs, everyone):
    ns, ne = len(gs), len(everyone)
    outs = [jax.ShapeDtypeStruct((4, g.shape[1] // 2, g.shape[2]), g.dtype) for g in gs]
    outs += [jax.ShapeDtypeStruct((8,) + a.shape, a.dtype) for a in everyone]

    def plan(i, o, x, y, c):
        me = 4 * x + 2 * y + c
        local, remote = [], []
        for t in range(ns):
            h = gs[t].shape[1] // 2
            theirs = i[t].at[:, pl.ds(pl.multiple_of((1 - c) * h, 8), h), :]
            remote.append((theirs, o[t], (x, y, 1 - c), o[t]))
        for t in range(ns, ns + ne):
            local.append((i[t], o[t].at[me]))
            for px, py, pc in _other_devices(x, y, c):
                remote.append((i[t], o[t].at[me], (px, py, pc), o[t].at[4 * px + 2 * py + pc]))
        return local, remote

    return _exchange("grad_swap_sibling", list(gs) + list(everyone), outs, ne, ns + 7 * ne, plan)


def _scatter_chips(parts):
    ns = len(parts)
    outs = [jax.ShapeDtypeStruct(a.shape, a.dtype) for a in parts]

    def plan(i, o, x, y, c):
        mine = 2 * x + y
        local, remote = [], []
        for t in range(ns):
            local.append((i[t].at[mine], o[t].at[mine]))
            for px, py in _other_chips(x, y):
                remote.append((i[t].at[2 * px + py], o[t].at[mine], (px, py, c), o[t].at[2 * px + py]))
        return local, remote

    return _exchange("grad_scatter_chips", list(parts), outs, ns, 3 * ns, plan)


def _join_halves(halves):
    ns = len(halves)
    outs = [jax.ShapeDtypeStruct(a.shape, a.dtype) for a in halves]

    def plan(i, o, x, y, c):
        return [], [(i[t], o[t], (x, y, 1 - c), o[t]) for t in range(ns)]

    return _exchange("grad_join_sibling", list(halves), outs, 0, ns, plan)


def _pad_heads_cols(w, per, used):
    k = w.shape[0]
    w = w.reshape(k, NH, per)[:, :, :used]
    return jnp.pad(w, ((0, 0), (0, 0), (0, HP - used))).reshape(k, NH * HP)


def _unpad_heads_cols(w, used):
    k = w.shape[0]
    return w.reshape(k, NH, HP)[:, :, :used]


def _prep_weights(wf):
    bf = lambda a: a.astype(BF16)
    out = {}
    out["w_in"] = jnp.pad(bf(wf["w_in"]), ((0, 0), (0, IN_PAD - IN_COLS)))
    out["w_glu"] = bf(wf["w_glu"])
    out["w_uq"] = _pad_heads_cols(bf(wf["w_uq"]), QK_NOPE + QK_ROPE, QK_NOPE + QK_ROPE)
    wkv = bf(wf["w_ukv"]).reshape(KV_LORA, NH, QK_NOPE + V_HEAD)
    wk = jnp.pad(wkv[:, :, :QK_NOPE], ((0, 0), (0, 0), (0, HP - QK_NOPE))).reshape(KV_LORA, NH * HP)
    wv = jnp.pad(wkv[:, :, QK_NOPE:], ((0, 0), (0, 0), (0, HP - V_HEAD))).reshape(KV_LORA, NH * HP)
    out["w_ukv"] = jnp.concatenate([wk, wv], axis=1)
    return out


def _prep_late_weights(wf):
    bf = lambda a: a.astype(BF16)
    out = {}
    wo = bf(wf["w_out"])
    wo_a = jnp.pad(wo[D_SSM:].reshape(NH, V_HEAD, D), ((0, 0), (0, HP - V_HEAD), (0, 0))).reshape(NH * HP, D)
    out["w_out"] = jnp.concatenate([wo[:D_SSM], wo_a], axis=0)
    out["w_ff1"] = bf(wf["w_ff1"])
    out["w_ff2"] = bf(wf["w_ff2"])
    return out


def _rope_tables(positions):
    inv_freq = ROPE_BASE ** (-jnp.arange(0, QK_ROPE, 2, dtype=F32) / QK_ROPE)
    ang = positions.astype(F32)[:, None] * inv_freq
    cos, sin = jnp.cos(ang), jnp.sin(ang)
    n = positions.shape[0]
    one = jnp.ones((n, QK_NOPE), F32)
    z16 = jnp.zeros((n, 16), F32)
    z32 = jnp.zeros((n, 32), F32)
    z64 = jnp.zeros((n, QK_NOPE), F32)
    rc = jnp.concatenate([one, cos, cos, z32], axis=1)
    rs1 = jnp.concatenate([z64, -sin, z16, z32], axis=1)
    rs2 = jnp.concatenate([z64, z16, sin, z32], axis=1)
    return rc, rs1, rs2


def _permute_rows(a, S):
    n, w = a.shape
    return a.reshape(n // S, 8, S // 8, w).transpose(0, 2, 1, 3).reshape(n, w)


def _unpermute_rows(a, S):
    n, w = a.shape
    return a.reshape(n // S, S // 8, 8, w).transpose(0, 2, 1, 3).reshape(n, w)


def _block_diag_in(bb):
    eye = jnp.eye(8, dtype=bb.dtype)
    blocks = jnp.einsum("qgph,gk->qghkp", bb.reshape(4, 8, P, H), eye).reshape(4, QB, QS)
    return blocks.transpose(1, 0, 2).reshape(QB, NST)


def _block_diag_out(cc):
    eye = jnp.eye(8, dtype=cc.dtype)
    return jnp.einsum("qghp,gk->qgpkh", cc.reshape(4, 8, H, P), eye).reshape(NST, QB)


def _slots(full):
    r, cdim = full.shape
    return full.reshape(r, 4, cdim // 4).transpose(1, 0, 2)


def _unslots(g):
    s, r, cs = g.shape
    return g.transpose(1, 0, 2).reshape(r, s * cs)


def _local_step(x, positions, target, modp, wf, late_weights=None, reducer=None):
    nb, S, _ = x.shape
    n = nb * S
    tm = min(256, S)
    tr = min(512, S)
    tt = min(256, S)
    tq = min(512, S // 2)
    kw = _prep_weights(wf)
    row = lambda a: a.reshape(1, -1).astype(F32)

    xf = x.reshape(n, D)
    tf = target.reshape(n, D)
    g1, g2, gf = row(wf["norm1_g"]), row(wf["norm2_g"]), row(wf["final_norm_g"])
    h1, proj = _f1_fwd(xf, modp, g1, kw["w_in"], S, tr)

    col = lambda a: a.reshape(NST, 1)
    lam_re, lam_im = col(wf["ssm_lambda_re"]), col(wf["ssm_lambda_im"])
    logdt = jnp.repeat(wf["ssm_log_dt"].reshape(G, 1), P, axis=1).reshape(NST, 1)
    b_re, b_im = wf["ssm_b_re"].reshape(NST, H), wf["ssm_b_im"].reshape(NST, H)
    lbr, lbi, bbr, bbi = _ssm_param_fwd(lam_re, lam_im, logdt, b_re, b_im)
    lre8 = jnp.broadcast_to(lbr.reshape(1, NST), (8, NST))
    lim8 = jnp.broadcast_to(lbi.reshape(1, NST), (8, NST))
    bm = jnp.concatenate([_block_diag_in(bbr.reshape(G, P, H)), _block_diag_in(bbi.reshape(G, P, H))],
                         axis=1).astype(BF16)
    cm = jnp.concatenate([_block_diag_out(wf["ssm_c_re"]), -_block_diag_out(wf["ssm_c_im"])], axis=0).astype(BF16)
    dvec = row(wf["ssm_d"])
    u_p = _permute_rows(proj[:, :D_SSM], S)
    fcr, fci = _ssm_local(u_p, bm, lre8, lim8, S, tt)
    st, ypre, z, gact, yssm_p = _ssm_fwd(u_p, fcr, fci, bm, cm, dvec, kw["w_glu"], lre8, lim8, S, tt)
    yssm = _unpermute_rows(yssm_p, S)

    rc, rs1, rs2 = _rope_tables(positions.reshape(n))
    gq, gkv = row(wf["q_norm_g"]), row(wf["kv_norm_g"])
    q, k, v, qn, kvn = _mla_fwd(proj, rc, rs1, rs2, gq, gkv, kw["w_uq"], kw["w_ukv"], tr)
    oattn, lrow = _attn_fwd(q, k, v, S, tq)

    gs = row(wf["ssm_out_g"])
    ga = jnp.pad(wf["attn_out_g"].reshape(NH, V_HEAD), ((0, 0), (0, HP - V_HEAD))).reshape(1, NH * HP)
    kw.update(_prep_late_weights(late_weights(oattn) if late_weights is not None else wf))
    yn, o, x1, h2 = _p1_fwd(yssm, oattn, xf, modp, gs, ga, kw["w_out"], g2, S, tr)
    dx1, r, da, dff, accs2, accg2 = _p2(x1, h2, tf, modp, g2, gf, kw["w_ff1"], kw["w_ff2"], S, tm)
    loss = accg2[2:3]
    g_ff1 = _wgrad(h2, da, "wgrad_ff1", col_slots=4)
    g_ff2 = _wgrad(r, dff, "wgrad_ff2").reshape(4, D_FF // 4, D)
    gs_b, gq_b = gs, gq
    if reducer is not None:
        gs_b = gs + reducer.start([g_ff1, g_ff2])[0, 0]
    do, dyssm, dob, drow, accs3, accg3 = _p3_bwd(dx1, o, yssm, oattn, modp, gs_b, ga, kw["w_out"], S, tr)

    dq, dk, dv = _attn_bwd(q, k, v, dob, lrow, drow, S, tq)
    if reducer is not None:
        gq_b = gq + reducer.middle(dq)[0, 0]
    dmla, dqb, dkvb, accm = _mla_bwd(dq, dk, dv, proj, rc, rs1, rs2, gq_b, gkv, kw["w_uq"], kw["w_ukv"], tr)

    dys_p = _permute_rows(dyssm, S)
    dy, dz, air, aii = _ssm_bwd_a(dys_p, z, ypre, kw["w_glu"], cm, lre8, lim8, S, tt)
    du_p, dcm, dbm, dd, dlr, dli = _ssm_bwd_b(dy, u_p, st, fcr, fci, air, aii, bm, cm, dvec, lre8, lim8, S, tt)
    du = _unpermute_rows(du_p, S)
    dcm = dcm.reshape(2, 4, 8, P, 8, H)
    dc_re = jnp.einsum("qgpgh->qghp", dcm[0]).reshape(G, H, P)
    dc_im = -jnp.einsum("qgpgh->qghp", dcm[1]).reshape(G, H, P)
    dbm = dbm.reshape(8, H, 2, 4, 8, P)
    dbb_re = jnp.einsum("ghqgp->qgph", dbm[:, :, 0]).reshape(NST, H)
    dbb_im = jnp.einsum("ghqgp->qgph", dbm[:, :, 1]).reshape(NST, H)
    gb_re, gb_im, glr, gli, gdt = _ssm_param_bwd(lam_re, lam_im, logdt, b_re, b_im, dlr.reshape(NST, 1),
                                                 dli.reshape(NST, 1), dbb_re, dbb_im)
    glogdt = _rowsum(gdt.reshape(G, P))

    dx, dproj, accs1, accg1 = _f1_bwd(du, dmla, dx1, xf, modp, g1, kw["w_in"], S, tr)

    big = {}
    big["w_in"] = _slots(_wgrad(h1, dproj, "wgrad_in")[:, :IN_COLS])
    big["w_glu"] = _wgrad(gact, dz, "wgrad_glu", col_slots=4)
    big["w_uq"] = _slots(_unpad_heads_cols(_wgrad(qn, dqb, "wgrad_uq"), QK_NOPE + QK_ROPE).reshape(Q_LORA, -1))
    gkvw = _wgrad(kvn, dkvb, "wgrad_ukv")
    big["w_ukv"] = _slots(jnp.concatenate([_unpad_heads_cols(gkvw[:, :NH * HP], QK_NOPE),
                                           _unpad_heads_cols(gkvw[:, NH * HP:], V_HEAD)], axis=2).reshape(KV_LORA, -1))
    gwo = _wgrad(yn, do, "wgrad_out")
    big["w_out"] = jnp.concatenate([gwo[:D_SSM].reshape(2, D_SSM // 2, D),
                                    gwo[D_SSM:].reshape(2, NH // 2 * HP, D).reshape(2, NH // 2, HP, D)[:, :, :V_HEAD]
                                    .reshape(2, D_ATTN // 2, D)], axis=0)
    big["w_ff1"] = g_ff1
    big["w_ff2"] = g_ff2

    small = {}
    small["norm1_g"] = accg1[0:1]
    small["norm2_g"] = accg2[0:1]
    small["final_norm_g"] = accg2[1:2]
    small["ssm_out_g"] = accg3[0:1, :D_SSM]
    small["attn_out_g"] = accg3[1].reshape(NH, HP)[:, :V_HEAD].reshape(1, D_ATTN)
    small["q_norm_g"] = accm[0:1, :Q_LORA]
    small["kv_norm_g"] = accm[1:2, :KV_LORA]
    small["ssm_lambda_re"] = glr.reshape(G, P)
    small["ssm_lambda_im"] = gli.reshape(G, P)
    small["ssm_b_re"] = gb_re
    small["ssm_b_im"] = gb_im
    small["ssm_c_re"] = dc_re.reshape(G * H, P)
    small["ssm_c_im"] = dc_im.reshape(G * H, P)
    small["ssm_d"] = dd.reshape(G, H)
    small["ssm_log_dt"] = glogdt.reshape(1, G)
    return loss, dx.reshape(nb, S, D), big, small, accs1 + accs2 + accs3


def _view2d(a):
    return a.reshape(-1, a.shape[-1]) if a.ndim > 1 else a.reshape(1, -1)


def kernel(x, c, positions, ada_w, ada_b, norm1_g, w_in, ssm_lambda_re, ssm_lambda_im, ssm_b_re, ssm_b_im, ssm_c_re, ssm_c_im, ssm_d, ssm_log_dt, w_glu, q_norm_g, w_uq, kv_norm_g, w_ukv, ssm_out_g, attn_out_g, w_out, norm2_g, w_ff1, w_ff2, final_ada_w, final_ada_b, final_norm_g, loss_target, m_ada_w, m_ada_b, m_norm1_g, m_w_in, m_ssm_lambda_re, m_ssm_lambda_im, m_ssm_b_re, m_ssm_b_im, m_ssm_c_re, m_ssm_c_im, m_ssm_d, m_ssm_log_dt, m_w_glu, m_q_norm_g, m_w_uq, m_kv_norm_g, m_w_ukv, m_ssm_out_g, m_attn_out_g, m_w_out, m_norm2_g, m_w_ff1, m_w_ff2, m_final_ada_w, m_final_ada_b, m_final_norm_g, v_ada_w, v_ada_b, v_norm1_g, v_w_in, v_ssm_lambda_re, v_ssm_lambda_im, v_ssm_b_re, v_ssm_b_im, v_ssm_c_re, v_ssm_c_im, v_ssm_d, v_ssm_log_dt, v_w_glu, v_q_norm_g, v_w_uq, v_kv_norm_g, v_w_ukv, v_ssm_out_g, v_attn_out_g, v_w_out, v_norm2_g, v_w_ff1, v_w_ff2, v_final_ada_w, v_final_ada_b, v_final_norm_g):
    args = dict(locals())
    names = list(inspect.signature(kernel).parameters)
    wnames = names[3:names.index("loss_target")]
    small_names = [nm for nm in wnames if nm not in GATHERED and nm not in TP]
    reduced_names = [nm for nm in small_names if nm not in ("ada_b", "final_ada_b")]
    w = {nm: args[nm] for nm in wnames}
    m = {nm: args["m_" + nm] for nm in wnames}
    v = {nm: args["v_" + nm] for nm in wnames}
    nb = x.shape[0]
    xi, yi, ci = lax.axis_index("x"), lax.axis_index("y"), lax.axis_index("c")
    chip, me = 2 * xi + yi, 4 * xi + 2 * yi + ci

    unslot = lambda nm, g: g.reshape(-1, g.shape[-1]) if nm in ROW_SHARDED else _unslots(g)
    early = [nm for nm in GATHERED if nm not in LATE]
    got = _gather_chips("gather_weights", [_view2d(w[nm]).astype(BF16) for nm in early], [c])
    wf = {nm: unslot(nm, g) for nm, g in zip(early, got)}
    for nm in small_names:
        wf[nm] = w[nm][0] if w[nm].ndim > 1 else w[nm]
    c_all = got[len(early)].reshape(8 * nb, D)

    na, nf = ada_w.shape[-1], final_ada_w.shape[-1]
    ada_b_s = lax.dynamic_slice(ada_b, (0, chip * na), (1, na))
    fada_b_s = lax.dynamic_slice(final_ada_b.reshape(1, -1), (0, chip * nf), (1, nf))
    cond_all, modcols = _mod_fwd(c_all, ada_w[0], ada_b_s, final_ada_w, fada_b_s)
    (mod_g,) = _gather_chips("gather_mod", [modcols])
    mine = lax.dynamic_slice(mod_g, (0, me * nb, 0), (4, nb, na + nf))
    modp = jnp.concatenate([mine[:, :, :na].transpose(1, 0, 2).reshape(nb, 6, D),
                            mine[:, :, na:].transpose(1, 0, 2).reshape(nb, 2, D)], axis=1)

    own_late = [_view2d(w[nm]).astype(BF16) for nm in LATE]
    late_gather, token = _split_start("gather_late", own_late,
                                      [jax.ShapeDtypeStruct((4,) + a.shape, a.dtype) for a in own_late],
                                      3 * len(LATE), _plan_to_chips, modp)
    modp = modp + token[0, 0]

    def late_weights(after):
        sent, landed = _split_wait(late_gather, after)
        return {nm: unslot(nm, lax.dynamic_update_slice(g, own[None], (chip, 0, 0)))
                for nm, g, own in zip(LATE, landed, sent)}

    cidx = ci.astype(jnp.int32).reshape(1)
    ahead = ["w_ff1", "w_ff2"]

    class Reducer:
        def start(self, gs):
            lands = [jax.ShapeDtypeStruct((4, g.shape[1] // 2, g.shape[2]), g.dtype) for g in gs]
            self.swap, tok = _split_start("grad_swap_ff", gs, lands, len(gs), _plan_swap_halves, modp)
            return tok

        def middle(self, after):
            gs, got = _split_wait(self.swap, after)
            sums = [_add_half(g, r, cidx, "grad_add_sibling_" + nm) for nm, g, r in zip(ahead, gs, got)]
            lands = [jax.ShapeDtypeStruct(s.shape, s.dtype) for s in sums]
            self.scatter, tok = _split_start("grad_scatter_ff", sums, lands, 3 * len(sums), _plan_scatter_chips, modp)
            return tok

        def finish(self, after):
            out = []
            for nm, s, l in zip(ahead, *_split_wait(self.scatter, after)):
                own = lax.dynamic_slice(s, (chip, 0, 0), (1,) + s.shape[1:])
                out.append(_add_chips(lax.dynamic_update_slice(l, own, (chip, 0, 0)), "grad_add_chips_" + nm))
            return out

    reducer = Reducer()
    loss_row, grad_x, big, small, dmodp = _local_step(x, positions, loss_target, modp, wf, late_weights, reducer)

    rest = [nm for nm in GATHERED if nm not in ahead]
    sizes = [small[nm].size for nm in reduced_names]
    pad = -sum(sizes) % 128
    packed = jnp.concatenate([small[nm].reshape(1, -1) for nm in reduced_names] + [jnp.zeros((1, pad), F32)],
                             axis=1).astype(BF16)
    swapped = _swap_halves([big[nm] for nm in rest], [dmodp.reshape(nb, 8 * D), packed, loss_row])
    chip_sums = [_add_half(big[nm], r, cidx, "grad_add_sibling_" + nm) for nm, r in zip(rest, swapped)]
    half_of = {nm: _add_chips(r, "grad_add_chips_" + nm) for nm, r in zip(rest, _scatter_chips(chip_sums))}
    half_of.update(zip(ahead, reducer.finish(grad_x)))
    halves = [half_of[nm] for nm in GATHERED]
    others = _join_halves(halves)
    grads = {}
    dmod_all = swapped[len(rest)].reshape(8 * nb, 8 * D)
    small_sum, loss_sum = _sum_devices(swapped[len(rest) + 1].reshape(8, -1), swapped[len(rest) + 2].reshape(8, -1))
    loss = jnp.sum(loss_sum)
    off = 0
    for nm, sz in zip(reduced_names, sizes):
        grads[nm] = small_sum[:, off:off + sz].reshape(small[nm].shape)
        off += sz

    dsl = jnp.concatenate([lax.dynamic_slice(dmod_all, (0, chip * na), (8 * nb, na)),
                           lax.dynamic_slice(dmod_all, (0, 6 * D + chip * nf), (8 * nb, nf))], axis=1)
    gw, gb = _mod_bwd(cond_all.T, dsl, dmod_all)
    grads["ada_w"], grads["final_ada_w"] = gw[:, :na], gw[:, na:]
    grads["ada_b"], grads["final_ada_b"] = gb[:, :6 * D], gb[:, 6 * D:]

    delta, new_m, new_v = {}, {}, {}
    for nm, mine_h, other_h in zip(GATHERED, halves, others):
        grads[nm], delta[nm], new_m[nm], new_v[nm] = _adamw_halves(
            _view2d(w[nm]), mine_h, other_h, _view2d(m[nm]), _view2d(v[nm]), cidx, "adamw_" + nm)
    for nm in TP:
        delta[nm], new_m[nm], new_v[nm] = _adamw(_view2d(w[nm]), grads[nm], _view2d(m[nm]), _view2d(v[nm]),
                                                  "adamw_" + nm)
    upd = _adamw_small([_view2d(w[nm]) for nm in small_names], [grads[nm] for nm in small_names],
                       [_view2d(m[nm]) for nm in small_names], [_view2d(v[nm]) for nm in small_names])
    k = len(small_names)
    for t, nm in enumerate(small_names):
        delta[nm], new_m[nm], new_v[nm] = upd[t], upd[k + t], upd[2 * k + t]

    outs = [grads, delta, new_m, new_v]
    return (loss, grad_x, *[d[nm].reshape(w[nm].shape) for d in outs for nm in wnames])
```

```python
import inspect
import math

import jax
import jax.numpy as jnp
from jax import lax
from jax.experimental import pallas as pl
from jax.experimental.pallas import tpu as pltpu

F32 = jnp.float32
BF16 = jnp.bfloat16

D = 1024
D_SSM = 512
G = 32
H = 16
P = 64
NST = G * P
D_ATTN = 512
NH = 8
QK_NOPE = 64
QK_ROPE = 32
V_HEAD = 64
HP = 128
Q_LORA = 384
KV_LORA = 256
IN_COLS = D_SSM + Q_LORA + KV_LORA + QK_ROPE
IN_PAD = 1280
D_FF = 4096
ROPE_BASE = 10000.0
EPS = 1e-6
ADAM_LR = 0.001
ADAM_B1 = 0.9
ADAM_B2 = 0.999
ADAM_EPS = 1e-08
ADAM_WD = 0.01
ADAM_STEP = 10
NEG = -1e30
VMEM_LIMIT = 60 << 20

MESH = pl.DeviceIdType.MESH
_VM = pl.BlockSpec(memory_space=pltpu.VMEM)
_ANY = pl.BlockSpec(memory_space=pl.ANY)

GATHERED = ["w_in", "w_glu", "w_uq", "w_ukv", "w_out", "w_ff1", "w_ff2"]
TP = ["ada_w", "final_ada_w"]
ROW_SHARDED = ("w_out", "w_ff2")
LATE = ["w_out", "w_ff1", "w_ff2"]


def _cp(sem=None, vmem=VMEM_LIMIT):
    kw = dict(vmem_limit_bytes=vmem)
    if sem is not None:
        kw["dimension_semantics"] = sem
    return pltpu.CompilerParams(**kw)


def _dot(a, b):
    return jnp.dot(a, b, preferred_element_type=F32)


def _dot_nt(a, b):
    return lax.dot_general(a, b, (((1,), (1,)), ((), ())), preferred_element_type=F32)


def _dot_tn(a, b):
    return lax.dot_general(a, b, (((0,), (0,)), ((), ())), preferred_element_type=F32)


def _rms(x, n):
    r = lax.rsqrt(jnp.sum(x * x, axis=-1, keepdims=True) * (1.0 / n) + EPS)
    return x * r, r


def _rms_bwd(dyg, xhat, r, n):
    return r * (dyg - xhat * (jnp.sum(dyg * xhat, axis=-1, keepdims=True) * (1.0 / n)))


def _sigmoid(x):
    return 1.0 / (1.0 + jnp.exp(-x))


_GK = math.sqrt(2.0 / math.pi)
_GC = 0.044715


def _gelu(y):
    t = jnp.tanh(_GK * (y + _GC * y * y * y))
    return 0.5 * y * (1.0 + t)


def _gelu_grad(y):
    t = jnp.tanh(_GK * (y + _GC * y * y * y))
    return 0.5 * (1.0 + t) + 0.5 * y * (1.0 - t * t) * _GK * (1.0 + 3.0 * _GC * y * y)


def _colsum(x):
    return jnp.sum(x, axis=0, keepdims=True)


def _roll(x, s):
    return pltpu.roll(x, s % x.shape[-1], x.ndim - 1)


def _mod_fwd(c_all, ada_w_s, ada_b_s, fada_w_s, fada_b_s):
    nseq = c_all.shape[0]
    na, nf = ada_w_s.shape[1], fada_w_s.shape[1]

    def body(c_ref, w_ref, b_ref, fw_ref, fb_ref, cond_ref, mod_ref):
        cv = c_ref[...]
        cond = cv * _sigmoid(cv)
        cond_ref[...] = cond
        cb = cond.astype(BF16)
        mod_ref[:, 0:na] = _dot(cb, w_ref[...].astype(BF16)) + b_ref[...]
        mod_ref[:, na:na + nf] = _dot(cb, fw_ref[...].astype(BF16)) + fb_ref[...]

    return pl.pallas_call(
        body, name="mod_fwd",
        out_shape=[jax.ShapeDtypeStruct((nseq, D), F32), jax.ShapeDtypeStruct((nseq, na + nf), F32)],
        in_specs=[_VM] * 5, out_specs=[_VM] * 2, compiler_params=_cp(),
    )(c_all, ada_w_s, ada_b_s, fada_w_s, fada_b_s)


def _mod_bwd(cond_t, dsl, dall):
    nseq, n = dsl.shape
    bc = 512

    def body(ct_ref, dm_ref, da_ref, gw_ref, gb_ref):
        ct = ct_ref[...]
        dm = dm_ref[...]
        acc = ct[:, 0:1] * dm[0:1, :]
        for b in range(1, nseq):
            acc = acc + ct[:, b:b + 1] * dm[b:b + 1, :]
        gw_ref[...] = acc

        @pl.when(pl.program_id(0) == 0)
        def _():
            gb_ref[...] = _colsum(da_ref[...])

    return pl.pallas_call(
        body, name="mod_bwd", grid=(n // bc,),
        out_shape=[jax.ShapeDtypeStruct((D, n), F32), jax.ShapeDtypeStruct((1, dall.shape[1]), F32)],
        in_specs=[_VM, pl.BlockSpec((nseq, bc), lambda i: (0, i)), _VM],
        out_specs=[pl.BlockSpec((D, bc), lambda i: (0, i)), pl.BlockSpec((1, dall.shape[1]), lambda i: (0, 0))],
        compiler_params=_cp(("arbitrary",)),
    )(cond_t, dsl, dall)


def _f1_fwd(x, modp, g1, w_in, S, tm):
    n = x.shape[0]
    tps = S // tm

    def body(x_ref, mod_ref, g_ref, w_ref, h_ref, proj_ref):
        xhat, _ = _rms(x_ref[...], D)
        h = (xhat * g_ref[...]) * (1.0 + mod_ref[0, 1:2, :]) + mod_ref[0, 0:1, :]
        hb = h.astype(BF16)
        h_ref[...] = hb
        proj_ref[...] = _dot(hb, w_ref[...])

    return pl.pallas_call(
        body, name="f1_fwd", grid=(n // tm,),
        out_shape=[jax.ShapeDtypeStruct((n, D), BF16), jax.ShapeDtypeStruct((n, IN_PAD), F32)],
        in_specs=[pl.BlockSpec((tm, D), lambda i: (i, 0)),
                  pl.BlockSpec((1, 8, D), lambda i: (i // tps, 0, 0)), _VM, _VM],
        out_specs=[pl.BlockSpec((tm, D), lambda i: (i, 0)), pl.BlockSpec((tm, IN_PAD), lambda i: (i, 0))],
        compiler_params=_cp(("parallel",)),
    )(x, modp, g1, w_in)


def _f1_bwd(du, dmla, dx1, x, modp, g1, w_in, S, tm):
    n = x.shape[0]
    tps = S // tm
    nb = n // S

    def body(du_ref, dm_ref, dx1_ref, x_ref, mod_ref, g_ref, w_ref, dx_ref, dproj_ref, accs_ref, accg_ref):
        i = pl.program_id(0)
        dproj = jnp.concatenate([du_ref[...], dm_ref[...]], axis=1).astype(BF16)
        dproj_ref[...] = dproj
        dh = _dot_nt(dproj, w_ref[...])
        xhat, r = _rms(x_ref[...], D)
        g = g_ref[...]
        dn = dh * (1.0 + mod_ref[0, 1:2, :])
        dx_ref[...] = dx1_ref[...] + _rms_bwd(dn * g, xhat, r, D)

        @pl.when(i % tps == 0)
        def _():
            accs_ref[...] = jnp.zeros_like(accs_ref)

        @pl.when(i == 0)
        def _():
            accg_ref[...] = jnp.zeros_like(accg_ref)

        accs_ref[0, 0:1, :] += _colsum(dh)
        accs_ref[0, 1:2, :] += _colsum(dh * (xhat * g))
        accg_ref[0:1, :] += _colsum(dn * xhat)

    return pl.pallas_call(
        body, name="f1_bwd", grid=(n // tm,),
        out_shape=[jax.ShapeDtypeStruct((n, D), F32), jax.ShapeDtypeStruct((n, IN_PAD), BF16),
                   jax.ShapeDtypeStruct((nb, 8, D), F32), jax.ShapeDtypeStruct((8, D), F32)],
        in_specs=[pl.BlockSpec((tm, D_SSM), lambda i: (i, 0)), pl.BlockSpec((tm, IN_PAD - D_SSM), lambda i: (i, 0)),
                  pl.BlockSpec((tm, D), lambda i: (i, 0)), pl.BlockSpec((tm, D), lambda i: (i, 0)),
                  pl.BlockSpec((1, 8, D), lambda i: (i // tps, 0, 0)), _VM, _VM],
        out_specs=[pl.BlockSpec((tm, D), lambda i: (i, 0)), pl.BlockSpec((tm, IN_PAD), lambda i: (i, 0)),
                   pl.BlockSpec((1, 8, D), lambda i: (i // tps, 0, 0)), pl.BlockSpec((8, D), lambda i: (0, 0))],
        compiler_params=_cp(("arbitrary",)),
    )(du, dmla, dx1, x, modp, g1, w_in)


def _ssm_param_fwd(lam_re, lam_im, logdt, b_re, b_im):
    def body(lr_ref, li_ref, ld_ref, br_ref, bi_ref, lbr_ref, lbi_ref, bbr_ref, bbi_ref):
        lr, li = lr_ref[...], li_ref[...]
        dt = jnp.exp(ld_ref[...])
        er = jnp.exp(lr * dt)
        lbr = er * jnp.cos(li * dt)
        lbi = er * jnp.sin(li * dt)
        den = 1.0 / (lr * lr + li * li)
        cr = ((lbr - 1.0) * lr + lbi * li) * den
        ci = (lbi * lr - (lbr - 1.0) * li) * den
        lbr_ref[...] = lbr
        lbi_ref[...] = lbi
        bbr_ref[...] = cr * br_ref[...] - ci * bi_ref[...]
        bbi_ref[...] = cr * bi_ref[...] + ci * br_ref[...]

    return pl.pallas_call(
        body, name="ssm_param_fwd",
        out_shape=[jax.ShapeDtypeStruct((NST, 1), F32)] * 2 + [jax.ShapeDtypeStruct((NST, H), F32)] * 2,
        in_specs=[_VM] * 5, out_specs=[_VM] * 4, compiler_params=_cp(),
    )(lam_re, lam_im, logdt, b_re, b_im)


def _ssm_param_bwd(lam_re, lam_im, logdt, b_re, b_im, dlb_re, dlb_im, dbb_re, dbb_im):
    def body(lr_ref, li_ref, ld_ref, br_ref, bi_ref, dlr_ref, dli_ref, dbr_ref, dbi_ref,
             gbr_ref, gbi_ref, glr_ref, gli_ref, gdt_ref):
        lr, li = lr_ref[...], li_ref[...]
        dt = jnp.exp(ld_ref[...])
        er = jnp.exp(lr * dt)
        lbr = er * jnp.cos(li * dt)
        lbi = er * jnp.sin(li * dt)
        den = 1.0 / (lr * lr + li * li)
        nr, ni = lbr - 1.0, lbi
        cr = (nr * lr + ni * li) * den
        ci = (ni * lr - nr * li) * den
        br, bi = br_ref[...], bi_ref[...]
        dbr, dbi = dbr_ref[...], dbi_ref[...]
        gbr_ref[...] = cr * dbr + ci * dbi
        gbi_ref[...] = cr * dbi - ci * dbr
        gcr = jnp.sum(dbr * br + dbi * bi, axis=1, keepdims=True)
        gci = jnp.sum(dbi * br - dbr * bi, axis=1, keepdims=True)
        ilr, ili = lr * den, -li * den
        glbr = dlr_ref[...] + (gcr * ilr + gci * ili)
        glbi = dli_ref[...] + (gci * ilr - gcr * ili)
        qr = -(cr * ilr - ci * ili)
        qi = -(cr * ili + ci * ilr)
        glr = gcr * qr + gci * qi
        gli = gci * qr - gcr * qi
        glr = glr + dt * (glbr * lbr + glbi * lbi)
        gli = gli + dt * (glbi * lbr - glbr * lbi)
        wr = lr * lbr - li * lbi
        wi = lr * lbi + li * lbr
        glr_ref[...] = glr
        gli_ref[...] = gli
        gdt_ref[...] = (glbr * wr + glbi * wi) * dt

    return pl.pallas_call(
        body, name="ssm_param_bwd",
        out_shape=[jax.ShapeDtypeStruct((NST, H), F32)] * 2 + [jax.ShapeDtypeStruct((NST, 1), F32)] * 3,
        in_specs=[_VM] * 9, out_specs=[_VM] * 5, compiler_params=_cp(),
    )(lam_re, lam_im, logdt, b_re, b_im, dlb_re, dlb_im, dbb_re, dbb_im)


def _rowsum(a):
    def body(a_ref, o_ref):
        o_ref[...] = jnp.sum(a_ref[...], axis=1, keepdims=True)

    return pl.pallas_call(
        body, name="rowsum", out_shape=jax.ShapeDtypeStruct((a.shape[0], 1), F32),
        in_specs=[_VM], out_specs=_VM, compiler_params=_cp(),
    )(a)


QB = D_SSM // 4
QS = 4 * QB


def _bd_lo(part, q):
    return part * NST + q * QS


def _bd_expand(ub, bm_ref, out_ref):
    for part in range(2):
        for q in range(4):
            lo = _bd_lo(part, q)
            out_ref[:, lo:lo + QS] = _dot(ub[:, q * QB:(q + 1) * QB], bm_ref[:, lo:lo + QS])


def _bd_expand_t(db, cm_ref, out_ref):
    for part in range(2):
        for q in range(4):
            lo = _bd_lo(part, q)
            out_ref[:, lo:lo + QS] = _dot_nt(db[:, q * QB:(q + 1) * QB], cm_ref[lo:lo + QS, :])


def _bd_project(sb, cm_ref):
    return jnp.concatenate(
        [_dot(sb[:, _bd_lo(0, q):_bd_lo(0, q) + QS], cm_ref[_bd_lo(0, q):_bd_lo(0, q) + QS, :])
         + _dot(sb[:, _bd_lo(1, q):_bd_lo(1, q) + QS], cm_ref[_bd_lo(1, q):_bd_lo(1, q) + QS, :])
         for q in range(4)], axis=1)


def _bd_project_t(ab, bm_ref):
    return jnp.concatenate(
        [_dot_nt(ab[:, _bd_lo(0, q):_bd_lo(0, q) + QS], bm_ref[:, _bd_lo(0, q):_bd_lo(0, q) + QS])
         + _dot_nt(ab[:, _bd_lo(1, q):_bd_lo(1, q) + QS], bm_ref[:, _bd_lo(1, q):_bd_lo(1, q) + QS])
         for q in range(4)], axis=1)


def _pow2k(pr, pi, nsq):
    for _ in range(nsq):
        pr, pi = pr * pr - pi * pi, 2.0 * pr * pi
    return pr, pi


def _ssm_local(u_p, bm, lre8, lim8, S, tt):
    n = u_p.shape[0]
    nb, nt = n // S, S // tt
    nsq = int(round(math.log2(S // 8)))
    assert 2 ** nsq == S // 8

    def body(u_ref, bm_ref, lre_ref, lim_ref, cre_ref, cim_ref, sre, sim, bu):
        j = pl.program_id(1)

        @pl.when(j == 0)
        def _():
            sre[...] = jnp.zeros_like(sre)
            sim[...] = jnp.zeros_like(sim)

        _bd_expand(u_ref[...].astype(BF16), bm_ref, bu)
        lre, lim = lre_ref[...], lim_ref[...]

        def step(i, c):
            sr, si = c
            off = pl.multiple_of(i * 8, 8)
            br = bu[pl.ds(off, 8), 0:NST]
            bi = bu[pl.ds(off, 8), NST:2 * NST]
            return lre * sr - lim * si + br, lre * si + lim * sr + bi

        sr, si = lax.fori_loop(0, tt // 8, step, (sre[...], sim[...]))
        sre[...] = sr
        sim[...] = si

        @pl.when(j == nt - 1)
        def _():
            pr, pi = _pow2k(lre[0:1], lim[0:1], nsq)
            cr = jnp.zeros((1, NST), F32)
            ci = jnp.zeros((1, NST), F32)
            cre_ref[0:1, :] = cr
            cim_ref[0:1, :] = ci
            for k in range(1, 8):
                cr, ci = sr[k - 1:k] + pr * cr - pi * ci, si[k - 1:k] + pr * ci + pi * cr
                cre_ref[k:k + 1, :] = cr
                cim_ref[k:k + 1, :] = ci

    return pl.pallas_call(
        body, name="ssm_local", grid=(nb, nt),
        out_shape=[jax.ShapeDtypeStruct((nb * 8, NST), F32)] * 2,
        in_specs=[pl.BlockSpec((tt, D_SSM), lambda b, j: (b * nt + j, 0)), _VM, _VM, _VM],
        out_specs=[pl.BlockSpec((8, NST), lambda b, j: (b, 0))] * 2,
        scratch_shapes=[pltpu.VMEM((8, NST), F32), pltpu.VMEM((8, NST), F32), pltpu.VMEM((tt, 2 * NST), F32)],
        compiler_params=_cp(("arbitrary", "arbitrary")),
    )(u_p, bm, lre8, lim8)


def _ssm_fwd(u_p, cre, cim, bm, cm, dvec, w_glu, lre8, lim8, S, tt):
    n = u_p.shape[0]
    nb, nt = n // S, S // tt

    def body(u_ref, cre_ref, cim_ref, bm_ref, cm_ref, d_ref, wg_ref, lre_ref, lim_ref,
             st_ref, ypre_ref, z_ref, gact_ref, yssm_ref, sre, sim, bu):
        j = pl.program_id(1)

        @pl.when(j == 0)
        def _():
            sre[...] = cre_ref[...]
            sim[...] = cim_ref[...]

        u = u_ref[...]
        _bd_expand(u.astype(BF16), bm_ref, bu)
        lre, lim = lre_ref[...], lim_ref[...]

        def step(i, c):
            sr, si = c
            off = pl.multiple_of(i * 8, 8)
            nr = lre * sr - lim * si + bu[pl.ds(off, 8), 0:NST]
            ni = lre * si + lim * sr + bu[pl.ds(off, 8), NST:2 * NST]
            bu[pl.ds(off, 8), 0:NST] = nr
            bu[pl.ds(off, 8), NST:2 * NST] = ni
            return nr, ni

        sr, si = lax.fori_loop(0, tt // 8, step, (sre[...], sim[...]))
        sre[...] = sr
        sim[...] = si
        stb = bu[...].astype(BF16)
        st_ref[...] = stb
        y = _bd_project(stb, cm_ref) + d_ref[...] * u
        ypre_ref[...] = y
        gb = _gelu(y).astype(BF16)
        gact_ref[...] = gb
        z = _dot(gb, wg_ref[...])
        z_ref[...] = z
        yssm_ref[...] = z[:, 0:D_SSM] * _sigmoid(z[:, D_SSM:2 * D_SSM])

    row = lambda w: pl.BlockSpec((tt, w), lambda b, j: (b * nt + j, 0))
    return pl.pallas_call(
        body, name="ssm_fwd", grid=(nb, nt),
        out_shape=[jax.ShapeDtypeStruct((n, 2 * NST), BF16), jax.ShapeDtypeStruct((n, D_SSM), F32),
                   jax.ShapeDtypeStruct((n, 2 * D_SSM), F32), jax.ShapeDtypeStruct((n, D_SSM), BF16),
                   jax.ShapeDtypeStruct((n, D_SSM), F32)],
        in_specs=[row(D_SSM), pl.BlockSpec((8, NST), lambda b, j: (b, 0)), pl.BlockSpec((8, NST), lambda b, j: (b, 0)),
                  _VM, _VM, _VM, _VM, _VM, _VM],
        out_specs=[row(2 * NST), row(D_SSM), row(2 * D_SSM), row(D_SSM), row(D_SSM)],
        scratch_shapes=[pltpu.VMEM((8, NST), F32), pltpu.VMEM((8, NST), F32), pltpu.VMEM((tt, 2 * NST), F32)],
        compiler_params=_cp(("arbitrary", "arbitrary")),
    )(u_p, cre, cim, bm, cm, dvec, w_glu, lre8, lim8)


def _ssm_bwd_a(dys_p, z, ypre, w_glu, cm, lre8, lim8, S, tt):
    n = z.shape[0]
    nb, nt = n // S, S // tt
    nsq = int(round(math.log2(S // 8)))
    ng = tt // 8

    def body(dys_ref, z_ref, y_ref, wg_ref, cm_ref, lre_ref, lim_ref, dy_ref, dz_ref, are_ref, aim_ref, sre, sim, gb):
        j = pl.program_id(1)

        @pl.when(j == 0)
        def _():
            sre[...] = jnp.zeros_like(sre)
            sim[...] = jnp.zeros_like(sim)

        z = z_ref[...]
        z1, z2 = z[:, 0:D_SSM], z[:, D_SSM:2 * D_SSM]
        sg = _sigmoid(z2)
        dys = dys_ref[...]
        dz = jnp.concatenate([dys * sg, dys * z1 * sg * (1.0 - sg)], axis=1).astype(BF16)
        dz_ref[...] = dz
        dy = _dot_nt(dz, wg_ref[...]) * _gelu_grad(y_ref[...])
        dy_ref[...] = dy
        _bd_expand_t(dy.astype(BF16), cm_ref, gb)
        lre, lim = lre_ref[...], lim_ref[...]

        def step(i, c):
            ar, ai = c
            off = pl.multiple_of((ng - 1 - i) * 8, 8)
            gr = gb[pl.ds(off, 8), 0:NST]
            gi = gb[pl.ds(off, 8), NST:2 * NST]
            return lre * ar + lim * ai + gr, lre * ai - lim * ar + gi

        ar, ai = lax.fori_loop(0, ng, step, (sre[...], sim[...]))
        sre[...] = ar
        sim[...] = ai

        @pl.when(j == nt - 1)
        def _():
            pr, pi = _pow2k(lre[0:1], -lim[0:1], nsq)
            cr = jnp.zeros((1, NST), F32)
            ci = jnp.zeros((1, NST), F32)
            are_ref[7:8, :] = cr
            aim_ref[7:8, :] = ci
            for k in range(6, -1, -1):
                cr, ci = ar[k + 1:k + 2] + pr * cr - pi * ci, ai[k + 1:k + 2] + pr * ci + pi * cr
                are_ref[k:k + 1, :] = cr
                aim_ref[k:k + 1, :] = ci

    row = lambda w: pl.BlockSpec((tt, w), lambda b, j: (b * nt + nt - 1 - j, 0))
    return pl.pallas_call(
        body, name="ssm_bwd_a", grid=(nb, nt),
        out_shape=[jax.ShapeDtypeStruct((n, D_SSM), F32), jax.ShapeDtypeStruct((n, 2 * D_SSM), BF16),
                   jax.ShapeDtypeStruct((nb * 8, NST), F32), jax.ShapeDtypeStruct((nb * 8, NST), F32)],
        in_specs=[row(D_SSM), row(2 * D_SSM), row(D_SSM), _VM, _VM, _VM, _VM],
        out_specs=[row(D_SSM), row(2 * D_SSM), pl.BlockSpec((8, NST), lambda b, j: (b, 0)),
                   pl.BlockSpec((8, NST), lambda b, j: (b, 0))],
        scratch_shapes=[pltpu.VMEM((8, NST), F32), pltpu.VMEM((8, NST), F32), pltpu.VMEM((tt, 2 * NST), F32)],
        compiler_params=_cp(("arbitrary", "arbitrary")),
    )(dys_p, z, ypre, w_glu, cm, lre8, lim8)


def _ssm_bwd_b(dy, u_p, st, fcr, fci, air, aii, bm, cm, dvec, lre8, lim8, S, tt):
    n = u_p.shape[0]
    nb, nt = n // S, S // tt
    ng = tt // 8

    def body(dy_ref, u_ref, st_ref, stp_ref, fcr_ref, fci_ref, air_ref, aii_ref, bm_ref, cm_ref, d_ref, lre_ref, lim_ref,
             du_ref, dcm_ref, dbm_ref, dd_ref, dlr_ref, dli_ref, are, aim, accr, acci, sp, ab):
        b = pl.program_id(0)
        j = pl.program_id(1)
        jt = nt - 1 - j

        @pl.when((b == 0) & (j == 0))
        def _():
            dcm_ref[...] = jnp.zeros_like(dcm_ref)
            dbm_ref[...] = jnp.zeros_like(dbm_ref)
            dd_ref[...] = jnp.zeros_like(dd_ref)
            accr[...] = jnp.zeros_like(accr)
            acci[...] = jnp.zeros_like(acci)

        @pl.when(j == 0)
        def _():
            are[...] = air_ref[...]
            aim[...] = aii_ref[...]

        sp[8:tt + 8, :] = st_ref[...].astype(F32)

        @pl.when(jt == 0)
        def _():
            sp[0:8, 0:NST] = fcr_ref[...]
            sp[0:8, NST:2 * NST] = fci_ref[...]

        @pl.when(jt != 0)
        def _():
            sp[0:8, :] = stp_ref[8:16, :].astype(F32)

        dy = dy_ref[...]
        u = u_ref[...]
        dyb = dy.astype(BF16)
        _bd_expand_t(dyb, cm_ref, ab)
        lre, lim = lre_ref[...], lim_ref[...]

        def step(i, c):
            ar, ai = c
            off = pl.multiple_of((ng - 1 - i) * 8, 8)
            nr = lre * ar + lim * ai + ab[pl.ds(off, 8), 0:NST]
            ni = lre * ai - lim * ar + ab[pl.ds(off, 8), NST:2 * NST]
            ab[pl.ds(off, 8), 0:NST] = nr
            ab[pl.ds(off, 8), NST:2 * NST] = ni
            pr = sp[pl.ds(off, 8), 0:NST]
            pi = sp[pl.ds(off, 8), NST:2 * NST]
            accr[...] += nr * pr + ni * pi
            acci[...] += ni * pr - nr * pi
            return nr, ni

        ar, ai = lax.fori_loop(0, ng, step, (are[...], aim[...]))
        are[...] = ar
        aim[...] = ai
        a_b = ab[...].astype(BF16)
        du_ref[...] = _bd_project_t(a_b, bm_ref) + d_ref[...] * dy
        ub = u.astype(BF16)
        for q in range(4):
            for part in range(2):
                lo = part * NST + q * 4 * QB
                s_q = st_ref[:, lo:lo + 4 * QB]
                dcm_ref[lo:lo + 4 * QB, :] += _dot_tn(s_q, dyb[:, q * QB:(q + 1) * QB])
                dbm_ref[:, lo:lo + 4 * QB] += _dot_tn(ub[:, q * QB:(q + 1) * QB], a_b[:, lo:lo + 4 * QB])
        dd_ref[...] += _colsum(dy * u)

        @pl.when((b == nb - 1) & (j == nt - 1))
        def _():
            dlr_ref[...] = _colsum(accr[...])
            dli_ref[...] = _colsum(acci[...])

    row = lambda w: pl.BlockSpec((tt, w), lambda b, j: (b * nt + nt - 1 - j, 0))
    seq8 = pl.BlockSpec((8, NST), lambda b, j: (b, 0))
    prev = pl.BlockSpec((16, 2 * NST), lambda b, j: (jnp.maximum((b * nt + nt - 1 - j) * (tt // 16) - 1, 0), 0))
    const = lambda shape: pl.BlockSpec(shape, lambda b, j: (0, 0))
    return pl.pallas_call(
        body, name="ssm_bwd_b", grid=(nb, nt),
        out_shape=[jax.ShapeDtypeStruct((n, D_SSM), F32), jax.ShapeDtypeStruct((2 * NST, QB), F32),
                   jax.ShapeDtypeStruct((QB, 2 * NST), F32), jax.ShapeDtypeStruct((1, D_SSM), F32),
                   jax.ShapeDtypeStruct((1, NST), F32), jax.ShapeDtypeStruct((1, NST), F32)],
        in_specs=[row(D_SSM), row(D_SSM), row(2 * NST), prev, seq8, seq8, seq8, seq8, _VM, _VM, _VM, _VM, _VM],
        out_specs=[row(D_SSM), const((2 * NST, QB)), const((QB, 2 * NST)), const((1, D_SSM)),
                   const((1, NST)), const((1, NST))],
        scratch_shapes=[pltpu.VMEM((8, NST), F32)] * 4 + [pltpu.VMEM((tt + 8, 2 * NST), F32),
                                                          pltpu.VMEM((tt, 2 * NST), F32)],
        compiler_params=_cp(("arbitrary", "arbitrary")),
    )(dy, u_p, st, st, fcr, fci, air, aii, bm, cm, dvec, lre8, lim8)


def _rope(v, c, s1, s2):
    return v * c + _roll(v, -16) * s1 + _roll(v, 16) * s2


def _rope_t(dv, c, s1, s2):
    return dv * c + _roll(dv * s1, 16) + _roll(dv * s2, -16)


def _mla_fwd(proj, rc, rs1, rs2, gq, gkv, w_uq, w_ukv, tm):
    n = proj.shape[0]

    def body(ql_ref, kvl_ref, kr_ref, c_ref, s1_ref, s2_ref, gq_ref, gkv_ref, wq_ref, wkv_ref,
             q_ref, k_ref, v_ref, qn_ref, kvn_ref):
        c, s1, s2 = c_ref[...], s1_ref[...], s2_ref[...]
        qhat, _ = _rms(ql_ref[...], Q_LORA)
        qn = (qhat * gq_ref[...]).astype(BF16)
        qn_ref[...] = qn
        q = _dot(qn, wq_ref[...])
        qr = _rope(q, jnp.tile(c, (1, NH)), jnp.tile(s1, (1, NH)), jnp.tile(s2, (1, NH)))
        q_ref[...] = (qr * _C2).astype(BF16)
        khat, _ = _rms(kvl_ref[...], KV_LORA)
        kvn = (khat * gkv_ref[...]).astype(BF16)
        kvn_ref[...] = kvn
        kv = _dot(kvn, wkv_ref[...])
        kr = _rope(_roll(kr_ref[...], 64), c, s1, s2)
        k_ref[...] = (kv[:, 0:NH * HP] + jnp.tile(kr, (1, NH))).astype(BF16)
        v_ref[...] = kv[:, NH * HP:2 * NH * HP].astype(BF16)

    def wrapped(proj_ref, *rest):
        ql = proj_ref.at[:, D_SSM:D_SSM + Q_LORA]
        kvl = proj_ref.at[:, D_SSM + Q_LORA:D_SSM + Q_LORA + KV_LORA]
        kr = proj_ref.at[:, IN_PAD - HP:IN_PAD]
        body(ql, kvl, kr, *rest)

    row = lambda w: pl.BlockSpec((tm, w), lambda i: (i, 0))
    return pl.pallas_call(
        wrapped, name="mla_fwd", grid=(n // tm,),
        out_shape=[jax.ShapeDtypeStruct((n, NH * HP), BF16)] * 3 +
                  [jax.ShapeDtypeStruct((n, Q_LORA), BF16), jax.ShapeDtypeStruct((n, KV_LORA), BF16)],
        in_specs=[row(IN_PAD), row(HP), row(HP), row(HP), _VM, _VM, _VM, _VM],
        out_specs=[row(NH * HP)] * 3 + [row(Q_LORA), row(KV_LORA)],
        compiler_params=_cp(("parallel",)),
    )(proj, rc, rs1, rs2, gq, gkv, w_uq, w_ukv)


def _mla_bwd(dq, dk, dv, proj, rc, rs1, rs2, gq, gkv, w_uq, w_ukv, tm):
    n = proj.shape[0]

    def body(dq_ref, dk_ref, dv_ref, proj_ref, c_ref, s1_ref, s2_ref, gq_ref, gkv_ref, wq_ref, wkv_ref,
             dmla_ref, dqb_ref, dkvb_ref, acc_ref):
        i = pl.program_id(0)
        c, s1, s2 = c_ref[...], s1_ref[...], s2_ref[...]
        dqu = _rope_t(dq_ref[...] * _SCALE, jnp.tile(c, (1, NH)), jnp.tile(s1, (1, NH)),
                      jnp.tile(s2, (1, NH))).astype(BF16)
        dqb_ref[...] = dqu
        dqn = _dot_nt(dqu, wq_ref[...])
        qhat, rq = _rms(proj_ref[:, D_SSM:D_SSM + Q_LORA], Q_LORA)
        dql = _rms_bwd(dqn * gq_ref[...], qhat, rq, Q_LORA)
        dkf = dk_ref[...] * (1.0 / _LOG2E)
        dkv = jnp.concatenate([dkf.astype(BF16), dv_ref[...].astype(BF16)], axis=1)
        dkvb_ref[...] = dkv
        dkvn = _dot_nt(dkv, wkv_ref[...])
        khat, rk = _rms(proj_ref[:, D_SSM + Q_LORA:D_SSM + Q_LORA + KV_LORA], KV_LORA)
        dkvl = _rms_bwd(dkvn * gkv_ref[...], khat, rk, KV_LORA)
        dkr = dkf[:, 0:HP]
        for h in range(1, NH):
            dkr = dkr + dkf[:, h * HP:(h + 1) * HP]
        lane = lax.broadcasted_iota(jnp.int32, dkr.shape, 1)
        dkr = jnp.where((lane >= QK_NOPE) & (lane < QK_NOPE + QK_ROPE), dkr, 0.0)
        dkr = _roll(_rope_t(dkr, c, s1, s2), -64)
        dmla_ref[...] = jnp.concatenate([dql, dkvl, dkr], axis=1)

        @pl.when(i == 0)
        def _():
            acc_ref[...] = jnp.zeros_like(acc_ref)

        acc_ref[0:1, 0:Q_LORA] += _colsum(dqn * qhat)
        acc_ref[1:2, 0:KV_LORA] += _colsum(dkvn * khat)

    row = lambda w: pl.BlockSpec((tm, w), lambda i: (i, 0))
    return pl.pallas_call(
        body, name="mla_bwd", grid=(n // tm,),
        out_shape=[jax.ShapeDtypeStruct((n, IN_PAD - D_SSM), F32), jax.ShapeDtypeStruct((n, NH * HP), BF16),
                   jax.ShapeDtypeStruct((n, 2 * NH * HP), BF16), jax.ShapeDtypeStruct((8, Q_LORA), F32)],
        in_specs=[row(NH * HP)] * 3 + [row(IN_PAD), row(HP), row(HP), row(HP), _VM, _VM, _VM, _VM],
        out_specs=[row(IN_PAD - D_SSM), row(NH * HP), row(2 * NH * HP), pl.BlockSpec((8, Q_LORA), lambda i: (0, 0))],
        compiler_params=_cp(("arbitrary",)),
    )(dq, dk, dv, proj, rc, rs1, rs2, gq, gkv, w_uq, w_ukv)


_SCALE = (QK_NOPE + QK_ROPE) ** -0.5
_LOG2E = 1.4426950408889634
_C2 = _SCALE * _LOG2E


def _attn_fwd(q, k, v, S, tq):
    n = q.shape[0]
    nb, nq = n // S, S // tq

    def body(q_ref, k_ref, v_ref, o_ref, lr_ref):
        qi = pl.program_id(2)
        qv = q_ref[...]

        def tile(j, c, diagonal):
            m, l, acc = c
            off = pl.multiple_of(j * tq, tq)
            s = _dot_nt(qv, k_ref[pl.ds(off, tq), :])
            if diagonal:
                rows = lax.broadcasted_iota(jnp.int32, s.shape, 0)
                cols = lax.broadcasted_iota(jnp.int32, s.shape, 1)
                s = jnp.where(cols <= rows, s, NEG)
            mn = jnp.maximum(m, jnp.max(s, axis=1, keepdims=True))
            p = jnp.exp2(s - mn)
            al = jnp.exp2(m - mn)
            l = al * l + jnp.sum(p, axis=1, keepdims=True)
            acc = al * acc + _dot(p.astype(BF16), v_ref[pl.ds(off, tq), :])
            return mn, l, acc

        init = (jnp.full((tq, 1), NEG, F32), jnp.zeros((tq, 1), F32), jnp.zeros((tq, HP), F32))
        c = lax.fori_loop(0, qi, lambda j, c: tile(j, c, False), init)
        m, l, acc = tile(qi, c, True)
        o_ref[...] = (acc / l).astype(BF16)
        lane = lax.broadcasted_iota(jnp.int32, (8, HP), 1)
        lse = jnp.broadcast_to(m + jnp.log(l) * _LOG2E, (tq, HP))
        lr_ref[...] = _rows_of(lse, jnp.where(lane == 0, 1.0, 0.0).astype(BF16))

    qs = pl.BlockSpec((tq, HP), lambda b, h, i: (b * nq + i, h))
    ks = pl.BlockSpec((S, HP), lambda b, h, i: (b, h))
    return pl.pallas_call(
        body, name="attn_fwd", grid=(nb, NH, nq),
        out_shape=[jax.ShapeDtypeStruct((n, NH * HP), BF16), jax.ShapeDtypeStruct((nb * NH * 8, S), F32)],
        in_specs=[qs, ks, ks], out_specs=[qs, pl.BlockSpec((8, tq), lambda b, h, i: (b * NH + h, i))],
        compiler_params=_cp(("parallel", "parallel", "arbitrary")),
    )(q, k, v)


def _rows_of(x, pick):
    x1 = x.astype(BF16)
    r1 = x - x1.astype(F32)
    x2 = r1.astype(BF16)
    x3 = (r1 - x2.astype(F32)).astype(BF16)
    return _dot_nt(pick, x1) + _dot_nt(pick, x2) + _dot_nt(pick, x3)


def _attn_bwd(q, k, v, dob, lrow, drow, S, tq):
    n = q.shape[0]
    nb, nq = n // S, S // tq

    def body(q_ref, k_ref, v_ref, do_ref, lr_ref, dr_ref, dqo_ref, dk_ref, dv_ref, dq_ref):
        kj = pl.program_id(2)

        @pl.when(kj == 0)
        def _():
            dq_ref[...] = jnp.zeros_like(dq_ref)

        kt = k_ref[...]
        vt = v_ref[...]

        def tile(i, c, diagonal):
            dk, dv = c
            off = pl.multiple_of(i * tq, tq)
            qv = q_ref[pl.ds(off, tq), :]
            dob = do_ref[pl.ds(off, tq), :]
            lr = lr_ref[0:1, pl.ds(off, tq)]
            dr = dr_ref[0:1, pl.ds(off, tq)]
            st = _dot_nt(kt, qv)
            dpt = _dot_nt(vt, dob)
            pt = jnp.exp2(st - lr)
            if diagonal:
                keys = lax.broadcasted_iota(jnp.int32, pt.shape, 0)
                qrys = lax.broadcasted_iota(jnp.int32, pt.shape, 1)
                pt = jnp.where(keys <= qrys, pt, 0.0)
            dst = (pt * (dpt - dr)).astype(BF16)
            dq_ref[pl.ds(off, tq), :] += _dot_tn(dst, kt)
            return dk + _dot(dst, qv), dv + _dot(pt.astype(BF16), dob)

        zero = jnp.zeros((tq, HP), F32)
        c = tile(kj, (zero, zero), True)
        dk, dv = lax.fori_loop(kj + 1, nq, lambda i, c: tile(i, c, False), c)
        dk_ref[...] = dk.astype(BF16)
        dv_ref[...] = dv.astype(BF16)

        @pl.when(kj == nq - 1)
        def _():
            dqo_ref[...] = dq_ref[...].astype(BF16)

    ts = pl.BlockSpec((tq, HP), lambda b, h, i: (b * nq + i, h))
    fs = pl.BlockSpec((S, HP), lambda b, h, i: (b, h))
    rs = pl.BlockSpec((8, S), lambda b, h, i: (b * NH + h, 0))
    return pl.pallas_call(
        body, name="attn_bwd", grid=(nb, NH, nq),
        out_shape=[jax.ShapeDtypeStruct((n, NH * HP), BF16)] * 3,
        in_specs=[fs, ts, ts, fs, rs, rs], out_specs=[fs, ts, ts],
        scratch_shapes=[pltpu.VMEM((S, HP), F32)],
        compiler_params=_cp(("parallel", "parallel", "arbitrary")),
    )(q, k, v, dob, lrow, drow)


def _p1_fwd(yssm, oattn, x, modp, gs, ga, w_out, g2, S, tm):
    n = x.shape[0]
    tps = S // tm

    def body(ys_ref, oa_ref, x_ref, mod_ref, gs_ref, ga_ref, w_ref, g2_ref, yn_ref, o_ref, x1_ref, h2_ref):
        yh, _ = _rms(ys_ref[...], D_SSM)
        ah, _ = _rms(oa_ref[...].astype(F32), D_ATTN)
        yn = jnp.concatenate([yh * gs_ref[...], ah * ga_ref[...]], axis=1).astype(BF16)
        yn_ref[...] = yn
        o = _dot(yn, w_ref[...])
        o_ref[...] = o.astype(BF16)
        x1 = x_ref[...] + mod_ref[0, 2:3, :] * o
        x1_ref[...] = x1
        xh, _ = _rms(x1, D)
        h2_ref[...] = ((xh * g2_ref[...]) * (1.0 + mod_ref[0, 4:5, :]) + mod_ref[0, 3:4, :]).astype(BF16)

    row = lambda w: pl.BlockSpec((tm, w), lambda i: (i, 0))
    return pl.pallas_call(
        body, name="p1_fwd", grid=(n // tm,),
        out_shape=[jax.ShapeDtypeStruct((n, D_SSM + NH * HP), BF16), jax.ShapeDtypeStruct((n, D), BF16),
                   jax.ShapeDtypeStruct((n, D), F32), jax.ShapeDtypeStruct((n, D), BF16)],
        in_specs=[row(D_SSM), row(NH * HP), row(D), pl.BlockSpec((1, 8, D), lambda i: (i // tps, 0, 0)),
                  _VM, _VM, _VM, _VM],
        out_specs=[row(D_SSM + NH * HP), row(D), row(D), row(D)],
        compiler_params=_cp(("parallel",)),
    )(yssm, oattn, x, modp, gs, ga, w_out, g2)


def _p2(x1, h2, target, modp, g2, gf, w_ff1, w_ff2, S, tm):
    n = x1.shape[0]
    tps = S // tm
    nb = n // S

    def body(x1_ref, h2_ref, t_ref, mod_ref, g2_ref, gf_ref, w1_ref, w2_ref,
             dx1_ref, r_ref, da_ref, dff_ref, accs_ref, accg_ref):
        i = pl.program_id(0)
        sh2, sc2, gt2 = mod_ref[0, 3:4, :], mod_ref[0, 4:5, :], mod_ref[0, 5:6, :]
        fsh, fsc = mod_ref[0, 6:7, :], mod_ref[0, 7:8, :]
        x1 = x1_ref[...]
        a = _dot(h2_ref[...], w1_ref[...])
        ra = jnp.maximum(a, 0.0)
        rb = (ra * ra).astype(BF16)
        r_ref[...] = rb
        ff = _dot(rb, w2_ref[...])
        x2 = x1 + gt2 * ff
        x2h, rf = _rms(x2, D)
        gf_v = gf_ref[...]
        outn = x2h * gf_v
        err = outn * (1.0 + fsc) + fsh - t_ref[...]
        dout = err * (1.0 / D)
        doutn = dout * (1.0 + fsc)
        dx2 = _rms_bwd(doutn * gf_v, x2h, rf, D)
        dff = (gt2 * dx2).astype(BF16)
        dff_ref[...] = dff
        dr = _dot_nt(dff, w2_ref[...])
        da = (dr * (2.0 * ra)).astype(BF16)
        da_ref[...] = da
        dh2 = _dot_nt(da, w1_ref[...])
        x1h, r2 = _rms(x1, D)
        g2_v = g2_ref[...]
        dn2 = dh2 * (1.0 + sc2)
        dx1_ref[...] = dx2 + _rms_bwd(dn2 * g2_v, x1h, r2, D)

        @pl.when(i % tps == 0)
        def _():
            accs_ref[...] = jnp.zeros_like(accs_ref)

        @pl.when(i == 0)
        def _():
            accg_ref[...] = jnp.zeros_like(accg_ref)

        accs_ref[0, 3:4, :] += _colsum(dh2)
        accs_ref[0, 4:5, :] += _colsum(dh2 * (x1h * g2_v))
        accs_ref[0, 5:6, :] += _colsum(dx2 * ff)
        accs_ref[0, 6:7, :] += _colsum(dout)
        accs_ref[0, 7:8, :] += _colsum(dout * outn)
        accg_ref[0:1, :] += _colsum(dn2 * x1h)
        accg_ref[1:2, :] += _colsum(doutn * x2h)
        accg_ref[2:3, :] += _colsum(err * err) * (0.5 / D)

    row = lambda w: pl.BlockSpec((tm, w), lambda i: (i, 0))
    return pl.pallas_call(
        body, name="p2_mlp_loss", grid=(n // tm,),
        out_shape=[jax.ShapeDtypeStruct((n, D), F32), jax.ShapeDtypeStruct((n, D_FF), BF16),
                   jax.ShapeDtypeStruct((n, D_FF), BF16), jax.ShapeDtypeStruct((n, D), BF16),
                   jax.ShapeDtypeStruct((nb, 8, D), F32), jax.ShapeDtypeStruct((8, D), F32)],
        in_specs=[row(D), row(D), row(D), pl.BlockSpec((1, 8, D), lambda i: (i // tps, 0, 0)), _VM, _VM, _VM, _VM],
        out_specs=[row(D), row(D_FF), row(D_FF), row(D), pl.BlockSpec((1, 8, D), lambda i: (i // tps, 0, 0)),
                   pl.BlockSpec((8, D), lambda i: (0, 0))],
        compiler_params=_cp(("arbitrary",)),
    )(x1, h2, target, modp, g2, gf, w_ff1, w_ff2)


def _p3_bwd(dx1, o, yssm, oattn, modp, gs, ga, w_out, S, tm):
    n = dx1.shape[0]
    tps = S // tm
    nb = n // S

    def body(dx1_ref, o_ref, ys_ref, oa_ref, mod_ref, gs_ref, ga_ref, w_ref,
             do_ref, dys_ref, doa_ref, dr_ref, accs_ref, accg_ref):
        i = pl.program_id(0)
        dx1 = dx1_ref[...]
        dob = (mod_ref[0, 2:3, :] * dx1).astype(BF16)
        do_ref[...] = dob
        dyn = _dot_nt(dob, w_ref[...])
        yh, rs = _rms(ys_ref[...], D_SSM)
        oa = oa_ref[...].astype(F32)
        ah, ra = _rms(oa, D_ATTN)
        d1 = dyn[:, 0:D_SSM]
        d2 = dyn[:, D_SSM:D_SSM + NH * HP]
        dys_ref[...] = _rms_bwd(d1 * gs_ref[...], yh, rs, D_SSM)
        doa = _rms_bwd(d2 * ga_ref[...], ah, ra, D_ATTN)
        doa_ref[...] = doa.astype(BF16)
        prod = doa * oa
        ones = jnp.ones((8, HP), BF16)
        for h in range(NH):
            dr_ref[h * 8:(h + 1) * 8, :] = _rows_of(prod[:, h * HP:(h + 1) * HP], ones)

        @pl.when(i % tps == 0)
        def _():
            accs_ref[...] = jnp.zeros_like(accs_ref)

        @pl.when(i == 0)
        def _():
            accg_ref[...] = jnp.zeros_like(accg_ref)

        accs_ref[0, 2:3, :] += _colsum(dx1 * o_ref[...])
        accg_ref[0:1, 0:D_SSM] += _colsum(d1 * yh)
        accg_ref[1:2, :] += _colsum(d2 * ah)

    row = lambda w: pl.BlockSpec((tm, w), lambda i: (i, 0))
    return pl.pallas_call(
        body, name="p3_bwd", grid=(n // tm,),
        out_shape=[jax.ShapeDtypeStruct((n, D), BF16), jax.ShapeDtypeStruct((n, D_SSM), F32),
                   jax.ShapeDtypeStruct((n, NH * HP), BF16), jax.ShapeDtypeStruct((nb * NH * 8, S), F32),
                   jax.ShapeDtypeStruct((nb, 8, D), F32), jax.ShapeDtypeStruct((8, NH * HP), F32)],
        in_specs=[row(D), row(D), row(D_SSM), row(NH * HP), pl.BlockSpec((1, 8, D), lambda i: (i // tps, 0, 0)),
                  _VM, _VM, _VM],
        out_specs=[row(D), row(D_SSM), row(NH * HP), pl.BlockSpec((NH * 8, tm), lambda i: (i // tps, i % tps)),
                   pl.BlockSpec((1, 8, D), lambda i: (i // tps, 0, 0)), pl.BlockSpec((8, NH * HP), lambda i: (0, 0))],
        compiler_params=_cp(("arbitrary",)),
    )(dx1, o, yssm, oattn, modp, gs, ga, w_out)


def _wgrad(a, b, name, col_slots=0):
    n, k1 = a.shape
    k2 = b.shape[1]
    bn = next((b for b in (1024, 512) if n % b == 0), n)
    bk1 = next((b for b in (1024, 512) if k1 % b == 0), k1)
    bk2 = k2 // col_slots if col_slots else (1024 if (k2 % 1024 == 0) else k2)

    def body(a_ref, b_ref, o_ref):
        @pl.when(pl.program_id(2) == 0)
        def _():
            o_ref[...] = jnp.zeros_like(o_ref)

        o_ref[...] += _dot_tn(a_ref[...], b_ref[...]).reshape(o_ref.shape)

    if col_slots:
        out_shape = jax.ShapeDtypeStruct((col_slots, k1, bk2), F32)
        out_spec = pl.BlockSpec((1, bk1, bk2), lambda i, j, t: (j, i, 0))
    else:
        out_shape = jax.ShapeDtypeStruct((k1, k2), F32)
        out_spec = pl.BlockSpec((bk1, bk2), lambda i, j, t: (i, j))
    return pl.pallas_call(
        body, name=name, grid=(k1 // bk1, k2 // bk2, n // bn),
        out_shape=out_shape,
        in_specs=[pl.BlockSpec((bn, bk1), lambda i, j, t: (t, i)), pl.BlockSpec((bn, bk2), lambda i, j, t: (t, j))],
        out_specs=out_spec,
        compiler_params=_cp(("parallel", "parallel", "arbitrary")),
    )(a, b)


def _row_block(rows):
    if rows <= 256:
        return rows
    return next(b for b in (256, 192, 128, 64, 32, 16, 8) if rows % b == 0)


def _add_half(g, recv, cidx, name):
    _, rows2, w = g.shape
    rows = rows2 // 2
    br = _row_block(rows)
    nblk = rows // br

    def body(c_ref, g_ref, r_ref, o_ref):
        o_ref[...] = (g_ref[...] + r_ref[...]).astype(BF16)

    return pl.pallas_call(
        body, name=name,
        grid_spec=pltpu.PrefetchScalarGridSpec(
            num_scalar_prefetch=1, grid=(4, nblk),
            in_specs=[pl.BlockSpec((1, br, w), lambda s, i, c: (s, c[0] * nblk + i, 0)),
                      pl.BlockSpec((1, br, w), lambda s, i, c: (s, i, 0))],
            out_specs=pl.BlockSpec((1, br, w), lambda s, i, c: (s, i, 0))),
        out_shape=jax.ShapeDtypeStruct((4, rows, w), BF16),
        compiler_params=_cp(("parallel", "parallel")),
    )(cidx, g, recv)


def _add_chips(r, name):
    _, rows, w = r.shape
    br = _row_block(rows)

    def body(r_ref, o_ref):
        f = lambda k: r_ref[k].astype(F32)
        o_ref[...] = ((f(0) + f(1)) + f(2)) + f(3)

    return pl.pallas_call(
        body, name=name, grid=(rows // br,),
        out_shape=jax.ShapeDtypeStruct((rows, w), F32),
        in_specs=[pl.BlockSpec((4, br, w), lambda i: (0, i, 0))],
        out_specs=pl.BlockSpec((br, w), lambda i: (i, 0)),
        compiler_params=_cp(("parallel",)),
    )(r)


def _sum_devices(a, b):
    def body(a_ref, b_ref, oa_ref, ob_ref):
        acc = a_ref[0:1, :].astype(F32)
        accb = b_ref[0:1, :]
        for k in range(1, 8):
            acc = acc + a_ref[k:k + 1, :].astype(F32)
            accb = accb + b_ref[k:k + 1, :]
        oa_ref[...] = acc
        ob_ref[...] = accb

    return pl.pallas_call(
        body, name="small_grad_sum",
        out_shape=[jax.ShapeDtypeStruct((1, a.shape[1]), F32), jax.ShapeDtypeStruct((1, b.shape[1]), F32)],
        in_specs=[_VM, _VM], out_specs=[_VM, _VM], compiler_params=_cp(),
    )(a, b)


def _adamw_math(wv, gv, mv, vv):
    m_new = ADAM_B1 * mv + (1.0 - ADAM_B1) * gv
    v_new = ADAM_B2 * vv + (1.0 - ADAM_B2) * (gv * gv)
    m_hat = m_new / (1.0 - ADAM_B1 ** ADAM_STEP)
    v_hat = v_new / (1.0 - ADAM_B2 ** ADAM_STEP)
    return -ADAM_LR * (m_hat / (jnp.sqrt(v_hat) + ADAM_EPS) + ADAM_WD * wv), m_new, v_new


def _adamw_small(ws, gs, ms, vs):
    k = len(ws)

    def body(*refs):
        ins, outs = refs[:4 * k], refs[4 * k:]
        for t in range(k):
            d, m_new, v_new = _adamw_math(ins[t][...], ins[k + t][...], ins[2 * k + t][...], ins[3 * k + t][...])
            outs[t][...] = d
            outs[k + t][...] = m_new
            outs[2 * k + t][...] = v_new

    shapes = [jax.ShapeDtypeStruct(w.shape, F32) for w in ws]
    return pl.pallas_call(
        body, name="adamw_small", out_shape=shapes * 3,
        in_specs=[_VM] * (4 * k), out_specs=[_VM] * (3 * k), compiler_params=_cp(),
    )(*ws, *gs, *ms, *vs)


def _adamw(w, g, m, v, name):
    rows, wd = w.shape
    br = _row_block(rows)

    def body(w_ref, g_ref, m_ref, v_ref, d_ref, nm_ref, nv_ref):
        d, m_new, v_new = _adamw_math(w_ref[...], g_ref[...], m_ref[...], v_ref[...])
        d_ref[...] = d
        nm_ref[...] = m_new
        nv_ref[...] = v_new

    spec = pl.BlockSpec((br, wd), lambda i: (i, 0))
    return pl.pallas_call(
        body, name=name, grid=(rows // br,),
        out_shape=[jax.ShapeDtypeStruct((rows, wd), F32)] * 3,
        in_specs=[spec] * 4, out_specs=[spec] * 3,
        compiler_params=_cp(("parallel",)),
    )(w, g, m, v)


def _adamw_halves(w, mine, other, m, v, cidx, name):
    rows, wd = w.shape
    h = rows // 2
    br = _row_block(h)
    nblk = h // br

    def body(c_ref, w_ref, a_ref, b_ref, m_ref, v_ref, g_ref, d_ref, nm_ref, nv_ref):
        gv = jnp.where(pl.program_id(0) == c_ref[0], a_ref[...], b_ref[...])
        d, m_new, v_new = _adamw_math(w_ref[...], gv, m_ref[...], v_ref[...])
        g_ref[...] = gv
        d_ref[...] = d
        nm_ref[...] = m_new
        nv_ref[...] = v_new

    full = pl.BlockSpec((br, wd), lambda hf, i, c: (hf * nblk + i, 0))
    half = pl.BlockSpec((br, wd), lambda hf, i, c: (i, 0))
    return pl.pallas_call(
        body, name=name,
        grid_spec=pltpu.PrefetchScalarGridSpec(
            num_scalar_prefetch=1, grid=(2, nblk),
            in_specs=[full, half, half, full, full], out_specs=[full] * 4),
        out_shape=[jax.ShapeDtypeStruct((rows, wd), F32)] * 4,
        compiler_params=_cp(("parallel", "parallel")),
    )(cidx, w, mine, other, m, v)


def _other_chips(x, y):
    return [(1 - x, y), (x, 1 - y), (1 - x, 1 - y)]


def _other_devices(x, y, c):
    flip = lambda v, d: (1 - v) if d else v
    return [(flip(x, dx), flip(y, dy), flip(c, dc))
            for dx in (0, 1) for dy in (0, 1) for dc in (0, 1) if (dx, dy, dc) != (0, 0, 0)]


def _exchange(name, ins, out_shapes, n_local, n_remote, plan):
    ni, no = len(ins), len(out_shapes)

    def body(*refs):
        in_refs, out_refs = refs[:ni], refs[ni:ni + no]
        send_sems, recv_sems, local_sems = refs[ni + no:]
        x, y, c = lax.axis_index("x"), lax.axis_index("y"), lax.axis_index("c")
        local, remote = plan(in_refs, out_refs, x, y, c)
        assert len(local) == n_local and len(remote) == n_remote

        def push(k, src, dst, dev):
            return pltpu.make_async_remote_copy(src_ref=src, dst_ref=dst, send_sem=send_sems.at[k],
                                                recv_sem=recv_sems.at[k], device_id=dev, device_id_type=MESH)

        own = [pltpu.make_async_copy(s, d, local_sems.at[i]) for i, (s, d) in enumerate(local)]
        for cp in own:
            cp.start()
        sends = [push(k, s, d, dev) for k, (s, d, dev, _) in enumerate(remote)]
        for cp in sends:
            cp.start()
        for k, (s, _, dev, landing) in enumerate(remote):
            push(k, s, landing, dev).wait_recv()
        for cp in sends:
            cp.wait_send()
        for cp in own:
            cp.wait()

    return pl.pallas_call(
        body, name=name, out_shape=out_shapes,
        in_specs=[_ANY] * ni, out_specs=[_ANY] * no,
        scratch_shapes=[pltpu.SemaphoreType.DMA((n_remote,)), pltpu.SemaphoreType.DMA((n_remote,)),
                        pltpu.SemaphoreType.DMA((max(n_local, 1),))],
        compiler_params=pltpu.CompilerParams(has_side_effects=True),
    )(*ins)


def _gather_chips(name, shards, everyone=()):
    ns, ne = len(shards), len(everyone)
    outs = [jax.ShapeDtypeStruct((4,) + a.shape, a.dtype) for a in shards]
    outs += [jax.ShapeDtypeStruct((8,) + a.shape, a.dtype) for a in everyone]

    def plan(i, o, x, y, c):
        mine, me = 2 * x + y, 4 * x + 2 * y + c
        local, remote = [], []
        for t in range(ns):
            local.append((i[t], o[t].at[mine]))
            for px, py in _other_chips(x, y):
                remote.append((i[t], o[t].at[mine], (px, py, c), o[t].at[2 * px + py]))
        for t in range(ns, ns + ne):
            local.append((i[t], o[t].at[me]))
            for px, py, pc in _other_devices(x, y, c):
                remote.append((i[t], o[t].at[me], (px, py, pc), o[t].at[4 * px + 2 * py + pc]))
        return local, remote

    return _exchange(name, list(shards) + list(everyone), outs, ns + ne, 3 * ns + 7 * ne, plan)


_HBM = pl.BlockSpec(memory_space=pltpu.HBM)
_SEM = pl.BlockSpec(memory_space=pltpu.SEMAPHORE)
_EFFECT = pltpu.SideEffectType.DATAFLOW_SIDE_EFFECTING


def _split_start(name, ins, land_shapes, n_remote, plan, after):
    ni, nl = len(ins), len(land_shapes)
    srcs = [pltpu.with_memory_space_constraint(a, pltpu.HBM) for a in ins]
    lands = [pltpu.with_memory_space_constraint(lax.empty(s.shape, s.dtype), pltpu.HBM) for s in land_shapes]

    def body(*refs):
        src, land = refs[:ni], refs[ni:ni + nl]
        first = ni + nl + 1
        send, recv = refs[first:first + n_remote], refs[first + n_remote:first + 2 * n_remote]
        token = refs[first + 2 * n_remote + ni + nl]
        x, y, c = lax.axis_index("x"), lax.axis_index("y"), lax.axis_index("c")
        remote = plan(src, land, x, y, c)
        assert len(remote) == n_remote
        for k, (s, d, dev, _) in enumerate(remote):
            pltpu.make_async_remote_copy(src_ref=s, dst_ref=d, send_sem=send[k], recv_sem=recv[k],
                                         device_id=dev, device_id_type=MESH).start()
        token[...] = jnp.zeros_like(token)

    out = pl.pallas_call(
        body, name=name + "_start",
        out_shape=[pltpu.SemaphoreType.DMA(())] * (2 * n_remote)
                  + [pltpu.HBM(a.shape, a.dtype) for a in ins] + [pltpu.HBM(s.shape, s.dtype) for s in land_shapes]
                  + [jax.ShapeDtypeStruct((8, 128), F32)],
        in_specs=[_HBM] * (ni + nl) + [_ANY], out_specs=[_SEM] * (2 * n_remote) + [_HBM] * (ni + nl) + [_VM],
        input_output_aliases={t: 2 * n_remote + t for t in range(ni + nl)},
        compiler_params=pltpu.CompilerParams(has_side_effects=_EFFECT),
    )(*srcs, *lands, after)
    sems, thru = out[:2 * n_remote], out[2 * n_remote:2 * n_remote + ni + nl]
    return (name, sems, thru[:ni], thru[ni:], n_remote, plan), out[-1]


def _split_wait(handle, after):
    name, sems, srcs, lands, n_remote, plan = handle
    ni, nl = len(srcs), len(lands)

    def body(*refs):
        src, land = refs[:ni], refs[ni:ni + nl]
        send, recv = refs[ni + nl:ni + nl + n_remote], refs[ni + nl + n_remote:ni + nl + 2 * n_remote]
        x, y, c = lax.axis_index("x"), lax.axis_index("y"), lax.axis_index("c")
        for k, (s, _, dev, landing) in enumerate(plan(src, land, x, y, c)):
            cp = pltpu.make_async_remote_copy(src_ref=s, dst_ref=landing, send_sem=send[k], recv_sem=recv[k],
                                              device_id=dev, device_id_type=MESH)
            cp.wait_send()
            cp.wait_recv()

    out = pl.pallas_call(
        body, name=name + "_wait",
        out_shape=[pltpu.HBM(a.shape, a.dtype) for a in srcs] + [pltpu.HBM(a.shape, a.dtype) for a in lands],
        in_specs=[_HBM] * (ni + nl) + [_SEM] * (2 * n_remote) + [_ANY], out_specs=[_HBM] * (ni + nl),
        input_output_aliases={t: t for t in range(ni + nl)},
        compiler_params=pltpu.CompilerParams(has_side_effects=_EFFECT),
    )(*srcs, *lands, *sems, after)
    return out[:ni], out[ni:]


def _plan_to_chips(src, land, x, y, c):
    mine = 2 * x + y
    return [(src[t], land[t].at[mine], (px, py, c), land[t].at[2 * px + py])
            for t in range(len(src)) for px, py in _other_chips(x, y)]


def _plan_swap_halves(src, land, x, y, c):
    out = []
    for t in range(len(src)):
        h = src[t].shape[1] // 2
        out.append((src[t].at[:, pl.ds(pl.multiple_of((1 - c) * h, 8), h), :], land[t], (x, y, 1 - c), land[t]))
    return out


def _plan_scatter_chips(src, land, x, y, c):
    mine = 2 * x + y
    return [(src[t].at[2 * px + py], land[t].at[mine], (px, py, c), land[t].at[2 * px + py])
            for t in range(len(src)) for px, py in _other_chips(x, y)]


def _swap_halves(gs, everyone):
    ns, ne = len(gs), len(everyone)
    outs = [jax.ShapeDtypeStruct((4, g.shape[1] // 2, g.shape[2]), g.dtype) for g in gs]
    outs += [jax.ShapeDtypeStruct((8,) + a.shape, a.dtype) for a in everyone]

    def plan(i, o, x, y, c):
        me = 4 * x + 2 * y + c
        local, remote = [], []
        for t in range(ns):
            h = gs[t].shape[1] // 2
            theirs = i[t].at[:, pl.ds(pl.multiple_of((1 - c) * h, 8), h), :]
            remote.append((theirs, o[t], (x, y, 1 - c), o[t]))
        for t in range(ns, ns + ne):
            local.append((i[t], o[t].at[me]))
            for px, py, pc in _other_devices(x, y, c):
                remote.append((i[t], o[t].at[me], (px, py, pc), o[t].at[4 * px + 2 * py + pc]))
        return local, remote

    return _exchange("grad_swap_sibling", list(gs) + list(everyone), outs, ne, ns + 7 * ne, plan)


def _scatter_chips(parts):
    ns = len(parts)
    outs = [jax.ShapeDtypeStruct(a.shape, a.dtype) for a in parts]

    def plan(i, o, x, y, c):
        mine = 2 * x + y
        local, remote = [], []
        for t in range(ns):
            local.append((i[t].at[mine], o[t].at[mine]))
            for px, py in _other_chips(x, y):
                remote.append((i[t].at[2 * px + py], o[t].at[mine], (px, py, c), o[t].at[2 * px + py]))
        return local, remote

    return _exchange("grad_scatter_chips", list(parts), outs, ns, 3 * ns, plan)


def _join_halves(halves):
    ns = len(halves)
    outs = [jax.ShapeDtypeStruct(a.shape, a.dtype) for a in halves]

    def plan(i, o, x, y, c):
        return [], [(i[t], o[t], (x, y, 1 - c), o[t]) for t in range(ns)]

    return _exchange("grad_join_sibling", list(halves), outs, 0, ns, plan)


def _pad_heads_cols(w, per, used):
    k = w.shape[0]
    w = w.reshape(k, NH, per)[:, :, :used]
    return jnp.pad(w, ((0, 0), (0, 0), (0, HP - used))).reshape(k, NH * HP)


def _unpad_heads_cols(w, used):
    k = w.shape[0]
    return w.reshape(k, NH, HP)[:, :, :used]


def _prep_weights(wf):
    bf = lambda a: a.astype(BF16)
    out = {}
    out["w_in"] = jnp.pad(bf(wf["w_in"]), ((0, 0), (0, IN_PAD - IN_COLS)))
    out["w_glu"] = bf(wf["w_glu"])
    out["w_uq"] = _pad_heads_cols(bf(wf["w_uq"]), QK_NOPE + QK_ROPE, QK_NOPE + QK_ROPE)
    wkv = bf(wf["w_ukv"]).reshape(KV_LORA, NH, QK_NOPE + V_HEAD)
    wk = jnp.pad(wkv[:, :, :QK_NOPE], ((0, 0), (0, 0), (0, HP - QK_NOPE))).reshape(KV_LORA, NH * HP)
    wv = jnp.pad(wkv[:, :, QK_NOPE:], ((0, 0), (0, 0), (0, HP - V_HEAD))).reshape(KV_LORA, NH * HP)
    out["w_ukv"] = jnp.concatenate([wk, wv], axis=1)
    return out


def _prep_late_weights(wf):
    bf = lambda a: a.astype(BF16)
    out = {}
    wo = bf(wf["w_out"])
    wo_a = jnp.pad(wo[D_SSM:].reshape(NH, V_HEAD, D), ((0, 0), (0, HP - V_HEAD), (0, 0))).reshape(NH * HP, D)
    out["w_out"] = jnp.concatenate([wo[:D_SSM], wo_a], axis=0)
    out["w_ff1"] = bf(wf["w_ff1"])
    out["w_ff2"] = bf(wf["w_ff2"])
    return out


def _rope_tables(positions):
    inv_freq = ROPE_BASE ** (-jnp.arange(0, QK_ROPE, 2, dtype=F32) / QK_ROPE)
    ang = positions.astype(F32)[:, None] * inv_freq
    cos, sin = jnp.cos(ang), jnp.sin(ang)
    n = positions.shape[0]
    one = jnp.ones((n, QK_NOPE), F32)
    z16 = jnp.zeros((n, 16), F32)
    z32 = jnp.zeros((n, 32), F32)
    z64 = jnp.zeros((n, QK_NOPE), F32)
    rc = jnp.concatenate([one, cos, cos, z32], axis=1)
    rs1 = jnp.concatenate([z64, -sin, z16, z32], axis=1)
    rs2 = jnp.concatenate([z64, z16, sin, z32], axis=1)
    return rc, rs1, rs2


def _permute_rows(a, S):
    n, w = a.shape
    return a.reshape(n // S, 8, S // 8, w).transpose(0, 2, 1, 3).reshape(n, w)


def _unpermute_rows(a, S):
    n, w = a.shape
    return a.reshape(n // S, S // 8, 8, w).transpose(0, 2, 1, 3).reshape(n, w)


def _block_diag_in(bb):
    eye = jnp.eye(8, dtype=bb.dtype)
    blocks = jnp.einsum("qgph,gk->qghkp", bb.reshape(4, 8, P, H), eye).reshape(4, QB, QS)
    return blocks.transpose(1, 0, 2).reshape(QB, NST)


def _block_diag_out(cc):
    eye = jnp.eye(8, dtype=cc.dtype)
    return jnp.einsum("qghp,gk->qgpkh", cc.reshape(4, 8, H, P), eye).reshape(NST, QB)


def _slots(full):
    r, cdim = full.shape
    return full.reshape(r, 4, cdim // 4).transpose(1, 0, 2)


def _unslots(g):
    s, r, cs = g.shape
    return g.transpose(1, 0, 2).reshape(r, s * cs)


def _local_step(x, positions, target, modp, wf, late_weights=None, reducer=None):
    nb, S, _ = x.shape
    n = nb * S
    tm = min(256, S)
    tr = min(512, S)
    tt = min(256, S)
    tq = min(512, S // 2)
    kw = _prep_weights(wf)
    row = lambda a: a.reshape(1, -1).astype(F32)

    xf = x.reshape(n, D)
    tf = target.reshape(n, D)
    g1, g2, gf = row(wf["norm1_g"]), row(wf["norm2_g"]), row(wf["final_norm_g"])
    h1, proj = _f1_fwd(xf, modp, g1, kw["w_in"], S, tr)

    col = lambda a: a.reshape(NST, 1)
    lam_re, lam_im = col(wf["ssm_lambda_re"]), col(wf["ssm_lambda_im"])
    logdt = jnp.repeat(wf["ssm_log_dt"].reshape(G, 1), P, axis=1).reshape(NST, 1)
    b_re, b_im = wf["ssm_b_re"].reshape(NST, H), wf["ssm_b_im"].reshape(NST, H)
    lbr, lbi, bbr, bbi = _ssm_param_fwd(lam_re, lam_im, logdt, b_re, b_im)
    lre8 = jnp.broadcast_to(lbr.reshape(1, NST), (8, NST))
    lim8 = jnp.broadcast_to(lbi.reshape(1, NST), (8, NST))
    bm = jnp.concatenate([_block_diag_in(bbr.reshape(G, P, H)), _block_diag_in(bbi.reshape(G, P, H))],
                         axis=1).astype(BF16)
    cm = jnp.concatenate([_block_diag_out(wf["ssm_c_re"]), -_block_diag_out(wf["ssm_c_im"])], axis=0).astype(BF16)
    dvec = row(wf["ssm_d"])
    u_p = _permute_rows(proj[:, :D_SSM], S)
    fcr, fci = _ssm_local(u_p, bm, lre8, lim8, S, tt)
    st, ypre, z, gact, yssm_p = _ssm_fwd(u_p, fcr, fci, bm, cm, dvec, kw["w_glu"], lre8, lim8, S, tt)
    yssm = _unpermute_rows(yssm_p, S)

    rc, rs1, rs2 = _rope_tables(positions.reshape(n))
    gq, gkv = row(wf["q_norm_g"]), row(wf["kv_norm_g"])
    q, k, v, qn, kvn = _mla_fwd(proj, rc, rs1, rs2, gq, gkv, kw["w_uq"], kw["w_ukv"], tr)
    oattn, lrow = _attn_fwd(q, k, v, S, tq)

    gs = row(wf["ssm_out_g"])
    ga = jnp.pad(wf["attn_out_g"].reshape(NH, V_HEAD), ((0, 0), (0, HP - V_HEAD))).reshape(1, NH * HP)
    kw.update(_prep_late_weights(late_weights(oattn) if late_weights is not None else wf))
    yn, o, x1, h2 = _p1_fwd(yssm, oattn, xf, modp, gs, ga, kw["w_out"], g2, S, tr)
    dx1, r, da, dff, accs2, accg2 = _p2(x1, h2, tf, modp, g2, gf, kw["w_ff1"], kw["w_ff2"], S, tm)
    loss = accg2[2:3]
    g_ff1 = _wgrad(h2, da, "wgrad_ff1", col_slots=4)
    g_ff2 = _wgrad(r, dff, "wgrad_ff2").reshape(4, D_FF // 4, D)
    gs_b, gq_b = gs, gq
    if reducer is not None:
        gs_b = gs + reducer.start([g_ff1, g_ff2])[0, 0]
    do, dyssm, dob, drow, accs3, accg3 = _p3_bwd(dx1, o, yssm, oattn, modp, gs_b, ga, kw["w_out"], S, tr)

    dq, dk, dv = _attn_bwd(q, k, v, dob, lrow, drow, S, tq)
    if reducer is not None:
        gq_b = gq + reducer.middle(dq)[0, 0]
    dmla, dqb, dkvb, accm = _mla_bwd(dq, dk, dv, proj, rc, rs1, rs2, gq_b, gkv, kw["w_uq"], kw["w_ukv"], tr)

    dys_p = _permute_rows(dyssm, S)
    dy, dz, air, aii = _ssm_bwd_a(dys_p, z, ypre, kw["w_glu"], cm, lre8, lim8, S, tt)
    du_p, dcm, dbm, dd, dlr, dli = _ssm_bwd_b(dy, u_p, st, fcr, fci, air, aii, bm, cm, dvec, lre8, lim8, S, tt)
    du = _unpermute_rows(du_p, S)
    dcm = dcm.reshape(2, 4, 8, P, 8, H)
    dc_re = jnp.einsum("qgpgh->qghp", dcm[0]).reshape(G, H, P)
    dc_im = -jnp.einsum("qgpgh->qghp", dcm[1]).reshape(G, H, P)
    dbm = dbm.reshape(8, H, 2, 4, 8, P)
    dbb_re = jnp.einsum("ghqgp->qgph", dbm[:, :, 0]).reshape(NST, H)
    dbb_im = jnp.einsum("ghqgp->qgph", dbm[:, :, 1]).reshape(NST, H)
    gb_re, gb_im, glr, gli, gdt = _ssm_param_bwd(lam_re, lam_im, logdt, b_re, b_im, dlr.reshape(NST, 1),
                                                 dli.reshape(NST, 1), dbb_re, dbb_im)
    glogdt = _rowsum(gdt.reshape(G, P))

    dx, dproj, accs1, accg1 = _f1_bwd(du, dmla, dx1, xf, modp, g1, kw["w_in"], S, tr)

    big = {}
    big["w_in"] = _slots(_wgrad(h1, dproj, "wgrad_in")[:, :IN_COLS])
    big["w_glu"] = _wgrad(gact, dz, "wgrad_glu", col_slots=4)
    big["w_uq"] = _slots(_unpad_heads_cols(_wgrad(qn, dqb, "wgrad_uq"), QK_NOPE + QK_ROPE).reshape(Q_LORA, -1))
    gkvw = _wgrad(kvn, dkvb, "wgrad_ukv")
    big["w_ukv"] = _slots(jnp.concatenate([_unpad_heads_cols(gkvw[:, :NH * HP], QK_NOPE),
                                           _unpad_heads_cols(gkvw[:, NH * HP:], V_HEAD)], axis=2).reshape(KV_LORA, -1))
    gwo = _wgrad(yn, do, "wgrad_out")
    big["w_out"] = jnp.concatenate([gwo[:D_SSM].reshape(2, D_SSM // 2, D),
                                    gwo[D_SSM:].reshape(2, NH // 2 * HP, D).reshape(2, NH // 2, HP, D)[:, :, :V_HEAD]
                                    .reshape(2, D_ATTN // 2, D)], axis=0)
    big["w_ff1"] = g_ff1
    big["w_ff2"] = g_ff2

    small = {}
    small["norm1_g"] = accg1[0:1]
    small["norm2_g"] = accg2[0:1]
    small["final_norm_g"] = accg2[1:2]
    small["ssm_out_g"] = accg3[0:1, :D_SSM]
    small["attn_out_g"] = accg3[1].reshape(NH, HP)[:, :V_HEAD].reshape(1, D_ATTN)
    small["q_norm_g"] = accm[0:1, :Q_LORA]
    small["kv_norm_g"] = accm[1:2, :KV_LORA]
    small["ssm_lambda_re"] = glr.reshape(G, P)
    small["ssm_lambda_im"] = gli.reshape(G, P)
    small["ssm_b_re"] = gb_re
    small["ssm_b_im"] = gb_im
    small["ssm_c_re"] = dc_re.reshape(G * H, P)
    small["ssm_c_im"] = dc_im.reshape(G * H, P)
    small["ssm_d"] = dd.reshape(G, H)
    small["ssm_log_dt"] = glogdt.reshape(1, G)
    return loss, dx.reshape(nb, S, D), big, small, accs1 + accs2 + accs3


def _view2d(a):
    return a.reshape(-1, a.shape[-1]) if a.ndim > 1 else a.reshape(1, -1)


def kernel(x, c, positions, ada_w, ada_b, norm1_g, w_in, ssm_lambda_re, ssm_lambda_im, ssm_b_re, ssm_b_im, ssm_c_re, ssm_c_im, ssm_d, ssm_log_dt, w_glu, q_norm_g, w_uq, kv_norm_g, w_ukv, ssm_out_g, attn_out_g, w_out, norm2_g, w_ff1, w_ff2, final_ada_w, final_ada_b, final_norm_g, loss_target, m_ada_w, m_ada_b, m_norm1_g, m_w_in, m_ssm_lambda_re, m_ssm_lambda_im, m_ssm_b_re, m_ssm_b_im, m_ssm_c_re, m_ssm_c_im, m_ssm_d, m_ssm_log_dt, m_w_glu, m_q_norm_g, m_w_uq, m_kv_norm_g, m_w_ukv, m_ssm_out_g, m_attn_out_g, m_w_out, m_norm2_g, m_w_ff1, m_w_ff2, m_final_ada_w, m_final_ada_b, m_final_norm_g, v_ada_w, v_ada_b, v_norm1_g, v_w_in, v_ssm_lambda_re, v_ssm_lambda_im, v_ssm_b_re, v_ssm_b_im, v_ssm_c_re, v_ssm_c_im, v_ssm_d, v_ssm_log_dt, v_w_glu, v_q_norm_g, v_w_uq, v_kv_norm_g, v_w_ukv, v_ssm_out_g, v_attn_out_g, v_w_out, v_norm2_g, v_w_ff1, v_w_ff2, v_final_ada_w, v_final_ada_b, v_final_norm_g):
    args = dict(locals())
    names = list(inspect.signature(kernel).parameters)
    wnames = names[3:names.index("loss_target")]
    small_names = [nm for nm in wnames if nm not in GATHERED and nm not in TP]
    reduced_names = [nm for nm in small_names if nm not in ("ada_b", "final_ada_b")]
    w = {nm: args[nm] for nm in wnames}
    m = {nm: args["m_" + nm] for nm in wnames}
    v = {nm: args["v_" + nm] for nm in wnames}
    nb = x.shape[0]
    xi, yi, ci = lax.axis_index("x"), lax.axis_index("y"), lax.axis_index("c")
    chip, me = 2 * xi + yi, 4 * xi + 2 * yi + ci

    unslot = lambda nm, g: g.reshape(-1, g.shape[-1]) if nm in ROW_SHARDED else _unslots(g)
    early = [nm for nm in GATHERED if nm not in LATE]
    got = _gather_chips("gather_weights", [_view2d(w[nm]).astype(BF16) for nm in early], [c])
    wf = {nm: unslot(nm, g) for nm, g in zip(early, got)}
    for nm in small_names:
        wf[nm] = w[nm][0] if w[nm].ndim > 1 else w[nm]
    c_all = got[len(early)].reshape(8 * nb, D)

    na, nf = ada_w.shape[-1], final_ada_w.shape[-1]
    ada_b_s = lax.dynamic_slice(ada_b, (0, chip * na), (1, na))
    fada_b_s = lax.dynamic_slice(final_ada_b.reshape(1, -1), (0, chip * nf), (1, nf))
    cond_all, modcols = _mod_fwd(c_all, ada_w[0], ada_b_s, final_ada_w, fada_b_s)
    (mod_g,) = _gather_chips("gather_mod", [modcols])
    mine = lax.dynamic_slice(mod_g, (0, me * nb, 0), (4, nb, na + nf))
    modp = jnp.concatenate([mine[:, :, :na].transpose(1, 0, 2).reshape(nb, 6, D),
                            mine[:, :, na:].transpose(1, 0, 2).reshape(nb, 2, D)], axis=1)

    own_late = [_view2d(w[nm]).astype(BF16) for nm in LATE]
    late_gather, token = _split_start("gather_late", own_late,
                                      [jax.ShapeDtypeStruct((4,) + a.shape, a.dtype) for a in own_late],
                                      3 * len(LATE), _plan_to_chips, modp)
    modp = modp + token[0, 0]

    def late_weights(after):
        sent, landed = _split_wait(late_gather, after)
        return {nm: unslot(nm, lax.dynamic_update_slice(g, own[None], (chip, 0, 0)))
                for nm, g, own in zip(LATE, landed, sent)}

    cidx = ci.astype(jnp.int32).reshape(1)
    ahead = ["w_ff1", "w_ff2"]

    class Reducer:
        def start(self, gs):
            lands = [jax.ShapeDtypeStruct((4, g.shape[1] // 2, g.shape[2]), g.dtype) for g in gs]
            self.swap, tok = _split_start("grad_swap_ff", gs, lands, len(gs), _plan_swap_halves, modp)
            return tok

        def middle(self, after):
            gs, got = _split_wait(self.swap, after)
            sums = [_add_half(g, r, cidx, "grad_add_sibling_" + nm) for nm, g, r in zip(ahead, gs, got)]
            lands = [jax.ShapeDtypeStruct(s.shape, s.dtype) for s in sums]
            self.scatter, tok = _split_start("grad_scatter_ff", sums, lands, 3 * len(sums), _plan_scatter_chips, modp)
            return tok

        def finish(self, after):
            out = []
            for nm, s, l in zip(ahead, *_split_wait(self.scatter, after)):
                own = lax.dynamic_slice(s, (chip, 0, 0), (1,) + s.shape[1:])
                out.append(_add_chips(lax.dynamic_update_slice(l, own, (chip, 0, 0)), "grad_add_chips_" + nm))
            return out

    reducer = Reducer()
    loss_row, grad_x, big, small, dmodp = _local_step(x, positions, loss_target, modp, wf, late_weights, reducer)

    rest = [nm for nm in GATHERED if nm not in ahead]
    sizes = [small[nm].size for nm in reduced_names]
    pad = -sum(sizes) % 128
    packed = jnp.concatenate([small[nm].reshape(1, -1) for nm in reduced_names] + [jnp.zeros((1, pad), F32)],
                             axis=1).astype(BF16)
    swapped = _swap_halves([big[nm] for nm in rest], [dmodp.reshape(nb, 8 * D), packed, loss_row])
    chip_sums = [_add_half(big[nm], r, cidx, "grad_add_sibling_" + nm) for nm, r in zip(rest, swapped)]
    half_of = {nm: _add_chips(r, "grad_add_chips_" + nm) for nm, r in zip(rest, _scatter_chips(chip_sums))}
    half_of.update(zip(ahead, reducer.finish(grad_x)))
    halves = [half_of[nm] for nm in GATHERED]
    others = _join_halves(halves)
    grads = {}
    dmod_all = swapped[len(rest)].reshape(8 * nb, 8 * D)
    small_sum, loss_sum = _sum_devices(swapped[len(rest) + 1].reshape(8, -1), swapped[len(rest) + 2].reshape(8, -1))
    loss = jnp.sum(loss_sum)
    off = 0
    for nm, sz in zip(reduced_names, sizes):
        grads[nm] = small_sum[:, off:off + sz].reshape(small[nm].shape)
        off += sz

    dsl = jnp.concatenate([lax.dynamic_slice(dmod_all, (0, chip * na), (8 * nb, na)),
                           lax.dynamic_slice(dmod_all, (0, 6 * D + chip * nf), (8 * nb, nf))], axis=1)
    gw, gb = _mod_bwd(cond_all.T, dsl, dmod_all)
    grads["ada_w"], grads["final_ada_w"] = gw[:, :na], gw[:, na:]
    grads["ada_b"], grads["final_ada_b"] = gb[:, :6 * D], gb[:, 6 * D:]

    delta, new_m, new_v = {}, {}, {}
    for nm, mine_h, other_h in zip(GATHERED, halves, others):
        grads[nm], delta[nm], new_m[nm], new_v[nm] = _adamw_halves(
            _view2d(w[nm]), mine_h, other_h, _view2d(m[nm]), _view2d(v[nm]), cidx, "adamw_" + nm)
    for nm in TP:
        delta[nm], new_m[nm], new_v[nm] = _adamw(_view2d(w[nm]), grads[nm], _view2d(m[nm]), _view2d(v[nm]),
                                                  "adamw_" + nm)
    upd = _adamw_small([_view2d(w[nm]) for nm in small_names], [grads[nm] for nm in small_names],
                       [_view2d(m[nm]) for nm in small_names], [_view2d(v[nm]) for nm in small_names])
    k = len(small_names)
    for t, nm in enumerate(small_names):
        delta[nm], new_m[nm], new_v[nm] = upd[t], upd[k + t], upd[2 * k + t]

    outs = [grads, delta, new_m, new_v]
    return (loss, grad_x, *[d[nm].reshape(w[nm].shape) for d in outs for nm in wnames])
```

```python
import inspect
import math

import jax
import jax.numpy as jnp
from jax import lax
from jax.experimental import pallas as pl
from jax.experimental.pallas import tpu as pltpu

F32 = jnp.float32
BF16 = jnp.bfloat16

D = 1024
D_SSM = 512
G = 32
H = 16
P = 64
NST = G * P
D_ATTN = 512
NH = 8
QK_NOPE = 64
QK_ROPE = 32
V_HEAD = 64
HP = 128
Q_LORA = 384
KV_LORA = 256
IN_COLS = D_SSM + Q_LORA + KV_LORA + QK_ROPE
IN_PAD = 1280
D_FF = 4096
ROPE_BASE = 10000.0
EPS = 1e-6
ADAM_LR = 0.001
ADAM_B1 = 0.9
ADAM_B2 = 0.999
ADAM_EPS = 1e-08
ADAM_WD = 0.01
ADAM_STEP = 10
NEG = -1e30
VMEM_LIMIT = 60 << 20

MESH = pl.DeviceIdType.MESH
_VM = pl.BlockSpec(memory_space=pltpu.VMEM)
_ANY = pl.BlockSpec(memory_space=pl.ANY)

GATHERED = ["w_in", "w_glu", "w_uq", "w_ukv", "w_out", "w_ff1", "w_ff2"]
TP = ["ada_w", "final_ada_w"]
ROW_SHARDED = ("w_out", "w_ff2")
LATE = ["w_out", "w_ff1", "w_ff2"]


def _cp(sem=None, vmem=VMEM_LIMIT):
    kw = dict(vmem_limit_bytes=vmem)
    if sem is not None:
        kw["dimension_semantics"] = sem
    return pltpu.CompilerParams(**kw)


def _dot(a, b):
    return jnp.dot(a, b, preferred_element_type=F32)


def _dot_nt(a, b):
    return lax.dot_general(a, b, (((1,), (1,)), ((), ())), preferred_element_type=F32)


def _dot_tn(a, b):
    return lax.dot_general(a, b, (((0,), (0,)), ((), ())), preferred_element_type=F32)


def _rms(x, n):
    r = lax.rsqrt(jnp.sum(x * x, axis=-1, keepdims=True) * (1.0 / n) + EPS)
    return x * r, r


def _rms_bwd(dyg, xhat, r, n):
    return r * (dyg - xhat * (jnp.sum(dyg * xhat, axis=-1, keepdims=True) * (1.0 / n)))


def _sigmoid(x):
    return 1.0 / (1.0 + jnp.exp(-x))


_GK = math.sqrt(2.0 / math.pi)
_GC = 0.044715


def _gelu(y):
    t = jnp.tanh(_GK * (y + _GC * y * y * y))
    return 0.5 * y * (1.0 + t)


def _gelu_grad(y):
    t = jnp.tanh(_GK * (y + _GC * y * y * y))
    return 0.5 * (1.0 + t) + 0.5 * y * (1.0 - t * t) * _GK * (1.0 + 3.0 * _GC * y * y)


def _colsum(x):
    return jnp.sum(x, axis=0, keepdims=True)


def _roll(x, s):
    return pltpu.roll(x, s % x.shape[-1], x.ndim - 1)


def _mod_fwd(c_all, ada_w_s, ada_b_s, fada_w_s, fada_b_s):
    nseq = c_all.shape[0]
    na, nf = ada_w_s.shape[1], fada_w_s.shape[1]

    def body(c_ref, w_ref, b_ref, fw_ref, fb_ref, cond_ref, mod_ref):
        cv = c_ref[...]
        cond = cv * _sigmoid(cv)
        cond_ref[...] = cond
        cb = cond.astype(BF16)
        mod_ref[:, 0:na] = _dot(cb, w_ref[...].astype(BF16)) + b_ref[...]
        mod_ref[:, na:na + nf] = _dot(cb, fw_ref[...].astype(BF16)) + fb_ref[...]

    return pl.pallas_call(
        body, name="mod_fwd",
        out_shape=[jax.ShapeDtypeStruct((nseq, D), F32), jax.ShapeDtypeStruct((nseq, na + nf), F32)],
        in_specs=[_VM] * 5, out_specs=[_VM] * 2, compiler_params=_cp(),
    )(c_all, ada_w_s, ada_b_s, fada_w_s, fada_b_s)


def _mod_bwd(cond_t, dsl, dall):
    nseq, n = dsl.shape
    bc = 512

    def body(ct_ref, dm_ref, da_ref, gw_ref, gb_ref):
        ct = ct_ref[...]
        dm = dm_ref[...]
        acc = ct[:, 0:1] * dm[0:1, :]
        for b in range(1, nseq):
            acc = acc + ct[:, b:b + 1] * dm[b:b + 1, :]
        gw_ref[...] = acc

        @pl.when(pl.program_id(0) == 0)
        def _():
            gb_ref[...] = _colsum(da_ref[...])

    return pl.pallas_call(
        body, name="mod_bwd", grid=(n // bc,),
        out_shape=[jax.ShapeDtypeStruct((D, n), F32), jax.ShapeDtypeStruct((1, dall.shape[1]), F32)],
        in_specs=[_VM, pl.BlockSpec((nseq, bc), lambda i: (0, i)), _VM],
        out_specs=[pl.BlockSpec((D, bc), lambda i: (0, i)), pl.BlockSpec((1, dall.shape[1]), lambda i: (0, 0))],
        compiler_params=_cp(("arbitrary",)),
    )(cond_t, dsl, dall)


def _f1_fwd(x, modp, g1, w_in, S, tm):
    n = x.shape[0]
    tps = S // tm

    def body(x_ref, mod_ref, g_ref, w_ref, h_ref, proj_ref):
        xhat, _ = _rms(x_ref[...], D)
        h = (xhat * g_ref[...]) * (1.0 + mod_ref[0, 1:2, :]) + mod_ref[0, 0:1, :]
        hb = h.astype(BF16)
        h_ref[...] = hb
        proj_ref[...] = _dot(hb, w_ref[...])

    return pl.pallas_call(
        body, name="f1_fwd", grid=(n // tm,),
        out_shape=[jax.ShapeDtypeStruct((n, D), BF16), jax.ShapeDtypeStruct((n, IN_PAD), F32)],
        in_specs=[pl.BlockSpec((tm, D), lambda i: (i, 0)),
                  pl.BlockSpec((1, 8, D), lambda i: (i // tps, 0, 0)), _VM, _VM],
        out_specs=[pl.BlockSpec((tm, D), lambda i: (i, 0)), pl.BlockSpec((tm, IN_PAD), lambda i: (i, 0))],
        compiler_params=_cp(("parallel",)),
    )(x, modp, g1, w_in)


def _f1_bwd(du, dmla, dx1, x, modp, g1, w_in, S, tm):
    n = x.shape[0]
    tps = S // tm
    nb = n // S

    def body(du_ref, dm_ref, dx1_ref, x_ref, mod_ref, g_ref, w_ref, dx_ref, dproj_ref, accs_ref, accg_ref):
        i = pl.program_id(0)
        dproj = jnp.concatenate([du_ref[...], dm_ref[...]], axis=1).astype(BF16)
        dproj_ref[...] = dproj
        dh = _dot_nt(dproj, w_ref[...])
        xhat, r = _rms(x_ref[...], D)
        g = g_ref[...]
        dn = dh * (1.0 + mod_ref[0, 1:2, :])
        dx_ref[...] = dx1_ref[...] + _rms_bwd(dn * g, xhat, r, D)

        @pl.when(i % tps == 0)
        def _():
            accs_ref[...] = jnp.zeros_like(accs_ref)

        @pl.when(i == 0)
        def _():
            accg_ref[...] = jnp.zeros_like(accg_ref)

        accs_ref[0, 0:1, :] += _colsum(dh)
        accs_ref[0, 1:2, :] += _colsum(dh * (xhat * g))
        accg_ref[0:1, :] += _colsum(dn * xhat)

    return pl.pallas_call(
        body, name="f1_bwd", grid=(n // tm,),
        out_shape=[jax.ShapeDtypeStruct((n, D), F32), jax.ShapeDtypeStruct((n, IN_PAD), BF16),
                   jax.ShapeDtypeStruct((nb, 8, D), F32), jax.ShapeDtypeStruct((8, D), F32)],
        in_specs=[pl.BlockSpec((tm, D_SSM), lambda i: (i, 0)), pl.BlockSpec((tm, IN_PAD - D_SSM), lambda i: (i, 0)),
                  pl.BlockSpec((tm, D), lambda i: (i, 0)), pl.BlockSpec((tm, D), lambda i: (i, 0)),
                  pl.BlockSpec((1, 8, D), lambda i: (i // tps, 0, 0)), _VM, _VM],
        out_specs=[pl.BlockSpec((tm, D), lambda i: (i, 0)), pl.BlockSpec((tm, IN_PAD), lambda i: (i, 0)),
                   pl.BlockSpec((1, 8, D), lambda i: (i // tps, 0, 0)), pl.BlockSpec((8, D), lambda i: (0, 0))],
        compiler_params=_cp(("arbitrary",)),
    )(du, dmla, dx1, x, modp, g1, w_in)


def _ssm_param_fwd(lam_re, lam_im, logdt, b_re, b_im):
    def body(lr_ref, li_ref, ld_ref, br_ref, bi_ref, lbr_ref, lbi_ref, bbr_ref, bbi_ref):
        lr, li = lr_ref[...], li_ref[...]
        dt = jnp.exp(ld_ref[...])
        er = jnp.exp(lr * dt)
        lbr = er * jnp.cos(li * dt)
        lbi = er * jnp.sin(li * dt)
        den = 1.0 / (lr * lr + li * li)
        cr = ((lbr - 1.0) * lr + lbi * li) * den
        ci = (lbi * lr - (lbr - 1.0) * li) * den
        lbr_ref[...] = lbr
        lbi_ref[...] = lbi
        bbr_ref[...] = cr * br_ref[...] - ci * bi_ref[...]
        bbi_ref[...] = cr * bi_ref[...] + ci * br_ref[...]

    return pl.pallas_call(
        body, name="ssm_param_fwd",
        out_shape=[jax.ShapeDtypeStruct((NST, 1), F32)] * 2 + [jax.ShapeDtypeStruct((NST, H), F32)] * 2,
        in_specs=[_VM] * 5, out_specs=[_VM] * 4, compiler_params=_cp(),
    )(lam_re, lam_im, logdt, b_re, b_im)


def _ssm_param_bwd(lam_re, lam_im, logdt, b_re, b_im, dlb_re, dlb_im, dbb_re, dbb_im):
    def body(lr_ref, li_ref, ld_ref, br_ref, bi_ref, dlr_ref, dli_ref, dbr_ref, dbi_ref,
             gbr_ref, gbi_ref, glr_ref, gli_ref, gdt_ref):
        lr, li = lr_ref[...], li_ref[...]
        dt = jnp.exp(ld_ref[...])
        er = jnp.exp(lr * dt)
        lbr = er * jnp.cos(li * dt)
        lbi = er * jnp.sin(li * dt)
        den = 1.0 / (lr * lr + li * li)
        nr, ni = lbr - 1.0, lbi
        cr = (nr * lr + ni * li) * den
        ci = (ni * lr - nr * li) * den
        br, bi = br_ref[...], bi_ref[...]
        dbr, dbi = dbr_ref[...], dbi_ref[...]
        gbr_ref[...] = cr * dbr + ci * dbi
        gbi_ref[...] = cr * dbi - ci * dbr
        gcr = jnp.sum(dbr * br + dbi * bi, axis=1, keepdims=True)
        gci = jnp.sum(dbi * br - dbr * bi, axis=1, keepdims=True)
        ilr, ili = lr * den, -li * den
        glbr = dlr_ref[...] + (gcr * ilr + gci * ili)
        glbi = dli_ref[...] + (gci * ilr - gcr * ili)
        qr = -(cr * ilr - ci * ili)
        qi = -(cr * ili + ci * ilr)
        glr = gcr * qr + gci * qi
        gli = gci * qr - gcr * qi
        glr = glr + dt * (glbr * lbr + glbi * lbi)
        gli = gli + dt * (glbi * lbr - glbr * lbi)
        wr = lr * lbr - li * lbi
        wi = lr * lbi + li * lbr
        glr_ref[...] = glr
        gli_ref[...] = gli
        gdt_ref[...] = (glbr * wr + glbi * wi) * dt

    return pl.pallas_call(
        body, name="ssm_param_bwd",
        out_shape=[jax.ShapeDtypeStruct((NST, H), F32)] * 2 + [jax.ShapeDtypeStruct((NST, 1), F32)] * 3,
        in_specs=[_VM] * 9, out_specs=[_VM] * 5, compiler_params=_cp(),
    )(lam_re, lam_im, logdt, b_re, b_im, dlb_re, dlb_im, dbb_re, dbb_im)


def _rowsum(a):
    def body(a_ref, o_ref):
        o_ref[...] = jnp.sum(a_ref[...], axis=1, keepdims=True)

    return pl.pallas_call(
        body, name="rowsum", out_shape=jax.ShapeDtypeStruct((a.shape[0], 1), F32),
        in_specs=[_VM], out_specs=_VM, compiler_params=_cp(),
    )(a)


QB = D_SSM // 4
QS = 4 * QB


def _bd_lo(part, q):
    return part * NST + q * QS


def _bd_expand(ub, bm_ref, out_ref):
    for part in range(2):
        for q in range(4):
            lo = _bd_lo(part, q)
            out_ref[:, lo:lo + QS] = _dot(ub[:, q * QB:(q + 1) * QB], bm_ref[:, lo:lo + QS])


def _bd_expand_t(db, cm_ref, out_ref):
    for part in range(2):
        for q in range(4):
            lo = _bd_lo(part, q)
            out_ref[:, lo:lo + QS] = _dot_nt(db[:, q * QB:(q + 1) * QB], cm_ref[lo:lo + QS, :])


def _bd_project(sb, cm_ref):
    return jnp.concatenate(
        [_dot(sb[:, _bd_lo(0, q):_bd_lo(0, q) + QS], cm_ref[_bd_lo(0, q):_bd_lo(0, q) + QS, :])
         + _dot(sb[:, _bd_lo(1, q):_bd_lo(1, q) + QS], cm_ref[_bd_lo(1, q):_bd_lo(1, q) + QS, :])
         for q in range(4)], axis=1)


def _bd_project_t(ab, bm_ref):
    return jnp.concatenate(
        [_dot_nt(ab[:, _bd_lo(0, q):_bd_lo(0, q) + QS], bm_ref[:, _bd_lo(0, q):_bd_lo(0, q) + QS])
         + _dot_nt(ab[:, _bd_lo(1, q):_bd_lo(1, q) + QS], bm_ref[:, _bd_lo(1, q):_bd_lo(1, q) + QS])
         for q in range(4)], axis=1)


def _pow2k(pr, pi, nsq):
    for _ in range(nsq):
        pr, pi = pr * pr - pi * pi, 2.0 * pr * pi
    return pr, pi


def _ssm_local(u_p, bm, lre8, lim8, S, tt):
    n = u_p.shape[0]
    nb, nt = n // S, S // tt
    nsq = int(round(math.log2(S // 8)))
    assert 2 ** nsq == S // 8

    def body(u_ref, bm_ref, lre_ref, lim_ref, cre_ref, cim_ref, sre, sim, bu):
        j = pl.program_id(1)

        @pl.when(j == 0)
        def _():
            sre[...] = jnp.zeros_like(sre)
            sim[...] = jnp.zeros_like(sim)

        _bd_expand(u_ref[...].astype(BF16), bm_ref, bu)
        lre, lim = lre_ref[...], lim_ref[...]

        def step(i, c):
            sr, si = c
            off = pl.multiple_of(i * 8, 8)
            br = bu[pl.ds(off, 8), 0:NST]
            bi = bu[pl.ds(off, 8), NST:2 * NST]
            return lre * sr - lim * si + br, lre * si + lim * sr + bi

        sr, si = lax.fori_loop(0, tt // 8, step, (sre[...], sim[...]))
        sre[...] = sr
        sim[...] = si

        @pl.when(j == nt - 1)
        def _():
            pr, pi = _pow2k(lre[0:1], lim[0:1], nsq)
            cr = jnp.zeros((1, NST), F32)
            ci = jnp.zeros((1, NST), F32)
            cre_ref[0:1, :] = cr
            cim_ref[0:1, :] = ci
            for k in range(1, 8):
                cr, ci = sr[k - 1:k] + pr * cr - pi * ci, si[k - 1:k] + pr * ci + pi * cr
                cre_ref[k:k + 1, :] = cr
                cim_ref[k:k + 1, :] = ci

    return pl.pallas_call(
        body, name="ssm_local", grid=(nb, nt),
        out_shape=[jax.ShapeDtypeStruct((nb * 8, NST), F32)] * 2,
        in_specs=[pl.BlockSpec((tt, D_SSM), lambda b, j: (b * nt + j, 0)), _VM, _VM, _VM],
        out_specs=[pl.BlockSpec((8, NST), lambda b, j: (b, 0))] * 2,
        scratch_shapes=[pltpu.VMEM((8, NST), F32), pltpu.VMEM((8, NST), F32), pltpu.VMEM((tt, 2 * NST), F32)],
        compiler_params=_cp(("arbitrary", "arbitrary")),
    )(u_p, bm, lre8, lim8)


def _ssm_fwd(u_p, cre, cim, bm, cm, dvec, w_glu, lre8, lim8, S, tt):
    n = u_p.shape[0]
    nb, nt = n // S, S // tt

    def body(u_ref, cre_ref, cim_ref, bm_ref, cm_ref, d_ref, wg_ref, lre_ref, lim_ref,
             st_ref, ypre_ref, z_ref, gact_ref, yssm_ref, sre, sim, bu):
        j = pl.program_id(1)

        @pl.when(j == 0)
        def _():
            sre[...] = cre_ref[...]
            sim[...] = cim_ref[...]

        u = u_ref[...]
        _bd_expand(u.astype(BF16), bm_ref, bu)
        lre, lim = lre_ref[...], lim_ref[...]

        def step(i, c):
            sr, si = c
            off = pl.multiple_of(i * 8, 8)
            nr = lre * sr - lim * si + bu[pl.ds(off, 8), 0:NST]
            ni = lre * si + lim * sr + bu[pl.ds(off, 8), NST:2 * NST]
            bu[pl.ds(off, 8), 0:NST] = nr
            bu[pl.ds(off, 8), NST:2 * NST] = ni
            return nr, ni

        sr, si = lax.fori_loop(0, tt // 8, step, (sre[...], sim[...]))
        sre[...] = sr
        sim[...] = si
        stb = bu[...].astype(BF16)
        st_ref[...] = stb
        y = _bd_project(stb, cm_ref) + d_ref[...] * u
        ypre_ref[...] = y
        gb = _gelu(y).astype(BF16)
        gact_ref[...] = gb
        z = _dot(gb, wg_ref[...])
        z_ref[...] = z
        yssm_ref[...] = z[:, 0:D_SSM] * _sigmoid(z[:, D_SSM:2 * D_SSM])

    row = lambda w: pl.BlockSpec((tt, w), lambda b, j: (b * nt + j, 0))
    return pl.pallas_call(
        body, name="ssm_fwd", grid=(nb, nt),
        out_shape=[jax.ShapeDtypeStruct((n, 2 * NST), BF16), jax.ShapeDtypeStruct((n, D_SSM), F32),
                   jax.ShapeDtypeStruct((n, 2 * D_SSM), F32), jax.ShapeDtypeStruct((n, D_SSM), BF16),
                   jax.ShapeDtypeStruct((n, D_SSM), F32)],
        in_specs=[row(D_SSM), pl.BlockSpec((8, NST), lambda b, j: (b, 0)), pl.BlockSpec((8, NST), lambda b, j: (b, 0)),
                  _VM, _VM, _VM, _VM, _VM, _VM],
        out_specs=[row(2 * NST), row(D_SSM), row(2 * D_SSM), row(D_SSM), row(D_SSM)],
        scratch_shapes=[pltpu.VMEM((8, NST), F32), pltpu.VMEM((8, NST), F32), pltpu.VMEM((tt, 2 * NST), F32)],
        compiler_params=_cp(("arbitrary", "arbitrary")),
    )(u_p, cre, cim, bm, cm, dvec, w_glu, lre8, lim8)


def _ssm_bwd_a(dys_p, z, ypre, w_glu, cm, lre8, lim8, S, tt):
    n = z.shape[0]
    nb, nt = n // S, S // tt
    nsq = int(round(math.log2(S // 8)))
    ng = tt // 8

    def body(dys_ref, z_ref, y_ref, wg_ref, cm_ref, lre_ref, lim_ref, dy_ref, dz_ref, are_ref, aim_ref, sre, sim, gb):
        j = pl.program_id(1)

        @pl.when(j == 0)
        def _():
            sre[...] = jnp.zeros_like(sre)
            sim[...] = jnp.zeros_like(sim)

        z = z_ref[...]
        z1, z2 = z[:, 0:D_SSM], z[:, D_SSM:2 * D_SSM]
        sg = _sigmoid(z2)
        dys = dys_ref[...]
        dz = jnp.concatenate([dys * sg, dys * z1 * sg * (1.0 - sg)], axis=1).astype(BF16)
        dz_ref[...] = dz
        dy = _dot_nt(dz, wg_ref[...]) * _gelu_grad(y_ref[...])
        dy_ref[...] = dy
        _bd_expand_t(dy.astype(BF16), cm_ref, gb)
        lre, lim = lre_ref[...], lim_ref[...]

        def step(i, c):
            ar, ai = c
            off = pl.multiple_of((ng - 1 - i) * 8, 8)
            gr = gb[pl.ds(off, 8), 0:NST]
            gi = gb[pl.ds(off, 8), NST:2 * NST]
            return lre * ar + lim * ai + gr, lre * ai - lim * ar + gi

        ar, ai = lax.fori_loop(0, ng, step, (sre[...], sim[...]))
        sre[...] = ar
        sim[...] = ai

        @pl.when(j == nt - 1)
        def _():
            pr, pi = _pow2k(lre[0:1], -lim[0:1], nsq)
            cr = jnp.zeros((1, NST), F32)
            ci = jnp.zeros((1, NST), F32)
            are_ref[7:8, :] = cr
            aim_ref[7:8, :] = ci
            for k in range(6, -1, -1):
                cr, ci = ar[k + 1:k + 2] + pr * cr - pi * ci, ai[k + 1:k + 2] + pr * ci + pi * cr
                are_ref[k:k + 1, :] = cr
                aim_ref[k:k + 1, :] = ci

    row = lambda w: pl.BlockSpec((tt, w), lambda b, j: (b * nt + nt - 1 - j, 0))
    return pl.pallas_call(
        body, name="ssm_bwd_a", grid=(nb, nt),
        out_shape=[jax.ShapeDtypeStruct((n, D_SSM), F32), jax.ShapeDtypeStruct((n, 2 * D_SSM), BF16),
                   jax.ShapeDtypeStruct((nb * 8, NST), F32), jax.ShapeDtypeStruct((nb * 8, NST), F32)],
        in_specs=[row(D_SSM), row(2 * D_SSM), row(D_SSM), _VM, _VM, _VM, _VM],
        out_specs=[row(D_SSM), row(2 * D_SSM), pl.BlockSpec((8, NST), lambda b, j: (b, 0)),
                   pl.BlockSpec((8, NST), lambda b, j: (b, 0))],
        scratch_shapes=[pltpu.VMEM((8, NST), F32), pltpu.VMEM((8, NST), F32), pltpu.VMEM((tt, 2 * NST), F32)],
        compiler_params=_cp(("arbitrary", "arbitrary")),
    )(dys_p, z, ypre, w_glu, cm, lre8, lim8)


def _ssm_bwd_b(dy, u_p, st, fcr, fci, air, aii, bm, cm, dvec, lre8, lim8, S, tt):
    n = u_p.shape[0]
    nb, nt = n // S, S // tt
    ng = tt // 8

    def body(dy_ref, u_ref, st_ref, stp_ref, fcr_ref, fci_ref, air_ref, aii_ref, bm_ref, cm_ref, d_ref, lre_ref, lim_ref,
             du_ref, dcm_ref, dbm_ref, dd_ref, dlr_ref, dli_ref, are, aim, accr, acci, sp, ab):
        b = pl.program_id(0)
        j = pl.program_id(1)
        jt = nt - 1 - j

        @pl.when((b == 0) & (j == 0))
        def _():
            dcm_ref[...] = jnp.zeros_like(dcm_ref)
            dbm_ref[...] = jnp.zeros_like(dbm_ref)
            dd_ref[...] = jnp.zeros_like(dd_ref)
            accr[...] = jnp.zeros_like(accr)
            acci[...] = jnp.zeros_like(acci)

        @pl.when(j == 0)
        def _():
            are[...] = air_ref[...]
            aim[...] = aii_ref[...]

        sp[8:tt + 8, :] = st_ref[...].astype(F32)

        @pl.when(jt == 0)
        def _():
            sp[0:8, 0:NST] = fcr_ref[...]
            sp[0:8, NST:2 * NST] = fci_ref[...]

        @pl.when(jt != 0)
        def _():
            sp[0:8, :] = stp_ref[8:16, :].astype(F32)

        dy = dy_ref[...]
        u = u_ref[...]
        dyb = dy.astype(BF16)
        _bd_expand_t(dyb, cm_ref, ab)
        lre, lim = lre_ref[...], lim_ref[...]

        def step(i, c):
            ar, ai = c
            off = pl.multiple_of((ng - 1 - i) * 8, 8)
            nr = lre * ar + lim * ai + ab[pl.ds(off, 8), 0:NST]
            ni = lre * ai - lim * ar + ab[pl.ds(off, 8), NST:2 * NST]
            ab[pl.ds(off, 8), 0:NST] = nr
            ab[pl.ds(off, 8), NST:2 * NST] = ni
            pr = sp[pl.ds(off, 8), 0:NST]
            pi = sp[pl.ds(off, 8), NST:2 * NST]
            accr[...] += nr * pr + ni * pi
            acci[...] += ni * pr - nr * pi
            return nr, ni

        ar, ai = lax.fori_loop(0, ng, step, (are[...], aim[...]))
        are[...] = ar
        aim[...] = ai
        a_b = ab[...].astype(BF16)
        du_ref[...] = _bd_project_t(a_b, bm_ref) + d_ref[...] * dy
        ub = u.astype(BF16)
        for q in range(4):
            for part in range(2):
                lo = part * NST + q * 4 * QB
                s_q = st_ref[:, lo:lo + 4 * QB]
                dcm_ref[lo:lo + 4 * QB, :] += _dot_tn(s_q, dyb[:, q * QB:(q + 1) * QB])
                dbm_ref[:, lo:lo + 4 * QB] += _dot_tn(ub[:, q * QB:(q + 1) * QB], a_b[:, lo:lo + 4 * QB])
        dd_ref[...] += _colsum(dy * u)

        @pl.when((b == nb - 1) & (j == nt - 1))
        def _():
            dlr_ref[...] = _colsum(accr[...])
            dli_ref[...] = _colsum(acci[...])

    row = lambda w: pl.BlockSpec((tt, w), lambda b, j: (b * nt + nt - 1 - j, 0))
    seq8 = pl.BlockSpec((8, NST), lambda b, j: (b, 0))
    prev = pl.BlockSpec((16, 2 * NST), lambda b, j: (jnp.maximum((b * nt + nt - 1 - j) * (tt // 16) - 1, 0), 0))
    const = lambda shape: pl.BlockSpec(shape, lambda b, j: (0, 0))
    return pl.pallas_call(
        body, name="ssm_bwd_b", grid=(nb, nt),
        out_shape=[jax.ShapeDtypeStruct((n, D_SSM), F32), jax.ShapeDtypeStruct((2 * NST, QB), F32),
                   jax.ShapeDtypeStruct((QB, 2 * NST), F32), jax.ShapeDtypeStruct((1, D_SSM), F32),
                   jax.ShapeDtypeStruct((1, NST), F32), jax.ShapeDtypeStruct((1, NST), F32)],
        in_specs=[row(D_SSM), row(D_SSM), row(2 * NST), prev, seq8, seq8, seq8, seq8, _VM, _VM, _VM, _VM, _VM],
        out_specs=[row(D_SSM), const((2 * NST, QB)), const((QB, 2 * NST)), const((1, D_SSM)),
                   const((1, NST)), const((1, NST))],
        scratch_shapes=[pltpu.VMEM((8, NST), F32)] * 4 + [pltpu.VMEM((tt + 8, 2 * NST), F32),
                                                          pltpu.VMEM((tt, 2 * NST), F32)],
        compiler_params=_cp(("arbitrary", "arbitrary")),
    )(dy, u_p, st, st, fcr, fci, air, aii, bm, cm, dvec, lre8, lim8)


def _rope(v, c, s1, s2):
    return v * c + _roll(v, -16) * s1 + _roll(v, 16) * s2


def _rope_t(dv, c, s1, s2):
    return dv * c + _roll(dv * s1, 16) + _roll(dv * s2, -16)


def _mla_fwd(proj, rc, rs1, rs2, gq, gkv, w_uq, w_ukv, tm):
    n = proj.shape[0]

    def body(ql_ref, kvl_ref, kr_ref, c_ref, s1_ref, s2_ref, gq_ref, gkv_ref, wq_ref, wkv_ref,
             q_ref, k_ref, v_ref, qn_ref, kvn_ref):
        c, s1, s2 = c_ref[...], s1_ref[...], s2_ref[...]
        qhat, _ = _rms(ql_ref[...], Q_LORA)
        qn = (qhat * gq_ref[...]).astype(BF16)
        qn_ref[...] = qn
        q = _dot(qn, wq_ref[...])
        qr = _rope(q, jnp.tile(c, (1, NH)), jnp.tile(s1, (1, NH)), jnp.tile(s2, (1, NH)))
        q_ref[...] = (qr * _C2).astype(BF16)
        khat, _ = _rms(kvl_ref[...], KV_LORA)
        kvn = (khat * gkv_ref[...]).astype(BF16)
        kvn_ref[...] = kvn
        kv = _dot(kvn, wkv_ref[...])
        kr = _rope(_roll(kr_ref[...], 64), c, s1, s2)
        k_ref[...] = (kv[:, 0:NH * HP] + jnp.tile(kr, (1, NH))).astype(BF16)
        v_ref[...] = kv[:, NH * HP:2 * NH * HP].astype(BF16)

    def wrapped(proj_ref, *rest):
        ql = proj_ref.at[:, D_SSM:D_SSM + Q_LORA]
        kvl = proj_ref.at[:, D_SSM + Q_LORA:D_SSM + Q_LORA + KV_LORA]
        kr = proj_ref.at[:, IN_PAD - HP:IN_PAD]
        body(ql, kvl, kr, *rest)

    row = lambda w: pl.BlockSpec((tm, w), lambda i: (i, 0))
    return pl.pallas_call(
        wrapped, name="mla_fwd", grid=(n // tm,),
        out_shape=[jax.ShapeDtypeStruct((n, NH * HP), BF16)] * 3 +
                  [jax.ShapeDtypeStruct((n, Q_LORA), BF16), jax.ShapeDtypeStruct((n, KV_LORA), BF16)],
        in_specs=[row(IN_PAD), row(HP), row(HP), row(HP), _VM, _VM, _VM, _VM],
        out_specs=[row(NH * HP)] * 3 + [row(Q_LORA), row(KV_LORA)],
        compiler_params=_cp(("parallel",)),
    )(proj, rc, rs1, rs2, gq, gkv, w_uq, w_ukv)


def _mla_bwd(dq, dk, dv, proj, rc, rs1, rs2, gq, gkv, w_uq, w_ukv, tm):
    n = proj.shape[0]

    def body(dq_ref, dk_ref, dv_ref, proj_ref, c_ref, s1_ref, s2_ref, gq_ref, gkv_ref, wq_ref, wkv_ref,
             dmla_ref, dqb_ref, dkvb_ref, acc_ref):
        i = pl.program_id(0)
        c, s1, s2 = c_ref[...], s1_ref[...], s2_ref[...]
        dqu = _rope_t(dq_ref[...] * _SCALE, jnp.tile(c, (1, NH)), jnp.tile(s1, (1, NH)),
                      jnp.tile(s2, (1, NH))).astype(BF16)
        dqb_ref[...] = dqu
        dqn = _dot_nt(dqu, wq_ref[...])
        qhat, rq = _rms(proj_ref[:, D_SSM:D_SSM + Q_LORA], Q_LORA)
        dql = _rms_bwd(dqn * gq_ref[...], qhat, rq, Q_LORA)
        dkf = dk_ref[...] * (1.0 / _LOG2E)
        dkv = jnp.concatenate([dkf.astype(BF16), dv_ref[...].astype(BF16)], axis=1)
        dkvb_ref[...] = dkv
        dkvn = _dot_nt(dkv, wkv_ref[...])
        khat, rk = _rms(proj_ref[:, D_SSM + Q_LORA:D_SSM + Q_LORA + KV_LORA], KV_LORA)
        dkvl = _rms_bwd(dkvn * gkv_ref[...], khat, rk, KV_LORA)
        dkr = dkf[:, 0:HP]
        for h in range(1, NH):
            dkr = dkr + dkf[:, h * HP:(h + 1) * HP]
        lane = lax.broadcasted_iota(jnp.int32, dkr.shape, 1)
        dkr = jnp.where((lane >= QK_NOPE) & (lane < QK_NOPE + QK_ROPE), dkr, 0.0)
        dkr = _roll(_rope_t(dkr, c, s1, s2), -64)
        dmla_ref[...] = jnp.concatenate([dql, dkvl, dkr], axis=1)

        @pl.when(i == 0)
        def _():
            acc_ref[...] = jnp.zeros_like(acc_ref)

        acc_ref[0:1, 0:Q_LORA] += _colsum(dqn * qhat)
        acc_ref[1:2, 0:KV_LORA] += _colsum(dkvn * khat)

    row = lambda w: pl.BlockSpec((tm, w), lambda i: (i, 0))
    return pl.pallas_call(
        body, name="mla_bwd", grid=(n // tm,),
        out_shape=[jax.ShapeDtypeStruct((n, IN_PAD - D_SSM), F32), jax.ShapeDtypeStruct((n, NH * HP), BF16),
                   jax.ShapeDtypeStruct((n, 2 * NH * HP), BF16), jax.ShapeDtypeStruct((8, Q_LORA), F32)],
        in_specs=[row(NH * HP)] * 3 + [row(IN_PAD), row(HP), row(HP), row(HP), _VM, _VM, _VM, _VM],
        out_specs=[row(IN_PAD - D_SSM), row(NH * HP), row(2 * NH * HP), pl.BlockSpec((8, Q_LORA), lambda i: (0, 0))],
        compiler_params=_cp(("arbitrary",)),
    )(dq, dk, dv, proj, rc, rs1, rs2, gq, gkv, w_uq, w_ukv)


_SCALE = (QK_NOPE + QK_ROPE) ** -0.5
_LOG2E = 1.4426950408889634
_C2 = _SCALE * _LOG2E


def _attn_fwd(q, k, v, S, tq):
    n = q.shape[0]
    nb, nq = n // S, S // tq

    def body(q_ref, k_ref, v_ref, o_ref, lr_ref):
        qi = pl.program_id(2)
        qv = q_ref[...]

        def tile(j, c, diagonal):
            m, l, acc = c
            off = pl.multiple_of(j * tq, tq)
            s = _dot_nt(qv, k_ref[pl.ds(off, tq), :])
            if diagonal:
                rows = lax.broadcasted_iota(jnp.int32, s.shape, 0)
                cols = lax.broadcasted_iota(jnp.int32, s.shape, 1)
                s = jnp.where(cols <= rows, s, NEG)
            mn = jnp.maximum(m, jnp.max(s, axis=1, keepdims=True))
            p = jnp.exp2(s - mn)
            al = jnp.exp2(m - mn)
            l = al * l + jnp.sum(p, axis=1, keepdims=True)
            acc = al * acc + _dot(p.astype(BF16), v_ref[pl.ds(off, tq), :])
            return mn, l, acc

        init = (jnp.full((tq, 1), NEG, F32), jnp.zeros((tq, 1), F32), jnp.zeros((tq, HP), F32))
        c = lax.fori_loop(0, qi, lambda j, c: tile(j, c, False), init)
        m, l, acc = tile(qi, c, True)
        o_ref[...] = (acc / l).astype(BF16)
        lane = lax.broadcasted_iota(jnp.int32, (8, HP), 1)
        lse = jnp.broadcast_to(m + jnp.log(l) * _LOG2E, (tq, HP))
        lr_ref[...] = _rows_of(lse, jnp.where(lane == 0, 1.0, 0.0).astype(BF16))

    qs = pl.BlockSpec((tq, HP), lambda b, h, i: (b * nq + i, h))
    ks = pl.BlockSpec((S, HP), lambda b, h, i: (b, h))
    return pl.pallas_call(
        body, name="attn_fwd", grid=(nb, NH, nq),
        out_shape=[jax.ShapeDtypeStruct((n, NH * HP), BF16), jax.ShapeDtypeStruct((nb * NH * 8, S), F32)],
        in_specs=[qs, ks, ks], out_specs=[qs, pl.BlockSpec((8, tq), lambda b, h, i: (b * NH + h, i))],
        compiler_params=_cp(("parallel", "parallel", "arbitrary")),
    )(q, k, v)


def _rows_of(x, pick):
    x1 = x.astype(BF16)
    r1 = x - x1.astype(F32)
    x2 = r1.astype(BF16)
    x3 = (r1 - x2.astype(F32)).astype(BF16)
    return _dot_nt(pick, x1) + _dot_nt(pick, x2) + _dot_nt(pick, x3)


def _attn_bwd(q, k, v, dob, lrow, drow, S, tq):
    n = q.shape[0]
    nb, nq = n // S, S // tq

    def body(q_ref, k_ref, v_ref, do_ref, lr_ref, dr_ref, dqo_ref, dk_ref, dv_ref, dq_ref):
        kj = pl.program_id(2)

        @pl.when(kj == 0)
        def _():
            dq_ref[...] = jnp.zeros_like(dq_ref)

        kt = k_ref[...]
        vt = v_ref[...]

        def tile(i, c, diagonal):
            dk, dv = c
            off = pl.multiple_of(i * tq, tq)
            qv = q_ref[pl.ds(off, tq), :]
            dob = do_ref[pl.ds(off, tq), :]
            lr = lr_ref[0:1, pl.ds(off, tq)]
            dr = dr_ref[0:1, pl.ds(off, tq)]
            st = _dot_nt(kt, qv)
            dpt = _dot_nt(vt, dob)
            pt = jnp.exp2(st - lr)
            if diagonal:
                keys = lax.broadcasted_iota(jnp.int32, pt.shape, 0)
                qrys = lax.broadcasted_iota(jnp.int32, pt.shape, 1)
                pt = jnp.where(keys <= qrys, pt, 0.0)
            dst = (pt * (dpt - dr)).astype(BF16)
            dq_ref[pl.ds(off, tq), :] += _dot_tn(dst, kt)
            return dk + _dot(dst, qv), dv + _dot(pt.astype(BF16), dob)

        zero = jnp.zeros((tq, HP), F32)
        c = tile(kj, (zero, zero), True)
        dk, dv = lax.fori_loop(kj + 1, nq, lambda i, c: tile(i, c, False), c)
        dk_ref[...] = dk.astype(BF16)
        dv_ref[...] = dv.astype(BF16)

        @pl.when(kj == nq - 1)
        def _():
            dqo_ref[...] = dq_ref[...].astype(BF16)

    ts = pl.BlockSpec((tq, HP), lambda b, h, i: (b * nq + i, h))
    fs = pl.BlockSpec((S, HP), lambda b, h, i: (b, h))
    rs = pl.BlockSpec((8, S), lambda b, h, i: (b * NH + h, 0))
    return pl.pallas_call(
        body, name="attn_bwd", grid=(nb, NH, nq),
        out_shape=[jax.ShapeDtypeStruct((n, NH * HP), BF16)] * 3,
        in_specs=[fs, ts, ts, fs, rs, rs], out_specs=[fs, ts, ts],
        scratch_shapes=[pltpu.VMEM((S, HP), F32)],
        compiler_params=_cp(("parallel", "parallel", "arbitrary")),
    )(q, k, v, dob, lrow, drow)


def _p1_fwd(yssm, oattn, x, modp, gs, ga, w_out, g2, S, tm):
    n = x.shape[0]
    tps = S // tm

    def body(ys_ref, oa_ref, x_ref, mod_ref, gs_ref, ga_ref, w_ref, g2_ref, yn_ref, o_ref, x1_ref, h2_ref):
        yh, _ = _rms(ys_ref[...], D_SSM)
        ah, _ = _rms(oa_ref[...].astype(F32), D_ATTN)
        yn = jnp.concatenate([yh * gs_ref[...], ah * ga_ref[...]], axis=1).astype(BF16)
        yn_ref[...] = yn
        o = _dot(yn, w_ref[...])
        o_ref[...] = o.astype(BF16)
        x1 = x_ref[...] + mod_ref[0, 2:3, :] * o
        x1_ref[...] = x1
        xh, _ = _rms(x1, D)
        h2_ref[...] = ((xh * g2_ref[...]) * (1.0 + mod_ref[0, 4:5, :]) + mod_ref[0, 3:4, :]).astype(BF16)

    row = lambda w: pl.BlockSpec((tm, w), lambda i: (i, 0))
    return pl.pallas_call(
        body, name="p1_fwd", grid=(n // tm,),
        out_shape=[jax.ShapeDtypeStruct((n, D_SSM + NH * HP), BF16), jax.ShapeDtypeStruct((n, D), BF16),
                   jax.ShapeDtypeStruct((n, D), F32), jax.ShapeDtypeStruct((n, D), BF16)],
        in_specs=[row(D_SSM), row(NH * HP), row(D), pl.BlockSpec((1, 8, D), lambda i: (i // tps, 0, 0)),
                  _VM, _VM, _VM, _VM],
        out_specs=[row(D_SSM + NH * HP), row(D), row(D), row(D)],
        compiler_params=_cp(("parallel",)),
    )(yssm, oattn, x, modp, gs, ga, w_out, g2)


def _p2(x1, h2, target, modp, g2, gf, w_ff1, w_ff2, S, tm):
    n = x1.shape[0]
    tps = S // tm
    nb = n // S

    def body(x1_ref, h2_ref, t_ref, mod_ref, g2_ref, gf_ref, w1_ref, w2_ref,
             dx1_ref, r_ref, da_ref, dff_ref, accs_ref, accg_ref):
        i = pl.program_id(0)
        sh2, sc2, gt2 = mod_ref[0, 3:4, :], mod_ref[0, 4:5, :], mod_ref[0, 5:6, :]
        fsh, fsc = mod_ref[0, 6:7, :], mod_ref[0, 7:8, :]
        x1 = x1_ref[...]
        a = _dot(h2_ref[...], w1_ref[...])
        ra = jnp.maximum(a, 0.0)
        rb = (ra * ra).astype(BF16)
        r_ref[...] = rb
        ff = _dot(rb, w2_ref[...])
        x2 = x1 + gt2 * ff
        x2h, rf = _rms(x2, D)
        gf_v = gf_ref[...]
        outn = x2h * gf_v
        err = outn * (1.0 + fsc) + fsh - t_ref[...]
        dout = err * (1.0 / D)
        doutn = dout * (1.0 + fsc)
        dx2 = _rms_bwd(doutn * gf_v, x2h, rf, D)
        dff = (gt2 * dx2).astype(BF16)
        dff_ref[...] = dff
        dr = _dot_nt(dff, w2_ref[...])
        da = (dr * (2.0 * ra)).astype(BF16)
        da_ref[...] = da
        dh2 = _dot_nt(da, w1_ref[...])
        x1h, r2 = _rms(x1, D)
        g2_v = g2_ref[...]
        dn2 = dh2 * (1.0 + sc2)
        dx1_ref[...] = dx2 + _rms_bwd(dn2 * g2_v, x1h, r2, D)

        @pl.when(i % tps == 0)
        def _():
            accs_ref[...] = jnp.zeros_like(accs_ref)

        @pl.when(i == 0)
        def _():
            accg_ref[...] = jnp.zeros_like(accg_ref)

        accs_ref[0, 3:4, :] += _colsum(dh2)
        accs_ref[0, 4:5, :] += _colsum(dh2 * (x1h * g2_v))
        accs_ref[0, 5:6, :] += _colsum(dx2 * ff)
        accs_ref[0, 6:7, :] += _colsum(dout)
        accs_ref[0, 7:8, :] += _colsum(dout * outn)
        accg_ref[0:1, :] += _colsum(dn2 * x1h)
        accg_ref[1:2, :] += _colsum(doutn * x2h)
        accg_ref[2:3, :] += _colsum(err * err) * (0.5 / D)

    row = lambda w: pl.BlockSpec((tm, w), lambda i: (i, 0))
    return pl.pallas_call(
        body, name="p2_mlp_loss", grid=(n // tm,),
        out_shape=[jax.ShapeDtypeStruct((n, D), F32), jax.ShapeDtypeStruct((n, D_FF), BF16),
                   jax.ShapeDtypeStruct((n, D_FF), BF16), jax.ShapeDtypeStruct((n, D), BF16),
                   jax.ShapeDtypeStruct((nb, 8, D), F32), jax.ShapeDtypeStruct((8, D), F32)],
        in_specs=[row(D), row(D), row(D), pl.BlockSpec((1, 8, D), lambda i: (i // tps, 0, 0)), _VM, _VM, _VM, _VM],
        out_specs=[row(D), row(D_FF), row(D_FF), row(D), pl.BlockSpec((1, 8, D), lambda i: (i // tps, 0, 0)),
                   pl.BlockSpec((8, D), lambda i: (0, 0))],
        compiler_params=_cp(("arbitrary",)),
    )(x1, h2, target, modp, g2, gf, w_ff1, w_ff2)


def _p3_bwd(dx1, o, yssm, oattn, modp, gs, ga, w_out, S, tm):
    n = dx1.shape[0]
    tps = S // tm
    nb = n // S

    def body(dx1_ref, o_ref, ys_ref, oa_ref, mod_ref, gs_ref, ga_ref, w_ref,
             do_ref, dys_ref, doa_ref, dr_ref, accs_ref, accg_ref):
        i = pl.program_id(0)
        dx1 = dx1_ref[...]
        dob = (mod_ref[0, 2:3, :] * dx1).astype(BF16)
        do_ref[...] = dob
        dyn = _dot_nt(dob, w_ref[...])
        yh, rs = _rms(ys_ref[...], D_SSM)
        oa = oa_ref[...].astype(F32)
        ah, ra = _rms(oa, D_ATTN)
        d1 = dyn[:, 0:D_SSM]
        d2 = dyn[:, D_SSM:D_SSM + NH * HP]
        dys_ref[...] = _rms_bwd(d1 * gs_ref[...], yh, rs, D_SSM)
        doa = _rms_bwd(d2 * ga_ref[...], ah, ra, D_ATTN)
        doa_ref[...] = doa.astype(BF16)
        prod = doa * oa
        ones = jnp.ones((8, HP), BF16)
        for h in range(NH):
            dr_ref[h * 8:(h + 1) * 8, :] = _rows_of(prod[:, h * HP:(h + 1) * HP], ones)

        @pl.when(i % tps == 0)
        def _():
            accs_ref[...] = jnp.zeros_like(accs_ref)

        @pl.when(i == 0)
        def _():
            accg_ref[...] = jnp.zeros_like(accg_ref)

        accs_ref[0, 2:3, :] += _colsum(dx1 * o_ref[...])
        accg_ref[0:1, 0:D_SSM] += _colsum(d1 * yh)
        accg_ref[1:2, :] += _colsum(d2 * ah)

    row = lambda w: pl.BlockSpec((tm, w), lambda i: (i, 0))
    return pl.pallas_call(
        body, name="p3_bwd", grid=(n // tm,),
        out_shape=[jax.ShapeDtypeStruct((n, D), BF16), jax.ShapeDtypeStruct((n, D_SSM), F32),
                   jax.ShapeDtypeStruct((n, NH * HP), BF16), jax.ShapeDtypeStruct((nb * NH * 8, S), F32),
                   jax.ShapeDtypeStruct((nb, 8, D), F32), jax.ShapeDtypeStruct((8, NH * HP), F32)],
        in_specs=[row(D), row(D), row(D_SSM), row(NH * HP), pl.BlockSpec((1, 8, D), lambda i: (i // tps, 0, 0)),
                  _VM, _VM, _VM],
        out_specs=[row(D), row(D_SSM), row(NH * HP), pl.BlockSpec((NH * 8, tm), lambda i: (i // tps, i % tps)),
                   pl.BlockSpec((1, 8, D), lambda i: (i // tps, 0, 0)), pl.BlockSpec((8, NH * HP), lambda i: (0, 0))],
        compiler_params=_cp(("arbitrary",)),
    )(dx1, o, yssm, oattn, modp, gs, ga, w_out)


def _wgrad(a, b, name, col_slots=0):
    n, k1 = a.shape
    k2 = b.shape[1]
    bn = next((b for b in (1024, 512) if n % b == 0), n)
    bk1 = next((b for b in (1024, 512) if k1 % b == 0), k1)
    bk2 = k2 // col_slots if col_slots else (1024 if (k2 % 1024 == 0) else k2)

    def body(a_ref, b_ref, o_ref):
        @pl.when(pl.program_id(2) == 0)
        def _():
            o_ref[...] = jnp.zeros_like(o_ref)

        o_ref[...] += _dot_tn(a_ref[...], b_ref[...]).reshape(o_ref.shape)

    if col_slots:
        out_shape = jax.ShapeDtypeStruct((col_slots, k1, bk2), F32)
        out_spec = pl.BlockSpec((1, bk1, bk2), lambda i, j, t: (j, i, 0))
    else:
        out_shape = jax.ShapeDtypeStruct((k1, k2), F32)
        out_spec = pl.BlockSpec((bk1, bk2), lambda i, j, t: (i, j))
    return pl.pallas_call(
        body, name=name, grid=(k1 // bk1, k2 // bk2, n // bn),
        out_shape=out_shape,
        in_specs=[pl.BlockSpec((bn, bk1), lambda i, j, t: (t, i)), pl.BlockSpec((bn, bk2), lambda i, j, t: (t, j))],
        out_specs=out_spec,
        compiler_params=_cp(("parallel", "parallel", "arbitrary")),
    )(a, b)


def _row_block(rows):
    if rows <= 256:
        return rows
    return next(b for b in (256, 192, 128, 64, 32, 16, 8) if rows % b == 0)


def _add_half(g, recv, cidx, name):
    _, rows2, w = g.shape
    rows = rows2 // 2
    br = _row_block(rows)
    nblk = rows // br

    def body(c_ref, g_ref, r_ref, o_ref):
        o_ref[...] = (g_ref[...] + r_ref[...]).astype(BF16)

    return pl.pallas_call(
        body, name=name,
        grid_spec=pltpu.PrefetchScalarGridSpec(
            num_scalar_prefetch=1, grid=(4, nblk),
            in_specs=[pl.BlockSpec((1, br, w), lambda s, i, c: (s, c[0] * nblk + i, 0)),
                      pl.BlockSpec((1, br, w), lambda s, i, c: (s, i, 0))],
            out_specs=pl.BlockSpec((1, br, w), lambda s, i, c: (s, i, 0))),
        out_shape=jax.ShapeDtypeStruct((4, rows, w), BF16),
        compiler_params=_cp(("parallel", "parallel")),
    )(cidx, g, recv)


def _add_chips(r, name):
    _, rows, w = r.shape
    br = _row_block(rows)

    def body(r_ref, o_ref):
        f = lambda k: r_ref[k].astype(F32)
        o_ref[...] = ((f(0) + f(1)) + f(2)) + f(3)

    return pl.pallas_call(
        body, name=name, grid=(rows // br,),
        out_shape=jax.ShapeDtypeStruct((rows, w), F32),
        in_specs=[pl.BlockSpec((4, br, w), lambda i: (0, i, 0))],
        out_specs=pl.BlockSpec((br, w), lambda i: (i, 0)),
        compiler_params=_cp(("parallel",)),
    )(r)


def _sum_devices(a, b):
    def body(a_ref, b_ref, oa_ref, ob_ref):
        acc = a_ref[0:1, :].astype(F32)
        accb = b_ref[0:1, :]
        for k in range(1, 8):
            acc = acc + a_ref[k:k + 1, :].astype(F32)
            accb = accb + b_ref[k:k + 1, :]
        oa_ref[...] = acc
        ob_ref[...] = accb

    return pl.pallas_call(
        body, name="small_grad_sum",
        out_shape=[jax.ShapeDtypeStruct((1, a.shape[1]), F32), jax.ShapeDtypeStruct((1, b.shape[1]), F32)],
        in_specs=[_VM, _VM], out_specs=[_VM, _VM], compiler_params=_cp(),
    )(a, b)


def _adamw_math(wv, gv, mv, vv):
    m_new = ADAM_B1 * mv + (1.0 - ADAM_B1) * gv
    v_new = ADAM_B2 * vv + (1.0 - ADAM_B2) * (gv * gv)
    m_hat = m_new / (1.0 - ADAM_B1 ** ADAM_STEP)
    v_hat = v_new / (1.0 - ADAM_B2 ** ADAM_STEP)
    return -ADAM_LR * (m_hat / (jnp.sqrt(v_hat) + ADAM_EPS) + ADAM_WD * wv), m_new, v_new


def _adamw_small(ws, gs, ms, vs):
    k = len(ws)

    def body(*refs):
        ins, outs = refs[:4 * k], refs[4 * k:]
        for t in range(k):
            d, m_new, v_new = _adamw_math(ins[t][...], ins[k + t][...], ins[2 * k + t][...], ins[3 * k + t][...])
            outs[t][...] = d
            outs[k + t][...] = m_new
            outs[2 * k + t][...] = v_new

    shapes = [jax.ShapeDtypeStruct(w.shape, F32) for w in ws]
    return pl.pallas_call(
        body, name="adamw_small", out_shape=shapes * 3,
        in_specs=[_VM] * (4 * k), out_specs=[_VM] * (3 * k), compiler_params=_cp(),
    )(*ws, *gs, *ms, *vs)


def _adamw(w, g, m, v, name):
    rows, wd = w.shape
    br = _row_block(rows)

    def body(w_ref, g_ref, m_ref, v_ref, d_ref, nm_ref, nv_ref):
        d, m_new, v_new = _adamw_math(w_ref[...], g_ref[...], m_ref[...], v_ref[...])
        d_ref[...] = d
        nm_ref[...] = m_new
        nv_ref[...] = v_new

    spec = pl.BlockSpec((br, wd), lambda i: (i, 0))
    return pl.pallas_call(
        body, name=name, grid=(rows // br,),
        out_shape=[jax.ShapeDtypeStruct((rows, wd), F32)] * 3,
        in_specs=[spec] * 4, out_specs=[spec] * 3,
        compiler_params=_cp(("parallel",)),
    )(w, g, m, v)


def _adamw_halves(w, mine, other, m, v, cidx, name):
    rows, wd = w.shape
    h = rows // 2
    br = _row_block(h)
    nblk = h // br

    def body(c_ref, w_ref, a_ref, b_ref, m_ref, v_ref, g_ref, d_ref, nm_ref, nv_ref):
        gv = jnp.where(pl.program_id(0) == c_ref[0], a_ref[...], b_ref[...])
        d, m_new, v_new = _adamw_math(w_ref[...], gv, m_ref[...], v_ref[...])
        g_ref[...] = gv
        d_ref[...] = d
        nm_ref[...] = m_new
        nv_ref[...] = v_new

    full = pl.BlockSpec((br, wd), lambda hf, i, c: (hf * nblk + i, 0))
    half = pl.BlockSpec((br, wd), lambda hf, i, c: (i, 0))
    return pl.pallas_call(
        body, name=name,
        grid_spec=pltpu.PrefetchScalarGridSpec(
            num_scalar_prefetch=1, grid=(2, nblk),
            in_specs=[full, half, half, full, full], out_specs=[full] * 4),
        out_shape=[jax.ShapeDtypeStruct((rows, wd), F32)] * 4,
        compiler_params=_cp(("parallel", "parallel")),
    )(cidx, w, mine, other, m, v)


def _other_chips(x, y):
    return [(1 - x, y), (x, 1 - y), (1 - x, 1 - y)]


def _other_devices(x, y, c):
    flip = lambda v, d: (1 - v) if d else v
    return [(flip(x, dx), flip(y, dy), flip(c, dc))
            for dx in (0, 1) for dy in (0, 1) for dc in (0, 1) if (dx, dy, dc) != (0, 0, 0)]


def _exchange(name, ins, out_shapes, n_local, n_remote, plan):
    ni, no = len(ins), len(out_shapes)

    def body(*refs):
        in_refs, out_refs = refs[:ni], refs[ni:ni + no]
        send_sems, recv_sems, local_sems = refs[ni + no:]
        x, y, c = lax.axis_index("x"), lax.axis_index("y"), lax.axis_index("c")
        local, remote = plan(in_refs, out_refs, x, y, c)
        assert len(local) == n_local and len(remote) == n_remote

        def push(k, src, dst, dev):
            return pltpu.make_async_remote_copy(src_ref=src, dst_ref=dst, send_sem=send_sems.at[k],
                                                recv_sem=recv_sems.at[k], device_id=dev, device_id_type=MESH)

        own = [pltpu.make_async_copy(s, d, local_sems.at[i]) for i, (s, d) in enumerate(local)]
        for cp in own:
            cp.start()
        sends = [push(k, s, d, dev) for k, (s, d, dev, _) in enumerate(remote)]
        for cp in sends:
            cp.start()
        for k, (s, _, dev, landing) in enumerate(remote):
            push(k, s, landing, dev).wait_recv()
        for cp in sends:
            cp.wait_send()
        for cp in own:
            cp.wait()

    return pl.pallas_call(
        body, name=name, out_shape=out_shapes,
        in_specs=[_ANY] * ni, out_specs=[_ANY] * no,
        scratch_shapes=[pltpu.SemaphoreType.DMA((n_remote,)), pltpu.SemaphoreType.DMA((n_remote,)),
                        pltpu.SemaphoreType.DMA((max(n_local, 1),))],
        compiler_params=pltpu.CompilerParams(has_side_effects=True),
    )(*ins)


def _gather_chips(name, shards, everyone=()):
    ns, ne = len(shards), len(everyone)
    outs = [jax.ShapeDtypeStruct((4,) + a.shape, a.dtype) for a in shards]
    outs += [jax.ShapeDtypeStruct((8,) + a.shape, a.dtype) for a in everyone]

    def plan(i, o, x, y, c):
        mine, me = 2 * x + y, 4 * x + 2 * y + c
        local, remote = [], []
        for t in range(ns):
            local.append((i[t], o[t].at[mine]))
            for px, py in _other_chips(x, y):
                remote.append((i[t], o[t].at[mine], (px, py, c), o[t].at[2 * px + py]))
        for t in range(ns, ns + ne):
            local.append((i[t], o[t].at[me]))
            for px, py, pc in _other_devices(x, y, c):
                remote.append((i[t], o[t].at[me], (px, py, pc), o[t].at[4 * px + 2 * py + pc]))
        return local, remote

    return _exchange(name, list(shards) + list(everyone), outs, ns + ne, 3 * ns + 7 * ne, plan)


_HBM = pl.BlockSpec(memory_space=pltpu.HBM)
_SEM = pl.BlockSpec(memory_space=pltpu.SEMAPHORE)
_EFFECT = pltpu.SideEffectType.DATAFLOW_SIDE_EFFECTING


def _split_start(name, ins, land_shapes, n_remote, plan, after):
    ni, nl = len(ins), len(land_shapes)
    srcs = [pltpu.with_memory_space_constraint(a, pltpu.HBM) for a in ins]
    lands = [pltpu.with_memory_space_constraint(lax.empty(s.shape, s.dtype), pltpu.HBM) for s in land_shapes]

    def body(*refs):
        src, land = refs[:ni], refs[ni:ni + nl]
        first = ni + nl + 1
        send, recv = refs[first:first + n_remote], refs[first + n_remote:first + 2 * n_remote]
        token = refs[first + 2 * n_remote + ni + nl]
        x, y, c = lax.axis_index("x"), lax.axis_index("y"), lax.axis_index("c")
        remote = plan(src, land, x, y, c)
        assert len(remote) == n_remote
        for k, (s, d, dev, _) in enumerate(remote):
            pltpu.make_async_remote_copy(src_ref=s, dst_ref=d, send_sem=send[k], recv_sem=recv[k],
                                         device_id=dev, device_id_type=MESH).start()
        token[...] = jnp.zeros_like(token)

    out = pl.pallas_call(
        body, name=name + "_start",
        out_shape=[pltpu.SemaphoreType.DMA(())] * (2 * n_remote)
                  + [pltpu.HBM(a.shape, a.dtype) for a in ins] + [pltpu.HBM(s.shape, s.dtype) for s in land_shapes]
                  + [jax.ShapeDtypeStruct((8, 128), F32)],
        in_specs=[_HBM] * (ni + nl) + [_ANY], out_specs=[_SEM] * (2 * n_remote) + [_HBM] * (ni + nl) + [_VM],
        input_output_aliases={t: 2 * n_remote + t for t in range(ni + nl)},
        compiler_params=pltpu.CompilerParams(has_side_effects=_EFFECT),
    )(*srcs, *lands, after)
    sems, thru = out[:2 * n_remote], out[2 * n_remote:2 * n_remote + ni + nl]
    return (name, sems, thru[:ni], thru[ni:], n_remote, plan), out[-1]


def _split_wait(handle, after):
    name, sems, srcs, lands, n_remote, plan = handle
    ni, nl = len(srcs), len(lands)

    def body(*refs):
        src, land = refs[:ni], refs[ni:ni + nl]
        send, recv = refs[ni + nl:ni + nl + n_remote], refs[ni + nl + n_remote:ni + nl + 2 * n_remote]
        x, y, c = lax.axis_index("x"), lax.axis_index("y"), lax.axis_index("c")
        for k, (s, _, dev, landing) in enumerate(plan(src, land, x, y, c)):
            cp = pltpu.make_async_remote_copy(src_ref=s, dst_ref=landing, send_sem=send[k], recv_sem=recv[k],
                                              device_id=dev, device_id_type=MESH)
            cp.wait_send()
            cp.wait_recv()

    out = pl.pallas_call(
        body, name=name + "_wait",
        out_shape=[pltpu.HBM(a.shape, a.dtype) for a in srcs] + [pltpu.HBM(a.shape, a.dtype) for a in lands],
        in_specs=[_HBM] * (ni + nl) + [_SEM] * (2 * n_remote) + [_ANY], out_specs=[_HBM] * (ni + nl),
        input_output_aliases={t: t for t in range(ni + nl)},
        compiler_params=pltpu.CompilerParams(has_side_effects=_EFFECT),
    )(*srcs, *lands, *sems, after)
    return out[:ni], out[ni:]


def _plan_to_chips(src, land, x, y, c):
    mine = 2 * x + y
    return [(src[t], land[t].at[mine], (px, py, c), land[t].at[2 * px + py])
            for t in range(len(src)) for px, py in _other_chips(x, y)]


def _plan_swap_halves(src, land, x, y, c):
    out = []
    for t in range(len(src)):
        h = src[t].shape[1] // 2
        out.append((src[t].at[:, pl.ds(pl.multiple_of((1 - c) * h, 8), h), :], land[t], (x, y, 1 - c), land[t]))
    return out


def _plan_scatter_chips(src, land, x, y, c):
    mine = 2 * x + y
    return [(src[t].at[2 * px + py], land[t].at[mine], (px, py, c), land[t].at[2 * px + py])
            for t in range(len(src)) for px, py in _other_chips(x, y)]


def _swap_halves(gs, everyone):
    ns, ne = len(gs), len(everyone)
    outs = [jax.ShapeDtypeStruct((4, g.shape[1] // 2, g.shape[2]), g.dtype) for g in gs]
    outs += [jax.ShapeDtypeStruct((8,) + a.shape, a.dtype) for a in everyone]

    def plan(i, o, x, y, c):
        me = 4 * x + 2 * y + c
        local, remote = [], []
        for t in range(ns):
            h = gs[t].shape[1] // 2
            theirs = i[t].at[:, pl.ds(pl.multiple_of((1 - c) * h, 8), h), :]
            remote.append((theirs, o[t], (x, y, 1 - c), o[t]))
        for t in range(ns, ns + ne):
            local.append((i[t], o[t].at[me]))
            for px, py, pc in _other_devices(x, y, c):
                remote.append((i[t], o[t].at[me], (px, py, pc), o[t].at[4 * px + 2 * py + pc]))
        return local, remote

    return _exchange("grad_swap_sibling", list(gs) + list(everyone), outs, ne, ns + 7 * ne, plan)


def _scatter_chips(parts):
    ns = len(parts)
    outs = [jax.ShapeDtypeStruct(a.shape, a.dtype) for a in parts]

    def plan(i, o, x, y, c):
        mine = 2 * x + y
        local, remote = [], []
        for t in range(ns):
            local.append((i[t].at[mine], o[t].at[mine]))
            for px, py in _other_chips(x, y):
                remote.append((i[t].at[2 * px + py], o[t].at[mine], (px, py, c), o[t].at[2 * px + py]))
        return local, remote

    return _exchange("grad_scatter_chips", list(parts), outs, ns, 3 * ns, plan)


def _join_halves(halves):
    ns = len(halves)
    outs = [jax.ShapeDtypeStruct(a.shape, a.dtype) for a in halves]

    def plan(i, o, x, y, c):
        return [], [(i[t], o[t], (x, y, 1 - c), o[t]) for t in range(ns)]

    return _exchange("grad_join_sibling", list(halves), outs, 0, ns, plan)


def _pad_heads_cols(w, per, used):
    k = w.shape[0]
    w = w.reshape(k, NH, per)[:, :, :used]
    return jnp.pad(w, ((0, 0), (0, 0), (0, HP - used))).reshape(k, NH * HP)


def _unpad_heads_cols(w, used):
    k = w.shape[0]
    return w.reshape(k, NH, HP)[:, :, :used]


def _prep_weights(wf):
    bf = lambda a: a.astype(BF16)
    out = {}
    out["w_in"] = jnp.pad(bf(wf["w_in"]), ((0, 0), (0, IN_PAD - IN_COLS)))
    out["w_glu"] = bf(wf["w_glu"])
    out["w_uq"] = _pad_heads_cols(bf(wf["w_uq"]), QK_NOPE + QK_ROPE, QK_NOPE + QK_ROPE)
    wkv = bf(wf["w_ukv"]).reshape(KV_LORA, NH, QK_NOPE + V_HEAD)
    wk = jnp.pad(wkv[:, :, :QK_NOPE], ((0, 0), (0, 0), (0, HP - QK_NOPE))).reshape(KV_LORA, NH * HP)
    wv = jnp.pad(wkv[:, :, QK_NOPE:], ((0, 0), (0, 0), (0, HP - V_HEAD))).reshape(KV_LORA, NH * HP)
    out["w_ukv"] = jnp.concatenate([wk, wv], axis=1)
    return out


def _prep_late_weights(wf):
    bf = lambda a: a.astype(BF16)
    out = {}
    wo = bf(wf["w_out"])
    wo_a = jnp.pad(wo[D_SSM:].reshape(NH, V_HEAD, D), ((0, 0), (0, HP - V_HEAD), (0, 0))).reshape(NH * HP, D)
    out["w_out"] = jnp.concatenate([wo[:D_SSM], wo_a], axis=0)
    out["w_ff1"] = bf(wf["w_ff1"])
    out["w_ff2"] = bf(wf["w_ff2"])
    return out


def _rope_tables(positions):
    inv_freq = ROPE_BASE ** (-jnp.arange(0, QK_ROPE, 2, dtype=F32) / QK_ROPE)
    ang = positions.astype(F32)[:, None] * inv_freq
    cos, sin = jnp.cos(ang), jnp.sin(ang)
    n = positions.shape[0]
    one = jnp.ones((n, QK_NOPE), F32)
    z16 = jnp.zeros((n, 16), F32)
    z32 = jnp.zeros((n, 32), F32)
    z64 = jnp.zeros((n, QK_NOPE), F32)
    rc = jnp.concatenate([one, cos, cos, z32], axis=1)
    rs1 = jnp.concatenate([z64, -sin, z16, z32], axis=1)
    rs2 = jnp.concatenate([z64, z16, sin, z32], axis=1)
    return rc, rs1, rs2


def _permute_rows(a, S):
    n, w = a.shape
    return a.reshape(n // S, 8, S // 8, w).transpose(0, 2, 1, 3).reshape(n, w)


def _unpermute_rows(a, S):
    n, w = a.shape
    return a.reshape(n // S, S // 8, 8, w).transpose(0, 2, 1, 3).reshape(n, w)


def _block_diag_in(bb):
    eye = jnp.eye(8, dtype=bb.dtype)
    blocks = jnp.einsum("qgph,gk->qghkp", bb.reshape(4, 8, P, H), eye).reshape(4, QB, QS)
    return blocks.transpose(1, 0, 2).reshape(QB, NST)


def _block_diag_out(cc):
    eye = jnp.eye(8, dtype=cc.dtype)
    return jnp.einsum("qghp,gk->qgpkh", cc.reshape(4, 8, H, P), eye).reshape(NST, QB)


def _slots(full):
    r, cdim = full.shape
    return full.reshape(r, 4, cdim // 4).transpose(1, 0, 2)


def _unslots(g):
    s, r, cs = g.shape
    return g.transpose(1, 0, 2).reshape(r, s * cs)


def _local_step(x, positions, target, modp, wf, late_weights=None, reducer=None):
    nb, S, _ = x.shape
    n = nb * S
    tm = min(256, S)
    tr = min(512, S)
    tt = min(512, S)
    tq = min(512, S // 2)
    kw = _prep_weights(wf)
    row = lambda a: a.reshape(1, -1).astype(F32)

    xf = x.reshape(n, D)
    tf = target.reshape(n, D)
    g1, g2, gf = row(wf["norm1_g"]), row(wf["norm2_g"]), row(wf["final_norm_g"])
    h1, proj = _f1_fwd(xf, modp, g1, kw["w_in"], S, tr)

    col = lambda a: a.reshape(NST, 1)
    lam_re, lam_im = col(wf["ssm_lambda_re"]), col(wf["ssm_lambda_im"])
    logdt = jnp.repeat(wf["ssm_log_dt"].reshape(G, 1), P, axis=1).reshape(NST, 1)
    b_re, b_im = wf["ssm_b_re"].reshape(NST, H), wf["ssm_b_im"].reshape(NST, H)
    lbr, lbi, bbr, bbi = _ssm_param_fwd(lam_re, lam_im, logdt, b_re, b_im)
    lre8 = jnp.broadcast_to(lbr.reshape(1, NST), (8, NST))
    lim8 = jnp.broadcast_to(lbi.reshape(1, NST), (8, NST))
    bm = jnp.concatenate([_block_diag_in(bbr.reshape(G, P, H)), _block_diag_in(bbi.reshape(G, P, H))],
                         axis=1).astype(BF16)
    cm = jnp.concatenate([_block_diag_out(wf["ssm_c_re"]), -_block_diag_out(wf["ssm_c_im"])], axis=0).astype(BF16)
    dvec = row(wf["ssm_d"])
    u_p = _permute_rows(proj[:, :D_SSM], S)
    fcr, fci = _ssm_local(u_p, bm, lre8, lim8, S, tt)
    st, ypre, z, gact, yssm_p = _ssm_fwd(u_p, fcr, fci, bm, cm, dvec, kw["w_glu"], lre8, lim8, S, tt)
    yssm = _unpermute_rows(yssm_p, S)

    rc, rs1, rs2 = _rope_tables(positions.reshape(n))
    gq, gkv = row(wf["q_norm_g"]), row(wf["kv_norm_g"])
    q, k, v, qn, kvn = _mla_fwd(proj, rc, rs1, rs2, gq, gkv, kw["w_uq"], kw["w_ukv"], tr)
    oattn, lrow = _attn_fwd(q, k, v, S, tq)

    gs = row(wf["ssm_out_g"])
    ga = jnp.pad(wf["attn_out_g"].reshape(NH, V_HEAD), ((0, 0), (0, HP - V_HEAD))).reshape(1, NH * HP)
    kw.update(_prep_late_weights(late_weights(oattn) if late_weights is not None else wf))
    yn, o, x1, h2 = _p1_fwd(yssm, oattn, xf, modp, gs, ga, kw["w_out"], g2, S, tr)
    dx1, r, da, dff, accs2, accg2 = _p2(x1, h2, tf, modp, g2, gf, kw["w_ff1"], kw["w_ff2"], S, tm)
    loss = accg2[2:3]
    g_ff1 = _wgrad(h2, da, "wgrad_ff1", col_slots=4)
    g_ff2 = _wgrad(r, dff, "wgrad_ff2").reshape(4, D_FF // 4, D)
    do, dyssm, dob, drow, accs3, accg3 = _p3_bwd(dx1, o, yssm, oattn, modp, gs, ga, kw["w_out"], S, tr)
    gwo = _wgrad(yn, do, "wgrad_out")
    g_out = jnp.concatenate([gwo[:D_SSM].reshape(2, D_SSM // 2, D),
                             gwo[D_SSM:].reshape(2, NH // 2 * HP, D).reshape(2, NH // 2, HP, D)[:, :, :V_HEAD]
                             .reshape(2, D_ATTN // 2, D)], axis=0)
    gq_b = gq
    if reducer is not None:
        drow = drow + reducer.start([g_ff1, g_ff2, g_out])[0, 0]

    dq, dk, dv = _attn_bwd(q, k, v, dob, lrow, drow, S, tq)
    if reducer is not None:
        gq_b = gq + reducer.middle(dq)[0, 0]
    dmla, dqb, dkvb, accm = _mla_bwd(dq, dk, dv, proj, rc, rs1, rs2, gq_b, gkv, kw["w_uq"], kw["w_ukv"], tr)

    dys_p = _permute_rows(dyssm, S)
    dy, dz, air, aii = _ssm_bwd_a(dys_p, z, ypre, kw["w_glu"], cm, lre8, lim8, S, tt)
    du_p, dcm, dbm, dd, dlr, dli = _ssm_bwd_b(dy, u_p, st, fcr, fci, air, aii, bm, cm, dvec, lre8, lim8, S, tt)
    du = _unpermute_rows(du_p, S)
    dcm = dcm.reshape(2, 4, 8, P, 8, H)
    dc_re = jnp.einsum("qgpgh->qghp", dcm[0]).reshape(G, H, P)
    dc_im = -jnp.einsum("qgpgh->qghp", dcm[1]).reshape(G, H, P)
    dbm = dbm.reshape(8, H, 2, 4, 8, P)
    dbb_re = jnp.einsum("ghqgp->qgph", dbm[:, :, 0]).reshape(NST, H)
    dbb_im = jnp.einsum("ghqgp->qgph", dbm[:, :, 1]).reshape(NST, H)
    gb_re, gb_im, glr, gli, gdt = _ssm_param_bwd(lam_re, lam_im, logdt, b_re, b_im, dlr.reshape(NST, 1),
                                                 dli.reshape(NST, 1), dbb_re, dbb_im)
    glogdt = _rowsum(gdt.reshape(G, P))

    dx, dproj, accs1, accg1 = _f1_bwd(du, dmla, dx1, xf, modp, g1, kw["w_in"], S, tr)

    big = {}
    big["w_in"] = _slots(_wgrad(h1, dproj, "wgrad_in")[:, :IN_COLS])
    big["w_glu"] = _wgrad(gact, dz, "wgrad_glu", col_slots=4)
    big["w_uq"] = _slots(_unpad_heads_cols(_wgrad(qn, dqb, "wgrad_uq"), QK_NOPE + QK_ROPE).reshape(Q_LORA, -1))
    gkvw = _wgrad(kvn, dkvb, "wgrad_ukv")
    big["w_ukv"] = _slots(jnp.concatenate([_unpad_heads_cols(gkvw[:, :NH * HP], QK_NOPE),
                                           _unpad_heads_cols(gkvw[:, NH * HP:], V_HEAD)], axis=2).reshape(KV_LORA, -1))
    big["w_out"] = g_out
    big["w_ff1"] = g_ff1
    big["w_ff2"] = g_ff2

    small = {}
    small["norm1_g"] = accg1[0:1]
    small["norm2_g"] = accg2[0:1]
    small["final_norm_g"] = accg2[1:2]
    small["ssm_out_g"] = accg3[0:1, :D_SSM]
    small["attn_out_g"] = accg3[1].reshape(NH, HP)[:, :V_HEAD].reshape(1, D_ATTN)
    small["q_norm_g"] = accm[0:1, :Q_LORA]
    small["kv_norm_g"] = accm[1:2, :KV_LORA]
    small["ssm_lambda_re"] = glr.reshape(G, P)
    small["ssm_lambda_im"] = gli.reshape(G, P)
    small["ssm_b_re"] = gb_re
    small["ssm_b_im"] = gb_im
    small["ssm_c_re"] = dc_re.reshape(G * H, P)
    small["ssm_c_im"] = dc_im.reshape(G * H, P)
    small["ssm_d"] = dd.reshape(G, H)
    small["ssm_log_dt"] = glogdt.reshape(1, G)
    return loss, dx.reshape(nb, S, D), big, small, accs1 + accs2 + accs3


def _view2d(a):
    return a.reshape(-1, a.shape[-1]) if a.ndim > 1 else a.reshape(1, -1)


def kernel(x, c, positions, ada_w, ada_b, norm1_g, w_in, ssm_lambda_re, ssm_lambda_im, ssm_b_re, ssm_b_im, ssm_c_re, ssm_c_im, ssm_d, ssm_log_dt, w_glu, q_norm_g, w_uq, kv_norm_g, w_ukv, ssm_out_g, attn_out_g, w_out, norm2_g, w_ff1, w_ff2, final_ada_w, final_ada_b, final_norm_g, loss_target, m_ada_w, m_ada_b, m_norm1_g, m_w_in, m_ssm_lambda_re, m_ssm_lambda_im, m_ssm_b_re, m_ssm_b_im, m_ssm_c_re, m_ssm_c_im, m_ssm_d, m_ssm_log_dt, m_w_glu, m_q_norm_g, m_w_uq, m_kv_norm_g, m_w_ukv, m_ssm_out_g, m_attn_out_g, m_w_out, m_norm2_g, m_w_ff1, m_w_ff2, m_final_ada_w, m_final_ada_b, m_final_norm_g, v_ada_w, v_ada_b, v_norm1_g, v_w_in, v_ssm_lambda_re, v_ssm_lambda_im, v_ssm_b_re, v_ssm_b_im, v_ssm_c_re, v_ssm_c_im, v_ssm_d, v_ssm_log_dt, v_w_glu, v_q_norm_g, v_w_uq, v_kv_norm_g, v_w_ukv, v_ssm_out_g, v_attn_out_g, v_w_out, v_norm2_g, v_w_ff1, v_w_ff2, v_final_ada_w, v_final_ada_b, v_final_norm_g):
    args = dict(locals())
    names = list(inspect.signature(kernel).parameters)
    wnames = names[3:names.index("loss_target")]
    small_names = [nm for nm in wnames if nm not in GATHERED and nm not in TP]
    reduced_names = [nm for nm in small_names if nm not in ("ada_b", "final_ada_b")]
    w = {nm: args[nm] for nm in wnames}
    m = {nm: args["m_" + nm] for nm in wnames}
    v = {nm: args["v_" + nm] for nm in wnames}
    nb = x.shape[0]
    xi, yi, ci = lax.axis_index("x"), lax.axis_index("y"), lax.axis_index("c")
    chip, me = 2 * xi + yi, 4 * xi + 2 * yi + ci

    unslot = lambda nm, g: g.reshape(-1, g.shape[-1]) if nm in ROW_SHARDED else _unslots(g)
    early = [nm for nm in GATHERED if nm not in LATE]
    got = _gather_chips("gather_weights", [_view2d(w[nm]).astype(BF16) for nm in early], [c])
    wf = {nm: unslot(nm, g) for nm, g in zip(early, got)}
    for nm in small_names:
        wf[nm] = w[nm][0] if w[nm].ndim > 1 else w[nm]
    c_all = got[len(early)].reshape(8 * nb, D)

    na, nf = ada_w.shape[-1], final_ada_w.shape[-1]
    ada_b_s = lax.dynamic_slice(ada_b, (0, chip * na), (1, na))
    fada_b_s = lax.dynamic_slice(final_ada_b.reshape(1, -1), (0, chip * nf), (1, nf))
    cond_all, modcols = _mod_fwd(c_all, ada_w[0], ada_b_s, final_ada_w, fada_b_s)
    (mod_g,) = _gather_chips("gather_mod", [modcols])
    mine = lax.dynamic_slice(mod_g, (0, me * nb, 0), (4, nb, na + nf))
    modp = jnp.concatenate([mine[:, :, :na].transpose(1, 0, 2).reshape(nb, 6, D),
                            mine[:, :, na:].transpose(1, 0, 2).reshape(nb, 2, D)], axis=1)

    own_late = [_view2d(w[nm]).astype(BF16) for nm in LATE]
    late_gather, token = _split_start("gather_late", own_late,
                                      [jax.ShapeDtypeStruct((4,) + a.shape, a.dtype) for a in own_late],
                                      3 * len(LATE), _plan_to_chips, modp)
    modp = modp + token[0, 0]

    def late_weights(after):
        sent, landed = _split_wait(late_gather, after)
        return {nm: unslot(nm, lax.dynamic_update_slice(g, own[None], (chip, 0, 0)))
                for nm, g, own in zip(LATE, landed, sent)}

    cidx = ci.astype(jnp.int32).reshape(1)
    ahead = ["w_ff1", "w_ff2", "w_out"]

    class Reducer:
        def start(self, gs):
            lands = [jax.ShapeDtypeStruct((4, g.shape[1] // 2, g.shape[2]), g.dtype) for g in gs]
            self.swap, tok = _split_start("grad_swap_ff", gs, lands, len(gs), _plan_swap_halves, modp)
            return tok

        def middle(self, after):
            gs, got = _split_wait(self.swap, after)
            sums = [_add_half(g, r, cidx, "grad_add_sibling_" + nm) for nm, g, r in zip(ahead, gs, got)]
            lands = [jax.ShapeDtypeStruct(s.shape, s.dtype) for s in sums]
            self.scatter, tok = _split_start("grad_scatter_ff", sums, lands, 3 * len(sums), _plan_scatter_chips, modp)
            return tok

        def finish(self, after):
            out = []
            for nm, s, l in zip(ahead, *_split_wait(self.scatter, after)):
                own = lax.dynamic_slice(s, (chip, 0, 0), (1,) + s.shape[1:])
                out.append(_add_chips(lax.dynamic_update_slice(l, own, (chip, 0, 0)), "grad_add_chips_" + nm))
            return out

    reducer = Reducer()
    loss_row, grad_x, big, small, dmodp = _local_step(x, positions, loss_target, modp, wf, late_weights, reducer)

    rest = [nm for nm in GATHERED if nm not in ahead]
    sizes = [small[nm].size for nm in reduced_names]
    pad = -sum(sizes) % 128
    packed = jnp.concatenate([small[nm].reshape(1, -1) for nm in reduced_names] + [jnp.zeros((1, pad), F32)],
                             axis=1).astype(BF16)
    swapped = _swap_halves([big[nm] for nm in rest], [dmodp.reshape(nb, 8 * D), packed, loss_row])
    chip_sums = [_add_half(big[nm], r, cidx, "grad_add_sibling_" + nm) for nm, r in zip(rest, swapped)]
    half_of = {nm: _add_chips(r, "grad_add_chips_" + nm) for nm, r in zip(rest, _scatter_chips(chip_sums))}
    half_of.update(zip(ahead, reducer.finish(grad_x)))
    halves = [half_of[nm] for nm in GATHERED]
    others = _join_halves(halves)
    grads = {}
    dmod_all = swapped[len(rest)].reshape(8 * nb, 8 * D)
    small_sum, loss_sum = _sum_devices(swapped[len(rest) + 1].reshape(8, -1), swapped[len(rest) + 2].reshape(8, -1))
    loss = jnp.sum(loss_sum)
    off = 0
    for nm, sz in zip(reduced_names, sizes):
        grads[nm] = small_sum[:, off:off + sz].reshape(small[nm].shape)
        off += sz

    dsl = jnp.concatenate([lax.dynamic_slice(dmod_all, (0, chip * na), (8 * nb, na)),
                           lax.dynamic_slice(dmod_all, (0, 6 * D + chip * nf), (8 * nb, nf))], axis=1)
    gw, gb = _mod_bwd(cond_all.T, dsl, dmod_all)
    grads["ada_w"], grads["final_ada_w"] = gw[:, :na], gw[:, na:]
    grads["ada_b"], grads["final_ada_b"] = gb[:, :6 * D], gb[:, 6 * D:]

    delta, new_m, new_v = {}, {}, {}
    for nm, mine_h, other_h in zip(GATHERED, halves, others):
        grads[nm], delta[nm], new_m[nm], new_v[nm] = _adamw_halves(
            _view2d(w[nm]), mine_h, other_h, _view2d(m[nm]), _view2d(v[nm]), cidx, "adamw_" + nm)
    for nm in TP:
        delta[nm], new_m[nm], new_v[nm] = _adamw(_view2d(w[nm]), grads[nm], _view2d(m[nm]), _view2d(v[nm]),
                                                  "adamw_" + nm)
    upd = _adamw_small([_view2d(w[nm]) for nm in small_names], [grads[nm] for nm in small_names],
                       [_view2d(m[nm]) for nm in small_names], [_view2d(v[nm]) for nm in small_names])
    k = len(small_names)
    for t, nm in enumerate(small_names):
        delta[nm], new_m[nm], new_v[nm] = upd[t], upd[k + t], upd[2 * k + t]

    outs = [grads, delta, new_m, new_v]
    return (loss, grad_x, *[d[nm].reshape(w[nm].shape) for d in outs for nm in wnames])
```

```python
import inspect
import math

import jax
import jax.numpy as jnp
from jax import lax
from jax.experimental import pallas as pl
from jax.experimental.pallas import tpu as pltpu

F32 = jnp.float32
BF16 = jnp.bfloat16

D = 1024
D_SSM = 512
G = 32
H = 16
P = 64
NST = G * P
D_ATTN = 512
NH = 8
QK_NOPE = 64
QK_ROPE = 32
V_HEAD = 64
HP = 128
Q_LORA = 384
KV_LORA = 256
IN_COLS = D_SSM + Q_LORA + KV_LORA + QK_ROPE
IN_PAD = 1280
D_FF = 4096
ROPE_BASE = 10000.0
EPS = 1e-6
ADAM_LR = 0.001
ADAM_B1 = 0.9
ADAM_B2 = 0.999
ADAM_EPS = 1e-08
ADAM_WD = 0.01
ADAM_STEP = 10
NEG = -1e30
VMEM_LIMIT = 60 << 20

MESH = pl.DeviceIdType.MESH
_VM = pl.BlockSpec(memory_space=pltpu.VMEM)
_ANY = pl.BlockSpec(memory_space=pl.ANY)

GATHERED = ["w_in", "w_glu", "w_uq", "w_ukv", "w_out", "w_ff1", "w_ff2"]
TP = ["ada_w", "final_ada_w"]
ROW_SHARDED = ("w_out", "w_ff2")
LATE = ["w_out", "w_ff1", "w_ff2"]


def _cp(sem=None, vmem=VMEM_LIMIT):
    kw = dict(vmem_limit_bytes=vmem)
    if sem is not None:
        kw["dimension_semantics"] = sem
    return pltpu.CompilerParams(**kw)


def _dot(a, b):
    return jnp.dot(a, b, preferred_element_type=F32)


def _dot_nt(a, b):
    return lax.dot_general(a, b, (((1,), (1,)), ((), ())), preferred_element_type=F32)


def _dot_tn(a, b):
    return lax.dot_general(a, b, (((0,), (0,)), ((), ())), preferred_element_type=F32)


def _rms(x, n):
    r = lax.rsqrt(jnp.sum(x * x, axis=-1, keepdims=True) * (1.0 / n) + EPS)
    return x * r, r


def _rms_bwd(dyg, xhat, r, n):
    return r * (dyg - xhat * (jnp.sum(dyg * xhat, axis=-1, keepdims=True) * (1.0 / n)))


def _sigmoid(x):
    return 1.0 / (1.0 + jnp.exp(-x))


_GK = math.sqrt(2.0 / math.pi)
_GC = 0.044715


def _gelu(y):
    t = jnp.tanh(_GK * (y + _GC * y * y * y))
    return 0.5 * y * (1.0 + t)


def _gelu_grad(y):
    t = jnp.tanh(_GK * (y + _GC * y * y * y))
    return 0.5 * (1.0 + t) + 0.5 * y * (1.0 - t * t) * _GK * (1.0 + 3.0 * _GC * y * y)


def _colsum(x):
    return jnp.sum(x, axis=0, keepdims=True)


def _roll(x, s):
    return pltpu.roll(x, s % x.shape[-1], x.ndim - 1)


def _mod_fwd(c_all, ada_w_s, ada_b_s, fada_w_s, fada_b_s):
    nseq = c_all.shape[0]
    na, nf = ada_w_s.shape[1], fada_w_s.shape[1]

    def body(c_ref, w_ref, b_ref, fw_ref, fb_ref, cond_ref, mod_ref):
        cv = c_ref[...]
        cond = cv * _sigmoid(cv)
        cond_ref[...] = cond
        cb = cond.astype(BF16)
        mod_ref[:, 0:na] = _dot(cb, w_ref[...].astype(BF16)) + b_ref[...]
        mod_ref[:, na:na + nf] = _dot(cb, fw_ref[...].astype(BF16)) + fb_ref[...]

    return pl.pallas_call(
        body, name="mod_fwd",
        out_shape=[jax.ShapeDtypeStruct((nseq, D), F32), jax.ShapeDtypeStruct((nseq, na + nf), F32)],
        in_specs=[_VM] * 5, out_specs=[_VM] * 2, compiler_params=_cp(),
    )(c_all, ada_w_s, ada_b_s, fada_w_s, fada_b_s)


def _mod_bwd(cond_t, dsl, dall):
    nseq, n = dsl.shape
    bc = 512

    def body(ct_ref, dm_ref, da_ref, gw_ref, gb_ref):
        ct = ct_ref[...]
        dm = dm_ref[...]
        acc = ct[:, 0:1] * dm[0:1, :]
        for b in range(1, nseq):
            acc = acc + ct[:, b:b + 1] * dm[b:b + 1, :]
        gw_ref[...] = acc

        @pl.when(pl.program_id(0) == 0)
        def _():
            gb_ref[...] = _colsum(da_ref[...])

    return pl.pallas_call(
        body, name="mod_bwd", grid=(n // bc,),
        out_shape=[jax.ShapeDtypeStruct((D, n), F32), jax.ShapeDtypeStruct((1, dall.shape[1]), F32)],
        in_specs=[_VM, pl.BlockSpec((nseq, bc), lambda i: (0, i)), _VM],
        out_specs=[pl.BlockSpec((D, bc), lambda i: (0, i)), pl.BlockSpec((1, dall.shape[1]), lambda i: (0, 0))],
        compiler_params=_cp(("arbitrary",)),
    )(cond_t, dsl, dall)


def _f1_fwd(x, modp, g1, w_in, S, tm):
    n = x.shape[0]
    tps = S // tm

    def body(x_ref, mod_ref, g_ref, w_ref, h_ref, proj_ref):
        xhat, _ = _rms(x_ref[...], D)
        h = (xhat * g_ref[...]) * (1.0 + mod_ref[0, 1:2, :]) + mod_ref[0, 0:1, :]
        hb = h.astype(BF16)
        h_ref[...] = hb
        proj_ref[...] = _dot(hb, w_ref[...])

    return pl.pallas_call(
        body, name="f1_fwd", grid=(n // tm,),
        out_shape=[jax.ShapeDtypeStruct((n, D), BF16), jax.ShapeDtypeStruct((n, IN_PAD), F32)],
        in_specs=[pl.BlockSpec((tm, D), lambda i: (i, 0)),
                  pl.BlockSpec((1, 8, D), lambda i: (i // tps, 0, 0)), _VM, _VM],
        out_specs=[pl.BlockSpec((tm, D), lambda i: (i, 0)), pl.BlockSpec((tm, IN_PAD), lambda i: (i, 0))],
        compiler_params=_cp(("parallel",)),
    )(x, modp, g1, w_in)


def _f1_bwd(du, dmla, dx1, x, modp, g1, w_in, S, tm):
    n = x.shape[0]
    tps = S // tm
    nb = n // S

    def body(du_ref, dm_ref, dx1_ref, x_ref, mod_ref, g_ref, w_ref, dx_ref, dproj_ref, accs_ref, accg_ref):
        i = pl.program_id(0)
        dproj = jnp.concatenate([du_ref[...], dm_ref[...]], axis=1).astype(BF16)
        dproj_ref[...] = dproj
        dh = _dot_nt(dproj, w_ref[...])
        xhat, r = _rms(x_ref[...], D)
        g = g_ref[...]
        dn = dh * (1.0 + mod_ref[0, 1:2, :])
        dx_ref[...] = dx1_ref[...] + _rms_bwd(dn * g, xhat, r, D)

        @pl.when(i % tps == 0)
        def _():
            accs_ref[...] = jnp.zeros_like(accs_ref)

        @pl.when(i == 0)
        def _():
            accg_ref[...] = jnp.zeros_like(accg_ref)

        accs_ref[0, 0:1, :] += _colsum(dh)
        accs_ref[0, 1:2, :] += _colsum(dh * (xhat * g))
        accg_ref[0:1, :] += _colsum(dn * xhat)

    return pl.pallas_call(
        body, name="f1_bwd", grid=(n // tm,),
        out_shape=[jax.ShapeDtypeStruct((n, D), F32), jax.ShapeDtypeStruct((n, IN_PAD), BF16),
                   jax.ShapeDtypeStruct((nb, 8, D), F32), jax.ShapeDtypeStruct((8, D), F32)],
        in_specs=[pl.BlockSpec((tm, D_SSM), lambda i: (i, 0)), pl.BlockSpec((tm, IN_PAD - D_SSM), lambda i: (i, 0)),
                  pl.BlockSpec((tm, D), lambda i: (i, 0)), pl.BlockSpec((tm, D), lambda i: (i, 0)),
                  pl.BlockSpec((1, 8, D), lambda i: (i // tps, 0, 0)), _VM, _VM],
        out_specs=[pl.BlockSpec((tm, D), lambda i: (i, 0)), pl.BlockSpec((tm, IN_PAD), lambda i: (i, 0)),
                   pl.BlockSpec((1, 8, D), lambda i: (i // tps, 0, 0)), pl.BlockSpec((8, D), lambda i: (0, 0))],
        compiler_params=_cp(("arbitrary",)),
    )(du, dmla, dx1, x, modp, g1, w_in)


def _ssm_param_fwd(lam_re, lam_im, logdt, b_re, b_im):
    def body(lr_ref, li_ref, ld_ref, br_ref, bi_ref, lbr_ref, lbi_ref, bbr_ref, bbi_ref):
        lr, li = lr_ref[...], li_ref[...]
        dt = jnp.exp(ld_ref[...])
        er = jnp.exp(lr * dt)
        lbr = er * jnp.cos(li * dt)
        lbi = er * jnp.sin(li * dt)
        den = 1.0 / (lr * lr + li * li)
        cr = ((lbr - 1.0) * lr + lbi * li) * den
        ci = (lbi * lr - (lbr - 1.0) * li) * den
        lbr_ref[...] = lbr
        lbi_ref[...] = lbi
        bbr_ref[...] = cr * br_ref[...] - ci * bi_ref[...]
        bbi_ref[...] = cr * bi_ref[...] + ci * br_ref[...]

    return pl.pallas_call(
        body, name="ssm_param_fwd",
        out_shape=[jax.ShapeDtypeStruct((NST, 1), F32)] * 2 + [jax.ShapeDtypeStruct((NST, H), F32)] * 2,
        in_specs=[_VM] * 5, out_specs=[_VM] * 4, compiler_params=_cp(),
    )(lam_re, lam_im, logdt, b_re, b_im)


def _ssm_param_bwd(lam_re, lam_im, logdt, b_re, b_im, dlb_re, dlb_im, dbb_re, dbb_im):
    def body(lr_ref, li_ref, ld_ref, br_ref, bi_ref, dlr_ref, dli_ref, dbr_ref, dbi_ref,
             gbr_ref, gbi_ref, glr_ref, gli_ref, gdt_ref):
        lr, li = lr_ref[...], li_ref[...]
        dt = jnp.exp(ld_ref[...])
        er = jnp.exp(lr * dt)
        lbr = er * jnp.cos(li * dt)
        lbi = er * jnp.sin(li * dt)
        den = 1.0 / (lr * lr + li * li)
        nr, ni = lbr - 1.0, lbi
        cr = (nr * lr + ni * li) * den
        ci = (ni * lr - nr * li) * den
        br, bi = br_ref[...], bi_ref[...]
        dbr, dbi = dbr_ref[...], dbi_ref[...]
        gbr_ref[...] = cr * dbr + ci * dbi
        gbi_ref[...] = cr * dbi - ci * dbr
        gcr = jnp.sum(dbr * br + dbi * bi, axis=1, keepdims=True)
        gci = jnp.sum(dbi * br - dbr * bi, axis=1, keepdims=True)
        ilr, ili = lr * den, -li * den
        glbr = dlr_ref[...] + (gcr * ilr + gci * ili)
        glbi = dli_ref[...] + (gci * ilr - gcr * ili)
        qr = -(cr * ilr - ci * ili)
        qi = -(cr * ili + ci * ilr)
        glr = gcr * qr + gci * qi
        gli = gci * qr - gcr * qi
        glr = glr + dt * (glbr * lbr + glbi * lbi)
        gli = gli + dt * (glbi * lbr - glbr * lbi)
        wr = lr * lbr - li * lbi
        wi = lr * lbi + li * lbr
        glr_ref[...] = glr
        gli_ref[...] = gli
        gdt_ref[...] = (glbr * wr + glbi * wi) * dt

    return pl.pallas_call(
        body, name="ssm_param_bwd",
        out_shape=[jax.ShapeDtypeStruct((NST, H), F32)] * 2 + [jax.ShapeDtypeStruct((NST, 1), F32)] * 3,
        in_specs=[_VM] * 9, out_specs=[_VM] * 5, compiler_params=_cp(),
    )(lam_re, lam_im, logdt, b_re, b_im, dlb_re, dlb_im, dbb_re, dbb_im)


def _rowsum(a):
    def body(a_ref, o_ref):
        o_ref[...] = jnp.sum(a_ref[...], axis=1, keepdims=True)

    return pl.pallas_call(
        body, name="rowsum", out_shape=jax.ShapeDtypeStruct((a.shape[0], 1), F32),
        in_specs=[_VM], out_specs=_VM, compiler_params=_cp(),
    )(a)


QB = D_SSM // 4
QS = 4 * QB


def _bd_lo(part, q):
    return part * NST + q * QS


def _bd_expand(ub, bm_ref, out_ref):
    for part in range(2):
        for q in range(4):
            lo = _bd_lo(part, q)
            out_ref[:, lo:lo + QS] = _dot(ub[:, q * QB:(q + 1) * QB], bm_ref[:, lo:lo + QS])


def _bd_expand_t(db, cm_ref, out_ref):
    for part in range(2):
        for q in range(4):
            lo = _bd_lo(part, q)
            out_ref[:, lo:lo + QS] = _dot_nt(db[:, q * QB:(q + 1) * QB], cm_ref[lo:lo + QS, :])


def _bd_project(sb, cm_ref):
    return jnp.concatenate(
        [_dot(sb[:, _bd_lo(0, q):_bd_lo(0, q) + QS], cm_ref[_bd_lo(0, q):_bd_lo(0, q) + QS, :])
         + _dot(sb[:, _bd_lo(1, q):_bd_lo(1, q) + QS], cm_ref[_bd_lo(1, q):_bd_lo(1, q) + QS, :])
         for q in range(4)], axis=1)


def _bd_project_t(ab, bm_ref):
    return jnp.concatenate(
        [_dot_nt(ab[:, _bd_lo(0, q):_bd_lo(0, q) + QS], bm_ref[:, _bd_lo(0, q):_bd_lo(0, q) + QS])
         + _dot_nt(ab[:, _bd_lo(1, q):_bd_lo(1, q) + QS], bm_ref[:, _bd_lo(1, q):_bd_lo(1, q) + QS])
         for q in range(4)], axis=1)


def _pow2k(pr, pi, nsq):
    for _ in range(nsq):
        pr, pi = pr * pr - pi * pi, 2.0 * pr * pi
    return pr, pi


def _ssm_local(u_p, bm, lre8, lim8, S, tt):
    n = u_p.shape[0]
    nb, nt = n // S, S // tt
    nsq = int(round(math.log2(S // 8)))
    assert 2 ** nsq == S // 8

    def body(u_ref, bm_ref, lre_ref, lim_ref, cre_ref, cim_ref, sre, sim, bu):
        j = pl.program_id(1)

        @pl.when(j == 0)
        def _():
            sre[...] = jnp.zeros_like(sre)
            sim[...] = jnp.zeros_like(sim)

        _bd_expand(u_ref[...].astype(BF16), bm_ref, bu)
        lre, lim = lre_ref[...], lim_ref[...]

        def step(i, c):
            sr, si = c
            off = pl.multiple_of(i * 8, 8)
            br = bu[pl.ds(off, 8), 0:NST]
            bi = bu[pl.ds(off, 8), NST:2 * NST]
            return lre * sr - lim * si + br, lre * si + lim * sr + bi

        sr, si = lax.fori_loop(0, tt // 8, step, (sre[...], sim[...]))
        sre[...] = sr
        sim[...] = si

        @pl.when(j == nt - 1)
        def _():
            pr, pi = _pow2k(lre[0:1], lim[0:1], nsq)
            cr = jnp.zeros((1, NST), F32)
            ci = jnp.zeros((1, NST), F32)
            cre_ref[0:1, :] = cr
            cim_ref[0:1, :] = ci
            for k in range(1, 8):
                cr, ci = sr[k - 1:k] + pr * cr - pi * ci, si[k - 1:k] + pr * ci + pi * cr
                cre_ref[k:k + 1, :] = cr
                cim_ref[k:k + 1, :] = ci

    return pl.pallas_call(
        body, name="ssm_local", grid=(nb, nt),
        out_shape=[jax.ShapeDtypeStruct((nb * 8, NST), F32)] * 2,
        in_specs=[pl.BlockSpec((tt, D_SSM), lambda b, j: (b * nt + j, 0)), _VM, _VM, _VM],
        out_specs=[pl.BlockSpec((8, NST), lambda b, j: (b, 0))] * 2,
        scratch_shapes=[pltpu.VMEM((8, NST), F32), pltpu.VMEM((8, NST), F32), pltpu.VMEM((tt, 2 * NST), F32)],
        compiler_params=_cp(("arbitrary", "arbitrary")),
    )(u_p, bm, lre8, lim8)


def _ssm_fwd(u_p, cre, cim, bm, cm, dvec, w_glu, lre8, lim8, S, tt):
    n = u_p.shape[0]
    nb, nt = n // S, S // tt

    def body(u_ref, cre_ref, cim_ref, bm_ref, cm_ref, d_ref, wg_ref, lre_ref, lim_ref,
             st_ref, ypre_ref, z_ref, gact_ref, yssm_ref, sre, sim, bu):
        j = pl.program_id(1)

        @pl.when(j == 0)
        def _():
            sre[...] = cre_ref[...]
            sim[...] = cim_ref[...]

        u = u_ref[...]
        _bd_expand(u.astype(BF16), bm_ref, bu)
        lre, lim = lre_ref[...], lim_ref[...]

        def step(i, c):
            sr, si = c
            off = pl.multiple_of(i * 8, 8)
            nr = lre * sr - lim * si + bu[pl.ds(off, 8), 0:NST]
            ni = lre * si + lim * sr + bu[pl.ds(off, 8), NST:2 * NST]
            bu[pl.ds(off, 8), 0:NST] = nr
            bu[pl.ds(off, 8), NST:2 * NST] = ni
            return nr, ni

        sr, si = lax.fori_loop(0, tt // 8, step, (sre[...], sim[...]))
        sre[...] = sr
        sim[...] = si
        stb = bu[...].astype(BF16)
        st_ref[...] = stb
        y = _bd_project(stb, cm_ref) + d_ref[...] * u
        ypre_ref[...] = y
        gb = _gelu(y).astype(BF16)
        gact_ref[...] = gb
        z = _dot(gb, wg_ref[...])
        z_ref[...] = z
        yssm_ref[...] = z[:, 0:D_SSM] * _sigmoid(z[:, D_SSM:2 * D_SSM])

    row = lambda w: pl.BlockSpec((tt, w), lambda b, j: (b * nt + j, 0))
    return pl.pallas_call(
        body, name="ssm_fwd", grid=(nb, nt),
        out_shape=[jax.ShapeDtypeStruct((n, 2 * NST), BF16), jax.ShapeDtypeStruct((n, D_SSM), F32),
                   jax.ShapeDtypeStruct((n, 2 * D_SSM), F32), jax.ShapeDtypeStruct((n, D_SSM), BF16),
                   jax.ShapeDtypeStruct((n, D_SSM), F32)],
        in_specs=[row(D_SSM), pl.BlockSpec((8, NST), lambda b, j: (b, 0)), pl.BlockSpec((8, NST), lambda b, j: (b, 0)),
                  _VM, _VM, _VM, _VM, _VM, _VM],
        out_specs=[row(2 * NST), row(D_SSM), row(2 * D_SSM), row(D_SSM), row(D_SSM)],
        scratch_shapes=[pltpu.VMEM((8, NST), F32), pltpu.VMEM((8, NST), F32), pltpu.VMEM((tt, 2 * NST), F32)],
        compiler_params=_cp(("arbitrary", "arbitrary")),
    )(u_p, cre, cim, bm, cm, dvec, w_glu, lre8, lim8)


def _ssm_bwd_a(dys_p, z, ypre, w_glu, cm, lre8, lim8, S, tt):
    n = z.shape[0]
    nb, nt = n // S, S // tt
    nsq = int(round(math.log2(S // 8)))
    ng = tt // 8

    def body(dys_ref, z_ref, y_ref, wg_ref, cm_ref, lre_ref, lim_ref, dy_ref, dz_ref, are_ref, aim_ref, sre, sim, gb):
        j = pl.program_id(1)

        @pl.when(j == 0)
        def _():
            sre[...] = jnp.zeros_like(sre)
            sim[...] = jnp.zeros_like(sim)

        z = z_ref[...]
        z1, z2 = z[:, 0:D_SSM], z[:, D_SSM:2 * D_SSM]
        sg = _sigmoid(z2)
        dys = dys_ref[...]
        dz = jnp.concatenate([dys * sg, dys * z1 * sg * (1.0 - sg)], axis=1).astype(BF16)
        dz_ref[...] = dz
        dy = _dot_nt(dz, wg_ref[...]) * _gelu_grad(y_ref[...])
        dy_ref[...] = dy
        _bd_expand_t(dy.astype(BF16), cm_ref, gb)
        lre, lim = lre_ref[...], lim_ref[...]

        def step(i, c):
            ar, ai = c
            off = pl.multiple_of((ng - 1 - i) * 8, 8)
            gr = gb[pl.ds(off, 8), 0:NST]
            gi = gb[pl.ds(off, 8), NST:2 * NST]
            return lre * ar + lim * ai + gr, lre * ai - lim * ar + gi

        ar, ai = lax.fori_loop(0, ng, step, (sre[...], sim[...]))
        sre[...] = ar
        sim[...] = ai

        @pl.when(j == nt - 1)
        def _():
            pr, pi = _pow2k(lre[0:1], -lim[0:1], nsq)
            cr = jnp.zeros((1, NST), F32)
            ci = jnp.zeros((1, NST), F32)
            are_ref[7:8, :] = cr
            aim_ref[7:8, :] = ci
            for k in range(6, -1, -1):
                cr, ci = ar[k + 1:k + 2] + pr * cr - pi * ci, ai[k + 1:k + 2] + pr * ci + pi * cr
                are_ref[k:k + 1, :] = cr
                aim_ref[k:k + 1, :] = ci

    row = lambda w: pl.BlockSpec((tt, w), lambda b, j: (b * nt + nt - 1 - j, 0))
    return pl.pallas_call(
        body, name="ssm_bwd_a", grid=(nb, nt),
        out_shape=[jax.ShapeDtypeStruct((n, D_SSM), F32), jax.ShapeDtypeStruct((n, 2 * D_SSM), BF16),
                   jax.ShapeDtypeStruct((nb * 8, NST), F32), jax.ShapeDtypeStruct((nb * 8, NST), F32)],
        in_specs=[row(D_SSM), row(2 * D_SSM), row(D_SSM), _VM, _VM, _VM, _VM],
        out_specs=[row(D_SSM), row(2 * D_SSM), pl.BlockSpec((8, NST), lambda b, j: (b, 0)),
                   pl.BlockSpec((8, NST), lambda b, j: (b, 0))],
        scratch_shapes=[pltpu.VMEM((8, NST), F32), pltpu.VMEM((8, NST), F32), pltpu.VMEM((tt, 2 * NST), F32)],
        compiler_params=_cp(("arbitrary", "arbitrary")),
    )(dys_p, z, ypre, w_glu, cm, lre8, lim8)


def _ssm_bwd_b(dy, u_p, st, fcr, fci, air, aii, bm, cm, dvec, lre8, lim8, S, tt):
    n = u_p.shape[0]
    nb, nt = n // S, S // tt
    ng = tt // 8

    def body(dy_ref, u_ref, st_ref, stp_ref, fcr_ref, fci_ref, air_ref, aii_ref, bm_ref, cm_ref, d_ref, lre_ref, lim_ref,
             du_ref, dcm_ref, dbm_ref, dd_ref, dlr_ref, dli_ref, are, aim, accr, acci, sp, ab):
        b = pl.program_id(0)
        j = pl.program_id(1)
        jt = nt - 1 - j

        @pl.when((b == 0) & (j == 0))
        def _():
            dcm_ref[...] = jnp.zeros_like(dcm_ref)
            dbm_ref[...] = jnp.zeros_like(dbm_ref)
            dd_ref[...] = jnp.zeros_like(dd_ref)
            accr[...] = jnp.zeros_like(accr)
            acci[...] = jnp.zeros_like(acci)

        @pl.when(j == 0)
        def _():
            are[...] = air_ref[...]
            aim[...] = aii_ref[...]

        sp[8:tt + 8, :] = st_ref[...].astype(F32)

        @pl.when(jt == 0)
        def _():
            sp[0:8, 0:NST] = fcr_ref[...]
            sp[0:8, NST:2 * NST] = fci_ref[...]

        @pl.when(jt != 0)
        def _():
            sp[0:8, :] = stp_ref[8:16, :].astype(F32)

        dy = dy_ref[...]
        u = u_ref[...]
        dyb = dy.astype(BF16)
        _bd_expand_t(dyb, cm_ref, ab)
        lre, lim = lre_ref[...], lim_ref[...]

        def step(i, c):
            ar, ai = c
            off = pl.multiple_of((ng - 1 - i) * 8, 8)
            nr = lre * ar + lim * ai + ab[pl.ds(off, 8), 0:NST]
            ni = lre * ai - lim * ar + ab[pl.ds(off, 8), NST:2 * NST]
            ab[pl.ds(off, 8), 0:NST] = nr
            ab[pl.ds(off, 8), NST:2 * NST] = ni
            pr = sp[pl.ds(off, 8), 0:NST]
            pi = sp[pl.ds(off, 8), NST:2 * NST]
            accr[...] += nr * pr + ni * pi
            acci[...] += ni * pr - nr * pi
            return nr, ni

        ar, ai = lax.fori_loop(0, ng, step, (are[...], aim[...]))
        are[...] = ar
        aim[...] = ai
        a_b = ab[...].astype(BF16)
        du_ref[...] = _bd_project_t(a_b, bm_ref) + d_ref[...] * dy
        ub = u.astype(BF16)
        for q in range(4):
            for part in range(2):
                lo = part * NST + q * 4 * QB
                s_q = st_ref[:, lo:lo + 4 * QB]
                dcm_ref[lo:lo + 4 * QB, :] += _dot_tn(s_q, dyb[:, q * QB:(q + 1) * QB])
                dbm_ref[:, lo:lo + 4 * QB] += _dot_tn(ub[:, q * QB:(q + 1) * QB], a_b[:, lo:lo + 4 * QB])
        dd_ref[...] += _colsum(dy * u)

        @pl.when((b == nb - 1) & (j == nt - 1))
        def _():
            dlr_ref[...] = _colsum(accr[...])
            dli_ref[...] = _colsum(acci[...])

    row = lambda w: pl.BlockSpec((tt, w), lambda b, j: (b * nt + nt - 1 - j, 0))
    seq8 = pl.BlockSpec((8, NST), lambda b, j: (b, 0))
    prev = pl.BlockSpec((16, 2 * NST), lambda b, j: (jnp.maximum((b * nt + nt - 1 - j) * (tt // 16) - 1, 0), 0))
    const = lambda shape: pl.BlockSpec(shape, lambda b, j: (0, 0))
    return pl.pallas_call(
        body, name="ssm_bwd_b", grid=(nb, nt),
        out_shape=[jax.ShapeDtypeStruct((n, D_SSM), F32), jax.ShapeDtypeStruct((2 * NST, QB), F32),
                   jax.ShapeDtypeStruct((QB, 2 * NST), F32), jax.ShapeDtypeStruct((1, D_SSM), F32),
                   jax.ShapeDtypeStruct((1, NST), F32), jax.ShapeDtypeStruct((1, NST), F32)],
        in_specs=[row(D_SSM), row(D_SSM), row(2 * NST), prev, seq8, seq8, seq8, seq8, _VM, _VM, _VM, _VM, _VM],
        out_specs=[row(D_SSM), const((2 * NST, QB)), const((QB, 2 * NST)), const((1, D_SSM)),
                   const((1, NST)), const((1, NST))],
        scratch_shapes=[pltpu.VMEM((8, NST), F32)] * 4 + [pltpu.VMEM((tt + 8, 2 * NST), F32),
                                                          pltpu.VMEM((tt, 2 * NST), F32)],
        compiler_params=_cp(("arbitrary", "arbitrary")),
    )(dy, u_p, st, st, fcr, fci, air, aii, bm, cm, dvec, lre8, lim8)


def _rope(v, c, s1, s2):
    return v * c + _roll(v, -16) * s1 + _roll(v, 16) * s2


def _rope_t(dv, c, s1, s2):
    return dv * c + _roll(dv * s1, 16) + _roll(dv * s2, -16)


def _mla_fwd(proj, rc, rs1, rs2, gq, gkv, w_uq, w_ukv, tm):
    n = proj.shape[0]

    def body(ql_ref, kvl_ref, kr_ref, c_ref, s1_ref, s2_ref, gq_ref, gkv_ref, wq_ref, wkv_ref,
             q_ref, k_ref, v_ref, qn_ref, kvn_ref):
        c, s1, s2 = c_ref[...], s1_ref[...], s2_ref[...]
        qhat, _ = _rms(ql_ref[...], Q_LORA)
        qn = (qhat * gq_ref[...]).astype(BF16)
        qn_ref[...] = qn
        q = _dot(qn, wq_ref[...])
        qr = _rope(q, jnp.tile(c, (1, NH)), jnp.tile(s1, (1, NH)), jnp.tile(s2, (1, NH)))
        q_ref[...] = (qr * _C2).astype(BF16)
        khat, _ = _rms(kvl_ref[...], KV_LORA)
        kvn = (khat * gkv_ref[...]).astype(BF16)
        kvn_ref[...] = kvn
        kv = _dot(kvn, wkv_ref[...])
        kr = _rope(_roll(kr_ref[...], 64), c, s1, s2)
        k_ref[...] = (kv[:, 0:NH * HP] + jnp.tile(kr, (1, NH))).astype(BF16)
        v_ref[...] = kv[:, NH * HP:2 * NH * HP].astype(BF16)

    def wrapped(proj_ref, *rest):
        ql = proj_ref.at[:, D_SSM:D_SSM + Q_LORA]
        kvl = proj_ref.at[:, D_SSM + Q_LORA:D_SSM + Q_LORA + KV_LORA]
        kr = proj_ref.at[:, IN_PAD - HP:IN_PAD]
        body(ql, kvl, kr, *rest)

    row = lambda w: pl.BlockSpec((tm, w), lambda i: (i, 0))
    return pl.pallas_call(
        wrapped, name="mla_fwd", grid=(n // tm,),
        out_shape=[jax.ShapeDtypeStruct((n, NH * HP), BF16)] * 3 +
                  [jax.ShapeDtypeStruct((n, Q_LORA), BF16), jax.ShapeDtypeStruct((n, KV_LORA), BF16)],
        in_specs=[row(IN_PAD), row(HP), row(HP), row(HP), _VM, _VM, _VM, _VM],
        out_specs=[row(NH * HP)] * 3 + [row(Q_LORA), row(KV_LORA)],
        compiler_params=_cp(("parallel",)),
    )(proj, rc, rs1, rs2, gq, gkv, w_uq, w_ukv)


def _mla_bwd(dq, dk, dv, proj, rc, rs1, rs2, gq, gkv, w_uq, w_ukv, tm):
    n = proj.shape[0]

    def body(dq_ref, dk_ref, dv_ref, proj_ref, c_ref, s1_ref, s2_ref, gq_ref, gkv_ref, wq_ref, wkv_ref,
             dmla_ref, dqb_ref, dkvb_ref, acc_ref):
        i = pl.program_id(0)
        c, s1, s2 = c_ref[...], s1_ref[...], s2_ref[...]
        dqu = _rope_t(dq_ref[...] * _SCALE, jnp.tile(c, (1, NH)), jnp.tile(s1, (1, NH)),
                      jnp.tile(s2, (1, NH))).astype(BF16)
        dqb_ref[...] = dqu
        dqn = _dot_nt(dqu, wq_ref[...])
        qhat, rq = _rms(proj_ref[:, D_SSM:D_SSM + Q_LORA], Q_LORA)
        dql = _rms_bwd(dqn * gq_ref[...], qhat, rq, Q_LORA)
        dkf = dk_ref[...] * (1.0 / _LOG2E)
        dkv = jnp.concatenate([dkf.astype(BF16), dv_ref[...].astype(BF16)], axis=1)
        dkvb_ref[...] = dkv
        dkvn = _dot_nt(dkv, wkv_ref[...])
        khat, rk = _rms(proj_ref[:, D_SSM + Q_LORA:D_SSM + Q_LORA + KV_LORA], KV_LORA)
        dkvl = _rms_bwd(dkvn * gkv_ref[...], khat, rk, KV_LORA)
        dkr = dkf[:, 0:HP]
        for h in range(1, NH):
            dkr = dkr + dkf[:, h * HP:(h + 1) * HP]
        lane = lax.broadcasted_iota(jnp.int32, dkr.shape, 1)
        dkr = jnp.where((lane >= QK_NOPE) & (lane < QK_NOPE + QK_ROPE), dkr, 0.0)
        dkr = _roll(_rope_t(dkr, c, s1, s2), -64)
        dmla_ref[...] = jnp.concatenate([dql, dkvl, dkr], axis=1)

        @pl.when(i == 0)
        def _():
            acc_ref[...] = jnp.zeros_like(acc_ref)

        acc_ref[0:1, 0:Q_LORA] += _colsum(dqn * qhat)
        acc_ref[1:2, 0:KV_LORA] += _colsum(dkvn * khat)

    row = lambda w: pl.BlockSpec((tm, w), lambda i: (i, 0))
    return pl.pallas_call(
        body, name="mla_bwd", grid=(n // tm,),
        out_shape=[jax.ShapeDtypeStruct((n, IN_PAD - D_SSM), F32), jax.ShapeDtypeStruct((n, NH * HP), BF16),
                   jax.ShapeDtypeStruct((n, 2 * NH * HP), BF16), jax.ShapeDtypeStruct((8, Q_LORA), F32)],
        in_specs=[row(NH * HP)] * 3 + [row(IN_PAD), row(HP), row(HP), row(HP), _VM, _VM, _VM, _VM],
        out_specs=[row(IN_PAD - D_SSM), row(NH * HP), row(2 * NH * HP), pl.BlockSpec((8, Q_LORA), lambda i: (0, 0))],
        compiler_params=_cp(("arbitrary",)),
    )(dq, dk, dv, proj, rc, rs1, rs2, gq, gkv, w_uq, w_ukv)


_SCALE = (QK_NOPE + QK_ROPE) ** -0.5
_LOG2E = 1.4426950408889634
_C2 = _SCALE * _LOG2E


def _attn_fwd(q, k, v, S, tq):
    n = q.shape[0]
    nb, nq = n // S, S // tq

    def body(q_ref, k_ref, v_ref, o_ref, lr_ref):
        qi = pl.program_id(2)
        qv = q_ref[...]

        def tile(j, c, diagonal):
            m, l, acc = c
            off = pl.multiple_of(j * tq, tq)
            s = _dot_nt(qv, k_ref[pl.ds(off, tq), :])
            if diagonal:
                rows = lax.broadcasted_iota(jnp.int32, s.shape, 0)
                cols = lax.broadcasted_iota(jnp.int32, s.shape, 1)
                s = jnp.where(cols <= rows, s, NEG)
            mn = jnp.maximum(m, jnp.max(s, axis=1, keepdims=True))
            p = jnp.exp2(s - mn)
            al = jnp.exp2(m - mn)
            l = al * l + jnp.sum(p, axis=1, keepdims=True)
            acc = al * acc + _dot(p.astype(BF16), v_ref[pl.ds(off, tq), :])
            return mn, l, acc

        init = (jnp.full((tq, 1), NEG, F32), jnp.zeros((tq, 1), F32), jnp.zeros((tq, HP), F32))
        c = lax.fori_loop(0, qi, lambda j, c: tile(j, c, False), init)
        m, l, acc = tile(qi, c, True)
        o_ref[...] = (acc / l).astype(BF16)
        lane = lax.broadcasted_iota(jnp.int32, (8, HP), 1)
        lse = jnp.broadcast_to(m + jnp.log(l) * _LOG2E, (tq, HP))
        lr_ref[...] = _rows_of(lse, jnp.where(lane == 0, 1.0, 0.0).astype(BF16))

    qs = pl.BlockSpec((tq, HP), lambda b, h, i: (b * nq + i, h))
    ks = pl.BlockSpec((S, HP), lambda b, h, i: (b, h))
    return pl.pallas_call(
        body, name="attn_fwd", grid=(nb, NH, nq),
        out_shape=[jax.ShapeDtypeStruct((n, NH * HP), BF16), jax.ShapeDtypeStruct((nb * NH * 8, S), F32)],
        in_specs=[qs, ks, ks], out_specs=[qs, pl.BlockSpec((8, tq), lambda b, h, i: (b * NH + h, i))],
        compiler_params=_cp(("parallel", "parallel", "arbitrary")),
    )(q, k, v)


def _rows_of(x, pick):
    x1 = x.astype(BF16)
    r1 = x - x1.astype(F32)
    x2 = r1.astype(BF16)
    x3 = (r1 - x2.astype(F32)).astype(BF16)
    return _dot_nt(pick, x1) + _dot_nt(pick, x2) + _dot_nt(pick, x3)


def _attn_bwd(q, k, v, dob, lrow, drow, S, tq):
    n = q.shape[0]
    nb, nq = n // S, S // tq

    def body(q_ref, k_ref, v_ref, do_ref, lr_ref, dr_ref, dqo_ref, dk_ref, dv_ref, dq_ref):
        kj = pl.program_id(2)

        @pl.when(kj == 0)
        def _():
            dq_ref[...] = jnp.zeros_like(dq_ref)

        kt = k_ref[...]
        vt = v_ref[...]

        def tile(i, c, diagonal):
            dk, dv = c
            off = pl.multiple_of(i * tq, tq)
            qv = q_ref[pl.ds(off, tq), :]
            dob = do_ref[pl.ds(off, tq), :]
            lr = lr_ref[0:1, pl.ds(off, tq)]
            dr = dr_ref[0:1, pl.ds(off, tq)]
            st = _dot_nt(kt, qv)
            dpt = _dot_nt(vt, dob)
            pt = jnp.exp2(st - lr)
            if diagonal:
                keys = lax.broadcasted_iota(jnp.int32, pt.shape, 0)
                qrys = lax.broadcasted_iota(jnp.int32, pt.shape, 1)
                pt = jnp.where(keys <= qrys, pt, 0.0)
            dst = (pt * (dpt - dr)).astype(BF16)
            dq_ref[pl.ds(off, tq), :] += _dot_tn(dst, kt)
            return dk + _dot(dst, qv), dv + _dot(pt.astype(BF16), dob)

        zero = jnp.zeros((tq, HP), F32)
        c = tile(kj, (zero, zero), True)
        dk, dv = lax.fori_loop(kj + 1, nq, lambda i, c: tile(i, c, False), c)
        dk_ref[...] = dk.astype(BF16)
        dv_ref[...] = dv.astype(BF16)

        @pl.when(kj == nq - 1)
        def _():
            dqo_ref[...] = dq_ref[...].astype(BF16)

    ts = pl.BlockSpec((tq, HP), lambda b, h, i: (b * nq + i, h))
    fs = pl.BlockSpec((S, HP), lambda b, h, i: (b, h))
    rs = pl.BlockSpec((8, S), lambda b, h, i: (b * NH + h, 0))
    return pl.pallas_call(
        body, name="attn_bwd", grid=(nb, NH, nq),
        out_shape=[jax.ShapeDtypeStruct((n, NH * HP), BF16)] * 3,
        in_specs=[fs, ts, ts, fs, rs, rs], out_specs=[fs, ts, ts],
        scratch_shapes=[pltpu.VMEM((S, HP), F32)],
        compiler_params=_cp(("parallel", "parallel", "arbitrary")),
    )(q, k, v, dob, lrow, drow)


def _p1_fwd(yssm, oattn, x, modp, gs, ga, w_out, g2, S, tm):
    n = x.shape[0]
    tps = S // tm

    def body(ys_ref, oa_ref, x_ref, mod_ref, gs_ref, ga_ref, w_ref, g2_ref, yn_ref, o_ref, x1_ref, h2_ref):
        yh, _ = _rms(ys_ref[...], D_SSM)
        ah, _ = _rms(oa_ref[...].astype(F32), D_ATTN)
        yn = jnp.concatenate([yh * gs_ref[...], ah * ga_ref[...]], axis=1).astype(BF16)
        yn_ref[...] = yn
        o = _dot(yn, w_ref[...])
        o_ref[...] = o.astype(BF16)
        x1 = x_ref[...] + mod_ref[0, 2:3, :] * o
        x1_ref[...] = x1
        xh, _ = _rms(x1, D)
        h2_ref[...] = ((xh * g2_ref[...]) * (1.0 + mod_ref[0, 4:5, :]) + mod_ref[0, 3:4, :]).astype(BF16)

    row = lambda w: pl.BlockSpec((tm, w), lambda i: (i, 0))
    return pl.pallas_call(
        body, name="p1_fwd", grid=(n // tm,),
        out_shape=[jax.ShapeDtypeStruct((n, D_SSM + NH * HP), BF16), jax.ShapeDtypeStruct((n, D), BF16),
                   jax.ShapeDtypeStruct((n, D), F32), jax.ShapeDtypeStruct((n, D), BF16)],
        in_specs=[row(D_SSM), row(NH * HP), row(D), pl.BlockSpec((1, 8, D), lambda i: (i // tps, 0, 0)),
                  _VM, _VM, _VM, _VM],
        out_specs=[row(D_SSM + NH * HP), row(D), row(D), row(D)],
        compiler_params=_cp(("parallel",)),
    )(yssm, oattn, x, modp, gs, ga, w_out, g2)


def _p2(x1, h2, target, modp, g2, gf, w_ff1, w_ff2, S, tm):
    n = x1.shape[0]
    tps = S // tm
    nb = n // S

    def body(x1_ref, h2_ref, t_ref, mod_ref, g2_ref, gf_ref, w1_ref, w2_ref,
             dx1_ref, r_ref, da_ref, dff_ref, accs_ref, accg_ref):
        i = pl.program_id(0)
        sh2, sc2, gt2 = mod_ref[0, 3:4, :], mod_ref[0, 4:5, :], mod_ref[0, 5:6, :]
        fsh, fsc = mod_ref[0, 6:7, :], mod_ref[0, 7:8, :]
        x1 = x1_ref[...]
        a = _dot(h2_ref[...], w1_ref[...])
        ra = jnp.maximum(a, 0.0)
        rb = (ra * ra).astype(BF16)
        r_ref[...] = rb
        ff = _dot(rb, w2_ref[...])
        x2 = x1 + gt2 * ff
        x2h, rf = _rms(x2, D)
        gf_v = gf_ref[...]
        outn = x2h * gf_v
        err = outn * (1.0 + fsc) + fsh - t_ref[...]
        dout = err * (1.0 / D)
        doutn = dout * (1.0 + fsc)
        dx2 = _rms_bwd(doutn * gf_v, x2h, rf, D)
        dff = (gt2 * dx2).astype(BF16)
        dff_ref[...] = dff
        dr = _dot_nt(dff, w2_ref[...])
        da = (dr * (2.0 * ra)).astype(BF16)
        da_ref[...] = da
        dh2 = _dot_nt(da, w1_ref[...])
        x1h, r2 = _rms(x1, D)
        g2_v = g2_ref[...]
        dn2 = dh2 * (1.0 + sc2)
        dx1_ref[...] = dx2 + _rms_bwd(dn2 * g2_v, x1h, r2, D)

        @pl.when(i % tps == 0)
        def _():
            accs_ref[...] = jnp.zeros_like(accs_ref)

        @pl.when(i == 0)
        def _():
            accg_ref[...] = jnp.zeros_like(accg_ref)

        accs_ref[0, 3:4, :] += _colsum(dh2)
        accs_ref[0, 4:5, :] += _colsum(dh2 * (x1h * g2_v))
        accs_ref[0, 5:6, :] += _colsum(dx2 * ff)
        accs_ref[0, 6:7, :] += _colsum(dout)
        accs_ref[0, 7:8, :] += _colsum(dout * outn)
        accg_ref[0:1, :] += _colsum(dn2 * x1h)
        accg_ref[1:2, :] += _colsum(doutn * x2h)
        accg_ref[2:3, :] += _colsum(err * err) * (0.5 / D)

    row = lambda w: pl.BlockSpec((tm, w), lambda i: (i, 0))
    return pl.pallas_call(
        body, name="p2_mlp_loss", grid=(n // tm,),
        out_shape=[jax.ShapeDtypeStruct((n, D), F32), jax.ShapeDtypeStruct((n, D_FF), BF16),
                   jax.ShapeDtypeStruct((n, D_FF), BF16), jax.ShapeDtypeStruct((n, D), BF16),
                   jax.ShapeDtypeStruct((nb, 8, D), F32), jax.ShapeDtypeStruct((8, D), F32)],
        in_specs=[row(D), row(D), row(D), pl.BlockSpec((1, 8, D), lambda i: (i // tps, 0, 0)), _VM, _VM, _VM, _VM],
        out_specs=[row(D), row(D_FF), row(D_FF), row(D), pl.BlockSpec((1, 8, D), lambda i: (i // tps, 0, 0)),
                   pl.BlockSpec((8, D), lambda i: (0, 0))],
        compiler_params=_cp(("arbitrary",)),
    )(x1, h2, target, modp, g2, gf, w_ff1, w_ff2)


def _p3_bwd(dx1, o, yssm, oattn, modp, gs, ga, w_out, S, tm):
    n = dx1.shape[0]
    tps = S // tm
    nb = n // S

    def body(dx1_ref, o_ref, ys_ref, oa_ref, mod_ref, gs_ref, ga_ref, w_ref,
             do_ref, dys_ref, doa_ref, dr_ref, accs_ref, accg_ref):
        i = pl.program_id(0)
        dx1 = dx1_ref[...]
        dob = (mod_ref[0, 2:3, :] * dx1).astype(BF16)
        do_ref[...] = dob
        dyn = _dot_nt(dob, w_ref[...])
        yh, rs = _rms(ys_ref[...], D_SSM)
        oa = oa_ref[...].astype(F32)
        ah, ra = _rms(oa, D_ATTN)
        d1 = dyn[:, 0:D_SSM]
        d2 = dyn[:, D_SSM:D_SSM + NH * HP]
        dys_ref[...] = _rms_bwd(d1 * gs_ref[...], yh, rs, D_SSM)
        doa = _rms_bwd(d2 * ga_ref[...], ah, ra, D_ATTN)
        doa_ref[...] = doa.astype(BF16)
        prod = doa * oa
        ones = jnp.ones((8, HP), BF16)
        for h in range(NH):
            dr_ref[h * 8:(h + 1) * 8, :] = _rows_of(prod[:, h * HP:(h + 1) * HP], ones)

        @pl.when(i % tps == 0)
        def _():
            accs_ref[...] = jnp.zeros_like(accs_ref)

        @pl.when(i == 0)
        def _():
            accg_ref[...] = jnp.zeros_like(accg_ref)

        accs_ref[0, 2:3, :] += _colsum(dx1 * o_ref[...])
        accg_ref[0:1, 0:D_SSM] += _colsum(d1 * yh)
        accg_ref[1:2, :] += _colsum(d2 * ah)

    row = lambda w: pl.BlockSpec((tm, w), lambda i: (i, 0))
    return pl.pallas_call(
        body, name="p3_bwd", grid=(n // tm,),
        out_shape=[jax.ShapeDtypeStruct((n, D), BF16), jax.ShapeDtypeStruct((n, D_SSM), F32),
                   jax.ShapeDtypeStruct((n, NH * HP), BF16), jax.ShapeDtypeStruct((nb * NH * 8, S), F32),
                   jax.ShapeDtypeStruct((nb, 8, D), F32), jax.ShapeDtypeStruct((8, NH * HP), F32)],
        in_specs=[row(D), row(D), row(D_SSM), row(NH * HP), pl.BlockSpec((1, 8, D), lambda i: (i // tps, 0, 0)),
                  _VM, _VM, _VM],
        out_specs=[row(D), row(D_SSM), row(NH * HP), pl.BlockSpec((NH * 8, tm), lambda i: (i // tps, i % tps)),
                   pl.BlockSpec((1, 8, D), lambda i: (i // tps, 0, 0)), pl.BlockSpec((8, NH * HP), lambda i: (0, 0))],
        compiler_params=_cp(("arbitrary",)),
    )(dx1, o, yssm, oattn, modp, gs, ga, w_out)


def _wgrad(a, b, name, col_slots=0):
    n, k1 = a.shape
    k2 = b.shape[1]
    bn = next((b for b in (1024, 512) if n % b == 0), n)
    bk1 = next((b for b in (1024, 512) if k1 % b == 0), k1)
    bk2 = k2 // col_slots if col_slots else (1024 if (k2 % 1024 == 0) else k2)

    def body(a_ref, b_ref, o_ref):
        @pl.when(pl.program_id(2) == 0)
        def _():
            o_ref[...] = jnp.zeros_like(o_ref)

        o_ref[...] += _dot_tn(a_ref[...], b_ref[...]).reshape(o_ref.shape)

    if col_slots:
        out_shape = jax.ShapeDtypeStruct((col_slots, k1, bk2), F32)
        out_spec = pl.BlockSpec((1, bk1, bk2), lambda i, j, t: (j, i, 0))
    else:
        out_shape = jax.ShapeDtypeStruct((k1, k2), F32)
        out_spec = pl.BlockSpec((bk1, bk2), lambda i, j, t: (i, j))
    return pl.pallas_call(
        body, name=name, grid=(k1 // bk1, k2 // bk2, n // bn),
        out_shape=out_shape,
        in_specs=[pl.BlockSpec((bn, bk1), lambda i, j, t: (t, i)), pl.BlockSpec((bn, bk2), lambda i, j, t: (t, j))],
        out_specs=out_spec,
        compiler_params=_cp(("parallel", "parallel", "arbitrary")),
    )(a, b)


def _row_block(rows):
    if rows <= 256:
        return rows
    return next(b for b in (256, 192, 128, 64, 32, 16, 8) if rows % b == 0)


def _add_half(g, recv, cidx, name):
    _, rows2, w = g.shape
    rows = rows2 // 2
    br = _row_block(rows)
    nblk = rows // br

    def body(c_ref, g_ref, r_ref, o_ref):
        o_ref[...] = (g_ref[...] + r_ref[...]).astype(BF16)

    return pl.pallas_call(
        body, name=name,
        grid_spec=pltpu.PrefetchScalarGridSpec(
            num_scalar_prefetch=1, grid=(4, nblk),
            in_specs=[pl.BlockSpec((1, br, w), lambda s, i, c: (s, c[0] * nblk + i, 0)),
                      pl.BlockSpec((1, br, w), lambda s, i, c: (s, i, 0))],
            out_specs=pl.BlockSpec((1, br, w), lambda s, i, c: (s, i, 0))),
        out_shape=jax.ShapeDtypeStruct((4, rows, w), BF16),
        compiler_params=_cp(("parallel", "parallel")),
    )(cidx, g, recv)


def _add_chips(r, name):
    _, rows, w = r.shape
    br = _row_block(rows)

    def body(r_ref, o_ref):
        f = lambda k: r_ref[k].astype(F32)
        o_ref[...] = ((f(0) + f(1)) + f(2)) + f(3)

    return pl.pallas_call(
        body, name=name, grid=(rows // br,),
        out_shape=jax.ShapeDtypeStruct((rows, w), F32),
        in_specs=[pl.BlockSpec((4, br, w), lambda i: (0, i, 0))],
        out_specs=pl.BlockSpec((br, w), lambda i: (i, 0)),
        compiler_params=_cp(("parallel",)),
    )(r)


def _pair_sum(a, sa, b, sb):
    def body(a_ref, sa_ref, b_ref, sb_ref, oa_ref, ob_ref):
        oa_ref[...] = (a_ref[...].astype(F32) + sa_ref[...].astype(F32)).astype(BF16)
        ob_ref[...] = b_ref[...] + sb_ref[...]

    return pl.pallas_call(
        body, name="small_grad_pair_sum",
        out_shape=[jax.ShapeDtypeStruct(a.shape, BF16), jax.ShapeDtypeStruct(b.shape, F32)],
        in_specs=[_VM] * 4, out_specs=[_VM, _VM], compiler_params=_cp(),
    )(a, sa, b, sb)


def _sum_devices(a, b):
    def body(a_ref, b_ref, oa_ref, ob_ref):
        acc = a_ref[0:1, :].astype(F32)
        accb = b_ref[0:1, :]
        for k in range(1, a.shape[0]):
            acc = acc + a_ref[k:k + 1, :].astype(F32)
            accb = accb + b_ref[k:k + 1, :]
        oa_ref[...] = acc
        ob_ref[...] = accb

    return pl.pallas_call(
        body, name="small_grad_sum",
        out_shape=[jax.ShapeDtypeStruct((1, a.shape[1]), F32), jax.ShapeDtypeStruct((1, b.shape[1]), F32)],
        in_specs=[_VM, _VM], out_specs=[_VM, _VM], compiler_params=_cp(),
    )(a, b)


def _adamw_math(wv, gv, mv, vv):
    m_new = ADAM_B1 * mv + (1.0 - ADAM_B1) * gv
    v_new = ADAM_B2 * vv + (1.0 - ADAM_B2) * (gv * gv)
    m_hat = m_new / (1.0 - ADAM_B1 ** ADAM_STEP)
    v_hat = v_new / (1.0 - ADAM_B2 ** ADAM_STEP)
    return -ADAM_LR * (m_hat / (jnp.sqrt(v_hat) + ADAM_EPS) + ADAM_WD * wv), m_new, v_new


def _adamw_small(ws, gs, ms, vs):
    k = len(ws)

    def body(*refs):
        ins, outs = refs[:4 * k], refs[4 * k:]
        for t in range(k):
            d, m_new, v_new = _adamw_math(ins[t][...], ins[k + t][...], ins[2 * k + t][...], ins[3 * k + t][...])
            outs[t][...] = d
            outs[k + t][...] = m_new
            outs[2 * k + t][...] = v_new

    shapes = [jax.ShapeDtypeStruct(w.shape, F32) for w in ws]
    return pl.pallas_call(
        body, name="adamw_small", out_shape=shapes * 3,
        in_specs=[_VM] * (4 * k), out_specs=[_VM] * (3 * k), compiler_params=_cp(),
    )(*ws, *gs, *ms, *vs)


def _adamw(w, g, m, v, name):
    rows, wd = w.shape
    br = _row_block(rows)

    def body(w_ref, g_ref, m_ref, v_ref, d_ref, nm_ref, nv_ref):
        d, m_new, v_new = _adamw_math(w_ref[...], g_ref[...], m_ref[...], v_ref[...])
        d_ref[...] = d
        nm_ref[...] = m_new
        nv_ref[...] = v_new

    spec = pl.BlockSpec((br, wd), lambda i: (i, 0))
    return pl.pallas_call(
        body, name=name, grid=(rows // br,),
        out_shape=[jax.ShapeDtypeStruct((rows, wd), F32)] * 3,
        in_specs=[spec] * 4, out_specs=[spec] * 3,
        compiler_params=_cp(("parallel",)),
    )(w, g, m, v)


def _adamw_halves(w, mine, other, m, v, cidx, name):
    rows, wd = w.shape
    h = rows // 2
    br = _row_block(h)
    nblk = h // br

    def body(c_ref, w_ref, a_ref, b_ref, m_ref, v_ref, g_ref, d_ref, nm_ref, nv_ref):
        gv = jnp.where(pl.program_id(0) == c_ref[0], a_ref[...], b_ref[...])
        d, m_new, v_new = _adamw_math(w_ref[...], gv, m_ref[...], v_ref[...])
        g_ref[...] = gv
        d_ref[...] = d
        nm_ref[...] = m_new
        nv_ref[...] = v_new

    full = pl.BlockSpec((br, wd), lambda hf, i, c: (hf * nblk + i, 0))
    half = pl.BlockSpec((br, wd), lambda hf, i, c: (i, 0))
    return pl.pallas_call(
        body, name=name,
        grid_spec=pltpu.PrefetchScalarGridSpec(
            num_scalar_prefetch=1, grid=(2, nblk),
            in_specs=[full, half, half, full, full], out_specs=[full] * 4),
        out_shape=[jax.ShapeDtypeStruct((rows, wd), F32)] * 4,
        compiler_params=_cp(("parallel", "parallel")),
    )(cidx, w, mine, other, m, v)


def _other_chips(x, y):
    return [(1 - x, y), (x, 1 - y), (1 - x, 1 - y)]


def _other_devices(x, y, c):
    flip = lambda v, d: (1 - v) if d else v
    return [(flip(x, dx), flip(y, dy), flip(c, dc))
            for dx in (0, 1) for dy in (0, 1) for dc in (0, 1) if (dx, dy, dc) != (0, 0, 0)]


def _exchange(name, ins, out_shapes, n_local, n_remote, plan):
    ni, no = len(ins), len(out_shapes)

    def body(*refs):
        in_refs, out_refs = refs[:ni], refs[ni:ni + no]
        send_sems, recv_sems, local_sems = refs[ni + no:]
        x, y, c = lax.axis_index("x"), lax.axis_index("y"), lax.axis_index("c")
        local, remote = plan(in_refs, out_refs, x, y, c)
        assert len(local) == n_local and len(remote) == n_remote

        def push(k, src, dst, dev):
            return pltpu.make_async_remote_copy(src_ref=src, dst_ref=dst, send_sem=send_sems.at[k],
                                                recv_sem=recv_sems.at[k], device_id=dev, device_id_type=MESH)

        own = [pltpu.make_async_copy(s, d, local_sems.at[i]) for i, (s, d) in enumerate(local)]
        for cp in own:
            cp.start()
        sends = [push(k, s, d, dev) for k, (s, d, dev, _) in enumerate(remote)]
        for cp in sends:
            cp.start()
        for k, (s, _, dev, landing) in enumerate(remote):
            push(k, s, landing, dev).wait_recv()
        for cp in sends:
            cp.wait_send()
        for cp in own:
            cp.wait()

    return pl.pallas_call(
        body, name=name, out_shape=out_shapes,
        in_specs=[_ANY] * ni, out_specs=[_ANY] * no,
        scratch_shapes=[pltpu.SemaphoreType.DMA((n_remote,)), pltpu.SemaphoreType.DMA((n_remote,)),
                        pltpu.SemaphoreType.DMA((max(n_local, 1),))],
        compiler_params=pltpu.CompilerParams(has_side_effects=True),
    )(*ins)


def _gather_chips(name, shards, everyone=()):
    ns, ne = len(shards), len(everyone)
    outs = [jax.ShapeDtypeStruct((4,) + a.shape, a.dtype) for a in shards]
    outs += [jax.ShapeDtypeStruct((8,) + a.shape, a.dtype) for a in everyone]

    def plan(i, o, x, y, c):
        mine, me = 2 * x + y, 4 * x + 2 * y + c
        local, remote = [], []
        for t in range(ns):
            local.append((i[t], o[t].at[mine]))
            for px, py in _other_chips(x, y):
                remote.append((i[t], o[t].at[mine], (px, py, c), o[t].at[2 * px + py]))
        for t in range(ns, ns + ne):
            local.append((i[t], o[t].at[me]))
            for px, py, pc in _other_devices(x, y, c):
                remote.append((i[t], o[t].at[me], (px, py, pc), o[t].at[4 * px + 2 * py + pc]))
        return local, remote

    return _exchange(name, list(shards) + list(everyone), outs, ns + ne, 3 * ns + 7 * ne, plan)


_HBM = pl.BlockSpec(memory_space=pltpu.HBM)
_SEM = pl.BlockSpec(memory_space=pltpu.SEMAPHORE)
_EFFECT = pltpu.SideEffectType.DATAFLOW_SIDE_EFFECTING


def _split_start(name, ins, land_shapes, n_remote, plan, after):
    ni, nl = len(ins), len(land_shapes)
    srcs = [pltpu.with_memory_space_constraint(a, pltpu.HBM) for a in ins]
    lands = [pltpu.with_memory_space_constraint(lax.empty(s.shape, s.dtype), pltpu.HBM) for s in land_shapes]

    def body(*refs):
        src, land = refs[:ni], refs[ni:ni + nl]
        first = ni + nl + 1
        send, recv = refs[first:first + n_remote], refs[first + n_remote:first + 2 * n_remote]
        token = refs[first + 2 * n_remote + ni + nl]
        x, y, c = lax.axis_index("x"), lax.axis_index("y"), lax.axis_index("c")
        remote = plan(src, land, x, y, c)
        assert len(remote) == n_remote
        for k, (s, d, dev, _) in enumerate(remote):
            pltpu.make_async_remote_copy(src_ref=s, dst_ref=d, send_sem=send[k], recv_sem=recv[k],
                                         device_id=dev, device_id_type=MESH).start()
        token[...] = jnp.zeros_like(token)

    out = pl.pallas_call(
        body, name=name + "_start",
        out_shape=[pltpu.SemaphoreType.DMA(())] * (2 * n_remote)
                  + [pltpu.HBM(a.shape, a.dtype) for a in ins] + [pltpu.HBM(s.shape, s.dtype) for s in land_shapes]
                  + [jax.ShapeDtypeStruct((8, 128), F32)],
        in_specs=[_HBM] * (ni + nl) + [_ANY], out_specs=[_SEM] * (2 * n_remote) + [_HBM] * (ni + nl) + [_VM],
        input_output_aliases={t: 2 * n_remote + t for t in range(ni + nl)},
        compiler_params=pltpu.CompilerParams(has_side_effects=_EFFECT),
    )(*srcs, *lands, after)
    sems, thru = out[:2 * n_remote], out[2 * n_remote:2 * n_remote + ni + nl]
    return (name, sems, thru[:ni], thru[ni:], n_remote, plan), out[-1]


def _split_wait(handle, after):
    name, sems, srcs, lands, n_remote, plan = handle
    ni, nl = len(srcs), len(lands)

    def body(*refs):
        src, land = refs[:ni], refs[ni:ni + nl]
        send, recv = refs[ni + nl:ni + nl + n_remote], refs[ni + nl + n_remote:ni + nl + 2 * n_remote]
        x, y, c = lax.axis_index("x"), lax.axis_index("y"), lax.axis_index("c")
        for k, (s, _, dev, landing) in enumerate(plan(src, land, x, y, c)):
            cp = pltpu.make_async_remote_copy(src_ref=s, dst_ref=landing, send_sem=send[k], recv_sem=recv[k],
                                              device_id=dev, device_id_type=MESH)
            cp.wait_send()
            cp.wait_recv()

    out = pl.pallas_call(
        body, name=name + "_wait",
        out_shape=[pltpu.HBM(a.shape, a.dtype) for a in srcs] + [pltpu.HBM(a.shape, a.dtype) for a in lands],
        in_specs=[_HBM] * (ni + nl) + [_SEM] * (2 * n_remote) + [_ANY], out_specs=[_HBM] * (ni + nl),
        input_output_aliases={t: t for t in range(ni + nl)},
        compiler_params=pltpu.CompilerParams(has_side_effects=_EFFECT),
    )(*srcs, *lands, *sems, after)
    return out[:ni], out[ni:]


def _plan_to_chips(src, land, x, y, c):
    mine = 2 * x + y
    return [(src[t], land[t].at[mine], (px, py, c), land[t].at[2 * px + py])
            for t in range(len(src)) for px, py in _other_chips(x, y)]


def _plan_swap_halves(src, land, x, y, c):
    out = []
    for t in range(len(src)):
        h = src[t].shape[1] // 2
        out.append((src[t].at[:, pl.ds(pl.multiple_of((1 - c) * h, 8), h), :], land[t], (x, y, 1 - c), land[t]))
    return out


def _plan_scatter_chips(src, land, x, y, c):
    mine = 2 * x + y
    return [(src[t].at[2 * px + py], land[t].at[mine], (px, py, c), land[t].at[2 * px + py])
            for t in range(len(src)) for px, py in _other_chips(x, y)]


def _swap_halves(gs, everyone, whole):
    ns, ne, nw = len(gs), len(everyone), len(whole)
    outs = [jax.ShapeDtypeStruct((4, g.shape[1] // 2, g.shape[2]), g.dtype) for g in gs]
    outs += [jax.ShapeDtypeStruct((8,) + a.shape, a.dtype) for a in everyone]
    outs += [jax.ShapeDtypeStruct(a.shape, a.dtype) for a in whole]

    def plan(i, o, x, y, c):
        me = 4 * x + 2 * y + c
        local, remote = [], []
        for t in range(ns):
            h = gs[t].shape[1] // 2
            theirs = i[t].at[:, pl.ds(pl.multiple_of((1 - c) * h, 8), h), :]
            remote.append((theirs, o[t], (x, y, 1 - c), o[t]))
        for t in range(ns, ns + ne):
            local.append((i[t], o[t].at[me]))
            for px, py, pc in _other_devices(x, y, c):
                remote.append((i[t], o[t].at[me], (px, py, pc), o[t].at[4 * px + 2 * py + pc]))
        for t in range(ns + ne, ns + ne + nw):
            remote.append((i[t], o[t], (x, y, 1 - c), o[t]))
        return local, remote

    return _exchange("grad_swap_sibling", list(gs) + list(everyone) + list(whole), outs, ne, ns + 7 * ne + nw, plan)


def _scatter_chips(parts, per_chip):
    ns, ng = len(parts), len(per_chip)
    outs = [jax.ShapeDtypeStruct(a.shape, a.dtype) for a in parts]
    outs += [jax.ShapeDtypeStruct((4,) + a.shape, a.dtype) for a in per_chip]

    def plan(i, o, x, y, c):
        mine = 2 * x + y
        local, remote = [], []
        for t in range(ns):
            local.append((i[t].at[mine], o[t].at[mine]))
            for px, py in _other_chips(x, y):
                remote.append((i[t].at[2 * px + py], o[t].at[mine], (px, py, c), o[t].at[2 * px + py]))
        for t in range(ns, ns + ng):
            local.append((i[t], o[t].at[mine]))
            for px, py in _other_chips(x, y):
                remote.append((i[t], o[t].at[mine], (px, py, c), o[t].at[2 * px + py]))
        return local, remote

    return _exchange("grad_scatter_chips", list(parts) + list(per_chip), outs, ns + ng, 3 * (ns + ng), plan)


def _join_halves(halves):
    ns = len(halves)
    outs = [jax.ShapeDtypeStruct(a.shape, a.dtype) for a in halves]

    def plan(i, o, x, y, c):
        return [], [(i[t], o[t], (x, y, 1 - c), o[t]) for t in range(ns)]

    return _exchange("grad_join_sibling", list(halves), outs, 0, ns, plan)


def _pad_heads_cols(w, per, used):
    k = w.shape[0]
    w = w.reshape(k, NH, per)[:, :, :used]
    return jnp.pad(w, ((0, 0), (0, 0), (0, HP - used))).reshape(k, NH * HP)


def _unpad_heads_cols(w, used):
    k = w.shape[0]
    return w.reshape(k, NH, HP)[:, :, :used]


def _prep_weights(wf):
    bf = lambda a: a.astype(BF16)
    out = {}
    out["w_in"] = jnp.pad(bf(wf["w_in"]), ((0, 0), (0, IN_PAD - IN_COLS)))
    out["w_glu"] = bf(wf["w_glu"])
    out["w_uq"] = _pad_heads_cols(bf(wf["w_uq"]), QK_NOPE + QK_ROPE, QK_NOPE + QK_ROPE)
    wkv = bf(wf["w_ukv"]).reshape(KV_LORA, NH, QK_NOPE + V_HEAD)
    wk = jnp.pad(wkv[:, :, :QK_NOPE], ((0, 0), (0, 0), (0, HP - QK_NOPE))).reshape(KV_LORA, NH * HP)
    wv = jnp.pad(wkv[:, :, QK_NOPE:], ((0, 0), (0, 0), (0, HP - V_HEAD))).reshape(KV_LORA, NH * HP)
    out["w_ukv"] = jnp.concatenate([wk, wv], axis=1)
    return out


def _prep_late_weights(wf):
    bf = lambda a: a.astype(BF16)
    out = {}
    wo = bf(wf["w_out"])
    wo_a = jnp.pad(wo[D_SSM:].reshape(NH, V_HEAD, D), ((0, 0), (0, HP - V_HEAD), (0, 0))).reshape(NH * HP, D)
    out["w_out"] = jnp.concatenate([wo[:D_SSM], wo_a], axis=0)
    out["w_ff1"] = bf(wf["w_ff1"])
    out["w_ff2"] = bf(wf["w_ff2"])
    return out


def _rope_tables(positions):
    inv_freq = ROPE_BASE ** (-jnp.arange(0, QK_ROPE, 2, dtype=F32) / QK_ROPE)
    ang = positions.astype(F32)[:, None] * inv_freq
    cos, sin = jnp.cos(ang), jnp.sin(ang)
    n = positions.shape[0]
    one = jnp.ones((n, QK_NOPE), F32)
    z16 = jnp.zeros((n, 16), F32)
    z32 = jnp.zeros((n, 32), F32)
    z64 = jnp.zeros((n, QK_NOPE), F32)
    rc = jnp.concatenate([one, cos, cos, z32], axis=1)
    rs1 = jnp.concatenate([z64, -sin, z16, z32], axis=1)
    rs2 = jnp.concatenate([z64, z16, sin, z32], axis=1)
    return rc, rs1, rs2


def _permute_rows(a, S):
    n, w = a.shape
    return a.reshape(n // S, 8, S // 8, w).transpose(0, 2, 1, 3).reshape(n, w)


def _unpermute_rows(a, S):
    n, w = a.shape
    return a.reshape(n // S, S // 8, 8, w).transpose(0, 2, 1, 3).reshape(n, w)


def _block_diag_in(bb):
    eye = jnp.eye(8, dtype=bb.dtype)
    blocks = jnp.einsum("qgph,gk->qghkp", bb.reshape(4, 8, P, H), eye).reshape(4, QB, QS)
    return blocks.transpose(1, 0, 2).reshape(QB, NST)


def _block_diag_out(cc):
    eye = jnp.eye(8, dtype=cc.dtype)
    return jnp.einsum("qghp,gk->qgpkh", cc.reshape(4, 8, H, P), eye).reshape(NST, QB)


def _slots(full):
    r, cdim = full.shape
    return full.reshape(r, 4, cdim // 4).transpose(1, 0, 2)


def _unslots(g):
    s, r, cs = g.shape
    return g.transpose(1, 0, 2).reshape(r, s * cs)


def _local_step(x, positions, target, modp, wf, late_weights=None, reducer=None):
    nb, S, _ = x.shape
    n = nb * S
    tm = min(256, S)
    tr = min(512, S)
    tt = min(512, S)
    tq = min(512, S // 2)
    kw = _prep_weights(wf)
    row = lambda a: a.reshape(1, -1).astype(F32)

    xf = x.reshape(n, D)
    tf = target.reshape(n, D)
    g1, g2, gf = row(wf["norm1_g"]), row(wf["norm2_g"]), row(wf["final_norm_g"])
    h1, proj = _f1_fwd(xf, modp, g1, kw["w_in"], S, tr)

    col = lambda a: a.reshape(NST, 1)
    lam_re, lam_im = col(wf["ssm_lambda_re"]), col(wf["ssm_lambda_im"])
    logdt = jnp.repeat(wf["ssm_log_dt"].reshape(G, 1), P, axis=1).reshape(NST, 1)
    b_re, b_im = wf["ssm_b_re"].reshape(NST, H), wf["ssm_b_im"].reshape(NST, H)
    lbr, lbi, bbr, bbi = _ssm_param_fwd(lam_re, lam_im, logdt, b_re, b_im)
    lre8 = jnp.broadcast_to(lbr.reshape(1, NST), (8, NST))
    lim8 = jnp.broadcast_to(lbi.reshape(1, NST), (8, NST))
    bm = jnp.concatenate([_block_diag_in(bbr.reshape(G, P, H)), _block_diag_in(bbi.reshape(G, P, H))],
                         axis=1).astype(BF16)
    cm = jnp.concatenate([_block_diag_out(wf["ssm_c_re"]), -_block_diag_out(wf["ssm_c_im"])], axis=0).astype(BF16)
    dvec = row(wf["ssm_d"])
    u_p = _permute_rows(proj[:, :D_SSM], S)
    fcr, fci = _ssm_local(u_p, bm, lre8, lim8, S, tt)
    st, ypre, z, gact, yssm_p = _ssm_fwd(u_p, fcr, fci, bm, cm, dvec, kw["w_glu"], lre8, lim8, S, tt)
    yssm = _unpermute_rows(yssm_p, S)

    rc, rs1, rs2 = _rope_tables(positions.reshape(n))
    gq, gkv = row(wf["q_norm_g"]), row(wf["kv_norm_g"])
    q, k, v, qn, kvn = _mla_fwd(proj, rc, rs1, rs2, gq, gkv, kw["w_uq"], kw["w_ukv"], tr)
    oattn, lrow = _attn_fwd(q, k, v, S, tq)

    gs = row(wf["ssm_out_g"])
    ga = jnp.pad(wf["attn_out_g"].reshape(NH, V_HEAD), ((0, 0), (0, HP - V_HEAD))).reshape(1, NH * HP)
    kw.update(_prep_late_weights(late_weights(oattn) if late_weights is not None else wf))
    yn, o, x1, h2 = _p1_fwd(yssm, oattn, xf, modp, gs, ga, kw["w_out"], g2, S, tr)
    dx1, r, da, dff, accs2, accg2 = _p2(x1, h2, tf, modp, g2, gf, kw["w_ff1"], kw["w_ff2"], S, tm)
    loss = accg2[2:3]
    g_ff1 = _wgrad(h2, da, "wgrad_ff1", col_slots=4)
    g_ff2 = _wgrad(r, dff, "wgrad_ff2").reshape(4, D_FF // 4, D)
    do, dyssm, dob, drow, accs3, accg3 = _p3_bwd(dx1, o, yssm, oattn, modp, gs, ga, kw["w_out"], S, tr)
    gwo = _wgrad(yn, do, "wgrad_out")
    g_out = jnp.concatenate([gwo[:D_SSM].reshape(2, D_SSM // 2, D),
                             gwo[D_SSM:].reshape(2, NH // 2 * HP, D).reshape(2, NH // 2, HP, D)[:, :, :V_HEAD]
                             .reshape(2, D_ATTN // 2, D)], axis=0)
    gq_b = gq
    if reducer is not None:
        drow = drow + reducer.start([g_ff1, g_ff2, g_out])[0, 0]

    dq, dk, dv = _attn_bwd(q, k, v, dob, lrow, drow, S, tq)
    if reducer is not None:
        gq_b = gq + reducer.middle(dq)[0, 0]
    dmla, dqb, dkvb, accm = _mla_bwd(dq, dk, dv, proj, rc, rs1, rs2, gq_b, gkv, kw["w_uq"], kw["w_ukv"], tr)

    dys_p = _permute_rows(dyssm, S)
    dy, dz, air, aii = _ssm_bwd_a(dys_p, z, ypre, kw["w_glu"], cm, lre8, lim8, S, tt)
    du_p, dcm, dbm, dd, dlr, dli = _ssm_bwd_b(dy, u_p, st, fcr, fci, air, aii, bm, cm, dvec, lre8, lim8, S, tt)
    du = _unpermute_rows(du_p, S)
    dcm = dcm.reshape(2, 4, 8, P, 8, H)
    dc_re = jnp.einsum("qgpgh->qghp", dcm[0]).reshape(G, H, P)
    dc_im = -jnp.einsum("qgpgh->qghp", dcm[1]).reshape(G, H, P)
    dbm = dbm.reshape(8, H, 2, 4, 8, P)
    dbb_re = jnp.einsum("ghqgp->qgph", dbm[:, :, 0]).reshape(NST, H)
    dbb_im = jnp.einsum("ghqgp->qgph", dbm[:, :, 1]).reshape(NST, H)
    gb_re, gb_im, glr, gli, gdt = _ssm_param_bwd(lam_re, lam_im, logdt, b_re, b_im, dlr.reshape(NST, 1),
                                                 dli.reshape(NST, 1), dbb_re, dbb_im)
    glogdt = _rowsum(gdt.reshape(G, P))

    dx, dproj, accs1, accg1 = _f1_bwd(du, dmla, dx1, xf, modp, g1, kw["w_in"], S, tr)

    big = {}
    big["w_in"] = _slots(_wgrad(h1, dproj, "wgrad_in")[:, :IN_COLS])
    big["w_glu"] = _wgrad(gact, dz, "wgrad_glu", col_slots=4)
    big["w_uq"] = _slots(_unpad_heads_cols(_wgrad(qn, dqb, "wgrad_uq"), QK_NOPE + QK_ROPE).reshape(Q_LORA, -1))
    gkvw = _wgrad(kvn, dkvb, "wgrad_ukv")
    big["w_ukv"] = _slots(jnp.concatenate([_unpad_heads_cols(gkvw[:, :NH * HP], QK_NOPE),
                                           _unpad_heads_cols(gkvw[:, NH * HP:], V_HEAD)], axis=2).reshape(KV_LORA, -1))
    big["w_out"] = g_out
    big["w_ff1"] = g_ff1
    big["w_ff2"] = g_ff2

    small = {}
    small["norm1_g"] = accg1[0:1]
    small["norm2_g"] = accg2[0:1]
    small["final_norm_g"] = accg2[1:2]
    small["ssm_out_g"] = accg3[0:1, :D_SSM]
    small["attn_out_g"] = accg3[1].reshape(NH, HP)[:, :V_HEAD].reshape(1, D_ATTN)
    small["q_norm_g"] = accm[0:1, :Q_LORA]
    small["kv_norm_g"] = accm[1:2, :KV_LORA]
    small["ssm_lambda_re"] = glr.reshape(G, P)
    small["ssm_lambda_im"] = gli.reshape(G, P)
    small["ssm_b_re"] = gb_re
    small["ssm_b_im"] = gb_im
    small["ssm_c_re"] = dc_re.reshape(G * H, P)
    small["ssm_c_im"] = dc_im.reshape(G * H, P)
    small["ssm_d"] = dd.reshape(G, H)
    small["ssm_log_dt"] = glogdt.reshape(1, G)
    return loss, dx.reshape(nb, S, D), big, small, accs1 + accs2 + accs3


def _view2d(a):
    return a.reshape(-1, a.shape[-1]) if a.ndim > 1 else a.reshape(1, -1)


def kernel(x, c, positions, ada_w, ada_b, norm1_g, w_in, ssm_lambda_re, ssm_lambda_im, ssm_b_re, ssm_b_im, ssm_c_re, ssm_c_im, ssm_d, ssm_log_dt, w_glu, q_norm_g, w_uq, kv_norm_g, w_ukv, ssm_out_g, attn_out_g, w_out, norm2_g, w_ff1, w_ff2, final_ada_w, final_ada_b, final_norm_g, loss_target, m_ada_w, m_ada_b, m_norm1_g, m_w_in, m_ssm_lambda_re, m_ssm_lambda_im, m_ssm_b_re, m_ssm_b_im, m_ssm_c_re, m_ssm_c_im, m_ssm_d, m_ssm_log_dt, m_w_glu, m_q_norm_g, m_w_uq, m_kv_norm_g, m_w_ukv, m_ssm_out_g, m_attn_out_g, m_w_out, m_norm2_g, m_w_ff1, m_w_ff2, m_final_ada_w, m_final_ada_b, m_final_norm_g, v_ada_w, v_ada_b, v_norm1_g, v_w_in, v_ssm_lambda_re, v_ssm_lambda_im, v_ssm_b_re, v_ssm_b_im, v_ssm_c_re, v_ssm_c_im, v_ssm_d, v_ssm_log_dt, v_w_glu, v_q_norm_g, v_w_uq, v_kv_norm_g, v_w_ukv, v_ssm_out_g, v_attn_out_g, v_w_out, v_norm2_g, v_w_ff1, v_w_ff2, v_final_ada_w, v_final_ada_b, v_final_norm_g):
    args = dict(locals())
    names = list(inspect.signature(kernel).parameters)
    wnames = names[3:names.index("loss_target")]
    small_names = [nm for nm in wnames if nm not in GATHERED and nm not in TP]
    reduced_names = [nm for nm in small_names if nm not in ("ada_b", "final_ada_b")]
    w = {nm: args[nm] for nm in wnames}
    m = {nm: args["m_" + nm] for nm in wnames}
    v = {nm: args["v_" + nm] for nm in wnames}
    nb = x.shape[0]
    xi, yi, ci = lax.axis_index("x"), lax.axis_index("y"), lax.axis_index("c")
    chip, me = 2 * xi + yi, 4 * xi + 2 * yi + ci

    unslot = lambda nm, g: g.reshape(-1, g.shape[-1]) if nm in ROW_SHARDED else _unslots(g)
    early = [nm for nm in GATHERED if nm not in LATE]
    got = _gather_chips("gather_weights", [_view2d(w[nm]).astype(BF16) for nm in early], [c])
    wf = {nm: unslot(nm, g) for nm, g in zip(early, got)}
    for nm in small_names:
        wf[nm] = w[nm][0] if w[nm].ndim > 1 else w[nm]
    c_all = got[len(early)].reshape(8 * nb, D)

    na, nf = ada_w.shape[-1], final_ada_w.shape[-1]
    ada_b_s = lax.dynamic_slice(ada_b, (0, chip * na), (1, na))
    fada_b_s = lax.dynamic_slice(final_ada_b.reshape(1, -1), (0, chip * nf), (1, nf))
    cond_all, modcols = _mod_fwd(c_all, ada_w[0], ada_b_s, final_ada_w, fada_b_s)
    (mod_g,) = _gather_chips("gather_mod", [modcols])
    mine = lax.dynamic_slice(mod_g, (0, me * nb, 0), (4, nb, na + nf))
    modp = jnp.concatenate([mine[:, :, :na].transpose(1, 0, 2).reshape(nb, 6, D),
                            mine[:, :, na:].transpose(1, 0, 2).reshape(nb, 2, D)], axis=1)

    own_late = [_view2d(w[nm]).astype(BF16) for nm in LATE]
    late_gather, token = _split_start("gather_late", own_late,
                                      [jax.ShapeDtypeStruct((4,) + a.shape, a.dtype) for a in own_late],
                                      3 * len(LATE), _plan_to_chips, modp)
    modp = modp + token[0, 0]

    def late_weights(after):
        sent, landed = _split_wait(late_gather, after)
        return {nm: unslot(nm, lax.dynamic_update_slice(g, own[None], (chip, 0, 0)))
                for nm, g, own in zip(LATE, landed, sent)}

    cidx = ci.astype(jnp.int32).reshape(1)
    ahead = ["w_ff1", "w_ff2", "w_out"]

    class Reducer:
        def start(self, gs):
            lands = [jax.ShapeDtypeStruct((4, g.shape[1] // 2, g.shape[2]), g.dtype) for g in gs]
            self.swap, tok = _split_start("grad_swap_ff", gs, lands, len(gs), _plan_swap_halves, modp)
            return tok

        def middle(self, after):
            gs, got = _split_wait(self.swap, after)
            sums = [_add_half(g, r, cidx, "grad_add_sibling_" + nm) for nm, g, r in zip(ahead, gs, got)]
            lands = [jax.ShapeDtypeStruct(s.shape, s.dtype) for s in sums]
            self.scatter, tok = _split_start("grad_scatter_ff", sums, lands, 3 * len(sums), _plan_scatter_chips, modp)
            return tok

        def finish(self, after):
            out = []
            for nm, s, l in zip(ahead, *_split_wait(self.scatter, after)):
                own = lax.dynamic_slice(s, (chip, 0, 0), (1,) + s.shape[1:])
                out.append(_add_chips(lax.dynamic_update_slice(l, own, (chip, 0, 0)), "grad_add_chips_" + nm))
            return out

    reducer = Reducer()
    loss_row, grad_x, big, small, dmodp = _local_step(x, positions, loss_target, modp, wf, late_weights, reducer)

    rest = [nm for nm in GATHERED if nm not in ahead]
    sizes = [small[nm].size for nm in reduced_names]
    pad = -sum(sizes) % 128
    packed = jnp.concatenate([small[nm].reshape(1, -1) for nm in reduced_names] + [jnp.zeros((1, pad), F32)],
                             axis=1).astype(BF16)
    swapped = _swap_halves([big[nm] for nm in rest], [dmodp.reshape(nb, 8 * D)], [packed, loss_row])
    chip_sums = [_add_half(big[nm], r, cidx, "grad_add_sibling_" + nm) for nm, r in zip(rest, swapped)]
    chip_small = _pair_sum(packed, swapped[len(rest) + 1], loss_row, swapped[len(rest) + 2])
    scattered = _scatter_chips(chip_sums, chip_small)
    half_of = {nm: _add_chips(r, "grad_add_chips_" + nm) for nm, r in zip(rest, scattered)}
    half_of.update(zip(ahead, reducer.finish(grad_x)))
    halves = [half_of[nm] for nm in GATHERED]
    others = _join_halves(halves)
    grads = {}
    dmod_all = swapped[len(rest)].reshape(8 * nb, 8 * D)
    small_sum, loss_sum = _sum_devices(scattered[len(rest)].reshape(4, -1), scattered[len(rest) + 1].reshape(4, -1))
    loss = jnp.sum(loss_sum)
    off = 0
    for nm, sz in zip(reduced_names, sizes):
        grads[nm] = small_sum[:, off:off + sz].reshape(small[nm].shape)
        off += sz

    dsl = jnp.concatenate([lax.dynamic_slice(dmod_all, (0, chip * na), (8 * nb, na)),
                           lax.dynamic_slice(dmod_all, (0, 6 * D + chip * nf), (8 * nb, nf))], axis=1)
    gw, gb = _mod_bwd(cond_all.T, dsl, dmod_all)
    grads["ada_w"], grads["final_ada_w"] = gw[:, :na], gw[:, na:]
    grads["ada_b"], grads["final_ada_b"] = gb[:, :6 * D], gb[:, 6 * D:]

    delta, new_m, new_v = {}, {}, {}
    for nm, mine_h, other_h in zip(GATHERED, halves, others):
        grads[nm], delta[nm], new_m[nm], new_v[nm] = _adamw_halves(
            _view2d(w[nm]), mine_h, other_h, _view2d(m[nm]), _view2d(v[nm]), cidx, "adamw_" + nm)
    for nm in TP:
        delta[nm], new_m[nm], new_v[nm] = _adamw(_view2d(w[nm]), grads[nm], _view2d(m[nm]), _view2d(v[nm]),
                                                  "adamw_" + nm)
    upd = _adamw_small([_view2d(w[nm]) for nm in small_names], [grads[nm] for nm in small_names],
                       [_view2d(m[nm]) for nm in small_names], [_view2d(v[nm]) for nm in small_names])
    k = len(small_names)
    for t, nm in enumerate(small_names):
        delta[nm], new_m[nm], new_v[nm] = upd[t], upd[k + t], upd[2 * k + t]

    outs = [grads, delta, new_m, new_v]
    return (loss, grad_x, *[d[nm].reshape(w[nm].shape) for d in outs for nm in wnames])
```

```python
import inspect
import math

import jax
import jax.numpy as jnp
from jax import lax
from jax.experimental import pallas as pl
from jax.experimental.pallas import tpu as pltpu

F32 = jnp.float32
BF16 = jnp.bfloat16

D = 1024
D_SSM = 512
G = 32
H = 16
P = 64
NST = G * P
D_ATTN = 512
NH = 8
QK_NOPE = 64
QK_ROPE = 32
V_HEAD = 64
HP = 128
Q_LORA = 384
KV_LORA = 256
IN_COLS = D_SSM + Q_LORA + KV_LORA + QK_ROPE
IN_PAD = 1280
D_FF = 4096
ROPE_BASE = 10000.0
EPS = 1e-6
ADAM_LR = 0.001
ADAM_B1 = 0.9
ADAM_B2 = 0.999
ADAM_EPS = 1e-08
ADAM_WD = 0.01
ADAM_STEP = 10
NEG = -1e30
VMEM_LIMIT = 60 << 20

MESH = pl.DeviceIdType.MESH
_VM = pl.BlockSpec(memory_space=pltpu.VMEM)
_ANY = pl.BlockSpec(memory_space=pl.ANY)

GATHERED = ["w_in", "w_glu", "w_uq", "w_ukv", "w_out", "w_ff1", "w_ff2"]
TP = ["ada_w", "final_ada_w"]
ROW_SHARDED = ("w_out", "w_ff2")
LATE = ["w_out", "w_ff1", "w_ff2"]


def _cp(sem=None, vmem=VMEM_LIMIT):
    kw = dict(vmem_limit_bytes=vmem)
    if sem is not None:
        kw["dimension_semantics"] = sem
    return pltpu.CompilerParams(**kw)


def _dot(a, b):
    return jnp.dot(a, b, preferred_element_type=F32)


def _dot_nt(a, b):
    return lax.dot_general(a, b, (((1,), (1,)), ((), ())), preferred_element_type=F32)


def _dot_tn(a, b):
    return lax.dot_general(a, b, (((0,), (0,)), ((), ())), preferred_element_type=F32)


def _rms(x, n):
    r = lax.rsqrt(jnp.sum(x * x, axis=-1, keepdims=True) * (1.0 / n) + EPS)
    return x * r, r


def _rms_bwd(dyg, xhat, r, n):
    return r * (dyg - xhat * (jnp.sum(dyg * xhat, axis=-1, keepdims=True) * (1.0 / n)))


def _sigmoid(x):
    return 1.0 / (1.0 + jnp.exp(-x))


_GK = math.sqrt(2.0 / math.pi)
_GC = 0.044715


def _gelu(y):
    t = jnp.tanh(_GK * (y + _GC * y * y * y))
    return 0.5 * y * (1.0 + t)


def _gelu_grad(y):
    t = jnp.tanh(_GK * (y + _GC * y * y * y))
    return 0.5 * (1.0 + t) + 0.5 * y * (1.0 - t * t) * _GK * (1.0 + 3.0 * _GC * y * y)


def _colsum(x):
    return jnp.sum(x, axis=0, keepdims=True)


def _roll(x, s):
    return pltpu.roll(x, s % x.shape[-1], x.ndim - 1)


def _mod_fwd(c_all, ada_w_s, ada_b_s, fada_w_s, fada_b_s):
    nseq = c_all.shape[0]
    na, nf = ada_w_s.shape[1], fada_w_s.shape[1]

    def body(c_ref, w_ref, b_ref, fw_ref, fb_ref, cond_ref, mod_ref):
        cv = c_ref[...]
        cond = cv * _sigmoid(cv)
        cond_ref[...] = cond
        cb = cond.astype(BF16)
        mod_ref[:, 0:na] = _dot(cb, w_ref[...].astype(BF16)) + b_ref[...]
        mod_ref[:, na:na + nf] = _dot(cb, fw_ref[...].astype(BF16)) + fb_ref[...]

    return pl.pallas_call(
        body, name="mod_fwd",
        out_shape=[jax.ShapeDtypeStruct((nseq, D), F32), jax.ShapeDtypeStruct((nseq, na + nf), F32)],
        in_specs=[_VM] * 5, out_specs=[_VM] * 2, compiler_params=_cp(),
    )(c_all, ada_w_s, ada_b_s, fada_w_s, fada_b_s)


def _mod_bwd(cond_t, dsl, dall):
    nseq, n = dsl.shape
    bc = 512

    def body(ct_ref, dm_ref, da_ref, gw_ref, gb_ref):
        ct = ct_ref[...]
        dm = dm_ref[...]
        acc = ct[:, 0:1] * dm[0:1, :]
        for b in range(1, nseq):
            acc = acc + ct[:, b:b + 1] * dm[b:b + 1, :]
        gw_ref[...] = acc

        @pl.when(pl.program_id(0) == 0)
        def _():
            gb_ref[...] = _colsum(da_ref[...])

    return pl.pallas_call(
        body, name="mod_bwd", grid=(n // bc,),
        out_shape=[jax.ShapeDtypeStruct((D, n), F32), jax.ShapeDtypeStruct((1, dall.shape[1]), F32)],
        in_specs=[_VM, pl.BlockSpec((nseq, bc), lambda i: (0, i)), _VM],
        out_specs=[pl.BlockSpec((D, bc), lambda i: (0, i)), pl.BlockSpec((1, dall.shape[1]), lambda i: (0, 0))],
        compiler_params=_cp(("arbitrary",)),
    )(cond_t, dsl, dall)


def _f1_fwd(x, modp, g1, w_in, S, tm):
    n = x.shape[0]
    tps = S // tm

    def body(x_ref, mod_ref, g_ref, w_ref, h_ref, proj_ref):
        xhat, _ = _rms(x_ref[...], D)
        h = (xhat * g_ref[...]) * (1.0 + mod_ref[0, 1:2, :]) + mod_ref[0, 0:1, :]
        hb = h.astype(BF16)
        h_ref[...] = hb
        proj_ref[...] = _dot(hb, w_ref[...])

    return pl.pallas_call(
        body, name="f1_fwd", grid=(n // tm,),
        out_shape=[jax.ShapeDtypeStruct((n, D), BF16), jax.ShapeDtypeStruct((n, IN_PAD), F32)],
        in_specs=[pl.BlockSpec((tm, D), lambda i: (i, 0)),
                  pl.BlockSpec((1, 8, D), lambda i: (i // tps, 0, 0)), _VM, _VM],
        out_specs=[pl.BlockSpec((tm, D), lambda i: (i, 0)), pl.BlockSpec((tm, IN_PAD), lambda i: (i, 0))],
        compiler_params=_cp(("parallel",)),
    )(x, modp, g1, w_in)


def _f1_bwd(du, dmla, dx1, x, modp, g1, w_in, S, tm):
    n = x.shape[0]
    tps = S // tm
    nb = n // S

    def body(du_ref, dm_ref, dx1_ref, x_ref, mod_ref, g_ref, w_ref, dx_ref, dproj_ref, accs_ref, accg_ref):
        i = pl.program_id(0)
        dproj = jnp.concatenate([du_ref[...], dm_ref[...]], axis=1).astype(BF16)
        dproj_ref[...] = dproj
        dh = _dot_nt(dproj, w_ref[...])
        xhat, r = _rms(x_ref[...], D)
        g = g_ref[...]
        dn = dh * (1.0 + mod_ref[0, 1:2, :])
        dx_ref[...] = dx1_ref[...] + _rms_bwd(dn * g, xhat, r, D)

        @pl.when(i % tps == 0)
        def _():
            accs_ref[...] = jnp.zeros_like(accs_ref)

        @pl.when(i == 0)
        def _():
            accg_ref[...] = jnp.zeros_like(accg_ref)

        accs_ref[0, 0:1, :] += _colsum(dh)
        accs_ref[0, 1:2, :] += _colsum(dh * (xhat * g))
        accg_ref[0:1, :] += _colsum(dn * xhat)

    return pl.pallas_call(
        body, name="f1_bwd", grid=(n // tm,),
        out_shape=[jax.ShapeDtypeStruct((n, D), F32), jax.ShapeDtypeStruct((n, IN_PAD), BF16),
                   jax.ShapeDtypeStruct((nb, 8, D), F32), jax.ShapeDtypeStruct((8, D), F32)],
        in_specs=[pl.BlockSpec((tm, D_SSM), lambda i: (i, 0)), pl.BlockSpec((tm, IN_PAD - D_SSM), lambda i: (i, 0)),
                  pl.BlockSpec((tm, D), lambda i: (i, 0)), pl.BlockSpec((tm, D), lambda i: (i, 0)),
                  pl.BlockSpec((1, 8, D), lambda i: (i // tps, 0, 0)), _VM, _VM],
        out_specs=[pl.BlockSpec((tm, D), lambda i: (i, 0)), pl.BlockSpec((tm, IN_PAD), lambda i: (i, 0)),
                   pl.BlockSpec((1, 8, D), lambda i: (i // tps, 0, 0)), pl.BlockSpec((8, D), lambda i: (0, 0))],
        compiler_params=_cp(("arbitrary",)),
    )(du, dmla, dx1, x, modp, g1, w_in)


def _ssm_param_fwd(lam_re, lam_im, logdt, b_re, b_im):
    def body(lr_ref, li_ref, ld_ref, br_ref, bi_ref, lbr_ref, lbi_ref, bbr_ref, bbi_ref):
        lr, li = lr_ref[...], li_ref[...]
        dt = jnp.exp(ld_ref[...])
        er = jnp.exp(lr * dt)
        lbr = er * jnp.cos(li * dt)
        lbi = er * jnp.sin(li * dt)
        den = 1.0 / (lr * lr + li * li)
        cr = ((lbr - 1.0) * lr + lbi * li) * den
        ci = (lbi * lr - (lbr - 1.0) * li) * den
        lbr_ref[...] = lbr
        lbi_ref[...] = lbi
        bbr_ref[...] = cr * br_ref[...] - ci * bi_ref[...]
        bbi_ref[...] = cr * bi_ref[...] + ci * br_ref[...]

    return pl.pallas_call(
        body, name="ssm_param_fwd",
        out_shape=[jax.ShapeDtypeStruct((NST, 1), F32)] * 2 + [jax.ShapeDtypeStruct((NST, H), F32)] * 2,
        in_specs=[_VM] * 5, out_specs=[_VM] * 4, compiler_params=_cp(),
    )(lam_re, lam_im, logdt, b_re, b_im)


def _ssm_param_bwd(lam_re, lam_im, logdt, b_re, b_im, dlb_re, dlb_im, dbb_re, dbb_im):
    def body(lr_ref, li_ref, ld_ref, br_ref, bi_ref, dlr_ref, dli_ref, dbr_ref, dbi_ref,
             gbr_ref, gbi_ref, glr_ref, gli_ref, gdt_ref):
        lr, li = lr_ref[...], li_ref[...]
        dt = jnp.exp(ld_ref[...])
        er = jnp.exp(lr * dt)
        lbr = er * jnp.cos(li * dt)
        lbi = er * jnp.sin(li * dt)
        den = 1.0 / (lr * lr + li * li)
        nr, ni = lbr - 1.0, lbi
        cr = (nr * lr + ni * li) * den
        ci = (ni * lr - nr * li) * den
        br, bi = br_ref[...], bi_ref[...]
        dbr, dbi = dbr_ref[...], dbi_ref[...]
        gbr_ref[...] = cr * dbr + ci * dbi
        gbi_ref[...] = cr * dbi - ci * dbr
        gcr = jnp.sum(dbr * br + dbi * bi, axis=1, keepdims=True)
        gci = jnp.sum(dbi * br - dbr * bi, axis=1, keepdims=True)
        ilr, ili = lr * den, -li * den
        glbr = dlr_ref[...] + (gcr * ilr + gci * ili)
        glbi = dli_ref[...] + (gci * ilr - gcr * ili)
        qr = -(cr * ilr - ci * ili)
        qi = -(cr * ili + ci * ilr)
        glr = gcr * qr + gci * qi
        gli = gci * qr - gcr * qi
        glr = glr + dt * (glbr * lbr + glbi * lbi)
        gli = gli + dt * (glbi * lbr - glbr * lbi)
        wr = lr * lbr - li * lbi
        wi = lr * lbi + li * lbr
        glr_ref[...] = glr
        gli_ref[...] = gli
        gdt_ref[...] = (glbr * wr + glbi * wi) * dt

    return pl.pallas_call(
        body, name="ssm_param_bwd",
        out_shape=[jax.ShapeDtypeStruct((NST, H), F32)] * 2 + [jax.ShapeDtypeStruct((NST, 1), F32)] * 3,
        in_specs=[_VM] * 9, out_specs=[_VM] * 5, compiler_params=_cp(),
    )(lam_re, lam_im, logdt, b_re, b_im, dlb_re, dlb_im, dbb_re, dbb_im)


def _rowsum(a):
    def body(a_ref, o_ref):
        o_ref[...] = jnp.sum(a_ref[...], axis=1, keepdims=True)

    return pl.pallas_call(
        body, name="rowsum", out_shape=jax.ShapeDtypeStruct((a.shape[0], 1), F32),
        in_specs=[_VM], out_specs=_VM, compiler_params=_cp(),
    )(a)


QB = D_SSM // 4
QS = 4 * QB


def _bd_lo(part, q):
    return part * NST + q * QS


def _bd_expand(ub, bm_ref, out_ref):
    for part in range(2):
        for q in range(4):
            lo = _bd_lo(part, q)
            out_ref[:, lo:lo + QS] = _dot(ub[:, q * QB:(q + 1) * QB], bm_ref[:, lo:lo + QS])


def _bd_expand_t(db, cm_ref, out_ref):
    for part in range(2):
        for q in range(4):
            lo = _bd_lo(part, q)
            out_ref[:, lo:lo + QS] = _dot_nt(db[:, q * QB:(q + 1) * QB], cm_ref[lo:lo + QS, :])


def _bd_project(sb, cm_ref):
    return jnp.concatenate(
        [_dot(sb[:, _bd_lo(0, q):_bd_lo(0, q) + QS], cm_ref[_bd_lo(0, q):_bd_lo(0, q) + QS, :])
         + _dot(sb[:, _bd_lo(1, q):_bd_lo(1, q) + QS], cm_ref[_bd_lo(1, q):_bd_lo(1, q) + QS, :])
         for q in range(4)], axis=1)


def _bd_project_t(ab, bm_ref):
    return jnp.concatenate(
        [_dot_nt(ab[:, _bd_lo(0, q):_bd_lo(0, q) + QS], bm_ref[:, _bd_lo(0, q):_bd_lo(0, q) + QS])
         + _dot_nt(ab[:, _bd_lo(1, q):_bd_lo(1, q) + QS], bm_ref[:, _bd_lo(1, q):_bd_lo(1, q) + QS])
         for q in range(4)], axis=1)


def _pow2k(pr, pi, nsq):
    for _ in range(nsq):
        pr, pi = pr * pr - pi * pi, 2.0 * pr * pi
    return pr, pi


def _ssm_local(u_p, bm, lre8, lim8, S, tt):
    n = u_p.shape[0]
    nb, nt = n // S, S // tt
    nsq = int(round(math.log2(S // 8)))
    assert 2 ** nsq == S // 8

    def body(u_ref, bm_ref, lre_ref, lim_ref, cre_ref, cim_ref, sre, sim, bu):
        j = pl.program_id(1)

        @pl.when(j == 0)
        def _():
            sre[...] = jnp.zeros_like(sre)
            sim[...] = jnp.zeros_like(sim)

        _bd_expand(u_ref[...].astype(BF16), bm_ref, bu)
        lre, lim = lre_ref[...], lim_ref[...]

        def step(i, c):
            sr, si = c
            off = pl.multiple_of(i * 8, 8)
            br = bu[pl.ds(off, 8), 0:NST]
            bi = bu[pl.ds(off, 8), NST:2 * NST]
            return lre * sr - lim * si + br, lre * si + lim * sr + bi

        sr, si = lax.fori_loop(0, tt // 8, step, (sre[...], sim[...]))
        sre[...] = sr
        sim[...] = si

        @pl.when(j == nt - 1)
        def _():
            pr, pi = _pow2k(lre[0:1], lim[0:1], nsq)
            cr = jnp.zeros((1, NST), F32)
            ci = jnp.zeros((1, NST), F32)
            cre_ref[0:1, :] = cr
            cim_ref[0:1, :] = ci
            for k in range(1, 8):
                cr, ci = sr[k - 1:k] + pr * cr - pi * ci, si[k - 1:k] + pr * ci + pi * cr
                cre_ref[k:k + 1, :] = cr
                cim_ref[k:k + 1, :] = ci

    return pl.pallas_call(
        body, name="ssm_local", grid=(nb, nt),
        out_shape=[jax.ShapeDtypeStruct((nb * 8, NST), F32)] * 2,
        in_specs=[pl.BlockSpec((tt, D_SSM), lambda b, j: (b * nt + j, 0)), _VM, _VM, _VM],
        out_specs=[pl.BlockSpec((8, NST), lambda b, j: (b, 0))] * 2,
        scratch_shapes=[pltpu.VMEM((8, NST), F32), pltpu.VMEM((8, NST), F32), pltpu.VMEM((tt, 2 * NST), F32)],
        compiler_params=_cp(("arbitrary", "arbitrary")),
    )(u_p, bm, lre8, lim8)


def _ssm_fwd(u_p, cre, cim, bm, cm, dvec, w_glu, lre8, lim8, S, tt):
    n = u_p.shape[0]
    nb, nt = n // S, S // tt

    def body(u_ref, cre_ref, cim_ref, bm_ref, cm_ref, d_ref, wg_ref, lre_ref, lim_ref,
             st_ref, ypre_ref, z_ref, gact_ref, yssm_ref, sre, sim, bu):
        j = pl.program_id(1)

        @pl.when(j == 0)
        def _():
            sre[...] = cre_ref[...]
            sim[...] = cim_ref[...]

        u = u_ref[...]
        _bd_expand(u.astype(BF16), bm_ref, bu)
        lre, lim = lre_ref[...], lim_ref[...]

        def step(i, c):
            sr, si = c
            off = pl.multiple_of(i * 8, 8)
            nr = lre * sr - lim * si + bu[pl.ds(off, 8), 0:NST]
            ni = lre * si + lim * sr + bu[pl.ds(off, 8), NST:2 * NST]
            bu[pl.ds(off, 8), 0:NST] = nr
            bu[pl.ds(off, 8), NST:2 * NST] = ni
            return nr, ni

        sr, si = lax.fori_loop(0, tt // 8, step, (sre[...], sim[...]))
        sre[...] = sr
        sim[...] = si
        stb = bu[...].astype(BF16)
        st_ref[...] = stb
        y = _bd_project(stb, cm_ref) + d_ref[...] * u
        ypre_ref[...] = y
        gb = _gelu(y).astype(BF16)
        gact_ref[...] = gb
        z = _dot(gb, wg_ref[...])
        z_ref[...] = z
        yssm_ref[...] = z[:, 0:D_SSM] * _sigmoid(z[:, D_SSM:2 * D_SSM])

    row = lambda w: pl.BlockSpec((tt, w), lambda b, j: (b * nt + j, 0))
    return pl.pallas_call(
        body, name="ssm_fwd", grid=(nb, nt),
        out_shape=[jax.ShapeDtypeStruct((n, 2 * NST), BF16), jax.ShapeDtypeStruct((n, D_SSM), F32),
                   jax.ShapeDtypeStruct((n, 2 * D_SSM), F32), jax.ShapeDtypeStruct((n, D_SSM), BF16),
                   jax.ShapeDtypeStruct((n, D_SSM), F32)],
        in_specs=[row(D_SSM), pl.BlockSpec((8, NST), lambda b, j: (b, 0)), pl.BlockSpec((8, NST), lambda b, j: (b, 0)),
                  _VM, _VM, _VM, _VM, _VM, _VM],
        out_specs=[row(2 * NST), row(D_SSM), row(2 * D_SSM), row(D_SSM), row(D_SSM)],
        scratch_shapes=[pltpu.VMEM((8, NST), F32), pltpu.VMEM((8, NST), F32), pltpu.VMEM((tt, 2 * NST), F32)],
        compiler_params=_cp(("arbitrary", "arbitrary")),
    )(u_p, cre, cim, bm, cm, dvec, w_glu, lre8, lim8)


def _ssm_bwd_a(dys_p, z, ypre, w_glu, cm, lre8, lim8, S, tt):
    n = z.shape[0]
    nb, nt = n // S, S // tt
    nsq = int(round(math.log2(S // 8)))
    ng = tt // 8

    def body(dys_ref, z_ref, y_ref, wg_ref, cm_ref, lre_ref, lim_ref, dy_ref, dz_ref, are_ref, aim_ref, sre, sim, gb):
        j = pl.program_id(1)

        @pl.when(j == 0)
        def _():
            sre[...] = jnp.zeros_like(sre)
            sim[...] = jnp.zeros_like(sim)

        z = z_ref[...]
        z1, z2 = z[:, 0:D_SSM], z[:, D_SSM:2 * D_SSM]
        sg = _sigmoid(z2)
        dys = dys_ref[...]
        dz = jnp.concatenate([dys * sg, dys * z1 * sg * (1.0 - sg)], axis=1).astype(BF16)
        dz_ref[...] = dz
        dy = _dot_nt(dz, wg_ref[...]) * _gelu_grad(y_ref[...])
        dy_ref[...] = dy
        _bd_expand_t(dy.astype(BF16), cm_ref, gb)
        lre, lim = lre_ref[...], lim_ref[...]

        def step(i, c):
            ar, ai = c
            off = pl.multiple_of((ng - 1 - i) * 8, 8)
            gr = gb[pl.ds(off, 8), 0:NST]
            gi = gb[pl.ds(off, 8), NST:2 * NST]
            return lre * ar + lim * ai + gr, lre * ai - lim * ar + gi

        ar, ai = lax.fori_loop(0, ng, step, (sre[...], sim[...]))
        sre[...] = ar
        sim[...] = ai

        @pl.when(j == nt - 1)
        def _():
            pr, pi = _pow2k(lre[0:1], -lim[0:1], nsq)
            cr = jnp.zeros((1, NST), F32)
            ci = jnp.zeros((1, NST), F32)
            are_ref[7:8, :] = cr
            aim_ref[7:8, :] = ci
            for k in range(6, -1, -1):
                cr, ci = ar[k + 1:k + 2] + pr * cr - pi * ci, ai[k + 1:k + 2] + pr * ci + pi * cr
                are_ref[k:k + 1, :] = cr
                aim_ref[k:k + 1, :] = ci

    row = lambda w: pl.BlockSpec((tt, w), lambda b, j: (b * nt + nt - 1 - j, 0))
    return pl.pallas_call(
        body, name="ssm_bwd_a", grid=(nb, nt),
        out_shape=[jax.ShapeDtypeStruct((n, D_SSM), F32), jax.ShapeDtypeStruct((n, 2 * D_SSM), BF16),
                   jax.ShapeDtypeStruct((nb * 8, NST), F32), jax.ShapeDtypeStruct((nb * 8, NST), F32)],
        in_specs=[row(D_SSM), row(2 * D_SSM), row(D_SSM), _VM, _VM, _VM, _VM],
        out_specs=[row(D_SSM), row(2 * D_SSM), pl.BlockSpec((8, NST), lambda b, j: (b, 0)),
                   pl.BlockSpec((8, NST), lambda b, j: (b, 0))],
        scratch_shapes=[pltpu.VMEM((8, NST), F32), pltpu.VMEM((8, NST), F32), pltpu.VMEM((tt, 2 * NST), F32)],
        compiler_params=_cp(("arbitrary", "arbitrary")),
    )(dys_p, z, ypre, w_glu, cm, lre8, lim8)


def _ssm_bwd_b(dy, u_p, st, fcr, fci, air, aii, bm, cm, dvec, lre8, lim8, S, tt):
    n = u_p.shape[0]
    nb, nt = n // S, S // tt
    ng = tt // 8

    def body(dy_ref, u_ref, st_ref, stp_ref, fcr_ref, fci_ref, air_ref, aii_ref, bm_ref, cm_ref, d_ref, lre_ref, lim_ref,
             du_ref, dcm_ref, dbm_ref, dd_ref, dlr_ref, dli_ref, are, aim, accr, acci, sp, ab):
        b = pl.program_id(0)
        j = pl.program_id(1)
        jt = nt - 1 - j

        @pl.when((b == 0) & (j == 0))
        def _():
            dcm_ref[...] = jnp.zeros_like(dcm_ref)
            dbm_ref[...] = jnp.zeros_like(dbm_ref)
            dd_ref[...] = jnp.zeros_like(dd_ref)
            accr[...] = jnp.zeros_like(accr)
            acci[...] = jnp.zeros_like(acci)

        @pl.when(j == 0)
        def _():
            are[...] = air_ref[...]
            aim[...] = aii_ref[...]

        sp[8:tt + 8, :] = st_ref[...].astype(F32)

        @pl.when(jt == 0)
        def _():
            sp[0:8, 0:NST] = fcr_ref[...]
            sp[0:8, NST:2 * NST] = fci_ref[...]

        @pl.when(jt != 0)
        def _():
            sp[0:8, :] = stp_ref[8:16, :].astype(F32)

        dy = dy_ref[...]
        u = u_ref[...]
        dyb = dy.astype(BF16)
        _bd_expand_t(dyb, cm_ref, ab)
        lre, lim = lre_ref[...], lim_ref[...]

        def step(i, c):
            ar, ai = c
            off = pl.multiple_of((ng - 1 - i) * 8, 8)
            nr = lre * ar + lim * ai + ab[pl.ds(off, 8), 0:NST]
            ni = lre * ai - lim * ar + ab[pl.ds(off, 8), NST:2 * NST]
            ab[pl.ds(off, 8), 0:NST] = nr
            ab[pl.ds(off, 8), NST:2 * NST] = ni
            pr = sp[pl.ds(off, 8), 0:NST]
            pi = sp[pl.ds(off, 8), NST:2 * NST]
            accr[...] += nr * pr + ni * pi
            acci[...] += ni * pr - nr * pi
            return nr, ni

        ar, ai = lax.fori_loop(0, ng, step, (are[...], aim[...]))
        are[...] = ar
        aim[...] = ai
        a_b = ab[...].astype(BF16)
        du_ref[...] = _bd_project_t(a_b, bm_ref) + d_ref[...] * dy
        ub = u.astype(BF16)
        for q in range(4):
            for part in range(2):
                lo = part * NST + q * 4 * QB
                s_q = st_ref[:, lo:lo + 4 * QB]
                dcm_ref[lo:lo + 4 * QB, :] += _dot_tn(s_q, dyb[:, q * QB:(q + 1) * QB])
                dbm_ref[:, lo:lo + 4 * QB] += _dot_tn(ub[:, q * QB:(q + 1) * QB], a_b[:, lo:lo + 4 * QB])
        dd_ref[...] += _colsum(dy * u)

        @pl.when((b == nb - 1) & (j == nt - 1))
        def _():
            dlr_ref[...] = _colsum(accr[...])
            dli_ref[...] = _colsum(acci[...])

    row = lambda w: pl.BlockSpec((tt, w), lambda b, j: (b * nt + nt - 1 - j, 0))
    seq8 = pl.BlockSpec((8, NST), lambda b, j: (b, 0))
    prev = pl.BlockSpec((16, 2 * NST), lambda b, j: (jnp.maximum((b * nt + nt - 1 - j) * (tt // 16) - 1, 0), 0))
    const = lambda shape: pl.BlockSpec(shape, lambda b, j: (0, 0))
    return pl.pallas_call(
        body, name="ssm_bwd_b", grid=(nb, nt),
        out_shape=[jax.ShapeDtypeStruct((n, D_SSM), F32), jax.ShapeDtypeStruct((2 * NST, QB), F32),
                   jax.ShapeDtypeStruct((QB, 2 * NST), F32), jax.ShapeDtypeStruct((1, D_SSM), F32),
                   jax.ShapeDtypeStruct((1, NST), F32), jax.ShapeDtypeStruct((1, NST), F32)],
        in_specs=[row(D_SSM), row(D_SSM), row(2 * NST), prev, seq8, seq8, seq8, seq8, _VM, _VM, _VM, _VM, _VM],
        out_specs=[row(D_SSM), const((2 * NST, QB)), const((QB, 2 * NST)), const((1, D_SSM)),
                   const((1, NST)), const((1, NST))],
        scratch_shapes=[pltpu.VMEM((8, NST), F32)] * 4 + [pltpu.VMEM((tt + 8, 2 * NST), F32),
                                                          pltpu.VMEM((tt, 2 * NST), F32)],
        compiler_params=_cp(("arbitrary", "arbitrary")),
    )(dy, u_p, st, st, fcr, fci, air, aii, bm, cm, dvec, lre8, lim8)


def _rope(v, c, s1, s2):
    return v * c + _roll(v, -16) * s1 + _roll(v, 16) * s2


def _rope_t(dv, c, s1, s2):
    return dv * c + _roll(dv * s1, 16) + _roll(dv * s2, -16)


def _mla_fwd(proj, rc, rs1, rs2, gq, gkv, w_uq, w_ukv, tm):
    n = proj.shape[0]

    def body(ql_ref, kvl_ref, kr_ref, c_ref, s1_ref, s2_ref, gq_ref, gkv_ref, wq_ref, wkv_ref,
             q_ref, k_ref, v_ref, qn_ref, kvn_ref):
        c, s1, s2 = c_ref[...], s1_ref[...], s2_ref[...]
        qhat, _ = _rms(ql_ref[...], Q_LORA)
        qn = (qhat * gq_ref[...]).astype(BF16)
        qn_ref[...] = qn
        q = _dot(qn, wq_ref[...])
        qr = _rope(q, jnp.tile(c, (1, NH)), jnp.tile(s1, (1, NH)), jnp.tile(s2, (1, NH)))
        q_ref[...] = (qr * _C2).astype(BF16)
        khat, _ = _rms(kvl_ref[...], KV_LORA)
        kvn = (khat * gkv_ref[...]).astype(BF16)
        kvn_ref[...] = kvn
        kv = _dot(kvn, wkv_ref[...])
        kr = _rope(_roll(kr_ref[...], 64), c, s1, s2)
        k_ref[...] = (kv[:, 0:NH * HP] + jnp.tile(kr, (1, NH))).astype(BF16)
        vv = kv[:, NH * HP:2 * NH * HP]
        lane = lax.broadcasted_iota(jnp.int32, vv.shape, 1)
        v_ref[...] = jnp.where(lane % HP == V_HEAD, 1.0, vv).astype(BF16)

    def wrapped(proj_ref, *rest):
        ql = proj_ref.at[:, D_SSM:D_SSM + Q_LORA]
        kvl = proj_ref.at[:, D_SSM + Q_LORA:D_SSM + Q_LORA + KV_LORA]
        kr = proj_ref.at[:, IN_PAD - HP:IN_PAD]
        body(ql, kvl, kr, *rest)

    row = lambda w: pl.BlockSpec((tm, w), lambda i: (i, 0))
    return pl.pallas_call(
        wrapped, name="mla_fwd", grid=(n // tm,),
        out_shape=[jax.ShapeDtypeStruct((n, NH * HP), BF16)] * 3 +
                  [jax.ShapeDtypeStruct((n, Q_LORA), BF16), jax.ShapeDtypeStruct((n, KV_LORA), BF16)],
        in_specs=[row(IN_PAD), row(HP), row(HP), row(HP), _VM, _VM, _VM, _VM],
        out_specs=[row(NH * HP)] * 3 + [row(Q_LORA), row(KV_LORA)],
        compiler_params=_cp(("parallel",)),
    )(proj, rc, rs1, rs2, gq, gkv, w_uq, w_ukv)


def _mla_bwd(dq, dk, dv, proj, rc, rs1, rs2, gq, gkv, w_uq, w_ukv, tm):
    n = proj.shape[0]

    def body(dq_ref, dk_ref, dv_ref, proj_ref, c_ref, s1_ref, s2_ref, gq_ref, gkv_ref, wq_ref, wkv_ref,
             dmla_ref, dqb_ref, dkvb_ref, acc_ref):
        i = pl.program_id(0)
        c, s1, s2 = c_ref[...], s1_ref[...], s2_ref[...]
        dqu = _rope_t(dq_ref[...] * _SCALE, jnp.tile(c, (1, NH)), jnp.tile(s1, (1, NH)),
                      jnp.tile(s2, (1, NH))).astype(BF16)
        dqb_ref[...] = dqu
        dqn = _dot_nt(dqu, wq_ref[...])
        qhat, rq = _rms(proj_ref[:, D_SSM:D_SSM + Q_LORA], Q_LORA)
        dql = _rms_bwd(dqn * gq_ref[...], qhat, rq, Q_LORA)
        dkf = dk_ref[...] * (1.0 / _LOG2E)
        dkv = jnp.concatenate([dkf.astype(BF16), dv_ref[...].astype(BF16)], axis=1)
        dkvb_ref[...] = dkv
        dkvn = _dot_nt(dkv, wkv_ref[...])
        khat, rk = _rms(proj_ref[:, D_SSM + Q_LORA:D_SSM + Q_LORA + KV_LORA], KV_LORA)
        dkvl = _rms_bwd(dkvn * gkv_ref[...], khat, rk, KV_LORA)
        dkr = dkf[:, 0:HP]
        for h in range(1, NH):
            dkr = dkr + dkf[:, h * HP:(h + 1) * HP]
        lane = lax.broadcasted_iota(jnp.int32, dkr.shape, 1)
        dkr = jnp.where((lane >= QK_NOPE) & (lane < QK_NOPE + QK_ROPE), dkr, 0.0)
        dkr = _roll(_rope_t(dkr, c, s1, s2), -64)
        dmla_ref[...] = jnp.concatenate([dql, dkvl, dkr], axis=1)

        @pl.when(i == 0)
        def _():
            acc_ref[...] = jnp.zeros_like(acc_ref)

        acc_ref[0:1, 0:Q_LORA] += _colsum(dqn * qhat)
        acc_ref[1:2, 0:KV_LORA] += _colsum(dkvn * khat)

    row = lambda w: pl.BlockSpec((tm, w), lambda i: (i, 0))
    return pl.pallas_call(
        body, name="mla_bwd", grid=(n // tm,),
        out_shape=[jax.ShapeDtypeStruct((n, IN_PAD - D_SSM), F32), jax.ShapeDtypeStruct((n, NH * HP), BF16),
                   jax.ShapeDtypeStruct((n, 2 * NH * HP), BF16), jax.ShapeDtypeStruct((8, Q_LORA), F32)],
        in_specs=[row(NH * HP)] * 3 + [row(IN_PAD), row(HP), row(HP), row(HP), _VM, _VM, _VM, _VM],
        out_specs=[row(IN_PAD - D_SSM), row(NH * HP), row(2 * NH * HP), pl.BlockSpec((8, Q_LORA), lambda i: (0, 0))],
        compiler_params=_cp(("arbitrary",)),
    )(dq, dk, dv, proj, rc, rs1, rs2, gq, gkv, w_uq, w_ukv)


_SCALE = (QK_NOPE + QK_ROPE) ** -0.5
_LOG2E = 1.4426950408889634
_C2 = _SCALE * _LOG2E


def _attn_fwd(q, k, v, S, tq):
    n = q.shape[0]
    nb, nq = n // S, S // tq

    def body(q_ref, k_ref, v_ref, o_ref, lr_ref):
        qi = pl.program_id(2)
        qv = q_ref[...]

        def tile(j, c, diagonal):
            m, acc = c
            off = pl.multiple_of(j * tq, tq)
            s = _dot_nt(qv, k_ref[pl.ds(off, tq), :])
            if diagonal:
                rows = lax.broadcasted_iota(jnp.int32, s.shape, 0)
                cols = lax.broadcasted_iota(jnp.int32, s.shape, 1)
                s = jnp.where(cols <= rows, s, NEG)
            mn = jnp.maximum(m, jnp.max(s, axis=1, keepdims=True))
            p = jnp.exp2(s - mn)
            acc = jnp.exp2(m - mn) * acc + _dot(p.astype(BF16), v_ref[pl.ds(off, tq), :])
            return mn, acc

        init = (jnp.full((tq, 1), NEG, F32), jnp.zeros((tq, HP), F32))
        c = lax.fori_loop(0, qi, lambda j, c: tile(j, c, False), init)
        m, acc = tile(qi, c, True)
        l = acc[:, V_HEAD:V_HEAD + 1]
        vlane = lax.broadcasted_iota(jnp.int32, acc.shape, 1)
        o_ref[...] = jnp.where(vlane < V_HEAD, acc / l, 0.0).astype(BF16)
        lane = lax.broadcasted_iota(jnp.int32, (8, HP), 1)
        lse = jnp.broadcast_to(m + jnp.log(l) * _LOG2E, (tq, HP))
        lr_ref[...] = _rows_of(lse, jnp.where(lane == 0, 1.0, 0.0).astype(BF16))

    qs = pl.BlockSpec((tq, HP), lambda b, h, i: (b * nq + i, h))
    ks = pl.BlockSpec((S, HP), lambda b, h, i: (b, h))
    return pl.pallas_call(
        body, name="attn_fwd", grid=(nb, NH, nq),
        out_shape=[jax.ShapeDtypeStruct((n, NH * HP), BF16), jax.ShapeDtypeStruct((nb * NH * 8, S), F32)],
        in_specs=[qs, ks, ks], out_specs=[qs, pl.BlockSpec((8, tq), lambda b, h, i: (b * NH + h, i))],
        compiler_params=_cp(("parallel", "parallel", "arbitrary")),
    )(q, k, v)


def _rows_of(x, pick):
    x1 = x.astype(BF16)
    r1 = x - x1.astype(F32)
    x2 = r1.astype(BF16)
    x3 = (r1 - x2.astype(F32)).astype(BF16)
    return _dot_nt(pick, x1) + _dot_nt(pick, x2) + _dot_nt(pick, x3)


def _attn_bwd(q, k, v, dob, lrow, drow, S, tq):
    n = q.shape[0]
    nb, nq = n // S, S // tq

    def body(q_ref, k_ref, v_ref, do_ref, lr_ref, dr_ref, dqo_ref, dk_ref, dv_ref, dq_ref):
        kj = pl.program_id(2)

        @pl.when(kj == 0)
        def _():
            dq_ref[...] = jnp.zeros_like(dq_ref)

        kt = k_ref[...]
        vt = v_ref[...]

        def tile(i, c, diagonal):
            dk, dv = c
            off = pl.multiple_of(i * tq, tq)
            qv = q_ref[pl.ds(off, tq), :]
            dob = do_ref[pl.ds(off, tq), :]
            lr = lr_ref[0:1, pl.ds(off, tq)]
            dr = dr_ref[0:1, pl.ds(off, tq)]
            st = _dot_nt(kt, qv)
            dpt = _dot_nt(vt, dob)
            pt = jnp.exp2(st - lr)
            if diagonal:
                keys = lax.broadcasted_iota(jnp.int32, pt.shape, 0)
                qrys = lax.broadcasted_iota(jnp.int32, pt.shape, 1)
                pt = jnp.where(keys <= qrys, pt, 0.0)
            dst = (pt * (dpt - dr)).astype(BF16)
            dq_ref[pl.ds(off, tq), :] += _dot_tn(dst, kt)
            return dk + _dot(dst, qv), dv + _dot(pt.astype(BF16), dob)

        zero = jnp.zeros((tq, HP), F32)
        c = tile(kj, (zero, zero), True)
        dk, dv = lax.fori_loop(kj + 1, nq, lambda i, c: tile(i, c, False), c)
        dk_ref[...] = dk.astype(BF16)
        dv_ref[...] = dv.astype(BF16)

        @pl.when(kj == nq - 1)
        def _():
            dqo_ref[...] = dq_ref[...].astype(BF16)

    ts = pl.BlockSpec((tq, HP), lambda b, h, i: (b * nq + i, h))
    fs = pl.BlockSpec((S, HP), lambda b, h, i: (b, h))
    rs = pl.BlockSpec((8, S), lambda b, h, i: (b * NH + h, 0))
    return pl.pallas_call(
        body, name="attn_bwd", grid=(nb, NH, nq),
        out_shape=[jax.ShapeDtypeStruct((n, NH * HP), BF16)] * 3,
        in_specs=[fs, ts, ts, fs, rs, rs], out_specs=[fs, ts, ts],
        scratch_shapes=[pltpu.VMEM((S, HP), F32)],
        compiler_params=_cp(("parallel", "parallel", "arbitrary")),
    )(q, k, v, dob, lrow, drow)


def _p1_fwd(yssm, oattn, x, modp, gs, ga, w_out, g2, S, tm):
    n = x.shape[0]
    tps = S // tm

    def body(ys_ref, oa_ref, x_ref, mod_ref, gs_ref, ga_ref, w_ref, g2_ref, yn_ref, o_ref, x1_ref, h2_ref):
        yh, _ = _rms(ys_ref[...], D_SSM)
        ah, _ = _rms(oa_ref[...].astype(F32), D_ATTN)
        yn = jnp.concatenate([yh * gs_ref[...], ah * ga_ref[...]], axis=1).astype(BF16)
        yn_ref[...] = yn
        o = _dot(yn, w_ref[...])
        o_ref[...] = o.astype(BF16)
        x1 = x_ref[...] + mod_ref[0, 2:3, :] * o
        x1_ref[...] = x1
        xh, _ = _rms(x1, D)
        h2_ref[...] = ((xh * g2_ref[...]) * (1.0 + mod_ref[0, 4:5, :]) + mod_ref[0, 3:4, :]).astype(BF16)

    row = lambda w: pl.BlockSpec((tm, w), lambda i: (i, 0))
    return pl.pallas_call(
        body, name="p1_fwd", grid=(n // tm,),
        out_shape=[jax.ShapeDtypeStruct((n, D_SSM + NH * HP), BF16), jax.ShapeDtypeStruct((n, D), BF16),
                   jax.ShapeDtypeStruct((n, D), F32), jax.ShapeDtypeStruct((n, D), BF16)],
        in_specs=[row(D_SSM), row(NH * HP), row(D), pl.BlockSpec((1, 8, D), lambda i: (i // tps, 0, 0)),
                  _VM, _VM, _VM, _VM],
        out_specs=[row(D_SSM + NH * HP), row(D), row(D), row(D)],
        compiler_params=_cp(("parallel",)),
    )(yssm, oattn, x, modp, gs, ga, w_out, g2)


def _p2(x1, h2, target, modp, g2, gf, w_ff1, w_ff2, S, tm):
    n = x1.shape[0]
    tps = S // tm
    nb = n // S

    def body(x1_ref, h2_ref, t_ref, mod_ref, g2_ref, gf_ref, w1_ref, w2_ref,
             dx1_ref, r_ref, da_ref, dff_ref, accs_ref, accg_ref):
        i = pl.program_id(0)
        sh2, sc2, gt2 = mod_ref[0, 3:4, :], mod_ref[0, 4:5, :], mod_ref[0, 5:6, :]
        fsh, fsc = mod_ref[0, 6:7, :], mod_ref[0, 7:8, :]
        x1 = x1_ref[...]
        a = _dot(h2_ref[...], w1_ref[...])
        ra = jnp.maximum(a, 0.0)
        rb = (ra * ra).astype(BF16)
        r_ref[...] = rb
        ff = _dot(rb, w2_ref[...])
        x2 = x1 + gt2 * ff
        x2h, rf = _rms(x2, D)
        gf_v = gf_ref[...]
        outn = x2h * gf_v
        err = outn * (1.0 + fsc) + fsh - t_ref[...]
        dout = err * (1.0 / D)
        doutn = dout * (1.0 + fsc)
        dx2 = _rms_bwd(doutn * gf_v, x2h, rf, D)
        dff = (gt2 * dx2).astype(BF16)
        dff_ref[...] = dff
        dr = _dot_nt(dff, w2_ref[...])
        da = (dr * (2.0 * ra)).astype(BF16)
        da_ref[...] = da
        dh2 = _dot_nt(da, w1_ref[...])
        x1h, r2 = _rms(x1, D)
        g2_v = g2_ref[...]
        dn2 = dh2 * (1.0 + sc2)
        dx1_ref[...] = dx2 + _rms_bwd(dn2 * g2_v, x1h, r2, D)

        @pl.when(i % tps == 0)
        def _():
            accs_ref[...] = jnp.zeros_like(accs_ref)

        @pl.when(i == 0)
        def _():
            accg_ref[...] = jnp.zeros_like(accg_ref)

        accs_ref[0, 3:4, :] += _colsum(dh2)
        accs_ref[0, 4:5, :] += _colsum(dh2 * (x1h * g2_v))
        accs_ref[0, 5:6, :] += _colsum(dx2 * ff)
        accs_ref[0, 6:7, :] += _colsum(dout)
        accs_ref[0, 7:8, :] += _colsum(dout * outn)
        accg_ref[0:1, :] += _colsum(dn2 * x1h)
        accg_ref[1:2, :] += _colsum(doutn * x2h)
        accg_ref[2:3, :] += _colsum(err * err) * (0.5 / D)

    row = lambda w: pl.BlockSpec((tm, w), lambda i: (i, 0))
    return pl.pallas_call(
        body, name="p2_mlp_loss", grid=(n // tm,),
        out_shape=[jax.ShapeDtypeStruct((n, D), F32), jax.ShapeDtypeStruct((n, D_FF), BF16),
                   jax.ShapeDtypeStruct((n, D_FF), BF16), jax.ShapeDtypeStruct((n, D), BF16),
                   jax.ShapeDtypeStruct((nb, 8, D), F32), jax.ShapeDtypeStruct((8, D), F32)],
        in_specs=[row(D), row(D), row(D), pl.BlockSpec((1, 8, D), lambda i: (i // tps, 0, 0)), _VM, _VM, _VM, _VM],
        out_specs=[row(D), row(D_FF), row(D_FF), row(D), pl.BlockSpec((1, 8, D), lambda i: (i // tps, 0, 0)),
                   pl.BlockSpec((8, D), lambda i: (0, 0))],
        compiler_params=_cp(("arbitrary",)),
    )(x1, h2, target, modp, g2, gf, w_ff1, w_ff2)


def _p3_bwd(dx1, o, yssm, oattn, modp, gs, ga, w_out, S, tm):
    n = dx1.shape[0]
    tps = S // tm
    nb = n // S

    def body(dx1_ref, o_ref, ys_ref, oa_ref, mod_ref, gs_ref, ga_ref, w_ref,
             do_ref, dys_ref, doa_ref, dr_ref, accs_ref, accg_ref):
        i = pl.program_id(0)
        dx1 = dx1_ref[...]
        dob = (mod_ref[0, 2:3, :] * dx1).astype(BF16)
        do_ref[...] = dob
        dyn = _dot_nt(dob, w_ref[...])
        yh, rs = _rms(ys_ref[...], D_SSM)
        oa = oa_ref[...].astype(F32)
        ah, ra = _rms(oa, D_ATTN)
        d1 = dyn[:, 0:D_SSM]
        d2 = dyn[:, D_SSM:D_SSM + NH * HP]
        dys_ref[...] = _rms_bwd(d1 * gs_ref[...], yh, rs, D_SSM)
        doa = _rms_bwd(d2 * ga_ref[...], ah, ra, D_ATTN)
        doa_ref[...] = doa.astype(BF16)
        prod = doa * oa
        ones = jnp.ones((8, HP), BF16)
        for h in range(NH):
            dr_ref[h * 8:(h + 1) * 8, :] = _rows_of(prod[:, h * HP:(h + 1) * HP], ones)

        @pl.when(i % tps == 0)
        def _():
            accs_ref[...] = jnp.zeros_like(accs_ref)

        @pl.when(i == 0)
        def _():
            accg_ref[...] = jnp.zeros_like(accg_ref)

        accs_ref[0, 2:3, :] += _colsum(dx1 * o_ref[...])
        accg_ref[0:1, 0:D_SSM] += _colsum(d1 * yh)
        accg_ref[1:2, :] += _colsum(d2 * ah)

    row = lambda w: pl.BlockSpec((tm, w), lambda i: (i, 0))
    return pl.pallas_call(
        body, name="p3_bwd", grid=(n // tm,),
        out_shape=[jax.ShapeDtypeStruct((n, D), BF16), jax.ShapeDtypeStruct((n, D_SSM), F32),
                   jax.ShapeDtypeStruct((n, NH * HP), BF16), jax.ShapeDtypeStruct((nb * NH * 8, S), F32),
                   jax.ShapeDtypeStruct((nb, 8, D), F32), jax.ShapeDtypeStruct((8, NH * HP), F32)],
        in_specs=[row(D), row(D), row(D_SSM), row(NH * HP), pl.BlockSpec((1, 8, D), lambda i: (i // tps, 0, 0)),
                  _VM, _VM, _VM],
        out_specs=[row(D), row(D_SSM), row(NH * HP), pl.BlockSpec((NH * 8, tm), lambda i: (i // tps, i % tps)),
                   pl.BlockSpec((1, 8, D), lambda i: (i // tps, 0, 0)), pl.BlockSpec((8, NH * HP), lambda i: (0, 0))],
        compiler_params=_cp(("arbitrary",)),
    )(dx1, o, yssm, oattn, modp, gs, ga, w_out)


def _wgrad(a, b, name, col_slots=0):
    n, k1 = a.shape
    k2 = b.shape[1]
    bn = next((b for b in (1024, 512) if n % b == 0), n)
    bk1 = next((b for b in (1024, 512) if k1 % b == 0), k1)
    bk2 = k2 // col_slots if col_slots else (1024 if (k2 % 1024 == 0) else k2)

    def body(a_ref, b_ref, o_ref):
        @pl.when(pl.program_id(2) == 0)
        def _():
            o_ref[...] = jnp.zeros_like(o_ref)

        o_ref[...] += _dot_tn(a_ref[...], b_ref[...]).reshape(o_ref.shape)

    if col_slots:
        out_shape = jax.ShapeDtypeStruct((col_slots, k1, bk2), F32)
        out_spec = pl.BlockSpec((1, bk1, bk2), lambda i, j, t: (j, i, 0))
    else:
        out_shape = jax.ShapeDtypeStruct((k1, k2), F32)
        out_spec = pl.BlockSpec((bk1, bk2), lambda i, j, t: (i, j))
    return pl.pallas_call(
        body, name=name, grid=(k1 // bk1, k2 // bk2, n // bn),
        out_shape=out_shape,
        in_specs=[pl.BlockSpec((bn, bk1), lambda i, j, t: (t, i)), pl.BlockSpec((bn, bk2), lambda i, j, t: (t, j))],
        out_specs=out_spec,
        compiler_params=_cp(("parallel", "parallel", "arbitrary")),
    )(a, b)


def _row_block(rows):
    if rows <= 256:
        return rows
    return next(b for b in (256, 192, 128, 64, 32, 16, 8) if rows % b == 0)


def _add_half(g, recv, cidx, name):
    _, rows2, w = g.shape
    rows = rows2 // 2
    br = _row_block(rows)
    nblk = rows // br

    def body(c_ref, g_ref, r_ref, o_ref):
        o_ref[...] = (g_ref[...] + r_ref[...]).astype(BF16)

    return pl.pallas_call(
        body, name=name,
        grid_spec=pltpu.PrefetchScalarGridSpec(
            num_scalar_prefetch=1, grid=(4, nblk),
            in_specs=[pl.BlockSpec((1, br, w), lambda s, i, c: (s, c[0] * nblk + i, 0)),
                      pl.BlockSpec((1, br, w), lambda s, i, c: (s, i, 0))],
            out_specs=pl.BlockSpec((1, br, w), lambda s, i, c: (s, i, 0))),
        out_shape=jax.ShapeDtypeStruct((4, rows, w), BF16),
        compiler_params=_cp(("parallel", "parallel")),
    )(cidx, g, recv)


def _add_chips(r, name):
    _, rows, w = r.shape
    br = _row_block(rows)

    def body(r_ref, o_ref):
        f = lambda k: r_ref[k].astype(F32)
        o_ref[...] = ((f(0) + f(1)) + f(2)) + f(3)

    return pl.pallas_call(
        body, name=name, grid=(rows // br,),
        out_shape=jax.ShapeDtypeStruct((rows, w), F32),
        in_specs=[pl.BlockSpec((4, br, w), lambda i: (0, i, 0))],
        out_specs=pl.BlockSpec((br, w), lambda i: (i, 0)),
        compiler_params=_cp(("parallel",)),
    )(r)


def _pair_sum(a, sa, b, sb):
    def body(a_ref, sa_ref, b_ref, sb_ref, oa_ref, ob_ref):
        oa_ref[...] = (a_ref[...].astype(F32) + sa_ref[...].astype(F32)).astype(BF16)
        ob_ref[...] = b_ref[...] + sb_ref[...]

    return pl.pallas_call(
        body, name="small_grad_pair_sum",
        out_shape=[jax.ShapeDtypeStruct(a.shape, BF16), jax.ShapeDtypeStruct(b.shape, F32)],
        in_specs=[_VM] * 4, out_specs=[_VM, _VM], compiler_params=_cp(),
    )(a, sa, b, sb)


def _sum_devices(a, b):
    def body(a_ref, b_ref, oa_ref, ob_ref):
        acc = a_ref[0:1, :].astype(F32)
        accb = b_ref[0:1, :]
        for k in range(1, a.shape[0]):
            acc = acc + a_ref[k:k + 1, :].astype(F32)
            accb = accb + b_ref[k:k + 1, :]
        oa_ref[...] = acc
        ob_ref[...] = accb

    return pl.pallas_call(
        body, name="small_grad_sum",
        out_shape=[jax.ShapeDtypeStruct((1, a.shape[1]), F32), jax.ShapeDtypeStruct((1, b.shape[1]), F32)],
        in_specs=[_VM, _VM], out_specs=[_VM, _VM], compiler_params=_cp(),
    )(a, b)


def _adamw_math(wv, gv, mv, vv):
    m_new = ADAM_B1 * mv + (1.0 - ADAM_B1) * gv
    v_new = ADAM_B2 * vv + (1.0 - ADAM_B2) * (gv * gv)
    m_hat = m_new / (1.0 - ADAM_B1 ** ADAM_STEP)
    v_hat = v_new / (1.0 - ADAM_B2 ** ADAM_STEP)
    return -ADAM_LR * (m_hat / (jnp.sqrt(v_hat) + ADAM_EPS) + ADAM_WD * wv), m_new, v_new


def _adamw_small(ws, gs, ms, vs):
    k = len(ws)

    def body(*refs):
        ins, outs = refs[:4 * k], refs[4 * k:]
        for t in range(k):
            d, m_new, v_new = _adamw_math(ins[t][...], ins[k + t][...], ins[2 * k + t][...], ins[3 * k + t][...])
            outs[t][...] = d
            outs[k + t][...] = m_new
            outs[2 * k + t][...] = v_new

    shapes = [jax.ShapeDtypeStruct(w.shape, F32) for w in ws]
    return pl.pallas_call(
        body, name="adamw_small", out_shape=shapes * 3,
        in_specs=[_VM] * (4 * k), out_specs=[_VM] * (3 * k), compiler_params=_cp(),
    )(*ws, *gs, *ms, *vs)


def _adamw(w, g, m, v, name):
    rows, wd = w.shape
    br = _row_block(rows)

    def body(w_ref, g_ref, m_ref, v_ref, d_ref, nm_ref, nv_ref):
        d, m_new, v_new = _adamw_math(w_ref[...], g_ref[...], m_ref[...], v_ref[...])
        d_ref[...] = d
        nm_ref[...] = m_new
        nv_ref[...] = v_new

    spec = pl.BlockSpec((br, wd), lambda i: (i, 0))
    return pl.pallas_call(
        body, name=name, grid=(rows // br,),
        out_shape=[jax.ShapeDtypeStruct((rows, wd), F32)] * 3,
        in_specs=[spec] * 4, out_specs=[spec] * 3,
        compiler_params=_cp(("parallel",)),
    )(w, g, m, v)


def _adamw_halves(w, mine, other, m, v, cidx, name):
    rows, wd = w.shape
    h = rows // 2
    br = _row_block(h)
    nblk = h // br

    def body(c_ref, w_ref, a_ref, b_ref, m_ref, v_ref, g_ref, d_ref, nm_ref, nv_ref):
        gv = jnp.where(pl.program_id(0) == c_ref[0], a_ref[...], b_ref[...])
        d, m_new, v_new = _adamw_math(w_ref[...], gv, m_ref[...], v_ref[...])
        g_ref[...] = gv
        d_ref[...] = d
        nm_ref[...] = m_new
        nv_ref[...] = v_new

    full = pl.BlockSpec((br, wd), lambda hf, i, c: (hf * nblk + i, 0))
    half = pl.BlockSpec((br, wd), lambda hf, i, c: (i, 0))
    return pl.pallas_call(
        body, name=name,
        grid_spec=pltpu.PrefetchScalarGridSpec(
            num_scalar_prefetch=1, grid=(2, nblk),
            in_specs=[full, half, half, full, full], out_specs=[full] * 4),
        out_shape=[jax.ShapeDtypeStruct((rows, wd), F32)] * 4,
        compiler_params=_cp(("parallel", "parallel")),
    )(cidx, w, mine, other, m, v)


def _other_chips(x, y):
    return [(1 - x, y), (x, 1 - y), (1 - x, 1 - y)]


def _other_devices(x, y, c):
    flip = lambda v, d: (1 - v) if d else v
    return [(flip(x, dx), flip(y, dy), flip(c, dc))
            for dx in (0, 1) for dy in (0, 1) for dc in (0, 1) if (dx, dy, dc) != (0, 0, 0)]


def _exchange(name, ins, out_shapes, n_local, n_remote, plan):
    ni, no = len(ins), len(out_shapes)

    def body(*refs):
        in_refs, out_refs = refs[:ni], refs[ni:ni + no]
        send_sems, recv_sems, local_sems = refs[ni + no:]
        x, y, c = lax.axis_index("x"), lax.axis_index("y"), lax.axis_index("c")
        local, remote = plan(in_refs, out_refs, x, y, c)
        assert len(local) == n_local and len(remote) == n_remote

        def push(k, src, dst, dev):
            return pltpu.make_async_remote_copy(src_ref=src, dst_ref=dst, send_sem=send_sems.at[k],
                                                recv_sem=recv_sems.at[k], device_id=dev, device_id_type=MESH)

        own = [pltpu.make_async_copy(s, d, local_sems.at[i]) for i, (s, d) in enumerate(local)]
        for cp in own:
            cp.start()
        sends = [push(k, s, d, dev) for k, (s, d, dev, _) in enumerate(remote)]
        for cp in sends:
            cp.start()
        for k, (s, _, dev, landing) in enumerate(remote):
            push(k, s, landing, dev).wait_recv()
        for cp in sends:
            cp.wait_send()
        for cp in own:
            cp.wait()

    return pl.pallas_call(
        body, name=name, out_shape=out_shapes,
        in_specs=[_ANY] * ni, out_specs=[_ANY] * no,
        scratch_shapes=[pltpu.SemaphoreType.DMA((n_remote,)), pltpu.SemaphoreType.DMA((n_remote,)),
                        pltpu.SemaphoreType.DMA((max(n_local, 1),))],
        compiler_params=pltpu.CompilerParams(has_side_effects=True),
    )(*ins)


def _gather_chips(name, shards, everyone=()):
    ns, ne = len(shards), len(everyone)
    outs = [jax.ShapeDtypeStruct((4,) + a.shape, a.dtype) for a in shards]
    outs += [jax.ShapeDtypeStruct((8,) + a.shape, a.dtype) for a in everyone]

    def plan(i, o, x, y, c):
        mine, me = 2 * x + y, 4 * x + 2 * y + c
        local, remote = [], []
        for t in range(ns):
            local.append((i[t], o[t].at[mine]))
            for px, py in _other_chips(x, y):
                remote.append((i[t], o[t].at[mine], (px, py, c), o[t].at[2 * px + py]))
        for t in range(ns, ns + ne):
            local.append((i[t], o[t].at[me]))
            for px, py, pc in _other_devices(x, y, c):
                remote.append((i[t], o[t].at[me], (px, py, pc), o[t].at[4 * px + 2 * py + pc]))
        return local, remote

    return _exchange(name, list(shards) + list(everyone), outs, ns + ne, 3 * ns + 7 * ne, plan)


_HBM = pl.BlockSpec(memory_space=pltpu.HBM)
_SEM = pl.BlockSpec(memory_space=pltpu.SEMAPHORE)
_EFFECT = pltpu.SideEffectType.DATAFLOW_SIDE_EFFECTING


def _split_start(name, ins, land_shapes, n_remote, plan, after):
    ni, nl = len(ins), len(land_shapes)
    srcs = [pltpu.with_memory_space_constraint(a, pltpu.HBM) for a in ins]
    lands = [pltpu.with_memory_space_constraint(lax.empty(s.shape, s.dtype), pltpu.HBM) for s in land_shapes]

    def body(*refs):
        src, land = refs[:ni], refs[ni:ni + nl]
        first = ni + nl + 1
        send, recv = refs[first:first + n_remote], refs[first + n_remote:first + 2 * n_remote]
        token = refs[first + 2 * n_remote + ni + nl]
        x, y, c = lax.axis_index("x"), lax.axis_index("y"), lax.axis_index("c")
        remote = plan(src, land, x, y, c)
        assert len(remote) == n_remote
        for k, (s, d, dev, _) in enumerate(remote):
            pltpu.make_async_remote_copy(src_ref=s, dst_ref=d, send_sem=send[k], recv_sem=recv[k],
                                         device_id=dev, device_id_type=MESH).start()
        token[...] = jnp.zeros_like(token)

    out = pl.pallas_call(
        body, name=name + "_start",
        out_shape=[pltpu.SemaphoreType.DMA(())] * (2 * n_remote)
                  + [pltpu.HBM(a.shape, a.dtype) for a in ins] + [pltpu.HBM(s.shape, s.dtype) for s in land_shapes]
                  + [jax.ShapeDtypeStruct((8, 128), F32)],
        in_specs=[_HBM] * (ni + nl) + [_ANY], out_specs=[_SEM] * (2 * n_remote) + [_HBM] * (ni + nl) + [_VM],
        input_output_aliases={t: 2 * n_remote + t for t in range(ni + nl)},
        compiler_params=pltpu.CompilerParams(has_side_effects=_EFFECT),
    )(*srcs, *lands, after)
    sems, thru = out[:2 * n_remote], out[2 * n_remote:2 * n_remote + ni + nl]
    return (name, sems, thru[:ni], thru[ni:], n_remote, plan), out[-1]


def _split_wait(handle, after):
    name, sems, srcs, lands, n_remote, plan = handle
    ni, nl = len(srcs), len(lands)

    def body(*refs):
        src, land = refs[:ni], refs[ni:ni + nl]
        send, recv = refs[ni + nl:ni + nl + n_remote], refs[ni + nl + n_remote:ni + nl + 2 * n_remote]
        x, y, c = lax.axis_index("x"), lax.axis_index("y"), lax.axis_index("c")
        for k, (s, _, dev, landing) in enumerate(plan(src, land, x, y, c)):
            cp = pltpu.make_async_remote_copy(src_ref=s, dst_ref=landing, send_sem=send[k], recv_sem=recv[k],
                                              device_id=dev, device_id_type=MESH)
            cp.wait_send()
            cp.wait_recv()

    out = pl.pallas_call(
        body, name=name + "_wait",
        out_shape=[pltpu.HBM(a.shape, a.dtype) for a in srcs] + [pltpu.HBM(a.shape, a.dtype) for a in lands],
        in_specs=[_HBM] * (ni + nl) + [_SEM] * (2 * n_remote) + [_ANY], out_specs=[_HBM] * (ni + nl),
        input_output_aliases={t: t for t in range(ni + nl)},
        compiler_params=pltpu.CompilerParams(has_side_effects=_EFFECT),
    )(*srcs, *lands, *sems, after)
    return out[:ni], out[ni:]


def _plan_to_chips(src, land, x, y, c):
    mine = 2 * x + y
    return [(src[t], land[t].at[mine], (px, py, c), land[t].at[2 * px + py])
            for t in range(len(src)) for px, py in _other_chips(x, y)]


def _plan_swap_halves(src, land, x, y, c):
    out = []
    for t in range(len(src)):
        h = src[t].shape[1] // 2
        out.append((src[t].at[:, pl.ds(pl.multiple_of((1 - c) * h, 8), h), :], land[t], (x, y, 1 - c), land[t]))
    return out


def _plan_scatter_chips(src, land, x, y, c):
    mine = 2 * x + y
    return [(src[t].at[2 * px + py], land[t].at[mine], (px, py, c), land[t].at[2 * px + py])
            for t in range(len(src)) for px, py in _other_chips(x, y)]


def _swap_halves(gs, everyone, whole):
    ns, ne, nw = len(gs), len(everyone), len(whole)
    outs = [jax.ShapeDtypeStruct((4, g.shape[1] // 2, g.shape[2]), g.dtype) for g in gs]
    outs += [jax.ShapeDtypeStruct((8,) + a.shape, a.dtype) for a in everyone]
    outs += [jax.ShapeDtypeStruct(a.shape, a.dtype) for a in whole]

    def plan(i, o, x, y, c):
        me = 4 * x + 2 * y + c
        local, remote = [], []
        for t in range(ns):
            h = gs[t].shape[1] // 2
            theirs = i[t].at[:, pl.ds(pl.multiple_of((1 - c) * h, 8), h), :]
            remote.append((theirs, o[t], (x, y, 1 - c), o[t]))
        for t in range(ns, ns + ne):
            local.append((i[t], o[t].at[me]))
            for px, py, pc in _other_devices(x, y, c):
                remote.append((i[t], o[t].at[me], (px, py, pc), o[t].at[4 * px + 2 * py + pc]))
        for t in range(ns + ne, ns + ne + nw):
            remote.append((i[t], o[t], (x, y, 1 - c), o[t]))
        return local, remote

    return _exchange("grad_swap_sibling", list(gs) + list(everyone) + list(whole), outs, ne, ns + 7 * ne + nw, plan)


def _scatter_chips(parts, per_chip):
    ns, ng = len(parts), len(per_chip)
    outs = [jax.ShapeDtypeStruct(a.shape, a.dtype) for a in parts]
    outs += [jax.ShapeDtypeStruct((4,) + a.shape, a.dtype) for a in per_chip]

    def plan(i, o, x, y, c):
        mine = 2 * x + y
        local, remote = [], []
        for t in range(ns):
            local.append((i[t].at[mine], o[t].at[mine]))
            for px, py in _other_chips(x, y):
                remote.append((i[t].at[2 * px + py], o[t].at[mine], (px, py, c), o[t].at[2 * px + py]))
        for t in range(ns, ns + ng):
            local.append((i[t], o[t].at[mine]))
            for px, py in _other_chips(x, y):
                remote.append((i[t], o[t].at[mine], (px, py, c), o[t].at[2 * px + py]))
        return local, remote

    return _exchange("grad_scatter_chips", list(parts) + list(per_chip), outs, ns + ng, 3 * (ns + ng), plan)


def _join_halves(halves):
    ns = len(halves)
    outs = [jax.ShapeDtypeStruct(a.shape, a.dtype) for a in halves]

    def plan(i, o, x, y, c):
        return [], [(i[t], o[t], (x, y, 1 - c), o[t]) for t in range(ns)]

    return _exchange("grad_join_sibling", list(halves), outs, 0, ns, plan)


def _pad_heads_cols(w, per, used):
    k = w.shape[0]
    w = w.reshape(k, NH, per)[:, :, :used]
    return jnp.pad(w, ((0, 0), (0, 0), (0, HP - used))).reshape(k, NH * HP)


def _unpad_heads_cols(w, used):
    k = w.shape[0]
    return w.reshape(k, NH, HP)[:, :, :used]


def _prep_weights(wf):
    bf = lambda a: a.astype(BF16)
    out = {}
    out["w_in"] = jnp.pad(bf(wf["w_in"]), ((0, 0), (0, IN_PAD - IN_COLS)))
    out["w_glu"] = bf(wf["w_glu"])
    out["w_uq"] = _pad_heads_cols(bf(wf["w_uq"]), QK_NOPE + QK_ROPE, QK_NOPE + QK_ROPE)
    wkv = bf(wf["w_ukv"]).reshape(KV_LORA, NH, QK_NOPE + V_HEAD)
    wk = jnp.pad(wkv[:, :, :QK_NOPE], ((0, 0), (0, 0), (0, HP - QK_NOPE))).reshape(KV_LORA, NH * HP)
    wv = jnp.pad(wkv[:, :, QK_NOPE:], ((0, 0), (0, 0), (0, HP - V_HEAD))).reshape(KV_LORA, NH * HP)
    out["w_ukv"] = jnp.concatenate([wk, wv], axis=1)
    return out


def _prep_late_weights(wf):
    bf = lambda a: a.astype(BF16)
    out = {}
    wo = bf(wf["w_out"])
    wo_a = jnp.pad(wo[D_SSM:].reshape(NH, V_HEAD, D), ((0, 0), (0, HP - V_HEAD), (0, 0))).reshape(NH * HP, D)
    out["w_out"] = jnp.concatenate([wo[:D_SSM], wo_a], axis=0)
    out["w_ff1"] = bf(wf["w_ff1"])
    out["w_ff2"] = bf(wf["w_ff2"])
    return out


def _rope_tables(positions):
    inv_freq = ROPE_BASE ** (-jnp.arange(0, QK_ROPE, 2, dtype=F32) / QK_ROPE)
    ang = positions.astype(F32)[:, None] * inv_freq
    cos, sin = jnp.cos(ang), jnp.sin(ang)
    n = positions.shape[0]
    one = jnp.ones((n, QK_NOPE), F32)
    z16 = jnp.zeros((n, 16), F32)
    z32 = jnp.zeros((n, 32), F32)
    z64 = jnp.zeros((n, QK_NOPE), F32)
    rc = jnp.concatenate([one, cos, cos, z32], axis=1)
    rs1 = jnp.concatenate([z64, -sin, z16, z32], axis=1)
    rs2 = jnp.concatenate([z64, z16, sin, z32], axis=1)
    return rc, rs1, rs2


def _permute_rows(a, S):
    n, w = a.shape
    return a.reshape(n // S, 8, S // 8, w).transpose(0, 2, 1, 3).reshape(n, w)


def _unpermute_rows(a, S):
    n, w = a.shape
    return a.reshape(n // S, S // 8, 8, w).transpose(0, 2, 1, 3).reshape(n, w)


def _block_diag_in(bb):
    eye = jnp.eye(8, dtype=bb.dtype)
    blocks = jnp.einsum("qgph,gk->qghkp", bb.reshape(4, 8, P, H), eye).reshape(4, QB, QS)
    return blocks.transpose(1, 0, 2).reshape(QB, NST)


def _block_diag_out(cc):
    eye = jnp.eye(8, dtype=cc.dtype)
    return jnp.einsum("qghp,gk->qgpkh", cc.reshape(4, 8, H, P), eye).reshape(NST, QB)


def _slots(full):
    r, cdim = full.shape
    return full.reshape(r, 4, cdim // 4).transpose(1, 0, 2)


def _unslots(g):
    s, r, cs = g.shape
    return g.transpose(1, 0, 2).reshape(r, s * cs)


def _local_step(x, positions, target, modp, wf, late_weights=None, reducer=None):
    nb, S, _ = x.shape
    n = nb * S
    tm = min(256, S)
    tr = min(512, S)
    tt = min(512, S)
    tq = min(512, S // 2)
    kw = _prep_weights(wf)
    row = lambda a: a.reshape(1, -1).astype(F32)

    xf = x.reshape(n, D)
    tf = target.reshape(n, D)
    g1, g2, gf = row(wf["norm1_g"]), row(wf["norm2_g"]), row(wf["final_norm_g"])
    h1, proj = _f1_fwd(xf, modp, g1, kw["w_in"], S, tr)

    col = lambda a: a.reshape(NST, 1)
    lam_re, lam_im = col(wf["ssm_lambda_re"]), col(wf["ssm_lambda_im"])
    logdt = jnp.repeat(wf["ssm_log_dt"].reshape(G, 1), P, axis=1).reshape(NST, 1)
    b_re, b_im = wf["ssm_b_re"].reshape(NST, H), wf["ssm_b_im"].reshape(NST, H)
    lbr, lbi, bbr, bbi = _ssm_param_fwd(lam_re, lam_im, logdt, b_re, b_im)
    lre8 = jnp.broadcast_to(lbr.reshape(1, NST), (8, NST))
    lim8 = jnp.broadcast_to(lbi.reshape(1, NST), (8, NST))
    bm = jnp.concatenate([_block_diag_in(bbr.reshape(G, P, H)), _block_diag_in(bbi.reshape(G, P, H))],
                         axis=1).astype(BF16)
    cm = jnp.concatenate([_block_diag_out(wf["ssm_c_re"]), -_block_diag_out(wf["ssm_c_im"])], axis=0).astype(BF16)
    dvec = row(wf["ssm_d"])
    u_p = _permute_rows(proj[:, :D_SSM], S)
    fcr, fci = _ssm_local(u_p, bm, lre8, lim8, S, tt)
    st, ypre, z, gact, yssm_p = _ssm_fwd(u_p, fcr, fci, bm, cm, dvec, kw["w_glu"], lre8, lim8, S, tt)
    yssm = _unpermute_rows(yssm_p, S)

    rc, rs1, rs2 = _rope_tables(positions.reshape(n))
    gq, gkv = row(wf["q_norm_g"]), row(wf["kv_norm_g"])
    q, k, v, qn, kvn = _mla_fwd(proj, rc, rs1, rs2, gq, gkv, kw["w_uq"], kw["w_ukv"], tr)
    oattn, lrow = _attn_fwd(q, k, v, S, tq)

    gs = row(wf["ssm_out_g"])
    ga = jnp.pad(wf["attn_out_g"].reshape(NH, V_HEAD), ((0, 0), (0, HP - V_HEAD))).reshape(1, NH * HP)
    kw.update(_prep_late_weights(late_weights(oattn) if late_weights is not None else wf))
    yn, o, x1, h2 = _p1_fwd(yssm, oattn, xf, modp, gs, ga, kw["w_out"], g2, S, tr)
    dx1, r, da, dff, accs2, accg2 = _p2(x1, h2, tf, modp, g2, gf, kw["w_ff1"], kw["w_ff2"], S, tm)
    loss = accg2[2:3]
    g_ff1 = _wgrad(h2, da, "wgrad_ff1", col_slots=4)
    g_ff2 = _wgrad(r, dff, "wgrad_ff2").reshape(4, D_FF // 4, D)
    do, dyssm, dob, drow, accs3, accg3 = _p3_bwd(dx1, o, yssm, oattn, modp, gs, ga, kw["w_out"], S, tr)
    gwo = _wgrad(yn, do, "wgrad_out")
    g_out = jnp.concatenate([gwo[:D_SSM].reshape(2, D_SSM // 2, D),
                             gwo[D_SSM:].reshape(2, NH // 2 * HP, D).reshape(2, NH // 2, HP, D)[:, :, :V_HEAD]
                             .reshape(2, D_ATTN // 2, D)], axis=0)
    gq_b = gq
    if reducer is not None:
        drow = drow + reducer.start([g_ff1, g_ff2, g_out])[0, 0]

    dq, dk, dv = _attn_bwd(q, k, v, dob, lrow, drow, S, tq)
    if reducer is not None:
        gq_b = gq + reducer.middle(dq)[0, 0]
    dmla, dqb, dkvb, accm = _mla_bwd(dq, dk, dv, proj, rc, rs1, rs2, gq_b, gkv, kw["w_uq"], kw["w_ukv"], tr)

    dys_p = _permute_rows(dyssm, S)
    dy, dz, air, aii = _ssm_bwd_a(dys_p, z, ypre, kw["w_glu"], cm, lre8, lim8, S, tt)
    du_p, dcm, dbm, dd, dlr, dli = _ssm_bwd_b(dy, u_p, st, fcr, fci, air, aii, bm, cm, dvec, lre8, lim8, S, tt)
    du = _unpermute_rows(du_p, S)
    dcm = dcm.reshape(2, 4, 8, P, 8, H)
    dc_re = jnp.einsum("qgpgh->qghp", dcm[0]).reshape(G, H, P)
    dc_im = -jnp.einsum("qgpgh->qghp", dcm[1]).reshape(G, H, P)
    dbm = dbm.reshape(8, H, 2, 4, 8, P)
    dbb_re = jnp.einsum("ghqgp->qgph", dbm[:, :, 0]).reshape(NST, H)
    dbb_im = jnp.einsum("ghqgp->qgph", dbm[:, :, 1]).reshape(NST, H)
    gb_re, gb_im, glr, gli, gdt = _ssm_param_bwd(lam_re, lam_im, logdt, b_re, b_im, dlr.reshape(NST, 1),
                                                 dli.reshape(NST, 1), dbb_re, dbb_im)
    glogdt = _rowsum(gdt.reshape(G, P))

    dx, dproj, accs1, accg1 = _f1_bwd(du, dmla, dx1, xf, modp, g1, kw["w_in"], S, tr)

    big = {}
    big["w_in"] = _slots(_wgrad(h1, dproj, "wgrad_in")[:, :IN_COLS])
    big["w_glu"] = _wgrad(gact, dz, "wgrad_glu", col_slots=4)
    big["w_uq"] = _slots(_unpad_heads_cols(_wgrad(qn, dqb, "wgrad_uq"), QK_NOPE + QK_ROPE).reshape(Q_LORA, -1))
    gkvw = _wgrad(kvn, dkvb, "wgrad_ukv")
    big["w_ukv"] = _slots(jnp.concatenate([_unpad_heads_cols(gkvw[:, :NH * HP], QK_NOPE),
                                           _unpad_heads_cols(gkvw[:, NH * HP:], V_HEAD)], axis=2).reshape(KV_LORA, -1))
    big["w_out"] = g_out
    big["w_ff1"] = g_ff1
    big["w_ff2"] = g_ff2

    small = {}
    small["norm1_g"] = accg1[0:1]
    small["norm2_g"] = accg2[0:1]
    small["final_norm_g"] = accg2[1:2]
    small["ssm_out_g"] = accg3[0:1, :D_SSM]
    small["attn_out_g"] = accg3[1].reshape(NH, HP)[:, :V_HEAD].reshape(1, D_ATTN)
    small["q_norm_g"] = accm[0:1, :Q_LORA]
    small["kv_norm_g"] = accm[1:2, :KV_LORA]
    small["ssm_lambda_re"] = glr.reshape(G, P)
    small["ssm_lambda_im"] = gli.reshape(G, P)
    small["ssm_b_re"] = gb_re
    small["ssm_b_im"] = gb_im
    small["ssm_c_re"] = dc_re.reshape(G * H, P)
    small["ssm_c_im"] = dc_im.reshape(G * H, P)
    small["ssm_d"] = dd.reshape(G, H)
    small["ssm_log_dt"] = glogdt.reshape(1, G)
    return loss, dx.reshape(nb, S, D), big, small, accs1 + accs2 + accs3


def _view2d(a):
    return a.reshape(-1, a.shape[-1]) if a.ndim > 1 else a.reshape(1, -1)


def kernel(x, c, positions, ada_w, ada_b, norm1_g, w_in, ssm_lambda_re, ssm_lambda_im, ssm_b_re, ssm_b_im, ssm_c_re, ssm_c_im, ssm_d, ssm_log_dt, w_glu, q_norm_g, w_uq, kv_norm_g, w_ukv, ssm_out_g, attn_out_g, w_out, norm2_g, w_ff1, w_ff2, final_ada_w, final_ada_b, final_norm_g, loss_target, m_ada_w, m_ada_b, m_norm1_g, m_w_in, m_ssm_lambda_re, m_ssm_lambda_im, m_ssm_b_re, m_ssm_b_im, m_ssm_c_re, m_ssm_c_im, m_ssm_d, m_ssm_log_dt, m_w_glu, m_q_norm_g, m_w_uq, m_kv_norm_g, m_w_ukv, m_ssm_out_g, m_attn_out_g, m_w_out, m_norm2_g, m_w_ff1, m_w_ff2, m_final_ada_w, m_final_ada_b, m_final_norm_g, v_ada_w, v_ada_b, v_norm1_g, v_w_in, v_ssm_lambda_re, v_ssm_lambda_im, v_ssm_b_re, v_ssm_b_im, v_ssm_c_re, v_ssm_c_im, v_ssm_d, v_ssm_log_dt, v_w_glu, v_q_norm_g, v_w_uq, v_kv_norm_g, v_w_ukv, v_ssm_out_g, v_attn_out_g, v_w_out, v_norm2_g, v_w_ff1, v_w_ff2, v_final_ada_w, v_final_ada_b, v_final_norm_g):
    args = dict(locals())
    names = list(inspect.signature(kernel).parameters)
    wnames = names[3:names.index("loss_target")]
    small_names = [nm for nm in wnames if nm not in GATHERED and nm not in TP]
    reduced_names = [nm for nm in small_names if nm not in ("ada_b", "final_ada_b")]
    w = {nm: args[nm] for nm in wnames}
    m = {nm: args["m_" + nm] for nm in wnames}
    v = {nm: args["v_" + nm] for nm in wnames}
    nb = x.shape[0]
    xi, yi, ci = lax.axis_index("x"), lax.axis_index("y"), lax.axis_index("c")
    chip, me = 2 * xi + yi, 4 * xi + 2 * yi + ci

    unslot = lambda nm, g: g.reshape(-1, g.shape[-1]) if nm in ROW_SHARDED else _unslots(g)
    early = [nm for nm in GATHERED if nm not in LATE]
    got = _gather_chips("gather_weights", [_view2d(w[nm]).astype(BF16) for nm in early], [c])
    wf = {nm: unslot(nm, g) for nm, g in zip(early, got)}
    for nm in small_names:
        wf[nm] = w[nm][0] if w[nm].ndim > 1 else w[nm]
    c_all = got[len(early)].reshape(8 * nb, D)

    na, nf = ada_w.shape[-1], final_ada_w.shape[-1]
    ada_b_s = lax.dynamic_slice(ada_b, (0, chip * na), (1, na))
    fada_b_s = lax.dynamic_slice(final_ada_b.reshape(1, -1), (0, chip * nf), (1, nf))
    cond_all, modcols = _mod_fwd(c_all, ada_w[0], ada_b_s, final_ada_w, fada_b_s)
    (mod_g,) = _gather_chips("gather_mod", [modcols])
    mine = lax.dynamic_slice(mod_g, (0, me * nb, 0), (4, nb, na + nf))
    modp = jnp.concatenate([mine[:, :, :na].transpose(1, 0, 2).reshape(nb, 6, D),
                            mine[:, :, na:].transpose(1, 0, 2).reshape(nb, 2, D)], axis=1)

    own_late = [_view2d(w[nm]).astype(BF16) for nm in LATE]
    late_gather, token = _split_start("gather_late", own_late,
                                      [jax.ShapeDtypeStruct((4,) + a.shape, a.dtype) for a in own_late],
                                      3 * len(LATE), _plan_to_chips, modp)
    modp = modp + token[0, 0]

    def late_weights(after):
        sent, landed = _split_wait(late_gather, after)
        return {nm: unslot(nm, lax.dynamic_update_slice(g, own[None], (chip, 0, 0)))
                for nm, g, own in zip(LATE, landed, sent)}

    cidx = ci.astype(jnp.int32).reshape(1)
    ahead = ["w_ff1", "w_ff2", "w_out"]

    class Reducer:
        def start(self, gs):
            lands = [jax.ShapeDtypeStruct((4, g.shape[1] // 2, g.shape[2]), g.dtype) for g in gs]
            self.swap, tok = _split_start("grad_swap_ff", gs, lands, len(gs), _plan_swap_halves, modp)
            return tok

        def middle(self, after):
            gs, got = _split_wait(self.swap, after)
            sums = [_add_half(g, r, cidx, "grad_add_sibling_" + nm) for nm, g, r in zip(ahead, gs, got)]
            lands = [jax.ShapeDtypeStruct(s.shape, s.dtype) for s in sums]
            self.scatter, tok = _split_start("grad_scatter_ff", sums, lands, 3 * len(sums), _plan_scatter_chips, modp)
            return tok

        def finish(self, after):
            out = []
            for nm, s, l in zip(ahead, *_split_wait(self.scatter, after)):
                own = lax.dynamic_slice(s, (chip, 0, 0), (1,) + s.shape[1:])
                out.append(_add_chips(lax.dynamic_update_slice(l, own, (chip, 0, 0)), "grad_add_chips_" + nm))
            return out

    reducer = Reducer()
    loss_row, grad_x, big, small, dmodp = _local_step(x, positions, loss_target, modp, wf, late_weights, reducer)

    rest = [nm for nm in GATHERED if nm not in ahead]
    sizes = [small[nm].size for nm in reduced_names]
    pad = -sum(sizes) % 128
    packed = jnp.concatenate([small[nm].reshape(1, -1) for nm in reduced_names] + [jnp.zeros((1, pad), F32)],
                             axis=1).astype(BF16)
    swapped = _swap_halves([big[nm] for nm in rest], [dmodp.reshape(nb, 8 * D)], [packed, loss_row])
    chip_sums = [_add_half(big[nm], r, cidx, "grad_add_sibling_" + nm) for nm, r in zip(rest, swapped)]
    chip_small = _pair_sum(packed, swapped[len(rest) + 1], loss_row, swapped[len(rest) + 2])
    scattered = _scatter_chips(chip_sums, chip_small)
    half_of = {nm: _add_chips(r, "grad_add_chips_" + nm) for nm, r in zip(rest, scattered)}
    half_of.update(zip(ahead, reducer.finish(grad_x)))
    halves = [half_of[nm] for nm in GATHERED]
    others = _join_halves(halves)
    grads = {}
    dmod_all = swapped[len(rest)].reshape(8 * nb, 8 * D)
    small_sum, loss_sum = _sum_devices(scattered[len(rest)].reshape(4, -1), scattered[len(rest) + 1].reshape(4, -1))
    loss = jnp.sum(loss_sum)
    off = 0
    for nm, sz in zip(reduced_names, sizes):
        grads[nm] = small_sum[:, off:off + sz].reshape(small[nm].shape)
        off += sz

    dsl = jnp.concatenate([lax.dynamic_slice(dmod_all, (0, chip * na), (8 * nb, na)),
                           lax.dynamic_slice(dmod_all, (0, 6 * D + chip * nf), (8 * nb, nf))], axis=1)
    gw, gb = _mod_bwd(cond_all.T, dsl, dmod_all)
    grads["ada_w"], grads["final_ada_w"] = gw[:, :na], gw[:, na:]
    grads["ada_b"], grads["final_ada_b"] = gb[:, :6 * D], gb[:, 6 * D:]

    delta, new_m, new_v = {}, {}, {}
    for nm, mine_h, other_h in zip(GATHERED, halves, others):
        grads[nm], delta[nm], new_m[nm], new_v[nm] = _adamw_halves(
            _view2d(w[nm]), mine_h, other_h, _view2d(m[nm]), _view2d(v[nm]), cidx, "adamw_" + nm)
    for nm in TP:
        delta[nm], new_m[nm], new_v[nm] = _adamw(_view2d(w[nm]), grads[nm], _view2d(m[nm]), _view2d(v[nm]),
                                                  "adamw_" + nm)
    upd = _adamw_small([_view2d(w[nm]) for nm in small_names], [grads[nm] for nm in small_names],
                       [_view2d(m[nm]) for nm in small_names], [_view2d(v[nm]) for nm in small_names])
    k = len(small_names)
    for t, nm in enumerate(small_names):
        delta[nm], new_m[nm], new_v[nm] = upd[t], upd[k + t], upd[2 * k + t]

    outs = [grads, delta, new_m, new_v]
    return (loss, grad_x, *[d[nm].reshape(w[nm].shape) for d in outs for nm in wnames])
```

```python
import inspect
import math

import jax
import jax.numpy as jnp
from jax import lax
from jax.experimental import pallas as pl
from jax.experimental.pallas import tpu as pltpu

F32 = jnp.float32
BF16 = jnp.bfloat16

D = 1024
D_SSM = 512
G = 32
H = 16
P = 64
NST = G * P
D_ATTN = 512
NH = 8
QK_NOPE = 64
QK_ROPE = 32
V_HEAD = 64
HP = 128
Q_LORA = 384
KV_LORA = 256
IN_COLS = D_SSM + Q_LORA + KV_LORA + QK_ROPE
IN_PAD = 1280
D_FF = 4096
ROPE_BASE = 10000.0
EPS = 1e-6
ADAM_LR = 0.001
ADAM_B1 = 0.9
ADAM_B2 = 0.999
ADAM_EPS = 1e-08
ADAM_WD = 0.01
ADAM_STEP = 10
NEG = -1e30
VMEM_LIMIT = 60 << 20

MESH = pl.DeviceIdType.MESH
_VM = pl.BlockSpec(memory_space=pltpu.VMEM)
_ANY = pl.BlockSpec(memory_space=pl.ANY)

GATHERED = ["w_in", "w_glu", "w_uq", "w_ukv", "w_out", "w_ff1", "w_ff2"]
TP = ["ada_w", "final_ada_w"]
ROW_SHARDED = ("w_out", "w_ff2")
LATE = ["w_out", "w_ff1", "w_ff2"]


def _cp(sem=None, vmem=VMEM_LIMIT):
    kw = dict(vmem_limit_bytes=vmem)
    if sem is not None:
        kw["dimension_semantics"] = sem
    return pltpu.CompilerParams(**kw)


def _dot(a, b):
    return jnp.dot(a, b, preferred_element_type=F32)


def _dot_nt(a, b):
    return lax.dot_general(a, b, (((1,), (1,)), ((), ())), preferred_element_type=F32)


def _dot_tn(a, b):
    return lax.dot_general(a, b, (((0,), (0,)), ((), ())), preferred_element_type=F32)


def _rms(x, n):
    r = lax.rsqrt(jnp.sum(x * x, axis=-1, keepdims=True) * (1.0 / n) + EPS)
    return x * r, r


def _rms_bwd(dyg, xhat, r, n):
    return r * (dyg - xhat * (jnp.sum(dyg * xhat, axis=-1, keepdims=True) * (1.0 / n)))


def _sigmoid(x):
    return 1.0 / (1.0 + jnp.exp(-x))


_GK = math.sqrt(2.0 / math.pi)
_GC = 0.044715


def _gelu(y):
    t = jnp.tanh(_GK * (y + _GC * y * y * y))
    return 0.5 * y * (1.0 + t)


def _gelu_grad(y):
    t = jnp.tanh(_GK * (y + _GC * y * y * y))
    return 0.5 * (1.0 + t) + 0.5 * y * (1.0 - t * t) * _GK * (1.0 + 3.0 * _GC * y * y)


def _colsum(x):
    return jnp.sum(x, axis=0, keepdims=True)


def _roll(x, s):
    return pltpu.roll(x, s % x.shape[-1], x.ndim - 1)


def _mod_fwd(c_all, ada_w_s, ada_b_s, fada_w_s, fada_b_s):
    nseq = c_all.shape[0]
    na, nf = ada_w_s.shape[1], fada_w_s.shape[1]

    def body(c_ref, w_ref, b_ref, fw_ref, fb_ref, cond_ref, mod_ref):
        cv = c_ref[...]
        cond = cv * _sigmoid(cv)
        cond_ref[...] = cond
        cb = cond.astype(BF16)
        mod_ref[:, 0:na] = _dot(cb, w_ref[...].astype(BF16)) + b_ref[...]
        mod_ref[:, na:na + nf] = _dot(cb, fw_ref[...].astype(BF16)) + fb_ref[...]

    return pl.pallas_call(
        body, name="mod_fwd",
        out_shape=[jax.ShapeDtypeStruct((nseq, D), F32), jax.ShapeDtypeStruct((nseq, na + nf), F32)],
        in_specs=[_VM] * 5, out_specs=[_VM] * 2, compiler_params=_cp(),
    )(c_all, ada_w_s, ada_b_s, fada_w_s, fada_b_s)


def _mod_bwd(cond_t, dsl, dall):
    nseq, n = dsl.shape
    bc = 512

    def body(ct_ref, dm_ref, da_ref, gw_ref, gb_ref):
        ct = ct_ref[...]
        dm = dm_ref[...]
        acc = ct[:, 0:1] * dm[0:1, :]
        for b in range(1, nseq):
            acc = acc + ct[:, b:b + 1] * dm[b:b + 1, :]
        gw_ref[...] = acc

        @pl.when(pl.program_id(0) == 0)
        def _():
            gb_ref[...] = _colsum(da_ref[...])

    return pl.pallas_call(
        body, name="mod_bwd", grid=(n // bc,),
        out_shape=[jax.ShapeDtypeStruct((D, n), F32), jax.ShapeDtypeStruct((1, dall.shape[1]), F32)],
        in_specs=[_VM, pl.BlockSpec((nseq, bc), lambda i: (0, i)), _VM],
        out_specs=[pl.BlockSpec((D, bc), lambda i: (0, i)), pl.BlockSpec((1, dall.shape[1]), lambda i: (0, 0))],
        compiler_params=_cp(("arbitrary",)),
    )(cond_t, dsl, dall)


def _f1_fwd(x, modp, g1, w_in, rc, rs1, rs2, gq, gkv, w_uq, w_ukv, S, tm):
    n = x.shape[0]
    tps = S // tm
    LAT = IN_PAD - D_SSM

    def body(x_ref, mod_ref, g_ref, w_ref, c_ref, s1_ref, s2_ref, gq_ref, gkv_ref, wq_ref, wkv_ref,
             h_ref, u_ref, lat_ref, q_ref, k_ref, v_ref, qn_ref, kvn_ref):
        xhat, _ = _rms(x_ref[...], D)
        h = (xhat * g_ref[...]) * (1.0 + mod_ref[0, 1:2, :]) + mod_ref[0, 0:1, :]
        hb = h.astype(BF16)
        h_ref[...] = hb
        proj = _dot(hb, w_ref[...])
        u_ref[...] = proj[:, 0:D_SSM]
        lat_ref[...] = proj[:, D_SSM:IN_PAD]
        c, s1, s2 = c_ref[...], s1_ref[...], s2_ref[...]
        qhat, _ = _rms(proj[:, D_SSM:D_SSM + Q_LORA], Q_LORA)
        qn = (qhat * gq_ref[...]).astype(BF16)
        qn_ref[...] = qn
        q = _dot(qn, wq_ref[...])
        qr = _rope(q, jnp.tile(c, (1, NH)), jnp.tile(s1, (1, NH)), jnp.tile(s2, (1, NH)))
        q_ref[...] = (qr * _C2).astype(BF16)
        khat, _ = _rms(proj[:, D_SSM + Q_LORA:D_SSM + Q_LORA + KV_LORA], KV_LORA)
        kvn = (khat * gkv_ref[...]).astype(BF16)
        kvn_ref[...] = kvn
        kv = _dot(kvn, wkv_ref[...])
        kr = _rope(_roll(proj[:, IN_PAD - HP:IN_PAD], 64), c, s1, s2)
        k_ref[...] = (kv[:, 0:NH * HP] + jnp.tile(kr, (1, NH))).astype(BF16)
        vv = kv[:, NH * HP:2 * NH * HP]
        lane = lax.broadcasted_iota(jnp.int32, vv.shape, 1)
        v_ref[...] = jnp.where(lane % HP == V_HEAD, 1.0, vv).astype(BF16)

    row = lambda w: pl.BlockSpec((tm, w), lambda i: (i, 0))
    return pl.pallas_call(
        body, name="f1_fwd", grid=(n // tm,),
        out_shape=[jax.ShapeDtypeStruct((n, D), BF16), jax.ShapeDtypeStruct((n, D_SSM), F32),
                   jax.ShapeDtypeStruct((n, LAT), F32)] + [jax.ShapeDtypeStruct((n, NH * HP), BF16)] * 3 +
                  [jax.ShapeDtypeStruct((n, Q_LORA), BF16), jax.ShapeDtypeStruct((n, KV_LORA), BF16)],
        in_specs=[row(D), pl.BlockSpec((1, 8, D), lambda i: (i // tps, 0, 0)), _VM, _VM,
                  row(HP), row(HP), row(HP), _VM, _VM, _VM, _VM],
        out_specs=[row(D), row(D_SSM), row(LAT)] + [row(NH * HP)] * 3 + [row(Q_LORA), row(KV_LORA)],
        compiler_params=_cp(("parallel",)),
    )(x, modp, g1, w_in, rc, rs1, rs2, gq, gkv, w_uq, w_ukv)


def _f1_bwd(du, dmla, dx1, x, modp, g1, w_in, S, tm):
    n = x.shape[0]
    tps = S // tm
    nb = n // S

    def body(du_ref, dm_ref, dx1_ref, x_ref, mod_ref, g_ref, w_ref, dx_ref, dproj_ref, accs_ref, accg_ref):
        i = pl.program_id(0)
        dproj = jnp.concatenate([du_ref[...], dm_ref[...]], axis=1).astype(BF16)
        dproj_ref[...] = dproj
        dh = _dot_nt(dproj, w_ref[...])
        xhat, r = _rms(x_ref[...], D)
        g = g_ref[...]
        dn = dh * (1.0 + mod_ref[0, 1:2, :])
        dx_ref[...] = dx1_ref[...] + _rms_bwd(dn * g, xhat, r, D)

        @pl.when(i % tps == 0)
        def _():
            accs_ref[...] = jnp.zeros_like(accs_ref)

        @pl.when(i == 0)
        def _():
            accg_ref[...] = jnp.zeros_like(accg_ref)

        accs_ref[0, 0:1, :] += _colsum(dh)
        accs_ref[0, 1:2, :] += _colsum(dh * (xhat * g))
        accg_ref[0:1, :] += _colsum(dn * xhat)

    return pl.pallas_call(
        body, name="f1_bwd", grid=(n // tm,),
        out_shape=[jax.ShapeDtypeStruct((n, D), F32), jax.ShapeDtypeStruct((n, IN_PAD), BF16),
                   jax.ShapeDtypeStruct((nb, 8, D), F32), jax.ShapeDtypeStruct((8, D), F32)],
        in_specs=[pl.BlockSpec((tm, D_SSM), lambda i: (i, 0)), pl.BlockSpec((tm, IN_PAD - D_SSM), lambda i: (i, 0)),
                  pl.BlockSpec((tm, D), lambda i: (i, 0)), pl.BlockSpec((tm, D), lambda i: (i, 0)),
                  pl.BlockSpec((1, 8, D), lambda i: (i // tps, 0, 0)), _VM, _VM],
        out_specs=[pl.BlockSpec((tm, D), lambda i: (i, 0)), pl.BlockSpec((tm, IN_PAD), lambda i: (i, 0)),
                   pl.BlockSpec((1, 8, D), lambda i: (i // tps, 0, 0)), pl.BlockSpec((8, D), lambda i: (0, 0))],
        compiler_params=_cp(("arbitrary",)),
    )(du, dmla, dx1, x, modp, g1, w_in)


def _ssm_param_fwd(lam_re, lam_im, logdt, b_re, b_im):
    def body(lr_ref, li_ref, ld_ref, br_ref, bi_ref, lbr_ref, lbi_ref, bbr_ref, bbi_ref):
        lr, li = lr_ref[...], li_ref[...]
        dt = jnp.exp(ld_ref[...])
        er = jnp.exp(lr * dt)
        lbr = er * jnp.cos(li * dt)
        lbi = er * jnp.sin(li * dt)
        den = 1.0 / (lr * lr + li * li)
        cr = ((lbr - 1.0) * lr + lbi * li) * den
        ci = (lbi * lr - (lbr - 1.0) * li) * den
        lbr_ref[...] = lbr
        lbi_ref[...] = lbi
        bbr_ref[...] = cr * br_ref[...] - ci * bi_ref[...]
        bbi_ref[...] = cr * bi_ref[...] + ci * br_ref[...]

    return pl.pallas_call(
        body, name="ssm_param_fwd",
        out_shape=[jax.ShapeDtypeStruct((NST, 1), F32)] * 2 + [jax.ShapeDtypeStruct((NST, H), F32)] * 2,
        in_specs=[_VM] * 5, out_specs=[_VM] * 4, compiler_params=_cp(),
    )(lam_re, lam_im, logdt, b_re, b_im)


def _ssm_param_bwd(lam_re, lam_im, logdt, b_re, b_im, dlb_re, dlb_im, dbb_re, dbb_im):
    def body(lr_ref, li_ref, ld_ref, br_ref, bi_ref, dlr_ref, dli_ref, dbr_ref, dbi_ref,
             gbr_ref, gbi_ref, glr_ref, gli_ref, gdt_ref):
        lr, li = lr_ref[...], li_ref[...]
        dt = jnp.exp(ld_ref[...])
        er = jnp.exp(lr * dt)
        lbr = er * jnp.cos(li * dt)
        lbi = er * jnp.sin(li * dt)
        den = 1.0 / (lr * lr + li * li)
        nr, ni = lbr - 1.0, lbi
        cr = (nr * lr + ni * li) * den
        ci = (ni * lr - nr * li) * den
        br, bi = br_ref[...], bi_ref[...]
        dbr, dbi = dbr_ref[...], dbi_ref[...]
        gbr_ref[...] = cr * dbr + ci * dbi
        gbi_ref[...] = cr * dbi - ci * dbr
        gcr = jnp.sum(dbr * br + dbi * bi, axis=1, keepdims=True)
        gci = jnp.sum(dbi * br - dbr * bi, axis=1, keepdims=True)
        ilr, ili = lr * den, -li * den
        glbr = dlr_ref[...] + (gcr * ilr + gci * ili)
        glbi = dli_ref[...] + (gci * ilr - gcr * ili)
        qr = -(cr * ilr - ci * ili)
        qi = -(cr * ili + ci * ilr)
        glr = gcr * qr + gci * qi
        gli = gci * qr - gcr * qi
        glr = glr + dt * (glbr * lbr + glbi * lbi)
        gli = gli + dt * (glbi * lbr - glbr * lbi)
        wr = lr * lbr - li * lbi
        wi = lr * lbi + li * lbr
        glr_ref[...] = glr
        gli_ref[...] = gli
        gdt_ref[...] = (glbr * wr + glbi * wi) * dt

    return pl.pallas_call(
        body, name="ssm_param_bwd",
        out_shape=[jax.ShapeDtypeStruct((NST, H), F32)] * 2 + [jax.ShapeDtypeStruct((NST, 1), F32)] * 3,
        in_specs=[_VM] * 9, out_specs=[_VM] * 5, compiler_params=_cp(),
    )(lam_re, lam_im, logdt, b_re, b_im, dlb_re, dlb_im, dbb_re, dbb_im)


def _rowsum(a):
    def body(a_ref, o_ref):
        o_ref[...] = jnp.sum(a_ref[...], axis=1, keepdims=True)

    return pl.pallas_call(
        body, name="rowsum", out_shape=jax.ShapeDtypeStruct((a.shape[0], 1), F32),
        in_specs=[_VM], out_specs=_VM, compiler_params=_cp(),
    )(a)


QB = D_SSM // 4
QS = 4 * QB


def _bd_lo(part, q):
    return part * NST + q * QS


def _bd_expand(ub, bm_ref, out_ref):
    for part in range(2):
        for q in range(4):
            lo = _bd_lo(part, q)
            out_ref[:, lo:lo + QS] = _dot(ub[:, q * QB:(q + 1) * QB], bm_ref[:, lo:lo + QS])


def _bd_expand_t(db, cm_ref, out_ref):
    for part in range(2):
        for q in range(4):
            lo = _bd_lo(part, q)
            out_ref[:, lo:lo + QS] = _dot_nt(db[:, q * QB:(q + 1) * QB], cm_ref[lo:lo + QS, :])


def _bd_project(sb, cm_ref):
    return jnp.concatenate(
        [_dot(sb[:, _bd_lo(0, q):_bd_lo(0, q) + QS], cm_ref[_bd_lo(0, q):_bd_lo(0, q) + QS, :])
         + _dot(sb[:, _bd_lo(1, q):_bd_lo(1, q) + QS], cm_ref[_bd_lo(1, q):_bd_lo(1, q) + QS, :])
         for q in range(4)], axis=1)


def _bd_project_t(ab, bm_ref):
    return jnp.concatenate(
        [_dot_nt(ab[:, _bd_lo(0, q):_bd_lo(0, q) + QS], bm_ref[:, _bd_lo(0, q):_bd_lo(0, q) + QS])
         + _dot_nt(ab[:, _bd_lo(1, q):_bd_lo(1, q) + QS], bm_ref[:, _bd_lo(1, q):_bd_lo(1, q) + QS])
         for q in range(4)], axis=1)


def _pow2k(pr, pi, nsq):
    for _ in range(nsq):
        pr, pi = pr * pr - pi * pi, 2.0 * pr * pi
    return pr, pi


def _ssm_local(u_p, bm, lre8, lim8, S, tt):
    n = u_p.shape[0]
    nb, nt = n // S, S // tt
    nsq = int(round(math.log2(S // 8)))
    assert 2 ** nsq == S // 8

    def body(u_ref, bm_ref, lre_ref, lim_ref, cre_ref, cim_ref, sre, sim, bu):
        j = pl.program_id(1)

        @pl.when(j == 0)
        def _():
            sre[...] = jnp.zeros_like(sre)
            sim[...] = jnp.zeros_like(sim)

        _bd_expand(u_ref[...].astype(BF16), bm_ref, bu)
        lre, lim = lre_ref[...], lim_ref[...]

        def step(i, c):
            sr, si = c
            off = pl.multiple_of(i * 8, 8)
            br = bu[pl.ds(off, 8), 0:NST]
            bi = bu[pl.ds(off, 8), NST:2 * NST]
            return lre * sr - lim * si + br, lre * si + lim * sr + bi

        sr, si = lax.fori_loop(0, tt // 8, step, (sre[...], sim[...]))
        sre[...] = sr
        sim[...] = si

        @pl.when(j == nt - 1)
        def _():
            pr, pi = _pow2k(lre[0:1], lim[0:1], nsq)
            cr = jnp.zeros((1, NST), F32)
            ci = jnp.zeros((1, NST), F32)
            cre_ref[0:1, :] = cr
            cim_ref[0:1, :] = ci
            for k in range(1, 8):
                cr, ci = sr[k - 1:k] + pr * cr - pi * ci, si[k - 1:k] + pr * ci + pi * cr
                cre_ref[k:k + 1, :] = cr
                cim_ref[k:k + 1, :] = ci

    return pl.pallas_call(
        body, name="ssm_local", grid=(nb, nt),
        out_shape=[jax.ShapeDtypeStruct((nb * 8, NST), F32)] * 2,
        in_specs=[pl.BlockSpec((tt, D_SSM), lambda b, j: (b * nt + j, 0)), _VM, _VM, _VM],
        out_specs=[pl.BlockSpec((8, NST), lambda b, j: (b, 0))] * 2,
        scratch_shapes=[pltpu.VMEM((8, NST), F32), pltpu.VMEM((8, NST), F32), pltpu.VMEM((tt, 2 * NST), F32)],
        compiler_params=_cp(("arbitrary", "arbitrary")),
    )(u_p, bm, lre8, lim8)


def _ssm_fwd(u_p, cre, cim, bm, cm, dvec, w_glu, lre8, lim8, S, tt):
    n = u_p.shape[0]
    nb, nt = n // S, S // tt

    def body(u_ref, cre_ref, cim_ref, bm_ref, cm_ref, d_ref, wg_ref, lre_ref, lim_ref,
             st_ref, ypre_ref, z_ref, gact_ref, yssm_ref, sre, sim, bu):
        j = pl.program_id(1)

        @pl.when(j == 0)
        def _():
            sre[...] = cre_ref[...]
            sim[...] = cim_ref[...]

        u = u_ref[...]
        _bd_expand(u.astype(BF16), bm_ref, bu)
        lre, lim = lre_ref[...], lim_ref[...]

        def step(i, c):
            sr, si = c
            off = pl.multiple_of(i * 8, 8)
            nr = lre * sr - lim * si + bu[pl.ds(off, 8), 0:NST]
            ni = lre * si + lim * sr + bu[pl.ds(off, 8), NST:2 * NST]
            bu[pl.ds(off, 8), 0:NST] = nr
            bu[pl.ds(off, 8), NST:2 * NST] = ni
            return nr, ni

        sr, si = lax.fori_loop(0, tt // 8, step, (sre[...], sim[...]))
        sre[...] = sr
        sim[...] = si
        stb = bu[...].astype(BF16)
        st_ref[...] = stb
        y = _bd_project(stb, cm_ref) + d_ref[...] * u
        ypre_ref[...] = y
        gb = _gelu(y).astype(BF16)
        gact_ref[...] = gb
        z = _dot(gb, wg_ref[...])
        z_ref[...] = z
        yssm_ref[...] = z[:, 0:D_SSM] * _sigmoid(z[:, D_SSM:2 * D_SSM])

    row = lambda w: pl.BlockSpec((tt, w), lambda b, j: (b * nt + j, 0))
    return pl.pallas_call(
        body, name="ssm_fwd", grid=(nb, nt),
        out_shape=[jax.ShapeDtypeStruct((n, 2 * NST), BF16), jax.ShapeDtypeStruct((n, D_SSM), F32),
                   jax.ShapeDtypeStruct((n, 2 * D_SSM), F32), jax.ShapeDtypeStruct((n, D_SSM), BF16),
                   jax.ShapeDtypeStruct((n, D_SSM), F32)],
        in_specs=[row(D_SSM), pl.BlockSpec((8, NST), lambda b, j: (b, 0)), pl.BlockSpec((8, NST), lambda b, j: (b, 0)),
                  _VM, _VM, _VM, _VM, _VM, _VM],
        out_specs=[row(2 * NST), row(D_SSM), row(2 * D_SSM), row(D_SSM), row(D_SSM)],
        scratch_shapes=[pltpu.VMEM((8, NST), F32), pltpu.VMEM((8, NST), F32), pltpu.VMEM((tt, 2 * NST), F32)],
        compiler_params=_cp(("arbitrary", "arbitrary")),
    )(u_p, cre, cim, bm, cm, dvec, w_glu, lre8, lim8)


def _ssm_bwd_a(dys_p, z, ypre, w_glu, cm, lre8, lim8, S, tt):
    n = z.shape[0]
    nb, nt = n // S, S // tt
    nsq = int(round(math.log2(S // 8)))
    ng = tt // 8

    def body(dys_ref, z_ref, y_ref, wg_ref, cm_ref, lre_ref, lim_ref, dy_ref, dz_ref, are_ref, aim_ref, sre, sim, gb):
        j = pl.program_id(1)

        @pl.when(j == 0)
        def _():
            sre[...] = jnp.zeros_like(sre)
            sim[...] = jnp.zeros_like(sim)

        z = z_ref[...]
        z1, z2 = z[:, 0:D_SSM], z[:, D_SSM:2 * D_SSM]
        sg = _sigmoid(z2)
        dys = dys_ref[...]
        dz = jnp.concatenate([dys * sg, dys * z1 * sg * (1.0 - sg)], axis=1).astype(BF16)
        dz_ref[...] = dz
        dy = _dot_nt(dz, wg_ref[...]) * _gelu_grad(y_ref[...])
        dy_ref[...] = dy
        _bd_expand_t(dy.astype(BF16), cm_ref, gb)
        lre, lim = lre_ref[...], lim_ref[...]

        def step(i, c):
            ar, ai = c
            off = pl.multiple_of((ng - 1 - i) * 8, 8)
            gr = gb[pl.ds(off, 8), 0:NST]
            gi = gb[pl.ds(off, 8), NST:2 * NST]
            return lre * ar + lim * ai + gr, lre * ai - lim * ar + gi

        ar, ai = lax.fori_loop(0, ng, step, (sre[...], sim[...]))
        sre[...] = ar
        sim[...] = ai

        @pl.when(j == nt - 1)
        def _():
            pr, pi = _pow2k(lre[0:1], -lim[0:1], nsq)
            cr = jnp.zeros((1, NST), F32)
            ci = jnp.zeros((1, NST), F32)
            are_ref[7:8, :] = cr
            aim_ref[7:8, :] = ci
            for k in range(6, -1, -1):
                cr, ci = ar[k + 1:k + 2] + pr * cr - pi * ci, ai[k + 1:k + 2] + pr * ci + pi * cr
                are_ref[k:k + 1, :] = cr
                aim_ref[k:k + 1, :] = ci

    row = lambda w: pl.BlockSpec((tt, w), lambda b, j: (b * nt + nt - 1 - j, 0))
    return pl.pallas_call(
        body, name="ssm_bwd_a", grid=(nb, nt),
        out_shape=[jax.ShapeDtypeStruct((n, D_SSM), F32), jax.ShapeDtypeStruct((n, 2 * D_SSM), BF16),
                   jax.ShapeDtypeStruct((nb * 8, NST), F32), jax.ShapeDtypeStruct((nb * 8, NST), F32)],
        in_specs=[row(D_SSM), row(2 * D_SSM), row(D_SSM), _VM, _VM, _VM, _VM],
        out_specs=[row(D_SSM), row(2 * D_SSM), pl.BlockSpec((8, NST), lambda b, j: (b, 0)),
                   pl.BlockSpec((8, NST), lambda b, j: (b, 0))],
        scratch_shapes=[pltpu.VMEM((8, NST), F32), pltpu.VMEM((8, NST), F32), pltpu.VMEM((tt, 2 * NST), F32)],
        compiler_params=_cp(("arbitrary", "arbitrary")),
    )(dys_p, z, ypre, w_glu, cm, lre8, lim8)


def _ssm_bwd_b(dy, u_p, st, fcr, fci, air, aii, bm, cm, dvec, lre8, lim8, S, tt):
    n = u_p.shape[0]
    nb, nt = n // S, S // tt
    ng = tt // 8

    def body(dy_ref, u_ref, st_ref, stp_ref, fcr_ref, fci_ref, air_ref, aii_ref, bm_ref, cm_ref, d_ref, lre_ref, lim_ref,
             du_ref, dcm_ref, dbm_ref, dd_ref, dlr_ref, dli_ref, are, aim, accr, acci, sp, ab):
        b = pl.program_id(0)
        j = pl.program_id(1)
        jt = nt - 1 - j

        @pl.when((b == 0) & (j == 0))
        def _():
            dcm_ref[...] = jnp.zeros_like(dcm_ref)
            dbm_ref[...] = jnp.zeros_like(dbm_ref)
            dd_ref[...] = jnp.zeros_like(dd_ref)
            accr[...] = jnp.zeros_like(accr)
            acci[...] = jnp.zeros_like(acci)

        @pl.when(j == 0)
        def _():
            are[...] = air_ref[...]
            aim[...] = aii_ref[...]

        sp[8:tt + 8, :] = st_ref[...].astype(F32)

        @pl.when(jt == 0)
        def _():
            sp[0:8, 0:NST] = fcr_ref[...]
            sp[0:8, NST:2 * NST] = fci_ref[...]

        @pl.when(jt != 0)
        def _():
            sp[0:8, :] = stp_ref[8:16, :].astype(F32)

        dy = dy_ref[...]
        u = u_ref[...]
        dyb = dy.astype(BF16)
        _bd_expand_t(dyb, cm_ref, ab)
        lre, lim = lre_ref[...], lim_ref[...]

        def step(i, c):
            ar, ai = c
            off = pl.multiple_of((ng - 1 - i) * 8, 8)
            nr = lre * ar + lim * ai + ab[pl.ds(off, 8), 0:NST]
            ni = lre * ai - lim * ar + ab[pl.ds(off, 8), NST:2 * NST]
            ab[pl.ds(off, 8), 0:NST] = nr
            ab[pl.ds(off, 8), NST:2 * NST] = ni
            pr = sp[pl.ds(off, 8), 0:NST]
            pi = sp[pl.ds(off, 8), NST:2 * NST]
            accr[...] += nr * pr + ni * pi
            acci[...] += ni * pr - nr * pi
            return nr, ni

        ar, ai = lax.fori_loop(0, ng, step, (are[...], aim[...]))
        are[...] = ar
        aim[...] = ai
        a_b = ab[...].astype(BF16)
        du_ref[...] = _bd_project_t(a_b, bm_ref) + d_ref[...] * dy
        ub = u.astype(BF16)
        for q in range(4):
            for part in range(2):
                lo = part * NST + q * 4 * QB
                s_q = st_ref[:, lo:lo + 4 * QB]
                dcm_ref[lo:lo + 4 * QB, :] += _dot_tn(s_q, dyb[:, q * QB:(q + 1) * QB])
                dbm_ref[:, lo:lo + 4 * QB] += _dot_tn(ub[:, q * QB:(q + 1) * QB], a_b[:, lo:lo + 4 * QB])
        dd_ref[...] += _colsum(dy * u)

        @pl.when((b == nb - 1) & (j == nt - 1))
        def _():
            dlr_ref[...] = _colsum(accr[...])
            dli_ref[...] = _colsum(acci[...])

    row = lambda w: pl.BlockSpec((tt, w), lambda b, j: (b * nt + nt - 1 - j, 0))
    seq8 = pl.BlockSpec((8, NST), lambda b, j: (b, 0))
    prev = pl.BlockSpec((16, 2 * NST), lambda b, j: (jnp.maximum((b * nt + nt - 1 - j) * (tt // 16) - 1, 0), 0))
    const = lambda shape: pl.BlockSpec(shape, lambda b, j: (0, 0))
    return pl.pallas_call(
        body, name="ssm_bwd_b", grid=(nb, nt),
        out_shape=[jax.ShapeDtypeStruct((n, D_SSM), F32), jax.ShapeDtypeStruct((2 * NST, QB), F32),
                   jax.ShapeDtypeStruct((QB, 2 * NST), F32), jax.ShapeDtypeStruct((1, D_SSM), F32),
                   jax.ShapeDtypeStruct((1, NST), F32), jax.ShapeDtypeStruct((1, NST), F32)],
        in_specs=[row(D_SSM), row(D_SSM), row(2 * NST), prev, seq8, seq8, seq8, seq8, _VM, _VM, _VM, _VM, _VM],
        out_specs=[row(D_SSM), const((2 * NST, QB)), const((QB, 2 * NST)), const((1, D_SSM)),
                   const((1, NST)), const((1, NST))],
        scratch_shapes=[pltpu.VMEM((8, NST), F32)] * 4 + [pltpu.VMEM((tt + 8, 2 * NST), F32),
                                                          pltpu.VMEM((tt, 2 * NST), F32)],
        compiler_params=_cp(("arbitrary", "arbitrary")),
    )(dy, u_p, st, st, fcr, fci, air, aii, bm, cm, dvec, lre8, lim8)


def _rope(v, c, s1, s2):
    return v * c + _roll(v, -16) * s1 + _roll(v, 16) * s2


def _rope_t(dv, c, s1, s2):
    return dv * c + _roll(dv * s1, 16) + _roll(dv * s2, -16)


def _mla_bwd(dq, dk, dv, lat, rc, rs1, rs2, gq, gkv, w_uq, w_ukv, tm):
    n = lat.shape[0]

    def body(dq_ref, dk_ref, dv_ref, lat_ref, c_ref, s1_ref, s2_ref, gq_ref, gkv_ref, wq_ref, wkv_ref,
             dmla_ref, dqb_ref, dkvb_ref, acc_ref):
        i = pl.program_id(0)
        c, s1, s2 = c_ref[...], s1_ref[...], s2_ref[...]
        dqu = _rope_t(dq_ref[...] * _SCALE, jnp.tile(c, (1, NH)), jnp.tile(s1, (1, NH)),
                      jnp.tile(s2, (1, NH))).astype(BF16)
        dqb_ref[...] = dqu
        dqn = _dot_nt(dqu, wq_ref[...])
        qhat, rq = _rms(lat_ref[:, 0:Q_LORA], Q_LORA)
        dql = _rms_bwd(dqn * gq_ref[...], qhat, rq, Q_LORA)
        dkf = dk_ref[...] * (1.0 / _LOG2E)
        dkv = jnp.concatenate([dkf.astype(BF16), dv_ref[...].astype(BF16)], axis=1)
        dkvb_ref[...] = dkv
        dkvn = _dot_nt(dkv, wkv_ref[...])
        khat, rk = _rms(lat_ref[:, Q_LORA:Q_LORA + KV_LORA], KV_LORA)
        dkvl = _rms_bwd(dkvn * gkv_ref[...], khat, rk, KV_LORA)
        dkr = dkf[:, 0:HP]
        for h in range(1, NH):
            dkr = dkr + dkf[:, h * HP:(h + 1) * HP]
        lane = lax.broadcasted_iota(jnp.int32, dkr.shape, 1)
        dkr = jnp.where((lane >= QK_NOPE) & (lane < QK_NOPE + QK_ROPE), dkr, 0.0)
        dkr = _roll(_rope_t(dkr, c, s1, s2), -64)
        dmla_ref[...] = jnp.concatenate([dql, dkvl, dkr], axis=1)

        @pl.when(i == 0)
        def _():
            acc_ref[...] = jnp.zeros_like(acc_ref)

        acc_ref[0:1, 0:Q_LORA] += _colsum(dqn * qhat)
        acc_ref[1:2, 0:KV_LORA] += _colsum(dkvn * khat)

    row = lambda w: pl.BlockSpec((tm, w), lambda i: (i, 0))
    return pl.pallas_call(
        body, name="mla_bwd", grid=(n // tm,),
        out_shape=[jax.ShapeDtypeStruct((n, IN_PAD - D_SSM), F32), jax.ShapeDtypeStruct((n, NH * HP), BF16),
                   jax.ShapeDtypeStruct((n, 2 * NH * HP), BF16), jax.ShapeDtypeStruct((8, Q_LORA), F32)],
        in_specs=[row(NH * HP)] * 3 + [row(IN_PAD - D_SSM), row(HP), row(HP), row(HP), _VM, _VM, _VM, _VM],
        out_specs=[row(IN_PAD - D_SSM), row(NH * HP), row(2 * NH * HP), pl.BlockSpec((8, Q_LORA), lambda i: (0, 0))],
        compiler_params=_cp(("arbitrary",)),
    )(dq, dk, dv, lat, rc, rs1, rs2, gq, gkv, w_uq, w_ukv)


_SCALE = (QK_NOPE + QK_ROPE) ** -0.5
_LOG2E = 1.4426950408889634
_C2 = _SCALE * _LOG2E


def _attn_fwd(q, k, v, S, tq):
    n = q.shape[0]
    nb, nq = n // S, S // tq

    def body(q_ref, k_ref, v_ref, o_ref, lr_ref):
        qi = pl.program_id(2)
        qv = q_ref[...]

        def tile(j, c, diagonal):
            m, acc = c
            off = pl.multiple_of(j * tq, tq)
            s = _dot_nt(qv, k_ref[pl.ds(off, tq), :])
            if diagonal:
                rows = lax.broadcasted_iota(jnp.int32, s.shape, 0)
                cols = lax.broadcasted_iota(jnp.int32, s.shape, 1)
                s = jnp.where(cols <= rows, s, NEG)
            mn = jnp.maximum(m, jnp.max(s, axis=1, keepdims=True))
            p = jnp.exp2(s - mn)
            acc = jnp.exp2(m - mn) * acc + _dot(p.astype(BF16), v_ref[pl.ds(off, tq), :])
            return mn, acc

        init = (jnp.full((tq, 1), NEG, F32), jnp.zeros((tq, HP), F32))
        c = lax.fori_loop(0, qi, lambda j, c: tile(j, c, False), init)
        m, acc = tile(qi, c, True)
        l = acc[:, V_HEAD:V_HEAD + 1]
        vlane = lax.broadcasted_iota(jnp.int32, acc.shape, 1)
        o_ref[...] = jnp.where(vlane < V_HEAD, acc / l, 0.0).astype(BF16)
        lane = lax.broadcasted_iota(jnp.int32, (8, HP), 1)
        lse = jnp.broadcast_to(m + jnp.log(l) * _LOG2E, (tq, HP))
        lr_ref[...] = _rows_of(lse, jnp.where(lane == 0, 1.0, 0.0).astype(BF16))

    qs = pl.BlockSpec((tq, HP), lambda b, h, i: (b * nq + i, h))
    ks = pl.BlockSpec((S, HP), lambda b, h, i: (b, h))
    return pl.pallas_call(
        body, name="attn_fwd", grid=(nb, NH, nq),
        out_shape=[jax.ShapeDtypeStruct((n, NH * HP), BF16), jax.ShapeDtypeStruct((nb * NH * 8, S), F32)],
        in_specs=[qs, ks, ks], out_specs=[qs, pl.BlockSpec((8, tq), lambda b, h, i: (b * NH + h, i))],
        compiler_params=_cp(("parallel", "parallel", "arbitrary")),
    )(q, k, v)


def _rows_of(x, pick):
    x1 = x.astype(BF16)
    r1 = x - x1.astype(F32)
    x2 = r1.astype(BF16)
    x3 = (r1 - x2.astype(F32)).astype(BF16)
    return _dot_nt(pick, x1) + _dot_nt(pick, x2) + _dot_nt(pick, x3)


def _attn_bwd(q, k, v, dob, lrow, drow, S, tq):
    n = q.shape[0]
    nb, nq = n // S, S // tq

    def body(q_ref, k_ref, v_ref, do_ref, lr_ref, dr_ref, dqo_ref, dk_ref, dv_ref, dq_ref):
        kj = pl.program_id(2)

        @pl.when(kj == 0)
        def _():
            dq_ref[...] = jnp.zeros_like(dq_ref)

        kt = k_ref[...]
        vt = v_ref[...]

        def tile(i, c, diagonal):
            dk, dv = c
            off = pl.multiple_of(i * tq, tq)
            qv = q_ref[pl.ds(off, tq), :]
            dob = do_ref[pl.ds(off, tq), :]
            lr = lr_ref[0:1, pl.ds(off, tq)]
            dr = dr_ref[0:1, pl.ds(off, tq)]
            st = _dot_nt(kt, qv)
            dpt = _dot_nt(vt, dob)
            pt = jnp.exp2(st - lr)
            if diagonal:
                keys = lax.broadcasted_iota(jnp.int32, pt.shape, 0)
                qrys = lax.broadcasted_iota(jnp.int32, pt.shape, 1)
                pt = jnp.where(keys <= qrys, pt, 0.0)
            dst = (pt * (dpt - dr)).astype(BF16)
            dq_ref[pl.ds(off, tq), :] += _dot_tn(dst, kt)
            return dk + _dot(dst, qv), dv + _dot(pt.astype(BF16), dob)

        zero = jnp.zeros((tq, HP), F32)
        c = tile(kj, (zero, zero), True)
        dk, dv = lax.fori_loop(kj + 1, nq, lambda i, c: tile(i, c, False), c)
        dk_ref[...] = dk.astype(BF16)
        dv_ref[...] = dv.astype(BF16)

        @pl.when(kj == nq - 1)
        def _():
            dqo_ref[...] = dq_ref[...].astype(BF16)

    ts = pl.BlockSpec((tq, HP), lambda b, h, i: (b * nq + i, h))
    fs = pl.BlockSpec((S, HP), lambda b, h, i: (b, h))
    rs = pl.BlockSpec((8, S), lambda b, h, i: (b * NH + h, 0))
    return pl.pallas_call(
        body, name="attn_bwd", grid=(nb, NH, nq),
        out_shape=[jax.ShapeDtypeStruct((n, NH * HP), BF16)] * 3,
        in_specs=[fs, ts, ts, fs, rs, rs], out_specs=[fs, ts, ts],
        scratch_shapes=[pltpu.VMEM((S, HP), F32)],
        compiler_params=_cp(("parallel", "parallel", "arbitrary")),
    )(q, k, v, dob, lrow, drow)


def _p1_fwd(yssm, oattn, x, modp, gs, ga, w_out, g2, S, tm):
    n = x.shape[0]
    tps = S // tm

    def body(ys_ref, oa_ref, x_ref, mod_ref, gs_ref, ga_ref, w_ref, g2_ref, yn_ref, o_ref, x1_ref, h2_ref):
        yh, _ = _rms(ys_ref[...], D_SSM)
        ah, _ = _rms(oa_ref[...].astype(F32), D_ATTN)
        yn = jnp.concatenate([yh * gs_ref[...], ah * ga_ref[...]], axis=1).astype(BF16)
        yn_ref[...] = yn
        o = _dot(yn, w_ref[...])
        o_ref[...] = o.astype(BF16)
        x1 = x_ref[...] + mod_ref[0, 2:3, :] * o
        x1_ref[...] = x1
        xh, _ = _rms(x1, D)
        h2_ref[...] = ((xh * g2_ref[...]) * (1.0 + mod_ref[0, 4:5, :]) + mod_ref[0, 3:4, :]).astype(BF16)

    row = lambda w: pl.BlockSpec((tm, w), lambda i: (i, 0))
    return pl.pallas_call(
        body, name="p1_fwd", grid=(n // tm,),
        out_shape=[jax.ShapeDtypeStruct((n, D_SSM + NH * HP), BF16), jax.ShapeDtypeStruct((n, D), BF16),
                   jax.ShapeDtypeStruct((n, D), F32), jax.ShapeDtypeStruct((n, D), BF16)],
        in_specs=[row(D_SSM), row(NH * HP), row(D), pl.BlockSpec((1, 8, D), lambda i: (i // tps, 0, 0)),
                  _VM, _VM, _VM, _VM],
        out_specs=[row(D_SSM + NH * HP), row(D), row(D), row(D)],
        compiler_params=_cp(("parallel",)),
    )(yssm, oattn, x, modp, gs, ga, w_out, g2)


def _p2(x1, h2, target, modp, g2, gf, w_ff1, w_ff2, S, tm):
    n = x1.shape[0]
    tps = S // tm
    nb = n // S

    def body(x1_ref, h2_ref, t_ref, mod_ref, g2_ref, gf_ref, w1_ref, w2_ref,
             dx1_ref, r_ref, da_ref, dff_ref, accs_ref, accg_ref):
        i = pl.program_id(0)
        sh2, sc2, gt2 = mod_ref[0, 3:4, :], mod_ref[0, 4:5, :], mod_ref[0, 5:6, :]
        fsh, fsc = mod_ref[0, 6:7, :], mod_ref[0, 7:8, :]
        x1 = x1_ref[...]
        a = _dot(h2_ref[...], w1_ref[...])
        ra = jnp.maximum(a, 0.0)
        rb = (ra * ra).astype(BF16)
        r_ref[...] = rb
        ff = _dot(rb, w2_ref[...])
        x2 = x1 + gt2 * ff
        x2h, rf = _rms(x2, D)
        gf_v = gf_ref[...]
        outn = x2h * gf_v
        err = outn * (1.0 + fsc) + fsh - t_ref[...]
        dout = err * (1.0 / D)
        doutn = dout * (1.0 + fsc)
        dx2 = _rms_bwd(doutn * gf_v, x2h, rf, D)
        dff = (gt2 * dx2).astype(BF16)
        dff_ref[...] = dff
        dr = _dot_nt(dff, w2_ref[...])
        da = (dr * (2.0 * ra)).astype(BF16)
        da_ref[...] = da
        dh2 = _dot_nt(da, w1_ref[...])
        x1h, r2 = _rms(x1, D)
        g2_v = g2_ref[...]
        dn2 = dh2 * (1.0 + sc2)
        dx1_ref[...] = dx2 + _rms_bwd(dn2 * g2_v, x1h, r2, D)

        @pl.when(i % tps == 0)
        def _():
            accs_ref[...] = jnp.zeros_like(accs_ref)

        @pl.when(i == 0)
        def _():
            accg_ref[...] = jnp.zeros_like(accg_ref)

        accs_ref[0, 3:4, :] += _colsum(dh2)
        accs_ref[0, 4:5, :] += _colsum(dh2 * (x1h * g2_v))
        accs_ref[0, 5:6, :] += _colsum(dx2 * ff)
        accs_ref[0, 6:7, :] += _colsum(dout)
        accs_ref[0, 7:8, :] += _colsum(dout * outn)
        accg_ref[0:1, :] += _colsum(dn2 * x1h)
        accg_ref[1:2, :] += _colsum(doutn * x2h)
        accg_ref[2:3, :] += _colsum(err * err) * (0.5 / D)

    row = lambda w: pl.BlockSpec((tm, w), lambda i: (i, 0))
    return pl.pallas_call(
        body, name="p2_mlp_loss", grid=(n // tm,),
        out_shape=[jax.ShapeDtypeStruct((n, D), F32), jax.ShapeDtypeStruct((n, D_FF), BF16),
                   jax.ShapeDtypeStruct((n, D_FF), BF16), jax.ShapeDtypeStruct((n, D), BF16),
                   jax.ShapeDtypeStruct((nb, 8, D), F32), jax.ShapeDtypeStruct((8, D), F32)],
        in_specs=[row(D), row(D), row(D), pl.BlockSpec((1, 8, D), lambda i: (i // tps, 0, 0)), _VM, _VM, _VM, _VM],
        out_specs=[row(D), row(D_FF), row(D_FF), row(D), pl.BlockSpec((1, 8, D), lambda i: (i // tps, 0, 0)),
                   pl.BlockSpec((8, D), lambda i: (0, 0))],
        compiler_params=_cp(("arbitrary",)),
    )(x1, h2, target, modp, g2, gf, w_ff1, w_ff2)


def _p3_bwd(dx1, o, yssm, oattn, modp, gs, ga, w_out, S, tm):
    n = dx1.shape[0]
    tps = S // tm
    nb = n // S

    def body(dx1_ref, o_ref, ys_ref, oa_ref, mod_ref, gs_ref, ga_ref, w_ref,
             do_ref, dys_ref, doa_ref, dr_ref, accs_ref, accg_ref):
        i = pl.program_id(0)
        dx1 = dx1_ref[...]
        dob = (mod_ref[0, 2:3, :] * dx1).astype(BF16)
        do_ref[...] = dob
        dyn = _dot_nt(dob, w_ref[...])
        yh, rs = _rms(ys_ref[...], D_SSM)
        oa = oa_ref[...].astype(F32)
        ah, ra = _rms(oa, D_ATTN)
        d1 = dyn[:, 0:D_SSM]
        d2 = dyn[:, D_SSM:D_SSM + NH * HP]
        dys_ref[...] = _rms_bwd(d1 * gs_ref[...], yh, rs, D_SSM)
        doa = _rms_bwd(d2 * ga_ref[...], ah, ra, D_ATTN)
        doa_ref[...] = doa.astype(BF16)
        prod = doa * oa
        ones = jnp.ones((8, HP), BF16)
        for h in range(NH):
            dr_ref[h * 8:(h + 1) * 8, :] = _rows_of(prod[:, h * HP:(h + 1) * HP], ones)

        @pl.when(i % tps == 0)
        def _():
            accs_ref[...] = jnp.zeros_like(accs_ref)

        @pl.when(i == 0)
        def _():
            accg_ref[...] = jnp.zeros_like(accg_ref)

        accs_ref[0, 2:3, :] += _colsum(dx1 * o_ref[...])
        accg_ref[0:1, 0:D_SSM] += _colsum(d1 * yh)
        accg_ref[1:2, :] += _colsum(d2 * ah)

    row = lambda w: pl.BlockSpec((tm, w), lambda i: (i, 0))
    return pl.pallas_call(
        body, name="p3_bwd", grid=(n // tm,),
        out_shape=[jax.ShapeDtypeStruct((n, D), BF16), jax.ShapeDtypeStruct((n, D_SSM), F32),
                   jax.ShapeDtypeStruct((n, NH * HP), BF16), jax.ShapeDtypeStruct((nb * NH * 8, S), F32),
                   jax.ShapeDtypeStruct((nb, 8, D), F32), jax.ShapeDtypeStruct((8, NH * HP), F32)],
        in_specs=[row(D), row(D), row(D_SSM), row(NH * HP), pl.BlockSpec((1, 8, D), lambda i: (i // tps, 0, 0)),
                  _VM, _VM, _VM],
        out_specs=[row(D), row(D_SSM), row(NH * HP), pl.BlockSpec((NH * 8, tm), lambda i: (i // tps, i % tps)),
                   pl.BlockSpec((1, 8, D), lambda i: (i // tps, 0, 0)), pl.BlockSpec((8, NH * HP), lambda i: (0, 0))],
        compiler_params=_cp(("arbitrary",)),
    )(dx1, o, yssm, oattn, modp, gs, ga, w_out)


def _wgrad(a, b, name, col_slots=0):
    n, k1 = a.shape
    k2 = b.shape[1]
    bn = next((b for b in (1024, 512) if n % b == 0), n)
    bk1 = next((b for b in (1024, 512) if k1 % b == 0), k1)
    bk2 = k2 // col_slots if col_slots else (1024 if (k2 % 1024 == 0) else k2)

    def body(a_ref, b_ref, o_ref):
        @pl.when(pl.program_id(2) == 0)
        def _():
            o_ref[...] = jnp.zeros_like(o_ref)

        o_ref[...] += _dot_tn(a_ref[...], b_ref[...]).reshape(o_ref.shape)

    if col_slots:
        out_shape = jax.ShapeDtypeStruct((col_slots, k1, bk2), F32)
        out_spec = pl.BlockSpec((1, bk1, bk2), lambda i, j, t: (j, i, 0))
    else:
        out_shape = jax.ShapeDtypeStruct((k1, k2), F32)
        out_spec = pl.BlockSpec((bk1, bk2), lambda i, j, t: (i, j))
    return pl.pallas_call(
        body, name=name, grid=(k1 // bk1, k2 // bk2, n // bn),
        out_shape=out_shape,
        in_specs=[pl.BlockSpec((bn, bk1), lambda i, j, t: (t, i)), pl.BlockSpec((bn, bk2), lambda i, j, t: (t, j))],
        out_specs=out_spec,
        compiler_params=_cp(("parallel", "parallel", "arbitrary")),
    )(a, b)


def _row_block(rows):
    if rows <= 256:
        return rows
    return next(b for b in (256, 192, 128, 64, 32, 16, 8) if rows % b == 0)


def _add_half(g, recv, cidx, name):
    _, rows2, w = g.shape
    rows = rows2 // 2
    br = _row_block(rows)
    nblk = rows // br

    def body(c_ref, g_ref, r_ref, o_ref):
        o_ref[...] = (g_ref[...] + r_ref[...]).astype(BF16)

    return pl.pallas_call(
        body, name=name,
        grid_spec=pltpu.PrefetchScalarGridSpec(
            num_scalar_prefetch=1, grid=(4, nblk),
            in_specs=[pl.BlockSpec((1, br, w), lambda s, i, c: (s, c[0] * nblk + i, 0)),
                      pl.BlockSpec((1, br, w), lambda s, i, c: (s, i, 0))],
            out_specs=pl.BlockSpec((1, br, w), lambda s, i, c: (s, i, 0))),
        out_shape=jax.ShapeDtypeStruct((4, rows, w), BF16),
        compiler_params=_cp(("parallel", "parallel")),
    )(cidx, g, recv)


def _add_chips(r, name):
    _, rows, w = r.shape
    br = _row_block(rows)

    def body(r_ref, o_ref):
        f = lambda k: r_ref[k].astype(F32)
        o_ref[...] = ((f(0) + f(1)) + f(2)) + f(3)

    return pl.pallas_call(
        body, name=name, grid=(rows // br,),
        out_shape=jax.ShapeDtypeStruct((rows, w), F32),
        in_specs=[pl.BlockSpec((4, br, w), lambda i: (0, i, 0))],
        out_specs=pl.BlockSpec((br, w), lambda i: (i, 0)),
        compiler_params=_cp(("parallel",)),
    )(r)


def _pair_sum(a, sa, b, sb):
    def body(a_ref, sa_ref, b_ref, sb_ref, oa_ref, ob_ref):
        oa_ref[...] = (a_ref[...].astype(F32) + sa_ref[...].astype(F32)).astype(BF16)
        ob_ref[...] = b_ref[...] + sb_ref[...]

    return pl.pallas_call(
        body, name="small_grad_pair_sum",
        out_shape=[jax.ShapeDtypeStruct(a.shape, BF16), jax.ShapeDtypeStruct(b.shape, F32)],
        in_specs=[_VM] * 4, out_specs=[_VM, _VM], compiler_params=_cp(),
    )(a, sa, b, sb)


def _sum_devices(a, b):
    def body(a_ref, b_ref, oa_ref, ob_ref):
        acc = a_ref[0:1, :].astype(F32)
        accb = b_ref[0:1, :]
        for k in range(1, a.shape[0]):
            acc = acc + a_ref[k:k + 1, :].astype(F32)
            accb = accb + b_ref[k:k + 1, :]
        oa_ref[...] = acc
        ob_ref[...] = accb

    return pl.pallas_call(
        body, name="small_grad_sum",
        out_shape=[jax.ShapeDtypeStruct((1, a.shape[1]), F32), jax.ShapeDtypeStruct((1, b.shape[1]), F32)],
        in_specs=[_VM, _VM], out_specs=[_VM, _VM], compiler_params=_cp(),
    )(a, b)


def _adamw_math(wv, gv, mv, vv):
    m_new = ADAM_B1 * mv + (1.0 - ADAM_B1) * gv
    v_new = ADAM_B2 * vv + (1.0 - ADAM_B2) * (gv * gv)
    m_hat = m_new / (1.0 - ADAM_B1 ** ADAM_STEP)
    v_hat = v_new / (1.0 - ADAM_B2 ** ADAM_STEP)
    return -ADAM_LR * (m_hat / (jnp.sqrt(v_hat) + ADAM_EPS) + ADAM_WD * wv), m_new, v_new


def _adamw_small(ws, gs, ms, vs):
    k = len(ws)

    def body(*refs):
        ins, outs = refs[:4 * k], refs[4 * k:]
        for t in range(k):
            d, m_new, v_new = _adamw_math(ins[t][...], ins[k + t][...], ins[2 * k + t][...], ins[3 * k + t][...])
            outs[t][...] = d
            outs[k + t][...] = m_new
            outs[2 * k + t][...] = v_new

    shapes = [jax.ShapeDtypeStruct(w.shape, F32) for w in ws]
    return pl.pallas_call(
        body, name="adamw_small", out_shape=shapes * 3,
        in_specs=[_VM] * (4 * k), out_specs=[_VM] * (3 * k), compiler_params=_cp(),
    )(*ws, *gs, *ms, *vs)


def _adamw(w, g, m, v, name):
    rows, wd = w.shape
    br = _row_block(rows)

    def body(w_ref, g_ref, m_ref, v_ref, d_ref, nm_ref, nv_ref):
        d, m_new, v_new = _adamw_math(w_ref[...], g_ref[...], m_ref[...], v_ref[...])
        d_ref[...] = d
        nm_ref[...] = m_new
        nv_ref[...] = v_new

    spec = pl.BlockSpec((br, wd), lambda i: (i, 0))
    return pl.pallas_call(
        body, name=name, grid=(rows // br,),
        out_shape=[jax.ShapeDtypeStruct((rows, wd), F32)] * 3,
        in_specs=[spec] * 4, out_specs=[spec] * 3,
        compiler_params=_cp(("parallel",)),
    )(w, g, m, v)


def _adamw_halves(w, mine, other, m, v, cidx, name):
    rows, wd = w.shape
    h = rows // 2
    br = _row_block(h)
    nblk = h // br

    def body(c_ref, w_ref, a_ref, b_ref, m_ref, v_ref, g_ref, d_ref, nm_ref, nv_ref):
        gv = jnp.where(pl.program_id(0) == c_ref[0], a_ref[...], b_ref[...])
        d, m_new, v_new = _adamw_math(w_ref[...], gv, m_ref[...], v_ref[...])
        g_ref[...] = gv
        d_ref[...] = d
        nm_ref[...] = m_new
        nv_ref[...] = v_new

    full = pl.BlockSpec((br, wd), lambda hf, i, c: (hf * nblk + i, 0))
    half = pl.BlockSpec((br, wd), lambda hf, i, c: (i, 0))
    return pl.pallas_call(
        body, name=name,
        grid_spec=pltpu.PrefetchScalarGridSpec(
            num_scalar_prefetch=1, grid=(2, nblk),
            in_specs=[full, half, half, full, full], out_specs=[full] * 4),
        out_shape=[jax.ShapeDtypeStruct((rows, wd), F32)] * 4,
        compiler_params=_cp(("parallel", "parallel")),
    )(cidx, w, mine, other, m, v)


def _other_chips(x, y):
    return [(1 - x, y), (x, 1 - y), (1 - x, 1 - y)]


def _other_devices(x, y, c):
    flip = lambda v, d: (1 - v) if d else v
    return [(flip(x, dx), flip(y, dy), flip(c, dc))
            for dx in (0, 1) for dy in (0, 1) for dc in (0, 1) if (dx, dy, dc) != (0, 0, 0)]


def _exchange(name, ins, out_shapes, n_local, n_remote, plan):
    ni, no = len(ins), len(out_shapes)

    def body(*refs):
        in_refs, out_refs = refs[:ni], refs[ni:ni + no]
        send_sems, recv_sems, local_sems = refs[ni + no:]
        x, y, c = lax.axis_index("x"), lax.axis_index("y"), lax.axis_index("c")
        local, remote = plan(in_refs, out_refs, x, y, c)
        assert len(local) == n_local and len(remote) == n_remote

        def push(k, src, dst, dev):
            return pltpu.make_async_remote_copy(src_ref=src, dst_ref=dst, send_sem=send_sems.at[k],
                                                recv_sem=recv_sems.at[k], device_id=dev, device_id_type=MESH)

        own = [pltpu.make_async_copy(s, d, local_sems.at[i]) for i, (s, d) in enumerate(local)]
        for cp in own:
            cp.start()
        sends = [push(k, s, d, dev) for k, (s, d, dev, _) in enumerate(remote)]
        for cp in sends:
            cp.start()
        for k, (s, _, dev, landing) in enumerate(remote):
            push(k, s, landing, dev).wait_recv()
        for cp in sends:
            cp.wait_send()
        for cp in own:
            cp.wait()

    return pl.pallas_call(
        body, name=name, out_shape=out_shapes,
        in_specs=[_ANY] * ni, out_specs=[_ANY] * no,
        scratch_shapes=[pltpu.SemaphoreType.DMA((n_remote,)), pltpu.SemaphoreType.DMA((n_remote,)),
                        pltpu.SemaphoreType.DMA((max(n_local, 1),))],
        compiler_params=pltpu.CompilerParams(has_side_effects=True),
    )(*ins)


def _gather_chips(name, shards, everyone=()):
    ns, ne = len(shards), len(everyone)
    outs = [jax.ShapeDtypeStruct((4,) + a.shape, a.dtype) for a in shards]
    outs += [jax.ShapeDtypeStruct((8,) + a.shape, a.dtype) for a in everyone]

    def plan(i, o, x, y, c):
        mine, me = 2 * x + y, 4 * x + 2 * y + c
        local, remote = [], []
        for t in range(ns):
            local.append((i[t], o[t].at[mine]))
            for px, py in _other_chips(x, y):
                remote.append((i[t], o[t].at[mine], (px, py, c), o[t].at[2 * px + py]))
        for t in range(ns, ns + ne):
            local.append((i[t], o[t].at[me]))
            for px, py, pc in _other_devices(x, y, c):
                remote.append((i[t], o[t].at[me], (px, py, pc), o[t].at[4 * px + 2 * py + pc]))
        return local, remote

    return _exchange(name, list(shards) + list(everyone), outs, ns + ne, 3 * ns + 7 * ne, plan)


_HBM = pl.BlockSpec(memory_space=pltpu.HBM)
_SEM = pl.BlockSpec(memory_space=pltpu.SEMAPHORE)
_EFFECT = pltpu.SideEffectType.DATAFLOW_SIDE_EFFECTING


def _split_start(name, ins, land_shapes, n_remote, plan, after):
    ni, nl = len(ins), len(land_shapes)
    srcs = [pltpu.with_memory_space_constraint(a, pltpu.HBM) for a in ins]
    lands = [pltpu.with_memory_space_constraint(lax.empty(s.shape, s.dtype), pltpu.HBM) for s in land_shapes]

    def body(*refs):
        src, land = refs[:ni], refs[ni:ni + nl]
        first = ni + nl + 1
        send, recv = refs[first:first + n_remote], refs[first + n_remote:first + 2 * n_remote]
        token = refs[first + 2 * n_remote + ni + nl]
        x, y, c = lax.axis_index("x"), lax.axis_index("y"), lax.axis_index("c")
        remote = plan(src, land, x, y, c)
        assert len(remote) == n_remote
        for k, (s, d, dev, _) in enumerate(remote):
            pltpu.make_async_remote_copy(src_ref=s, dst_ref=d, send_sem=send[k], recv_sem=recv[k],
                                         device_id=dev, device_id_type=MESH).start()
        token[...] = jnp.zeros_like(token)

    out = pl.pallas_call(
        body, name=name + "_start",
        out_shape=[pltpu.SemaphoreType.DMA(())] * (2 * n_remote)
                  + [pltpu.HBM(a.shape, a.dtype) for a in ins] + [pltpu.HBM(s.shape, s.dtype) for s in land_shapes]
                  + [jax.ShapeDtypeStruct((8, 128), F32)],
        in_specs=[_HBM] * (ni + nl) + [_ANY], out_specs=[_SEM] * (2 * n_remote) + [_HBM] * (ni + nl) + [_VM],
        input_output_aliases={t: 2 * n_remote + t for t in range(ni + nl)},
        compiler_params=pltpu.CompilerParams(has_side_effects=_EFFECT),
    )(*srcs, *lands, after)
    sems, thru = out[:2 * n_remote], out[2 * n_remote:2 * n_remote + ni + nl]
    return (name, sems, thru[:ni], thru[ni:], n_remote, plan), out[-1]


def _split_wait(handle, after):
    name, sems, srcs, lands, n_remote, plan = handle
    ni, nl = len(srcs), len(lands)

    def body(*refs):
        src, land = refs[:ni], refs[ni:ni + nl]
        send, recv = refs[ni + nl:ni + nl + n_remote], refs[ni + nl + n_remote:ni + nl + 2 * n_remote]
        x, y, c = lax.axis_index("x"), lax.axis_index("y"), lax.axis_index("c")
        for k, (s, _, dev, landing) in enumerate(plan(src, land, x, y, c)):
            cp = pltpu.make_async_remote_copy(src_ref=s, dst_ref=landing, send_sem=send[k], recv_sem=recv[k],
                                              device_id=dev, device_id_type=MESH)
            cp.wait_send()
            cp.wait_recv()

    out = pl.pallas_call(
        body, name=name + "_wait",
        out_shape=[pltpu.HBM(a.shape, a.dtype) for a in srcs] + [pltpu.HBM(a.shape, a.dtype) for a in lands],
        in_specs=[_HBM] * (ni + nl) + [_SEM] * (2 * n_remote) + [_ANY], out_specs=[_HBM] * (ni + nl),
        input_output_aliases={t: t for t in range(ni + nl)},
        compiler_params=pltpu.CompilerParams(has_side_effects=_EFFECT),
    )(*srcs, *lands, *sems, after)
    return out[:ni], out[ni:]


def _plan_to_chips(src, land, x, y, c):
    mine = 2 * x + y
    return [(src[t], land[t].at[mine], (px, py, c), land[t].at[2 * px + py])
            for t in range(len(src)) for px, py in _other_chips(x, y)]


def _plan_swap_halves(src, land, x, y, c):
    out = []
    for t in range(len(src)):
        h = src[t].shape[1] // 2
        out.append((src[t].at[:, pl.ds(pl.multiple_of((1 - c) * h, 8), h), :], land[t], (x, y, 1 - c), land[t]))
    return out


def _plan_scatter_chips(src, land, x, y, c):
    mine = 2 * x + y
    return [(src[t].at[2 * px + py], land[t].at[mine], (px, py, c), land[t].at[2 * px + py])
            for t in range(len(src)) for px, py in _other_chips(x, y)]


def _swap_halves(gs, everyone, whole):
    ns, ne, nw = len(gs), len(everyone), len(whole)
    outs = [jax.ShapeDtypeStruct((4, g.shape[1] // 2, g.shape[2]), g.dtype) for g in gs]
    outs += [jax.ShapeDtypeStruct((8,) + a.shape, a.dtype) for a in everyone]
    outs += [jax.ShapeDtypeStruct(a.shape, a.dtype) for a in whole]

    def plan(i, o, x, y, c):
        me = 4 * x + 2 * y + c
        local, remote = [], []
        for t in range(ns):
            h = gs[t].shape[1] // 2
            theirs = i[t].at[:, pl.ds(pl.multiple_of((1 - c) * h, 8), h), :]
            remote.append((theirs, o[t], (x, y, 1 - c), o[t]))
        for t in range(ns, ns + ne):
            local.append((i[t], o[t].at[me]))
            for px, py, pc in _other_devices(x, y, c):
                remote.append((i[t], o[t].at[me], (px, py, pc), o[t].at[4 * px + 2 * py + pc]))
        for t in range(ns + ne, ns + ne + nw):
            remote.append((i[t], o[t], (x, y, 1 - c), o[t]))
        return local, remote

    return _exchange("grad_swap_sibling", list(gs) + list(everyone) + list(whole), outs, ne, ns + 7 * ne + nw, plan)


def _scatter_chips(parts, per_chip):
    ns, ng = len(parts), len(per_chip)
    outs = [jax.ShapeDtypeStruct(a.shape, a.dtype) for a in parts]
    outs += [jax.ShapeDtypeStruct((4,) + a.shape, a.dtype) for a in per_chip]

    def plan(i, o, x, y, c):
        mine = 2 * x + y
        local, remote = [], []
        for t in range(ns):
            local.append((i[t].at[mine], o[t].at[mine]))
            for px, py in _other_chips(x, y):
                remote.append((i[t].at[2 * px + py], o[t].at[mine], (px, py, c), o[t].at[2 * px + py]))
        for t in range(ns, ns + ng):
            local.append((i[t], o[t].at[mine]))
            for px, py in _other_chips(x, y):
                remote.append((i[t], o[t].at[mine], (px, py, c), o[t].at[2 * px + py]))
        return local, remote

    return _exchange("grad_scatter_chips", list(parts) + list(per_chip), outs, ns + ng, 3 * (ns + ng), plan)


def _join_halves(halves):
    ns = len(halves)
    outs = [jax.ShapeDtypeStruct(a.shape, a.dtype) for a in halves]

    def plan(i, o, x, y, c):
        return [], [(i[t], o[t], (x, y, 1 - c), o[t]) for t in range(ns)]

    return _exchange("grad_join_sibling", list(halves), outs, 0, ns, plan)


def _pad_heads_cols(w, per, used):
    k = w.shape[0]
    w = w.reshape(k, NH, per)[:, :, :used]
    return jnp.pad(w, ((0, 0), (0, 0), (0, HP - used))).reshape(k, NH * HP)


def _unpad_heads_cols(w, used):
    k = w.shape[0]
    return w.reshape(k, NH, HP)[:, :, :used]


def _prep_weights(wf):
    bf = lambda a: a.astype(BF16)
    out = {}
    out["w_in"] = jnp.pad(bf(wf["w_in"]), ((0, 0), (0, IN_PAD - IN_COLS)))
    out["w_glu"] = bf(wf["w_glu"])
    out["w_uq"] = _pad_heads_cols(bf(wf["w_uq"]), QK_NOPE + QK_ROPE, QK_NOPE + QK_ROPE)
    wkv = bf(wf["w_ukv"]).reshape(KV_LORA, NH, QK_NOPE + V_HEAD)
    wk = jnp.pad(wkv[:, :, :QK_NOPE], ((0, 0), (0, 0), (0, HP - QK_NOPE))).reshape(KV_LORA, NH * HP)
    wv = jnp.pad(wkv[:, :, QK_NOPE:], ((0, 0), (0, 0), (0, HP - V_HEAD))).reshape(KV_LORA, NH * HP)
    out["w_ukv"] = jnp.concatenate([wk, wv], axis=1)
    return out


def _prep_late_weights(wf):
    bf = lambda a: a.astype(BF16)
    out = {}
    wo = bf(wf["w_out"])
    wo_a = jnp.pad(wo[D_SSM:].reshape(NH, V_HEAD, D), ((0, 0), (0, HP - V_HEAD), (0, 0))).reshape(NH * HP, D)
    out["w_out"] = jnp.concatenate([wo[:D_SSM], wo_a], axis=0)
    out["w_ff1"] = bf(wf["w_ff1"])
    out["w_ff2"] = bf(wf["w_ff2"])
    return out


def _rope_tables(positions):
    inv_freq = ROPE_BASE ** (-jnp.arange(0, QK_ROPE, 2, dtype=F32) / QK_ROPE)
    ang = positions.astype(F32)[:, None] * inv_freq
    cos, sin = jnp.cos(ang), jnp.sin(ang)
    n = positions.shape[0]
    one = jnp.ones((n, QK_NOPE), F32)
    z16 = jnp.zeros((n, 16), F32)
    z32 = jnp.zeros((n, 32), F32)
    z64 = jnp.zeros((n, QK_NOPE), F32)
    rc = jnp.concatenate([one, cos, cos, z32], axis=1)
    rs1 = jnp.concatenate([z64, -sin, z16, z32], axis=1)
    rs2 = jnp.concatenate([z64, z16, sin, z32], axis=1)
    return rc, rs1, rs2


def _permute_rows(a, S):
    n, w = a.shape
    return a.reshape(n // S, 8, S // 8, w).transpose(0, 2, 1, 3).reshape(n, w)


def _unpermute_rows(a, S):
    n, w = a.shape
    return a.reshape(n // S, S // 8, 8, w).transpose(0, 2, 1, 3).reshape(n, w)


def _block_diag_in(bb):
    eye = jnp.eye(8, dtype=bb.dtype)
    blocks = jnp.einsum("qgph,gk->qghkp", bb.reshape(4, 8, P, H), eye).reshape(4, QB, QS)
    return blocks.transpose(1, 0, 2).reshape(QB, NST)


def _block_diag_out(cc):
    eye = jnp.eye(8, dtype=cc.dtype)
    return jnp.einsum("qghp,gk->qgpkh", cc.reshape(4, 8, H, P), eye).reshape(NST, QB)


def _slots(full):
    r, cdim = full.shape
    return full.reshape(r, 4, cdim // 4).transpose(1, 0, 2)


def _unslots(g):
    s, r, cs = g.shape
    return g.transpose(1, 0, 2).reshape(r, s * cs)


def _local_step(x, positions, target, modp, wf, late_weights=None, reducer=None):
    nb, S, _ = x.shape
    n = nb * S
    tm = min(256, S)
    tr = min(512, S)
    tt = min(512, S)
    tq = min(512, S // 2)
    kw = _prep_weights(wf)
    row = lambda a: a.reshape(1, -1).astype(F32)

    xf = x.reshape(n, D)
    tf = target.reshape(n, D)
    g1, g2, gf = row(wf["norm1_g"]), row(wf["norm2_g"]), row(wf["final_norm_g"])
    rc, rs1, rs2 = _rope_tables(positions.reshape(n))
    gq, gkv = row(wf["q_norm_g"]), row(wf["kv_norm_g"])
    h1, u, lat, q, k, v, qn, kvn = _f1_fwd(xf, modp, g1, kw["w_in"], rc, rs1, rs2, gq, gkv,
                                           kw["w_uq"], kw["w_ukv"], S, tr)

    col = lambda a: a.reshape(NST, 1)
    lam_re, lam_im = col(wf["ssm_lambda_re"]), col(wf["ssm_lambda_im"])
    logdt = jnp.repeat(wf["ssm_log_dt"].reshape(G, 1), P, axis=1).reshape(NST, 1)
    b_re, b_im = wf["ssm_b_re"].reshape(NST, H), wf["ssm_b_im"].reshape(NST, H)
    lbr, lbi, bbr, bbi = _ssm_param_fwd(lam_re, lam_im, logdt, b_re, b_im)
    lre8 = jnp.broadcast_to(lbr.reshape(1, NST), (8, NST))
    lim8 = jnp.broadcast_to(lbi.reshape(1, NST), (8, NST))
    bm = jnp.concatenate([_block_diag_in(bbr.reshape(G, P, H)), _block_diag_in(bbi.reshape(G, P, H))],
                         axis=1).astype(BF16)
    cm = jnp.concatenate([_block_diag_out(wf["ssm_c_re"]), -_block_diag_out(wf["ssm_c_im"])], axis=0).astype(BF16)
    dvec = row(wf["ssm_d"])
    u_p = _permute_rows(u, S)
    fcr, fci = _ssm_local(u_p, bm, lre8, lim8, S, tt)
    st, ypre, z, gact, yssm_p = _ssm_fwd(u_p, fcr, fci, bm, cm, dvec, kw["w_glu"], lre8, lim8, S, tt)
    yssm = _unpermute_rows(yssm_p, S)

    oattn, lrow = _attn_fwd(q, k, v, S, tq)

    gs = row(wf["ssm_out_g"])
    ga = jnp.pad(wf["attn_out_g"].reshape(NH, V_HEAD), ((0, 0), (0, HP - V_HEAD))).reshape(1, NH * HP)
    kw.update(_prep_late_weights(late_weights(oattn) if late_weights is not None else wf))
    yn, o, x1, h2 = _p1_fwd(yssm, oattn, xf, modp, gs, ga, kw["w_out"], g2, S, tr)
    dx1, r, da, dff, accs2, accg2 = _p2(x1, h2, tf, modp, g2, gf, kw["w_ff1"], kw["w_ff2"], S, tm)
    loss = accg2[2:3]
    g_ff1 = _wgrad(h2, da, "wgrad_ff1", col_slots=4)
    g_ff2 = _wgrad(r, dff, "wgrad_ff2").reshape(4, D_FF // 4, D)
    do, dyssm, dob, drow, accs3, accg3 = _p3_bwd(dx1, o, yssm, oattn, modp, gs, ga, kw["w_out"], S, tr)
    gwo = _wgrad(yn, do, "wgrad_out")
    g_out = jnp.concatenate([gwo[:D_SSM].reshape(2, D_SSM // 2, D),
                             gwo[D_SSM:].reshape(2, NH // 2 * HP, D).reshape(2, NH // 2, HP, D)[:, :, :V_HEAD]
                             .reshape(2, D_ATTN // 2, D)], axis=0)
    gq_b = gq
    if reducer is not None:
        drow = drow + reducer.start([g_ff1, g_ff2, g_out])[0, 0]

    dq, dk, dv = _attn_bwd(q, k, v, dob, lrow, drow, S, tq)
    if reducer is not None:
        gq_b = gq + reducer.middle(dq)[0, 0]
    dmla, dqb, dkvb, accm = _mla_bwd(dq, dk, dv, lat, rc, rs1, rs2, gq_b, gkv, kw["w_uq"], kw["w_ukv"], tr)

    dys_p = _permute_rows(dyssm, S)
    dy, dz, air, aii = _ssm_bwd_a(dys_p, z, ypre, kw["w_glu"], cm, lre8, lim8, S, tt)
    du_p, dcm, dbm, dd, dlr, dli = _ssm_bwd_b(dy, u_p, st, fcr, fci, air, aii, bm, cm, dvec, lre8, lim8, S, tt)
    du = _unpermute_rows(du_p, S)
    dcm = dcm.reshape(2, 4, 8, P, 8, H)
    dc_re = jnp.einsum("qgpgh->qghp", dcm[0]).reshape(G, H, P)
    dc_im = -jnp.einsum("qgpgh->qghp", dcm[1]).reshape(G, H, P)
    dbm = dbm.reshape(8, H, 2, 4, 8, P)
    dbb_re = jnp.einsum("ghqgp->qgph", dbm[:, :, 0]).reshape(NST, H)
    dbb_im = jnp.einsum("ghqgp->qgph", dbm[:, :, 1]).reshape(NST, H)
    gb_re, gb_im, glr, gli, gdt = _ssm_param_bwd(lam_re, lam_im, logdt, b_re, b_im, dlr.reshape(NST, 1),
                                                 dli.reshape(NST, 1), dbb_re, dbb_im)
    glogdt = _rowsum(gdt.reshape(G, P))

    dx, dproj, accs1, accg1 = _f1_bwd(du, dmla, dx1, xf, modp, g1, kw["w_in"], S, tr)

    big = {}
    big["w_in"] = _slots(_wgrad(h1, dproj, "wgrad_in")[:, :IN_COLS])
    big["w_glu"] = _wgrad(gact, dz, "wgrad_glu", col_slots=4)
    big["w_uq"] = _slots(_unpad_heads_cols(_wgrad(qn, dqb, "wgrad_uq"), QK_NOPE + QK_ROPE).reshape(Q_LORA, -1))
    gkvw = _wgrad(kvn, dkvb, "wgrad_ukv")
    big["w_ukv"] = _slots(jnp.concatenate([_unpad_heads_cols(gkvw[:, :NH * HP], QK_NOPE),
                                           _unpad_heads_cols(gkvw[:, NH * HP:], V_HEAD)], axis=2).reshape(KV_LORA, -1))
    big["w_out"] = g_out
    big["w_ff1"] = g_ff1
    big["w_ff2"] = g_ff2

    small = {}
    small["norm1_g"] = accg1[0:1]
    small["norm2_g"] = accg2[0:1]
    small["final_norm_g"] = accg2[1:2]
    small["ssm_out_g"] = accg3[0:1, :D_SSM]
    small["attn_out_g"] = accg3[1].reshape(NH, HP)[:, :V_HEAD].reshape(1, D_ATTN)
    small["q_norm_g"] = accm[0:1, :Q_LORA]
    small["kv_norm_g"] = accm[1:2, :KV_LORA]
    small["ssm_lambda_re"] = glr.reshape(G, P)
    small["ssm_lambda_im"] = gli.reshape(G, P)
    small["ssm_b_re"] = gb_re
    small["ssm_b_im"] = gb_im
    small["ssm_c_re"] = dc_re.reshape(G * H, P)
    small["ssm_c_im"] = dc_im.reshape(G * H, P)
    small["ssm_d"] = dd.reshape(G, H)
    small["ssm_log_dt"] = glogdt.reshape(1, G)
    return loss, dx.reshape(nb, S, D), big, small, accs1 + accs2 + accs3


def _view2d(a):
    return a.reshape(-1, a.shape[-1]) if a.ndim > 1 else a.reshape(1, -1)


def kernel(x, c, positions, ada_w, ada_b, norm1_g, w_in, ssm_lambda_re, ssm_lambda_im, ssm_b_re, ssm_b_im, ssm_c_re, ssm_c_im, ssm_d, ssm_log_dt, w_glu, q_norm_g, w_uq, kv_norm_g, w_ukv, ssm_out_g, attn_out_g, w_out, norm2_g, w_ff1, w_ff2, final_ada_w, final_ada_b, final_norm_g, loss_target, m_ada_w, m_ada_b, m_norm1_g, m_w_in, m_ssm_lambda_re, m_ssm_lambda_im, m_ssm_b_re, m_ssm_b_im, m_ssm_c_re, m_ssm_c_im, m_ssm_d, m_ssm_log_dt, m_w_glu, m_q_norm_g, m_w_uq, m_kv_norm_g, m_w_ukv, m_ssm_out_g, m_attn_out_g, m_w_out, m_norm2_g, m_w_ff1, m_w_ff2, m_final_ada_w, m_final_ada_b, m_final_norm_g, v_ada_w, v_ada_b, v_norm1_g, v_w_in, v_ssm_lambda_re, v_ssm_lambda_im, v_ssm_b_re, v_ssm_b_im, v_ssm_c_re, v_ssm_c_im, v_ssm_d, v_ssm_log_dt, v_w_glu, v_q_norm_g, v_w_uq, v_kv_norm_g, v_w_ukv, v_ssm_out_g, v_attn_out_g, v_w_out, v_norm2_g, v_w_ff1, v_w_ff2, v_final_ada_w, v_final_ada_b, v_final_norm_g):
    args = dict(locals())
    names = list(inspect.signature(kernel).parameters)
    wnames = names[3:names.index("loss_target")]
    small_names = [nm for nm in wnames if nm not in GATHERED and nm not in TP]
    reduced_names = [nm for nm in small_names if nm not in ("ada_b", "final_ada_b")]
    w = {nm: args[nm] for nm in wnames}
    m = {nm: args["m_" + nm] for nm in wnames}
    v = {nm: args["v_" + nm] for nm in wnames}
    nb = x.shape[0]
    xi, yi, ci = lax.axis_index("x"), lax.axis_index("y"), lax.axis_index("c")
    chip, me = 2 * xi + yi, 4 * xi + 2 * yi + ci

    unslot = lambda nm, g: g.reshape(-1, g.shape[-1]) if nm in ROW_SHARDED else _unslots(g)
    early = [nm for nm in GATHERED if nm not in LATE]
    got = _gather_chips("gather_weights", [_view2d(w[nm]).astype(BF16) for nm in early], [c])
    wf = {nm: unslot(nm, g) for nm, g in zip(early, got)}
    for nm in small_names:
        wf[nm] = w[nm][0] if w[nm].ndim > 1 else w[nm]
    c_all = got[len(early)].reshape(8 * nb, D)

    na, nf = ada_w.shape[-1], final_ada_w.shape[-1]
    ada_b_s = lax.dynamic_slice(ada_b, (0, chip * na), (1, na))
    fada_b_s = lax.dynamic_slice(final_ada_b.reshape(1, -1), (0, chip * nf), (1, nf))
    cond_all, modcols = _mod_fwd(c_all, ada_w[0], ada_b_s, final_ada_w, fada_b_s)
    (mod_g,) = _gather_chips("gather_mod", [modcols])
    mine = lax.dynamic_slice(mod_g, (0, me * nb, 0), (4, nb, na + nf))
    modp = jnp.concatenate([mine[:, :, :na].transpose(1, 0, 2).reshape(nb, 6, D),
                            mine[:, :, na:].transpose(1, 0, 2).reshape(nb, 2, D)], axis=1)

    own_late = [_view2d(w[nm]).astype(BF16) for nm in LATE]
    late_gather, token = _split_start("gather_late", own_late,
                                      [jax.ShapeDtypeStruct((4,) + a.shape, a.dtype) for a in own_late],
                                      3 * len(LATE), _plan_to_chips, modp)
    modp = modp + token[0, 0]

    def late_weights(after):
        sent, landed = _split_wait(late_gather, after)
        return {nm: unslot(nm, lax.dynamic_update_slice(g, own[None], (chip, 0, 0)))
                for nm, g, own in zip(LATE, landed, sent)}

    cidx = ci.astype(jnp.int32).reshape(1)
    ahead = ["w_ff1", "w_ff2", "w_out"]

    class Reducer:
        def start(self, gs):
            lands = [jax.ShapeDtypeStruct((4, g.shape[1] // 2, g.shape[2]), g.dtype) for g in gs]
            self.swap, tok = _split_start("grad_swap_ff", gs, lands, len(gs), _plan_swap_halves, modp)
            return tok

        def middle(self, after):
            gs, got = _split_wait(self.swap, after)
            sums = [_add_half(g, r, cidx, "grad_add_sibling_" + nm) for nm, g, r in zip(ahead, gs, got)]
            lands = [jax.ShapeDtypeStruct(s.shape, s.dtype) for s in sums]
            self.scatter, tok = _split_start("grad_scatter_ff", sums, lands, 3 * len(sums), _plan_scatter_chips, modp)
            return tok

        def finish(self, after):
            out = []
            for nm, s, l in zip(ahead, *_split_wait(self.scatter, after)):
                own = lax.dynamic_slice(s, (chip, 0, 0), (1,) + s.shape[1:])
                out.append(_add_chips(lax.dynamic_update_slice(l, own, (chip, 0, 0)), "grad_add_chips_" + nm))
            return out

    reducer = Reducer()
    loss_row, grad_x, big, small, dmodp = _local_step(x, positions, loss_target, modp, wf, late_weights, reducer)

    rest = [nm for nm in GATHERED if nm not in ahead]
    sizes = [small[nm].size for nm in reduced_names]
    pad = -sum(sizes) % 128
    packed = jnp.concatenate([small[nm].reshape(1, -1) for nm in reduced_names] + [jnp.zeros((1, pad), F32)],
                             axis=1).astype(BF16)
    swapped = _swap_halves([big[nm] for nm in rest], [dmodp.reshape(nb, 8 * D)], [packed, loss_row])
    chip_sums = [_add_half(big[nm], r, cidx, "grad_add_sibling_" + nm) for nm, r in zip(rest, swapped)]
    chip_small = _pair_sum(packed, swapped[len(rest) + 1], loss_row, swapped[len(rest) + 2])
    scattered = _scatter_chips(chip_sums, chip_small)
    half_of = {nm: _add_chips(r, "grad_add_chips_" + nm) for nm, r in zip(rest, scattered)}
    half_of.update(zip(ahead, reducer.finish(grad_x)))
    halves = [half_of[nm] for nm in GATHERED]
    others = _join_halves(halves)
    grads = {}
    dmod_all = swapped[len(rest)].reshape(8 * nb, 8 * D)
    small_sum, loss_sum = _sum_devices(scattered[len(rest)].reshape(4, -1), scattered[len(rest) + 1].reshape(4, -1))
    loss = jnp.sum(loss_sum)
    off = 0
    for nm, sz in zip(reduced_names, sizes):
        grads[nm] = small_sum[:, off:off + sz].reshape(small[nm].shape)
        off += sz

    dsl = jnp.concatenate([lax.dynamic_slice(dmod_all, (0, chip * na), (8 * nb, na)),
                           lax.dynamic_slice(dmod_all, (0, 6 * D + chip * nf), (8 * nb, nf))], axis=1)
    gw, gb = _mod_bwd(cond_all.T, dsl, dmod_all)
    grads["ada_w"], grads["final_ada_w"] = gw[:, :na], gw[:, na:]
    grads["ada_b"], grads["final_ada_b"] = gb[:, :6 * D], gb[:, 6 * D:]

    delta, new_m, new_v = {}, {}, {}
    for nm, mine_h, other_h in zip(GATHERED, halves, others):
        grads[nm], delta[nm], new_m[nm], new_v[nm] = _adamw_halves(
            _view2d(w[nm]), mine_h, other_h, _view2d(m[nm]), _view2d(v[nm]), cidx, "adamw_" + nm)
    for nm in TP:
        delta[nm], new_m[nm], new_v[nm] = _adamw(_view2d(w[nm]), grads[nm], _view2d(m[nm]), _view2d(v[nm]),
                                                  "adamw_" + nm)
    upd = _adamw_small([_view2d(w[nm]) for nm in small_names], [grads[nm] for nm in small_names],
                       [_view2d(m[nm]) for nm in small_names], [_view2d(v[nm]) for nm in small_names])
    k = len(small_names)
    for t, nm in enumerate(small_names):
        delta[nm], new_m[nm], new_v[nm] = upd[t], upd[k + t], upd[2 * k + t]

    outs = [grads, delta, new_m, new_v]
    return (loss, grad_x, *[d[nm].reshape(w[nm].shape) for d in outs for nm in wnames])
```

```python
import inspect
import math

import jax
import jax.numpy as jnp
from jax import lax
from jax.experimental import pallas as pl
from jax.experimental.pallas import tpu as pltpu

F32 = jnp.float32
BF16 = jnp.bfloat16

D = 1024
D_SSM = 512
G = 32
H = 16
P = 64
NST = G * P
D_ATTN = 512
NH = 8
QK_NOPE = 64
QK_ROPE = 32
V_HEAD = 64
HP = 128
Q_LORA = 384
KV_LORA = 256
IN_COLS = D_SSM + Q_LORA + KV_LORA + QK_ROPE
IN_PAD = 1280
D_FF = 4096
ROPE_BASE = 10000.0
EPS = 1e-6
ADAM_LR = 0.001
ADAM_B1 = 0.9
ADAM_B2 = 0.999
ADAM_EPS = 1e-08
ADAM_WD = 0.01
ADAM_STEP = 10
NEG = -1e30
VMEM_LIMIT = 60 << 20

MESH = pl.DeviceIdType.MESH
_VM = pl.BlockSpec(memory_space=pltpu.VMEM)
_ANY = pl.BlockSpec(memory_space=pl.ANY)

GATHERED = ["w_in", "w_glu", "w_uq", "w_ukv", "w_out", "w_ff1", "w_ff2"]
TP = ["ada_w", "final_ada_w"]
ROW_SHARDED = ("w_out", "w_ff2")
LATE = ["w_out", "w_ff1", "w_ff2"]


def _cp(sem=None, vmem=VMEM_LIMIT):
    kw = dict(vmem_limit_bytes=vmem)
    if sem is not None:
        kw["dimension_semantics"] = sem
    return pltpu.CompilerParams(**kw)


def _dot(a, b):
    return jnp.dot(a, b, preferred_element_type=F32)


def _dot_nt(a, b):
    return lax.dot_general(a, b, (((1,), (1,)), ((), ())), preferred_element_type=F32)


def _dot_tn(a, b):
    return lax.dot_general(a, b, (((0,), (0,)), ((), ())), preferred_element_type=F32)


def _rms(x, n):
    r = lax.rsqrt(jnp.sum(x * x, axis=-1, keepdims=True) * (1.0 / n) + EPS)
    return x * r, r


def _rms_bwd(dyg, xhat, r, n):
    return r * (dyg - xhat * (jnp.sum(dyg * xhat, axis=-1, keepdims=True) * (1.0 / n)))


def _sigmoid(x):
    return 1.0 / (1.0 + jnp.exp(-x))


_GK = math.sqrt(2.0 / math.pi)
_GC = 0.044715


def _gelu(y):
    t = jnp.tanh(_GK * (y + _GC * y * y * y))
    return 0.5 * y * (1.0 + t)


def _gelu_grad(y):
    t = jnp.tanh(_GK * (y + _GC * y * y * y))
    return 0.5 * (1.0 + t) + 0.5 * y * (1.0 - t * t) * _GK * (1.0 + 3.0 * _GC * y * y)


def _colsum(x):
    return jnp.sum(x, axis=0, keepdims=True)


def _roll(x, s):
    return pltpu.roll(x, s % x.shape[-1], x.ndim - 1)


def _mod_fwd(c_all, ada_w_s, ada_b_s, fada_w_s, fada_b_s):
    nseq = c_all.shape[0]
    na, nf = ada_w_s.shape[1], fada_w_s.shape[1]

    def body(c_ref, w_ref, b_ref, fw_ref, fb_ref, cond_ref, mod_ref):
        cv = c_ref[...]
        cond = cv * _sigmoid(cv)
        cond_ref[...] = cond
        cb = cond.astype(BF16)
        mod_ref[:, 0:na] = _dot(cb, w_ref[...].astype(BF16)) + b_ref[...]
        mod_ref[:, na:na + nf] = _dot(cb, fw_ref[...].astype(BF16)) + fb_ref[...]

    return pl.pallas_call(
        body, name="mod_fwd",
        out_shape=[jax.ShapeDtypeStruct((nseq, D), F32), jax.ShapeDtypeStruct((nseq, na + nf), F32)],
        in_specs=[_VM] * 5, out_specs=[_VM] * 2, compiler_params=_cp(),
    )(c_all, ada_w_s, ada_b_s, fada_w_s, fada_b_s)


def _mod_bwd(cond_t, dsl, dall):
    nseq, n = dsl.shape
    bc = 512

    def body(ct_ref, dm_ref, da_ref, gw_ref, gb_ref):
        ct = ct_ref[...]
        dm = dm_ref[...]
        acc = ct[:, 0:1] * dm[0:1, :]
        for b in range(1, nseq):
            acc = acc + ct[:, b:b + 1] * dm[b:b + 1, :]
        gw_ref[...] = acc

        @pl.when(pl.program_id(0) == 0)
        def _():
            gb_ref[...] = _colsum(da_ref[...])

    return pl.pallas_call(
        body, name="mod_bwd", grid=(n // bc,),
        out_shape=[jax.ShapeDtypeStruct((D, n), F32), jax.ShapeDtypeStruct((1, dall.shape[1]), F32)],
        in_specs=[_VM, pl.BlockSpec((nseq, bc), lambda i: (0, i)), _VM],
        out_specs=[pl.BlockSpec((D, bc), lambda i: (0, i)), pl.BlockSpec((1, dall.shape[1]), lambda i: (0, 0))],
        compiler_params=_cp(("arbitrary",)),
    )(cond_t, dsl, dall)


def _f1_fwd(x, modp, g1, w_in, rc, rs1, rs2, gq, gkv, w_uq, w_ukv, S, tm):
    n = x.shape[0]
    tps = S // tm
    LAT = IN_PAD - D_SSM

    def body(x_ref, mod_ref, g_ref, w_ref, c_ref, s1_ref, s2_ref, gq_ref, gkv_ref, wq_ref, wkv_ref,
             h_ref, u_ref, lat_ref, q_ref, k_ref, v_ref, qn_ref, kvn_ref):
        xhat, _ = _rms(x_ref[...], D)
        h = (xhat * g_ref[...]) * (1.0 + mod_ref[0, 1:2, :]) + mod_ref[0, 0:1, :]
        hb = h.astype(BF16)
        h_ref[...] = hb
        proj = _dot(hb, w_ref[...])
        u_ref[...] = proj[:, 0:D_SSM]
        lat_ref[...] = proj[:, D_SSM:IN_PAD]
        c, s1, s2 = c_ref[...], s1_ref[...], s2_ref[...]
        qhat, _ = _rms(proj[:, D_SSM:D_SSM + Q_LORA], Q_LORA)
        qn = (qhat * gq_ref[...]).astype(BF16)
        qn_ref[...] = qn
        q = _dot(qn, wq_ref[...])
        qr = _rope(q, jnp.tile(c, (1, NH)), jnp.tile(s1, (1, NH)), jnp.tile(s2, (1, NH)))
        q_ref[...] = (qr * _C2).astype(BF16)
        khat, _ = _rms(proj[:, D_SSM + Q_LORA:D_SSM + Q_LORA + KV_LORA], KV_LORA)
        kvn = (khat * gkv_ref[...]).astype(BF16)
        kvn_ref[...] = kvn
        kv = _dot(kvn, wkv_ref[...])
        kr = _rope(_roll(proj[:, IN_PAD - HP:IN_PAD], 64), c, s1, s2)
        k_ref[...] = (kv[:, 0:NH * HP] + jnp.tile(kr, (1, NH))).astype(BF16)
        vv = kv[:, NH * HP:2 * NH * HP]
        lane = lax.broadcasted_iota(jnp.int32, vv.shape, 1)
        v_ref[...] = jnp.where(lane % HP == V_HEAD, 1.0, vv).astype(BF16)

    row = lambda w: pl.BlockSpec((tm, w), lambda i: (i, 0))
    return pl.pallas_call(
        body, name="f1_fwd", grid=(n // tm,),
        out_shape=[jax.ShapeDtypeStruct((n, D), BF16), jax.ShapeDtypeStruct((n, D_SSM), F32),
                   jax.ShapeDtypeStruct((n, LAT), F32)] + [jax.ShapeDtypeStruct((n, NH * HP), BF16)] * 3 +
                  [jax.ShapeDtypeStruct((n, Q_LORA), BF16), jax.ShapeDtypeStruct((n, KV_LORA), BF16)],
        in_specs=[row(D), pl.BlockSpec((1, 8, D), lambda i: (i // tps, 0, 0)), _VM, _VM,
                  row(HP), row(HP), row(HP), _VM, _VM, _VM, _VM],
        out_specs=[row(D), row(D_SSM), row(LAT)] + [row(NH * HP)] * 3 + [row(Q_LORA), row(KV_LORA)],
        compiler_params=_cp(("parallel",)),
    )(x, modp, g1, w_in, rc, rs1, rs2, gq, gkv, w_uq, w_ukv)


def _f1_bwd(du, dq, dk, dv, lat, rc, rs1, rs2, gq, gkv, w_uq, w_ukv, dx1, x, modp, g1, w_in, S, tm):
    n = x.shape[0]
    tps = S // tm
    nb = n // S

    def body(du_ref, dq_ref, dk_ref, dv_ref, lat_ref, c_ref, s1_ref, s2_ref, gq_ref, gkv_ref, wq_ref, wkv_ref,
             dx1_ref, x_ref, mod_ref, g_ref, w_ref,
             dx_ref, dproj_ref, dqb_ref, dkvb_ref, accs_ref, accg_ref, accm_ref):
        i = pl.program_id(0)
        c, s1, s2 = c_ref[...], s1_ref[...], s2_ref[...]
        dqu = _rope_t(dq_ref[...] * _SCALE, jnp.tile(c, (1, NH)), jnp.tile(s1, (1, NH)),
                      jnp.tile(s2, (1, NH))).astype(BF16)
        dqb_ref[...] = dqu
        dqn = _dot_nt(dqu, wq_ref[...])
        qhat, rq = _rms(lat_ref[:, 0:Q_LORA], Q_LORA)
        dql = _rms_bwd(dqn * gq_ref[...], qhat, rq, Q_LORA)
        dkf = dk_ref[...] * (1.0 / _LOG2E)
        dkv = jnp.concatenate([dkf.astype(BF16), dv_ref[...].astype(BF16)], axis=1)
        dkvb_ref[...] = dkv
        dkvn = _dot_nt(dkv, wkv_ref[...])
        khat, rk = _rms(lat_ref[:, Q_LORA:Q_LORA + KV_LORA], KV_LORA)
        dkvl = _rms_bwd(dkvn * gkv_ref[...], khat, rk, KV_LORA)
        dkr = dkf[:, 0:HP]
        for h in range(1, NH):
            dkr = dkr + dkf[:, h * HP:(h + 1) * HP]
        lane = lax.broadcasted_iota(jnp.int32, dkr.shape, 1)
        dkr = jnp.where((lane >= QK_NOPE) & (lane < QK_NOPE + QK_ROPE), dkr, 0.0)
        dkr = _roll(_rope_t(dkr, c, s1, s2), -64)

        @pl.when(i == 0)
        def _():
            accm_ref[...] = jnp.zeros_like(accm_ref)

        accm_ref[0:1, 0:Q_LORA] += _colsum(dqn * qhat)
        accm_ref[1:2, 0:KV_LORA] += _colsum(dkvn * khat)

        dproj = jnp.concatenate([du_ref[...], dql, dkvl, dkr], axis=1).astype(BF16)
        dproj_ref[...] = dproj
        dh = _dot_nt(dproj, w_ref[...])
        xhat, r = _rms(x_ref[...], D)
        g = g_ref[...]
        dn = dh * (1.0 + mod_ref[0, 1:2, :])
        dx_ref[...] = dx1_ref[...] + _rms_bwd(dn * g, xhat, r, D)

        @pl.when(i % tps == 0)
        def _():
            accs_ref[...] = jnp.zeros_like(accs_ref)

        @pl.when(i == 0)
        def _():
            accg_ref[...] = jnp.zeros_like(accg_ref)

        accs_ref[0, 0:1, :] += _colsum(dh)
        accs_ref[0, 1:2, :] += _colsum(dh * (xhat * g))
        accg_ref[0:1, :] += _colsum(dn * xhat)

    row = lambda w: pl.BlockSpec((tm, w), lambda i: (i, 0))
    return pl.pallas_call(
        body, name="f1_bwd", grid=(n // tm,),
        out_shape=[jax.ShapeDtypeStruct((n, D), F32), jax.ShapeDtypeStruct((n, IN_PAD), BF16),
                   jax.ShapeDtypeStruct((n, NH * HP), BF16), jax.ShapeDtypeStruct((n, 2 * NH * HP), BF16),
                   jax.ShapeDtypeStruct((nb, 8, D), F32), jax.ShapeDtypeStruct((8, D), F32),
                   jax.ShapeDtypeStruct((8, Q_LORA), F32)],
        in_specs=[row(D_SSM)] + [row(NH * HP)] * 3 + [row(IN_PAD - D_SSM), row(HP), row(HP), row(HP),
                                                     _VM, _VM, _VM, _VM, row(D), row(D),
                                                     pl.BlockSpec((1, 8, D), lambda i: (i // tps, 0, 0)), _VM, _VM],
        out_specs=[row(D), row(IN_PAD), row(NH * HP), row(2 * NH * HP),
                   pl.BlockSpec((1, 8, D), lambda i: (i // tps, 0, 0)), pl.BlockSpec((8, D), lambda i: (0, 0)),
                   pl.BlockSpec((8, Q_LORA), lambda i: (0, 0))],
        compiler_params=_cp(("arbitrary",)),
    )(du, dq, dk, dv, lat, rc, rs1, rs2, gq, gkv, w_uq, w_ukv, dx1, x, modp, g1, w_in)


def _ssm_param_fwd(lam_re, lam_im, logdt, b_re, b_im):
    def body(lr_ref, li_ref, ld_ref, br_ref, bi_ref, lbr_ref, lbi_ref, bbr_ref, bbi_ref):
        lr, li = lr_ref[...], li_ref[...]
        dt = jnp.exp(ld_ref[...])
        er = jnp.exp(lr * dt)
        lbr = er * jnp.cos(li * dt)
        lbi = er * jnp.sin(li * dt)
        den = 1.0 / (lr * lr + li * li)
        cr = ((lbr - 1.0) * lr + lbi * li) * den
        ci = (lbi * lr - (lbr - 1.0) * li) * den
        lbr_ref[...] = lbr
        lbi_ref[...] = lbi
        bbr_ref[...] = cr * br_ref[...] - ci * bi_ref[...]
        bbi_ref[...] = cr * bi_ref[...] + ci * br_ref[...]

    return pl.pallas_call(
        body, name="ssm_param_fwd",
        out_shape=[jax.ShapeDtypeStruct((NST, 1), F32)] * 2 + [jax.ShapeDtypeStruct((NST, H), F32)] * 2,
        in_specs=[_VM] * 5, out_specs=[_VM] * 4, compiler_params=_cp(),
    )(lam_re, lam_im, logdt, b_re, b_im)


def _ssm_param_bwd(lam_re, lam_im, logdt, b_re, b_im, dlb_re, dlb_im, dbb_re, dbb_im):
    def body(lr_ref, li_ref, ld_ref, br_ref, bi_ref, dlr_ref, dli_ref, dbr_ref, dbi_ref,
             gbr_ref, gbi_ref, glr_ref, gli_ref, gdt_ref):
        lr, li = lr_ref[...], li_ref[...]
        dt = jnp.exp(ld_ref[...])
        er = jnp.exp(lr * dt)
        lbr = er * jnp.cos(li * dt)
        lbi = er * jnp.sin(li * dt)
        den = 1.0 / (lr * lr + li * li)
        nr, ni = lbr - 1.0, lbi
        cr = (nr * lr + ni * li) * den
        ci = (ni * lr - nr * li) * den
        br, bi = br_ref[...], bi_ref[...]
        dbr, dbi = dbr_ref[...], dbi_ref[...]
        gbr_ref[...] = cr * dbr + ci * dbi
        gbi_ref[...] = cr * dbi - ci * dbr
        gcr = jnp.sum(dbr * br + dbi * bi, axis=1, keepdims=True)
        gci = jnp.sum(dbi * br - dbr * bi, axis=1, keepdims=True)
        ilr, ili = lr * den, -li * den
        glbr = dlr_ref[...] + (gcr * ilr + gci * ili)
        glbi = dli_ref[...] + (gci * ilr - gcr * ili)
        qr = -(cr * ilr - ci * ili)
        qi = -(cr * ili + ci * ilr)
        glr = gcr * qr + gci * qi
        gli = gci * qr - gcr * qi
        glr = glr + dt * (glbr * lbr + glbi * lbi)
        gli = gli + dt * (glbi * lbr - glbr * lbi)
        wr = lr * lbr - li * lbi
        wi = lr * lbi + li * lbr
        glr_ref[...] = glr
        gli_ref[...] = gli
        gdt_ref[...] = (glbr * wr + glbi * wi) * dt

    return pl.pallas_call(
        body, name="ssm_param_bwd",
        out_shape=[jax.ShapeDtypeStruct((NST, H), F32)] * 2 + [jax.ShapeDtypeStruct((NST, 1), F32)] * 3,
        in_specs=[_VM] * 9, out_specs=[_VM] * 5, compiler_params=_cp(),
    )(lam_re, lam_im, logdt, b_re, b_im, dlb_re, dlb_im, dbb_re, dbb_im)


def _rowsum(a):
    def body(a_ref, o_ref):
        o_ref[...] = jnp.sum(a_ref[...], axis=1, keepdims=True)

    return pl.pallas_call(
        body, name="rowsum", out_shape=jax.ShapeDtypeStruct((a.shape[0], 1), F32),
        in_specs=[_VM], out_specs=_VM, compiler_params=_cp(),
    )(a)


QB = D_SSM // 4
QS = 4 * QB


def _bd_lo(part, q):
    return part * NST + q * QS


def _bd_expand(ub, bm_ref, out_ref):
    for part in range(2):
        for q in range(4):
            lo = _bd_lo(part, q)
            out_ref[:, lo:lo + QS] = _dot(ub[:, q * QB:(q + 1) * QB], bm_ref[:, lo:lo + QS])


def _bd_expand_t(db, cm_ref, out_ref):
    for part in range(2):
        for q in range(4):
            lo = _bd_lo(part, q)
            out_ref[:, lo:lo + QS] = _dot_nt(db[:, q * QB:(q + 1) * QB], cm_ref[lo:lo + QS, :])


def _bd_project(sb, cm_ref):
    return jnp.concatenate(
        [_dot(sb[:, _bd_lo(0, q):_bd_lo(0, q) + QS], cm_ref[_bd_lo(0, q):_bd_lo(0, q) + QS, :])
         + _dot(sb[:, _bd_lo(1, q):_bd_lo(1, q) + QS], cm_ref[_bd_lo(1, q):_bd_lo(1, q) + QS, :])
         for q in range(4)], axis=1)


def _bd_project_t(ab, bm_ref):
    return jnp.concatenate(
        [_dot_nt(ab[:, _bd_lo(0, q):_bd_lo(0, q) + QS], bm_ref[:, _bd_lo(0, q):_bd_lo(0, q) + QS])
         + _dot_nt(ab[:, _bd_lo(1, q):_bd_lo(1, q) + QS], bm_ref[:, _bd_lo(1, q):_bd_lo(1, q) + QS])
         for q in range(4)], axis=1)


def _pow2k(pr, pi, nsq):
    for _ in range(nsq):
        pr, pi = pr * pr - pi * pi, 2.0 * pr * pi
    return pr, pi


def _ssm_local(u_p, bm, lre8, lim8, S, tt):
    n = u_p.shape[0]
    nb, nt = n // S, S // tt
    nsq = int(round(math.log2(S // 8)))
    assert 2 ** nsq == S // 8

    def body(u_ref, bm_ref, lre_ref, lim_ref, cre_ref, cim_ref, sre, sim, bu):
        j = pl.program_id(1)

        @pl.when(j == 0)
        def _():
            sre[...] = jnp.zeros_like(sre)
            sim[...] = jnp.zeros_like(sim)

        _bd_expand(u_ref[...].astype(BF16), bm_ref, bu)
        lre, lim = lre_ref[...], lim_ref[...]

        def step(i, c):
            sr, si = c
            off = pl.multiple_of(i * 8, 8)
            br = bu[pl.ds(off, 8), 0:NST]
            bi = bu[pl.ds(off, 8), NST:2 * NST]
            return lre * sr - lim * si + br, lre * si + lim * sr + bi

        sr, si = lax.fori_loop(0, tt // 8, step, (sre[...], sim[...]))
        sre[...] = sr
        sim[...] = si

        @pl.when(j == nt - 1)
        def _():
            pr, pi = _pow2k(lre[0:1], lim[0:1], nsq)
            cr = jnp.zeros((1, NST), F32)
            ci = jnp.zeros((1, NST), F32)
            cre_ref[0:1, :] = cr
            cim_ref[0:1, :] = ci
            for k in range(1, 8):
                cr, ci = sr[k - 1:k] + pr * cr - pi * ci, si[k - 1:k] + pr * ci + pi * cr
                cre_ref[k:k + 1, :] = cr
                cim_ref[k:k + 1, :] = ci

    return pl.pallas_call(
        body, name="ssm_local", grid=(nb, nt),
        out_shape=[jax.ShapeDtypeStruct((nb * 8, NST), F32)] * 2,
        in_specs=[pl.BlockSpec((tt, D_SSM), lambda b, j: (b * nt + j, 0)), _VM, _VM, _VM],
        out_specs=[pl.BlockSpec((8, NST), lambda b, j: (b, 0))] * 2,
        scratch_shapes=[pltpu.VMEM((8, NST), F32), pltpu.VMEM((8, NST), F32), pltpu.VMEM((tt, 2 * NST), F32)],
        compiler_params=_cp(("arbitrary", "arbitrary")),
    )(u_p, bm, lre8, lim8)


def _ssm_fwd(u_p, cre, cim, bm, cm, dvec, w_glu, lre8, lim8, S, tt):
    n = u_p.shape[0]
    nb, nt = n // S, S // tt

    def body(u_ref, cre_ref, cim_ref, bm_ref, cm_ref, d_ref, wg_ref, lre_ref, lim_ref,
             st_ref, ypre_ref, z_ref, gact_ref, yssm_ref, sre, sim, bu):
        j = pl.program_id(1)

        @pl.when(j == 0)
        def _():
            sre[...] = cre_ref[...]
            sim[...] = cim_ref[...]

        u = u_ref[...]
        _bd_expand(u.astype(BF16), bm_ref, bu)
        lre, lim = lre_ref[...], lim_ref[...]

        def step(i, c):
            sr, si = c
            off = pl.multiple_of(i * 8, 8)
            nr = lre * sr - lim * si + bu[pl.ds(off, 8), 0:NST]
            ni = lre * si + lim * sr + bu[pl.ds(off, 8), NST:2 * NST]
            bu[pl.ds(off, 8), 0:NST] = nr
            bu[pl.ds(off, 8), NST:2 * NST] = ni
            return nr, ni

        sr, si = lax.fori_loop(0, tt // 8, step, (sre[...], sim[...]))
        sre[...] = sr
        sim[...] = si
        stb = bu[...].astype(BF16)
        st_ref[...] = stb
        y = _bd_project(stb, cm_ref) + d_ref[...] * u
        ypre_ref[...] = y
        gb = _gelu(y).astype(BF16)
        gact_ref[...] = gb
        z = _dot(gb, wg_ref[...])
        z_ref[...] = z
        yssm_ref[...] = z[:, 0:D_SSM] * _sigmoid(z[:, D_SSM:2 * D_SSM])

    row = lambda w: pl.BlockSpec((tt, w), lambda b, j: (b * nt + j, 0))
    return pl.pallas_call(
        body, name="ssm_fwd", grid=(nb, nt),
        out_shape=[jax.ShapeDtypeStruct((n, 2 * NST), BF16), jax.ShapeDtypeStruct((n, D_SSM), F32),
                   jax.ShapeDtypeStruct((n, 2 * D_SSM), F32), jax.ShapeDtypeStruct((n, D_SSM), BF16),
                   jax.ShapeDtypeStruct((n, D_SSM), F32)],
        in_specs=[row(D_SSM), pl.BlockSpec((8, NST), lambda b, j: (b, 0)), pl.BlockSpec((8, NST), lambda b, j: (b, 0)),
                  _VM, _VM, _VM, _VM, _VM, _VM],
        out_specs=[row(2 * NST), row(D_SSM), row(2 * D_SSM), row(D_SSM), row(D_SSM)],
        scratch_shapes=[pltpu.VMEM((8, NST), F32), pltpu.VMEM((8, NST), F32), pltpu.VMEM((tt, 2 * NST), F32)],
        compiler_params=_cp(("arbitrary", "arbitrary")),
    )(u_p, cre, cim, bm, cm, dvec, w_glu, lre8, lim8)


def _ssm_bwd_a(dys_p, z, ypre, w_glu, cm, lre8, lim8, S, tt):
    n = z.shape[0]
    nb, nt = n // S, S // tt
    nsq = int(round(math.log2(S // 8)))
    ng = tt // 8

    def body(dys_ref, z_ref, y_ref, wg_ref, cm_ref, lre_ref, lim_ref, dy_ref, dz_ref, are_ref, aim_ref, sre, sim, gb):
        j = pl.program_id(1)

        @pl.when(j == 0)
        def _():
            sre[...] = jnp.zeros_like(sre)
            sim[...] = jnp.zeros_like(sim)

        z = z_ref[...]
        z1, z2 = z[:, 0:D_SSM], z[:, D_SSM:2 * D_SSM]
        sg = _sigmoid(z2)
        dys = dys_ref[...]
        dz = jnp.concatenate([dys * sg, dys * z1 * sg * (1.0 - sg)], axis=1).astype(BF16)
        dz_ref[...] = dz
        dy = _dot_nt(dz, wg_ref[...]) * _gelu_grad(y_ref[...])
        dy_ref[...] = dy
        _bd_expand_t(dy.astype(BF16), cm_ref, gb)
        lre, lim = lre_ref[...], lim_ref[...]

        def step(i, c):
            ar, ai = c
            off = pl.multiple_of((ng - 1 - i) * 8, 8)
            gr = gb[pl.ds(off, 8), 0:NST]
            gi = gb[pl.ds(off, 8), NST:2 * NST]
            return lre * ar + lim * ai + gr, lre * ai - lim * ar + gi

        ar, ai = lax.fori_loop(0, ng, step, (sre[...], sim[...]))
        sre[...] = ar
        sim[...] = ai

        @pl.when(j == nt - 1)
        def _():
            pr, pi = _pow2k(lre[0:1], -lim[0:1], nsq)
            cr = jnp.zeros((1, NST), F32)
            ci = jnp.zeros((1, NST), F32)
            are_ref[7:8, :] = cr
            aim_ref[7:8, :] = ci
            for k in range(6, -1, -1):
                cr, ci = ar[k + 1:k + 2] + pr * cr - pi * ci, ai[k + 1:k + 2] + pr * ci + pi * cr
                are_ref[k:k + 1, :] = cr
                aim_ref[k:k + 1, :] = ci

    row = lambda w: pl.BlockSpec((tt, w), lambda b, j: (b * nt + nt - 1 - j, 0))
    return pl.pallas_call(
        body, name="ssm_bwd_a", grid=(nb, nt),
        out_shape=[jax.ShapeDtypeStruct((n, D_SSM), F32), jax.ShapeDtypeStruct((n, 2 * D_SSM), BF16),
                   jax.ShapeDtypeStruct((nb * 8, NST), F32), jax.ShapeDtypeStruct((nb * 8, NST), F32)],
        in_specs=[row(D_SSM), row(2 * D_SSM), row(D_SSM), _VM, _VM, _VM, _VM],
        out_specs=[row(D_SSM), row(2 * D_SSM), pl.BlockSpec((8, NST), lambda b, j: (b, 0)),
                   pl.BlockSpec((8, NST), lambda b, j: (b, 0))],
        scratch_shapes=[pltpu.VMEM((8, NST), F32), pltpu.VMEM((8, NST), F32), pltpu.VMEM((tt, 2 * NST), F32)],
        compiler_params=_cp(("arbitrary", "arbitrary")),
    )(dys_p, z, ypre, w_glu, cm, lre8, lim8)


def _ssm_bwd_b(dy, u_p, st, fcr, fci, air, aii, bm, cm, dvec, lre8, lim8, S, tt):
    n = u_p.shape[0]
    nb, nt = n // S, S // tt
    ng = tt // 8

    def body(dy_ref, u_ref, st_ref, stp_ref, fcr_ref, fci_ref, air_ref, aii_ref, bm_ref, cm_ref, d_ref, lre_ref, lim_ref,
             du_ref, dcm_ref, dbm_ref, dd_ref, dlr_ref, dli_ref, are, aim, accr, acci, sp, ab):
        b = pl.program_id(0)
        j = pl.program_id(1)
        jt = nt - 1 - j

        @pl.when((b == 0) & (j == 0))
        def _():
            dcm_ref[...] = jnp.zeros_like(dcm_ref)
            dbm_ref[...] = jnp.zeros_like(dbm_ref)
            dd_ref[...] = jnp.zeros_like(dd_ref)
            accr[...] = jnp.zeros_like(accr)
            acci[...] = jnp.zeros_like(acci)

        @pl.when(j == 0)
        def _():
            are[...] = air_ref[...]
            aim[...] = aii_ref[...]

        sp[8:tt + 8, :] = st_ref[...].astype(F32)

        @pl.when(jt == 0)
        def _():
            sp[0:8, 0:NST] = fcr_ref[...]
            sp[0:8, NST:2 * NST] = fci_ref[...]

        @pl.when(jt != 0)
        def _():
            sp[0:8, :] = stp_ref[8:16, :].astype(F32)

        dy = dy_ref[...]
        u = u_ref[...]
        dyb = dy.astype(BF16)
        _bd_expand_t(dyb, cm_ref, ab)
        lre, lim = lre_ref[...], lim_ref[...]

        def step(i, c):
            ar, ai = c
            off = pl.multiple_of((ng - 1 - i) * 8, 8)
            nr = lre * ar + lim * ai + ab[pl.ds(off, 8), 0:NST]
            ni = lre * ai - lim * ar + ab[pl.ds(off, 8), NST:2 * NST]
            ab[pl.ds(off, 8), 0:NST] = nr
            ab[pl.ds(off, 8), NST:2 * NST] = ni
            pr = sp[pl.ds(off, 8), 0:NST]
            pi = sp[pl.ds(off, 8), NST:2 * NST]
            accr[...] += nr * pr + ni * pi
            acci[...] += ni * pr - nr * pi
            return nr, ni

        ar, ai = lax.fori_loop(0, ng, step, (are[...], aim[...]))
        are[...] = ar
        aim[...] = ai
        a_b = ab[...].astype(BF16)
        du_ref[...] = _bd_project_t(a_b, bm_ref) + d_ref[...] * dy
        ub = u.astype(BF16)
        for q in range(4):
            for part in range(2):
                lo = part * NST + q * 4 * QB
                s_q = st_ref[:, lo:lo + 4 * QB]
                dcm_ref[lo:lo + 4 * QB, :] += _dot_tn(s_q, dyb[:, q * QB:(q + 1) * QB])
                dbm_ref[:, lo:lo + 4 * QB] += _dot_tn(ub[:, q * QB:(q + 1) * QB], a_b[:, lo:lo + 4 * QB])
        dd_ref[...] += _colsum(dy * u)

        @pl.when((b == nb - 1) & (j == nt - 1))
        def _():
            dlr_ref[...] = _colsum(accr[...])
            dli_ref[...] = _colsum(acci[...])

    row = lambda w: pl.BlockSpec((tt, w), lambda b, j: (b * nt + nt - 1 - j, 0))
    seq8 = pl.BlockSpec((8, NST), lambda b, j: (b, 0))
    prev = pl.BlockSpec((16, 2 * NST), lambda b, j: (jnp.maximum((b * nt + nt - 1 - j) * (tt // 16) - 1, 0), 0))
    const = lambda shape: pl.BlockSpec(shape, lambda b, j: (0, 0))
    return pl.pallas_call(
        body, name="ssm_bwd_b", grid=(nb, nt),
        out_shape=[jax.ShapeDtypeStruct((n, D_SSM), F32), jax.ShapeDtypeStruct((2 * NST, QB), F32),
                   jax.ShapeDtypeStruct((QB, 2 * NST), F32), jax.ShapeDtypeStruct((1, D_SSM), F32),
                   jax.ShapeDtypeStruct((1, NST), F32), jax.ShapeDtypeStruct((1, NST), F32)],
        in_specs=[row(D_SSM), row(D_SSM), row(2 * NST), prev, seq8, seq8, seq8, seq8, _VM, _VM, _VM, _VM, _VM],
        out_specs=[row(D_SSM), const((2 * NST, QB)), const((QB, 2 * NST)), const((1, D_SSM)),
                   const((1, NST)), const((1, NST))],
        scratch_shapes=[pltpu.VMEM((8, NST), F32)] * 4 + [pltpu.VMEM((tt + 8, 2 * NST), F32),
                                                          pltpu.VMEM((tt, 2 * NST), F32)],
        compiler_params=_cp(("arbitrary", "arbitrary")),
    )(dy, u_p, st, st, fcr, fci, air, aii, bm, cm, dvec, lre8, lim8)


def _rope(v, c, s1, s2):
    return v * c + _roll(v, -16) * s1 + _roll(v, 16) * s2


def _rope_t(dv, c, s1, s2):
    return dv * c + _roll(dv * s1, 16) + _roll(dv * s2, -16)


_SCALE = (QK_NOPE + QK_ROPE) ** -0.5
_LOG2E = 1.4426950408889634
_C2 = _SCALE * _LOG2E


def _attn_fwd(q, k, v, S, tq):
    n = q.shape[0]
    nb, nq = n // S, S // tq

    def body(q_ref, k_ref, v_ref, o_ref, lr_ref):
        qi = pl.program_id(2)
        qv = q_ref[...]

        def tile(j, c, diagonal):
            m, acc = c
            off = pl.multiple_of(j * tq, tq)
            s = _dot_nt(qv, k_ref[pl.ds(off, tq), :])
            if diagonal:
                rows = lax.broadcasted_iota(jnp.int32, s.shape, 0)
                cols = lax.broadcasted_iota(jnp.int32, s.shape, 1)
                s = jnp.where(cols <= rows, s, NEG)
            mn = jnp.maximum(m, jnp.max(s, axis=1, keepdims=True))
            p = jnp.exp2(s - mn)
            acc = jnp.exp2(m - mn) * acc + _dot(p.astype(BF16), v_ref[pl.ds(off, tq), :])
            return mn, acc

        init = (jnp.full((tq, 1), NEG, F32), jnp.zeros((tq, HP), F32))
        c = lax.fori_loop(0, qi, lambda j, c: tile(j, c, False), init)
        m, acc = tile(qi, c, True)
        l = acc[:, V_HEAD:V_HEAD + 1]
        vlane = lax.broadcasted_iota(jnp.int32, acc.shape, 1)
        o_ref[...] = jnp.where(vlane < V_HEAD, acc / l, 0.0).astype(BF16)
        lane = lax.broadcasted_iota(jnp.int32, (8, HP), 1)
        lse = jnp.broadcast_to(m + jnp.log(l) * _LOG2E, (tq, HP))
        lr_ref[...] = _rows_of(lse, jnp.where(lane == 0, 1.0, 0.0).astype(BF16))

    qs = pl.BlockSpec((tq, HP), lambda b, h, i: (b * nq + i, h))
    ks = pl.BlockSpec((S, HP), lambda b, h, i: (b, h))
    return pl.pallas_call(
        body, name="attn_fwd", grid=(nb, NH, nq),
        out_shape=[jax.ShapeDtypeStruct((n, NH * HP), BF16), jax.ShapeDtypeStruct((nb * NH * 8, S), F32)],
        in_specs=[qs, ks, ks], out_specs=[qs, pl.BlockSpec((8, tq), lambda b, h, i: (b * NH + h, i))],
        compiler_params=_cp(("parallel", "parallel", "arbitrary")),
    )(q, k, v)


def _rows_of(x, pick):
    x1 = x.astype(BF16)
    r1 = x - x1.astype(F32)
    x2 = r1.astype(BF16)
    x3 = (r1 - x2.astype(F32)).astype(BF16)
    return _dot_nt(pick, x1) + _dot_nt(pick, x2) + _dot_nt(pick, x3)


def _attn_bwd(q, k, v, dob, lrow, drow, S, tq):
    n = q.shape[0]
    nb, nq = n // S, S // tq

    def body(q_ref, k_ref, v_ref, do_ref, lr_ref, dr_ref, dqo_ref, dk_ref, dv_ref, dq_ref):
        kj = pl.program_id(2)

        @pl.when(kj == 0)
        def _():
            dq_ref[...] = jnp.zeros_like(dq_ref)

        kt = k_ref[...]
        vt = v_ref[...]

        def tile(i, c, diagonal):
            dk, dv = c
            off = pl.multiple_of(i * tq, tq)
            qv = q_ref[pl.ds(off, tq), :]
            dob = do_ref[pl.ds(off, tq), :]
            lr = lr_ref[0:1, pl.ds(off, tq)]
            dr = dr_ref[0:1, pl.ds(off, tq)]
            st = _dot_nt(kt, qv)
            dpt = _dot_nt(vt, dob)
            pt = jnp.exp2(st - lr)
            if diagonal:
                keys = lax.broadcasted_iota(jnp.int32, pt.shape, 0)
                qrys = lax.broadcasted_iota(jnp.int32, pt.shape, 1)
                pt = jnp.where(keys <= qrys, pt, 0.0)
            dst = (pt * (dpt - dr)).astype(BF16)
            dq_ref[pl.ds(off, tq), :] += _dot_tn(dst, kt)
            return dk + _dot(dst, qv), dv + _dot(pt.astype(BF16), dob)

        zero = jnp.zeros((tq, HP), F32)
        c = tile(kj, (zero, zero), True)
        dk, dv = lax.fori_loop(kj + 1, nq, lambda i, c: tile(i, c, False), c)
        dk_ref[...] = dk.astype(BF16)
        dv_ref[...] = dv.astype(BF16)

        @pl.when(kj == nq - 1)
        def _():
            dqo_ref[...] = dq_ref[...].astype(BF16)

    ts = pl.BlockSpec((tq, HP), lambda b, h, i: (b * nq + i, h))
    fs = pl.BlockSpec((S, HP), lambda b, h, i: (b, h))
    rs = pl.BlockSpec((8, S), lambda b, h, i: (b * NH + h, 0))
    return pl.pallas_call(
        body, name="attn_bwd", grid=(nb, NH, nq),
        out_shape=[jax.ShapeDtypeStruct((n, NH * HP), BF16)] * 3,
        in_specs=[fs, ts, ts, fs, rs, rs], out_specs=[fs, ts, ts],
        scratch_shapes=[pltpu.VMEM((S, HP), F32)],
        compiler_params=_cp(("parallel", "parallel", "arbitrary")),
    )(q, k, v, dob, lrow, drow)


def _p1_fwd(yssm, oattn, x, modp, gs, ga, w_out, g2, S, tm):
    n = x.shape[0]
    tps = S // tm

    def body(ys_ref, oa_ref, x_ref, mod_ref, gs_ref, ga_ref, w_ref, g2_ref, yn_ref, o_ref, x1_ref, h2_ref):
        yh, _ = _rms(ys_ref[...], D_SSM)
        ah, _ = _rms(oa_ref[...].astype(F32), D_ATTN)
        yn = jnp.concatenate([yh * gs_ref[...], ah * ga_ref[...]], axis=1).astype(BF16)
        yn_ref[...] = yn
        o = _dot(yn, w_ref[...])
        o_ref[...] = o.astype(BF16)
        x1 = x_ref[...] + mod_ref[0, 2:3, :] * o
        x1_ref[...] = x1
        xh, _ = _rms(x1, D)
        h2_ref[...] = ((xh * g2_ref[...]) * (1.0 + mod_ref[0, 4:5, :]) + mod_ref[0, 3:4, :]).astype(BF16)

    row = lambda w: pl.BlockSpec((tm, w), lambda i: (i, 0))
    return pl.pallas_call(
        body, name="p1_fwd", grid=(n // tm,),
        out_shape=[jax.ShapeDtypeStruct((n, D_SSM + NH * HP), BF16), jax.ShapeDtypeStruct((n, D), BF16),
                   jax.ShapeDtypeStruct((n, D), F32), jax.ShapeDtypeStruct((n, D), BF16)],
        in_specs=[row(D_SSM), row(NH * HP), row(D), pl.BlockSpec((1, 8, D), lambda i: (i // tps, 0, 0)),
                  _VM, _VM, _VM, _VM],
        out_specs=[row(D_SSM + NH * HP), row(D), row(D), row(D)],
        compiler_params=_cp(("parallel",)),
    )(yssm, oattn, x, modp, gs, ga, w_out, g2)


def _p2(x1, h2, target, modp, g2, gf, w_ff1, w_ff2, S, tm):
    n = x1.shape[0]
    tps = S // tm
    nb = n // S

    def body(x1_ref, h2_ref, t_ref, mod_ref, g2_ref, gf_ref, w1_ref, w2_ref,
             dx1_ref, r_ref, da_ref, dff_ref, accs_ref, accg_ref):
        i = pl.program_id(0)
        sh2, sc2, gt2 = mod_ref[0, 3:4, :], mod_ref[0, 4:5, :], mod_ref[0, 5:6, :]
        fsh, fsc = mod_ref[0, 6:7, :], mod_ref[0, 7:8, :]
        x1 = x1_ref[...]
        a = _dot(h2_ref[...], w1_ref[...])
        ra = jnp.maximum(a, 0.0)
        rb = (ra * ra).astype(BF16)
        r_ref[...] = rb
        ff = _dot(rb, w2_ref[...])
        x2 = x1 + gt2 * ff
        x2h, rf = _rms(x2, D)
        gf_v = gf_ref[...]
        outn = x2h * gf_v
        err = outn * (1.0 + fsc) + fsh - t_ref[...]
        dout = err * (1.0 / D)
        doutn = dout * (1.0 + fsc)
        dx2 = _rms_bwd(doutn * gf_v, x2h, rf, D)
        dff = (gt2 * dx2).astype(BF16)
        dff_ref[...] = dff
        dr = _dot_nt(dff, w2_ref[...])
        da = (dr * (2.0 * ra)).astype(BF16)
        da_ref[...] = da
        dh2 = _dot_nt(da, w1_ref[...])
        x1h, r2 = _rms(x1, D)
        g2_v = g2_ref[...]
        dn2 = dh2 * (1.0 + sc2)
        dx1_ref[...] = dx2 + _rms_bwd(dn2 * g2_v, x1h, r2, D)

        @pl.when(i % tps == 0)
        def _():
            accs_ref[...] = jnp.zeros_like(accs_ref)

        @pl.when(i == 0)
        def _():
            accg_ref[...] = jnp.zeros_like(accg_ref)

        accs_ref[0, 3:4, :] += _colsum(dh2)
        accs_ref[0, 4:5, :] += _colsum(dh2 * (x1h * g2_v))
        accs_ref[0, 5:6, :] += _colsum(dx2 * ff)
        accs_ref[0, 6:7, :] += _colsum(dout)
        accs_ref[0, 7:8, :] += _colsum(dout * outn)
        accg_ref[0:1, :] += _colsum(dn2 * x1h)
        accg_ref[1:2, :] += _colsum(doutn * x2h)
        accg_ref[2:3, :] += _colsum(err * err) * (0.5 / D)

    row = lambda w: pl.BlockSpec((tm, w), lambda i: (i, 0))
    return pl.pallas_call(
        body, name="p2_mlp_loss", grid=(n // tm,),
        out_shape=[jax.ShapeDtypeStruct((n, D), F32), jax.ShapeDtypeStruct((n, D_FF), BF16),
                   jax.ShapeDtypeStruct((n, D_FF), BF16), jax.ShapeDtypeStruct((n, D), BF16),
                   jax.ShapeDtypeStruct((nb, 8, D), F32), jax.ShapeDtypeStruct((8, D), F32)],
        in_specs=[row(D), row(D), row(D), pl.BlockSpec((1, 8, D), lambda i: (i // tps, 0, 0)), _VM, _VM, _VM, _VM],
        out_specs=[row(D), row(D_FF), row(D_FF), row(D), pl.BlockSpec((1, 8, D), lambda i: (i // tps, 0, 0)),
                   pl.BlockSpec((8, D), lambda i: (0, 0))],
        compiler_params=_cp(("arbitrary",)),
    )(x1, h2, target, modp, g2, gf, w_ff1, w_ff2)


def _p3_bwd(dx1, o, yssm, oattn, modp, gs, ga, w_out, S, tm):
    n = dx1.shape[0]
    tps = S // tm
    nb = n // S

    def body(dx1_ref, o_ref, ys_ref, oa_ref, mod_ref, gs_ref, ga_ref, w_ref,
             do_ref, dys_ref, doa_ref, dr_ref, accs_ref, accg_ref):
        i = pl.program_id(0)
        dx1 = dx1_ref[...]
        dob = (mod_ref[0, 2:3, :] * dx1).astype(BF16)
        do_ref[...] = dob
        dyn = _dot_nt(dob, w_ref[...])
        yh, rs = _rms(ys_ref[...], D_SSM)
        oa = oa_ref[...].astype(F32)
        ah, ra = _rms(oa, D_ATTN)
        d1 = dyn[:, 0:D_SSM]
        d2 = dyn[:, D_SSM:D_SSM + NH * HP]
        dys_ref[...] = _rms_bwd(d1 * gs_ref[...], yh, rs, D_SSM)
        doa = _rms_bwd(d2 * ga_ref[...], ah, ra, D_ATTN)
        doa_ref[...] = doa.astype(BF16)
        prod = doa * oa
        ones = jnp.ones((8, HP), BF16)
        for h in range(NH):
            dr_ref[h * 8:(h + 1) * 8, :] = _rows_of(prod[:, h * HP:(h + 1) * HP], ones)

        @pl.when(i % tps == 0)
        def _():
            accs_ref[...] = jnp.zeros_like(accs_ref)

        @pl.when(i == 0)
        def _():
            accg_ref[...] = jnp.zeros_like(accg_ref)

        accs_ref[0, 2:3, :] += _colsum(dx1 * o_ref[...])
        accg_ref[0:1, 0:D_SSM] += _colsum(d1 * yh)
        accg_ref[1:2, :] += _colsum(d2 * ah)

    row = lambda w: pl.BlockSpec((tm, w), lambda i: (i, 0))
    return pl.pallas_call(
        body, name="p3_bwd", grid=(n // tm,),
        out_shape=[jax.ShapeDtypeStruct((n, D), BF16), jax.ShapeDtypeStruct((n, D_SSM), F32),
                   jax.ShapeDtypeStruct((n, NH * HP), BF16), jax.ShapeDtypeStruct((nb * NH * 8, S), F32),
                   jax.ShapeDtypeStruct((nb, 8, D), F32), jax.ShapeDtypeStruct((8, NH * HP), F32)],
        in_specs=[row(D), row(D), row(D_SSM), row(NH * HP), pl.BlockSpec((1, 8, D), lambda i: (i // tps, 0, 0)),
                  _VM, _VM, _VM],
        out_specs=[row(D), row(D_SSM), row(NH * HP), pl.BlockSpec((NH * 8, tm), lambda i: (i // tps, i % tps)),
                   pl.BlockSpec((1, 8, D), lambda i: (i // tps, 0, 0)), pl.BlockSpec((8, NH * HP), lambda i: (0, 0))],
        compiler_params=_cp(("arbitrary",)),
    )(dx1, o, yssm, oattn, modp, gs, ga, w_out)


def _wgrad(a, b, name, col_slots=0):
    n, k1 = a.shape
    k2 = b.shape[1]
    bn = next((b for b in (1024, 512) if n % b == 0), n)
    bk1 = next((b for b in (1024, 512) if k1 % b == 0), k1)
    bk2 = k2 // col_slots if col_slots else (1024 if (k2 % 1024 == 0) else k2)

    def body(a_ref, b_ref, o_ref):
        @pl.when(pl.program_id(2) == 0)
        def _():
            o_ref[...] = jnp.zeros_like(o_ref)

        o_ref[...] += _dot_tn(a_ref[...], b_ref[...]).reshape(o_ref.shape)

    if col_slots:
        out_shape = jax.ShapeDtypeStruct((col_slots, k1, bk2), F32)
        out_spec = pl.BlockSpec((1, bk1, bk2), lambda i, j, t: (j, i, 0))
    else:
        out_shape = jax.ShapeDtypeStruct((k1, k2), F32)
        out_spec = pl.BlockSpec((bk1, bk2), lambda i, j, t: (i, j))
    return pl.pallas_call(
        body, name=name, grid=(k1 // bk1, k2 // bk2, n // bn),
        out_shape=out_shape,
        in_specs=[pl.BlockSpec((bn, bk1), lambda i, j, t: (t, i)), pl.BlockSpec((bn, bk2), lambda i, j, t: (t, j))],
        out_specs=out_spec,
        compiler_params=_cp(("parallel", "parallel", "arbitrary")),
    )(a, b)


def _row_block(rows):
    if rows <= 256:
        return rows
    return next(b for b in (256, 192, 128, 64, 32, 16, 8) if rows % b == 0)


def _add_half(g, recv, cidx, name):
    _, rows2, w = g.shape
    rows = rows2 // 2
    br = _row_block(rows)
    nblk = rows // br

    def body(c_ref, g_ref, r_ref, o_ref):
        o_ref[...] = (g_ref[...] + r_ref[...]).astype(BF16)

    return pl.pallas_call(
        body, name=name,
        grid_spec=pltpu.PrefetchScalarGridSpec(
            num_scalar_prefetch=1, grid=(4, nblk),
            in_specs=[pl.BlockSpec((1, br, w), lambda s, i, c: (s, c[0] * nblk + i, 0)),
                      pl.BlockSpec((1, br, w), lambda s, i, c: (s, i, 0))],
            out_specs=pl.BlockSpec((1, br, w), lambda s, i, c: (s, i, 0))),
        out_shape=jax.ShapeDtypeStruct((4, rows, w), BF16),
        compiler_params=_cp(("parallel", "parallel")),
    )(cidx, g, recv)


def _add_chips(r, name):
    _, rows, w = r.shape
    br = _row_block(rows)

    def body(r_ref, o_ref):
        f = lambda k: r_ref[k].astype(F32)
        o_ref[...] = ((f(0) + f(1)) + f(2)) + f(3)

    return pl.pallas_call(
        body, name=name, grid=(rows // br,),
        out_shape=jax.ShapeDtypeStruct((rows, w), F32),
        in_specs=[pl.BlockSpec((4, br, w), lambda i: (0, i, 0))],
        out_specs=pl.BlockSpec((br, w), lambda i: (i, 0)),
        compiler_params=_cp(("parallel",)),
    )(r)


def _pair_sum(a, sa, b, sb):
    def body(a_ref, sa_ref, b_ref, sb_ref, oa_ref, ob_ref):
        oa_ref[...] = (a_ref[...].astype(F32) + sa_ref[...].astype(F32)).astype(BF16)
        ob_ref[...] = b_ref[...] + sb_ref[...]

    return pl.pallas_call(
        body, name="small_grad_pair_sum",
        out_shape=[jax.ShapeDtypeStruct(a.shape, BF16), jax.ShapeDtypeStruct(b.shape, F32)],
        in_specs=[_VM] * 4, out_specs=[_VM, _VM], compiler_params=_cp(),
    )(a, sa, b, sb)


def _sum_devices(a, b):
    def body(a_ref, b_ref, oa_ref, ob_ref):
        acc = a_ref[0:1, :].astype(F32)
        accb = b_ref[0:1, :]
        for k in range(1, a.shape[0]):
            acc = acc + a_ref[k:k + 1, :].astype(F32)
            accb = accb + b_ref[k:k + 1, :]
        oa_ref[...] = acc
        ob_ref[...] = accb

    return pl.pallas_call(
        body, name="small_grad_sum",
        out_shape=[jax.ShapeDtypeStruct((1, a.shape[1]), F32), jax.ShapeDtypeStruct((1, b.shape[1]), F32)],
        in_specs=[_VM, _VM], out_specs=[_VM, _VM], compiler_params=_cp(),
    )(a, b)


def _adamw_math(wv, gv, mv, vv):
    m_new = ADAM_B1 * mv + (1.0 - ADAM_B1) * gv
    v_new = ADAM_B2 * vv + (1.0 - ADAM_B2) * (gv * gv)
    m_hat = m_new / (1.0 - ADAM_B1 ** ADAM_STEP)
    v_hat = v_new / (1.0 - ADAM_B2 ** ADAM_STEP)
    return -ADAM_LR * (m_hat / (jnp.sqrt(v_hat) + ADAM_EPS) + ADAM_WD * wv), m_new, v_new


def _adamw_small(ws, gs, ms, vs):
    k = len(ws)

    def body(*refs):
        ins, outs = refs[:4 * k], refs[4 * k:]
        for t in range(k):
            d, m_new, v_new = _adamw_math(ins[t][...], ins[k + t][...], ins[2 * k + t][...], ins[3 * k + t][...])
            outs[t][...] = d
            outs[k + t][...] = m_new
            outs[2 * k + t][...] = v_new

    shapes = [jax.ShapeDtypeStruct(w.shape, F32) for w in ws]
    return pl.pallas_call(
        body, name="adamw_small", out_shape=shapes * 3,
        in_specs=[_VM] * (4 * k), out_specs=[_VM] * (3 * k), compiler_params=_cp(),
    )(*ws, *gs, *ms, *vs)


def _adamw(w, g, m, v, name):
    rows, wd = w.shape
    br = _row_block(rows)

    def body(w_ref, g_ref, m_ref, v_ref, d_ref, nm_ref, nv_ref):
        d, m_new, v_new = _adamw_math(w_ref[...], g_ref[...], m_ref[...], v_ref[...])
        d_ref[...] = d
        nm_ref[...] = m_new
        nv_ref[...] = v_new

    spec = pl.BlockSpec((br, wd), lambda i: (i, 0))
    return pl.pallas_call(
        body, name=name, grid=(rows // br,),
        out_shape=[jax.ShapeDtypeStruct((rows, wd), F32)] * 3,
        in_specs=[spec] * 4, out_specs=[spec] * 3,
        compiler_params=_cp(("parallel",)),
    )(w, g, m, v)


def _adamw_halves(w, mine, other, m, v, cidx, name):
    rows, wd = w.shape
    h = rows // 2
    br = _row_block(h)
    nblk = h // br

    def body(c_ref, w_ref, a_ref, b_ref, m_ref, v_ref, g_ref, d_ref, nm_ref, nv_ref):
        gv = jnp.where(pl.program_id(0) == c_ref[0], a_ref[...], b_ref[...])
        d, m_new, v_new = _adamw_math(w_ref[...], gv, m_ref[...], v_ref[...])
        g_ref[...] = gv
        d_ref[...] = d
        nm_ref[...] = m_new
        nv_ref[...] = v_new

    full = pl.BlockSpec((br, wd), lambda hf, i, c: (hf * nblk + i, 0))
    half = pl.BlockSpec((br, wd), lambda hf, i, c: (i, 0))
    return pl.pallas_call(
        body, name=name,
        grid_spec=pltpu.PrefetchScalarGridSpec(
            num_scalar_prefetch=1, grid=(2, nblk),
            in_specs=[full, half, half, full, full], out_specs=[full] * 4),
        out_shape=[jax.ShapeDtypeStruct((rows, wd), F32)] * 4,
        compiler_params=_cp(("parallel", "parallel")),
    )(cidx, w, mine, other, m, v)


def _other_chips(x, y):
    return [(1 - x, y), (x, 1 - y), (1 - x, 1 - y)]


def _other_devices(x, y, c):
    flip = lambda v, d: (1 - v) if d else v
    return [(flip(x, dx), flip(y, dy), flip(c, dc))
            for dx in (0, 1) for dy in (0, 1) for dc in (0, 1) if (dx, dy, dc) != (0, 0, 0)]


def _exchange(name, ins, out_shapes, n_local, n_remote, plan):
    ni, no = len(ins), len(out_shapes)

    def body(*refs):
        in_refs, out_refs = refs[:ni], refs[ni:ni + no]
        send_sems, recv_sems, local_sems = refs[ni + no:]
        x, y, c = lax.axis_index("x"), lax.axis_index("y"), lax.axis_index("c")
        local, remote = plan(in_refs, out_refs, x, y, c)
        assert len(local) == n_local and len(remote) == n_remote

        def push(k, src, dst, dev):
            return pltpu.make_async_remote_copy(src_ref=src, dst_ref=dst, send_sem=send_sems.at[k],
                                                recv_sem=recv_sems.at[k], device_id=dev, device_id_type=MESH)

        own = [pltpu.make_async_copy(s, d, local_sems.at[i]) for i, (s, d) in enumerate(local)]
        for cp in own:
            cp.start()
        sends = [push(k, s, d, dev) for k, (s, d, dev, _) in enumerate(remote)]
        for cp in sends:
            cp.start()
        for k, (s, _, dev, landing) in enumerate(remote):
            push(k, s, landing, dev).wait_recv()
        for cp in sends:
            cp.wait_send()
        for cp in own:
            cp.wait()

    return pl.pallas_call(
        body, name=name, out_shape=out_shapes,
        in_specs=[_ANY] * ni, out_specs=[_ANY] * no,
        scratch_shapes=[pltpu.SemaphoreType.DMA((n_remote,)), pltpu.SemaphoreType.DMA((n_remote,)),
                        pltpu.SemaphoreType.DMA((max(n_local, 1),))],
        compiler_params=pltpu.CompilerParams(has_side_effects=True),
    )(*ins)


def _gather_chips(name, shards, everyone=()):
    ns, ne = len(shards), len(everyone)
    outs = [jax.ShapeDtypeStruct((4,) + a.shape, a.dtype) for a in shards]
    outs += [jax.ShapeDtypeStruct((8,) + a.shape, a.dtype) for a in everyone]

    def plan(i, o, x, y, c):
        mine, me = 2 * x + y, 4 * x + 2 * y + c
        local, remote = [], []
        for t in range(ns):
            local.append((i[t], o[t].at[mine]))
            for px, py in _other_chips(x, y):
                remote.append((i[t], o[t].at[mine], (px, py, c), o[t].at[2 * px + py]))
        for t in range(ns, ns + ne):
            local.append((i[t], o[t].at[me]))
            for px, py, pc in _other_devices(x, y, c):
                remote.append((i[t], o[t].at[me], (px, py, pc), o[t].at[4 * px + 2 * py + pc]))
        return local, remote

    return _exchange(name, list(shards) + list(everyone), outs, ns + ne, 3 * ns + 7 * ne, plan)


_HBM = pl.BlockSpec(memory_space=pltpu.HBM)
_SEM = pl.BlockSpec(memory_space=pltpu.SEMAPHORE)
_EFFECT = pltpu.SideEffectType.DATAFLOW_SIDE_EFFECTING


def _split_start(name, ins, land_shapes, n_remote, plan, after):
    ni, nl = len(ins), len(land_shapes)
    srcs = [pltpu.with_memory_space_constraint(a, pltpu.HBM) for a in ins]
    lands = [pltpu.with_memory_space_constraint(lax.empty(s.shape, s.dtype), pltpu.HBM) for s in land_shapes]

    def body(*refs):
        src, land = refs[:ni], refs[ni:ni + nl]
        first = ni + nl + 1
        send, recv = refs[first:first + n_remote], refs[first + n_remote:first + 2 * n_remote]
        token = refs[first + 2 * n_remote + ni + nl]
        x, y, c = lax.axis_index("x"), lax.axis_index("y"), lax.axis_index("c")
        remote = plan(src, land, x, y, c)
        assert len(remote) == n_remote
        for k, (s, d, dev, _) in enumerate(remote):
            pltpu.make_async_remote_copy(src_ref=s, dst_ref=d, send_sem=send[k], recv_sem=recv[k],
                                         device_id=dev, device_id_type=MESH).start()
        token[...] = jnp.zeros_like(token)

    out = pl.pallas_call(
        body, name=name + "_start",
        out_shape=[pltpu.SemaphoreType.DMA(())] * (2 * n_remote)
                  + [pltpu.HBM(a.shape, a.dtype) for a in ins] + [pltpu.HBM(s.shape, s.dtype) for s in land_shapes]
                  + [jax.ShapeDtypeStruct((8, 128), F32)],
        in_specs=[_HBM] * (ni + nl) + [_ANY], out_specs=[_SEM] * (2 * n_remote) + [_HBM] * (ni + nl) + [_VM],
        input_output_aliases={t: 2 * n_remote + t for t in range(ni + nl)},
        compiler_params=pltpu.CompilerParams(has_side_effects=_EFFECT),
    )(*srcs, *lands, after)
    sems, thru = out[:2 * n_remote], out[2 * n_remote:2 * n_remote + ni + nl]
    return (name, sems, thru[:ni], thru[ni:], n_remote, plan), out[-1]


def _split_wait(handle, after):
    name, sems, srcs, lands, n_remote, plan = handle
    ni, nl = len(srcs), len(lands)

    def body(*refs):
        src, land = refs[:ni], refs[ni:ni + nl]
        send, recv = refs[ni + nl:ni + nl + n_remote], refs[ni + nl + n_remote:ni + nl + 2 * n_remote]
        x, y, c = lax.axis_index("x"), lax.axis_index("y"), lax.axis_index("c")
        for k, (s, _, dev, landing) in enumerate(plan(src, land, x, y, c)):
            cp = pltpu.make_async_remote_copy(src_ref=s, dst_ref=landing, send_sem=send[k], recv_sem=recv[k],
                                              device_id=dev, device_id_type=MESH)
            cp.wait_send()
            cp.wait_recv()

    out = pl.pallas_call(
        body, name=name + "_wait",
        out_shape=[pltpu.HBM(a.shape, a.dtype) for a in srcs] + [pltpu.HBM(a.shape, a.dtype) for a in lands],
        in_specs=[_HBM] * (ni + nl) + [_SEM] * (2 * n_remote) + [_ANY], out_specs=[_HBM] * (ni + nl),
        input_output_aliases={t: t for t in range(ni + nl)},
        compiler_params=pltpu.CompilerParams(has_side_effects=_EFFECT),
    )(*srcs, *lands, *sems, after)
    return out[:ni], out[ni:]


def _plan_to_chips(src, land, x, y, c):
    mine = 2 * x + y
    return [(src[t], land[t].at[mine], (px, py, c), land[t].at[2 * px + py])
            for t in range(len(src)) for px, py in _other_chips(x, y)]


def _plan_swap_halves(src, land, x, y, c):
    out = []
    for t in range(len(src)):
        h = src[t].shape[1] // 2
        out.append((src[t].at[:, pl.ds(pl.multiple_of((1 - c) * h, 8), h), :], land[t], (x, y, 1 - c), land[t]))
    return out


def _plan_scatter_chips(src, land, x, y, c):
    mine = 2 * x + y
    return [(src[t].at[2 * px + py], land[t].at[mine], (px, py, c), land[t].at[2 * px + py])
            for t in range(len(src)) for px, py in _other_chips(x, y)]


def _swap_halves(gs, everyone, whole):
    ns, ne, nw = len(gs), len(everyone), len(whole)
    outs = [jax.ShapeDtypeStruct((4, g.shape[1] // 2, g.shape[2]), g.dtype) for g in gs]
    outs += [jax.ShapeDtypeStruct((8,) + a.shape, a.dtype) for a in everyone]
    outs += [jax.ShapeDtypeStruct(a.shape, a.dtype) for a in whole]

    def plan(i, o, x, y, c):
        me = 4 * x + 2 * y + c
        local, remote = [], []
        for t in range(ns):
            h = gs[t].shape[1] // 2
            theirs = i[t].at[:, pl.ds(pl.multiple_of((1 - c) * h, 8), h), :]
            remote.append((theirs, o[t], (x, y, 1 - c), o[t]))
        for t in range(ns, ns + ne):
            local.append((i[t], o[t].at[me]))
            for px, py, pc in _other_devices(x, y, c):
                remote.append((i[t], o[t].at[me], (px, py, pc), o[t].at[4 * px + 2 * py + pc]))
        for t in range(ns + ne, ns + ne + nw):
            remote.append((i[t], o[t], (x, y, 1 - c), o[t]))
        return local, remote

    return _exchange("grad_swap_sibling", list(gs) + list(everyone) + list(whole), outs, ne, ns + 7 * ne + nw, plan)


def _scatter_chips(parts, per_chip):
    ns, ng = len(parts), len(per_chip)
    outs = [jax.ShapeDtypeStruct(a.shape, a.dtype) for a in parts]
    outs += [jax.ShapeDtypeStruct((4,) + a.shape, a.dtype) for a in per_chip]

    def plan(i, o, x, y, c):
        mine = 2 * x + y
        local, remote = [], []
        for t in range(ns):
            local.append((i[t].at[mine], o[t].at[mine]))
            for px, py in _other_chips(x, y):
                remote.append((i[t].at[2 * px + py], o[t].at[mine], (px, py, c), o[t].at[2 * px + py]))
        for t in range(ns, ns + ng):
            local.append((i[t], o[t].at[mine]))
            for px, py in _other_chips(x, y):
                remote.append((i[t], o[t].at[mine], (px, py, c), o[t].at[2 * px + py]))
        return local, remote

    return _exchange("grad_scatter_chips", list(parts) + list(per_chip), outs, ns + ng, 3 * (ns + ng), plan)


def _join_halves(halves):
    ns = len(halves)
    outs = [jax.ShapeDtypeStruct(a.shape, a.dtype) for a in halves]

    def plan(i, o, x, y, c):
        return [], [(i[t], o[t], (x, y, 1 - c), o[t]) for t in range(ns)]

    return _exchange("grad_join_sibling", list(halves), outs, 0, ns, plan)


def _pad_heads_cols(w, per, used):
    k = w.shape[0]
    w = w.reshape(k, NH, per)[:, :, :used]
    return jnp.pad(w, ((0, 0), (0, 0), (0, HP - used))).reshape(k, NH * HP)


def _unpad_heads_cols(w, used):
    k = w.shape[0]
    return w.reshape(k, NH, HP)[:, :, :used]


def _prep_weights(wf):
    bf = lambda a: a.astype(BF16)
    out = {}
    out["w_in"] = jnp.pad(bf(wf["w_in"]), ((0, 0), (0, IN_PAD - IN_COLS)))
    out["w_glu"] = bf(wf["w_glu"])
    out["w_uq"] = _pad_heads_cols(bf(wf["w_uq"]), QK_NOPE + QK_ROPE, QK_NOPE + QK_ROPE)
    wkv = bf(wf["w_ukv"]).reshape(KV_LORA, NH, QK_NOPE + V_HEAD)
    wk = jnp.pad(wkv[:, :, :QK_NOPE], ((0, 0), (0, 0), (0, HP - QK_NOPE))).reshape(KV_LORA, NH * HP)
    wv = jnp.pad(wkv[:, :, QK_NOPE:], ((0, 0), (0, 0), (0, HP - V_HEAD))).reshape(KV_LORA, NH * HP)
    out["w_ukv"] = jnp.concatenate([wk, wv], axis=1)
    return out


def _prep_late_weights(wf):
    bf = lambda a: a.astype(BF16)
    out = {}
    wo = bf(wf["w_out"])
    wo_a = jnp.pad(wo[D_SSM:].reshape(NH, V_HEAD, D), ((0, 0), (0, HP - V_HEAD), (0, 0))).reshape(NH * HP, D)
    out["w_out"] = jnp.concatenate([wo[:D_SSM], wo_a], axis=0)
    out["w_ff1"] = bf(wf["w_ff1"])
    out["w_ff2"] = bf(wf["w_ff2"])
    return out


def _rope_tables(positions):
    inv_freq = ROPE_BASE ** (-jnp.arange(0, QK_ROPE, 2, dtype=F32) / QK_ROPE)
    ang = positions.astype(F32)[:, None] * inv_freq
    cos, sin = jnp.cos(ang), jnp.sin(ang)
    n = positions.shape[0]
    one = jnp.ones((n, QK_NOPE), F32)
    z16 = jnp.zeros((n, 16), F32)
    z32 = jnp.zeros((n, 32), F32)
    z64 = jnp.zeros((n, QK_NOPE), F32)
    rc = jnp.concatenate([one, cos, cos, z32], axis=1)
    rs1 = jnp.concatenate([z64, -sin, z16, z32], axis=1)
    rs2 = jnp.concatenate([z64, z16, sin, z32], axis=1)
    return rc, rs1, rs2


def _permute_rows(a, S):
    n, w = a.shape
    return a.reshape(n // S, 8, S // 8, w).transpose(0, 2, 1, 3).reshape(n, w)


def _unpermute_rows(a, S):
    n, w = a.shape
    return a.reshape(n // S, S // 8, 8, w).transpose(0, 2, 1, 3).reshape(n, w)


def _block_diag_in(bb):
    eye = jnp.eye(8, dtype=bb.dtype)
    blocks = jnp.einsum("qgph,gk->qghkp", bb.reshape(4, 8, P, H), eye).reshape(4, QB, QS)
    return blocks.transpose(1, 0, 2).reshape(QB, NST)


def _block_diag_out(cc):
    eye = jnp.eye(8, dtype=cc.dtype)
    return jnp.einsum("qghp,gk->qgpkh", cc.reshape(4, 8, H, P), eye).reshape(NST, QB)


def _slots(full):
    r, cdim = full.shape
    return full.reshape(r, 4, cdim // 4).transpose(1, 0, 2)


def _unslots(g):
    s, r, cs = g.shape
    return g.transpose(1, 0, 2).reshape(r, s * cs)


def _local_step(x, positions, target, modp, wf, late_weights=None, reducer=None):
    nb, S, _ = x.shape
    n = nb * S
    tm = min(256, S)
    tr = min(512, S)
    tt = min(512, S)
    tq = min(512, S // 2)
    kw = _prep_weights(wf)
    row = lambda a: a.reshape(1, -1).astype(F32)

    xf = x.reshape(n, D)
    tf = target.reshape(n, D)
    g1, g2, gf = row(wf["norm1_g"]), row(wf["norm2_g"]), row(wf["final_norm_g"])
    rc, rs1, rs2 = _rope_tables(positions.reshape(n))
    gq, gkv = row(wf["q_norm_g"]), row(wf["kv_norm_g"])
    h1, u, lat, q, k, v, qn, kvn = _f1_fwd(xf, modp, g1, kw["w_in"], rc, rs1, rs2, gq, gkv,
                                           kw["w_uq"], kw["w_ukv"], S, tr)

    col = lambda a: a.reshape(NST, 1)
    lam_re, lam_im = col(wf["ssm_lambda_re"]), col(wf["ssm_lambda_im"])
    logdt = jnp.repeat(wf["ssm_log_dt"].reshape(G, 1), P, axis=1).reshape(NST, 1)
    b_re, b_im = wf["ssm_b_re"].reshape(NST, H), wf["ssm_b_im"].reshape(NST, H)
    lbr, lbi, bbr, bbi = _ssm_param_fwd(lam_re, lam_im, logdt, b_re, b_im)
    lre8 = jnp.broadcast_to(lbr.reshape(1, NST), (8, NST))
    lim8 = jnp.broadcast_to(lbi.reshape(1, NST), (8, NST))
    bm = jnp.concatenate([_block_diag_in(bbr.reshape(G, P, H)), _block_diag_in(bbi.reshape(G, P, H))],
                         axis=1).astype(BF16)
    cm = jnp.concatenate([_block_diag_out(wf["ssm_c_re"]), -_block_diag_out(wf["ssm_c_im"])], axis=0).astype(BF16)
    dvec = row(wf["ssm_d"])
    u_p = _permute_rows(u, S)
    fcr, fci = _ssm_local(u_p, bm, lre8, lim8, S, tt)
    st, ypre, z, gact, yssm_p = _ssm_fwd(u_p, fcr, fci, bm, cm, dvec, kw["w_glu"], lre8, lim8, S, tt)
    yssm = _unpermute_rows(yssm_p, S)

    oattn, lrow = _attn_fwd(q, k, v, S, tq)

    gs = row(wf["ssm_out_g"])
    ga = jnp.pad(wf["attn_out_g"].reshape(NH, V_HEAD), ((0, 0), (0, HP - V_HEAD))).reshape(1, NH * HP)
    kw.update(_prep_late_weights(late_weights(oattn) if late_weights is not None else wf))
    yn, o, x1, h2 = _p1_fwd(yssm, oattn, xf, modp, gs, ga, kw["w_out"], g2, S, tr)
    dx1, r, da, dff, accs2, accg2 = _p2(x1, h2, tf, modp, g2, gf, kw["w_ff1"], kw["w_ff2"], S, tm)
    loss = accg2[2:3]
    g_ff1 = _wgrad(h2, da, "wgrad_ff1", col_slots=4)
    g_ff2 = _wgrad(r, dff, "wgrad_ff2").reshape(4, D_FF // 4, D)
    do, dyssm, dob, drow, accs3, accg3 = _p3_bwd(dx1, o, yssm, oattn, modp, gs, ga, kw["w_out"], S, tr)
    gwo = _wgrad(yn, do, "wgrad_out")
    g_out = jnp.concatenate([gwo[:D_SSM].reshape(2, D_SSM // 2, D),
                             gwo[D_SSM:].reshape(2, NH // 2 * HP, D).reshape(2, NH // 2, HP, D)[:, :, :V_HEAD]
                             .reshape(2, D_ATTN // 2, D)], axis=0)
    lre8_b = lre8
    if reducer is not None:
        drow = drow + reducer.start([g_ff1, g_ff2, g_out])[0, 0]

    dq, dk, dv = _attn_bwd(q, k, v, dob, lrow, drow, S, tq)
    if reducer is not None:
        lre8_b = lre8 + reducer.middle(dq)[0, 0]

    dys_p = _permute_rows(dyssm, S)
    dy, dz, air, aii = _ssm_bwd_a(dys_p, z, ypre, kw["w_glu"], cm, lre8_b, lim8, S, tt)
    du_p, dcm, dbm, dd, dlr, dli = _ssm_bwd_b(dy, u_p, st, fcr, fci, air, aii, bm, cm, dvec, lre8, lim8, S, tt)
    du = _unpermute_rows(du_p, S)
    dcm = dcm.reshape(2, 4, 8, P, 8, H)
    dc_re = jnp.einsum("qgpgh->qghp", dcm[0]).reshape(G, H, P)
    dc_im = -jnp.einsum("qgpgh->qghp", dcm[1]).reshape(G, H, P)
    dbm = dbm.reshape(8, H, 2, 4, 8, P)
    dbb_re = jnp.einsum("ghqgp->qgph", dbm[:, :, 0]).reshape(NST, H)
    dbb_im = jnp.einsum("ghqgp->qgph", dbm[:, :, 1]).reshape(NST, H)
    gb_re, gb_im, glr, gli, gdt = _ssm_param_bwd(lam_re, lam_im, logdt, b_re, b_im, dlr.reshape(NST, 1),
                                                 dli.reshape(NST, 1), dbb_re, dbb_im)
    glogdt = _rowsum(gdt.reshape(G, P))

    dx, dproj, dqb, dkvb, accs1, accg1, accm = _f1_bwd(du, dq, dk, dv, lat, rc, rs1, rs2, gq, gkv, kw["w_uq"],
                                                       kw["w_ukv"], dx1, xf, modp, g1, kw["w_in"], S, tr)

    big = {}
    big["w_in"] = _slots(_wgrad(h1, dproj, "wgrad_in")[:, :IN_COLS])
    big["w_glu"] = _wgrad(gact, dz, "wgrad_glu", col_slots=4)
    big["w_uq"] = _slots(_unpad_heads_cols(_wgrad(qn, dqb, "wgrad_uq"), QK_NOPE + QK_ROPE).reshape(Q_LORA, -1))
    gkvw = _wgrad(kvn, dkvb, "wgrad_ukv")
    big["w_ukv"] = _slots(jnp.concatenate([_unpad_heads_cols(gkvw[:, :NH * HP], QK_NOPE),
                                           _unpad_heads_cols(gkvw[:, NH * HP:], V_HEAD)], axis=2).reshape(KV_LORA, -1))
    big["w_out"] = g_out
    big["w_ff1"] = g_ff1
    big["w_ff2"] = g_ff2

    small = {}
    small["norm1_g"] = accg1[0:1]
    small["norm2_g"] = accg2[0:1]
    small["final_norm_g"] = accg2[1:2]
    small["ssm_out_g"] = accg3[0:1, :D_SSM]
    small["attn_out_g"] = accg3[1].reshape(NH, HP)[:, :V_HEAD].reshape(1, D_ATTN)
    small["q_norm_g"] = accm[0:1, :Q_LORA]
    small["kv_norm_g"] = accm[1:2, :KV_LORA]
    small["ssm_lambda_re"] = glr.reshape(G, P)
    small["ssm_lambda_im"] = gli.reshape(G, P)
    small["ssm_b_re"] = gb_re
    small["ssm_b_im"] = gb_im
    small["ssm_c_re"] = dc_re.reshape(G * H, P)
    small["ssm_c_im"] = dc_im.reshape(G * H, P)
    small["ssm_d"] = dd.reshape(G, H)
    small["ssm_log_dt"] = glogdt.reshape(1, G)
    return loss, dx.reshape(nb, S, D), big, small, accs1 + accs2 + accs3


def _view2d(a):
    return a.reshape(-1, a.shape[-1]) if a.ndim > 1 else a.reshape(1, -1)


def kernel(x, c, positions, ada_w, ada_b, norm1_g, w_in, ssm_lambda_re, ssm_lambda_im, ssm_b_re, ssm_b_im, ssm_c_re, ssm_c_im, ssm_d, ssm_log_dt, w_glu, q_norm_g, w_uq, kv_norm_g, w_ukv, ssm_out_g, attn_out_g, w_out, norm2_g, w_ff1, w_ff2, final_ada_w, final_ada_b, final_norm_g, loss_target, m_ada_w, m_ada_b, m_norm1_g, m_w_in, m_ssm_lambda_re, m_ssm_lambda_im, m_ssm_b_re, m_ssm_b_im, m_ssm_c_re, m_ssm_c_im, m_ssm_d, m_ssm_log_dt, m_w_glu, m_q_norm_g, m_w_uq, m_kv_norm_g, m_w_ukv, m_ssm_out_g, m_attn_out_g, m_w_out, m_norm2_g, m_w_ff1, m_w_ff2, m_final_ada_w, m_final_ada_b, m_final_norm_g, v_ada_w, v_ada_b, v_norm1_g, v_w_in, v_ssm_lambda_re, v_ssm_lambda_im, v_ssm_b_re, v_ssm_b_im, v_ssm_c_re, v_ssm_c_im, v_ssm_d, v_ssm_log_dt, v_w_glu, v_q_norm_g, v_w_uq, v_kv_norm_g, v_w_ukv, v_ssm_out_g, v_attn_out_g, v_w_out, v_norm2_g, v_w_ff1, v_w_ff2, v_final_ada_w, v_final_ada_b, v_final_norm_g):
    args = dict(locals())
    names = list(inspect.signature(kernel).parameters)
    wnames = names[3:names.index("loss_target")]
    small_names = [nm for nm in wnames if nm not in GATHERED and nm not in TP]
    reduced_names = [nm for nm in small_names if nm not in ("ada_b", "final_ada_b")]
    w = {nm: args[nm] for nm in wnames}
    m = {nm: args["m_" + nm] for nm in wnames}
    v = {nm: args["v_" + nm] for nm in wnames}
    nb = x.shape[0]
    xi, yi, ci = lax.axis_index("x"), lax.axis_index("y"), lax.axis_index("c")
    chip, me = 2 * xi + yi, 4 * xi + 2 * yi + ci

    unslot = lambda nm, g: g.reshape(-1, g.shape[-1]) if nm in ROW_SHARDED else _unslots(g)
    early = [nm for nm in GATHERED if nm not in LATE]
    got = _gather_chips("gather_weights", [_view2d(w[nm]).astype(BF16) for nm in early], [c])
    wf = {nm: unslot(nm, g) for nm, g in zip(early, got)}
    for nm in small_names:
        wf[nm] = w[nm][0] if w[nm].ndim > 1 else w[nm]
    c_all = got[len(early)].reshape(8 * nb, D)

    na, nf = ada_w.shape[-1], final_ada_w.shape[-1]
    ada_b_s = lax.dynamic_slice(ada_b, (0, chip * na), (1, na))
    fada_b_s = lax.dynamic_slice(final_ada_b.reshape(1, -1), (0, chip * nf), (1, nf))
    cond_all, modcols = _mod_fwd(c_all, ada_w[0], ada_b_s, final_ada_w, fada_b_s)
    (mod_g,) = _gather_chips("gather_mod", [modcols])
    mine = lax.dynamic_slice(mod_g, (0, me * nb, 0), (4, nb, na + nf))
    modp = jnp.concatenate([mine[:, :, :na].transpose(1, 0, 2).reshape(nb, 6, D),
                            mine[:, :, na:].transpose(1, 0, 2).reshape(nb, 2, D)], axis=1)

    own_late = [_view2d(w[nm]).astype(BF16) for nm in LATE]
    late_gather, token = _split_start("gather_late", own_late,
                                      [jax.ShapeDtypeStruct((4,) + a.shape, a.dtype) for a in own_late],
                                      3 * len(LATE), _plan_to_chips, modp)
    modp = modp + token[0, 0]

    def late_weights(after):
        sent, landed = _split_wait(late_gather, after)
        return {nm: unslot(nm, lax.dynamic_update_slice(g, own[None], (chip, 0, 0)))
                for nm, g, own in zip(LATE, landed, sent)}

    cidx = ci.astype(jnp.int32).reshape(1)
    ahead = ["w_ff1", "w_ff2", "w_out"]

    class Reducer:
        def start(self, gs):
            lands = [jax.ShapeDtypeStruct((4, g.shape[1] // 2, g.shape[2]), g.dtype) for g in gs]
            self.swap, tok = _split_start("grad_swap_ff", gs, lands, len(gs), _plan_swap_halves, modp)
            return tok

        def middle(self, after):
            gs, got = _split_wait(self.swap, after)
            sums = [_add_half(g, r, cidx, "grad_add_sibling_" + nm) for nm, g, r in zip(ahead, gs, got)]
            lands = [jax.ShapeDtypeStruct(s.shape, s.dtype) for s in sums]
            self.scatter, tok = _split_start("grad_scatter_ff", sums, lands, 3 * len(sums), _plan_scatter_chips, modp)
            return tok

        def finish(self, after):
            out = []
            for nm, s, l in zip(ahead, *_split_wait(self.scatter, after)):
                own = lax.dynamic_slice(s, (chip, 0, 0), (1,) + s.shape[1:])
                out.append(_add_chips(lax.dynamic_update_slice(l, own, (chip, 0, 0)), "grad_add_chips_" + nm))
            return out

    reducer = Reducer()
    loss_row, grad_x, big, small, dmodp = _local_step(x, positions, loss_target, modp, wf, late_weights, reducer)

    rest = [nm for nm in GATHERED if nm not in ahead]
    sizes = [small[nm].size for nm in reduced_names]
    pad = -sum(sizes) % 128
    packed = jnp.concatenate([small[nm].reshape(1, -1) for nm in reduced_names] + [jnp.zeros((1, pad), F32)],
                             axis=1).astype(BF16)
    swapped = _swap_halves([big[nm] for nm in rest], [dmodp.reshape(nb, 8 * D)], [packed, loss_row])
    chip_sums = [_add_half(big[nm], r, cidx, "grad_add_sibling_" + nm) for nm, r in zip(rest, swapped)]
    chip_small = _pair_sum(packed, swapped[len(rest) + 1], loss_row, swapped[len(rest) + 2])
    scattered = _scatter_chips(chip_sums, chip_small)
    half_of = {nm: _add_chips(r, "grad_add_chips_" + nm) for nm, r in zip(rest, scattered)}
    half_of.update(zip(ahead, reducer.finish(grad_x)))
    halves = [half_of[nm] for nm in GATHERED]
    others = _join_halves(halves)
    grads = {}
    dmod_all = swapped[len(rest)].reshape(8 * nb, 8 * D)
    small_sum, loss_sum = _sum_devices(scattered[len(rest)].reshape(4, -1), scattered[len(rest) + 1].reshape(4, -1))
    loss = jnp.sum(loss_sum)
    off = 0
    for nm, sz in zip(reduced_names, sizes):
        grads[nm] = small_sum[:, off:off + sz].reshape(small[nm].shape)
        off += sz

    dsl = jnp.concatenate([lax.dynamic_slice(dmod_all, (0, chip * na), (8 * nb, na)),
                           lax.dynamic_slice(dmod_all, (0, 6 * D + chip * nf), (8 * nb, nf))], axis=1)
    gw, gb = _mod_bwd(cond_all.T, dsl, dmod_all)
    grads["ada_w"], grads["final_ada_w"] = gw[:, :na], gw[:, na:]
    grads["ada_b"], grads["final_ada_b"] = gb[:, :6 * D], gb[:, 6 * D:]

    delta, new_m, new_v = {}, {}, {}
    for nm, mine_h, other_h in zip(GATHERED, halves, others):
        grads[nm], delta[nm], new_m[nm], new_v[nm] = _adamw_halves(
            _view2d(w[nm]), mine_h, other_h, _view2d(m[nm]), _view2d(v[nm]), cidx, "adamw_" + nm)
    for nm in TP:
        delta[nm], new_m[nm], new_v[nm] = _adamw(_view2d(w[nm]), grads[nm], _view2d(m[nm]), _view2d(v[nm]),
                                                  "adamw_" + nm)
    upd = _adamw_small([_view2d(w[nm]) for nm in small_names], [grads[nm] for nm in small_names],
                       [_view2d(m[nm]) for nm in small_names], [_view2d(v[nm]) for nm in small_names])
    k = len(small_names)
    for t, nm in enumerate(small_names):
        delta[nm], new_m[nm], new_v[nm] = upd[t], upd[k + t], upd[2 * k + t]

    outs = [grads, delta, new_m, new_v]
    return (loss, grad_x, *[d[nm].reshape(w[nm].shape) for d in outs for nm in wnames])
```

```python
import inspect
import math

import jax
import jax.numpy as jnp
from jax import lax
from jax.experimental import pallas as pl
from jax.experimental.pallas import tpu as pltpu

F32 = jnp.float32
BF16 = jnp.bfloat16

D = 1024
D_SSM = 512
G = 32
H = 16
P = 64
NST = G * P
D_ATTN = 512
NH = 8
QK_NOPE = 64
QK_ROPE = 32
V_HEAD = 64
HP = 128
Q_LORA = 384
KV_LORA = 256
IN_COLS = D_SSM + Q_LORA + KV_LORA + QK_ROPE
IN_PAD = 1280
D_FF = 4096
ROPE_BASE = 10000.0
EPS = 1e-6
ADAM_LR = 0.001
ADAM_B1 = 0.9
ADAM_B2 = 0.999
ADAM_EPS = 1e-08
ADAM_WD = 0.01
ADAM_STEP = 10
NEG = -1e30
VMEM_LIMIT = 60 << 20

MESH = pl.DeviceIdType.MESH
_VM = pl.BlockSpec(memory_space=pltpu.VMEM)
_ANY = pl.BlockSpec(memory_space=pl.ANY)

GATHERED = ["w_in", "w_glu", "w_uq", "w_ukv", "w_out", "w_ff1", "w_ff2"]
TP = ["ada_w", "final_ada_w"]
ROW_SHARDED = ("w_out", "w_ff2")
LATE = ["w_out", "w_ff1", "w_ff2"]


def _cp(sem=None, vmem=VMEM_LIMIT):
    kw = dict(vmem_limit_bytes=vmem)
    if sem is not None:
        kw["dimension_semantics"] = sem
    return pltpu.CompilerParams(**kw)


def _dot(a, b):
    return jnp.dot(a, b, preferred_element_type=F32)


def _dot_nt(a, b):
    return lax.dot_general(a, b, (((1,), (1,)), ((), ())), preferred_element_type=F32)


def _dot_tn(a, b):
    return lax.dot_general(a, b, (((0,), (0,)), ((), ())), preferred_element_type=F32)


def _rms(x, n):
    r = lax.rsqrt(jnp.sum(x * x, axis=-1, keepdims=True) * (1.0 / n) + EPS)
    return x * r, r


def _rms_bwd(dyg, xhat, r, n):
    return r * (dyg - xhat * (jnp.sum(dyg * xhat, axis=-1, keepdims=True) * (1.0 / n)))


def _sigmoid(x):
    return 1.0 / (1.0 + jnp.exp(-x))


_GK = math.sqrt(2.0 / math.pi)
_GC = 0.044715


def _gelu(y):
    t = jnp.tanh(_GK * (y + _GC * y * y * y))
    return 0.5 * y * (1.0 + t)


def _gelu_grad(y):
    t = jnp.tanh(_GK * (y + _GC * y * y * y))
    return 0.5 * (1.0 + t) + 0.5 * y * (1.0 - t * t) * _GK * (1.0 + 3.0 * _GC * y * y)


def _colsum(x):
    return jnp.sum(x, axis=0, keepdims=True)


def _roll(x, s):
    return pltpu.roll(x, s % x.shape[-1], x.ndim - 1)


def _mod_fwd(c_all, ada_w_s, ada_b_s, fada_w_s, fada_b_s):
    nseq = c_all.shape[0]
    na, nf = ada_w_s.shape[1], fada_w_s.shape[1]

    def body(c_ref, w_ref, b_ref, fw_ref, fb_ref, cond_ref, mod_ref):
        cv = c_ref[...]
        cond = cv * _sigmoid(cv)
        cond_ref[...] = cond
        cb = cond.astype(BF16)
        mod_ref[:, 0:na] = _dot(cb, w_ref[...].astype(BF16)) + b_ref[...]
        mod_ref[:, na:na + nf] = _dot(cb, fw_ref[...].astype(BF16)) + fb_ref[...]

    return pl.pallas_call(
        body, name="mod_fwd",
        out_shape=[jax.ShapeDtypeStruct((nseq, D), F32), jax.ShapeDtypeStruct((nseq, na + nf), F32)],
        in_specs=[_VM] * 5, out_specs=[_VM] * 2, compiler_params=_cp(),
    )(c_all, ada_w_s, ada_b_s, fada_w_s, fada_b_s)


def _mod_bwd(cond_t, dsl, dall):
    nseq, n = dsl.shape
    bc = 512

    def body(ct_ref, dm_ref, da_ref, gw_ref, gb_ref):
        ct = ct_ref[...]
        dm = dm_ref[...]
        acc = ct[:, 0:1] * dm[0:1, :]
        for b in range(1, nseq):
            acc = acc + ct[:, b:b + 1] * dm[b:b + 1, :]
        gw_ref[...] = acc

        @pl.when(pl.program_id(0) == 0)
        def _():
            gb_ref[...] = _colsum(da_ref[...])

    return pl.pallas_call(
        body, name="mod_bwd", grid=(n // bc,),
        out_shape=[jax.ShapeDtypeStruct((D, n), F32), jax.ShapeDtypeStruct((1, dall.shape[1]), F32)],
        in_specs=[_VM, pl.BlockSpec((nseq, bc), lambda i: (0, i)), _VM],
        out_specs=[pl.BlockSpec((D, bc), lambda i: (0, i)), pl.BlockSpec((1, dall.shape[1]), lambda i: (0, 0))],
        compiler_params=_cp(("arbitrary",)),
    )(cond_t, dsl, dall)


def _f1_fwd(x, modp, g1, w_in, rc, rs1, rs2, gq, gkv, w_uq, w_ukv, S, tm):
    n = x.shape[0]
    tps = S // tm
    LAT = IN_PAD - D_SSM

    def body(x_ref, mod_ref, g_ref, w_ref, c_ref, s1_ref, s2_ref, gq_ref, gkv_ref, wq_ref, wkv_ref,
             h_ref, u_ref, lat_ref, q_ref, k_ref, v_ref, qn_ref, kvn_ref):
        xhat, _ = _rms(x_ref[...], D)
        h = (xhat * g_ref[...]) * (1.0 + mod_ref[0, 1:2, :]) + mod_ref[0, 0:1, :]
        hb = h.astype(BF16)
        h_ref[...] = hb
        proj = _dot(hb, w_ref[...])
        u_ref[...] = proj[:, 0:D_SSM]
        lat_ref[...] = proj[:, D_SSM:IN_PAD]
        c, s1, s2 = c_ref[...], s1_ref[...], s2_ref[...]
        qhat, _ = _rms(proj[:, D_SSM:D_SSM + Q_LORA], Q_LORA)
        qn = (qhat * gq_ref[...]).astype(BF16)
        qn_ref[...] = qn
        q = _dot(qn, wq_ref[...])
        qr = _rope(q, jnp.tile(c, (1, NH)), jnp.tile(s1, (1, NH)), jnp.tile(s2, (1, NH)))
        q_ref[...] = (qr * _C2).astype(BF16)
        khat, _ = _rms(proj[:, D_SSM + Q_LORA:D_SSM + Q_LORA + KV_LORA], KV_LORA)
        kvn = (khat * gkv_ref[...]).astype(BF16)
        kvn_ref[...] = kvn
        kv = _dot(kvn, wkv_ref[...])
        kr = _rope(_roll(proj[:, IN_PAD - HP:IN_PAD], 64), c, s1, s2)
        k_ref[...] = (kv[:, 0:NH * HP] + jnp.tile(kr, (1, NH))).astype(BF16)
        vv = kv[:, NH * HP:2 * NH * HP]
        lane = lax.broadcasted_iota(jnp.int32, vv.shape, 1)
        v_ref[...] = jnp.where(lane % HP == V_HEAD, 1.0, vv).astype(BF16)

    row = lambda w: pl.BlockSpec((tm, w), lambda i: (i, 0))
    return pl.pallas_call(
        body, name="f1_fwd", grid=(n // tm,),
        out_shape=[jax.ShapeDtypeStruct((n, D), BF16), jax.ShapeDtypeStruct((n, D_SSM), F32),
                   jax.ShapeDtypeStruct((n, LAT), F32)] + [jax.ShapeDtypeStruct((n, NH * HP), BF16)] * 3 +
                  [jax.ShapeDtypeStruct((n, Q_LORA), BF16), jax.ShapeDtypeStruct((n, KV_LORA), BF16)],
        in_specs=[row(D), pl.BlockSpec((1, 8, D), lambda i: (i // tps, 0, 0)), _VM, _VM,
                  row(HP), row(HP), row(HP), _VM, _VM, _VM, _VM],
        out_specs=[row(D), row(D_SSM), row(LAT)] + [row(NH * HP)] * 3 + [row(Q_LORA), row(KV_LORA)],
        compiler_params=_cp(("parallel",)),
    )(x, modp, g1, w_in, rc, rs1, rs2, gq, gkv, w_uq, w_ukv)


def _f1_bwd(du, dq, dk, dv, lat, rc, rs1, rs2, gq, gkv, w_uq, w_ukv, dx1, x, modp, g1, w_in, S, tm):
    n = x.shape[0]
    tps = S // tm
    nb = n // S

    def body(du_ref, dq_ref, dk_ref, dv_ref, lat_ref, c_ref, s1_ref, s2_ref, gq_ref, gkv_ref, wq_ref, wkv_ref,
             dx1_ref, x_ref, mod_ref, g_ref, w_ref,
             dx_ref, dproj_ref, dqb_ref, dkvb_ref, accs_ref, accg_ref, accm_ref):
        i = pl.program_id(0)
        c, s1, s2 = c_ref[...], s1_ref[...], s2_ref[...]
        dqu = _rope_t(dq_ref[...] * _SCALE, jnp.tile(c, (1, NH)), jnp.tile(s1, (1, NH)),
                      jnp.tile(s2, (1, NH))).astype(BF16)
        dqb_ref[...] = dqu
        dqn = _dot_nt(dqu, wq_ref[...])
        qhat, rq = _rms(lat_ref[:, 0:Q_LORA], Q_LORA)
        dql = _rms_bwd(dqn * gq_ref[...], qhat, rq, Q_LORA)
        dkf = dk_ref[...] * (1.0 / _LOG2E)
        dkv = jnp.concatenate([dkf.astype(BF16), dv_ref[...].astype(BF16)], axis=1)
        dkvb_ref[...] = dkv
        dkvn = _dot_nt(dkv, wkv_ref[...])
        khat, rk = _rms(lat_ref[:, Q_LORA:Q_LORA + KV_LORA], KV_LORA)
        dkvl = _rms_bwd(dkvn * gkv_ref[...], khat, rk, KV_LORA)
        dkr = dkf[:, 0:HP]
        for h in range(1, NH):
            dkr = dkr + dkf[:, h * HP:(h + 1) * HP]
        lane = lax.broadcasted_iota(jnp.int32, dkr.shape, 1)
        dkr = jnp.where((lane >= QK_NOPE) & (lane < QK_NOPE + QK_ROPE), dkr, 0.0)
        dkr = _roll(_rope_t(dkr, c, s1, s2), -64)

        @pl.when(i == 0)
        def _():
            accm_ref[...] = jnp.zeros_like(accm_ref)

        accm_ref[0:1, 0:Q_LORA] += _colsum(dqn * qhat)
        accm_ref[1:2, 0:KV_LORA] += _colsum(dkvn * khat)

        dproj = jnp.concatenate([du_ref[...], dql, dkvl, dkr], axis=1).astype(BF16)
        dproj_ref[...] = dproj
        dh = _dot_nt(dproj, w_ref[...])
        xhat, r = _rms(x_ref[...], D)
        g = g_ref[...]
        dn = dh * (1.0 + mod_ref[0, 1:2, :])
        dx_ref[...] = dx1_ref[...] + _rms_bwd(dn * g, xhat, r, D)

        @pl.when(i % tps == 0)
        def _():
            accs_ref[...] = jnp.zeros_like(accs_ref)

        @pl.when(i == 0)
        def _():
            accg_ref[...] = jnp.zeros_like(accg_ref)

        accs_ref[0, 0:1, :] += _colsum(dh)
        accs_ref[0, 1:2, :] += _colsum(dh * (xhat * g))
        accg_ref[0:1, :] += _colsum(dn * xhat)

    row = lambda w: pl.BlockSpec((tm, w), lambda i: (i, 0))
    return pl.pallas_call(
        body, name="f1_bwd", grid=(n // tm,),
        out_shape=[jax.ShapeDtypeStruct((n, D), F32), jax.ShapeDtypeStruct((n, IN_PAD), BF16),
                   jax.ShapeDtypeStruct((n, NH * HP), BF16), jax.ShapeDtypeStruct((n, 2 * NH * HP), BF16),
                   jax.ShapeDtypeStruct((nb, 8, D), F32), jax.ShapeDtypeStruct((8, D), F32),
                   jax.ShapeDtypeStruct((8, Q_LORA), F32)],
        in_specs=[row(D_SSM)] + [row(NH * HP)] * 3 + [row(IN_PAD - D_SSM), row(HP), row(HP), row(HP),
                                                     _VM, _VM, _VM, _VM, row(D), row(D),
                                                     pl.BlockSpec((1, 8, D), lambda i: (i // tps, 0, 0)), _VM, _VM],
        out_specs=[row(D), row(IN_PAD), row(NH * HP), row(2 * NH * HP),
                   pl.BlockSpec((1, 8, D), lambda i: (i // tps, 0, 0)), pl.BlockSpec((8, D), lambda i: (0, 0)),
                   pl.BlockSpec((8, Q_LORA), lambda i: (0, 0))],
        compiler_params=_cp(("arbitrary",)),
    )(du, dq, dk, dv, lat, rc, rs1, rs2, gq, gkv, w_uq, w_ukv, dx1, x, modp, g1, w_in)


def _ssm_param_fwd(lam_re, lam_im, logdt, b_re, b_im):
    def body(lr_ref, li_ref, ld_ref, br_ref, bi_ref, lbr_ref, lbi_ref, bbr_ref, bbi_ref):
        lr, li = lr_ref[...], li_ref[...]
        dt = jnp.exp(ld_ref[...])
        er = jnp.exp(lr * dt)
        lbr = er * jnp.cos(li * dt)
        lbi = er * jnp.sin(li * dt)
        den = 1.0 / (lr * lr + li * li)
        cr = ((lbr - 1.0) * lr + lbi * li) * den
        ci = (lbi * lr - (lbr - 1.0) * li) * den
        lbr_ref[...] = lbr
        lbi_ref[...] = lbi
        bbr_ref[...] = cr * br_ref[...] - ci * bi_ref[...]
        bbi_ref[...] = cr * bi_ref[...] + ci * br_ref[...]

    return pl.pallas_call(
        body, name="ssm_param_fwd",
        out_shape=[jax.ShapeDtypeStruct((NST, 1), F32)] * 2 + [jax.ShapeDtypeStruct((NST, H), F32)] * 2,
        in_specs=[_VM] * 5, out_specs=[_VM] * 4, compiler_params=_cp(),
    )(lam_re, lam_im, logdt, b_re, b_im)


def _ssm_param_bwd(lam_re, lam_im, logdt, b_re, b_im, dlb_re, dlb_im, dbb_re, dbb_im):
    def body(lr_ref, li_ref, ld_ref, br_ref, bi_ref, dlr_ref, dli_ref, dbr_ref, dbi_ref,
             gbr_ref, gbi_ref, glr_ref, gli_ref, gdt_ref):
        lr, li = lr_ref[...], li_ref[...]
        dt = jnp.exp(ld_ref[...])
        er = jnp.exp(lr * dt)
        lbr = er * jnp.cos(li * dt)
        lbi = er * jnp.sin(li * dt)
        den = 1.0 / (lr * lr + li * li)
        nr, ni = lbr - 1.0, lbi
        cr = (nr * lr + ni * li) * den
        ci = (ni * lr - nr * li) * den
        br, bi = br_ref[...], bi_ref[...]
        dbr, dbi = dbr_ref[...], dbi_ref[...]
        gbr_ref[...] = cr * dbr + ci * dbi
        gbi_ref[...] = cr * dbi - ci * dbr
        gcr = jnp.sum(dbr * br + dbi * bi, axis=1, keepdims=True)
        gci = jnp.sum(dbi * br - dbr * bi, axis=1, keepdims=True)
        ilr, ili = lr * den, -li * den
        glbr = dlr_ref[...] + (gcr * ilr + gci * ili)
        glbi = dli_ref[...] + (gci * ilr - gcr * ili)
        qr = -(cr * ilr - ci * ili)
        qi = -(cr * ili + ci * ilr)
        glr = gcr * qr + gci * qi
        gli = gci * qr - gcr * qi
        glr = glr + dt * (glbr * lbr + glbi * lbi)
        gli = gli + dt * (glbi * lbr - glbr * lbi)
        wr = lr * lbr - li * lbi
        wi = lr * lbi + li * lbr
        glr_ref[...] = glr
        gli_ref[...] = gli
        gdt_ref[...] = (glbr * wr + glbi * wi) * dt

    return pl.pallas_call(
        body, name="ssm_param_bwd",
        out_shape=[jax.ShapeDtypeStruct((NST, H), F32)] * 2 + [jax.ShapeDtypeStruct((NST, 1), F32)] * 3,
        in_specs=[_VM] * 9, out_specs=[_VM] * 5, compiler_params=_cp(),
    )(lam_re, lam_im, logdt, b_re, b_im, dlb_re, dlb_im, dbb_re, dbb_im)


def _rowsum(a):
    def body(a_ref, o_ref):
        o_ref[...] = jnp.sum(a_ref[...], axis=1, keepdims=True)

    return pl.pallas_call(
        body, name="rowsum", out_shape=jax.ShapeDtypeStruct((a.shape[0], 1), F32),
        in_specs=[_VM], out_specs=_VM, compiler_params=_cp(),
    )(a)


QB = D_SSM // 4
QS = 4 * QB


def _bd_lo(part, q):
    return part * NST + q * QS


def _bd_expand(ub, bm_ref, out_ref):
    for part in range(2):
        for q in range(4):
            lo = _bd_lo(part, q)
            out_ref[:, lo:lo + QS] = _dot(ub[:, q * QB:(q + 1) * QB], bm_ref[:, lo:lo + QS])


def _bd_expand_t(db, cm_ref, out_ref):
    for part in range(2):
        for q in range(4):
            lo = _bd_lo(part, q)
            out_ref[:, lo:lo + QS] = _dot_nt(db[:, q * QB:(q + 1) * QB], cm_ref[lo:lo + QS, :])


def _bd_project(sb, cm_ref):
    return jnp.concatenate(
        [_dot(sb[:, _bd_lo(0, q):_bd_lo(0, q) + QS], cm_ref[_bd_lo(0, q):_bd_lo(0, q) + QS, :])
         + _dot(sb[:, _bd_lo(1, q):_bd_lo(1, q) + QS], cm_ref[_bd_lo(1, q):_bd_lo(1, q) + QS, :])
         for q in range(4)], axis=1)


def _bd_project_t(ab, bm_ref):
    return jnp.concatenate(
        [_dot_nt(ab[:, _bd_lo(0, q):_bd_lo(0, q) + QS], bm_ref[:, _bd_lo(0, q):_bd_lo(0, q) + QS])
         + _dot_nt(ab[:, _bd_lo(1, q):_bd_lo(1, q) + QS], bm_ref[:, _bd_lo(1, q):_bd_lo(1, q) + QS])
         for q in range(4)], axis=1)


def _pow2k(pr, pi, nsq):
    for _ in range(nsq):
        pr, pi = pr * pr - pi * pi, 2.0 * pr * pi
    return pr, pi


def _ssm_local(u_p, bm, lre8, lim8, S, tt):
    n = u_p.shape[0]
    nb, nt = n // S, S // tt
    nsq = int(round(math.log2(S // 8)))
    assert 2 ** nsq == S // 8

    def body(u_ref, bm_ref, lre_ref, lim_ref, cre_ref, cim_ref, sre, sim, bu):
        j = pl.program_id(1)

        @pl.when(j == 0)
        def _():
            sre[...] = jnp.zeros_like(sre)
            sim[...] = jnp.zeros_like(sim)

        _bd_expand(u_ref[...].astype(BF16), bm_ref, bu)
        lre, lim = lre_ref[...], lim_ref[...]

        def step(i, c):
            sr, si = c
            off = pl.multiple_of(i * 8, 8)
            br = bu[pl.ds(off, 8), 0:NST]
            bi = bu[pl.ds(off, 8), NST:2 * NST]
            return lre * sr - lim * si + br, lre * si + lim * sr + bi

        sr, si = lax.fori_loop(0, tt // 8, step, (sre[...], sim[...]))
        sre[...] = sr
        sim[...] = si

        @pl.when(j == nt - 1)
        def _():
            pr, pi = _pow2k(lre[0:1], lim[0:1], nsq)
            cr = jnp.zeros((1, NST), F32)
            ci = jnp.zeros((1, NST), F32)
            cre_ref[0:1, :] = cr
            cim_ref[0:1, :] = ci
            for k in range(1, 8):
                cr, ci = sr[k - 1:k] + pr * cr - pi * ci, si[k - 1:k] + pr * ci + pi * cr
                cre_ref[k:k + 1, :] = cr
                cim_ref[k:k + 1, :] = ci

    return pl.pallas_call(
        body, name="ssm_local", grid=(nb, nt),
        out_shape=[jax.ShapeDtypeStruct((nb * 8, NST), F32)] * 2,
        in_specs=[pl.BlockSpec((tt, D_SSM), lambda b, j: (b * nt + j, 0)), _VM, _VM, _VM],
        out_specs=[pl.BlockSpec((8, NST), lambda b, j: (b, 0))] * 2,
        scratch_shapes=[pltpu.VMEM((8, NST), F32), pltpu.VMEM((8, NST), F32), pltpu.VMEM((tt, 2 * NST), F32)],
        compiler_params=_cp(("arbitrary", "arbitrary")),
    )(u_p, bm, lre8, lim8)


def _ssm_fwd(u_p, cre, cim, bm, cm, dvec, w_glu, lre8, lim8, S, tt):
    n = u_p.shape[0]
    nb, nt = n // S, S // tt

    def body(u_ref, cre_ref, cim_ref, bm_ref, cm_ref, d_ref, wg_ref, lre_ref, lim_ref,
             st_ref, ypre_ref, z_ref, gact_ref, yssm_ref, sre, sim, bu):
        j = pl.program_id(1)

        @pl.when(j == 0)
        def _():
            sre[...] = cre_ref[...]
            sim[...] = cim_ref[...]

        u = u_ref[...]
        _bd_expand(u.astype(BF16), bm_ref, bu)
        lre, lim = lre_ref[...], lim_ref[...]

        def step(i, c):
            sr, si = c
            off = pl.multiple_of(i * 8, 8)
            nr = lre * sr - lim * si + bu[pl.ds(off, 8), 0:NST]
            ni = lre * si + lim * sr + bu[pl.ds(off, 8), NST:2 * NST]
            bu[pl.ds(off, 8), 0:NST] = nr
            bu[pl.ds(off, 8), NST:2 * NST] = ni
            return nr, ni

        sr, si = lax.fori_loop(0, tt // 8, step, (sre[...], sim[...]))
        sre[...] = sr
        sim[...] = si
        stb = bu[...].astype(BF16)
        st_ref[...] = stb
        y = _bd_project(stb, cm_ref) + d_ref[...] * u
        ypre_ref[...] = y
        gb = _gelu(y).astype(BF16)
        gact_ref[...] = gb
        z = _dot(gb, wg_ref[...])
        z_ref[...] = z
        yssm_ref[...] = z[:, 0:D_SSM] * _sigmoid(z[:, D_SSM:2 * D_SSM])

    row = lambda w: pl.BlockSpec((tt, w), lambda b, j: (b * nt + j, 0))
    return pl.pallas_call(
        body, name="ssm_fwd", grid=(nb, nt),
        out_shape=[jax.ShapeDtypeStruct((n, 2 * NST), BF16), jax.ShapeDtypeStruct((n, D_SSM), F32),
                   jax.ShapeDtypeStruct((n, 2 * D_SSM), F32), jax.ShapeDtypeStruct((n, D_SSM), BF16),
                   jax.ShapeDtypeStruct((n, D_SSM), F32)],
        in_specs=[row(D_SSM), pl.BlockSpec((8, NST), lambda b, j: (b, 0)), pl.BlockSpec((8, NST), lambda b, j: (b, 0)),
                  _VM, _VM, _VM, _VM, _VM, _VM],
        out_specs=[row(2 * NST), row(D_SSM), row(2 * D_SSM), row(D_SSM), row(D_SSM)],
        scratch_shapes=[pltpu.VMEM((8, NST), F32), pltpu.VMEM((8, NST), F32), pltpu.VMEM((tt, 2 * NST), F32)],
        compiler_params=_cp(("arbitrary", "arbitrary")),
    )(u_p, cre, cim, bm, cm, dvec, w_glu, lre8, lim8)


def _ssm_bwd_a(dys_p, z, ypre, w_glu, cm, lre8, lim8, S, tt):
    n = z.shape[0]
    nb, nt = n // S, S // tt
    nsq = int(round(math.log2(S // 8)))
    ng = tt // 8

    def body(dys_ref, z_ref, y_ref, wg_ref, cm_ref, lre_ref, lim_ref, dy_ref, dz_ref, are_ref, aim_ref, sre, sim, gb):
        j = pl.program_id(1)

        @pl.when(j == 0)
        def _():
            sre[...] = jnp.zeros_like(sre)
            sim[...] = jnp.zeros_like(sim)

        z = z_ref[...]
        z1, z2 = z[:, 0:D_SSM], z[:, D_SSM:2 * D_SSM]
        sg = _sigmoid(z2)
        dys = dys_ref[...]
        dz = jnp.concatenate([dys * sg, dys * z1 * sg * (1.0 - sg)], axis=1).astype(BF16)
        dz_ref[...] = dz
        dy = _dot_nt(dz, wg_ref[...]) * _gelu_grad(y_ref[...])
        dy_ref[...] = dy
        _bd_expand_t(dy.astype(BF16), cm_ref, gb)
        lre, lim = lre_ref[...], lim_ref[...]

        def step(i, c):
            ar, ai = c
            off = pl.multiple_of((ng - 1 - i) * 8, 8)
            gr = gb[pl.ds(off, 8), 0:NST]
            gi = gb[pl.ds(off, 8), NST:2 * NST]
            return lre * ar + lim * ai + gr, lre * ai - lim * ar + gi

        ar, ai = lax.fori_loop(0, ng, step, (sre[...], sim[...]))
        sre[...] = ar
        sim[...] = ai

        @pl.when(j == nt - 1)
        def _():
            pr, pi = _pow2k(lre[0:1], -lim[0:1], nsq)
            cr = jnp.zeros((1, NST), F32)
            ci = jnp.zeros((1, NST), F32)
            are_ref[7:8, :] = cr
            aim_ref[7:8, :] = ci
            for k in range(6, -1, -1):
                cr, ci = ar[k + 1:k + 2] + pr * cr - pi * ci, ai[k + 1:k + 2] + pr * ci + pi * cr
                are_ref[k:k + 1, :] = cr
                aim_ref[k:k + 1, :] = ci

    row = lambda w: pl.BlockSpec((tt, w), lambda b, j: (b * nt + nt - 1 - j, 0))
    return pl.pallas_call(
        body, name="ssm_bwd_a", grid=(nb, nt),
        out_shape=[jax.ShapeDtypeStruct((n, D_SSM), F32), jax.ShapeDtypeStruct((n, 2 * D_SSM), BF16),
                   jax.ShapeDtypeStruct((nb * 8, NST), F32), jax.ShapeDtypeStruct((nb * 8, NST), F32)],
        in_specs=[row(D_SSM), row(2 * D_SSM), row(D_SSM), _VM, _VM, _VM, _VM],
        out_specs=[row(D_SSM), row(2 * D_SSM), pl.BlockSpec((8, NST), lambda b, j: (b, 0)),
                   pl.BlockSpec((8, NST), lambda b, j: (b, 0))],
        scratch_shapes=[pltpu.VMEM((8, NST), F32), pltpu.VMEM((8, NST), F32), pltpu.VMEM((tt, 2 * NST), F32)],
        compiler_params=_cp(("arbitrary", "arbitrary")),
    )(dys_p, z, ypre, w_glu, cm, lre8, lim8)


def _ssm_bwd_b(dy, u_p, st, fcr, fci, air, aii, bm, cm, dvec, lre8, lim8, S, tt):
    n = u_p.shape[0]
    nb, nt = n // S, S // tt
    ng = tt // 8

    def body(dy_ref, u_ref, st_ref, stp_ref, fcr_ref, fci_ref, air_ref, aii_ref, bm_ref, cm_ref, d_ref, lre_ref, lim_ref,
             du_ref, dcm_ref, dbm_ref, dd_ref, dlr_ref, dli_ref, are, aim, accr, acci, sp, ab):
        b = pl.program_id(0)
        j = pl.program_id(1)
        jt = nt - 1 - j

        @pl.when((b == 0) & (j == 0))
        def _():
            dcm_ref[...] = jnp.zeros_like(dcm_ref)
            dbm_ref[...] = jnp.zeros_like(dbm_ref)
            dd_ref[...] = jnp.zeros_like(dd_ref)
            accr[...] = jnp.zeros_like(accr)
            acci[...] = jnp.zeros_like(acci)

        @pl.when(j == 0)
        def _():
            are[...] = air_ref[...]
            aim[...] = aii_ref[...]

        sp[8:tt + 8, :] = st_ref[...].astype(F32)

        @pl.when(jt == 0)
        def _():
            sp[0:8, 0:NST] = fcr_ref[...]
            sp[0:8, NST:2 * NST] = fci_ref[...]

        @pl.when(jt != 0)
        def _():
            sp[0:8, :] = stp_ref[8:16, :].astype(F32)

        dy = dy_ref[...]
        u = u_ref[...]
        dyb = dy.astype(BF16)
        _bd_expand_t(dyb, cm_ref, ab)
        lre, lim = lre_ref[...], lim_ref[...]

        def step(i, c):
            ar, ai = c
            off = pl.multiple_of((ng - 1 - i) * 8, 8)
            nr = lre * ar + lim * ai + ab[pl.ds(off, 8), 0:NST]
            ni = lre * ai - lim * ar + ab[pl.ds(off, 8), NST:2 * NST]
            ab[pl.ds(off, 8), 0:NST] = nr
            ab[pl.ds(off, 8), NST:2 * NST] = ni
            pr = sp[pl.ds(off, 8), 0:NST]
            pi = sp[pl.ds(off, 8), NST:2 * NST]
            accr[...] += nr * pr + ni * pi
            acci[...] += ni * pr - nr * pi
            return nr, ni

        ar, ai = lax.fori_loop(0, ng, step, (are[...], aim[...]))
        are[...] = ar
        aim[...] = ai
        a_b = ab[...].astype(BF16)
        du_ref[...] = _bd_project_t(a_b, bm_ref) + d_ref[...] * dy
        ub = u.astype(BF16)
        for q in range(4):
            for part in range(2):
                lo = part * NST + q * 4 * QB
                s_q = st_ref[:, lo:lo + 4 * QB]
                dcm_ref[lo:lo + 4 * QB, :] += _dot_tn(s_q, dyb[:, q * QB:(q + 1) * QB])
                dbm_ref[:, lo:lo + 4 * QB] += _dot_tn(ub[:, q * QB:(q + 1) * QB], a_b[:, lo:lo + 4 * QB])
        dd_ref[...] += _colsum(dy * u)

        @pl.when((b == nb - 1) & (j == nt - 1))
        def _():
            dlr_ref[...] = _colsum(accr[...])
            dli_ref[...] = _colsum(acci[...])

    row = lambda w: pl.BlockSpec((tt, w), lambda b, j: (b * nt + nt - 1 - j, 0))
    seq8 = pl.BlockSpec((8, NST), lambda b, j: (b, 0))
    prev = pl.BlockSpec((16, 2 * NST), lambda b, j: (jnp.maximum((b * nt + nt - 1 - j) * (tt // 16) - 1, 0), 0))
    const = lambda shape: pl.BlockSpec(shape, lambda b, j: (0, 0))
    return pl.pallas_call(
        body, name="ssm_bwd_b", grid=(nb, nt),
        out_shape=[jax.ShapeDtypeStruct((n, D_SSM), F32), jax.ShapeDtypeStruct((2 * NST, QB), F32),
                   jax.ShapeDtypeStruct((QB, 2 * NST), F32), jax.ShapeDtypeStruct((1, D_SSM), F32),
                   jax.ShapeDtypeStruct((1, NST), F32), jax.ShapeDtypeStruct((1, NST), F32)],
        in_specs=[row(D_SSM), row(D_SSM), row(2 * NST), prev, seq8, seq8, seq8, seq8, _VM, _VM, _VM, _VM, _VM],
        out_specs=[row(D_SSM), const((2 * NST, QB)), const((QB, 2 * NST)), const((1, D_SSM)),
                   const((1, NST)), const((1, NST))],
        scratch_shapes=[pltpu.VMEM((8, NST), F32)] * 4 + [pltpu.VMEM((tt + 8, 2 * NST), F32),
                                                          pltpu.VMEM((tt, 2 * NST), F32)],
        compiler_params=_cp(("arbitrary", "arbitrary")),
    )(dy, u_p, st, st, fcr, fci, air, aii, bm, cm, dvec, lre8, lim8)


def _rope(v, c, s1, s2):
    return v * c + _roll(v, -16) * s1 + _roll(v, 16) * s2


def _rope_t(dv, c, s1, s2):
    return dv * c + _roll(dv * s1, 16) + _roll(dv * s2, -16)


_SCALE = (QK_NOPE + QK_ROPE) ** -0.5
_LOG2E = 1.4426950408889634
_C2 = _SCALE * _LOG2E


def _attn_fwd(q, k, v, S, tq):
    n = q.shape[0]
    nb, nq = n // S, S // tq

    def body(q_ref, k_ref, v_ref, o_ref, lr_ref):
        qi = pl.program_id(2)
        qv = q_ref[...]

        def tile(j, c, diagonal):
            m, acc = c
            off = pl.multiple_of(j * tq, tq)
            s = _dot_nt(qv, k_ref[pl.ds(off, tq), :])
            if diagonal:
                rows = lax.broadcasted_iota(jnp.int32, s.shape, 0)
                cols = lax.broadcasted_iota(jnp.int32, s.shape, 1)
                s = jnp.where(cols <= rows, s, NEG)
            mn = jnp.maximum(m, jnp.max(s, axis=1, keepdims=True))
            p = jnp.exp2(s - mn)
            acc = jnp.exp2(m - mn) * acc + _dot(p.astype(BF16), v_ref[pl.ds(off, tq), :])
            return mn, acc

        init = (jnp.full((tq, 1), NEG, F32), jnp.zeros((tq, HP), F32))
        c = lax.fori_loop(0, qi, lambda j, c: tile(j, c, False), init)
        m, acc = tile(qi, c, True)
        l = acc[:, V_HEAD:V_HEAD + 1]
        vlane = lax.broadcasted_iota(jnp.int32, acc.shape, 1)
        o_ref[...] = jnp.where(vlane < V_HEAD, acc / l, 0.0).astype(BF16)
        lane = lax.broadcasted_iota(jnp.int32, (8, HP), 1)
        lse = jnp.broadcast_to(m + jnp.log(l) * _LOG2E, (tq, HP))
        lr_ref[...] = _rows_of(lse, jnp.where(lane == 0, 1.0, 0.0).astype(BF16))

    qs = pl.BlockSpec((tq, HP), lambda b, h, i: (b * nq + i, h))
    ks = pl.BlockSpec((S, HP), lambda b, h, i: (b, h))
    return pl.pallas_call(
        body, name="attn_fwd", grid=(nb, NH, nq),
        out_shape=[jax.ShapeDtypeStruct((n, NH * HP), BF16), jax.ShapeDtypeStruct((nb * NH * 8, S), F32)],
        in_specs=[qs, ks, ks], out_specs=[qs, pl.BlockSpec((8, tq), lambda b, h, i: (b * NH + h, i))],
        compiler_params=_cp(("parallel", "parallel", "arbitrary")),
    )(q, k, v)


def _rows_of(x, pick):
    x1 = x.astype(BF16)
    r1 = x - x1.astype(F32)
    x2 = r1.astype(BF16)
    x3 = (r1 - x2.astype(F32)).astype(BF16)
    return _dot_nt(pick, x1) + _dot_nt(pick, x2) + _dot_nt(pick, x3)


def _attn_bwd(q, k, v, dob, lrow, drow, S, tq):
    n = q.shape[0]
    nb, nq = n // S, S // tq

    def body(q_ref, k_ref, v_ref, do_ref, lr_ref, dr_ref, dqo_ref, dk_ref, dv_ref, dq_ref):
        kj = pl.program_id(2)

        @pl.when(kj == 0)
        def _():
            dq_ref[...] = jnp.zeros_like(dq_ref)

        kt = k_ref[...]
        vt = v_ref[...]

        def tile(i, c, diagonal):
            dk, dv = c
            off = pl.multiple_of(i * tq, tq)
            qv = q_ref[pl.ds(off, tq), :]
            dob = do_ref[pl.ds(off, tq), :]
            lr = lr_ref[0:1, pl.ds(off, tq)]
            dr = dr_ref[0:1, pl.ds(off, tq)]
            st = _dot_nt(kt, qv)
            dpt = _dot_nt(vt, dob)
            pt = jnp.exp2(st - lr)
            if diagonal:
                keys = lax.broadcasted_iota(jnp.int32, pt.shape, 0)
                qrys = lax.broadcasted_iota(jnp.int32, pt.shape, 1)
                pt = jnp.where(keys <= qrys, pt, 0.0)
            dst = (pt * (dpt - dr)).astype(BF16)
            dq_ref[pl.ds(off, tq), :] += _dot_tn(dst, kt)
            return dk + _dot(dst, qv), dv + _dot(pt.astype(BF16), dob)

        zero = jnp.zeros((tq, HP), F32)
        c = tile(kj, (zero, zero), True)
        dk, dv = lax.fori_loop(kj + 1, nq, lambda i, c: tile(i, c, False), c)
        dk_ref[...] = dk.astype(BF16)
        dv_ref[...] = dv.astype(BF16)

        @pl.when(kj == nq - 1)
        def _():
            dqo_ref[...] = dq_ref[...].astype(BF16)

    ts = pl.BlockSpec((tq, HP), lambda b, h, i: (b * nq + i, h))
    fs = pl.BlockSpec((S, HP), lambda b, h, i: (b, h))
    rs = pl.BlockSpec((8, S), lambda b, h, i: (b * NH + h, 0))
    return pl.pallas_call(
        body, name="attn_bwd", grid=(nb, NH, nq),
        out_shape=[jax.ShapeDtypeStruct((n, NH * HP), BF16)] * 3,
        in_specs=[fs, ts, ts, fs, rs, rs], out_specs=[fs, ts, ts],
        scratch_shapes=[pltpu.VMEM((S, HP), F32)],
        compiler_params=_cp(("parallel", "parallel", "arbitrary")),
    )(q, k, v, dob, lrow, drow)


def _p2(yssm, oattn, x, target, modp, gs, ga, w_out, g2, gf, w_ff1, w_ff2, S, tm):
    n = x.shape[0]
    tps = S // tm
    nb = n // S

    def body(ys_ref, oa_ref, x_ref, t_ref, mod_ref, gs_ref, ga_ref, wo_ref, g2_ref, gf_ref, w1_ref, w2_ref,
             yn_ref, o_ref, h2_ref, dx1_ref, r_ref, da_ref, dff_ref, accs_ref, accg_ref):
        i = pl.program_id(0)
        sh2, sc2, gt2 = mod_ref[0, 3:4, :], mod_ref[0, 4:5, :], mod_ref[0, 5:6, :]
        fsh, fsc = mod_ref[0, 6:7, :], mod_ref[0, 7:8, :]
        yh, _ = _rms(ys_ref[...], D_SSM)
        ah, _ = _rms(oa_ref[...].astype(F32), D_ATTN)
        yn = jnp.concatenate([yh * gs_ref[...], ah * ga_ref[...]], axis=1).astype(BF16)
        yn_ref[...] = yn
        o = _dot(yn, wo_ref[...])
        o_ref[...] = o.astype(BF16)
        x1 = x_ref[...] + mod_ref[0, 2:3, :] * o
        x1h, r2 = _rms(x1, D)
        g2_v = g2_ref[...]
        h2 = ((x1h * g2_v) * (1.0 + sc2) + sh2).astype(BF16)
        h2_ref[...] = h2
        a = _dot(h2, w1_ref[...])
        ra = jnp.maximum(a, 0.0)
        rb = (ra * ra).astype(BF16)
        r_ref[...] = rb
        ff = _dot(rb, w2_ref[...])
        x2 = x1 + gt2 * ff
        x2h, rf = _rms(x2, D)
        gf_v = gf_ref[...]
        outn = x2h * gf_v
        err = outn * (1.0 + fsc) + fsh - t_ref[...]
        dout = err * (1.0 / D)
        doutn = dout * (1.0 + fsc)
        dx2 = _rms_bwd(doutn * gf_v, x2h, rf, D)
        dff = (gt2 * dx2).astype(BF16)
        dff_ref[...] = dff
        dr = _dot_nt(dff, w2_ref[...])
        da = (dr * (2.0 * ra)).astype(BF16)
        da_ref[...] = da
        dh2 = _dot_nt(da, w1_ref[...])
        dn2 = dh2 * (1.0 + sc2)
        dx1_ref[...] = dx2 + _rms_bwd(dn2 * g2_v, x1h, r2, D)

        @pl.when(i % tps == 0)
        def _():
            accs_ref[...] = jnp.zeros_like(accs_ref)

        @pl.when(i == 0)
        def _():
            accg_ref[...] = jnp.zeros_like(accg_ref)

        accs_ref[0, 3:4, :] += _colsum(dh2)
        accs_ref[0, 4:5, :] += _colsum(dh2 * (x1h * g2_v))
        accs_ref[0, 5:6, :] += _colsum(dx2 * ff)
        accs_ref[0, 6:7, :] += _colsum(dout)
        accs_ref[0, 7:8, :] += _colsum(dout * outn)
        accg_ref[0:1, :] += _colsum(dn2 * x1h)
        accg_ref[1:2, :] += _colsum(doutn * x2h)
        accg_ref[2:3, :] += _colsum(err * err) * (0.5 / D)

    row = lambda w: pl.BlockSpec((tm, w), lambda i: (i, 0))
    return pl.pallas_call(
        body, name="p2_mlp_loss", grid=(n // tm,),
        out_shape=[jax.ShapeDtypeStruct((n, D_SSM + NH * HP), BF16), jax.ShapeDtypeStruct((n, D), BF16),
                   jax.ShapeDtypeStruct((n, D), BF16),
                   jax.ShapeDtypeStruct((n, D), F32), jax.ShapeDtypeStruct((n, D_FF), BF16),
                   jax.ShapeDtypeStruct((n, D_FF), BF16), jax.ShapeDtypeStruct((n, D), BF16),
                   jax.ShapeDtypeStruct((nb, 8, D), F32), jax.ShapeDtypeStruct((8, D), F32)],
        in_specs=[row(D_SSM), row(NH * HP), row(D), row(D), pl.BlockSpec((1, 8, D), lambda i: (i // tps, 0, 0)),
                  _VM, _VM, _VM, _VM, _VM, _VM, _VM],
        out_specs=[row(D_SSM + NH * HP), row(D), row(D),
                   row(D), row(D_FF), row(D_FF), row(D), pl.BlockSpec((1, 8, D), lambda i: (i // tps, 0, 0)),
                   pl.BlockSpec((8, D), lambda i: (0, 0))],
        compiler_params=_cp(("arbitrary",)),
    )(yssm, oattn, x, target, modp, gs, ga, w_out, g2, gf, w_ff1, w_ff2)


def _p3_bwd(dx1, o, yssm, oattn, modp, gs, ga, w_out, S, tm):
    n = dx1.shape[0]
    tps = S // tm
    nb = n // S

    def body(dx1_ref, o_ref, ys_ref, oa_ref, mod_ref, gs_ref, ga_ref, w_ref,
             do_ref, dys_ref, doa_ref, dr_ref, accs_ref, accg_ref):
        i = pl.program_id(0)
        dx1 = dx1_ref[...]
        dob = (mod_ref[0, 2:3, :] * dx1).astype(BF16)
        do_ref[...] = dob
        dyn = _dot_nt(dob, w_ref[...])
        yh, rs = _rms(ys_ref[...], D_SSM)
        oa = oa_ref[...].astype(F32)
        ah, ra = _rms(oa, D_ATTN)
        d1 = dyn[:, 0:D_SSM]
        d2 = dyn[:, D_SSM:D_SSM + NH * HP]
        dys_ref[...] = _rms_bwd(d1 * gs_ref[...], yh, rs, D_SSM)
        doa = _rms_bwd(d2 * ga_ref[...], ah, ra, D_ATTN)
        doa_ref[...] = doa.astype(BF16)
        prod = doa * oa
        ones = jnp.ones((8, HP), BF16)
        for h in range(NH):
            dr_ref[h * 8:(h + 1) * 8, :] = _rows_of(prod[:, h * HP:(h + 1) * HP], ones)

        @pl.when(i % tps == 0)
        def _():
            accs_ref[...] = jnp.zeros_like(accs_ref)

        @pl.when(i == 0)
        def _():
            accg_ref[...] = jnp.zeros_like(accg_ref)

        accs_ref[0, 2:3, :] += _colsum(dx1 * o_ref[...])
        accg_ref[0:1, 0:D_SSM] += _colsum(d1 * yh)
        accg_ref[1:2, :] += _colsum(d2 * ah)

    row = lambda w: pl.BlockSpec((tm, w), lambda i: (i, 0))
    return pl.pallas_call(
        body, name="p3_bwd", grid=(n // tm,),
        out_shape=[jax.ShapeDtypeStruct((n, D), BF16), jax.ShapeDtypeStruct((n, D_SSM), F32),
                   jax.ShapeDtypeStruct((n, NH * HP), BF16), jax.ShapeDtypeStruct((nb * NH * 8, S), F32),
                   jax.ShapeDtypeStruct((nb, 8, D), F32), jax.ShapeDtypeStruct((8, NH * HP), F32)],
        in_specs=[row(D), row(D), row(D_SSM), row(NH * HP), pl.BlockSpec((1, 8, D), lambda i: (i // tps, 0, 0)),
                  _VM, _VM, _VM],
        out_specs=[row(D), row(D_SSM), row(NH * HP), pl.BlockSpec((NH * 8, tm), lambda i: (i // tps, i % tps)),
                   pl.BlockSpec((1, 8, D), lambda i: (i // tps, 0, 0)), pl.BlockSpec((8, NH * HP), lambda i: (0, 0))],
        compiler_params=_cp(("arbitrary",)),
    )(dx1, o, yssm, oattn, modp, gs, ga, w_out)


def _wgrad(a, b, name, col_slots=0):
    n, k1 = a.shape
    k2 = b.shape[1]
    bn = next((b for b in (1024, 512) if n % b == 0), n)
    bk1 = next((b for b in (1024, 512) if k1 % b == 0), k1)
    bk2 = k2 // col_slots if col_slots else (1024 if (k2 % 1024 == 0) else k2)

    def body(a_ref, b_ref, o_ref):
        @pl.when(pl.program_id(2) == 0)
        def _():
            o_ref[...] = jnp.zeros_like(o_ref)

        o_ref[...] += _dot_tn(a_ref[...], b_ref[...]).reshape(o_ref.shape)

    if col_slots:
        out_shape = jax.ShapeDtypeStruct((col_slots, k1, bk2), F32)
        out_spec = pl.BlockSpec((1, bk1, bk2), lambda i, j, t: (j, i, 0))
    else:
        out_shape = jax.ShapeDtypeStruct((k1, k2), F32)
        out_spec = pl.BlockSpec((bk1, bk2), lambda i, j, t: (i, j))
    return pl.pallas_call(
        body, name=name, grid=(k1 // bk1, k2 // bk2, n // bn),
        out_shape=out_shape,
        in_specs=[pl.BlockSpec((bn, bk1), lambda i, j, t: (t, i)), pl.BlockSpec((bn, bk2), lambda i, j, t: (t, j))],
        out_specs=out_spec,
        compiler_params=_cp(("parallel", "parallel", "arbitrary")),
    )(a, b)


def _row_block(rows):
    if rows <= 256:
        return rows
    return next(b for b in (256, 192, 128, 64, 32, 16, 8) if rows % b == 0)


def _add_half(g, recv, cidx, name):
    _, rows2, w = g.shape
    rows = rows2 // 2
    br = _row_block(rows)
    nblk = rows // br

    def body(c_ref, g_ref, r_ref, o_ref):
        o_ref[...] = (g_ref[...] + r_ref[...]).astype(BF16)

    return pl.pallas_call(
        body, name=name,
        grid_spec=pltpu.PrefetchScalarGridSpec(
            num_scalar_prefetch=1, grid=(4, nblk),
            in_specs=[pl.BlockSpec((1, br, w), lambda s, i, c: (s, c[0] * nblk + i, 0)),
                      pl.BlockSpec((1, br, w), lambda s, i, c: (s, i, 0))],
            out_specs=pl.BlockSpec((1, br, w), lambda s, i, c: (s, i, 0))),
        out_shape=jax.ShapeDtypeStruct((4, rows, w), BF16),
        compiler_params=_cp(("parallel", "parallel")),
    )(cidx, g, recv)


def _add_chips(r, name):
    _, rows, w = r.shape
    br = _row_block(rows)

    def body(r_ref, o_ref):
        f = lambda k: r_ref[k].astype(F32)
        o_ref[...] = ((f(0) + f(1)) + f(2)) + f(3)

    return pl.pallas_call(
        body, name=name, grid=(rows // br,),
        out_shape=jax.ShapeDtypeStruct((rows, w), F32),
        in_specs=[pl.BlockSpec((4, br, w), lambda i: (0, i, 0))],
        out_specs=pl.BlockSpec((br, w), lambda i: (i, 0)),
        compiler_params=_cp(("parallel",)),
    )(r)


def _pair_sum(a, sa, b, sb):
    def body(a_ref, sa_ref, b_ref, sb_ref, oa_ref, ob_ref):
        oa_ref[...] = (a_ref[...].astype(F32) + sa_ref[...].astype(F32)).astype(BF16)
        ob_ref[...] = b_ref[...] + sb_ref[...]

    return pl.pallas_call(
        body, name="small_grad_pair_sum",
        out_shape=[jax.ShapeDtypeStruct(a.shape, BF16), jax.ShapeDtypeStruct(b.shape, F32)],
        in_specs=[_VM] * 4, out_specs=[_VM, _VM], compiler_params=_cp(),
    )(a, sa, b, sb)


def _sum_devices(a, b):
    def body(a_ref, b_ref, oa_ref, ob_ref):
        acc = a_ref[0:1, :].astype(F32)
        accb = b_ref[0:1, :]
        for k in range(1, a.shape[0]):
            acc = acc + a_ref[k:k + 1, :].astype(F32)
            accb = accb + b_ref[k:k + 1, :]
        oa_ref[...] = acc
        ob_ref[...] = accb

    return pl.pallas_call(
        body, name="small_grad_sum",
        out_shape=[jax.ShapeDtypeStruct((1, a.shape[1]), F32), jax.ShapeDtypeStruct((1, b.shape[1]), F32)],
        in_specs=[_VM, _VM], out_specs=[_VM, _VM], compiler_params=_cp(),
    )(a, b)


def _adamw_math(wv, gv, mv, vv):
    m_new = ADAM_B1 * mv + (1.0 - ADAM_B1) * gv
    v_new = ADAM_B2 * vv + (1.0 - ADAM_B2) * (gv * gv)
    m_hat = m_new / (1.0 - ADAM_B1 ** ADAM_STEP)
    v_hat = v_new / (1.0 - ADAM_B2 ** ADAM_STEP)
    return -ADAM_LR * (m_hat / (jnp.sqrt(v_hat) + ADAM_EPS) + ADAM_WD * wv), m_new, v_new


def _adamw_small(ws, gs, ms, vs):
    k = len(ws)

    def body(*refs):
        ins, outs = refs[:4 * k], refs[4 * k:]
        for t in range(k):
            d, m_new, v_new = _adamw_math(ins[t][...], ins[k + t][...], ins[2 * k + t][...], ins[3 * k + t][...])
            outs[t][...] = d
            outs[k + t][...] = m_new
            outs[2 * k + t][...] = v_new

    shapes = [jax.ShapeDtypeStruct(w.shape, F32) for w in ws]
    return pl.pallas_call(
        body, name="adamw_small", out_shape=shapes * 3,
        in_specs=[_VM] * (4 * k), out_specs=[_VM] * (3 * k), compiler_params=_cp(),
    )(*ws, *gs, *ms, *vs)


def _adamw(w, g, m, v, name):
    rows, wd = w.shape
    br = _row_block(rows)

    def body(w_ref, g_ref, m_ref, v_ref, d_ref, nm_ref, nv_ref):
        d, m_new, v_new = _adamw_math(w_ref[...], g_ref[...], m_ref[...], v_ref[...])
        d_ref[...] = d
        nm_ref[...] = m_new
        nv_ref[...] = v_new

    spec = pl.BlockSpec((br, wd), lambda i: (i, 0))
    return pl.pallas_call(
        body, name=name, grid=(rows // br,),
        out_shape=[jax.ShapeDtypeStruct((rows, wd), F32)] * 3,
        in_specs=[spec] * 4, out_specs=[spec] * 3,
        compiler_params=_cp(("parallel",)),
    )(w, g, m, v)


def _adamw_halves(w, mine, other, m, v, cidx, name):
    rows, wd = w.shape
    h = rows // 2
    br = _row_block(h)
    nblk = h // br

    def body(c_ref, w_ref, a_ref, b_ref, m_ref, v_ref, g_ref, d_ref, nm_ref, nv_ref):
        gv = jnp.where(pl.program_id(0) == c_ref[0], a_ref[...], b_ref[...])
        d, m_new, v_new = _adamw_math(w_ref[...], gv, m_ref[...], v_ref[...])
        g_ref[...] = gv
        d_ref[...] = d
        nm_ref[...] = m_new
        nv_ref[...] = v_new

    full = pl.BlockSpec((br, wd), lambda hf, i, c: (hf * nblk + i, 0))
    half = pl.BlockSpec((br, wd), lambda hf, i, c: (i, 0))
    return pl.pallas_call(
        body, name=name,
        grid_spec=pltpu.PrefetchScalarGridSpec(
            num_scalar_prefetch=1, grid=(2, nblk),
            in_specs=[full, half, half, full, full], out_specs=[full] * 4),
        out_shape=[jax.ShapeDtypeStruct((rows, wd), F32)] * 4,
        compiler_params=_cp(("parallel", "parallel")),
    )(cidx, w, mine, other, m, v)


def _other_chips(x, y):
    return [(1 - x, y), (x, 1 - y), (1 - x, 1 - y)]


def _other_devices(x, y, c):
    flip = lambda v, d: (1 - v) if d else v
    return [(flip(x, dx), flip(y, dy), flip(c, dc))
            for dx in (0, 1) for dy in (0, 1) for dc in (0, 1) if (dx, dy, dc) != (0, 0, 0)]


def _exchange(name, ins, out_shapes, n_local, n_remote, plan):
    ni, no = len(ins), len(out_shapes)

    def body(*refs):
        in_refs, out_refs = refs[:ni], refs[ni:ni + no]
        send_sems, recv_sems, local_sems = refs[ni + no:]
        x, y, c = lax.axis_index("x"), lax.axis_index("y"), lax.axis_index("c")
        local, remote = plan(in_refs, out_refs, x, y, c)
        assert len(local) == n_local and len(remote) == n_remote

        def push(k, src, dst, dev):
            return pltpu.make_async_remote_copy(src_ref=src, dst_ref=dst, send_sem=send_sems.at[k],
                                                recv_sem=recv_sems.at[k], device_id=dev, device_id_type=MESH)

        own = [pltpu.make_async_copy(s, d, local_sems.at[i]) for i, (s, d) in enumerate(local)]
        for cp in own:
            cp.start()
        sends = [push(k, s, d, dev) for k, (s, d, dev, _) in enumerate(remote)]
        for cp in sends:
            cp.start()
        for k, (s, _, dev, landing) in enumerate(remote):
            push(k, s, landing, dev).wait_recv()
        for cp in sends:
            cp.wait_send()
        for cp in own:
            cp.wait()

    return pl.pallas_call(
        body, name=name, out_shape=out_shapes,
        in_specs=[_ANY] * ni, out_specs=[_ANY] * no,
        scratch_shapes=[pltpu.SemaphoreType.DMA((n_remote,)), pltpu.SemaphoreType.DMA((n_remote,)),
                        pltpu.SemaphoreType.DMA((max(n_local, 1),))],
        compiler_params=pltpu.CompilerParams(has_side_effects=True),
    )(*ins)


def _gather_chips(name, shards, everyone=()):
    ns, ne = len(shards), len(everyone)
    outs = [jax.ShapeDtypeStruct((4,) + a.shape, a.dtype) for a in shards]
    outs += [jax.ShapeDtypeStruct((8,) + a.shape, a.dtype) for a in everyone]

    def plan(i, o, x, y, c):
        mine, me = 2 * x + y, 4 * x + 2 * y + c
        local, remote = [], []
        for t in range(ns):
            local.append((i[t], o[t].at[mine]))
            for px, py in _other_chips(x, y):
                remote.append((i[t], o[t].at[mine], (px, py, c), o[t].at[2 * px + py]))
        for t in range(ns, ns + ne):
            local.append((i[t], o[t].at[me]))
            for px, py, pc in _other_devices(x, y, c):
                remote.append((i[t], o[t].at[me], (px, py, pc), o[t].at[4 * px + 2 * py + pc]))
        return local, remote

    return _exchange(name, list(shards) + list(everyone), outs, ns + ne, 3 * ns + 7 * ne, plan)


_HBM = pl.BlockSpec(memory_space=pltpu.HBM)
_SEM = pl.BlockSpec(memory_space=pltpu.SEMAPHORE)
_EFFECT = pltpu.SideEffectType.DATAFLOW_SIDE_EFFECTING


def _split_start(name, ins, land_shapes, n_remote, plan, after):
    ni, nl = len(ins), len(land_shapes)
    srcs = [pltpu.with_memory_space_constraint(a, pltpu.HBM) for a in ins]
    lands = [pltpu.with_memory_space_constraint(lax.empty(s.shape, s.dtype), pltpu.HBM) for s in land_shapes]

    def body(*refs):
        src, land = refs[:ni], refs[ni:ni + nl]
        first = ni + nl + 1
        send, recv = refs[first:first + n_remote], refs[first + n_remote:first + 2 * n_remote]
        token = refs[first + 2 * n_remote + ni + nl]
        x, y, c = lax.axis_index("x"), lax.axis_index("y"), lax.axis_index("c")
        remote = plan(src, land, x, y, c)
        assert len(remote) == n_remote
        for k, (s, d, dev, _) in enumerate(remote):
            pltpu.make_async_remote_copy(src_ref=s, dst_ref=d, send_sem=send[k], recv_sem=recv[k],
                                         device_id=dev, device_id_type=MESH).start()
        token[...] = jnp.zeros_like(token)

    out = pl.pallas_call(
        body, name=name + "_start",
        out_shape=[pltpu.SemaphoreType.DMA(())] * (2 * n_remote)
                  + [pltpu.HBM(a.shape, a.dtype) for a in ins] + [pltpu.HBM(s.shape, s.dtype) for s in land_shapes]
                  + [jax.ShapeDtypeStruct((8, 128), F32)],
        in_specs=[_HBM] * (ni + nl) + [_ANY], out_specs=[_SEM] * (2 * n_remote) + [_HBM] * (ni + nl) + [_VM],
        input_output_aliases={t: 2 * n_remote + t for t in range(ni + nl)},
        compiler_params=pltpu.CompilerParams(has_side_effects=_EFFECT),
    )(*srcs, *lands, after)
    sems, thru = out[:2 * n_remote], out[2 * n_remote:2 * n_remote + ni + nl]
    return (name, sems, thru[:ni], thru[ni:], n_remote, plan), out[-1]


def _split_wait(handle, after):
    name, sems, srcs, lands, n_remote, plan = handle
    ni, nl = len(srcs), len(lands)

    def body(*refs):
        src, land = refs[:ni], refs[ni:ni + nl]
        send, recv = refs[ni + nl:ni + nl + n_remote], refs[ni + nl + n_remote:ni + nl + 2 * n_remote]
        x, y, c = lax.axis_index("x"), lax.axis_index("y"), lax.axis_index("c")
        for k, (s, _, dev, landing) in enumerate(plan(src, land, x, y, c)):
            cp = pltpu.make_async_remote_copy(src_ref=s, dst_ref=landing, send_sem=send[k], recv_sem=recv[k],
                                              device_id=dev, device_id_type=MESH)
            cp.wait_send()
            cp.wait_recv()

    out = pl.pallas_call(
        body, name=name + "_wait",
        out_shape=[pltpu.HBM(a.shape, a.dtype) for a in srcs] + [pltpu.HBM(a.shape, a.dtype) for a in lands],
        in_specs=[_HBM] * (ni + nl) + [_SEM] * (2 * n_remote) + [_ANY], out_specs=[_HBM] * (ni + nl),
        input_output_aliases={t: t for t in range(ni + nl)},
        compiler_params=pltpu.CompilerParams(has_side_effects=_EFFECT),
    )(*srcs, *lands, *sems, after)
    return out[:ni], out[ni:]


def _plan_to_chips(src, land, x, y, c):
    mine = 2 * x + y
    return [(src[t], land[t].at[mine], (px, py, c), land[t].at[2 * px + py])
            for t in range(len(src)) for px, py in _other_chips(x, y)]


def _plan_swap_halves(src, land, x, y, c):
    out = []
    for t in range(len(src)):
        h = src[t].shape[1] // 2
        out.append((src[t].at[:, pl.ds(pl.multiple_of((1 - c) * h, 8), h), :], land[t], (x, y, 1 - c), land[t]))
    return out


def _plan_scatter_chips(src, land, x, y, c):
    mine = 2 * x + y
    return [(src[t].at[2 * px + py], land[t].at[mine], (px, py, c), land[t].at[2 * px + py])
            for t in range(len(src)) for px, py in _other_chips(x, y)]


def _swap_halves(gs, everyone, whole):
    ns, ne, nw = len(gs), len(everyone), len(whole)
    outs = [jax.ShapeDtypeStruct((4, g.shape[1] // 2, g.shape[2]), g.dtype) for g in gs]
    outs += [jax.ShapeDtypeStruct((8,) + a.shape, a.dtype) for a in everyone]
    outs += [jax.ShapeDtypeStruct(a.shape, a.dtype) for a in whole]

    def plan(i, o, x, y, c):
        me = 4 * x + 2 * y + c
        local, remote = [], []
        for t in range(ns):
            h = gs[t].shape[1] // 2
            theirs = i[t].at[:, pl.ds(pl.multiple_of((1 - c) * h, 8), h), :]
            remote.append((theirs, o[t], (x, y, 1 - c), o[t]))
        for t in range(ns, ns + ne):
            local.append((i[t], o[t].at[me]))
            for px, py, pc in _other_devices(x, y, c):
                remote.append((i[t], o[t].at[me], (px, py, pc), o[t].at[4 * px + 2 * py + pc]))
        for t in range(ns + ne, ns + ne + nw):
            remote.append((i[t], o[t], (x, y, 1 - c), o[t]))
        return local, remote

    return _exchange("grad_swap_sibling", list(gs) + list(everyone) + list(whole), outs, ne, ns + 7 * ne + nw, plan)


def _scatter_chips(parts, per_chip):
    ns, ng = len(parts), len(per_chip)
    outs = [jax.ShapeDtypeStruct(a.shape, a.dtype) for a in parts]
    outs += [jax.ShapeDtypeStruct((4,) + a.shape, a.dtype) for a in per_chip]

    def plan(i, o, x, y, c):
        mine = 2 * x + y
        local, remote = [], []
        for t in range(ns):
            local.append((i[t].at[mine], o[t].at[mine]))
            for px, py in _other_chips(x, y):
                remote.append((i[t].at[2 * px + py], o[t].at[mine], (px, py, c), o[t].at[2 * px + py]))
        for t in range(ns, ns + ng):
            local.append((i[t], o[t].at[mine]))
            for px, py in _other_chips(x, y):
                remote.append((i[t], o[t].at[mine], (px, py, c), o[t].at[2 * px + py]))
        return local, remote

    return _exchange("grad_scatter_chips", list(parts) + list(per_chip), outs, ns + ng, 3 * (ns + ng), plan)


def _join_halves(halves):
    ns = len(halves)
    outs = [jax.ShapeDtypeStruct(a.shape, a.dtype) for a in halves]

    def plan(i, o, x, y, c):
        return [], [(i[t], o[t], (x, y, 1 - c), o[t]) for t in range(ns)]

    return _exchange("grad_join_sibling", list(halves), outs, 0, ns, plan)


def _pad_heads_cols(w, per, used):
    k = w.shape[0]
    w = w.reshape(k, NH, per)[:, :, :used]
    return jnp.pad(w, ((0, 0), (0, 0), (0, HP - used))).reshape(k, NH * HP)


def _unpad_heads_cols(w, used):
    k = w.shape[0]
    return w.reshape(k, NH, HP)[:, :, :used]


def _prep_weights(wf):
    bf = lambda a: a.astype(BF16)
    out = {}
    out["w_in"] = jnp.pad(bf(wf["w_in"]), ((0, 0), (0, IN_PAD - IN_COLS)))
    out["w_glu"] = bf(wf["w_glu"])
    out["w_uq"] = _pad_heads_cols(bf(wf["w_uq"]), QK_NOPE + QK_ROPE, QK_NOPE + QK_ROPE)
    wkv = bf(wf["w_ukv"]).reshape(KV_LORA, NH, QK_NOPE + V_HEAD)
    wk = jnp.pad(wkv[:, :, :QK_NOPE], ((0, 0), (0, 0), (0, HP - QK_NOPE))).reshape(KV_LORA, NH * HP)
    wv = jnp.pad(wkv[:, :, QK_NOPE:], ((0, 0), (0, 0), (0, HP - V_HEAD))).reshape(KV_LORA, NH * HP)
    out["w_ukv"] = jnp.concatenate([wk, wv], axis=1)
    return out


def _prep_late_weights(wf):
    bf = lambda a: a.astype(BF16)
    out = {}
    wo = bf(wf["w_out"])
    wo_a = jnp.pad(wo[D_SSM:].reshape(NH, V_HEAD, D), ((0, 0), (0, HP - V_HEAD), (0, 0))).reshape(NH * HP, D)
    out["w_out"] = jnp.concatenate([wo[:D_SSM], wo_a], axis=0)
    out["w_ff1"] = bf(wf["w_ff1"])
    out["w_ff2"] = bf(wf["w_ff2"])
    return out


def _rope_tables(positions):
    inv_freq = ROPE_BASE ** (-jnp.arange(0, QK_ROPE, 2, dtype=F32) / QK_ROPE)
    ang = positions.astype(F32)[:, None] * inv_freq
    cos, sin = jnp.cos(ang), jnp.sin(ang)
    n = positions.shape[0]
    one = jnp.ones((n, QK_NOPE), F32)
    z16 = jnp.zeros((n, 16), F32)
    z32 = jnp.zeros((n, 32), F32)
    z64 = jnp.zeros((n, QK_NOPE), F32)
    rc = jnp.concatenate([one, cos, cos, z32], axis=1)
    rs1 = jnp.concatenate([z64, -sin, z16, z32], axis=1)
    rs2 = jnp.concatenate([z64, z16, sin, z32], axis=1)
    return rc, rs1, rs2


def _permute_rows(a, S):
    n, w = a.shape
    return a.reshape(n // S, 8, S // 8, w).transpose(0, 2, 1, 3).reshape(n, w)


def _unpermute_rows(a, S):
    n, w = a.shape
    return a.reshape(n // S, S // 8, 8, w).transpose(0, 2, 1, 3).reshape(n, w)


def _block_diag_in(bb):
    eye = jnp.eye(8, dtype=bb.dtype)
    blocks = jnp.einsum("qgph,gk->qghkp", bb.reshape(4, 8, P, H), eye).reshape(4, QB, QS)
    return blocks.transpose(1, 0, 2).reshape(QB, NST)


def _block_diag_out(cc):
    eye = jnp.eye(8, dtype=cc.dtype)
    return jnp.einsum("qghp,gk->qgpkh", cc.reshape(4, 8, H, P), eye).reshape(NST, QB)


def _slots(full):
    r, cdim = full.shape
    return full.reshape(r, 4, cdim // 4).transpose(1, 0, 2)


def _unslots(g):
    s, r, cs = g.shape
    return g.transpose(1, 0, 2).reshape(r, s * cs)


def _local_step(x, positions, target, modp, wf, late_weights=None, reducer=None):
    nb, S, _ = x.shape
    n = nb * S
    tm = min(256, S)
    tr = min(512, S)
    tt = min(512, S)
    tq = min(512, S // 2)
    kw = _prep_weights(wf)
    row = lambda a: a.reshape(1, -1).astype(F32)

    xf = x.reshape(n, D)
    tf = target.reshape(n, D)
    g1, g2, gf = row(wf["norm1_g"]), row(wf["norm2_g"]), row(wf["final_norm_g"])
    rc, rs1, rs2 = _rope_tables(positions.reshape(n))
    gq, gkv = row(wf["q_norm_g"]), row(wf["kv_norm_g"])
    h1, u, lat, q, k, v, qn, kvn = _f1_fwd(xf, modp, g1, kw["w_in"], rc, rs1, rs2, gq, gkv,
                                           kw["w_uq"], kw["w_ukv"], S, tr)

    col = lambda a: a.reshape(NST, 1)
    lam_re, lam_im = col(wf["ssm_lambda_re"]), col(wf["ssm_lambda_im"])
    logdt = jnp.repeat(wf["ssm_log_dt"].reshape(G, 1), P, axis=1).reshape(NST, 1)
    b_re, b_im = wf["ssm_b_re"].reshape(NST, H), wf["ssm_b_im"].reshape(NST, H)
    lbr, lbi, bbr, bbi = _ssm_param_fwd(lam_re, lam_im, logdt, b_re, b_im)
    lre8 = jnp.broadcast_to(lbr.reshape(1, NST), (8, NST))
    lim8 = jnp.broadcast_to(lbi.reshape(1, NST), (8, NST))
    bm = jnp.concatenate([_block_diag_in(bbr.reshape(G, P, H)), _block_diag_in(bbi.reshape(G, P, H))],
                         axis=1).astype(BF16)
    cm = jnp.concatenate([_block_diag_out(wf["ssm_c_re"]), -_block_diag_out(wf["ssm_c_im"])], axis=0).astype(BF16)
    dvec = row(wf["ssm_d"])
    u_p = _permute_rows(u, S)
    fcr, fci = _ssm_local(u_p, bm, lre8, lim8, S, tt)
    st, ypre, z, gact, yssm_p = _ssm_fwd(u_p, fcr, fci, bm, cm, dvec, kw["w_glu"], lre8, lim8, S, tt)
    yssm = _unpermute_rows(yssm_p, S)

    oattn, lrow = _attn_fwd(q, k, v, S, tq)

    gs = row(wf["ssm_out_g"])
    ga = jnp.pad(wf["attn_out_g"].reshape(NH, V_HEAD), ((0, 0), (0, HP - V_HEAD))).reshape(1, NH * HP)
    kw.update(_prep_late_weights(late_weights(oattn) if late_weights is not None else wf))
    yn, o, h2, dx1, r, da, dff, accs2, accg2 = _p2(yssm, oattn, xf, tf, modp, gs, ga, kw["w_out"], g2, gf,
                                                   kw["w_ff1"], kw["w_ff2"], S, tm)
    loss = accg2[2:3]
    g_ff1 = _wgrad(h2, da, "wgrad_ff1", col_slots=4)
    g_ff2 = _wgrad(r, dff, "wgrad_ff2").reshape(4, D_FF // 4, D)
    do, dyssm, dob, drow, accs3, accg3 = _p3_bwd(dx1, o, yssm, oattn, modp, gs, ga, kw["w_out"], S, tr)
    gwo = _wgrad(yn, do, "wgrad_out")
    g_out = jnp.concatenate([gwo[:D_SSM].reshape(2, D_SSM // 2, D),
                             gwo[D_SSM:].reshape(2, NH // 2 * HP, D).reshape(2, NH // 2, HP, D)[:, :, :V_HEAD]
                             .reshape(2, D_ATTN // 2, D)], axis=0)
    lre8_b = lre8
    if reducer is not None:
        drow = drow + reducer.start([g_ff1, g_ff2, g_out])[0, 0]

    dq, dk, dv = _attn_bwd(q, k, v, dob, lrow, drow, S, tq)
    if reducer is not None:
        lre8_b = lre8 + reducer.middle(dq)[0, 0]

    dys_p = _permute_rows(dyssm, S)
    dy, dz, air, aii = _ssm_bwd_a(dys_p, z, ypre, kw["w_glu"], cm, lre8_b, lim8, S, tt)
    du_p, dcm, dbm, dd, dlr, dli = _ssm_bwd_b(dy, u_p, st, fcr, fci, air, aii, bm, cm, dvec, lre8, lim8, S, tt)
    du = _unpermute_rows(du_p, S)
    dcm = dcm.reshape(2, 4, 8, P, 8, H)
    dc_re = jnp.einsum("qgpgh->qghp", dcm[0]).reshape(G, H, P)
    dc_im = -jnp.einsum("qgpgh->qghp", dcm[1]).reshape(G, H, P)
    dbm = dbm.reshape(8, H, 2, 4, 8, P)
    dbb_re = jnp.einsum("ghqgp->qgph", dbm[:, :, 0]).reshape(NST, H)
    dbb_im = jnp.einsum("ghqgp->qgph", dbm[:, :, 1]).reshape(NST, H)
    gb_re, gb_im, glr, gli, gdt = _ssm_param_bwd(lam_re, lam_im, logdt, b_re, b_im, dlr.reshape(NST, 1),
                                                 dli.reshape(NST, 1), dbb_re, dbb_im)
    glogdt = _rowsum(gdt.reshape(G, P))

    dx, dproj, dqb, dkvb, accs1, accg1, accm = _f1_bwd(du, dq, dk, dv, lat, rc, rs1, rs2, gq, gkv, kw["w_uq"],
                                                       kw["w_ukv"], dx1, xf, modp, g1, kw["w_in"], S, tr)

    big = {}
    big["w_in"] = _slots(_wgrad(h1, dproj, "wgrad_in")[:, :IN_COLS])
    big["w_glu"] = _wgrad(gact, dz, "wgrad_glu", col_slots=4)
    big["w_uq"] = _slots(_unpad_heads_cols(_wgrad(qn, dqb, "wgrad_uq"), QK_NOPE + QK_ROPE).reshape(Q_LORA, -1))
    gkvw = _wgrad(kvn, dkvb, "wgrad_ukv")
    big["w_ukv"] = _slots(jnp.concatenate([_unpad_heads_cols(gkvw[:, :NH * HP], QK_NOPE),
                                           _unpad_heads_cols(gkvw[:, NH * HP:], V_HEAD)], axis=2).reshape(KV_LORA, -1))
    big["w_out"] = g_out
    big["w_ff1"] = g_ff1
    big["w_ff2"] = g_ff2

    small = {}
    small["norm1_g"] = accg1[0:1]
    small["norm2_g"] = accg2[0:1]
    small["final_norm_g"] = accg2[1:2]
    small["ssm_out_g"] = accg3[0:1, :D_SSM]
    small["attn_out_g"] = accg3[1].reshape(NH, HP)[:, :V_HEAD].reshape(1, D_ATTN)
    small["q_norm_g"] = accm[0:1, :Q_LORA]
    small["kv_norm_g"] = accm[1:2, :KV_LORA]
    small["ssm_lambda_re"] = glr.reshape(G, P)
    small["ssm_lambda_im"] = gli.reshape(G, P)
    small["ssm_b_re"] = gb_re
    small["ssm_b_im"] = gb_im
    small["ssm_c_re"] = dc_re.reshape(G * H, P)
    small["ssm_c_im"] = dc_im.reshape(G * H, P)
    small["ssm_d"] = dd.reshape(G, H)
    small["ssm_log_dt"] = glogdt.reshape(1, G)
    return loss, dx.reshape(nb, S, D), big, small, accs1 + accs2 + accs3


def _view2d(a):
    return a.reshape(-1, a.shape[-1]) if a.ndim > 1 else a.reshape(1, -1)


def kernel(x, c, positions, ada_w, ada_b, norm1_g, w_in, ssm_lambda_re, ssm_lambda_im, ssm_b_re, ssm_b_im, ssm_c_re, ssm_c_im, ssm_d, ssm_log_dt, w_glu, q_norm_g, w_uq, kv_norm_g, w_ukv, ssm_out_g, attn_out_g, w_out, norm2_g, w_ff1, w_ff2, final_ada_w, final_ada_b, final_norm_g, loss_target, m_ada_w, m_ada_b, m_norm1_g, m_w_in, m_ssm_lambda_re, m_ssm_lambda_im, m_ssm_b_re, m_ssm_b_im, m_ssm_c_re, m_ssm_c_im, m_ssm_d, m_ssm_log_dt, m_w_glu, m_q_norm_g, m_w_uq, m_kv_norm_g, m_w_ukv, m_ssm_out_g, m_attn_out_g, m_w_out, m_norm2_g, m_w_ff1, m_w_ff2, m_final_ada_w, m_final_ada_b, m_final_norm_g, v_ada_w, v_ada_b, v_norm1_g, v_w_in, v_ssm_lambda_re, v_ssm_lambda_im, v_ssm_b_re, v_ssm_b_im, v_ssm_c_re, v_ssm_c_im, v_ssm_d, v_ssm_log_dt, v_w_glu, v_q_norm_g, v_w_uq, v_kv_norm_g, v_w_ukv, v_ssm_out_g, v_attn_out_g, v_w_out, v_norm2_g, v_w_ff1, v_w_ff2, v_final_ada_w, v_final_ada_b, v_final_norm_g):
    args = dict(locals())
    names = list(inspect.signature(kernel).parameters)
    wnames = names[3:names.index("loss_target")]
    small_names = [nm for nm in wnames if nm not in GATHERED and nm not in TP]
    reduced_names = [nm for nm in small_names if nm not in ("ada_b", "final_ada_b")]
    w = {nm: args[nm] for nm in wnames}
    m = {nm: args["m_" + nm] for nm in wnames}
    v = {nm: args["v_" + nm] for nm in wnames}
    nb = x.shape[0]
    xi, yi, ci = lax.axis_index("x"), lax.axis_index("y"), lax.axis_index("c")
    chip, me = 2 * xi + yi, 4 * xi + 2 * yi + ci

    unslot = lambda nm, g: g.reshape(-1, g.shape[-1]) if nm in ROW_SHARDED else _unslots(g)
    early = [nm for nm in GATHERED if nm not in LATE]
    got = _gather_chips("gather_weights", [_view2d(w[nm]).astype(BF16) for nm in early], [c])
    wf = {nm: unslot(nm, g) for nm, g in zip(early, got)}
    for nm in small_names:
        wf[nm] = w[nm][0] if w[nm].ndim > 1 else w[nm]
    c_all = got[len(early)].reshape(8 * nb, D)

    na, nf = ada_w.shape[-1], final_ada_w.shape[-1]
    ada_b_s = lax.dynamic_slice(ada_b, (0, chip * na), (1, na))
    fada_b_s = lax.dynamic_slice(final_ada_b.reshape(1, -1), (0, chip * nf), (1, nf))
    cond_all, modcols = _mod_fwd(c_all, ada_w[0], ada_b_s, final_ada_w, fada_b_s)
    (mod_g,) = _gather_chips("gather_mod", [modcols])
    mine = lax.dynamic_slice(mod_g, (0, me * nb, 0), (4, nb, na + nf))
    modp = jnp.concatenate([mine[:, :, :na].transpose(1, 0, 2).reshape(nb, 6, D),
                            mine[:, :, na:].transpose(1, 0, 2).reshape(nb, 2, D)], axis=1)

    own_late = [_view2d(w[nm]).astype(BF16) for nm in LATE]
    late_gather, token = _split_start("gather_late", own_late,
                                      [jax.ShapeDtypeStruct((4,) + a.shape, a.dtype) for a in own_late],
                                      3 * len(LATE), _plan_to_chips, modp)
    modp = modp + token[0, 0]

    def late_weights(after):
        sent, landed = _split_wait(late_gather, after)
        return {nm: unslot(nm, lax.dynamic_update_slice(g, own[None], (chip, 0, 0)))
                for nm, g, own in zip(LATE, landed, sent)}

    cidx = ci.astype(jnp.int32).reshape(1)
    ahead = ["w_ff1", "w_ff2", "w_out"]

    class Reducer:
        def start(self, gs):
            lands = [jax.ShapeDtypeStruct((4, g.shape[1] // 2, g.shape[2]), g.dtype) for g in gs]
            self.swap, tok = _split_start("grad_swap_ff", gs, lands, len(gs), _plan_swap_halves, modp)
            return tok

        def middle(self, after):
            gs, got = _split_wait(self.swap, after)
            sums = [_add_half(g, r, cidx, "grad_add_sibling_" + nm) for nm, g, r in zip(ahead, gs, got)]
            lands = [jax.ShapeDtypeStruct(s.shape, s.dtype) for s in sums]
            self.scatter, tok = _split_start("grad_scatter_ff", sums, lands, 3 * len(sums), _plan_scatter_chips, modp)
            return tok

        def finish(self, after):
            out = []
            for nm, s, l in zip(ahead, *_split_wait(self.scatter, after)):
                own = lax.dynamic_slice(s, (chip, 0, 0), (1,) + s.shape[1:])
                out.append(_add_chips(lax.dynamic_update_slice(l, own, (chip, 0, 0)), "grad_add_chips_" + nm))
            return out

    reducer = Reducer()
    loss_row, grad_x, big, small, dmodp = _local_step(x, positions, loss_target, modp, wf, late_weights, reducer)

    rest = [nm for nm in GATHERED if nm not in ahead]
    sizes = [small[nm].size for nm in reduced_names]
    pad = -sum(sizes) % 128
    packed = jnp.concatenate([small[nm].reshape(1, -1) for nm in reduced_names] + [jnp.zeros((1, pad), F32)],
                             axis=1).astype(BF16)
    swapped = _swap_halves([big[nm] for nm in rest], [dmodp.reshape(nb, 8 * D)], [packed, loss_row])
    chip_sums = [_add_half(big[nm], r, cidx, "grad_add_sibling_" + nm) for nm, r in zip(rest, swapped)]
    chip_small = _pair_sum(packed, swapped[len(rest) + 1], loss_row, swapped[len(rest) + 2])
    scattered = _scatter_chips(chip_sums, chip_small)
    half_of = {nm: _add_chips(r, "grad_add_chips_" + nm) for nm, r in zip(rest, scattered)}
    half_of.update(zip(ahead, reducer.finish(grad_x)))
    halves = [half_of[nm] for nm in GATHERED]
    others = _join_halves(halves)
    grads = {}
    dmod_all = swapped[len(rest)].reshape(8 * nb, 8 * D)
    small_sum, loss_sum = _sum_devices(scattered[len(rest)].reshape(4, -1), scattered[len(rest) + 1].reshape(4, -1))
    loss = jnp.sum(loss_sum)
    off = 0
    for nm, sz in zip(reduced_names, sizes):
        grads[nm] = small_sum[:, off:off + sz].reshape(small[nm].shape)
        off += sz

    dsl = jnp.concatenate([lax.dynamic_slice(dmod_all, (0, chip * na), (8 * nb, na)),
                           lax.dynamic_slice(dmod_all, (0, 6 * D + chip * nf), (8 * nb, nf))], axis=1)
    gw, gb = _mod_bwd(cond_all.T, dsl, dmod_all)
    grads["ada_w"], grads["final_ada_w"] = gw[:, :na], gw[:, na:]
    grads["ada_b"], grads["final_ada_b"] = gb[:, :6 * D], gb[:, 6 * D:]

    delta, new_m, new_v = {}, {}, {}
    for nm, mine_h, other_h in zip(GATHERED, halves, others):
        grads[nm], delta[nm], new_m[nm], new_v[nm] = _adamw_halves(
            _view2d(w[nm]), mine_h, other_h, _view2d(m[nm]), _view2d(v[nm]), cidx, "adamw_" + nm)
    for nm in TP:
        delta[nm], new_m[nm], new_v[nm] = _adamw(_view2d(w[nm]), grads[nm], _view2d(m[nm]), _view2d(v[nm]),
                                                  "adamw_" + nm)
    upd = _adamw_small([_view2d(w[nm]) for nm in small_names], [grads[nm] for nm in small_names],
                       [_view2d(m[nm]) for nm in small_names], [_view2d(v[nm]) for nm in small_names])
    k = len(small_names)
    for t, nm in enumerate(small_names):
        delta[nm], new_m[nm], new_v[nm] = upd[t], upd[k + t], upd[2 * k + t]

    outs = [grads, delta, new_m, new_v]
    return (loss, grad_x, *[d[nm].reshape(w[nm].shape) for d in outs for nm in wnames])
```

```python
import inspect
import math

import jax
import jax.numpy as jnp
from jax import lax
from jax.experimental import pallas as pl
from jax.experimental.pallas import tpu as pltpu

F32 = jnp.float32
BF16 = jnp.bfloat16

D = 1024
D_SSM = 512
G = 32
H = 16
P = 64
NST = G * P
D_ATTN = 512
NH = 8
QK_NOPE = 64
QK_ROPE = 32
V_HEAD = 64
HP = 128
Q_LORA = 384
KV_LORA = 256
IN_COLS = D_SSM + Q_LORA + KV_LORA + QK_ROPE
IN_PAD = 1280
D_FF = 4096
ROPE_BASE = 10000.0
EPS = 1e-6
ADAM_LR = 0.001
ADAM_B1 = 0.9
ADAM_B2 = 0.999
ADAM_EPS = 1e-08
ADAM_WD = 0.01
ADAM_STEP = 10
NEG = -1e30
VMEM_LIMIT = 60 << 20

MESH = pl.DeviceIdType.MESH
_VM = pl.BlockSpec(memory_space=pltpu.VMEM)
_ANY = pl.BlockSpec(memory_space=pl.ANY)

GATHERED = ["w_in", "w_glu", "w_uq", "w_ukv", "w_out", "w_ff1", "w_ff2"]
TP = ["ada_w", "final_ada_w"]
ROW_SHARDED = ("w_out", "w_ff2")
LATE = ["w_out", "w_ff1", "w_ff2"]


def _cp(sem=None, vmem=VMEM_LIMIT):
    kw = dict(vmem_limit_bytes=vmem)
    if sem is not None:
        kw["dimension_semantics"] = sem
    return pltpu.CompilerParams(**kw)


def _dot(a, b):
    return jnp.dot(a, b, preferred_element_type=F32)


def _dot_nt(a, b):
    return lax.dot_general(a, b, (((1,), (1,)), ((), ())), preferred_element_type=F32)


def _dot_tn(a, b):
    return lax.dot_general(a, b, (((0,), (0,)), ((), ())), preferred_element_type=F32)


def _rms(x, n):
    r = lax.rsqrt(jnp.sum(x * x, axis=-1, keepdims=True) * (1.0 / n) + EPS)
    return x * r, r


def _rms_bwd(dyg, xhat, r, n):
    return r * (dyg - xhat * (jnp.sum(dyg * xhat, axis=-1, keepdims=True) * (1.0 / n)))


def _sigmoid(x):
    return 1.0 / (1.0 + jnp.exp(-x))


_GK = math.sqrt(2.0 / math.pi)
_GC = 0.044715


def _gelu(y):
    t = jnp.tanh(_GK * (y + _GC * y * y * y))
    return 0.5 * y * (1.0 + t)


def _gelu_grad(y):
    t = jnp.tanh(_GK * (y + _GC * y * y * y))
    return 0.5 * (1.0 + t) + 0.5 * y * (1.0 - t * t) * _GK * (1.0 + 3.0 * _GC * y * y)


def _colsum(x):
    return jnp.sum(x, axis=0, keepdims=True)


def _roll(x, s):
    return pltpu.roll(x, s % x.shape[-1], x.ndim - 1)


def _mod_fwd(c_all, ada_w_s, ada_b_s, fada_w_s, fada_b_s):
    nseq = c_all.shape[0]
    na, nf = ada_w_s.shape[1], fada_w_s.shape[1]

    def body(c_ref, w_ref, b_ref, fw_ref, fb_ref, cond_ref, mod_ref):
        cv = c_ref[...]
        cond = cv * _sigmoid(cv)
        cond_ref[...] = cond
        cb = cond.astype(BF16)
        mod_ref[:, 0:na] = _dot(cb, w_ref[...].astype(BF16)) + b_ref[...]
        mod_ref[:, na:na + nf] = _dot(cb, fw_ref[...].astype(BF16)) + fb_ref[...]

    return pl.pallas_call(
        body, name="mod_fwd",
        out_shape=[jax.ShapeDtypeStruct((nseq, D), F32), jax.ShapeDtypeStruct((nseq, na + nf), F32)],
        in_specs=[_VM] * 5, out_specs=[_VM] * 2, compiler_params=_cp(),
    )(c_all, ada_w_s, ada_b_s, fada_w_s, fada_b_s)


def _mod_bwd(cond_t, dsl, dall):
    nseq, n = dsl.shape
    bc = 512

    def body(ct_ref, dm_ref, da_ref, gw_ref, gb_ref):
        ct = ct_ref[...]
        dm = dm_ref[...]
        acc = ct[:, 0:1] * dm[0:1, :]
        for b in range(1, nseq):
            acc = acc + ct[:, b:b + 1] * dm[b:b + 1, :]
        gw_ref[...] = acc

        @pl.when(pl.program_id(0) == 0)
        def _():
            gb_ref[...] = _colsum(da_ref[...])

    return pl.pallas_call(
        body, name="mod_bwd", grid=(n // bc,),
        out_shape=[jax.ShapeDtypeStruct((D, n), F32), jax.ShapeDtypeStruct((1, dall.shape[1]), F32)],
        in_specs=[_VM, pl.BlockSpec((nseq, bc), lambda i: (0, i)), _VM],
        out_specs=[pl.BlockSpec((D, bc), lambda i: (0, i)), pl.BlockSpec((1, dall.shape[1]), lambda i: (0, 0))],
        compiler_params=_cp(("arbitrary",)),
    )(cond_t, dsl, dall)


def _f1_fwd(x, modp, g1, w_in, rc, rs1, rs2, gq, gkv, w_uq, w_ukv, S, tm):
    n = x.shape[0]
    tps = S // tm
    LAT = IN_PAD - D_SSM

    def body(x_ref, mod_ref, g_ref, w_ref, c_ref, s1_ref, s2_ref, gq_ref, gkv_ref, wq_ref, wkv_ref,
             h_ref, u_ref, lat_ref, q_ref, k_ref, v_ref, qn_ref, kvn_ref):
        xhat, _ = _rms(x_ref[...], D)
        h = (xhat * g_ref[...]) * (1.0 + mod_ref[0, 1:2, :]) + mod_ref[0, 0:1, :]
        hb = h.astype(BF16)
        h_ref[...] = hb
        proj = _dot(hb, w_ref[...])
        u_ref[...] = proj[:, 0:D_SSM]
        lat_ref[...] = proj[:, D_SSM:IN_PAD]
        c, s1, s2 = c_ref[...], s1_ref[...], s2_ref[...]
        qhat, _ = _rms(proj[:, D_SSM:D_SSM + Q_LORA], Q_LORA)
        qn = (qhat * gq_ref[...]).astype(BF16)
        qn_ref[...] = qn
        q = _dot(qn, wq_ref[...])
        qr = _rope(q, jnp.tile(c, (1, NH)), jnp.tile(s1, (1, NH)), jnp.tile(s2, (1, NH)))
        q_ref[...] = (qr * _C2).astype(BF16)
        khat, _ = _rms(proj[:, D_SSM + Q_LORA:D_SSM + Q_LORA + KV_LORA], KV_LORA)
        kvn = (khat * gkv_ref[...]).astype(BF16)
        kvn_ref[...] = kvn
        kv = _dot(kvn, wkv_ref[...])
        kr = _rope(_roll(proj[:, IN_PAD - HP:IN_PAD], 64), c, s1, s2)
        k_ref[...] = (kv[:, 0:NH * HP] + jnp.tile(kr, (1, NH))).astype(BF16)
        vv = kv[:, NH * HP:2 * NH * HP]
        lane = lax.broadcasted_iota(jnp.int32, vv.shape, 1)
        v_ref[...] = jnp.where(lane % HP == V_HEAD, 1.0, vv).astype(BF16)

    row = lambda w: pl.BlockSpec((tm, w), lambda i: (i, 0))
    return pl.pallas_call(
        body, name="f1_fwd", grid=(n // tm,),
        out_shape=[jax.ShapeDtypeStruct((n, D), BF16), jax.ShapeDtypeStruct((n, D_SSM), F32),
                   jax.ShapeDtypeStruct((n, LAT), F32)] + [jax.ShapeDtypeStruct((n, NH * HP), BF16)] * 3 +
                  [jax.ShapeDtypeStruct((n, Q_LORA), BF16), jax.ShapeDtypeStruct((n, KV_LORA), BF16)],
        in_specs=[row(D), pl.BlockSpec((1, 8, D), lambda i: (i // tps, 0, 0)), _VM, _VM,
                  row(HP), row(HP), row(HP), _VM, _VM, _VM, _VM],
        out_specs=[row(D), row(D_SSM), row(LAT)] + [row(NH * HP)] * 3 + [row(Q_LORA), row(KV_LORA)],
        compiler_params=_cp(("parallel",)),
    )(x, modp, g1, w_in, rc, rs1, rs2, gq, gkv, w_uq, w_ukv)


def _f1_bwd(du, dq, dk, dv, lat, rc, rs1, rs2, gq, gkv, w_uq, w_ukv, dx1, x, modp, g1, w_in, S, tm):
    n = x.shape[0]
    tps = S // tm
    nb = n // S

    def body(du_ref, dq_ref, dk_ref, dv_ref, lat_ref, c_ref, s1_ref, s2_ref, gq_ref, gkv_ref, wq_ref, wkv_ref,
             dx1_ref, x_ref, mod_ref, g_ref, w_ref,
             dx_ref, dproj_ref, dqb_ref, dkvb_ref, accs_ref, accg_ref, accm_ref):
        i = pl.program_id(0)
        c, s1, s2 = c_ref[...], s1_ref[...], s2_ref[...]
        dqu = _rope_t(dq_ref[...] * _SCALE, jnp.tile(c, (1, NH)), jnp.tile(s1, (1, NH)),
                      jnp.tile(s2, (1, NH))).astype(BF16)
        dqb_ref[...] = dqu
        dqn = _dot_nt(dqu, wq_ref[...])
        qhat, rq = _rms(lat_ref[:, 0:Q_LORA], Q_LORA)
        dql = _rms_bwd(dqn * gq_ref[...], qhat, rq, Q_LORA)
        dkf = dk_ref[...] * (1.0 / _LOG2E)
        dkv = jnp.concatenate([dkf.astype(BF16), dv_ref[...].astype(BF16)], axis=1)
        dkvb_ref[...] = dkv
        dkvn = _dot_nt(dkv, wkv_ref[...])
        khat, rk = _rms(lat_ref[:, Q_LORA:Q_LORA + KV_LORA], KV_LORA)
        dkvl = _rms_bwd(dkvn * gkv_ref[...], khat, rk, KV_LORA)
        dkr = dkf[:, 0:HP]
        for h in range(1, NH):
            dkr = dkr + dkf[:, h * HP:(h + 1) * HP]
        lane = lax.broadcasted_iota(jnp.int32, dkr.shape, 1)
        dkr = jnp.where((lane >= QK_NOPE) & (lane < QK_NOPE + QK_ROPE), dkr, 0.0)
        dkr = _roll(_rope_t(dkr, c, s1, s2), -64)

        @pl.when(i == 0)
        def _():
            accm_ref[...] = jnp.zeros_like(accm_ref)

        accm_ref[0:1, 0:Q_LORA] += _colsum(dqn * qhat)
        accm_ref[1:2, 0:KV_LORA] += _colsum(dkvn * khat)

        dproj = jnp.concatenate([du_ref[...], dql, dkvl, dkr], axis=1).astype(BF16)
        dproj_ref[...] = dproj
        dh = _dot_nt(dproj, w_ref[...])
        xhat, r = _rms(x_ref[...], D)
        g = g_ref[...]
        dn = dh * (1.0 + mod_ref[0, 1:2, :])
        dx_ref[...] = dx1_ref[...] + _rms_bwd(dn * g, xhat, r, D)

        @pl.when(i % tps == 0)
        def _():
            accs_ref[...] = jnp.zeros_like(accs_ref)

        @pl.when(i == 0)
        def _():
            accg_ref[...] = jnp.zeros_like(accg_ref)

        accs_ref[0, 0:1, :] += _colsum(dh)
        accs_ref[0, 1:2, :] += _colsum(dh * (xhat * g))
        accg_ref[0:1, :] += _colsum(dn * xhat)

    row = lambda w: pl.BlockSpec((tm, w), lambda i: (i, 0))
    return pl.pallas_call(
        body, name="f1_bwd", grid=(n // tm,),
        out_shape=[jax.ShapeDtypeStruct((n, D), F32), jax.ShapeDtypeStruct((n, IN_PAD), BF16),
                   jax.ShapeDtypeStruct((n, NH * HP), BF16), jax.ShapeDtypeStruct((n, 2 * NH * HP), BF16),
                   jax.ShapeDtypeStruct((nb, 8, D), F32), jax.ShapeDtypeStruct((8, D), F32),
                   jax.ShapeDtypeStruct((8, Q_LORA), F32)],
        in_specs=[row(D_SSM)] + [row(NH * HP)] * 3 + [row(IN_PAD - D_SSM), row(HP), row(HP), row(HP),
                                                     _VM, _VM, _VM, _VM, row(D), row(D),
                                                     pl.BlockSpec((1, 8, D), lambda i: (i // tps, 0, 0)), _VM, _VM],
        out_specs=[row(D), row(IN_PAD), row(NH * HP), row(2 * NH * HP),
                   pl.BlockSpec((1, 8, D), lambda i: (i // tps, 0, 0)), pl.BlockSpec((8, D), lambda i: (0, 0)),
                   pl.BlockSpec((8, Q_LORA), lambda i: (0, 0))],
        compiler_params=_cp(("arbitrary",)),
    )(du, dq, dk, dv, lat, rc, rs1, rs2, gq, gkv, w_uq, w_ukv, dx1, x, modp, g1, w_in)


def _ssm_param_fwd(lam_re, lam_im, logdt, b_re, b_im):
    def body(lr_ref, li_ref, ld_ref, br_ref, bi_ref, lbr_ref, lbi_ref, bbr_ref, bbi_ref):
        lr, li = lr_ref[...], li_ref[...]
        dt = jnp.exp(ld_ref[...])
        er = jnp.exp(lr * dt)
        lbr = er * jnp.cos(li * dt)
        lbi = er * jnp.sin(li * dt)
        den = 1.0 / (lr * lr + li * li)
        cr = ((lbr - 1.0) * lr + lbi * li) * den
        ci = (lbi * lr - (lbr - 1.0) * li) * den
        lbr_ref[...] = lbr
        lbi_ref[...] = lbi
        bbr_ref[...] = cr * br_ref[...] - ci * bi_ref[...]
        bbi_ref[...] = cr * bi_ref[...] + ci * br_ref[...]

    return pl.pallas_call(
        body, name="ssm_param_fwd",
        out_shape=[jax.ShapeDtypeStruct((NST, 1), F32)] * 2 + [jax.ShapeDtypeStruct((NST, H), F32)] * 2,
        in_specs=[_VM] * 5, out_specs=[_VM] * 4, compiler_params=_cp(),
    )(lam_re, lam_im, logdt, b_re, b_im)


def _ssm_param_bwd(lam_re, lam_im, logdt, b_re, b_im, dlb_re, dlb_im, dbb_re, dbb_im):
    def body(lr_ref, li_ref, ld_ref, br_ref, bi_ref, dlr_ref, dli_ref, dbr_ref, dbi_ref,
             gbr_ref, gbi_ref, glr_ref, gli_ref, gdt_ref):
        lr, li = lr_ref[...], li_ref[...]
        dt = jnp.exp(ld_ref[...])
        er = jnp.exp(lr * dt)
        lbr = er * jnp.cos(li * dt)
        lbi = er * jnp.sin(li * dt)
        den = 1.0 / (lr * lr + li * li)
        nr, ni = lbr - 1.0, lbi
        cr = (nr * lr + ni * li) * den
        ci = (ni * lr - nr * li) * den
        br, bi = br_ref[...], bi_ref[...]
        dbr, dbi = dbr_ref[...], dbi_ref[...]
        gbr_ref[...] = cr * dbr + ci * dbi
        gbi_ref[...] = cr * dbi - ci * dbr
        gcr = jnp.sum(dbr * br + dbi * bi, axis=1, keepdims=True)
        gci = jnp.sum(dbi * br - dbr * bi, axis=1, keepdims=True)
        ilr, ili = lr * den, -li * den
        glbr = dlr_ref[...] + (gcr * ilr + gci * ili)
        glbi = dli_ref[...] + (gci * ilr - gcr * ili)
        qr = -(cr * ilr - ci * ili)
        qi = -(cr * ili + ci * ilr)
        glr = gcr * qr + gci * qi
        gli = gci * qr - gcr * qi
        glr = glr + dt * (glbr * lbr + glbi * lbi)
        gli = gli + dt * (glbi * lbr - glbr * lbi)
        wr = lr * lbr - li * lbi
        wi = lr * lbi + li * lbr
        glr_ref[...] = glr
        gli_ref[...] = gli
        gdt_ref[...] = (glbr * wr + glbi * wi) * dt

    return pl.pallas_call(
        body, name="ssm_param_bwd",
        out_shape=[jax.ShapeDtypeStruct((NST, H), F32)] * 2 + [jax.ShapeDtypeStruct((NST, 1), F32)] * 3,
        in_specs=[_VM] * 9, out_specs=[_VM] * 5, compiler_params=_cp(),
    )(lam_re, lam_im, logdt, b_re, b_im, dlb_re, dlb_im, dbb_re, dbb_im)


def _rowsum(a):
    def body(a_ref, o_ref):
        o_ref[...] = jnp.sum(a_ref[...], axis=1, keepdims=True)

    return pl.pallas_call(
        body, name="rowsum", out_shape=jax.ShapeDtypeStruct((a.shape[0], 1), F32),
        in_specs=[_VM], out_specs=_VM, compiler_params=_cp(),
    )(a)


QB = D_SSM // 4
QS = 4 * QB


def _bd_lo(part, q):
    return part * NST + q * QS


def _bd_expand(ub, bm_ref, out_ref):
    for part in range(2):
        for q in range(4):
            lo = _bd_lo(part, q)
            out_ref[:, lo:lo + QS] = _dot(ub[:, q * QB:(q + 1) * QB], bm_ref[:, lo:lo + QS])


def _bd_expand_t(db, cm_ref, out_ref):
    for part in range(2):
        for q in range(4):
            lo = _bd_lo(part, q)
            out_ref[:, lo:lo + QS] = _dot_nt(db[:, q * QB:(q + 1) * QB], cm_ref[lo:lo + QS, :])


def _bd_project(sb, cm_ref):
    return jnp.concatenate(
        [_dot(sb[:, _bd_lo(0, q):_bd_lo(0, q) + QS], cm_ref[_bd_lo(0, q):_bd_lo(0, q) + QS, :])
         + _dot(sb[:, _bd_lo(1, q):_bd_lo(1, q) + QS], cm_ref[_bd_lo(1, q):_bd_lo(1, q) + QS, :])
         for q in range(4)], axis=1)


def _bd_project_t(ab, bm_ref):
    return jnp.concatenate(
        [_dot_nt(ab[:, _bd_lo(0, q):_bd_lo(0, q) + QS], bm_ref[:, _bd_lo(0, q):_bd_lo(0, q) + QS])
         + _dot_nt(ab[:, _bd_lo(1, q):_bd_lo(1, q) + QS], bm_ref[:, _bd_lo(1, q):_bd_lo(1, q) + QS])
         for q in range(4)], axis=1)


def _pow2k(pr, pi, nsq):
    for _ in range(nsq):
        pr, pi = pr * pr - pi * pi, 2.0 * pr * pi
    return pr, pi


def _ssm_local(u_p, bm, lre8, lim8, S, tt):
    n = u_p.shape[0]
    nb, nt = n // S, S // tt
    nsq = int(round(math.log2(S // 8)))
    assert 2 ** nsq == S // 8

    def body(u_ref, bm_ref, lre_ref, lim_ref, cre_ref, cim_ref, sre, sim, bu):
        j = pl.program_id(1)

        @pl.when(j == 0)
        def _():
            sre[...] = jnp.zeros_like(sre)
            sim[...] = jnp.zeros_like(sim)

        _bd_expand(u_ref[...].astype(BF16), bm_ref, bu)
        lre, lim = lre_ref[...], lim_ref[...]

        def step(i, c):
            sr, si = c
            off = pl.multiple_of(i * 8, 8)
            br = bu[pl.ds(off, 8), 0:NST]
            bi = bu[pl.ds(off, 8), NST:2 * NST]
            return lre * sr - lim * si + br, lre * si + lim * sr + bi

        sr, si = lax.fori_loop(0, tt // 8, step, (sre[...], sim[...]))
        sre[...] = sr
        sim[...] = si

        @pl.when(j == nt - 1)
        def _():
            pr, pi = _pow2k(lre[0:1], lim[0:1], nsq)
            cr = jnp.zeros((1, NST), F32)
            ci = jnp.zeros((1, NST), F32)
            cre_ref[0:1, :] = cr
            cim_ref[0:1, :] = ci
            for k in range(1, 8):
                cr, ci = sr[k - 1:k] + pr * cr - pi * ci, si[k - 1:k] + pr * ci + pi * cr
                cre_ref[k:k + 1, :] = cr
                cim_ref[k:k + 1, :] = ci

    return pl.pallas_call(
        body, name="ssm_local", grid=(nb, nt),
        out_shape=[jax.ShapeDtypeStruct((nb * 8, NST), F32)] * 2,
        in_specs=[pl.BlockSpec((tt, D_SSM), lambda b, j: (b * nt + j, 0)), _VM, _VM, _VM],
        out_specs=[pl.BlockSpec((8, NST), lambda b, j: (b, 0))] * 2,
        scratch_shapes=[pltpu.VMEM((8, NST), F32), pltpu.VMEM((8, NST), F32), pltpu.VMEM((tt, 2 * NST), F32)],
        compiler_params=_cp(("arbitrary", "arbitrary")),
    )(u_p, bm, lre8, lim8)


def _ssm_fwd(u_p, cre, cim, bm, cm, dvec, w_glu, lre8, lim8, S, tt):
    n = u_p.shape[0]
    nb, nt = n // S, S // tt

    def body(u_ref, cre_ref, cim_ref, bm_ref, cm_ref, d_ref, wg_ref, lre_ref, lim_ref,
             st_ref, ypre_ref, z_ref, gact_ref, yssm_ref, sre, sim, bu):
        j = pl.program_id(1)

        @pl.when(j == 0)
        def _():
            sre[...] = cre_ref[...]
            sim[...] = cim_ref[...]

        u = u_ref[...]
        _bd_expand(u.astype(BF16), bm_ref, bu)
        lre, lim = lre_ref[...], lim_ref[...]

        def step(i, c):
            sr, si = c
            off = pl.multiple_of(i * 8, 8)
            nr = lre * sr - lim * si + bu[pl.ds(off, 8), 0:NST]
            ni = lre * si + lim * sr + bu[pl.ds(off, 8), NST:2 * NST]
            bu[pl.ds(off, 8), 0:NST] = nr
            bu[pl.ds(off, 8), NST:2 * NST] = ni
            return nr, ni

        sr, si = lax.fori_loop(0, tt // 8, step, (sre[...], sim[...]))
        sre[...] = sr
        sim[...] = si
        stb = bu[...].astype(BF16)
        st_ref[...] = stb
        y = _bd_project(stb, cm_ref) + d_ref[...] * u
        ypre_ref[...] = y
        gb = _gelu(y).astype(BF16)
        gact_ref[...] = gb
        z = _dot(gb, wg_ref[...])
        z_ref[...] = z
        yssm_ref[...] = z[:, 0:D_SSM] * _sigmoid(z[:, D_SSM:2 * D_SSM])

    row = lambda w: pl.BlockSpec((tt, w), lambda b, j: (b * nt + j, 0))
    return pl.pallas_call(
        body, name="ssm_fwd", grid=(nb, nt),
        out_shape=[jax.ShapeDtypeStruct((n, 2 * NST), BF16), jax.ShapeDtypeStruct((n, D_SSM), F32),
                   jax.ShapeDtypeStruct((n, 2 * D_SSM), F32), jax.ShapeDtypeStruct((n, D_SSM), BF16),
                   jax.ShapeDtypeStruct((n, D_SSM), F32)],
        in_specs=[row(D_SSM), pl.BlockSpec((8, NST), lambda b, j: (b, 0)), pl.BlockSpec((8, NST), lambda b, j: (b, 0)),
                  _VM, _VM, _VM, _VM, _VM, _VM],
        out_specs=[row(2 * NST), row(D_SSM), row(2 * D_SSM), row(D_SSM), row(D_SSM)],
        scratch_shapes=[pltpu.VMEM((8, NST), F32), pltpu.VMEM((8, NST), F32), pltpu.VMEM((tt, 2 * NST), F32)],
        compiler_params=_cp(("arbitrary", "arbitrary")),
    )(u_p, cre, cim, bm, cm, dvec, w_glu, lre8, lim8)


def _ssm_bwd_a(dys_p, z, ypre, w_glu, cm, lre8, lim8, S, tt):
    n = z.shape[0]
    nb, nt = n // S, S // tt
    nsq = int(round(math.log2(S // 8)))
    ng = tt // 8

    def body(dys_ref, z_ref, y_ref, wg_ref, cm_ref, lre_ref, lim_ref, dy_ref, dz_ref, are_ref, aim_ref, sre, sim, gb):
        j = pl.program_id(1)

        @pl.when(j == 0)
        def _():
            sre[...] = jnp.zeros_like(sre)
            sim[...] = jnp.zeros_like(sim)

        z = z_ref[...]
        z1, z2 = z[:, 0:D_SSM], z[:, D_SSM:2 * D_SSM]
        sg = _sigmoid(z2)
        dys = dys_ref[...]
        dz = jnp.concatenate([dys * sg, dys * z1 * sg * (1.0 - sg)], axis=1).astype(BF16)
        dz_ref[...] = dz
        dy = _dot_nt(dz, wg_ref[...]) * _gelu_grad(y_ref[...])
        dy_ref[...] = dy
        _bd_expand_t(dy.astype(BF16), cm_ref, gb)
        lre, lim = lre_ref[...], lim_ref[...]

        def step(i, c):
            ar, ai = c
            off = pl.multiple_of((ng - 1 - i) * 8, 8)
            gr = gb[pl.ds(off, 8), 0:NST]
            gi = gb[pl.ds(off, 8), NST:2 * NST]
            return lre * ar + lim * ai + gr, lre * ai - lim * ar + gi

        ar, ai = lax.fori_loop(0, ng, step, (sre[...], sim[...]))
        sre[...] = ar
        sim[...] = ai

        @pl.when(j == nt - 1)
        def _():
            pr, pi = _pow2k(lre[0:1], -lim[0:1], nsq)
            cr = jnp.zeros((1, NST), F32)
            ci = jnp.zeros((1, NST), F32)
            are_ref[7:8, :] = cr
            aim_ref[7:8, :] = ci
            for k in range(6, -1, -1):
                cr, ci = ar[k + 1:k + 2] + pr * cr - pi * ci, ai[k + 1:k + 2] + pr * ci + pi * cr
                are_ref[k:k + 1, :] = cr
                aim_ref[k:k + 1, :] = ci

    row = lambda w: pl.BlockSpec((tt, w), lambda b, j: (b * nt + nt - 1 - j, 0))
    return pl.pallas_call(
        body, name="ssm_bwd_a", grid=(nb, nt),
        out_shape=[jax.ShapeDtypeStruct((n, D_SSM), F32), jax.ShapeDtypeStruct((n, 2 * D_SSM), BF16),
                   jax.ShapeDtypeStruct((nb * 8, NST), F32), jax.ShapeDtypeStruct((nb * 8, NST), F32)],
        in_specs=[row(D_SSM), row(2 * D_SSM), row(D_SSM), _VM, _VM, _VM, _VM],
        out_specs=[row(D_SSM), row(2 * D_SSM), pl.BlockSpec((8, NST), lambda b, j: (b, 0)),
                   pl.BlockSpec((8, NST), lambda b, j: (b, 0))],
        scratch_shapes=[pltpu.VMEM((8, NST), F32), pltpu.VMEM((8, NST), F32), pltpu.VMEM((tt, 2 * NST), F32)],
        compiler_params=_cp(("arbitrary", "arbitrary")),
    )(dys_p, z, ypre, w_glu, cm, lre8, lim8)


def _ssm_bwd_b(dy, u_p, st, fcr, fci, air, aii, bm, cm, dvec, lre8, lim8, S, tt):
    n = u_p.shape[0]
    nb, nt = n // S, S // tt
    ng = tt // 8

    def body(dy_ref, u_ref, st_ref, stp_ref, fcr_ref, fci_ref, air_ref, aii_ref, bm_ref, cm_ref, d_ref, lre_ref, lim_ref,
             du_ref, dcm_ref, dbm_ref, dd_ref, dlr_ref, dli_ref, are, aim, accr, acci, sp, ab):
        b = pl.program_id(0)
        j = pl.program_id(1)
        jt = nt - 1 - j

        @pl.when((b == 0) & (j == 0))
        def _():
            dcm_ref[...] = jnp.zeros_like(dcm_ref)
            dbm_ref[...] = jnp.zeros_like(dbm_ref)
            dd_ref[...] = jnp.zeros_like(dd_ref)
            accr[...] = jnp.zeros_like(accr)
            acci[...] = jnp.zeros_like(acci)

        @pl.when(j == 0)
        def _():
            are[...] = air_ref[...]
            aim[...] = aii_ref[...]

        sp[8:tt + 8, :] = st_ref[...].astype(F32)

        @pl.when(jt == 0)
        def _():
            sp[0:8, 0:NST] = fcr_ref[...]
            sp[0:8, NST:2 * NST] = fci_ref[...]

        @pl.when(jt != 0)
        def _():
            sp[0:8, :] = stp_ref[8:16, :].astype(F32)

        dy = dy_ref[...]
        u = u_ref[...]
        dyb = dy.astype(BF16)
        _bd_expand_t(dyb, cm_ref, ab)
        lre, lim = lre_ref[...], lim_ref[...]

        def step(i, c):
            ar, ai = c
            off = pl.multiple_of((ng - 1 - i) * 8, 8)
            nr = lre * ar + lim * ai + ab[pl.ds(off, 8), 0:NST]
            ni = lre * ai - lim * ar + ab[pl.ds(off, 8), NST:2 * NST]
            ab[pl.ds(off, 8), 0:NST] = nr
            ab[pl.ds(off, 8), NST:2 * NST] = ni
            pr = sp[pl.ds(off, 8), 0:NST]
            pi = sp[pl.ds(off, 8), NST:2 * NST]
            accr[...] += nr * pr + ni * pi
            acci[...] += ni * pr - nr * pi
            return nr, ni

        ar, ai = lax.fori_loop(0, ng, step, (are[...], aim[...]))
        are[...] = ar
        aim[...] = ai
        a_b = ab[...].astype(BF16)
        du_ref[...] = _bd_project_t(a_b, bm_ref) + d_ref[...] * dy
        ub = u.astype(BF16)
        for q in range(4):
            for part in range(2):
                lo = part * NST + q * 4 * QB
                s_q = st_ref[:, lo:lo + 4 * QB]
                dcm_ref[lo:lo + 4 * QB, :] += _dot_tn(s_q, dyb[:, q * QB:(q + 1) * QB])
                dbm_ref[:, lo:lo + 4 * QB] += _dot_tn(ub[:, q * QB:(q + 1) * QB], a_b[:, lo:lo + 4 * QB])
        dd_ref[...] += _colsum(dy * u)

        @pl.when((b == nb - 1) & (j == nt - 1))
        def _():
            dlr_ref[...] = _colsum(accr[...])
            dli_ref[...] = _colsum(acci[...])

    row = lambda w: pl.BlockSpec((tt, w), lambda b, j: (b * nt + nt - 1 - j, 0))
    seq8 = pl.BlockSpec((8, NST), lambda b, j: (b, 0))
    prev = pl.BlockSpec((16, 2 * NST), lambda b, j: (jnp.maximum((b * nt + nt - 1 - j) * (tt // 16) - 1, 0), 0))
    const = lambda shape: pl.BlockSpec(shape, lambda b, j: (0, 0))
    return pl.pallas_call(
        body, name="ssm_bwd_b", grid=(nb, nt),
        out_shape=[jax.ShapeDtypeStruct((n, D_SSM), F32), jax.ShapeDtypeStruct((2 * NST, QB), F32),
                   jax.ShapeDtypeStruct((QB, 2 * NST), F32), jax.ShapeDtypeStruct((1, D_SSM), F32),
                   jax.ShapeDtypeStruct((1, NST), F32), jax.ShapeDtypeStruct((1, NST), F32)],
        in_specs=[row(D_SSM), row(D_SSM), row(2 * NST), prev, seq8, seq8, seq8, seq8, _VM, _VM, _VM, _VM, _VM],
        out_specs=[row(D_SSM), const((2 * NST, QB)), const((QB, 2 * NST)), const((1, D_SSM)),
                   const((1, NST)), const((1, NST))],
        scratch_shapes=[pltpu.VMEM((8, NST), F32)] * 4 + [pltpu.VMEM((tt + 8, 2 * NST), F32),
                                                          pltpu.VMEM((tt, 2 * NST), F32)],
        compiler_params=_cp(("arbitrary", "arbitrary")),
    )(dy, u_p, st, st, fcr, fci, air, aii, bm, cm, dvec, lre8, lim8)


def _rope(v, c, s1, s2):
    return v * c + _roll(v, -16) * s1 + _roll(v, 16) * s2


def _rope_t(dv, c, s1, s2):
    return dv * c + _roll(dv * s1, 16) + _roll(dv * s2, -16)


_SCALE = (QK_NOPE + QK_ROPE) ** -0.5
_LOG2E = 1.4426950408889634
_C2 = _SCALE * _LOG2E


def _attn_fwd(q, k, v, S, tq):
    n = q.shape[0]
    nb, nq = n // S, S // tq

    def body(q_ref, k_ref, v_ref, o_ref, lr_ref):
        qi = pl.program_id(2)
        qv = q_ref[...]

        def tile(j, c, diagonal):
            m, acc = c
            off = pl.multiple_of(j * tq, tq)
            s = _dot_nt(qv, k_ref[pl.ds(off, tq), :])
            if diagonal:
                rows = lax.broadcasted_iota(jnp.int32, s.shape, 0)
                cols = lax.broadcasted_iota(jnp.int32, s.shape, 1)
                s = jnp.where(cols <= rows, s, NEG)
            mn = jnp.maximum(m, jnp.max(s, axis=1, keepdims=True))
            p = jnp.exp2(s - mn)
            acc = jnp.exp2(m - mn) * acc + _dot(p.astype(BF16), v_ref[pl.ds(off, tq), :])
            return mn, acc

        init = (jnp.full((tq, 1), NEG, F32), jnp.zeros((tq, HP), F32))
        c = lax.fori_loop(0, qi, lambda j, c: tile(j, c, False), init)
        m, acc = tile(qi, c, True)
        l = acc[:, V_HEAD:V_HEAD + 1]
        vlane = lax.broadcasted_iota(jnp.int32, acc.shape, 1)
        o_ref[...] = jnp.where(vlane < V_HEAD, acc / l, 0.0).astype(BF16)
        lane = lax.broadcasted_iota(jnp.int32, (8, HP), 1)
        lse = jnp.broadcast_to(m + jnp.log(l) * _LOG2E, (tq, HP))
        lr_ref[...] = _rows_of(lse, jnp.where(lane == 0, 1.0, 0.0).astype(BF16))

    qs = pl.BlockSpec((tq, HP), lambda b, h, i: (b * nq + i, h))
    ks = pl.BlockSpec((S, HP), lambda b, h, i: (b, h))
    return pl.pallas_call(
        body, name="attn_fwd", grid=(nb, NH, nq),
        out_shape=[jax.ShapeDtypeStruct((n, NH * HP), BF16), jax.ShapeDtypeStruct((nb * NH * 8, S), F32)],
        in_specs=[qs, ks, ks], out_specs=[qs, pl.BlockSpec((8, tq), lambda b, h, i: (b * NH + h, i))],
        compiler_params=_cp(("parallel", "parallel", "arbitrary")),
    )(q, k, v)


def _rows_of(x, pick):
    x1 = x.astype(BF16)
    r1 = x - x1.astype(F32)
    x2 = r1.astype(BF16)
    x3 = (r1 - x2.astype(F32)).astype(BF16)
    return _dot_nt(pick, x1) + _dot_nt(pick, x2) + _dot_nt(pick, x3)


def _attn_bwd(q, k, v, dob, lrow, drow, S, tq):
    n = q.shape[0]
    nb, nq = n // S, S // tq

    def body(q_ref, k_ref, v_ref, do_ref, lr_ref, dr_ref, dqo_ref, dk_ref, dv_ref, dq_ref):
        kj = pl.program_id(2)

        @pl.when(kj == 0)
        def _():
            dq_ref[...] = jnp.zeros_like(dq_ref)

        kt = k_ref[...]
        vt = v_ref[...]

        def tile(i, c, diagonal):
            dk, dv = c
            off = pl.multiple_of(i * tq, tq)
            qv = q_ref[pl.ds(off, tq), :]
            dob = do_ref[pl.ds(off, tq), :]
            lr = lr_ref[0:1, pl.ds(off, tq)]
            dr = dr_ref[0:1, pl.ds(off, tq)]
            st = _dot_nt(kt, qv)
            dpt = _dot_nt(vt, dob)
            pt = jnp.exp2(st - lr)
            if diagonal:
                keys = lax.broadcasted_iota(jnp.int32, pt.shape, 0)
                qrys = lax.broadcasted_iota(jnp.int32, pt.shape, 1)
                pt = jnp.where(keys <= qrys, pt, 0.0)
            dst = (pt * (dpt - dr)).astype(BF16)
            dq_ref[pl.ds(off, tq), :] += _dot_tn(dst, kt)
            return dk + _dot(dst, qv), dv + _dot(pt.astype(BF16), dob)

        zero = jnp.zeros((tq, HP), F32)
        c = tile(kj, (zero, zero), True)
        dk, dv = lax.fori_loop(kj + 1, nq, lambda i, c: tile(i, c, False), c)
        dk_ref[...] = dk.astype(BF16)
        dv_ref[...] = dv.astype(BF16)

        @pl.when(kj == nq - 1)
        def _():
            dqo_ref[...] = dq_ref[...].astype(BF16)

    ts = pl.BlockSpec((tq, HP), lambda b, h, i: (b * nq + i, h))
    fs = pl.BlockSpec((S, HP), lambda b, h, i: (b, h))
    rs = pl.BlockSpec((8, S), lambda b, h, i: (b * NH + h, 0))
    return pl.pallas_call(
        body, name="attn_bwd", grid=(nb, NH, nq),
        out_shape=[jax.ShapeDtypeStruct((n, NH * HP), BF16)] * 3,
        in_specs=[fs, ts, ts, fs, rs, rs], out_specs=[fs, ts, ts],
        scratch_shapes=[pltpu.VMEM((S, HP), F32)],
        compiler_params=_cp(("parallel", "parallel", "arbitrary")),
    )(q, k, v, dob, lrow, drow)


def _p2(yssm, oattn, x, target, modp, gs, ga, w_out, g2, gf, w_ff1, w_ff2, S, tm):
    n = x.shape[0]
    tps = S // tm
    nb = n // S

    def body(ys_ref, oa_ref, x_ref, t_ref, mod_ref, gs_ref, ga_ref, wo_ref, g2_ref, gf_ref, w1_ref, w2_ref,
             yn_ref, h2_ref, dx1_ref, r_ref, da_ref, dff_ref, do_ref, dys_ref, doa_ref, dr_ref,
             accs_ref, accg_ref, accg3_ref):
        i = pl.program_id(0)
        sh2, sc2, gt2 = mod_ref[0, 3:4, :], mod_ref[0, 4:5, :], mod_ref[0, 5:6, :]
        fsh, fsc = mod_ref[0, 6:7, :], mod_ref[0, 7:8, :]
        yh, rs = _rms(ys_ref[...], D_SSM)
        oa = oa_ref[...].astype(F32)
        ah, ra_ = _rms(oa, D_ATTN)
        yn = jnp.concatenate([yh * gs_ref[...], ah * ga_ref[...]], axis=1).astype(BF16)
        yn_ref[...] = yn
        o = _dot(yn, wo_ref[...])
        x1 = x_ref[...] + mod_ref[0, 2:3, :] * o
        x1h, r2 = _rms(x1, D)
        g2_v = g2_ref[...]
        h2 = ((x1h * g2_v) * (1.0 + sc2) + sh2).astype(BF16)
        h2_ref[...] = h2
        a = _dot(h2, w1_ref[...])
        ra = jnp.maximum(a, 0.0)
        rb = (ra * ra).astype(BF16)
        r_ref[...] = rb
        ff = _dot(rb, w2_ref[...])
        x2 = x1 + gt2 * ff
        x2h, rf = _rms(x2, D)
        gf_v = gf_ref[...]
        outn = x2h * gf_v
        err = outn * (1.0 + fsc) + fsh - t_ref[...]
        dout = err * (1.0 / D)
        doutn = dout * (1.0 + fsc)
        dx2 = _rms_bwd(doutn * gf_v, x2h, rf, D)
        dff = (gt2 * dx2).astype(BF16)
        dff_ref[...] = dff
        dr = _dot_nt(dff, w2_ref[...])
        da = (dr * (2.0 * ra)).astype(BF16)
        da_ref[...] = da
        dh2 = _dot_nt(da, w1_ref[...])
        dn2 = dh2 * (1.0 + sc2)
        dx1 = dx2 + _rms_bwd(dn2 * g2_v, x1h, r2, D)
        dx1_ref[...] = dx1
        dob = (mod_ref[0, 2:3, :] * dx1).astype(BF16)
        do_ref[...] = dob
        dyn = _dot_nt(dob, wo_ref[...])
        d1 = dyn[:, 0:D_SSM]
        d2 = dyn[:, D_SSM:D_SSM + NH * HP]
        dys_ref[...] = _rms_bwd(d1 * gs_ref[...], yh, rs, D_SSM)
        doa = _rms_bwd(d2 * ga_ref[...], ah, ra_, D_ATTN)
        doa_ref[...] = doa.astype(BF16)
        prod = doa * oa
        ones = jnp.ones((8, HP), BF16)
        for h in range(NH):
            dr_ref[h * 8:(h + 1) * 8, :] = _rows_of(prod[:, h * HP:(h + 1) * HP], ones)

        @pl.when(i % tps == 0)
        def _():
            accs_ref[...] = jnp.zeros_like(accs_ref)

        @pl.when(i == 0)
        def _():
            accg_ref[...] = jnp.zeros_like(accg_ref)
            accg3_ref[...] = jnp.zeros_like(accg3_ref)

        accs_ref[0, 2:3, :] += _colsum(dx1 * o)
        accg3_ref[0:1, 0:D_SSM] += _colsum(d1 * yh)
        accg3_ref[1:2, :] += _colsum(d2 * ah)
        accs_ref[0, 3:4, :] += _colsum(dh2)
        accs_ref[0, 4:5, :] += _colsum(dh2 * (x1h * g2_v))
        accs_ref[0, 5:6, :] += _colsum(dx2 * ff)
        accs_ref[0, 6:7, :] += _colsum(dout)
        accs_ref[0, 7:8, :] += _colsum(dout * outn)
        accg_ref[0:1, :] += _colsum(dn2 * x1h)
        accg_ref[1:2, :] += _colsum(doutn * x2h)
        accg_ref[2:3, :] += _colsum(err * err) * (0.5 / D)

    row = lambda w: pl.BlockSpec((tm, w), lambda i: (i, 0))
    return pl.pallas_call(
        body, name="p2_mlp_loss", grid=(n // tm,),
        out_shape=[jax.ShapeDtypeStruct((n, D_SSM + NH * HP), BF16), jax.ShapeDtypeStruct((n, D), BF16),
                   jax.ShapeDtypeStruct((n, D), F32), jax.ShapeDtypeStruct((n, D_FF), BF16),
                   jax.ShapeDtypeStruct((n, D_FF), BF16), jax.ShapeDtypeStruct((n, D), BF16),
                   jax.ShapeDtypeStruct((n, D), BF16), jax.ShapeDtypeStruct((n, D_SSM), F32),
                   jax.ShapeDtypeStruct((n, NH * HP), BF16), jax.ShapeDtypeStruct((nb * NH * 8, S), F32),
                   jax.ShapeDtypeStruct((nb, 8, D), F32), jax.ShapeDtypeStruct((8, D), F32),
                   jax.ShapeDtypeStruct((8, NH * HP), F32)],
        in_specs=[row(D_SSM), row(NH * HP), row(D), row(D), pl.BlockSpec((1, 8, D), lambda i: (i // tps, 0, 0)),
                  _VM, _VM, _VM, _VM, _VM, _VM, _VM],
        out_specs=[row(D_SSM + NH * HP), row(D),
                   row(D), row(D_FF), row(D_FF), row(D),
                   row(D), row(D_SSM), row(NH * HP), pl.BlockSpec((NH * 8, tm), lambda i: (i // tps, i % tps)),
                   pl.BlockSpec((1, 8, D), lambda i: (i // tps, 0, 0)),
                   pl.BlockSpec((8, D), lambda i: (0, 0)), pl.BlockSpec((8, NH * HP), lambda i: (0, 0))],
        compiler_params=_cp(("arbitrary",)),
    )(yssm, oattn, x, target, modp, gs, ga, w_out, g2, gf, w_ff1, w_ff2)


def _wgrad(a, b, name, col_slots=0):
    n, k1 = a.shape
    k2 = b.shape[1]
    bn = next((b for b in (1024, 512) if n % b == 0), n)
    bk1 = next((b for b in (1024, 512) if k1 % b == 0), k1)
    bk2 = k2 // col_slots if col_slots else (1024 if (k2 % 1024 == 0) else k2)

    def body(a_ref, b_ref, o_ref):
        @pl.when(pl.program_id(2) == 0)
        def _():
            o_ref[...] = jnp.zeros_like(o_ref)

        o_ref[...] += _dot_tn(a_ref[...], b_ref[...]).reshape(o_ref.shape)

    if col_slots:
        out_shape = jax.ShapeDtypeStruct((col_slots, k1, bk2), F32)
        out_spec = pl.BlockSpec((1, bk1, bk2), lambda i, j, t: (j, i, 0))
    else:
        out_shape = jax.ShapeDtypeStruct((k1, k2), F32)
        out_spec = pl.BlockSpec((bk1, bk2), lambda i, j, t: (i, j))
    return pl.pallas_call(
        body, name=name, grid=(k1 // bk1, k2 // bk2, n // bn),
        out_shape=out_shape,
        in_specs=[pl.BlockSpec((bn, bk1), lambda i, j, t: (t, i)), pl.BlockSpec((bn, bk2), lambda i, j, t: (t, j))],
        out_specs=out_spec,
        compiler_params=_cp(("parallel", "parallel", "arbitrary")),
    )(a, b)


def _row_block(rows):
    if rows <= 256:
        return rows
    return next(b for b in (256, 192, 128, 64, 32, 16, 8) if rows % b == 0)


def _add_half(g, recv, cidx, name):
    _, rows2, w = g.shape
    rows = rows2 // 2
    br = _row_block(rows)
    nblk = rows // br

    def body(c_ref, g_ref, r_ref, o_ref):
        o_ref[...] = (g_ref[...] + r_ref[...]).astype(BF16)

    return pl.pallas_call(
        body, name=name,
        grid_spec=pltpu.PrefetchScalarGridSpec(
            num_scalar_prefetch=1, grid=(4, nblk),
            in_specs=[pl.BlockSpec((1, br, w), lambda s, i, c: (s, c[0] * nblk + i, 0)),
                      pl.BlockSpec((1, br, w), lambda s, i, c: (s, i, 0))],
            out_specs=pl.BlockSpec((1, br, w), lambda s, i, c: (s, i, 0))),
        out_shape=jax.ShapeDtypeStruct((4, rows, w), BF16),
        compiler_params=_cp(("parallel", "parallel")),
    )(cidx, g, recv)


def _add_chips(r, name):
    _, rows, w = r.shape
    br = _row_block(rows)

    def body(r_ref, o_ref):
        f = lambda k: r_ref[k].astype(F32)
        o_ref[...] = ((f(0) + f(1)) + f(2)) + f(3)

    return pl.pallas_call(
        body, name=name, grid=(rows // br,),
        out_shape=jax.ShapeDtypeStruct((rows, w), F32),
        in_specs=[pl.BlockSpec((4, br, w), lambda i: (0, i, 0))],
        out_specs=pl.BlockSpec((br, w), lambda i: (i, 0)),
        compiler_params=_cp(("parallel",)),
    )(r)


def _pair_sum(a, sa, b, sb):
    def body(a_ref, sa_ref, b_ref, sb_ref, oa_ref, ob_ref):
        oa_ref[...] = (a_ref[...].astype(F32) + sa_ref[...].astype(F32)).astype(BF16)
        ob_ref[...] = b_ref[...] + sb_ref[...]

    return pl.pallas_call(
        body, name="small_grad_pair_sum",
        out_shape=[jax.ShapeDtypeStruct(a.shape, BF16), jax.ShapeDtypeStruct(b.shape, F32)],
        in_specs=[_VM] * 4, out_specs=[_VM, _VM], compiler_params=_cp(),
    )(a, sa, b, sb)


def _sum_devices(a, b):
    def body(a_ref, b_ref, oa_ref, ob_ref):
        acc = a_ref[0:1, :].astype(F32)
        accb = b_ref[0:1, :]
        for k in range(1, a.shape[0]):
            acc = acc + a_ref[k:k + 1, :].astype(F32)
            accb = accb + b_ref[k:k + 1, :]
        oa_ref[...] = acc
        ob_ref[...] = accb

    return pl.pallas_call(
        body, name="small_grad_sum",
        out_shape=[jax.ShapeDtypeStruct((1, a.shape[1]), F32), jax.ShapeDtypeStruct((1, b.shape[1]), F32)],
        in_specs=[_VM, _VM], out_specs=[_VM, _VM], compiler_params=_cp(),
    )(a, b)


def _adamw_math(wv, gv, mv, vv):
    m_new = ADAM_B1 * mv + (1.0 - ADAM_B1) * gv
    v_new = ADAM_B2 * vv + (1.0 - ADAM_B2) * (gv * gv)
    m_hat = m_new / (1.0 - ADAM_B1 ** ADAM_STEP)
    v_hat = v_new / (1.0 - ADAM_B2 ** ADAM_STEP)
    return -ADAM_LR * (m_hat / (jnp.sqrt(v_hat) + ADAM_EPS) + ADAM_WD * wv), m_new, v_new


def _adamw_small(ws, gs, ms, vs):
    k = len(ws)

    def body(*refs):
        ins, outs = refs[:4 * k], refs[4 * k:]
        for t in range(k):
            d, m_new, v_new = _adamw_math(ins[t][...], ins[k + t][...], ins[2 * k + t][...], ins[3 * k + t][...])
            outs[t][...] = d
            outs[k + t][...] = m_new
            outs[2 * k + t][...] = v_new

    shapes = [jax.ShapeDtypeStruct(w.shape, F32) for w in ws]
    return pl.pallas_call(
        body, name="adamw_small", out_shape=shapes * 3,
        in_specs=[_VM] * (4 * k), out_specs=[_VM] * (3 * k), compiler_params=_cp(),
    )(*ws, *gs, *ms, *vs)


def _adamw(w, g, m, v, name):
    rows, wd = w.shape
    br = _row_block(rows)

    def body(w_ref, g_ref, m_ref, v_ref, d_ref, nm_ref, nv_ref):
        d, m_new, v_new = _adamw_math(w_ref[...], g_ref[...], m_ref[...], v_ref[...])
        d_ref[...] = d
        nm_ref[...] = m_new
        nv_ref[...] = v_new

    spec = pl.BlockSpec((br, wd), lambda i: (i, 0))
    return pl.pallas_call(
        body, name=name, grid=(rows // br,),
        out_shape=[jax.ShapeDtypeStruct((rows, wd), F32)] * 3,
        in_specs=[spec] * 4, out_specs=[spec] * 3,
        compiler_params=_cp(("parallel",)),
    )(w, g, m, v)


def _adamw_halves(w, mine, other, m, v, cidx, name):
    rows, wd = w.shape
    h = rows // 2
    br = _row_block(h)
    nblk = h // br

    def body(c_ref, w_ref, a_ref, b_ref, m_ref, v_ref, g_ref, d_ref, nm_ref, nv_ref):
        gv = jnp.where(pl.program_id(0) == c_ref[0], a_ref[...], b_ref[...])
        d, m_new, v_new = _adamw_math(w_ref[...], gv, m_ref[...], v_ref[...])
        g_ref[...] = gv
        d_ref[...] = d
        nm_ref[...] = m_new
        nv_ref[...] = v_new

    full = pl.BlockSpec((br, wd), lambda hf, i, c: (hf * nblk + i, 0))
    half = pl.BlockSpec((br, wd), lambda hf, i, c: (i, 0))
    return pl.pallas_call(
        body, name=name,
        grid_spec=pltpu.PrefetchScalarGridSpec(
            num_scalar_prefetch=1, grid=(2, nblk),
            in_specs=[full, half, half, full, full], out_specs=[full] * 4),
        out_shape=[jax.ShapeDtypeStruct((rows, wd), F32)] * 4,
        compiler_params=_cp(("parallel", "parallel")),
    )(cidx, w, mine, other, m, v)


def _other_chips(x, y):
    return [(1 - x, y), (x, 1 - y), (1 - x, 1 - y)]


def _other_devices(x, y, c):
    flip = lambda v, d: (1 - v) if d else v
    return [(flip(x, dx), flip(y, dy), flip(c, dc))
            for dx in (0, 1) for dy in (0, 1) for dc in (0, 1) if (dx, dy, dc) != (0, 0, 0)]


def _exchange(name, ins, out_shapes, n_local, n_remote, plan):
    ni, no = len(ins), len(out_shapes)

    def body(*refs):
        in_refs, out_refs = refs[:ni], refs[ni:ni + no]
        send_sems, recv_sems, local_sems = refs[ni + no:]
        x, y, c = lax.axis_index("x"), lax.axis_index("y"), lax.axis_index("c")
        local, remote = plan(in_refs, out_refs, x, y, c)
        assert len(local) == n_local and len(remote) == n_remote

        def push(k, src, dst, dev):
            return pltpu.make_async_remote_copy(src_ref=src, dst_ref=dst, send_sem=send_sems.at[k],
                                                recv_sem=recv_sems.at[k], device_id=dev, device_id_type=MESH)

        own = [pltpu.make_async_copy(s, d, local_sems.at[i]) for i, (s, d) in enumerate(local)]
        for cp in own:
            cp.start()
        sends = [push(k, s, d, dev) for k, (s, d, dev, _) in enumerate(remote)]
        for cp in sends:
            cp.start()
        for k, (s, _, dev, landing) in enumerate(remote):
            push(k, s, landing, dev).wait_recv()
        for cp in sends:
            cp.wait_send()
        for cp in own:
            cp.wait()

    return pl.pallas_call(
        body, name=name, out_shape=out_shapes,
        in_specs=[_ANY] * ni, out_specs=[_ANY] * no,
        scratch_shapes=[pltpu.SemaphoreType.DMA((n_remote,)), pltpu.SemaphoreType.DMA((n_remote,)),
                        pltpu.SemaphoreType.DMA((max(n_local, 1),))],
        compiler_params=pltpu.CompilerParams(has_side_effects=True),
    )(*ins)


def _gather_chips(name, shards, everyone=()):
    ns, ne = len(shards), len(everyone)
    outs = [jax.ShapeDtypeStruct((4,) + a.shape, a.dtype) for a in shards]
    outs += [jax.ShapeDtypeStruct((8,) + a.shape, a.dtype) for a in everyone]

    def plan(i, o, x, y, c):
        mine, me = 2 * x + y, 4 * x + 2 * y + c
        local, remote = [], []
        for t in range(ns):
            local.append((i[t], o[t].at[mine]))
            for px, py in _other_chips(x, y):
                remote.append((i[t], o[t].at[mine], (px, py, c), o[t].at[2 * px + py]))
        for t in range(ns, ns + ne):
            local.append((i[t], o[t].at[me]))
            for px, py, pc in _other_devices(x, y, c):
                remote.append((i[t], o[t].at[me], (px, py, pc), o[t].at[4 * px + 2 * py + pc]))
        return local, remote

    return _exchange(name, list(shards) + list(everyone), outs, ns + ne, 3 * ns + 7 * ne, plan)


_HBM = pl.BlockSpec(memory_space=pltpu.HBM)
_SEM = pl.BlockSpec(memory_space=pltpu.SEMAPHORE)
_EFFECT = pltpu.SideEffectType.DATAFLOW_SIDE_EFFECTING


def _split_start(name, ins, land_shapes, n_remote, plan, after):
    ni, nl = len(ins), len(land_shapes)
    srcs = [pltpu.with_memory_space_constraint(a, pltpu.HBM) for a in ins]
    lands = [pltpu.with_memory_space_constraint(lax.empty(s.shape, s.dtype), pltpu.HBM) for s in land_shapes]

    def body(*refs):
        src, land = refs[:ni], refs[ni:ni + nl]
        first = ni + nl + 1
        send, recv = refs[first:first + n_remote], refs[first + n_remote:first + 2 * n_remote]
        token = refs[first + 2 * n_remote + ni + nl]
        x, y, c = lax.axis_index("x"), lax.axis_index("y"), lax.axis_index("c")
        remote = plan(src, land, x, y, c)
        assert len(remote) == n_remote
        for k, (s, d, dev, _) in enumerate(remote):
            pltpu.make_async_remote_copy(src_ref=s, dst_ref=d, send_sem=send[k], recv_sem=recv[k],
                                         device_id=dev, device_id_type=MESH).start()
        token[...] = jnp.zeros_like(token)

    out = pl.pallas_call(
        body, name=name + "_start",
        out_shape=[pltpu.SemaphoreType.DMA(())] * (2 * n_remote)
                  + [pltpu.HBM(a.shape, a.dtype) for a in ins] + [pltpu.HBM(s.shape, s.dtype) for s in land_shapes]
                  + [jax.ShapeDtypeStruct((8, 128), F32)],
        in_specs=[_HBM] * (ni + nl) + [_ANY], out_specs=[_SEM] * (2 * n_remote) + [_HBM] * (ni + nl) + [_VM],
        input_output_aliases={t: 2 * n_remote + t for t in range(ni + nl)},
        compiler_params=pltpu.CompilerParams(has_side_effects=_EFFECT),
    )(*srcs, *lands, after)
    sems, thru = out[:2 * n_remote], out[2 * n_remote:2 * n_remote + ni + nl]
    return (name, sems, thru[:ni], thru[ni:], n_remote, plan), out[-1]


def _split_wait(handle, after):
    name, sems, srcs, lands, n_remote, plan = handle
    ni, nl = len(srcs), len(lands)

    def body(*refs):
        src, land = refs[:ni], refs[ni:ni + nl]
        send, recv = refs[ni + nl:ni + nl + n_remote], refs[ni + nl + n_remote:ni + nl + 2 * n_remote]
        x, y, c = lax.axis_index("x"), lax.axis_index("y"), lax.axis_index("c")
        for k, (s, _, dev, landing) in enumerate(plan(src, land, x, y, c)):
            cp = pltpu.make_async_remote_copy(src_ref=s, dst_ref=landing, send_sem=send[k], recv_sem=recv[k],
                                              device_id=dev, device_id_type=MESH)
            cp.wait_send()
            cp.wait_recv()

    out = pl.pallas_call(
        body, name=name + "_wait",
        out_shape=[pltpu.HBM(a.shape, a.dtype) for a in srcs] + [pltpu.HBM(a.shape, a.dtype) for a in lands],
        in_specs=[_HBM] * (ni + nl) + [_SEM] * (2 * n_remote) + [_ANY], out_specs=[_HBM] * (ni + nl),
        input_output_aliases={t: t for t in range(ni + nl)},
        compiler_params=pltpu.CompilerParams(has_side_effects=_EFFECT),
    )(*srcs, *lands, *sems, after)
    return out[:ni], out[ni:]


def _plan_to_chips(src, land, x, y, c):
    mine = 2 * x + y
    return [(src[t], land[t].at[mine], (px, py, c), land[t].at[2 * px + py])
            for t in range(len(src)) for px, py in _other_chips(x, y)]


def _plan_swap_halves(src, land, x, y, c):
    out = []
    for t in range(len(src)):
        h = src[t].shape[1] // 2
        out.append((src[t].at[:, pl.ds(pl.multiple_of((1 - c) * h, 8), h), :], land[t], (x, y, 1 - c), land[t]))
    return out


def _plan_scatter_chips(src, land, x, y, c):
    mine = 2 * x + y
    return [(src[t].at[2 * px + py], land[t].at[mine], (px, py, c), land[t].at[2 * px + py])
            for t in range(len(src)) for px, py in _other_chips(x, y)]


def _swap_halves(gs, everyone, whole):
    ns, ne, nw = len(gs), len(everyone), len(whole)
    outs = [jax.ShapeDtypeStruct((4, g.shape[1] // 2, g.shape[2]), g.dtype) for g in gs]
    outs += [jax.ShapeDtypeStruct((8,) + a.shape, a.dtype) for a in everyone]
    outs += [jax.ShapeDtypeStruct(a.shape, a.dtype) for a in whole]

    def plan(i, o, x, y, c):
        me = 4 * x + 2 * y + c
        local, remote = [], []
        for t in range(ns):
            h = gs[t].shape[1] // 2
            theirs = i[t].at[:, pl.ds(pl.multiple_of((1 - c) * h, 8), h), :]
            remote.append((theirs, o[t], (x, y, 1 - c), o[t]))
        for t in range(ns, ns + ne):
            local.append((i[t], o[t].at[me]))
            for px, py, pc in _other_devices(x, y, c):
                remote.append((i[t], o[t].at[me], (px, py, pc), o[t].at[4 * px + 2 * py + pc]))
        for t in range(ns + ne, ns + ne + nw):
            remote.append((i[t], o[t], (x, y, 1 - c), o[t]))
        return local, remote

    return _exchange("grad_swap_sibling", list(gs) + list(everyone) + list(whole), outs, ne, ns + 7 * ne + nw, plan)


def _scatter_chips(parts, per_chip):
    ns, ng = len(parts), len(per_chip)
    outs = [jax.ShapeDtypeStruct(a.shape, a.dtype) for a in parts]
    outs += [jax.ShapeDtypeStruct((4,) + a.shape, a.dtype) for a in per_chip]

    def plan(i, o, x, y, c):
        mine = 2 * x + y
        local, remote = [], []
        for t in range(ns):
            local.append((i[t].at[mine], o[t].at[mine]))
            for px, py in _other_chips(x, y):
                remote.append((i[t].at[2 * px + py], o[t].at[mine], (px, py, c), o[t].at[2 * px + py]))
        for t in range(ns, ns + ng):
            local.append((i[t], o[t].at[mine]))
            for px, py in _other_chips(x, y):
                remote.append((i[t], o[t].at[mine], (px, py, c), o[t].at[2 * px + py]))
        return local, remote

    return _exchange("grad_scatter_chips", list(parts) + list(per_chip), outs, ns + ng, 3 * (ns + ng), plan)


def _join_halves(halves):
    ns = len(halves)
    outs = [jax.ShapeDtypeStruct(a.shape, a.dtype) for a in halves]

    def plan(i, o, x, y, c):
        return [], [(i[t], o[t], (x, y, 1 - c), o[t]) for t in range(ns)]

    return _exchange("grad_join_sibling", list(halves), outs, 0, ns, plan)


def _pad_heads_cols(w, per, used):
    k = w.shape[0]
    w = w.reshape(k, NH, per)[:, :, :used]
    return jnp.pad(w, ((0, 0), (0, 0), (0, HP - used))).reshape(k, NH * HP)


def _unpad_heads_cols(w, used):
    k = w.shape[0]
    return w.reshape(k, NH, HP)[:, :, :used]


def _prep_weights(wf):
    bf = lambda a: a.astype(BF16)
    out = {}
    out["w_in"] = jnp.pad(bf(wf["w_in"]), ((0, 0), (0, IN_PAD - IN_COLS)))
    out["w_glu"] = bf(wf["w_glu"])
    out["w_uq"] = _pad_heads_cols(bf(wf["w_uq"]), QK_NOPE + QK_ROPE, QK_NOPE + QK_ROPE)
    wkv = bf(wf["w_ukv"]).reshape(KV_LORA, NH, QK_NOPE + V_HEAD)
    wk = jnp.pad(wkv[:, :, :QK_NOPE], ((0, 0), (0, 0), (0, HP - QK_NOPE))).reshape(KV_LORA, NH * HP)
    wv = jnp.pad(wkv[:, :, QK_NOPE:], ((0, 0), (0, 0), (0, HP - V_HEAD))).reshape(KV_LORA, NH * HP)
    out["w_ukv"] = jnp.concatenate([wk, wv], axis=1)
    return out


def _prep_late_weights(wf):
    bf = lambda a: a.astype(BF16)
    out = {}
    wo = bf(wf["w_out"])
    wo_a = jnp.pad(wo[D_SSM:].reshape(NH, V_HEAD, D), ((0, 0), (0, HP - V_HEAD), (0, 0))).reshape(NH * HP, D)
    out["w_out"] = jnp.concatenate([wo[:D_SSM], wo_a], axis=0)
    out["w_ff1"] = bf(wf["w_ff1"])
    out["w_ff2"] = bf(wf["w_ff2"])
    return out


def _rope_tables(positions):
    inv_freq = ROPE_BASE ** (-jnp.arange(0, QK_ROPE, 2, dtype=F32) / QK_ROPE)
    ang = positions.astype(F32)[:, None] * inv_freq
    cos, sin = jnp.cos(ang), jnp.sin(ang)
    n = positions.shape[0]
    one = jnp.ones((n, QK_NOPE), F32)
    z16 = jnp.zeros((n, 16), F32)
    z32 = jnp.zeros((n, 32), F32)
    z64 = jnp.zeros((n, QK_NOPE), F32)
    rc = jnp.concatenate([one, cos, cos, z32], axis=1)
    rs1 = jnp.concatenate([z64, -sin, z16, z32], axis=1)
    rs2 = jnp.concatenate([z64, z16, sin, z32], axis=1)
    return rc, rs1, rs2


def _permute_rows(a, S):
    n, w = a.shape
    return a.reshape(n // S, 8, S // 8, w).transpose(0, 2, 1, 3).reshape(n, w)


def _unpermute_rows(a, S):
    n, w = a.shape
    return a.reshape(n // S, S // 8, 8, w).transpose(0, 2, 1, 3).reshape(n, w)


def _block_diag_in(bb):
    eye = jnp.eye(8, dtype=bb.dtype)
    blocks = jnp.einsum("qgph,gk->qghkp", bb.reshape(4, 8, P, H), eye).reshape(4, QB, QS)
    return blocks.transpose(1, 0, 2).reshape(QB, NST)


def _block_diag_out(cc):
    eye = jnp.eye(8, dtype=cc.dtype)
    return jnp.einsum("qghp,gk->qgpkh", cc.reshape(4, 8, H, P), eye).reshape(NST, QB)


def _slots(full):
    r, cdim = full.shape
    return full.reshape(r, 4, cdim // 4).transpose(1, 0, 2)


def _unslots(g):
    s, r, cs = g.shape
    return g.transpose(1, 0, 2).reshape(r, s * cs)


def _local_step(x, positions, target, modp, wf, late_weights=None, reducer=None):
    nb, S, _ = x.shape
    n = nb * S
    tm = min(256, S)
    tr = min(512, S)
    tt = min(512, S)
    tq = min(512, S // 2)
    kw = _prep_weights(wf)
    row = lambda a: a.reshape(1, -1).astype(F32)

    xf = x.reshape(n, D)
    tf = target.reshape(n, D)
    g1, g2, gf = row(wf["norm1_g"]), row(wf["norm2_g"]), row(wf["final_norm_g"])
    rc, rs1, rs2 = _rope_tables(positions.reshape(n))
    gq, gkv = row(wf["q_norm_g"]), row(wf["kv_norm_g"])
    h1, u, lat, q, k, v, qn, kvn = _f1_fwd(xf, modp, g1, kw["w_in"], rc, rs1, rs2, gq, gkv,
                                           kw["w_uq"], kw["w_ukv"], S, tr)

    col = lambda a: a.reshape(NST, 1)
    lam_re, lam_im = col(wf["ssm_lambda_re"]), col(wf["ssm_lambda_im"])
    logdt = jnp.repeat(wf["ssm_log_dt"].reshape(G, 1), P, axis=1).reshape(NST, 1)
    b_re, b_im = wf["ssm_b_re"].reshape(NST, H), wf["ssm_b_im"].reshape(NST, H)
    lbr, lbi, bbr, bbi = _ssm_param_fwd(lam_re, lam_im, logdt, b_re, b_im)
    lre8 = jnp.broadcast_to(lbr.reshape(1, NST), (8, NST))
    lim8 = jnp.broadcast_to(lbi.reshape(1, NST), (8, NST))
    bm = jnp.concatenate([_block_diag_in(bbr.reshape(G, P, H)), _block_diag_in(bbi.reshape(G, P, H))],
                         axis=1).astype(BF16)
    cm = jnp.concatenate([_block_diag_out(wf["ssm_c_re"]), -_block_diag_out(wf["ssm_c_im"])], axis=0).astype(BF16)
    dvec = row(wf["ssm_d"])
    u_p = _permute_rows(u, S)
    fcr, fci = _ssm_local(u_p, bm, lre8, lim8, S, tt)
    st, ypre, z, gact, yssm_p = _ssm_fwd(u_p, fcr, fci, bm, cm, dvec, kw["w_glu"], lre8, lim8, S, tt)
    yssm = _unpermute_rows(yssm_p, S)

    oattn, lrow = _attn_fwd(q, k, v, S, tq)

    gs = row(wf["ssm_out_g"])
    ga = jnp.pad(wf["attn_out_g"].reshape(NH, V_HEAD), ((0, 0), (0, HP - V_HEAD))).reshape(1, NH * HP)
    kw.update(_prep_late_weights(late_weights(oattn) if late_weights is not None else wf))
    (yn, h2, dx1, r, da, dff, do, dyssm, dob, drow, accs2, accg2, accg3) = _p2(
        yssm, oattn, xf, tf, modp, gs, ga, kw["w_out"], g2, gf, kw["w_ff1"], kw["w_ff2"], S, tm)
    loss = accg2[2:3]
    g_ff1 = _wgrad(h2, da, "wgrad_ff1", col_slots=4)
    g_ff2 = _wgrad(r, dff, "wgrad_ff2").reshape(4, D_FF // 4, D)
    gwo = _wgrad(yn, do, "wgrad_out")
    g_out = jnp.concatenate([gwo[:D_SSM].reshape(2, D_SSM // 2, D),
                             gwo[D_SSM:].reshape(2, NH // 2 * HP, D).reshape(2, NH // 2, HP, D)[:, :, :V_HEAD]
                             .reshape(2, D_ATTN // 2, D)], axis=0)
    lre8_b = lre8
    if reducer is not None:
        drow = drow + reducer.start([g_ff1, g_ff2, g_out])[0, 0]

    dq, dk, dv = _attn_bwd(q, k, v, dob, lrow, drow, S, tq)
    if reducer is not None:
        lre8_b = lre8 + reducer.middle(dq)[0, 0]

    dys_p = _permute_rows(dyssm, S)
    dy, dz, air, aii = _ssm_bwd_a(dys_p, z, ypre, kw["w_glu"], cm, lre8_b, lim8, S, tt)
    du_p, dcm, dbm, dd, dlr, dli = _ssm_bwd_b(dy, u_p, st, fcr, fci, air, aii, bm, cm, dvec, lre8, lim8, S, tt)
    du = _unpermute_rows(du_p, S)
    dcm = dcm.reshape(2, 4, 8, P, 8, H)
    dc_re = jnp.einsum("qgpgh->qghp", dcm[0]).reshape(G, H, P)
    dc_im = -jnp.einsum("qgpgh->qghp", dcm[1]).reshape(G, H, P)
    dbm = dbm.reshape(8, H, 2, 4, 8, P)
    dbb_re = jnp.einsum("ghqgp->qgph", dbm[:, :, 0]).reshape(NST, H)
    dbb_im = jnp.einsum("ghqgp->qgph", dbm[:, :, 1]).reshape(NST, H)
    gb_re, gb_im, glr, gli, gdt = _ssm_param_bwd(lam_re, lam_im, logdt, b_re, b_im, dlr.reshape(NST, 1),
                                                 dli.reshape(NST, 1), dbb_re, dbb_im)
    glogdt = _rowsum(gdt.reshape(G, P))

    dx, dproj, dqb, dkvb, accs1, accg1, accm = _f1_bwd(du, dq, dk, dv, lat, rc, rs1, rs2, gq, gkv, kw["w_uq"],
                                                       kw["w_ukv"], dx1, xf, modp, g1, kw["w_in"], S, tr)

    big = {}
    big["w_in"] = _slots(_wgrad(h1, dproj, "wgrad_in")[:, :IN_COLS])
    big["w_glu"] = _wgrad(gact, dz, "wgrad_glu", col_slots=4)
    big["w_uq"] = _slots(_unpad_heads_cols(_wgrad(qn, dqb, "wgrad_uq"), QK_NOPE + QK_ROPE).reshape(Q_LORA, -1))
    gkvw = _wgrad(kvn, dkvb, "wgrad_ukv")
    big["w_ukv"] = _slots(jnp.concatenate([_unpad_heads_cols(gkvw[:, :NH * HP], QK_NOPE),
                                           _unpad_heads_cols(gkvw[:, NH * HP:], V_HEAD)], axis=2).reshape(KV_LORA, -1))
    big["w_out"] = g_out
    big["w_ff1"] = g_ff1
    big["w_ff2"] = g_ff2

    small = {}
    small["norm1_g"] = accg1[0:1]
    small["norm2_g"] = accg2[0:1]
    small["final_norm_g"] = accg2[1:2]
    small["ssm_out_g"] = accg3[0:1, :D_SSM]
    small["attn_out_g"] = accg3[1].reshape(NH, HP)[:, :V_HEAD].reshape(1, D_ATTN)
    small["q_norm_g"] = accm[0:1, :Q_LORA]
    small["kv_norm_g"] = accm[1:2, :KV_LORA]
    small["ssm_lambda_re"] = glr.reshape(G, P)
    small["ssm_lambda_im"] = gli.reshape(G, P)
    small["ssm_b_re"] = gb_re
    small["ssm_b_im"] = gb_im
    small["ssm_c_re"] = dc_re.reshape(G * H, P)
    small["ssm_c_im"] = dc_im.reshape(G * H, P)
    small["ssm_d"] = dd.reshape(G, H)
    small["ssm_log_dt"] = glogdt.reshape(1, G)
    return loss, dx.reshape(nb, S, D), big, small, accs1 + accs2


def _view2d(a):
    return a.reshape(-1, a.shape[-1]) if a.ndim > 1 else a.reshape(1, -1)


def kernel(x, c, positions, ada_w, ada_b, norm1_g, w_in, ssm_lambda_re, ssm_lambda_im, ssm_b_re, ssm_b_im, ssm_c_re, ssm_c_im, ssm_d, ssm_log_dt, w_glu, q_norm_g, w_uq, kv_norm_g, w_ukv, ssm_out_g, attn_out_g, w_out, norm2_g, w_ff1, w_ff2, final_ada_w, final_ada_b, final_norm_g, loss_target, m_ada_w, m_ada_b, m_norm1_g, m_w_in, m_ssm_lambda_re, m_ssm_lambda_im, m_ssm_b_re, m_ssm_b_im, m_ssm_c_re, m_ssm_c_im, m_ssm_d, m_ssm_log_dt, m_w_glu, m_q_norm_g, m_w_uq, m_kv_norm_g, m_w_ukv, m_ssm_out_g, m_attn_out_g, m_w_out, m_norm2_g, m_w_ff1, m_w_ff2, m_final_ada_w, m_final_ada_b, m_final_norm_g, v_ada_w, v_ada_b, v_norm1_g, v_w_in, v_ssm_lambda_re, v_ssm_lambda_im, v_ssm_b_re, v_ssm_b_im, v_ssm_c_re, v_ssm_c_im, v_ssm_d, v_ssm_log_dt, v_w_glu, v_q_norm_g, v_w_uq, v_kv_norm_g, v_w_ukv, v_ssm_out_g, v_attn_out_g, v_w_out, v_norm2_g, v_w_ff1, v_w_ff2, v_final_ada_w, v_final_ada_b, v_final_norm_g):
    args = dict(locals())
    names = list(inspect.signature(kernel).parameters)
    wnames = names[3:names.index("loss_target")]
    small_names = [nm for nm in wnames if nm not in GATHERED and nm not in TP]
    reduced_names = [nm for nm in small_names if nm not in ("ada_b", "final_ada_b")]
    w = {nm: args[nm] for nm in wnames}
    m = {nm: args["m_" + nm] for nm in wnames}
    v = {nm: args["v_" + nm] for nm in wnames}
    nb = x.shape[0]
    xi, yi, ci = lax.axis_index("x"), lax.axis_index("y"), lax.axis_index("c")
    chip, me = 2 * xi + yi, 4 * xi + 2 * yi + ci

    unslot = lambda nm, g: g.reshape(-1, g.shape[-1]) if nm in ROW_SHARDED else _unslots(g)
    early = [nm for nm in GATHERED if nm not in LATE]
    got = _gather_chips("gather_weights", [_view2d(w[nm]).astype(BF16) for nm in early], [c])
    wf = {nm: unslot(nm, g) for nm, g in zip(early, got)}
    for nm in small_names:
        wf[nm] = w[nm][0] if w[nm].ndim > 1 else w[nm]
    c_all = got[len(early)].reshape(8 * nb, D)

    na, nf = ada_w.shape[-1], final_ada_w.shape[-1]
    ada_b_s = lax.dynamic_slice(ada_b, (0, chip * na), (1, na))
    fada_b_s = lax.dynamic_slice(final_ada_b.reshape(1, -1), (0, chip * nf), (1, nf))
    cond_all, modcols = _mod_fwd(c_all, ada_w[0], ada_b_s, final_ada_w, fada_b_s)
    (mod_g,) = _gather_chips("gather_mod", [modcols])
    mine = lax.dynamic_slice(mod_g, (0, me * nb, 0), (4, nb, na + nf))
    modp = jnp.concatenate([mine[:, :, :na].transpose(1, 0, 2).reshape(nb, 6, D),
                            mine[:, :, na:].transpose(1, 0, 2).reshape(nb, 2, D)], axis=1)

    own_late = [_view2d(w[nm]).astype(BF16) for nm in LATE]
    late_gather, token = _split_start("gather_late", own_late,
                                      [jax.ShapeDtypeStruct((4,) + a.shape, a.dtype) for a in own_late],
                                      3 * len(LATE), _plan_to_chips, modp)
    modp = modp + token[0, 0]

    def late_weights(after):
        sent, landed = _split_wait(late_gather, after)
        return {nm: unslot(nm, lax.dynamic_update_slice(g, own[None], (chip, 0, 0)))
                for nm, g, own in zip(LATE, landed, sent)}

    cidx = ci.astype(jnp.int32).reshape(1)
    ahead = ["w_ff1", "w_ff2", "w_out"]

    class Reducer:
        def start(self, gs):
            lands = [jax.ShapeDtypeStruct((4, g.shape[1] // 2, g.shape[2]), g.dtype) for g in gs]
            self.swap, tok = _split_start("grad_swap_ff", gs, lands, len(gs), _plan_swap_halves, modp)
            return tok

        def middle(self, after):
            gs, got = _split_wait(self.swap, after)
            sums = [_add_half(g, r, cidx, "grad_add_sibling_" + nm) for nm, g, r in zip(ahead, gs, got)]
            lands = [jax.ShapeDtypeStruct(s.shape, s.dtype) for s in sums]
            self.scatter, tok = _split_start("grad_scatter_ff", sums, lands, 3 * len(sums), _plan_scatter_chips, modp)
            return tok

        def finish(self, after):
            out = []
            for nm, s, l in zip(ahead, *_split_wait(self.scatter, after)):
                own = lax.dynamic_slice(s, (chip, 0, 0), (1,) + s.shape[1:])
                out.append(_add_chips(lax.dynamic_update_slice(l, own, (chip, 0, 0)), "grad_add_chips_" + nm))
            return out

    reducer = Reducer()
    loss_row, grad_x, big, small, dmodp = _local_step(x, positions, loss_target, modp, wf, late_weights, reducer)

    rest = [nm for nm in GATHERED if nm not in ahead]
    sizes = [small[nm].size for nm in reduced_names]
    pad = -sum(sizes) % 128
    packed = jnp.concatenate([small[nm].reshape(1, -1) for nm in reduced_names] + [jnp.zeros((1, pad), F32)],
                             axis=1).astype(BF16)
    swapped = _swap_halves([big[nm] for nm in rest], [dmodp.reshape(nb, 8 * D)], [packed, loss_row])
    chip_sums = [_add_half(big[nm], r, cidx, "grad_add_sibling_" + nm) for nm, r in zip(rest, swapped)]
    chip_small = _pair_sum(packed, swapped[len(rest) + 1], loss_row, swapped[len(rest) + 2])
    scattered = _scatter_chips(chip_sums, chip_small)
    half_of = {nm: _add_chips(r, "grad_add_chips_" + nm) for nm, r in zip(rest, scattered)}
    half_of.update(zip(ahead, reducer.finish(grad_x)))
    halves = [half_of[nm] for nm in GATHERED]
    others = _join_halves(halves)
    grads = {}
    dmod_all = swapped[len(rest)].reshape(8 * nb, 8 * D)
    small_sum, loss_sum = _sum_devices(scattered[len(rest)].reshape(4, -1), scattered[len(rest) + 1].reshape(4, -1))
    loss = jnp.sum(loss_sum)
    off = 0
    for nm, sz in zip(reduced_names, sizes):
        grads[nm] = small_sum[:, off:off + sz].reshape(small[nm].shape)
        off += sz

    dsl = jnp.concatenate([lax.dynamic_slice(dmod_all, (0, chip * na), (8 * nb, na)),
                           lax.dynamic_slice(dmod_all, (0, 6 * D + chip * nf), (8 * nb, nf))], axis=1)
    gw, gb = _mod_bwd(cond_all.T, dsl, dmod_all)
    grads["ada_w"], grads["final_ada_w"] = gw[:, :na], gw[:, na:]
    grads["ada_b"], grads["final_ada_b"] = gb[:, :6 * D], gb[:, 6 * D:]

    delta, new_m, new_v = {}, {}, {}
    for nm, mine_h, other_h in zip(GATHERED, halves, others):
        grads[nm], delta[nm], new_m[nm], new_v[nm] = _adamw_halves(
            _view2d(w[nm]), mine_h, other_h, _view2d(m[nm]), _view2d(v[nm]), cidx, "adamw_" + nm)
    for nm in TP:
        delta[nm], new_m[nm], new_v[nm] = _adamw(_view2d(w[nm]), grads[nm], _view2d(m[nm]), _view2d(v[nm]),
                                                  "adamw_" + nm)
    upd = _adamw_small([_view2d(w[nm]) for nm in small_names], [grads[nm] for nm in small_names],
                       [_view2d(m[nm]) for nm in small_names], [_view2d(v[nm]) for nm in small_names])
    k = len(small_names)
    for t, nm in enumerate(small_names):
        delta[nm], new_m[nm], new_v[nm] = upd[t], upd[k + t], upd[2 * k + t]

    outs = [grads, delta, new_m, new_v]
    return (loss, grad_x, *[d[nm].reshape(w[nm].shape) for d in outs for nm in wnames])
```

```python
import inspect
import math

import jax
import jax.numpy as jnp
from jax import lax
from jax.experimental import pallas as pl
from jax.experimental.pallas import tpu as pltpu

F32 = jnp.float32
BF16 = jnp.bfloat16

D = 1024
D_SSM = 512
G = 32
H = 16
P = 64
NST = G * P
D_ATTN = 512
NH = 8
QK_NOPE = 64
QK_ROPE = 32
V_HEAD = 64
HP = 128
Q_LORA = 384
KV_LORA = 256
IN_COLS = D_SSM + Q_LORA + KV_LORA + QK_ROPE
IN_PAD = 1280
D_FF = 4096
ROPE_BASE = 10000.0
EPS = 1e-6
ADAM_LR = 0.001
ADAM_B1 = 0.9
ADAM_B2 = 0.999
ADAM_EPS = 1e-08
ADAM_WD = 0.01
ADAM_STEP = 10
NEG = -1e30
VMEM_LIMIT = 60 << 20

MESH = pl.DeviceIdType.MESH
_VM = pl.BlockSpec(memory_space=pltpu.VMEM)
_ANY = pl.BlockSpec(memory_space=pl.ANY)

GATHERED = ["w_in", "w_glu", "w_uq", "w_ukv", "w_out", "w_ff1", "w_ff2"]
TP = ["ada_w", "final_ada_w"]
ROW_SHARDED = ("w_out", "w_ff2")
LATE = ["w_out", "w_ff1", "w_ff2"]


def _cp(sem=None, vmem=VMEM_LIMIT):
    kw = dict(vmem_limit_bytes=vmem)
    if sem is not None:
        kw["dimension_semantics"] = sem
    return pltpu.CompilerParams(**kw)


def _dot(a, b):
    return jnp.dot(a, b, preferred_element_type=F32)


def _dot_nt(a, b):
    return lax.dot_general(a, b, (((1,), (1,)), ((), ())), preferred_element_type=F32)


def _dot_tn(a, b):
    return lax.dot_general(a, b, (((0,), (0,)), ((), ())), preferred_element_type=F32)


def _rms(x, n):
    r = lax.rsqrt(jnp.sum(x * x, axis=-1, keepdims=True) * (1.0 / n) + EPS)
    return x * r, r


def _rms_bwd(dyg, xhat, r, n):
    return r * (dyg - xhat * (jnp.sum(dyg * xhat, axis=-1, keepdims=True) * (1.0 / n)))


def _sigmoid(x):
    return 1.0 / (1.0 + jnp.exp(-x))


_GK = math.sqrt(2.0 / math.pi)
_GC = 0.044715


def _gelu(y):
    t = jnp.tanh(_GK * (y + _GC * y * y * y))
    return 0.5 * y * (1.0 + t)


def _gelu_grad(y):
    t = jnp.tanh(_GK * (y + _GC * y * y * y))
    return 0.5 * (1.0 + t) + 0.5 * y * (1.0 - t * t) * _GK * (1.0 + 3.0 * _GC * y * y)


def _colsum(x):
    return jnp.sum(x, axis=0, keepdims=True)


def _roll(x, s):
    return pltpu.roll(x, s % x.shape[-1], x.ndim - 1)


def _mod_fwd(c_all, ada_w_s, ada_b_s, fada_w_s, fada_b_s):
    nseq = c_all.shape[0]
    na, nf = ada_w_s.shape[1], fada_w_s.shape[1]

    def body(c_ref, w_ref, b_ref, fw_ref, fb_ref, cond_ref, mod_ref):
        cv = c_ref[...]
        cond = cv * _sigmoid(cv)
        cond_ref[...] = cond
        cb = cond.astype(BF16)
        mod_ref[:, 0:na] = _dot(cb, w_ref[...].astype(BF16)) + b_ref[...]
        mod_ref[:, na:na + nf] = _dot(cb, fw_ref[...].astype(BF16)) + fb_ref[...]

    return pl.pallas_call(
        body, name="mod_fwd",
        out_shape=[jax.ShapeDtypeStruct((nseq, D), F32), jax.ShapeDtypeStruct((nseq, na + nf), F32)],
        in_specs=[_VM] * 5, out_specs=[_VM] * 2, compiler_params=_cp(),
    )(c_all, ada_w_s, ada_b_s, fada_w_s, fada_b_s)


def _mod_bwd(cond_t, dsl, dall):
    nseq, n = dsl.shape
    bc = 512

    def body(ct_ref, dm_ref, da_ref, gw_ref, gb_ref):
        ct = ct_ref[...]
        dm = dm_ref[...]
        acc = ct[:, 0:1] * dm[0:1, :]
        for b in range(1, nseq):
            acc = acc + ct[:, b:b + 1] * dm[b:b + 1, :]
        gw_ref[...] = acc

        @pl.when(pl.program_id(0) == 0)
        def _():
            gb_ref[...] = _colsum(da_ref[...])

    return pl.pallas_call(
        body, name="mod_bwd", grid=(n // bc,),
        out_shape=[jax.ShapeDtypeStruct((D, n), F32), jax.ShapeDtypeStruct((1, dall.shape[1]), F32)],
        in_specs=[_VM, pl.BlockSpec((nseq, bc), lambda i: (0, i)), _VM],
        out_specs=[pl.BlockSpec((D, bc), lambda i: (0, i)), pl.BlockSpec((1, dall.shape[1]), lambda i: (0, 0))],
        compiler_params=_cp(("arbitrary",)),
    )(cond_t, dsl, dall)


def _f1_fwd(x, modp, g1, w_in, rc, rs1, rs2, gq, gkv, w_uq, w_ukv, S, tm):
    n = x.shape[0]
    tps = S // tm
    LAT = IN_PAD - D_SSM

    def body(x_ref, mod_ref, g_ref, w_ref, c_ref, s1_ref, s2_ref, gq_ref, gkv_ref, wq_ref, wkv_ref,
             h_ref, u_ref, lat_ref, q_ref, k_ref, v_ref, qn_ref, kvn_ref):
        xhat, _ = _rms(x_ref[...], D)
        h = (xhat * g_ref[...]) * (1.0 + mod_ref[0, 1:2, :]) + mod_ref[0, 0:1, :]
        hb = h.astype(BF16)
        h_ref[...] = hb
        proj = _dot(hb, w_ref[...])
        u_ref[...] = proj[:, 0:D_SSM]
        lat_ref[...] = proj[:, D_SSM:IN_PAD]
        c, s1, s2 = c_ref[...], s1_ref[...], s2_ref[...]
        qhat, _ = _rms(proj[:, D_SSM:D_SSM + Q_LORA], Q_LORA)
        qn = (qhat * gq_ref[...]).astype(BF16)
        qn_ref[...] = qn
        q = _dot(qn, wq_ref[...])
        qr = _rope(q, jnp.tile(c, (1, NH)), jnp.tile(s1, (1, NH)), jnp.tile(s2, (1, NH)))
        q_ref[...] = (qr * _C2).astype(BF16)
        khat, _ = _rms(proj[:, D_SSM + Q_LORA:D_SSM + Q_LORA + KV_LORA], KV_LORA)
        kvn = (khat * gkv_ref[...]).astype(BF16)
        kvn_ref[...] = kvn
        kv = _dot(kvn, wkv_ref[...])
        kr = _rope(_roll(proj[:, IN_PAD - HP:IN_PAD], 64), c, s1, s2)
        k_ref[...] = (kv[:, 0:NH * HP] + jnp.tile(kr, (1, NH))).astype(BF16)
        vv = kv[:, NH * HP:2 * NH * HP]
        lane = lax.broadcasted_iota(jnp.int32, vv.shape, 1)
        v_ref[...] = jnp.where(lane % HP == V_HEAD, 1.0, vv).astype(BF16)

    row = lambda w: pl.BlockSpec((tm, w), lambda i: (i, 0))
    return pl.pallas_call(
        body, name="f1_fwd", grid=(n // tm,),
        out_shape=[jax.ShapeDtypeStruct((n, D), BF16), jax.ShapeDtypeStruct((n, D_SSM), F32),
                   jax.ShapeDtypeStruct((n, LAT), F32)] + [jax.ShapeDtypeStruct((n, NH * HP), BF16)] * 3 +
                  [jax.ShapeDtypeStruct((n, Q_LORA), BF16), jax.ShapeDtypeStruct((n, KV_LORA), BF16)],
        in_specs=[row(D), pl.BlockSpec((1, 8, D), lambda i: (i // tps, 0, 0)), _VM, _VM,
                  row(HP), row(HP), row(HP), _VM, _VM, _VM, _VM],
        out_specs=[row(D), row(D_SSM), row(LAT)] + [row(NH * HP)] * 3 + [row(Q_LORA), row(KV_LORA)],
        compiler_params=_cp(("parallel",)),
    )(x, modp, g1, w_in, rc, rs1, rs2, gq, gkv, w_uq, w_ukv)


def _f1_bwd(du, dq, dk, dv, lat, rc, rs1, rs2, gq, gkv, w_uq, w_ukv, dx1, x, modp, g1, w_in, S, tm):
    n = x.shape[0]
    tps = S // tm
    nb = n // S

    def body(du_ref, dq_ref, dk_ref, dv_ref, lat_ref, c_ref, s1_ref, s2_ref, gq_ref, gkv_ref, wq_ref, wkv_ref,
             dx1_ref, x_ref, mod_ref, g_ref, w_ref,
             dx_ref, dproj_ref, dqb_ref, dkvb_ref, accs_ref, accg_ref, accm_ref):
        i = pl.program_id(0)
        c, s1, s2 = c_ref[...], s1_ref[...], s2_ref[...]
        dqu = _rope_t(dq_ref[...] * _SCALE, jnp.tile(c, (1, NH)), jnp.tile(s1, (1, NH)),
                      jnp.tile(s2, (1, NH))).astype(BF16)
        dqb_ref[...] = dqu
        dqn = _dot_nt(dqu, wq_ref[...])
        qhat, rq = _rms(lat_ref[:, 0:Q_LORA], Q_LORA)
        dql = _rms_bwd(dqn * gq_ref[...], qhat, rq, Q_LORA)
        dkf = dk_ref[...] * (1.0 / _LOG2E)
        dkv = jnp.concatenate([dkf.astype(BF16), dv_ref[...].astype(BF16)], axis=1)
        dkvb_ref[...] = dkv
        dkvn = _dot_nt(dkv, wkv_ref[...])
        khat, rk = _rms(lat_ref[:, Q_LORA:Q_LORA + KV_LORA], KV_LORA)
        dkvl = _rms_bwd(dkvn * gkv_ref[...], khat, rk, KV_LORA)
        dkr = dkf[:, 0:HP]
        for h in range(1, NH):
            dkr = dkr + dkf[:, h * HP:(h + 1) * HP]
        lane = lax.broadcasted_iota(jnp.int32, dkr.shape, 1)
        dkr = jnp.where((lane >= QK_NOPE) & (lane < QK_NOPE + QK_ROPE), dkr, 0.0)
        dkr = _roll(_rope_t(dkr, c, s1, s2), -64)

        @pl.when(i == 0)
        def _():
            accm_ref[...] = jnp.zeros_like(accm_ref)

        accm_ref[0:1, 0:Q_LORA] += _colsum(dqn * qhat)
        accm_ref[1:2, 0:KV_LORA] += _colsum(dkvn * khat)

        dproj = jnp.concatenate([du_ref[...], dql, dkvl, dkr], axis=1).astype(BF16)
        dproj_ref[...] = dproj
        dh = _dot_nt(dproj, w_ref[...])
        xhat, r = _rms(x_ref[...], D)
        g = g_ref[...]
        dn = dh * (1.0 + mod_ref[0, 1:2, :])
        dx_ref[...] = dx1_ref[...] + _rms_bwd(dn * g, xhat, r, D)

        @pl.when(i % tps == 0)
        def _():
            accs_ref[...] = jnp.zeros_like(accs_ref)

        @pl.when(i == 0)
        def _():
            accg_ref[...] = jnp.zeros_like(accg_ref)

        accs_ref[0, 0:1, :] += _colsum(dh)
        accs_ref[0, 1:2, :] += _colsum(dh * (xhat * g))
        accg_ref[0:1, :] += _colsum(dn * xhat)

    row = lambda w: pl.BlockSpec((tm, w), lambda i: (i, 0))
    return pl.pallas_call(
        body, name="f1_bwd", grid=(n // tm,),
        out_shape=[jax.ShapeDtypeStruct((n, D), F32), jax.ShapeDtypeStruct((n, IN_PAD), BF16),
                   jax.ShapeDtypeStruct((n, NH * HP), BF16), jax.ShapeDtypeStruct((n, 2 * NH * HP), BF16),
                   jax.ShapeDtypeStruct((nb, 8, D), F32), jax.ShapeDtypeStruct((8, D), F32),
                   jax.ShapeDtypeStruct((8, Q_LORA), F32)],
        in_specs=[row(D_SSM)] + [row(NH * HP)] * 3 + [row(IN_PAD - D_SSM), row(HP), row(HP), row(HP),
                                                     _VM, _VM, _VM, _VM, row(D), row(D),
                                                     pl.BlockSpec((1, 8, D), lambda i: (i // tps, 0, 0)), _VM, _VM],
        out_specs=[row(D), row(IN_PAD), row(NH * HP), row(2 * NH * HP),
                   pl.BlockSpec((1, 8, D), lambda i: (i // tps, 0, 0)), pl.BlockSpec((8, D), lambda i: (0, 0)),
                   pl.BlockSpec((8, Q_LORA), lambda i: (0, 0))],
        compiler_params=_cp(("arbitrary",)),
    )(du, dq, dk, dv, lat, rc, rs1, rs2, gq, gkv, w_uq, w_ukv, dx1, x, modp, g1, w_in)


def _ssm_param_fwd(lam_re, lam_im, logdt, b_re, b_im):
    def body(lr_ref, li_ref, ld_ref, br_ref, bi_ref, lbr_ref, lbi_ref, bbr_ref, bbi_ref):
        lr, li = lr_ref[...], li_ref[...]
        dt = jnp.exp(ld_ref[...])
        er = jnp.exp(lr * dt)
        lbr = er * jnp.cos(li * dt)
        lbi = er * jnp.sin(li * dt)
        den = 1.0 / (lr * lr + li * li)
        cr = ((lbr - 1.0) * lr + lbi * li) * den
        ci = (lbi * lr - (lbr - 1.0) * li) * den
        lbr_ref[...] = lbr
        lbi_ref[...] = lbi
        bbr_ref[...] = cr * br_ref[...] - ci * bi_ref[...]
        bbi_ref[...] = cr * bi_ref[...] + ci * br_ref[...]

    return pl.pallas_call(
        body, name="ssm_param_fwd",
        out_shape=[jax.ShapeDtypeStruct((NST, 1), F32)] * 2 + [jax.ShapeDtypeStruct((NST, H), F32)] * 2,
        in_specs=[_VM] * 5, out_specs=[_VM] * 4, compiler_params=_cp(),
    )(lam_re, lam_im, logdt, b_re, b_im)


def _ssm_param_bwd(lam_re, lam_im, logdt, b_re, b_im, dlb_re, dlb_im, dbb_re, dbb_im):
    def body(lr_ref, li_ref, ld_ref, br_ref, bi_ref, dlr_ref, dli_ref, dbr_ref, dbi_ref,
             gbr_ref, gbi_ref, glr_ref, gli_ref, gdt_ref):
        lr, li = lr_ref[...], li_ref[...]
        dt = jnp.exp(ld_ref[...])
        er = jnp.exp(lr * dt)
        lbr = er * jnp.cos(li * dt)
        lbi = er * jnp.sin(li * dt)
        den = 1.0 / (lr * lr + li * li)
        nr, ni = lbr - 1.0, lbi
        cr = (nr * lr + ni * li) * den
        ci = (ni * lr - nr * li) * den
        br, bi = br_ref[...], bi_ref[...]
        dbr, dbi = dbr_ref[...], dbi_ref[...]
        gbr_ref[...] = cr * dbr + ci * dbi
        gbi_ref[...] = cr * dbi - ci * dbr
        gcr = jnp.sum(dbr * br + dbi * bi, axis=1, keepdims=True)
        gci = jnp.sum(dbi * br - dbr * bi, axis=1, keepdims=True)
        ilr, ili = lr * den, -li * den
        glbr = dlr_ref[...] + (gcr * ilr + gci * ili)
        glbi = dli_ref[...] + (gci * ilr - gcr * ili)
        qr = -(cr * ilr - ci * ili)
        qi = -(cr * ili + ci * ilr)
        glr = gcr * qr + gci * qi
        gli = gci * qr - gcr * qi
        glr = glr + dt * (glbr * lbr + glbi * lbi)
        gli = gli + dt * (glbi * lbr - glbr * lbi)
        wr = lr * lbr - li * lbi
        wi = lr * lbi + li * lbr
        glr_ref[...] = glr
        gli_ref[...] = gli
        gdt_ref[...] = (glbr * wr + glbi * wi) * dt

    return pl.pallas_call(
        body, name="ssm_param_bwd",
        out_shape=[jax.ShapeDtypeStruct((NST, H), F32)] * 2 + [jax.ShapeDtypeStruct((NST, 1), F32)] * 3,
        in_specs=[_VM] * 9, out_specs=[_VM] * 5, compiler_params=_cp(),
    )(lam_re, lam_im, logdt, b_re, b_im, dlb_re, dlb_im, dbb_re, dbb_im)


def _rowsum(a):
    def body(a_ref, o_ref):
        o_ref[...] = jnp.sum(a_ref[...], axis=1, keepdims=True)

    return pl.pallas_call(
        body, name="rowsum", out_shape=jax.ShapeDtypeStruct((a.shape[0], 1), F32),
        in_specs=[_VM], out_specs=_VM, compiler_params=_cp(),
    )(a)


QB = D_SSM // 4
QS = 4 * QB


def _bd_lo(part, q):
    return part * NST + q * QS


def _bd_expand(ub, bm_ref, out_ref):
    for part in range(2):
        for q in range(4):
            lo = _bd_lo(part, q)
            out_ref[:, lo:lo + QS] = _dot(ub[:, q * QB:(q + 1) * QB], bm_ref[:, lo:lo + QS])


def _bd_expand_t(db, cm_ref, out_ref):
    for part in range(2):
        for q in range(4):
            lo = _bd_lo(part, q)
            out_ref[:, lo:lo + QS] = _dot_nt(db[:, q * QB:(q + 1) * QB], cm_ref[lo:lo + QS, :])


def _bd_project(sb, cm_ref):
    return jnp.concatenate(
        [_dot(sb[:, _bd_lo(0, q):_bd_lo(0, q) + QS], cm_ref[_bd_lo(0, q):_bd_lo(0, q) + QS, :])
         + _dot(sb[:, _bd_lo(1, q):_bd_lo(1, q) + QS], cm_ref[_bd_lo(1, q):_bd_lo(1, q) + QS, :])
         for q in range(4)], axis=1)


def _bd_project_t(ab, bm_ref):
    return jnp.concatenate(
        [_dot_nt(ab[:, _bd_lo(0, q):_bd_lo(0, q) + QS], bm_ref[:, _bd_lo(0, q):_bd_lo(0, q) + QS])
         + _dot_nt(ab[:, _bd_lo(1, q):_bd_lo(1, q) + QS], bm_ref[:, _bd_lo(1, q):_bd_lo(1, q) + QS])
         for q in range(4)], axis=1)


def _pow2k(pr, pi, nsq):
    for _ in range(nsq):
        pr, pi = pr * pr - pi * pi, 2.0 * pr * pi
    return pr, pi


def _ssm_local(u_p, bm, lre8, lim8, S, tt):
    n = u_p.shape[0]
    nb, nt = n // S, S // tt
    nsq = int(round(math.log2(S // 8)))
    assert 2 ** nsq == S // 8

    def body(u_ref, bm_ref, lre_ref, lim_ref, cre_ref, cim_ref, sre, sim, bu):
        j = pl.program_id(1)

        @pl.when(j == 0)
        def _():
            sre[...] = jnp.zeros_like(sre)
            sim[...] = jnp.zeros_like(sim)

        _bd_expand(u_ref[...].astype(BF16), bm_ref, bu)
        lre, lim = lre_ref[...], lim_ref[...]

        def step(i, c):
            sr, si = c
            off = pl.multiple_of(i * 8, 8)
            br = bu[pl.ds(off, 8), 0:NST]
            bi = bu[pl.ds(off, 8), NST:2 * NST]
            return lre * sr - lim * si + br, lre * si + lim * sr + bi

        sr, si = lax.fori_loop(0, tt // 8, step, (sre[...], sim[...]))
        sre[...] = sr
        sim[...] = si

        @pl.when(j == nt - 1)
        def _():
            pr, pi = _pow2k(lre[0:1], lim[0:1], nsq)
            cr = jnp.zeros((1, NST), F32)
            ci = jnp.zeros((1, NST), F32)
            cre_ref[0:1, :] = cr
            cim_ref[0:1, :] = ci
            for k in range(1, 8):
                cr, ci = sr[k - 1:k] + pr * cr - pi * ci, si[k - 1:k] + pr * ci + pi * cr
                cre_ref[k:k + 1, :] = cr
                cim_ref[k:k + 1, :] = ci

    return pl.pallas_call(
        body, name="ssm_local", grid=(nb, nt),
        out_shape=[jax.ShapeDtypeStruct((nb * 8, NST), F32)] * 2,
        in_specs=[pl.BlockSpec((tt, D_SSM), lambda b, j: (b * nt + j, 0)), _VM, _VM, _VM],
        out_specs=[pl.BlockSpec((8, NST), lambda b, j: (b, 0))] * 2,
        scratch_shapes=[pltpu.VMEM((8, NST), F32), pltpu.VMEM((8, NST), F32), pltpu.VMEM((tt, 2 * NST), F32)],
        compiler_params=_cp(("arbitrary", "arbitrary")),
    )(u_p, bm, lre8, lim8)


def _ssm_fwd(u_p, cre, cim, bm, cm, dvec, w_glu, lre8, lim8, S, tt):
    n = u_p.shape[0]
    nb, nt = n // S, S // tt

    def body(u_ref, cre_ref, cim_ref, bm_ref, cm_ref, d_ref, wg_ref, lre_ref, lim_ref,
             st_ref, ypre_ref, z_ref, gact_ref, yssm_ref, sre, sim, bu):
        j = pl.program_id(1)

        @pl.when(j == 0)
        def _():
            sre[...] = cre_ref[...]
            sim[...] = cim_ref[...]

        u = u_ref[...]
        _bd_expand(u.astype(BF16), bm_ref, bu)
        lre, lim = lre_ref[...], lim_ref[...]

        def step(i, c):
            sr, si = c
            off = pl.multiple_of(i * 8, 8)
            nr = lre * sr - lim * si + bu[pl.ds(off, 8), 0:NST]
            ni = lre * si + lim * sr + bu[pl.ds(off, 8), NST:2 * NST]
            bu[pl.ds(off, 8), 0:NST] = nr
            bu[pl.ds(off, 8), NST:2 * NST] = ni
            return nr, ni

        sr, si = lax.fori_loop(0, tt // 8, step, (sre[...], sim[...]))
        sre[...] = sr
        sim[...] = si
        stb = bu[...].astype(BF16)
        st_ref[...] = stb
        y = _bd_project(stb, cm_ref) + d_ref[...] * u
        ypre_ref[...] = y
        gb = _gelu(y).astype(BF16)
        gact_ref[...] = gb
        z = _dot(gb, wg_ref[...])
        z_ref[...] = z
        yssm_ref[...] = z[:, 0:D_SSM] * _sigmoid(z[:, D_SSM:2 * D_SSM])

    row = lambda w: pl.BlockSpec((tt, w), lambda b, j: (b * nt + j, 0))
    return pl.pallas_call(
        body, name="ssm_fwd", grid=(nb, nt),
        out_shape=[jax.ShapeDtypeStruct((n, 2 * NST), BF16), jax.ShapeDtypeStruct((n, D_SSM), F32),
                   jax.ShapeDtypeStruct((n, 2 * D_SSM), F32), jax.ShapeDtypeStruct((n, D_SSM), BF16),
                   jax.ShapeDtypeStruct((n, D_SSM), F32)],
        in_specs=[row(D_SSM), pl.BlockSpec((8, NST), lambda b, j: (b, 0)), pl.BlockSpec((8, NST), lambda b, j: (b, 0)),
                  _VM, _VM, _VM, _VM, _VM, _VM],
        out_specs=[row(2 * NST), row(D_SSM), row(2 * D_SSM), row(D_SSM), row(D_SSM)],
        scratch_shapes=[pltpu.VMEM((8, NST), F32), pltpu.VMEM((8, NST), F32), pltpu.VMEM((tt, 2 * NST), F32)],
        compiler_params=_cp(("arbitrary", "arbitrary")),
    )(u_p, cre, cim, bm, cm, dvec, w_glu, lre8, lim8)


def _ssm_bwd_a(dys_p, z, ypre, w_glu, cm, lre8, lim8, S, tt):
    n = z.shape[0]
    nb, nt = n // S, S // tt
    nsq = int(round(math.log2(S // 8)))
    ng = tt // 8

    def body(dys_ref, z_ref, y_ref, wg_ref, cm_ref, lre_ref, lim_ref, dy_ref, dz_ref, are_ref, aim_ref, sre, sim, gb):
        j = pl.program_id(1)

        @pl.when(j == 0)
        def _():
            sre[...] = jnp.zeros_like(sre)
            sim[...] = jnp.zeros_like(sim)

        z = z_ref[...]
        z1, z2 = z[:, 0:D_SSM], z[:, D_SSM:2 * D_SSM]
        sg = _sigmoid(z2)
        dys = dys_ref[...]
        dz = jnp.concatenate([dys * sg, dys * z1 * sg * (1.0 - sg)], axis=1).astype(BF16)
        dz_ref[...] = dz
        dy = _dot_nt(dz, wg_ref[...]) * _gelu_grad(y_ref[...])
        dy_ref[...] = dy
        _bd_expand_t(dy.astype(BF16), cm_ref, gb)
        lre, lim = lre_ref[...], lim_ref[...]

        def step(i, c):
            ar, ai = c
            off = pl.multiple_of((ng - 1 - i) * 8, 8)
            gr = gb[pl.ds(off, 8), 0:NST]
            gi = gb[pl.ds(off, 8), NST:2 * NST]
            return lre * ar + lim * ai + gr, lre * ai - lim * ar + gi

        ar, ai = lax.fori_loop(0, ng, step, (sre[...], sim[...]))
        sre[...] = ar
        sim[...] = ai

        @pl.when(j == nt - 1)
        def _():
            pr, pi = _pow2k(lre[0:1], -lim[0:1], nsq)
            cr = jnp.zeros((1, NST), F32)
            ci = jnp.zeros((1, NST), F32)
            are_ref[7:8, :] = cr
            aim_ref[7:8, :] = ci
            for k in range(6, -1, -1):
                cr, ci = ar[k + 1:k + 2] + pr * cr - pi * ci, ai[k + 1:k + 2] + pr * ci + pi * cr
                are_ref[k:k + 1, :] = cr
                aim_ref[k:k + 1, :] = ci

    row = lambda w: pl.BlockSpec((tt, w), lambda b, j: (b * nt + nt - 1 - j, 0))
    return pl.pallas_call(
        body, name="ssm_bwd_a", grid=(nb, nt),
        out_shape=[jax.ShapeDtypeStruct((n, D_SSM), F32), jax.ShapeDtypeStruct((n, 2 * D_SSM), BF16),
                   jax.ShapeDtypeStruct((nb * 8, NST), F32), jax.ShapeDtypeStruct((nb * 8, NST), F32)],
        in_specs=[row(D_SSM), row(2 * D_SSM), row(D_SSM), _VM, _VM, _VM, _VM],
        out_specs=[row(D_SSM), row(2 * D_SSM), pl.BlockSpec((8, NST), lambda b, j: (b, 0)),
                   pl.BlockSpec((8, NST), lambda b, j: (b, 0))],
        scratch_shapes=[pltpu.VMEM((8, NST), F32), pltpu.VMEM((8, NST), F32), pltpu.VMEM((tt, 2 * NST), F32)],
        compiler_params=_cp(("arbitrary", "arbitrary")),
    )(dys_p, z, ypre, w_glu, cm, lre8, lim8)


def _ssm_bwd_b(dy, u_p, st, fcr, fci, air, aii, bm, cm, dvec, lre8, lim8, S, tt):
    n = u_p.shape[0]
    nb, nt = n // S, S // tt
    ng = tt // 8

    def body(dy_ref, u_ref, st_ref, stp_ref, fcr_ref, fci_ref, air_ref, aii_ref, bm_ref, cm_ref, d_ref, lre_ref, lim_ref,
             du_ref, dcm_ref, dbm_ref, dd_ref, dlr_ref, dli_ref, are, aim, accr, acci, sp, ab):
        b = pl.program_id(0)
        j = pl.program_id(1)
        jt = nt - 1 - j

        @pl.when((b == 0) & (j == 0))
        def _():
            dcm_ref[...] = jnp.zeros_like(dcm_ref)
            dbm_ref[...] = jnp.zeros_like(dbm_ref)
            dd_ref[...] = jnp.zeros_like(dd_ref)
            accr[...] = jnp.zeros_like(accr)
            acci[...] = jnp.zeros_like(acci)

        @pl.when(j == 0)
        def _():
            are[...] = air_ref[...]
            aim[...] = aii_ref[...]

        sp[8:tt + 8, :] = st_ref[...].astype(F32)

        @pl.when(jt == 0)
        def _():
            sp[0:8, 0:NST] = fcr_ref[...]
            sp[0:8, NST:2 * NST] = fci_ref[...]

        @pl.when(jt != 0)
        def _():
            sp[0:8, :] = stp_ref[8:16, :].astype(F32)

        dy = dy_ref[...]
        u = u_ref[...]
        dyb = dy.astype(BF16)
        _bd_expand_t(dyb, cm_ref, ab)
        lre, lim = lre_ref[...], lim_ref[...]

        def step(i, c):
            ar, ai = c
            off = pl.multiple_of((ng - 1 - i) * 8, 8)
            nr = lre * ar + lim * ai + ab[pl.ds(off, 8), 0:NST]
            ni = lre * ai - lim * ar + ab[pl.ds(off, 8), NST:2 * NST]
            ab[pl.ds(off, 8), 0:NST] = nr
            ab[pl.ds(off, 8), NST:2 * NST] = ni
            pr = sp[pl.ds(off, 8), 0:NST]
            pi = sp[pl.ds(off, 8), NST:2 * NST]
            accr[...] += nr * pr + ni * pi
            acci[...] += ni * pr - nr * pi
            return nr, ni

        ar, ai = lax.fori_loop(0, ng, step, (are[...], aim[...]))
        are[...] = ar
        aim[...] = ai
        a_b = ab[...].astype(BF16)
        du_ref[...] = _bd_project_t(a_b, bm_ref) + d_ref[...] * dy
        ub = u.astype(BF16)
        for q in range(4):
            for part in range(2):
                lo = part * NST + q * 4 * QB
                s_q = st_ref[:, lo:lo + 4 * QB]
                dcm_ref[lo:lo + 4 * QB, :] += _dot_tn(s_q, dyb[:, q * QB:(q + 1) * QB])
                dbm_ref[:, lo:lo + 4 * QB] += _dot_tn(ub[:, q * QB:(q + 1) * QB], a_b[:, lo:lo + 4 * QB])
        dd_ref[...] += _colsum(dy * u)

        @pl.when((b == nb - 1) & (j == nt - 1))
        def _():
            dlr_ref[...] = _colsum(accr[...])
            dli_ref[...] = _colsum(acci[...])

    row = lambda w: pl.BlockSpec((tt, w), lambda b, j: (b * nt + nt - 1 - j, 0))
    seq8 = pl.BlockSpec((8, NST), lambda b, j: (b, 0))
    prev = pl.BlockSpec((16, 2 * NST), lambda b, j: (jnp.maximum((b * nt + nt - 1 - j) * (tt // 16) - 1, 0), 0))
    const = lambda shape: pl.BlockSpec(shape, lambda b, j: (0, 0))
    return pl.pallas_call(
        body, name="ssm_bwd_b", grid=(nb, nt),
        out_shape=[jax.ShapeDtypeStruct((n, D_SSM), F32), jax.ShapeDtypeStruct((2 * NST, QB), F32),
                   jax.ShapeDtypeStruct((QB, 2 * NST), F32), jax.ShapeDtypeStruct((1, D_SSM), F32),
                   jax.ShapeDtypeStruct((1, NST), F32), jax.ShapeDtypeStruct((1, NST), F32)],
        in_specs=[row(D_SSM), row(D_SSM), row(2 * NST), prev, seq8, seq8, seq8, seq8, _VM, _VM, _VM, _VM, _VM],
        out_specs=[row(D_SSM), const((2 * NST, QB)), const((QB, 2 * NST)), const((1, D_SSM)),
                   const((1, NST)), const((1, NST))],
        scratch_shapes=[pltpu.VMEM((8, NST), F32)] * 4 + [pltpu.VMEM((tt + 8, 2 * NST), F32),
                                                          pltpu.VMEM((tt, 2 * NST), F32)],
        compiler_params=_cp(("arbitrary", "arbitrary")),
    )(dy, u_p, st, st, fcr, fci, air, aii, bm, cm, dvec, lre8, lim8)


def _rope(v, c, s1, s2):
    return v * c + _roll(v, -16) * s1 + _roll(v, 16) * s2


def _rope_t(dv, c, s1, s2):
    return dv * c + _roll(dv * s1, 16) + _roll(dv * s2, -16)


_SCALE = (QK_NOPE + QK_ROPE) ** -0.5
_LOG2E = 1.4426950408889634
_C2 = _SCALE * _LOG2E


def _attn_fwd(q, k, v, S, tq):
    n = q.shape[0]
    nb, nq = n // S, S // tq

    def body(q_ref, k_ref, v_ref, o_ref, lr_ref):
        qi = pl.program_id(2)
        qv = q_ref[...]

        def tile(j, c, diagonal):
            m, acc = c
            off = pl.multiple_of(j * tq, tq)
            s = _dot_nt(qv, k_ref[pl.ds(off, tq), :])
            if diagonal:
                rows = lax.broadcasted_iota(jnp.int32, s.shape, 0)
                cols = lax.broadcasted_iota(jnp.int32, s.shape, 1)
                s = jnp.where(cols <= rows, s, NEG)
            mn = jnp.maximum(m, jnp.max(s, axis=1, keepdims=True))
            p = jnp.exp2(s - mn)
            acc = jnp.exp2(m - mn) * acc + _dot(p.astype(BF16), v_ref[pl.ds(off, tq), :])
            return mn, acc

        init = (jnp.full((tq, 1), NEG, F32), jnp.zeros((tq, HP), F32))
        c = lax.fori_loop(0, qi, lambda j, c: tile(j, c, False), init)
        m, acc = tile(qi, c, True)
        l = acc[:, V_HEAD:V_HEAD + 1]
        vlane = lax.broadcasted_iota(jnp.int32, acc.shape, 1)
        o_ref[...] = jnp.where(vlane < V_HEAD, acc / l, 0.0).astype(BF16)
        lane = lax.broadcasted_iota(jnp.int32, (8, HP), 1)
        lse = jnp.broadcast_to(m + jnp.log(l) * _LOG2E, (tq, HP))
        lr_ref[...] = _rows_of(lse, jnp.where(lane == 0, 1.0, 0.0).astype(BF16))

    qs = pl.BlockSpec((tq, HP), lambda b, h, i: (b * nq + i, h))
    ks = pl.BlockSpec((S, HP), lambda b, h, i: (b, h))
    return pl.pallas_call(
        body, name="attn_fwd", grid=(nb, NH, nq),
        out_shape=[jax.ShapeDtypeStruct((n, NH * HP), BF16), jax.ShapeDtypeStruct((nb * NH * 8, S), F32)],
        in_specs=[qs, ks, ks], out_specs=[qs, pl.BlockSpec((8, tq), lambda b, h, i: (b * NH + h, i))],
        compiler_params=_cp(("parallel", "parallel", "arbitrary")),
    )(q, k, v)


def _rows_of(x, pick):
    x1 = x.astype(BF16)
    r1 = x - x1.astype(F32)
    x2 = r1.astype(BF16)
    x3 = (r1 - x2.astype(F32)).astype(BF16)
    return _dot_nt(pick, x1) + _dot_nt(pick, x2) + _dot_nt(pick, x3)


def _attn_bwd(q, k, v, dob, lrow, drow, S, tq):
    n = q.shape[0]
    nb, nq = n // S, S // tq

    def body(q_ref, k_ref, v_ref, do_ref, lr_ref, dr_ref, dqo_ref, dk_ref, dv_ref, dq_ref):
        kj = pl.program_id(2)

        @pl.when(kj == 0)
        def _():
            dq_ref[...] = jnp.zeros_like(dq_ref)

        kt = k_ref[...]
        vt = v_ref[...]

        def tile(i, c, diagonal):
            dk, dv = c
            off = pl.multiple_of(i * tq, tq)
            qv = q_ref[pl.ds(off, tq), :]
            dob = do_ref[pl.ds(off, tq), :]
            lr = lr_ref[0:1, pl.ds(off, tq)]
            dr = dr_ref[0:1, pl.ds(off, tq)]
            st = _dot_nt(kt, qv)
            dpt = _dot_nt(vt, dob)
            pt = jnp.exp2(st - lr)
            if diagonal:
                keys = lax.broadcasted_iota(jnp.int32, pt.shape, 0)
                qrys = lax.broadcasted_iota(jnp.int32, pt.shape, 1)
                pt = jnp.where(keys <= qrys, pt, 0.0)
            dst = (pt * (dpt - dr)).astype(BF16)
            dq_ref[pl.ds(off, tq), :] += _dot_tn(dst, kt)
            return dk + _dot(dst, qv), dv + _dot(pt.astype(BF16), dob)

        zero = jnp.zeros((tq, HP), F32)
        c = tile(kj, (zero, zero), True)
        dk, dv = lax.fori_loop(kj + 1, nq, lambda i, c: tile(i, c, False), c)
        dk_ref[...] = dk.astype(BF16)
        dv_ref[...] = dv.astype(BF16)

        @pl.when(kj == nq - 1)
        def _():
            dqo_ref[...] = dq_ref[...].astype(BF16)

    ts = pl.BlockSpec((tq, HP), lambda b, h, i: (b * nq + i, h))
    fs = pl.BlockSpec((S, HP), lambda b, h, i: (b, h))
    rs = pl.BlockSpec((8, S), lambda b, h, i: (b * NH + h, 0))
    return pl.pallas_call(
        body, name="attn_bwd", grid=(nb, NH, nq),
        out_shape=[jax.ShapeDtypeStruct((n, NH * HP), BF16)] * 3,
        in_specs=[fs, ts, ts, fs, rs, rs], out_specs=[fs, ts, ts],
        scratch_shapes=[pltpu.VMEM((S, HP), F32)],
        compiler_params=_cp(("parallel", "parallel", "arbitrary")),
    )(q, k, v, dob, lrow, drow)


def _p2(yssm, oattn, x, target, modp, gs, ga, w_out, g2, gf, w_ff1, w_ff2, S, tm):
    n = x.shape[0]
    tps = S // tm
    nb = n // S

    def body(ys_ref, oa_ref, x_ref, t_ref, mod_ref, gs_ref, ga_ref, wo_ref, g2_ref, gf_ref, w1_ref, w2_ref,
             yn_ref, h2_ref, dx1_ref, r_ref, da_ref, dff_ref, do_ref, dys_ref, doa_ref, dr_ref,
             accs_ref, accg_ref, accg3_ref):
        i = pl.program_id(0)
        sh2, sc2, gt2 = mod_ref[0, 3:4, :], mod_ref[0, 4:5, :], mod_ref[0, 5:6, :]
        fsh, fsc = mod_ref[0, 6:7, :], mod_ref[0, 7:8, :]
        yh, rs = _rms(ys_ref[...], D_SSM)
        oa = oa_ref[...].astype(F32)
        ah, ra_ = _rms(oa, D_ATTN)
        yn = jnp.concatenate([yh * gs_ref[...], ah * ga_ref[...]], axis=1).astype(BF16)
        yn_ref[...] = yn
        o = _dot(yn, wo_ref[...])
        x1 = x_ref[...] + mod_ref[0, 2:3, :] * o
        x1h, r2 = _rms(x1, D)
        g2_v = g2_ref[...]
        h2 = ((x1h * g2_v) * (1.0 + sc2) + sh2).astype(BF16)
        h2_ref[...] = h2
        a = _dot(h2, w1_ref[...])
        ra = jnp.maximum(a, 0.0)
        rb = (ra * ra).astype(BF16)
        r_ref[...] = rb
        ff = _dot(rb, w2_ref[...])
        x2 = x1 + gt2 * ff
        x2h, rf = _rms(x2, D)
        gf_v = gf_ref[...]
        outn = x2h * gf_v
        err = outn * (1.0 + fsc) + fsh - t_ref[...]
        dout = err * (1.0 / D)
        doutn = dout * (1.0 + fsc)
        dx2 = _rms_bwd(doutn * gf_v, x2h, rf, D)
        dff = (gt2 * dx2).astype(BF16)
        dff_ref[...] = dff
        dr = _dot_nt(dff, w2_ref[...])
        da = (dr * (2.0 * ra)).astype(BF16)
        da_ref[...] = da
        dh2 = _dot_nt(da, w1_ref[...])
        dn2 = dh2 * (1.0 + sc2)
        dx1 = dx2 + _rms_bwd(dn2 * g2_v, x1h, r2, D)
        dx1_ref[...] = dx1
        dob = (mod_ref[0, 2:3, :] * dx1).astype(BF16)
        do_ref[...] = dob
        dyn = _dot_nt(dob, wo_ref[...])
        d1 = dyn[:, 0:D_SSM]
        d2 = dyn[:, D_SSM:D_SSM + NH * HP]
        dys_ref[...] = _rms_bwd(d1 * gs_ref[...], yh, rs, D_SSM)
        doa = _rms_bwd(d2 * ga_ref[...], ah, ra_, D_ATTN)
        doa_ref[...] = doa.astype(BF16)
        prod = doa * oa
        ones = jnp.ones((8, HP), BF16)
        for h in range(NH):
            dr_ref[h * 8:(h + 1) * 8, :] = _rows_of(prod[:, h * HP:(h + 1) * HP], ones)

        @pl.when(i % tps == 0)
        def _():
            accs_ref[...] = jnp.zeros_like(accs_ref)

        @pl.when(i == 0)
        def _():
            accg_ref[...] = jnp.zeros_like(accg_ref)
            accg3_ref[...] = jnp.zeros_like(accg3_ref)

        accs_ref[0, 2:3, :] += _colsum(dx1 * o)
        accg3_ref[0:1, 0:D_SSM] += _colsum(d1 * yh)
        accg3_ref[1:2, :] += _colsum(d2 * ah)
        accs_ref[0, 3:4, :] += _colsum(dh2)
        accs_ref[0, 4:5, :] += _colsum(dh2 * (x1h * g2_v))
        accs_ref[0, 5:6, :] += _colsum(dx2 * ff)
        accs_ref[0, 6:7, :] += _colsum(dout)
        accs_ref[0, 7:8, :] += _colsum(dout * outn)
        accg_ref[0:1, :] += _colsum(dn2 * x1h)
        accg_ref[1:2, :] += _colsum(doutn * x2h)
        accg_ref[2:3, :] += _colsum(err * err) * (0.5 / D)

    row = lambda w: pl.BlockSpec((tm, w), lambda i: (i, 0))
    return pl.pallas_call(
        body, name="p2_mlp_loss", grid=(n // tm,),
        out_shape=[jax.ShapeDtypeStruct((n, D_SSM + NH * HP), BF16), jax.ShapeDtypeStruct((n, D), BF16),
                   jax.ShapeDtypeStruct((n, D), F32), jax.ShapeDtypeStruct((n, D_FF), BF16),
                   jax.ShapeDtypeStruct((n, D_FF), BF16), jax.ShapeDtypeStruct((n, D), BF16),
                   jax.ShapeDtypeStruct((n, D), BF16), jax.ShapeDtypeStruct((n, D_SSM), F32),
                   jax.ShapeDtypeStruct((n, NH * HP), BF16), jax.ShapeDtypeStruct((nb * NH * 8, S), F32),
                   jax.ShapeDtypeStruct((nb, 8, D), F32), jax.ShapeDtypeStruct((8, D), F32),
                   jax.ShapeDtypeStruct((8, NH * HP), F32)],
        in_specs=[row(D_SSM), row(NH * HP), row(D), row(D), pl.BlockSpec((1, 8, D), lambda i: (i // tps, 0, 0)),
                  _VM, _VM, _VM, _VM, _VM, _VM, _VM],
        out_specs=[row(D_SSM + NH * HP), row(D),
                   row(D), row(D_FF), row(D_FF), row(D),
                   row(D), row(D_SSM), row(NH * HP), pl.BlockSpec((NH * 8, tm), lambda i: (i // tps, i % tps)),
                   pl.BlockSpec((1, 8, D), lambda i: (i // tps, 0, 0)),
                   pl.BlockSpec((8, D), lambda i: (0, 0)), pl.BlockSpec((8, NH * HP), lambda i: (0, 0))],
        compiler_params=_cp(("arbitrary",)),
    )(yssm, oattn, x, target, modp, gs, ga, w_out, g2, gf, w_ff1, w_ff2)


def _wgrad(a, b, name, col_slots=0):
    n, k1 = a.shape
    k2 = b.shape[1]
    bn = next((b for b in (2048, 1024, 512) if n % b == 0), n)
    bk1 = next((b for b in (1024, 512) if k1 % b == 0), k1)
    bk2 = k2 // col_slots if col_slots else (1024 if (k2 % 1024 == 0) else k2)

    def body(a_ref, b_ref, o_ref):
        @pl.when(pl.program_id(2) == 0)
        def _():
            o_ref[...] = jnp.zeros_like(o_ref)

        o_ref[...] += _dot_tn(a_ref[...], b_ref[...]).reshape(o_ref.shape)

    if col_slots:
        out_shape = jax.ShapeDtypeStruct((col_slots, k1, bk2), F32)
        out_spec = pl.BlockSpec((1, bk1, bk2), lambda i, j, t: (j, i, 0))
    else:
        out_shape = jax.ShapeDtypeStruct((k1, k2), F32)
        out_spec = pl.BlockSpec((bk1, bk2), lambda i, j, t: (i, j))
    return pl.pallas_call(
        body, name=name, grid=(k1 // bk1, k2 // bk2, n // bn),
        out_shape=out_shape,
        in_specs=[pl.BlockSpec((bn, bk1), lambda i, j, t: (t, i)), pl.BlockSpec((bn, bk2), lambda i, j, t: (t, j))],
        out_specs=out_spec,
        compiler_params=_cp(("parallel", "parallel", "arbitrary")),
    )(a, b)


def _row_block(rows):
    if rows <= 256:
        return rows
    return next(b for b in (256, 192, 128, 64, 32, 16, 8) if rows % b == 0)


def _add_half(g, recv, cidx, name):
    _, rows2, w = g.shape
    rows = rows2 // 2
    br = _row_block(rows)
    nblk = rows // br

    def body(c_ref, g_ref, r_ref, o_ref):
        o_ref[...] = (g_ref[...] + r_ref[...]).astype(BF16)

    return pl.pallas_call(
        body, name=name,
        grid_spec=pltpu.PrefetchScalarGridSpec(
            num_scalar_prefetch=1, grid=(4, nblk),
            in_specs=[pl.BlockSpec((1, br, w), lambda s, i, c: (s, c[0] * nblk + i, 0)),
                      pl.BlockSpec((1, br, w), lambda s, i, c: (s, i, 0))],
            out_specs=pl.BlockSpec((1, br, w), lambda s, i, c: (s, i, 0))),
        out_shape=jax.ShapeDtypeStruct((4, rows, w), BF16),
        compiler_params=_cp(("parallel", "parallel")),
    )(cidx, g, recv)


def _add_chips(r, name):
    _, rows, w = r.shape
    br = _row_block(rows)

    def body(r_ref, o_ref):
        f = lambda k: r_ref[k].astype(F32)
        o_ref[...] = ((f(0) + f(1)) + f(2)) + f(3)

    return pl.pallas_call(
        body, name=name, grid=(rows // br,),
        out_shape=jax.ShapeDtypeStruct((rows, w), F32),
        in_specs=[pl.BlockSpec((4, br, w), lambda i: (0, i, 0))],
        out_specs=pl.BlockSpec((br, w), lambda i: (i, 0)),
        compiler_params=_cp(("parallel",)),
    )(r)


def _pair_sum(a, sa, b, sb):
    def body(a_ref, sa_ref, b_ref, sb_ref, oa_ref, ob_ref):
        oa_ref[...] = (a_ref[...].astype(F32) + sa_ref[...].astype(F32)).astype(BF16)
        ob_ref[...] = b_ref[...] + sb_ref[...]

    return pl.pallas_call(
        body, name="small_grad_pair_sum",
        out_shape=[jax.ShapeDtypeStruct(a.shape, BF16), jax.ShapeDtypeStruct(b.shape, F32)],
        in_specs=[_VM] * 4, out_specs=[_VM, _VM], compiler_params=_cp(),
    )(a, sa, b, sb)


def _sum_devices(a, b):
    def body(a_ref, b_ref, oa_ref, ob_ref):
        acc = a_ref[0:1, :].astype(F32)
        accb = b_ref[0:1, :]
        for k in range(1, a.shape[0]):
            acc = acc + a_ref[k:k + 1, :].astype(F32)
            accb = accb + b_ref[k:k + 1, :]
        oa_ref[...] = acc
        ob_ref[...] = accb

    return pl.pallas_call(
        body, name="small_grad_sum",
        out_shape=[jax.ShapeDtypeStruct((1, a.shape[1]), F32), jax.ShapeDtypeStruct((1, b.shape[1]), F32)],
        in_specs=[_VM, _VM], out_specs=[_VM, _VM], compiler_params=_cp(),
    )(a, b)


def _adamw_math(wv, gv, mv, vv):
    m_new = ADAM_B1 * mv + (1.0 - ADAM_B1) * gv
    v_new = ADAM_B2 * vv + (1.0 - ADAM_B2) * (gv * gv)
    m_hat = m_new / (1.0 - ADAM_B1 ** ADAM_STEP)
    v_hat = v_new / (1.0 - ADAM_B2 ** ADAM_STEP)
    return -ADAM_LR * (m_hat / (jnp.sqrt(v_hat) + ADAM_EPS) + ADAM_WD * wv), m_new, v_new


def _adamw_small(ws, gs, ms, vs):
    k = len(ws)

    def body(*refs):
        ins, outs = refs[:4 * k], refs[4 * k:]
        for t in range(k):
            d, m_new, v_new = _adamw_math(ins[t][...], ins[k + t][...], ins[2 * k + t][...], ins[3 * k + t][...])
            outs[t][...] = d
            outs[k + t][...] = m_new
            outs[2 * k + t][...] = v_new

    shapes = [jax.ShapeDtypeStruct(w.shape, F32) for w in ws]
    return pl.pallas_call(
        body, name="adamw_small", out_shape=shapes * 3,
        in_specs=[_VM] * (4 * k), out_specs=[_VM] * (3 * k), compiler_params=_cp(),
    )(*ws, *gs, *ms, *vs)


def _adamw(w, g, m, v, name):
    rows, wd = w.shape
    br = _row_block(rows)

    def body(w_ref, g_ref, m_ref, v_ref, d_ref, nm_ref, nv_ref):
        d, m_new, v_new = _adamw_math(w_ref[...], g_ref[...], m_ref[...], v_ref[...])
        d_ref[...] = d
        nm_ref[...] = m_new
        nv_ref[...] = v_new

    spec = pl.BlockSpec((br, wd), lambda i: (i, 0))
    return pl.pallas_call(
        body, name=name, grid=(rows // br,),
        out_shape=[jax.ShapeDtypeStruct((rows, wd), F32)] * 3,
        in_specs=[spec] * 4, out_specs=[spec] * 3,
        compiler_params=_cp(("parallel",)),
    )(w, g, m, v)


def _adamw_halves(w, mine, other, m, v, cidx, name):
    rows, wd = w.shape
    h = rows // 2
    br = _row_block(h)
    nblk = h // br

    def body(c_ref, w_ref, a_ref, b_ref, m_ref, v_ref, g_ref, d_ref, nm_ref, nv_ref):
        gv = jnp.where(pl.program_id(0) == c_ref[0], a_ref[...], b_ref[...])
        d, m_new, v_new = _adamw_math(w_ref[...], gv, m_ref[...], v_ref[...])
        g_ref[...] = gv
        d_ref[...] = d
        nm_ref[...] = m_new
        nv_ref[...] = v_new

    full = pl.BlockSpec((br, wd), lambda hf, i, c: (hf * nblk + i, 0))
    half = pl.BlockSpec((br, wd), lambda hf, i, c: (i, 0))
    return pl.pallas_call(
        body, name=name,
        grid_spec=pltpu.PrefetchScalarGridSpec(
            num_scalar_prefetch=1, grid=(2, nblk),
            in_specs=[full, half, half, full, full], out_specs=[full] * 4),
        out_shape=[jax.ShapeDtypeStruct((rows, wd), F32)] * 4,
        compiler_params=_cp(("parallel", "parallel")),
    )(cidx, w, mine, other, m, v)


def _other_chips(x, y):
    return [(1 - x, y), (x, 1 - y), (1 - x, 1 - y)]


def _other_devices(x, y, c):
    flip = lambda v, d: (1 - v) if d else v
    return [(flip(x, dx), flip(y, dy), flip(c, dc))
            for dx in (0, 1) for dy in (0, 1) for dc in (0, 1) if (dx, dy, dc) != (0, 0, 0)]


def _exchange(name, ins, out_shapes, n_local, n_remote, plan):
    ni, no = len(ins), len(out_shapes)

    def body(*refs):
        in_refs, out_refs = refs[:ni], refs[ni:ni + no]
        send_sems, recv_sems, local_sems = refs[ni + no:]
        x, y, c = lax.axis_index("x"), lax.axis_index("y"), lax.axis_index("c")
        local, remote = plan(in_refs, out_refs, x, y, c)
        assert len(local) == n_local and len(remote) == n_remote

        def push(k, src, dst, dev):
            return pltpu.make_async_remote_copy(src_ref=src, dst_ref=dst, send_sem=send_sems.at[k],
                                                recv_sem=recv_sems.at[k], device_id=dev, device_id_type=MESH)

        own = [pltpu.make_async_copy(s, d, local_sems.at[i]) for i, (s, d) in enumerate(local)]
        for cp in own:
            cp.start()
        sends = [push(k, s, d, dev) for k, (s, d, dev, _) in enumerate(remote)]
        for cp in sends:
            cp.start()
        for k, (s, _, dev, landing) in enumerate(remote):
            push(k, s, landing, dev).wait_recv()
        for cp in sends:
            cp.wait_send()
        for cp in own:
            cp.wait()

    return pl.pallas_call(
        body, name=name, out_shape=out_shapes,
        in_specs=[_ANY] * ni, out_specs=[_ANY] * no,
        scratch_shapes=[pltpu.SemaphoreType.DMA((n_remote,)), pltpu.SemaphoreType.DMA((n_remote,)),
                        pltpu.SemaphoreType.DMA((max(n_local, 1),))],
        compiler_params=pltpu.CompilerParams(has_side_effects=True),
    )(*ins)


def _gather_chips(name, shards, everyone=()):
    ns, ne = len(shards), len(everyone)
    outs = [jax.ShapeDtypeStruct((4,) + a.shape, a.dtype) for a in shards]
    outs += [jax.ShapeDtypeStruct((8,) + a.shape, a.dtype) for a in everyone]

    def plan(i, o, x, y, c):
        mine, me = 2 * x + y, 4 * x + 2 * y + c
        local, remote = [], []
        for t in range(ns):
            local.append((i[t], o[t].at[mine]))
            for px, py in _other_chips(x, y):
                remote.append((i[t], o[t].at[mine], (px, py, c), o[t].at[2 * px + py]))
        for t in range(ns, ns + ne):
            local.append((i[t], o[t].at[me]))
            for px, py, pc in _other_devices(x, y, c):
                remote.append((i[t], o[t].at[me], (px, py, pc), o[t].at[4 * px + 2 * py + pc]))
        return local, remote

    return _exchange(name, list(shards) + list(everyone), outs, ns + ne, 3 * ns + 7 * ne, plan)


_HBM = pl.BlockSpec(memory_space=pltpu.HBM)
_SEM = pl.BlockSpec(memory_space=pltpu.SEMAPHORE)
_EFFECT = pltpu.SideEffectType.DATAFLOW_SIDE_EFFECTING


def _split_start(name, ins, land_shapes, n_remote, plan, after):
    ni, nl = len(ins), len(land_shapes)
    srcs = [pltpu.with_memory_space_constraint(a, pltpu.HBM) for a in ins]
    lands = [pltpu.with_memory_space_constraint(lax.empty(s.shape, s.dtype), pltpu.HBM) for s in land_shapes]

    def body(*refs):
        src, land = refs[:ni], refs[ni:ni + nl]
        first = ni + nl + 1
        send, recv = refs[first:first + n_remote], refs[first + n_remote:first + 2 * n_remote]
        token = refs[first + 2 * n_remote + ni + nl]
        x, y, c = lax.axis_index("x"), lax.axis_index("y"), lax.axis_index("c")
        remote = plan(src, land, x, y, c)
        assert len(remote) == n_remote
        for k, (s, d, dev, _) in enumerate(remote):
            pltpu.make_async_remote_copy(src_ref=s, dst_ref=d, send_sem=send[k], recv_sem=recv[k],
                                         device_id=dev, device_id_type=MESH).start()
        token[...] = jnp.zeros_like(token)

    out = pl.pallas_call(
        body, name=name + "_start",
        out_shape=[pltpu.SemaphoreType.DMA(())] * (2 * n_remote)
                  + [pltpu.HBM(a.shape, a.dtype) for a in ins] + [pltpu.HBM(s.shape, s.dtype) for s in land_shapes]
                  + [jax.ShapeDtypeStruct((8, 128), F32)],
        in_specs=[_HBM] * (ni + nl) + [_ANY], out_specs=[_SEM] * (2 * n_remote) + [_HBM] * (ni + nl) + [_VM],
        input_output_aliases={t: 2 * n_remote + t for t in range(ni + nl)},
        compiler_params=pltpu.CompilerParams(has_side_effects=_EFFECT),
    )(*srcs, *lands, after)
    sems, thru = out[:2 * n_remote], out[2 * n_remote:2 * n_remote + ni + nl]
    return (name, sems, thru[:ni], thru[ni:], n_remote, plan), out[-1]


def _split_wait(handle, after):
    name, sems, srcs, lands, n_remote, plan = handle
    ni, nl = len(srcs), len(lands)

    def body(*refs):
        src, land = refs[:ni], refs[ni:ni + nl]
        send, recv = refs[ni + nl:ni + nl + n_remote], refs[ni + nl + n_remote:ni + nl + 2 * n_remote]
        x, y, c = lax.axis_index("x"), lax.axis_index("y"), lax.axis_index("c")
        for k, (s, _, dev, landing) in enumerate(plan(src, land, x, y, c)):
            cp = pltpu.make_async_remote_copy(src_ref=s, dst_ref=landing, send_sem=send[k], recv_sem=recv[k],
                                              device_id=dev, device_id_type=MESH)
            cp.wait_send()
            cp.wait_recv()

    out = pl.pallas_call(
        body, name=name + "_wait",
        out_shape=[pltpu.HBM(a.shape, a.dtype) for a in srcs] + [pltpu.HBM(a.shape, a.dtype) for a in lands],
        in_specs=[_HBM] * (ni + nl) + [_SEM] * (2 * n_remote) + [_ANY], out_specs=[_HBM] * (ni + nl),
        input_output_aliases={t: t for t in range(ni + nl)},
        compiler_params=pltpu.CompilerParams(has_side_effects=_EFFECT),
    )(*srcs, *lands, *sems, after)
    return out[:ni], out[ni:]


def _plan_to_chips(src, land, x, y, c):
    mine = 2 * x + y
    return [(src[t], land[t].at[mine], (px, py, c), land[t].at[2 * px + py])
            for t in range(len(src)) for px, py in _other_chips(x, y)]


def _plan_swap_halves(src, land, x, y, c):
    out = []
    for t in range(len(src)):
        h = src[t].shape[1] // 2
        out.append((src[t].at[:, pl.ds(pl.multiple_of((1 - c) * h, 8), h), :], land[t], (x, y, 1 - c), land[t]))
    return out


def _plan_scatter_chips(src, land, x, y, c):
    mine = 2 * x + y
    return [(src[t].at[2 * px + py], land[t].at[mine], (px, py, c), land[t].at[2 * px + py])
            for t in range(len(src)) for px, py in _other_chips(x, y)]


def _swap_halves(gs, everyone, whole):
    ns, ne, nw = len(gs), len(everyone), len(whole)
    outs = [jax.ShapeDtypeStruct((4, g.shape[1] // 2, g.shape[2]), g.dtype) for g in gs]
    outs += [jax.ShapeDtypeStruct((8,) + a.shape, a.dtype) for a in everyone]
    outs += [jax.ShapeDtypeStruct(a.shape, a.dtype) for a in whole]

    def plan(i, o, x, y, c):
        me = 4 * x + 2 * y + c
        local, remote = [], []
        for t in range(ns):
            h = gs[t].shape[1] // 2
            theirs = i[t].at[:, pl.ds(pl.multiple_of((1 - c) * h, 8), h), :]
            remote.append((theirs, o[t], (x, y, 1 - c), o[t]))
        for t in range(ns, ns + ne):
            local.append((i[t], o[t].at[me]))
            for px, py, pc in _other_devices(x, y, c):
                remote.append((i[t], o[t].at[me], (px, py, pc), o[t].at[4 * px + 2 * py + pc]))
        for t in range(ns + ne, ns + ne + nw):
            remote.append((i[t], o[t], (x, y, 1 - c), o[t]))
        return local, remote

    return _exchange("grad_swap_sibling", list(gs) + list(everyone) + list(whole), outs, ne, ns + 7 * ne + nw, plan)


def _scatter_chips(parts, per_chip):
    ns, ng = len(parts), len(per_chip)
    outs = [jax.ShapeDtypeStruct(a.shape, a.dtype) for a in parts]
    outs += [jax.ShapeDtypeStruct((4,) + a.shape, a.dtype) for a in per_chip]

    def plan(i, o, x, y, c):
        mine = 2 * x + y
        local, remote = [], []
        for t in range(ns):
            local.append((i[t].at[mine], o[t].at[mine]))
            for px, py in _other_chips(x, y):
                remote.append((i[t].at[2 * px + py], o[t].at[mine], (px, py, c), o[t].at[2 * px + py]))
        for t in range(ns, ns + ng):
            local.append((i[t], o[t].at[mine]))
            for px, py in _other_chips(x, y):
                remote.append((i[t], o[t].at[mine], (px, py, c), o[t].at[2 * px + py]))
        return local, remote

    return _exchange("grad_scatter_chips", list(parts) + list(per_chip), outs, ns + ng, 3 * (ns + ng), plan)


def _join_halves(halves):
    ns = len(halves)
    outs = [jax.ShapeDtypeStruct(a.shape, a.dtype) for a in halves]

    def plan(i, o, x, y, c):
        return [], [(i[t], o[t], (x, y, 1 - c), o[t]) for t in range(ns)]

    return _exchange("grad_join_sibling", list(halves), outs, 0, ns, plan)


def _pad_heads_cols(w, per, used):
    k = w.shape[0]
    w = w.reshape(k, NH, per)[:, :, :used]
    return jnp.pad(w, ((0, 0), (0, 0), (0, HP - used))).reshape(k, NH * HP)


def _unpad_heads_cols(w, used):
    k = w.shape[0]
    return w.reshape(k, NH, HP)[:, :, :used]


def _prep_weights(wf):
    bf = lambda a: a.astype(BF16)
    out = {}
    out["w_in"] = jnp.pad(bf(wf["w_in"]), ((0, 0), (0, IN_PAD - IN_COLS)))
    out["w_glu"] = bf(wf["w_glu"])
    out["w_uq"] = _pad_heads_cols(bf(wf["w_uq"]), QK_NOPE + QK_ROPE, QK_NOPE + QK_ROPE)
    wkv = bf(wf["w_ukv"]).reshape(KV_LORA, NH, QK_NOPE + V_HEAD)
    wk = jnp.pad(wkv[:, :, :QK_NOPE], ((0, 0), (0, 0), (0, HP - QK_NOPE))).reshape(KV_LORA, NH * HP)
    wv = jnp.pad(wkv[:, :, QK_NOPE:], ((0, 0), (0, 0), (0, HP - V_HEAD))).reshape(KV_LORA, NH * HP)
    out["w_ukv"] = jnp.concatenate([wk, wv], axis=1)
    return out


def _prep_late_weights(wf):
    bf = lambda a: a.astype(BF16)
    out = {}
    wo = bf(wf["w_out"])
    wo_a = jnp.pad(wo[D_SSM:].reshape(NH, V_HEAD, D), ((0, 0), (0, HP - V_HEAD), (0, 0))).reshape(NH * HP, D)
    out["w_out"] = jnp.concatenate([wo[:D_SSM], wo_a], axis=0)
    out["w_ff1"] = bf(wf["w_ff1"])
    out["w_ff2"] = bf(wf["w_ff2"])
    return out


def _rope_tables(positions):
    inv_freq = ROPE_BASE ** (-jnp.arange(0, QK_ROPE, 2, dtype=F32) / QK_ROPE)
    ang = positions.astype(F32)[:, None] * inv_freq
    cos, sin = jnp.cos(ang), jnp.sin(ang)
    n = positions.shape[0]
    one = jnp.ones((n, QK_NOPE), F32)
    z16 = jnp.zeros((n, 16), F32)
    z32 = jnp.zeros((n, 32), F32)
    z64 = jnp.zeros((n, QK_NOPE), F32)
    rc = jnp.concatenate([one, cos, cos, z32], axis=1)
    rs1 = jnp.concatenate([z64, -sin, z16, z32], axis=1)
    rs2 = jnp.concatenate([z64, z16, sin, z32], axis=1)
    return rc, rs1, rs2


def _permute_rows(a, S):
    n, w = a.shape
    return a.reshape(n // S, 8, S // 8, w).transpose(0, 2, 1, 3).reshape(n, w)


def _unpermute_rows(a, S):
    n, w = a.shape
    return a.reshape(n // S, S // 8, 8, w).transpose(0, 2, 1, 3).reshape(n, w)


def _block_diag_in(bb):
    eye = jnp.eye(8, dtype=bb.dtype)
    blocks = jnp.einsum("qgph,gk->qghkp", bb.reshape(4, 8, P, H), eye).reshape(4, QB, QS)
    return blocks.transpose(1, 0, 2).reshape(QB, NST)


def _block_diag_out(cc):
    eye = jnp.eye(8, dtype=cc.dtype)
    return jnp.einsum("qghp,gk->qgpkh", cc.reshape(4, 8, H, P), eye).reshape(NST, QB)


def _slots(full):
    r, cdim = full.shape
    return full.reshape(r, 4, cdim // 4).transpose(1, 0, 2)


def _unslots(g):
    s, r, cs = g.shape
    return g.transpose(1, 0, 2).reshape(r, s * cs)


def _local_step(x, positions, target, modp, wf, late_weights=None, reducer=None):
    nb, S, _ = x.shape
    n = nb * S
    tm = min(256, S)
    tr = min(512, S)
    tt = min(512, S)
    tq = min(512, S // 2)
    kw = _prep_weights(wf)
    row = lambda a: a.reshape(1, -1).astype(F32)

    xf = x.reshape(n, D)
    tf = target.reshape(n, D)
    g1, g2, gf = row(wf["norm1_g"]), row(wf["norm2_g"]), row(wf["final_norm_g"])
    rc, rs1, rs2 = _rope_tables(positions.reshape(n))
    gq, gkv = row(wf["q_norm_g"]), row(wf["kv_norm_g"])
    h1, u, lat, q, k, v, qn, kvn = _f1_fwd(xf, modp, g1, kw["w_in"], rc, rs1, rs2, gq, gkv,
                                           kw["w_uq"], kw["w_ukv"], S, tr)

    col = lambda a: a.reshape(NST, 1)
    lam_re, lam_im = col(wf["ssm_lambda_re"]), col(wf["ssm_lambda_im"])
    logdt = jnp.repeat(wf["ssm_log_dt"].reshape(G, 1), P, axis=1).reshape(NST, 1)
    b_re, b_im = wf["ssm_b_re"].reshape(NST, H), wf["ssm_b_im"].reshape(NST, H)
    lbr, lbi, bbr, bbi = _ssm_param_fwd(lam_re, lam_im, logdt, b_re, b_im)
    lre8 = jnp.broadcast_to(lbr.reshape(1, NST), (8, NST))
    lim8 = jnp.broadcast_to(lbi.reshape(1, NST), (8, NST))
    bm = jnp.concatenate([_block_diag_in(bbr.reshape(G, P, H)), _block_diag_in(bbi.reshape(G, P, H))],
                         axis=1).astype(BF16)
    cm = jnp.concatenate([_block_diag_out(wf["ssm_c_re"]), -_block_diag_out(wf["ssm_c_im"])], axis=0).astype(BF16)
    dvec = row(wf["ssm_d"])
    u_p = _permute_rows(u, S)
    fcr, fci = _ssm_local(u_p, bm, lre8, lim8, S, tt)
    st, ypre, z, gact, yssm_p = _ssm_fwd(u_p, fcr, fci, bm, cm, dvec, kw["w_glu"], lre8, lim8, S, tt)
    yssm = _unpermute_rows(yssm_p, S)

    oattn, lrow = _attn_fwd(q, k, v, S, tq)

    gs = row(wf["ssm_out_g"])
    ga = jnp.pad(wf["attn_out_g"].reshape(NH, V_HEAD), ((0, 0), (0, HP - V_HEAD))).reshape(1, NH * HP)
    kw.update(_prep_late_weights(late_weights(oattn) if late_weights is not None else wf))
    (yn, h2, dx1, r, da, dff, do, dyssm, dob, drow, accs2, accg2, accg3) = _p2(
        yssm, oattn, xf, tf, modp, gs, ga, kw["w_out"], g2, gf, kw["w_ff1"], kw["w_ff2"], S, tm)
    loss = accg2[2:3]
    g_ff1 = _wgrad(h2, da, "wgrad_ff1", col_slots=4)
    g_ff2 = _wgrad(r, dff, "wgrad_ff2").reshape(4, D_FF // 4, D)
    gwo = _wgrad(yn, do, "wgrad_out")
    g_out = jnp.concatenate([gwo[:D_SSM].reshape(2, D_SSM // 2, D),
                             gwo[D_SSM:].reshape(2, NH // 2 * HP, D).reshape(2, NH // 2, HP, D)[:, :, :V_HEAD]
                             .reshape(2, D_ATTN // 2, D)], axis=0)
    lre8_b = lre8
    if reducer is not None:
        drow = drow + reducer.start([g_ff1, g_ff2, g_out])[0, 0]

    dq, dk, dv = _attn_bwd(q, k, v, dob, lrow, drow, S, tq)
    if reducer is not None:
        lre8_b = lre8 + reducer.middle(dq)[0, 0]

    dys_p = _permute_rows(dyssm, S)
    dy, dz, air, aii = _ssm_bwd_a(dys_p, z, ypre, kw["w_glu"], cm, lre8_b, lim8, S, tt)
    du_p, dcm, dbm, dd, dlr, dli = _ssm_bwd_b(dy, u_p, st, fcr, fci, air, aii, bm, cm, dvec, lre8, lim8, S, tt)
    du = _unpermute_rows(du_p, S)
    dcm = dcm.reshape(2, 4, 8, P, 8, H)
    dc_re = jnp.einsum("qgpgh->qghp", dcm[0]).reshape(G, H, P)
    dc_im = -jnp.einsum("qgpgh->qghp", dcm[1]).reshape(G, H, P)
    dbm = dbm.reshape(8, H, 2, 4, 8, P)
    dbb_re = jnp.einsum("ghqgp->qgph", dbm[:, :, 0]).reshape(NST, H)
    dbb_im = jnp.einsum("ghqgp->qgph", dbm[:, :, 1]).reshape(NST, H)
    gb_re, gb_im, glr, gli, gdt = _ssm_param_bwd(lam_re, lam_im, logdt, b_re, b_im, dlr.reshape(NST, 1),
                                                 dli.reshape(NST, 1), dbb_re, dbb_im)
    glogdt = _rowsum(gdt.reshape(G, P))

    dx, dproj, dqb, dkvb, accs1, accg1, accm = _f1_bwd(du, dq, dk, dv, lat, rc, rs1, rs2, gq, gkv, kw["w_uq"],
                                                       kw["w_ukv"], dx1, xf, modp, g1, kw["w_in"], S, tr)

    big = {}
    big["w_in"] = _slots(_wgrad(h1, dproj, "wgrad_in")[:, :IN_COLS])
    big["w_glu"] = _wgrad(gact, dz, "wgrad_glu", col_slots=4)
    big["w_uq"] = _slots(_unpad_heads_cols(_wgrad(qn, dqb, "wgrad_uq"), QK_NOPE + QK_ROPE).reshape(Q_LORA, -1))
    gkvw = _wgrad(kvn, dkvb, "wgrad_ukv")
    big["w_ukv"] = _slots(jnp.concatenate([_unpad_heads_cols(gkvw[:, :NH * HP], QK_NOPE),
                                           _unpad_heads_cols(gkvw[:, NH * HP:], V_HEAD)], axis=2).reshape(KV_LORA, -1))
    big["w_out"] = g_out
    big["w_ff1"] = g_ff1
    big["w_ff2"] = g_ff2

    small = {}
    small["norm1_g"] = accg1[0:1]
    small["norm2_g"] = accg2[0:1]
    small["final_norm_g"] = accg2[1:2]
    small["ssm_out_g"] = accg3[0:1, :D_SSM]
    small["attn_out_g"] = accg3[1].reshape(NH, HP)[:, :V_HEAD].reshape(1, D_ATTN)
    small["q_norm_g"] = accm[0:1, :Q_LORA]
    small["kv_norm_g"] = accm[1:2, :KV_LORA]
    small["ssm_lambda_re"] = glr.reshape(G, P)
    small["ssm_lambda_im"] = gli.reshape(G, P)
    small["ssm_b_re"] = gb_re
    small["ssm_b_im"] = gb_im
    small["ssm_c_re"] = dc_re.reshape(G * H, P)
    small["ssm_c_im"] = dc_im.reshape(G * H, P)
    small["ssm_d"] = dd.reshape(G, H)
    small["ssm_log_dt"] = glogdt.reshape(1, G)
    return loss, dx.reshape(nb, S, D), big, small, accs1 + accs2


def _view2d(a):
    return a.reshape(-1, a.shape[-1]) if a.ndim > 1 else a.reshape(1, -1)


def kernel(x, c, positions, ada_w, ada_b, norm1_g, w_in, ssm_lambda_re, ssm_lambda_im, ssm_b_re, ssm_b_im, ssm_c_re, ssm_c_im, ssm_d, ssm_log_dt, w_glu, q_norm_g, w_uq, kv_norm_g, w_ukv, ssm_out_g, attn_out_g, w_out, norm2_g, w_ff1, w_ff2, final_ada_w, final_ada_b, final_norm_g, loss_target, m_ada_w, m_ada_b, m_norm1_g, m_w_in, m_ssm_lambda_re, m_ssm_lambda_im, m_ssm_b_re, m_ssm_b_im, m_ssm_c_re, m_ssm_c_im, m_ssm_d, m_ssm_log_dt, m_w_glu, m_q_norm_g, m_w_uq, m_kv_norm_g, m_w_ukv, m_ssm_out_g, m_attn_out_g, m_w_out, m_norm2_g, m_w_ff1, m_w_ff2, m_final_ada_w, m_final_ada_b, m_final_norm_g, v_ada_w, v_ada_b, v_norm1_g, v_w_in, v_ssm_lambda_re, v_ssm_lambda_im, v_ssm_b_re, v_ssm_b_im, v_ssm_c_re, v_ssm_c_im, v_ssm_d, v_ssm_log_dt, v_w_glu, v_q_norm_g, v_w_uq, v_kv_norm_g, v_w_ukv, v_ssm_out_g, v_attn_out_g, v_w_out, v_norm2_g, v_w_ff1, v_w_ff2, v_final_ada_w, v_final_ada_b, v_final_norm_g):
    args = dict(locals())
    names = list(inspect.signature(kernel).parameters)
    wnames = names[3:names.index("loss_target")]
    small_names = [nm for nm in wnames if nm not in GATHERED and nm not in TP]
    reduced_names = [nm for nm in small_names if nm not in ("ada_b", "final_ada_b")]
    w = {nm: args[nm] for nm in wnames}
    m = {nm: args["m_" + nm] for nm in wnames}
    v = {nm: args["v_" + nm] for nm in wnames}
    nb = x.shape[0]
    xi, yi, ci = lax.axis_index("x"), lax.axis_index("y"), lax.axis_index("c")
    chip, me = 2 * xi + yi, 4 * xi + 2 * yi + ci

    unslot = lambda nm, g: g.reshape(-1, g.shape[-1]) if nm in ROW_SHARDED else _unslots(g)
    early = [nm for nm in GATHERED if nm not in LATE]
    got = _gather_chips("gather_weights", [_view2d(w[nm]).astype(BF16) for nm in early], [c])
    wf = {nm: unslot(nm, g) for nm, g in zip(early, got)}
    for nm in small_names:
        wf[nm] = w[nm][0] if w[nm].ndim > 1 else w[nm]
    c_all = got[len(early)].reshape(8 * nb, D)

    na, nf = ada_w.shape[-1], final_ada_w.shape[-1]
    ada_b_s = lax.dynamic_slice(ada_b, (0, chip * na), (1, na))
    fada_b_s = lax.dynamic_slice(final_ada_b.reshape(1, -1), (0, chip * nf), (1, nf))
    cond_all, modcols = _mod_fwd(c_all, ada_w[0], ada_b_s, final_ada_w, fada_b_s)
    (mod_g,) = _gather_chips("gather_mod", [modcols])
    mine = lax.dynamic_slice(mod_g, (0, me * nb, 0), (4, nb, na + nf))
    modp = jnp.concatenate([mine[:, :, :na].transpose(1, 0, 2).reshape(nb, 6, D),
                            mine[:, :, na:].transpose(1, 0, 2).reshape(nb, 2, D)], axis=1)

    own_late = [_view2d(w[nm]).astype(BF16) for nm in LATE]
    late_gather, token = _split_start("gather_late", own_late,
                                      [jax.ShapeDtypeStruct((4,) + a.shape, a.dtype) for a in own_late],
                                      3 * len(LATE), _plan_to_chips, modp)
    modp = modp + token[0, 0]

    def late_weights(after):
        sent, landed = _split_wait(late_gather, after)
        return {nm: unslot(nm, lax.dynamic_update_slice(g, own[None], (chip, 0, 0)))
                for nm, g, own in zip(LATE, landed, sent)}

    cidx = ci.astype(jnp.int32).reshape(1)
    ahead = ["w_ff1", "w_ff2", "w_out"]

    class Reducer:
        def start(self, gs):
            lands = [jax.ShapeDtypeStruct((4, g.shape[1] // 2, g.shape[2]), g.dtype) for g in gs]
            self.swap, tok = _split_start("grad_swap_ff", gs, lands, len(gs), _plan_swap_halves, modp)
            return tok

        def middle(self, after):
            gs, got = _split_wait(self.swap, after)
            sums = [_add_half(g, r, cidx, "grad_add_sibling_" + nm) for nm, g, r in zip(ahead, gs, got)]
            lands = [jax.ShapeDtypeStruct(s.shape, s.dtype) for s in sums]
            self.scatter, tok = _split_start("grad_scatter_ff", sums, lands, 3 * len(sums), _plan_scatter_chips, modp)
            return tok

        def finish(self, after):
            out = []
            for nm, s, l in zip(ahead, *_split_wait(self.scatter, after)):
                own = lax.dynamic_slice(s, (chip, 0, 0), (1,) + s.shape[1:])
                out.append(_add_chips(lax.dynamic_update_slice(l, own, (chip, 0, 0)), "grad_add_chips_" + nm))
            return out

    reducer = Reducer()
    loss_row, grad_x, big, small, dmodp = _local_step(x, positions, loss_target, modp, wf, late_weights, reducer)

    rest = [nm for nm in GATHERED if nm not in ahead]
    sizes = [small[nm].size for nm in reduced_names]
    pad = -sum(sizes) % 128
    packed = jnp.concatenate([small[nm].reshape(1, -1) for nm in reduced_names] + [jnp.zeros((1, pad), F32)],
                             axis=1).astype(BF16)
    swapped = _swap_halves([big[nm] for nm in rest], [dmodp.reshape(nb, 8 * D)], [packed, loss_row])
    chip_sums = [_add_half(big[nm], r, cidx, "grad_add_sibling_" + nm) for nm, r in zip(rest, swapped)]
    chip_small = _pair_sum(packed, swapped[len(rest) + 1], loss_row, swapped[len(rest) + 2])
    scattered = _scatter_chips(chip_sums, chip_small)
    half_of = {nm: _add_chips(r, "grad_add_chips_" + nm) for nm, r in zip(rest, scattered)}
    half_of.update(zip(ahead, reducer.finish(grad_x)))
    halves = [half_of[nm] for nm in GATHERED]
    others = _join_halves(halves)
    grads = {}
    dmod_all = swapped[len(rest)].reshape(8 * nb, 8 * D)
    small_sum, loss_sum = _sum_devices(scattered[len(rest)].reshape(4, -1), scattered[len(rest) + 1].reshape(4, -1))
    loss = jnp.sum(loss_sum)
    off = 0
    for nm, sz in zip(reduced_names, sizes):
        grads[nm] = small_sum[:, off:off + sz].reshape(small[nm].shape)
        off += sz

    dsl = jnp.concatenate([lax.dynamic_slice(dmod_all, (0, chip * na), (8 * nb, na)),
                           lax.dynamic_slice(dmod_all, (0, 6 * D + chip * nf), (8 * nb, nf))], axis=1)
    gw, gb = _mod_bwd(cond_all.T, dsl, dmod_all)
    grads["ada_w"], grads["final_ada_w"] = gw[:, :na], gw[:, na:]
    grads["ada_b"], grads["final_ada_b"] = gb[:, :6 * D], gb[:, 6 * D:]

    delta, new_m, new_v = {}, {}, {}
    for nm, mine_h, other_h in zip(GATHERED, halves, others):
        grads[nm], delta[nm], new_m[nm], new_v[nm] = _adamw_halves(
            _view2d(w[nm]), mine_h, other_h, _view2d(m[nm]), _view2d(v[nm]), cidx, "adamw_" + nm)
    for nm in TP:
        delta[nm], new_m[nm], new_v[nm] = _adamw(_view2d(w[nm]), grads[nm], _view2d(m[nm]), _view2d(v[nm]),
                                                  "adamw_" + nm)
    upd = _adamw_small([_view2d(w[nm]) for nm in small_names], [grads[nm] for nm in small_names],
                       [_view2d(m[nm]) for nm in small_names], [_view2d(v[nm]) for nm in small_names])
    k = len(small_names)
    for t, nm in enumerate(small_names):
        delta[nm], new_m[nm], new_v[nm] = upd[t], upd[k + t], upd[2 * k + t]

    outs = [grads, delta, new_m, new_v]
    return (loss, grad_x, *[d[nm].reshape(w[nm].shape) for d in outs for nm in wnames])
```

```python
import inspect
import math

import jax
import jax.numpy as jnp
from jax import lax
from jax.experimental import pallas as pl
from jax.experimental.pallas import tpu as pltpu

F32 = jnp.float32
BF16 = jnp.bfloat16

D = 1024
D_SSM = 512
G = 32
H = 16
P = 64
NST = G * P
D_ATTN = 512
NH = 8
QK_NOPE = 64
QK_ROPE = 32
V_HEAD = 64
HP = 128
Q_LORA = 384
KV_LORA = 256
IN_COLS = D_SSM + Q_LORA + KV_LORA + QK_ROPE
IN_PAD = 1280
D_FF = 4096
ROPE_BASE = 10000.0
EPS = 1e-6
ADAM_LR = 0.001
ADAM_B1 = 0.9
ADAM_B2 = 0.999
ADAM_EPS = 1e-08
ADAM_WD = 0.01
ADAM_STEP = 10
NEG = -1e30
VMEM_LIMIT = 60 << 20

MESH = pl.DeviceIdType.MESH
_VM = pl.BlockSpec(memory_space=pltpu.VMEM)
_ANY = pl.BlockSpec(memory_space=pl.ANY)

GATHERED = ["w_in", "w_glu", "w_uq", "w_ukv", "w_out", "w_ff1", "w_ff2"]
TP = ["ada_w", "final_ada_w"]
ROW_SHARDED = ("w_out", "w_ff2")
LATE = ["w_out", "w_ff1", "w_ff2"]


def _cp(sem=None, vmem=VMEM_LIMIT):
    kw = dict(vmem_limit_bytes=vmem)
    if sem is not None:
        kw["dimension_semantics"] = sem
    return pltpu.CompilerParams(**kw)


def _dot(a, b):
    return jnp.dot(a, b, preferred_element_type=F32)


def _dot_nt(a, b):
    return lax.dot_general(a, b, (((1,), (1,)), ((), ())), preferred_element_type=F32)


def _dot_tn(a, b):
    return lax.dot_general(a, b, (((0,), (0,)), ((), ())), preferred_element_type=F32)


def _rms(x, n):
    r = lax.rsqrt(jnp.sum(x * x, axis=-1, keepdims=True) * (1.0 / n) + EPS)
    return x * r, r


def _rms_bwd(dyg, xhat, r, n):
    return r * (dyg - xhat * (jnp.sum(dyg * xhat, axis=-1, keepdims=True) * (1.0 / n)))


def _sigmoid(x):
    return 1.0 / (1.0 + jnp.exp(-x))


_GK = math.sqrt(2.0 / math.pi)
_GC = 0.044715


def _gelu(y):
    t = jnp.tanh(_GK * (y + _GC * y * y * y))
    return 0.5 * y * (1.0 + t)


def _gelu_grad(y):
    t = jnp.tanh(_GK * (y + _GC * y * y * y))
    return 0.5 * (1.0 + t) + 0.5 * y * (1.0 - t * t) * _GK * (1.0 + 3.0 * _GC * y * y)


def _colsum(x):
    return jnp.sum(x, axis=0, keepdims=True)


def _roll(x, s):
    return pltpu.roll(x, s % x.shape[-1], x.ndim - 1)


def _mod_fwd(c_all, ada_w_s, ada_b_s, fada_w_s, fada_b_s):
    nseq = c_all.shape[0]
    na, nf = ada_w_s.shape[1], fada_w_s.shape[1]

    def body(c_ref, w_ref, b_ref, fw_ref, fb_ref, cond_ref, mod_ref):
        cv = c_ref[...]
        cond = cv * _sigmoid(cv)
        cond_ref[...] = cond
        cb = cond.astype(BF16)
        mod_ref[:, 0:na] = _dot(cb, w_ref[...].astype(BF16)) + b_ref[...]
        mod_ref[:, na:na + nf] = _dot(cb, fw_ref[...].astype(BF16)) + fb_ref[...]

    return pl.pallas_call(
        body, name="mod_fwd",
        out_shape=[jax.ShapeDtypeStruct((nseq, D), F32), jax.ShapeDtypeStruct((nseq, na + nf), F32)],
        in_specs=[_VM] * 5, out_specs=[_VM] * 2, compiler_params=_cp(),
    )(c_all, ada_w_s, ada_b_s, fada_w_s, fada_b_s)


def _mod_bwd(cond_t, dsl, dall):
    nseq, n = dsl.shape
    bc = 512

    def body(ct_ref, dm_ref, da_ref, gw_ref, gb_ref):
        ct = ct_ref[...]
        dm = dm_ref[...]
        acc = ct[:, 0:1] * dm[0:1, :]
        for b in range(1, nseq):
            acc = acc + ct[:, b:b + 1] * dm[b:b + 1, :]
        gw_ref[...] = acc

        @pl.when(pl.program_id(0) == 0)
        def _():
            gb_ref[...] = _colsum(da_ref[...])

    return pl.pallas_call(
        body, name="mod_bwd", grid=(n // bc,),
        out_shape=[jax.ShapeDtypeStruct((D, n), F32), jax.ShapeDtypeStruct((1, dall.shape[1]), F32)],
        in_specs=[_VM, pl.BlockSpec((nseq, bc), lambda i: (0, i)), _VM],
        out_specs=[pl.BlockSpec((D, bc), lambda i: (0, i)), pl.BlockSpec((1, dall.shape[1]), lambda i: (0, 0))],
        compiler_params=_cp(("arbitrary",)),
    )(cond_t, dsl, dall)


def _f1_fwd(x, modp, g1, w_in, rc, rs1, rs2, gq, gkv, w_uq, w_ukv, S, tm):
    n = x.shape[0]
    tps = S // tm
    LAT = IN_PAD - D_SSM

    def body(x_ref, mod_ref, g_ref, w_ref, c_ref, s1_ref, s2_ref, gq_ref, gkv_ref, wq_ref, wkv_ref,
             h_ref, u_ref, lat_ref, q_ref, k_ref, v_ref, qn_ref, kvn_ref):
        xhat, _ = _rms(x_ref[...], D)
        h = (xhat * g_ref[...]) * (1.0 + mod_ref[0, 1:2, :]) + mod_ref[0, 0:1, :]
        hb = h.astype(BF16)
        h_ref[...] = hb
        proj = _dot(hb, w_ref[...])
        u_ref[...] = proj[:, 0:D_SSM]
        lat_ref[...] = proj[:, D_SSM:IN_PAD]
        c, s1, s2 = c_ref[...], s1_ref[...], s2_ref[...]
        qhat, _ = _rms(proj[:, D_SSM:D_SSM + Q_LORA], Q_LORA)
        qn = (qhat * gq_ref[...]).astype(BF16)
        qn_ref[...] = qn
        q = _dot(qn, wq_ref[...])
        qr = _rope(q, jnp.tile(c, (1, NH)), jnp.tile(s1, (1, NH)), jnp.tile(s2, (1, NH)))
        q_ref[...] = (qr * _C2).astype(BF16)
        khat, _ = _rms(proj[:, D_SSM + Q_LORA:D_SSM + Q_LORA + KV_LORA], KV_LORA)
        kvn = (khat * gkv_ref[...]).astype(BF16)
        kvn_ref[...] = kvn
        kv = _dot(kvn, wkv_ref[...])
        kr = _rope(_roll(proj[:, IN_PAD - HP:IN_PAD], 64), c, s1, s2)
        k_ref[...] = (kv[:, 0:NH * HP] + jnp.tile(kr, (1, NH))).astype(BF16)
        vv = kv[:, NH * HP:2 * NH * HP]
        lane = lax.broadcasted_iota(jnp.int32, vv.shape, 1)
        v_ref[...] = jnp.where(lane % HP == V_HEAD, 1.0, vv).astype(BF16)

    row = lambda w: pl.BlockSpec((tm, w), lambda i: (i, 0))
    return pl.pallas_call(
        body, name="f1_fwd", grid=(n // tm,),
        out_shape=[jax.ShapeDtypeStruct((n, D), BF16), jax.ShapeDtypeStruct((n, D_SSM), F32),
                   jax.ShapeDtypeStruct((n, LAT), F32)] + [jax.ShapeDtypeStruct((n, NH * HP), BF16)] * 3 +
                  [jax.ShapeDtypeStruct((n, Q_LORA), BF16), jax.ShapeDtypeStruct((n, KV_LORA), BF16)],
        in_specs=[row(D), pl.BlockSpec((1, 8, D), lambda i: (i // tps, 0, 0)), _VM, _VM,
                  row(HP), row(HP), row(HP), _VM, _VM, _VM, _VM],
        out_specs=[row(D), row(D_SSM), row(LAT)] + [row(NH * HP)] * 3 + [row(Q_LORA), row(KV_LORA)],
        compiler_params=_cp(("parallel",)),
    )(x, modp, g1, w_in, rc, rs1, rs2, gq, gkv, w_uq, w_ukv)


def _f1_bwd(du, dq, dk, dv, lat, rc, rs1, rs2, gq, gkv, w_uq, w_ukv, dx1, x, modp, g1, w_in, S, tm):
    n = x.shape[0]
    tps = S // tm
    nb = n // S

    def body(du_ref, dq_ref, dk_ref, dv_ref, lat_ref, c_ref, s1_ref, s2_ref, gq_ref, gkv_ref, wq_ref, wkv_ref,
             dx1_ref, x_ref, mod_ref, g_ref, w_ref,
             dx_ref, dproj_ref, dqb_ref, dkvb_ref, accs_ref, accg_ref, accm_ref):
        i = pl.program_id(0)
        c, s1, s2 = c_ref[...], s1_ref[...], s2_ref[...]
        dqu = _rope_t(dq_ref[...] * _SCALE, jnp.tile(c, (1, NH)), jnp.tile(s1, (1, NH)),
                      jnp.tile(s2, (1, NH))).astype(BF16)
        dqb_ref[...] = dqu
        dqn = _dot_nt(dqu, wq_ref[...])
        qhat, rq = _rms(lat_ref[:, 0:Q_LORA], Q_LORA)
        dql = _rms_bwd(dqn * gq_ref[...], qhat, rq, Q_LORA)
        dkf = dk_ref[...] * (1.0 / _LOG2E)
        dkv = jnp.concatenate([dkf.astype(BF16), dv_ref[...].astype(BF16)], axis=1)
        dkvb_ref[...] = dkv
        dkvn = _dot_nt(dkv, wkv_ref[...])
        khat, rk = _rms(lat_ref[:, Q_LORA:Q_LORA + KV_LORA], KV_LORA)
        dkvl = _rms_bwd(dkvn * gkv_ref[...], khat, rk, KV_LORA)
        dkr = dkf[:, 0:HP]
        for h in range(1, NH):
            dkr = dkr + dkf[:, h * HP:(h + 1) * HP]
        lane = lax.broadcasted_iota(jnp.int32, dkr.shape, 1)
        dkr = jnp.where((lane >= QK_NOPE) & (lane < QK_NOPE + QK_ROPE), dkr, 0.0)
        dkr = _roll(_rope_t(dkr, c, s1, s2), -64)

        @pl.when(i == 0)
        def _():
            accm_ref[...] = jnp.zeros_like(accm_ref)

        accm_ref[0:1, 0:Q_LORA] += _colsum(dqn * qhat)
        accm_ref[1:2, 0:KV_LORA] += _colsum(dkvn * khat)

        dproj = jnp.concatenate([du_ref[...], dql, dkvl, dkr], axis=1).astype(BF16)
        dproj_ref[...] = dproj
        dh = _dot_nt(dproj, w_ref[...])
        xhat, r = _rms(x_ref[...], D)
        g = g_ref[...]
        dn = dh * (1.0 + mod_ref[0, 1:2, :])
        dx_ref[...] = dx1_ref[...] + _rms_bwd(dn * g, xhat, r, D)

        @pl.when(i % tps == 0)
        def _():
            accs_ref[...] = jnp.zeros_like(accs_ref)

        @pl.when(i == 0)
        def _():
            accg_ref[...] = jnp.zeros_like(accg_ref)

        accs_ref[0, 0:1, :] += _colsum(dh)
        accs_ref[0, 1:2, :] += _colsum(dh * (xhat * g))
        accg_ref[0:1, :] += _colsum(dn * xhat)

    row = lambda w: pl.BlockSpec((tm, w), lambda i: (i, 0))
    return pl.pallas_call(
        body, name="f1_bwd", grid=(n // tm,),
        out_shape=[jax.ShapeDtypeStruct((n, D), F32), jax.ShapeDtypeStruct((n, IN_PAD), BF16),
                   jax.ShapeDtypeStruct((n, NH * HP), BF16), jax.ShapeDtypeStruct((n, 2 * NH * HP), BF16),
                   jax.ShapeDtypeStruct((nb, 8, D), F32), jax.ShapeDtypeStruct((8, D), F32),
                   jax.ShapeDtypeStruct((8, Q_LORA), F32)],
        in_specs=[row(D_SSM)] + [row(NH * HP)] * 3 + [row(IN_PAD - D_SSM), row(HP), row(HP), row(HP),
                                                     _VM, _VM, _VM, _VM, row(D), row(D),
                                                     pl.BlockSpec((1, 8, D), lambda i: (i // tps, 0, 0)), _VM, _VM],
        out_specs=[row(D), row(IN_PAD), row(NH * HP), row(2 * NH * HP),
                   pl.BlockSpec((1, 8, D), lambda i: (i // tps, 0, 0)), pl.BlockSpec((8, D), lambda i: (0, 0)),
                   pl.BlockSpec((8, Q_LORA), lambda i: (0, 0))],
        compiler_params=_cp(("arbitrary",)),
    )(du, dq, dk, dv, lat, rc, rs1, rs2, gq, gkv, w_uq, w_ukv, dx1, x, modp, g1, w_in)


def _ssm_param_fwd(lam_re, lam_im, logdt, b_re, b_im):
    def body(lr_ref, li_ref, ld_ref, br_ref, bi_ref, lbr_ref, lbi_ref, bbr_ref, bbi_ref):
        lr, li = lr_ref[...], li_ref[...]
        dt = jnp.exp(ld_ref[...])
        er = jnp.exp(lr * dt)
        lbr = er * jnp.cos(li * dt)
        lbi = er * jnp.sin(li * dt)
        den = 1.0 / (lr * lr + li * li)
        cr = ((lbr - 1.0) * lr + lbi * li) * den
        ci = (lbi * lr - (lbr - 1.0) * li) * den
        lbr_ref[...] = lbr
        lbi_ref[...] = lbi
        bbr_ref[...] = cr * br_ref[...] - ci * bi_ref[...]
        bbi_ref[...] = cr * bi_ref[...] + ci * br_ref[...]

    return pl.pallas_call(
        body, name="ssm_param_fwd",
        out_shape=[jax.ShapeDtypeStruct((NST, 1), F32)] * 2 + [jax.ShapeDtypeStruct((NST, H), F32)] * 2,
        in_specs=[_VM] * 5, out_specs=[_VM] * 4, compiler_params=_cp(),
    )(lam_re, lam_im, logdt, b_re, b_im)


def _ssm_param_bwd(lam_re, lam_im, logdt, b_re, b_im, dlb_re, dlb_im, dbb_re, dbb_im):
    def body(lr_ref, li_ref, ld_ref, br_ref, bi_ref, dlr_ref, dli_ref, dbr_ref, dbi_ref,
             gbr_ref, gbi_ref, glr_ref, gli_ref, gdt_ref):
        lr, li = lr_ref[...], li_ref[...]
        dt = jnp.exp(ld_ref[...])
        er = jnp.exp(lr * dt)
        lbr = er * jnp.cos(li * dt)
        lbi = er * jnp.sin(li * dt)
        den = 1.0 / (lr * lr + li * li)
        nr, ni = lbr - 1.0, lbi
        cr = (nr * lr + ni * li) * den
        ci = (ni * lr - nr * li) * den
        br, bi = br_ref[...], bi_ref[...]
        dbr, dbi = dbr_ref[...], dbi_ref[...]
        gbr_ref[...] = cr * dbr + ci * dbi
        gbi_ref[...] = cr * dbi - ci * dbr
        gcr = jnp.sum(dbr * br + dbi * bi, axis=1, keepdims=True)
        gci = jnp.sum(dbi * br - dbr * bi, axis=1, keepdims=True)
        ilr, ili = lr * den, -li * den
        glbr = dlr_ref[...] + (gcr * ilr + gci * ili)
        glbi = dli_ref[...] + (gci * ilr - gcr * ili)
        qr = -(cr * ilr - ci * ili)
        qi = -(cr * ili + ci * ilr)
        glr = gcr * qr + gci * qi
        gli = gci * qr - gcr * qi
        glr = glr + dt * (glbr * lbr + glbi * lbi)
        gli = gli + dt * (glbi * lbr - glbr * lbi)
        wr = lr * lbr - li * lbi
        wi = lr * lbi + li * lbr
        glr_ref[...] = glr
        gli_ref[...] = gli
        gdt_ref[...] = (glbr * wr + glbi * wi) * dt

    return pl.pallas_call(
        body, name="ssm_param_bwd",
        out_shape=[jax.ShapeDtypeStruct((NST, H), F32)] * 2 + [jax.ShapeDtypeStruct((NST, 1), F32)] * 3,
        in_specs=[_VM] * 9, out_specs=[_VM] * 5, compiler_params=_cp(),
    )(lam_re, lam_im, logdt, b_re, b_im, dlb_re, dlb_im, dbb_re, dbb_im)


def _rowsum(a):
    def body(a_ref, o_ref):
        o_ref[...] = jnp.sum(a_ref[...], axis=1, keepdims=True)

    return pl.pallas_call(
        body, name="rowsum", out_shape=jax.ShapeDtypeStruct((a.shape[0], 1), F32),
        in_specs=[_VM], out_specs=_VM, compiler_params=_cp(),
    )(a)


QB = D_SSM // 4
QS = 4 * QB


def _bd_lo(part, q):
    return part * NST + q * QS


def _bd_expand(ub, bm_ref, out_ref):
    for part in range(2):
        for q in range(4):
            lo = _bd_lo(part, q)
            out_ref[:, lo:lo + QS] = _dot(ub[:, q * QB:(q + 1) * QB], bm_ref[:, lo:lo + QS])


def _bd_expand_t(db, cm_ref, out_ref):
    for part in range(2):
        for q in range(4):
            lo = _bd_lo(part, q)
            out_ref[:, lo:lo + QS] = _dot_nt(db[:, q * QB:(q + 1) * QB], cm_ref[lo:lo + QS, :])


def _bd_project(sb, cm_ref):
    return jnp.concatenate(
        [_dot(sb[:, _bd_lo(0, q):_bd_lo(0, q) + QS], cm_ref[_bd_lo(0, q):_bd_lo(0, q) + QS, :])
         + _dot(sb[:, _bd_lo(1, q):_bd_lo(1, q) + QS], cm_ref[_bd_lo(1, q):_bd_lo(1, q) + QS, :])
         for q in range(4)], axis=1)


def _bd_project_t(ab, bm_ref):
    return jnp.concatenate(
        [_dot_nt(ab[:, _bd_lo(0, q):_bd_lo(0, q) + QS], bm_ref[:, _bd_lo(0, q):_bd_lo(0, q) + QS])
         + _dot_nt(ab[:, _bd_lo(1, q):_bd_lo(1, q) + QS], bm_ref[:, _bd_lo(1, q):_bd_lo(1, q) + QS])
         for q in range(4)], axis=1)


def _pow2k(pr, pi, nsq):
    for _ in range(nsq):
        pr, pi = pr * pr - pi * pi, 2.0 * pr * pi
    return pr, pi


def _ssm_local(u_p, bm, lre8, lim8, S, tt):
    n = u_p.shape[0]
    nb, nt = n // S, S // tt
    nsq = int(round(math.log2(S // 8)))
    assert 2 ** nsq == S // 8

    def body(u_ref, bm_ref, lre_ref, lim_ref, cre_ref, cim_ref, sre, sim, bu):
        j = pl.program_id(1)

        @pl.when(j == 0)
        def _():
            sre[...] = jnp.zeros_like(sre)
            sim[...] = jnp.zeros_like(sim)

        _bd_expand(u_ref[...].astype(BF16), bm_ref, bu)
        lre, lim = lre_ref[...], lim_ref[...]

        def step(i, c):
            sr, si = c
            off = pl.multiple_of(i * 8, 8)
            br = bu[pl.ds(off, 8), 0:NST]
            bi = bu[pl.ds(off, 8), NST:2 * NST]
            return lre * sr - lim * si + br, lre * si + lim * sr + bi

        sr, si = lax.fori_loop(0, tt // 8, step, (sre[...], sim[...]))
        sre[...] = sr
        sim[...] = si

        @pl.when(j == nt - 1)
        def _():
            pr, pi = _pow2k(lre[0:1], lim[0:1], nsq)
            cr = jnp.zeros((1, NST), F32)
            ci = jnp.zeros((1, NST), F32)
            cre_ref[0:1, :] = cr
            cim_ref[0:1, :] = ci
            for k in range(1, 8):
                cr, ci = sr[k - 1:k] + pr * cr - pi * ci, si[k - 1:k] + pr * ci + pi * cr
                cre_ref[k:k + 1, :] = cr
                cim_ref[k:k + 1, :] = ci

    return pl.pallas_call(
        body, name="ssm_local", grid=(nb, nt),
        out_shape=[jax.ShapeDtypeStruct((nb * 8, NST), F32)] * 2,
        in_specs=[pl.BlockSpec((tt, D_SSM), lambda b, j: (b * nt + j, 0)), _VM, _VM, _VM],
        out_specs=[pl.BlockSpec((8, NST), lambda b, j: (b, 0))] * 2,
        scratch_shapes=[pltpu.VMEM((8, NST), F32), pltpu.VMEM((8, NST), F32), pltpu.VMEM((tt, 2 * NST), F32)],
        compiler_params=_cp(("arbitrary", "arbitrary")),
    )(u_p, bm, lre8, lim8)


def _ssm_fwd(u_p, cre, cim, bm, cm, dvec, w_glu, lre8, lim8, S, tt):
    n = u_p.shape[0]
    nb, nt = n // S, S // tt

    def body(u_ref, cre_ref, cim_ref, bm_ref, cm_ref, d_ref, wg_ref, lre_ref, lim_ref,
             st_ref, ypre_ref, z_ref, gact_ref, yssm_ref, sre, sim, bu):
        j = pl.program_id(1)

        @pl.when(j == 0)
        def _():
            sre[...] = cre_ref[...]
            sim[...] = cim_ref[...]

        u = u_ref[...]
        _bd_expand(u.astype(BF16), bm_ref, bu)
        lre, lim = lre_ref[...], lim_ref[...]

        def step(i, c):
            sr, si = c
            off = pl.multiple_of(i * 8, 8)
            nr = lre * sr - lim * si + bu[pl.ds(off, 8), 0:NST]
            ni = lre * si + lim * sr + bu[pl.ds(off, 8), NST:2 * NST]
            bu[pl.ds(off, 8), 0:NST] = nr
            bu[pl.ds(off, 8), NST:2 * NST] = ni
            return nr, ni

        sr, si = lax.fori_loop(0, tt // 8, step, (sre[...], sim[...]))
        sre[...] = sr
        sim[...] = si
        stb = bu[...].astype(BF16)
        st_ref[...] = stb
        y = _bd_project(stb, cm_ref) + d_ref[...] * u
        ypre_ref[...] = y
        gb = _gelu(y).astype(BF16)
        gact_ref[...] = gb
        z = _dot(gb, wg_ref[...])
        z_ref[...] = z
        yssm_ref[...] = z[:, 0:D_SSM] * _sigmoid(z[:, D_SSM:2 * D_SSM])

    row = lambda w: pl.BlockSpec((tt, w), lambda b, j: (b * nt + j, 0))
    return pl.pallas_call(
        body, name="ssm_fwd", grid=(nb, nt),
        out_shape=[jax.ShapeDtypeStruct((n, 2 * NST), BF16), jax.ShapeDtypeStruct((n, D_SSM), F32),
                   jax.ShapeDtypeStruct((n, 2 * D_SSM), F32), jax.ShapeDtypeStruct((n, D_SSM), BF16),
                   jax.ShapeDtypeStruct((n, D_SSM), F32)],
        in_specs=[row(D_SSM), pl.BlockSpec((8, NST), lambda b, j: (b, 0)), pl.BlockSpec((8, NST), lambda b, j: (b, 0)),
                  _VM, _VM, _VM, _VM, _VM, _VM],
        out_specs=[row(2 * NST), row(D_SSM), row(2 * D_SSM), row(D_SSM), row(D_SSM)],
        scratch_shapes=[pltpu.VMEM((8, NST), F32), pltpu.VMEM((8, NST), F32), pltpu.VMEM((tt, 2 * NST), F32)],
        compiler_params=_cp(("arbitrary", "arbitrary")),
    )(u_p, cre, cim, bm, cm, dvec, w_glu, lre8, lim8)


def _ssm_bwd_a(dys_p, z, ypre, w_glu, cm, lre8, lim8, S, tt):
    n = z.shape[0]
    nb, nt = n // S, S // tt
    nsq = int(round(math.log2(S // 8)))
    ng = tt // 8

    def body(dys_ref, z_ref, y_ref, wg_ref, cm_ref, lre_ref, lim_ref, dy_ref, dz_ref, are_ref, aim_ref, sre, sim, gb):
        j = pl.program_id(1)

        @pl.when(j == 0)
        def _():
            sre[...] = jnp.zeros_like(sre)
            sim[...] = jnp.zeros_like(sim)

        z = z_ref[...]
        z1, z2 = z[:, 0:D_SSM], z[:, D_SSM:2 * D_SSM]
        sg = _sigmoid(z2)
        dys = dys_ref[...]
        dz = jnp.concatenate([dys * sg, dys * z1 * sg * (1.0 - sg)], axis=1).astype(BF16)
        dz_ref[...] = dz
        dy = _dot_nt(dz, wg_ref[...]) * _gelu_grad(y_ref[...])
        dy_ref[...] = dy
        _bd_expand_t(dy.astype(BF16), cm_ref, gb)
        lre, lim = lre_ref[...], lim_ref[...]

        def step(i, c):
            ar, ai = c
            off = pl.multiple_of((ng - 1 - i) * 8, 8)
            gr = gb[pl.ds(off, 8), 0:NST]
            gi = gb[pl.ds(off, 8), NST:2 * NST]
            return lre * ar + lim * ai + gr, lre * ai - lim * ar + gi

        ar, ai = lax.fori_loop(0, ng, step, (sre[...], sim[...]))
        sre[...] = ar
        sim[...] = ai

        @pl.when(j == nt - 1)
        def _():
            pr, pi = _pow2k(lre[0:1], -lim[0:1], nsq)
            cr = jnp.zeros((1, NST), F32)
            ci = jnp.zeros((1, NST), F32)
            are_ref[7:8, :] = cr
            aim_ref[7:8, :] = ci
            for k in range(6, -1, -1):
                cr, ci = ar[k + 1:k + 2] + pr * cr - pi * ci, ai[k + 1:k + 2] + pr * ci + pi * cr
                are_ref[k:k + 1, :] = cr
                aim_ref[k:k + 1, :] = ci

    row = lambda w: pl.BlockSpec((tt, w), lambda b, j: (b * nt + nt - 1 - j, 0))
    return pl.pallas_call(
        body, name="ssm_bwd_a", grid=(nb, nt),
        out_shape=[jax.ShapeDtypeStruct((n, D_SSM), F32), jax.ShapeDtypeStruct((n, 2 * D_SSM), BF16),
                   jax.ShapeDtypeStruct((nb * 8, NST), F32), jax.ShapeDtypeStruct((nb * 8, NST), F32)],
        in_specs=[row(D_SSM), row(2 * D_SSM), row(D_SSM), _VM, _VM, _VM, _VM],
        out_specs=[row(D_SSM), row(2 * D_SSM), pl.BlockSpec((8, NST), lambda b, j: (b, 0)),
                   pl.BlockSpec((8, NST), lambda b, j: (b, 0))],
        scratch_shapes=[pltpu.VMEM((8, NST), F32), pltpu.VMEM((8, NST), F32), pltpu.VMEM((tt, 2 * NST), F32)],
        compiler_params=_cp(("arbitrary", "arbitrary")),
    )(dys_p, z, ypre, w_glu, cm, lre8, lim8)


def _ssm_bwd_b(dy, u_p, st, fcr, fci, air, aii, bm, cm, dvec, lre8, lim8, S, tt):
    n = u_p.shape[0]
    nb, nt = n // S, S // tt
    ng = tt // 8

    def body(dy_ref, u_ref, st_ref, stp_ref, fcr_ref, fci_ref, air_ref, aii_ref, bm_ref, cm_ref, d_ref, lre_ref, lim_ref,
             du_ref, dcm_ref, dbm_ref, dd_ref, dlr_ref, dli_ref, are, aim, accr, acci, sp, ab):
        b = pl.program_id(0)
        j = pl.program_id(1)
        jt = nt - 1 - j

        @pl.when((b == 0) & (j == 0))
        def _():
            dcm_ref[...] = jnp.zeros_like(dcm_ref)
            dbm_ref[...] = jnp.zeros_like(dbm_ref)
            dd_ref[...] = jnp.zeros_like(dd_ref)
            accr[...] = jnp.zeros_like(accr)
            acci[...] = jnp.zeros_like(acci)

        @pl.when(j == 0)
        def _():
            are[...] = air_ref[...]
            aim[...] = aii_ref[...]

        sp[8:tt + 8, :] = st_ref[...].astype(F32)

        @pl.when(jt == 0)
        def _():
            sp[0:8, 0:NST] = fcr_ref[...]
            sp[0:8, NST:2 * NST] = fci_ref[...]

        @pl.when(jt != 0)
        def _():
            sp[0:8, :] = stp_ref[8:16, :].astype(F32)

        dy = dy_ref[...]
        u = u_ref[...]
        dyb = dy.astype(BF16)
        _bd_expand_t(dyb, cm_ref, ab)
        lre, lim = lre_ref[...], lim_ref[...]

        def step(i, c):
            ar, ai = c
            off = pl.multiple_of((ng - 1 - i) * 8, 8)
            nr = lre * ar + lim * ai + ab[pl.ds(off, 8), 0:NST]
            ni = lre * ai - lim * ar + ab[pl.ds(off, 8), NST:2 * NST]
            ab[pl.ds(off, 8), 0:NST] = nr
            ab[pl.ds(off, 8), NST:2 * NST] = ni
            pr = sp[pl.ds(off, 8), 0:NST]
            pi = sp[pl.ds(off, 8), NST:2 * NST]
            accr[...] += nr * pr + ni * pi
            acci[...] += ni * pr - nr * pi
            return nr, ni

        ar, ai = lax.fori_loop(0, ng, step, (are[...], aim[...]))
        are[...] = ar
        aim[...] = ai
        a_b = ab[...].astype(BF16)
        du_ref[...] = _bd_project_t(a_b, bm_ref) + d_ref[...] * dy
        ub = u.astype(BF16)
        for q in range(4):
            for part in range(2):
                lo = part * NST + q * 4 * QB
                s_q = st_ref[:, lo:lo + 4 * QB]
                dcm_ref[lo:lo + 4 * QB, :] += _dot_tn(s_q, dyb[:, q * QB:(q + 1) * QB])
                dbm_ref[:, lo:lo + 4 * QB] += _dot_tn(ub[:, q * QB:(q + 1) * QB], a_b[:, lo:lo + 4 * QB])
        dd_ref[...] += _colsum(dy * u)

        @pl.when((b == nb - 1) & (j == nt - 1))
        def _():
            dlr_ref[...] = _colsum(accr[...])
            dli_ref[...] = _colsum(acci[...])

    row = lambda w: pl.BlockSpec((tt, w), lambda b, j: (b * nt + nt - 1 - j, 0))
    seq8 = pl.BlockSpec((8, NST), lambda b, j: (b, 0))
    prev = pl.BlockSpec((16, 2 * NST), lambda b, j: (jnp.maximum((b * nt + nt - 1 - j) * (tt // 16) - 1, 0), 0))
    const = lambda shape: pl.BlockSpec(shape, lambda b, j: (0, 0))
    return pl.pallas_call(
        body, name="ssm_bwd_b", grid=(nb, nt),
        out_shape=[jax.ShapeDtypeStruct((n, D_SSM), F32), jax.ShapeDtypeStruct((2 * NST, QB), F32),
                   jax.ShapeDtypeStruct((QB, 2 * NST), F32), jax.ShapeDtypeStruct((1, D_SSM), F32),
                   jax.ShapeDtypeStruct((1, NST), F32), jax.ShapeDtypeStruct((1, NST), F32)],
        in_specs=[row(D_SSM), row(D_SSM), row(2 * NST), prev, seq8, seq8, seq8, seq8, _VM, _VM, _VM, _VM, _VM],
        out_specs=[row(D_SSM), const((2 * NST, QB)), const((QB, 2 * NST)), const((1, D_SSM)),
                   const((1, NST)), const((1, NST))],
        scratch_shapes=[pltpu.VMEM((8, NST), F32)] * 4 + [pltpu.VMEM((tt + 8, 2 * NST), F32),
                                                          pltpu.VMEM((tt, 2 * NST), F32)],
        compiler_params=_cp(("arbitrary", "arbitrary")),
    )(dy, u_p, st, st, fcr, fci, air, aii, bm, cm, dvec, lre8, lim8)


def _rope(v, c, s1, s2):
    return v * c + _roll(v, -16) * s1 + _roll(v, 16) * s2


def _rope_t(dv, c, s1, s2):
    return dv * c + _roll(dv * s1, 16) + _roll(dv * s2, -16)


_SCALE = (QK_NOPE + QK_ROPE) ** -0.5
_LOG2E = 1.4426950408889634
_C2 = _SCALE * _LOG2E


def _attn_fwd(q, k, v, S, tq):
    n = q.shape[0]
    nb, nq = n // S, S // tq

    def body(q_ref, k_ref, v_ref, o_ref, lr_ref):
        qi = pl.program_id(2)
        qv = q_ref[...]

        def tile(j, c, diagonal):
            m, acc = c
            off = pl.multiple_of(j * tq, tq)
            s = _dot_nt(qv, k_ref[pl.ds(off, tq), :])
            if diagonal:
                rows = lax.broadcasted_iota(jnp.int32, s.shape, 0)
                cols = lax.broadcasted_iota(jnp.int32, s.shape, 1)
                s = jnp.where(cols <= rows, s, NEG)
            mn = jnp.maximum(m, jnp.max(s, axis=1, keepdims=True))
            p = jnp.exp2(s - mn)
            acc = jnp.exp2(m - mn) * acc + _dot(p.astype(BF16), v_ref[pl.ds(off, tq), :])
            return mn, acc

        init = (jnp.full((tq, 1), NEG, F32), jnp.zeros((tq, HP), F32))
        c = lax.fori_loop(0, qi, lambda j, c: tile(j, c, False), init)
        m, acc = tile(qi, c, True)
        l = acc[:, V_HEAD:V_HEAD + 1]
        vlane = lax.broadcasted_iota(jnp.int32, acc.shape, 1)
        o_ref[...] = jnp.where(vlane < V_HEAD, acc / l, 0.0).astype(BF16)
        lane = lax.broadcasted_iota(jnp.int32, (8, HP), 1)
        lse = jnp.broadcast_to(m + jnp.log(l) * _LOG2E, (tq, HP))
        lr_ref[...] = _rows_of(lse, jnp.where(lane == 0, 1.0, 0.0).astype(BF16))

    qs = pl.BlockSpec((tq, HP), lambda b, h, i: (b * nq + i, h))
    ks = pl.BlockSpec((S, HP), lambda b, h, i: (b, h))
    return pl.pallas_call(
        body, name="attn_fwd", grid=(nb, NH, nq),
        out_shape=[jax.ShapeDtypeStruct((n, NH * HP), BF16), jax.ShapeDtypeStruct((nb * NH * 8, S), F32)],
        in_specs=[qs, ks, ks], out_specs=[qs, pl.BlockSpec((8, tq), lambda b, h, i: (b * NH + h, i))],
        compiler_params=_cp(("parallel", "parallel", "arbitrary")),
    )(q, k, v)


def _rows_of(x, pick):
    x1 = x.astype(BF16)
    r1 = x - x1.astype(F32)
    x2 = r1.astype(BF16)
    x3 = (r1 - x2.astype(F32)).astype(BF16)
    return _dot_nt(pick, x1) + _dot_nt(pick, x2) + _dot_nt(pick, x3)


def _attn_bwd(q, k, v, dob, lrow, drow, S, tq):
    n = q.shape[0]
    nb, nq = n // S, S // tq

    def body(q_ref, k_ref, v_ref, do_ref, lr_ref, dr_ref, dqo_ref, dk_ref, dv_ref, dq_ref):
        kj = pl.program_id(2)

        @pl.when(kj == 0)
        def _():
            dq_ref[...] = jnp.zeros_like(dq_ref)

        kt = k_ref[...]
        vt = v_ref[...]

        def tile(i, c, diagonal):
            dk, dv = c
            off = pl.multiple_of(i * tq, tq)
            qv = q_ref[pl.ds(off, tq), :]
            dob = do_ref[pl.ds(off, tq), :]
            lr = lr_ref[0:1, pl.ds(off, tq)]
            dr = dr_ref[0:1, pl.ds(off, tq)]
            st = _dot_nt(kt, qv)
            dpt = _dot_nt(vt, dob)
            pt = jnp.exp2(st - lr)
            if diagonal:
                keys = lax.broadcasted_iota(jnp.int32, pt.shape, 0)
                qrys = lax.broadcasted_iota(jnp.int32, pt.shape, 1)
                pt = jnp.where(keys <= qrys, pt, 0.0)
            dst = (pt * (dpt - dr)).astype(BF16)
            dq_ref[pl.ds(off, tq), :] += _dot_tn(dst, kt)
            return dk + _dot(dst, qv), dv + _dot(pt.astype(BF16), dob)

        zero = jnp.zeros((tq, HP), F32)
        c = tile(kj, (zero, zero), True)
        dk, dv = lax.fori_loop(kj + 1, nq, lambda i, c: tile(i, c, False), c)
        dk_ref[...] = dk.astype(BF16)
        dv_ref[...] = dv.astype(BF16)

        @pl.when(kj == nq - 1)
        def _():
            dqo_ref[...] = dq_ref[...].astype(BF16)

    ts = pl.BlockSpec((tq, HP), lambda b, h, i: (b * nq + i, h))
    fs = pl.BlockSpec((S, HP), lambda b, h, i: (b, h))
    rs = pl.BlockSpec((8, S), lambda b, h, i: (b * NH + h, 0))
    return pl.pallas_call(
        body, name="attn_bwd", grid=(nb, NH, nq),
        out_shape=[jax.ShapeDtypeStruct((n, NH * HP), BF16)] * 3,
        in_specs=[fs, ts, ts, fs, rs, rs], out_specs=[fs, ts, ts],
        scratch_shapes=[pltpu.VMEM((S, HP), F32)],
        compiler_params=_cp(("parallel", "parallel", "arbitrary")),
    )(q, k, v, dob, lrow, drow)


def _p2(yssm, oattn, x, target, modp, gs, ga, w_out, g2, gf, w_ff1, w_ff2, S, tm):
    n = x.shape[0]
    tps = S // tm
    nb = n // S

    def body(ys_ref, oa_ref, x_ref, t_ref, mod_ref, gs_ref, ga_ref, wo_ref, g2_ref, gf_ref, w1_ref, w2_ref,
             yn_ref, h2_ref, dx1_ref, r_ref, da_ref, dff_ref, do_ref, dys_ref, doa_ref, dr_ref,
             accs_ref, accg_ref, accg3_ref):
        i = pl.program_id(0)
        sh2, sc2, gt2 = mod_ref[0, 3:4, :], mod_ref[0, 4:5, :], mod_ref[0, 5:6, :]
        fsh, fsc = mod_ref[0, 6:7, :], mod_ref[0, 7:8, :]
        yh, rs = _rms(ys_ref[...], D_SSM)
        oa = oa_ref[...].astype(F32)
        ah, ra_ = _rms(oa, D_ATTN)
        yn = jnp.concatenate([yh * gs_ref[...], ah * ga_ref[...]], axis=1).astype(BF16)
        yn_ref[...] = yn
        o = _dot(yn, wo_ref[...])
        x1 = x_ref[...] + mod_ref[0, 2:3, :] * o
        x1h, r2 = _rms(x1, D)
        g2_v = g2_ref[...]
        h2 = ((x1h * g2_v) * (1.0 + sc2) + sh2).astype(BF16)
        h2_ref[...] = h2
        a = _dot(h2, w1_ref[...])
        ra = jnp.maximum(a, 0.0)
        rb = (ra * ra).astype(BF16)
        r_ref[...] = rb
        ff = _dot(rb, w2_ref[...])
        x2 = x1 + gt2 * ff
        x2h, rf = _rms(x2, D)
        gf_v = gf_ref[...]
        outn = x2h * gf_v
        err = outn * (1.0 + fsc) + fsh - t_ref[...]
        dout = err * (1.0 / D)
        doutn = dout * (1.0 + fsc)
        dx2 = _rms_bwd(doutn * gf_v, x2h, rf, D)
        dff = (gt2 * dx2).astype(BF16)
        dff_ref[...] = dff
        dr = _dot_nt(dff, w2_ref[...])
        da = (dr * (2.0 * ra)).astype(BF16)
        da_ref[...] = da
        dh2 = _dot_nt(da, w1_ref[...])
        dn2 = dh2 * (1.0 + sc2)
        dx1 = dx2 + _rms_bwd(dn2 * g2_v, x1h, r2, D)
        dx1_ref[...] = dx1
        dob = (mod_ref[0, 2:3, :] * dx1).astype(BF16)
        do_ref[...] = dob
        dyn = _dot_nt(dob, wo_ref[...])
        d1 = dyn[:, 0:D_SSM]
        d2 = dyn[:, D_SSM:D_SSM + NH * HP]
        dys_ref[...] = _rms_bwd(d1 * gs_ref[...], yh, rs, D_SSM)
        doa = _rms_bwd(d2 * ga_ref[...], ah, ra_, D_ATTN)
        doa_ref[...] = doa.astype(BF16)
        prod = doa * oa
        ones = jnp.ones((8, HP), BF16)
        for h in range(NH):
            dr_ref[h * 8:(h + 1) * 8, :] = _rows_of(prod[:, h * HP:(h + 1) * HP], ones)

        @pl.when(i % tps == 0)
        def _():
            accs_ref[...] = jnp.zeros_like(accs_ref)

        @pl.when(i == 0)
        def _():
            accg_ref[...] = jnp.zeros_like(accg_ref)
            accg3_ref[...] = jnp.zeros_like(accg3_ref)

        accs_ref[0, 2:3, :] += _colsum(dx1 * o)
        accg3_ref[0:1, 0:D_SSM] += _colsum(d1 * yh)
        accg3_ref[1:2, :] += _colsum(d2 * ah)
        accs_ref[0, 3:4, :] += _colsum(dh2)
        accs_ref[0, 4:5, :] += _colsum(dh2 * (x1h * g2_v))
        accs_ref[0, 5:6, :] += _colsum(dx2 * ff)
        accs_ref[0, 6:7, :] += _colsum(dout)
        accs_ref[0, 7:8, :] += _colsum(dout * outn)
        accg_ref[0:1, :] += _colsum(dn2 * x1h)
        accg_ref[1:2, :] += _colsum(doutn * x2h)
        accg_ref[2:3, :] += _colsum(err * err) * (0.5 / D)

    row = lambda w: pl.BlockSpec((tm, w), lambda i: (i, 0))
    return pl.pallas_call(
        body, name="p2_mlp_loss", grid=(n // tm,),
        out_shape=[jax.ShapeDtypeStruct((n, D_SSM + NH * HP), BF16), jax.ShapeDtypeStruct((n, D), BF16),
                   jax.ShapeDtypeStruct((n, D), F32), jax.ShapeDtypeStruct((n, D_FF), BF16),
                   jax.ShapeDtypeStruct((n, D_FF), BF16), jax.ShapeDtypeStruct((n, D), BF16),
                   jax.ShapeDtypeStruct((n, D), BF16), jax.ShapeDtypeStruct((n, D_SSM), F32),
                   jax.ShapeDtypeStruct((n, NH * HP), BF16), jax.ShapeDtypeStruct((nb * NH * 8, S), F32),
                   jax.ShapeDtypeStruct((nb, 8, D), F32), jax.ShapeDtypeStruct((8, D), F32),
                   jax.ShapeDtypeStruct((8, NH * HP), F32)],
        in_specs=[row(D_SSM), row(NH * HP), row(D), row(D), pl.BlockSpec((1, 8, D), lambda i: (i // tps, 0, 0)),
                  _VM, _VM, _VM, _VM, _VM, _VM, _VM],
        out_specs=[row(D_SSM + NH * HP), row(D),
                   row(D), row(D_FF), row(D_FF), row(D),
                   row(D), row(D_SSM), row(NH * HP), pl.BlockSpec((NH * 8, tm), lambda i: (i // tps, i % tps)),
                   pl.BlockSpec((1, 8, D), lambda i: (i // tps, 0, 0)),
                   pl.BlockSpec((8, D), lambda i: (0, 0)), pl.BlockSpec((8, NH * HP), lambda i: (0, 0))],
        compiler_params=_cp(("arbitrary",)),
    )(yssm, oattn, x, target, modp, gs, ga, w_out, g2, gf, w_ff1, w_ff2)


def _wgrad(a, b, name, col_slots=0):
    n, k1 = a.shape
    k2 = b.shape[1]
    bn = next((b for b in (4096, 2048, 1024, 512) if n % b == 0), n)
    bk1 = next((b for b in (1024, 512) if k1 % b == 0), k1)
    bk2 = k2 // col_slots if col_slots else (1024 if (k2 % 1024 == 0) else k2)

    def body(a_ref, b_ref, o_ref):
        @pl.when(pl.program_id(2) == 0)
        def _():
            o_ref[...] = jnp.zeros_like(o_ref)

        o_ref[...] += _dot_tn(a_ref[...], b_ref[...]).reshape(o_ref.shape)

    if col_slots:
        out_shape = jax.ShapeDtypeStruct((col_slots, k1, bk2), F32)
        out_spec = pl.BlockSpec((1, bk1, bk2), lambda i, j, t: (j, i, 0))
    else:
        out_shape = jax.ShapeDtypeStruct((k1, k2), F32)
        out_spec = pl.BlockSpec((bk1, bk2), lambda i, j, t: (i, j))
    return pl.pallas_call(
        body, name=name, grid=(k1 // bk1, k2 // bk2, n // bn),
        out_shape=out_shape,
        in_specs=[pl.BlockSpec((bn, bk1), lambda i, j, t: (t, i)), pl.BlockSpec((bn, bk2), lambda i, j, t: (t, j))],
        out_specs=out_spec,
        compiler_params=_cp(("parallel", "parallel", "arbitrary")),
    )(a, b)


def _row_block(rows):
    if rows <= 256:
        return rows
    return next(b for b in (256, 192, 128, 64, 32, 16, 8) if rows % b == 0)


def _add_half(g, recv, cidx, name):
    _, rows2, w = g.shape
    rows = rows2 // 2
    br = _row_block(rows)
    nblk = rows // br

    def body(c_ref, g_ref, r_ref, o_ref):
        o_ref[...] = (g_ref[...] + r_ref[...]).astype(BF16)

    return pl.pallas_call(
        body, name=name,
        grid_spec=pltpu.PrefetchScalarGridSpec(
            num_scalar_prefetch=1, grid=(4, nblk),
            in_specs=[pl.BlockSpec((1, br, w), lambda s, i, c: (s, c[0] * nblk + i, 0)),
                      pl.BlockSpec((1, br, w), lambda s, i, c: (s, i, 0))],
            out_specs=pl.BlockSpec((1, br, w), lambda s, i, c: (s, i, 0))),
        out_shape=jax.ShapeDtypeStruct((4, rows, w), BF16),
        compiler_params=_cp(("parallel", "parallel")),
    )(cidx, g, recv)


def _add_chips(r, name):
    _, rows, w = r.shape
    br = _row_block(rows)

    def body(r_ref, o_ref):
        f = lambda k: r_ref[k].astype(F32)
        o_ref[...] = ((f(0) + f(1)) + f(2)) + f(3)

    return pl.pallas_call(
        body, name=name, grid=(rows // br,),
        out_shape=jax.ShapeDtypeStruct((rows, w), F32),
        in_specs=[pl.BlockSpec((4, br, w), lambda i: (0, i, 0))],
        out_specs=pl.BlockSpec((br, w), lambda i: (i, 0)),
        compiler_params=_cp(("parallel",)),
    )(r)


def _pair_sum(a, sa, b, sb):
    def body(a_ref, sa_ref, b_ref, sb_ref, oa_ref, ob_ref):
        oa_ref[...] = (a_ref[...].astype(F32) + sa_ref[...].astype(F32)).astype(BF16)
        ob_ref[...] = b_ref[...] + sb_ref[...]

    return pl.pallas_call(
        body, name="small_grad_pair_sum",
        out_shape=[jax.ShapeDtypeStruct(a.shape, BF16), jax.ShapeDtypeStruct(b.shape, F32)],
        in_specs=[_VM] * 4, out_specs=[_VM, _VM], compiler_params=_cp(),
    )(a, sa, b, sb)


def _sum_devices(a, b):
    def body(a_ref, b_ref, oa_ref, ob_ref):
        acc = a_ref[0:1, :].astype(F32)
        accb = b_ref[0:1, :]
        for k in range(1, a.shape[0]):
            acc = acc + a_ref[k:k + 1, :].astype(F32)
            accb = accb + b_ref[k:k + 1, :]
        oa_ref[...] = acc
        ob_ref[...] = accb

    return pl.pallas_call(
        body, name="small_grad_sum",
        out_shape=[jax.ShapeDtypeStruct((1, a.shape[1]), F32), jax.ShapeDtypeStruct((1, b.shape[1]), F32)],
        in_specs=[_VM, _VM], out_specs=[_VM, _VM], compiler_params=_cp(),
    )(a, b)


def _adamw_math(wv, gv, mv, vv):
    m_new = ADAM_B1 * mv + (1.0 - ADAM_B1) * gv
    v_new = ADAM_B2 * vv + (1.0 - ADAM_B2) * (gv * gv)
    m_hat = m_new / (1.0 - ADAM_B1 ** ADAM_STEP)
    v_hat = v_new / (1.0 - ADAM_B2 ** ADAM_STEP)
    return -ADAM_LR * (m_hat / (jnp.sqrt(v_hat) + ADAM_EPS) + ADAM_WD * wv), m_new, v_new


def _adamw_small(ws, gs, ms, vs):
    k = len(ws)

    def body(*refs):
        ins, outs = refs[:4 * k], refs[4 * k:]
        for t in range(k):
            d, m_new, v_new = _adamw_math(ins[t][...], ins[k + t][...], ins[2 * k + t][...], ins[3 * k + t][...])
            outs[t][...] = d
            outs[k + t][...] = m_new
            outs[2 * k + t][...] = v_new

    shapes = [jax.ShapeDtypeStruct(w.shape, F32) for w in ws]
    return pl.pallas_call(
        body, name="adamw_small", out_shape=shapes * 3,
        in_specs=[_VM] * (4 * k), out_specs=[_VM] * (3 * k), compiler_params=_cp(),
    )(*ws, *gs, *ms, *vs)


def _adamw(w, g, m, v, name):
    rows, wd = w.shape
    br = _row_block(rows)

    def body(w_ref, g_ref, m_ref, v_ref, d_ref, nm_ref, nv_ref):
        d, m_new, v_new = _adamw_math(w_ref[...], g_ref[...], m_ref[...], v_ref[...])
        d_ref[...] = d
        nm_ref[...] = m_new
        nv_ref[...] = v_new

    spec = pl.BlockSpec((br, wd), lambda i: (i, 0))
    return pl.pallas_call(
        body, name=name, grid=(rows // br,),
        out_shape=[jax.ShapeDtypeStruct((rows, wd), F32)] * 3,
        in_specs=[spec] * 4, out_specs=[spec] * 3,
        compiler_params=_cp(("parallel",)),
    )(w, g, m, v)


def _adamw_halves(w, mine, other, m, v, cidx, name):
    rows, wd = w.shape
    h = rows // 2
    br = _row_block(h)
    nblk = h // br

    def body(c_ref, w_ref, a_ref, b_ref, m_ref, v_ref, g_ref, d_ref, nm_ref, nv_ref):
        gv = jnp.where(pl.program_id(0) == c_ref[0], a_ref[...], b_ref[...])
        d, m_new, v_new = _adamw_math(w_ref[...], gv, m_ref[...], v_ref[...])
        g_ref[...] = gv
        d_ref[...] = d
        nm_ref[...] = m_new
        nv_ref[...] = v_new

    full = pl.BlockSpec((br, wd), lambda hf, i, c: (hf * nblk + i, 0))
    half = pl.BlockSpec((br, wd), lambda hf, i, c: (i, 0))
    return pl.pallas_call(
        body, name=name,
        grid_spec=pltpu.PrefetchScalarGridSpec(
            num_scalar_prefetch=1, grid=(2, nblk),
            in_specs=[full, half, half, full, full], out_specs=[full] * 4),
        out_shape=[jax.ShapeDtypeStruct((rows, wd), F32)] * 4,
        compiler_params=_cp(("parallel", "parallel")),
    )(cidx, w, mine, other, m, v)


def _other_chips(x, y):
    return [(1 - x, y), (x, 1 - y), (1 - x, 1 - y)]


def _other_devices(x, y, c):
    flip = lambda v, d: (1 - v) if d else v
    return [(flip(x, dx), flip(y, dy), flip(c, dc))
            for dx in (0, 1) for dy in (0, 1) for dc in (0, 1) if (dx, dy, dc) != (0, 0, 0)]


def _exchange(name, ins, out_shapes, n_local, n_remote, plan):
    ni, no = len(ins), len(out_shapes)

    def body(*refs):
        in_refs, out_refs = refs[:ni], refs[ni:ni + no]
        send_sems, recv_sems, local_sems = refs[ni + no:]
        x, y, c = lax.axis_index("x"), lax.axis_index("y"), lax.axis_index("c")
        local, remote = plan(in_refs, out_refs, x, y, c)
        assert len(local) == n_local and len(remote) == n_remote

        def push(k, src, dst, dev):
            return pltpu.make_async_remote_copy(src_ref=src, dst_ref=dst, send_sem=send_sems.at[k],
                                                recv_sem=recv_sems.at[k], device_id=dev, device_id_type=MESH)

        own = [pltpu.make_async_copy(s, d, local_sems.at[i]) for i, (s, d) in enumerate(local)]
        for cp in own:
            cp.start()
        sends = [push(k, s, d, dev) for k, (s, d, dev, _) in enumerate(remote)]
        for cp in sends:
            cp.start()
        for k, (s, _, dev, landing) in enumerate(remote):
            push(k, s, landing, dev).wait_recv()
        for cp in sends:
            cp.wait_send()
        for cp in own:
            cp.wait()

    return pl.pallas_call(
        body, name=name, out_shape=out_shapes,
        in_specs=[_ANY] * ni, out_specs=[_ANY] * no,
        scratch_shapes=[pltpu.SemaphoreType.DMA((n_remote,)), pltpu.SemaphoreType.DMA((n_remote,)),
                        pltpu.SemaphoreType.DMA((max(n_local, 1),))],
        compiler_params=pltpu.CompilerParams(has_side_effects=True),
    )(*ins)


def _gather_chips(name, shards, everyone=()):
    ns, ne = len(shards), len(everyone)
    outs = [jax.ShapeDtypeStruct((4,) + a.shape, a.dtype) for a in shards]
    outs += [jax.ShapeDtypeStruct((8,) + a.shape, a.dtype) for a in everyone]

    def plan(i, o, x, y, c):
        mine, me = 2 * x + y, 4 * x + 2 * y + c
        local, remote = [], []
        for t in range(ns):
            local.append((i[t], o[t].at[mine]))
            for px, py in _other_chips(x, y):
                remote.append((i[t], o[t].at[mine], (px, py, c), o[t].at[2 * px + py]))
        for t in range(ns, ns + ne):
            local.append((i[t], o[t].at[me]))
            for px, py, pc in _other_devices(x, y, c):
                remote.append((i[t], o[t].at[me], (px, py, pc), o[t].at[4 * px + 2 * py + pc]))
        return local, remote

    return _exchange(name, list(shards) + list(everyone), outs, ns + ne, 3 * ns + 7 * ne, plan)


_HBM = pl.BlockSpec(memory_space=pltpu.HBM)
_SEM = pl.BlockSpec(memory_space=pltpu.SEMAPHORE)
_EFFECT = pltpu.SideEffectType.DATAFLOW_SIDE_EFFECTING


def _split_start(name, ins, land_shapes, n_remote, plan, after):
    ni, nl = len(ins), len(land_shapes)
    srcs = [pltpu.with_memory_space_constraint(a, pltpu.HBM) for a in ins]
    lands = [pltpu.with_memory_space_constraint(lax.empty(s.shape, s.dtype), pltpu.HBM) for s in land_shapes]

    def body(*refs):
        src, land = refs[:ni], refs[ni:ni + nl]
        first = ni + nl + 1
        send, recv = refs[first:first + n_remote], refs[first + n_remote:first + 2 * n_remote]
        token = refs[first + 2 * n_remote + ni + nl]
        x, y, c = lax.axis_index("x"), lax.axis_index("y"), lax.axis_index("c")
        remote = plan(src, land, x, y, c)
        assert len(remote) == n_remote
        for k, (s, d, dev, _) in enumerate(remote):
            pltpu.make_async_remote_copy(src_ref=s, dst_ref=d, send_sem=send[k], recv_sem=recv[k],
                                         device_id=dev, device_id_type=MESH).start()
        token[...] = jnp.zeros_like(token)

    out = pl.pallas_call(
        body, name=name + "_start",
        out_shape=[pltpu.SemaphoreType.DMA(())] * (2 * n_remote)
                  + [pltpu.HBM(a.shape, a.dtype) for a in ins] + [pltpu.HBM(s.shape, s.dtype) for s in land_shapes]
                  + [jax.ShapeDtypeStruct((8, 128), F32)],
        in_specs=[_HBM] * (ni + nl) + [_ANY], out_specs=[_SEM] * (2 * n_remote) + [_HBM] * (ni + nl) + [_VM],
        input_output_aliases={t: 2 * n_remote + t for t in range(ni + nl)},
        compiler_params=pltpu.CompilerParams(has_side_effects=_EFFECT),
    )(*srcs, *lands, after)
    sems, thru = out[:2 * n_remote], out[2 * n_remote:2 * n_remote + ni + nl]
    return (name, sems, thru[:ni], thru[ni:], n_remote, plan), out[-1]


def _split_wait(handle, after):
    name, sems, srcs, lands, n_remote, plan = handle
    ni, nl = len(srcs), len(lands)

    def body(*refs):
        src, land = refs[:ni], refs[ni:ni + nl]
        send, recv = refs[ni + nl:ni + nl + n_remote], refs[ni + nl + n_remote:ni + nl + 2 * n_remote]
        x, y, c = lax.axis_index("x"), lax.axis_index("y"), lax.axis_index("c")
        for k, (s, _, dev, landing) in enumerate(plan(src, land, x, y, c)):
            cp = pltpu.make_async_remote_copy(src_ref=s, dst_ref=landing, send_sem=send[k], recv_sem=recv[k],
                                              device_id=dev, device_id_type=MESH)
            cp.wait_send()
            cp.wait_recv()

    out = pl.pallas_call(
        body, name=name + "_wait",
        out_shape=[pltpu.HBM(a.shape, a.dtype) for a in srcs] + [pltpu.HBM(a.shape, a.dtype) for a in lands],
        in_specs=[_HBM] * (ni + nl) + [_SEM] * (2 * n_remote) + [_ANY], out_specs=[_HBM] * (ni + nl),
        input_output_aliases={t: t for t in range(ni + nl)},
        compiler_params=pltpu.CompilerParams(has_side_effects=_EFFECT),
    )(*srcs, *lands, *sems, after)
    return out[:ni], out[ni:]


def _plan_to_chips(src, land, x, y, c):
    mine = 2 * x + y
    return [(src[t], land[t].at[mine], (px, py, c), land[t].at[2 * px + py])
            for t in range(len(src)) for px, py in _other_chips(x, y)]


def _plan_swap_halves(src, land, x, y, c):
    out = []
    for t in range(len(src)):
        h = src[t].shape[1] // 2
        out.append((src[t].at[:, pl.ds(pl.multiple_of((1 - c) * h, 8), h), :], land[t], (x, y, 1 - c), land[t]))
    return out


def _plan_scatter_chips(src, land, x, y, c):
    mine = 2 * x + y
    return [(src[t].at[2 * px + py], land[t].at[mine], (px, py, c), land[t].at[2 * px + py])
            for t in range(len(src)) for px, py in _other_chips(x, y)]


def _swap_halves(gs, everyone, whole):
    ns, ne, nw = len(gs), len(everyone), len(whole)
    outs = [jax.ShapeDtypeStruct((4, g.shape[1] // 2, g.shape[2]), g.dtype) for g in gs]
    outs += [jax.ShapeDtypeStruct((8,) + a.shape, a.dtype) for a in everyone]
    outs += [jax.ShapeDtypeStruct(a.shape, a.dtype) for a in whole]

    def plan(i, o, x, y, c):
        me = 4 * x + 2 * y + c
        local, remote = [], []
        for t in range(ns):
            h = gs[t].shape[1] // 2
            theirs = i[t].at[:, pl.ds(pl.multiple_of((1 - c) * h, 8), h), :]
            remote.append((theirs, o[t], (x, y, 1 - c), o[t]))
        for t in range(ns, ns + ne):
            local.append((i[t], o[t].at[me]))
            for px, py, pc in _other_devices(x, y, c):
                remote.append((i[t], o[t].at[me], (px, py, pc), o[t].at[4 * px + 2 * py + pc]))
        for t in range(ns + ne, ns + ne + nw):
            remote.append((i[t], o[t], (x, y, 1 - c), o[t]))
        return local, remote

    return _exchange("grad_swap_sibling", list(gs) + list(everyone) + list(whole), outs, ne, ns + 7 * ne + nw, plan)


def _scatter_chips(parts, per_chip):
    ns, ng = len(parts), len(per_chip)
    outs = [jax.ShapeDtypeStruct(a.shape, a.dtype) for a in parts]
    outs += [jax.ShapeDtypeStruct((4,) + a.shape, a.dtype) for a in per_chip]

    def plan(i, o, x, y, c):
        mine = 2 * x + y
        local, remote = [], []
        for t in range(ns):
            local.append((i[t].at[mine], o[t].at[mine]))
            for px, py in _other_chips(x, y):
                remote.append((i[t].at[2 * px + py], o[t].at[mine], (px, py, c), o[t].at[2 * px + py]))
        for t in range(ns, ns + ng):
            local.append((i[t], o[t].at[mine]))
            for px, py in _other_chips(x, y):
                remote.append((i[t], o[t].at[mine], (px, py, c), o[t].at[2 * px + py]))
        return local, remote

    return _exchange("grad_scatter_chips", list(parts) + list(per_chip), outs, ns + ng, 3 * (ns + ng), plan)


def _join_halves(halves):
    ns = len(halves)
    outs = [jax.ShapeDtypeStruct(a.shape, a.dtype) for a in halves]

    def plan(i, o, x, y, c):
        return [], [(i[t], o[t], (x, y, 1 - c), o[t]) for t in range(ns)]

    return _exchange("grad_join_sibling", list(halves), outs, 0, ns, plan)


def _pad_heads_cols(w, per, used):
    k = w.shape[0]
    w = w.reshape(k, NH, per)[:, :, :used]
    return jnp.pad(w, ((0, 0), (0, 0), (0, HP - used))).reshape(k, NH * HP)


def _unpad_heads_cols(w, used):
    k = w.shape[0]
    return w.reshape(k, NH, HP)[:, :, :used]


def _prep_weights(wf):
    bf = lambda a: a.astype(BF16)
    out = {}
    out["w_in"] = jnp.pad(bf(wf["w_in"]), ((0, 0), (0, IN_PAD - IN_COLS)))
    out["w_glu"] = bf(wf["w_glu"])
    out["w_uq"] = _pad_heads_cols(bf(wf["w_uq"]), QK_NOPE + QK_ROPE, QK_NOPE + QK_ROPE)
    wkv = bf(wf["w_ukv"]).reshape(KV_LORA, NH, QK_NOPE + V_HEAD)
    wk = jnp.pad(wkv[:, :, :QK_NOPE], ((0, 0), (0, 0), (0, HP - QK_NOPE))).reshape(KV_LORA, NH * HP)
    wv = jnp.pad(wkv[:, :, QK_NOPE:], ((0, 0), (0, 0), (0, HP - V_HEAD))).reshape(KV_LORA, NH * HP)
    out["w_ukv"] = jnp.concatenate([wk, wv], axis=1)
    return out


def _prep_late_weights(wf):
    bf = lambda a: a.astype(BF16)
    out = {}
    wo = bf(wf["w_out"])
    wo_a = jnp.pad(wo[D_SSM:].reshape(NH, V_HEAD, D), ((0, 0), (0, HP - V_HEAD), (0, 0))).reshape(NH * HP, D)
    out["w_out"] = jnp.concatenate([wo[:D_SSM], wo_a], axis=0)
    out["w_ff1"] = bf(wf["w_ff1"])
    out["w_ff2"] = bf(wf["w_ff2"])
    return out


def _rope_tables(positions):
    inv_freq = ROPE_BASE ** (-jnp.arange(0, QK_ROPE, 2, dtype=F32) / QK_ROPE)
    ang = positions.astype(F32)[:, None] * inv_freq
    cos, sin = jnp.cos(ang), jnp.sin(ang)
    n = positions.shape[0]
    one = jnp.ones((n, QK_NOPE), F32)
    z16 = jnp.zeros((n, 16), F32)
    z32 = jnp.zeros((n, 32), F32)
    z64 = jnp.zeros((n, QK_NOPE), F32)
    rc = jnp.concatenate([one, cos, cos, z32], axis=1)
    rs1 = jnp.concatenate([z64, -sin, z16, z32], axis=1)
    rs2 = jnp.concatenate([z64, z16, sin, z32], axis=1)
    return rc, rs1, rs2


def _permute_rows(a, S):
    n, w = a.shape
    return a.reshape(n // S, 8, S // 8, w).transpose(0, 2, 1, 3).reshape(n, w)


def _unpermute_rows(a, S):
    n, w = a.shape
    return a.reshape(n // S, S // 8, 8, w).transpose(0, 2, 1, 3).reshape(n, w)


def _block_diag_in(bb):
    eye = jnp.eye(8, dtype=bb.dtype)
    blocks = jnp.einsum("qgph,gk->qghkp", bb.reshape(4, 8, P, H), eye).reshape(4, QB, QS)
    return blocks.transpose(1, 0, 2).reshape(QB, NST)


def _block_diag_out(cc):
    eye = jnp.eye(8, dtype=cc.dtype)
    return jnp.einsum("qghp,gk->qgpkh", cc.reshape(4, 8, H, P), eye).reshape(NST, QB)


def _slots(full):
    r, cdim = full.shape
    return full.reshape(r, 4, cdim // 4).transpose(1, 0, 2)


def _unslots(g):
    s, r, cs = g.shape
    return g.transpose(1, 0, 2).reshape(r, s * cs)


def _local_step(x, positions, target, modp, wf, late_weights=None, reducer=None):
    nb, S, _ = x.shape
    n = nb * S
    tm = min(256, S)
    tr = min(512, S)
    tt = min(512, S)
    tq = min(512, S // 2)
    kw = _prep_weights(wf)
    row = lambda a: a.reshape(1, -1).astype(F32)

    xf = x.reshape(n, D)
    tf = target.reshape(n, D)
    g1, g2, gf = row(wf["norm1_g"]), row(wf["norm2_g"]), row(wf["final_norm_g"])
    rc, rs1, rs2 = _rope_tables(positions.reshape(n))
    gq, gkv = row(wf["q_norm_g"]), row(wf["kv_norm_g"])
    h1, u, lat, q, k, v, qn, kvn = _f1_fwd(xf, modp, g1, kw["w_in"], rc, rs1, rs2, gq, gkv,
                                           kw["w_uq"], kw["w_ukv"], S, tr)

    col = lambda a: a.reshape(NST, 1)
    lam_re, lam_im = col(wf["ssm_lambda_re"]), col(wf["ssm_lambda_im"])
    logdt = jnp.repeat(wf["ssm_log_dt"].reshape(G, 1), P, axis=1).reshape(NST, 1)
    b_re, b_im = wf["ssm_b_re"].reshape(NST, H), wf["ssm_b_im"].reshape(NST, H)
    lbr, lbi, bbr, bbi = _ssm_param_fwd(lam_re, lam_im, logdt, b_re, b_im)
    lre8 = jnp.broadcast_to(lbr.reshape(1, NST), (8, NST))
    lim8 = jnp.broadcast_to(lbi.reshape(1, NST), (8, NST))
    bm = jnp.concatenate([_block_diag_in(bbr.reshape(G, P, H)), _block_diag_in(bbi.reshape(G, P, H))],
                         axis=1).astype(BF16)
    cm = jnp.concatenate([_block_diag_out(wf["ssm_c_re"]), -_block_diag_out(wf["ssm_c_im"])], axis=0).astype(BF16)
    dvec = row(wf["ssm_d"])
    u_p = _permute_rows(u, S)
    fcr, fci = _ssm_local(u_p, bm, lre8, lim8, S, tt)
    st, ypre, z, gact, yssm_p = _ssm_fwd(u_p, fcr, fci, bm, cm, dvec, kw["w_glu"], lre8, lim8, S, tt)
    yssm = _unpermute_rows(yssm_p, S)

    oattn, lrow = _attn_fwd(q, k, v, S, tq)

    gs = row(wf["ssm_out_g"])
    ga = jnp.pad(wf["attn_out_g"].reshape(NH, V_HEAD), ((0, 0), (0, HP - V_HEAD))).reshape(1, NH * HP)
    kw.update(_prep_late_weights(late_weights(oattn) if late_weights is not None else wf))
    (yn, h2, dx1, r, da, dff, do, dyssm, dob, drow, accs2, accg2, accg3) = _p2(
        yssm, oattn, xf, tf, modp, gs, ga, kw["w_out"], g2, gf, kw["w_ff1"], kw["w_ff2"], S, tm)
    loss = accg2[2:3]
    g_ff1 = _wgrad(h2, da, "wgrad_ff1", col_slots=4)
    g_ff2 = _wgrad(r, dff, "wgrad_ff2").reshape(4, D_FF // 4, D)
    gwo = _wgrad(yn, do, "wgrad_out")
    g_out = jnp.concatenate([gwo[:D_SSM].reshape(2, D_SSM // 2, D),
                             gwo[D_SSM:].reshape(2, NH // 2 * HP, D).reshape(2, NH // 2, HP, D)[:, :, :V_HEAD]
                             .reshape(2, D_ATTN // 2, D)], axis=0)
    lre8_b = lre8
    if reducer is not None:
        drow = drow + reducer.start([g_ff1, g_ff2, g_out])[0, 0]

    dq, dk, dv = _attn_bwd(q, k, v, dob, lrow, drow, S, tq)
    if reducer is not None:
        lre8_b = lre8 + reducer.middle(dq)[0, 0]

    dys_p = _permute_rows(dyssm, S)
    dy, dz, air, aii = _ssm_bwd_a(dys_p, z, ypre, kw["w_glu"], cm, lre8_b, lim8, S, tt)
    du_p, dcm, dbm, dd, dlr, dli = _ssm_bwd_b(dy, u_p, st, fcr, fci, air, aii, bm, cm, dvec, lre8, lim8, S, tt)
    du = _unpermute_rows(du_p, S)
    dcm = dcm.reshape(2, 4, 8, P, 8, H)
    dc_re = jnp.einsum("qgpgh->qghp", dcm[0]).reshape(G, H, P)
    dc_im = -jnp.einsum("qgpgh->qghp", dcm[1]).reshape(G, H, P)
    dbm = dbm.reshape(8, H, 2, 4, 8, P)
    dbb_re = jnp.einsum("ghqgp->qgph", dbm[:, :, 0]).reshape(NST, H)
    dbb_im = jnp.einsum("ghqgp->qgph", dbm[:, :, 1]).reshape(NST, H)
    gb_re, gb_im, glr, gli, gdt = _ssm_param_bwd(lam_re, lam_im, logdt, b_re, b_im, dlr.reshape(NST, 1),
                                                 dli.reshape(NST, 1), dbb_re, dbb_im)
    glogdt = _rowsum(gdt.reshape(G, P))

    dx, dproj, dqb, dkvb, accs1, accg1, accm = _f1_bwd(du, dq, dk, dv, lat, rc, rs1, rs2, gq, gkv, kw["w_uq"],
                                                       kw["w_ukv"], dx1, xf, modp, g1, kw["w_in"], S, tr)

    big = {}
    big["w_in"] = _slots(_wgrad(h1, dproj, "wgrad_in")[:, :IN_COLS])
    big["w_glu"] = _wgrad(gact, dz, "wgrad_glu", col_slots=4)
    big["w_uq"] = _slots(_unpad_heads_cols(_wgrad(qn, dqb, "wgrad_uq"), QK_NOPE + QK_ROPE).reshape(Q_LORA, -1))
    gkvw = _wgrad(kvn, dkvb, "wgrad_ukv")
    big["w_ukv"] = _slots(jnp.concatenate([_unpad_heads_cols(gkvw[:, :NH * HP], QK_NOPE),
                                           _unpad_heads_cols(gkvw[:, NH * HP:], V_HEAD)], axis=2).reshape(KV_LORA, -1))
    big["w_out"] = g_out
    big["w_ff1"] = g_ff1
    big["w_ff2"] = g_ff2

    small = {}
    small["norm1_g"] = accg1[0:1]
    small["norm2_g"] = accg2[0:1]
    small["final_norm_g"] = accg2[1:2]
    small["ssm_out_g"] = accg3[0:1, :D_SSM]
    small["attn_out_g"] = accg3[1].reshape(NH, HP)[:, :V_HEAD].reshape(1, D_ATTN)
    small["q_norm_g"] = accm[0:1, :Q_LORA]
    small["kv_norm_g"] = accm[1:2, :KV_LORA]
    small["ssm_lambda_re"] = glr.reshape(G, P)
    small["ssm_lambda_im"] = gli.reshape(G, P)
    small["ssm_b_re"] = gb_re
    small["ssm_b_im"] = gb_im
    small["ssm_c_re"] = dc_re.reshape(G * H, P)
    small["ssm_c_im"] = dc_im.reshape(G * H, P)
    small["ssm_d"] = dd.reshape(G, H)
    small["ssm_log_dt"] = glogdt.reshape(1, G)
    return loss, dx.reshape(nb, S, D), big, small, accs1 + accs2


def _view2d(a):
    return a.reshape(-1, a.shape[-1]) if a.ndim > 1 else a.reshape(1, -1)


def kernel(x, c, positions, ada_w, ada_b, norm1_g, w_in, ssm_lambda_re, ssm_lambda_im, ssm_b_re, ssm_b_im, ssm_c_re, ssm_c_im, ssm_d, ssm_log_dt, w_glu, q_norm_g, w_uq, kv_norm_g, w_ukv, ssm_out_g, attn_out_g, w_out, norm2_g, w_ff1, w_ff2, final_ada_w, final_ada_b, final_norm_g, loss_target, m_ada_w, m_ada_b, m_norm1_g, m_w_in, m_ssm_lambda_re, m_ssm_lambda_im, m_ssm_b_re, m_ssm_b_im, m_ssm_c_re, m_ssm_c_im, m_ssm_d, m_ssm_log_dt, m_w_glu, m_q_norm_g, m_w_uq, m_kv_norm_g, m_w_ukv, m_ssm_out_g, m_attn_out_g, m_w_out, m_norm2_g, m_w_ff1, m_w_ff2, m_final_ada_w, m_final_ada_b, m_final_norm_g, v_ada_w, v_ada_b, v_norm1_g, v_w_in, v_ssm_lambda_re, v_ssm_lambda_im, v_ssm_b_re, v_ssm_b_im, v_ssm_c_re, v_ssm_c_im, v_ssm_d, v_ssm_log_dt, v_w_glu, v_q_norm_g, v_w_uq, v_kv_norm_g, v_w_ukv, v_ssm_out_g, v_attn_out_g, v_w_out, v_norm2_g, v_w_ff1, v_w_ff2, v_final_ada_w, v_final_ada_b, v_final_norm_g):
    args = dict(locals())
    names = list(inspect.signature(kernel).parameters)
    wnames = names[3:names.index("loss_target")]
    small_names = [nm for nm in wnames if nm not in GATHERED and nm not in TP]
    reduced_names = [nm for nm in small_names if nm not in ("ada_b", "final_ada_b")]
    w = {nm: args[nm] for nm in wnames}
    m = {nm: args["m_" + nm] for nm in wnames}
    v = {nm: args["v_" + nm] for nm in wnames}
    nb = x.shape[0]
    xi, yi, ci = lax.axis_index("x"), lax.axis_index("y"), lax.axis_index("c")
    chip, me = 2 * xi + yi, 4 * xi + 2 * yi + ci

    unslot = lambda nm, g: g.reshape(-1, g.shape[-1]) if nm in ROW_SHARDED else _unslots(g)
    early = [nm for nm in GATHERED if nm not in LATE]
    got = _gather_chips("gather_weights", [_view2d(w[nm]).astype(BF16) for nm in early], [c])
    wf = {nm: unslot(nm, g) for nm, g in zip(early, got)}
    for nm in small_names:
        wf[nm] = w[nm][0] if w[nm].ndim > 1 else w[nm]
    c_all = got[len(early)].reshape(8 * nb, D)

    na, nf = ada_w.shape[-1], final_ada_w.shape[-1]
    ada_b_s = lax.dynamic_slice(ada_b, (0, chip * na), (1, na))
    fada_b_s = lax.dynamic_slice(final_ada_b.reshape(1, -1), (0, chip * nf), (1, nf))
    cond_all, modcols = _mod_fwd(c_all, ada_w[0], ada_b_s, final_ada_w, fada_b_s)
    (mod_g,) = _gather_chips("gather_mod", [modcols])
    mine = lax.dynamic_slice(mod_g, (0, me * nb, 0), (4, nb, na + nf))
    modp = jnp.concatenate([mine[:, :, :na].transpose(1, 0, 2).reshape(nb, 6, D),
                            mine[:, :, na:].transpose(1, 0, 2).reshape(nb, 2, D)], axis=1)

    own_late = [_view2d(w[nm]).astype(BF16) for nm in LATE]
    late_gather, token = _split_start("gather_late", own_late,
                                      [jax.ShapeDtypeStruct((4,) + a.shape, a.dtype) for a in own_late],
                                      3 * len(LATE), _plan_to_chips, modp)
    modp = modp + token[0, 0]

    def late_weights(after):
        sent, landed = _split_wait(late_gather, after)
        return {nm: unslot(nm, lax.dynamic_update_slice(g, own[None], (chip, 0, 0)))
                for nm, g, own in zip(LATE, landed, sent)}

    cidx = ci.astype(jnp.int32).reshape(1)
    ahead = ["w_ff1", "w_ff2", "w_out"]

    class Reducer:
        def start(self, gs):
            lands = [jax.ShapeDtypeStruct((4, g.shape[1] // 2, g.shape[2]), g.dtype) for g in gs]
            self.swap, tok = _split_start("grad_swap_ff", gs, lands, len(gs), _plan_swap_halves, modp)
            return tok

        def middle(self, after):
            gs, got = _split_wait(self.swap, after)
            sums = [_add_half(g, r, cidx, "grad_add_sibling_" + nm) for nm, g, r in zip(ahead, gs, got)]
            lands = [jax.ShapeDtypeStruct(s.shape, s.dtype) for s in sums]
            self.scatter, tok = _split_start("grad_scatter_ff", sums, lands, 3 * len(sums), _plan_scatter_chips, modp)
            return tok

        def finish(self, after):
            out = []
            for nm, s, l in zip(ahead, *_split_wait(self.scatter, after)):
                own = lax.dynamic_slice(s, (chip, 0, 0), (1,) + s.shape[1:])
                out.append(_add_chips(lax.dynamic_update_slice(l, own, (chip, 0, 0)), "grad_add_chips_" + nm))
            return out

    reducer = Reducer()
    loss_row, grad_x, big, small, dmodp = _local_step(x, positions, loss_target, modp, wf, late_weights, reducer)

    rest = [nm for nm in GATHERED if nm not in ahead]
    sizes = [small[nm].size for nm in reduced_names]
    pad = -sum(sizes) % 128
    packed = jnp.concatenate([small[nm].reshape(1, -1) for nm in reduced_names] + [jnp.zeros((1, pad), F32)],
                             axis=1).astype(BF16)
    swapped = _swap_halves([big[nm] for nm in rest], [dmodp.reshape(nb, 8 * D)], [packed, loss_row])
    chip_sums = [_add_half(big[nm], r, cidx, "grad_add_sibling_" + nm) for nm, r in zip(rest, swapped)]
    chip_small = _pair_sum(packed, swapped[len(rest) + 1], loss_row, swapped[len(rest) + 2])
    scattered = _scatter_chips(chip_sums, chip_small)
    half_of = {nm: _add_chips(r, "grad_add_chips_" + nm) for nm, r in zip(rest, scattered)}
    half_of.update(zip(ahead, reducer.finish(grad_x)))
    halves = [half_of[nm] for nm in GATHERED]
    others = _join_halves(halves)
    grads = {}
    dmod_all = swapped[len(rest)].reshape(8 * nb, 8 * D)
    small_sum, loss_sum = _sum_devices(scattered[len(rest)].reshape(4, -1), scattered[len(rest) + 1].reshape(4, -1))
    loss = jnp.sum(loss_sum)
    off = 0
    for nm, sz in zip(reduced_names, sizes):
        grads[nm] = small_sum[:, off:off + sz].reshape(small[nm].shape)
        off += sz

    dsl = jnp.concatenate([lax.dynamic_slice(dmod_all, (0, chip * na), (8 * nb, na)),
                           lax.dynamic_slice(dmod_all, (0, 6 * D + chip * nf), (8 * nb, nf))], axis=1)
    gw, gb = _mod_bwd(cond_all.T, dsl, dmod_all)
    grads["ada_w"], grads["final_ada_w"] = gw[:, :na], gw[:, na:]
    grads["ada_b"], grads["final_ada_b"] = gb[:, :6 * D], gb[:, 6 * D:]

    delta, new_m, new_v = {}, {}, {}
    for nm, mine_h, other_h in zip(GATHERED, halves, others):
        grads[nm], delta[nm], new_m[nm], new_v[nm] = _adamw_halves(
            _view2d(w[nm]), mine_h, other_h, _view2d(m[nm]), _view2d(v[nm]), cidx, "adamw_" + nm)
    for nm in TP:
        delta[nm], new_m[nm], new_v[nm] = _adamw(_view2d(w[nm]), grads[nm], _view2d(m[nm]), _view2d(v[nm]),
                                                  "adamw_" + nm)
    upd = _adamw_small([_view2d(w[nm]) for nm in small_names], [grads[nm] for nm in small_names],
                       [_view2d(m[nm]) for nm in small_names], [_view2d(v[nm]) for nm in small_names])
    k = len(small_names)
    for t, nm in enumerate(small_names):
        delta[nm], new_m[nm], new_v[nm] = upd[t], upd[k + t], upd[2 * k + t]

    outs = [grads, delta, new_m, new_v]
    return (loss, grad_x, *[d[nm].reshape(w[nm].shape) for d in outs for nm in wnames])
```

```python
import inspect
import math

import jax
import jax.numpy as jnp
from jax import lax
from jax.experimental import pallas as pl
from jax.experimental.pallas import tpu as pltpu

F32 = jnp.float32
BF16 = jnp.bfloat16

D = 1024
D_SSM = 512
G = 32
H = 16
P = 64
NST = G * P
D_ATTN = 512
NH = 8
QK_NOPE = 64
QK_ROPE = 32
V_HEAD = 64
HP = 128
Q_LORA = 384
KV_LORA = 256
IN_COLS = D_SSM + Q_LORA + KV_LORA + QK_ROPE
IN_PAD = 1280
D_FF = 4096
ROPE_BASE = 10000.0
EPS = 1e-6
ADAM_LR = 0.001
ADAM_B1 = 0.9
ADAM_B2 = 0.999
ADAM_EPS = 1e-08
ADAM_WD = 0.01
ADAM_STEP = 10
NEG = -1e30
VMEM_LIMIT = 60 << 20

MESH = pl.DeviceIdType.MESH
_VM = pl.BlockSpec(memory_space=pltpu.VMEM)
_ANY = pl.BlockSpec(memory_space=pl.ANY)

GATHERED = ["w_in", "w_glu", "w_uq", "w_ukv", "w_out", "w_ff1", "w_ff2"]
TP = ["ada_w", "final_ada_w"]
ROW_SHARDED = ("w_out", "w_ff2")
LATE = ["w_out", "w_ff1", "w_ff2"]


def _cp(sem=None, vmem=VMEM_LIMIT):
    kw = dict(vmem_limit_bytes=vmem)
    if sem is not None:
        kw["dimension_semantics"] = sem
    return pltpu.CompilerParams(**kw)


def _dot(a, b):
    return jnp.dot(a, b, preferred_element_type=F32)


def _dot_nt(a, b):
    return lax.dot_general(a, b, (((1,), (1,)), ((), ())), preferred_element_type=F32)


def _dot_tn(a, b):
    return lax.dot_general(a, b, (((0,), (0,)), ((), ())), preferred_element_type=F32)


def _rms(x, n):
    r = lax.rsqrt(jnp.sum(x * x, axis=-1, keepdims=True) * (1.0 / n) + EPS)
    return x * r, r


def _rms_bwd(dyg, xhat, r, n):
    return r * (dyg - xhat * (jnp.sum(dyg * xhat, axis=-1, keepdims=True) * (1.0 / n)))


def _sigmoid(x):
    return 1.0 / (1.0 + jnp.exp(-x))


_GK = math.sqrt(2.0 / math.pi)
_GC = 0.044715


def _gelu(y):
    t = jnp.tanh(_GK * (y + _GC * y * y * y))
    return 0.5 * y * (1.0 + t)


def _gelu_grad(y):
    t = jnp.tanh(_GK * (y + _GC * y * y * y))
    return 0.5 * (1.0 + t) + 0.5 * y * (1.0 - t * t) * _GK * (1.0 + 3.0 * _GC * y * y)


def _colsum(x):
    return jnp.sum(x, axis=0, keepdims=True)


def _roll(x, s):
    return pltpu.roll(x, s % x.shape[-1], x.ndim - 1)


def _mod_fwd(c_all, ada_w_s, ada_b_s, fada_w_s, fada_b_s):
    nseq = c_all.shape[0]
    na, nf = ada_w_s.shape[1], fada_w_s.shape[1]

    def body(c_ref, w_ref, b_ref, fw_ref, fb_ref, cond_ref, mod_ref):
        cv = c_ref[...]
        cond = cv * _sigmoid(cv)
        cond_ref[...] = cond
        cb = cond.astype(BF16)
        mod_ref[:, 0:na] = _dot(cb, w_ref[...].astype(BF16)) + b_ref[...]
        mod_ref[:, na:na + nf] = _dot(cb, fw_ref[...].astype(BF16)) + fb_ref[...]

    return pl.pallas_call(
        body, name="mod_fwd",
        out_shape=[jax.ShapeDtypeStruct((nseq, D), F32), jax.ShapeDtypeStruct((nseq, na + nf), F32)],
        in_specs=[_VM] * 5, out_specs=[_VM] * 2, compiler_params=_cp(),
    )(c_all, ada_w_s, ada_b_s, fada_w_s, fada_b_s)


def _mod_bwd(cond_t, dsl, dall):
    nseq, n = dsl.shape
    bc = 512

    def body(ct_ref, dm_ref, da_ref, gw_ref, gb_ref):
        ct = ct_ref[...]
        dm = dm_ref[...]
        acc = ct[:, 0:1] * dm[0:1, :]
        for b in range(1, nseq):
            acc = acc + ct[:, b:b + 1] * dm[b:b + 1, :]
        gw_ref[...] = acc

        @pl.when(pl.program_id(0) == 0)
        def _():
            gb_ref[...] = _colsum(da_ref[...])

    return pl.pallas_call(
        body, name="mod_bwd", grid=(n // bc,),
        out_shape=[jax.ShapeDtypeStruct((D, n), F32), jax.ShapeDtypeStruct((1, dall.shape[1]), F32)],
        in_specs=[_VM, pl.BlockSpec((nseq, bc), lambda i: (0, i)), _VM],
        out_specs=[pl.BlockSpec((D, bc), lambda i: (0, i)), pl.BlockSpec((1, dall.shape[1]), lambda i: (0, 0))],
        compiler_params=_cp(("arbitrary",)),
    )(cond_t, dsl, dall)


def _f1_fwd(x, modp, g1, w_in, rc, rs1, rs2, gq, gkv, w_uq, w_ukv, S, tm):
    n = x.shape[0]
    tps = S // tm
    LAT = IN_PAD - D_SSM

    def body(x_ref, mod_ref, g_ref, w_ref, c_ref, s1_ref, s2_ref, gq_ref, gkv_ref, wq_ref, wkv_ref,
             h_ref, u_ref, lat_ref, q_ref, k_ref, v_ref, qn_ref, kvn_ref):
        xhat, _ = _rms(x_ref[...], D)
        h = (xhat * g_ref[...]) * (1.0 + mod_ref[0, 1:2, :]) + mod_ref[0, 0:1, :]
        hb = h.astype(BF16)
        h_ref[...] = hb
        proj = _dot(hb, w_ref[...])
        u_ref[...] = proj[:, 0:D_SSM]
        lat_ref[...] = proj[:, D_SSM:IN_PAD]
        c, s1, s2 = c_ref[...], s1_ref[...], s2_ref[...]
        qhat, _ = _rms(proj[:, D_SSM:D_SSM + Q_LORA], Q_LORA)
        qn = (qhat * gq_ref[...]).astype(BF16)
        qn_ref[...] = qn
        q = _dot(qn, wq_ref[...])
        qr = _rope(q, jnp.tile(c, (1, NH)), jnp.tile(s1, (1, NH)), jnp.tile(s2, (1, NH)))
        q_ref[...] = (qr * _C2).astype(BF16)
        khat, _ = _rms(proj[:, D_SSM + Q_LORA:D_SSM + Q_LORA + KV_LORA], KV_LORA)
        kvn = (khat * gkv_ref[...]).astype(BF16)
        kvn_ref[...] = kvn
        kv = _dot(kvn, wkv_ref[...])
        kr = _rope(_roll(proj[:, IN_PAD - HP:IN_PAD], 64), c, s1, s2)
        k_ref[...] = (kv[:, 0:NH * HP] + jnp.tile(kr, (1, NH))).astype(BF16)
        vv = kv[:, NH * HP:2 * NH * HP]
        lane = lax.broadcasted_iota(jnp.int32, vv.shape, 1)
        v_ref[...] = jnp.where(lane % HP == V_HEAD, 1.0, vv).astype(BF16)

    row = lambda w: pl.BlockSpec((tm, w), lambda i: (i, 0))
    return pl.pallas_call(
        body, name="f1_fwd", grid=(n // tm,),
        out_shape=[jax.ShapeDtypeStruct((n, D), BF16), jax.ShapeDtypeStruct((n, D_SSM), F32),
                   jax.ShapeDtypeStruct((n, LAT), F32)] + [jax.ShapeDtypeStruct((n, NH * HP), BF16)] * 3 +
                  [jax.ShapeDtypeStruct((n, Q_LORA), BF16), jax.ShapeDtypeStruct((n, KV_LORA), BF16)],
        in_specs=[row(D), pl.BlockSpec((1, 8, D), lambda i: (i // tps, 0, 0)), _VM, _VM,
                  row(HP), row(HP), row(HP), _VM, _VM, _VM, _VM],
        out_specs=[row(D), row(D_SSM), row(LAT)] + [row(NH * HP)] * 3 + [row(Q_LORA), row(KV_LORA)],
        compiler_params=_cp(("parallel",)),
    )(x, modp, g1, w_in, rc, rs1, rs2, gq, gkv, w_uq, w_ukv)


def _f1_bwd(du, dq, dk, dv, lat, rc, rs1, rs2, gq, gkv, w_uq, w_ukv, dx1, x, modp, g1, w_in, S, tm):
    n = x.shape[0]
    tps = S // tm
    nb = n // S

    def body(du_ref, dq_ref, dk_ref, dv_ref, lat_ref, c_ref, s1_ref, s2_ref, gq_ref, gkv_ref, wq_ref, wkv_ref,
             dx1_ref, x_ref, mod_ref, g_ref, w_ref,
             dx_ref, dproj_ref, dqb_ref, dkvb_ref, accs_ref, accg_ref, accm_ref):
        i = pl.program_id(0)
        c, s1, s2 = c_ref[...], s1_ref[...], s2_ref[...]
        dqu = _rope_t(dq_ref[...] * _SCALE, jnp.tile(c, (1, NH)), jnp.tile(s1, (1, NH)),
                      jnp.tile(s2, (1, NH))).astype(BF16)
        dqb_ref[...] = dqu
        dqn = _dot_nt(dqu, wq_ref[...])
        qhat, rq = _rms(lat_ref[:, 0:Q_LORA], Q_LORA)
        dql = _rms_bwd(dqn * gq_ref[...], qhat, rq, Q_LORA)
        dkf = dk_ref[...] * (1.0 / _LOG2E)
        dkv = jnp.concatenate([dkf.astype(BF16), dv_ref[...].astype(BF16)], axis=1)
        dkvb_ref[...] = dkv
        dkvn = _dot_nt(dkv, wkv_ref[...])
        khat, rk = _rms(lat_ref[:, Q_LORA:Q_LORA + KV_LORA], KV_LORA)
        dkvl = _rms_bwd(dkvn * gkv_ref[...], khat, rk, KV_LORA)
        dkr = dkf[:, 0:HP]
        for h in range(1, NH):
            dkr = dkr + dkf[:, h * HP:(h + 1) * HP]
        lane = lax.broadcasted_iota(jnp.int32, dkr.shape, 1)
        dkr = jnp.where((lane >= QK_NOPE) & (lane < QK_NOPE + QK_ROPE), dkr, 0.0)
        dkr = _roll(_rope_t(dkr, c, s1, s2), -64)

        @pl.when(i == 0)
        def _():
            accm_ref[...] = jnp.zeros_like(accm_ref)

        accm_ref[0:1, 0:Q_LORA] += _colsum(dqn * qhat)
        accm_ref[1:2, 0:KV_LORA] += _colsum(dkvn * khat)

        dproj = jnp.concatenate([du_ref[...], dql, dkvl, dkr], axis=1).astype(BF16)
        dproj_ref[...] = dproj
        dh = _dot_nt(dproj, w_ref[...])
        xhat, r = _rms(x_ref[...], D)
        g = g_ref[...]
        dn = dh * (1.0 + mod_ref[0, 1:2, :])
        dx_ref[...] = dx1_ref[...] + _rms_bwd(dn * g, xhat, r, D)

        @pl.when(i % tps == 0)
        def _():
            accs_ref[...] = jnp.zeros_like(accs_ref)

        @pl.when(i == 0)
        def _():
            accg_ref[...] = jnp.zeros_like(accg_ref)

        accs_ref[0, 0:1, :] += _colsum(dh)
        accs_ref[0, 1:2, :] += _colsum(dh * (xhat * g))
        accg_ref[0:1, :] += _colsum(dn * xhat)

    row = lambda w: pl.BlockSpec((tm, w), lambda i: (i, 0))
    return pl.pallas_call(
        body, name="f1_bwd", grid=(n // tm,),
        out_shape=[jax.ShapeDtypeStruct((n, D), F32), jax.ShapeDtypeStruct((n, IN_PAD), BF16),
                   jax.ShapeDtypeStruct((n, NH * HP), BF16), jax.ShapeDtypeStruct((n, 2 * NH * HP), BF16),
                   jax.ShapeDtypeStruct((nb, 8, D), F32), jax.ShapeDtypeStruct((8, D), F32),
                   jax.ShapeDtypeStruct((8, Q_LORA), F32)],
        in_specs=[row(D_SSM)] + [row(NH * HP)] * 3 + [row(IN_PAD - D_SSM), row(HP), row(HP), row(HP),
                                                     _VM, _VM, _VM, _VM, row(D), row(D),
                                                     pl.BlockSpec((1, 8, D), lambda i: (i // tps, 0, 0)), _VM, _VM],
        out_specs=[row(D), row(IN_PAD), row(NH * HP), row(2 * NH * HP),
                   pl.BlockSpec((1, 8, D), lambda i: (i // tps, 0, 0)), pl.BlockSpec((8, D), lambda i: (0, 0)),
                   pl.BlockSpec((8, Q_LORA), lambda i: (0, 0))],
        compiler_params=_cp(("arbitrary",)),
    )(du, dq, dk, dv, lat, rc, rs1, rs2, gq, gkv, w_uq, w_ukv, dx1, x, modp, g1, w_in)


def _ssm_param_fwd(lam_re, lam_im, logdt, b_re, b_im):
    def body(lr_ref, li_ref, ld_ref, br_ref, bi_ref, lbr_ref, lbi_ref, bbr_ref, bbi_ref):
        lr, li = lr_ref[...], li_ref[...]
        dt = jnp.exp(ld_ref[...])
        er = jnp.exp(lr * dt)
        lbr = er * jnp.cos(li * dt)
        lbi = er * jnp.sin(li * dt)
        den = 1.0 / (lr * lr + li * li)
        cr = ((lbr - 1.0) * lr + lbi * li) * den
        ci = (lbi * lr - (lbr - 1.0) * li) * den
        lbr_ref[...] = lbr
        lbi_ref[...] = lbi
        bbr_ref[...] = cr * br_ref[...] - ci * bi_ref[...]
        bbi_ref[...] = cr * bi_ref[...] + ci * br_ref[...]

    return pl.pallas_call(
        body, name="ssm_param_fwd",
        out_shape=[jax.ShapeDtypeStruct((NST, 1), F32)] * 2 + [jax.ShapeDtypeStruct((NST, H), F32)] * 2,
        in_specs=[_VM] * 5, out_specs=[_VM] * 4, compiler_params=_cp(),
    )(lam_re, lam_im, logdt, b_re, b_im)


def _ssm_param_bwd(lam_re, lam_im, logdt, b_re, b_im, dlb_re, dlb_im, dbb_re, dbb_im):
    def body(lr_ref, li_ref, ld_ref, br_ref, bi_ref, dlr_ref, dli_ref, dbr_ref, dbi_ref,
             gbr_ref, gbi_ref, glr_ref, gli_ref, gdt_ref):
        lr, li = lr_ref[...], li_ref[...]
        dt = jnp.exp(ld_ref[...])
        er = jnp.exp(lr * dt)
        lbr = er * jnp.cos(li * dt)
        lbi = er * jnp.sin(li * dt)
        den = 1.0 / (lr * lr + li * li)
        nr, ni = lbr - 1.0, lbi
        cr = (nr * lr + ni * li) * den
        ci = (ni * lr - nr * li) * den
        br, bi = br_ref[...], bi_ref[...]
        dbr, dbi = dbr_ref[...], dbi_ref[...]
        gbr_ref[...] = cr * dbr + ci * dbi
        gbi_ref[...] = cr * dbi - ci * dbr
        gcr = jnp.sum(dbr * br + dbi * bi, axis=1, keepdims=True)
        gci = jnp.sum(dbi * br - dbr * bi, axis=1, keepdims=True)
        ilr, ili = lr * den, -li * den
        glbr = dlr_ref[...] + (gcr * ilr + gci * ili)
        glbi = dli_ref[...] + (gci * ilr - gcr * ili)
        qr = -(cr * ilr - ci * ili)
        qi = -(cr * ili + ci * ilr)
        glr = gcr * qr + gci * qi
        gli = gci * qr - gcr * qi
        glr = glr + dt * (glbr * lbr + glbi * lbi)
        gli = gli + dt * (glbi * lbr - glbr * lbi)
        wr = lr * lbr - li * lbi
        wi = lr * lbi + li * lbr
        glr_ref[...] = glr
        gli_ref[...] = gli
        gdt_ref[...] = (glbr * wr + glbi * wi) * dt

    return pl.pallas_call(
        body, name="ssm_param_bwd",
        out_shape=[jax.ShapeDtypeStruct((NST, H), F32)] * 2 + [jax.ShapeDtypeStruct((NST, 1), F32)] * 3,
        in_specs=[_VM] * 9, out_specs=[_VM] * 5, compiler_params=_cp(),
    )(lam_re, lam_im, logdt, b_re, b_im, dlb_re, dlb_im, dbb_re, dbb_im)


def _rowsum(a):
    def body(a_ref, o_ref):
        o_ref[...] = jnp.sum(a_ref[...], axis=1, keepdims=True)

    return pl.pallas_call(
        body, name="rowsum", out_shape=jax.ShapeDtypeStruct((a.shape[0], 1), F32),
        in_specs=[_VM], out_specs=_VM, compiler_params=_cp(),
    )(a)


QB = D_SSM // 4
QS = 4 * QB


def _bd_lo(part, q):
    return part * NST + q * QS


def _bd_expand(ub, bm_ref, out_ref):
    for part in range(2):
        for q in range(4):
            lo = _bd_lo(part, q)
            out_ref[:, lo:lo + QS] = _dot(ub[:, q * QB:(q + 1) * QB], bm_ref[:, lo:lo + QS])


def _bd_expand_t(db, cm_ref, out_ref):
    for part in range(2):
        for q in range(4):
            lo = _bd_lo(part, q)
            out_ref[:, lo:lo + QS] = _dot_nt(db[:, q * QB:(q + 1) * QB], cm_ref[lo:lo + QS, :])


def _bd_project(sb, cm_ref):
    return jnp.concatenate(
        [_dot(sb[:, _bd_lo(0, q):_bd_lo(0, q) + QS], cm_ref[_bd_lo(0, q):_bd_lo(0, q) + QS, :])
         + _dot(sb[:, _bd_lo(1, q):_bd_lo(1, q) + QS], cm_ref[_bd_lo(1, q):_bd_lo(1, q) + QS, :])
         for q in range(4)], axis=1)


def _bd_project_t(ab, bm_ref):
    return jnp.concatenate(
        [_dot_nt(ab[:, _bd_lo(0, q):_bd_lo(0, q) + QS], bm_ref[:, _bd_lo(0, q):_bd_lo(0, q) + QS])
         + _dot_nt(ab[:, _bd_lo(1, q):_bd_lo(1, q) + QS], bm_ref[:, _bd_lo(1, q):_bd_lo(1, q) + QS])
         for q in range(4)], axis=1)


def _pow2k(pr, pi, nsq):
    for _ in range(nsq):
        pr, pi = pr * pr - pi * pi, 2.0 * pr * pi
    return pr, pi


def _ssm_local(u_p, bm, lre8, lim8, S, tt):
    n = u_p.shape[0]
    nb, nt = n // S, S // tt
    nsq = int(round(math.log2(S // 8)))
    assert 2 ** nsq == S // 8

    def body(u_ref, bm_ref, lre_ref, lim_ref, cre_ref, cim_ref, sre, sim, bu):
        j = pl.program_id(1)

        @pl.when(j == 0)
        def _():
            sre[...] = jnp.zeros_like(sre)
            sim[...] = jnp.zeros_like(sim)

        _bd_expand(u_ref[...].astype(BF16), bm_ref, bu)
        lre, lim = lre_ref[...], lim_ref[...]

        def step(i, c):
            sr, si = c
            off = pl.multiple_of(i * 8, 8)
            br = bu[pl.ds(off, 8), 0:NST]
            bi = bu[pl.ds(off, 8), NST:2 * NST]
            return lre * sr - lim * si + br, lre * si + lim * sr + bi

        sr, si = lax.fori_loop(0, tt // 8, step, (sre[...], sim[...]))
        sre[...] = sr
        sim[...] = si

        @pl.when(j == nt - 1)
        def _():
            pr, pi = _pow2k(lre[0:1], lim[0:1], nsq)
            cr = jnp.zeros((1, NST), F32)
            ci = jnp.zeros((1, NST), F32)
            cre_ref[0:1, :] = cr
            cim_ref[0:1, :] = ci
            for k in range(1, 8):
                cr, ci = sr[k - 1:k] + pr * cr - pi * ci, si[k - 1:k] + pr * ci + pi * cr
                cre_ref[k:k + 1, :] = cr
                cim_ref[k:k + 1, :] = ci

    return pl.pallas_call(
        body, name="ssm_local", grid=(nb, nt),
        out_shape=[jax.ShapeDtypeStruct((nb * 8, NST), F32)] * 2,
        in_specs=[pl.BlockSpec((tt, D_SSM), lambda b, j: (b * nt + j, 0)), _VM, _VM, _VM],
        out_specs=[pl.BlockSpec((8, NST), lambda b, j: (b, 0))] * 2,
        scratch_shapes=[pltpu.VMEM((8, NST), F32), pltpu.VMEM((8, NST), F32), pltpu.VMEM((tt, 2 * NST), F32)],
        compiler_params=_cp(("arbitrary", "arbitrary")),
    )(u_p, bm, lre8, lim8)


def _ssm_fwd(u_p, cre, cim, bm, cm, dvec, w_glu, lre8, lim8, S, tt):
    n = u_p.shape[0]
    nb, nt = n // S, S // tt

    def body(u_ref, cre_ref, cim_ref, bm_ref, cm_ref, d_ref, wg_ref, lre_ref, lim_ref,
             st_ref, ypre_ref, z_ref, gact_ref, yssm_ref, sre, sim, bu):
        j = pl.program_id(1)

        @pl.when(j == 0)
        def _():
            sre[...] = cre_ref[...]
            sim[...] = cim_ref[...]

        u = u_ref[...]
        _bd_expand(u.astype(BF16), bm_ref, bu)
        lre, lim = lre_ref[...], lim_ref[...]

        def step(i, c):
            sr, si = c
            off = pl.multiple_of(i * 8, 8)
            nr = lre * sr - lim * si + bu[pl.ds(off, 8), 0:NST]
            ni = lre * si + lim * sr + bu[pl.ds(off, 8), NST:2 * NST]
            bu[pl.ds(off, 8), 0:NST] = nr
            bu[pl.ds(off, 8), NST:2 * NST] = ni
            return nr, ni

        sr, si = lax.fori_loop(0, tt // 8, step, (sre[...], sim[...]))
        sre[...] = sr
        sim[...] = si
        stb = bu[...].astype(BF16)
        st_ref[...] = stb
        y = _bd_project(stb, cm_ref) + d_ref[...] * u
        ypre_ref[...] = y
        gb = _gelu(y).astype(BF16)
        gact_ref[...] = gb
        z = _dot(gb, wg_ref[...])
        z_ref[...] = z
        yssm_ref[...] = z[:, 0:D_SSM] * _sigmoid(z[:, D_SSM:2 * D_SSM])

    row = lambda w: pl.BlockSpec((tt, w), lambda b, j: (b * nt + j, 0))
    return pl.pallas_call(
        body, name="ssm_fwd", grid=(nb, nt),
        out_shape=[jax.ShapeDtypeStruct((n, 2 * NST), BF16), jax.ShapeDtypeStruct((n, D_SSM), F32),
                   jax.ShapeDtypeStruct((n, 2 * D_SSM), F32), jax.ShapeDtypeStruct((n, D_SSM), BF16),
                   jax.ShapeDtypeStruct((n, D_SSM), F32)],
        in_specs=[row(D_SSM), pl.BlockSpec((8, NST), lambda b, j: (b, 0)), pl.BlockSpec((8, NST), lambda b, j: (b, 0)),
                  _VM, _VM, _VM, _VM, _VM, _VM],
        out_specs=[row(2 * NST), row(D_SSM), row(2 * D_SSM), row(D_SSM), row(D_SSM)],
        scratch_shapes=[pltpu.VMEM((8, NST), F32), pltpu.VMEM((8, NST), F32), pltpu.VMEM((tt, 2 * NST), F32)],
        compiler_params=_cp(("arbitrary", "arbitrary")),
    )(u_p, cre, cim, bm, cm, dvec, w_glu, lre8, lim8)


def _ssm_bwd_a(dys_p, z, ypre, w_glu, cm, lre8, lim8, S, tt):
    n = z.shape[0]
    nb, nt = n // S, S // tt
    nsq = int(round(math.log2(S // 8)))
    ng = tt // 8

    def body(dys_ref, z_ref, y_ref, wg_ref, cm_ref, lre_ref, lim_ref, dy_ref, dz_ref, are_ref, aim_ref, sre, sim, gb):
        j = pl.program_id(1)

        @pl.when(j == 0)
        def _():
            sre[...] = jnp.zeros_like(sre)
            sim[...] = jnp.zeros_like(sim)

        z = z_ref[...]
        z1, z2 = z[:, 0:D_SSM], z[:, D_SSM:2 * D_SSM]
        sg = _sigmoid(z2)
        dys = dys_ref[...]
        dz = jnp.concatenate([dys * sg, dys * z1 * sg * (1.0 - sg)], axis=1).astype(BF16)
        dz_ref[...] = dz
        dy = _dot_nt(dz, wg_ref[...]) * _gelu_grad(y_ref[...])
        dy_ref[...] = dy
        _bd_expand_t(dy.astype(BF16), cm_ref, gb)
        lre, lim = lre_ref[...], lim_ref[...]

        def step(i, c):
            ar, ai = c
            off = pl.multiple_of((ng - 1 - i) * 8, 8)
            gr = gb[pl.ds(off, 8), 0:NST]
            gi = gb[pl.ds(off, 8), NST:2 * NST]
            return lre * ar + lim * ai + gr, lre * ai - lim * ar + gi

        ar, ai = lax.fori_loop(0, ng, step, (sre[...], sim[...]))
        sre[...] = ar
        sim[...] = ai

        @pl.when(j == nt - 1)
        def _():
            pr, pi = _pow2k(lre[0:1], -lim[0:1], nsq)
            cr = jnp.zeros((1, NST), F32)
            ci = jnp.zeros((1, NST), F32)
            are_ref[7:8, :] = cr
            aim_ref[7:8, :] = ci
            for k in range(6, -1, -1):
                cr, ci = ar[k + 1:k + 2] + pr * cr - pi * ci, ai[k + 1:k + 2] + pr * ci + pi * cr
                are_ref[k:k + 1, :] = cr
                aim_ref[k:k + 1, :] = ci

    row = lambda w: pl.BlockSpec((tt, w), lambda b, j: (b * nt + nt - 1 - j, 0))
    return pl.pallas_call(
        body, name="ssm_bwd_a", grid=(nb, nt),
        out_shape=[jax.ShapeDtypeStruct((n, D_SSM), F32), jax.ShapeDtypeStruct((n, 2 * D_SSM), BF16),
                   jax.ShapeDtypeStruct((nb * 8, NST), F32), jax.ShapeDtypeStruct((nb * 8, NST), F32)],
        in_specs=[row(D_SSM), row(2 * D_SSM), row(D_SSM), _VM, _VM, _VM, _VM],
        out_specs=[row(D_SSM), row(2 * D_SSM), pl.BlockSpec((8, NST), lambda b, j: (b, 0)),
                   pl.BlockSpec((8, NST), lambda b, j: (b, 0))],
        scratch_shapes=[pltpu.VMEM((8, NST), F32), pltpu.VMEM((8, NST), F32), pltpu.VMEM((tt, 2 * NST), F32)],
        compiler_params=_cp(("arbitrary", "arbitrary")),
    )(dys_p, z, ypre, w_glu, cm, lre8, lim8)


def _ssm_bwd_b(dy, u_p, st, fcr, fci, air, aii, bm, cm, dvec, lre8, lim8, S, tt):
    n = u_p.shape[0]
    nb, nt = n // S, S // tt
    ng = tt // 8

    def body(dy_ref, u_ref, st_ref, stp_ref, fcr_ref, fci_ref, air_ref, aii_ref, bm_ref, cm_ref, d_ref, lre_ref, lim_ref,
             du_ref, dcm_ref, dbm_ref, dd_ref, dlr_ref, dli_ref, are, aim, accr, acci, sp, ab):
        b = pl.program_id(0)
        j = pl.program_id(1)
        jt = nt - 1 - j

        @pl.when((b == 0) & (j == 0))
        def _():
            dcm_ref[...] = jnp.zeros_like(dcm_ref)
            dbm_ref[...] = jnp.zeros_like(dbm_ref)
            dd_ref[...] = jnp.zeros_like(dd_ref)
            accr[...] = jnp.zeros_like(accr)
            acci[...] = jnp.zeros_like(acci)

        @pl.when(j == 0)
        def _():
            are[...] = air_ref[...]
            aim[...] = aii_ref[...]

        sp[8:tt + 8, :] = st_ref[...].astype(F32)

        @pl.when(jt == 0)
        def _():
            sp[0:8, 0:NST] = fcr_ref[...]
            sp[0:8, NST:2 * NST] = fci_ref[...]

        @pl.when(jt != 0)
        def _():
            sp[0:8, :] = stp_ref[8:16, :].astype(F32)

        dy = dy_ref[...]
        u = u_ref[...]
        dyb = dy.astype(BF16)
        _bd_expand_t(dyb, cm_ref, ab)
        lre, lim = lre_ref[...], lim_ref[...]

        def step(i, c):
            ar, ai = c
            off = pl.multiple_of((ng - 1 - i) * 8, 8)
            nr = lre * ar + lim * ai + ab[pl.ds(off, 8), 0:NST]
            ni = lre * ai - lim * ar + ab[pl.ds(off, 8), NST:2 * NST]
            ab[pl.ds(off, 8), 0:NST] = nr
            ab[pl.ds(off, 8), NST:2 * NST] = ni
            pr = sp[pl.ds(off, 8), 0:NST]
            pi = sp[pl.ds(off, 8), NST:2 * NST]
            accr[...] += nr * pr + ni * pi
            acci[...] += ni * pr - nr * pi
            return nr, ni

        ar, ai = lax.fori_loop(0, ng, step, (are[...], aim[...]))
        are[...] = ar
        aim[...] = ai
        a_b = ab[...].astype(BF16)
        du_ref[...] = _bd_project_t(a_b, bm_ref) + d_ref[...] * dy
        ub = u.astype(BF16)
        for q in range(4):
            for part in range(2):
                lo = part * NST + q * 4 * QB
                s_q = st_ref[:, lo:lo + 4 * QB]
                dcm_ref[lo:lo + 4 * QB, :] += _dot_tn(s_q, dyb[:, q * QB:(q + 1) * QB])
                dbm_ref[:, lo:lo + 4 * QB] += _dot_tn(ub[:, q * QB:(q + 1) * QB], a_b[:, lo:lo + 4 * QB])
        dd_ref[...] += _colsum(dy * u)

        @pl.when((b == nb - 1) & (j == nt - 1))
        def _():
            dlr_ref[...] = _colsum(accr[...])
            dli_ref[...] = _colsum(acci[...])

    row = lambda w: pl.BlockSpec((tt, w), lambda b, j: (b * nt + nt - 1 - j, 0))
    seq8 = pl.BlockSpec((8, NST), lambda b, j: (b, 0))
    prev = pl.BlockSpec((16, 2 * NST), lambda b, j: (jnp.maximum((b * nt + nt - 1 - j) * (tt // 16) - 1, 0), 0))
    const = lambda shape: pl.BlockSpec(shape, lambda b, j: (0, 0))
    return pl.pallas_call(
        body, name="ssm_bwd_b", grid=(nb, nt),
        out_shape=[jax.ShapeDtypeStruct((n, D_SSM), F32), jax.ShapeDtypeStruct((2 * NST, QB), F32),
                   jax.ShapeDtypeStruct((QB, 2 * NST), F32), jax.ShapeDtypeStruct((1, D_SSM), F32),
                   jax.ShapeDtypeStruct((1, NST), F32), jax.ShapeDtypeStruct((1, NST), F32)],
        in_specs=[row(D_SSM), row(D_SSM), row(2 * NST), prev, seq8, seq8, seq8, seq8, _VM, _VM, _VM, _VM, _VM],
        out_specs=[row(D_SSM), const((2 * NST, QB)), const((QB, 2 * NST)), const((1, D_SSM)),
                   const((1, NST)), const((1, NST))],
        scratch_shapes=[pltpu.VMEM((8, NST), F32)] * 4 + [pltpu.VMEM((tt + 8, 2 * NST), F32),
                                                          pltpu.VMEM((tt, 2 * NST), F32)],
        compiler_params=_cp(("arbitrary", "arbitrary")),
    )(dy, u_p, st, st, fcr, fci, air, aii, bm, cm, dvec, lre8, lim8)


def _rope(v, c, s1, s2):
    return v * c + _roll(v, -16) * s1 + _roll(v, 16) * s2


def _rope_t(dv, c, s1, s2):
    return dv * c + _roll(dv * s1, 16) + _roll(dv * s2, -16)


_SCALE = (QK_NOPE + QK_ROPE) ** -0.5
_LOG2E = 1.4426950408889634
_C2 = _SCALE * _LOG2E


def _attn_fwd(q, k, v, S, tq):
    n = q.shape[0]
    nb, nq = n // S, S // tq

    HG = 2

    def body(q_ref, k_ref, v_ref, o_ref, lr_ref):
        qi = pl.program_id(2)
        qv = q_ref[...]

        def tile(j, c, diagonal):
            off = pl.multiple_of(j * tq, tq)
            kt = k_ref[pl.ds(off, tq), :]
            vt = v_ref[pl.ds(off, tq), :]
            out = []
            for g in range(HG):
                m, acc = c[g]
                hs = slice(g * HP, (g + 1) * HP)
                s = _dot_nt(qv[:, hs], kt[:, hs])
                if diagonal:
                    rows = lax.broadcasted_iota(jnp.int32, s.shape, 0)
                    cols = lax.broadcasted_iota(jnp.int32, s.shape, 1)
                    s = jnp.where(cols <= rows, s, NEG)
                mn = jnp.maximum(m, jnp.max(s, axis=1, keepdims=True))
                p = jnp.exp2(s - mn)
                out.append((mn, jnp.exp2(m - mn) * acc + _dot(p.astype(BF16), vt[:, hs])))
            return tuple(out)

        init = tuple((jnp.full((tq, 1), NEG, F32), jnp.zeros((tq, HP), F32)) for _ in range(HG))
        c = lax.fori_loop(0, qi, lambda j, c: tile(j, c, False), init)
        c = tile(qi, c, True)
        lane = lax.broadcasted_iota(jnp.int32, (8, HP), 1)
        pick = jnp.where(lane == 0, 1.0, 0.0).astype(BF16)
        for g in range(HG):
            m, acc = c[g]
            l = acc[:, V_HEAD:V_HEAD + 1]
            vlane = lax.broadcasted_iota(jnp.int32, acc.shape, 1)
            o_ref[:, g * HP:(g + 1) * HP] = jnp.where(vlane < V_HEAD, acc / l, 0.0).astype(BF16)
            lse = jnp.broadcast_to(m + jnp.log(l) * _LOG2E, (tq, HP))
            lr_ref[g * 8:(g + 1) * 8, :] = _rows_of(lse, pick)

    qs = pl.BlockSpec((tq, HG * HP), lambda b, h, i: (b * nq + i, h))
    ks = pl.BlockSpec((S, HG * HP), lambda b, h, i: (b, h))
    return pl.pallas_call(
        body, name="attn_fwd", grid=(nb, NH // HG, nq),
        out_shape=[jax.ShapeDtypeStruct((n, NH * HP), BF16), jax.ShapeDtypeStruct((nb * NH * 8, S), F32)],
        in_specs=[qs, ks, ks],
        out_specs=[qs, pl.BlockSpec((HG * 8, tq), lambda b, h, i: (b * (NH // HG) + h, i))],
        compiler_params=_cp(("parallel", "parallel", "arbitrary")),
    )(q, k, v)


def _rows_of(x, pick):
    x1 = x.astype(BF16)
    r1 = x - x1.astype(F32)
    x2 = r1.astype(BF16)
    x3 = (r1 - x2.astype(F32)).astype(BF16)
    return _dot_nt(pick, x1) + _dot_nt(pick, x2) + _dot_nt(pick, x3)


def _attn_bwd(q, k, v, dob, lrow, drow, S, tq):
    n = q.shape[0]
    nb, nq = n // S, S // tq

    def body(q_ref, k_ref, v_ref, do_ref, lr_ref, dr_ref, dqo_ref, dk_ref, dv_ref, dq_ref):
        kj = pl.program_id(2)

        @pl.when(kj == 0)
        def _():
            dq_ref[...] = jnp.zeros_like(dq_ref)

        kt = k_ref[...]
        vt = v_ref[...]

        def tile(i, c, diagonal):
            dk, dv = c
            off = pl.multiple_of(i * tq, tq)
            qv = q_ref[pl.ds(off, tq), :]
            dob = do_ref[pl.ds(off, tq), :]
            lr = lr_ref[0:1, pl.ds(off, tq)]
            dr = dr_ref[0:1, pl.ds(off, tq)]
            st = _dot_nt(kt, qv)
            dpt = _dot_nt(vt, dob)
            pt = jnp.exp2(st - lr)
            if diagonal:
                keys = lax.broadcasted_iota(jnp.int32, pt.shape, 0)
                qrys = lax.broadcasted_iota(jnp.int32, pt.shape, 1)
                pt = jnp.where(keys <= qrys, pt, 0.0)
            dst = (pt * (dpt - dr)).astype(BF16)
            dq_ref[pl.ds(off, tq), :] += _dot_tn(dst, kt)
            return dk + _dot(dst, qv), dv + _dot(pt.astype(BF16), dob)

        zero = jnp.zeros((tq, HP), F32)
        c = tile(kj, (zero, zero), True)
        dk, dv = lax.fori_loop(kj + 1, nq, lambda i, c: tile(i, c, False), c)
        dk_ref[...] = dk.astype(BF16)
        dv_ref[...] = dv.astype(BF16)

        @pl.when(kj == nq - 1)
        def _():
            dqo_ref[...] = dq_ref[...].astype(BF16)

    ts = pl.BlockSpec((tq, HP), lambda b, h, i: (b * nq + i, h))
    fs = pl.BlockSpec((S, HP), lambda b, h, i: (b, h))
    rs = pl.BlockSpec((8, S), lambda b, h, i: (b * NH + h, 0))
    return pl.pallas_call(
        body, name="attn_bwd", grid=(nb, NH, nq),
        out_shape=[jax.ShapeDtypeStruct((n, NH * HP), BF16)] * 3,
        in_specs=[fs, ts, ts, fs, rs, rs], out_specs=[fs, ts, ts],
        scratch_shapes=[pltpu.VMEM((S, HP), F32)],
        compiler_params=_cp(("parallel", "parallel", "arbitrary")),
    )(q, k, v, dob, lrow, drow)


def _p2(yssm, oattn, x, target, modp, gs, ga, w_out, g2, gf, w_ff1, w_ff2, S, tm):
    n = x.shape[0]
    tps = S // tm
    nb = n // S

    def body(ys_ref, oa_ref, x_ref, t_ref, mod_ref, gs_ref, ga_ref, wo_ref, g2_ref, gf_ref, w1_ref, w2_ref,
             yn_ref, h2_ref, dx1_ref, r_ref, da_ref, dff_ref, do_ref, dys_ref, doa_ref, dr_ref,
             accs_ref, accg_ref, accg3_ref):
        i = pl.program_id(0)
        sh2, sc2, gt2 = mod_ref[0, 3:4, :], mod_ref[0, 4:5, :], mod_ref[0, 5:6, :]
        fsh, fsc = mod_ref[0, 6:7, :], mod_ref[0, 7:8, :]
        yh, rs = _rms(ys_ref[...], D_SSM)
        oa = oa_ref[...].astype(F32)
        ah, ra_ = _rms(oa, D_ATTN)
        yn = jnp.concatenate([yh * gs_ref[...], ah * ga_ref[...]], axis=1).astype(BF16)
        yn_ref[...] = yn
        o = _dot(yn, wo_ref[...])
        x1 = x_ref[...] + mod_ref[0, 2:3, :] * o
        x1h, r2 = _rms(x1, D)
        g2_v = g2_ref[...]
        h2 = ((x1h * g2_v) * (1.0 + sc2) + sh2).astype(BF16)
        h2_ref[...] = h2
        a = _dot(h2, w1_ref[...])
        ra = jnp.maximum(a, 0.0)
        rb = (ra * ra).astype(BF16)
        r_ref[...] = rb
        ff = _dot(rb, w2_ref[...])
        x2 = x1 + gt2 * ff
        x2h, rf = _rms(x2, D)
        gf_v = gf_ref[...]
        outn = x2h * gf_v
        err = outn * (1.0 + fsc) + fsh - t_ref[...]
        dout = err * (1.0 / D)
        doutn = dout * (1.0 + fsc)
        dx2 = _rms_bwd(doutn * gf_v, x2h, rf, D)
        dff = (gt2 * dx2).astype(BF16)
        dff_ref[...] = dff
        dr = _dot_nt(dff, w2_ref[...])
        da = (dr * (2.0 * ra)).astype(BF16)
        da_ref[...] = da
        dh2 = _dot_nt(da, w1_ref[...])
        dn2 = dh2 * (1.0 + sc2)
        dx1 = dx2 + _rms_bwd(dn2 * g2_v, x1h, r2, D)
        dx1_ref[...] = dx1
        dob = (mod_ref[0, 2:3, :] * dx1).astype(BF16)
        do_ref[...] = dob
        dyn = _dot_nt(dob, wo_ref[...])
        d1 = dyn[:, 0:D_SSM]
        d2 = dyn[:, D_SSM:D_SSM + NH * HP]
        dys_ref[...] = _rms_bwd(d1 * gs_ref[...], yh, rs, D_SSM)
        doa = _rms_bwd(d2 * ga_ref[...], ah, ra_, D_ATTN)
        doa_ref[...] = doa.astype(BF16)
        prod = doa * oa
        ones = jnp.ones((8, HP), BF16)
        for h in range(NH):
            dr_ref[h * 8:(h + 1) * 8, :] = _rows_of(prod[:, h * HP:(h + 1) * HP], ones)

        @pl.when(i % tps == 0)
        def _():
            accs_ref[...] = jnp.zeros_like(accs_ref)

        @pl.when(i == 0)
        def _():
            accg_ref[...] = jnp.zeros_like(accg_ref)
            accg3_ref[...] = jnp.zeros_like(accg3_ref)

        accs_ref[0, 2:3, :] += _colsum(dx1 * o)
        accg3_ref[0:1, 0:D_SSM] += _colsum(d1 * yh)
        accg3_ref[1:2, :] += _colsum(d2 * ah)
        accs_ref[0, 3:4, :] += _colsum(dh2)
        accs_ref[0, 4:5, :] += _colsum(dh2 * (x1h * g2_v))
        accs_ref[0, 5:6, :] += _colsum(dx2 * ff)
        accs_ref[0, 6:7, :] += _colsum(dout)
        accs_ref[0, 7:8, :] += _colsum(dout * outn)
        accg_ref[0:1, :] += _colsum(dn2 * x1h)
        accg_ref[1:2, :] += _colsum(doutn * x2h)
        accg_ref[2:3, :] += _colsum(err * err) * (0.5 / D)

    row = lambda w: pl.BlockSpec((tm, w), lambda i: (i, 0))
    return pl.pallas_call(
        body, name="p2_mlp_loss", grid=(n // tm,),
        out_shape=[jax.ShapeDtypeStruct((n, D_SSM + NH * HP), BF16), jax.ShapeDtypeStruct((n, D), BF16),
                   jax.ShapeDtypeStruct((n, D), F32), jax.ShapeDtypeStruct((n, D_FF), BF16),
                   jax.ShapeDtypeStruct((n, D_FF), BF16), jax.ShapeDtypeStruct((n, D), BF16),
                   jax.ShapeDtypeStruct((n, D), BF16), jax.ShapeDtypeStruct((n, D_SSM), F32),
                   jax.ShapeDtypeStruct((n, NH * HP), BF16), jax.ShapeDtypeStruct((nb * NH * 8, S), F32),
                   jax.ShapeDtypeStruct((nb, 8, D), F32), jax.ShapeDtypeStruct((8, D), F32),
                   jax.ShapeDtypeStruct((8, NH * HP), F32)],
        in_specs=[row(D_SSM), row(NH * HP), row(D), row(D), pl.BlockSpec((1, 8, D), lambda i: (i // tps, 0, 0)),
                  _VM, _VM, _VM, _VM, _VM, _VM, _VM],
        out_specs=[row(D_SSM + NH * HP), row(D),
                   row(D), row(D_FF), row(D_FF), row(D),
                   row(D), row(D_SSM), row(NH * HP), pl.BlockSpec((NH * 8, tm), lambda i: (i // tps, i % tps)),
                   pl.BlockSpec((1, 8, D), lambda i: (i // tps, 0, 0)),
                   pl.BlockSpec((8, D), lambda i: (0, 0)), pl.BlockSpec((8, NH * HP), lambda i: (0, 0))],
        compiler_params=_cp(("arbitrary",)),
    )(yssm, oattn, x, target, modp, gs, ga, w_out, g2, gf, w_ff1, w_ff2)


def _wgrad(a, b, name, col_slots=0):
    n, k1 = a.shape
    k2 = b.shape[1]
    bn = next((b for b in (4096, 2048, 1024, 512) if n % b == 0), n)
    bk1 = next((b for b in (1024, 512) if k1 % b == 0), k1)
    bk2 = k2 // col_slots if col_slots else (1024 if (k2 % 1024 == 0) else k2)

    def body(a_ref, b_ref, o_ref):
        @pl.when(pl.program_id(2) == 0)
        def _():
            o_ref[...] = jnp.zeros_like(o_ref)

        o_ref[...] += _dot_tn(a_ref[...], b_ref[...]).reshape(o_ref.shape)

    if col_slots:
        out_shape = jax.ShapeDtypeStruct((col_slots, k1, bk2), F32)
        out_spec = pl.BlockSpec((1, bk1, bk2), lambda i, j, t: (j, i, 0))
    else:
        out_shape = jax.ShapeDtypeStruct((k1, k2), F32)
        out_spec = pl.BlockSpec((bk1, bk2), lambda i, j, t: (i, j))
    return pl.pallas_call(
        body, name=name, grid=(k1 // bk1, k2 // bk2, n // bn),
        out_shape=out_shape,
        in_specs=[pl.BlockSpec((bn, bk1), lambda i, j, t: (t, i)), pl.BlockSpec((bn, bk2), lambda i, j, t: (t, j))],
        out_specs=out_spec,
        compiler_params=_cp(("parallel", "parallel", "arbitrary")),
    )(a, b)


def _row_block(rows):
    if rows <= 256:
        return rows
    return next(b for b in (256, 192, 128, 64, 32, 16, 8) if rows % b == 0)


def _add_half(g, recv, cidx, name):
    _, rows2, w = g.shape
    rows = rows2 // 2
    br = _row_block(rows)
    nblk = rows // br

    def body(c_ref, g_ref, r_ref, o_ref):
        o_ref[...] = (g_ref[...] + r_ref[...]).astype(BF16)

    return pl.pallas_call(
        body, name=name,
        grid_spec=pltpu.PrefetchScalarGridSpec(
            num_scalar_prefetch=1, grid=(4, nblk),
            in_specs=[pl.BlockSpec((1, br, w), lambda s, i, c: (s, c[0] * nblk + i, 0)),
                      pl.BlockSpec((1, br, w), lambda s, i, c: (s, i, 0))],
            out_specs=pl.BlockSpec((1, br, w), lambda s, i, c: (s, i, 0))),
        out_shape=jax.ShapeDtypeStruct((4, rows, w), BF16),
        compiler_params=_cp(("parallel", "parallel")),
    )(cidx, g, recv)


def _add_chips(r, name):
    _, rows, w = r.shape
    br = _row_block(rows)

    def body(r_ref, o_ref):
        f = lambda k: r_ref[k].astype(F32)
        o_ref[...] = ((f(0) + f(1)) + f(2)) + f(3)

    return pl.pallas_call(
        body, name=name, grid=(rows // br,),
        out_shape=jax.ShapeDtypeStruct((rows, w), F32),
        in_specs=[pl.BlockSpec((4, br, w), lambda i: (0, i, 0))],
        out_specs=pl.BlockSpec((br, w), lambda i: (i, 0)),
        compiler_params=_cp(("parallel",)),
    )(r)


def _pair_sum(a, sa, b, sb):
    def body(a_ref, sa_ref, b_ref, sb_ref, oa_ref, ob_ref):
        oa_ref[...] = (a_ref[...].astype(F32) + sa_ref[...].astype(F32)).astype(BF16)
        ob_ref[...] = b_ref[...] + sb_ref[...]

    return pl.pallas_call(
        body, name="small_grad_pair_sum",
        out_shape=[jax.ShapeDtypeStruct(a.shape, BF16), jax.ShapeDtypeStruct(b.shape, F32)],
        in_specs=[_VM] * 4, out_specs=[_VM, _VM], compiler_params=_cp(),
    )(a, sa, b, sb)


def _sum_devices(a, b):
    def body(a_ref, b_ref, oa_ref, ob_ref):
        acc = a_ref[0:1, :].astype(F32)
        accb = b_ref[0:1, :]
        for k in range(1, a.shape[0]):
            acc = acc + a_ref[k:k + 1, :].astype(F32)
            accb = accb + b_ref[k:k + 1, :]
        oa_ref[...] = acc
        ob_ref[...] = accb

    return pl.pallas_call(
        body, name="small_grad_sum",
        out_shape=[jax.ShapeDtypeStruct((1, a.shape[1]), F32), jax.ShapeDtypeStruct((1, b.shape[1]), F32)],
        in_specs=[_VM, _VM], out_specs=[_VM, _VM], compiler_params=_cp(),
    )(a, b)


def _adamw_math(wv, gv, mv, vv):
    m_new = ADAM_B1 * mv + (1.0 - ADAM_B1) * gv
    v_new = ADAM_B2 * vv + (1.0 - ADAM_B2) * (gv * gv)
    m_hat = m_new / (1.0 - ADAM_B1 ** ADAM_STEP)
    v_hat = v_new / (1.0 - ADAM_B2 ** ADAM_STEP)
    return -ADAM_LR * (m_hat / (jnp.sqrt(v_hat) + ADAM_EPS) + ADAM_WD * wv), m_new, v_new


def _adamw_small(ws, gs, ms, vs):
    k = len(ws)

    def body(*refs):
        ins, outs = refs[:4 * k], refs[4 * k:]
        for t in range(k):
            d, m_new, v_new = _adamw_math(ins[t][...], ins[k + t][...], ins[2 * k + t][...], ins[3 * k + t][...])
            outs[t][...] = d
            outs[k + t][...] = m_new
            outs[2 * k + t][...] = v_new

    shapes = [jax.ShapeDtypeStruct(w.shape, F32) for w in ws]
    return pl.pallas_call(
        body, name="adamw_small", out_shape=shapes * 3,
        in_specs=[_VM] * (4 * k), out_specs=[_VM] * (3 * k), compiler_params=_cp(),
    )(*ws, *gs, *ms, *vs)


def _adamw(w, g, m, v, name):
    rows, wd = w.shape
    br = _row_block(rows)

    def body(w_ref, g_ref, m_ref, v_ref, d_ref, nm_ref, nv_ref):
        d, m_new, v_new = _adamw_math(w_ref[...], g_ref[...], m_ref[...], v_ref[...])
        d_ref[...] = d
        nm_ref[...] = m_new
        nv_ref[...] = v_new

    spec = pl.BlockSpec((br, wd), lambda i: (i, 0))
    return pl.pallas_call(
        body, name=name, grid=(rows // br,),
        out_shape=[jax.ShapeDtypeStruct((rows, wd), F32)] * 3,
        in_specs=[spec] * 4, out_specs=[spec] * 3,
        compiler_params=_cp(("parallel",)),
    )(w, g, m, v)


def _adamw_halves(w, mine, other, m, v, cidx, name):
    rows, wd = w.shape
    h = rows // 2
    br = _row_block(h)
    nblk = h // br

    def body(c_ref, w_ref, a_ref, b_ref, m_ref, v_ref, g_ref, d_ref, nm_ref, nv_ref):
        gv = jnp.where(pl.program_id(0) == c_ref[0], a_ref[...], b_ref[...])
        d, m_new, v_new = _adamw_math(w_ref[...], gv, m_ref[...], v_ref[...])
        g_ref[...] = gv
        d_ref[...] = d
        nm_ref[...] = m_new
        nv_ref[...] = v_new

    full = pl.BlockSpec((br, wd), lambda hf, i, c: (hf * nblk + i, 0))
    half = pl.BlockSpec((br, wd), lambda hf, i, c: (i, 0))
    return pl.pallas_call(
        body, name=name,
        grid_spec=pltpu.PrefetchScalarGridSpec(
            num_scalar_prefetch=1, grid=(2, nblk),
            in_specs=[full, half, half, full, full], out_specs=[full] * 4),
        out_shape=[jax.ShapeDtypeStruct((rows, wd), F32)] * 4,
        compiler_params=_cp(("parallel", "parallel")),
    )(cidx, w, mine, other, m, v)


def _other_chips(x, y):
    return [(1 - x, y), (x, 1 - y), (1 - x, 1 - y)]


def _other_devices(x, y, c):
    flip = lambda v, d: (1 - v) if d else v
    return [(flip(x, dx), flip(y, dy), flip(c, dc))
            for dx in (0, 1) for dy in (0, 1) for dc in (0, 1) if (dx, dy, dc) != (0, 0, 0)]


def _exchange(name, ins, out_shapes, n_local, n_remote, plan):
    ni, no = len(ins), len(out_shapes)

    def body(*refs):
        in_refs, out_refs = refs[:ni], refs[ni:ni + no]
        send_sems, recv_sems, local_sems = refs[ni + no:]
        x, y, c = lax.axis_index("x"), lax.axis_index("y"), lax.axis_index("c")
        local, remote = plan(in_refs, out_refs, x, y, c)
        assert len(local) == n_local and len(remote) == n_remote

        def push(k, src, dst, dev):
            return pltpu.make_async_remote_copy(src_ref=src, dst_ref=dst, send_sem=send_sems.at[k],
                                                recv_sem=recv_sems.at[k], device_id=dev, device_id_type=MESH)

        own = [pltpu.make_async_copy(s, d, local_sems.at[i]) for i, (s, d) in enumerate(local)]
        for cp in own:
            cp.start()
        sends = [push(k, s, d, dev) for k, (s, d, dev, _) in enumerate(remote)]
        for cp in sends:
            cp.start()
        for k, (s, _, dev, landing) in enumerate(remote):
            push(k, s, landing, dev).wait_recv()
        for cp in sends:
            cp.wait_send()
        for cp in own:
            cp.wait()

    return pl.pallas_call(
        body, name=name, out_shape=out_shapes,
        in_specs=[_ANY] * ni, out_specs=[_ANY] * no,
        scratch_shapes=[pltpu.SemaphoreType.DMA((n_remote,)), pltpu.SemaphoreType.DMA((n_remote,)),
                        pltpu.SemaphoreType.DMA((max(n_local, 1),))],
        compiler_params=pltpu.CompilerParams(has_side_effects=True),
    )(*ins)


def _gather_chips(name, shards, everyone=()):
    ns, ne = len(shards), len(everyone)
    outs = [jax.ShapeDtypeStruct((4,) + a.shape, a.dtype) for a in shards]
    outs += [jax.ShapeDtypeStruct((8,) + a.shape, a.dtype) for a in everyone]

    def plan(i, o, x, y, c):
        mine, me = 2 * x + y, 4 * x + 2 * y + c
        local, remote = [], []
        for t in range(ns):
            local.append((i[t], o[t].at[mine]))
            for px, py in _other_chips(x, y):
                remote.append((i[t], o[t].at[mine], (px, py, c), o[t].at[2 * px + py]))
        for t in range(ns, ns + ne):
            local.append((i[t], o[t].at[me]))
            for px, py, pc in _other_devices(x, y, c):
                remote.append((i[t], o[t].at[me], (px, py, pc), o[t].at[4 * px + 2 * py + pc]))
        return local, remote

    return _exchange(name, list(shards) + list(everyone), outs, ns + ne, 3 * ns + 7 * ne, plan)


_HBM = pl.BlockSpec(memory_space=pltpu.HBM)
_SEM = pl.BlockSpec(memory_space=pltpu.SEMAPHORE)
_EFFECT = pltpu.SideEffectType.DATAFLOW_SIDE_EFFECTING


def _split_start(name, ins, land_shapes, n_remote, plan, after):
    ni, nl = len(ins), len(land_shapes)
    srcs = [pltpu.with_memory_space_constraint(a, pltpu.HBM) for a in ins]
    lands = [pltpu.with_memory_space_constraint(lax.empty(s.shape, s.dtype), pltpu.HBM) for s in land_shapes]

    def body(*refs):
        src, land = refs[:ni], refs[ni:ni + nl]
        first = ni + nl + 1
        send, recv = refs[first:first + n_remote], refs[first + n_remote:first + 2 * n_remote]
        token = refs[first + 2 * n_remote + ni + nl]
        x, y, c = lax.axis_index("x"), lax.axis_index("y"), lax.axis_index("c")
        remote = plan(src, land, x, y, c)
        assert len(remote) == n_remote
        for k, (s, d, dev, _) in enumerate(remote):
            pltpu.make_async_remote_copy(src_ref=s, dst_ref=d, send_sem=send[k], recv_sem=recv[k],
                                         device_id=dev, device_id_type=MESH).start()
        token[...] = jnp.zeros_like(token)

    out = pl.pallas_call(
        body, name=name + "_start",
        out_shape=[pltpu.SemaphoreType.DMA(())] * (2 * n_remote)
                  + [pltpu.HBM(a.shape, a.dtype) for a in ins] + [pltpu.HBM(s.shape, s.dtype) for s in land_shapes]
                  + [jax.ShapeDtypeStruct((8, 128), F32)],
        in_specs=[_HBM] * (ni + nl) + [_ANY], out_specs=[_SEM] * (2 * n_remote) + [_HBM] * (ni + nl) + [_VM],
        input_output_aliases={t: 2 * n_remote + t for t in range(ni + nl)},
        compiler_params=pltpu.CompilerParams(has_side_effects=_EFFECT),
    )(*srcs, *lands, after)
    sems, thru = out[:2 * n_remote], out[2 * n_remote:2 * n_remote + ni + nl]
    return (name, sems, thru[:ni], thru[ni:], n_remote, plan), out[-1]


def _split_wait(handle, after):
    name, sems, srcs, lands, n_remote, plan = handle
    ni, nl = len(srcs), len(lands)

    def body(*refs):
        src, land = refs[:ni], refs[ni:ni + nl]
        send, recv = refs[ni + nl:ni + nl + n_remote], refs[ni + nl + n_remote:ni + nl + 2 * n_remote]
        x, y, c = lax.axis_index("x"), lax.axis_index("y"), lax.axis_index("c")
        for k, (s, _, dev, landing) in enumerate(plan(src, land, x, y, c)):
            cp = pltpu.make_async_remote_copy(src_ref=s, dst_ref=landing, send_sem=send[k], recv_sem=recv[k],
                                              device_id=dev, device_id_type=MESH)
            cp.wait_send()
            cp.wait_recv()

    out = pl.pallas_call(
        body, name=name + "_wait",
        out_shape=[pltpu.HBM(a.shape, a.dtype) for a in srcs] + [pltpu.HBM(a.shape, a.dtype) for a in lands],
        in_specs=[_HBM] * (ni + nl) + [_SEM] * (2 * n_remote) + [_ANY], out_specs=[_HBM] * (ni + nl),
        input_output_aliases={t: t for t in range(ni + nl)},
        compiler_params=pltpu.CompilerParams(has_side_effects=_EFFECT),
    )(*srcs, *lands, *sems, after)
    return out[:ni], out[ni:]


def _plan_to_chips(src, land, x, y, c):
    mine = 2 * x + y
    return [(src[t], land[t].at[mine], (px, py, c), land[t].at[2 * px + py])
            for t in range(len(src)) for px, py in _other_chips(x, y)]


def _plan_swap_halves(src, land, x, y, c):
    out = []
    for t in range(len(src)):
        h = src[t].shape[1] // 2
        out.append((src[t].at[:, pl.ds(pl.multiple_of((1 - c) * h, 8), h), :], land[t], (x, y, 1 - c), land[t]))
    return out


def _plan_scatter_chips(src, land, x, y, c):
    mine = 2 * x + y
    return [(src[t].at[2 * px + py], land[t].at[mine], (px, py, c), land[t].at[2 * px + py])
            for t in range(len(src)) for px, py in _other_chips(x, y)]


def _swap_halves(gs, everyone, whole):
    ns, ne, nw = len(gs), len(everyone), len(whole)
    outs = [jax.ShapeDtypeStruct((4, g.shape[1] // 2, g.shape[2]), g.dtype) for g in gs]
    outs += [jax.ShapeDtypeStruct((8,) + a.shape, a.dtype) for a in everyone]
    outs += [jax.ShapeDtypeStruct(a.shape, a.dtype) for a in whole]

    def plan(i, o, x, y, c):
        me = 4 * x + 2 * y + c
        local, remote = [], []
        for t in range(ns):
            h = gs[t].shape[1] // 2
            theirs = i[t].at[:, pl.ds(pl.multiple_of((1 - c) * h, 8), h), :]
            remote.append((theirs, o[t], (x, y, 1 - c), o[t]))
        for t in range(ns, ns + ne):
            local.append((i[t], o[t].at[me]))
            for px, py, pc in _other_devices(x, y, c):
                remote.append((i[t], o[t].at[me], (px, py, pc), o[t].at[4 * px + 2 * py + pc]))
        for t in range(ns + ne, ns + ne + nw):
            remote.append((i[t], o[t], (x, y, 1 - c), o[t]))
        return local, remote

    return _exchange("grad_swap_sibling", list(gs) + list(everyone) + list(whole), outs, ne, ns + 7 * ne + nw, plan)


def _scatter_chips(parts, per_chip):
    ns, ng = len(parts), len(per_chip)
    outs = [jax.ShapeDtypeStruct(a.shape, a.dtype) for a in parts]
    outs += [jax.ShapeDtypeStruct((4,) + a.shape, a.dtype) for a in per_chip]

    def plan(i, o, x, y, c):
        mine = 2 * x + y
        local, remote = [], []
        for t in range(ns):
            local.append((i[t].at[mine], o[t].at[mine]))
            for px, py in _other_chips(x, y):
                remote.append((i[t].at[2 * px + py], o[t].at[mine], (px, py, c), o[t].at[2 * px + py]))
        for t in range(ns, ns + ng):
            local.append((i[t], o[t].at[mine]))
            for px, py in _other_chips(x, y):
                remote.append((i[t], o[t].at[mine], (px, py, c), o[t].at[2 * px + py]))
        return local, remote

    return _exchange("grad_scatter_chips", list(parts) + list(per_chip), outs, ns + ng, 3 * (ns + ng), plan)


def _join_halves(halves):
    ns = len(halves)
    outs = [jax.ShapeDtypeStruct(a.shape, a.dtype) for a in halves]

    def plan(i, o, x, y, c):
        return [], [(i[t], o[t], (x, y, 1 - c), o[t]) for t in range(ns)]

    return _exchange("grad_join_sibling", list(halves), outs, 0, ns, plan)


def _pad_heads_cols(w, per, used):
    k = w.shape[0]
    w = w.reshape(k, NH, per)[:, :, :used]
    return jnp.pad(w, ((0, 0), (0, 0), (0, HP - used))).reshape(k, NH * HP)


def _unpad_heads_cols(w, used):
    k = w.shape[0]
    return w.reshape(k, NH, HP)[:, :, :used]


def _prep_weights(wf):
    bf = lambda a: a.astype(BF16)
    out = {}
    out["w_in"] = jnp.pad(bf(wf["w_in"]), ((0, 0), (0, IN_PAD - IN_COLS)))
    out["w_glu"] = bf(wf["w_glu"])
    out["w_uq"] = _pad_heads_cols(bf(wf["w_uq"]), QK_NOPE + QK_ROPE, QK_NOPE + QK_ROPE)
    wkv = bf(wf["w_ukv"]).reshape(KV_LORA, NH, QK_NOPE + V_HEAD)
    wk = jnp.pad(wkv[:, :, :QK_NOPE], ((0, 0), (0, 0), (0, HP - QK_NOPE))).reshape(KV_LORA, NH * HP)
    wv = jnp.pad(wkv[:, :, QK_NOPE:], ((0, 0), (0, 0), (0, HP - V_HEAD))).reshape(KV_LORA, NH * HP)
    out["w_ukv"] = jnp.concatenate([wk, wv], axis=1)
    return out


def _prep_late_weights(wf):
    bf = lambda a: a.astype(BF16)
    out = {}
    wo = bf(wf["w_out"])
    wo_a = jnp.pad(wo[D_SSM:].reshape(NH, V_HEAD, D), ((0, 0), (0, HP - V_HEAD), (0, 0))).reshape(NH * HP, D)
    out["w_out"] = jnp.concatenate([wo[:D_SSM], wo_a], axis=0)
    out["w_ff1"] = bf(wf["w_ff1"])
    out["w_ff2"] = bf(wf["w_ff2"])
    return out


def _rope_tables(positions):
    inv_freq = ROPE_BASE ** (-jnp.arange(0, QK_ROPE, 2, dtype=F32) / QK_ROPE)
    ang = positions.astype(F32)[:, None] * inv_freq
    cos, sin = jnp.cos(ang), jnp.sin(ang)
    n = positions.shape[0]
    one = jnp.ones((n, QK_NOPE), F32)
    z16 = jnp.zeros((n, 16), F32)
    z32 = jnp.zeros((n, 32), F32)
    z64 = jnp.zeros((n, QK_NOPE), F32)
    rc = jnp.concatenate([one, cos, cos, z32], axis=1)
    rs1 = jnp.concatenate([z64, -sin, z16, z32], axis=1)
    rs2 = jnp.concatenate([z64, z16, sin, z32], axis=1)
    return rc, rs1, rs2


def _permute_rows(a, S):
    n, w = a.shape
    return a.reshape(n // S, 8, S // 8, w).transpose(0, 2, 1, 3).reshape(n, w)


def _unpermute_rows(a, S):
    n, w = a.shape
    return a.reshape(n // S, S // 8, 8, w).transpose(0, 2, 1, 3).reshape(n, w)


def _block_diag_in(bb):
    eye = jnp.eye(8, dtype=bb.dtype)
    blocks = jnp.einsum("qgph,gk->qghkp", bb.reshape(4, 8, P, H), eye).reshape(4, QB, QS)
    return blocks.transpose(1, 0, 2).reshape(QB, NST)


def _block_diag_out(cc):
    eye = jnp.eye(8, dtype=cc.dtype)
    return jnp.einsum("qghp,gk->qgpkh", cc.reshape(4, 8, H, P), eye).reshape(NST, QB)


def _slots(full):
    r, cdim = full.shape
    return full.reshape(r, 4, cdim // 4).transpose(1, 0, 2)


def _unslots(g):
    s, r, cs = g.shape
    return g.transpose(1, 0, 2).reshape(r, s * cs)


def _local_step(x, positions, target, modp, wf, late_weights=None, reducer=None):
    nb, S, _ = x.shape
    n = nb * S
    tm = min(256, S)
    tr = min(512, S)
    tt = min(512, S)
    tq = min(512, S // 2)
    kw = _prep_weights(wf)
    row = lambda a: a.reshape(1, -1).astype(F32)

    xf = x.reshape(n, D)
    tf = target.reshape(n, D)
    g1, g2, gf = row(wf["norm1_g"]), row(wf["norm2_g"]), row(wf["final_norm_g"])
    rc, rs1, rs2 = _rope_tables(positions.reshape(n))
    gq, gkv = row(wf["q_norm_g"]), row(wf["kv_norm_g"])
    h1, u, lat, q, k, v, qn, kvn = _f1_fwd(xf, modp, g1, kw["w_in"], rc, rs1, rs2, gq, gkv,
                                           kw["w_uq"], kw["w_ukv"], S, tr)

    col = lambda a: a.reshape(NST, 1)
    lam_re, lam_im = col(wf["ssm_lambda_re"]), col(wf["ssm_lambda_im"])
    logdt = jnp.repeat(wf["ssm_log_dt"].reshape(G, 1), P, axis=1).reshape(NST, 1)
    b_re, b_im = wf["ssm_b_re"].reshape(NST, H), wf["ssm_b_im"].reshape(NST, H)
    lbr, lbi, bbr, bbi = _ssm_param_fwd(lam_re, lam_im, logdt, b_re, b_im)
    lre8 = jnp.broadcast_to(lbr.reshape(1, NST), (8, NST))
    lim8 = jnp.broadcast_to(lbi.reshape(1, NST), (8, NST))
    bm = jnp.concatenate([_block_diag_in(bbr.reshape(G, P, H)), _block_diag_in(bbi.reshape(G, P, H))],
                         axis=1).astype(BF16)
    cm = jnp.concatenate([_block_diag_out(wf["ssm_c_re"]), -_block_diag_out(wf["ssm_c_im"])], axis=0).astype(BF16)
    dvec = row(wf["ssm_d"])
    u_p = _permute_rows(u, S)
    fcr, fci = _ssm_local(u_p, bm, lre8, lim8, S, tt)
    st, ypre, z, gact, yssm_p = _ssm_fwd(u_p, fcr, fci, bm, cm, dvec, kw["w_glu"], lre8, lim8, S, tt)
    yssm = _unpermute_rows(yssm_p, S)

    oattn, lrow = _attn_fwd(q, k, v, S, tq)

    gs = row(wf["ssm_out_g"])
    ga = jnp.pad(wf["attn_out_g"].reshape(NH, V_HEAD), ((0, 0), (0, HP - V_HEAD))).reshape(1, NH * HP)
    kw.update(_prep_late_weights(late_weights(oattn) if late_weights is not None else wf))
    (yn, h2, dx1, r, da, dff, do, dyssm, dob, drow, accs2, accg2, accg3) = _p2(
        yssm, oattn, xf, tf, modp, gs, ga, kw["w_out"], g2, gf, kw["w_ff1"], kw["w_ff2"], S, tm)
    loss = accg2[2:3]
    g_ff1 = _wgrad(h2, da, "wgrad_ff1", col_slots=4)
    g_ff2 = _wgrad(r, dff, "wgrad_ff2").reshape(4, D_FF // 4, D)
    gwo = _wgrad(yn, do, "wgrad_out")
    g_out = jnp.concatenate([gwo[:D_SSM].reshape(2, D_SSM // 2, D),
                             gwo[D_SSM:].reshape(2, NH // 2 * HP, D).reshape(2, NH // 2, HP, D)[:, :, :V_HEAD]
                             .reshape(2, D_ATTN // 2, D)], axis=0)
    lre8_b = lre8
    if reducer is not None:
        drow = drow + reducer.start([g_ff1, g_ff2, g_out])[0, 0]

    dq, dk, dv = _attn_bwd(q, k, v, dob, lrow, drow, S, tq)
    if reducer is not None:
        lre8_b = lre8 + reducer.middle(dq)[0, 0]

    dys_p = _permute_rows(dyssm, S)
    dy, dz, air, aii = _ssm_bwd_a(dys_p, z, ypre, kw["w_glu"], cm, lre8_b, lim8, S, tt)
    du_p, dcm, dbm, dd, dlr, dli = _ssm_bwd_b(dy, u_p, st, fcr, fci, air, aii, bm, cm, dvec, lre8, lim8, S, tt)
    du = _unpermute_rows(du_p, S)
    dcm = dcm.reshape(2, 4, 8, P, 8, H)
    dc_re = jnp.einsum("qgpgh->qghp", dcm[0]).reshape(G, H, P)
    dc_im = -jnp.einsum("qgpgh->qghp", dcm[1]).reshape(G, H, P)
    dbm = dbm.reshape(8, H, 2, 4, 8, P)
    dbb_re = jnp.einsum("ghqgp->qgph", dbm[:, :, 0]).reshape(NST, H)
    dbb_im = jnp.einsum("ghqgp->qgph", dbm[:, :, 1]).reshape(NST, H)
    gb_re, gb_im, glr, gli, gdt = _ssm_param_bwd(lam_re, lam_im, logdt, b_re, b_im, dlr.reshape(NST, 1),
                                                 dli.reshape(NST, 1), dbb_re, dbb_im)
    glogdt = _rowsum(gdt.reshape(G, P))

    dx, dproj, dqb, dkvb, accs1, accg1, accm = _f1_bwd(du, dq, dk, dv, lat, rc, rs1, rs2, gq, gkv, kw["w_uq"],
                                                       kw["w_ukv"], dx1, xf, modp, g1, kw["w_in"], S, tr)

    big = {}
    big["w_in"] = _slots(_wgrad(h1, dproj, "wgrad_in")[:, :IN_COLS])
    big["w_glu"] = _wgrad(gact, dz, "wgrad_glu", col_slots=4)
    big["w_uq"] = _slots(_unpad_heads_cols(_wgrad(qn, dqb, "wgrad_uq"), QK_NOPE + QK_ROPE).reshape(Q_LORA, -1))
    gkvw = _wgrad(kvn, dkvb, "wgrad_ukv")
    big["w_ukv"] = _slots(jnp.concatenate([_unpad_heads_cols(gkvw[:, :NH * HP], QK_NOPE),
                                           _unpad_heads_cols(gkvw[:, NH * HP:], V_HEAD)], axis=2).reshape(KV_LORA, -1))
    big["w_out"] = g_out
    big["w_ff1"] = g_ff1
    big["w_ff2"] = g_ff2

    small = {}
    small["norm1_g"] = accg1[0:1]
    small["norm2_g"] = accg2[0:1]
    small["final_norm_g"] = accg2[1:2]
    small["ssm_out_g"] = accg3[0:1, :D_SSM]
    small["attn_out_g"] = accg3[1].reshape(NH, HP)[:, :V_HEAD].reshape(1, D_ATTN)
    small["q_norm_g"] = accm[0:1, :Q_LORA]
    small["kv_norm_g"] = accm[1:2, :KV_LORA]
    small["ssm_lambda_re"] = glr.reshape(G, P)
    small["ssm_lambda_im"] = gli.reshape(G, P)
    small["ssm_b_re"] = gb_re
    small["ssm_b_im"] = gb_im
    small["ssm_c_re"] = dc_re.reshape(G * H, P)
    small["ssm_c_im"] = dc_im.reshape(G * H, P)
    small["ssm_d"] = dd.reshape(G, H)
    small["ssm_log_dt"] = glogdt.reshape(1, G)
    return loss, dx.reshape(nb, S, D), big, small, accs1 + accs2


def _view2d(a):
    return a.reshape(-1, a.shape[-1]) if a.ndim > 1 else a.reshape(1, -1)


def kernel(x, c, positions, ada_w, ada_b, norm1_g, w_in, ssm_lambda_re, ssm_lambda_im, ssm_b_re, ssm_b_im, ssm_c_re, ssm_c_im, ssm_d, ssm_log_dt, w_glu, q_norm_g, w_uq, kv_norm_g, w_ukv, ssm_out_g, attn_out_g, w_out, norm2_g, w_ff1, w_ff2, final_ada_w, final_ada_b, final_norm_g, loss_target, m_ada_w, m_ada_b, m_norm1_g, m_w_in, m_ssm_lambda_re, m_ssm_lambda_im, m_ssm_b_re, m_ssm_b_im, m_ssm_c_re, m_ssm_c_im, m_ssm_d, m_ssm_log_dt, m_w_glu, m_q_norm_g, m_w_uq, m_kv_norm_g, m_w_ukv, m_ssm_out_g, m_attn_out_g, m_w_out, m_norm2_g, m_w_ff1, m_w_ff2, m_final_ada_w, m_final_ada_b, m_final_norm_g, v_ada_w, v_ada_b, v_norm1_g, v_w_in, v_ssm_lambda_re, v_ssm_lambda_im, v_ssm_b_re, v_ssm_b_im, v_ssm_c_re, v_ssm_c_im, v_ssm_d, v_ssm_log_dt, v_w_glu, v_q_norm_g, v_w_uq, v_kv_norm_g, v_w_ukv, v_ssm_out_g, v_attn_out_g, v_w_out, v_norm2_g, v_w_ff1, v_w_ff2, v_final_ada_w, v_final_ada_b, v_final_norm_g):
    args = dict(locals())
    names = list(inspect.signature(kernel).parameters)
    wnames = names[3:names.index("loss_target")]
    small_names = [nm for nm in wnames if nm not in GATHERED and nm not in TP]
    reduced_names = [nm for nm in small_names if nm not in ("ada_b", "final_ada_b")]
    w = {nm: args[nm] for nm in wnames}
    m = {nm: args["m_" + nm] for nm in wnames}
    v = {nm: args["v_" + nm] for nm in wnames}
    nb = x.shape[0]
    xi, yi, ci = lax.axis_index("x"), lax.axis_index("y"), lax.axis_index("c")
    chip, me = 2 * xi + yi, 4 * xi + 2 * yi + ci

    unslot = lambda nm, g: g.reshape(-1, g.shape[-1]) if nm in ROW_SHARDED else _unslots(g)
    early = [nm for nm in GATHERED if nm not in LATE]
    got = _gather_chips("gather_weights", [_view2d(w[nm]).astype(BF16) for nm in early], [c])
    wf = {nm: unslot(nm, g) for nm, g in zip(early, got)}
    for nm in small_names:
        wf[nm] = w[nm][0] if w[nm].ndim > 1 else w[nm]
    c_all = got[len(early)].reshape(8 * nb, D)

    na, nf = ada_w.shape[-1], final_ada_w.shape[-1]
    ada_b_s = lax.dynamic_slice(ada_b, (0, chip * na), (1, na))
    fada_b_s = lax.dynamic_slice(final_ada_b.reshape(1, -1), (0, chip * nf), (1, nf))
    cond_all, modcols = _mod_fwd(c_all, ada_w[0], ada_b_s, final_ada_w, fada_b_s)
    (mod_g,) = _gather_chips("gather_mod", [modcols])
    mine = lax.dynamic_slice(mod_g, (0, me * nb, 0), (4, nb, na + nf))
    modp = jnp.concatenate([mine[:, :, :na].transpose(1, 0, 2).reshape(nb, 6, D),
                            mine[:, :, na:].transpose(1, 0, 2).reshape(nb, 2, D)], axis=1)

    own_late = [_view2d(w[nm]).astype(BF16) for nm in LATE]
    late_gather, token = _split_start("gather_late", own_late,
                                      [jax.ShapeDtypeStruct((4,) + a.shape, a.dtype) for a in own_late],
                                      3 * len(LATE), _plan_to_chips, modp)
    modp = modp + token[0, 0]

    def late_weights(after):
        sent, landed = _split_wait(late_gather, after)
        return {nm: unslot(nm, lax.dynamic_update_slice(g, own[None], (chip, 0, 0)))
                for nm, g, own in zip(LATE, landed, sent)}

    cidx = ci.astype(jnp.int32).reshape(1)
    ahead = ["w_ff1", "w_ff2", "w_out"]

    class Reducer:
        def start(self, gs):
            lands = [jax.ShapeDtypeStruct((4, g.shape[1] // 2, g.shape[2]), g.dtype) for g in gs]
            self.swap, tok = _split_start("grad_swap_ff", gs, lands, len(gs), _plan_swap_halves, modp)
            return tok

        def middle(self, after):
            gs, got = _split_wait(self.swap, after)
            sums = [_add_half(g, r, cidx, "grad_add_sibling_" + nm) for nm, g, r in zip(ahead, gs, got)]
            lands = [jax.ShapeDtypeStruct(s.shape, s.dtype) for s in sums]
            self.scatter, tok = _split_start("grad_scatter_ff", sums, lands, 3 * len(sums), _plan_scatter_chips, modp)
            return tok

        def finish(self, after):
            out = []
            for nm, s, l in zip(ahead, *_split_wait(self.scatter, after)):
                own = lax.dynamic_slice(s, (chip, 0, 0), (1,) + s.shape[1:])
                out.append(_add_chips(lax.dynamic_update_slice(l, own, (chip, 0, 0)), "grad_add_chips_" + nm))
            return out

    reducer = Reducer()
    loss_row, grad_x, big, small, dmodp = _local_step(x, positions, loss_target, modp, wf, late_weights, reducer)

    rest = [nm for nm in GATHERED if nm not in ahead]
    sizes = [small[nm].size for nm in reduced_names]
    pad = -sum(sizes) % 128
    packed = jnp.concatenate([small[nm].reshape(1, -1) for nm in reduced_names] + [jnp.zeros((1, pad), F32)],
                             axis=1).astype(BF16)
    swapped = _swap_halves([big[nm] for nm in rest], [dmodp.reshape(nb, 8 * D)], [packed, loss_row])
    chip_sums = [_add_half(big[nm], r, cidx, "grad_add_sibling_" + nm) for nm, r in zip(rest, swapped)]
    chip_small = _pair_sum(packed, swapped[len(rest) + 1], loss_row, swapped[len(rest) + 2])
    scattered = _scatter_chips(chip_sums, chip_small)
    half_of = {nm: _add_chips(r, "grad_add_chips_" + nm) for nm, r in zip(rest, scattered)}
    half_of.update(zip(ahead, reducer.finish(grad_x)))
    halves = [half_of[nm] for nm in GATHERED]
    others = _join_halves(halves)
    grads = {}
    dmod_all = swapped[len(rest)].reshape(8 * nb, 8 * D)
    small_sum, loss_sum = _sum_devices(scattered[len(rest)].reshape(4, -1), scattered[len(rest) + 1].reshape(4, -1))
    loss = jnp.sum(loss_sum)
    off = 0
    for nm, sz in zip(reduced_names, sizes):
        grads[nm] = small_sum[:, off:off + sz].reshape(small[nm].shape)
        off += sz

    dsl = jnp.concatenate([lax.dynamic_slice(dmod_all, (0, chip * na), (8 * nb, na)),
                           lax.dynamic_slice(dmod_all, (0, 6 * D + chip * nf), (8 * nb, nf))], axis=1)
    gw, gb = _mod_bwd(cond_all.T, dsl, dmod_all)
    grads["ada_w"], grads["final_ada_w"] = gw[:, :na], gw[:, na:]
    grads["ada_b"], grads["final_ada_b"] = gb[:, :6 * D], gb[:, 6 * D:]

    delta, new_m, new_v = {}, {}, {}
    for nm, mine_h, other_h in zip(GATHERED, halves, others):
        grads[nm], delta[nm], new_m[nm], new_v[nm] = _adamw_halves(
            _view2d(w[nm]), mine_h, other_h, _view2d(m[nm]), _view2d(v[nm]), cidx, "adamw_" + nm)
    for nm in TP:
        delta[nm], new_m[nm], new_v[nm] = _adamw(_view2d(w[nm]), grads[nm], _view2d(m[nm]), _view2d(v[nm]),
                                                  "adamw_" + nm)
    upd = _adamw_small([_view2d(w[nm]) for nm in small_names], [grads[nm] for nm in small_names],
                       [_view2d(m[nm]) for nm in small_names], [_view2d(v[nm]) for nm in small_names])
    k = len(small_names)
    for t, nm in enumerate(small_names):
        delta[nm], new_m[nm], new_v[nm] = upd[t], upd[k + t], upd[2 * k + t]

    outs = [grads, delta, new_m, new_v]
    return (loss, grad_x, *[d[nm].reshape(w[nm].shape) for d in outs for nm in wnames])
```
